```python
import jax, jax.numpy as jnp
from jax import lax
import numpy as np

D_MODEL = 2048
BATCH = 8
SEQ = 2048
DEPTH = 2

HEAD_DIM = 128
ATTN_WIDTH = D_MODEL // 2
N_Q_HEADS = ATTN_WIDTH // HEAD_DIM
N_KV_HEADS = max(1, N_Q_HEADS // 4)
GQA_GROUP = N_Q_HEADS // N_KV_HEADS
KV_WIDTH = N_KV_HEADS * HEAD_DIM
WINDOW = 128
BLOCK = 128
GMLP_WIDTH = D_MODEL - ATTN_WIDTH
GMLP_HEAD_DIM = 128
N_GMLP_HEADS = GMLP_WIDTH // GMLP_HEAD_DIM
CHUNK = 128
IN_WIDTH = ATTN_WIDTH + 2 * KV_WIDTH + 2 * GMLP_WIDTH
D_FF = 5632
CONV_WIDTH = 3
ROPE_THETA = 10000.0
EPS = 1e-6
MASK_VALUE = -1e30

kernel_name = "hybrid_window_gqa_sgu_convffn_encoder"


def rms_norm(x, g):
    xf = x.astype(jnp.float32)
    y = xf * lax.rsqrt(jnp.mean(xf * xf, axis=-1, keepdims=True) + EPS)
    return (y * g.astype(jnp.float32)).astype(x.dtype)


def layer_norm(x, g, b):
    xf = x.astype(jnp.float32)
    mu = jnp.mean(xf, axis=-1, keepdims=True)
    xc = xf - mu
    y = xc * lax.rsqrt(jnp.mean(xc * xc, axis=-1, keepdims=True) + EPS)
    return (y * g.astype(jnp.float32) + b.astype(jnp.float32)).astype(x.dtype)


def rope_tables(seq):
    inv_freq = ROPE_THETA ** (-jnp.arange(0, HEAD_DIM, 2, dtype=jnp.float32) / HEAD_DIM)
    ang = jnp.arange(seq, dtype=jnp.float32)[:, None] * inv_freq[None, :]
    return jnp.cos(ang), jnp.sin(ang)


def apply_rope(x, cos, sin):
    xf = x.astype(jnp.float32)
    x1, x2 = jnp.split(xf, 2, axis=-1)
    c = cos[None, :, None, :]
    s = sin[None, :, None, :]
    return jnp.concatenate([x1 * c - x2 * s, x2 * c + x1 * s], axis=-1).astype(x.dtype)


def banded_window_attention(q, k, v, sink):
    B, S, _, D = q.shape
    nb = S // BLOCK
    qb = q.reshape(B, nb, BLOCK, N_KV_HEADS, GQA_GROUP, D)

    def band(t):
        tp = jnp.pad(t, ((0, 0), (BLOCK, BLOCK), (0, 0), (0, 0)))
        tp = tp.reshape(B, nb + 2, BLOCK, N_KV_HEADS, D)
        return jnp.concatenate([tp[:, :-2], tp[:, 1:-1], tp[:, 2:]], axis=2)

    kb, vb = band(k), band(v)
    s = jnp.einsum('bnqhgd,bnkhd->bnhgqk', qb, kb).astype(jnp.float32) * (D ** -0.5)
    blk = jnp.arange(nb)[:, None, None]
    q_pos = blk * BLOCK + jnp.arange(BLOCK)[None, :, None]
    k_pos = blk * BLOCK - BLOCK + jnp.arange(3 * BLOCK)[None, None, :]
    valid = (jnp.abs(k_pos - q_pos) <= WINDOW) & (k_pos >= 0) & (k_pos < S)
    s = jnp.where(valid[None, :, None, None], s, MASK_VALUE)
    sk = sink.astype(jnp.float32).reshape(N_KV_HEADS, GQA_GROUP)[None, None, :, :, None, None]
    m = jnp.maximum(jnp.max(s, axis=-1, keepdims=True), sk)
    p = jnp.exp(s - m)
    probs = p / (jnp.sum(p, axis=-1, keepdims=True) + jnp.exp(sk - m))
    out = jnp.einsum('bnhgqk,bnkhd->bnqhgd', probs.astype(v.dtype), vb)
    return out.reshape(B, S, N_Q_HEADS * D)


def chunked_spatial_gating(u, v, ln_g, ln_b, w_s, b_s):
    B, S, _ = u.shape
    nc = S // CHUNK
    vn = layer_norm(v, ln_g, ln_b).reshape(B, nc, CHUNK, N_GMLP_HEADS, GMLP_HEAD_DIM)
    f = jnp.einsum('hpq,bcqhd->bcphd', w_s, vn) + b_s.T[None, None, :, :, None]
    return u * f.reshape(B, S, GMLP_WIDTH)


def depthwise_conv_centred(h, w, b):
    S = h.shape[1]
    half = CONV_WIDTH // 2
    hp = jnp.pad(h, ((0, 0), (half, half), (0, 0)))
    out = b
    for t in range(CONV_WIDTH):
        out = out + hp[:, t:t + S] * w[t]
    return out


def conv_gated_ffn(h, w_up, conv_w, conv_b, w_down):
    a = depthwise_conv_centred(h @ w_up, conv_w, conv_b)
    g, u = jnp.split(a, 2, axis=-1)
    return (jax.nn.silu(g) * u) @ w_down


def _fwd_setup_inputs(seed: int = 0) -> dict:
    key = jax.random.key(seed)
    ks = jax.random.split(key, 20)
    f32 = jnp.float32
    nrm = lambda k, shape, scale: jax.random.normal(k, shape, f32) * scale
    res_scale = (2.0 * DEPTH) ** -0.5
    return {
        "x": jax.random.normal(ks[0], (BATCH, SEQ, D_MODEL), f32),
        "norm1_g": 1.0 + nrm(ks[1], (DEPTH, D_MODEL), 0.02),
        "w_in": nrm(ks[2], (DEPTH, D_MODEL, IN_WIDTH), D_MODEL ** -0.5),
        "q_norm_g": 1.0 + nrm(ks[3], (DEPTH, HEAD_DIM), 0.02),
        "k_norm_g": 1.0 + nrm(ks[4], (DEPTH, HEAD_DIM), 0.02),
        "sink": nrm(ks[5], (DEPTH, N_Q_HEADS), 0.5),
        "sgu_ln_g": 1.0 + nrm(ks[6], (DEPTH, GMLP_WIDTH), 0.02),
        "sgu_ln_b": nrm(ks[7], (DEPTH, GMLP_WIDTH), 0.02),
        "w_s": nrm(ks[8], (DEPTH, N_GMLP_HEADS, CHUNK, CHUNK), 0.5 * CHUNK ** -0.5),
        "b_s": 1.0 + nrm(ks[9], (DEPTH, N_GMLP_HEADS, CHUNK), 0.02),
        "attn_out_g": 1.0 + nrm(ks[10], (DEPTH, ATTN_WIDTH), 0.02),
        "sgu_out_g": 1.0 + nrm(ks[11], (DEPTH, GMLP_WIDTH), 0.02),
        "w_o": nrm(ks[12], (DEPTH, D_MODEL, D_MODEL), D_MODEL ** -0.5 * res_scale),
        "norm2_g": 1.0 + nrm(ks[13], (DEPTH, D_MODEL), 0.02),
        "w_up": nrm(ks[14], (DEPTH, D_MODEL, 2 * D_FF), D_MODEL ** -0.5),
        "conv_w": nrm(ks[15], (DEPTH, CONV_WIDTH, 2 * D_FF), CONV_WIDTH ** -0.5),
        "conv_b": nrm(ks[16], (DEPTH, 2 * D_FF), 0.01),
        "w_down": nrm(ks[17], (DEPTH, D_FF, D_MODEL), D_FF ** -0.5 * res_scale),
    }


def _fwd_reference(x, norm1_g, w_in, q_norm_g, k_norm_g, sink, sgu_ln_g, sgu_ln_b, w_s, b_s,
              attn_out_g, sgu_out_g, w_o, norm2_g, w_up, conv_w, conv_b, w_down):
    B, S, _ = x.shape
    cos, sin = rope_tables(S)
    splits = [ATTN_WIDTH, ATTN_WIDTH + KV_WIDTH, ATTN_WIDTH + 2 * KV_WIDTH,
              ATTN_WIDTH + 2 * KV_WIDTH + GMLP_WIDTH]
    for l in range(DEPTH):
        h = rms_norm(x, norm1_g[l])
        q, k, v, gu, gv = jnp.split(h @ w_in[l], splits, axis=-1)
        q = apply_rope(rms_norm(q.reshape(B, S, N_Q_HEADS, HEAD_DIM), q_norm_g[l]), cos, sin)
        k = apply_rope(rms_norm(k.reshape(B, S, N_KV_HEADS, HEAD_DIM), k_norm_g[l]), cos, sin)
        v = v.reshape(B, S, N_KV_HEADS, HEAD_DIM)
        attn = banded_window_attention(q, k, v, sink[l])
        sgu = chunked_spatial_gating(jax.nn.gelu(gu), jax.nn.gelu(gv),
                                     sgu_ln_g[l], sgu_ln_b[l], w_s[l], b_s[l])
        mixed = jnp.concatenate([rms_norm(attn, attn_out_g[l]), rms_norm(sgu, sgu_out_g[l])], axis=-1)
        x = x + mixed @ w_o[l]
        x = x + conv_gated_ffn(rms_norm(x, norm2_g[l]), w_up[l], conv_w[l], conv_b[l], w_down[l])
    return x


import jax as _jax
import jax.numpy as _jnp

TWIN_FORMAT = 'train_step'
FWD_PARAMS = ['x', 'norm1_g', 'w_in', 'q_norm_g', 'k_norm_g', 'sink', 'sgu_ln_g', 'sgu_ln_b', 'w_s', 'b_s', 'attn_out_g', 'sgu_out_g', 'w_o', 'norm2_g', 'w_up', 'conv_w', 'conv_b', 'w_down']
TWIN_WEIGHTS = ['norm1_g', 'w_in', 'q_norm_g', 'k_norm_g', 'sink', 'sgu_ln_g', 'sgu_ln_b', 'w_s', 'b_s', 'attn_out_g', 'sgu_out_g', 'w_o', 'norm2_g', 'w_up', 'conv_w', 'conv_b', 'w_down']
TWIN_DIFF_INPUT = 'x'
TWIN_INPUTS = ['x', 'norm1_g', 'w_in', 'q_norm_g', 'k_norm_g', 'sink', 'sgu_ln_g', 'sgu_ln_b', 'w_s', 'b_s', 'attn_out_g', 'sgu_out_g', 'w_o', 'norm2_g', 'w_up', 'conv_w', 'conv_b', 'w_down', 'loss_target', 'm_norm1_g', 'm_w_in', 'm_q_norm_g', 'm_k_norm_g', 'm_sink', 'm_sgu_ln_g', 'm_sgu_ln_b', 'm_w_s', 'm_b_s', 'm_attn_out_g', 'm_sgu_out_g', 'm_w_o', 'm_norm2_g', 'm_w_up', 'm_conv_w', 'm_conv_b', 'm_w_down', 'v_norm1_g', 'v_w_in', 'v_q_norm_g', 'v_k_norm_g', 'v_sink', 'v_sgu_ln_g', 'v_sgu_ln_b', 'v_w_s', 'v_b_s', 'v_attn_out_g', 'v_sgu_out_g', 'v_w_o', 'v_norm2_g', 'v_w_up', 'v_conv_w', 'v_conv_b', 'v_w_down']
TWIN_OUTPUTS = ['loss', 'grad_x', 'grad_norm1_g', 'grad_w_in', 'grad_q_norm_g', 'grad_k_norm_g', 'grad_sink', 'grad_sgu_ln_g', 'grad_sgu_ln_b', 'grad_w_s', 'grad_b_s', 'grad_attn_out_g', 'grad_sgu_out_g', 'grad_w_o', 'grad_norm2_g', 'grad_w_up', 'grad_conv_w', 'grad_conv_b', 'grad_w_down', 'delta_norm1_g', 'delta_w_in', 'delta_q_norm_g', 'delta_k_norm_g', 'delta_sink', 'delta_sgu_ln_g', 'delta_sgu_ln_b', 'delta_w_s', 'delta_b_s', 'delta_attn_out_g', 'delta_sgu_out_g', 'delta_w_o', 'delta_norm2_g', 'delta_w_up', 'delta_conv_w', 'delta_conv_b', 'delta_w_down', 'new_m_norm1_g', 'new_m_w_in', 'new_m_q_norm_g', 'new_m_k_norm_g', 'new_m_sink', 'new_m_sgu_ln_g', 'new_m_sgu_ln_b', 'new_m_w_s', 'new_m_b_s', 'new_m_attn_out_g', 'new_m_sgu_out_g', 'new_m_w_o', 'new_m_norm2_g', 'new_m_w_up', 'new_m_conv_w', 'new_m_conv_b', 'new_m_w_down', 'new_v_norm1_g', 'new_v_w_in', 'new_v_q_norm_g', 'new_v_k_norm_g', 'new_v_sink', 'new_v_sgu_ln_g', 'new_v_sgu_ln_b', 'new_v_w_s', 'new_v_b_s', 'new_v_attn_out_g', 'new_v_sgu_out_g', 'new_v_w_o', 'new_v_norm2_g', 'new_v_w_up', 'new_v_conv_w', 'new_v_conv_b', 'new_v_w_down']
TWIN_LEAF_KINDS = {'loss': 'loss', 'grad_x': 'grad_x', 'grad_norm1_g': 'grad_w', 'grad_w_in': 'grad_w', 'grad_q_norm_g': 'grad_w', 'grad_k_norm_g': 'grad_w', 'grad_sink': 'grad_w', 'grad_sgu_ln_g': 'grad_w', 'grad_sgu_ln_b': 'grad_w', 'grad_w_s': 'grad_w', 'grad_b_s': 'grad_w', 'grad_attn_out_g': 'grad_w', 'grad_sgu_out_g': 'grad_w', 'grad_w_o': 'grad_w', 'grad_norm2_g': 'grad_w', 'grad_w_up': 'grad_w', 'grad_conv_w': 'grad_w', 'grad_conv_b': 'grad_w', 'grad_w_down': 'grad_w', 'delta_norm1_g': 'delta_w', 'delta_w_in': 'delta_w', 'delta_q_norm_g': 'delta_w', 'delta_k_norm_g': 'delta_w', 'delta_sink': 'delta_w', 'delta_sgu_ln_g': 'delta_w', 'delta_sgu_ln_b': 'delta_w', 'delta_w_s': 'delta_w', 'delta_b_s': 'delta_w', 'delta_attn_out_g': 'delta_w', 'delta_sgu_out_g': 'delta_w', 'delta_w_o': 'delta_w', 'delta_norm2_g': 'delta_w', 'delta_w_up': 'delta_w', 'delta_conv_w': 'delta_w', 'delta_conv_b': 'delta_w', 'delta_w_down': 'delta_w', 'new_m_norm1_g': 'new_m', 'new_m_w_in': 'new_m', 'new_m_q_norm_g': 'new_m', 'new_m_k_norm_g': 'new_m', 'new_m_sink': 'new_m', 'new_m_sgu_ln_g': 'new_m', 'new_m_sgu_ln_b': 'new_m', 'new_m_w_s': 'new_m', 'new_m_b_s': 'new_m', 'new_m_attn_out_g': 'new_m', 'new_m_sgu_out_g': 'new_m', 'new_m_w_o': 'new_m', 'new_m_norm2_g': 'new_m', 'new_m_w_up': 'new_m', 'new_m_conv_w': 'new_m', 'new_m_conv_b': 'new_m', 'new_m_w_down': 'new_m', 'new_v_norm1_g': 'new_v', 'new_v_w_in': 'new_v', 'new_v_q_norm_g': 'new_v', 'new_v_k_norm_g': 'new_v', 'new_v_sink': 'new_v', 'new_v_sgu_ln_g': 'new_v', 'new_v_sgu_ln_b': 'new_v', 'new_v_w_s': 'new_v', 'new_v_b_s': 'new_v', 'new_v_attn_out_g': 'new_v', 'new_v_sgu_out_g': 'new_v', 'new_v_w_o': 'new_v', 'new_v_norm2_g': 'new_v', 'new_v_w_up': 'new_v', 'new_v_conv_w': 'new_v', 'new_v_conv_b': 'new_v', 'new_v_w_down': 'new_v'}


def _forward(args):
    return _fwd_reference(*[args[k] for k in FWD_PARAMS])


def _output_shape():
    out = _jax.eval_shape(lambda: _forward(_fwd_setup_inputs(0)))
    return out.shape, out.dtype

N_MICROBATCH = 1
ADAM_LR = 0.001
ADAM_B1 = 0.9
ADAM_B2 = 0.999
ADAM_EPS = 1e-08
ADAM_WD = 0.01
ADAM_STEP = 10
PER_EXAMPLE_BATCH_AXIS = {'x': 0, 'loss_target': 0}
SHARED_INPUTS = []
_WEIGHT_DTYPES = {'norm1_g': _jnp.float32, 'w_in': _jnp.float32, 'q_norm_g': _jnp.float32, 'k_norm_g': _jnp.float32, 'sink': _jnp.float32, 'sgu_ln_g': _jnp.float32, 'sgu_ln_b': _jnp.float32, 'w_s': _jnp.float32, 'b_s': _jnp.float32, 'attn_out_g': _jnp.float32, 'sgu_out_g': _jnp.float32, 'w_o': _jnp.float32, 'norm2_g': _jnp.float32, 'w_up': _jnp.float32, 'conv_w': _jnp.float32, 'conv_b': _jnp.float32, 'w_down': _jnp.float32}
MOMENT_SCALE = {'norm1_g': 3.285000e-01, 'w_in': 2.514534e-01, 'q_norm_g': 1.658793e-01, 'k_norm_g': 1.693184e-01, 'sink': 1.195201e-02, 'sgu_ln_g': 3.076417e-02, 'sgu_ln_b': 3.248407e-02, 'w_s': 5.607222e-02, 'b_s': 6.064867e-02, 'attn_out_g': 2.000592e+00, 'sgu_out_g': 2.104192e+00, 'w_o': 1.093452e+00, 'norm2_g': 1.621325e+00, 'w_up': 6.250870e-02, 'conv_w': 2.401901e-01, 'conv_b': 2.674085e-01, 'w_down': 1.332605e-01}


def _to_microbatches(a, axis):
    t = _jnp.moveaxis(a, axis, 0)
    t = t.reshape((N_MICROBATCH, t.shape[0] // N_MICROBATCH) + t.shape[1:])
    return _jnp.moveaxis(t, 1, axis + 1)


def setup_inputs(seed: int = 0) -> dict:
    inp = _fwd_setup_inputs(seed)
    key = _jax.random.fold_in(_jax.random.key(seed), 7919)
    shape, _ = _output_shape()
    out = dict(inp)
    out["loss_target"] = _jax.random.normal(_jax.random.fold_in(key, 0), shape, _jnp.float32)
    for i, name in enumerate(TWIN_WEIGHTS):
        w = inp[name].astype(_jnp.float32)
        if MOMENT_SCALE is None:
            s = _jnp.sqrt(_jnp.mean(_jnp.square(w)) + 1e-30)
        else:
            s = MOMENT_SCALE[name]
        km, kv = _jax.random.split(_jax.random.fold_in(key, i + 1))
        out[name] = w
        out["m_" + name] = s * _jax.random.normal(km, w.shape, _jnp.float32)
        out["v_" + name] = (s * s) * _jax.random.uniform(kv, w.shape, _jnp.float32, 0.5, 1.5)
    if N_MICROBATCH > 1:
        for name, axis in PER_EXAMPLE_BATCH_AXIS.items():
            out[name] = _to_microbatches(out[name], axis)
    return {'x': out['x'], 'norm1_g': out['norm1_g'], 'w_in': out['w_in'], 'q_norm_g': out['q_norm_g'], 'k_norm_g': out['k_norm_g'], 'sink': out['sink'], 'sgu_ln_g': out['sgu_ln_g'], 'sgu_ln_b': out['sgu_ln_b'], 'w_s': out['w_s'], 'b_s': out['b_s'], 'attn_out_g': out['attn_out_g'], 'sgu_out_g': out['sgu_out_g'], 'w_o': out['w_o'], 'norm2_g': out['norm2_g'], 'w_up': out['w_up'], 'conv_w': out['conv_w'], 'conv_b': out['conv_b'], 'w_down': out['w_down'], 'loss_target': out['loss_target'], 'm_norm1_g': out['m_norm1_g'], 'm_w_in': out['m_w_in'], 'm_q_norm_g': out['m_q_norm_g'], 'm_k_norm_g': out['m_k_norm_g'], 'm_sink': out['m_sink'], 'm_sgu_ln_g': out['m_sgu_ln_g'], 'm_sgu_ln_b': out['m_sgu_ln_b'], 'm_w_s': out['m_w_s'], 'm_b_s': out['m_b_s'], 'm_attn_out_g': out['m_attn_out_g'], 'm_sgu_out_g': out['m_sgu_out_g'], 'm_w_o': out['m_w_o'], 'm_norm2_g': out['m_norm2_g'], 'm_w_up': out['m_w_up'], 'm_conv_w': out['m_conv_w'], 'm_conv_b': out['m_conv_b'], 'm_w_down': out['m_w_down'], 'v_norm1_g': out['v_norm1_g'], 'v_w_in': out['v_w_in'], 'v_q_norm_g': out['v_q_norm_g'], 'v_k_norm_g': out['v_k_norm_g'], 'v_sink': out['v_sink'], 'v_sgu_ln_g': out['v_sgu_ln_g'], 'v_sgu_ln_b': out['v_sgu_ln_b'], 'v_w_s': out['v_w_s'], 'v_b_s': out['v_b_s'], 'v_attn_out_g': out['v_attn_out_g'], 'v_sgu_out_g': out['v_sgu_out_g'], 'v_w_o': out['v_w_o'], 'v_norm2_g': out['v_norm2_g'], 'v_w_up': out['v_w_up'], 'v_conv_w': out['v_conv_w'], 'v_conv_b': out['v_conv_b'], 'v_w_down': out['v_w_down']}


def _loss(weights, diff, rest, loss_target):
    with _jax.named_scope("forward"):
        args = {**rest, TWIN_DIFF_INPUT: diff, **{k: w.astype(_WEIGHT_DTYPES[k]) for k, w in weights.items()}}
        y = _forward(args)
    with _jax.named_scope("loss_head"):
        err = _jnp.square(y.astype(_jnp.float32) - loss_target)
        return 0.5 * _jnp.sum(_jnp.mean(err, axis=-1)) if err.ndim else 0.5 * err


def _adamw(w, g, m, v):
    m = ADAM_B1 * m + (1.0 - ADAM_B1) * g
    v = ADAM_B2 * v + (1.0 - ADAM_B2) * _jnp.square(g)
    m_hat = m / (1.0 - ADAM_B1 ** ADAM_STEP)
    v_hat = v / (1.0 - ADAM_B2 ** ADAM_STEP)
    delta = -ADAM_LR * (m_hat / (_jnp.sqrt(v_hat) + ADAM_EPS) + ADAM_WD * w)
    return delta, m, v


def reference(x, norm1_g, w_in, q_norm_g, k_norm_g, sink, sgu_ln_g, sgu_ln_b, w_s, b_s, attn_out_g, sgu_out_g, w_o, norm2_g, w_up, conv_w, conv_b, w_down, loss_target, m_norm1_g, m_w_in, m_q_norm_g, m_k_norm_g, m_sink, m_sgu_ln_g, m_sgu_ln_b, m_w_s, m_b_s, m_attn_out_g, m_sgu_out_g, m_w_o, m_norm2_g, m_w_up, m_conv_w, m_conv_b, m_w_down, v_norm1_g, v_w_in, v_q_norm_g, v_k_norm_g, v_sink, v_sgu_ln_g, v_sgu_ln_b, v_w_s, v_b_s, v_attn_out_g, v_sgu_out_g, v_w_o, v_norm2_g, v_w_up, v_conv_w, v_conv_b, v_w_down):
    given = dict(x=x, norm1_g=norm1_g, w_in=w_in, q_norm_g=q_norm_g, k_norm_g=k_norm_g, sink=sink, sgu_ln_g=sgu_ln_g, sgu_ln_b=sgu_ln_b, w_s=w_s, b_s=b_s, attn_out_g=attn_out_g, sgu_out_g=sgu_out_g, w_o=w_o, norm2_g=norm2_g, w_up=w_up, conv_w=conv_w, conv_b=conv_b, w_down=w_down, loss_target=loss_target, m_norm1_g=m_norm1_g, m_w_in=m_w_in, m_q_norm_g=m_q_norm_g, m_k_norm_g=m_k_norm_g, m_sink=m_sink, m_sgu_ln_g=m_sgu_ln_g, m_sgu_ln_b=m_sgu_ln_b, m_w_s=m_w_s, m_b_s=m_b_s, m_attn_out_g=m_attn_out_g, m_sgu_out_g=m_sgu_out_g, m_w_o=m_w_o, m_norm2_g=m_norm2_g, m_w_up=m_w_up, m_conv_w=m_conv_w, m_conv_b=m_conv_b, m_w_down=m_w_down, v_norm1_g=v_norm1_g, v_w_in=v_w_in, v_q_norm_g=v_q_norm_g, v_k_norm_g=v_k_norm_g, v_sink=v_sink, v_sgu_ln_g=v_sgu_ln_g, v_sgu_ln_b=v_sgu_ln_b, v_w_s=v_w_s, v_b_s=v_b_s, v_attn_out_g=v_attn_out_g, v_sgu_out_g=v_sgu_out_g, v_w_o=v_w_o, v_norm2_g=v_norm2_g, v_w_up=v_w_up, v_conv_w=v_conv_w, v_conv_b=v_conv_b, v_w_down=v_w_down)
    weights = {n: given[n] for n in TWIN_WEIGHTS}
    shared = {n: given[n] for n in SHARED_INPUTS}
    per_example = {n: given[n] for n in ['x']}
    grad_fn = _jax.value_and_grad(_loss, argnums=(0, 1))

    def one_microbatch(ex, loss_target):
        ex = dict(ex)
        diff = ex.pop(TWIN_DIFF_INPUT)
        return grad_fn(weights, diff, {**shared, **ex}, loss_target)

    if N_MICROBATCH == 1:
        loss, (grad_w, grad_x) = one_microbatch(per_example, given["loss_target"])
    else:
        def body(carry, xs):
            loss_sum, grad_sum = carry
            l_k, (gw_k, gx_k) = one_microbatch(xs[0], xs[1])
            with _jax.named_scope("update"):
                return (loss_sum + l_k, _jax.tree.map(_jnp.add, grad_sum, gw_k)), gx_k

        init = (_jnp.zeros((), _jnp.float32), _jax.tree.map(_jnp.zeros_like, weights))
        (loss, grad_w), grad_x = _jax.lax.scan(body, init, (per_example, given["loss_target"]))
    with _jax.named_scope("update"):
        delta_w, new_m, new_v = {}, {}, {}
        for n in TWIN_WEIGHTS:
            delta_w[n], new_m[n], new_v[n] = _adamw(weights[n], grad_w[n], given["m_" + n], given["v_" + n])
    return (loss, grad_x, *[grad_w[n] for n in TWIN_WEIGHTS], *[delta_w[n] for n in TWIN_WEIGHTS],
            *[new_m[n] for n in TWIN_WEIGHTS], *[new_v[n] for n in TWIN_WEIGHTS])
```

```python
import functools
import math

import jax
import jax.numpy as jnp
from jax import lax
from jax.experimental import pallas as pl
from jax.experimental.pallas import tpu as pltpu

F32 = jnp.float32
BF16 = jnp.bfloat16

D_MODEL = 2048
HEAD_DIM = 128
ATTN_WIDTH = 1024
N_Q_HEADS = 8
N_KV_HEADS = 2
GQA_GROUP = 4
KV_WIDTH = 256
GMLP_WIDTH = 1024
N_GMLP_HEADS = 8
BLOCK = 128
IN_WIDTH = 3584
D_FF = 5632
DEPTH = 2
EPS = 1e-6
MASK_VALUE = -1e30
ROPE_THETA = 10000.0
N_CHIPS = 4

ADAM_LR = 0.001
ADAM_B1 = 0.9
ADAM_B2 = 0.999
ADAM_EPS = 1e-08
ADAM_WD = 0.01
ADAM_STEP = 10

V7X_VMEM_LIMIT = 48 * 1024 * 1024
MESH = pl.DeviceIdType.MESH

_GELU_C = math.sqrt(2.0 / math.pi)
_GELU_A = 0.044715


def _params(sem=None):
    return pltpu.CompilerParams(dimension_semantics=sem, vmem_limit_bytes=V7X_VMEM_LIMIT)


def _gelu(x):
    return x * (0.5 * (1.0 + jnp.tanh(_GELU_C * (x + _GELU_A * (x * x * x)))))


def _gelu_grad(x):
    x2 = x * x
    t = jnp.tanh(_GELU_C * (x + _GELU_A * (x * x2)))
    return 0.5 * (1.0 + t) + 0.5 * x * (1.0 - t * t) * (_GELU_C * (1.0 + 3.0 * _GELU_A * x2))


def _mean_last(x):
    return jnp.mean(x, axis=-1, keepdims=True)


def _sum_rows(x):
    return jnp.sum(x, axis=0, keepdims=True)


def _sum_all(x):
    return jnp.sum(jnp.sum(x, axis=1, keepdims=True), axis=0, keepdims=True)


def _matmul(a, b, *, mode, out_dtype, tm, tn, tk, name, res=None, a_parts=0, b_parts=0, out_parts=0):
    if mode == "nn":
        assert not a_parts
        m, k = a.shape
        n = b.shape[0] * b.shape[2] if b_parts else b.shape[1]
    elif mode == "nt":
        m, k = (a.shape[1], a.shape[0] * a.shape[2]) if a_parts else a.shape
        n = b.shape[1] if b_parts else b.shape[0]
    else:
        assert not a_parts
        k, m = a.shape
        n = b.shape[0] * b.shape[2] if b_parts else b.shape[1]
    tm, tn, tk = min(tm, m), min(tn, n), min(tk, k)
    assert m % tm == 0 and n % tn == 0 and k % tk == 0, (name, m, n, k, tm, tn, tk)
    nm, nn, nk = m // tm, n // tn, k // tk

    def slab(idx, total_tiles, parts):
        per = total_tiles // parts
        assert per * parts == total_tiles, (name, total_tiles, parts)
        return idx // per, idx % per

    if mode == "nn":
        a_spec = pl.BlockSpec((tm, tk), lambda i, j, kk: (i, kk))
        if b_parts:
            b_spec = pl.BlockSpec((None, tk, tn), lambda i, j, kk: (slab(j, nn, b_parts)[0], kk, slab(j, nn, b_parts)[1]))
        else:
            b_spec = pl.BlockSpec((tk, tn), lambda i, j, kk: (kk, j))
        dims = (((1,), (0,)), ((), ()))
    elif mode == "nt":
        if a_parts:
            a_spec = pl.BlockSpec((None, tm, tk), lambda i, j, kk: (slab(kk, nk, a_parts)[0], i, slab(kk, nk, a_parts)[1]))
        else:
            a_spec = pl.BlockSpec((tm, tk), lambda i, j, kk: (i, kk))
        if b_parts:
            b_spec = pl.BlockSpec((None, tn, tk), lambda i, j, kk: (slab(kk, nk, b_parts)[0], j, slab(kk, nk, b_parts)[1]))
        else:
            b_spec = pl.BlockSpec((tn, tk), lambda i, j, kk: (j, kk))
        dims = (((1,), (1,)), ((), ()))
    else:
        a_spec = pl.BlockSpec((tk, tm), lambda i, j, kk: (kk, i))
        if b_parts:
            b_spec = pl.BlockSpec((None, tk, tn), lambda i, j, kk: (slab(j, nn, b_parts)[0], kk, slab(j, nn, b_parts)[1]))
        else:
            b_spec = pl.BlockSpec((tk, tn), lambda i, j, kk: (kk, j))
        dims = (((0,), (0,)), ((), ()))
    if out_parts:
        out_shape = jax.ShapeDtypeStruct((out_parts, m, n // out_parts), out_dtype)
        out_spec = pl.BlockSpec((None, tm, tn), lambda i, j, kk: (slab(j, nn, out_parts)[0], i, slab(j, nn, out_parts)[1]))
    else:
        out_shape = jax.ShapeDtypeStruct((m, n), out_dtype)
        out_spec = pl.BlockSpec((tm, tn), lambda i, j, kk: (i, j))
    in_specs = [a_spec, b_spec]
    operands = [a, b]
    if res is not None:
        in_specs.append(pl.BlockSpec((tm, tn), lambda i, j, kk: (i, j)))
        operands.append(res)

    def body(*refs):
        a_ref, b_ref = refs[0], refs[1]
        res_ref = refs[2] if res is not None else None
        o_ref = refs[3] if res is not None else refs[2]
        p = lax.dot_general(a_ref[...], b_ref[...], dims, preferred_element_type=F32)

        def finish(total):
            if res_ref is not None:
                total = res_ref[...] + total
            o_ref[...] = total.astype(out_dtype)

        if nk == 1:
            finish(p)
        else:
            acc_ref = refs[-1]
            kk = pl.program_id(2)

            @pl.when(kk == 0)
            def _():
                acc_ref[...] = p

            @pl.when(jnp.logical_and(kk > 0, kk < nk - 1))
            def _():
                acc_ref[...] += p

            @pl.when(kk == nk - 1)
            def _():
                finish(acc_ref[...] + p)

    scratch = [pltpu.VMEM((tm, tn), F32)] if nk > 1 else []
    return pl.pallas_call(
        body, name=name, out_shape=out_shape, grid=(nm, nn, nk), in_specs=in_specs, out_specs=out_spec,
        scratch_shapes=scratch, compiler_params=_params(("parallel", "parallel", "arbitrary")),
    )(*operands)


def _row_tile(s):
    return min(256, s)


def _rows(width, tr):
    return pl.BlockSpec((tr, width), lambda i: (i, 0))


def _const2(shape):
    return pl.BlockSpec(shape, lambda i: (0, 0))


def _rms_fwd(x, g, name):
    s, d = x.shape
    tr = _row_tile(s)

    def body(x_ref, g_ref, o_ref):
        xv = x_ref[...]
        r = lax.rsqrt(_mean_last(xv * xv) + EPS)
        o_ref[...] = (xv * r * g_ref[...]).astype(BF16)

    return pl.pallas_call(
        body, name=name, out_shape=jax.ShapeDtypeStruct((s, d), BF16), grid=(s // tr,),
        in_specs=[_rows(d, tr), _const2((1, d))], out_specs=_rows(d, tr), compiler_params=_params(("parallel",)),
    )(x, g)


def _rms_bwd(x, g, dh, dres, name):
    s, d = x.shape
    tr = _row_tile(s)

    def body(x_ref, g_ref, dh_ref, dres_ref, dx_ref, dxb_ref, dg_ref):
        xv, dy = x_ref[...], dh_ref[...]
        r = lax.rsqrt(_mean_last(xv * xv) + EPS)
        gdy = dy * g_ref[...]
        dx = dres_ref[...] + r * gdy - xv * ((r * r * r) * _mean_last(xv * gdy))
        dx_ref[...] = dx
        dxb_ref[...] = dx.astype(BF16)

        @pl.when(pl.program_id(0) == 0)
        def _():
            dg_ref[...] = jnp.zeros_like(dg_ref)

        dg_ref[...] += _sum_rows(xv * r * dy)

    return pl.pallas_call(
        body, name=name,
        out_shape=(jax.ShapeDtypeStruct((s, d), F32), jax.ShapeDtypeStruct((s, d), BF16), jax.ShapeDtypeStruct((1, d), F32)),
        grid=(s // tr,), in_specs=[_rows(d, tr), _const2((1, d)), _rows(d, tr), _rows(d, tr)],
        out_specs=(_rows(d, tr), _rows(d, tr), _const2((1, d))), compiler_params=_params(("arbitrary",)),
    )(x, g, dh, dres)


Q0, K0, V0, GU0, GV0 = 0, ATTN_WIDTH, ATTN_WIDTH + KV_WIDTH, ATTN_WIDTH + 2 * KV_WIDTH, ATTN_WIDTH + 2 * KV_WIDTH + GMLP_WIDTH


def _head(h, base=0):
    return slice(base + h * HEAD_DIM, base + (h + 1) * HEAD_DIM)


def _proj_post(z, qg, kg, lg, lb, cosf, sinf, name):
    s = z.shape[0]
    tr = _row_tile(s)

    def body(z_ref, qg_ref, kg_ref, lg_ref, lb_ref, cos_ref, sin_ref, qn_ref, kn_ref, vb_ref, ug_ref, vn_ref):
        cos, sin = cos_ref[...], sin_ref[...]

        def norm_rope(xh, g):
            y = xh * lax.rsqrt(_mean_last(xh * xh) + EPS) * g
            return y * cos + pltpu.roll(y, HEAD_DIM // 2, 1) * sin

        for h in range(N_Q_HEADS):
            qn_ref[:, _head(h)] = norm_rope(z_ref[:, _head(h, Q0)], qg_ref[...]).astype(BF16)
        for h in range(N_KV_HEADS):
            kn_ref[:, _head(h)] = norm_rope(z_ref[:, _head(h, K0)], kg_ref[...]).astype(BF16)
        vb_ref[...] = z_ref[:, V0:GU0].astype(BF16)
        ug_ref[...] = _gelu(z_ref[:, GU0:GV0])
        vg = _gelu(z_ref[:, GV0:IN_WIDTH])
        xc = vg - _mean_last(vg)
        y = xc * lax.rsqrt(_mean_last(xc * xc) + EPS)
        vn_ref[...] = (y * lg_ref[...] + lb_ref[...]).astype(BF16)

    return pl.pallas_call(
        body, name=name,
        out_shape=(jax.ShapeDtypeStruct((s, ATTN_WIDTH), BF16), jax.ShapeDtypeStruct((s, KV_WIDTH), BF16),
                   jax.ShapeDtypeStruct((s, KV_WIDTH), BF16), jax.ShapeDtypeStruct((s, GMLP_WIDTH), F32),
                   jax.ShapeDtypeStruct((s, GMLP_WIDTH), BF16)),
        grid=(s // tr,),
        in_specs=[_rows(IN_WIDTH, tr), _const2((1, HEAD_DIM)), _const2((1, HEAD_DIM)), _const2((1, GMLP_WIDTH)),
                  _const2((1, GMLP_WIDTH)), _rows(HEAD_DIM, tr), _rows(HEAD_DIM, tr)],
        out_specs=(_rows(ATTN_WIDTH, tr), _rows(KV_WIDTH, tr), _rows(KV_WIDTH, tr), _rows(GMLP_WIDTH, tr), _rows(GMLP_WIDTH, tr)),
        compiler_params=_params(("parallel",)),
    )(z, qg, kg, lg, lb, cosf, sinf)


def _proj_post_bwd(z, dqn, dkn, dvb, dug, dvn, qg, kg, lg, cosf, sinf, name):
    s = z.shape[0]
    tr = _row_tile(s)

    def body(z_ref, dqn_ref, dkn_ref, dvb_ref, dug_ref, dvn_ref, qg_ref, kg_ref, lg_ref, cos_ref, sin_ref,
             dz_ref, dqg_ref, dkg_ref, dlg_ref, dlb_ref):
        cos, sin = cos_ref[...], sin_ref[...]

        @pl.when(pl.program_id(0) == 0)
        def _():
            dqg_ref[...] = jnp.zeros_like(dqg_ref)
            dkg_ref[...] = jnp.zeros_like(dkg_ref)
            dlg_ref[...] = jnp.zeros_like(dlg_ref)
            dlb_ref[...] = jnp.zeros_like(dlb_ref)

        def norm_rope_bwd(xh, g, dout):
            dy = dout * cos - pltpu.roll(dout, HEAD_DIM // 2, 1) * sin
            r = lax.rsqrt(_mean_last(xh * xh) + EPS)
            xhat = xh * r
            gdy = dy * g
            return r * (gdy - xhat * _mean_last(xhat * gdy)), _sum_rows(xhat * dy)

        dqg = jnp.zeros((1, HEAD_DIM), F32)
        for h in range(N_Q_HEADS):
            dx, dg = norm_rope_bwd(z_ref[:, _head(h, Q0)], qg_ref[...], dqn_ref[:, _head(h)])
            dz_ref[:, _head(h, Q0)] = dx.astype(BF16)
            dqg = dqg + dg
        dqg_ref[...] += dqg
        dkg = jnp.zeros((1, HEAD_DIM), F32)
        for h in range(N_KV_HEADS):
            dx, dg = norm_rope_bwd(z_ref[:, _head(h, K0)], kg_ref[...], dkn_ref[:, _head(h)])
            dz_ref[:, _head(h, K0)] = dx.astype(BF16)
            dkg = dkg + dg
        dkg_ref[...] += dkg
        dz_ref[:, V0:GU0] = dvb_ref[...].astype(BF16)
        dz_ref[:, GU0:GV0] = (dug_ref[...] * _gelu_grad(z_ref[:, GU0:GV0])).astype(BF16)
        gv = z_ref[:, GV0:IN_WIDTH]
        vg = _gelu(gv)
        xc = vg - _mean_last(vg)
        r = lax.rsqrt(_mean_last(xc * xc) + EPS)
        xhat = xc * r
        dvn_v = dvn_ref[...]
        dlg_ref[...] += _sum_rows(xhat * dvn_v)
        dlb_ref[...] += _sum_rows(dvn_v)
        dxh = dvn_v * lg_ref[...]
        dvg = r * (dxh - _mean_last(dxh) - xhat * _mean_last(dxh * xhat))
        dz_ref[:, GV0:IN_WIDTH] = (dvg * _gelu_grad(gv)).astype(BF16)

    return pl.pallas_call(
        body, name=name,
        out_shape=(jax.ShapeDtypeStruct((s, IN_WIDTH), BF16), jax.ShapeDtypeStruct((1, HEAD_DIM), F32),
                   jax.ShapeDtypeStruct((1, HEAD_DIM), F32), jax.ShapeDtypeStruct((1, GMLP_WIDTH), F32),
                   jax.ShapeDtypeStruct((1, GMLP_WIDTH), F32)),
        grid=(s // tr,),
        in_specs=[_rows(IN_WIDTH, tr), _rows(ATTN_WIDTH, tr), _rows(KV_WIDTH, tr), _rows(KV_WIDTH, tr), _rows(GMLP_WIDTH, tr),
                  _rows(GMLP_WIDTH, tr), _const2((1, HEAD_DIM)), _const2((1, HEAD_DIM)), _const2((1, GMLP_WIDTH)),
                  _rows(HEAD_DIM, tr), _rows(HEAD_DIM, tr)],
        out_specs=(_rows(IN_WIDTH, tr), _const2((1, HEAD_DIM)), _const2((1, HEAD_DIM)), _const2((1, GMLP_WIDTH)),
                   _const2((1, GMLP_WIDTH))),
        compiler_params=_params(("arbitrary",)),
    )(z, dqn, dkn, dvb, dug, dvn, qg, kg, lg, cosf, sinf)


def _band_valid(n, s):
    i = lax.broadcasted_iota(jnp.int32, (BLOCK, 3 * BLOCK), 0)
    j = lax.broadcasted_iota(jnp.int32, (BLOCK, 3 * BLOCK), 1)
    k_pos = n * BLOCK - BLOCK + j
    return (jnp.abs(j - BLOCK - i) <= BLOCK) & (k_pos >= 0) & (k_pos < s)


def _probs(q, kb, sink_h, valid):
    sc = lax.dot_general(q, kb, (((1,), (1,)), ((), ())), preferred_element_type=F32) * (HEAD_DIM ** -0.5)
    sc = jnp.where(valid, sc, MASK_VALUE)
    m = jnp.maximum(jnp.max(sc, axis=-1, keepdims=True), sink_h)
    p = jnp.exp(sc - m)
    es = jnp.exp(sink_h - m)
    den = jnp.sum(p, axis=-1, keepdims=True) + es
    return p / den, es / den


def _band_specs(width, nb):
    return [pl.BlockSpec((BLOCK, width), lambda n: (jnp.maximum(n - 1, 0), 0)),
            pl.BlockSpec((BLOCK, width), lambda n: (n, 0)),
            pl.BlockSpec((BLOCK, width), lambda n: (jnp.minimum(n + 1, nb - 1), 0))]


def _blk(width):
    return pl.BlockSpec((BLOCK, width), lambda n: (n, 0))


def _whole3(shape):
    return pl.BlockSpec(shape, lambda n: (0, 0, 0))


def _smem():
    return pl.BlockSpec(memory_space=pltpu.SMEM)


def _mixer_fwd(qn, kn, vb, ug, vn, wsb, bsb, sink, ga, gs, name):
    s = qn.shape[0]
    nb = s // BLOCK

    def body(sink_ref, q_ref, kp_ref, kc_ref, kx_ref, vp_ref, vc_ref, vx_ref, ug_ref, vn_ref, ws_ref, bs_ref, ga_ref, gs_ref,
             attn_ref, sgu_ref, mix_ref):
        n = pl.program_id(0)
        valid = _band_valid(n, s)
        ssq = jnp.zeros((BLOCK, 1), F32)
        for kh in range(N_KV_HEADS):
            kb = jnp.concatenate([kp_ref[:, _head(kh)], kc_ref[:, _head(kh)], kx_ref[:, _head(kh)]], axis=0)
            vbd = jnp.concatenate([vp_ref[:, _head(kh)], vc_ref[:, _head(kh)], vx_ref[:, _head(kh)]], axis=0)
            for g in range(GQA_GROUP):
                h = kh * GQA_GROUP + g
                p, _ = _probs(q_ref[:, _head(h)], kb, sink_ref[h], valid)
                o = jnp.dot(p.astype(BF16), vbd, preferred_element_type=F32)
                attn_ref[:, _head(h)] = o
                ssq = ssq + jnp.sum(o * o, axis=-1, keepdims=True)
        r = lax.rsqrt(ssq * (1.0 / ATTN_WIDTH) + EPS)
        mix_ref[:, 0:ATTN_WIDTH] = (attn_ref[...] * r * ga_ref[...]).astype(BF16)
        ssq = jnp.zeros((BLOCK, 1), F32)
        for h in range(N_GMLP_HEADS):
            f = jnp.dot(ws_ref[h], vn_ref[:, _head(h)], preferred_element_type=F32) + bs_ref[h]
            o = ug_ref[:, _head(h)] * f
            sgu_ref[:, _head(h)] = o
            ssq = ssq + jnp.sum(o * o, axis=-1, keepdims=True)
        r = lax.rsqrt(ssq * (1.0 / GMLP_WIDTH) + EPS)
        mix_ref[:, ATTN_WIDTH:D_MODEL] = (sgu_ref[...] * r * gs_ref[...]).astype(BF16)

    hh = (N_GMLP_HEADS, BLOCK, BLOCK)
    return pl.pallas_call(
        body, name=name,
        out_shape=(jax.ShapeDtypeStruct((s, ATTN_WIDTH), F32), jax.ShapeDtypeStruct((s, GMLP_WIDTH), F32),
                   jax.ShapeDtypeStruct((s, D_MODEL), BF16)),
        grid=(nb,),
        in_specs=[_smem(), _blk(ATTN_WIDTH)] + _band_specs(KV_WIDTH, nb) + _band_specs(KV_WIDTH, nb)
        + [_blk(GMLP_WIDTH), _blk(GMLP_WIDTH), _whole3(hh), _whole3(hh),
           pl.BlockSpec((1, ATTN_WIDTH), lambda n: (0, 0)), pl.BlockSpec((1, GMLP_WIDTH), lambda n: (0, 0))],
        out_specs=(_blk(ATTN_WIDTH), _blk(GMLP_WIDTH), _blk(D_MODEL)),
        compiler_params=_params(("parallel",)),
    )(sink, qn, kn, kn, kn, vb, vb, vb, ug, vn, wsb, bsb, ga, gs)


def _mixer_bwd(qn, kn, vb, ug, vn, attn, sgu, dmixed, wsb, bsb, sink, ga, gs, name):
    s = qn.shape[0]
    nb = s // BLOCK
    tn_dims = (((0,), (0,)), ((), ()))
    nt_dims = (((1,), (1,)), ((), ()))

    def body(sink_ref, q_ref, kp_ref, kc_ref, kx_ref, vp_ref, vc_ref, vx_ref, ug_ref, vn_ref, attn_ref, sgu_ref, dm_ref,
             ws_ref, bs_ref, ga_ref, gs_ref,
             dq_ref, dk_ref, dv_ref, dug_ref, dvn_ref, dws_ref, dbs_ref, dsk_ref, dga_ref, dgs_ref, dk_acc, dv_acc):
        n = pl.program_id(0)

        @pl.when(n == 0)
        def _():
            for ref in (dk_acc, dv_acc, dws_ref, dbs_ref, dsk_ref, dga_ref, dgs_ref):
                ref[...] = jnp.zeros_like(ref)

        def out_norm_bwd(o, g, dy):
            r = lax.rsqrt(_mean_last(o * o) + EPS)
            gdy = dy * g
            return r * gdy - o * ((r * r * r) * _mean_last(o * gdy)), _sum_rows(o * r * dy)

        d_attn, dga = out_norm_bwd(attn_ref[...], ga_ref[...], dm_ref[:, 0:ATTN_WIDTH])
        dga_ref[...] += dga
        d_sgu, dgs = out_norm_bwd(sgu_ref[...], gs_ref[...], dm_ref[:, ATTN_WIDTH:D_MODEL])
        dgs_ref[...] += dgs

        for h in range(N_GMLP_HEADS):
            vn_h = vn_ref[:, _head(h)]
            f = jnp.dot(ws_ref[h], vn_h, preferred_element_type=F32) + bs_ref[h]
            ds_h = d_sgu[:, _head(h)]
            dug_ref[:, _head(h)] = ds_h * f
            df = ds_h * ug_ref[:, _head(h)]
            dfb = df.astype(BF16)
            dvn_ref[:, _head(h)] = lax.dot_general(ws_ref[h], dfb, tn_dims, preferred_element_type=F32)
            dws_ref[h] += lax.dot_general(dfb, vn_h, nt_dims, preferred_element_type=F32)
            dbs_ref[h] += jnp.broadcast_to(jnp.sum(df, axis=-1, keepdims=True), (BLOCK, BLOCK))

        valid = _band_valid(n, s)
        row0 = pl.multiple_of(n * BLOCK, BLOCK)
        for kh in range(N_KV_HEADS):
            kb = jnp.concatenate([kp_ref[:, _head(kh)], kc_ref[:, _head(kh)], kx_ref[:, _head(kh)]], axis=0)
            vbd = jnp.concatenate([vp_ref[:, _head(kh)], vc_ref[:, _head(kh)], vx_ref[:, _head(kh)]], axis=0)
            dkb = jnp.zeros((3 * BLOCK, HEAD_DIM), F32)
            dvb = jnp.zeros((3 * BLOCK, HEAD_DIM), F32)
            for g in range(GQA_GROUP):
                h = kh * GQA_GROUP + g
                q = q_ref[:, _head(h)]
                p, p_sink = _probs(q, kb, sink_ref[h], valid)
                do = d_attn[:, _head(h)].astype(BF16)
                dp = lax.dot_general(do, vbd, nt_dims, preferred_element_type=F32)
                delta = jnp.sum(p * dp, axis=-1, keepdims=True)
                dsc = (p * (dp - delta) * (HEAD_DIM ** -0.5)).astype(BF16)
                dsk_ref[h:h + 1, :] += jnp.broadcast_to(_sum_all(-(p_sink * delta)), (1, BLOCK))
                dq_ref[:, _head(h)] = jnp.dot(dsc, kb, preferred_element_type=F32)
                dkb = dkb + lax.dot_general(dsc, q, tn_dims, preferred_element_type=F32)
                dvb = dvb + lax.dot_general(p.astype(BF16), do, tn_dims, preferred_element_type=F32)
            dk_acc[pl.ds(row0, 3 * BLOCK), _head(kh)] += dkb
            dv_acc[pl.ds(row0, 3 * BLOCK), _head(kh)] += dvb

        @pl.when(n == nb - 1)
        def _():
            dk_ref[...] = dk_acc[BLOCK:BLOCK + s, :]
            dv_ref[...] = dv_acc[BLOCK:BLOCK + s, :]

    hh = (N_GMLP_HEADS, BLOCK, BLOCK)
    full_kv = pl.BlockSpec((s, KV_WIDTH), lambda n: (0, 0))
    return pl.pallas_call(
        body, name=name,
        out_shape=(jax.ShapeDtypeStruct((s, ATTN_WIDTH), F32), jax.ShapeDtypeStruct((s, KV_WIDTH), F32),
                   jax.ShapeDtypeStruct((s, KV_WIDTH), F32), jax.ShapeDtypeStruct((s, GMLP_WIDTH), F32),
                   jax.ShapeDtypeStruct((s, GMLP_WIDTH), F32), jax.ShapeDtypeStruct(hh, F32), jax.ShapeDtypeStruct(hh, F32),
                   jax.ShapeDtypeStruct((N_Q_HEADS, BLOCK), F32), jax.ShapeDtypeStruct((1, ATTN_WIDTH), F32),
                   jax.ShapeDtypeStruct((1, GMLP_WIDTH), F32)),
        grid=(nb,),
        in_specs=[_smem(), _blk(ATTN_WIDTH)] + _band_specs(KV_WIDTH, nb) + _band_specs(KV_WIDTH, nb)
        + [_blk(GMLP_WIDTH), _blk(GMLP_WIDTH), _blk(ATTN_WIDTH), _blk(GMLP_WIDTH), _blk(D_MODEL), _whole3(hh), _whole3(hh),
           pl.BlockSpec((1, ATTN_WIDTH), lambda n: (0, 0)), pl.BlockSpec((1, GMLP_WIDTH), lambda n: (0, 0))],
        out_specs=(_blk(ATTN_WIDTH), full_kv, full_kv, _blk(GMLP_WIDTH), _blk(GMLP_WIDTH), _whole3(hh), _whole3(hh),
                   pl.BlockSpec((N_Q_HEADS, BLOCK), lambda n: (0, 0)), pl.BlockSpec((1, ATTN_WIDTH), lambda n: (0, 0)),
                   pl.BlockSpec((1, GMLP_WIDTH), lambda n: (0, 0))),
        scratch_shapes=[pltpu.VMEM((s + 2 * BLOCK, KV_WIDTH), F32), pltpu.VMEM((s + 2 * BLOCK, KV_WIDTH), F32)],
        compiler_params=_params(("arbitrary",)),
    )(sink, qn, kn, kn, kn, vb, vb, vb, ug, vn, attn, sgu, dmixed, wsb, bsb, ga, gs)


CONV_TILE = 128


def _shift_rows(a, rows):
    s = a.shape[0]
    prev = jnp.where(rows == 0, 0.0, pltpu.roll(a, 1, 0))
    nxt = jnp.where(rows == s - 1, 0.0, pltpu.roll(a, s - 1, 0))
    return prev, nxt


def _conv_specs(s):
    tc = CONV_TILE
    nj = D_FF // tc
    return (tc, nj, pl.BlockSpec((2, s, tc), lambda j: (0, 0, j)),
            [pl.BlockSpec((3, tc), lambda j: (0, j)), pl.BlockSpec((3, tc), lambda j: (0, j + nj))],
            [pl.BlockSpec((1, tc), lambda j: (0, j)), pl.BlockSpec((1, tc), lambda j: (0, j + nj))])


def _conv_gate_fwd(a_pre, cw, cb, name):
    s = a_pre.shape[1]
    tc, nj, a_spec, w_specs, b_specs = _conv_specs(s)

    def body(a_ref, wg_ref, wu_ref, bg_ref, bu_ref, act_ref):
        rows = lax.broadcasted_iota(jnp.int32, (s, tc), 0)

        def conv(a, w_ref, b_ref):
            prev, nxt = _shift_rows(a, rows)
            return b_ref[...] + prev * w_ref[0:1, :] + a * w_ref[1:2, :] + nxt * w_ref[2:3, :]

        g = conv(a_ref[0], wg_ref, bg_ref)
        u = conv(a_ref[1], wu_ref, bu_ref)
        act_ref[...] = (g * (1.0 / (1.0 + jnp.exp(-g))) * u).astype(BF16)

    return pl.pallas_call(
        body, name=name, out_shape=jax.ShapeDtypeStruct((s, D_FF), BF16), grid=(nj,),
        in_specs=[a_spec] + w_specs + b_specs, out_specs=pl.BlockSpec((s, tc), lambda j: (0, j)),
        compiler_params=_params(("parallel",)),
    )(a_pre, cw, cw, cb, cb)


def _conv_gate_bwd(a_pre, cw, cb, dact, name):
    s = a_pre.shape[1]
    tc, nj, a_spec, w_specs, b_specs = _conv_specs(s)

    def body(a_ref, wg_ref, wu_ref, bg_ref, bu_ref, dact_ref, dap_ref, dcw_ref, dcb_ref):
        rows = lax.broadcasted_iota(jnp.int32, (s, tc), 0)
        shifted = []
        pre = []
        for part, (w_ref, b_ref) in enumerate(((wg_ref, bg_ref), (wu_ref, bu_ref))):
            a = a_ref[part]
            prev, nxt = _shift_rows(a, rows)
            shifted.append((prev, a, nxt))
            pre.append(b_ref[...] + prev * w_ref[0:1, :] + a * w_ref[1:2, :] + nxt * w_ref[2:3, :])
        g, u = pre
        sg = 1.0 / (1.0 + jnp.exp(-g))
        dact_v = dact_ref[...]
        das = (dact_v * u * (sg * (1.0 + g * (1.0 - sg))), dact_v * (g * sg))
        for part, w_ref in enumerate((wg_ref, wu_ref)):
            da = das[part]
            prev, a, nxt = shifted[part]
            da_prev, da_next = _shift_rows(da, rows)
            dap_ref[part] = (da_next * w_ref[0:1, :] + da * w_ref[1:2, :] + da_prev * w_ref[2:3, :]).astype(BF16)
            dcw_ref[part, 0:1, :] = _sum_rows(prev * da)
            dcw_ref[part, 1:2, :] = _sum_rows(a * da)
            dcw_ref[part, 2:3, :] = _sum_rows(nxt * da)
            dcb_ref[part] = _sum_rows(da)

    return pl.pallas_call(
        body, name=name,
        out_shape=(jax.ShapeDtypeStruct((2, s, D_FF), BF16), jax.ShapeDtypeStruct((2, 3, D_FF), F32),
                   jax.ShapeDtypeStruct((2, 1, D_FF), F32)),
        grid=(nj,),
        in_specs=[a_spec] + w_specs + b_specs + [pl.BlockSpec((s, tc), lambda j: (0, j))],
        out_specs=(pl.BlockSpec((2, s, tc), lambda j: (0, 0, j)), pl.BlockSpec((2, 3, tc), lambda j: (0, 0, j)),
                   pl.BlockSpec((2, 1, tc), lambda j: (0, 0, j))),
        compiler_params=_params(("parallel",)),
    )(a_pre, cw, cw, cb, cb, dact)


def _loss_head(y, target, name):
    s, d = y.shape
    tr = _row_tile(s)

    def body(y_ref, t_ref, loss_ref, dy_ref, dyb_ref):
        err = y_ref[...] - t_ref[...]

        @pl.when(pl.program_id(0) == 0)
        def _():
            loss_ref[...] = jnp.zeros_like(loss_ref)

        loss_ref[...] += jnp.broadcast_to(0.5 * _sum_all(_mean_last(err * err)), (8, 128))
        dy = err * (1.0 / d)
        dy_ref[...] = dy
        dyb_ref[...] = dy.astype(BF16)

    return pl.pallas_call(
        body, name=name,
        out_shape=(jax.ShapeDtypeStruct((8, 128), F32), jax.ShapeDtypeStruct((s, d), F32), jax.ShapeDtypeStruct((s, d), BF16)),
        grid=(s // tr,), in_specs=[_rows(d, tr), _rows(d, tr)],
        out_specs=(_const2((8, 128)), _rows(d, tr), _rows(d, tr)), compiler_params=_params(("arbitrary",)),
    )(y, target)


def _row_block(rows, cols, budget=1 << 20):
    if rows * cols <= budget:
        return rows
    best = None
    for tr in range(16, rows, 16):
        if rows % tr == 0 and tr * cols <= budget:
            best = tr
    assert best is not None, (rows, cols)
    return best


def _cast_bf16(x2, name):
    rows, cols = x2.shape
    tr = _row_block(rows, cols)

    def body(x_ref, o_ref):
        o_ref[...] = x_ref[...].astype(BF16)

    return pl.pallas_call(
        body, name=name, out_shape=jax.ShapeDtypeStruct((rows, cols), BF16), grid=(rows // tr,),
        in_specs=[_rows(cols, tr)], out_specs=_rows(cols, tr), compiler_params=_params(("parallel",)),
    )(x2)


def _adamw(w, g, m, v, name, budget=1 << 18):
    rows, cols = w.shape
    tr = _row_block(rows, cols, budget)

    def body(w_ref, g_ref, m_ref, v_ref, d_ref, nm_ref, nv_ref):
        gv = g_ref[...]
        mn = ADAM_B1 * m_ref[...] + (1.0 - ADAM_B1) * gv
        vn = ADAM_B2 * v_ref[...] + (1.0 - ADAM_B2) * (gv * gv)
        m_hat = mn / (1.0 - ADAM_B1 ** ADAM_STEP)
        v_hat = vn / (1.0 - ADAM_B2 ** ADAM_STEP)
        d_ref[...] = -ADAM_LR * (m_hat / (jnp.sqrt(v_hat) + ADAM_EPS) + ADAM_WD * w_ref[...])
        nm_ref[...] = mn
        nv_ref[...] = vn

    sds = jax.ShapeDtypeStruct((rows, cols), F32)
    return pl.pallas_call(
        body, name=name, out_shape=(sds, sds, sds), grid=(rows // tr,),
        in_specs=[_rows(cols, tr)] * 4, out_specs=(_rows(cols, tr),) * 3, compiler_params=_params(("parallel",)),
    )(w, g, m, v)


def _pair_sum(g5, recv, c_arr, name):
    _, _, rh, cols = g5.shape
    tr = _row_block(rh, cols)

    def body(c_ref, g_ref, r_ref, o_ref):
        o_ref[...] = (g_ref[...].astype(F32) + r_ref[...].astype(F32)).astype(BF16)

    grid_spec = pltpu.PrefetchScalarGridSpec(
        num_scalar_prefetch=1, grid=(N_CHIPS, rh // tr),
        in_specs=[pl.BlockSpec((None, None, tr, cols), lambda j, i, c_ref: (j, c_ref[0], i, 0)),
                  pl.BlockSpec((None, tr, cols), lambda j, i, c_ref: (j, i, 0))],
        out_specs=pl.BlockSpec((None, tr, cols), lambda j, i, c_ref: (j, i, 0)))
    return pl.pallas_call(
        body, name=name, out_shape=jax.ShapeDtypeStruct((N_CHIPS, rh, cols), BF16), grid_spec=grid_spec,
        compiler_params=_params(("parallel", "parallel")),
    )(c_arr, g5, recv)


def _chip_sum(p4, recv3, j_arr, name):
    _, rh, cols = p4.shape
    tr = _row_block(rh, cols, 1 << 19)

    def body(j_ref, p_ref, r_ref, o_ref):
        total = p_ref[...].astype(F32)
        for peer in range(3):
            total = total + r_ref[peer].astype(F32)
        o_ref[...] = total

    grid_spec = pltpu.PrefetchScalarGridSpec(
        num_scalar_prefetch=1, grid=(rh // tr,),
        in_specs=[pl.BlockSpec((None, tr, cols), lambda i, j_ref: (j_ref[0], i, 0)),
                  pl.BlockSpec((3, tr, cols), lambda i, j_ref: (0, i, 0))],
        out_specs=pl.BlockSpec((tr, cols), lambda i, j_ref: (i, 0)))
    return pl.pallas_call(
        body, name=name, out_shape=jax.ShapeDtypeStruct((rh, cols), F32), grid_spec=grid_spec,
        compiler_params=_params(("parallel",)),
    )(j_arr, p4, recv3)


ANY = pl.BlockSpec(memory_space=pl.ANY)


def _place():
    x, y, c = lax.axis_index("x"), lax.axis_index("y"), lax.axis_index("c")
    chips = [(1 - x, y), (x, 1 - y), (1 - x, 1 - y)]
    return x, y, c, chips


def _gather_weights(shards, halves):
    n = len(shards)
    outs = []
    for sh, hv in zip(shards, halves):
        outs.append(jax.ShapeDtypeStruct((sh.shape[0], N_CHIPS) + sh.shape[1:], sh.dtype))

    def body(*refs):
        srcs, dsts = refs[:n], refs[n:2 * n]
        send_sems, recv_sems, fwd_send_sems, fwd_recv_sems, local_sems = refs[2 * n:]
        x, y, c, chips = _place()
        j_me = 2 * x + y

        def piece(ref, a, j, half):
            return ref.at[:, j, half] if halves[a] else ref.at[:, j]

        def own(a, half):
            return srcs[a].at[:, half] if halves[a] else srcs[a]

        def remote(a, k, src, dst, to, sems):
            return pltpu.make_async_remote_copy(src_ref=src, dst_ref=dst, send_sem=sems[0].at[a, k], recv_sem=sems[1].at[a, k],
                                                device_id=to, device_id_type=MESH)

        locals_ = [pltpu.make_async_copy(srcs[a], dsts[a].at[:, j_me], local_sems.at[a]) for a in range(n)]
        for cp in locals_:
            cp.start()
        first = []
        for a in range(n):
            for k, chip in enumerate(chips):
                first.append(remote(a, k, own(a, c), piece(dsts[a], a, j_me, c), (*chip, c), (send_sems, recv_sems)))
        for cp in first:
            cp.start()
        passed = []
        for a in range(n):
            for k, chip in enumerate(chips):
                j_k = 2 * chip[0] + chip[1]
                landed = piece(dsts[a], a, j_k, c)
                remote(a, k, landed, landed, (x, y, c), (send_sems, recv_sems)).wait_recv()
                if halves[a]:
                    fwd = remote(a, k, landed, landed, (x, y, 1 - c), (fwd_send_sems, fwd_recv_sems))
                    fwd.start()
                    passed.append(fwd)
        for a in range(n):
            if halves[a]:
                for k, chip in enumerate(chips):
                    j_k = 2 * chip[0] + chip[1]
                    theirs = piece(dsts[a], a, j_k, 1 - c)
                    remote(a, k, theirs, theirs, (x, y, c), (fwd_send_sems, fwd_recv_sems)).wait_recv()
        for cp in first + passed:
            cp.wait_send()
        for cp in locals_:
            cp.wait()

    return pl.pallas_call(
        body, name="gather_weights", out_shape=tuple(outs), in_specs=[ANY] * n, out_specs=tuple([ANY] * n),
        scratch_shapes=[pltpu.SemaphoreType.DMA((n, 3))] * 4 + [pltpu.SemaphoreType.DMA((n,))],
    )(*shards)


def _swap_halves_to_sibling(g5s):
    n = len(g5s)
    outs = [jax.ShapeDtypeStruct((N_CHIPS,) + g.shape[2:], g.dtype) for g in g5s]

    def body(*refs):
        srcs, dsts = refs[:n], refs[n:2 * n]
        send_sems, recv_sems = refs[2 * n:]
        x, y, c, _ = _place()
        copies = [pltpu.make_async_remote_copy(src_ref=srcs[a].at[:, 1 - c], dst_ref=dsts[a], send_sem=send_sems.at[a],
                                               recv_sem=recv_sems.at[a], device_id=(x, y, 1 - c), device_id_type=MESH)
                  for a in range(n)]
        for cp in copies:
            cp.start()
        for cp in copies:
            cp.wait()

    return pl.pallas_call(
        body, name="grads_to_sibling", out_shape=tuple(outs), in_specs=[ANY] * n, out_specs=tuple([ANY] * n),
        scratch_shapes=[pltpu.SemaphoreType.DMA((n,))] * 2,
    )(*g5s)


def _scatter_to_chips(p4s):
    n = len(p4s)
    outs = [jax.ShapeDtypeStruct((3,) + p.shape[1:], p.dtype) for p in p4s]

    def body(*refs):
        srcs, dsts = refs[:n], refs[n:2 * n]
        send_sems, recv_sems = refs[2 * n:]
        x, y, c, chips = _place()
        copies = []
        for a in range(n):
            for k, chip in enumerate(chips):
                copies.append(pltpu.make_async_remote_copy(
                    src_ref=srcs[a].at[2 * chip[0] + chip[1]], dst_ref=dsts[a].at[k], send_sem=send_sems.at[a, k],
                    recv_sem=recv_sems.at[a, k], device_id=(*chip, c), device_id_type=MESH))
        for cp in copies:
            cp.start()
        for cp in copies:
            cp.wait()

    return pl.pallas_call(
        body, name="grads_to_chips", out_shape=tuple(outs), in_specs=[ANY] * n, out_specs=tuple([ANY] * n),
        scratch_shapes=[pltpu.SemaphoreType.DMA((n, 3))] * 2,
    )(*p4s)


def _join_halves(halves_):
    n = len(halves_)
    outs = [jax.ShapeDtypeStruct((2,) + h.shape, h.dtype) for h in halves_]

    def body(*refs):
        srcs, dsts = refs[:n], refs[n:2 * n]
        send_sems, recv_sems, local_sems = refs[2 * n:]
        x, y, c, _ = _place()
        locals_ = [pltpu.make_async_copy(srcs[a], dsts[a].at[c], local_sems.at[a]) for a in range(n)]
        copies = [pltpu.make_async_remote_copy(src_ref=srcs[a], dst_ref=dsts[a].at[c], send_sem=send_sems.at[a],
                                               recv_sem=recv_sems.at[a], device_id=(x, y, 1 - c), device_id_type=MESH)
                  for a in range(n)]
        for cp in locals_ + copies:
            cp.start()
        for a in range(n):
            copies[a].wait_send()
            pltpu.make_async_remote_copy(src_ref=srcs[a], dst_ref=dsts[a].at[1 - c], send_sem=send_sems.at[a],
                                         recv_sem=recv_sems.at[a], device_id=(x, y, c), device_id_type=MESH).wait_recv()
        for cp in locals_:
            cp.wait()

    return pl.pallas_call(
        body, name="join_halves", out_shape=tuple(outs), in_specs=[ANY] * n, out_specs=tuple([ANY] * n),
        scratch_shapes=[pltpu.SemaphoreType.DMA((n,))] * 3,
    )(*halves_)


def _all_reduce_small(v):
    rows, lanes = v.shape

    def body(x_ref, all_ref, sum_ref, send_sems, recv_sems, local_sem):
        x, y, c, chips = _place()
        me, sibling = (x, y, c), (x, y, 1 - c)

        def slot(px, py, pc):
            return all_ref.at[pl.ds(pl.multiple_of((4 * px + 2 * py + pc) * rows, 8), rows), :]

        def copy(k, block, to, src=None):
            return pltpu.make_async_remote_copy(src_ref=slot(*block) if src is None else src, dst_ref=slot(*block),
                                                send_sem=send_sems.at[k], recv_sem=recv_sems.at[k], device_id=to,
                                                device_id_type=MESH)

        mine = pltpu.make_async_copy(x_ref, slot(*me), local_sem)
        mine.start()
        first = [copy(0, me, sibling, src=x_ref)]
        first += [copy(1 + k, me, (*chip, c), src=x_ref) for k, chip in enumerate(chips)]
        for cp in first:
            cp.start()
        passed = [copy(4 + k, (*chip, c), sibling) for k, chip in enumerate(chips)]
        for k, chip in enumerate(chips):
            copy(1 + k, (*chip, c), me).wait_recv()
            passed[k].start()
        copy(0, sibling, me).wait_recv()
        for k, chip in enumerate(chips):
            copy(4 + k, (*chip, 1 - c), me).wait_recv()
        for cp in first + passed:
            cp.wait_send()
        mine.wait()
        total = all_ref[0:rows, :]
        for dev in range(1, 8):
            total = total + all_ref[dev * rows:(dev + 1) * rows, :]
        sum_ref[...] = total

    vm = pl.BlockSpec(memory_space=pltpu.VMEM)
    return pl.pallas_call(
        body, name="all_reduce_small",
        out_shape=(jax.ShapeDtypeStruct((8 * rows, lanes), v.dtype), jax.ShapeDtypeStruct((rows, lanes), v.dtype)),
        in_specs=[vm], out_specs=(vm, vm),
        scratch_shapes=[pltpu.SemaphoreType.DMA((7,)), pltpu.SemaphoreType.DMA((7,)), pltpu.SemaphoreType.DMA],
        compiler_params=pltpu.CompilerParams(vmem_limit_bytes=V7X_VMEM_LIMIT),
    )(v)[1]


def _rope_tables(s):
    inv_freq = ROPE_THETA ** (-jnp.arange(0, HEAD_DIM, 2, dtype=F32) / HEAD_DIM)
    ang = jnp.arange(s, dtype=F32)[:, None] * inv_freq[None, :]
    cos, sin = jnp.cos(ang), jnp.sin(ang)
    return jnp.concatenate([cos, cos], axis=-1), jnp.concatenate([-sin, sin], axis=-1)


def _local_step(x, target, big, small):
    s = x.shape[0]
    cosf, sinf = _rope_tables(s)
    saved = []
    for l in range(DEPTH):
        w, p = big[l], small[l]
        t = f"l{l}"
        h = _rms_fwd(x, p["norm1_g"], f"norm1_{t}")
        z = _matmul(h, w["w_in"], mode="nn", out_dtype=F32, tm=1024, tn=896, tk=2048, b_parts=4, name=f"proj_in_{t}")
        qn, kn, vb, ug, vn = _proj_post(z, p["q_norm_g"], p["k_norm_g"], p["sgu_ln_g"], p["sgu_ln_b"], cosf, sinf, f"proj_post_{t}")
        attn, sgu, mixed = _mixer_fwd(qn, kn, vb, ug, vn, p["w_s_bf16"], p["b_s_tile"], p["sink"], p["attn_out_g"],
                                      p["sgu_out_g"], f"mixer_{t}")
        x1 = _matmul(mixed, w["w_o"], mode="nn", out_dtype=F32, tm=1024, tn=512, tk=2048, res=x, name=f"proj_out_{t}")
        h2 = _rms_fwd(x1, p["norm2_g"], f"norm2_{t}")
        a_pre = _matmul(h2, w["w_up"], mode="nn", out_dtype=F32, tm=1024, tn=1408, tk=2048, b_parts=4, out_parts=2,
                        name=f"ffn_up_{t}")
        act = _conv_gate_fwd(a_pre, w["conv_w"], p["conv_b"], f"conv_gate_{t}")
        x2 = _matmul(act, w["w_down"], mode="nn", out_dtype=F32, tm=1024, tn=512, tk=2816, res=x1, name=f"ffn_down_{t}")
        saved.append(dict(x=x, h=h, z=z, qn=qn, kn=kn, vb=vb, ug=ug, vn=vn, attn=attn, sgu=sgu, mixed=mixed, x1=x1, h2=h2,
                          a_pre=a_pre, act=act))
        x = x2
    loss_tile, dx, dxb = _loss_head(x, target, "loss_head")
    big_grads, small_grads = [None] * DEPTH, [None] * DEPTH
    for l in reversed(range(DEPTH)):
        w, p, sv = big[l], small[l], saved[l]
        t = f"l{l}"
        g_down = _matmul(sv["act"], dxb, mode="tn", out_dtype=BF16, tm=512, tn=1024, tk=2048, name=f"g_w_down_{t}")
        dact = _matmul(dxb, w["w_down"], mode="nt", out_dtype=F32, tm=1024, tn=512, tk=2048, name=f"d_act_{t}")
        dap, dcw, dcb = _conv_gate_bwd(sv["a_pre"], w["conv_w"], p["conv_b"], dact, f"conv_gate_bwd_{t}")
        g_up = _matmul(sv["h2"], dap, mode="tn", out_dtype=BF16, tm=1024, tn=1408, tk=2048, b_parts=2, out_parts=4,
                       name=f"g_w_up_{t}")
        dh2 = _matmul(dap, w["w_up"], mode="nt", out_dtype=F32, tm=1024, tn=512, tk=2816, a_parts=2, b_parts=4, name=f"d_h2_{t}")
        dx1, dx1b, dg2 = _rms_bwd(sv["x1"], p["norm2_g"], dh2, dx, f"norm2_bwd_{t}")
        g_o = _matmul(sv["mixed"], dx1b, mode="tn", out_dtype=BF16, tm=1024, tn=512, tk=2048, name=f"g_w_o_{t}")
        dmixed = _matmul(dx1b, w["w_o"], mode="nt", out_dtype=F32, tm=1024, tn=512, tk=2048, name=f"d_mixed_{t}")
        dqn, dkn, dvb, dug, dvn, dws, dbs, dsk, dga, dgs = _mixer_bwd(
            sv["qn"], sv["kn"], sv["vb"], sv["ug"], sv["vn"], sv["attn"], sv["sgu"], dmixed, p["w_s_bf16"], p["b_s_tile"],
            p["sink"], p["attn_out_g"], p["sgu_out_g"], f"mixer_bwd_{t}")
        dz, dqg, dkg, dlg, dlb = _proj_post_bwd(sv["z"], dqn, dkn, dvb, dug, dvn, p["q_norm_g"], p["k_norm_g"], p["sgu_ln_g"],
                                                 cosf, sinf, f"proj_post_bwd_{t}")
        g_in = _matmul(sv["h"], dz, mode="tn", out_dtype=BF16, tm=1024, tn=896, tk=2048, out_parts=4, name=f"g_w_in_{t}")
        dh = _matmul(dz, w["w_in"], mode="nt", out_dtype=F32, tm=1024, tn=512, tk=896, b_parts=4, name=f"d_h_{t}")
        dx, dxb, dg1 = _rms_bwd(sv["x"], p["norm1_g"], dh, dx1, f"norm1_bwd_{t}")
        big_grads[l] = dict(w_in=g_in, w_o=g_o, w_up=g_up, w_down=g_down)
        small_grads[l] = dict(
            norm1_g=dg1[0], q_norm_g=dqg[0], k_norm_g=dkg[0], sink=dsk[:, 0], sgu_ln_g=dlg[0], sgu_ln_b=dlb[0], w_s=dws,
            b_s=dbs[:, :, 0], attn_out_g=dga[0], sgu_out_g=dgs[0], norm2_g=dg2[0],
            conv_w=jnp.concatenate([dcw[0], dcw[1]], axis=-1), conv_b=jnp.concatenate([dcb[0, 0], dcb[1, 0]], axis=-1))
    return loss_tile, dx, big_grads, small_grads


def _small_views(l, norm1_g, q_norm_g, k_norm_g, sink, sgu_ln_g, sgu_ln_b, w_s, b_s, attn_out_g, sgu_out_g, norm2_g, conv_b):
    return dict(
        norm1_g=norm1_g[l][None], q_norm_g=q_norm_g[l][None], k_norm_g=k_norm_g[l][None], sink=sink[l],
        sgu_ln_g=sgu_ln_g[l][None], sgu_ln_b=sgu_ln_b[l][None], w_s_bf16=w_s[l].astype(BF16),
        b_s_tile=jnp.broadcast_to(b_s[l][:, :, None], (N_GMLP_HEADS, BLOCK, BLOCK)), attn_out_g=attn_out_g[l][None],
        sgu_out_g=sgu_out_g[l][None], norm2_g=norm2_g[l][None], conv_b=conv_b[l][None])


SMALL_NAMES = ("norm1_g", "q_norm_g", "k_norm_g", "sink", "sgu_ln_g", "sgu_ln_b", "w_s", "b_s", "attn_out_g", "sgu_out_g",
               "norm2_g", "conv_w", "conv_b")
BIG_NAMES = ("w_in", "w_o", "w_up", "w_down")
PACK_LANES = 128
PACK_ALIGN = 8 * PACK_LANES


def _pack(arrays):
    flat = jnp.concatenate([a.reshape(-1) for a in arrays])
    pad = (-flat.shape[0]) % PACK_ALIGN
    return jnp.pad(flat, (0, pad)).reshape(-1, PACK_LANES)


def _unpack(packed, shapes):
    flat = packed.reshape(-1)
    out, at = [], 0
    for shp in shapes:
        size = math.prod(shp)
        out.append(flat[at:at + size].reshape(shp))
        at += size
    return out


def kernel(x, norm1_g, w_in, q_norm_g, k_norm_g, sink, sgu_ln_g, sgu_ln_b, w_s, b_s, attn_out_g, sgu_out_g, w_o, norm2_g, w_up, conv_w, conv_b, w_down, loss_target, m_norm1_g, m_w_in, m_q_norm_g, m_k_norm_g, m_sink, m_sgu_ln_g, m_sgu_ln_b, m_w_s, m_b_s, m_attn_out_g, m_sgu_out_g, m_w_o, m_norm2_g, m_w_up, m_conv_w, m_conv_b, m_w_down, v_norm1_g, v_w_in, v_q_norm_g, v_k_norm_g, v_sink, v_sgu_ln_g, v_sgu_ln_b, v_w_s, v_b_s, v_attn_out_g, v_sgu_out_g, v_w_o, v_norm2_g, v_w_up, v_conv_w, v_conv_b, v_w_down):
    weights = dict(norm1_g=norm1_g, w_in=w_in, q_norm_g=q_norm_g, k_norm_g=k_norm_g, sink=sink, sgu_ln_g=sgu_ln_g,
                   sgu_ln_b=sgu_ln_b, w_s=w_s, b_s=b_s, attn_out_g=attn_out_g, sgu_out_g=sgu_out_g, w_o=w_o, norm2_g=norm2_g,
                   w_up=w_up, conv_w=conv_w, conv_b=conv_b, w_down=w_down)
    m_in = dict(norm1_g=m_norm1_g, w_in=m_w_in, q_norm_g=m_q_norm_g, k_norm_g=m_k_norm_g, sink=m_sink, sgu_ln_g=m_sgu_ln_g,
                sgu_ln_b=m_sgu_ln_b, w_s=m_w_s, b_s=m_b_s, attn_out_g=m_attn_out_g, sgu_out_g=m_sgu_out_g, w_o=m_w_o,
                norm2_g=m_norm2_g, w_up=m_w_up, conv_w=m_conv_w, conv_b=m_conv_b, w_down=m_w_down)
    v_in = dict(norm1_g=v_norm1_g, w_in=v_w_in, q_norm_g=v_q_norm_g, k_norm_g=v_k_norm_g, sink=v_sink, sgu_ln_g=v_sgu_ln_g,
                sgu_ln_b=v_sgu_ln_b, w_s=v_w_s, b_s=v_b_s, attn_out_g=v_attn_out_g, sgu_out_g=v_sgu_out_g, w_o=v_w_o,
                norm2_g=v_norm2_g, w_up=v_w_up, conv_w=v_conv_w, conv_b=v_conv_b, w_down=v_w_down)
    cx, cy, cc = lax.axis_index("x"), lax.axis_index("y"), lax.axis_index("c")
    j_me = 2 * cx + cy
    c_arr = jnp.reshape(cc, (1,)).astype(jnp.int32)
    j_arr = jnp.reshape(j_me, (1,)).astype(jnp.int32)

    shards = []
    for name in BIG_NAMES:
        wsh = weights[name]
        l, r, cdim = wsh.shape
        shards.append(_cast_bf16(wsh.reshape(l * r, cdim), f"cast_{name}").reshape(l, 2, r // 2, cdim))
    shards.append(conv_w)
    full = _gather_weights(shards, [True] * len(BIG_NAMES) + [False])
    full = dict(zip(BIG_NAMES + ("conv_w",), full))
    big = []
    for l in range(DEPTH):
        big.append(dict(
            w_in=full["w_in"][l].reshape(N_CHIPS, D_MODEL, IN_WIDTH // N_CHIPS),
            w_o=full["w_o"][l].reshape(D_MODEL, D_MODEL),
            w_up=full["w_up"][l].reshape(N_CHIPS, D_MODEL, 2 * D_FF // N_CHIPS),
            w_down=full["w_down"][l].reshape(D_FF, D_MODEL),
            conv_w=jnp.transpose(full["conv_w"][l], (1, 0, 2)).reshape(3, 2 * D_FF)))
    small = [_small_views(l, norm1_g, q_norm_g, k_norm_g, sink, sgu_ln_g, sgu_ln_b, w_s, b_s, attn_out_g, sgu_out_g, norm2_g,
                          conv_b) for l in range(DEPTH)]

    loss_tile, dx, big_grads, small_grads = _local_step(x[0], loss_target[0], big, small)

    small_partials = [jnp.stack([small_grads[l][nm] for l in range(DEPTH)]) for nm in SMALL_NAMES]
    small_shapes = [a.shape for a in small_partials]
    reduced = _all_reduce_small(_pack(small_partials + [loss_tile[0, 0:1]]))
    small_full = dict(zip(SMALL_NAMES + ("loss",), _unpack(reduced, small_shapes + [(1,)])))
    loss = small_full.pop("loss")[0]
    cw_cols = 2 * D_FF // N_CHIPS
    small_full["conv_w"] = lax.dynamic_slice_in_dim(small_full["conv_w"], j_me * cw_cols, cw_cols, axis=2)
    pw, pg, pm, pv = (_pack([src[nm] for nm in SMALL_NAMES]) for src in (weights, small_full, m_in, v_in))
    sd, sm, sv = _adamw(pw, pg, pm, pv, "adamw_small", budget=1 << 20)
    shapes = [weights[nm].shape for nm in SMALL_NAMES]
    grads = dict(small_full)
    delta = dict(zip(SMALL_NAMES, _unpack(sd, shapes)))
    new_m = dict(zip(SMALL_NAMES, _unpack(sm, shapes)))
    new_v = dict(zip(SMALL_NAMES, _unpack(sv, shapes)))

    g5s = []
    for l in range(DEPTH):
        for name in BIG_NAMES:
            g = big_grads[l][name]
            rows, cols = weights[name].shape[1:]
            g5s.append(g.reshape(N_CHIPS, 2, rows // 2, cols))
    from_sibling = _swap_halves_to_sibling(g5s)
    p4s = [_pair_sum(g5, rv, c_arr, f"pair_sum_{i}") for i, (g5, rv) in enumerate(zip(g5s, from_sibling))]
    from_chips = _scatter_to_chips(p4s)
    halves_ = [_chip_sum(p4, rv, j_arr, f"chip_sum_{i}") for i, (p4, rv) in enumerate(zip(p4s, from_chips))]
    joined = _join_halves(halves_)
    for i, name in enumerate(BIG_NAMES):
        l_, rows, cols = weights[name].shape
        g = jnp.stack([joined[l * len(BIG_NAMES) + i].reshape(rows, cols) for l in range(DEPTH)])
        flat = lambda a: a.reshape(l_ * rows, cols)
        bd, bm, bv = _adamw(flat(weights[name]), flat(g), flat(m_in[name]), flat(v_in[name]), f"adamw_{name}")
        grads[name] = g
        delta[name], new_m[name], new_v[name] = (a.reshape(l_, rows, cols) for a in (bd, bm, bv))

    order = ("norm1_g", "w_in", "q_norm_g", "k_norm_g", "sink", "sgu_ln_g", "sgu_ln_b", "w_s", "b_s", "attn_out_g", "sgu_out_g",
             "w_o", "norm2_g", "w_up", "conv_w", "conv_b", "w_down")
    return (loss, dx[None], *[grads[nm] for nm in order], *[delta[nm] for nm in order], *[new_m[nm] for nm in order],
            *[new_v[nm] for nm in order])
```

```python
import functools
import math

import jax
import jax.numpy as jnp
from jax import lax
from jax.experimental import pallas as pl
from jax.experimental.pallas import tpu as pltpu

F32 = jnp.float32
BF16 = jnp.bfloat16

D_MODEL = 2048
HEAD_DIM = 128
ATTN_WIDTH = 1024
N_Q_HEADS = 8
N_KV_HEADS = 2
GQA_GROUP = 4
KV_WIDTH = 256
GMLP_WIDTH = 1024
N_GMLP_HEADS = 8
BLOCK = 128
IN_WIDTH = 3584
D_FF = 5632
DEPTH = 2
EPS = 1e-6
MASK_VALUE = -1e30
ROPE_THETA = 10000.0
N_CHIPS = 4

ADAM_LR = 0.001
ADAM_B1 = 0.9
ADAM_B2 = 0.999
ADAM_EPS = 1e-08
ADAM_WD = 0.01
ADAM_STEP = 10

V7X_VMEM_LIMIT = 48 * 1024 * 1024
MESH = pl.DeviceIdType.MESH

_GELU_C = math.sqrt(2.0 / math.pi)
_GELU_A = 0.044715


def _params(sem=None):
    return pltpu.CompilerParams(dimension_semantics=sem, vmem_limit_bytes=V7X_VMEM_LIMIT)


def _gelu(x):
    return x * (0.5 * (1.0 + jnp.tanh(_GELU_C * (x + _GELU_A * (x * x * x)))))


def _gelu_grad(x):
    x2 = x * x
    t = jnp.tanh(_GELU_C * (x + _GELU_A * (x * x2)))
    return 0.5 * (1.0 + t) + 0.5 * x * (1.0 - t * t) * (_GELU_C * (1.0 + 3.0 * _GELU_A * x2))


def _mean_last(x):
    return jnp.mean(x, axis=-1, keepdims=True)


def _sum_rows(x):
    return jnp.sum(x, axis=0, keepdims=True)


def _sum_all(x):
    return jnp.sum(jnp.sum(x, axis=1, keepdims=True), axis=0, keepdims=True)


def _matmul(a, b, *, mode, out_dtype, tm, tn, tk, name, res=None, a_parts=0, b_parts=0, out_parts=0, b_lead=()):
    b_full = b
    b = jax.ShapeDtypeStruct(b.shape[len(b_lead):], b.dtype)
    if mode == "nn":
        assert not a_parts
        m, k = a.shape
        n = b.shape[0] * b.shape[2] if b_parts else b.shape[1]
    elif mode == "nt":
        m, k = (a.shape[1], a.shape[0] * a.shape[2]) if a_parts else a.shape
        n = b.shape[1] if b_parts else b.shape[0]
    else:
        assert not a_parts
        k, m = a.shape
        n = b.shape[0] * b.shape[2] if b_parts else b.shape[1]
    tm, tn, tk = min(tm, m), min(tn, n), min(tk, k)
    assert m % tm == 0 and n % tn == 0 and k % tk == 0, (name, m, n, k, tm, tn, tk)
    nm, nn, nk = m // tm, n // tn, k // tk

    def slab(idx, total_tiles, parts):
        per = total_tiles // parts
        assert per * parts == total_tiles, (name, total_tiles, parts)
        return idx // per, idx % per

    if mode == "nn":
        a_spec = pl.BlockSpec((tm, tk), lambda i, j, kk: (i, kk))
        if b_parts:
            b_spec = pl.BlockSpec((None, tk, tn), lambda i, j, kk: (slab(j, nn, b_parts)[0], kk, slab(j, nn, b_parts)[1]))
        else:
            b_spec = pl.BlockSpec((tk, tn), lambda i, j, kk: (kk, j))
        dims = (((1,), (0,)), ((), ()))
    elif mode == "nt":
        if a_parts:
            a_spec = pl.BlockSpec((None, tm, tk), lambda i, j, kk: (slab(kk, nk, a_parts)[0], i, slab(kk, nk, a_parts)[1]))
        else:
            a_spec = pl.BlockSpec((tm, tk), lambda i, j, kk: (i, kk))
        if b_parts:
            b_spec = pl.BlockSpec((None, tn, tk), lambda i, j, kk: (slab(kk, nk, b_parts)[0], j, slab(kk, nk, b_parts)[1]))
        else:
            b_spec = pl.BlockSpec((tn, tk), lambda i, j, kk: (j, kk))
        dims = (((1,), (1,)), ((), ()))
    else:
        a_spec = pl.BlockSpec((tk, tm), lambda i, j, kk: (kk, i))
        if b_parts:
            b_spec = pl.BlockSpec((None, tk, tn), lambda i, j, kk: (slab(j, nn, b_parts)[0], kk, slab(j, nn, b_parts)[1]))
        else:
            b_spec = pl.BlockSpec((tk, tn), lambda i, j, kk: (kk, j))
        dims = (((0,), (0,)), ((), ()))
    if out_parts:
        out_shape = jax.ShapeDtypeStruct((out_parts, m, n // out_parts), out_dtype)
        out_spec = pl.BlockSpec((None, tm, tn), lambda i, j, kk: (slab(j, nn, out_parts)[0], i, slab(j, nn, out_parts)[1]))
    else:
        out_shape = jax.ShapeDtypeStruct((m, n), out_dtype)
        out_spec = pl.BlockSpec((tm, tn), lambda i, j, kk: (i, j))
    if b_lead:
        inner_map = b_spec.index_map
        b_spec = pl.BlockSpec((None,) * len(b_lead) + tuple(b_spec.block_shape),
                              lambda i, j, kk: tuple(b_lead) + tuple(inner_map(i, j, kk)))
    in_specs = [a_spec, b_spec]
    operands = [a, b_full]
    if res is not None:
        in_specs.append(pl.BlockSpec((tm, tn), lambda i, j, kk: (i, j)))
        operands.append(res)

    def body(*refs):
        a_ref, b_ref = refs[0], refs[1]
        res_ref = refs[2] if res is not None else None
        o_ref = refs[3] if res is not None else refs[2]
        p = lax.dot_general(a_ref[...], b_ref[...], dims, preferred_element_type=F32)

        def finish(total):
            if res_ref is not None:
                total = res_ref[...] + total
            o_ref[...] = total.astype(out_dtype)

        if nk == 1:
            finish(p)
        else:
            acc_ref = refs[-1]
            kk = pl.program_id(2)

            @pl.when(kk == 0)
            def _():
                acc_ref[...] = p

            @pl.when(jnp.logical_and(kk > 0, kk < nk - 1))
            def _():
                acc_ref[...] += p

            @pl.when(kk == nk - 1)
            def _():
                finish(acc_ref[...] + p)

    scratch = [pltpu.VMEM((tm, tn), F32)] if nk > 1 else []
    return pl.pallas_call(
        body, name=name, out_shape=out_shape, grid=(nm, nn, nk), in_specs=in_specs, out_specs=out_spec,
        scratch_shapes=scratch, compiler_params=_params(("parallel", "parallel", "arbitrary")),
    )(*operands)


def _row_tile(s):
    return min(256, s)


def _rows(width, tr):
    return pl.BlockSpec((tr, width), lambda i: (i, 0))


def _const2(shape):
    return pl.BlockSpec(shape, lambda i: (0, 0))


def _rms_fwd(x, g, name):
    s, d = x.shape
    tr = _row_tile(s)

    def body(x_ref, g_ref, o_ref):
        xv = x_ref[...]
        r = lax.rsqrt(_mean_last(xv * xv) + EPS)
        o_ref[...] = (xv * r * g_ref[...]).astype(BF16)

    return pl.pallas_call(
        body, name=name, out_shape=jax.ShapeDtypeStruct((s, d), BF16), grid=(s // tr,),
        in_specs=[_rows(d, tr), _const2((1, d))], out_specs=_rows(d, tr), compiler_params=_params(("parallel",)),
    )(x, g)


def _rms_bwd(x, g, dh, dres, name):
    s, d = x.shape
    tr = _row_tile(s)

    def body(x_ref, g_ref, dh_ref, dres_ref, dx_ref, dxb_ref, dg_ref):
        xv, dy = x_ref[...], dh_ref[...]
        r = lax.rsqrt(_mean_last(xv * xv) + EPS)
        gdy = dy * g_ref[...]
        dx = dres_ref[...] + r * gdy - xv * ((r * r * r) * _mean_last(xv * gdy))
        dx_ref[...] = dx
        dxb_ref[...] = dx.astype(BF16)

        @pl.when(pl.program_id(0) == 0)
        def _():
            dg_ref[...] = jnp.zeros_like(dg_ref)

        dg_ref[...] += _sum_rows(xv * r * dy)

    return pl.pallas_call(
        body, name=name,
        out_shape=(jax.ShapeDtypeStruct((s, d), F32), jax.ShapeDtypeStruct((s, d), BF16), jax.ShapeDtypeStruct((1, d), F32)),
        grid=(s // tr,), in_specs=[_rows(d, tr), _const2((1, d)), _rows(d, tr), _rows(d, tr)],
        out_specs=(_rows(d, tr), _rows(d, tr), _const2((1, d))), compiler_params=_params(("arbitrary",)),
    )(x, g, dh, dres)


Q0, K0, V0, GU0, GV0 = 0, ATTN_WIDTH, ATTN_WIDTH + KV_WIDTH, ATTN_WIDTH + 2 * KV_WIDTH, ATTN_WIDTH + 2 * KV_WIDTH + GMLP_WIDTH


def _head(h, base=0):
    return slice(base + h * HEAD_DIM, base + (h + 1) * HEAD_DIM)


def _proj_post(z, qg, kg, lg, lb, cosf, sinf, name):
    s = z.shape[0]
    tr = _row_tile(s)

    def body(z_ref, qg_ref, kg_ref, lg_ref, lb_ref, cos_ref, sin_ref, qn_ref, kn_ref, vb_ref, ug_ref, vn_ref):
        cos, sin = cos_ref[...], sin_ref[...]

        def norm_rope(xh, g):
            y = xh * lax.rsqrt(_mean_last(xh * xh) + EPS) * g
            return y * cos + pltpu.roll(y, HEAD_DIM // 2, 1) * sin

        for h in range(N_Q_HEADS):
            qn_ref[:, _head(h)] = norm_rope(z_ref[:, _head(h, Q0)], qg_ref[...]).astype(BF16)
        for h in range(N_KV_HEADS):
            kn_ref[:, _head(h)] = norm_rope(z_ref[:, _head(h, K0)], kg_ref[...]).astype(BF16)
        vb_ref[...] = z_ref[:, V0:GU0].astype(BF16)
        ug_ref[...] = _gelu(z_ref[:, GU0:GV0])
        vg = _gelu(z_ref[:, GV0:IN_WIDTH])
        xc = vg - _mean_last(vg)
        y = xc * lax.rsqrt(_mean_last(xc * xc) + EPS)
        vn_ref[...] = (y * lg_ref[...] + lb_ref[...]).astype(BF16)

    return pl.pallas_call(
        body, name=name,
        out_shape=(jax.ShapeDtypeStruct((s, ATTN_WIDTH), BF16), jax.ShapeDtypeStruct((s, KV_WIDTH), BF16),
                   jax.ShapeDtypeStruct((s, KV_WIDTH), BF16), jax.ShapeDtypeStruct((s, GMLP_WIDTH), F32),
                   jax.ShapeDtypeStruct((s, GMLP_WIDTH), BF16)),
        grid=(s // tr,),
        in_specs=[_rows(IN_WIDTH, tr), _const2((1, HEAD_DIM)), _const2((1, HEAD_DIM)), _const2((1, GMLP_WIDTH)),
                  _const2((1, GMLP_WIDTH)), _rows(HEAD_DIM, tr), _rows(HEAD_DIM, tr)],
        out_specs=(_rows(ATTN_WIDTH, tr), _rows(KV_WIDTH, tr), _rows(KV_WIDTH, tr), _rows(GMLP_WIDTH, tr), _rows(GMLP_WIDTH, tr)),
        compiler_params=_params(("parallel",)),
    )(z, qg, kg, lg, lb, cosf, sinf)


def _proj_post_bwd(z, dqn, dkn, dvb, dug, dvn, qg, kg, lg, cosf, sinf, name):
    s = z.shape[0]
    tr = _row_tile(s)

    def body(z_ref, dqn_ref, dkn_ref, dvb_ref, dug_ref, dvn_ref, qg_ref, kg_ref, lg_ref, cos_ref, sin_ref,
             dz_ref, dqg_ref, dkg_ref, dlg_ref, dlb_ref):
        cos, sin = cos_ref[...], sin_ref[...]

        @pl.when(pl.program_id(0) == 0)
        def _():
            dqg_ref[...] = jnp.zeros_like(dqg_ref)
            dkg_ref[...] = jnp.zeros_like(dkg_ref)
            dlg_ref[...] = jnp.zeros_like(dlg_ref)
            dlb_ref[...] = jnp.zeros_like(dlb_ref)

        def norm_rope_bwd(xh, g, dout):
            dy = dout * cos - pltpu.roll(dout, HEAD_DIM // 2, 1) * sin
            r = lax.rsqrt(_mean_last(xh * xh) + EPS)
            xhat = xh * r
            gdy = dy * g
            return r * (gdy - xhat * _mean_last(xhat * gdy)), _sum_rows(xhat * dy)

        dqg = jnp.zeros((1, HEAD_DIM), F32)
        for h in range(N_Q_HEADS):
            dx, dg = norm_rope_bwd(z_ref[:, _head(h, Q0)], qg_ref[...], dqn_ref[:, _head(h)])
            dz_ref[:, _head(h, Q0)] = dx.astype(BF16)
            dqg = dqg + dg
        dqg_ref[...] += dqg
        dkg = jnp.zeros((1, HEAD_DIM), F32)
        for h in range(N_KV_HEADS):
            dx, dg = norm_rope_bwd(z_ref[:, _head(h, K0)], kg_ref[...], dkn_ref[:, _head(h)])
            dz_ref[:, _head(h, K0)] = dx.astype(BF16)
            dkg = dkg + dg
        dkg_ref[...] += dkg
        dz_ref[:, V0:GU0] = dvb_ref[...].astype(BF16)
        dz_ref[:, GU0:GV0] = (dug_ref[...] * _gelu_grad(z_ref[:, GU0:GV0])).astype(BF16)
        gv = z_ref[:, GV0:IN_WIDTH]
        vg = _gelu(gv)
        xc = vg - _mean_last(vg)
        r = lax.rsqrt(_mean_last(xc * xc) + EPS)
        xhat = xc * r
        dvn_v = dvn_ref[...]
        dlg_ref[...] += _sum_rows(xhat * dvn_v)
        dlb_ref[...] += _sum_rows(dvn_v)
        dxh = dvn_v * lg_ref[...]
        dvg = r * (dxh - _mean_last(dxh) - xhat * _mean_last(dxh * xhat))
        dz_ref[:, GV0:IN_WIDTH] = (dvg * _gelu_grad(gv)).astype(BF16)

    return pl.pallas_call(
        body, name=name,
        out_shape=(jax.ShapeDtypeStruct((s, IN_WIDTH), BF16), jax.ShapeDtypeStruct((1, HEAD_DIM), F32),
                   jax.ShapeDtypeStruct((1, HEAD_DIM), F32), jax.ShapeDtypeStruct((1, GMLP_WIDTH), F32),
                   jax.ShapeDtypeStruct((1, GMLP_WIDTH), F32)),
        grid=(s // tr,),
        in_specs=[_rows(IN_WIDTH, tr), _rows(ATTN_WIDTH, tr), _rows(KV_WIDTH, tr), _rows(KV_WIDTH, tr), _rows(GMLP_WIDTH, tr),
                  _rows(GMLP_WIDTH, tr), _const2((1, HEAD_DIM)), _const2((1, HEAD_DIM)), _const2((1, GMLP_WIDTH)),
                  _rows(HEAD_DIM, tr), _rows(HEAD_DIM, tr)],
        out_specs=(_rows(IN_WIDTH, tr), _const2((1, HEAD_DIM)), _const2((1, HEAD_DIM)), _const2((1, GMLP_WIDTH)),
                   _const2((1, GMLP_WIDTH))),
        compiler_params=_params(("arbitrary",)),
    )(z, dqn, dkn, dvb, dug, dvn, qg, kg, lg, cosf, sinf)


def _band_valid(n, s):
    i = lax.broadcasted_iota(jnp.int32, (BLOCK, 3 * BLOCK), 0)
    j = lax.broadcasted_iota(jnp.int32, (BLOCK, 3 * BLOCK), 1)
    k_pos = n * BLOCK - BLOCK + j
    return (jnp.abs(j - BLOCK - i) <= BLOCK) & (k_pos >= 0) & (k_pos < s)


def _probs(q, kb, sink_h, valid):
    sc = lax.dot_general(q, kb, (((1,), (1,)), ((), ())), preferred_element_type=F32) * (HEAD_DIM ** -0.5)
    sc = jnp.where(valid, sc, MASK_VALUE)
    m = jnp.maximum(jnp.max(sc, axis=-1, keepdims=True), sink_h)
    p = jnp.exp(sc - m)
    es = jnp.exp(sink_h - m)
    den = jnp.sum(p, axis=-1, keepdims=True) + es
    return p / den, es / den


def _band_specs(width, nb):
    return [pl.BlockSpec((BLOCK, width), lambda n: (jnp.maximum(n - 1, 0), 0)),
            pl.BlockSpec((BLOCK, width), lambda n: (n, 0)),
            pl.BlockSpec((BLOCK, width), lambda n: (jnp.minimum(n + 1, nb - 1), 0))]


def _blk(width):
    return pl.BlockSpec((BLOCK, width), lambda n: (n, 0))


def _whole3(shape):
    return pl.BlockSpec(shape, lambda n: (0, 0, 0))


def _smem():
    return pl.BlockSpec(memory_space=pltpu.SMEM)


def _mixer_fwd(qn, kn, vb, ug, vn, wsb, bsb, sink, ga, gs, name):
    s = qn.shape[0]
    nb = s // BLOCK

    def body(sink_ref, q_ref, kp_ref, kc_ref, kx_ref, vp_ref, vc_ref, vx_ref, ug_ref, vn_ref, ws_ref, bs_ref, ga_ref, gs_ref,
             attn_ref, sgu_ref, mix_ref):
        n = pl.program_id(0)
        valid = _band_valid(n, s)
        ssq = jnp.zeros((BLOCK, 1), F32)
        for kh in range(N_KV_HEADS):
            kb = jnp.concatenate([kp_ref[:, _head(kh)], kc_ref[:, _head(kh)], kx_ref[:, _head(kh)]], axis=0)
            vbd = jnp.concatenate([vp_ref[:, _head(kh)], vc_ref[:, _head(kh)], vx_ref[:, _head(kh)]], axis=0)
            for g in range(GQA_GROUP):
                h = kh * GQA_GROUP + g
                p, _ = _probs(q_ref[:, _head(h)], kb, sink_ref[h], valid)
                o = jnp.dot(p.astype(BF16), vbd, preferred_element_type=F32)
                attn_ref[:, _head(h)] = o
                ssq = ssq + jnp.sum(o * o, axis=-1, keepdims=True)
        r = lax.rsqrt(ssq * (1.0 / ATTN_WIDTH) + EPS)
        mix_ref[:, 0:ATTN_WIDTH] = (attn_ref[...] * r * ga_ref[...]).astype(BF16)
        ssq = jnp.zeros((BLOCK, 1), F32)
        for h in range(N_GMLP_HEADS):
            f = jnp.dot(ws_ref[h], vn_ref[:, _head(h)], preferred_element_type=F32) + bs_ref[h]
            o = ug_ref[:, _head(h)] * f
            sgu_ref[:, _head(h)] = o
            ssq = ssq + jnp.sum(o * o, axis=-1, keepdims=True)
        r = lax.rsqrt(ssq * (1.0 / GMLP_WIDTH) + EPS)
        mix_ref[:, ATTN_WIDTH:D_MODEL] = (sgu_ref[...] * r * gs_ref[...]).astype(BF16)

    hh = (N_GMLP_HEADS, BLOCK, BLOCK)
    return pl.pallas_call(
        body, name=name,
        out_shape=(jax.ShapeDtypeStruct((s, ATTN_WIDTH), F32), jax.ShapeDtypeStruct((s, GMLP_WIDTH), F32),
                   jax.ShapeDtypeStruct((s, D_MODEL), BF16)),
        grid=(nb,),
        in_specs=[_smem(), _blk(ATTN_WIDTH)] + _band_specs(KV_WIDTH, nb) + _band_specs(KV_WIDTH, nb)
        + [_blk(GMLP_WIDTH), _blk(GMLP_WIDTH), _whole3(hh), _whole3(hh),
           pl.BlockSpec((1, ATTN_WIDTH), lambda n: (0, 0)), pl.BlockSpec((1, GMLP_WIDTH), lambda n: (0, 0))],
        out_specs=(_blk(ATTN_WIDTH), _blk(GMLP_WIDTH), _blk(D_MODEL)),
        compiler_params=_params(("parallel",)),
    )(sink, qn, kn, kn, kn, vb, vb, vb, ug, vn, wsb, bsb, ga, gs)


def _mixer_bwd(qn, kn, vb, ug, vn, attn, sgu, dmixed, wsb, bsb, sink, ga, gs, name):
    s = qn.shape[0]
    nb = s // BLOCK
    tn_dims = (((0,), (0,)), ((), ()))
    nt_dims = (((1,), (1,)), ((), ()))

    def body(sink_ref, q_ref, kp_ref, kc_ref, kx_ref, vp_ref, vc_ref, vx_ref, ug_ref, vn_ref, attn_ref, sgu_ref, dm_ref,
             ws_ref, bs_ref, ga_ref, gs_ref,
             dq_ref, dk_ref, dv_ref, dug_ref, dvn_ref, dws_ref, dbs_ref, dsk_ref, dga_ref, dgs_ref, dk_acc, dv_acc):
        n = pl.program_id(0)

        @pl.when(n == 0)
        def _():
            for ref in (dk_acc, dv_acc, dws_ref, dbs_ref, dsk_ref, dga_ref, dgs_ref):
                ref[...] = jnp.zeros_like(ref)

        def out_norm_bwd(o, g, dy):
            r = lax.rsqrt(_mean_last(o * o) + EPS)
            gdy = dy * g
            return r * gdy - o * ((r * r * r) * _mean_last(o * gdy)), _sum_rows(o * r * dy)

        d_attn, dga = out_norm_bwd(attn_ref[...], ga_ref[...], dm_ref[:, 0:ATTN_WIDTH])
        dga_ref[...] += dga
        d_sgu, dgs = out_norm_bwd(sgu_ref[...], gs_ref[...], dm_ref[:, ATTN_WIDTH:D_MODEL])
        dgs_ref[...] += dgs

        for h in range(N_GMLP_HEADS):
            vn_h = vn_ref[:, _head(h)]
            f = jnp.dot(ws_ref[h], vn_h, preferred_element_type=F32) + bs_ref[h]
            ds_h = d_sgu[:, _head(h)]
            dug_ref[:, _head(h)] = ds_h * f
            df = ds_h * ug_ref[:, _head(h)]
            dfb = df.astype(BF16)
            dvn_ref[:, _head(h)] = lax.dot_general(ws_ref[h], dfb, tn_dims, preferred_element_type=F32)
            dws_ref[h] += lax.dot_general(dfb, vn_h, nt_dims, preferred_element_type=F32)
            dbs_ref[h] += jnp.broadcast_to(jnp.sum(df, axis=-1, keepdims=True), (BLOCK, BLOCK))

        valid = _band_valid(n, s)
        row0 = pl.multiple_of(n * BLOCK, BLOCK)
        for kh in range(N_KV_HEADS):
            kb = jnp.concatenate([kp_ref[:, _head(kh)], kc_ref[:, _head(kh)], kx_ref[:, _head(kh)]], axis=0)
            vbd = jnp.concatenate([vp_ref[:, _head(kh)], vc_ref[:, _head(kh)], vx_ref[:, _head(kh)]], axis=0)
            dkb = jnp.zeros((3 * BLOCK, HEAD_DIM), F32)
            dvb = jnp.zeros((3 * BLOCK, HEAD_DIM), F32)
            for g in range(GQA_GROUP):
                h = kh * GQA_GROUP + g
                q = q_ref[:, _head(h)]
                p, p_sink = _probs(q, kb, sink_ref[h], valid)
                do = d_attn[:, _head(h)].astype(BF16)
                dp = lax.dot_general(do, vbd, nt_dims, preferred_element_type=F32)
                delta = jnp.sum(p * dp, axis=-1, keepdims=True)
                dsc = (p * (dp - delta) * (HEAD_DIM ** -0.5)).astype(BF16)
                dsk_ref[h:h + 1, :] += jnp.broadcast_to(_sum_all(-(p_sink * delta)), (1, BLOCK))
                dq_ref[:, _head(h)] = jnp.dot(dsc, kb, preferred_element_type=F32)
                dkb = dkb + lax.dot_general(dsc, q, tn_dims, preferred_element_type=F32)
                dvb = dvb + lax.dot_general(p.astype(BF16), do, tn_dims, preferred_element_type=F32)
            dk_acc[pl.ds(row0, 3 * BLOCK), _head(kh)] += dkb
            dv_acc[pl.ds(row0, 3 * BLOCK), _head(kh)] += dvb

        @pl.when(n == nb - 1)
        def _():
            dk_ref[...] = dk_acc[BLOCK:BLOCK + s, :]
            dv_ref[...] = dv_acc[BLOCK:BLOCK + s, :]

    hh = (N_GMLP_HEADS, BLOCK, BLOCK)
    full_kv = pl.BlockSpec((s, KV_WIDTH), lambda n: (0, 0))
    return pl.pallas_call(
        body, name=name,
        out_shape=(jax.ShapeDtypeStruct((s, ATTN_WIDTH), F32), jax.ShapeDtypeStruct((s, KV_WIDTH), F32),
                   jax.ShapeDtypeStruct((s, KV_WIDTH), F32), jax.ShapeDtypeStruct((s, GMLP_WIDTH), F32),
                   jax.ShapeDtypeStruct((s, GMLP_WIDTH), F32), jax.ShapeDtypeStruct(hh, F32), jax.ShapeDtypeStruct(hh, F32),
                   jax.ShapeDtypeStruct((N_Q_HEADS, BLOCK), F32), jax.ShapeDtypeStruct((1, ATTN_WIDTH), F32),
                   jax.ShapeDtypeStruct((1, GMLP_WIDTH), F32)),
        grid=(nb,),
        in_specs=[_smem(), _blk(ATTN_WIDTH)] + _band_specs(KV_WIDTH, nb) + _band_specs(KV_WIDTH, nb)
        + [_blk(GMLP_WIDTH), _blk(GMLP_WIDTH), _blk(ATTN_WIDTH), _blk(GMLP_WIDTH), _blk(D_MODEL), _whole3(hh), _whole3(hh),
           pl.BlockSpec((1, ATTN_WIDTH), lambda n: (0, 0)), pl.BlockSpec((1, GMLP_WIDTH), lambda n: (0, 0))],
        out_specs=(_blk(ATTN_WIDTH), full_kv, full_kv, _blk(GMLP_WIDTH), _blk(GMLP_WIDTH), _whole3(hh), _whole3(hh),
                   pl.BlockSpec((N_Q_HEADS, BLOCK), lambda n: (0, 0)), pl.BlockSpec((1, ATTN_WIDTH), lambda n: (0, 0)),
                   pl.BlockSpec((1, GMLP_WIDTH), lambda n: (0, 0))),
        scratch_shapes=[pltpu.VMEM((s + 2 * BLOCK, KV_WIDTH), F32), pltpu.VMEM((s + 2 * BLOCK, KV_WIDTH), F32)],
        compiler_params=_params(("arbitrary",)),
    )(sink, qn, kn, kn, kn, vb, vb, vb, ug, vn, attn, sgu, dmixed, wsb, bsb, ga, gs)


CONV_TILE = 128


def _shift_rows(a, rows):
    s = a.shape[0]
    prev = jnp.where(rows == 0, 0.0, pltpu.roll(a, 1, 0))
    nxt = jnp.where(rows == s - 1, 0.0, pltpu.roll(a, s - 1, 0))
    return prev, nxt


def _conv_specs(s):
    tc = CONV_TILE
    nj = D_FF // tc
    return (tc, nj, pl.BlockSpec((2, s, tc), lambda j: (0, 0, j)),
            [pl.BlockSpec((3, tc), lambda j: (0, j)), pl.BlockSpec((3, tc), lambda j: (0, j + nj))],
            [pl.BlockSpec((1, tc), lambda j: (0, j)), pl.BlockSpec((1, tc), lambda j: (0, j + nj))])


def _conv_gate_fwd(a_pre, cw, cb, name):
    s = a_pre.shape[1]
    tc, nj, a_spec, w_specs, b_specs = _conv_specs(s)

    def body(a_ref, wg_ref, wu_ref, bg_ref, bu_ref, act_ref):
        rows = lax.broadcasted_iota(jnp.int32, (s, tc), 0)

        def conv(a, w_ref, b_ref):
            prev, nxt = _shift_rows(a, rows)
            return b_ref[...] + prev * w_ref[0:1, :] + a * w_ref[1:2, :] + nxt * w_ref[2:3, :]

        g = conv(a_ref[0], wg_ref, bg_ref)
        u = conv(a_ref[1], wu_ref, bu_ref)
        act_ref[...] = (g * (1.0 / (1.0 + jnp.exp(-g))) * u).astype(BF16)

    return pl.pallas_call(
        body, name=name, out_shape=jax.ShapeDtypeStruct((s, D_FF), BF16), grid=(nj,),
        in_specs=[a_spec] + w_specs + b_specs, out_specs=pl.BlockSpec((s, tc), lambda j: (0, j)),
        compiler_params=_params(("parallel",)),
    )(a_pre, cw, cw, cb, cb)


def _conv_gate_bwd(a_pre, cw, cb, dact, name):
    s = a_pre.shape[1]
    tc, nj, a_spec, w_specs, b_specs = _conv_specs(s)

    def body(a_ref, wg_ref, wu_ref, bg_ref, bu_ref, dact_ref, dap_ref, dcw_ref, dcb_ref):
        rows = lax.broadcasted_iota(jnp.int32, (s, tc), 0)
        shifted = []
        pre = []
        for part, (w_ref, b_ref) in enumerate(((wg_ref, bg_ref), (wu_ref, bu_ref))):
            a = a_ref[part]
            prev, nxt = _shift_rows(a, rows)
            shifted.append((prev, a, nxt))
            pre.append(b_ref[...] + prev * w_ref[0:1, :] + a * w_ref[1:2, :] + nxt * w_ref[2:3, :])
        g, u = pre
        sg = 1.0 / (1.0 + jnp.exp(-g))
        dact_v = dact_ref[...]
        das = (dact_v * u * (sg * (1.0 + g * (1.0 - sg))), dact_v * (g * sg))
        for part, w_ref in enumerate((wg_ref, wu_ref)):
            da = das[part]
            prev, a, nxt = shifted[part]
            da_prev, da_next = _shift_rows(da, rows)
            dap_ref[part] = (da_next * w_ref[0:1, :] + da * w_ref[1:2, :] + da_prev * w_ref[2:3, :]).astype(BF16)
            dcw_ref[part, 0:1, :] = _sum_rows(prev * da)
            dcw_ref[part, 1:2, :] = _sum_rows(a * da)
            dcw_ref[part, 2:3, :] = _sum_rows(nxt * da)
            dcb_ref[part] = _sum_rows(da)

    return pl.pallas_call(
        body, name=name,
        out_shape=(jax.ShapeDtypeStruct((2, s, D_FF), BF16), jax.ShapeDtypeStruct((2, 3, D_FF), F32),
                   jax.ShapeDtypeStruct((2, 1, D_FF), F32)),
        grid=(nj,),
        in_specs=[a_spec] + w_specs + b_specs + [pl.BlockSpec((s, tc), lambda j: (0, j))],
        out_specs=(pl.BlockSpec((2, s, tc), lambda j: (0, 0, j)), pl.BlockSpec((2, 3, tc), lambda j: (0, 0, j)),
                   pl.BlockSpec((2, 1, tc), lambda j: (0, 0, j))),
        compiler_params=_params(("parallel",)),
    )(a_pre, cw, cw, cb, cb, dact)


def _loss_head(y, target, name):
    s, d = y.shape
    tr = _row_tile(s)

    def body(y_ref, t_ref, loss_ref, dy_ref, dyb_ref):
        err = y_ref[...] - t_ref[...]

        @pl.when(pl.program_id(0) == 0)
        def _():
            loss_ref[...] = jnp.zeros_like(loss_ref)

        loss_ref[...] += jnp.broadcast_to(0.5 * _sum_all(_mean_last(err * err)), (8, 128))
        dy = err * (1.0 / d)
        dy_ref[...] = dy
        dyb_ref[...] = dy.astype(BF16)

    return pl.pallas_call(
        body, name=name,
        out_shape=(jax.ShapeDtypeStruct((8, 128), F32), jax.ShapeDtypeStruct((s, d), F32), jax.ShapeDtypeStruct((s, d), BF16)),
        grid=(s // tr,), in_specs=[_rows(d, tr), _rows(d, tr)],
        out_specs=(_const2((8, 128)), _rows(d, tr), _rows(d, tr)), compiler_params=_params(("arbitrary",)),
    )(y, target)


def _row_block(rows, cols, budget=1 << 20):
    if rows * cols <= budget:
        return rows
    best = None
    for tr in range(16, rows, 16):
        if rows % tr == 0 and tr * cols <= budget:
            best = tr
    assert best is not None, (rows, cols)
    return best


def _place_shard(x4, j_arr, out_dtype, name):
    nl, nh, r, cols = x4.shape
    tr = _row_block(r, cols)

    def body(j_ref, x_ref, o_ref):
        o_ref[...] = x_ref[...].astype(out_dtype)

    grid_spec = pltpu.PrefetchScalarGridSpec(
        num_scalar_prefetch=1, grid=(nl, nh, r // tr),
        in_specs=[pl.BlockSpec((None, None, tr, cols), lambda l, h, i, j_ref: (l, h, i, 0))],
        out_specs=pl.BlockSpec((None, None, None, tr, cols), lambda l, h, i, j_ref: (l, j_ref[0], h, i, 0)))
    return pl.pallas_call(
        body, name=name, out_shape=jax.ShapeDtypeStruct((nl, N_CHIPS, nh, r, cols), out_dtype), grid_spec=grid_spec,
        compiler_params=_params(("parallel", "parallel", "parallel")),
    )(j_arr, x4)


def _adamw(w, g, m, v, name, budget=1 << 18):
    rows, cols = w.shape
    tr = _row_block(rows, cols, budget)

    def body(w_ref, g_ref, m_ref, v_ref, d_ref, nm_ref, nv_ref):
        gv = g_ref[...]
        mn = ADAM_B1 * m_ref[...] + (1.0 - ADAM_B1) * gv
        vn = ADAM_B2 * v_ref[...] + (1.0 - ADAM_B2) * (gv * gv)
        m_hat = mn / (1.0 - ADAM_B1 ** ADAM_STEP)
        v_hat = vn / (1.0 - ADAM_B2 ** ADAM_STEP)
        d_ref[...] = -ADAM_LR * (m_hat / (jnp.sqrt(v_hat) + ADAM_EPS) + ADAM_WD * w_ref[...])
        nm_ref[...] = mn
        nv_ref[...] = vn

    sds = jax.ShapeDtypeStruct((rows, cols), F32)
    return pl.pallas_call(
        body, name=name, out_shape=(sds, sds, sds), grid=(rows // tr,),
        in_specs=[_rows(cols, tr)] * 4, out_specs=(_rows(cols, tr),) * 3, compiler_params=_params(("parallel",)),
    )(w, g, m, v)


def _pair_sum(g5, recv, c_arr, name):
    _, _, rh, cols = g5.shape
    tr = _row_block(rh, cols)

    def body(c_ref, g_ref, r_ref, o_ref):
        o_ref[...] = (g_ref[...].astype(F32) + r_ref[...].astype(F32)).astype(BF16)

    grid_spec = pltpu.PrefetchScalarGridSpec(
        num_scalar_prefetch=1, grid=(N_CHIPS, rh // tr),
        in_specs=[pl.BlockSpec((None, None, tr, cols), lambda j, i, c_ref: (j, c_ref[0], i, 0)),
                  pl.BlockSpec((None, tr, cols), lambda j, i, c_ref: (j, i, 0))],
        out_specs=pl.BlockSpec((None, tr, cols), lambda j, i, c_ref: (j, i, 0)))
    return pl.pallas_call(
        body, name=name, out_shape=jax.ShapeDtypeStruct((N_CHIPS, rh, cols), BF16), grid_spec=grid_spec,
        compiler_params=_params(("parallel", "parallel")),
    )(c_arr, g5, recv)


def _chip_sum(p4, recv3, j_arr, c_arr, layer, into, name):
    _, rh, cols = p4.shape
    tr = _row_block(rh, cols, 1 << 19)

    def body(j_ref, c_ref, p_ref, r_ref, *rest):
        o_ref = rest[-1]
        total = p_ref[...].astype(F32)
        for peer in range(3):
            total = total + r_ref[peer].astype(F32)
        o_ref[...] = total

    in_specs = [pl.BlockSpec((None, tr, cols), lambda i, j_ref, c_ref: (j_ref[0], i, 0)),
                pl.BlockSpec((3, tr, cols), lambda i, j_ref, c_ref: (0, i, 0))]
    operands = [j_arr, c_arr, p4, recv3]
    aliases = {}
    if into is not None:
        in_specs.append(ANY)
        operands.append(into)
        aliases = {4: 0}
    grid_spec = pltpu.PrefetchScalarGridSpec(
        num_scalar_prefetch=2, grid=(rh // tr,), in_specs=in_specs,
        out_specs=pl.BlockSpec((None, None, tr, cols), lambda i, j_ref, c_ref: (layer, c_ref[0], i, 0)))
    return pl.pallas_call(
        body, name=name, out_shape=jax.ShapeDtypeStruct((DEPTH, 2, rh, cols), F32), grid_spec=grid_spec,
        input_output_aliases=aliases, compiler_params=_params(("parallel",)),
    )(*operands)


ANY = pl.BlockSpec(memory_space=pl.ANY)


def _place():
    x, y, c = lax.axis_index("x"), lax.axis_index("y"), lax.axis_index("c")
    chips = [(1 - x, y), (x, 1 - y), (1 - x, 1 - y)]
    return x, y, c, chips


def _gather_weights(fulls, halves):
    n = len(fulls)

    def body(*refs):
        dsts = refs[n:2 * n]
        send_sems, recv_sems, fwd_send_sems, fwd_recv_sems = refs[2 * n:]
        x, y, c, chips = _place()
        j_me = 2 * x + y

        def piece(ref, a, j, half):
            return ref.at[:, j, half] if halves[a] else ref.at[:, j]

        def remote(a, k, src, dst, to, sems):
            return pltpu.make_async_remote_copy(src_ref=src, dst_ref=dst, send_sem=sems[0].at[a, k], recv_sem=sems[1].at[a, k],
                                                device_id=to, device_id_type=MESH)

        first = []
        for a in range(n):
            for k, chip in enumerate(chips):
                mine = piece(dsts[a], a, j_me, c)
                first.append(remote(a, k, mine, mine, (*chip, c), (send_sems, recv_sems)))
        for cp in first:
            cp.start()
        passed = []
        for a in range(n):
            for k, chip in enumerate(chips):
                j_k = 2 * chip[0] + chip[1]
                landed = piece(dsts[a], a, j_k, c)
                remote(a, k, landed, landed, (x, y, c), (send_sems, recv_sems)).wait_recv()
                if halves[a]:
                    fwd = remote(a, k, landed, landed, (x, y, 1 - c), (fwd_send_sems, fwd_recv_sems))
                    fwd.start()
                    passed.append(fwd)
        for a in range(n):
            if halves[a]:
                for k, chip in enumerate(chips):
                    j_k = 2 * chip[0] + chip[1]
                    theirs = piece(dsts[a], a, j_k, 1 - c)
                    remote(a, k, theirs, theirs, (x, y, c), (fwd_send_sems, fwd_recv_sems)).wait_recv()
        for cp in first + passed:
            cp.wait_send()

    return pl.pallas_call(
        body, name="gather_weights", out_shape=tuple(jax.ShapeDtypeStruct(f.shape, f.dtype) for f in fulls),
        in_specs=[ANY] * n, out_specs=tuple([ANY] * n), input_output_aliases={a: a for a in range(n)},
        scratch_shapes=[pltpu.SemaphoreType.DMA((n, 3))] * 4,
    )(*fulls)


def _swap_halves_to_sibling(g5s):
    n = len(g5s)
    outs = [jax.ShapeDtypeStruct((N_CHIPS,) + g.shape[2:], g.dtype) for g in g5s]

    def body(*refs):
        srcs, dsts = refs[:n], refs[n:2 * n]
        send_sems, recv_sems = refs[2 * n:]
        x, y, c, _ = _place()
        copies = [pltpu.make_async_remote_copy(src_ref=srcs[a].at[:, 1 - c], dst_ref=dsts[a], send_sem=send_sems.at[a],
                                               recv_sem=recv_sems.at[a], device_id=(x, y, 1 - c), device_id_type=MESH)
                  for a in range(n)]
        for cp in copies:
            cp.start()
        for cp in copies:
            cp.wait()

    return pl.pallas_call(
        body, name="grads_to_sibling", out_shape=tuple(outs), in_specs=[ANY] * n, out_specs=tuple([ANY] * n),
        scratch_shapes=[pltpu.SemaphoreType.DMA((n,))] * 2,
    )(*g5s)


def _scatter_to_chips(p4s):
    n = len(p4s)
    outs = [jax.ShapeDtypeStruct((3,) + p.shape[1:], p.dtype) for p in p4s]

    def body(*refs):
        srcs, dsts = refs[:n], refs[n:2 * n]
        send_sems, recv_sems = refs[2 * n:]
        x, y, c, chips = _place()
        copies = []
        for a in range(n):
            for k, chip in enumerate(chips):
                copies.append(pltpu.make_async_remote_copy(
                    src_ref=srcs[a].at[2 * chip[0] + chip[1]], dst_ref=dsts[a].at[k], send_sem=send_sems.at[a, k],
                    recv_sem=recv_sems.at[a, k], device_id=(*chip, c), device_id_type=MESH))
        for cp in copies:
            cp.start()
        for cp in copies:
            cp.wait()

    return pl.pallas_call(
        body, name="grads_to_chips", out_shape=tuple(outs), in_specs=[ANY] * n, out_specs=tuple([ANY] * n),
        scratch_shapes=[pltpu.SemaphoreType.DMA((n, 3))] * 2,
    )(*p4s)


def _join_halves(bufs):
    n = len(bufs)

    def body(*refs):
        dsts = refs[n:2 * n]
        send_sems, recv_sems = refs[2 * n:]
        x, y, c, _ = _place()
        copies = [pltpu.make_async_remote_copy(src_ref=dsts[a].at[:, c], dst_ref=dsts[a].at[:, c], send_sem=send_sems.at[a],
                                               recv_sem=recv_sems.at[a], device_id=(x, y, 1 - c), device_id_type=MESH)
                  for a in range(n)]
        for cp in copies:
            cp.start()
        for a in range(n):
            copies[a].wait_send()
            theirs = dsts[a].at[:, 1 - c]
            pltpu.make_async_remote_copy(src_ref=theirs, dst_ref=theirs, send_sem=send_sems.at[a], recv_sem=recv_sems.at[a],
                                         device_id=(x, y, c), device_id_type=MESH).wait_recv()

    return pl.pallas_call(
        body, name="join_halves", out_shape=tuple(jax.ShapeDtypeStruct(b.shape, b.dtype) for b in bufs),
        in_specs=[ANY] * n, out_specs=tuple([ANY] * n), input_output_aliases={a: a for a in range(n)},
        scratch_shapes=[pltpu.SemaphoreType.DMA((n,))] * 2,
    )(*bufs)


def _all_reduce_small(v):
    rows, lanes = v.shape

    def body(x_ref, all_ref, sum_ref, send_sems, recv_sems, local_sem):
        x, y, c, chips = _place()
        me, sibling = (x, y, c), (x, y, 1 - c)

        def slot(px, py, pc):
            return all_ref.at[pl.ds(pl.multiple_of((4 * px + 2 * py + pc) * rows, 8), rows), :]

        def copy(k, block, to, src=None):
            return pltpu.make_async_remote_copy(src_ref=slot(*block) if src is None else src, dst_ref=slot(*block),
                                                send_sem=send_sems.at[k], recv_sem=recv_sems.at[k], device_id=to,
                                                device_id_type=MESH)

        mine = pltpu.make_async_copy(x_ref, slot(*me), local_sem)
        mine.start()
        first = [copy(0, me, sibling, src=x_ref)]
        first += [copy(1 + k, me, (*chip, c), src=x_ref) for k, chip in enumerate(chips)]
        for cp in first:
            cp.start()
        passed = [copy(4 + k, (*chip, c), sibling) for k, chip in enumerate(chips)]
        for k, chip in enumerate(chips):
            copy(1 + k, (*chip, c), me).wait_recv()
            passed[k].start()
        copy(0, sibling, me).wait_recv()
        for k, chip in enumerate(chips):
            copy(4 + k, (*chip, 1 - c), me).wait_recv()
        for cp in first + passed:
            cp.wait_send()
        mine.wait()
        total = all_ref[0:rows, :]
        for dev in range(1, 8):
            total = total + all_ref[dev * rows:(dev + 1) * rows, :]
        sum_ref[...] = total

    vm = pl.BlockSpec(memory_space=pltpu.VMEM)
    return pl.pallas_call(
        body, name="all_reduce_small",
        out_shape=(jax.ShapeDtypeStruct((8 * rows, lanes), v.dtype), jax.ShapeDtypeStruct((rows, lanes), v.dtype)),
        in_specs=[vm], out_specs=(vm, vm),
        scratch_shapes=[pltpu.SemaphoreType.DMA((7,)), pltpu.SemaphoreType.DMA((7,)), pltpu.SemaphoreType.DMA],
        compiler_params=pltpu.CompilerParams(vmem_limit_bytes=V7X_VMEM_LIMIT),
    )(v)[1]


def _rope_tables(s):
    inv_freq = ROPE_THETA ** (-jnp.arange(0, HEAD_DIM, 2, dtype=F32) / HEAD_DIM)
    ang = jnp.arange(s, dtype=F32)[:, None] * inv_freq[None, :]
    cos, sin = jnp.cos(ang), jnp.sin(ang)
    return jnp.concatenate([cos, cos], axis=-1), jnp.concatenate([-sin, sin], axis=-1)


def _local_step(x, target, big, small):
    s = x.shape[0]
    cosf, sinf = _rope_tables(s)
    saved = []
    for l in range(DEPTH):
        w, p = big, small[l]
        t = f"l{l}"
        h = _rms_fwd(x, p["norm1_g"], f"norm1_{t}")
        z = _matmul(h, w["w_in"], mode="nn", out_dtype=F32, tm=1024, tn=896, tk=2048, b_parts=4, b_lead=(l,), name=f"proj_in_{t}")
        qn, kn, vb, ug, vn = _proj_post(z, p["q_norm_g"], p["k_norm_g"], p["sgu_ln_g"], p["sgu_ln_b"], cosf, sinf, f"proj_post_{t}")
        attn, sgu, mixed = _mixer_fwd(qn, kn, vb, ug, vn, p["w_s_bf16"], p["b_s_tile"], p["sink"], p["attn_out_g"],
                                      p["sgu_out_g"], f"mixer_{t}")
        x1 = _matmul(mixed, w["w_o"], mode="nn", out_dtype=F32, tm=1024, tn=512, tk=2048, res=x, b_lead=(l,), name=f"proj_out_{t}")
        h2 = _rms_fwd(x1, p["norm2_g"], f"norm2_{t}")
        a_pre = _matmul(h2, w["w_up"], mode="nn", out_dtype=F32, tm=1024, tn=1408, tk=2048, b_parts=4, out_parts=2,
                        b_lead=(l,), name=f"ffn_up_{t}")
        act = _conv_gate_fwd(a_pre, w["conv_w"][l], p["conv_b"], f"conv_gate_{t}")
        x2 = _matmul(act, w["w_down"], mode="nn", out_dtype=F32, tm=1024, tn=512, tk=2816, res=x1, b_lead=(l,), name=f"ffn_down_{t}")
        saved.append(dict(x=x, h=h, z=z, qn=qn, kn=kn, vb=vb, ug=ug, vn=vn, attn=attn, sgu=sgu, mixed=mixed, x1=x1, h2=h2,
                          a_pre=a_pre, act=act))
        x = x2
    loss_tile, dx, dxb = _loss_head(x, target, "loss_head")
    big_grads, small_grads = [None] * DEPTH, [None] * DEPTH
    for l in reversed(range(DEPTH)):
        w, p, sv = big, small[l], saved[l]
        t = f"l{l}"
        g_down = _matmul(sv["act"], dxb, mode="tn", out_dtype=BF16, tm=512, tn=1024, tk=2048, name=f"g_w_down_{t}")
        dact = _matmul(dxb, w["w_down"], mode="nt", out_dtype=F32, tm=1024, tn=512, tk=2048, b_lead=(l,), name=f"d_act_{t}")
        dap, dcw, dcb = _conv_gate_bwd(sv["a_pre"], w["conv_w"][l], p["conv_b"], dact, f"conv_gate_bwd_{t}")
        g_up = _matmul(sv["h2"], dap, mode="tn", out_dtype=BF16, tm=1024, tn=1408, tk=2048, b_parts=2, out_parts=4,
                       name=f"g_w_up_{t}")
        dh2 = _matmul(dap, w["w_up"], mode="nt", out_dtype=F32, tm=1024, tn=512, tk=2816, a_parts=2, b_parts=4, b_lead=(l,),
                      name=f"d_h2_{t}")
        dx1, dx1b, dg2 = _rms_bwd(sv["x1"], p["norm2_g"], dh2, dx, f"norm2_bwd_{t}")
        g_o = _matmul(sv["mixed"], dx1b, mode="tn", out_dtype=BF16, tm=1024, tn=512, tk=2048, name=f"g_w_o_{t}")
        dmixed = _matmul(dx1b, w["w_o"], mode="nt", out_dtype=F32, tm=1024, tn=512, tk=2048, b_lead=(l,), name=f"d_mixed_{t}")
        dqn, dkn, dvb, dug, dvn, dws, dbs, dsk, dga, dgs = _mixer_bwd(
            sv["qn"], sv["kn"], sv["vb"], sv["ug"], sv["vn"], sv["attn"], sv["sgu"], dmixed, p["w_s_bf16"], p["b_s_tile"],
            p["sink"], p["attn_out_g"], p["sgu_out_g"], f"mixer_bwd_{t}")
        dz, dqg, dkg, dlg, dlb = _proj_post_bwd(sv["z"], dqn, dkn, dvb, dug, dvn, p["q_norm_g"], p["k_norm_g"], p["sgu_ln_g"],
                                                 cosf, sinf, f"proj_post_bwd_{t}")
        g_in = _matmul(sv["h"], dz, mode="tn", out_dtype=BF16, tm=1024, tn=896, tk=2048, out_parts=4, name=f"g_w_in_{t}")
        dh = _matmul(dz, w["w_in"], mode="nt", out_dtype=F32, tm=1024, tn=512, tk=896, b_parts=4, b_lead=(l,), name=f"d_h_{t}")
        dx, dxb, dg1 = _rms_bwd(sv["x"], p["norm1_g"], dh, dx1, f"norm1_bwd_{t}")
        big_grads[l] = dict(w_in=g_in, w_o=g_o, w_up=g_up, w_down=g_down)
        small_grads[l] = dict(
            norm1_g=dg1[0], q_norm_g=dqg[0], k_norm_g=dkg[0], sink=dsk[:, 0], sgu_ln_g=dlg[0], sgu_ln_b=dlb[0], w_s=dws,
            b_s=dbs[:, :, 0], attn_out_g=dga[0], sgu_out_g=dgs[0], norm2_g=dg2[0],
            conv_w=jnp.concatenate([dcw[0], dcw[1]], axis=-1), conv_b=jnp.concatenate([dcb[0, 0], dcb[1, 0]], axis=-1))
    return loss_tile, dx, big_grads, small_grads


def _small_views(l, norm1_g, q_norm_g, k_norm_g, sink, sgu_ln_g, sgu_ln_b, w_s, b_s, attn_out_g, sgu_out_g, norm2_g, conv_b):
    return dict(
        norm1_g=norm1_g[l][None], q_norm_g=q_norm_g[l][None], k_norm_g=k_norm_g[l][None], sink=sink[l],
        sgu_ln_g=sgu_ln_g[l][None], sgu_ln_b=sgu_ln_b[l][None], w_s_bf16=w_s[l].astype(BF16),
        b_s_tile=jnp.broadcast_to(b_s[l][:, :, None], (N_GMLP_HEADS, BLOCK, BLOCK)), attn_out_g=attn_out_g[l][None],
        sgu_out_g=sgu_out_g[l][None], norm2_g=norm2_g[l][None], conv_b=conv_b[l][None])


SMALL_NAMES = ("norm1_g", "q_norm_g", "k_norm_g", "sink", "sgu_ln_g", "sgu_ln_b", "w_s", "b_s", "attn_out_g", "sgu_out_g",
               "norm2_g", "conv_w", "conv_b")
BIG_NAMES = ("w_in", "w_o", "w_up", "w_down")
PACK_LANES = 128
PACK_ALIGN = 8 * PACK_LANES


def _pack_rows(shape):
    return -(-math.prod(shape) // PACK_ALIGN) * 8


def _pack(arrays):
    parts = []
    for a in arrays:
        flat = a.reshape(-1)
        parts.append(jnp.pad(flat, (0, _pack_rows(a.shape) * PACK_LANES - flat.shape[0])).reshape(-1, PACK_LANES))
    return jnp.concatenate(parts, axis=0)


def _unpack(packed, shapes):
    out, at = [], 0
    for shp in shapes:
        rows = _pack_rows(shp)
        out.append(packed[at:at + rows].reshape(-1)[:math.prod(shp)].reshape(shp))
        at += rows
    return out


def kernel(x, norm1_g, w_in, q_norm_g, k_norm_g, sink, sgu_ln_g, sgu_ln_b, w_s, b_s, attn_out_g, sgu_out_g, w_o, norm2_g, w_up, conv_w, conv_b, w_down, loss_target, m_norm1_g, m_w_in, m_q_norm_g, m_k_norm_g, m_sink, m_sgu_ln_g, m_sgu_ln_b, m_w_s, m_b_s, m_attn_out_g, m_sgu_out_g, m_w_o, m_norm2_g, m_w_up, m_conv_w, m_conv_b, m_w_down, v_norm1_g, v_w_in, v_q_norm_g, v_k_norm_g, v_sink, v_sgu_ln_g, v_sgu_ln_b, v_w_s, v_b_s, v_attn_out_g, v_sgu_out_g, v_w_o, v_norm2_g, v_w_up, v_conv_w, v_conv_b, v_w_down):
    weights = dict(norm1_g=norm1_g, w_in=w_in, q_norm_g=q_norm_g, k_norm_g=k_norm_g, sink=sink, sgu_ln_g=sgu_ln_g,
                   sgu_ln_b=sgu_ln_b, w_s=w_s, b_s=b_s, attn_out_g=attn_out_g, sgu_out_g=sgu_out_g, w_o=w_o, norm2_g=norm2_g,
                   w_up=w_up, conv_w=conv_w, conv_b=conv_b, w_down=w_down)
    m_in = dict(norm1_g=m_norm1_g, w_in=m_w_in, q_norm_g=m_q_norm_g, k_norm_g=m_k_norm_g, sink=m_sink, sgu_ln_g=m_sgu_ln_g,
                sgu_ln_b=m_sgu_ln_b, w_s=m_w_s, b_s=m_b_s, attn_out_g=m_attn_out_g, sgu_out_g=m_sgu_out_g, w_o=m_w_o,
                norm2_g=m_norm2_g, w_up=m_w_up, conv_w=m_conv_w, conv_b=m_conv_b, w_down=m_w_down)
    v_in = dict(norm1_g=v_norm1_g, w_in=v_w_in, q_norm_g=v_q_norm_g, k_norm_g=v_k_norm_g, sink=v_sink, sgu_ln_g=v_sgu_ln_g,
                sgu_ln_b=v_sgu_ln_b, w_s=v_w_s, b_s=v_b_s, attn_out_g=v_attn_out_g, sgu_out_g=v_sgu_out_g, w_o=v_w_o,
                norm2_g=v_norm2_g, w_up=v_w_up, conv_w=v_conv_w, conv_b=v_conv_b, w_down=v_w_down)
    cx, cy, cc = lax.axis_index("x"), lax.axis_index("y"), lax.axis_index("c")
    j_me = 2 * cx + cy
    c_arr = jnp.reshape(cc, (1,)).astype(jnp.int32)
    j_arr = jnp.reshape(j_me, (1,)).astype(jnp.int32)

    placed = []
    for name in BIG_NAMES:
        wsh = weights[name]
        l, r, cdim = wsh.shape
        placed.append(_place_shard(wsh.reshape(l, 2, r // 2, cdim), j_arr, BF16, f"place_{name}"))
    placed.append(_place_shard(conv_w[:, None], j_arr, F32, "place_conv_w"))
    full = _gather_weights(placed, [True] * len(BIG_NAMES) + [False])
    full = dict(zip(BIG_NAMES + ("conv_w",), full))
    big = dict(
        w_in=full["w_in"].reshape(DEPTH, N_CHIPS, D_MODEL, IN_WIDTH // N_CHIPS),
        w_o=full["w_o"].reshape(DEPTH, D_MODEL, D_MODEL),
        w_up=full["w_up"].reshape(DEPTH, N_CHIPS, D_MODEL, 2 * D_FF // N_CHIPS),
        w_down=full["w_down"].reshape(DEPTH, D_FF, D_MODEL),
        conv_w=jnp.transpose(full["conv_w"][:, :, 0], (0, 2, 1, 3)).reshape(DEPTH, 3, 2 * D_FF))
    small =[_small_views(l, norm1_g, q_norm_g, k_norm_g, sink, sgu_ln_g, sgu_ln_b, w_s, b_s, attn_out_g, sgu_out_g, norm2_g,
                          conv_b) for l in range(DEPTH)]

    loss_tile, dx, big_grads, small_grads = _local_step(x[0], loss_target[0], big, small)

    small_partials = [jnp.stack([small_grads[l][nm] for l in range(DEPTH)]) for nm in SMALL_NAMES]
    small_shapes = [a.shape for a in small_partials]
    reduced = _all_reduce_small(_pack(small_partials + [loss_tile[0, 0:1]]))
    small_full = dict(zip(SMALL_NAMES + ("loss",), _unpack(reduced, small_shapes + [(1,)])))
    loss = small_full.pop("loss")[0]
    cw_cols = 2 * D_FF // N_CHIPS
    small_full["conv_w"] = lax.dynamic_slice_in_dim(small_full["conv_w"], j_me * cw_cols, cw_cols, axis=2)
    pw, pg, pm, pv = (_pack([src[nm] for nm in SMALL_NAMES]) for src in (weights, small_full, m_in, v_in))
    sd, sm, sv = _adamw(pw, pg, pm, pv, "adamw_small", budget=1 << 20)
    shapes = [weights[nm].shape for nm in SMALL_NAMES]
    grads = dict(small_full)
    delta = dict(zip(SMALL_NAMES, _unpack(sd, shapes)))
    new_m = dict(zip(SMALL_NAMES, _unpack(sm, shapes)))
    new_v = dict(zip(SMALL_NAMES, _unpack(sv, shapes)))

    g5s = []
    for l in range(DEPTH):
        for name in BIG_NAMES:
            g = big_grads[l][name]
            rows, cols = weights[name].shape[1:]
            g5s.append(g.reshape(N_CHIPS, 2, rows // 2, cols))
    from_sibling = _swap_halves_to_sibling(g5s)
    p4s = [_pair_sum(g5, rv, c_arr, f"pair_sum_{i}") for i, (g5, rv) in enumerate(zip(g5s, from_sibling))]
    from_chips = _scatter_to_chips(p4s)
    reduced_halves = []
    for i in range(len(BIG_NAMES)):
        buf = None
        for l in range(DEPTH):
            a = l * len(BIG_NAMES) + i
            buf = _chip_sum(p4s[a], from_chips[a], j_arr, c_arr, l, buf, f"chip_sum_{a}")
        reduced_halves.append(buf)
    joined = _join_halves(reduced_halves)
    for i, name in enumerate(BIG_NAMES):
        l_, rows, cols = weights[name].shape
        flat = lambda a: a.reshape(l_ * rows, cols)
        bd, bm, bv = _adamw(flat(weights[name]), flat(joined[i]), flat(m_in[name]), flat(v_in[name]), f"adamw_{name}")
        grads[name] = joined[i].reshape(l_, rows, cols)
        delta[name], new_m[name], new_v[name] = (a.reshape(l_, rows, cols) for a in (bd, bm, bv))

    order = ("norm1_g", "w_in", "q_norm_g", "k_norm_g", "sink", "sgu_ln_g", "sgu_ln_b", "w_s", "b_s", "attn_out_g", "sgu_out_g",
             "w_o", "norm2_g", "w_up", "conv_w", "conv_b", "w_down")
    return (loss, dx[None], *[grads[nm] for nm in order], *[delta[nm] for nm in order], *[new_m[nm] for nm in order],
            *[new_v[nm] for nm in order])
```

```python
import functools
import math

import jax
import jax.numpy as jnp
from jax import lax
from jax.experimental import pallas as pl
from jax.experimental.pallas import tpu as pltpu

F32 = jnp.float32
BF16 = jnp.bfloat16

D_MODEL = 2048
HEAD_DIM = 128
ATTN_WIDTH = 1024
N_Q_HEADS = 8
N_KV_HEADS = 2
GQA_GROUP = 4
KV_WIDTH = 256
GMLP_WIDTH = 1024
N_GMLP_HEADS = 8
BLOCK = 128
IN_WIDTH = 3584
D_FF = 5632
DEPTH = 2
EPS = 1e-6
MASK_VALUE = -1e30
ROPE_THETA = 10000.0
N_CHIPS = 4

ADAM_LR = 0.001
ADAM_B1 = 0.9
ADAM_B2 = 0.999
ADAM_EPS = 1e-08
ADAM_WD = 0.01
ADAM_STEP = 10

V7X_VMEM_LIMIT = 48 * 1024 * 1024
MESH = pl.DeviceIdType.MESH

_GELU_C = math.sqrt(2.0 / math.pi)
_GELU_A = 0.044715


def _params(sem=None):
    return pltpu.CompilerParams(dimension_semantics=sem, vmem_limit_bytes=V7X_VMEM_LIMIT)


ANY = pl.BlockSpec(memory_space=pl.ANY)


class _Order:
    last = None


def _ordered_call(body, *, token_index=0, **kw):
    def run(*operands):
        tok = _Order.last
        if tok is None or any(op is tok for op in operands):
            call = pl.pallas_call(body, **kw)
        else:
            n_in = len(operands)

            def ordered_body(*refs):
                return body(*refs[:n_in], *refs[n_in + 1:])

            kw2 = dict(kw)
            if "grid_spec" in kw2:
                gs = kw2["grid_spec"]
                kw2["grid_spec"] = pltpu.PrefetchScalarGridSpec(
                    num_scalar_prefetch=gs.num_scalar_prefetch, grid=gs.grid, in_specs=list(gs.in_specs) + [ANY],
                    out_specs=gs.out_specs, scratch_shapes=gs.scratch_shapes)
            else:
                kw2["in_specs"] = list(kw2["in_specs"]) + [ANY]
            call = pl.pallas_call(ordered_body, **kw2)
            operands = operands + (tok,)
        out = call(*operands)
        _Order.last = out[token_index] if isinstance(out, (tuple, list)) else out
        return out

    return run


def _gelu(x):
    return x * (0.5 * (1.0 + jnp.tanh(_GELU_C * (x + _GELU_A * (x * x * x)))))


def _gelu_grad(x):
    x2 = x * x
    t = jnp.tanh(_GELU_C * (x + _GELU_A * (x * x2)))
    return 0.5 * (1.0 + t) + 0.5 * x * (1.0 - t * t) * (_GELU_C * (1.0 + 3.0 * _GELU_A * x2))


def _mean_last(x):
    return jnp.mean(x, axis=-1, keepdims=True)


def _sum_rows(x):
    return jnp.sum(x, axis=0, keepdims=True)


def _sum_all(x):
    return jnp.sum(jnp.sum(x, axis=1, keepdims=True), axis=0, keepdims=True)


def _matmul(a, b, *, mode, out_dtype, tm, tn, tk, name, res=None, a_parts=0, b_parts=0, out_parts=0, b_lead=()):
    b_full = b
    b = jax.ShapeDtypeStruct(b.shape[len(b_lead):], b.dtype)
    if mode == "nn":
        assert not a_parts
        m, k = a.shape
        n = b.shape[0] * b.shape[2] if b_parts else b.shape[1]
    elif mode == "nt":
        m, k = (a.shape[1], a.shape[0] * a.shape[2]) if a_parts else a.shape
        n = b.shape[1] if b_parts else b.shape[0]
    else:
        assert not a_parts
        k, m = a.shape
        n = b.shape[0] * b.shape[2] if b_parts else b.shape[1]
    tm, tn, tk = min(tm, m), min(tn, n), min(tk, k)
    assert m % tm == 0 and n % tn == 0 and k % tk == 0, (name, m, n, k, tm, tn, tk)
    nm, nn, nk = m // tm, n // tn, k // tk

    def slab(idx, total_tiles, parts):
        per = total_tiles // parts
        assert per * parts == total_tiles, (name, total_tiles, parts)
        return idx // per, idx % per

    if mode == "nn":
        a_spec = pl.BlockSpec((tm, tk), lambda i, j, kk: (i, kk))
        if b_parts:
            b_spec = pl.BlockSpec((None, tk, tn), lambda i, j, kk: (slab(j, nn, b_parts)[0], kk, slab(j, nn, b_parts)[1]))
        else:
            b_spec = pl.BlockSpec((tk, tn), lambda i, j, kk: (kk, j))
        dims = (((1,), (0,)), ((), ()))
    elif mode == "nt":
        if a_parts:
            a_spec = pl.BlockSpec((None, tm, tk), lambda i, j, kk: (slab(kk, nk, a_parts)[0], i, slab(kk, nk, a_parts)[1]))
        else:
            a_spec = pl.BlockSpec((tm, tk), lambda i, j, kk: (i, kk))
        if b_parts:
            b_spec = pl.BlockSpec((None, tn, tk), lambda i, j, kk: (slab(kk, nk, b_parts)[0], j, slab(kk, nk, b_parts)[1]))
        else:
            b_spec = pl.BlockSpec((tn, tk), lambda i, j, kk: (j, kk))
        dims = (((1,), (1,)), ((), ()))
    else:
        a_spec = pl.BlockSpec((tk, tm), lambda i, j, kk: (kk, i))
        if b_parts:
            b_spec = pl.BlockSpec((None, tk, tn), lambda i, j, kk: (slab(j, nn, b_parts)[0], kk, slab(j, nn, b_parts)[1]))
        else:
            b_spec = pl.BlockSpec((tk, tn), lambda i, j, kk: (kk, j))
        dims = (((0,), (0,)), ((), ()))
    if out_parts:
        out_shape = jax.ShapeDtypeStruct((out_parts, m, n // out_parts), out_dtype)
        out_spec = pl.BlockSpec((None, tm, tn), lambda i, j, kk: (slab(j, nn, out_parts)[0], i, slab(j, nn, out_parts)[1]))
    else:
        out_shape = jax.ShapeDtypeStruct((m, n), out_dtype)
        out_spec = pl.BlockSpec((tm, tn), lambda i, j, kk: (i, j))
    if b_lead:
        inner_map = b_spec.index_map
        b_spec = pl.BlockSpec((None,) * len(b_lead) + tuple(b_spec.block_shape),
                              lambda i, j, kk: tuple(b_lead) + tuple(inner_map(i, j, kk)))
    in_specs = [a_spec, b_spec]
    operands = [a, b_full]
    if res is not None:
        in_specs.append(pl.BlockSpec((tm, tn), lambda i, j, kk: (i, j)))
        operands.append(res)

    def body(*refs):
        a_ref, b_ref = refs[0], refs[1]
        res_ref = refs[2] if res is not None else None
        o_ref = refs[3] if res is not None else refs[2]
        p = lax.dot_general(a_ref[...], b_ref[...], dims, preferred_element_type=F32)

        def finish(total):
            if res_ref is not None:
                total = res_ref[...] + total
            o_ref[...] = total.astype(out_dtype)

        if nk == 1:
            finish(p)
        else:
            acc_ref = refs[-1]
            kk = pl.program_id(2)

            @pl.when(kk == 0)
            def _():
                acc_ref[...] = p

            @pl.when(jnp.logical_and(kk > 0, kk < nk - 1))
            def _():
                acc_ref[...] += p

            @pl.when(kk == nk - 1)
            def _():
                finish(acc_ref[...] + p)

    scratch = [pltpu.VMEM((tm, tn), F32)] if nk > 1 else []
    return _ordered_call(
        body, name=name, out_shape=out_shape, grid=(nm, nn, nk), in_specs=in_specs, out_specs=out_spec,
        scratch_shapes=scratch, compiler_params=_params(("parallel", "parallel", "arbitrary")),
    )(*operands)


def _row_tile(s):
    return min(256, s)


def _rows(width, tr):
    return pl.BlockSpec((tr, width), lambda i: (i, 0))


def _const2(shape):
    return pl.BlockSpec(shape, lambda i: (0, 0))


def _rms_fwd(x, g, name):
    s, d = x.shape
    tr = _row_tile(s)

    def body(x_ref, g_ref, o_ref):
        xv = x_ref[...]
        r = lax.rsqrt(_mean_last(xv * xv) + EPS)
        o_ref[...] = (xv * r * g_ref[...]).astype(BF16)

    return _ordered_call(
        body, name=name, out_shape=jax.ShapeDtypeStruct((s, d), BF16), grid=(s // tr,),
        in_specs=[_rows(d, tr), _const2((1, d))], out_specs=_rows(d, tr), compiler_params=_params(("parallel",)),
    )(x, g)


def _rms_bwd(x, g, dh, dres, name):
    s, d = x.shape
    tr = _row_tile(s)

    def body(x_ref, g_ref, dh_ref, dres_ref, dx_ref, dxb_ref, dg_ref):
        xv, dy = x_ref[...], dh_ref[...]
        r = lax.rsqrt(_mean_last(xv * xv) + EPS)
        gdy = dy * g_ref[...]
        dx = dres_ref[...] + r * gdy - xv * ((r * r * r) * _mean_last(xv * gdy))
        dx_ref[...] = dx
        dxb_ref[...] = dx.astype(BF16)

        @pl.when(pl.program_id(0) == 0)
        def _():
            dg_ref[...] = jnp.zeros_like(dg_ref)

        dg_ref[...] += _sum_rows(xv * r * dy)

    return _ordered_call(
        body, name=name,
        out_shape=(jax.ShapeDtypeStruct((s, d), F32), jax.ShapeDtypeStruct((s, d), BF16), jax.ShapeDtypeStruct((1, d), F32)),
        grid=(s // tr,), in_specs=[_rows(d, tr), _const2((1, d)), _rows(d, tr), _rows(d, tr)],
        out_specs=(_rows(d, tr), _rows(d, tr), _const2((1, d))), compiler_params=_params(("arbitrary",)),
    )(x, g, dh, dres)


Q0, K0, V0, GU0, GV0 = 0, ATTN_WIDTH, ATTN_WIDTH + KV_WIDTH, ATTN_WIDTH + 2 * KV_WIDTH, ATTN_WIDTH + 2 * KV_WIDTH + GMLP_WIDTH


def _head(h, base=0):
    return slice(base + h * HEAD_DIM, base + (h + 1) * HEAD_DIM)


def _proj_post(z, qg, kg, lg, lb, cosf, sinf, name):
    s = z.shape[0]
    tr = _row_tile(s)

    def body(z_ref, qg_ref, kg_ref, lg_ref, lb_ref, cos_ref, sin_ref, qn_ref, kn_ref, vb_ref, ug_ref, vn_ref):
        cos, sin = cos_ref[...], sin_ref[...]

        def norm_rope(xh, g):
            y = xh * lax.rsqrt(_mean_last(xh * xh) + EPS) * g
            return y * cos + pltpu.roll(y, HEAD_DIM // 2, 1) * sin

        for h in range(N_Q_HEADS):
            qn_ref[:, _head(h)] = norm_rope(z_ref[:, _head(h, Q0)], qg_ref[...]).astype(BF16)
        for h in range(N_KV_HEADS):
            kn_ref[:, _head(h)] = norm_rope(z_ref[:, _head(h, K0)], kg_ref[...]).astype(BF16)
        vb_ref[...] = z_ref[:, V0:GU0].astype(BF16)
        ug_ref[...] = _gelu(z_ref[:, GU0:GV0])
        vg = _gelu(z_ref[:, GV0:IN_WIDTH])
        xc = vg - _mean_last(vg)
        y = xc * lax.rsqrt(_mean_last(xc * xc) + EPS)
        vn_ref[...] = (y * lg_ref[...] + lb_ref[...]).astype(BF16)

    return _ordered_call(
        body, name=name,
        out_shape=(jax.ShapeDtypeStruct((s, ATTN_WIDTH), BF16), jax.ShapeDtypeStruct((s, KV_WIDTH), BF16),
                   jax.ShapeDtypeStruct((s, KV_WIDTH), BF16), jax.ShapeDtypeStruct((s, GMLP_WIDTH), F32),
                   jax.ShapeDtypeStruct((s, GMLP_WIDTH), BF16)),
        grid=(s // tr,),
        in_specs=[_rows(IN_WIDTH, tr), _const2((1, HEAD_DIM)), _const2((1, HEAD_DIM)), _const2((1, GMLP_WIDTH)),
                  _const2((1, GMLP_WIDTH)), _rows(HEAD_DIM, tr), _rows(HEAD_DIM, tr)],
        out_specs=(_rows(ATTN_WIDTH, tr), _rows(KV_WIDTH, tr), _rows(KV_WIDTH, tr), _rows(GMLP_WIDTH, tr), _rows(GMLP_WIDTH, tr)),
        compiler_params=_params(("parallel",)),
    )(z, qg, kg, lg, lb, cosf, sinf)


def _proj_post_bwd(z, dqn, dkn, dvb, dug, dvn, qg, kg, lg, cosf, sinf, name):
    s = z.shape[0]
    tr = _row_tile(s)

    def body(z_ref, dqn_ref, dkn_ref, dvb_ref, dug_ref, dvn_ref, qg_ref, kg_ref, lg_ref, cos_ref, sin_ref,
             dz_ref, dqg_ref, dkg_ref, dlg_ref, dlb_ref):
        cos, sin = cos_ref[...], sin_ref[...]

        @pl.when(pl.program_id(0) == 0)
        def _():
            dqg_ref[...] = jnp.zeros_like(dqg_ref)
            dkg_ref[...] = jnp.zeros_like(dkg_ref)
            dlg_ref[...] = jnp.zeros_like(dlg_ref)
            dlb_ref[...] = jnp.zeros_like(dlb_ref)

        def norm_rope_bwd(xh, g, dout):
            dy = dout * cos - pltpu.roll(dout, HEAD_DIM // 2, 1) * sin
            r = lax.rsqrt(_mean_last(xh * xh) + EPS)
            xhat = xh * r
            gdy = dy * g
            return r * (gdy - xhat * _mean_last(xhat * gdy)), _sum_rows(xhat * dy)

        dqg = jnp.zeros((1, HEAD_DIM), F32)
        for h in range(N_Q_HEADS):
            dx, dg = norm_rope_bwd(z_ref[:, _head(h, Q0)], qg_ref[...], dqn_ref[:, _head(h)])
            dz_ref[:, _head(h, Q0)] = dx.astype(BF16)
            dqg = dqg + dg
        dqg_ref[...] += dqg
        dkg = jnp.zeros((1, HEAD_DIM), F32)
        for h in range(N_KV_HEADS):
            dx, dg = norm_rope_bwd(z_ref[:, _head(h, K0)], kg_ref[...], dkn_ref[:, _head(h)])
            dz_ref[:, _head(h, K0)] = dx.astype(BF16)
            dkg = dkg + dg
        dkg_ref[...] += dkg
        dz_ref[:, V0:GU0] = dvb_ref[...].astype(BF16)
        dz_ref[:, GU0:GV0] = (dug_ref[...] * _gelu_grad(z_ref[:, GU0:GV0])).astype(BF16)
        gv = z_ref[:, GV0:IN_WIDTH]
        vg = _gelu(gv)
        xc = vg - _mean_last(vg)
        r = lax.rsqrt(_mean_last(xc * xc) + EPS)
        xhat = xc * r
        dvn_v = dvn_ref[...]
        dlg_ref[...] += _sum_rows(xhat * dvn_v)
        dlb_ref[...] += _sum_rows(dvn_v)
        dxh = dvn_v * lg_ref[...]
        dvg = r * (dxh - _mean_last(dxh) - xhat * _mean_last(dxh * xhat))
        dz_ref[:, GV0:IN_WIDTH] = (dvg * _gelu_grad(gv)).astype(BF16)

    return _ordered_call(
        body, name=name,
        out_shape=(jax.ShapeDtypeStruct((s, IN_WIDTH), BF16), jax.ShapeDtypeStruct((1, HEAD_DIM), F32),
                   jax.ShapeDtypeStruct((1, HEAD_DIM), F32), jax.ShapeDtypeStruct((1, GMLP_WIDTH), F32),
                   jax.ShapeDtypeStruct((1, GMLP_WIDTH), F32)),
        grid=(s // tr,),
        in_specs=[_rows(IN_WIDTH, tr), _rows(ATTN_WIDTH, tr), _rows(KV_WIDTH, tr), _rows(KV_WIDTH, tr), _rows(GMLP_WIDTH, tr),
                  _rows(GMLP_WIDTH, tr), _const2((1, HEAD_DIM)), _const2((1, HEAD_DIM)), _const2((1, GMLP_WIDTH)),
                  _rows(HEAD_DIM, tr), _rows(HEAD_DIM, tr)],
        out_specs=(_rows(IN_WIDTH, tr), _const2((1, HEAD_DIM)), _const2((1, HEAD_DIM)), _const2((1, GMLP_WIDTH)),
                   _const2((1, GMLP_WIDTH))),
        compiler_params=_params(("arbitrary",)),
    )(z, dqn, dkn, dvb, dug, dvn, qg, kg, lg, cosf, sinf)


def _band_valid(n, s):
    i = lax.broadcasted_iota(jnp.int32, (BLOCK, 3 * BLOCK), 0)
    j = lax.broadcasted_iota(jnp.int32, (BLOCK, 3 * BLOCK), 1)
    k_pos = n * BLOCK - BLOCK + j
    return (jnp.abs(j - BLOCK - i) <= BLOCK) & (k_pos >= 0) & (k_pos < s)


def _probs(q, kb, sink_h, valid):
    sc = lax.dot_general(q, kb, (((1,), (1,)), ((), ())), preferred_element_type=F32) * (HEAD_DIM ** -0.5)
    sc = jnp.where(valid, sc, MASK_VALUE)
    m = jnp.maximum(jnp.max(sc, axis=-1, keepdims=True), sink_h)
    p = jnp.exp(sc - m)
    es = jnp.exp(sink_h - m)
    den = jnp.sum(p, axis=-1, keepdims=True) + es
    return p / den, es / den


def _band_specs(width, nb):
    return [pl.BlockSpec((BLOCK, width), lambda n: (jnp.maximum(n - 1, 0), 0)),
            pl.BlockSpec((BLOCK, width), lambda n: (n, 0)),
            pl.BlockSpec((BLOCK, width), lambda n: (jnp.minimum(n + 1, nb - 1), 0))]


def _blk(width):
    return pl.BlockSpec((BLOCK, width), lambda n: (n, 0))


def _whole3(shape):
    return pl.BlockSpec(shape, lambda n: (0, 0, 0))


def _smem():
    return pl.BlockSpec(memory_space=pltpu.SMEM)


def _mixer_fwd(qn, kn, vb, ug, vn, wsb, bsb, sink, ga, gs, name):
    s = qn.shape[0]
    nb = s // BLOCK

    def body(sink_ref, q_ref, kp_ref, kc_ref, kx_ref, vp_ref, vc_ref, vx_ref, ug_ref, vn_ref, ws_ref, bs_ref, ga_ref, gs_ref,
             attn_ref, sgu_ref, mix_ref):
        n = pl.program_id(0)
        valid = _band_valid(n, s)
        ssq = jnp.zeros((BLOCK, 1), F32)
        for kh in range(N_KV_HEADS):
            kb = jnp.concatenate([kp_ref[:, _head(kh)], kc_ref[:, _head(kh)], kx_ref[:, _head(kh)]], axis=0)
            vbd = jnp.concatenate([vp_ref[:, _head(kh)], vc_ref[:, _head(kh)], vx_ref[:, _head(kh)]], axis=0)
            for g in range(GQA_GROUP):
                h = kh * GQA_GROUP + g
                p, _ = _probs(q_ref[:, _head(h)], kb, sink_ref[h], valid)
                o = jnp.dot(p.astype(BF16), vbd, preferred_element_type=F32)
                attn_ref[:, _head(h)] = o
                ssq = ssq + jnp.sum(o * o, axis=-1, keepdims=True)
        r = lax.rsqrt(ssq * (1.0 / ATTN_WIDTH) + EPS)
        mix_ref[:, 0:ATTN_WIDTH] = (attn_ref[...] * r * ga_ref[...]).astype(BF16)
        ssq = jnp.zeros((BLOCK, 1), F32)
        for h in range(N_GMLP_HEADS):
            f = jnp.dot(ws_ref[h], vn_ref[:, _head(h)], preferred_element_type=F32) + bs_ref[h]
            o = ug_ref[:, _head(h)] * f
            sgu_ref[:, _head(h)] = o
            ssq = ssq + jnp.sum(o * o, axis=-1, keepdims=True)
        r = lax.rsqrt(ssq * (1.0 / GMLP_WIDTH) + EPS)
        mix_ref[:, ATTN_WIDTH:D_MODEL] = (sgu_ref[...] * r * gs_ref[...]).astype(BF16)

    hh = (N_GMLP_HEADS, BLOCK, BLOCK)
    return _ordered_call(
        body, name=name,
        out_shape=(jax.ShapeDtypeStruct((s, ATTN_WIDTH), F32), jax.ShapeDtypeStruct((s, GMLP_WIDTH), F32),
                   jax.ShapeDtypeStruct((s, D_MODEL), BF16)),
        grid=(nb,),
        in_specs=[_smem(), _blk(ATTN_WIDTH)] + _band_specs(KV_WIDTH, nb) + _band_specs(KV_WIDTH, nb)
        + [_blk(GMLP_WIDTH), _blk(GMLP_WIDTH), _whole3(hh), _whole3(hh),
           pl.BlockSpec((1, ATTN_WIDTH), lambda n: (0, 0)), pl.BlockSpec((1, GMLP_WIDTH), lambda n: (0, 0))],
        out_specs=(_blk(ATTN_WIDTH), _blk(GMLP_WIDTH), _blk(D_MODEL)),
        compiler_params=_params(("parallel",)),
    )(sink, qn, kn, kn, kn, vb, vb, vb, ug, vn, wsb, bsb, ga, gs)


def _mixer_bwd(qn, kn, vb, ug, vn, attn, sgu, dmixed, wsb, bsb, sink, ga, gs, name):
    s = qn.shape[0]
    nb = s // BLOCK
    tn_dims = (((0,), (0,)), ((), ()))
    nt_dims = (((1,), (1,)), ((), ()))

    def body(sink_ref, q_ref, kp_ref, kc_ref, kx_ref, vp_ref, vc_ref, vx_ref, ug_ref, vn_ref, attn_ref, sgu_ref, dm_ref,
             ws_ref, bs_ref, ga_ref, gs_ref,
             dq_ref, dk_ref, dv_ref, dug_ref, dvn_ref, dws_ref, dbs_ref, dsk_ref, dga_ref, dgs_ref, dk_acc, dv_acc):
        n = pl.program_id(0)

        @pl.when(n == 0)
        def _():
            for ref in (dk_acc, dv_acc, dws_ref, dbs_ref, dsk_ref, dga_ref, dgs_ref):
                ref[...] = jnp.zeros_like(ref)

        def out_norm_bwd(o, g, dy):
            r = lax.rsqrt(_mean_last(o * o) + EPS)
            gdy = dy * g
            return r * gdy - o * ((r * r * r) * _mean_last(o * gdy)), _sum_rows(o * r * dy)

        d_attn, dga = out_norm_bwd(attn_ref[...], ga_ref[...], dm_ref[:, 0:ATTN_WIDTH])
        dga_ref[...] += dga
        d_sgu, dgs = out_norm_bwd(sgu_ref[...], gs_ref[...], dm_ref[:, ATTN_WIDTH:D_MODEL])
        dgs_ref[...] += dgs

        for h in range(N_GMLP_HEADS):
            vn_h = vn_ref[:, _head(h)]
            f = jnp.dot(ws_ref[h], vn_h, preferred_element_type=F32) + bs_ref[h]
            ds_h = d_sgu[:, _head(h)]
            dug_ref[:, _head(h)] = ds_h * f
            df = ds_h * ug_ref[:, _head(h)]
            dfb = df.astype(BF16)
            dvn_ref[:, _head(h)] = lax.dot_general(ws_ref[h], dfb, tn_dims, preferred_element_type=F32)
            dws_ref[h] += lax.dot_general(dfb, vn_h, nt_dims, preferred_element_type=F32)
            dbs_ref[h] += jnp.broadcast_to(jnp.sum(df, axis=-1, keepdims=True), (BLOCK, BLOCK))

        valid = _band_valid(n, s)
        row0 = pl.multiple_of(n * BLOCK, BLOCK)
        for kh in range(N_KV_HEADS):
            kb = jnp.concatenate([kp_ref[:, _head(kh)], kc_ref[:, _head(kh)], kx_ref[:, _head(kh)]], axis=0)
            vbd = jnp.concatenate([vp_ref[:, _head(kh)], vc_ref[:, _head(kh)], vx_ref[:, _head(kh)]], axis=0)
            dkb = jnp.zeros((3 * BLOCK, HEAD_DIM), F32)
            dvb = jnp.zeros((3 * BLOCK, HEAD_DIM), F32)
            for g in range(GQA_GROUP):
                h = kh * GQA_GROUP + g
                q = q_ref[:, _head(h)]
                p, p_sink = _probs(q, kb, sink_ref[h], valid)
                do = d_attn[:, _head(h)].astype(BF16)
                dp = lax.dot_general(do, vbd, nt_dims, preferred_element_type=F32)
                delta = jnp.sum(p * dp, axis=-1, keepdims=True)
                dsc = (p * (dp - delta) * (HEAD_DIM ** -0.5)).astype(BF16)
                dsk_ref[h:h + 1, :] += jnp.broadcast_to(_sum_all(-(p_sink * delta)), (1, BLOCK))
                dq_ref[:, _head(h)] = jnp.dot(dsc, kb, preferred_element_type=F32)
                dkb = dkb + lax.dot_general(dsc, q, tn_dims, preferred_element_type=F32)
                dvb = dvb + lax.dot_general(p.astype(BF16), do, tn_dims, preferred_element_type=F32)
            dk_acc[pl.ds(row0, 3 * BLOCK), _head(kh)] += dkb
            dv_acc[pl.ds(row0, 3 * BLOCK), _head(kh)] += dvb

        @pl.when(n == nb - 1)
        def _():
            dk_ref[...] = dk_acc[BLOCK:BLOCK + s, :]
            dv_ref[...] = dv_acc[BLOCK:BLOCK + s, :]

    hh = (N_GMLP_HEADS, BLOCK, BLOCK)
    full_kv = pl.BlockSpec((s, KV_WIDTH), lambda n: (0, 0))
    return _ordered_call(
        body, name=name,
        out_shape=(jax.ShapeDtypeStruct((s, ATTN_WIDTH), F32), jax.ShapeDtypeStruct((s, KV_WIDTH), F32),
                   jax.ShapeDtypeStruct((s, KV_WIDTH), F32), jax.ShapeDtypeStruct((s, GMLP_WIDTH), F32),
                   jax.ShapeDtypeStruct((s, GMLP_WIDTH), F32), jax.ShapeDtypeStruct(hh, F32), jax.ShapeDtypeStruct(hh, F32),
                   jax.ShapeDtypeStruct((N_Q_HEADS, BLOCK), F32), jax.ShapeDtypeStruct((1, ATTN_WIDTH), F32),
                   jax.ShapeDtypeStruct((1, GMLP_WIDTH), F32)),
        grid=(nb,),
        in_specs=[_smem(), _blk(ATTN_WIDTH)] + _band_specs(KV_WIDTH, nb) + _band_specs(KV_WIDTH, nb)
        + [_blk(GMLP_WIDTH), _blk(GMLP_WIDTH), _blk(ATTN_WIDTH), _blk(GMLP_WIDTH), _blk(D_MODEL), _whole3(hh), _whole3(hh),
           pl.BlockSpec((1, ATTN_WIDTH), lambda n: (0, 0)), pl.BlockSpec((1, GMLP_WIDTH), lambda n: (0, 0))],
        out_specs=(_blk(ATTN_WIDTH), full_kv, full_kv, _blk(GMLP_WIDTH), _blk(GMLP_WIDTH), _whole3(hh), _whole3(hh),
                   pl.BlockSpec((N_Q_HEADS, BLOCK), lambda n: (0, 0)), pl.BlockSpec((1, ATTN_WIDTH), lambda n: (0, 0)),
                   pl.BlockSpec((1, GMLP_WIDTH), lambda n: (0, 0))),
        scratch_shapes=[pltpu.VMEM((s + 2 * BLOCK, KV_WIDTH), F32), pltpu.VMEM((s + 2 * BLOCK, KV_WIDTH), F32)],
        compiler_params=_params(("arbitrary",)),
    )(sink, qn, kn, kn, kn, vb, vb, vb, ug, vn, attn, sgu, dmixed, wsb, bsb, ga, gs)


CONV_TILE = 128


def _shift_rows(a, rows):
    s = a.shape[0]
    prev = jnp.where(rows == 0, 0.0, pltpu.roll(a, 1, 0))
    nxt = jnp.where(rows == s - 1, 0.0, pltpu.roll(a, s - 1, 0))
    return prev, nxt


def _conv_specs(s):
    tc = CONV_TILE
    nj = D_FF // tc
    return (tc, nj, pl.BlockSpec((2, s, tc), lambda j: (0, 0, j)),
            [pl.BlockSpec((3, tc), lambda j: (0, j)), pl.BlockSpec((3, tc), lambda j: (0, j + nj))],
            [pl.BlockSpec((1, tc), lambda j: (0, j)), pl.BlockSpec((1, tc), lambda j: (0, j + nj))])


def _conv_gate_fwd(a_pre, cw, cb, name):
    s = a_pre.shape[1]
    tc, nj, a_spec, w_specs, b_specs = _conv_specs(s)

    def body(a_ref, wg_ref, wu_ref, bg_ref, bu_ref, act_ref):
        rows = lax.broadcasted_iota(jnp.int32, (s, tc), 0)

        def conv(a, w_ref, b_ref):
            prev, nxt = _shift_rows(a, rows)
            return b_ref[...] + prev * w_ref[0:1, :] + a * w_ref[1:2, :] + nxt * w_ref[2:3, :]

        g = conv(a_ref[0], wg_ref, bg_ref)
        u = conv(a_ref[1], wu_ref, bu_ref)
        act_ref[...] = (g * (1.0 / (1.0 + jnp.exp(-g))) * u).astype(BF16)

    return _ordered_call(
        body, name=name, out_shape=jax.ShapeDtypeStruct((s, D_FF), BF16), grid=(nj,),
        in_specs=[a_spec] + w_specs + b_specs, out_specs=pl.BlockSpec((s, tc), lambda j: (0, j)),
        compiler_params=_params(("parallel",)),
    )(a_pre, cw, cw, cb, cb)


def _conv_gate_bwd(a_pre, cw, cb, dact, name):
    s = a_pre.shape[1]
    tc, nj, a_spec, w_specs, b_specs = _conv_specs(s)

    def body(a_ref, wg_ref, wu_ref, bg_ref, bu_ref, dact_ref, dap_ref, dcw_ref, dcb_ref):
        rows = lax.broadcasted_iota(jnp.int32, (s, tc), 0)
        shifted = []
        pre = []
        for part, (w_ref, b_ref) in enumerate(((wg_ref, bg_ref), (wu_ref, bu_ref))):
            a = a_ref[part]
            prev, nxt = _shift_rows(a, rows)
            shifted.append((prev, a, nxt))
            pre.append(b_ref[...] + prev * w_ref[0:1, :] + a * w_ref[1:2, :] + nxt * w_ref[2:3, :])
        g, u = pre
        sg = 1.0 / (1.0 + jnp.exp(-g))
        dact_v = dact_ref[...]
        das = (dact_v * u * (sg * (1.0 + g * (1.0 - sg))), dact_v * (g * sg))
        for part, w_ref in enumerate((wg_ref, wu_ref)):
            da = das[part]
            prev, a, nxt = shifted[part]
            da_prev, da_next = _shift_rows(da, rows)
            dap_ref[part] = (da_next * w_ref[0:1, :] + da * w_ref[1:2, :] + da_prev * w_ref[2:3, :]).astype(BF16)
            dcw_ref[part, 0:1, :] = _sum_rows(prev * da)
            dcw_ref[part, 1:2, :] = _sum_rows(a * da)
            dcw_ref[part, 2:3, :] = _sum_rows(nxt * da)
            dcb_ref[part] = _sum_rows(da)

    return _ordered_call(
        body, name=name,
        out_shape=(jax.ShapeDtypeStruct((2, s, D_FF), BF16), jax.ShapeDtypeStruct((2, 3, D_FF), F32),
                   jax.ShapeDtypeStruct((2, 1, D_FF), F32)),
        grid=(nj,),
        in_specs=[a_spec] + w_specs + b_specs + [pl.BlockSpec((s, tc), lambda j: (0, j))],
        out_specs=(pl.BlockSpec((2, s, tc), lambda j: (0, 0, j)), pl.BlockSpec((2, 3, tc), lambda j: (0, 0, j)),
                   pl.BlockSpec((2, 1, tc), lambda j: (0, 0, j))),
        compiler_params=_params(("parallel",)),
    )(a_pre, cw, cw, cb, cb, dact)


def _loss_head(y, target, name):
    s, d = y.shape
    tr = _row_tile(s)

    def body(y_ref, t_ref, loss_ref, dy_ref, dyb_ref):
        err = y_ref[...] - t_ref[...]

        @pl.when(pl.program_id(0) == 0)
        def _():
            loss_ref[...] = jnp.zeros_like(loss_ref)

        loss_ref[...] += jnp.broadcast_to(0.5 * _sum_all(_mean_last(err * err)), (8, 128))
        dy = err * (1.0 / d)
        dy_ref[...] = dy
        dyb_ref[...] = dy.astype(BF16)

    return _ordered_call(
        body, name=name,
        out_shape=(jax.ShapeDtypeStruct((8, 128), F32), jax.ShapeDtypeStruct((s, d), F32), jax.ShapeDtypeStruct((s, d), BF16)),
        grid=(s // tr,), in_specs=[_rows(d, tr), _rows(d, tr)],
        out_specs=(_const2((8, 128)), _rows(d, tr), _rows(d, tr)), compiler_params=_params(("arbitrary",)),
    )(y, target)


def _row_block(rows, cols, budget=1 << 20):
    if rows * cols <= budget:
        return rows
    best = None
    for tr in range(16, rows, 16):
        if rows % tr == 0 and tr * cols <= budget:
            best = tr
    assert best is not None, (rows, cols)
    return best


def _place_shard(x4, layer, j_arr, out_dtype, name):
    _, nh, r, cols = x4.shape
    tr = _row_block(r, cols)

    def body(j_ref, x_ref, o_ref):
        o_ref[...] = x_ref[...].astype(out_dtype)

    grid_spec = pltpu.PrefetchScalarGridSpec(
        num_scalar_prefetch=1, grid=(nh, r // tr),
        in_specs=[pl.BlockSpec((None, None, tr, cols), lambda h, i, j_ref: (layer, h, i, 0))],
        out_specs=pl.BlockSpec((None, None, tr, cols), lambda h, i, j_ref: (j_ref[0], h, i, 0)))
    return _ordered_call(
        body, name=name, out_shape=jax.ShapeDtypeStruct((N_CHIPS, nh, r, cols), out_dtype), grid_spec=grid_spec,
        compiler_params=_params(("parallel", "parallel")),
    )(j_arr, x4)


def _adamw(w, g, m, v, name, budget=1 << 18):
    rows, cols = w.shape
    tr = _row_block(rows, cols, budget)

    def body(w_ref, g_ref, m_ref, v_ref, d_ref, nm_ref, nv_ref):
        gv = g_ref[...]
        mn = ADAM_B1 * m_ref[...] + (1.0 - ADAM_B1) * gv
        vn = ADAM_B2 * v_ref[...] + (1.0 - ADAM_B2) * (gv * gv)
        m_hat = mn / (1.0 - ADAM_B1 ** ADAM_STEP)
        v_hat = vn / (1.0 - ADAM_B2 ** ADAM_STEP)
        d_ref[...] = -ADAM_LR * (m_hat / (jnp.sqrt(v_hat) + ADAM_EPS) + ADAM_WD * w_ref[...])
        nm_ref[...] = mn
        nv_ref[...] = vn

    sds = jax.ShapeDtypeStruct((rows, cols), F32)
    return _ordered_call(
        body, name=name, out_shape=(sds, sds, sds), grid=(rows // tr,),
        in_specs=[_rows(cols, tr)] * 4, out_specs=(_rows(cols, tr),) * 3, compiler_params=_params(("parallel",)),
    )(w, g, m, v)


def _pair_sum(g5, recv, c_arr, name):
    _, _, rh, cols = g5.shape
    tr = _row_block(rh, cols)

    def body(c_ref, g_ref, r_ref, o_ref):
        o_ref[...] = (g_ref[...].astype(F32) + r_ref[...].astype(F32)).astype(BF16)

    grid_spec = pltpu.PrefetchScalarGridSpec(
        num_scalar_prefetch=1, grid=(N_CHIPS, rh // tr),
        in_specs=[pl.BlockSpec((None, None, tr, cols), lambda j, i, c_ref: (j, c_ref[0], i, 0)),
                  pl.BlockSpec((None, tr, cols), lambda j, i, c_ref: (j, i, 0))],
        out_specs=pl.BlockSpec((None, tr, cols), lambda j, i, c_ref: (j, i, 0)))
    return _ordered_call(
        body, name=name, out_shape=jax.ShapeDtypeStruct((N_CHIPS, rh, cols), BF16), grid_spec=grid_spec,
        compiler_params=_params(("parallel", "parallel")),
    )(c_arr, g5, recv)


def _chip_sum(p4, recv3, j_arr, c_arr, layer, into, name):
    _, rh, cols = p4.shape
    tr = _row_block(rh, cols, 1 << 19)

    def body(j_ref, c_ref, p_ref, r_ref, *rest):
        o_ref = rest[-1]
        total = p_ref[...].astype(F32)
        for peer in range(3):
            total = total + r_ref[peer].astype(F32)
        o_ref[...] = total

    in_specs = [pl.BlockSpec((None, tr, cols), lambda i, j_ref, c_ref: (j_ref[0], i, 0)),
                pl.BlockSpec((3, tr, cols), lambda i, j_ref, c_ref: (0, i, 0))]
    operands = [j_arr, c_arr, p4, recv3]
    aliases = {}
    if into is not None:
        in_specs.append(ANY)
        operands.append(into)
        aliases = {4: 0}
    grid_spec = pltpu.PrefetchScalarGridSpec(
        num_scalar_prefetch=2, grid=(rh // tr,), in_specs=in_specs,
        out_specs=pl.BlockSpec((None, None, tr, cols), lambda i, j_ref, c_ref: (layer, c_ref[0], i, 0)))
    return _ordered_call(
        body, name=name, out_shape=jax.ShapeDtypeStruct((DEPTH, 2, rh, cols), F32), grid_spec=grid_spec,
        input_output_aliases=aliases, compiler_params=_params(("parallel",)),
    )(*operands)


def _place():
    x, y, c = lax.axis_index("x"), lax.axis_index("y"), lax.axis_index("c")
    chips = [(1 - x, y), (x, 1 - y), (1 - x, 1 - y)]
    return x, y, c, chips


HBM = pl.BlockSpec(memory_space=pltpu.HBM)
SEM = pl.BlockSpec(memory_space=pltpu.SEMAPHORE)
TOKEN = jax.ShapeDtypeStruct((8, 128), F32)


def _remote(src, dst, send_sem, recv_sem, to):
    return pltpu.make_async_remote_copy(src_ref=src, dst_ref=dst, send_sem=send_sem, recv_sem=recv_sem, device_id=to,
                                        device_id_type=MESH)


def _split_call(body, name, thru, sems_in=(), fresh=(), new_sems=(), after_last=True):
    n_t, n_s, n_f = len(thru), len(sems_in), len(fresh)

    def call_body(*refs):
        outs = refs[n_t + n_s:]
        body(refs[:n_t], refs[n_t:n_t + n_s], outs[1 + n_t:1 + n_t + n_f], outs[1 + n_t + n_f:])
        outs[0][...] = jnp.zeros_like(outs[0])

    out_shape = ([TOKEN] + [pltpu.HBM(t.shape, t.dtype) for t in thru] + [pltpu.HBM(shp, dt) for shp, dt in fresh]
                 + [pltpu.SemaphoreType.DMA(shp) for shp in new_sems])
    out_specs = [pl.BlockSpec(memory_space=pltpu.VMEM)] + [HBM] * (n_t + n_f) + [SEM] * len(new_sems)
    if not after_last:
        _Order.last = None
    out = _ordered_call(
        call_body, name=name, out_shape=tuple(out_shape), in_specs=[HBM] * n_t + [SEM] * n_s, out_specs=tuple(out_specs),
        input_output_aliases={i: 1 + i for i in range(n_t)},
        compiler_params=pltpu.CompilerParams(has_side_effects=pltpu.SideEffectType.DATAFLOW_SIDE_EFFECTING),
    )(*[pltpu.with_memory_space_constraint(t, pltpu.HBM) for t in thru], *sems_in)
    return out[1:1 + n_t], out[1 + n_t:1 + n_t + n_f], out[1 + n_t + n_f:]


class _Exchange:
    def __init__(self, weights, j_arr, c_arr):
        self.j_arr, self.c_arr = j_arr, c_arr
        self.groups = [(l, name) for l in range(DEPTH) for name in BIG_NAMES]
        self.shard_shape = {name: weights[name].shape[1:] for name in BIG_NAMES}
        bufs = []
        for l, name in self.groups:
            nl, r, cols = weights[name].shape
            bufs.append(_place_shard(weights[name].reshape(nl, 2, r // 2, cols), l, j_arr, BF16, f"place_{name}_l{l}"))
        convs = [_place_shard(weights["conv_w"][:, None], l, j_arr, F32, f"place_conv_w_l{l}") for l in range(DEPTH)]
        n_g, n_c = len(bufs), len(convs)

        def start(thru, _, __, sems):
            x, y, c, chips = _place()
            j_me = 2 * x + y
            copies = []
            for i in range(n_c + n_g):
                mine = thru[i].at[j_me] if i < n_c else thru[i].at[j_me, c]
                copies += [_remote(mine, mine, sems[2 * i].at[k], sems[2 * i + 1].at[k], (*chip, c))
                           for k, chip in enumerate(chips)]
            for cp in copies:
                cp.start()

        thru, _, sems = _split_call(start, "gather_start", convs + bufs, new_sems=[(3,)] * (2 * (n_c + n_g)))
        self.conv_state = [(thru[l], sems[2 * l], sems[2 * l + 1]) for l in range(n_c)]
        self.state = {grp: (thru[n_c + g], sems[2 * (n_c + g)], sems[2 * (n_c + g) + 1]) for g, grp in enumerate(self.groups)}
        self.ready, self.conv_ready = {}, {}
        self.pending, self.tick, self.reduced = [], 0, {}

    def conv_w(self, l):
        if l not in self.conv_ready:
            buf, send, recv = self.conv_state[l]

            def wait(thru, sems, _, __):
                x, y, c, chips = _place()
                for k, chip in enumerate(chips):
                    mine, theirs = thru[0].at[2 * x + y], thru[0].at[2 * chip[0] + chip[1]]
                    _remote(mine, mine, sems[0].at[k], sems[1].at[k], (*chip, c)).wait_send()
                    _remote(theirs, theirs, sems[0].at[k], sems[1].at[k], (x, y, c)).wait_recv()

            (buf,), _, _ = _split_call(wait, f"gather_conv_w_l{l}", [buf], sems_in=[send, recv])
            self.conv_ready[l] = jnp.transpose(buf[:, 0], (1, 0, 2)).reshape(3, 2 * D_FF)
        return self.conv_ready[l]

    def weight(self, l, name):
        grp = (l, name)
        if grp not in self.ready:
            buf, send, recv = self.state[grp]

            def forward(thru, sems, _, new):
                x, y, c, chips = _place()
                for k, chip in enumerate(chips):
                    landed = thru[0].at[2 * chip[0] + chip[1], c]
                    _remote(landed, landed, new[0].at[k], sems[0].at[k], (x, y, c)).wait_recv()
                    _remote(landed, landed, new[0].at[k], new[1].at[k], (x, y, 1 - c)).start()

            (buf,), _, (fsend, frecv) = _split_call(forward, f"gather_pass_{name}_l{l}", [buf], sems_in=[recv],
                                                    new_sems=[(3,), (3,)])

            def finish(thru, sems, _, __):
                x, y, c, chips = _place()
                mine = thru[0].at[2 * x + y, c]
                for k, chip in enumerate(chips):
                    j_k = 2 * chip[0] + chip[1]
                    theirs, landed = thru[0].at[j_k, 1 - c], thru[0].at[j_k, c]
                    _remote(theirs, theirs, sems[1].at[k], sems[2].at[k], (x, y, c)).wait_recv()
                    _remote(landed, landed, sems[1].at[k], sems[2].at[k], (x, y, 1 - c)).wait_send()
                    _remote(mine, mine, sems[0].at[k], sems[2].at[k], (*chip, c)).wait_send()

            (buf,), _, _ = _split_call(finish, f"gather_done_{name}_l{l}", [buf], sems_in=[send, fsend, frecv])
            r, cols = self.shard_shape[name]
            self.ready[grp] = buf.reshape(N_CHIPS, r, cols) if name in ("w_in", "w_up") else buf.reshape(N_CHIPS * r, cols)
        return self.ready[grp]

    def grad(self, l, name, g):
        r, cols = self.shard_shape[name]
        g5 = g.reshape(N_CHIPS, 2, r // 2, cols)

        def start(thru, _, fresh, sems):
            x, y, c, _chips = _place()
            _remote(thru[0].at[:, 1 - c], fresh[0], sems[0], sems[1], (x, y, 1 - c)).start()

        (g5,), (recv,), sems = _split_call(start, f"pair_start_{name}_l{l}", [g5], fresh=[((N_CHIPS, r // 2, cols), BF16)],
                                          new_sems=[(), ()], after_last=False)
        self.pending.append(dict(l=l, name=name, stage=1, at=self.tick, bufs=(g5, recv), sems=sems))

    def _pair(self, grp):
        l, name = grp["l"], grp["name"]
        r, cols = self.shard_shape[name]

        def wait(thru, sems, _, __):
            x, y, c, _chips = _place()
            cp = _remote(thru[0].at[:, 1 - c], thru[1], sems[0], sems[1], (x, y, 1 - c))
            cp.wait_send()
            cp.wait_recv()

        (g5, recv), _, _ = _split_call(wait, f"pair_done_{name}_l{l}", list(grp["bufs"]), sems_in=list(grp["sems"]))
        p4 = _pair_sum(g5, recv, self.c_arr, f"pair_sum_{name}_l{l}")

        def start(thru, _, fresh, sems):
            x, y, c, chips = _place()
            for k, chip in enumerate(chips):
                _remote(thru[0].at[2 * chip[0] + chip[1]], fresh[0].at[k], sems[0].at[k], sems[1].at[k], (*chip, c)).start()

        (p4,), (recv3,), sems = _split_call(start, f"chips_start_{name}_l{l}", [p4], fresh=[((3, r // 2, cols), BF16)],
                                           new_sems=[(3,), (3,)], after_last=False)
        grp.update(stage=2, at=self.tick, bufs=(p4, recv3), sems=sems)

    def _chips(self, grp):
        l, name = grp["l"], grp["name"]

        def wait(thru, sems, _, __):
            x, y, c, chips = _place()
            for k, chip in enumerate(chips):
                cp = _remote(thru[0].at[2 * chip[0] + chip[1]], thru[1].at[k], sems[0].at[k], sems[1].at[k], (*chip, c))
                cp.wait_send()
                cp.wait_recv()

        (p4, recv3), _, _ = _split_call(wait, f"chips_done_{name}_l{l}", list(grp["bufs"]), sems_in=list(grp["sems"]))
        self.reduced[name] = _chip_sum(p4, recv3, self.j_arr, self.c_arr, l, self.reduced.get(name), f"chip_sum_{name}_l{l}")
        grp.update(stage=3)

    def point(self, drain=False):
        self.tick += 1
        for grp in self.pending:
            if grp["stage"] == 1 and (drain or grp["at"] < self.tick):
                self._pair(grp)
            elif grp["stage"] == 2 and (drain or grp["at"] + len(BIG_NAMES) <= self.tick):
                self._chips(grp)

    def finish(self):
        self.point(drain=True)
        self.point(drain=True)
        joined = _join_halves([self.reduced[name] for name in BIG_NAMES])
        return {name: buf.reshape((DEPTH,) + self.shard_shape[name]) for name, buf in zip(BIG_NAMES, joined)}


def _join_halves(bufs):
    n = len(bufs)

    def body(*refs):
        dsts = refs[n:2 * n]
        send_sems, recv_sems = refs[2 * n:]
        x, y, c, _ = _place()
        copies = [pltpu.make_async_remote_copy(src_ref=dsts[a].at[:, c], dst_ref=dsts[a].at[:, c], send_sem=send_sems.at[a],
                                               recv_sem=recv_sems.at[a], device_id=(x, y, 1 - c), device_id_type=MESH)
                  for a in range(n)]
        for cp in copies:
            cp.start()
        for a in range(n):
            copies[a].wait_send()
            theirs = dsts[a].at[:, 1 - c]
            pltpu.make_async_remote_copy(src_ref=theirs, dst_ref=theirs, send_sem=send_sems.at[a], recv_sem=recv_sems.at[a],
                                         device_id=(x, y, c), device_id_type=MESH).wait_recv()

    return _ordered_call(
        body, name="join_halves", out_shape=tuple(jax.ShapeDtypeStruct(b.shape, b.dtype) for b in bufs),
        in_specs=[ANY] * n, out_specs=tuple([ANY] * n), input_output_aliases={a: a for a in range(n)},
        scratch_shapes=[pltpu.SemaphoreType.DMA((n,))] * 2,
    )(*bufs)


def _all_reduce_small(v):
    rows, lanes = v.shape

    def body(x_ref, all_ref, sum_ref, send_sems, recv_sems, local_sem):
        x, y, c, chips = _place()
        me, sibling = (x, y, c), (x, y, 1 - c)

        def slot(px, py, pc):
            return all_ref.at[pl.ds(pl.multiple_of((4 * px + 2 * py + pc) * rows, 8), rows), :]

        def copy(k, block, to, src=None):
            return pltpu.make_async_remote_copy(src_ref=slot(*block) if src is None else src, dst_ref=slot(*block),
                                                send_sem=send_sems.at[k], recv_sem=recv_sems.at[k], device_id=to,
                                                device_id_type=MESH)

        mine = pltpu.make_async_copy(x_ref, slot(*me), local_sem)
        mine.start()
        first = [copy(0, me, sibling, src=x_ref)]
        first += [copy(1 + k, me, (*chip, c), src=x_ref) for k, chip in enumerate(chips)]
        for cp in first:
            cp.start()
        passed = [copy(4 + k, (*chip, c), sibling) for k, chip in enumerate(chips)]
        for k, chip in enumerate(chips):
            copy(1 + k, (*chip, c), me).wait_recv()
            passed[k].start()
        copy(0, sibling, me).wait_recv()
        for k, chip in enumerate(chips):
            copy(4 + k, (*chip, 1 - c), me).wait_recv()
        for cp in first + passed:
            cp.wait_send()
        mine.wait()
        total = all_ref[0:rows, :]
        for dev in range(1, 8):
            total = total + all_ref[dev * rows:(dev + 1) * rows, :]
        sum_ref[...] = total

    vm = pl.BlockSpec(memory_space=pltpu.VMEM)
    return _ordered_call(
        body, name="all_reduce_small",
        out_shape=(jax.ShapeDtypeStruct((8 * rows, lanes), v.dtype), jax.ShapeDtypeStruct((rows, lanes), v.dtype)),
        in_specs=[vm], out_specs=(vm, vm),
        scratch_shapes=[pltpu.SemaphoreType.DMA((7,)), pltpu.SemaphoreType.DMA((7,)), pltpu.SemaphoreType.DMA],
        compiler_params=pltpu.CompilerParams(vmem_limit_bytes=V7X_VMEM_LIMIT),
    )(v)[1]


def _rope_tables(s):
    inv_freq = ROPE_THETA ** (-jnp.arange(0, HEAD_DIM, 2, dtype=F32) / HEAD_DIM)
    ang = jnp.arange(s, dtype=F32)[:, None] * inv_freq[None, :]
    cos, sin = jnp.cos(ang), jnp.sin(ang)
    return jnp.concatenate([cos, cos], axis=-1), jnp.concatenate([-sin, sin], axis=-1)


def _local_step(x, target, ex, small):
    s = x.shape[0]
    cosf, sinf = _rope_tables(s)
    saved = []
    for l in range(DEPTH):
        p = small[l]
        t = f"l{l}"
        h = _rms_fwd(x, p["norm1_g"], f"norm1_{t}")
        z = _matmul(h, ex.weight(l, "w_in"), mode="nn", out_dtype=F32, tm=1024, tn=896, tk=2048, b_parts=4, name=f"proj_in_{t}")
        qn, kn, vb, ug, vn = _proj_post(z, p["q_norm_g"], p["k_norm_g"], p["sgu_ln_g"], p["sgu_ln_b"], cosf, sinf, f"proj_post_{t}")
        attn, sgu, mixed = _mixer_fwd(qn, kn, vb, ug, vn, p["w_s_bf16"], p["b_s_tile"], p["sink"], p["attn_out_g"],
                                      p["sgu_out_g"], f"mixer_{t}")
        x1 = _matmul(mixed, ex.weight(l, "w_o"), mode="nn", out_dtype=F32, tm=1024, tn=512, tk=2048, res=x, name=f"proj_out_{t}")
        h2 = _rms_fwd(x1, p["norm2_g"], f"norm2_{t}")
        a_pre = _matmul(h2, ex.weight(l, "w_up"), mode="nn", out_dtype=F32, tm=1024, tn=1408, tk=2048, b_parts=4, out_parts=2,
                        name=f"ffn_up_{t}")
        act = _conv_gate_fwd(a_pre, ex.conv_w(l), p["conv_b"], f"conv_gate_{t}")
        x2 = _matmul(act, ex.weight(l, "w_down"), mode="nn", out_dtype=F32, tm=1024, tn=512, tk=2816, res=x1, name=f"ffn_down_{t}")
        saved.append(dict(x=x, h=h, z=z, qn=qn, kn=kn, vb=vb, ug=ug, vn=vn, attn=attn, sgu=sgu, mixed=mixed, x1=x1, h2=h2,
                          a_pre=a_pre, act=act))
        x = x2
    loss_tile, dx, dxb = _loss_head(x, target, "loss_head")
    small_grads = [None] * DEPTH
    for l in reversed(range(DEPTH)):
        p, sv = small[l], saved[l]
        t = f"l{l}"
        ex.grad(l, "w_down", _matmul(sv["act"], dxb, mode="tn", out_dtype=BF16, tm=512, tn=1024, tk=2048, name=f"g_w_down_{t}"))
        dact = _matmul(dxb, ex.weight(l, "w_down"), mode="nt", out_dtype=F32, tm=1024, tn=512, tk=2048, name=f"d_act_{t}")
        dap, dcw, dcb = _conv_gate_bwd(sv["a_pre"], ex.conv_w(l), p["conv_b"], dact, f"conv_gate_bwd_{t}")
        ex.point()
        ex.grad(l, "w_up", _matmul(sv["h2"], dap, mode="tn", out_dtype=BF16, tm=1024, tn=1408, tk=2048, b_parts=2, out_parts=4,
                                   name=f"g_w_up_{t}"))
        dh2 = _matmul(dap, ex.weight(l, "w_up"), mode="nt", out_dtype=F32, tm=1024, tn=512, tk=2816, a_parts=2, b_parts=4,
                      name=f"d_h2_{t}")
        dx1, dx1b, dg2 = _rms_bwd(sv["x1"], p["norm2_g"], dh2, dx, f"norm2_bwd_{t}")
        ex.point()
        ex.grad(l, "w_o", _matmul(sv["mixed"], dx1b, mode="tn", out_dtype=BF16, tm=1024, tn=512, tk=2048, name=f"g_w_o_{t}"))
        dmixed = _matmul(dx1b, ex.weight(l, "w_o"), mode="nt", out_dtype=F32, tm=1024, tn=512, tk=2048, name=f"d_mixed_{t}")
        dqn, dkn, dvb, dug, dvn, dws, dbs, dsk, dga, dgs = _mixer_bwd(
            sv["qn"], sv["kn"], sv["vb"], sv["ug"], sv["vn"], sv["attn"], sv["sgu"], dmixed, p["w_s_bf16"], p["b_s_tile"],
            p["sink"], p["attn_out_g"], p["sgu_out_g"], f"mixer_bwd_{t}")
        dz, dqg, dkg, dlg, dlb = _proj_post_bwd(sv["z"], dqn, dkn, dvb, dug, dvn, p["q_norm_g"], p["k_norm_g"], p["sgu_ln_g"],
                                                 cosf, sinf, f"proj_post_bwd_{t}")
        ex.point()
        ex.grad(l, "w_in", _matmul(sv["h"], dz, mode="tn", out_dtype=BF16, tm=1024, tn=896, tk=2048, out_parts=4,
                                   name=f"g_w_in_{t}"))
        dh = _matmul(dz, ex.weight(l, "w_in"), mode="nt", out_dtype=F32, tm=1024, tn=512, tk=896, b_parts=4, name=f"d_h_{t}")
        dx, dxb, dg1 = _rms_bwd(sv["x"], p["norm1_g"], dh, dx1, f"norm1_bwd_{t}")
        ex.point()
        small_grads[l] = dict(
            norm1_g=dg1[0], q_norm_g=dqg[0], k_norm_g=dkg[0], sink=dsk[:, 0], sgu_ln_g=dlg[0], sgu_ln_b=dlb[0], w_s=dws,
            b_s=dbs[:, :, 0], attn_out_g=dga[0], sgu_out_g=dgs[0], norm2_g=dg2[0],
            conv_w=jnp.concatenate([dcw[0], dcw[1]], axis=-1), conv_b=jnp.concatenate([dcb[0, 0], dcb[1, 0]], axis=-1))
    return loss_tile, dx, small_grads


def _small_views(l, norm1_g, q_norm_g, k_norm_g, sink, sgu_ln_g, sgu_ln_b, w_s, b_s, attn_out_g, sgu_out_g, norm2_g, conv_b):
    return dict(
        norm1_g=norm1_g[l][None], q_norm_g=q_norm_g[l][None], k_norm_g=k_norm_g[l][None], sink=sink[l],
        sgu_ln_g=sgu_ln_g[l][None], sgu_ln_b=sgu_ln_b[l][None], w_s_bf16=w_s[l].astype(BF16),
        b_s_tile=jnp.broadcast_to(b_s[l][:, :, None], (N_GMLP_HEADS, BLOCK, BLOCK)), attn_out_g=attn_out_g[l][None],
        sgu_out_g=sgu_out_g[l][None], norm2_g=norm2_g[l][None], conv_b=conv_b[l][None])


SMALL_NAMES = ("norm1_g", "q_norm_g", "k_norm_g", "sink", "sgu_ln_g", "sgu_ln_b", "w_s", "b_s", "attn_out_g", "sgu_out_g",
               "norm2_g", "conv_w", "conv_b")
BIG_NAMES = ("w_in", "w_o", "w_up", "w_down")
PACK_LANES = 128
PACK_ALIGN = 8 * PACK_LANES


def _pack_rows(shape):
    return -(-math.prod(shape) // PACK_ALIGN) * 8


def _pack(arrays):
    parts = []
    for a in arrays:
        flat = a.reshape(-1)
        parts.append(jnp.pad(flat, (0, _pack_rows(a.shape) * PACK_LANES - flat.shape[0])).reshape(-1, PACK_LANES))
    return jnp.concatenate(parts, axis=0)


def _unpack(packed, shapes):
    out, at = [], 0
    for shp in shapes:
        rows = _pack_rows(shp)
        out.append(packed[at:at + rows].reshape(-1)[:math.prod(shp)].reshape(shp))
        at += rows
    return out


def kernel(x, norm1_g, w_in, q_norm_g, k_norm_g, sink, sgu_ln_g, sgu_ln_b, w_s, b_s, attn_out_g, sgu_out_g, w_o, norm2_g, w_up, conv_w, conv_b, w_down, loss_target, m_norm1_g, m_w_in, m_q_norm_g, m_k_norm_g, m_sink, m_sgu_ln_g, m_sgu_ln_b, m_w_s, m_b_s, m_attn_out_g, m_sgu_out_g, m_w_o, m_norm2_g, m_w_up, m_conv_w, m_conv_b, m_w_down, v_norm1_g, v_w_in, v_q_norm_g, v_k_norm_g, v_sink, v_sgu_ln_g, v_sgu_ln_b, v_w_s, v_b_s, v_attn_out_g, v_sgu_out_g, v_w_o, v_norm2_g, v_w_up, v_conv_w, v_conv_b, v_w_down):
    weights = dict(norm1_g=norm1_g, w_in=w_in, q_norm_g=q_norm_g, k_norm_g=k_norm_g, sink=sink, sgu_ln_g=sgu_ln_g,
                   sgu_ln_b=sgu_ln_b, w_s=w_s, b_s=b_s, attn_out_g=attn_out_g, sgu_out_g=sgu_out_g, w_o=w_o, norm2_g=norm2_g,
                   w_up=w_up, conv_w=conv_w, conv_b=conv_b, w_down=w_down)
    m_in = dict(norm1_g=m_norm1_g, w_in=m_w_in, q_norm_g=m_q_norm_g, k_norm_g=m_k_norm_g, sink=m_sink, sgu_ln_g=m_sgu_ln_g,
                sgu_ln_b=m_sgu_ln_b, w_s=m_w_s, b_s=m_b_s, attn_out_g=m_attn_out_g, sgu_out_g=m_sgu_out_g, w_o=m_w_o,
                norm2_g=m_norm2_g, w_up=m_w_up, conv_w=m_conv_w, conv_b=m_conv_b, w_down=m_w_down)
    v_in = dict(norm1_g=v_norm1_g, w_in=v_w_in, q_norm_g=v_q_norm_g, k_norm_g=v_k_norm_g, sink=v_sink, sgu_ln_g=v_sgu_ln_g,
                sgu_ln_b=v_sgu_ln_b, w_s=v_w_s, b_s=v_b_s, attn_out_g=v_attn_out_g, sgu_out_g=v_sgu_out_g, w_o=v_w_o,
                norm2_g=v_norm2_g, w_up=v_w_up, conv_w=v_conv_w, conv_b=v_conv_b, w_down=v_w_down)
    cx, cy, cc = lax.axis_index("x"), lax.axis_index("y"), lax.axis_index("c")
    j_me = 2 * cx + cy
    c_arr = jnp.reshape(cc, (1,)).astype(jnp.int32)
    j_arr = jnp.reshape(j_me, (1,)).astype(jnp.int32)

    _Order.last = None
    ex = _Exchange(weights, j_arr, c_arr)
    small = [_small_views(l, norm1_g, q_norm_g, k_norm_g, sink, sgu_ln_g, sgu_ln_b, w_s, b_s, attn_out_g, sgu_out_g, norm2_g,
                          conv_b) for l in range(DEPTH)]
    loss_tile, dx, small_grads = _local_step(x[0], loss_target[0], ex, small)
    joined = ex.finish()

    small_partials = [jnp.stack([small_grads[l][nm] for l in range(DEPTH)]) for nm in SMALL_NAMES]
    small_shapes = [a.shape for a in small_partials]
    reduced = _all_reduce_small(_pack(small_partials + [loss_tile[0, 0:1]]))
    small_full = dict(zip(SMALL_NAMES + ("loss",), _unpack(reduced, small_shapes + [(1,)])))
    loss = small_full.pop("loss")[0]
    cw_cols = 2 * D_FF // N_CHIPS
    small_full["conv_w"] = lax.dynamic_slice_in_dim(small_full["conv_w"], j_me * cw_cols, cw_cols, axis=2)
    pw, pg, pm, pv = (_pack([src[nm] for nm in SMALL_NAMES]) for src in (weights, small_full, m_in, v_in))
    sd, sm, sv = _adamw(pw, pg, pm, pv, "adamw_small", budget=1 << 20)
    shapes = [weights[nm].shape for nm in SMALL_NAMES]
    grads = dict(small_full)
    delta = dict(zip(SMALL_NAMES, _unpack(sd, shapes)))
    new_m = dict(zip(SMALL_NAMES, _unpack(sm, shapes)))
    new_v = dict(zip(SMALL_NAMES, _unpack(sv, shapes)))

    for name in BIG_NAMES:
        l_, rows, cols = weights[name].shape
        flat = lambda a: a.reshape(l_ * rows, cols)
        bd, bm, bv = _adamw(flat(weights[name]), flat(joined[name]), flat(m_in[name]), flat(v_in[name]), f"adamw_{name}")
        grads[name] = joined[name]
        delta[name], new_m[name], new_v[name] = (a.reshape(l_, rows, cols) for a in (bd, bm, bv))

    order = ("norm1_g", "w_in", "q_norm_g", "k_norm_g", "sink", "sgu_ln_g", "sgu_ln_b", "w_s", "b_s", "attn_out_g", "sgu_out_g",
             "w_o", "norm2_g", "w_up", "conv_w", "conv_b", "w_down")
    return (loss, dx[None], *[grads[nm] for nm in order], *[delta[nm] for nm in order], *[new_m[nm] for nm in order],
            *[new_v[nm] for nm in order])
```

```python
import functools
import math

import jax
import jax.numpy as jnp
from jax import lax
from jax.experimental import pallas as pl
from jax.experimental.pallas import tpu as pltpu

F32 = jnp.float32
BF16 = jnp.bfloat16

D_MODEL = 2048
HEAD_DIM = 128
ATTN_WIDTH = 1024
N_Q_HEADS = 8
N_KV_HEADS = 2
GQA_GROUP = 4
KV_WIDTH = 256
GMLP_WIDTH = 1024
N_GMLP_HEADS = 8
BLOCK = 128
IN_WIDTH = 3584
D_FF = 5632
DEPTH = 2
EPS = 1e-6
MASK_VALUE = -1e30
ROPE_THETA = 10000.0
N_CHIPS = 4

ADAM_LR = 0.001
ADAM_B1 = 0.9
ADAM_B2 = 0.999
ADAM_EPS = 1e-08
ADAM_WD = 0.01
ADAM_STEP = 10

V7X_VMEM_LIMIT = 48 * 1024 * 1024
MESH = pl.DeviceIdType.MESH

_GELU_C = math.sqrt(2.0 / math.pi)
_GELU_A = 0.044715


def _params(sem=None):
    return pltpu.CompilerParams(dimension_semantics=sem, vmem_limit_bytes=V7X_VMEM_LIMIT)


ANY = pl.BlockSpec(memory_space=pl.ANY)


class _Order:
    last = None


def _ordered_call(body, *, token_index=0, **kw):
    def run(*operands):
        tok = _Order.last
        if tok is None or any(op is tok for op in operands):
            call = pl.pallas_call(body, **kw)
        else:
            n_in = len(operands)

            def ordered_body(*refs):
                return body(*refs[:n_in], *refs[n_in + 1:])

            kw2 = dict(kw)
            if "grid_spec" in kw2:
                gs = kw2["grid_spec"]
                kw2["grid_spec"] = pltpu.PrefetchScalarGridSpec(
                    num_scalar_prefetch=gs.num_scalar_prefetch, grid=gs.grid, in_specs=list(gs.in_specs) + [ANY],
                    out_specs=gs.out_specs, scratch_shapes=gs.scratch_shapes)
            else:
                kw2["in_specs"] = list(kw2["in_specs"]) + [ANY]
            call = pl.pallas_call(ordered_body, **kw2)
            operands = operands + (tok,)
        out = call(*operands)
        _Order.last = out[token_index] if isinstance(out, (tuple, list)) else out
        return out

    return run


def _gelu(x):
    return x * (0.5 * (1.0 + jnp.tanh(_GELU_C * (x + _GELU_A * (x * x * x)))))


def _gelu_grad(x):
    x2 = x * x
    t = jnp.tanh(_GELU_C * (x + _GELU_A * (x * x2)))
    return 0.5 * (1.0 + t) + 0.5 * x * (1.0 - t * t) * (_GELU_C * (1.0 + 3.0 * _GELU_A * x2))


def _mean_last(x):
    return jnp.mean(x, axis=-1, keepdims=True)


def _sum_rows(x):
    return jnp.sum(x, axis=0, keepdims=True)


def _sum_all(x):
    return jnp.sum(jnp.sum(x, axis=1, keepdims=True), axis=0, keepdims=True)


def _matmul(a, b, *, mode, out_dtype, tm, tn, tk, name, res=None, a_parts=0, b_parts=0, out_parts=0, b_lead=()):
    b_full = b
    b = jax.ShapeDtypeStruct(b.shape[len(b_lead):], b.dtype)
    if mode == "nn":
        assert not a_parts
        m, k = a.shape
        n = b.shape[0] * b.shape[2] if b_parts else b.shape[1]
    elif mode == "nt":
        m, k = (a.shape[1], a.shape[0] * a.shape[2]) if a_parts else a.shape
        n = b.shape[1] if b_parts else b.shape[0]
    else:
        assert not a_parts
        k, m = a.shape
        n = b.shape[0] * b.shape[2] if b_parts else b.shape[1]
    tm, tn, tk = min(tm, m), min(tn, n), min(tk, k)
    assert m % tm == 0 and n % tn == 0 and k % tk == 0, (name, m, n, k, tm, tn, tk)
    nm, nn, nk = m // tm, n // tn, k // tk

    def slab(idx, total_tiles, parts):
        per = total_tiles // parts
        assert per * parts == total_tiles, (name, total_tiles, parts)
        return idx // per, idx % per

    if mode == "nn":
        a_spec = pl.BlockSpec((tm, tk), lambda i, j, kk: (i, kk))
        if b_parts:
            b_spec = pl.BlockSpec((None, tk, tn), lambda i, j, kk: (slab(j, nn, b_parts)[0], kk, slab(j, nn, b_parts)[1]))
        else:
            b_spec = pl.BlockSpec((tk, tn), lambda i, j, kk: (kk, j))
        dims = (((1,), (0,)), ((), ()))
    elif mode == "nt":
        if a_parts:
            a_spec = pl.BlockSpec((None, tm, tk), lambda i, j, kk: (slab(kk, nk, a_parts)[0], i, slab(kk, nk, a_parts)[1]))
        else:
            a_spec = pl.BlockSpec((tm, tk), lambda i, j, kk: (i, kk))
        if b_parts:
            b_spec = pl.BlockSpec((None, tn, tk), lambda i, j, kk: (slab(kk, nk, b_parts)[0], j, slab(kk, nk, b_parts)[1]))
        else:
            b_spec = pl.BlockSpec((tn, tk), lambda i, j, kk: (j, kk))
        dims = (((1,), (1,)), ((), ()))
    else:
        a_spec = pl.BlockSpec((tk, tm), lambda i, j, kk: (kk, i))
        if b_parts:
            b_spec = pl.BlockSpec((None, tk, tn), lambda i, j, kk: (slab(j, nn, b_parts)[0], kk, slab(j, nn, b_parts)[1]))
        else:
            b_spec = pl.BlockSpec((tk, tn), lambda i, j, kk: (kk, j))
        dims = (((0,), (0,)), ((), ()))
    if out_parts:
        out_shape = jax.ShapeDtypeStruct((out_parts, m, n // out_parts), out_dtype)
        out_spec = pl.BlockSpec((None, tm, tn), lambda i, j, kk: (slab(j, nn, out_parts)[0], i, slab(j, nn, out_parts)[1]))
    else:
        out_shape = jax.ShapeDtypeStruct((m, n), out_dtype)
        out_spec = pl.BlockSpec((tm, tn), lambda i, j, kk: (i, j))
    if b_lead:
        inner_map = b_spec.index_map
        b_spec = pl.BlockSpec((None,) * len(b_lead) + tuple(b_spec.block_shape),
                              lambda i, j, kk: tuple(b_lead) + tuple(inner_map(i, j, kk)))
    in_specs = [a_spec, b_spec]
    operands = [a, b_full]
    if res is not None:
        in_specs.append(pl.BlockSpec((tm, tn), lambda i, j, kk: (i, j)))
        operands.append(res)

    def body(*refs):
        a_ref, b_ref = refs[0], refs[1]
        res_ref = refs[2] if res is not None else None
        o_ref = refs[3] if res is not None else refs[2]
        p = lax.dot_general(a_ref[...], b_ref[...], dims, preferred_element_type=F32)

        def finish(total):
            if res_ref is not None:
                total = res_ref[...] + total
            o_ref[...] = total.astype(out_dtype)

        if nk == 1:
            finish(p)
        else:
            acc_ref = refs[-1]
            kk = pl.program_id(2)

            @pl.when(kk == 0)
            def _():
                acc_ref[...] = p

            @pl.when(jnp.logical_and(kk > 0, kk < nk - 1))
            def _():
                acc_ref[...] += p

            @pl.when(kk == nk - 1)
            def _():
                finish(acc_ref[...] + p)

    scratch = [pltpu.VMEM((tm, tn), F32)] if nk > 1 else []
    return _ordered_call(
        body, name=name, out_shape=out_shape, grid=(nm, nn, nk), in_specs=in_specs, out_specs=out_spec,
        scratch_shapes=scratch, compiler_params=_params(("parallel", "parallel", "arbitrary")),
    )(*operands)


def _matmul_nt_slabs(a, b, *, tm, tn, name, a_parts=0):
    nslab, n, ks = b.shape
    m = a.shape[1] if a_parts else a.shape[0]
    tm, tn = min(tm, m), min(tn, n)
    assert m % tm == 0 and n % tn == 0, (name, m, n, tm, tn)
    if a_parts:
        per = nslab // a_parts
        assert per * a_parts == nslab and a.shape[2] == per * ks, (name, a.shape, b.shape)
        a_spec = pl.BlockSpec((a_parts, tm, per * ks), lambda i, j: (0, i, 0))
    else:
        assert a.shape[1] == nslab * ks, (name, a.shape, b.shape)
        a_spec = pl.BlockSpec((tm, nslab * ks), lambda i, j: (i, 0))

    def body(a_ref, b_ref, o_ref):
        total = None
        for sl in range(nslab):
            if a_parts:
                a_sl = a_ref[sl // per, :, (sl % per) * ks:(sl % per + 1) * ks]
            else:
                a_sl = a_ref[:, sl * ks:(sl + 1) * ks]
            p = lax.dot_general(a_sl, b_ref[sl], (((1,), (1,)), ((), ())), preferred_element_type=F32)
            total = p if total is None else total + p
        o_ref[...] = total

    return _ordered_call(
        body, name=name, out_shape=jax.ShapeDtypeStruct((m, n), F32), grid=(m // tm, n // tn),
        in_specs=[a_spec, pl.BlockSpec((nslab, tn, ks), lambda i, j: (0, j, 0))],
        out_specs=pl.BlockSpec((tm, tn), lambda i, j: (i, j)), compiler_params=_params(("parallel", "parallel")),
    )(a, b)


def _row_tile(s):
    return min(256, s)


def _rows(width, tr):
    return pl.BlockSpec((tr, width), lambda i: (i, 0))


def _const2(shape):
    return pl.BlockSpec(shape, lambda i: (0, 0))


def _rms_fwd(x, g, name):
    s, d = x.shape
    tr = _row_tile(s)

    def body(x_ref, g_ref, o_ref):
        xv = x_ref[...]
        r = lax.rsqrt(_mean_last(xv * xv) + EPS)
        o_ref[...] = (xv * r * g_ref[...]).astype(BF16)

    return _ordered_call(
        body, name=name, out_shape=jax.ShapeDtypeStruct((s, d), BF16), grid=(s // tr,),
        in_specs=[_rows(d, tr), _const2((1, d))], out_specs=_rows(d, tr), compiler_params=_params(("parallel",)),
    )(x, g)


def _rms_bwd(x, g, dh, dres, name):
    s, d = x.shape
    tr = _row_tile(s)

    def body(x_ref, g_ref, dh_ref, dres_ref, dx_ref, dxb_ref, dg_ref):
        xv, dy = x_ref[...], dh_ref[...]
        r = lax.rsqrt(_mean_last(xv * xv) + EPS)
        gdy = dy * g_ref[...]
        dx = dres_ref[...] + r * gdy - xv * ((r * r * r) * _mean_last(xv * gdy))
        dx_ref[...] = dx
        dxb_ref[...] = dx.astype(BF16)

        @pl.when(pl.program_id(0) == 0)
        def _():
            dg_ref[...] = jnp.zeros_like(dg_ref)

        dg_ref[...] += _sum_rows(xv * r * dy)

    return _ordered_call(
        body, name=name,
        out_shape=(jax.ShapeDtypeStruct((s, d), F32), jax.ShapeDtypeStruct((s, d), BF16), jax.ShapeDtypeStruct((1, d), F32)),
        grid=(s // tr,), in_specs=[_rows(d, tr), _const2((1, d)), _rows(d, tr), _rows(d, tr)],
        out_specs=(_rows(d, tr), _rows(d, tr), _const2((1, d))), compiler_params=_params(("arbitrary",)),
    )(x, g, dh, dres)


Q0, K0, V0, GU0, GV0 = 0, ATTN_WIDTH, ATTN_WIDTH + KV_WIDTH, ATTN_WIDTH + 2 * KV_WIDTH, ATTN_WIDTH + 2 * KV_WIDTH + GMLP_WIDTH


def _head(h, base=0):
    return slice(base + h * HEAD_DIM, base + (h + 1) * HEAD_DIM)


def _proj_post(z, qg, kg, lg, lb, cosf, sinf, name):
    s = z.shape[0]
    tr = _row_tile(s)

    def body(z_ref, qg_ref, kg_ref, lg_ref, lb_ref, cos_ref, sin_ref, qn_ref, kn_ref, vb_ref, ug_ref, vn_ref):
        cos, sin = cos_ref[...], sin_ref[...]

        def norm_rope(xh, g):
            y = xh * lax.rsqrt(_mean_last(xh * xh) + EPS) * g
            return y * cos + pltpu.roll(y, HEAD_DIM // 2, 1) * sin

        for h in range(N_Q_HEADS):
            qn_ref[:, _head(h)] = norm_rope(z_ref[:, _head(h, Q0)], qg_ref[...]).astype(BF16)
        for h in range(N_KV_HEADS):
            kn_ref[:, _head(h)] = norm_rope(z_ref[:, _head(h, K0)], kg_ref[...]).astype(BF16)
        vb_ref[...] = z_ref[:, V0:GU0].astype(BF16)
        ug_ref[...] = _gelu(z_ref[:, GU0:GV0])
        vg = _gelu(z_ref[:, GV0:IN_WIDTH])
        xc = vg - _mean_last(vg)
        y = xc * lax.rsqrt(_mean_last(xc * xc) + EPS)
        vn_ref[...] = (y * lg_ref[...] + lb_ref[...]).astype(BF16)

    return _ordered_call(
        body, name=name,
        out_shape=(jax.ShapeDtypeStruct((s, ATTN_WIDTH), BF16), jax.ShapeDtypeStruct((s, KV_WIDTH), BF16),
                   jax.ShapeDtypeStruct((s, KV_WIDTH), BF16), jax.ShapeDtypeStruct((s, GMLP_WIDTH), F32),
                   jax.ShapeDtypeStruct((s, GMLP_WIDTH), BF16)),
        grid=(s // tr,),
        in_specs=[_rows(IN_WIDTH, tr), _const2((1, HEAD_DIM)), _const2((1, HEAD_DIM)), _const2((1, GMLP_WIDTH)),
                  _const2((1, GMLP_WIDTH)), _rows(HEAD_DIM, tr), _rows(HEAD_DIM, tr)],
        out_specs=(_rows(ATTN_WIDTH, tr), _rows(KV_WIDTH, tr), _rows(KV_WIDTH, tr), _rows(GMLP_WIDTH, tr), _rows(GMLP_WIDTH, tr)),
        compiler_params=_params(("parallel",)),
    )(z, qg, kg, lg, lb, cosf, sinf)


def _proj_post_bwd(z, dqn, dkn, dvb, dug, dvn, qg, kg, lg, cosf, sinf, name):
    s = z.shape[0]
    tr = _row_tile(s)

    def body(z_ref, dqn_ref, dkn_ref, dvb_ref, dug_ref, dvn_ref, qg_ref, kg_ref, lg_ref, cos_ref, sin_ref,
             dz_ref, dqg_ref, dkg_ref, dlg_ref, dlb_ref):
        cos, sin = cos_ref[...], sin_ref[...]

        @pl.when(pl.program_id(0) == 0)
        def _():
            dqg_ref[...] = jnp.zeros_like(dqg_ref)
            dkg_ref[...] = jnp.zeros_like(dkg_ref)
            dlg_ref[...] = jnp.zeros_like(dlg_ref)
            dlb_ref[...] = jnp.zeros_like(dlb_ref)

        def norm_rope_bwd(xh, g, dout):
            dy = dout * cos - pltpu.roll(dout, HEAD_DIM // 2, 1) * sin
            r = lax.rsqrt(_mean_last(xh * xh) + EPS)
            xhat = xh * r
            gdy = dy * g
            return r * (gdy - xhat * _mean_last(xhat * gdy)), _sum_rows(xhat * dy)

        dqg = jnp.zeros((1, HEAD_DIM), F32)
        for h in range(N_Q_HEADS):
            dx, dg = norm_rope_bwd(z_ref[:, _head(h, Q0)], qg_ref[...], dqn_ref[:, _head(h)])
            dz_ref[:, _head(h, Q0)] = dx.astype(BF16)
            dqg = dqg + dg
        dqg_ref[...] += dqg
        dkg = jnp.zeros((1, HEAD_DIM), F32)
        for h in range(N_KV_HEADS):
            dx, dg = norm_rope_bwd(z_ref[:, _head(h, K0)], kg_ref[...], dkn_ref[:, _head(h)])
            dz_ref[:, _head(h, K0)] = dx.astype(BF16)
            dkg = dkg + dg
        dkg_ref[...] += dkg
        dz_ref[:, V0:GU0] = dvb_ref[...].astype(BF16)
        dz_ref[:, GU0:GV0] = (dug_ref[...] * _gelu_grad(z_ref[:, GU0:GV0])).astype(BF16)
        gv = z_ref[:, GV0:IN_WIDTH]
        vg = _gelu(gv)
        xc = vg - _mean_last(vg)
        r = lax.rsqrt(_mean_last(xc * xc) + EPS)
        xhat = xc * r
        dvn_v = dvn_ref[...]
        dlg_ref[...] += _sum_rows(xhat * dvn_v)
        dlb_ref[...] += _sum_rows(dvn_v)
        dxh = dvn_v * lg_ref[...]
        dvg = r * (dxh - _mean_last(dxh) - xhat * _mean_last(dxh * xhat))
        dz_ref[:, GV0:IN_WIDTH] = (dvg * _gelu_grad(gv)).astype(BF16)

    return _ordered_call(
        body, name=name,
        out_shape=(jax.ShapeDtypeStruct((s, IN_WIDTH), BF16), jax.ShapeDtypeStruct((1, HEAD_DIM), F32),
                   jax.ShapeDtypeStruct((1, HEAD_DIM), F32), jax.ShapeDtypeStruct((1, GMLP_WIDTH), F32),
                   jax.ShapeDtypeStruct((1, GMLP_WIDTH), F32)),
        grid=(s // tr,),
        in_specs=[_rows(IN_WIDTH, tr), _rows(ATTN_WIDTH, tr), _rows(KV_WIDTH, tr), _rows(KV_WIDTH, tr), _rows(GMLP_WIDTH, tr),
                  _rows(GMLP_WIDTH, tr), _const2((1, HEAD_DIM)), _const2((1, HEAD_DIM)), _const2((1, GMLP_WIDTH)),
                  _rows(HEAD_DIM, tr), _rows(HEAD_DIM, tr)],
        out_specs=(_rows(IN_WIDTH, tr), _const2((1, HEAD_DIM)), _const2((1, HEAD_DIM)), _const2((1, GMLP_WIDTH)),
                   _const2((1, GMLP_WIDTH))),
        compiler_params=_params(("arbitrary",)),
    )(z, dqn, dkn, dvb, dug, dvn, qg, kg, lg, cosf, sinf)


def _band_valid(n, s):
    i = lax.broadcasted_iota(jnp.int32, (BLOCK, 3 * BLOCK), 0)
    j = lax.broadcasted_iota(jnp.int32, (BLOCK, 3 * BLOCK), 1)
    k_pos = n * BLOCK - BLOCK + j
    return (jnp.abs(j - BLOCK - i) <= BLOCK) & (k_pos >= 0) & (k_pos < s)


def _probs(q, kb, sink_h, valid):
    sc = lax.dot_general(q, kb, (((1,), (1,)), ((), ())), preferred_element_type=F32) * (HEAD_DIM ** -0.5)
    sc = jnp.where(valid, sc, MASK_VALUE)
    m = jnp.maximum(jnp.max(sc, axis=-1, keepdims=True), sink_h)
    p = jnp.exp(sc - m)
    es = jnp.exp(sink_h - m)
    den = jnp.sum(p, axis=-1, keepdims=True) + es
    return p / den, es / den


def _band_specs(width, nb):
    return [pl.BlockSpec((BLOCK, width), lambda n: (jnp.maximum(n - 1, 0), 0)),
            pl.BlockSpec((BLOCK, width), lambda n: (n, 0)),
            pl.BlockSpec((BLOCK, width), lambda n: (jnp.minimum(n + 1, nb - 1), 0))]


def _blk(width):
    return pl.BlockSpec((BLOCK, width), lambda n: (n, 0))


def _whole3(shape):
    return pl.BlockSpec(shape, lambda n: (0, 0, 0))


def _smem():
    return pl.BlockSpec(memory_space=pltpu.SMEM)


def _mixer_fwd(qn, kn, vb, ug, vn, wsb, bsb, sink, ga, gs, name):
    s = qn.shape[0]
    nb = s // BLOCK

    def body(sink_ref, q_ref, kp_ref, kc_ref, kx_ref, vp_ref, vc_ref, vx_ref, ug_ref, vn_ref, ws_ref, bs_ref, ga_ref, gs_ref,
             attn_ref, sgu_ref, mix_ref):
        n = pl.program_id(0)
        valid = _band_valid(n, s)
        ssq = jnp.zeros((BLOCK, 1), F32)
        for kh in range(N_KV_HEADS):
            kb = jnp.concatenate([kp_ref[:, _head(kh)], kc_ref[:, _head(kh)], kx_ref[:, _head(kh)]], axis=0)
            vbd = jnp.concatenate([vp_ref[:, _head(kh)], vc_ref[:, _head(kh)], vx_ref[:, _head(kh)]], axis=0)
            for g in range(GQA_GROUP):
                h = kh * GQA_GROUP + g
                p, _ = _probs(q_ref[:, _head(h)], kb, sink_ref[h], valid)
                o = jnp.dot(p.astype(BF16), vbd, preferred_element_type=F32)
                attn_ref[:, _head(h)] = o
                ssq = ssq + jnp.sum(o * o, axis=-1, keepdims=True)
        r = lax.rsqrt(ssq * (1.0 / ATTN_WIDTH) + EPS)
        mix_ref[:, 0:ATTN_WIDTH] = (attn_ref[...] * r * ga_ref[...]).astype(BF16)
        ssq = jnp.zeros((BLOCK, 1), F32)
        for h in range(N_GMLP_HEADS):
            f = jnp.dot(ws_ref[h], vn_ref[:, _head(h)], preferred_element_type=F32) + bs_ref[h]
            o = ug_ref[:, _head(h)] * f
            sgu_ref[:, _head(h)] = o
            ssq = ssq + jnp.sum(o * o, axis=-1, keepdims=True)
        r = lax.rsqrt(ssq * (1.0 / GMLP_WIDTH) + EPS)
        mix_ref[:, ATTN_WIDTH:D_MODEL] = (sgu_ref[...] * r * gs_ref[...]).astype(BF16)

    hh = (N_GMLP_HEADS, BLOCK, BLOCK)
    return _ordered_call(
        body, name=name,
        out_shape=(jax.ShapeDtypeStruct((s, ATTN_WIDTH), F32), jax.ShapeDtypeStruct((s, GMLP_WIDTH), F32),
                   jax.ShapeDtypeStruct((s, D_MODEL), BF16)),
        grid=(nb,),
        in_specs=[_smem(), _blk(ATTN_WIDTH)] + _band_specs(KV_WIDTH, nb) + _band_specs(KV_WIDTH, nb)
        + [_blk(GMLP_WIDTH), _blk(GMLP_WIDTH), _whole3(hh), _whole3(hh),
           pl.BlockSpec((1, ATTN_WIDTH), lambda n: (0, 0)), pl.BlockSpec((1, GMLP_WIDTH), lambda n: (0, 0))],
        out_specs=(_blk(ATTN_WIDTH), _blk(GMLP_WIDTH), _blk(D_MODEL)),
        compiler_params=_params(("parallel",)),
    )(sink, qn, kn, kn, kn, vb, vb, vb, ug, vn, wsb, bsb, ga, gs)


def _mixer_bwd(qn, kn, vb, ug, vn, attn, sgu, dmixed, wsb, bsb, sink, ga, gs, name):
    s = qn.shape[0]
    nb = s // BLOCK
    tn_dims = (((0,), (0,)), ((), ()))
    nt_dims = (((1,), (1,)), ((), ()))

    def body(sink_ref, q_ref, kp_ref, kc_ref, kx_ref, vp_ref, vc_ref, vx_ref, ug_ref, vn_ref, attn_ref, sgu_ref, dm_ref,
             ws_ref, bs_ref, ga_ref, gs_ref,
             dq_ref, dk_ref, dv_ref, dug_ref, dvn_ref, dws_ref, dbs_ref, dsk_ref, dga_ref, dgs_ref, dk_acc, dv_acc):
        n = pl.program_id(0)

        @pl.when(n == 0)
        def _():
            for ref in (dk_acc, dv_acc, dws_ref, dbs_ref, dsk_ref, dga_ref, dgs_ref):
                ref[...] = jnp.zeros_like(ref)

        def out_norm_bwd(o, g, dy):
            r = lax.rsqrt(_mean_last(o * o) + EPS)
            gdy = dy * g
            return r * gdy - o * ((r * r * r) * _mean_last(o * gdy)), _sum_rows(o * r * dy)

        d_attn, dga = out_norm_bwd(attn_ref[...], ga_ref[...], dm_ref[:, 0:ATTN_WIDTH])
        dga_ref[...] += dga
        d_sgu, dgs = out_norm_bwd(sgu_ref[...], gs_ref[...], dm_ref[:, ATTN_WIDTH:D_MODEL])
        dgs_ref[...] += dgs

        for h in range(N_GMLP_HEADS):
            vn_h = vn_ref[:, _head(h)]
            f = jnp.dot(ws_ref[h], vn_h, preferred_element_type=F32) + bs_ref[h]
            ds_h = d_sgu[:, _head(h)]
            dug_ref[:, _head(h)] = ds_h * f
            df = ds_h * ug_ref[:, _head(h)]
            dfb = df.astype(BF16)
            dvn_ref[:, _head(h)] = lax.dot_general(ws_ref[h], dfb, tn_dims, preferred_element_type=F32)
            dws_ref[h] += lax.dot_general(dfb, vn_h, nt_dims, preferred_element_type=F32)
            dbs_ref[h] += jnp.broadcast_to(jnp.sum(df, axis=-1, keepdims=True), (BLOCK, BLOCK))

        valid = _band_valid(n, s)
        row0 = pl.multiple_of(n * BLOCK, BLOCK)
        for kh in range(N_KV_HEADS):
            kb = jnp.concatenate([kp_ref[:, _head(kh)], kc_ref[:, _head(kh)], kx_ref[:, _head(kh)]], axis=0)
            vbd = jnp.concatenate([vp_ref[:, _head(kh)], vc_ref[:, _head(kh)], vx_ref[:, _head(kh)]], axis=0)
            dkb = jnp.zeros((3 * BLOCK, HEAD_DIM), F32)
            dvb = jnp.zeros((3 * BLOCK, HEAD_DIM), F32)
            for g in range(GQA_GROUP):
                h = kh * GQA_GROUP + g
                q = q_ref[:, _head(h)]
                p, p_sink = _probs(q, kb, sink_ref[h], valid)
                do = d_attn[:, _head(h)].astype(BF16)
                dp = lax.dot_general(do, vbd, nt_dims, preferred_element_type=F32)
                delta = jnp.sum(p * dp, axis=-1, keepdims=True)
                dsc = (p * (dp - delta) * (HEAD_DIM ** -0.5)).astype(BF16)
                dsk_ref[h:h + 1, :] += jnp.broadcast_to(_sum_all(-(p_sink * delta)), (1, BLOCK))
                dq_ref[:, _head(h)] = jnp.dot(dsc, kb, preferred_element_type=F32)
                dkb = dkb + lax.dot_general(dsc, q, tn_dims, preferred_element_type=F32)
                dvb = dvb + lax.dot_general(p.astype(BF16), do, tn_dims, preferred_element_type=F32)
            dk_acc[pl.ds(row0, 3 * BLOCK), _head(kh)] += dkb
            dv_acc[pl.ds(row0, 3 * BLOCK), _head(kh)] += dvb

        @pl.when(n == nb - 1)
        def _():
            dk_ref[...] = dk_acc[BLOCK:BLOCK + s, :]
            dv_ref[...] = dv_acc[BLOCK:BLOCK + s, :]

    hh = (N_GMLP_HEADS, BLOCK, BLOCK)
    full_kv = pl.BlockSpec((s, KV_WIDTH), lambda n: (0, 0))
    return _ordered_call(
        body, name=name,
        out_shape=(jax.ShapeDtypeStruct((s, ATTN_WIDTH), F32), jax.ShapeDtypeStruct((s, KV_WIDTH), F32),
                   jax.ShapeDtypeStruct((s, KV_WIDTH), F32), jax.ShapeDtypeStruct((s, GMLP_WIDTH), F32),
                   jax.ShapeDtypeStruct((s, GMLP_WIDTH), F32), jax.ShapeDtypeStruct(hh, F32), jax.ShapeDtypeStruct(hh, F32),
                   jax.ShapeDtypeStruct((N_Q_HEADS, BLOCK), F32), jax.ShapeDtypeStruct((1, ATTN_WIDTH), F32),
                   jax.ShapeDtypeStruct((1, GMLP_WIDTH), F32)),
        grid=(nb,),
        in_specs=[_smem(), _blk(ATTN_WIDTH)] + _band_specs(KV_WIDTH, nb) + _band_specs(KV_WIDTH, nb)
        + [_blk(GMLP_WIDTH), _blk(GMLP_WIDTH), _blk(ATTN_WIDTH), _blk(GMLP_WIDTH), _blk(D_MODEL), _whole3(hh), _whole3(hh),
           pl.BlockSpec((1, ATTN_WIDTH), lambda n: (0, 0)), pl.BlockSpec((1, GMLP_WIDTH), lambda n: (0, 0))],
        out_specs=(_blk(ATTN_WIDTH), full_kv, full_kv, _blk(GMLP_WIDTH), _blk(GMLP_WIDTH), _whole3(hh), _whole3(hh),
                   pl.BlockSpec((N_Q_HEADS, BLOCK), lambda n: (0, 0)), pl.BlockSpec((1, ATTN_WIDTH), lambda n: (0, 0)),
                   pl.BlockSpec((1, GMLP_WIDTH), lambda n: (0, 0))),
        scratch_shapes=[pltpu.VMEM((s + 2 * BLOCK, KV_WIDTH), F32), pltpu.VMEM((s + 2 * BLOCK, KV_WIDTH), F32)],
        compiler_params=_params(("arbitrary",)),
    )(sink, qn, kn, kn, kn, vb, vb, vb, ug, vn, attn, sgu, dmixed, wsb, bsb, ga, gs)


CONV_TILE = 128


def _shift_rows(a, rows):
    s = a.shape[0]
    prev = jnp.where(rows == 0, 0.0, pltpu.roll(a, 1, 0))
    nxt = jnp.where(rows == s - 1, 0.0, pltpu.roll(a, s - 1, 0))
    return prev, nxt


def _conv_specs(s):
    tc = CONV_TILE
    nj = D_FF // tc
    return (tc, nj, pl.BlockSpec((2, s, tc), lambda j: (0, 0, j)),
            [pl.BlockSpec((3, tc), lambda j: (0, j)), pl.BlockSpec((3, tc), lambda j: (0, j + nj))],
            [pl.BlockSpec((1, tc), lambda j: (0, j)), pl.BlockSpec((1, tc), lambda j: (0, j + nj))])


def _conv_gate_fwd(a_pre, cw, cb, name):
    s = a_pre.shape[1]
    tc, nj, a_spec, w_specs, b_specs = _conv_specs(s)

    def body(a_ref, wg_ref, wu_ref, bg_ref, bu_ref, act_ref):
        rows = lax.broadcasted_iota(jnp.int32, (s, tc), 0)

        def conv(a, w_ref, b_ref):
            prev, nxt = _shift_rows(a, rows)
            return b_ref[...] + prev * w_ref[0:1, :] + a * w_ref[1:2, :] + nxt * w_ref[2:3, :]

        g = conv(a_ref[0], wg_ref, bg_ref)
        u = conv(a_ref[1], wu_ref, bu_ref)
        act_ref[...] = (g * (1.0 / (1.0 + jnp.exp(-g))) * u).astype(BF16)

    return _ordered_call(
        body, name=name, out_shape=jax.ShapeDtypeStruct((s, D_FF), BF16), grid=(nj,),
        in_specs=[a_spec] + w_specs + b_specs, out_specs=pl.BlockSpec((s, tc), lambda j: (0, j)),
        compiler_params=_params(("parallel",)),
    )(a_pre, cw, cw, cb, cb)


def _conv_gate_bwd(a_pre, cw, cb, dact, name):
    s = a_pre.shape[1]
    tc, nj, a_spec, w_specs, b_specs = _conv_specs(s)

    def body(a_ref, wg_ref, wu_ref, bg_ref, bu_ref, dact_ref, dap_ref, dcw_ref, dcb_ref):
        rows = lax.broadcasted_iota(jnp.int32, (s, tc), 0)
        shifted = []
        pre = []
        for part, (w_ref, b_ref) in enumerate(((wg_ref, bg_ref), (wu_ref, bu_ref))):
            a = a_ref[part]
            prev, nxt = _shift_rows(a, rows)
            shifted.append((prev, a, nxt))
            pre.append(b_ref[...] + prev * w_ref[0:1, :] + a * w_ref[1:2, :] + nxt * w_ref[2:3, :])
        g, u = pre
        sg = 1.0 / (1.0 + jnp.exp(-g))
        dact_v = dact_ref[...]
        das = (dact_v * u * (sg * (1.0 + g * (1.0 - sg))), dact_v * (g * sg))
        for part, w_ref in enumerate((wg_ref, wu_ref)):
            da = das[part]
            prev, a, nxt = shifted[part]
            da_prev, da_next = _shift_rows(da, rows)
            dap_ref[part] = (da_next * w_ref[0:1, :] + da * w_ref[1:2, :] + da_prev * w_ref[2:3, :]).astype(BF16)
            dcw_ref[part, 0:1, :] = _sum_rows(prev * da)
            dcw_ref[part, 1:2, :] = _sum_rows(a * da)
            dcw_ref[part, 2:3, :] = _sum_rows(nxt * da)
            dcb_ref[part] = _sum_rows(da)

    return _ordered_call(
        body, name=name,
        out_shape=(jax.ShapeDtypeStruct((2, s, D_FF), BF16), jax.ShapeDtypeStruct((2, 3, D_FF), F32),
                   jax.ShapeDtypeStruct((2, 1, D_FF), F32)),
        grid=(nj,),
        in_specs=[a_spec] + w_specs + b_specs + [pl.BlockSpec((s, tc), lambda j: (0, j))],
        out_specs=(pl.BlockSpec((2, s, tc), lambda j: (0, 0, j)), pl.BlockSpec((2, 3, tc), lambda j: (0, 0, j)),
                   pl.BlockSpec((2, 1, tc), lambda j: (0, 0, j))),
        compiler_params=_params(("parallel",)),
    )(a_pre, cw, cw, cb, cb, dact)


def _loss_head(y, target, name):
    s, d = y.shape
    tr = _row_tile(s)

    def body(y_ref, t_ref, loss_ref, dy_ref, dyb_ref):
        err = y_ref[...] - t_ref[...]

        @pl.when(pl.program_id(0) == 0)
        def _():
            loss_ref[...] = jnp.zeros_like(loss_ref)

        loss_ref[...] += jnp.broadcast_to(0.5 * _sum_all(_mean_last(err * err)), (8, 128))
        dy = err * (1.0 / d)
        dy_ref[...] = dy
        dyb_ref[...] = dy.astype(BF16)

    return _ordered_call(
        body, name=name,
        out_shape=(jax.ShapeDtypeStruct((8, 128), F32), jax.ShapeDtypeStruct((s, d), F32), jax.ShapeDtypeStruct((s, d), BF16)),
        grid=(s // tr,), in_specs=[_rows(d, tr), _rows(d, tr)],
        out_specs=(_const2((8, 128)), _rows(d, tr), _rows(d, tr)), compiler_params=_params(("arbitrary",)),
    )(y, target)


def _row_block(rows, cols, budget=1 << 20):
    if rows * cols <= budget:
        return rows
    best = None
    for tr in range(16, rows, 16):
        if rows % tr == 0 and tr * cols <= budget:
            best = tr
    assert best is not None, (rows, cols)
    return best


def _place_shard(x4, layer, j_arr, out_dtype, name):
    _, nh, r, cols = x4.shape
    tr = _row_block(r, cols)

    def body(j_ref, x_ref, o_ref):
        o_ref[...] = x_ref[...].astype(out_dtype)

    grid_spec = pltpu.PrefetchScalarGridSpec(
        num_scalar_prefetch=1, grid=(nh, r // tr),
        in_specs=[pl.BlockSpec((None, None, tr, cols), lambda h, i, j_ref: (layer, h, i, 0))],
        out_specs=pl.BlockSpec((None, None, tr, cols), lambda h, i, j_ref: (j_ref[0], h, i, 0)))
    return _ordered_call(
        body, name=name, out_shape=jax.ShapeDtypeStruct((N_CHIPS, nh, r, cols), out_dtype), grid_spec=grid_spec,
        compiler_params=_params(("parallel", "parallel")),
    )(j_arr, x4)


def _adamw(w, g, m, v, name, budget=1 << 18):
    rows, cols = w.shape
    tr = _row_block(rows, cols, budget)

    def body(w_ref, g_ref, m_ref, v_ref, go_ref, d_ref, nm_ref, nv_ref):
        gv = g_ref[...]
        go_ref[...] = gv
        mn = ADAM_B1 * m_ref[...] + (1.0 - ADAM_B1) * gv
        vn = ADAM_B2 * v_ref[...] + (1.0 - ADAM_B2) * (gv * gv)
        m_hat = mn / (1.0 - ADAM_B1 ** ADAM_STEP)
        v_hat = vn / (1.0 - ADAM_B2 ** ADAM_STEP)
        d_ref[...] = -ADAM_LR * (m_hat / (jnp.sqrt(v_hat) + ADAM_EPS) + ADAM_WD * w_ref[...])
        nm_ref[...] = mn
        nv_ref[...] = vn

    sds = jax.ShapeDtypeStruct((rows, cols), F32)
    return _ordered_call(
        body, name=name, out_shape=(sds, sds, sds, sds), grid=(rows // tr,),
        in_specs=[_rows(cols, tr)] * 4, out_specs=(_rows(cols, tr),) * 4, compiler_params=_params(("parallel",)),
    )(w, g, m, v)


def _pair_sum(g5, recv, c_arr, name):
    _, _, rh, cols = g5.shape
    tr = _row_block(rh, cols)

    def body(c_ref, g_ref, r_ref, o_ref):
        o_ref[...] = (g_ref[...].astype(F32) + r_ref[...].astype(F32)).astype(BF16)

    grid_spec = pltpu.PrefetchScalarGridSpec(
        num_scalar_prefetch=1, grid=(N_CHIPS, rh // tr),
        in_specs=[pl.BlockSpec((None, None, tr, cols), lambda j, i, c_ref: (j, c_ref[0], i, 0)),
                  pl.BlockSpec((None, tr, cols), lambda j, i, c_ref: (j, i, 0))],
        out_specs=pl.BlockSpec((None, tr, cols), lambda j, i, c_ref: (j, i, 0)))
    return _ordered_call(
        body, name=name, out_shape=jax.ShapeDtypeStruct((N_CHIPS, rh, cols), BF16), grid_spec=grid_spec,
        compiler_params=_params(("parallel", "parallel")),
    )(c_arr, g5, recv)


def _chip_sum(p4, recv3, j_arr, c_arr, layer, into, name):
    _, rh, cols = p4.shape
    tr = _row_block(rh, cols, 1 << 19)

    def body(j_ref, c_ref, p_ref, r_ref, *rest):
        o_ref = rest[-1]
        total = p_ref[...].astype(F32)
        for peer in range(3):
            total = total + r_ref[peer].astype(F32)
        o_ref[...] = total

    in_specs = [pl.BlockSpec((None, tr, cols), lambda i, j_ref, c_ref: (j_ref[0], i, 0)),
                pl.BlockSpec((3, tr, cols), lambda i, j_ref, c_ref: (0, i, 0))]
    operands = [j_arr, c_arr, p4, recv3]
    aliases = {}
    if into is not None:
        in_specs.append(ANY)
        operands.append(into)
        aliases = {4: 0}
    grid_spec = pltpu.PrefetchScalarGridSpec(
        num_scalar_prefetch=2, grid=(rh // tr,), in_specs=in_specs,
        out_specs=pl.BlockSpec((None, None, tr, cols), lambda i, j_ref, c_ref: (layer, c_ref[0], i, 0)))
    return _ordered_call(
        body, name=name, out_shape=jax.ShapeDtypeStruct((DEPTH, 2, rh, cols), F32), grid_spec=grid_spec,
        input_output_aliases=aliases, compiler_params=_params(("parallel",)),
    )(*operands)


def _place():
    x, y, c = lax.axis_index("x"), lax.axis_index("y"), lax.axis_index("c")
    chips = [(1 - x, y), (x, 1 - y), (1 - x, 1 - y)]
    return x, y, c, chips


HBM = pl.BlockSpec(memory_space=pltpu.HBM)
SEM = pl.BlockSpec(memory_space=pltpu.SEMAPHORE)
TOKEN = jax.ShapeDtypeStruct((8, 128), F32)


def _remote(src, dst, send_sem, recv_sem, to):
    return pltpu.make_async_remote_copy(src_ref=src, dst_ref=dst, send_sem=send_sem, recv_sem=recv_sem, device_id=to,
                                        device_id_type=MESH)


def _split_call(body, name, thru, sems_in=(), fresh=(), new_sems=(), after_last=True):
    n_t, n_s, n_f = len(thru), len(sems_in), len(fresh)

    def call_body(*refs):
        outs = refs[n_t + n_s:]
        body(refs[:n_t], refs[n_t:n_t + n_s], outs[1 + n_t:1 + n_t + n_f], outs[1 + n_t + n_f:])
        outs[0][...] = jnp.zeros_like(outs[0])

    out_shape = ([TOKEN] + [pltpu.HBM(t.shape, t.dtype) for t in thru] + [pltpu.HBM(shp, dt) for shp, dt in fresh]
                 + [pltpu.SemaphoreType.DMA(shp) for shp in new_sems])
    out_specs = [pl.BlockSpec(memory_space=pltpu.VMEM)] + [HBM] * (n_t + n_f) + [SEM] * len(new_sems)
    if not after_last:
        _Order.last = None
    out = _ordered_call(
        call_body, name=name, out_shape=tuple(out_shape), in_specs=[HBM] * n_t + [SEM] * n_s, out_specs=tuple(out_specs),
        input_output_aliases={i: 1 + i for i in range(n_t)},
        compiler_params=pltpu.CompilerParams(has_side_effects=pltpu.SideEffectType.DATAFLOW_SIDE_EFFECTING),
    )(*[pltpu.with_memory_space_constraint(t, pltpu.HBM) for t in thru], *sems_in)
    return out[1:1 + n_t], out[1 + n_t:1 + n_t + n_f], out[1 + n_t + n_f:]


class _Exchange:
    def __init__(self, weights, j_arr, c_arr):
        self.j_arr, self.c_arr = j_arr, c_arr
        self.groups = [(l, name) for l in range(DEPTH) for name in BIG_NAMES]
        self.shard_shape = {name: weights[name].shape[1:] for name in BIG_NAMES}
        self.conv_state, self.state = [], {}
        self.ready, self.conv_ready = {}, {}
        self.pending, self.tick, self.reduced = [], 0, {}

        def place(grp):
            l, name = grp
            nl, r, cols = weights[name].shape
            return _place_shard(weights[name].reshape(nl, 2, r // 2, cols), l, j_arr, BF16, f"place_{name}_l{l}")

        def start_copies(tag, convs, groups, bufs):
            n_c = len(convs)

            def start(thru, _, __, sems):
                x, y, c, chips = _place()
                j_me = 2 * x + y
                copies = []
                for i in range(len(thru)):
                    mine = thru[i].at[j_me] if i < n_c else thru[i].at[j_me, c]
                    copies += [_remote(mine, mine, sems[2 * i].at[k], sems[2 * i + 1].at[k], (*chip, c))
                               for k, chip in enumerate(chips)]
                for cp in copies:
                    cp.start()

            thru, _, sems = _split_call(start, tag, convs + bufs, new_sems=[(3,)] * (2 * (n_c + len(bufs))))
            self.conv_state += [(thru[i], sems[2 * i], sems[2 * i + 1]) for i in range(n_c)]
            for g, grp in enumerate(groups):
                self.state[grp] = (thru[n_c + g], sems[2 * (n_c + g)], sems[2 * (n_c + g) + 1])

        convs = [_place_shard(weights["conv_w"][:, None], l, j_arr, F32, f"place_conv_w_l{l}") for l in range(DEPTH)]
        start_copies("gather_start_first", convs, self.groups[:1], [place(self.groups[0])])
        start_copies("gather_start_rest", [], self.groups[1:], [place(grp) for grp in self.groups[1:]])

    def conv_w(self, l):
        if l not in self.conv_ready:
            buf, send, recv = self.conv_state[l]

            def wait(thru, sems, _, __):
                x, y, c, chips = _place()
                for k, chip in enumerate(chips):
                    mine, theirs = thru[0].at[2 * x + y], thru[0].at[2 * chip[0] + chip[1]]
                    _remote(mine, mine, sems[0].at[k], sems[1].at[k], (*chip, c)).wait_send()
                    _remote(theirs, theirs, sems[0].at[k], sems[1].at[k], (x, y, c)).wait_recv()

            (buf,), _, _ = _split_call(wait, f"gather_conv_w_l{l}", [buf], sems_in=[send, recv])
            self.conv_ready[l] = jnp.transpose(buf[:, 0], (1, 0, 2)).reshape(3, 2 * D_FF)
        return self.conv_ready[l]

    def weight(self, l, name):
        grp = (l, name)
        if grp not in self.ready:
            buf, send, recv = self.state[grp]

            def forward(thru, sems, _, new):
                x, y, c, chips = _place()
                for k, chip in enumerate(chips):
                    landed = thru[0].at[2 * chip[0] + chip[1], c]
                    _remote(landed, landed, new[0].at[k], sems[0].at[k], (x, y, c)).wait_recv()
                    _remote(landed, landed, new[0].at[k], new[1].at[k], (x, y, 1 - c)).start()

            (buf,), _, (fsend, frecv) = _split_call(forward, f"gather_pass_{name}_l{l}", [buf], sems_in=[recv],
                                                    new_sems=[(3,), (3,)])

            def finish(thru, sems, _, __):
                x, y, c, chips = _place()
                mine = thru[0].at[2 * x + y, c]
                for k, chip in enumerate(chips):
                    j_k = 2 * chip[0] + chip[1]
                    theirs, landed = thru[0].at[j_k, 1 - c], thru[0].at[j_k, c]
                    _remote(theirs, theirs, sems[1].at[k], sems[2].at[k], (x, y, c)).wait_recv()
                    _remote(landed, landed, sems[1].at[k], sems[2].at[k], (x, y, 1 - c)).wait_send()
                    _remote(mine, mine, sems[0].at[k], sems[2].at[k], (*chip, c)).wait_send()

            (buf,), _, _ = _split_call(finish, f"gather_done_{name}_l{l}", [buf], sems_in=[send, fsend, frecv])
            r, cols = self.shard_shape[name]
            self.ready[grp] = buf.reshape(N_CHIPS, r, cols) if name in ("w_in", "w_up") else buf.reshape(N_CHIPS * r, cols)
        return self.ready[grp]

    def grad(self, l, name, g):
        r, cols = self.shard_shape[name]
        g5 = g.reshape(N_CHIPS, 2, r // 2, cols)

        def start(thru, _, fresh, sems):
            x, y, c, _chips = _place()
            _remote(thru[0].at[:, 1 - c], fresh[0], sems[0], sems[1], (x, y, 1 - c)).start()

        (g5,), (recv,), sems = _split_call(start, f"pair_start_{name}_l{l}", [g5], fresh=[((N_CHIPS, r // 2, cols), BF16)],
                                          new_sems=[(), ()], after_last=False)
        self.pending.append(dict(l=l, name=name, stage=1, at=self.tick, bufs=(g5, recv), sems=sems))

    def _pair(self, grp):
        l, name = grp["l"], grp["name"]
        r, cols = self.shard_shape[name]

        def wait(thru, sems, _, __):
            x, y, c, _chips = _place()
            cp = _remote(thru[0].at[:, 1 - c], thru[1], sems[0], sems[1], (x, y, 1 - c))
            cp.wait_send()
            cp.wait_recv()

        (g5, recv), _, _ = _split_call(wait, f"pair_done_{name}_l{l}", list(grp["bufs"]), sems_in=list(grp["sems"]))
        p4 = _pair_sum(g5, recv, self.c_arr, f"pair_sum_{name}_l{l}")

        def start(thru, _, fresh, sems):
            x, y, c, chips = _place()
            for k, chip in enumerate(chips):
                _remote(thru[0].at[2 * chip[0] + chip[1]], fresh[0].at[k], sems[0].at[k], sems[1].at[k], (*chip, c)).start()

        (p4,), (recv3,), sems = _split_call(start, f"chips_start_{name}_l{l}", [p4], fresh=[((3, r // 2, cols), BF16)],
                                           new_sems=[(3,), (3,)], after_last=False)
        grp.update(stage=2, at=self.tick, bufs=(p4, recv3), sems=sems)

    def _chips(self, grp):
        l, name = grp["l"], grp["name"]

        def wait(thru, sems, _, __):
            x, y, c, chips = _place()
            for k, chip in enumerate(chips):
                cp = _remote(thru[0].at[2 * chip[0] + chip[1]], thru[1].at[k], sems[0].at[k], sems[1].at[k], (*chip, c))
                cp.wait_send()
                cp.wait_recv()

        (p4, recv3), _, _ = _split_call(wait, f"chips_done_{name}_l{l}", list(grp["bufs"]), sems_in=list(grp["sems"]))
        self.reduced[name] = _chip_sum(p4, recv3, self.j_arr, self.c_arr, l, self.reduced.get(name), f"chip_sum_{name}_l{l}")
        grp.update(stage=3)

    def point(self, drain=False):
        self.tick += 1
        for grp in self.pending:
            if grp["stage"] == 1 and (drain or grp["at"] < self.tick):
                self._pair(grp)
            elif grp["stage"] == 2 and (drain or grp["at"] + len(BIG_NAMES) <= self.tick):
                self._chips(grp)

    def finish(self):
        self.point(drain=True)
        self.point(drain=True)
        joined = _join_halves([self.reduced[name] for name in BIG_NAMES])
        return {name: buf.reshape((DEPTH,) + self.shard_shape[name]) for name, buf in zip(BIG_NAMES, joined)}


def _join_halves(bufs):
    n = len(bufs)

    def body(*refs):
        dsts = refs[n:2 * n]
        send_sems, recv_sems = refs[2 * n:]
        x, y, c, _ = _place()
        copies = [pltpu.make_async_remote_copy(src_ref=dsts[a].at[:, c], dst_ref=dsts[a].at[:, c], send_sem=send_sems.at[a],
                                               recv_sem=recv_sems.at[a], device_id=(x, y, 1 - c), device_id_type=MESH)
                  for a in range(n)]
        for cp in copies:
            cp.start()
        for a in range(n):
            copies[a].wait_send()
            theirs = dsts[a].at[:, 1 - c]
            pltpu.make_async_remote_copy(src_ref=theirs, dst_ref=theirs, send_sem=send_sems.at[a], recv_sem=recv_sems.at[a],
                                         device_id=(x, y, c), device_id_type=MESH).wait_recv()

    return _ordered_call(
        body, name="join_halves", out_shape=tuple(jax.ShapeDtypeStruct(b.shape, b.dtype) for b in bufs),
        in_specs=[ANY] * n, out_specs=tuple([ANY] * n), input_output_aliases={a: a for a in range(n)},
        scratch_shapes=[pltpu.SemaphoreType.DMA((n,))] * 2,
    )(*bufs)


def _all_reduce_small(v):
    rows, lanes = v.shape

    def body(x_ref, all_ref, sum_ref, send_sems, recv_sems, local_sem):
        x, y, c, chips = _place()
        me, sibling = (x, y, c), (x, y, 1 - c)

        def slot(px, py, pc):
            return all_ref.at[pl.ds(pl.multiple_of((4 * px + 2 * py + pc) * rows, 8), rows), :]

        def copy(k, block, to, src=None):
            return pltpu.make_async_remote_copy(src_ref=slot(*block) if src is None else src, dst_ref=slot(*block),
                                                send_sem=send_sems.at[k], recv_sem=recv_sems.at[k], device_id=to,
                                                device_id_type=MESH)

        mine = pltpu.make_async_copy(x_ref, slot(*me), local_sem)
        mine.start()
        first = [copy(0, me, sibling, src=x_ref)]
        first += [copy(1 + k, me, (*chip, c), src=x_ref) for k, chip in enumerate(chips)]
        for cp in first:
            cp.start()
        passed = [copy(4 + k, (*chip, c), sibling) for k, chip in enumerate(chips)]
        for k, chip in enumerate(chips):
            copy(1 + k, (*chip, c), me).wait_recv()
            passed[k].start()
        copy(0, sibling, me).wait_recv()
        for k, chip in enumerate(chips):
            copy(4 + k, (*chip, 1 - c), me).wait_recv()
        for cp in first + passed:
            cp.wait_send()
        mine.wait()
        total = all_ref[0:rows, :]
        for dev in range(1, 8):
            total = total + all_ref[dev * rows:(dev + 1) * rows, :]
        sum_ref[...] = total

    vm = pl.BlockSpec(memory_space=pltpu.VMEM)
    return _ordered_call(
        body, name="all_reduce_small",
        out_shape=(jax.ShapeDtypeStruct((8 * rows, lanes), v.dtype), jax.ShapeDtypeStruct((rows, lanes), v.dtype)),
        in_specs=[vm], out_specs=(vm, vm),
        scratch_shapes=[pltpu.SemaphoreType.DMA((7,)), pltpu.SemaphoreType.DMA((7,)), pltpu.SemaphoreType.DMA],
        compiler_params=pltpu.CompilerParams(vmem_limit_bytes=V7X_VMEM_LIMIT),
    )(v)[1]


def _rope_tables(s):
    inv_freq = ROPE_THETA ** (-jnp.arange(0, HEAD_DIM, 2, dtype=F32) / HEAD_DIM)
    ang = jnp.arange(s, dtype=F32)[:, None] * inv_freq[None, :]
    cos, sin = jnp.cos(ang), jnp.sin(ang)
    return jnp.concatenate([cos, cos], axis=-1), jnp.concatenate([-sin, sin], axis=-1)


def _local_step(x, target, ex, small):
    s = x.shape[0]
    cosf, sinf = _rope_tables(s)
    saved = []
    for l in range(DEPTH):
        p = small[l]
        t = f"l{l}"
        h = _rms_fwd(x, p["norm1_g"], f"norm1_{t}")
        z = _matmul(h, ex.weight(l, "w_in"), mode="nn", out_dtype=F32, tm=1024, tn=896, tk=2048, b_parts=4, name=f"proj_in_{t}")
        qn, kn, vb, ug, vn = _proj_post(z, p["q_norm_g"], p["k_norm_g"], p["sgu_ln_g"], p["sgu_ln_b"], cosf, sinf, f"proj_post_{t}")
        attn, sgu, mixed = _mixer_fwd(qn, kn, vb, ug, vn, p["w_s_bf16"], p["b_s_tile"], p["sink"], p["attn_out_g"],
                                      p["sgu_out_g"], f"mixer_{t}")
        x1 = _matmul(mixed, ex.weight(l, "w_o"), mode="nn", out_dtype=F32, tm=1024, tn=512, tk=2048, res=x, name=f"proj_out_{t}")
        h2 = _rms_fwd(x1, p["norm2_g"], f"norm2_{t}")
        a_pre = _matmul(h2, ex.weight(l, "w_up"), mode="nn", out_dtype=F32, tm=1024, tn=1408, tk=2048, b_parts=4, out_parts=2,
                        name=f"ffn_up_{t}")
        act = _conv_gate_fwd(a_pre, ex.conv_w(l), p["conv_b"], f"conv_gate_{t}")
        x2 = _matmul(act, ex.weight(l, "w_down"), mode="nn", out_dtype=F32, tm=512, tn=512, tk=D_FF, res=x1, name=f"ffn_down_{t}")
        saved.append(dict(x=x, h=h, z=z, qn=qn, kn=kn, vb=vb, ug=ug, vn=vn, attn=attn, sgu=sgu, mixed=mixed, x1=x1, h2=h2,
                          a_pre=a_pre, act=act))
        x = x2
    loss_tile, dx, dxb = _loss_head(x, target, "loss_head")
    small_grads = [None] * DEPTH
    for l in reversed(range(DEPTH)):
        p, sv = small[l], saved[l]
        t = f"l{l}"
        ex.grad(l, "w_down", _matmul(sv["act"], dxb, mode="tn", out_dtype=BF16, tm=512, tn=1024, tk=2048, name=f"g_w_down_{t}"))
        dact = _matmul(dxb, ex.weight(l, "w_down"), mode="nt", out_dtype=F32, tm=1024, tn=512, tk=2048, name=f"d_act_{t}")
        dap, dcw, dcb = _conv_gate_bwd(sv["a_pre"], ex.conv_w(l), p["conv_b"], dact, f"conv_gate_bwd_{t}")
        ex.point()
        ex.grad(l, "w_up", _matmul(sv["h2"], dap, mode="tn", out_dtype=BF16, tm=1024, tn=1408, tk=2048, b_parts=2, out_parts=4,
                                   name=f"g_w_up_{t}"))
        dh2 = _matmul_nt_slabs(dap, ex.weight(l, "w_up"), tm=512, tn=256, a_parts=2, name=f"d_h2_{t}")
        dx1, dx1b, dg2 = _rms_bwd(sv["x1"], p["norm2_g"], dh2, dx, f"norm2_bwd_{t}")
        ex.point()
        ex.grad(l, "w_o", _matmul(sv["mixed"], dx1b, mode="tn", out_dtype=BF16, tm=1024, tn=512, tk=2048, name=f"g_w_o_{t}"))
        dmixed = _matmul(dx1b, ex.weight(l, "w_o"), mode="nt", out_dtype=F32, tm=1024, tn=512, tk=2048, name=f"d_mixed_{t}")
        dqn, dkn, dvb, dug, dvn, dws, dbs, dsk, dga, dgs = _mixer_bwd(
            sv["qn"], sv["kn"], sv["vb"], sv["ug"], sv["vn"], sv["attn"], sv["sgu"], dmixed, p["w_s_bf16"], p["b_s_tile"],
            p["sink"], p["attn_out_g"], p["sgu_out_g"], f"mixer_bwd_{t}")
        dz, dqg, dkg, dlg, dlb = _proj_post_bwd(sv["z"], dqn, dkn, dvb, dug, dvn, p["q_norm_g"], p["k_norm_g"], p["sgu_ln_g"],
                                                 cosf, sinf, f"proj_post_bwd_{t}")
        ex.point()
        ex.grad(l, "w_in", _matmul(sv["h"], dz, mode="tn", out_dtype=BF16, tm=1024, tn=896, tk=2048, out_parts=4,
                                   name=f"g_w_in_{t}"))
        dh = _matmul_nt_slabs(dz, ex.weight(l, "w_in"), tm=1024, tn=512, name=f"d_h_{t}")
        dx, dxb, dg1 = _rms_bwd(sv["x"], p["norm1_g"], dh, dx1, f"norm1_bwd_{t}")
        ex.point()
        small_grads[l] = dict(
            norm1_g=dg1[0], q_norm_g=dqg[0], k_norm_g=dkg[0], sink=dsk[:, 0], sgu_ln_g=dlg[0], sgu_ln_b=dlb[0], w_s=dws,
            b_s=dbs[:, :, 0], attn_out_g=dga[0], sgu_out_g=dgs[0], norm2_g=dg2[0],
            conv_w=jnp.concatenate([dcw[0], dcw[1]], axis=-1), conv_b=jnp.concatenate([dcb[0, 0], dcb[1, 0]], axis=-1))
    return loss_tile, dx, small_grads


def _small_views(l, norm1_g, q_norm_g, k_norm_g, sink, sgu_ln_g, sgu_ln_b, w_s, b_s, attn_out_g, sgu_out_g, norm2_g, conv_b):
    return dict(
        norm1_g=norm1_g[l][None], q_norm_g=q_norm_g[l][None], k_norm_g=k_norm_g[l][None], sink=sink[l],
        sgu_ln_g=sgu_ln_g[l][None], sgu_ln_b=sgu_ln_b[l][None], w_s_bf16=w_s[l].astype(BF16),
        b_s_tile=jnp.broadcast_to(b_s[l][:, :, None], (N_GMLP_HEADS, BLOCK, BLOCK)), attn_out_g=attn_out_g[l][None],
        sgu_out_g=sgu_out_g[l][None], norm2_g=norm2_g[l][None], conv_b=conv_b[l][None])


SMALL_NAMES = ("norm1_g", "q_norm_g", "k_norm_g", "sink", "sgu_ln_g", "sgu_ln_b", "w_s", "b_s", "attn_out_g", "sgu_out_g",
               "norm2_g", "conv_w", "conv_b")
BIG_NAMES = ("w_in", "w_o", "w_up", "w_down")
PACK_LANES = 128
PACK_ALIGN = 8 * PACK_LANES


def _pack_rows(shape):
    return -(-math.prod(shape) // PACK_ALIGN) * 8


def _pack(arrays):
    parts = []
    for a in arrays:
        flat = a.reshape(-1)
        parts.append(jnp.pad(flat, (0, _pack_rows(a.shape) * PACK_LANES - flat.shape[0])).reshape(-1, PACK_LANES))
    return jnp.concatenate(parts, axis=0)


def _unpack(packed, shapes):
    out, at = [], 0
    for shp in shapes:
        rows = _pack_rows(shp)
        out.append(packed[at:at + rows].reshape(-1)[:math.prod(shp)].reshape(shp))
        at += rows
    return out


def kernel(x, norm1_g, w_in, q_norm_g, k_norm_g, sink, sgu_ln_g, sgu_ln_b, w_s, b_s, attn_out_g, sgu_out_g, w_o, norm2_g, w_up, conv_w, conv_b, w_down, loss_target, m_norm1_g, m_w_in, m_q_norm_g, m_k_norm_g, m_sink, m_sgu_ln_g, m_sgu_ln_b, m_w_s, m_b_s, m_attn_out_g, m_sgu_out_g, m_w_o, m_norm2_g, m_w_up, m_conv_w, m_conv_b, m_w_down, v_norm1_g, v_w_in, v_q_norm_g, v_k_norm_g, v_sink, v_sgu_ln_g, v_sgu_ln_b, v_w_s, v_b_s, v_attn_out_g, v_sgu_out_g, v_w_o, v_norm2_g, v_w_up, v_conv_w, v_conv_b, v_w_down):
    weights = dict(norm1_g=norm1_g, w_in=w_in, q_norm_g=q_norm_g, k_norm_g=k_norm_g, sink=sink, sgu_ln_g=sgu_ln_g,
                   sgu_ln_b=sgu_ln_b, w_s=w_s, b_s=b_s, attn_out_g=attn_out_g, sgu_out_g=sgu_out_g, w_o=w_o, norm2_g=norm2_g,
                   w_up=w_up, conv_w=conv_w, conv_b=conv_b, w_down=w_down)
    m_in = dict(norm1_g=m_norm1_g, w_in=m_w_in, q_norm_g=m_q_norm_g, k_norm_g=m_k_norm_g, sink=m_sink, sgu_ln_g=m_sgu_ln_g,
                sgu_ln_b=m_sgu_ln_b, w_s=m_w_s, b_s=m_b_s, attn_out_g=m_attn_out_g, sgu_out_g=m_sgu_out_g, w_o=m_w_o,
                norm2_g=m_norm2_g, w_up=m_w_up, conv_w=m_conv_w, conv_b=m_conv_b, w_down=m_w_down)
    v_in = dict(norm1_g=v_norm1_g, w_in=v_w_in, q_norm_g=v_q_norm_g, k_norm_g=v_k_norm_g, sink=v_sink, sgu_ln_g=v_sgu_ln_g,
                sgu_ln_b=v_sgu_ln_b, w_s=v_w_s, b_s=v_b_s, attn_out_g=v_attn_out_g, sgu_out_g=v_sgu_out_g, w_o=v_w_o,
                norm2_g=v_norm2_g, w_up=v_w_up, conv_w=v_conv_w, conv_b=v_conv_b, w_down=v_w_down)
    cx, cy, cc = lax.axis_index("x"), lax.axis_index("y"), lax.axis_index("c")
    j_me = 2 * cx + cy
    c_arr = jnp.reshape(cc, (1,)).astype(jnp.int32)
    j_arr = jnp.reshape(j_me, (1,)).astype(jnp.int32)

    _Order.last = None
    ex = _Exchange(weights, j_arr, c_arr)
    small = [_small_views(l, norm1_g, q_norm_g, k_norm_g, sink, sgu_ln_g, sgu_ln_b, w_s, b_s, attn_out_g, sgu_out_g, norm2_g,
                          conv_b) for l in range(DEPTH)]
    loss_tile, dx, small_grads = _local_step(x[0], loss_target[0], ex, small)
    joined = ex.finish()

    small_partials = [jnp.stack([small_grads[l][nm] for l in range(DEPTH)]) for nm in SMALL_NAMES]
    small_shapes = [a.shape for a in small_partials]
    reduced = _all_reduce_small(_pack(small_partials + [loss_tile[0, 0:1]]))
    small_full = dict(zip(SMALL_NAMES + ("loss",), _unpack(reduced, small_shapes + [(1,)])))
    loss = small_full.pop("loss")[0]
    cw_cols = 2 * D_FF // N_CHIPS
    small_full["conv_w"] = lax.dynamic_slice_in_dim(small_full["conv_w"], j_me * cw_cols, cw_cols, axis=2)
    pw, pg, pm, pv = (_pack([src[nm] for nm in SMALL_NAMES]) for src in (weights, small_full, m_in, v_in))
    _, sd, sm, sv = _adamw(pw, pg, pm, pv, "adamw_small", budget=1 << 20)
    shapes = [weights[nm].shape for nm in SMALL_NAMES]
    grads = dict(small_full)
    delta = dict(zip(SMALL_NAMES, _unpack(sd, shapes)))
    new_m = dict(zip(SMALL_NAMES, _unpack(sm, shapes)))
    new_v = dict(zip(SMALL_NAMES, _unpack(sv, shapes)))

    for name in BIG_NAMES:
        l_, rows, cols = weights[name].shape
        flat = lambda a: a.reshape(l_ * rows, cols)
        outs = _adamw(flat(weights[name]), flat(joined[name]), flat(m_in[name]), flat(v_in[name]), f"adamw_{name}")
        grads[name], delta[name], new_m[name], new_v[name] = (a.reshape(l_, rows, cols) for a in outs)

    order = ("norm1_g", "w_in", "q_norm_g", "k_norm_g", "sink", "sgu_ln_g", "sgu_ln_b", "w_s", "b_s", "attn_out_g", "sgu_out_g",
             "w_o", "norm2_g", "w_up", "conv_w", "conv_b", "w_down")
    return (loss, dx[None], *[grads[nm] for nm in order], *[delta[nm] for nm in order], *[new_m[nm] for nm in order],
            *[new_v[nm] for nm in order])
```

```python
import functools
import math

import jax
import jax.numpy as jnp
from jax import lax
from jax.experimental import pallas as pl
from jax.experimental.pallas import tpu as pltpu

F32 = jnp.float32
BF16 = jnp.bfloat16

D_MODEL = 2048
HEAD_DIM = 128
ATTN_WIDTH = 1024
N_Q_HEADS = 8
N_KV_HEADS = 2
GQA_GROUP = 4
KV_WIDTH = 256
GMLP_WIDTH = 1024
N_GMLP_HEADS = 8
BLOCK = 128
IN_WIDTH = 3584
D_FF = 5632
DEPTH = 2
EPS = 1e-6
MASK_VALUE = -1e30
ROPE_THETA = 10000.0
N_CHIPS = 4

ADAM_LR = 0.001
ADAM_B1 = 0.9
ADAM_B2 = 0.999
ADAM_EPS = 1e-08
ADAM_WD = 0.01
ADAM_STEP = 10

V7X_VMEM_LIMIT = 48 * 1024 * 1024
MESH = pl.DeviceIdType.MESH

_GELU_C = math.sqrt(2.0 / math.pi)
_GELU_A = 0.044715


def _params(sem=None):
    return pltpu.CompilerParams(dimension_semantics=sem, vmem_limit_bytes=V7X_VMEM_LIMIT)


ANY = pl.BlockSpec(memory_space=pl.ANY)


class _Order:
    last = None


def _ordered_call(body, *, token_index=0, **kw):
    def run(*operands):
        tok = _Order.last
        if tok is None or any(op is tok for op in operands):
            call = pl.pallas_call(body, **kw)
        else:
            n_in = len(operands)

            def ordered_body(*refs):
                return body(*refs[:n_in], *refs[n_in + 1:])

            kw2 = dict(kw)
            if "grid_spec" in kw2:
                gs = kw2["grid_spec"]
                kw2["grid_spec"] = pltpu.PrefetchScalarGridSpec(
                    num_scalar_prefetch=gs.num_scalar_prefetch, grid=gs.grid, in_specs=list(gs.in_specs) + [ANY],
                    out_specs=gs.out_specs, scratch_shapes=gs.scratch_shapes)
            else:
                kw2["in_specs"] = list(kw2["in_specs"]) + [ANY]
            call = pl.pallas_call(ordered_body, **kw2)
            operands = operands + (tok,)
        out = call(*operands)
        _Order.last = out[token_index] if isinstance(out, (tuple, list)) else out
        return out

    return run


def _gelu(x):
    return x * (0.5 * (1.0 + jnp.tanh(_GELU_C * (x + _GELU_A * (x * x * x)))))


def _gelu_grad(x):
    x2 = x * x
    t = jnp.tanh(_GELU_C * (x + _GELU_A * (x * x2)))
    return 0.5 * (1.0 + t) + 0.5 * x * (1.0 - t * t) * (_GELU_C * (1.0 + 3.0 * _GELU_A * x2))


def _mean_last(x):
    return jnp.mean(x, axis=-1, keepdims=True)


def _sum_rows(x):
    return jnp.sum(x, axis=0, keepdims=True)


def _sum_all(x):
    return jnp.sum(jnp.sum(x, axis=1, keepdims=True), axis=0, keepdims=True)


def _matmul(a, b, *, mode, out_dtype, tm, tn, tk, name, res=None, a_parts=0, b_parts=0, out_parts=0, b_lead=()):
    b_full = b
    b = jax.ShapeDtypeStruct(b.shape[len(b_lead):], b.dtype)
    if mode == "nn":
        assert not a_parts
        m, k = a.shape
        n = b.shape[0] * b.shape[2] if b_parts else b.shape[1]
    elif mode == "nt":
        m, k = (a.shape[1], a.shape[0] * a.shape[2]) if a_parts else a.shape
        n = b.shape[1] if b_parts else b.shape[0]
    else:
        assert not a_parts
        k, m = a.shape
        n = b.shape[0] * b.shape[2] if b_parts else b.shape[1]
    tm, tn, tk = min(tm, m), min(tn, n), min(tk, k)
    assert m % tm == 0 and n % tn == 0 and k % tk == 0, (name, m, n, k, tm, tn, tk)
    nm, nn, nk = m // tm, n // tn, k // tk

    def slab(idx, total_tiles, parts):
        per = total_tiles // parts
        assert per * parts == total_tiles, (name, total_tiles, parts)
        return idx // per, idx % per

    if mode == "nn":
        a_spec = pl.BlockSpec((tm, tk), lambda i, j, kk: (i, kk))
        if b_parts:
            b_spec = pl.BlockSpec((None, tk, tn), lambda i, j, kk: (slab(j, nn, b_parts)[0], kk, slab(j, nn, b_parts)[1]))
        else:
            b_spec = pl.BlockSpec((tk, tn), lambda i, j, kk: (kk, j))
        dims = (((1,), (0,)), ((), ()))
    elif mode == "nt":
        if a_parts:
            a_spec = pl.BlockSpec((None, tm, tk), lambda i, j, kk: (slab(kk, nk, a_parts)[0], i, slab(kk, nk, a_parts)[1]))
        else:
            a_spec = pl.BlockSpec((tm, tk), lambda i, j, kk: (i, kk))
        if b_parts:
            b_spec = pl.BlockSpec((None, tn, tk), lambda i, j, kk: (slab(kk, nk, b_parts)[0], j, slab(kk, nk, b_parts)[1]))
        else:
            b_spec = pl.BlockSpec((tn, tk), lambda i, j, kk: (j, kk))
        dims = (((1,), (1,)), ((), ()))
    else:
        a_spec = pl.BlockSpec((tk, tm), lambda i, j, kk: (kk, i))
        if b_parts:
            b_spec = pl.BlockSpec((None, tk, tn), lambda i, j, kk: (slab(j, nn, b_parts)[0], kk, slab(j, nn, b_parts)[1]))
        else:
            b_spec = pl.BlockSpec((tk, tn), lambda i, j, kk: (kk, j))
        dims = (((0,), (0,)), ((), ()))
    if out_parts:
        out_shape = jax.ShapeDtypeStruct((out_parts, m, n // out_parts), out_dtype)
        out_spec = pl.BlockSpec((None, tm, tn), lambda i, j, kk: (slab(j, nn, out_parts)[0], i, slab(j, nn, out_parts)[1]))
    else:
        out_shape = jax.ShapeDtypeStruct((m, n), out_dtype)
        out_spec = pl.BlockSpec((tm, tn), lambda i, j, kk: (i, j))
    if b_lead:
        inner_map = b_spec.index_map
        b_spec = pl.BlockSpec((None,) * len(b_lead) + tuple(b_spec.block_shape),
                              lambda i, j, kk: tuple(b_lead) + tuple(inner_map(i, j, kk)))
    in_specs = [a_spec, b_spec]
    operands = [a, b_full]
    if res is not None:
        in_specs.append(pl.BlockSpec((tm, tn), lambda i, j, kk: (i, j)))
        operands.append(res)

    def body(*refs):
        a_ref, b_ref = refs[0], refs[1]
        res_ref = refs[2] if res is not None else None
        o_ref = refs[3] if res is not None else refs[2]
        p = lax.dot_general(a_ref[...], b_ref[...], dims, preferred_element_type=F32)

        def finish(total):
            if res_ref is not None:
                total = res_ref[...] + total
            o_ref[...] = total.astype(out_dtype)

        if nk == 1:
            finish(p)
        else:
            acc_ref = refs[-1]
            kk = pl.program_id(2)

            @pl.when(kk == 0)
            def _():
                acc_ref[...] = p

            @pl.when(jnp.logical_and(kk > 0, kk < nk - 1))
            def _():
                acc_ref[...] += p

            @pl.when(kk == nk - 1)
            def _():
                finish(acc_ref[...] + p)

    scratch = [pltpu.VMEM((tm, tn), F32)] if nk > 1 else []
    return _ordered_call(
        body, name=name, out_shape=out_shape, grid=(nm, nn, nk), in_specs=in_specs, out_specs=out_spec,
        scratch_shapes=scratch, compiler_params=_params(("parallel", "parallel", "arbitrary")),
    )(*operands)


def _matmul_nt_slabs(a, b, *, tm, tn, name, a_parts=0):
    nslab, n, ks = b.shape
    m = a.shape[1] if a_parts else a.shape[0]
    tm, tn = min(tm, m), min(tn, n)
    assert m % tm == 0 and n % tn == 0, (name, m, n, tm, tn)
    if a_parts:
        per = nslab // a_parts
        assert per * a_parts == nslab and a.shape[2] == per * ks, (name, a.shape, b.shape)
        a_spec = pl.BlockSpec((a_parts, tm, per * ks), lambda i, j: (0, i, 0))
    else:
        assert a.shape[1] == nslab * ks, (name, a.shape, b.shape)
        a_spec = pl.BlockSpec((tm, nslab * ks), lambda i, j: (i, 0))

    def body(a_ref, b_ref, o_ref):
        total = None
        for sl in range(nslab):
            if a_parts:
                a_sl = a_ref[sl // per, :, (sl % per) * ks:(sl % per + 1) * ks]
            else:
                a_sl = a_ref[:, sl * ks:(sl + 1) * ks]
            p = lax.dot_general(a_sl, b_ref[sl], (((1,), (1,)), ((), ())), preferred_element_type=F32)
            total = p if total is None else total + p
        o_ref[...] = total

    return _ordered_call(
        body, name=name, out_shape=jax.ShapeDtypeStruct((m, n), F32), grid=(m // tm, n // tn),
        in_specs=[a_spec, pl.BlockSpec((nslab, tn, ks), lambda i, j: (0, j, 0))],
        out_specs=pl.BlockSpec((tm, tn), lambda i, j: (i, j)), compiler_params=_params(("parallel", "parallel")),
    )(a, b)


def _row_tile(s):
    return min(256, s)


def _rows(width, tr):
    return pl.BlockSpec((tr, width), lambda i: (i, 0))


def _const2(shape):
    return pl.BlockSpec(shape, lambda i: (0, 0))


def _rms_fwd(x, g, name):
    s, d = x.shape
    tr = _row_tile(s)

    def body(x_ref, g_ref, o_ref):
        xv = x_ref[...]
        r = lax.rsqrt(_mean_last(xv * xv) + EPS)
        o_ref[...] = (xv * r * g_ref[...]).astype(BF16)

    return _ordered_call(
        body, name=name, out_shape=jax.ShapeDtypeStruct((s, d), BF16), grid=(s // tr,),
        in_specs=[_rows(d, tr), _const2((1, d))], out_specs=_rows(d, tr), compiler_params=_params(("parallel",)),
    )(x, g)


def _rms_bwd(x, g, dh, dres, name):
    s, d = x.shape
    tr = _row_tile(s)

    def body(x_ref, g_ref, dh_ref, dres_ref, dx_ref, dxb_ref, dg_ref):
        xv, dy = x_ref[...], dh_ref[...]
        r = lax.rsqrt(_mean_last(xv * xv) + EPS)
        gdy = dy * g_ref[...]
        dx = dres_ref[...] + r * gdy - xv * ((r * r * r) * _mean_last(xv * gdy))
        dx_ref[...] = dx
        dxb_ref[...] = dx.astype(BF16)

        @pl.when(pl.program_id(0) == 0)
        def _():
            dg_ref[...] = jnp.zeros_like(dg_ref)

        dg_ref[...] += _sum_rows(xv * r * dy)

    return _ordered_call(
        body, name=name,
        out_shape=(jax.ShapeDtypeStruct((s, d), F32), jax.ShapeDtypeStruct((s, d), BF16), jax.ShapeDtypeStruct((1, d), F32)),
        grid=(s // tr,), in_specs=[_rows(d, tr), _const2((1, d)), _rows(d, tr), _rows(d, tr)],
        out_specs=(_rows(d, tr), _rows(d, tr), _const2((1, d))), compiler_params=_params(("arbitrary",)),
    )(x, g, dh, dres)


Q0, K0, V0, GU0, GV0 = 0, ATTN_WIDTH, ATTN_WIDTH + KV_WIDTH, ATTN_WIDTH + 2 * KV_WIDTH, ATTN_WIDTH + 2 * KV_WIDTH + GMLP_WIDTH


def _head(h, base=0):
    return slice(base + h * HEAD_DIM, base + (h + 1) * HEAD_DIM)


def _proj_post(z, qg, kg, lg, lb, cosf, sinf, name):
    s = z.shape[0]
    tr = _row_tile(s)

    def body(z_ref, qg_ref, kg_ref, lg_ref, lb_ref, cos_ref, sin_ref, qn_ref, kn_ref, vb_ref, ug_ref, vn_ref):
        cos, sin = cos_ref[...], sin_ref[...]

        def norm_rope(xh, g):
            y = xh * lax.rsqrt(_mean_last(xh * xh) + EPS) * g
            return y * cos + pltpu.roll(y, HEAD_DIM // 2, 1) * sin

        for h in range(N_Q_HEADS):
            qn_ref[:, _head(h)] = norm_rope(z_ref[:, _head(h, Q0)], qg_ref[...]).astype(BF16)
        for h in range(N_KV_HEADS):
            kn_ref[:, _head(h)] = norm_rope(z_ref[:, _head(h, K0)], kg_ref[...]).astype(BF16)
        vb_ref[...] = z_ref[:, V0:GU0].astype(BF16)
        ug_ref[...] = _gelu(z_ref[:, GU0:GV0])
        vg = _gelu(z_ref[:, GV0:IN_WIDTH])
        xc = vg - _mean_last(vg)
        y = xc * lax.rsqrt(_mean_last(xc * xc) + EPS)
        vn_ref[...] = (y * lg_ref[...] + lb_ref[...]).astype(BF16)

    return _ordered_call(
        body, name=name,
        out_shape=(jax.ShapeDtypeStruct((s, ATTN_WIDTH), BF16), jax.ShapeDtypeStruct((s, KV_WIDTH), BF16),
                   jax.ShapeDtypeStruct((s, KV_WIDTH), BF16), jax.ShapeDtypeStruct((s, GMLP_WIDTH), F32),
                   jax.ShapeDtypeStruct((s, GMLP_WIDTH), BF16)),
        grid=(s // tr,),
        in_specs=[_rows(IN_WIDTH, tr), _const2((1, HEAD_DIM)), _const2((1, HEAD_DIM)), _const2((1, GMLP_WIDTH)),
                  _const2((1, GMLP_WIDTH)), _rows(HEAD_DIM, tr), _rows(HEAD_DIM, tr)],
        out_specs=(_rows(ATTN_WIDTH, tr), _rows(KV_WIDTH, tr), _rows(KV_WIDTH, tr), _rows(GMLP_WIDTH, tr), _rows(GMLP_WIDTH, tr)),
        compiler_params=_params(("parallel",)),
    )(z, qg, kg, lg, lb, cosf, sinf)


def _proj_post_bwd(z, dqn, dkn, dvb, dug, dvn, qg, kg, lg, cosf, sinf, name):
    s = z.shape[0]
    tr = _row_tile(s)

    def body(z_ref, dqn_ref, dkn_ref, dvb_ref, dug_ref, dvn_ref, qg_ref, kg_ref, lg_ref, cos_ref, sin_ref,
             dz_ref, dqg_ref, dkg_ref, dlg_ref, dlb_ref):
        cos, sin = cos_ref[...], sin_ref[...]

        @pl.when(pl.program_id(0) == 0)
        def _():
            dqg_ref[...] = jnp.zeros_like(dqg_ref)
            dkg_ref[...] = jnp.zeros_like(dkg_ref)
            dlg_ref[...] = jnp.zeros_like(dlg_ref)
            dlb_ref[...] = jnp.zeros_like(dlb_ref)

        def norm_rope_bwd(xh, g, dout):
            dy = dout * cos - pltpu.roll(dout, HEAD_DIM // 2, 1) * sin
            r = lax.rsqrt(_mean_last(xh * xh) + EPS)
            xhat = xh * r
            gdy = dy * g
            return r * (gdy - xhat * _mean_last(xhat * gdy)), _sum_rows(xhat * dy)

        dqg = jnp.zeros((1, HEAD_DIM), F32)
        for h in range(N_Q_HEADS):
            dx, dg = norm_rope_bwd(z_ref[:, _head(h, Q0)], qg_ref[...], dqn_ref[:, _head(h)])
            dz_ref[:, _head(h, Q0)] = dx.astype(BF16)
            dqg = dqg + dg
        dqg_ref[...] += dqg
        dkg = jnp.zeros((1, HEAD_DIM), F32)
        for h in range(N_KV_HEADS):
            dx, dg = norm_rope_bwd(z_ref[:, _head(h, K0)], kg_ref[...], dkn_ref[:, _head(h)])
            dz_ref[:, _head(h, K0)] = dx.astype(BF16)
            dkg = dkg + dg
        dkg_ref[...] += dkg
        dz_ref[:, V0:GU0] = dvb_ref[...].astype(BF16)
        dz_ref[:, GU0:GV0] = (dug_ref[...] * _gelu_grad(z_ref[:, GU0:GV0])).astype(BF16)
        gv = z_ref[:, GV0:IN_WIDTH]
        vg = _gelu(gv)
        xc = vg - _mean_last(vg)
        r = lax.rsqrt(_mean_last(xc * xc) + EPS)
        xhat = xc * r
        dvn_v = dvn_ref[...]
        dlg_ref[...] += _sum_rows(xhat * dvn_v)
        dlb_ref[...] += _sum_rows(dvn_v)
        dxh = dvn_v * lg_ref[...]
        dvg = r * (dxh - _mean_last(dxh) - xhat * _mean_last(dxh * xhat))
        dz_ref[:, GV0:IN_WIDTH] = (dvg * _gelu_grad(gv)).astype(BF16)

    return _ordered_call(
        body, name=name,
        out_shape=(jax.ShapeDtypeStruct((s, IN_WIDTH), BF16), jax.ShapeDtypeStruct((1, HEAD_DIM), F32),
                   jax.ShapeDtypeStruct((1, HEAD_DIM), F32), jax.ShapeDtypeStruct((1, GMLP_WIDTH), F32),
                   jax.ShapeDtypeStruct((1, GMLP_WIDTH), F32)),
        grid=(s // tr,),
        in_specs=[_rows(IN_WIDTH, tr), _rows(ATTN_WIDTH, tr), _rows(KV_WIDTH, tr), _rows(KV_WIDTH, tr), _rows(GMLP_WIDTH, tr),
                  _rows(GMLP_WIDTH, tr), _const2((1, HEAD_DIM)), _const2((1, HEAD_DIM)), _const2((1, GMLP_WIDTH)),
                  _rows(HEAD_DIM, tr), _rows(HEAD_DIM, tr)],
        out_specs=(_rows(IN_WIDTH, tr), _const2((1, HEAD_DIM)), _const2((1, HEAD_DIM)), _const2((1, GMLP_WIDTH)),
                   _const2((1, GMLP_WIDTH))),
        compiler_params=_params(("arbitrary",)),
    )(z, dqn, dkn, dvb, dug, dvn, qg, kg, lg, cosf, sinf)


def _band_valid(n, s):
    i = lax.broadcasted_iota(jnp.int32, (BLOCK, 3 * BLOCK), 0)
    j = lax.broadcasted_iota(jnp.int32, (BLOCK, 3 * BLOCK), 1)
    k_pos = n * BLOCK - BLOCK + j
    return (jnp.abs(j - BLOCK - i) <= BLOCK) & (k_pos >= 0) & (k_pos < s)


def _probs(q, kb, sink_h, valid):
    sc = lax.dot_general(q, kb, (((1,), (1,)), ((), ())), preferred_element_type=F32) * (HEAD_DIM ** -0.5)
    sc = jnp.where(valid, sc, MASK_VALUE)
    m = jnp.maximum(jnp.max(sc, axis=-1, keepdims=True), sink_h)
    p = jnp.exp(sc - m)
    es = jnp.exp(sink_h - m)
    den = jnp.sum(p, axis=-1, keepdims=True) + es
    return p / den, es / den


def _band_specs(width, nb):
    return [pl.BlockSpec((BLOCK, width), lambda n: (jnp.maximum(n - 1, 0), 0)),
            pl.BlockSpec((BLOCK, width), lambda n: (n, 0)),
            pl.BlockSpec((BLOCK, width), lambda n: (jnp.minimum(n + 1, nb - 1), 0))]


def _blk(width):
    return pl.BlockSpec((BLOCK, width), lambda n: (n, 0))


def _whole3(shape):
    return pl.BlockSpec(shape, lambda n: (0, 0, 0))


def _smem():
    return pl.BlockSpec(memory_space=pltpu.SMEM)


def _mixer_fwd(qn, kn, vb, ug, vn, wsb, bsb, sink, ga, gs, name):
    s = qn.shape[0]
    nb = s // BLOCK

    def body(sink_ref, q_ref, kp_ref, kc_ref, kx_ref, vp_ref, vc_ref, vx_ref, ug_ref, vn_ref, ws_ref, bs_ref, ga_ref, gs_ref,
             attn_ref, sgu_ref, mix_ref):
        n = pl.program_id(0)
        valid = _band_valid(n, s)
        ssq = jnp.zeros((BLOCK, 1), F32)
        for kh in range(N_KV_HEADS):
            kb = jnp.concatenate([kp_ref[:, _head(kh)], kc_ref[:, _head(kh)], kx_ref[:, _head(kh)]], axis=0)
            vbd = jnp.concatenate([vp_ref[:, _head(kh)], vc_ref[:, _head(kh)], vx_ref[:, _head(kh)]], axis=0)
            for g in range(GQA_GROUP):
                h = kh * GQA_GROUP + g
                p, _ = _probs(q_ref[:, _head(h)], kb, sink_ref[h], valid)
                o = jnp.dot(p.astype(BF16), vbd, preferred_element_type=F32)
                attn_ref[:, _head(h)] = o
                ssq = ssq + jnp.sum(o * o, axis=-1, keepdims=True)
        r = lax.rsqrt(ssq * (1.0 / ATTN_WIDTH) + EPS)
        mix_ref[:, 0:ATTN_WIDTH] = (attn_ref[...] * r * ga_ref[...]).astype(BF16)
        ssq = jnp.zeros((BLOCK, 1), F32)
        for h in range(N_GMLP_HEADS):
            f = jnp.dot(ws_ref[h], vn_ref[:, _head(h)], preferred_element_type=F32) + bs_ref[h]
            o = ug_ref[:, _head(h)] * f
            sgu_ref[:, _head(h)] = o
            ssq = ssq + jnp.sum(o * o, axis=-1, keepdims=True)
        r = lax.rsqrt(ssq * (1.0 / GMLP_WIDTH) + EPS)
        mix_ref[:, ATTN_WIDTH:D_MODEL] = (sgu_ref[...] * r * gs_ref[...]).astype(BF16)

    hh = (N_GMLP_HEADS, BLOCK, BLOCK)
    return _ordered_call(
        body, name=name,
        out_shape=(jax.ShapeDtypeStruct((s, ATTN_WIDTH), F32), jax.ShapeDtypeStruct((s, GMLP_WIDTH), F32),
                   jax.ShapeDtypeStruct((s, D_MODEL), BF16)),
        grid=(nb,),
        in_specs=[_smem(), _blk(ATTN_WIDTH)] + _band_specs(KV_WIDTH, nb) + _band_specs(KV_WIDTH, nb)
        + [_blk(GMLP_WIDTH), _blk(GMLP_WIDTH), _whole3(hh), _whole3(hh),
           pl.BlockSpec((1, ATTN_WIDTH), lambda n: (0, 0)), pl.BlockSpec((1, GMLP_WIDTH), lambda n: (0, 0))],
        out_specs=(_blk(ATTN_WIDTH), _blk(GMLP_WIDTH), _blk(D_MODEL)),
        compiler_params=_params(("parallel",)),
    )(sink, qn, kn, kn, kn, vb, vb, vb, ug, vn, wsb, bsb, ga, gs)


def _mixer_bwd(qn, kn, vb, ug, vn, attn, sgu, dmixed, wsb, bsb, sink, ga, gs, name):
    s = qn.shape[0]
    nb = s // BLOCK
    tn_dims = (((0,), (0,)), ((), ()))
    nt_dims = (((1,), (1,)), ((), ()))

    def body(sink_ref, q_ref, kp_ref, kc_ref, kx_ref, vp_ref, vc_ref, vx_ref, ug_ref, vn_ref, attn_ref, sgu_ref, dm_ref,
             ws_ref, bs_ref, ga_ref, gs_ref,
             dq_ref, dk_ref, dv_ref, dug_ref, dvn_ref, dws_ref, dbs_ref, dsk_ref, dga_ref, dgs_ref, dk_acc, dv_acc):
        n = pl.program_id(0)

        @pl.when(n == 0)
        def _():
            for ref in (dk_acc, dv_acc, dws_ref, dbs_ref, dsk_ref, dga_ref, dgs_ref):
                ref[...] = jnp.zeros_like(ref)

        def out_norm_bwd(o, g, dy):
            r = lax.rsqrt(_mean_last(o * o) + EPS)
            gdy = dy * g
            return r * gdy - o * ((r * r * r) * _mean_last(o * gdy)), _sum_rows(o * r * dy)

        d_attn, dga = out_norm_bwd(attn_ref[...], ga_ref[...], dm_ref[:, 0:ATTN_WIDTH])
        dga_ref[...] += dga
        d_sgu, dgs = out_norm_bwd(sgu_ref[...], gs_ref[...], dm_ref[:, ATTN_WIDTH:D_MODEL])
        dgs_ref[...] += dgs

        for h in range(N_GMLP_HEADS):
            vn_h = vn_ref[:, _head(h)]
            f = jnp.dot(ws_ref[h], vn_h, preferred_element_type=F32) + bs_ref[h]
            ds_h = d_sgu[:, _head(h)]
            dug_ref[:, _head(h)] = ds_h * f
            df = ds_h * ug_ref[:, _head(h)]
            dfb = df.astype(BF16)
            dvn_ref[:, _head(h)] = lax.dot_general(ws_ref[h], dfb, tn_dims, preferred_element_type=F32)
            dws_ref[h] += lax.dot_general(dfb, vn_h, nt_dims, preferred_element_type=F32)
            dbs_ref[h] += jnp.broadcast_to(jnp.sum(df, axis=-1, keepdims=True), (BLOCK, BLOCK))

        valid = _band_valid(n, s)
        row0 = pl.multiple_of(n * BLOCK, BLOCK)
        for kh in range(N_KV_HEADS):
            kb = jnp.concatenate([kp_ref[:, _head(kh)], kc_ref[:, _head(kh)], kx_ref[:, _head(kh)]], axis=0)
            vbd = jnp.concatenate([vp_ref[:, _head(kh)], vc_ref[:, _head(kh)], vx_ref[:, _head(kh)]], axis=0)
            dkb = jnp.zeros((3 * BLOCK, HEAD_DIM), F32)
            dvb = jnp.zeros((3 * BLOCK, HEAD_DIM), F32)
            for g in range(GQA_GROUP):
                h = kh * GQA_GROUP + g
                q = q_ref[:, _head(h)]
                p, p_sink = _probs(q, kb, sink_ref[h], valid)
                do = d_attn[:, _head(h)].astype(BF16)
                dp = lax.dot_general(do, vbd, nt_dims, preferred_element_type=F32)
                delta = jnp.sum(p * dp, axis=-1, keepdims=True)
                dsc = (p * (dp - delta) * (HEAD_DIM ** -0.5)).astype(BF16)
                dsk_ref[h:h + 1, :] += jnp.broadcast_to(_sum_all(-(p_sink * delta)), (1, BLOCK))
                dq_ref[:, _head(h)] = jnp.dot(dsc, kb, preferred_element_type=F32)
                dkb = dkb + lax.dot_general(dsc, q, tn_dims, preferred_element_type=F32)
                dvb = dvb + lax.dot_general(p.astype(BF16), do, tn_dims, preferred_element_type=F32)
            dk_acc[pl.ds(row0, 3 * BLOCK), _head(kh)] += dkb
            dv_acc[pl.ds(row0, 3 * BLOCK), _head(kh)] += dvb

        @pl.when(n == nb - 1)
        def _():
            dk_ref[...] = dk_acc[BLOCK:BLOCK + s, :]
            dv_ref[...] = dv_acc[BLOCK:BLOCK + s, :]

    hh = (N_GMLP_HEADS, BLOCK, BLOCK)
    full_kv = pl.BlockSpec((s, KV_WIDTH), lambda n: (0, 0))
    return _ordered_call(
        body, name=name,
        out_shape=(jax.ShapeDtypeStruct((s, ATTN_WIDTH), F32), jax.ShapeDtypeStruct((s, KV_WIDTH), F32),
                   jax.ShapeDtypeStruct((s, KV_WIDTH), F32), jax.ShapeDtypeStruct((s, GMLP_WIDTH), F32),
                   jax.ShapeDtypeStruct((s, GMLP_WIDTH), F32), jax.ShapeDtypeStruct(hh, F32), jax.ShapeDtypeStruct(hh, F32),
                   jax.ShapeDtypeStruct((N_Q_HEADS, BLOCK), F32), jax.ShapeDtypeStruct((1, ATTN_WIDTH), F32),
                   jax.ShapeDtypeStruct((1, GMLP_WIDTH), F32)),
        grid=(nb,),
        in_specs=[_smem(), _blk(ATTN_WIDTH)] + _band_specs(KV_WIDTH, nb) + _band_specs(KV_WIDTH, nb)
        + [_blk(GMLP_WIDTH), _blk(GMLP_WIDTH), _blk(ATTN_WIDTH), _blk(GMLP_WIDTH), _blk(D_MODEL), _whole3(hh), _whole3(hh),
           pl.BlockSpec((1, ATTN_WIDTH), lambda n: (0, 0)), pl.BlockSpec((1, GMLP_WIDTH), lambda n: (0, 0))],
        out_specs=(_blk(ATTN_WIDTH), full_kv, full_kv, _blk(GMLP_WIDTH), _blk(GMLP_WIDTH), _whole3(hh), _whole3(hh),
                   pl.BlockSpec((N_Q_HEADS, BLOCK), lambda n: (0, 0)), pl.BlockSpec((1, ATTN_WIDTH), lambda n: (0, 0)),
                   pl.BlockSpec((1, GMLP_WIDTH), lambda n: (0, 0))),
        scratch_shapes=[pltpu.VMEM((s + 2 * BLOCK, KV_WIDTH), F32), pltpu.VMEM((s + 2 * BLOCK, KV_WIDTH), F32)],
        compiler_params=_params(("arbitrary",)),
    )(sink, qn, kn, kn, kn, vb, vb, vb, ug, vn, attn, sgu, dmixed, wsb, bsb, ga, gs)


CONV_TILE = 128


def _shift_rows(a, rows):
    s = a.shape[0]
    prev = jnp.where(rows == 0, 0.0, pltpu.roll(a, 1, 0))
    nxt = jnp.where(rows == s - 1, 0.0, pltpu.roll(a, s - 1, 0))
    return prev, nxt


def _conv_specs(s):
    tc = CONV_TILE
    nj = D_FF // tc
    return (tc, nj, pl.BlockSpec((2, s, tc), lambda j: (0, 0, j)),
            [pl.BlockSpec((3, tc), lambda j: (0, j)), pl.BlockSpec((3, tc), lambda j: (0, j + nj))],
            [pl.BlockSpec((1, tc), lambda j: (0, j)), pl.BlockSpec((1, tc), lambda j: (0, j + nj))])


def _conv_gate_fwd(a_pre, cw, cb, name):
    s = a_pre.shape[1]
    tc, nj, a_spec, w_specs, b_specs = _conv_specs(s)

    def body(a_ref, wg_ref, wu_ref, bg_ref, bu_ref, act_ref):
        rows = lax.broadcasted_iota(jnp.int32, (s, tc), 0)

        def conv(a, w_ref, b_ref):
            prev, nxt = _shift_rows(a, rows)
            return b_ref[...] + prev * w_ref[0:1, :] + a * w_ref[1:2, :] + nxt * w_ref[2:3, :]

        g = conv(a_ref[0], wg_ref, bg_ref)
        u = conv(a_ref[1], wu_ref, bu_ref)
        act_ref[...] = (g * (1.0 / (1.0 + jnp.exp(-g))) * u).astype(BF16)

    return _ordered_call(
        body, name=name, out_shape=jax.ShapeDtypeStruct((s, D_FF), BF16), grid=(nj,),
        in_specs=[a_spec] + w_specs + b_specs, out_specs=pl.BlockSpec((s, tc), lambda j: (0, j)),
        compiler_params=_params(("parallel",)),
    )(a_pre, cw, cw, cb, cb)


def _conv_gate_bwd(a_pre, cw, cb, dact, name):
    s = a_pre.shape[1]
    tc, nj, a_spec, w_specs, b_specs = _conv_specs(s)

    def body(a_ref, wg_ref, wu_ref, bg_ref, bu_ref, dact_ref, dap_ref, dcw_ref, dcb_ref):
        rows = lax.broadcasted_iota(jnp.int32, (s, tc), 0)
        shifted = []
        pre = []
        for part, (w_ref, b_ref) in enumerate(((wg_ref, bg_ref), (wu_ref, bu_ref))):
            a = a_ref[part]
            prev, nxt = _shift_rows(a, rows)
            shifted.append((prev, a, nxt))
            pre.append(b_ref[...] + prev * w_ref[0:1, :] + a * w_ref[1:2, :] + nxt * w_ref[2:3, :])
        g, u = pre
        sg = 1.0 / (1.0 + jnp.exp(-g))
        dact_v = dact_ref[...]
        das = (dact_v * u * (sg * (1.0 + g * (1.0 - sg))), dact_v * (g * sg))
        for part, w_ref in enumerate((wg_ref, wu_ref)):
            da = das[part]
            prev, a, nxt = shifted[part]
            da_prev, da_next = _shift_rows(da, rows)
            dap_ref[part] = (da_next * w_ref[0:1, :] + da * w_ref[1:2, :] + da_prev * w_ref[2:3, :]).astype(BF16)
            dcw_ref[part, 0:1, :] = _sum_rows(prev * da)
            dcw_ref[part, 1:2, :] = _sum_rows(a * da)
            dcw_ref[part, 2:3, :] = _sum_rows(nxt * da)
            dcb_ref[part] = _sum_rows(da)

    return _ordered_call(
        body, name=name,
        out_shape=(jax.ShapeDtypeStruct((2, s, D_FF), BF16), jax.ShapeDtypeStruct((2, 3, D_FF), F32),
                   jax.ShapeDtypeStruct((2, 1, D_FF), F32)),
        grid=(nj,),
        in_specs=[a_spec] + w_specs + b_specs + [pl.BlockSpec((s, tc), lambda j: (0, j))],
        out_specs=(pl.BlockSpec((2, s, tc), lambda j: (0, 0, j)), pl.BlockSpec((2, 3, tc), lambda j: (0, 0, j)),
                   pl.BlockSpec((2, 1, tc), lambda j: (0, 0, j))),
        compiler_params=_params(("parallel",)),
    )(a_pre, cw, cw, cb, cb, dact)


def _loss_head(y, target, name):
    s, d = y.shape
    tr = _row_tile(s)

    def body(y_ref, t_ref, loss_ref, dy_ref, dyb_ref):
        err = y_ref[...] - t_ref[...]

        @pl.when(pl.program_id(0) == 0)
        def _():
            loss_ref[...] = jnp.zeros_like(loss_ref)

        loss_ref[...] += jnp.broadcast_to(0.5 * _sum_all(_mean_last(err * err)), (8, 128))
        dy = err * (1.0 / d)
        dy_ref[...] = dy
        dyb_ref[...] = dy.astype(BF16)

    return _ordered_call(
        body, name=name,
        out_shape=(jax.ShapeDtypeStruct((8, 128), F32), jax.ShapeDtypeStruct((s, d), F32), jax.ShapeDtypeStruct((s, d), BF16)),
        grid=(s // tr,), in_specs=[_rows(d, tr), _rows(d, tr)],
        out_specs=(_const2((8, 128)), _rows(d, tr), _rows(d, tr)), compiler_params=_params(("arbitrary",)),
    )(y, target)


def _row_block(rows, cols, budget=1 << 20):
    if rows * cols <= budget:
        return rows
    best = None
    for tr in range(16, rows, 16):
        if rows % tr == 0 and tr * cols <= budget:
            best = tr
    assert best is not None, (rows, cols)
    return best


def _place_shard(x4, layer, j_arr, out_dtype, name):
    _, nh, r, cols = x4.shape
    tr = _row_block(r, cols)

    def body(j_ref, x_ref, o_ref):
        o_ref[...] = x_ref[...].astype(out_dtype)

    grid_spec = pltpu.PrefetchScalarGridSpec(
        num_scalar_prefetch=1, grid=(nh, r // tr),
        in_specs=[pl.BlockSpec((None, None, tr, cols), lambda h, i, j_ref: (layer, h, i, 0))],
        out_specs=pl.BlockSpec((None, None, tr, cols), lambda h, i, j_ref: (j_ref[0], h, i, 0)))
    return _ordered_call(
        body, name=name, out_shape=jax.ShapeDtypeStruct((N_CHIPS, nh, r, cols), out_dtype), grid_spec=grid_spec,
        compiler_params=_params(("parallel", "parallel")),
    )(j_arr, x4)


def _adamw(w, g, m, v, name, budget=1 << 18):
    rows, cols = w.shape
    tr = _row_block(rows, cols, budget)

    def body(w_ref, g_ref, m_ref, v_ref, go_ref, d_ref, nm_ref, nv_ref):
        gv = g_ref[...]
        go_ref[...] = gv
        mn = ADAM_B1 * m_ref[...] + (1.0 - ADAM_B1) * gv
        vn = ADAM_B2 * v_ref[...] + (1.0 - ADAM_B2) * (gv * gv)
        m_hat = mn / (1.0 - ADAM_B1 ** ADAM_STEP)
        v_hat = vn / (1.0 - ADAM_B2 ** ADAM_STEP)
        d_ref[...] = -ADAM_LR * (m_hat / (jnp.sqrt(v_hat) + ADAM_EPS) + ADAM_WD * w_ref[...])
        nm_ref[...] = mn
        nv_ref[...] = vn

    sds = jax.ShapeDtypeStruct((rows, cols), F32)
    return _ordered_call(
        body, name=name, out_shape=(sds, sds, sds, sds), grid=(rows // tr,),
        in_specs=[_rows(cols, tr)] * 4, out_specs=(_rows(cols, tr),) * 4, compiler_params=_params(("parallel",)),
    )(w, g, m, v)


def _pair_sum(g5, recv, c_arr, name):
    _, _, rh, cols = g5.shape
    tr = _row_block(rh, cols)

    def body(c_ref, g_ref, r_ref, o_ref):
        o_ref[...] = (g_ref[...].astype(F32) + r_ref[...].astype(F32)).astype(BF16)

    grid_spec = pltpu.PrefetchScalarGridSpec(
        num_scalar_prefetch=1, grid=(N_CHIPS, rh // tr),
        in_specs=[pl.BlockSpec((None, None, tr, cols), lambda j, i, c_ref: (j, c_ref[0], i, 0)),
                  pl.BlockSpec((None, tr, cols), lambda j, i, c_ref: (j, i, 0))],
        out_specs=pl.BlockSpec((None, tr, cols), lambda j, i, c_ref: (j, i, 0)))
    return _ordered_call(
        body, name=name, out_shape=jax.ShapeDtypeStruct((N_CHIPS, rh, cols), BF16), grid_spec=grid_spec,
        compiler_params=_params(("parallel", "parallel")),
    )(c_arr, g5, recv)


def _chip_sum(p4, recv3, j_arr, c_arr, name):
    _, rh, cols = p4.shape
    tr = _row_block(rh, cols, 1 << 19)

    def body(j_ref, c_ref, p_ref, r_ref, o_ref):
        total = p_ref[...].astype(F32)
        for peer in range(3):
            total = total + r_ref[peer].astype(F32)
        o_ref[...] = total

    grid_spec = pltpu.PrefetchScalarGridSpec(
        num_scalar_prefetch=2, grid=(rh // tr,),
        in_specs=[pl.BlockSpec((None, tr, cols), lambda i, j_ref, c_ref: (j_ref[0], i, 0)),
                  pl.BlockSpec((3, tr, cols), lambda i, j_ref, c_ref: (0, i, 0))],
        out_specs=pl.BlockSpec((None, tr, cols), lambda i, j_ref, c_ref: (c_ref[0], i, 0)))
    return _ordered_call(
        body, name=name, out_shape=jax.ShapeDtypeStruct((2, rh, cols), F32), grid_spec=grid_spec,
        compiler_params=_params(("parallel",)),
    )(j_arr, c_arr, p4, recv3)


def _adamw_layer(w, g, m, v, layer, into, name):
    nl, rows, cols = w.shape
    tr = _row_block(rows, cols, 1 << 18)
    at_layer = pl.BlockSpec((None, tr, cols), lambda i: (layer, i, 0))

    def body(w_ref, g_ref, m_ref, v_ref, *rest):
        go_ref, d_ref, nm_ref, nv_ref = rest[-4:]
        gv = g_ref[...]
        go_ref[...] = gv
        mn = ADAM_B1 * m_ref[...] + (1.0 - ADAM_B1) * gv
        vn = ADAM_B2 * v_ref[...] + (1.0 - ADAM_B2) * (gv * gv)
        m_hat = mn / (1.0 - ADAM_B1 ** ADAM_STEP)
        v_hat = vn / (1.0 - ADAM_B2 ** ADAM_STEP)
        d_ref[...] = -ADAM_LR * (m_hat / (jnp.sqrt(v_hat) + ADAM_EPS) + ADAM_WD * w_ref[...])
        nm_ref[...] = mn
        nv_ref[...] = vn

    in_specs = [at_layer, _rows(cols, tr), at_layer, at_layer]
    operands = [w, g, m, v]
    aliases = {}
    if into is not None:
        in_specs += [ANY] * 4
        operands += list(into)
        aliases = {4 + i: i for i in range(4)}
    sds = jax.ShapeDtypeStruct((nl, rows, cols), F32)
    return _ordered_call(
        body, name=name, out_shape=(sds,) * 4, grid=(rows // tr,), in_specs=in_specs, out_specs=(at_layer,) * 4,
        input_output_aliases=aliases, compiler_params=_params(("parallel",)),
    )(*operands)


def _sum_devices(mine, landed, me_arr, name):
    rows, lanes = mine.shape

    def body(me_ref, mine_ref, landed_ref, o_ref):
        total = None
        for dev in range(8):
            part = jnp.where(me_ref[0] == dev, mine_ref[...], landed_ref[dev])
            total = part if total is None else total + part
        o_ref[...] = total

    grid_spec = pltpu.PrefetchScalarGridSpec(
        num_scalar_prefetch=1, grid=(1,),
        in_specs=[pl.BlockSpec((rows, lanes), lambda i, me_ref: (0, 0)), pl.BlockSpec((8, rows, lanes), lambda i, me_ref: (0, 0, 0))],
        out_specs=pl.BlockSpec((rows, lanes), lambda i, me_ref: (0, 0)))
    return _ordered_call(
        body, name=name, out_shape=jax.ShapeDtypeStruct((rows, lanes), F32), grid_spec=grid_spec,
        compiler_params=_params(("arbitrary",)),
    )(me_arr, mine, landed)


def _place():
    x, y, c = lax.axis_index("x"), lax.axis_index("y"), lax.axis_index("c")
    chips = [(1 - x, y), (x, 1 - y), (1 - x, 1 - y)]
    return x, y, c, chips


HBM = pl.BlockSpec(memory_space=pltpu.HBM)
SEM = pl.BlockSpec(memory_space=pltpu.SEMAPHORE)
TOKEN = jax.ShapeDtypeStruct((8, 128), F32)


def _remote(src, dst, send_sem, recv_sem, to):
    return pltpu.make_async_remote_copy(src_ref=src, dst_ref=dst, send_sem=send_sem, recv_sem=recv_sem, device_id=to,
                                        device_id_type=MESH)


def _split_call(body, name, thru, sems_in=(), fresh=(), new_sems=(), after_last=True):
    n_t, n_s, n_f = len(thru), len(sems_in), len(fresh)

    def call_body(*refs):
        outs = refs[n_t + n_s:]
        body(refs[:n_t], refs[n_t:n_t + n_s], outs[1 + n_t:1 + n_t + n_f], outs[1 + n_t + n_f:])
        outs[0][...] = jnp.zeros_like(outs[0])

    out_shape = ([TOKEN] + [pltpu.HBM(t.shape, t.dtype) for t in thru] + [pltpu.HBM(shp, dt) for shp, dt in fresh]
                 + [pltpu.SemaphoreType.DMA(shp) for shp in new_sems])
    out_specs = [pl.BlockSpec(memory_space=pltpu.VMEM)] + [HBM] * (n_t + n_f) + [SEM] * len(new_sems)
    if not after_last:
        _Order.last = None
    out = _ordered_call(
        call_body, name=name, out_shape=tuple(out_shape), in_specs=[HBM] * n_t + [SEM] * n_s, out_specs=tuple(out_specs),
        input_output_aliases={i: 1 + i for i in range(n_t)},
        compiler_params=pltpu.CompilerParams(has_side_effects=pltpu.SideEffectType.DATAFLOW_SIDE_EFFECTING),
    )(*[pltpu.with_memory_space_constraint(t, pltpu.HBM) for t in thru], *sems_in)
    return out[1:1 + n_t], out[1 + n_t:1 + n_t + n_f], out[1 + n_t + n_f:]


class _Exchange:
    def __init__(self, weights, m_in, v_in, j_arr, c_arr, me_arr):
        self.w, self.m, self.v = weights, m_in, v_in
        self.j_arr, self.c_arr, self.me_arr = j_arr, c_arr, me_arr
        self.adam, self.small = {}, {}
        self.groups = [(l, name) for l in range(DEPTH) for name in BIG_NAMES]
        self.shard_shape = {name: weights[name].shape[1:] for name in BIG_NAMES}
        self.conv_state, self.state = [], {}
        self.ready, self.conv_ready = {}, {}
        self.pending, self.tick, self.reduced = [], 0, {}

        def place(grp):
            l, name = grp
            nl, r, cols = weights[name].shape
            return _place_shard(weights[name].reshape(nl, 2, r // 2, cols), l, j_arr, BF16, f"place_{name}_l{l}")

        def start_copies(tag, convs, groups, bufs):
            n_c = len(convs)

            def start(thru, _, __, sems):
                x, y, c, chips = _place()
                j_me = 2 * x + y
                copies = []
                for i in range(len(thru)):
                    mine = thru[i].at[j_me] if i < n_c else thru[i].at[j_me, c]
                    copies += [_remote(mine, mine, sems[2 * i].at[k], sems[2 * i + 1].at[k], (*chip, c))
                               for k, chip in enumerate(chips)]
                for cp in copies:
                    cp.start()

            thru, _, sems = _split_call(start, tag, convs + bufs, new_sems=[(3,)] * (2 * (n_c + len(bufs))))
            self.conv_state += [(thru[i], sems[2 * i], sems[2 * i + 1]) for i in range(n_c)]
            for g, grp in enumerate(groups):
                self.state[grp] = (thru[n_c + g], sems[2 * (n_c + g)], sems[2 * (n_c + g) + 1])

        convs = [_place_shard(weights["conv_w"][:, None], l, j_arr, F32, f"place_conv_w_l{l}") for l in range(DEPTH)]
        start_copies("gather_start_first", convs, self.groups[:1], [place(self.groups[0])])
        start_copies("gather_start_rest", [], self.groups[1:], [place(grp) for grp in self.groups[1:]])

    def conv_w(self, l):
        if l not in self.conv_ready:
            buf, send, recv = self.conv_state[l]

            def wait(thru, sems, _, __):
                x, y, c, chips = _place()
                for k, chip in enumerate(chips):
                    mine, theirs = thru[0].at[2 * x + y], thru[0].at[2 * chip[0] + chip[1]]
                    _remote(mine, mine, sems[0].at[k], sems[1].at[k], (*chip, c)).wait_send()
                    _remote(theirs, theirs, sems[0].at[k], sems[1].at[k], (x, y, c)).wait_recv()

            (buf,), _, _ = _split_call(wait, f"gather_conv_w_l{l}", [buf], sems_in=[send, recv])
            self.conv_ready[l] = jnp.transpose(buf[:, 0], (1, 0, 2)).reshape(3, 2 * D_FF)
        return self.conv_ready[l]

    def weight(self, l, name):
        grp = (l, name)
        if grp not in self.ready:
            buf, send, recv = self.state[grp]

            def forward(thru, sems, _, new):
                x, y, c, chips = _place()
                for k, chip in enumerate(chips):
                    landed = thru[0].at[2 * chip[0] + chip[1], c]
                    _remote(landed, landed, new[0].at[k], sems[0].at[k], (x, y, c)).wait_recv()
                    _remote(landed, landed, new[0].at[k], new[1].at[k], (x, y, 1 - c)).start()

            (buf,), _, (fsend, frecv) = _split_call(forward, f"gather_pass_{name}_l{l}", [buf], sems_in=[recv],
                                                    new_sems=[(3,), (3,)])

            def finish(thru, sems, _, __):
                x, y, c, chips = _place()
                mine = thru[0].at[2 * x + y, c]
                for k, chip in enumerate(chips):
                    j_k = 2 * chip[0] + chip[1]
                    theirs, landed = thru[0].at[j_k, 1 - c], thru[0].at[j_k, c]
                    _remote(theirs, theirs, sems[1].at[k], sems[2].at[k], (x, y, c)).wait_recv()
                    _remote(landed, landed, sems[1].at[k], sems[2].at[k], (x, y, 1 - c)).wait_send()
                    _remote(mine, mine, sems[0].at[k], sems[2].at[k], (*chip, c)).wait_send()

            (buf,), _, _ = _split_call(finish, f"gather_done_{name}_l{l}", [buf], sems_in=[send, fsend, frecv])
            r, cols = self.shard_shape[name]
            self.ready[grp] = buf.reshape(N_CHIPS, r, cols) if name in ("w_in", "w_up") else buf.reshape(N_CHIPS * r, cols)
        return self.ready[grp]

    def grad(self, l, name, g):
        r, cols = self.shard_shape[name]
        g5 = g.reshape(N_CHIPS, 2, r // 2, cols)

        def start(thru, _, fresh, sems):
            x, y, c, _chips = _place()
            _remote(thru[0].at[:, 1 - c], fresh[0], sems[0], sems[1], (x, y, 1 - c)).start()

        (g5,), (recv,), sems = _split_call(start, f"pair_start_{name}_l{l}", [g5], fresh=[((N_CHIPS, r // 2, cols), BF16)],
                                          new_sems=[(), ()], after_last=False)
        self.pending.append(dict(l=l, name=name, stage=1, at=self.tick, bufs=(g5, recv), sems=sems))

    def _pair(self, grp):
        l, name = grp["l"], grp["name"]
        r, cols = self.shard_shape[name]

        def wait(thru, sems, _, __):
            x, y, c, _chips = _place()
            cp = _remote(thru[0].at[:, 1 - c], thru[1], sems[0], sems[1], (x, y, 1 - c))
            cp.wait_send()
            cp.wait_recv()

        (g5, recv), _, _ = _split_call(wait, f"pair_done_{name}_l{l}", list(grp["bufs"]), sems_in=list(grp["sems"]))
        p4 = _pair_sum(g5, recv, self.c_arr, f"pair_sum_{name}_l{l}")

        def start(thru, _, fresh, sems):
            x, y, c, chips = _place()
            for k, chip in enumerate(chips):
                _remote(thru[0].at[2 * chip[0] + chip[1]], fresh[0].at[k], sems[0].at[k], sems[1].at[k], (*chip, c)).start()

        (p4,), (recv3,), sems = _split_call(start, f"chips_start_{name}_l{l}", [p4], fresh=[((3, r // 2, cols), BF16)],
                                           new_sems=[(3,), (3,)], after_last=False)
        grp.update(stage=2, at=self.tick, bufs=(p4, recv3), sems=sems)

    def _chips(self, grp):
        l, name = grp["l"], grp["name"]

        def wait(thru, sems, _, __):
            x, y, c, chips = _place()
            for k, chip in enumerate(chips):
                cp = _remote(thru[0].at[2 * chip[0] + chip[1]], thru[1].at[k], sems[0].at[k], sems[1].at[k], (*chip, c))
                cp.wait_send()
                cp.wait_recv()

        (p4, recv3), _, _ = _split_call(wait, f"chips_done_{name}_l{l}", list(grp["bufs"]), sems_in=list(grp["sems"]))
        half = _chip_sum(p4, recv3, self.j_arr, self.c_arr, f"chip_sum_{name}_l{l}")

        def start(thru, _, __, sems):
            x, y, c, _chips = _place()
            _remote(thru[0].at[c], thru[0].at[c], sems[0], sems[1], (x, y, 1 - c)).start()

        (half,), _, sems = _split_call(start, f"join_start_{name}_l{l}", [half], new_sems=[(), ()], after_last=False)
        grp.update(stage=3, at=self.tick, bufs=(half,), sems=sems)

    def _update(self, grp):
        l, name = grp["l"], grp["name"]

        def wait(thru, sems, _, __):
            x, y, c, _chips = _place()
            _remote(thru[0].at[c], thru[0].at[c], sems[0], sems[1], (x, y, 1 - c)).wait_send()
            _remote(thru[0].at[1 - c], thru[0].at[1 - c], sems[0], sems[1], (x, y, c)).wait_recv()

        (full,), _, _ = _split_call(wait, f"join_done_{name}_l{l}", list(grp["bufs"]), sems_in=list(grp["sems"]))
        self.adam[name] = _adamw_layer(self.w[name], full.reshape(self.shard_shape[name]), self.m[name], self.v[name], l,
                                       self.adam.get(name), f"adamw_{name}_l{l}")
        grp.update(stage=4)

    def point(self, drain=False):
        self.tick += 1
        for grp in self.pending:
            if grp["stage"] == 3 and (drain or grp["at"] < self.tick):
                self._update(grp)
            elif grp["stage"] == 2 and (drain or grp["at"] + 2 <= self.tick):
                self._chips(grp)
            elif grp["stage"] == 1 and (drain or grp["at"] < self.tick):
                self._pair(grp)

    def finish(self):
        while any(grp["stage"] < 4 for grp in self.pending):
            self.point(drain=True)
        return self.adam

    @staticmethod
    def _peer(k, x, y, c):
        return (1 - x if k & 4 else x, 1 - y if k & 2 else y, 1 - c if k & 1 else c)

    def small_grads(self, l, grads, loss_tile):
        parts = [grads[nm] for nm in SMALL_NAMES] + ([loss_tile[0, 0:1]] if loss_tile is not None else [])
        packed = _pack_call(parts, f"small_pack_l{l}")
        rows = packed.shape[0]

        def start(thru, _, fresh, sems):
            x, y, c, _chips = _place()
            for k in range(1, 8):
                _remote(thru[0], fresh[0].at[4 * x + 2 * y + c], sems[0].at[k - 1], sems[1].at[k - 1],
                        self._peer(k, x, y, c)).start()

        (packed,), (landed,), sems = _split_call(start, f"small_start_l{l}", [packed], fresh=[((8, rows, PACK_LANES), F32)],
                                                 new_sems=[(7,), (7,)], after_last=False)
        self.small[l] =(packed, landed, sems, [p.shape for p in parts])

    def small_sum(self, l):
        packed, landed, sems, shapes = self.small[l]

        def wait(thru, sems, _, __):
            x, y, c, _chips = _place()
            for k in range(1, 8):
                px, py, pc = self._peer(k, x, y, c)
                _remote(thru[0], thru[1].at[4 * x + 2 * y + c], sems[0].at[k - 1], sems[1].at[k - 1], (px, py, pc)).wait_send()
                _remote(thru[0], thru[1].at[4 * px + 2 * py + pc], sems[0].at[k - 1], sems[1].at[k - 1], (x, y, c)).wait_recv()

        (packed, landed), _, _ = _split_call(wait, f"small_done_l{l}", [packed, landed], sems_in=list(sems))
        return _unpack(_sum_devices(packed, landed, self.me_arr, f"small_sum_l{l}"), shapes)


def _rope_tables(s):
    inv_freq = ROPE_THETA ** (-jnp.arange(0, HEAD_DIM, 2, dtype=F32) / HEAD_DIM)
    ang = jnp.arange(s, dtype=F32)[:, None] * inv_freq[None, :]
    cos, sin = jnp.cos(ang), jnp.sin(ang)
    return jnp.concatenate([cos, cos], axis=-1), jnp.concatenate([-sin, sin], axis=-1)


def _local_step(x, target, ex, small):
    s = x.shape[0]
    cosf, sinf = _rope_tables(s)
    saved = []
    for l in range(DEPTH):
        p = small[l]
        t = f"l{l}"
        h = _rms_fwd(x, p["norm1_g"], f"norm1_{t}")
        z = _matmul(h, ex.weight(l, "w_in"), mode="nn", out_dtype=F32, tm=1024, tn=896, tk=2048, b_parts=4, name=f"proj_in_{t}")
        qn, kn, vb, ug, vn = _proj_post(z, p["q_norm_g"], p["k_norm_g"], p["sgu_ln_g"], p["sgu_ln_b"], cosf, sinf, f"proj_post_{t}")
        attn, sgu, mixed = _mixer_fwd(qn, kn, vb, ug, vn, p["w_s_bf16"], p["b_s_tile"], p["sink"], p["attn_out_g"],
                                      p["sgu_out_g"], f"mixer_{t}")
        x1 = _matmul(mixed, ex.weight(l, "w_o"), mode="nn", out_dtype=F32, tm=1024, tn=512, tk=2048, res=x, name=f"proj_out_{t}")
        h2 = _rms_fwd(x1, p["norm2_g"], f"norm2_{t}")
        a_pre = _matmul(h2, ex.weight(l, "w_up"), mode="nn", out_dtype=F32, tm=1024, tn=1408, tk=2048, b_parts=4, out_parts=2,
                        name=f"ffn_up_{t}")
        act = _conv_gate_fwd(a_pre, ex.conv_w(l), p["conv_b"], f"conv_gate_{t}")
        x2 = _matmul(act, ex.weight(l, "w_down"), mode="nn", out_dtype=F32, tm=512, tn=512, tk=D_FF, res=x1, name=f"ffn_down_{t}")
        saved.append(dict(x=x, h=h, z=z, qn=qn, kn=kn, vb=vb, ug=ug, vn=vn, attn=attn, sgu=sgu, mixed=mixed, x1=x1, h2=h2,
                          a_pre=a_pre, act=act))
        x = x2
    loss_tile, dx, dxb = _loss_head(x, target, "loss_head")
    for l in reversed(range(DEPTH)):
        p, sv = small[l], saved[l]
        t = f"l{l}"
        ex.grad(l, "w_down", _matmul(sv["act"], dxb, mode="tn", out_dtype=BF16, tm=512, tn=1024, tk=2048, name=f"g_w_down_{t}"))
        dact = _matmul(dxb, ex.weight(l, "w_down"), mode="nt", out_dtype=F32, tm=1024, tn=512, tk=2048, name=f"d_act_{t}")
        dap, dcw, dcb = _conv_gate_bwd(sv["a_pre"], ex.conv_w(l), p["conv_b"], dact, f"conv_gate_bwd_{t}")
        ex.point()
        ex.grad(l, "w_up", _matmul(sv["h2"], dap, mode="tn", out_dtype=BF16, tm=1024, tn=1408, tk=2048, b_parts=2, out_parts=4,
                                   name=f"g_w_up_{t}"))
        dh2 = _matmul_nt_slabs(dap, ex.weight(l, "w_up"), tm=512, tn=256, a_parts=2, name=f"d_h2_{t}")
        dx1, dx1b, dg2 = _rms_bwd(sv["x1"], p["norm2_g"], dh2, dx, f"norm2_bwd_{t}")
        ex.point()
        ex.grad(l, "w_o", _matmul(sv["mixed"], dx1b, mode="tn", out_dtype=BF16, tm=1024, tn=512, tk=2048, name=f"g_w_o_{t}"))
        dmixed = _matmul(dx1b, ex.weight(l, "w_o"), mode="nt", out_dtype=F32, tm=1024, tn=512, tk=2048, name=f"d_mixed_{t}")
        dqn, dkn, dvb, dug, dvn, dws, dbs, dsk, dga, dgs = _mixer_bwd(
            sv["qn"], sv["kn"], sv["vb"], sv["ug"], sv["vn"], sv["attn"], sv["sgu"], dmixed, p["w_s_bf16"], p["b_s_tile"],
            p["sink"], p["attn_out_g"], p["sgu_out_g"], f"mixer_bwd_{t}")
        dz, dqg, dkg, dlg, dlb = _proj_post_bwd(sv["z"], dqn, dkn, dvb, dug, dvn, p["q_norm_g"], p["k_norm_g"], p["sgu_ln_g"],
                                                 cosf, sinf, f"proj_post_bwd_{t}")
        ex.point()
        ex.grad(l, "w_in", _matmul(sv["h"], dz, mode="tn", out_dtype=BF16, tm=1024, tn=896, tk=2048, out_parts=4,
                                   name=f"g_w_in_{t}"))
        dh = _matmul_nt_slabs(dz, ex.weight(l, "w_in"), tm=1024, tn=512, name=f"d_h_{t}")
        dx, dxb, dg1 = _rms_bwd(sv["x"], p["norm1_g"], dh, dx1, f"norm1_bwd_{t}")
        ex.point()
        ex.small_grads(l, dict(
            norm1_g=dg1[0], q_norm_g=dqg[0], k_norm_g=dkg[0], sink=dsk[:, 0], sgu_ln_g=dlg[0], sgu_ln_b=dlb[0], w_s=dws,
            b_s=dbs[:, :, 0], attn_out_g=dga[0], sgu_out_g=dgs[0], norm2_g=dg2[0],
            conv_w=jnp.concatenate([dcw[0], dcw[1]], axis=-1), conv_b=jnp.concatenate([dcb[0, 0], dcb[1, 0]], axis=-1)),
            loss_tile if l == 0 else None)
    return dx


def _small_views(l, norm1_g, q_norm_g, k_norm_g, sink, sgu_ln_g, sgu_ln_b, w_s, b_s, attn_out_g, sgu_out_g, norm2_g, conv_b):
    return dict(
        norm1_g=norm1_g[l][None], q_norm_g=q_norm_g[l][None], k_norm_g=k_norm_g[l][None], sink=sink[l],
        sgu_ln_g=sgu_ln_g[l][None], sgu_ln_b=sgu_ln_b[l][None], w_s_bf16=w_s[l].astype(BF16),
        b_s_tile=jnp.broadcast_to(b_s[l][:, :, None], (N_GMLP_HEADS, BLOCK, BLOCK)), attn_out_g=attn_out_g[l][None],
        sgu_out_g=sgu_out_g[l][None], norm2_g=norm2_g[l][None], conv_b=conv_b[l][None])


SMALL_NAMES = ("norm1_g", "q_norm_g", "k_norm_g", "sink", "sgu_ln_g", "sgu_ln_b", "w_s", "b_s", "attn_out_g", "sgu_out_g",
               "norm2_g", "conv_w", "conv_b")
BIG_NAMES = ("w_in", "w_o", "w_up", "w_down")
PACK_LANES = 128
PACK_ALIGN = 8 * PACK_LANES


def _pack_rows(shape):
    return -(-math.prod(shape) // PACK_ALIGN) * 8


def _pack_parts(arrays):
    parts = []
    for a in arrays:
        flat = a.reshape(-1)
        parts.append(jnp.pad(flat, (0, _pack_rows(a.shape) * PACK_LANES - flat.shape[0])).reshape(-1, PACK_LANES))
    return parts


def _pack(arrays):
    return jnp.concatenate(_pack_parts(arrays), axis=0)


def _pack_call(arrays, name):
    parts = _pack_parts(arrays)
    total = sum(p.shape[0] for p in parts)

    def body(*refs):
        o_ref, at = refs[-1], 0
        for p_ref in refs[:-1]:
            o_ref[at:at + p_ref.shape[0], :] = p_ref[...]
            at += p_ref.shape[0]

    vm = pl.BlockSpec(memory_space=pltpu.VMEM)
    return _ordered_call(
        body, name=name, out_shape=jax.ShapeDtypeStruct((total, PACK_LANES), F32), in_specs=[vm] * len(parts), out_specs=vm,
        compiler_params=pltpu.CompilerParams(vmem_limit_bytes=V7X_VMEM_LIMIT),
    )(*parts)


def _unpack(packed, shapes):
    out, at = [], 0
    for shp in shapes:
        rows = _pack_rows(shp)
        out.append(packed[at:at + rows].reshape(-1)[:math.prod(shp)].reshape(shp))
        at += rows
    return out


def kernel(x, norm1_g, w_in, q_norm_g, k_norm_g, sink, sgu_ln_g, sgu_ln_b, w_s, b_s, attn_out_g, sgu_out_g, w_o, norm2_g, w_up, conv_w, conv_b, w_down, loss_target, m_norm1_g, m_w_in, m_q_norm_g, m_k_norm_g, m_sink, m_sgu_ln_g, m_sgu_ln_b, m_w_s, m_b_s, m_attn_out_g, m_sgu_out_g, m_w_o, m_norm2_g, m_w_up, m_conv_w, m_conv_b, m_w_down, v_norm1_g, v_w_in, v_q_norm_g, v_k_norm_g, v_sink, v_sgu_ln_g, v_sgu_ln_b, v_w_s, v_b_s, v_attn_out_g, v_sgu_out_g, v_w_o, v_norm2_g, v_w_up, v_conv_w, v_conv_b, v_w_down):
    weights = dict(norm1_g=norm1_g, w_in=w_in, q_norm_g=q_norm_g, k_norm_g=k_norm_g, sink=sink, sgu_ln_g=sgu_ln_g,
                   sgu_ln_b=sgu_ln_b, w_s=w_s, b_s=b_s, attn_out_g=attn_out_g, sgu_out_g=sgu_out_g, w_o=w_o, norm2_g=norm2_g,
                   w_up=w_up, conv_w=conv_w, conv_b=conv_b, w_down=w_down)
    m_in = dict(norm1_g=m_norm1_g, w_in=m_w_in, q_norm_g=m_q_norm_g, k_norm_g=m_k_norm_g, sink=m_sink, sgu_ln_g=m_sgu_ln_g,
                sgu_ln_b=m_sgu_ln_b, w_s=m_w_s, b_s=m_b_s, attn_out_g=m_attn_out_g, sgu_out_g=m_sgu_out_g, w_o=m_w_o,
                norm2_g=m_norm2_g, w_up=m_w_up, conv_w=m_conv_w, conv_b=m_conv_b, w_down=m_w_down)
    v_in = dict(norm1_g=v_norm1_g, w_in=v_w_in, q_norm_g=v_q_norm_g, k_norm_g=v_k_norm_g, sink=v_sink, sgu_ln_g=v_sgu_ln_g,
                sgu_ln_b=v_sgu_ln_b, w_s=v_w_s, b_s=v_b_s, attn_out_g=v_attn_out_g, sgu_out_g=v_sgu_out_g, w_o=v_w_o,
                norm2_g=v_norm2_g, w_up=v_w_up, conv_w=v_conv_w, conv_b=v_conv_b, w_down=v_w_down)
    cx, cy, cc = lax.axis_index("x"), lax.axis_index("y"), lax.axis_index("c")
    j_me = 2 * cx + cy
    c_arr = jnp.reshape(cc, (1,)).astype(jnp.int32)
    j_arr = jnp.reshape(j_me, (1,)).astype(jnp.int32)

    _Order.last = None
    ex = _Exchange(weights, m_in, v_in, j_arr, c_arr, jnp.reshape(4 * cx + 2 * cy + cc, (1,)).astype(jnp.int32))
    small = [_small_views(l, norm1_g, q_norm_g, k_norm_g, sink, sgu_ln_g, sgu_ln_b, w_s, b_s, attn_out_g, sgu_out_g, norm2_g,
                          conv_b) for l in range(DEPTH)]
    dx = _local_step(x[0], loss_target[0], ex, small)
    big_out = ex.finish()

    per_layer = [ex.small_sum(l) for l in range(DEPTH)]
    loss = per_layer[0][len(SMALL_NAMES)][0]
    small_full = {nm: jnp.stack([per_layer[l][i] for l in range(DEPTH)]) for i, nm in enumerate(SMALL_NAMES)}
    cw_cols = 2 * D_FF // N_CHIPS
    small_full["conv_w"] = lax.dynamic_slice_in_dim(small_full["conv_w"], j_me * cw_cols, cw_cols, axis=2)
    pw, pg, pm, pv = (_pack([src[nm] for nm in SMALL_NAMES]) for src in (weights, small_full, m_in, v_in))
    _, sd, sm, sv = _adamw(pw, pg, pm, pv, "adamw_small", budget=1 << 20)
    shapes = [weights[nm].shape for nm in SMALL_NAMES]
    grads = dict(small_full)
    delta = dict(zip(SMALL_NAMES, _unpack(sd, shapes)))
    new_m = dict(zip(SMALL_NAMES, _unpack(sm, shapes)))
    new_v = dict(zip(SMALL_NAMES, _unpack(sv, shapes)))

    for name in BIG_NAMES:
        grads[name], delta[name], new_m[name], new_v[name] = big_out[name]

    order = ("norm1_g", "w_in", "q_norm_g", "k_norm_g", "sink", "sgu_ln_g", "sgu_ln_b", "w_s", "b_s", "attn_out_g", "sgu_out_g",
             "w_o", "norm2_g", "w_up", "conv_w", "conv_b", "w_down")
    return (loss, dx[None], *[grads[nm] for nm in order], *[delta[nm] for nm in order], *[new_m[nm] for nm in order],
            *[new_v[nm] for nm in order])
```

```python
import functools
import math

import jax
import jax.numpy as jnp
from jax import lax
from jax.experimental import pallas as pl
from jax.experimental.pallas import tpu as pltpu

F32 = jnp.float32
BF16 = jnp.bfloat16

D_MODEL = 2048
HEAD_DIM = 128
ATTN_WIDTH = 1024
N_Q_HEADS = 8
N_KV_HEADS = 2
GQA_GROUP = 4
KV_WIDTH = 256
GMLP_WIDTH = 1024
N_GMLP_HEADS = 8
BLOCK = 128
IN_WIDTH = 3584
D_FF = 5632
DEPTH = 2
EPS = 1e-6
MASK_VALUE = -1e30
ROPE_THETA = 10000.0
N_CHIPS = 4

ADAM_LR = 0.001
ADAM_B1 = 0.9
ADAM_B2 = 0.999
ADAM_EPS = 1e-08
ADAM_WD = 0.01
ADAM_STEP = 10

V7X_VMEM_LIMIT = 48 * 1024 * 1024
MESH = pl.DeviceIdType.MESH

_GELU_C = math.sqrt(2.0 / math.pi)
_GELU_A = 0.044715


def _params(sem=None):
    return pltpu.CompilerParams(dimension_semantics=sem, vmem_limit_bytes=V7X_VMEM_LIMIT)


ANY = pl.BlockSpec(memory_space=pl.ANY)


class _Order:
    last = None


def _ordered_call(body, *, token_index=0, **kw):
    def run(*operands):
        tok = _Order.last
        if tok is None or any(op is tok for op in operands):
            call = pl.pallas_call(body, **kw)
        else:
            n_in = len(operands)

            def ordered_body(*refs):
                return body(*refs[:n_in], *refs[n_in + 1:])

            kw2 = dict(kw)
            if "grid_spec" in kw2:
                gs = kw2["grid_spec"]
                kw2["grid_spec"] = pltpu.PrefetchScalarGridSpec(
                    num_scalar_prefetch=gs.num_scalar_prefetch, grid=gs.grid, in_specs=list(gs.in_specs) + [ANY],
                    out_specs=gs.out_specs, scratch_shapes=gs.scratch_shapes)
            else:
                kw2["in_specs"] = list(kw2["in_specs"]) + [ANY]
            call = pl.pallas_call(ordered_body, **kw2)
            operands = operands + (tok,)
        out = call(*operands)
        _Order.last = out[token_index] if isinstance(out, (tuple, list)) else out
        return out

    return run


def _gelu(x):
    return x * (0.5 * (1.0 + jnp.tanh(_GELU_C * (x + _GELU_A * (x * x * x)))))


def _gelu_grad(x):
    x2 = x * x
    t = jnp.tanh(_GELU_C * (x + _GELU_A * (x * x2)))
    return 0.5 * (1.0 + t) + 0.5 * x * (1.0 - t * t) * (_GELU_C * (1.0 + 3.0 * _GELU_A * x2))


def _mean_last(x):
    return jnp.mean(x, axis=-1, keepdims=True)


def _sum_rows(x):
    return jnp.sum(x, axis=0, keepdims=True)


def _sum_all(x):
    return jnp.sum(jnp.sum(x, axis=1, keepdims=True), axis=0, keepdims=True)


def _matmul(a, b, *, mode, out_dtype, tm, tn, tk, name, res=None, a_parts=0, b_parts=0, out_parts=0, b_lead=()):
    b_full = b
    b = jax.ShapeDtypeStruct(b.shape[len(b_lead):], b.dtype)
    if mode == "nn":
        assert not a_parts
        m, k = a.shape
        n = b.shape[0] * b.shape[2] if b_parts else b.shape[1]
    elif mode == "nt":
        m, k = (a.shape[1], a.shape[0] * a.shape[2]) if a_parts else a.shape
        n = b.shape[1] if b_parts else b.shape[0]
    else:
        assert not a_parts
        k, m = a.shape
        n = b.shape[0] * b.shape[2] if b_parts else b.shape[1]
    tm, tn, tk = min(tm, m), min(tn, n), min(tk, k)
    assert m % tm == 0 and n % tn == 0 and k % tk == 0, (name, m, n, k, tm, tn, tk)
    nm, nn, nk = m // tm, n // tn, k // tk

    def slab(idx, total_tiles, parts):
        per = total_tiles // parts
        assert per * parts == total_tiles, (name, total_tiles, parts)
        return idx // per, idx % per

    if mode == "nn":
        a_spec = pl.BlockSpec((tm, tk), lambda i, j, kk: (i, kk))
        if b_parts:
            b_spec = pl.BlockSpec((None, tk, tn), lambda i, j, kk: (slab(j, nn, b_parts)[0], kk, slab(j, nn, b_parts)[1]))
        else:
            b_spec = pl.BlockSpec((tk, tn), lambda i, j, kk: (kk, j))
        dims = (((1,), (0,)), ((), ()))
    elif mode == "nt":
        if a_parts:
            a_spec = pl.BlockSpec((None, tm, tk), lambda i, j, kk: (slab(kk, nk, a_parts)[0], i, slab(kk, nk, a_parts)[1]))
        else:
            a_spec = pl.BlockSpec((tm, tk), lambda i, j, kk: (i, kk))
        if b_parts:
            b_spec = pl.BlockSpec((None, tn, tk), lambda i, j, kk: (slab(kk, nk, b_parts)[0], j, slab(kk, nk, b_parts)[1]))
        else:
            b_spec = pl.BlockSpec((tn, tk), lambda i, j, kk: (j, kk))
        dims = (((1,), (1,)), ((), ()))
    else:
        a_spec = pl.BlockSpec((tk, tm), lambda i, j, kk: (kk, i))
        if b_parts:
            b_spec = pl.BlockSpec((None, tk, tn), lambda i, j, kk: (slab(j, nn, b_parts)[0], kk, slab(j, nn, b_parts)[1]))
        else:
            b_spec = pl.BlockSpec((tk, tn), lambda i, j, kk: (kk, j))
        dims = (((0,), (0,)), ((), ()))
    if out_parts:
        out_shape = jax.ShapeDtypeStruct((out_parts, m, n // out_parts), out_dtype)
        out_spec = pl.BlockSpec((None, tm, tn), lambda i, j, kk: (slab(j, nn, out_parts)[0], i, slab(j, nn, out_parts)[1]))
    else:
        out_shape = jax.ShapeDtypeStruct((m, n), out_dtype)
        out_spec = pl.BlockSpec((tm, tn), lambda i, j, kk: (i, j))
    if b_lead:
        inner_map = b_spec.index_map
        b_spec = pl.BlockSpec((None,) * len(b_lead) + tuple(b_spec.block_shape),
                              lambda i, j, kk: tuple(b_lead) + tuple(inner_map(i, j, kk)))
    in_specs = [a_spec, b_spec]
    operands = [a, b_full]
    if res is not None:
        in_specs.append(pl.BlockSpec((tm, tn), lambda i, j, kk: (i, j)))
        operands.append(res)

    def body(*refs):
        a_ref, b_ref = refs[0], refs[1]
        res_ref = refs[2] if res is not None else None
        o_ref = refs[3] if res is not None else refs[2]
        p = lax.dot_general(a_ref[...], b_ref[...], dims, preferred_element_type=F32)

        def finish(total):
            if res_ref is not None:
                total = res_ref[...] + total
            o_ref[...] = total.astype(out_dtype)

        if nk == 1:
            finish(p)
        else:
            acc_ref = refs[-1]
            kk = pl.program_id(2)

            @pl.when(kk == 0)
            def _():
                acc_ref[...] = p

            @pl.when(jnp.logical_and(kk > 0, kk < nk - 1))
            def _():
                acc_ref[...] += p

            @pl.when(kk == nk - 1)
            def _():
                finish(acc_ref[...] + p)

    scratch = [pltpu.VMEM((tm, tn), F32)] if nk > 1 else []
    return _ordered_call(
        body, name=name, out_shape=out_shape, grid=(nm, nn, nk), in_specs=in_specs, out_specs=out_spec,
        scratch_shapes=scratch, compiler_params=_params(("parallel", "parallel", "arbitrary")),
    )(*operands)


def _matmul_nt_slabs(a, b, *, tm, tn, name, a_parts=0):
    nslab, n, ks = b.shape
    m = a.shape[1] if a_parts else a.shape[0]
    tm, tn = min(tm, m), min(tn, n)
    assert m % tm == 0 and n % tn == 0, (name, m, n, tm, tn)
    if a_parts:
        per = nslab // a_parts
        assert per * a_parts == nslab and a.shape[2] == per * ks, (name, a.shape, b.shape)
        a_spec = pl.BlockSpec((a_parts, tm, per * ks), lambda i, j: (0, i, 0))
    else:
        assert a.shape[1] == nslab * ks, (name, a.shape, b.shape)
        a_spec = pl.BlockSpec((tm, nslab * ks), lambda i, j: (i, 0))

    def body(a_ref, b_ref, o_ref):
        total = None
        for sl in range(nslab):
            if a_parts:
                a_sl = a_ref[sl // per, :, (sl % per) * ks:(sl % per + 1) * ks]
            else:
                a_sl = a_ref[:, sl * ks:(sl + 1) * ks]
            p = lax.dot_general(a_sl, b_ref[sl], (((1,), (1,)), ((), ())), preferred_element_type=F32)
            total = p if total is None else total + p
        o_ref[...] = total

    return _ordered_call(
        body, name=name, out_shape=jax.ShapeDtypeStruct((m, n), F32), grid=(m // tm, n // tn),
        in_specs=[a_spec, pl.BlockSpec((nslab, tn, ks), lambda i, j: (0, j, 0))],
        out_specs=pl.BlockSpec((tm, tn), lambda i, j: (i, j)), compiler_params=_params(("parallel", "parallel")),
    )(a, b)


def _row_tile(s):
    return min(256, s)


def _rows(width, tr):
    return pl.BlockSpec((tr, width), lambda i: (i, 0))


def _const2(shape):
    return pl.BlockSpec(shape, lambda i: (0, 0))


def _rms_fwd(x, g, name):
    s, d = x.shape
    tr = _row_tile(s)

    def body(x_ref, g_ref, o_ref):
        xv = x_ref[...]
        r = lax.rsqrt(_mean_last(xv * xv) + EPS)
        o_ref[...] = (xv * r * g_ref[...]).astype(BF16)

    return _ordered_call(
        body, name=name, out_shape=jax.ShapeDtypeStruct((s, d), BF16), grid=(s // tr,),
        in_specs=[_rows(d, tr), _const2((1, d))], out_specs=_rows(d, tr), compiler_params=_params(("parallel",)),
    )(x, g)


def _rms_bwd(x, g, dh, dres, name):
    s, d = x.shape
    tr = _row_tile(s)

    def body(x_ref, g_ref, dh_ref, dres_ref, dx_ref, dxb_ref, dg_ref):
        xv, dy = x_ref[...], dh_ref[...]
        r = lax.rsqrt(_mean_last(xv * xv) + EPS)
        gdy = dy * g_ref[...]
        dx = dres_ref[...] + r * gdy - xv * ((r * r * r) * _mean_last(xv * gdy))
        dx_ref[...] = dx
        dxb_ref[...] = dx.astype(BF16)

        @pl.when(pl.program_id(0) == 0)
        def _():
            dg_ref[...] = jnp.zeros_like(dg_ref)

        dg_ref[...] += _sum_rows(xv * r * dy)

    return _ordered_call(
        body, name=name,
        out_shape=(jax.ShapeDtypeStruct((s, d), F32), jax.ShapeDtypeStruct((s, d), BF16), jax.ShapeDtypeStruct((1, d), F32)),
        grid=(s // tr,), in_specs=[_rows(d, tr), _const2((1, d)), _rows(d, tr), _rows(d, tr)],
        out_specs=(_rows(d, tr), _rows(d, tr), _const2((1, d))), compiler_params=_params(("arbitrary",)),
    )(x, g, dh, dres)


Q0, K0, V0, GU0, GV0 = 0, ATTN_WIDTH, ATTN_WIDTH + KV_WIDTH, ATTN_WIDTH + 2 * KV_WIDTH, ATTN_WIDTH + 2 * KV_WIDTH + GMLP_WIDTH


def _head(h, base=0):
    return slice(base + h * HEAD_DIM, base + (h + 1) * HEAD_DIM)


def _proj_post(z, qg, kg, lg, lb, cosf, sinf, name):
    s = z.shape[0]
    tr = _row_tile(s)

    def body(z_ref, qg_ref, kg_ref, lg_ref, lb_ref, cos_ref, sin_ref, qn_ref, kn_ref, vb_ref, ug_ref, vn_ref):
        cos, sin = cos_ref[...], sin_ref[...]

        def norm_rope(xh, g):
            y = xh * lax.rsqrt(_mean_last(xh * xh) + EPS) * g
            return y * cos + pltpu.roll(y, HEAD_DIM // 2, 1) * sin

        for h in range(N_Q_HEADS):
            qn_ref[:, _head(h)] = norm_rope(z_ref[:, _head(h, Q0)], qg_ref[...]).astype(BF16)
        for h in range(N_KV_HEADS):
            kn_ref[:, _head(h)] = norm_rope(z_ref[:, _head(h, K0)], kg_ref[...]).astype(BF16)
        vb_ref[...] = z_ref[:, V0:GU0].astype(BF16)
        ug_ref[...] = _gelu(z_ref[:, GU0:GV0])
        vg = _gelu(z_ref[:, GV0:IN_WIDTH])
        xc = vg - _mean_last(vg)
        y = xc * lax.rsqrt(_mean_last(xc * xc) + EPS)
        vn_ref[...] = (y * lg_ref[...] + lb_ref[...]).astype(BF16)

    return _ordered_call(
        body, name=name,
        out_shape=(jax.ShapeDtypeStruct((s, ATTN_WIDTH), BF16), jax.ShapeDtypeStruct((s, KV_WIDTH), BF16),
                   jax.ShapeDtypeStruct((s, KV_WIDTH), BF16), jax.ShapeDtypeStruct((s, GMLP_WIDTH), F32),
                   jax.ShapeDtypeStruct((s, GMLP_WIDTH), BF16)),
        grid=(s // tr,),
        in_specs=[_rows(IN_WIDTH, tr), _const2((1, HEAD_DIM)), _const2((1, HEAD_DIM)), _const2((1, GMLP_WIDTH)),
                  _const2((1, GMLP_WIDTH)), _rows(HEAD_DIM, tr), _rows(HEAD_DIM, tr)],
        out_specs=(_rows(ATTN_WIDTH, tr), _rows(KV_WIDTH, tr), _rows(KV_WIDTH, tr), _rows(GMLP_WIDTH, tr), _rows(GMLP_WIDTH, tr)),
        compiler_params=_params(("parallel",)),
    )(z, qg, kg, lg, lb, cosf, sinf)


def _proj_post_bwd(z, dqn, dkn, dvb, dug, dvn, qg, kg, lg, cosf, sinf, name):
    s = z.shape[0]
    tr = _row_tile(s)

    def body(z_ref, dqn_ref, dkn_ref, dvb_ref, dug_ref, dvn_ref, qg_ref, kg_ref, lg_ref, cos_ref, sin_ref,
             dz_ref, dqg_ref, dkg_ref, dlg_ref, dlb_ref):
        cos, sin = cos_ref[...], sin_ref[...]

        @pl.when(pl.program_id(0) == 0)
        def _():
            dqg_ref[...] = jnp.zeros_like(dqg_ref)
            dkg_ref[...] = jnp.zeros_like(dkg_ref)
            dlg_ref[...] = jnp.zeros_like(dlg_ref)
            dlb_ref[...] = jnp.zeros_like(dlb_ref)

        def norm_rope_bwd(xh, g, dout):
            dy = dout * cos - pltpu.roll(dout, HEAD_DIM // 2, 1) * sin
            r = lax.rsqrt(_mean_last(xh * xh) + EPS)
            xhat = xh * r
            gdy = dy * g
            return r * (gdy - xhat * _mean_last(xhat * gdy)), _sum_rows(xhat * dy)

        dqg = jnp.zeros((1, HEAD_DIM), F32)
        for h in range(N_Q_HEADS):
            dx, dg = norm_rope_bwd(z_ref[:, _head(h, Q0)], qg_ref[...], dqn_ref[:, _head(h)])
            dz_ref[:, _head(h, Q0)] = dx.astype(BF16)
            dqg = dqg + dg
        dqg_ref[...] += dqg
        dkg = jnp.zeros((1, HEAD_DIM), F32)
        for h in range(N_KV_HEADS):
            dx, dg = norm_rope_bwd(z_ref[:, _head(h, K0)], kg_ref[...], dkn_ref[:, _head(h)])
            dz_ref[:, _head(h, K0)] = dx.astype(BF16)
            dkg = dkg + dg
        dkg_ref[...] += dkg
        dz_ref[:, V0:GU0] = dvb_ref[...].astype(BF16)
        dz_ref[:, GU0:GV0] = (dug_ref[...] * _gelu_grad(z_ref[:, GU0:GV0])).astype(BF16)
        gv = z_ref[:, GV0:IN_WIDTH]
        vg = _gelu(gv)
        xc = vg - _mean_last(vg)
        r = lax.rsqrt(_mean_last(xc * xc) + EPS)
        xhat = xc * r
        dvn_v = dvn_ref[...]
        dlg_ref[...] += _sum_rows(xhat * dvn_v)
        dlb_ref[...] += _sum_rows(dvn_v)
        dxh = dvn_v * lg_ref[...]
        dvg = r * (dxh - _mean_last(dxh) - xhat * _mean_last(dxh * xhat))
        dz_ref[:, GV0:IN_WIDTH] = (dvg * _gelu_grad(gv)).astype(BF16)

    return _ordered_call(
        body, name=name,
        out_shape=(jax.ShapeDtypeStruct((s, IN_WIDTH), BF16), jax.ShapeDtypeStruct((1, HEAD_DIM), F32),
                   jax.ShapeDtypeStruct((1, HEAD_DIM), F32), jax.ShapeDtypeStruct((1, GMLP_WIDTH), F32),
                   jax.ShapeDtypeStruct((1, GMLP_WIDTH), F32)),
        grid=(s // tr,),
        in_specs=[_rows(IN_WIDTH, tr), _rows(ATTN_WIDTH, tr), _rows(KV_WIDTH, tr), _rows(KV_WIDTH, tr), _rows(GMLP_WIDTH, tr),
                  _rows(GMLP_WIDTH, tr), _const2((1, HEAD_DIM)), _const2((1, HEAD_DIM)), _const2((1, GMLP_WIDTH)),
                  _rows(HEAD_DIM, tr), _rows(HEAD_DIM, tr)],
        out_specs=(_rows(IN_WIDTH, tr), _const2((1, HEAD_DIM)), _const2((1, HEAD_DIM)), _const2((1, GMLP_WIDTH)),
                   _const2((1, GMLP_WIDTH))),
        compiler_params=_params(("arbitrary",)),
    )(z, dqn, dkn, dvb, dug, dvn, qg, kg, lg, cosf, sinf)


def _band_valid(n, s):
    i = lax.broadcasted_iota(jnp.int32, (BLOCK, 3 * BLOCK), 0)
    j = lax.broadcasted_iota(jnp.int32, (BLOCK, 3 * BLOCK), 1)
    k_pos = n * BLOCK - BLOCK + j
    return (jnp.abs(j - BLOCK - i) <= BLOCK) & (k_pos >= 0) & (k_pos < s)


def _probs(q, kb, sink_h, valid):
    sc = lax.dot_general(q, kb, (((1,), (1,)), ((), ())), preferred_element_type=F32) * (HEAD_DIM ** -0.5)
    sc = jnp.where(valid, sc, MASK_VALUE)
    m = jnp.maximum(jnp.max(sc, axis=-1, keepdims=True), sink_h)
    p = jnp.exp(sc - m)
    es = jnp.exp(sink_h - m)
    den = jnp.sum(p, axis=-1, keepdims=True) + es
    return p / den, es / den


def _band_specs(width, nb):
    return [pl.BlockSpec((BLOCK, width), lambda n: (jnp.maximum(n - 1, 0), 0)),
            pl.BlockSpec((BLOCK, width), lambda n: (n, 0)),
            pl.BlockSpec((BLOCK, width), lambda n: (jnp.minimum(n + 1, nb - 1), 0))]


def _blk(width):
    return pl.BlockSpec((BLOCK, width), lambda n: (n, 0))


def _whole3(shape):
    return pl.BlockSpec(shape, lambda n: (0, 0, 0))


def _smem():
    return pl.BlockSpec(memory_space=pltpu.SMEM)


def _mixer_fwd(qn, kn, vb, ug, vn, wsb, bsb, sink, ga, gs, name):
    s = qn.shape[0]
    nb = s // BLOCK

    def body(sink_ref, q_ref, kp_ref, kc_ref, kx_ref, vp_ref, vc_ref, vx_ref, ug_ref, vn_ref, ws_ref, bs_ref, ga_ref, gs_ref,
             attn_ref, sgu_ref, mix_ref):
        n = pl.program_id(0)
        valid = _band_valid(n, s)
        ssq = jnp.zeros((BLOCK, 1), F32)
        for kh in range(N_KV_HEADS):
            kb = jnp.concatenate([kp_ref[:, _head(kh)], kc_ref[:, _head(kh)], kx_ref[:, _head(kh)]], axis=0)
            vbd = jnp.concatenate([vp_ref[:, _head(kh)], vc_ref[:, _head(kh)], vx_ref[:, _head(kh)]], axis=0)
            for g in range(GQA_GROUP):
                h = kh * GQA_GROUP + g
                p, _ = _probs(q_ref[:, _head(h)], kb, sink_ref[h], valid)
                o = jnp.dot(p.astype(BF16), vbd, preferred_element_type=F32)
                attn_ref[:, _head(h)] = o
                ssq = ssq + jnp.sum(o * o, axis=-1, keepdims=True)
        r = lax.rsqrt(ssq * (1.0 / ATTN_WIDTH) + EPS)
        mix_ref[:, 0:ATTN_WIDTH] = (attn_ref[...] * r * ga_ref[...]).astype(BF16)
        ssq = jnp.zeros((BLOCK, 1), F32)
        for h in range(N_GMLP_HEADS):
            f = jnp.dot(ws_ref[h], vn_ref[:, _head(h)], preferred_element_type=F32) + bs_ref[h]
            o = ug_ref[:, _head(h)] * f
            sgu_ref[:, _head(h)] = o
            ssq = ssq + jnp.sum(o * o, axis=-1, keepdims=True)
        r = lax.rsqrt(ssq * (1.0 / GMLP_WIDTH) + EPS)
        mix_ref[:, ATTN_WIDTH:D_MODEL] = (sgu_ref[...] * r * gs_ref[...]).astype(BF16)

    hh = (N_GMLP_HEADS, BLOCK, BLOCK)
    return _ordered_call(
        body, name=name,
        out_shape=(jax.ShapeDtypeStruct((s, ATTN_WIDTH), F32), jax.ShapeDtypeStruct((s, GMLP_WIDTH), F32),
                   jax.ShapeDtypeStruct((s, D_MODEL), BF16)),
        grid=(nb,),
        in_specs=[_smem(), _blk(ATTN_WIDTH)] + _band_specs(KV_WIDTH, nb) + _band_specs(KV_WIDTH, nb)
        + [_blk(GMLP_WIDTH), _blk(GMLP_WIDTH), _whole3(hh), _whole3(hh),
           pl.BlockSpec((1, ATTN_WIDTH), lambda n: (0, 0)), pl.BlockSpec((1, GMLP_WIDTH), lambda n: (0, 0))],
        out_specs=(_blk(ATTN_WIDTH), _blk(GMLP_WIDTH), _blk(D_MODEL)),
        compiler_params=_params(("parallel",)),
    )(sink, qn, kn, kn, kn, vb, vb, vb, ug, vn, wsb, bsb, ga, gs)


def _mixer_bwd(qn, kn, vb, ug, vn, attn, sgu, dmixed, wsb, bsb, sink, ga, gs, name):
    s = qn.shape[0]
    nb = s // BLOCK
    tn_dims = (((0,), (0,)), ((), ()))
    nt_dims = (((1,), (1,)), ((), ()))

    def body(sink_ref, q_ref, kp_ref, kc_ref, kx_ref, vp_ref, vc_ref, vx_ref, ug_ref, vn_ref, attn_ref, sgu_ref, dm_ref,
             ws_ref, bs_ref, ga_ref, gs_ref,
             dq_ref, dk_ref, dv_ref, dug_ref, dvn_ref, dws_ref, dbs_ref, dsk_ref, dga_ref, dgs_ref, dk_acc, dv_acc):
        n = pl.program_id(0)

        @pl.when(n == 0)
        def _():
            for ref in (dk_acc, dv_acc, dws_ref, dbs_ref, dsk_ref, dga_ref, dgs_ref):
                ref[...] = jnp.zeros_like(ref)

        def out_norm_bwd(o, g, dy):
            r = lax.rsqrt(_mean_last(o * o) + EPS)
            gdy = dy * g
            return r * gdy - o * ((r * r * r) * _mean_last(o * gdy)), _sum_rows(o * r * dy)

        d_attn, dga = out_norm_bwd(attn_ref[...], ga_ref[...], dm_ref[:, 0:ATTN_WIDTH])
        dga_ref[...] += dga
        d_sgu, dgs = out_norm_bwd(sgu_ref[...], gs_ref[...], dm_ref[:, ATTN_WIDTH:D_MODEL])
        dgs_ref[...] += dgs

        for h in range(N_GMLP_HEADS):
            vn_h = vn_ref[:, _head(h)]
            f = jnp.dot(ws_ref[h], vn_h, preferred_element_type=F32) + bs_ref[h]
            ds_h = d_sgu[:, _head(h)]
            dug_ref[:, _head(h)] = ds_h * f
            df = ds_h * ug_ref[:, _head(h)]
            dfb = df.astype(BF16)
            dvn_ref[:, _head(h)] = lax.dot_general(ws_ref[h], dfb, tn_dims, preferred_element_type=F32)
            dws_ref[h] += lax.dot_general(dfb, vn_h, nt_dims, preferred_element_type=F32)
            dbs_ref[h] += jnp.broadcast_to(jnp.sum(df, axis=-1, keepdims=True), (BLOCK, BLOCK))

        valid = _band_valid(n, s)
        row0 = pl.multiple_of(n * BLOCK, BLOCK)
        for kh in range(N_KV_HEADS):
            kb = jnp.concatenate([kp_ref[:, _head(kh)], kc_ref[:, _head(kh)], kx_ref[:, _head(kh)]], axis=0)
            vbd = jnp.concatenate([vp_ref[:, _head(kh)], vc_ref[:, _head(kh)], vx_ref[:, _head(kh)]], axis=0)
            dkb = jnp.zeros((3 * BLOCK, HEAD_DIM), F32)
            dvb = jnp.zeros((3 * BLOCK, HEAD_DIM), F32)
            for g in range(GQA_GROUP):
                h = kh * GQA_GROUP + g
                q = q_ref[:, _head(h)]
                p, p_sink = _probs(q, kb, sink_ref[h], valid)
                do = d_attn[:, _head(h)].astype(BF16)
                dp = lax.dot_general(do, vbd, nt_dims, preferred_element_type=F32)
                delta = jnp.sum(p * dp, axis=-1, keepdims=True)
                dsc = (p * (dp - delta) * (HEAD_DIM ** -0.5)).astype(BF16)
                dsk_ref[h:h + 1, :] += jnp.broadcast_to(_sum_all(-(p_sink * delta)), (1, BLOCK))
                dq_ref[:, _head(h)] = jnp.dot(dsc, kb, preferred_element_type=F32)
                dkb = dkb + lax.dot_general(dsc, q, tn_dims, preferred_element_type=F32)
                dvb = dvb + lax.dot_general(p.astype(BF16), do, tn_dims, preferred_element_type=F32)
            dk_acc[pl.ds(row0, 3 * BLOCK), _head(kh)] += dkb
            dv_acc[pl.ds(row0, 3 * BLOCK), _head(kh)] += dvb

        @pl.when(n == nb - 1)
        def _():
            dk_ref[...] = dk_acc[BLOCK:BLOCK + s, :]
            dv_ref[...] = dv_acc[BLOCK:BLOCK + s, :]

    hh = (N_GMLP_HEADS, BLOCK, BLOCK)
    full_kv = pl.BlockSpec((s, KV_WIDTH), lambda n: (0, 0))
    return _ordered_call(
        body, name=name,
        out_shape=(jax.ShapeDtypeStruct((s, ATTN_WIDTH), F32), jax.ShapeDtypeStruct((s, KV_WIDTH), F32),
                   jax.ShapeDtypeStruct((s, KV_WIDTH), F32), jax.ShapeDtypeStruct((s, GMLP_WIDTH), F32),
                   jax.ShapeDtypeStruct((s, GMLP_WIDTH), F32), jax.ShapeDtypeStruct(hh, F32), jax.ShapeDtypeStruct(hh, F32),
                   jax.ShapeDtypeStruct((N_Q_HEADS, BLOCK), F32), jax.ShapeDtypeStruct((1, ATTN_WIDTH), F32),
                   jax.ShapeDtypeStruct((1, GMLP_WIDTH), F32)),
        grid=(nb,),
        in_specs=[_smem(), _blk(ATTN_WIDTH)] + _band_specs(KV_WIDTH, nb) + _band_specs(KV_WIDTH, nb)
        + [_blk(GMLP_WIDTH), _blk(GMLP_WIDTH), _blk(ATTN_WIDTH), _blk(GMLP_WIDTH), _blk(D_MODEL), _whole3(hh), _whole3(hh),
           pl.BlockSpec((1, ATTN_WIDTH), lambda n: (0, 0)), pl.BlockSpec((1, GMLP_WIDTH), lambda n: (0, 0))],
        out_specs=(_blk(ATTN_WIDTH), full_kv, full_kv, _blk(GMLP_WIDTH), _blk(GMLP_WIDTH), _whole3(hh), _whole3(hh),
                   pl.BlockSpec((N_Q_HEADS, BLOCK), lambda n: (0, 0)), pl.BlockSpec((1, ATTN_WIDTH), lambda n: (0, 0)),
                   pl.BlockSpec((1, GMLP_WIDTH), lambda n: (0, 0))),
        scratch_shapes=[pltpu.VMEM((s + 2 * BLOCK, KV_WIDTH), F32), pltpu.VMEM((s + 2 * BLOCK, KV_WIDTH), F32)],
        compiler_params=_params(("arbitrary",)),
    )(sink, qn, kn, kn, kn, vb, vb, vb, ug, vn, attn, sgu, dmixed, wsb, bsb, ga, gs)


CONV_TILE = 128


def _shift_rows(a, rows):
    s = a.shape[0]
    prev = jnp.where(rows == 0, 0.0, pltpu.roll(a, 1, 0))
    nxt = jnp.where(rows == s - 1, 0.0, pltpu.roll(a, s - 1, 0))
    return prev, nxt


def _conv_specs(s):
    tc = CONV_TILE
    nj = D_FF // tc
    return (tc, nj, pl.BlockSpec((2, s, tc), lambda j: (0, 0, j)),
            [pl.BlockSpec((3, tc), lambda j: (0, j)), pl.BlockSpec((3, tc), lambda j: (0, j + nj))],
            [pl.BlockSpec((1, tc), lambda j: (0, j)), pl.BlockSpec((1, tc), lambda j: (0, j + nj))])


def _conv_gate_fwd(a_pre, cw, cb, name):
    s = a_pre.shape[1]
    tc, nj, a_spec, w_specs, b_specs = _conv_specs(s)

    def body(a_ref, wg_ref, wu_ref, bg_ref, bu_ref, act_ref):
        rows = lax.broadcasted_iota(jnp.int32, (s, tc), 0)

        def conv(a, w_ref, b_ref):
            prev, nxt = _shift_rows(a, rows)
            return b_ref[...] + prev * w_ref[0:1, :] + a * w_ref[1:2, :] + nxt * w_ref[2:3, :]

        g = conv(a_ref[0], wg_ref, bg_ref)
        u = conv(a_ref[1], wu_ref, bu_ref)
        act_ref[...] = (g * (1.0 / (1.0 + jnp.exp(-g))) * u).astype(BF16)

    return _ordered_call(
        body, name=name, out_shape=jax.ShapeDtypeStruct((s, D_FF), BF16), grid=(nj,),
        in_specs=[a_spec] + w_specs + b_specs, out_specs=pl.BlockSpec((s, tc), lambda j: (0, j)),
        compiler_params=_params(("parallel",)),
    )(a_pre, cw, cw, cb, cb)


def _conv_gate_bwd(a_pre, cw, cb, dact, name):
    s = a_pre.shape[1]
    tc, nj, a_spec, w_specs, b_specs = _conv_specs(s)

    def body(a_ref, wg_ref, wu_ref, bg_ref, bu_ref, dact_ref, dap_ref, dcw_ref, dcb_ref):
        rows = lax.broadcasted_iota(jnp.int32, (s, tc), 0)
        shifted = []
        pre = []
        for part, (w_ref, b_ref) in enumerate(((wg_ref, bg_ref), (wu_ref, bu_ref))):
            a = a_ref[part]
            prev, nxt = _shift_rows(a, rows)
            shifted.append((prev, a, nxt))
            pre.append(b_ref[...] + prev * w_ref[0:1, :] + a * w_ref[1:2, :] + nxt * w_ref[2:3, :])
        g, u = pre
        sg = 1.0 / (1.0 + jnp.exp(-g))
        dact_v = dact_ref[...]
        das = (dact_v * u * (sg * (1.0 + g * (1.0 - sg))), dact_v * (g * sg))
        for part, w_ref in enumerate((wg_ref, wu_ref)):
            da = das[part]
            prev, a, nxt = shifted[part]
            da_prev, da_next = _shift_rows(da, rows)
            dap_ref[part] = (da_next * w_ref[0:1, :] + da * w_ref[1:2, :] + da_prev * w_ref[2:3, :]).astype(BF16)
            dcw_ref[part, 0:1, :] = _sum_rows(prev * da)
            dcw_ref[part, 1:2, :] = _sum_rows(a * da)
            dcw_ref[part, 2:3, :] = _sum_rows(nxt * da)
            dcb_ref[part] = _sum_rows(da)

    return _ordered_call(
        body, name=name,
        out_shape=(jax.ShapeDtypeStruct((2, s, D_FF), BF16), jax.ShapeDtypeStruct((2, 3, D_FF), F32),
                   jax.ShapeDtypeStruct((2, 1, D_FF), F32)),
        grid=(nj,),
        in_specs=[a_spec] + w_specs + b_specs + [pl.BlockSpec((s, tc), lambda j: (0, j))],
        out_specs=(pl.BlockSpec((2, s, tc), lambda j: (0, 0, j)), pl.BlockSpec((2, 3, tc), lambda j: (0, 0, j)),
                   pl.BlockSpec((2, 1, tc), lambda j: (0, 0, j))),
        compiler_params=_params(("parallel",)),
    )(a_pre, cw, cw, cb, cb, dact)


def _loss_head(y, target, name):
    s, d = y.shape
    tr = _row_tile(s)

    def body(y_ref, t_ref, loss_ref, dy_ref, dyb_ref):
        err = y_ref[...] - t_ref[...]

        @pl.when(pl.program_id(0) == 0)
        def _():
            loss_ref[...] = jnp.zeros_like(loss_ref)

        loss_ref[...] += jnp.broadcast_to(0.5 * _sum_all(_mean_last(err * err)), (8, 128))
        dy = err * (1.0 / d)
        dy_ref[...] = dy
        dyb_ref[...] = dy.astype(BF16)

    return _ordered_call(
        body, name=name,
        out_shape=(jax.ShapeDtypeStruct((8, 128), F32), jax.ShapeDtypeStruct((s, d), F32), jax.ShapeDtypeStruct((s, d), BF16)),
        grid=(s // tr,), in_specs=[_rows(d, tr), _rows(d, tr)],
        out_specs=(_const2((8, 128)), _rows(d, tr), _rows(d, tr)), compiler_params=_params(("arbitrary",)),
    )(y, target)


def _row_block(rows, cols, budget=1 << 20):
    if rows * cols <= budget:
        return rows
    best = None
    for tr in range(16, rows, 16):
        if rows % tr == 0 and tr * cols <= budget:
            best = tr
    assert best is not None, (rows, cols)
    return best


def _place_shard(x4, layer, j_arr, out_dtype, name):
    _, nh, r, cols = x4.shape
    tr = _row_block(r, cols)

    def body(j_ref, x_ref, o_ref):
        o_ref[...] = x_ref[...].astype(out_dtype)

    grid_spec = pltpu.PrefetchScalarGridSpec(
        num_scalar_prefetch=1, grid=(nh, r // tr),
        in_specs=[pl.BlockSpec((None, None, tr, cols), lambda h, i, j_ref: (layer, h, i, 0))],
        out_specs=pl.BlockSpec((None, None, tr, cols), lambda h, i, j_ref: (j_ref[0], h, i, 0)))
    return _ordered_call(
        body, name=name, out_shape=jax.ShapeDtypeStruct((N_CHIPS, nh, r, cols), out_dtype), grid_spec=grid_spec,
        compiler_params=_params(("parallel", "parallel")),
    )(j_arr, x4)


def _adamw(w, g, m, v, name, budget=1 << 18):
    rows, cols = w.shape
    tr = _row_block(rows, cols, budget)

    def body(w_ref, g_ref, m_ref, v_ref, go_ref, d_ref, nm_ref, nv_ref):
        gv = g_ref[...]
        go_ref[...] = gv
        mn = ADAM_B1 * m_ref[...] + (1.0 - ADAM_B1) * gv
        vn = ADAM_B2 * v_ref[...] + (1.0 - ADAM_B2) * (gv * gv)
        m_hat = mn / (1.0 - ADAM_B1 ** ADAM_STEP)
        v_hat = vn / (1.0 - ADAM_B2 ** ADAM_STEP)
        d_ref[...] = -ADAM_LR * (m_hat / (jnp.sqrt(v_hat) + ADAM_EPS) + ADAM_WD * w_ref[...])
        nm_ref[...] = mn
        nv_ref[...] = vn

    sds = jax.ShapeDtypeStruct((rows, cols), F32)
    return _ordered_call(
        body, name=name, out_shape=(sds, sds, sds, sds), grid=(rows // tr,),
        in_specs=[_rows(cols, tr)] * 4, out_specs=(_rows(cols, tr),) * 4, compiler_params=_params(("parallel",)),
    )(w, g, m, v)


def _pair_sum(g5, recv, c_arr, name):
    _, _, rh, cols = g5.shape
    tr = _row_block(rh, cols)

    def body(c_ref, g_ref, r_ref, o_ref):
        o_ref[...] = (g_ref[...].astype(F32) + r_ref[...].astype(F32)).astype(BF16)

    grid_spec = pltpu.PrefetchScalarGridSpec(
        num_scalar_prefetch=1, grid=(N_CHIPS, rh // tr),
        in_specs=[pl.BlockSpec((None, None, tr, cols), lambda j, i, c_ref: (j, c_ref[0], i, 0)),
                  pl.BlockSpec((None, tr, cols), lambda j, i, c_ref: (j, i, 0))],
        out_specs=pl.BlockSpec((None, tr, cols), lambda j, i, c_ref: (j, i, 0)))
    return _ordered_call(
        body, name=name, out_shape=jax.ShapeDtypeStruct((N_CHIPS, rh, cols), BF16), grid_spec=grid_spec,
        compiler_params=_params(("parallel", "parallel")),
    )(c_arr, g5, recv)


def _chip_sum(p4, recv3, j_arr, c_arr, name):
    _, rh, cols = p4.shape
    tr = _row_block(rh, cols, 1 << 19)

    def body(j_ref, c_ref, p_ref, r_ref, o_ref):
        total = p_ref[...].astype(F32)
        for peer in range(3):
            total = total + r_ref[peer].astype(F32)
        o_ref[...] = total

    grid_spec = pltpu.PrefetchScalarGridSpec(
        num_scalar_prefetch=2, grid=(rh // tr,),
        in_specs=[pl.BlockSpec((None, tr, cols), lambda i, j_ref, c_ref: (j_ref[0], i, 0)),
                  pl.BlockSpec((3, tr, cols), lambda i, j_ref, c_ref: (0, i, 0))],
        out_specs=pl.BlockSpec((None, tr, cols), lambda i, j_ref, c_ref: (c_ref[0], i, 0)))
    return _ordered_call(
        body, name=name, out_shape=jax.ShapeDtypeStruct((2, rh, cols), F32), grid_spec=grid_spec,
        compiler_params=_params(("parallel",)),
    )(j_arr, c_arr, p4, recv3)


def _adamw_layer(w, g, m, v, layer, into, name):
    nl, rows, cols = w.shape
    tr = _row_block(rows, cols, 1 << 18)
    at_layer = pl.BlockSpec((None, tr, cols), lambda i: (layer, i, 0))

    def body(w_ref, g_ref, m_ref, v_ref, *rest):
        go_ref, d_ref, nm_ref, nv_ref = rest[-4:]
        gv = g_ref[...]
        go_ref[...] = gv
        mn = ADAM_B1 * m_ref[...] + (1.0 - ADAM_B1) * gv
        vn = ADAM_B2 * v_ref[...] + (1.0 - ADAM_B2) * (gv * gv)
        m_hat = mn / (1.0 - ADAM_B1 ** ADAM_STEP)
        v_hat = vn / (1.0 - ADAM_B2 ** ADAM_STEP)
        d_ref[...] = -ADAM_LR * (m_hat / (jnp.sqrt(v_hat) + ADAM_EPS) + ADAM_WD * w_ref[...])
        nm_ref[...] = mn
        nv_ref[...] = vn

    in_specs = [at_layer, _rows(cols, tr), at_layer, at_layer]
    operands = [w, g, m, v]
    aliases = {}
    if into is not None:
        in_specs += [ANY] * 4
        operands += list(into)
        aliases = {4 + i: i for i in range(4)}
    sds = jax.ShapeDtypeStruct((nl, rows, cols), F32)
    return _ordered_call(
        body, name=name, out_shape=(sds,) * 4, grid=(rows // tr,), in_specs=in_specs, out_specs=(at_layer,) * 4,
        input_output_aliases=aliases, compiler_params=_params(("parallel",)),
    )(*operands)


def _sum_devices(mine, landed, me_arr, name):
    rows, lanes = mine.shape

    def body(me_ref, mine_ref, landed_ref, o_ref):
        total = None
        for dev in range(8):
            part = jnp.where(me_ref[0] == dev, mine_ref[...], landed_ref[dev])
            total = part if total is None else total + part
        o_ref[...] = total

    grid_spec = pltpu.PrefetchScalarGridSpec(
        num_scalar_prefetch=1, grid=(1,),
        in_specs=[pl.BlockSpec((rows, lanes), lambda i, me_ref: (0, 0)), pl.BlockSpec((8, rows, lanes), lambda i, me_ref: (0, 0, 0))],
        out_specs=pl.BlockSpec((rows, lanes), lambda i, me_ref: (0, 0)))
    return _ordered_call(
        body, name=name, out_shape=jax.ShapeDtypeStruct((rows, lanes), F32), grid_spec=grid_spec,
        compiler_params=_params(("arbitrary",)),
    )(me_arr, mine, landed)


def _place():
    x, y, c = lax.axis_index("x"), lax.axis_index("y"), lax.axis_index("c")
    chips = [(1 - x, y), (x, 1 - y), (1 - x, 1 - y)]
    return x, y, c, chips


HBM = pl.BlockSpec(memory_space=pltpu.HBM)
SEM = pl.BlockSpec(memory_space=pltpu.SEMAPHORE)
TOKEN = jax.ShapeDtypeStruct((8, 128), F32)


def _remote(src, dst, send_sem, recv_sem, to):
    return pltpu.make_async_remote_copy(src_ref=src, dst_ref=dst, send_sem=send_sem, recv_sem=recv_sem, device_id=to,
                                        device_id_type=MESH)


def _split_call(body, name, thru, sems_in=(), fresh=(), new_sems=(), after_last=True):
    n_t, n_s, n_f = len(thru), len(sems_in), len(fresh)

    def call_body(*refs):
        outs = refs[n_t + n_s:]
        body(refs[:n_t], refs[n_t:n_t + n_s], outs[1 + n_t:1 + n_t + n_f], outs[1 + n_t + n_f:])
        outs[0][...] = jnp.zeros_like(outs[0])

    out_shape = ([TOKEN] + [pltpu.HBM(t.shape, t.dtype) for t in thru] + [pltpu.HBM(shp, dt) for shp, dt in fresh]
                 + [pltpu.SemaphoreType.DMA(shp) for shp in new_sems])
    out_specs = [pl.BlockSpec(memory_space=pltpu.VMEM)] + [HBM] * (n_t + n_f) + [SEM] * len(new_sems)
    if not after_last:
        _Order.last = None
    out = _ordered_call(
        call_body, name=name, out_shape=tuple(out_shape), in_specs=[HBM] * n_t + [SEM] * n_s, out_specs=tuple(out_specs),
        input_output_aliases={i: 1 + i for i in range(n_t)},
        compiler_params=pltpu.CompilerParams(has_side_effects=pltpu.SideEffectType.DATAFLOW_SIDE_EFFECTING),
    )(*[pltpu.with_memory_space_constraint(t, pltpu.HBM) for t in thru], *sems_in)
    return out[1:1 + n_t], out[1 + n_t:1 + n_t + n_f], out[1 + n_t + n_f:]


class _Exchange:
    def __init__(self, weights, m_in, v_in, j_arr, c_arr, me_arr):
        self.w, self.m, self.v = weights, m_in, v_in
        self.j_arr, self.c_arr, self.me_arr = j_arr, c_arr, me_arr
        self.adam, self.small = {}, {}
        self.groups = [(l, name) for l in range(DEPTH) for name in BIG_NAMES]
        self.shard_shape = {name: weights[name].shape[1:] for name in BIG_NAMES}
        self.conv_state, self.state = [], {}
        self.ready, self.conv_ready = {}, {}
        self.pending, self.tick, self.reduced = [], 0, {}

        def place(grp):
            l, name = grp
            nl, r, cols = weights[name].shape
            buf = _place_shard(weights[name].reshape(nl, 4, r // 4, cols), l, j_arr, BF16, f"place_{name}_l{l}")
            return buf.reshape(N_CHIPS, 2, 2, r // 4, cols)

        convs = [_place_shard(weights["conv_w"][:, None], l, j_arr, F32, f"place_conv_w_l{l}") for l in range(DEPTH)]
        n_c, ahead = len(convs), self.groups[:self.AHEAD]

        def start(thru, _, __, sems):
            x, y, c, chips = _place()
            copies = []
            for i in range(n_c):
                mine = thru[i].at[2 * x + y]
                copies += [_remote(mine, mine, sems[2 * i].at[k], sems[2 * i + 1].at[k], (*chip, c)) for k, chip in enumerate(chips)]
            for i in range(n_c, len(thru)):
                copies += self._to_neighbours(thru[i], sems[2 * i], sems[2 * i + 1], x, y, c)
            for cp in copies:
                cp.start()

        thru, _, sems = _split_call(start, "gather_start", convs + [place(grp) for grp in ahead],
                                    new_sems=[(3,)] * (2 * n_c) + [(2,)] * (2 * len(ahead)))
        self.conv_state = [(thru[i], sems[2 * i], sems[2 * i + 1]) for i in range(n_c)]
        for i, grp in enumerate(ahead):
            self.state[grp] = dict(buf=thru[n_c + i], send=sems[2 * (n_c + i)], recv=sems[2 * (n_c + i) + 1])
        for grp in self.groups[self.AHEAD:]:
            self.state[grp] = dict(buf=place(grp))

    AHEAD = 2

    @staticmethod
    def _to_neighbours(buf, send, recv, x, y, c):
        mine = buf.at[2 * x + y, c]
        return [_remote(mine, mine, send.at[k], recv.at[k], (*chip, c)) for k, chip in enumerate([(1 - x, y), (x, 1 - y)])]

    def conv_w(self, l):
        if l not in self.conv_ready:
            buf, send, recv = self.conv_state[l]

            def wait(thru, sems, _, __):
                x, y, c, chips = _place()
                for k, chip in enumerate(chips):
                    mine, theirs = thru[0].at[2 * x + y], thru[0].at[2 * chip[0] + chip[1]]
                    _remote(mine, mine, sems[0].at[k], sems[1].at[k], (*chip, c)).wait_send()
                    _remote(theirs, theirs, sems[0].at[k], sems[1].at[k], (x, y, c)).wait_recv()

            (buf,), _, _ = _split_call(wait, f"gather_conv_w_l{l}", [buf], sems_in=[send, recv])
            self.conv_ready[l] = jnp.transpose(buf[:, 0], (1, 0, 2)).reshape(3, 2 * D_FF)
        return self.conv_ready[l]

    def weight(self, l, name):
        grp = (l, name)
        if grp not in self.ready:
            st = self.state[grp]
            g = self.groups.index(grp)
            nxt = self.groups[g + self.AHEAD] if g + self.AHEAD < len(self.groups) else None
            tag = f"{name}_l{l}"

            def relay(thru, sems, _, new):
                x, y, c, _chips = _place()
                j_x, j_y = 2 * (1 - x) + y, 2 * x + (1 - y)
                for k, j_k in enumerate((j_x, j_y)):
                    landed = thru[0].at[j_k, c]
                    _remote(landed, landed, new[0].at[k], sems[0].at[k], (x, y, c)).wait_recv()
                copies = [_remote(thru[0].at[j_k, c], thru[0].at[j_k, c], new[0].at[k], new[1].at[k], (x, y, 1 - c))
                          for k, j_k in enumerate((j_x, j_y))]
                from_y, from_x = thru[0].at[j_y, c, 1], thru[0].at[j_x, c, 0]
                copies += [_remote(from_y, from_y, new[2].at[0], new[3].at[0], (1 - x, y, c)),
                           _remote(from_x, from_x, new[2].at[1], new[3].at[1], (x, 1 - y, c))]
                if nxt is not None:
                    copies += self._to_neighbours(thru[1], new[4], new[5], x, y, c)
                for cp in copies:
                    cp.start()

            thru = [st["buf"]] + ([self.state[nxt]["buf"]] if nxt is not None else [])
            thru, _, new = _split_call(relay, f"gather_relay_{tag}", thru, sems_in=[st["recv"]],
                                       new_sems=[(2,)] * (6 if nxt is not None else 4))
            if nxt is not None:
                self.state[nxt].update(buf=thru[1], send=new[4], recv=new[5])

            def diagonal(thru, sems, _, new):
                x, y, c, _chips = _place()
                j_g = 2 * (1 - x) + (1 - y)
                for k in range(2):
                    quarter = thru[0].at[j_g, c, 1 - k]
                    _remote(quarter, quarter, new[0].at[0], sems[0].at[k], (x, y, c)).wait_recv()
                half = thru[0].at[j_g, c]
                _remote(half, half, new[0].at[0], new[1].at[0], (x, y, 1 - c)).start()

            (buf,), _, dnew = _split_call(diagonal, f"gather_diag_{tag}", [thru[0]], sems_in=[new[3]], new_sems=[(1,), (1,)])

            def finish(thru, sems, _, __):
                s1_send, fwd_send, fwd_recv, rel_send, d_send, d_recv = sems
                x, y, c, _chips = _place()
                j_x, j_y, j_g = 2 * (1 - x) + y, 2 * x + (1 - y), 2 * (1 - x) + (1 - y)
                for k, j_k in enumerate((j_x, j_y)):
                    theirs, landed = thru[0].at[j_k, 1 - c], thru[0].at[j_k, c]
                    _remote(theirs, theirs, fwd_send.at[k], fwd_recv.at[k], (x, y, c)).wait_recv()
                    _remote(landed, landed, fwd_send.at[k], fwd_recv.at[k], (x, y, 1 - c)).wait_send()
                theirs, landed = thru[0].at[j_g, 1 - c], thru[0].at[j_g, c]
                _remote(theirs, theirs, d_send.at[0], d_recv.at[0], (x, y, c)).wait_recv()
                _remote(landed, landed, d_send.at[0], d_recv.at[0], (x, y, 1 - c)).wait_send()
                for k, cp in enumerate(self._to_neighbours(thru[0], s1_send, fwd_recv, x, y, c)):
                    cp.wait_send()
                from_y, from_x = thru[0].at[j_y, c, 1], thru[0].at[j_x, c, 0]
                _remote(from_y, from_y, rel_send.at[0], fwd_recv.at[0], (1 - x, y, c)).wait_send()
                _remote(from_x, from_x, rel_send.at[1], fwd_recv.at[1], (x, 1 - y, c)).wait_send()

            (buf,), _, _ = _split_call(finish, f"gather_done_{tag}", [buf],
                                       sems_in=[st["send"], new[0], new[1], new[2], dnew[0], dnew[1]])
            r, cols = self.shard_shape[name]
            self.ready[grp] = buf.reshape(N_CHIPS, r, cols) if name in ("w_in", "w_up") else buf.reshape(N_CHIPS * r, cols)
        return self.ready[grp]

    def grad(self, l, name, g):
        r, cols = self.shard_shape[name]
        g5 = g.reshape(N_CHIPS, 2, r // 2, cols)

        def start(thru, _, fresh, sems):
            x, y, c, _chips = _place()
            _remote(thru[0].at[:, 1 - c], fresh[0], sems[0], sems[1], (x, y, 1 - c)).start()

        (g5,), (recv,), sems = _split_call(start, f"pair_start_{name}_l{l}", [g5], fresh=[((N_CHIPS, r // 2, cols), BF16)],
                                          new_sems=[(), ()], after_last=False)
        self.pending.append(dict(l=l, name=name, stage=1, at=self.tick, bufs=(g5, recv), sems=sems))

    def _pair(self, grp):
        l, name = grp["l"], grp["name"]
        r, cols = self.shard_shape[name]

        def wait(thru, sems, _, __):
            x, y, c, _chips = _place()
            cp = _remote(thru[0].at[:, 1 - c], thru[1], sems[0], sems[1], (x, y, 1 - c))
            cp.wait_send()
            cp.wait_recv()

        (g5, recv), _, _ = _split_call(wait, f"pair_done_{name}_l{l}", list(grp["bufs"]), sems_in=list(grp["sems"]))
        p4 = _pair_sum(g5, recv, self.c_arr, f"pair_sum_{name}_l{l}")

        def start(thru, _, fresh, sems):
            x, y, c, chips = _place()
            for k, chip in enumerate(chips):
                _remote(thru[0].at[2 * chip[0] + chip[1]], fresh[0].at[k], sems[0].at[k], sems[1].at[k], (*chip, c)).start()

        (p4,), (recv3,), sems = _split_call(start, f"chips_start_{name}_l{l}", [p4], fresh=[((3, r // 2, cols), BF16)],
                                           new_sems=[(3,), (3,)], after_last=False)
        grp.update(stage=2, at=self.tick, bufs=(p4, recv3), sems=sems)

    def _chips(self, grp):
        l, name = grp["l"], grp["name"]

        def wait(thru, sems, _, __):
            x, y, c, chips = _place()
            for k, chip in enumerate(chips):
                cp = _remote(thru[0].at[2 * chip[0] + chip[1]], thru[1].at[k], sems[0].at[k], sems[1].at[k], (*chip, c))
                cp.wait_send()
                cp.wait_recv()

        (p4, recv3), _, _ = _split_call(wait, f"chips_done_{name}_l{l}", list(grp["bufs"]), sems_in=list(grp["sems"]))
        half = _chip_sum(p4, recv3, self.j_arr, self.c_arr, f"chip_sum_{name}_l{l}")

        def start(thru, _, __, sems):
            x, y, c, _chips = _place()
            _remote(thru[0].at[c], thru[0].at[c], sems[0], sems[1], (x, y, 1 - c)).start()

        (half,), _, sems = _split_call(start, f"join_start_{name}_l{l}", [half], new_sems=[(), ()], after_last=False)
        grp.update(stage=3, at=self.tick, bufs=(half,), sems=sems)

    def _update(self, grp):
        l, name = grp["l"], grp["name"]

        def wait(thru, sems, _, __):
            x, y, c, _chips = _place()
            _remote(thru[0].at[c], thru[0].at[c], sems[0], sems[1], (x, y, 1 - c)).wait_send()
            _remote(thru[0].at[1 - c], thru[0].at[1 - c], sems[0], sems[1], (x, y, c)).wait_recv()

        (full,), _, _ = _split_call(wait, f"join_done_{name}_l{l}", list(grp["bufs"]), sems_in=list(grp["sems"]))
        self.adam[name] = _adamw_layer(self.w[name], full.reshape(self.shard_shape[name]), self.m[name], self.v[name], l,
                                       self.adam.get(name), f"adamw_{name}_l{l}")
        grp.update(stage=4)

    def point(self, drain=False):
        self.tick += 1
        for grp in self.pending:
            if grp["stage"] == 3 and (drain or grp["at"] < self.tick):
                self._update(grp)
            elif grp["stage"] == 2 and (drain or grp["at"] + 2 <= self.tick):
                self._chips(grp)
            elif grp["stage"] == 1 and (drain or grp["at"] < self.tick):
                self._pair(grp)

    def finish(self):
        while any(grp["stage"] < 4 for grp in self.pending):
            self.point(drain=True)
        return self.adam

    @staticmethod
    def _peer(k, x, y, c):
        return (1 - x if k & 4 else x, 1 - y if k & 2 else y, 1 - c if k & 1 else c)

    def small_grads(self, l, grads, loss_tile):
        parts = [grads[nm] for nm in SMALL_NAMES] + ([loss_tile[0, 0:1]] if loss_tile is not None else [])
        packed = _pack_call(parts, f"small_pack_l{l}")
        rows = packed.shape[0]

        def start(thru, _, fresh, sems):
            x, y, c, _chips = _place()
            for k in range(1, 8):
                _remote(thru[0], fresh[0].at[4 * x + 2 * y + c], sems[0].at[k - 1], sems[1].at[k - 1],
                        self._peer(k, x, y, c)).start()

        (packed,), (landed,), sems = _split_call(start, f"small_start_l{l}", [packed], fresh=[((8, rows, PACK_LANES), F32)],
                                                 new_sems=[(7,), (7,)], after_last=False)
        self.small[l] =(packed, landed, sems, [p.shape for p in parts])

    def small_sum(self, l):
        packed, landed, sems, shapes = self.small[l]

        def wait(thru, sems, _, __):
            x, y, c, _chips = _place()
            for k in range(1, 8):
                px, py, pc = self._peer(k, x, y, c)
                _remote(thru[0], thru[1].at[4 * x + 2 * y + c], sems[0].at[k - 1], sems[1].at[k - 1], (px, py, pc)).wait_send()
                _remote(thru[0], thru[1].at[4 * px + 2 * py + pc], sems[0].at[k - 1], sems[1].at[k - 1], (x, y, c)).wait_recv()

        (packed, landed), _, _ = _split_call(wait, f"small_done_l{l}", [packed, landed], sems_in=list(sems))
        return _unpack(_sum_devices(packed, landed, self.me_arr, f"small_sum_l{l}"), shapes)


def _rope_tables(s):
    inv_freq = ROPE_THETA ** (-jnp.arange(0, HEAD_DIM, 2, dtype=F32) / HEAD_DIM)
    ang = jnp.arange(s, dtype=F32)[:, None] * inv_freq[None, :]
    cos, sin = jnp.cos(ang), jnp.sin(ang)
    return jnp.concatenate([cos, cos], axis=-1), jnp.concatenate([-sin, sin], axis=-1)


def _local_step(x, target, ex, small):
    s = x.shape[0]
    cosf, sinf = _rope_tables(s)
    saved = []
    for l in range(DEPTH):
        p = small[l]
        t = f"l{l}"
        h = _rms_fwd(x, p["norm1_g"], f"norm1_{t}")
        z = _matmul(h, ex.weight(l, "w_in"), mode="nn", out_dtype=F32, tm=1024, tn=896, tk=2048, b_parts=4, name=f"proj_in_{t}")
        qn, kn, vb, ug, vn = _proj_post(z, p["q_norm_g"], p["k_norm_g"], p["sgu_ln_g"], p["sgu_ln_b"], cosf, sinf, f"proj_post_{t}")
        attn, sgu, mixed = _mixer_fwd(qn, kn, vb, ug, vn, p["w_s_bf16"], p["b_s_tile"], p["sink"], p["attn_out_g"],
                                      p["sgu_out_g"], f"mixer_{t}")
        x1 = _matmul(mixed, ex.weight(l, "w_o"), mode="nn", out_dtype=F32, tm=1024, tn=512, tk=2048, res=x, name=f"proj_out_{t}")
        h2 = _rms_fwd(x1, p["norm2_g"], f"norm2_{t}")
        a_pre = _matmul(h2, ex.weight(l, "w_up"), mode="nn", out_dtype=F32, tm=1024, tn=1408, tk=2048, b_parts=4, out_parts=2,
                        name=f"ffn_up_{t}")
        act = _conv_gate_fwd(a_pre, ex.conv_w(l), p["conv_b"], f"conv_gate_{t}")
        x2 = _matmul(act, ex.weight(l, "w_down"), mode="nn", out_dtype=F32, tm=512, tn=512, tk=D_FF, res=x1, name=f"ffn_down_{t}")
        saved.append(dict(x=x, h=h, z=z, qn=qn, kn=kn, vb=vb, ug=ug, vn=vn, attn=attn, sgu=sgu, mixed=mixed, x1=x1, h2=h2,
                          a_pre=a_pre, act=act))
        x = x2
    loss_tile, dx, dxb = _loss_head(x, target, "loss_head")
    for l in reversed(range(DEPTH)):
        p, sv = small[l], saved[l]
        t = f"l{l}"
        ex.grad(l, "w_down", _matmul(sv["act"], dxb, mode="tn", out_dtype=BF16, tm=512, tn=1024, tk=2048, name=f"g_w_down_{t}"))
        dact = _matmul(dxb, ex.weight(l, "w_down"), mode="nt", out_dtype=F32, tm=1024, tn=512, tk=2048, name=f"d_act_{t}")
        dap, dcw, dcb = _conv_gate_bwd(sv["a_pre"], ex.conv_w(l), p["conv_b"], dact, f"conv_gate_bwd_{t}")
        ex.point()
        ex.grad(l, "w_up", _matmul(sv["h2"], dap, mode="tn", out_dtype=BF16, tm=1024, tn=1408, tk=2048, b_parts=2, out_parts=4,
                                   name=f"g_w_up_{t}"))
        dh2 = _matmul_nt_slabs(dap, ex.weight(l, "w_up"), tm=512, tn=256, a_parts=2, name=f"d_h2_{t}")
        dx1, dx1b, dg2 = _rms_bwd(sv["x1"], p["norm2_g"], dh2, dx, f"norm2_bwd_{t}")
        ex.point()
        ex.grad(l, "w_o", _matmul(sv["mixed"], dx1b, mode="tn", out_dtype=BF16, tm=1024, tn=512, tk=2048, name=f"g_w_o_{t}"))
        dmixed = _matmul(dx1b, ex.weight(l, "w_o"), mode="nt", out_dtype=F32, tm=1024, tn=512, tk=2048, name=f"d_mixed_{t}")
        dqn, dkn, dvb, dug, dvn, dws, dbs, dsk, dga, dgs = _mixer_bwd(
            sv["qn"], sv["kn"], sv["vb"], sv["ug"], sv["vn"], sv["attn"], sv["sgu"], dmixed, p["w_s_bf16"], p["b_s_tile"],
            p["sink"], p["attn_out_g"], p["sgu_out_g"], f"mixer_bwd_{t}")
        dz, dqg, dkg, dlg, dlb = _proj_post_bwd(sv["z"], dqn, dkn, dvb, dug, dvn, p["q_norm_g"], p["k_norm_g"], p["sgu_ln_g"],
                                                 cosf, sinf, f"proj_post_bwd_{t}")
        ex.point()
        ex.grad(l, "w_in", _matmul(sv["h"], dz, mode="tn", out_dtype=BF16, tm=1024, tn=896, tk=2048, out_parts=4,
                                   name=f"g_w_in_{t}"))
        dh = _matmul_nt_slabs(dz, ex.weight(l, "w_in"), tm=1024, tn=512, name=f"d_h_{t}")
        dx, dxb, dg1 = _rms_bwd(sv["x"], p["norm1_g"], dh, dx1, f"norm1_bwd_{t}")
        ex.point()
        ex.small_grads(l, dict(
            norm1_g=dg1[0], q_norm_g=dqg[0], k_norm_g=dkg[0], sink=dsk[:, 0], sgu_ln_g=dlg[0], sgu_ln_b=dlb[0], w_s=dws,
            b_s=dbs[:, :, 0], attn_out_g=dga[0], sgu_out_g=dgs[0], norm2_g=dg2[0],
            conv_w=jnp.concatenate([dcw[0], dcw[1]], axis=-1), conv_b=jnp.concatenate([dcb[0, 0], dcb[1, 0]], axis=-1)),
            loss_tile if l == 0 else None)
    return dx


def _small_views(l, norm1_g, q_norm_g, k_norm_g, sink, sgu_ln_g, sgu_ln_b, w_s, b_s, attn_out_g, sgu_out_g, norm2_g, conv_b):
    return dict(
        norm1_g=norm1_g[l][None], q_norm_g=q_norm_g[l][None], k_norm_g=k_norm_g[l][None], sink=sink[l],
        sgu_ln_g=sgu_ln_g[l][None], sgu_ln_b=sgu_ln_b[l][None], w_s_bf16=w_s[l].astype(BF16),
        b_s_tile=jnp.broadcast_to(b_s[l][:, :, None], (N_GMLP_HEADS, BLOCK, BLOCK)), attn_out_g=attn_out_g[l][None],
        sgu_out_g=sgu_out_g[l][None], norm2_g=norm2_g[l][None], conv_b=conv_b[l][None])


SMALL_NAMES = ("norm1_g", "q_norm_g", "k_norm_g", "sink", "sgu_ln_g", "sgu_ln_b", "w_s", "b_s", "attn_out_g", "sgu_out_g",
               "norm2_g", "conv_w", "conv_b")
BIG_NAMES = ("w_in", "w_o", "w_up", "w_down")
PACK_LANES = 128
PACK_ALIGN = 8 * PACK_LANES


def _pack_rows(shape):
    return -(-math.prod(shape) // PACK_ALIGN) * 8


def _pack_parts(arrays):
    parts = []
    for a in arrays:
        flat = a.reshape(-1)
        parts.append(jnp.pad(flat, (0, _pack_rows(a.shape) * PACK_LANES - flat.shape[0])).reshape(-1, PACK_LANES))
    return parts


def _pack(arrays):
    return jnp.concatenate(_pack_parts(arrays), axis=0)


def _pack_call(arrays, name):
    parts = _pack_parts(arrays)
    total = sum(p.shape[0] for p in parts)

    def body(*refs):
        o_ref, at = refs[-1], 0
        for p_ref in refs[:-1]:
            o_ref[at:at + p_ref.shape[0], :] = p_ref[...]
            at += p_ref.shape[0]

    vm = pl.BlockSpec(memory_space=pltpu.VMEM)
    return _ordered_call(
        body, name=name, out_shape=jax.ShapeDtypeStruct((total, PACK_LANES), F32), in_specs=[vm] * len(parts), out_specs=vm,
        compiler_params=pltpu.CompilerParams(vmem_limit_bytes=V7X_VMEM_LIMIT),
    )(*parts)


def _unpack(packed, shapes):
    out, at = [], 0
    for shp in shapes:
        rows = _pack_rows(shp)
        out.append(packed[at:at + rows].reshape(-1)[:math.prod(shp)].reshape(shp))
        at += rows
    return out


def kernel(x, norm1_g, w_in, q_norm_g, k_norm_g, sink, sgu_ln_g, sgu_ln_b, w_s, b_s, attn_out_g, sgu_out_g, w_o, norm2_g, w_up, conv_w, conv_b, w_down, loss_target, m_norm1_g, m_w_in, m_q_norm_g, m_k_norm_g, m_sink, m_sgu_ln_g, m_sgu_ln_b, m_w_s, m_b_s, m_attn_out_g, m_sgu_out_g, m_w_o, m_norm2_g, m_w_up, m_conv_w, m_conv_b, m_w_down, v_norm1_g, v_w_in, v_q_norm_g, v_k_norm_g, v_sink, v_sgu_ln_g, v_sgu_ln_b, v_w_s, v_b_s, v_attn_out_g, v_sgu_out_g, v_w_o, v_norm2_g, v_w_up, v_conv_w, v_conv_b, v_w_down):
    weights = dict(norm1_g=norm1_g, w_in=w_in, q_norm_g=q_norm_g, k_norm_g=k_norm_g, sink=sink, sgu_ln_g=sgu_ln_g,
                   sgu_ln_b=sgu_ln_b, w_s=w_s, b_s=b_s, attn_out_g=attn_out_g, sgu_out_g=sgu_out_g, w_o=w_o, norm2_g=norm2_g,
                   w_up=w_up, conv_w=conv_w, conv_b=conv_b, w_down=w_down)
    m_in = dict(norm1_g=m_norm1_g, w_in=m_w_in, q_norm_g=m_q_norm_g, k_norm_g=m_k_norm_g, sink=m_sink, sgu_ln_g=m_sgu_ln_g,
                sgu_ln_b=m_sgu_ln_b, w_s=m_w_s, b_s=m_b_s, attn_out_g=m_attn_out_g, sgu_out_g=m_sgu_out_g, w_o=m_w_o,
                norm2_g=m_norm2_g, w_up=m_w_up, conv_w=m_conv_w, conv_b=m_conv_b, w_down=m_w_down)
    v_in = dict(norm1_g=v_norm1_g, w_in=v_w_in, q_norm_g=v_q_norm_g, k_norm_g=v_k_norm_g, sink=v_sink, sgu_ln_g=v_sgu_ln_g,
                sgu_ln_b=v_sgu_ln_b, w_s=v_w_s, b_s=v_b_s, attn_out_g=v_attn_out_g, sgu_out_g=v_sgu_out_g, w_o=v_w_o,
                norm2_g=v_norm2_g, w_up=v_w_up, conv_w=v_conv_w, conv_b=v_conv_b, w_down=v_w_down)
    cx, cy, cc = lax.axis_index("x"), lax.axis_index("y"), lax.axis_index("c")
    j_me = 2 * cx + cy
    c_arr = jnp.reshape(cc, (1,)).astype(jnp.int32)
    j_arr = jnp.reshape(j_me, (1,)).astype(jnp.int32)

    _Order.last = None
    ex = _Exchange(weights, m_in, v_in, j_arr, c_arr, jnp.reshape(4 * cx + 2 * cy + cc, (1,)).astype(jnp.int32))
    small = [_small_views(l, norm1_g, q_norm_g, k_norm_g, sink, sgu_ln_g, sgu_ln_b, w_s, b_s, attn_out_g, sgu_out_g, norm2_g,
                          conv_b) for l in range(DEPTH)]
    dx = _local_step(x[0], loss_target[0], ex, small)
    big_out = ex.finish()

    per_layer = [ex.small_sum(l) for l in range(DEPTH)]
    loss = per_layer[0][len(SMALL_NAMES)][0]
    small_full = {nm: jnp.stack([per_layer[l][i] for l in range(DEPTH)]) for i, nm in enumerate(SMALL_NAMES)}
    cw_cols = 2 * D_FF // N_CHIPS
    small_full["conv_w"] = lax.dynamic_slice_in_dim(small_full["conv_w"], j_me * cw_cols, cw_cols, axis=2)
    pw, pg, pm, pv = (_pack([src[nm] for nm in SMALL_NAMES]) for src in (weights, small_full, m_in, v_in))
    _, sd, sm, sv = _adamw(pw, pg, pm, pv, "adamw_small", budget=1 << 20)
    shapes = [weights[nm].shape for nm in SMALL_NAMES]
    grads = dict(small_full)
    delta = dict(zip(SMALL_NAMES, _unpack(sd, shapes)))
    new_m = dict(zip(SMALL_NAMES, _unpack(sm, shapes)))
    new_v = dict(zip(SMALL_NAMES, _unpack(sv, shapes)))

    for name in BIG_NAMES:
        grads[name], delta[name], new_m[name], new_v[name] = big_out[name]

    order = ("norm1_g", "w_in", "q_norm_g", "k_norm_g", "sink", "sgu_ln_g", "sgu_ln_b", "w_s", "b_s", "attn_out_g", "sgu_out_g",
             "w_o", "norm2_g", "w_up", "conv_w", "conv_b", "w_down")
    return (loss, dx[None], *[grads[nm] for nm in order], *[delta[nm] for nm in order], *[new_m[nm] for nm in order],
            *[new_v[nm] for nm in order])
```

```python
import functools
import math

import jax
import jax.numpy as jnp
from jax import lax
from jax.experimental import pallas as pl
from jax.experimental.pallas import tpu as pltpu

F32 = jnp.float32
BF16 = jnp.bfloat16

D_MODEL = 2048
HEAD_DIM = 128
ATTN_WIDTH = 1024
N_Q_HEADS = 8
N_KV_HEADS = 2
GQA_GROUP = 4
KV_WIDTH = 256
GMLP_WIDTH = 1024
N_GMLP_HEADS = 8
BLOCK = 128
IN_WIDTH = 3584
D_FF = 5632
DEPTH = 2
EPS = 1e-6
MASK_VALUE = -1e30
ROPE_THETA = 10000.0
N_CHIPS = 4

ADAM_LR = 0.001
ADAM_B1 = 0.9
ADAM_B2 = 0.999
ADAM_EPS = 1e-08
ADAM_WD = 0.01
ADAM_STEP = 10

V7X_VMEM_LIMIT = 48 * 1024 * 1024
MESH = pl.DeviceIdType.MESH

_GELU_C = math.sqrt(2.0 / math.pi)
_GELU_A = 0.044715


def _params(sem=None):
    return pltpu.CompilerParams(dimension_semantics=sem, vmem_limit_bytes=V7X_VMEM_LIMIT)


ANY = pl.BlockSpec(memory_space=pl.ANY)


class _Order:
    last = None


def _ordered_call(body, *, token_index=0, **kw):
    def run(*operands):
        tok = _Order.last
        if tok is None or any(op is tok for op in operands):
            call = pl.pallas_call(body, **kw)
        else:
            n_in = len(operands)

            def ordered_body(*refs):
                return body(*refs[:n_in], *refs[n_in + 1:])

            kw2 = dict(kw)
            if "grid_spec" in kw2:
                gs = kw2["grid_spec"]
                kw2["grid_spec"] = pltpu.PrefetchScalarGridSpec(
                    num_scalar_prefetch=gs.num_scalar_prefetch, grid=gs.grid, in_specs=list(gs.in_specs) + [ANY],
                    out_specs=gs.out_specs, scratch_shapes=gs.scratch_shapes)
            else:
                kw2["in_specs"] = list(kw2["in_specs"]) + [ANY]
            call = pl.pallas_call(ordered_body, **kw2)
            operands = operands + (tok,)
        out = call(*operands)
        _Order.last = out[token_index] if isinstance(out, (tuple, list)) else out
        return out

    return run


def _gelu(x):
    return x * (0.5 * (1.0 + jnp.tanh(_GELU_C * (x + _GELU_A * (x * x * x)))))


def _gelu_grad(x):
    x2 = x * x
    t = jnp.tanh(_GELU_C * (x + _GELU_A * (x * x2)))
    return 0.5 * (1.0 + t) + 0.5 * x * (1.0 - t * t) * (_GELU_C * (1.0 + 3.0 * _GELU_A * x2))


def _mean_last(x):
    return jnp.mean(x, axis=-1, keepdims=True)


def _sum_rows(x):
    return jnp.sum(x, axis=0, keepdims=True)


def _sum_all(x):
    return jnp.sum(jnp.sum(x, axis=1, keepdims=True), axis=0, keepdims=True)


def _matmul(a, b, *, mode, out_dtype, tm, tn, tk, name, res=None, a_parts=0, b_parts=0, out_parts=0, b_lead=()):
    b_full = b
    b = jax.ShapeDtypeStruct(b.shape[len(b_lead):], b.dtype)
    if mode == "nn":
        assert not a_parts
        m, k = a.shape
        n = b.shape[0] * b.shape[2] if b_parts else b.shape[1]
    elif mode == "nt":
        m, k = (a.shape[1], a.shape[0] * a.shape[2]) if a_parts else a.shape
        n = b.shape[1] if b_parts else b.shape[0]
    else:
        assert not a_parts
        k, m = a.shape
        n = b.shape[0] * b.shape[2] if b_parts else b.shape[1]
    tm, tn, tk = min(tm, m), min(tn, n), min(tk, k)
    assert m % tm == 0 and n % tn == 0 and k % tk == 0, (name, m, n, k, tm, tn, tk)
    nm, nn, nk = m // tm, n // tn, k // tk

    def slab(idx, total_tiles, parts):
        per = total_tiles // parts
        assert per * parts == total_tiles, (name, total_tiles, parts)
        return idx // per, idx % per

    if mode == "nn":
        a_spec = pl.BlockSpec((tm, tk), lambda i, j, kk: (i, kk))
        if b_parts:
            b_spec = pl.BlockSpec((None, tk, tn), lambda i, j, kk: (slab(j, nn, b_parts)[0], kk, slab(j, nn, b_parts)[1]))
        else:
            b_spec = pl.BlockSpec((tk, tn), lambda i, j, kk: (kk, j))
        dims = (((1,), (0,)), ((), ()))
    elif mode == "nt":
        if a_parts:
            a_spec = pl.BlockSpec((None, tm, tk), lambda i, j, kk: (slab(kk, nk, a_parts)[0], i, slab(kk, nk, a_parts)[1]))
        else:
            a_spec = pl.BlockSpec((tm, tk), lambda i, j, kk: (i, kk))
        if b_parts:
            b_spec = pl.BlockSpec((None, tn, tk), lambda i, j, kk: (slab(kk, nk, b_parts)[0], j, slab(kk, nk, b_parts)[1]))
        else:
            b_spec = pl.BlockSpec((tn, tk), lambda i, j, kk: (j, kk))
        dims = (((1,), (1,)), ((), ()))
    else:
        a_spec = pl.BlockSpec((tk, tm), lambda i, j, kk: (kk, i))
        if b_parts:
            b_spec = pl.BlockSpec((None, tk, tn), lambda i, j, kk: (slab(j, nn, b_parts)[0], kk, slab(j, nn, b_parts)[1]))
        else:
            b_spec = pl.BlockSpec((tk, tn), lambda i, j, kk: (kk, j))
        dims = (((0,), (0,)), ((), ()))
    if out_parts:
        out_shape = jax.ShapeDtypeStruct((out_parts, m, n // out_parts), out_dtype)
        out_spec = pl.BlockSpec((None, tm, tn), lambda i, j, kk: (slab(j, nn, out_parts)[0], i, slab(j, nn, out_parts)[1]))
    else:
        out_shape = jax.ShapeDtypeStruct((m, n), out_dtype)
        out_spec = pl.BlockSpec((tm, tn), lambda i, j, kk: (i, j))
    if b_lead:
        inner_map = b_spec.index_map
        b_spec = pl.BlockSpec((None,) * len(b_lead) + tuple(b_spec.block_shape),
                              lambda i, j, kk: tuple(b_lead) + tuple(inner_map(i, j, kk)))
    in_specs = [a_spec, b_spec]
    operands = [a, b_full]
    if res is not None:
        in_specs.append(pl.BlockSpec((tm, tn), lambda i, j, kk: (i, j)))
        operands.append(res)

    def body(*refs):
        a_ref, b_ref = refs[0], refs[1]
        res_ref = refs[2] if res is not None else None
        o_ref = refs[3] if res is not None else refs[2]
        p = lax.dot_general(a_ref[...], b_ref[...], dims, preferred_element_type=F32)

        def finish(total):
            if res_ref is not None:
                total = res_ref[...] + total
            o_ref[...] = total.astype(out_dtype)

        if nk == 1:
            finish(p)
        else:
            acc_ref = refs[-1]
            kk = pl.program_id(2)

            @pl.when(kk == 0)
            def _():
                acc_ref[...] = p

            @pl.when(jnp.logical_and(kk > 0, kk < nk - 1))
            def _():
                acc_ref[...] += p

            @pl.when(kk == nk - 1)
            def _():
                finish(acc_ref[...] + p)

    scratch = [pltpu.VMEM((tm, tn), F32)] if nk > 1 else []
    return _ordered_call(
        body, name=name, out_shape=out_shape, grid=(nm, nn, nk), in_specs=in_specs, out_specs=out_spec,
        scratch_shapes=scratch, compiler_params=_params(("parallel", "parallel", "arbitrary")),
    )(*operands)


def _matmul_nt_slabs(a, b, *, tm, tn, name, a_parts=0):
    nslab, n, ks = b.shape
    m = a.shape[1] if a_parts else a.shape[0]
    tm, tn = min(tm, m), min(tn, n)
    assert m % tm == 0 and n % tn == 0, (name, m, n, tm, tn)
    if a_parts:
        per = nslab // a_parts
        assert per * a_parts == nslab and a.shape[2] == per * ks, (name, a.shape, b.shape)
        a_spec = pl.BlockSpec((a_parts, tm, per * ks), lambda i, j: (0, i, 0))
    else:
        assert a.shape[1] == nslab * ks, (name, a.shape, b.shape)
        a_spec = pl.BlockSpec((tm, nslab * ks), lambda i, j: (i, 0))

    def body(a_ref, b_ref, o_ref):
        total = None
        for sl in range(nslab):
            if a_parts:
                a_sl = a_ref[sl // per, :, (sl % per) * ks:(sl % per + 1) * ks]
            else:
                a_sl = a_ref[:, sl * ks:(sl + 1) * ks]
            p = lax.dot_general(a_sl, b_ref[sl], (((1,), (1,)), ((), ())), preferred_element_type=F32)
            total = p if total is None else total + p
        o_ref[...] = total

    return _ordered_call(
        body, name=name, out_shape=jax.ShapeDtypeStruct((m, n), F32), grid=(m // tm, n // tn),
        in_specs=[a_spec, pl.BlockSpec((nslab, tn, ks), lambda i, j: (0, j, 0))],
        out_specs=pl.BlockSpec((tm, tn), lambda i, j: (i, j)), compiler_params=_params(("parallel", "parallel")),
    )(a, b)


def _row_tile(s):
    return min(256, s)


def _rows(width, tr):
    return pl.BlockSpec((tr, width), lambda i: (i, 0))


def _const2(shape):
    return pl.BlockSpec(shape, lambda i: (0, 0))


def _rms_fwd(x, g, name):
    s, d = x.shape
    tr = _row_tile(s)

    def body(x_ref, g_ref, o_ref):
        xv = x_ref[...]
        r = lax.rsqrt(_mean_last(xv * xv) + EPS)
        o_ref[...] = (xv * r * g_ref[...]).astype(BF16)

    return _ordered_call(
        body, name=name, out_shape=jax.ShapeDtypeStruct((s, d), BF16), grid=(s // tr,),
        in_specs=[_rows(d, tr), _const2((1, d))], out_specs=_rows(d, tr), compiler_params=_params(("parallel",)),
    )(x, g)


def _rms_bwd(x, g, dh, dres, name):
    s, d = x.shape
    tr = _row_tile(s)

    def body(x_ref, g_ref, dh_ref, dres_ref, dx_ref, dxb_ref, dg_ref):
        xv, dy = x_ref[...], dh_ref[...]
        r = lax.rsqrt(_mean_last(xv * xv) + EPS)
        gdy = dy * g_ref[...]
        dx = dres_ref[...] + r * gdy - xv * ((r * r * r) * _mean_last(xv * gdy))
        dx_ref[...] = dx
        dxb_ref[...] = dx.astype(BF16)

        @pl.when(pl.program_id(0) == 0)
        def _():
            dg_ref[...] = jnp.zeros_like(dg_ref)

        dg_ref[...] += _sum_rows(xv * r * dy)

    return _ordered_call(
        body, name=name,
        out_shape=(jax.ShapeDtypeStruct((s, d), F32), jax.ShapeDtypeStruct((s, d), BF16), jax.ShapeDtypeStruct((1, d), F32)),
        grid=(s // tr,), in_specs=[_rows(d, tr), _const2((1, d)), _rows(d, tr), _rows(d, tr)],
        out_specs=(_rows(d, tr), _rows(d, tr), _const2((1, d))), compiler_params=_params(("arbitrary",)),
    )(x, g, dh, dres)


Q0, K0, V0, GU0, GV0 = 0, ATTN_WIDTH, ATTN_WIDTH + KV_WIDTH, ATTN_WIDTH + 2 * KV_WIDTH, ATTN_WIDTH + 2 * KV_WIDTH + GMLP_WIDTH


def _head(h, base=0):
    return slice(base + h * HEAD_DIM, base + (h + 1) * HEAD_DIM)


def _proj_post(z, qg, kg, lg, lb, cosf, sinf, name):
    s = z.shape[0]
    tr = _row_tile(s)

    def body(z_ref, qg_ref, kg_ref, lg_ref, lb_ref, cos_ref, sin_ref, qn_ref, kn_ref, vb_ref, ug_ref, vn_ref):
        cos, sin = cos_ref[...], sin_ref[...]

        def norm_rope(xh, g):
            y = xh * lax.rsqrt(_mean_last(xh * xh) + EPS) * g
            return y * cos + pltpu.roll(y, HEAD_DIM // 2, 1) * sin

        for h in range(N_Q_HEADS):
            qn_ref[:, _head(h)] = norm_rope(z_ref[:, _head(h, Q0)], qg_ref[...]).astype(BF16)
        for h in range(N_KV_HEADS):
            kn_ref[:, _head(h)] = norm_rope(z_ref[:, _head(h, K0)], kg_ref[...]).astype(BF16)
        vb_ref[...] = z_ref[:, V0:GU0].astype(BF16)
        ug_ref[...] = _gelu(z_ref[:, GU0:GV0])
        vg = _gelu(z_ref[:, GV0:IN_WIDTH])
        xc = vg - _mean_last(vg)
        y = xc * lax.rsqrt(_mean_last(xc * xc) + EPS)
        vn_ref[...] = (y * lg_ref[...] + lb_ref[...]).astype(BF16)

    return _ordered_call(
        body, name=name,
        out_shape=(jax.ShapeDtypeStruct((s, ATTN_WIDTH), BF16), jax.ShapeDtypeStruct((s, KV_WIDTH), BF16),
                   jax.ShapeDtypeStruct((s, KV_WIDTH), BF16), jax.ShapeDtypeStruct((s, GMLP_WIDTH), F32),
                   jax.ShapeDtypeStruct((s, GMLP_WIDTH), BF16)),
        grid=(s // tr,),
        in_specs=[_rows(IN_WIDTH, tr), _const2((1, HEAD_DIM)), _const2((1, HEAD_DIM)), _const2((1, GMLP_WIDTH)),
                  _const2((1, GMLP_WIDTH)), _rows(HEAD_DIM, tr), _rows(HEAD_DIM, tr)],
        out_specs=(_rows(ATTN_WIDTH, tr), _rows(KV_WIDTH, tr), _rows(KV_WIDTH, tr), _rows(GMLP_WIDTH, tr), _rows(GMLP_WIDTH, tr)),
        compiler_params=_params(("parallel",)),
    )(z, qg, kg, lg, lb, cosf, sinf)


def _proj_post_bwd(z, dqn, dkn, dvb, dug, dvn, qg, kg, lg, cosf, sinf, name):
    s = z.shape[0]
    tr = _row_tile(s)

    def body(z_ref, dqn_ref, dkn_ref, dvb_ref, dug_ref, dvn_ref, qg_ref, kg_ref, lg_ref, cos_ref, sin_ref,
             dz_ref, dqg_ref, dkg_ref, dlg_ref, dlb_ref):
        cos, sin = cos_ref[...], sin_ref[...]

        @pl.when(pl.program_id(0) == 0)
        def _():
            dqg_ref[...] = jnp.zeros_like(dqg_ref)
            dkg_ref[...] = jnp.zeros_like(dkg_ref)
            dlg_ref[...] = jnp.zeros_like(dlg_ref)
            dlb_ref[...] = jnp.zeros_like(dlb_ref)

        def norm_rope_bwd(xh, g, dout):
            dy = dout * cos - pltpu.roll(dout, HEAD_DIM // 2, 1) * sin
            r = lax.rsqrt(_mean_last(xh * xh) + EPS)
            xhat = xh * r
            gdy = dy * g
            return r * (gdy - xhat * _mean_last(xhat * gdy)), _sum_rows(xhat * dy)

        dqg = jnp.zeros((1, HEAD_DIM), F32)
        for h in range(N_Q_HEADS):
            dx, dg = norm_rope_bwd(z_ref[:, _head(h, Q0)], qg_ref[...], dqn_ref[:, _head(h)])
            dz_ref[:, _head(h, Q0)] = dx.astype(BF16)
            dqg = dqg + dg
        dqg_ref[...] += dqg
        dkg = jnp.zeros((1, HEAD_DIM), F32)
        for h in range(N_KV_HEADS):
            dx, dg = norm_rope_bwd(z_ref[:, _head(h, K0)], kg_ref[...], dkn_ref[:, _head(h)])
            dz_ref[:, _head(h, K0)] = dx.astype(BF16)
            dkg = dkg + dg
        dkg_ref[...] += dkg
        dz_ref[:, V0:GU0] = dvb_ref[...].astype(BF16)
        dz_ref[:, GU0:GV0] = (dug_ref[...] * _gelu_grad(z_ref[:, GU0:GV0])).astype(BF16)
        gv = z_ref[:, GV0:IN_WIDTH]
        vg = _gelu(gv)
        xc = vg - _mean_last(vg)
        r = lax.rsqrt(_mean_last(xc * xc) + EPS)
        xhat = xc * r
        dvn_v = dvn_ref[...]
        dlg_ref[...] += _sum_rows(xhat * dvn_v)
        dlb_ref[...] += _sum_rows(dvn_v)
        dxh = dvn_v * lg_ref[...]
        dvg = r * (dxh - _mean_last(dxh) - xhat * _mean_last(dxh * xhat))
        dz_ref[:, GV0:IN_WIDTH] = (dvg * _gelu_grad(gv)).astype(BF16)

    return _ordered_call(
        body, name=name,
        out_shape=(jax.ShapeDtypeStruct((s, IN_WIDTH), BF16), jax.ShapeDtypeStruct((1, HEAD_DIM), F32),
                   jax.ShapeDtypeStruct((1, HEAD_DIM), F32), jax.ShapeDtypeStruct((1, GMLP_WIDTH), F32),
                   jax.ShapeDtypeStruct((1, GMLP_WIDTH), F32)),
        grid=(s // tr,),
        in_specs=[_rows(IN_WIDTH, tr), _rows(ATTN_WIDTH, tr), _rows(KV_WIDTH, tr), _rows(KV_WIDTH, tr), _rows(GMLP_WIDTH, tr),
                  _rows(GMLP_WIDTH, tr), _const2((1, HEAD_DIM)), _const2((1, HEAD_DIM)), _const2((1, GMLP_WIDTH)),
                  _rows(HEAD_DIM, tr), _rows(HEAD_DIM, tr)],
        out_specs=(_rows(IN_WIDTH, tr), _const2((1, HEAD_DIM)), _const2((1, HEAD_DIM)), _const2((1, GMLP_WIDTH)),
                   _const2((1, GMLP_WIDTH))),
        compiler_params=_params(("arbitrary",)),
    )(z, dqn, dkn, dvb, dug, dvn, qg, kg, lg, cosf, sinf)


def _band_valid(n, s):
    i = lax.broadcasted_iota(jnp.int32, (BLOCK, 3 * BLOCK), 0)
    j = lax.broadcasted_iota(jnp.int32, (BLOCK, 3 * BLOCK), 1)
    k_pos = n * BLOCK - BLOCK + j
    return (jnp.abs(j - BLOCK - i) <= BLOCK) & (k_pos >= 0) & (k_pos < s)


def _probs(q, kb, sink_h, valid):
    sc = lax.dot_general(q, kb, (((1,), (1,)), ((), ())), preferred_element_type=F32) * (HEAD_DIM ** -0.5)
    sc = jnp.where(valid, sc, MASK_VALUE)
    m = jnp.maximum(jnp.max(sc, axis=-1, keepdims=True), sink_h)
    p = jnp.exp(sc - m)
    es = jnp.exp(sink_h - m)
    den = jnp.sum(p, axis=-1, keepdims=True) + es
    return p / den, es / den


def _band_specs(width, nb):
    return [pl.BlockSpec((BLOCK, width), lambda n: (jnp.maximum(n - 1, 0), 0)),
            pl.BlockSpec((BLOCK, width), lambda n: (n, 0)),
            pl.BlockSpec((BLOCK, width), lambda n: (jnp.minimum(n + 1, nb - 1), 0))]


def _blk(width):
    return pl.BlockSpec((BLOCK, width), lambda n: (n, 0))


def _whole3(shape):
    return pl.BlockSpec(shape, lambda n: (0, 0, 0))


def _smem():
    return pl.BlockSpec(memory_space=pltpu.SMEM)


def _mixer_fwd(qn, kn, vb, ug, vn, wsb, bsb, sink, ga, gs, name):
    s = qn.shape[0]
    nb = s // BLOCK

    def body(sink_ref, q_ref, kp_ref, kc_ref, kx_ref, vp_ref, vc_ref, vx_ref, ug_ref, vn_ref, ws_ref, bs_ref, ga_ref, gs_ref,
             attn_ref, sgu_ref, mix_ref):
        n = pl.program_id(0)
        valid = _band_valid(n, s)
        ssq = jnp.zeros((BLOCK, 1), F32)
        for kh in range(N_KV_HEADS):
            kb = jnp.concatenate([kp_ref[:, _head(kh)], kc_ref[:, _head(kh)], kx_ref[:, _head(kh)]], axis=0)
            vbd = jnp.concatenate([vp_ref[:, _head(kh)], vc_ref[:, _head(kh)], vx_ref[:, _head(kh)]], axis=0)
            for g in range(GQA_GROUP):
                h = kh * GQA_GROUP + g
                p, _ = _probs(q_ref[:, _head(h)], kb, sink_ref[h], valid)
                o = jnp.dot(p.astype(BF16), vbd, preferred_element_type=F32)
                attn_ref[:, _head(h)] = o
                ssq = ssq + jnp.sum(o * o, axis=-1, keepdims=True)
        r = lax.rsqrt(ssq * (1.0 / ATTN_WIDTH) + EPS)
        mix_ref[:, 0:ATTN_WIDTH] = (attn_ref[...] * r * ga_ref[...]).astype(BF16)
        ssq = jnp.zeros((BLOCK, 1), F32)
        for h in range(N_GMLP_HEADS):
            f = jnp.dot(ws_ref[h], vn_ref[:, _head(h)], preferred_element_type=F32) + bs_ref[h]
            o = ug_ref[:, _head(h)] * f
            sgu_ref[:, _head(h)] = o
            ssq = ssq + jnp.sum(o * o, axis=-1, keepdims=True)
        r = lax.rsqrt(ssq * (1.0 / GMLP_WIDTH) + EPS)
        mix_ref[:, ATTN_WIDTH:D_MODEL] = (sgu_ref[...] * r * gs_ref[...]).astype(BF16)

    hh = (N_GMLP_HEADS, BLOCK, BLOCK)
    return _ordered_call(
        body, name=name,
        out_shape=(jax.ShapeDtypeStruct((s, ATTN_WIDTH), F32), jax.ShapeDtypeStruct((s, GMLP_WIDTH), F32),
                   jax.ShapeDtypeStruct((s, D_MODEL), BF16)),
        grid=(nb,),
        in_specs=[_smem(), _blk(ATTN_WIDTH)] + _band_specs(KV_WIDTH, nb) + _band_specs(KV_WIDTH, nb)
        + [_blk(GMLP_WIDTH), _blk(GMLP_WIDTH), _whole3(hh), _whole3(hh),
           pl.BlockSpec((1, ATTN_WIDTH), lambda n: (0, 0)), pl.BlockSpec((1, GMLP_WIDTH), lambda n: (0, 0))],
        out_specs=(_blk(ATTN_WIDTH), _blk(GMLP_WIDTH), _blk(D_MODEL)),
        compiler_params=_params(("parallel",)),
    )(sink, qn, kn, kn, kn, vb, vb, vb, ug, vn, wsb, bsb, ga, gs)


def _mixer_bwd(qn, kn, vb, ug, vn, attn, sgu, dmixed, wsb, bsb, sink, ga, gs, name):
    s = qn.shape[0]
    nb = s // BLOCK
    tn_dims = (((0,), (0,)), ((), ()))
    nt_dims = (((1,), (1,)), ((), ()))

    def body(sink_ref, q_ref, kp_ref, kc_ref, kx_ref, vp_ref, vc_ref, vx_ref, ug_ref, vn_ref, attn_ref, sgu_ref, dm_ref,
             ws_ref, bs_ref, ga_ref, gs_ref,
             dq_ref, dk_ref, dv_ref, dug_ref, dvn_ref, dws_ref, dbs_ref, dsk_ref, dga_ref, dgs_ref, dk_acc, dv_acc):
        n = pl.program_id(0)

        @pl.when(n == 0)
        def _():
            for ref in (dk_acc, dv_acc, dws_ref, dbs_ref, dsk_ref, dga_ref, dgs_ref):
                ref[...] = jnp.zeros_like(ref)

        def out_norm_bwd(o, g, dy):
            r = lax.rsqrt(_mean_last(o * o) + EPS)
            gdy = dy * g
            return r * gdy - o * ((r * r * r) * _mean_last(o * gdy)), _sum_rows(o * r * dy)

        d_attn, dga = out_norm_bwd(attn_ref[...], ga_ref[...], dm_ref[:, 0:ATTN_WIDTH])
        dga_ref[...] += dga
        d_sgu, dgs = out_norm_bwd(sgu_ref[...], gs_ref[...], dm_ref[:, ATTN_WIDTH:D_MODEL])
        dgs_ref[...] += dgs

        for h in range(N_GMLP_HEADS):
            vn_h = vn_ref[:, _head(h)]
            f = jnp.dot(ws_ref[h], vn_h, preferred_element_type=F32) + bs_ref[h]
            ds_h = d_sgu[:, _head(h)]
            dug_ref[:, _head(h)] = ds_h * f
            df = ds_h * ug_ref[:, _head(h)]
            dfb = df.astype(BF16)
            dvn_ref[:, _head(h)] = lax.dot_general(ws_ref[h], dfb, tn_dims, preferred_element_type=F32)
            dws_ref[h] += lax.dot_general(dfb, vn_h, nt_dims, preferred_element_type=F32)
            dbs_ref[h] += jnp.broadcast_to(jnp.sum(df, axis=-1, keepdims=True), (BLOCK, BLOCK))

        valid = _band_valid(n, s)
        row0 = pl.multiple_of(n * BLOCK, BLOCK)
        for kh in range(N_KV_HEADS):
            kb = jnp.concatenate([kp_ref[:, _head(kh)], kc_ref[:, _head(kh)], kx_ref[:, _head(kh)]], axis=0)
            vbd = jnp.concatenate([vp_ref[:, _head(kh)], vc_ref[:, _head(kh)], vx_ref[:, _head(kh)]], axis=0)
            dkb = jnp.zeros((3 * BLOCK, HEAD_DIM), F32)
            dvb = jnp.zeros((3 * BLOCK, HEAD_DIM), F32)
            for g in range(GQA_GROUP):
                h = kh * GQA_GROUP + g
                q = q_ref[:, _head(h)]
                p, p_sink = _probs(q, kb, sink_ref[h], valid)
                do = d_attn[:, _head(h)].astype(BF16)
                dp = lax.dot_general(do, vbd, nt_dims, preferred_element_type=F32)
                delta = jnp.sum(p * dp, axis=-1, keepdims=True)
                dsc = (p * (dp - delta) * (HEAD_DIM ** -0.5)).astype(BF16)
                dsk_ref[h:h + 1, :] += jnp.broadcast_to(_sum_all(-(p_sink * delta)), (1, BLOCK))
                dq_ref[:, _head(h)] = jnp.dot(dsc, kb, preferred_element_type=F32)
                dkb = dkb + lax.dot_general(dsc, q, tn_dims, preferred_element_type=F32)
                dvb = dvb + lax.dot_general(p.astype(BF16), do, tn_dims, preferred_element_type=F32)
            dk_acc[pl.ds(row0, 3 * BLOCK), _head(kh)] += dkb
            dv_acc[pl.ds(row0, 3 * BLOCK), _head(kh)] += dvb

        @pl.when(n == nb - 1)
        def _():
            dk_ref[...] = dk_acc[BLOCK:BLOCK + s, :]
            dv_ref[...] = dv_acc[BLOCK:BLOCK + s, :]

    hh = (N_GMLP_HEADS, BLOCK, BLOCK)
    full_kv = pl.BlockSpec((s, KV_WIDTH), lambda n: (0, 0))
    return _ordered_call(
        body, name=name,
        out_shape=(jax.ShapeDtypeStruct((s, ATTN_WIDTH), F32), jax.ShapeDtypeStruct((s, KV_WIDTH), F32),
                   jax.ShapeDtypeStruct((s, KV_WIDTH), F32), jax.ShapeDtypeStruct((s, GMLP_WIDTH), F32),
                   jax.ShapeDtypeStruct((s, GMLP_WIDTH), F32), jax.ShapeDtypeStruct(hh, F32), jax.ShapeDtypeStruct(hh, F32),
                   jax.ShapeDtypeStruct((N_Q_HEADS, BLOCK), F32), jax.ShapeDtypeStruct((1, ATTN_WIDTH), F32),
                   jax.ShapeDtypeStruct((1, GMLP_WIDTH), F32)),
        grid=(nb,),
        in_specs=[_smem(), _blk(ATTN_WIDTH)] + _band_specs(KV_WIDTH, nb) + _band_specs(KV_WIDTH, nb)
        + [_blk(GMLP_WIDTH), _blk(GMLP_WIDTH), _blk(ATTN_WIDTH), _blk(GMLP_WIDTH), _blk(D_MODEL), _whole3(hh), _whole3(hh),
           pl.BlockSpec((1, ATTN_WIDTH), lambda n: (0, 0)), pl.BlockSpec((1, GMLP_WIDTH), lambda n: (0, 0))],
        out_specs=(_blk(ATTN_WIDTH), full_kv, full_kv, _blk(GMLP_WIDTH), _blk(GMLP_WIDTH), _whole3(hh), _whole3(hh),
                   pl.BlockSpec((N_Q_HEADS, BLOCK), lambda n: (0, 0)), pl.BlockSpec((1, ATTN_WIDTH), lambda n: (0, 0)),
                   pl.BlockSpec((1, GMLP_WIDTH), lambda n: (0, 0))),
        scratch_shapes=[pltpu.VMEM((s + 2 * BLOCK, KV_WIDTH), F32), pltpu.VMEM((s + 2 * BLOCK, KV_WIDTH), F32)],
        compiler_params=_params(("arbitrary",)),
    )(sink, qn, kn, kn, kn, vb, vb, vb, ug, vn, attn, sgu, dmixed, wsb, bsb, ga, gs)


CONV_TILE = 128


PAD_ROWS = 8


def _zero_pad_rows(pad_ref):
    s = pad_ref.shape[0] - 2 * PAD_ROWS
    zeros = jnp.zeros((PAD_ROWS, pad_ref.shape[1]), F32)
    pad_ref[0:PAD_ROWS, :] = zeros
    pad_ref[PAD_ROWS + s:2 * PAD_ROWS + s, :] = zeros


def _shift_rows(a, pad_ref):
    s = a.shape[0]
    pad_ref[PAD_ROWS:PAD_ROWS + s, :] = a
    padded = pad_ref[...]
    prev = pltpu.roll(padded, 1, 0)[PAD_ROWS:PAD_ROWS + s]
    nxt = pltpu.roll(padded, s + 2 * PAD_ROWS - 1, 0)[PAD_ROWS:PAD_ROWS + s]
    return prev, nxt


def _conv_specs(s):
    tc = CONV_TILE
    nj = D_FF // tc
    return (tc, nj, pl.BlockSpec((2, s, tc), lambda j: (0, 0, j)),
            [pl.BlockSpec((3, tc), lambda j: (0, j)), pl.BlockSpec((3, tc), lambda j: (0, j + nj))],
            [pl.BlockSpec((1, tc), lambda j: (0, j)), pl.BlockSpec((1, tc), lambda j: (0, j + nj))])


def _conv_gate_fwd(a_pre, cw, cb, name):
    s = a_pre.shape[1]
    tc, nj, a_spec, w_specs, b_specs = _conv_specs(s)

    def body(a_ref, wg_ref, wu_ref, bg_ref, bu_ref, act_ref, dgu_ref, pad_ref):
        _zero_pad_rows(pad_ref)

        def conv(a, w_ref, b_ref):
            prev, nxt = _shift_rows(a, pad_ref)
            return b_ref[...] + prev * w_ref[0:1, :] + a * w_ref[1:2, :] + nxt * w_ref[2:3, :]

        g = conv(a_ref[0], wg_ref, bg_ref)
        u = conv(a_ref[1], wu_ref, bu_ref)
        sg = 1.0 / (1.0 + jnp.exp(-g))
        silu = g * sg
        act_ref[...] = (silu * u).astype(BF16)
        dgu_ref[0] = (u * (sg * (1.0 + g * (1.0 - sg)))).astype(BF16)
        dgu_ref[1] = silu.astype(BF16)

    return _ordered_call(
        body, name=name, out_shape=(jax.ShapeDtypeStruct((s, D_FF), BF16), jax.ShapeDtypeStruct((2, s, D_FF), BF16)),
        grid=(nj,), in_specs=[a_spec] + w_specs + b_specs,
        out_specs=(pl.BlockSpec((s, tc), lambda j: (0, j)), pl.BlockSpec((2, s, tc), lambda j: (0, 0, j))),
        scratch_shapes=[pltpu.VMEM((s + 2 * PAD_ROWS, tc), F32)], compiler_params=_params(("parallel",)),
    )(a_pre, cw, cw, cb, cb)


def _conv_gate_bwd(a_pre, dgu, cw, dact, name):
    s = a_pre.shape[1]
    tc, nj, a_spec, w_specs, _ = _conv_specs(s)

    def body(a_ref, dgu_ref, wg_ref, wu_ref, dact_ref, dap_ref, dcw_ref, dcb_ref, pad_ref):
        _zero_pad_rows(pad_ref)
        dact_v = dact_ref[...]
        for part, w_ref in enumerate((wg_ref, wu_ref)):
            da = dact_v * dgu_ref[part].astype(F32)
            a = a_ref[part]
            prev, nxt = _shift_rows(a, pad_ref)
            dcw_ref[part, 0:1, :] = _sum_rows(prev * da)
            dcw_ref[part, 1:2, :] = _sum_rows(a * da)
            dcw_ref[part, 2:3, :] = _sum_rows(nxt * da)
            dcb_ref[part] = _sum_rows(da)
            da_prev, da_next = _shift_rows(da, pad_ref)
            dap_ref[part] = (da_next * w_ref[0:1, :] + da * w_ref[1:2, :] + da_prev * w_ref[2:3, :]).astype(BF16)

    return _ordered_call(
        body, name=name,
        out_shape=(jax.ShapeDtypeStruct((2, s, D_FF), BF16), jax.ShapeDtypeStruct((2, 3, D_FF), F32),
                   jax.ShapeDtypeStruct((2, 1, D_FF), F32)),
        grid=(nj,),
        in_specs=[a_spec, pl.BlockSpec((2, s, tc), lambda j: (0, 0, j))] + w_specs + [pl.BlockSpec((s, tc), lambda j: (0, j))],
        out_specs=(pl.BlockSpec((2, s, tc), lambda j: (0, 0, j)), pl.BlockSpec((2, 3, tc), lambda j: (0, 0, j)),
                   pl.BlockSpec((2, 1, tc), lambda j: (0, 0, j))),
        scratch_shapes=[pltpu.VMEM((s + 2 * PAD_ROWS, tc), F32)], compiler_params=_params(("parallel",)),
    )(a_pre, dgu, cw, cw, dact)


def _loss_head(y, target, name):
    s, d = y.shape
    tr = _row_tile(s)

    def body(y_ref, t_ref, loss_ref, dy_ref, dyb_ref):
        err = y_ref[...] - t_ref[...]

        @pl.when(pl.program_id(0) == 0)
        def _():
            loss_ref[...] = jnp.zeros_like(loss_ref)

        loss_ref[...] += jnp.broadcast_to(0.5 * _sum_all(_mean_last(err * err)), (8, 128))
        dy = err * (1.0 / d)
        dy_ref[...] = dy
        dyb_ref[...] = dy.astype(BF16)

    return _ordered_call(
        body, name=name,
        out_shape=(jax.ShapeDtypeStruct((8, 128), F32), jax.ShapeDtypeStruct((s, d), F32), jax.ShapeDtypeStruct((s, d), BF16)),
        grid=(s // tr,), in_specs=[_rows(d, tr), _rows(d, tr)],
        out_specs=(_const2((8, 128)), _rows(d, tr), _rows(d, tr)), compiler_params=_params(("arbitrary",)),
    )(y, target)


def _row_block(rows, cols, budget=1 << 20):
    if rows * cols <= budget:
        return rows
    best = None
    for tr in range(16, rows, 16):
        if rows % tr == 0 and tr * cols <= budget:
            best = tr
    assert best is not None, (rows, cols)
    return best


def _place_shard(x4, layer, j_arr, out_dtype, name):
    _, nh, r, cols = x4.shape
    tr = _row_block(r, cols)

    def body(j_ref, x_ref, o_ref):
        o_ref[...] = x_ref[...].astype(out_dtype)

    grid_spec = pltpu.PrefetchScalarGridSpec(
        num_scalar_prefetch=1, grid=(nh, r // tr),
        in_specs=[pl.BlockSpec((None, None, tr, cols), lambda h, i, j_ref: (layer, h, i, 0))],
        out_specs=pl.BlockSpec((None, None, tr, cols), lambda h, i, j_ref: (j_ref[0], h, i, 0)))
    return _ordered_call(
        body, name=name, out_shape=jax.ShapeDtypeStruct((N_CHIPS, nh, r, cols), out_dtype), grid_spec=grid_spec,
        compiler_params=_params(("parallel", "parallel")),
    )(j_arr, x4)


def _adamw(w, g, m, v, name, budget=1 << 18):
    rows, cols = w.shape
    tr = _row_block(rows, cols, budget)

    def body(w_ref, g_ref, m_ref, v_ref, go_ref, d_ref, nm_ref, nv_ref):
        gv = g_ref[...]
        go_ref[...] = gv
        mn = ADAM_B1 * m_ref[...] + (1.0 - ADAM_B1) * gv
        vn = ADAM_B2 * v_ref[...] + (1.0 - ADAM_B2) * (gv * gv)
        m_hat = mn / (1.0 - ADAM_B1 ** ADAM_STEP)
        v_hat = vn / (1.0 - ADAM_B2 ** ADAM_STEP)
        d_ref[...] = -ADAM_LR * (m_hat / (jnp.sqrt(v_hat) + ADAM_EPS) + ADAM_WD * w_ref[...])
        nm_ref[...] = mn
        nv_ref[...] = vn

    sds = jax.ShapeDtypeStruct((rows, cols), F32)
    return _ordered_call(
        body, name=name, out_shape=(sds, sds, sds, sds), grid=(rows // tr,),
        in_specs=[_rows(cols, tr)] * 4, out_specs=(_rows(cols, tr),) * 4, compiler_params=_params(("parallel",)),
    )(w, g, m, v)


def _pair_sum(g5, recv, c_arr, name):
    _, _, rh, cols = g5.shape
    tr = _row_block(rh, cols)

    def body(c_ref, g_ref, r_ref, o_ref):
        o_ref[...] = (g_ref[...].astype(F32) + r_ref[...].astype(F32)).astype(BF16)

    grid_spec = pltpu.PrefetchScalarGridSpec(
        num_scalar_prefetch=1, grid=(N_CHIPS, rh // tr),
        in_specs=[pl.BlockSpec((None, None, tr, cols), lambda j, i, c_ref: (j, c_ref[0], i, 0)),
                  pl.BlockSpec((None, tr, cols), lambda j, i, c_ref: (j, i, 0))],
        out_specs=pl.BlockSpec((None, tr, cols), lambda j, i, c_ref: (j, i, 0)))
    return _ordered_call(
        body, name=name, out_shape=jax.ShapeDtypeStruct((N_CHIPS, rh, cols), BF16), grid_spec=grid_spec,
        compiler_params=_params(("parallel", "parallel")),
    )(c_arr, g5, recv)


def _chip_sum(p4, recv3, j_arr, c_arr, name):
    _, rh, cols = p4.shape
    tr = _row_block(rh, cols, 1 << 19)

    def body(j_ref, c_ref, p_ref, r_ref, o_ref):
        total = p_ref[...].astype(F32)
        for peer in range(3):
            total = total + r_ref[peer].astype(F32)
        o_ref[...] = total

    grid_spec = pltpu.PrefetchScalarGridSpec(
        num_scalar_prefetch=2, grid=(rh // tr,),
        in_specs=[pl.BlockSpec((None, tr, cols), lambda i, j_ref, c_ref: (j_ref[0], i, 0)),
                  pl.BlockSpec((3, tr, cols), lambda i, j_ref, c_ref: (0, i, 0))],
        out_specs=pl.BlockSpec((None, tr, cols), lambda i, j_ref, c_ref: (c_ref[0], i, 0)))
    return _ordered_call(
        body, name=name, out_shape=jax.ShapeDtypeStruct((2, rh, cols), F32), grid_spec=grid_spec,
        compiler_params=_params(("parallel",)),
    )(j_arr, c_arr, p4, recv3)


def _adamw_layer(w, g, m, v, layer, into, name):
    nl, rows, cols = w.shape
    tr = _row_block(rows, cols, 1 << 18)
    at_layer = pl.BlockSpec((None, tr, cols), lambda i: (layer, i, 0))

    def body(w_ref, g_ref, m_ref, v_ref, *rest):
        go_ref, d_ref, nm_ref, nv_ref = rest[-4:]
        gv = g_ref[...]
        go_ref[...] = gv
        mn = ADAM_B1 * m_ref[...] + (1.0 - ADAM_B1) * gv
        vn = ADAM_B2 * v_ref[...] + (1.0 - ADAM_B2) * (gv * gv)
        m_hat = mn / (1.0 - ADAM_B1 ** ADAM_STEP)
        v_hat = vn / (1.0 - ADAM_B2 ** ADAM_STEP)
        d_ref[...] = -ADAM_LR * (m_hat / (jnp.sqrt(v_hat) + ADAM_EPS) + ADAM_WD * w_ref[...])
        nm_ref[...] = mn
        nv_ref[...] = vn

    in_specs = [at_layer, _rows(cols, tr), at_layer, at_layer]
    operands = [w, g, m, v]
    aliases = {}
    if into is not None:
        in_specs += [ANY] * 4
        operands += list(into)
        aliases = {4 + i: i for i in range(4)}
    sds = jax.ShapeDtypeStruct((nl, rows, cols), F32)
    return _ordered_call(
        body, name=name, out_shape=(sds,) * 4, grid=(rows // tr,), in_specs=in_specs, out_specs=(at_layer,) * 4,
        input_output_aliases=aliases, compiler_params=_params(("parallel",)),
    )(*operands)


def _sum_devices(mine, landed, me_arr, name):
    rows, lanes = mine.shape

    def body(me_ref, mine_ref, landed_ref, o_ref):
        total = None
        for dev in range(8):
            part = jnp.where(me_ref[0] == dev, mine_ref[...], landed_ref[dev])
            total = part if total is None else total + part
        o_ref[...] = total

    grid_spec = pltpu.PrefetchScalarGridSpec(
        num_scalar_prefetch=1, grid=(1,),
        in_specs=[pl.BlockSpec((rows, lanes), lambda i, me_ref: (0, 0)), pl.BlockSpec((8, rows, lanes), lambda i, me_ref: (0, 0, 0))],
        out_specs=pl.BlockSpec((rows, lanes), lambda i, me_ref: (0, 0)))
    return _ordered_call(
        body, name=name, out_shape=jax.ShapeDtypeStruct((rows, lanes), F32), grid_spec=grid_spec,
        compiler_params=_params(("arbitrary",)),
    )(me_arr, mine, landed)


def _place():
    x, y, c = lax.axis_index("x"), lax.axis_index("y"), lax.axis_index("c")
    chips = [(1 - x, y), (x, 1 - y), (1 - x, 1 - y)]
    return x, y, c, chips


HBM = pl.BlockSpec(memory_space=pltpu.HBM)
SEM = pl.BlockSpec(memory_space=pltpu.SEMAPHORE)
TOKEN = jax.ShapeDtypeStruct((8, 128), F32)


def _remote(src, dst, send_sem, recv_sem, to):
    return pltpu.make_async_remote_copy(src_ref=src, dst_ref=dst, send_sem=send_sem, recv_sem=recv_sem, device_id=to,
                                        device_id_type=MESH)


def _split_call(body, name, thru, sems_in=(), fresh=(), new_sems=(), after_last=True):
    n_t, n_s, n_f = len(thru), len(sems_in), len(fresh)

    def call_body(*refs):
        outs = refs[n_t + n_s:]
        body(refs[:n_t], refs[n_t:n_t + n_s], outs[1 + n_t:1 + n_t + n_f], outs[1 + n_t + n_f:])
        outs[0][...] = jnp.zeros_like(outs[0])

    out_shape = ([TOKEN] + [pltpu.HBM(t.shape, t.dtype) for t in thru] + [pltpu.HBM(shp, dt) for shp, dt in fresh]
                 + [pltpu.SemaphoreType.DMA(shp) for shp in new_sems])
    out_specs = [pl.BlockSpec(memory_space=pltpu.VMEM)] + [HBM] * (n_t + n_f) + [SEM] * len(new_sems)
    if not after_last:
        _Order.last = None
    out = _ordered_call(
        call_body, name=name, out_shape=tuple(out_shape), in_specs=[HBM] * n_t + [SEM] * n_s, out_specs=tuple(out_specs),
        input_output_aliases={i: 1 + i for i in range(n_t)},
        compiler_params=pltpu.CompilerParams(has_side_effects=pltpu.SideEffectType.DATAFLOW_SIDE_EFFECTING),
    )(*[pltpu.with_memory_space_constraint(t, pltpu.HBM) for t in thru], *sems_in)
    return out[1:1 + n_t], out[1 + n_t:1 + n_t + n_f], out[1 + n_t + n_f:]


class _Exchange:
    def __init__(self, weights, m_in, v_in, j_arr, c_arr, me_arr):
        self.w, self.m, self.v = weights, m_in, v_in
        self.j_arr, self.c_arr, self.me_arr = j_arr, c_arr, me_arr
        self.adam, self.small = {}, {}
        self.groups = [(l, name) for l in range(DEPTH) for name in BIG_NAMES]
        self.shard_shape = {name: weights[name].shape[1:] for name in BIG_NAMES}
        self.conv_state, self.state = [], {}
        self.ready, self.conv_ready = {}, {}
        self.pending, self.tick, self.reduced = [], 0, {}

        def place(grp):
            l, name = grp
            nl, r, cols = weights[name].shape
            return _place_shard(weights[name].reshape(nl, 2, r // 2, cols), l, j_arr, BF16, f"place_{name}_l{l}")

        def start_copies(tag, convs, groups, bufs):
            n_c = len(convs)

            def start(thru, _, __, sems):
                x, y, c, chips = _place()
                j_me = 2 * x + y
                copies = []
                for i in range(len(thru)):
                    mine = thru[i].at[j_me] if i < n_c else thru[i].at[j_me, c]
                    copies += [_remote(mine, mine, sems[2 * i].at[k], sems[2 * i + 1].at[k], (*chip, c))
                               for k, chip in enumerate(chips)]
                for cp in copies:
                    cp.start()

            thru, _, sems = _split_call(start, tag, convs + bufs, new_sems=[(3,)] * (2 * (n_c + len(bufs))))
            self.conv_state += [(thru[i], sems[2 * i], sems[2 * i + 1]) for i in range(n_c)]
            for g, grp in enumerate(groups):
                self.state[grp] = (thru[n_c + g], sems[2 * (n_c + g)], sems[2 * (n_c + g) + 1])

        convs = [_place_shard(weights["conv_w"][:, None], l, j_arr, F32, f"place_conv_w_l{l}") for l in range(DEPTH)]
        start_copies("gather_start_first", convs, self.groups[:1], [place(self.groups[0])])
        start_copies("gather_start_rest", [], self.groups[1:], [place(grp) for grp in self.groups[1:]])

    def conv_w(self, l):
        if l not in self.conv_ready:
            buf, send, recv = self.conv_state[l]

            def wait(thru, sems, _, __):
                x, y, c, chips = _place()
                for k, chip in enumerate(chips):
                    mine, theirs = thru[0].at[2 * x + y], thru[0].at[2 * chip[0] + chip[1]]
                    _remote(mine, mine, sems[0].at[k], sems[1].at[k], (*chip, c)).wait_send()
                    _remote(theirs, theirs, sems[0].at[k], sems[1].at[k], (x, y, c)).wait_recv()

            (buf,), _, _ = _split_call(wait, f"gather_conv_w_l{l}", [buf], sems_in=[send, recv])
            self.conv_ready[l] = jnp.transpose(buf[:, 0], (1, 0, 2)).reshape(3, 2 * D_FF)
        return self.conv_ready[l]

    def weight(self, l, name):
        grp = (l, name)
        if grp not in self.ready:
            buf, send, recv = self.state[grp]

            def forward(thru, sems, _, new):
                x, y, c, chips = _place()
                for k, chip in enumerate(chips):
                    landed = thru[0].at[2 * chip[0] + chip[1], c]
                    _remote(landed, landed, new[0].at[k], sems[0].at[k], (x, y, c)).wait_recv()
                    _remote(landed, landed, new[0].at[k], new[1].at[k], (x, y, 1 - c)).start()

            (buf,), _, (fsend, frecv) = _split_call(forward, f"gather_pass_{name}_l{l}", [buf], sems_in=[recv],
                                                    new_sems=[(3,), (3,)])

            def finish(thru, sems, _, __):
                x, y, c, chips = _place()
                mine = thru[0].at[2 * x + y, c]
                for k, chip in enumerate(chips):
                    j_k = 2 * chip[0] + chip[1]
                    theirs, landed = thru[0].at[j_k, 1 - c], thru[0].at[j_k, c]
                    _remote(theirs, theirs, sems[1].at[k], sems[2].at[k], (x, y, c)).wait_recv()
                    _remote(landed, landed, sems[1].at[k], sems[2].at[k], (x, y, 1 - c)).wait_send()
                    _remote(mine, mine, sems[0].at[k], sems[2].at[k], (*chip, c)).wait_send()

            (buf,), _, _ = _split_call(finish, f"gather_done_{name}_l{l}", [buf], sems_in=[send, fsend, frecv])
            r, cols = self.shard_shape[name]
            self.ready[grp] = buf.reshape(N_CHIPS, r, cols) if name in ("w_in", "w_up") else buf.reshape(N_CHIPS * r, cols)
        return self.ready[grp]

    def grad(self, l, name, g):
        r, cols = self.shard_shape[name]
        g5 = g.reshape(N_CHIPS, 2, r // 2, cols)

        def start(thru, _, fresh, sems):
            x, y, c, _chips = _place()
            _remote(thru[0].at[:, 1 - c], fresh[0], sems[0], sems[1], (x, y, 1 - c)).start()

        (g5,), (recv,), sems = _split_call(start, f"pair_start_{name}_l{l}", [g5], fresh=[((N_CHIPS, r // 2, cols), BF16)],
                                          new_sems=[(), ()], after_last=False)
        self.pending.append(dict(l=l, name=name, stage=1, at=self.tick, bufs=(g5, recv), sems=sems))

    def _pair(self, grp):
        l, name = grp["l"], grp["name"]
        r, cols = self.shard_shape[name]

        def wait(thru, sems, _, __):
            x, y, c, _chips = _place()
            cp = _remote(thru[0].at[:, 1 - c], thru[1], sems[0], sems[1], (x, y, 1 - c))
            cp.wait_send()
            cp.wait_recv()

        (g5, recv), _, _ = _split_call(wait, f"pair_done_{name}_l{l}", list(grp["bufs"]), sems_in=list(grp["sems"]))
        p4 = _pair_sum(g5, recv, self.c_arr, f"pair_sum_{name}_l{l}")

        def start(thru, _, fresh, sems):
            x, y, c, chips = _place()
            for k, chip in enumerate(chips):
                _remote(thru[0].at[2 * chip[0] + chip[1]], fresh[0].at[k], sems[0].at[k], sems[1].at[k], (*chip, c)).start()

        (p4,), (recv3,), sems = _split_call(start, f"chips_start_{name}_l{l}", [p4], fresh=[((3, r // 2, cols), BF16)],
                                           new_sems=[(3,), (3,)], after_last=False)
        grp.update(stage=2, at=self.tick, bufs=(p4, recv3), sems=sems)

    def _chips(self, grp):
        l, name = grp["l"], grp["name"]

        def wait(thru, sems, _, __):
            x, y, c, chips = _place()
            for k, chip in enumerate(chips):
                cp = _remote(thru[0].at[2 * chip[0] + chip[1]], thru[1].at[k], sems[0].at[k], sems[1].at[k], (*chip, c))
                cp.wait_send()
                cp.wait_recv()

        (p4, recv3), _, _ = _split_call(wait, f"chips_done_{name}_l{l}", list(grp["bufs"]), sems_in=list(grp["sems"]))
        half = _chip_sum(p4, recv3, self.j_arr, self.c_arr, f"chip_sum_{name}_l{l}")

        def start(thru, _, __, sems):
            x, y, c, _chips = _place()
            _remote(thru[0].at[c], thru[0].at[c], sems[0], sems[1], (x, y, 1 - c)).start()

        (half,), _, sems = _split_call(start, f"join_start_{name}_l{l}", [half], new_sems=[(), ()], after_last=False)
        grp.update(stage=3, at=self.tick, bufs=(half,), sems=sems)

    def _update(self, grp):
        l, name = grp["l"], grp["name"]

        def wait(thru, sems, _, __):
            x, y, c, _chips = _place()
            _remote(thru[0].at[c], thru[0].at[c], sems[0], sems[1], (x, y, 1 - c)).wait_send()
            _remote(thru[0].at[1 - c], thru[0].at[1 - c], sems[0], sems[1], (x, y, c)).wait_recv()

        (full,), _, _ = _split_call(wait, f"join_done_{name}_l{l}", list(grp["bufs"]), sems_in=list(grp["sems"]))
        self.adam[name] = _adamw_layer(self.w[name], full.reshape(self.shard_shape[name]), self.m[name], self.v[name], l,
                                       self.adam.get(name), f"adamw_{name}_l{l}")
        grp.update(stage=4)

    def point(self, drain=False):
        self.tick += 1
        for grp in self.pending:
            if grp["stage"] == 3 and (drain or grp["at"] < self.tick):
                self._update(grp)
            elif grp["stage"] == 2 and (drain or grp["at"] + 2 <= self.tick):
                self._chips(grp)
            elif grp["stage"] == 1 and (drain or grp["at"] < self.tick):
                self._pair(grp)

    def finish(self):
        while any(grp["stage"] < 4 for grp in self.pending):
            self.point(drain=True)
        return self.adam

    @staticmethod
    def _peer(k, x, y, c):
        return (1 - x if k & 4 else x, 1 - y if k & 2 else y, 1 - c if k & 1 else c)

    def small_grads(self, l, grads, loss_tile):
        parts = [grads[nm] for nm in SMALL_NAMES] + ([loss_tile[0, 0:1]] if loss_tile is not None else [])
        packed = _pack_call(parts, f"small_pack_l{l}")
        rows = packed.shape[0]

        def start(thru, _, fresh, sems):
            x, y, c, _chips = _place()
            for k in range(1, 8):
                _remote(thru[0], fresh[0].at[4 * x + 2 * y + c], sems[0].at[k - 1], sems[1].at[k - 1],
                        self._peer(k, x, y, c)).start()

        (packed,), (landed,), sems = _split_call(start, f"small_start_l{l}", [packed], fresh=[((8, rows, PACK_LANES), F32)],
                                                 new_sems=[(7,), (7,)], after_last=False)
        self.small[l] =(packed, landed, sems, [p.shape for p in parts])

    def small_sum(self, l):
        packed, landed, sems, _shapes = self.small[l]

        def wait(thru, sems, _, __):
            x, y, c, _chips = _place()
            for k in range(1, 8):
                px, py, pc = self._peer(k, x, y, c)
                _remote(thru[0], thru[1].at[4 * x + 2 * y + c], sems[0].at[k - 1], sems[1].at[k - 1], (px, py, pc)).wait_send()
                _remote(thru[0], thru[1].at[4 * px + 2 * py + pc], sems[0].at[k - 1], sems[1].at[k - 1], (x, y, c)).wait_recv()

        (packed, landed), _, _ = _split_call(wait, f"small_done_l{l}", [packed, landed], sems_in=list(sems))
        return _sum_devices(packed, landed, self.me_arr, f"small_sum_l{l}")


def _rope_tables(s):
    inv_freq = ROPE_THETA ** (-jnp.arange(0, HEAD_DIM, 2, dtype=F32) / HEAD_DIM)
    ang = jnp.arange(s, dtype=F32)[:, None] * inv_freq[None, :]
    cos, sin = jnp.cos(ang), jnp.sin(ang)
    return jnp.concatenate([cos, cos], axis=-1), jnp.concatenate([-sin, sin], axis=-1)


def _local_step(x, target, ex, small):
    s = x.shape[0]
    cosf, sinf = _rope_tables(s)
    saved = []
    for l in range(DEPTH):
        p = small[l]
        t = f"l{l}"
        h = _rms_fwd(x, p["norm1_g"], f"norm1_{t}")
        z = _matmul(h, ex.weight(l, "w_in"), mode="nn", out_dtype=F32, tm=1024, tn=896, tk=2048, b_parts=4, name=f"proj_in_{t}")
        qn, kn, vb, ug, vn = _proj_post(z, p["q_norm_g"], p["k_norm_g"], p["sgu_ln_g"], p["sgu_ln_b"], cosf, sinf, f"proj_post_{t}")
        attn, sgu, mixed = _mixer_fwd(qn, kn, vb, ug, vn, p["w_s_bf16"], p["b_s_tile"], p["sink"], p["attn_out_g"],
                                      p["sgu_out_g"], f"mixer_{t}")
        x1 = _matmul(mixed, ex.weight(l, "w_o"), mode="nn", out_dtype=F32, tm=1024, tn=512, tk=2048, res=x, name=f"proj_out_{t}")
        h2 = _rms_fwd(x1, p["norm2_g"], f"norm2_{t}")
        a_pre = _matmul(h2, ex.weight(l, "w_up"), mode="nn", out_dtype=F32, tm=1024, tn=1408, tk=2048, b_parts=4, out_parts=2,
                        name=f"ffn_up_{t}")
        act, dgu = _conv_gate_fwd(a_pre, ex.conv_w(l), p["conv_b"], f"conv_gate_{t}")
        x2 = _matmul(act, ex.weight(l, "w_down"), mode="nn", out_dtype=F32, tm=512, tn=512, tk=D_FF, res=x1, name=f"ffn_down_{t}")
        saved.append(dict(x=x, h=h, z=z, qn=qn, kn=kn, vb=vb, ug=ug, vn=vn, attn=attn, sgu=sgu, mixed=mixed, x1=x1, h2=h2,
                          a_pre=a_pre, act=act, dgu=dgu))
        x = x2
    loss_tile, dx, dxb = _loss_head(x, target, "loss_head")
    for l in reversed(range(DEPTH)):
        p, sv = small[l], saved[l]
        t = f"l{l}"
        ex.grad(l, "w_down", _matmul(sv["act"], dxb, mode="tn", out_dtype=BF16, tm=512, tn=1024, tk=2048, name=f"g_w_down_{t}"))
        dact = _matmul(dxb, ex.weight(l, "w_down"), mode="nt", out_dtype=F32, tm=1024, tn=512, tk=2048, name=f"d_act_{t}")
        dap, dcw, dcb = _conv_gate_bwd(sv["a_pre"], sv["dgu"], ex.conv_w(l), dact, f"conv_gate_bwd_{t}")
        ex.point()
        ex.grad(l, "w_up", _matmul(sv["h2"], dap, mode="tn", out_dtype=BF16, tm=1024, tn=1408, tk=2048, b_parts=2, out_parts=4,
                                   name=f"g_w_up_{t}"))
        dh2 = _matmul_nt_slabs(dap, ex.weight(l, "w_up"), tm=512, tn=256, a_parts=2, name=f"d_h2_{t}")
        dx1, dx1b, dg2 = _rms_bwd(sv["x1"], p["norm2_g"], dh2, dx, f"norm2_bwd_{t}")
        ex.point()
        ex.grad(l, "w_o", _matmul(sv["mixed"], dx1b, mode="tn", out_dtype=BF16, tm=1024, tn=512, tk=2048, name=f"g_w_o_{t}"))
        dmixed = _matmul(dx1b, ex.weight(l, "w_o"), mode="nt", out_dtype=F32, tm=1024, tn=512, tk=2048, name=f"d_mixed_{t}")
        dqn, dkn, dvb, dug, dvn, dws, dbs, dsk, dga, dgs = _mixer_bwd(
            sv["qn"], sv["kn"], sv["vb"], sv["ug"], sv["vn"], sv["attn"], sv["sgu"], dmixed, p["w_s_bf16"], p["b_s_tile"],
            p["sink"], p["attn_out_g"], p["sgu_out_g"], f"mixer_bwd_{t}")
        dz, dqg, dkg, dlg, dlb = _proj_post_bwd(sv["z"], dqn, dkn, dvb, dug, dvn, p["q_norm_g"], p["k_norm_g"], p["sgu_ln_g"],
                                                 cosf, sinf, f"proj_post_bwd_{t}")
        ex.point()
        ex.grad(l, "w_in", _matmul(sv["h"], dz, mode="tn", out_dtype=BF16, tm=1024, tn=896, tk=2048, out_parts=4,
                                   name=f"g_w_in_{t}"))
        dh = _matmul_nt_slabs(dz, ex.weight(l, "w_in"), tm=1024, tn=512, name=f"d_h_{t}")
        dx, dxb, dg1 = _rms_bwd(sv["x"], p["norm1_g"], dh, dx1, f"norm1_bwd_{t}")
        ex.point()
        ex.small_grads(l, dict(
            norm1_g=dg1[0], q_norm_g=dqg[0], k_norm_g=dkg[0], sink=dsk[:, 0], sgu_ln_g=dlg[0], sgu_ln_b=dlb[0], w_s=dws,
            b_s=dbs[:, :, 0], attn_out_g=dga[0], sgu_out_g=dgs[0], norm2_g=dg2[0],
            conv_w=jnp.concatenate([dcw[0], dcw[1]], axis=-1), conv_b=jnp.concatenate([dcb[0, 0], dcb[1, 0]], axis=-1)),
            loss_tile if l == 0 else None)
    return dx


def _small_views(l, norm1_g, q_norm_g, k_norm_g, sink, sgu_ln_g, sgu_ln_b, w_s, b_s, attn_out_g, sgu_out_g, norm2_g, conv_b):
    return dict(
        norm1_g=norm1_g[l][None], q_norm_g=q_norm_g[l][None], k_norm_g=k_norm_g[l][None], sink=sink[l],
        sgu_ln_g=sgu_ln_g[l][None], sgu_ln_b=sgu_ln_b[l][None], w_s_bf16=w_s[l].astype(BF16),
        b_s_tile=jnp.broadcast_to(b_s[l][:, :, None], (N_GMLP_HEADS, BLOCK, BLOCK)), attn_out_g=attn_out_g[l][None],
        sgu_out_g=sgu_out_g[l][None], norm2_g=norm2_g[l][None], conv_b=conv_b[l][None])


SMALL_NAMES = ("norm1_g", "q_norm_g", "k_norm_g", "sink", "sgu_ln_g", "sgu_ln_b", "w_s", "b_s", "attn_out_g", "sgu_out_g",
               "norm2_g", "conv_b", "conv_w")
REPLICATED_NAMES = SMALL_NAMES[:-1]
BIG_NAMES = ("w_in", "w_o", "w_up", "w_down")
PACK_LANES = 128
PACK_ALIGN = 8 * PACK_LANES


def _pack_rows(shape):
    return -(-math.prod(shape) // PACK_ALIGN) * 8


def _pack_parts(arrays):
    parts = []
    for a in arrays:
        flat = a.reshape(-1)
        parts.append(jnp.pad(flat, (0, _pack_rows(a.shape) * PACK_LANES - flat.shape[0])).reshape(-1, PACK_LANES))
    return parts


def _pack(arrays):
    return jnp.concatenate(_pack_parts(arrays), axis=0)


def _pack_call(arrays, name):
    parts = _pack_parts(arrays)
    total = sum(p.shape[0] for p in parts)

    def body(*refs):
        o_ref, at = refs[-1], 0
        for p_ref in refs[:-1]:
            o_ref[at:at + p_ref.shape[0], :] = p_ref[...]
            at += p_ref.shape[0]

    vm = pl.BlockSpec(memory_space=pltpu.VMEM)
    return _ordered_call(
        body, name=name, out_shape=jax.ShapeDtypeStruct((total, PACK_LANES), F32), in_specs=[vm] * len(parts), out_specs=vm,
        compiler_params=pltpu.CompilerParams(vmem_limit_bytes=V7X_VMEM_LIMIT),
    )(*parts)


def _unpack_layers(stacked, shapes):
    nl = stacked.shape[0]
    out, at = [], 0
    for shp in shapes:
        rows = _pack_rows(shp)
        out.append(stacked[:, at:at + rows].reshape(nl, -1)[:, :math.prod(shp)].reshape((nl,) + tuple(shp)))
        at += rows
    return out


def _adamw_packed(w, g, m, v, rows, layer, into, name):
    head = pl.BlockSpec((rows, PACK_LANES), lambda i: (0, 0))
    at_layer = pl.BlockSpec((None, rows, PACK_LANES), lambda i: (layer, 0, 0))

    def body(w_ref, g_ref, m_ref, v_ref, *rest):
        d_ref, nm_ref, nv_ref = rest[-3:]
        gv = g_ref[...]
        mn = ADAM_B1 * m_ref[...] + (1.0 - ADAM_B1) * gv
        vn = ADAM_B2 * v_ref[...] + (1.0 - ADAM_B2) * (gv * gv)
        m_hat = mn / (1.0 - ADAM_B1 ** ADAM_STEP)
        v_hat = vn / (1.0 - ADAM_B2 ** ADAM_STEP)
        d_ref[...] = -ADAM_LR * (m_hat / (jnp.sqrt(v_hat) + ADAM_EPS) + ADAM_WD * w_ref[...])
        nm_ref[...] = mn
        nv_ref[...] = vn

    in_specs = [head] * 4
    operands = [w, g, m, v]
    aliases = {}
    if into is not None:
        in_specs += [ANY] * 3
        operands += list(into)
        aliases = {4 + i: i for i in range(3)}
    sds = jax.ShapeDtypeStruct((DEPTH, rows, PACK_LANES), F32)
    return _ordered_call(
        body, name=name, out_shape=(sds,) * 3, grid=(1,), in_specs=in_specs, out_specs=(at_layer,) * 3,
        input_output_aliases=aliases, compiler_params=_params(("arbitrary",)),
    )(*operands)


def kernel(x, norm1_g, w_in, q_norm_g, k_norm_g, sink, sgu_ln_g, sgu_ln_b, w_s, b_s, attn_out_g, sgu_out_g, w_o, norm2_g, w_up, conv_w, conv_b, w_down, loss_target, m_norm1_g, m_w_in, m_q_norm_g, m_k_norm_g, m_sink, m_sgu_ln_g, m_sgu_ln_b, m_w_s, m_b_s, m_attn_out_g, m_sgu_out_g, m_w_o, m_norm2_g, m_w_up, m_conv_w, m_conv_b, m_w_down, v_norm1_g, v_w_in, v_q_norm_g, v_k_norm_g, v_sink, v_sgu_ln_g, v_sgu_ln_b, v_w_s, v_b_s, v_attn_out_g, v_sgu_out_g, v_w_o, v_norm2_g, v_w_up, v_conv_w, v_conv_b, v_w_down):
    weights = dict(norm1_g=norm1_g, w_in=w_in, q_norm_g=q_norm_g, k_norm_g=k_norm_g, sink=sink, sgu_ln_g=sgu_ln_g,
                   sgu_ln_b=sgu_ln_b, w_s=w_s, b_s=b_s, attn_out_g=attn_out_g, sgu_out_g=sgu_out_g, w_o=w_o, norm2_g=norm2_g,
                   w_up=w_up, conv_w=conv_w, conv_b=conv_b, w_down=w_down)
    m_in = dict(norm1_g=m_norm1_g, w_in=m_w_in, q_norm_g=m_q_norm_g, k_norm_g=m_k_norm_g, sink=m_sink, sgu_ln_g=m_sgu_ln_g,
                sgu_ln_b=m_sgu_ln_b, w_s=m_w_s, b_s=m_b_s, attn_out_g=m_attn_out_g, sgu_out_g=m_sgu_out_g, w_o=m_w_o,
                norm2_g=m_norm2_g, w_up=m_w_up, conv_w=m_conv_w, conv_b=m_conv_b, w_down=m_w_down)
    v_in = dict(norm1_g=v_norm1_g, w_in=v_w_in, q_norm_g=v_q_norm_g, k_norm_g=v_k_norm_g, sink=v_sink, sgu_ln_g=v_sgu_ln_g,
                sgu_ln_b=v_sgu_ln_b, w_s=v_w_s, b_s=v_b_s, attn_out_g=v_attn_out_g, sgu_out_g=v_sgu_out_g, w_o=v_w_o,
                norm2_g=v_norm2_g, w_up=v_w_up, conv_w=v_conv_w, conv_b=v_conv_b, w_down=v_w_down)
    cx, cy, cc = lax.axis_index("x"), lax.axis_index("y"), lax.axis_index("c")
    j_me = 2 * cx + cy
    c_arr = jnp.reshape(cc, (1,)).astype(jnp.int32)
    j_arr = jnp.reshape(j_me, (1,)).astype(jnp.int32)

    _Order.last = None
    ex = _Exchange(weights, m_in, v_in, j_arr, c_arr, jnp.reshape(4 * cx + 2 * cy + cc, (1,)).astype(jnp.int32))
    small = [_small_views(l, norm1_g, q_norm_g, k_norm_g, sink, sgu_ln_g, sgu_ln_b, w_s, b_s, attn_out_g, sgu_out_g, norm2_g,
                          conv_b) for l in range(DEPTH)]
    packed_in = [[_pack_call([src[nm][l] for nm in REPLICATED_NAMES], f"pack_{tag}_l{l}")
                  for tag, src in (("w", weights), ("m", m_in), ("v", v_in))] for l in range(DEPTH)]
    dx = _local_step(x[0], loss_target[0], ex, small)
    big_out = ex.finish()

    rep_shapes = [weights[nm].shape[1:] for nm in REPLICATED_NAMES]
    rep_rows = sum(_pack_rows(shp) for shp in rep_shapes)
    cw_shape = (3, 2 * D_FF)
    sums, adam_small = [None] * DEPTH, None
    for l in reversed(range(DEPTH)):
        sums[l] = ex.small_sum(l)
        pw, pm, pv = packed_in[l]
        adam_small = _adamw_packed(pw, sums[l], pm, pv, rep_rows, l, adam_small, f"adamw_small_l{l}")
    cw_rows = _pack_rows(cw_shape)
    loss = sums[0][rep_rows + cw_rows, 0]
    stacked = jnp.stack([sm[:rep_rows + cw_rows] for sm in sums])
    grads = dict(zip(REPLICATED_NAMES, _unpack_layers(stacked[:, :rep_rows], rep_shapes)))
    delta, new_m, new_v = (dict(zip(REPLICATED_NAMES, _unpack_layers(arr, rep_shapes))) for arr in adam_small)
    cw_cols = 2 * D_FF // N_CHIPS
    cw_grad = lax.dynamic_slice_in_dim(_unpack_layers(stacked[:, rep_rows:], [cw_shape])[0], j_me * cw_cols, cw_cols, axis=2)
    flat = lambda a: a.reshape(DEPTH * 3, cw_cols)
    cw_out = _adamw(flat(conv_w), flat(cw_grad), flat(m_conv_w), flat(v_conv_w), "adamw_conv_w")
    grads["conv_w"], delta["conv_w"], new_m["conv_w"], new_v["conv_w"] = (a.reshape(DEPTH, 3, cw_cols) for a in cw_out)

    for name in BIG_NAMES:
        grads[name], delta[name], new_m[name], new_v[name] = big_out[name]

    order = ("norm1_g", "w_in", "q_norm_g", "k_norm_g", "sink", "sgu_ln_g", "sgu_ln_b", "w_s", "b_s", "attn_out_g", "sgu_out_g",
             "w_o", "norm2_g", "w_up", "conv_w", "conv_b", "w_down")
    return (loss, dx[None], *[grads[nm] for nm in order], *[delta[nm] for nm in order], *[new_m[nm] for nm in order],
            *[new_v[nm] for nm in order])
```

```python
import functools
import math

import jax
import jax.numpy as jnp
from jax import lax
from jax.experimental import pallas as pl
from jax.experimental.pallas import tpu as pltpu

F32 = jnp.float32
BF16 = jnp.bfloat16

D_MODEL = 2048
HEAD_DIM = 128
ATTN_WIDTH = 1024
N_Q_HEADS = 8
N_KV_HEADS = 2
GQA_GROUP = 4
KV_WIDTH = 256
GMLP_WIDTH = 1024
N_GMLP_HEADS = 8
BLOCK = 128
IN_WIDTH = 3584
D_FF = 5632
DEPTH = 2
EPS = 1e-6
MASK_VALUE = -1e30
ROPE_THETA = 10000.0
N_CHIPS = 4

ADAM_LR = 0.001
ADAM_B1 = 0.9
ADAM_B2 = 0.999
ADAM_EPS = 1e-08
ADAM_WD = 0.01
ADAM_STEP = 10

V7X_VMEM_LIMIT = 48 * 1024 * 1024
MESH = pl.DeviceIdType.MESH

_GELU_C = math.sqrt(2.0 / math.pi)
_GELU_A = 0.044715


def _params(sem=None):
    return pltpu.CompilerParams(dimension_semantics=sem, vmem_limit_bytes=V7X_VMEM_LIMIT)


ANY = pl.BlockSpec(memory_space=pl.ANY)


class _Order:
    last = None


def _ordered_call(body, *, token_index=0, **kw):
    def run(*operands):
        tok = _Order.last
        if tok is None or any(op is tok for op in operands):
            call = pl.pallas_call(body, **kw)
        else:
            n_in = len(operands)

            def ordered_body(*refs):
                return body(*refs[:n_in], *refs[n_in + 1:])

            kw2 = dict(kw)
            if "grid_spec" in kw2:
                gs = kw2["grid_spec"]
                kw2["grid_spec"] = pltpu.PrefetchScalarGridSpec(
                    num_scalar_prefetch=gs.num_scalar_prefetch, grid=gs.grid, in_specs=list(gs.in_specs) + [ANY],
                    out_specs=gs.out_specs, scratch_shapes=gs.scratch_shapes)
            else:
                kw2["in_specs"] = list(kw2["in_specs"]) + [ANY]
            call = pl.pallas_call(ordered_body, **kw2)
            operands = operands + (tok,)
        out = call(*operands)
        _Order.last = out[token_index] if isinstance(out, (tuple, list)) else out
        return out

    return run


def _gelu(x):
    return x * (0.5 * (1.0 + jnp.tanh(_GELU_C * (x + _GELU_A * (x * x * x)))))


def _gelu_grad(x):
    x2 = x * x
    t = jnp.tanh(_GELU_C * (x + _GELU_A * (x * x2)))
    return 0.5 * (1.0 + t) + 0.5 * x * (1.0 - t * t) * (_GELU_C * (1.0 + 3.0 * _GELU_A * x2))


def _mean_last(x):
    return jnp.mean(x, axis=-1, keepdims=True)


def _sum_rows(x):
    return jnp.sum(x, axis=0, keepdims=True)


def _sum_all(x):
    return jnp.sum(jnp.sum(x, axis=1, keepdims=True), axis=0, keepdims=True)


def _matmul(a, b, *, mode, out_dtype, tm, tn, tk, name, res=None, a_parts=0, b_parts=0, out_parts=0, b_lead=()):
    b_full = b
    b = jax.ShapeDtypeStruct(b.shape[len(b_lead):], b.dtype)
    if mode == "nn":
        assert not a_parts
        m, k = a.shape
        n = b.shape[0] * b.shape[2] if b_parts else b.shape[1]
    elif mode == "nt":
        m, k = (a.shape[1], a.shape[0] * a.shape[2]) if a_parts else a.shape
        n = b.shape[1] if b_parts else b.shape[0]
    else:
        assert not a_parts
        k, m = a.shape
        n = b.shape[0] * b.shape[2] if b_parts else b.shape[1]
    tm, tn, tk = min(tm, m), min(tn, n), min(tk, k)
    assert m % tm == 0 and n % tn == 0 and k % tk == 0, (name, m, n, k, tm, tn, tk)
    nm, nn, nk = m // tm, n // tn, k // tk

    def slab(idx, total_tiles, parts):
        per = total_tiles // parts
        assert per * parts == total_tiles, (name, total_tiles, parts)
        return idx // per, idx % per

    if mode == "nn":
        a_spec = pl.BlockSpec((tm, tk), lambda i, j, kk: (i, kk))
        if b_parts:
            b_spec = pl.BlockSpec((None, tk, tn), lambda i, j, kk: (slab(j, nn, b_parts)[0], kk, slab(j, nn, b_parts)[1]))
        else:
            b_spec = pl.BlockSpec((tk, tn), lambda i, j, kk: (kk, j))
        dims = (((1,), (0,)), ((), ()))
    elif mode == "nt":
        if a_parts:
            a_spec = pl.BlockSpec((None, tm, tk), lambda i, j, kk: (slab(kk, nk, a_parts)[0], i, slab(kk, nk, a_parts)[1]))
        else:
            a_spec = pl.BlockSpec((tm, tk), lambda i, j, kk: (i, kk))
        if b_parts:
            b_spec = pl.BlockSpec((None, tn, tk), lambda i, j, kk: (slab(kk, nk, b_parts)[0], j, slab(kk, nk, b_parts)[1]))
        else:
            b_spec = pl.BlockSpec((tn, tk), lambda i, j, kk: (j, kk))
        dims = (((1,), (1,)), ((), ()))
    else:
        a_spec = pl.BlockSpec((tk, tm), lambda i, j, kk: (kk, i))
        if b_parts:
            b_spec = pl.BlockSpec((None, tk, tn), lambda i, j, kk: (slab(j, nn, b_parts)[0], kk, slab(j, nn, b_parts)[1]))
        else:
            b_spec = pl.BlockSpec((tk, tn), lambda i, j, kk: (kk, j))
        dims = (((0,), (0,)), ((), ()))
    if out_parts:
        out_shape = jax.ShapeDtypeStruct((out_parts, m, n // out_parts), out_dtype)
        out_spec = pl.BlockSpec((None, tm, tn), lambda i, j, kk: (slab(j, nn, out_parts)[0], i, slab(j, nn, out_parts)[1]))
    else:
        out_shape = jax.ShapeDtypeStruct((m, n), out_dtype)
        out_spec = pl.BlockSpec((tm, tn), lambda i, j, kk: (i, j))
    if b_lead:
        inner_map = b_spec.index_map
        b_spec = pl.BlockSpec((None,) * len(b_lead) + tuple(b_spec.block_shape),
                              lambda i, j, kk: tuple(b_lead) + tuple(inner_map(i, j, kk)))
    in_specs = [a_spec, b_spec]
    operands = [a, b_full]
    if res is not None:
        in_specs.append(pl.BlockSpec((tm, tn), lambda i, j, kk: (i, j)))
        operands.append(res)

    def body(*refs):
        a_ref, b_ref = refs[0], refs[1]
        res_ref = refs[2] if res is not None else None
        o_ref = refs[3] if res is not None else refs[2]
        p = lax.dot_general(a_ref[...], b_ref[...], dims, preferred_element_type=F32)

        def finish(total):
            if res_ref is not None:
                total = res_ref[...] + total
            o_ref[...] = total.astype(out_dtype)

        if nk == 1:
            finish(p)
        else:
            acc_ref = refs[-1]
            kk = pl.program_id(2)

            @pl.when(kk == 0)
            def _():
                acc_ref[...] = p

            @pl.when(jnp.logical_and(kk > 0, kk < nk - 1))
            def _():
                acc_ref[...] += p

            @pl.when(kk == nk - 1)
            def _():
                finish(acc_ref[...] + p)

    scratch = [pltpu.VMEM((tm, tn), F32)] if nk > 1 else []
    return _ordered_call(
        body, name=name, out_shape=out_shape, grid=(nm, nn, nk), in_specs=in_specs, out_specs=out_spec,
        scratch_shapes=scratch, compiler_params=_params(("parallel", "parallel", "arbitrary")),
    )(*operands)


def _matmul_nt_slabs(a, b, *, tm, tn, name, a_parts=0):
    nslab, n, ks = b.shape
    m = a.shape[1] if a_parts else a.shape[0]
    tm, tn = min(tm, m), min(tn, n)
    assert m % tm == 0 and n % tn == 0, (name, m, n, tm, tn)
    if a_parts:
        per = nslab // a_parts
        assert per * a_parts == nslab and a.shape[2] == per * ks, (name, a.shape, b.shape)
        a_spec = pl.BlockSpec((a_parts, tm, per * ks), lambda i, j: (0, i, 0))
    else:
        assert a.shape[1] == nslab * ks, (name, a.shape, b.shape)
        a_spec = pl.BlockSpec((tm, nslab * ks), lambda i, j: (i, 0))

    def body(a_ref, b_ref, o_ref):
        total = None
        for sl in range(nslab):
            if a_parts:
                a_sl = a_ref[sl // per, :, (sl % per) * ks:(sl % per + 1) * ks]
            else:
                a_sl = a_ref[:, sl * ks:(sl + 1) * ks]
            p = lax.dot_general(a_sl, b_ref[sl], (((1,), (1,)), ((), ())), preferred_element_type=F32)
            total = p if total is None else total + p
        o_ref[...] = total

    return _ordered_call(
        body, name=name, out_shape=jax.ShapeDtypeStruct((m, n), F32), grid=(m // tm, n // tn),
        in_specs=[a_spec, pl.BlockSpec((nslab, tn, ks), lambda i, j: (0, j, 0))],
        out_specs=pl.BlockSpec((tm, tn), lambda i, j: (i, j)), compiler_params=_params(("parallel", "parallel")),
    )(a, b)


def _row_tile(s):
    return min(256, s)


def _rows(width, tr):
    return pl.BlockSpec((tr, width), lambda i: (i, 0))


def _const2(shape):
    return pl.BlockSpec(shape, lambda i: (0, 0))


def _rms_fwd(x, g, name):
    s, d = x.shape
    tr = _row_tile(s)

    def body(x_ref, g_ref, o_ref):
        xv = x_ref[...]
        r = lax.rsqrt(_mean_last(xv * xv) + EPS)
        o_ref[...] = (xv * r * g_ref[...]).astype(BF16)

    return _ordered_call(
        body, name=name, out_shape=jax.ShapeDtypeStruct((s, d), BF16), grid=(s // tr,),
        in_specs=[_rows(d, tr), _const2((1, d))], out_specs=_rows(d, tr), compiler_params=_params(("parallel",)),
    )(x, g)


def _rms_bwd(x, g, dh, dres, name):
    s, d = x.shape
    tr = _row_tile(s)

    def body(x_ref, g_ref, dh_ref, dres_ref, dx_ref, dxb_ref, dg_ref):
        xv, dy = x_ref[...], dh_ref[...]
        r = lax.rsqrt(_mean_last(xv * xv) + EPS)
        gdy = dy * g_ref[...]
        dx = dres_ref[...] + r * gdy - xv * ((r * r * r) * _mean_last(xv * gdy))
        dx_ref[...] = dx
        dxb_ref[...] = dx.astype(BF16)

        @pl.when(pl.program_id(0) == 0)
        def _():
            dg_ref[...] = jnp.zeros_like(dg_ref)

        dg_ref[...] += _sum_rows(xv * r * dy)

    return _ordered_call(
        body, name=name,
        out_shape=(jax.ShapeDtypeStruct((s, d), F32), jax.ShapeDtypeStruct((s, d), BF16), jax.ShapeDtypeStruct((1, d), F32)),
        grid=(s // tr,), in_specs=[_rows(d, tr), _const2((1, d)), _rows(d, tr), _rows(d, tr)],
        out_specs=(_rows(d, tr), _rows(d, tr), _const2((1, d))), compiler_params=_params(("arbitrary",)),
    )(x, g, dh, dres)


Q0, K0, V0, GU0, GV0 = 0, ATTN_WIDTH, ATTN_WIDTH + KV_WIDTH, ATTN_WIDTH + 2 * KV_WIDTH, ATTN_WIDTH + 2 * KV_WIDTH + GMLP_WIDTH


def _head(h, base=0):
    return slice(base + h * HEAD_DIM, base + (h + 1) * HEAD_DIM)


def _proj_post(z, qg, kg, lg, lb, cosf, sinf, name):
    s = z.shape[0]
    tr = _row_tile(s)

    def body(z_ref, qg_ref, kg_ref, lg_ref, lb_ref, cos_ref, sin_ref, qn_ref, kn_ref, vb_ref, ug_ref, vn_ref):
        cos, sin = cos_ref[...], sin_ref[...]

        def norm_rope(xh, g):
            y = xh * lax.rsqrt(_mean_last(xh * xh) + EPS) * g
            return y * cos + pltpu.roll(y, HEAD_DIM // 2, 1) * sin

        for h in range(N_Q_HEADS):
            qn_ref[:, _head(h)] = norm_rope(z_ref[:, _head(h, Q0)].astype(F32), qg_ref[...]).astype(BF16)
        for h in range(N_KV_HEADS):
            kn_ref[:, _head(h)] = norm_rope(z_ref[:, _head(h, K0)].astype(F32), kg_ref[...]).astype(BF16)
        vb_ref[...] = z_ref[:, V0:GU0]
        ug_ref[...] = _gelu(z_ref[:, GU0:GV0].astype(F32))
        vg = _gelu(z_ref[:, GV0:IN_WIDTH].astype(F32))
        xc = vg - _mean_last(vg)
        y = xc * lax.rsqrt(_mean_last(xc * xc) + EPS)
        vn_ref[...] = (y * lg_ref[...] + lb_ref[...]).astype(BF16)

    return _ordered_call(
        body, name=name,
        out_shape=(jax.ShapeDtypeStruct((s, ATTN_WIDTH), BF16), jax.ShapeDtypeStruct((s, KV_WIDTH), BF16),
                   jax.ShapeDtypeStruct((s, KV_WIDTH), BF16), jax.ShapeDtypeStruct((s, GMLP_WIDTH), F32),
                   jax.ShapeDtypeStruct((s, GMLP_WIDTH), BF16)),
        grid=(s // tr,),
        in_specs=[_rows(IN_WIDTH, tr), _const2((1, HEAD_DIM)), _const2((1, HEAD_DIM)), _const2((1, GMLP_WIDTH)),
                  _const2((1, GMLP_WIDTH)), _rows(HEAD_DIM, tr), _rows(HEAD_DIM, tr)],
        out_specs=(_rows(ATTN_WIDTH, tr), _rows(KV_WIDTH, tr), _rows(KV_WIDTH, tr), _rows(GMLP_WIDTH, tr), _rows(GMLP_WIDTH, tr)),
        compiler_params=_params(("parallel",)),
    )(z, qg, kg, lg, lb, cosf, sinf)


def _proj_post_bwd(z, dqn, dkn, dvb, dug, dvn, qg, kg, lg, cosf, sinf, name):
    s = z.shape[0]
    tr = _row_tile(s)

    def body(z_ref, dqn_ref, dkn_ref, dvb_ref, dug_ref, dvn_ref, qg_ref, kg_ref, lg_ref, cos_ref, sin_ref,
             dz_ref, dqg_ref, dkg_ref, dlg_ref, dlb_ref):
        cos, sin = cos_ref[...], sin_ref[...]

        @pl.when(pl.program_id(0) == 0)
        def _():
            dqg_ref[...] = jnp.zeros_like(dqg_ref)
            dkg_ref[...] = jnp.zeros_like(dkg_ref)
            dlg_ref[...] = jnp.zeros_like(dlg_ref)
            dlb_ref[...] = jnp.zeros_like(dlb_ref)

        def norm_rope_bwd(xh, g, dout):
            dy = dout * cos - pltpu.roll(dout, HEAD_DIM // 2, 1) * sin
            r = lax.rsqrt(_mean_last(xh * xh) + EPS)
            xhat = xh * r
            gdy = dy * g
            return r * (gdy - xhat * _mean_last(xhat * gdy)), _sum_rows(xhat * dy)

        dqg = jnp.zeros((1, HEAD_DIM), F32)
        for h in range(N_Q_HEADS):
            dx, dg = norm_rope_bwd(z_ref[:, _head(h, Q0)].astype(F32), qg_ref[...], dqn_ref[:, _head(h)])
            dz_ref[:, _head(h, Q0)] = dx.astype(BF16)
            dqg = dqg + dg
        dqg_ref[...] += dqg
        dkg = jnp.zeros((1, HEAD_DIM), F32)
        for h in range(N_KV_HEADS):
            dx, dg = norm_rope_bwd(z_ref[:, _head(h, K0)].astype(F32), kg_ref[...], dkn_ref[:, _head(h)])
            dz_ref[:, _head(h, K0)] = dx.astype(BF16)
            dkg = dkg + dg
        dkg_ref[...] += dkg
        dz_ref[:, V0:GU0] = dvb_ref[...].astype(BF16)
        dz_ref[:, GU0:GV0] = (dug_ref[...] * _gelu_grad(z_ref[:, GU0:GV0].astype(F32))).astype(BF16)
        gv = z_ref[:, GV0:IN_WIDTH].astype(F32)
        vg = _gelu(gv)
        xc = vg - _mean_last(vg)
        r = lax.rsqrt(_mean_last(xc * xc) + EPS)
        xhat = xc * r
        dvn_v = dvn_ref[...]
        dlg_ref[...] += _sum_rows(xhat * dvn_v)
        dlb_ref[...] += _sum_rows(dvn_v)
        dxh = dvn_v * lg_ref[...]
        dvg = r * (dxh - _mean_last(dxh) - xhat * _mean_last(dxh * xhat))
        dz_ref[:, GV0:IN_WIDTH] = (dvg * _gelu_grad(gv)).astype(BF16)

    return _ordered_call(
        body, name=name,
        out_shape=(jax.ShapeDtypeStruct((s, IN_WIDTH), BF16), jax.ShapeDtypeStruct((1, HEAD_DIM), F32),
                   jax.ShapeDtypeStruct((1, HEAD_DIM), F32), jax.ShapeDtypeStruct((1, GMLP_WIDTH), F32),
                   jax.ShapeDtypeStruct((1, GMLP_WIDTH), F32)),
        grid=(s // tr,),
        in_specs=[_rows(IN_WIDTH, tr), _rows(ATTN_WIDTH, tr), _rows(KV_WIDTH, tr), _rows(KV_WIDTH, tr), _rows(GMLP_WIDTH, tr),
                  _rows(GMLP_WIDTH, tr), _const2((1, HEAD_DIM)), _const2((1, HEAD_DIM)), _const2((1, GMLP_WIDTH)),
                  _rows(HEAD_DIM, tr), _rows(HEAD_DIM, tr)],
        out_specs=(_rows(IN_WIDTH, tr), _const2((1, HEAD_DIM)), _const2((1, HEAD_DIM)), _const2((1, GMLP_WIDTH)),
                   _const2((1, GMLP_WIDTH))),
        compiler_params=_params(("arbitrary",)),
    )(z, dqn, dkn, dvb, dug, dvn, qg, kg, lg, cosf, sinf)


def _band_valid(n, s):
    i = lax.broadcasted_iota(jnp.int32, (BLOCK, 3 * BLOCK), 0)
    j = lax.broadcasted_iota(jnp.int32, (BLOCK, 3 * BLOCK), 1)
    k_pos = n * BLOCK - BLOCK + j
    return (jnp.abs(j - BLOCK - i) <= BLOCK) & (k_pos >= 0) & (k_pos < s)


def _probs(q, kb, sink_h, valid):
    sc = lax.dot_general(q, kb, (((1,), (1,)), ((), ())), preferred_element_type=F32) * (HEAD_DIM ** -0.5)
    sc = jnp.where(valid, sc, MASK_VALUE)
    m = jnp.maximum(jnp.max(sc, axis=-1, keepdims=True), sink_h)
    p = jnp.exp(sc - m)
    es = jnp.exp(sink_h - m)
    den = jnp.sum(p, axis=-1, keepdims=True) + es
    return p / den, es / den


def _band_specs(width, nb):
    return [pl.BlockSpec((BLOCK, width), lambda n: (jnp.maximum(n - 1, 0), 0)),
            pl.BlockSpec((BLOCK, width), lambda n: (n, 0)),
            pl.BlockSpec((BLOCK, width), lambda n: (jnp.minimum(n + 1, nb - 1), 0))]


def _blk(width):
    return pl.BlockSpec((BLOCK, width), lambda n: (n, 0))


def _whole3(shape):
    return pl.BlockSpec(shape, lambda n: (0, 0, 0))


def _smem():
    return pl.BlockSpec(memory_space=pltpu.SMEM)


def _mixer_fwd(qn, kn, vb, ug, vn, wsb, bsb, sink, ga, gs, name):
    s = qn.shape[0]
    nb = s // BLOCK

    def body(sink_ref, q_ref, kp_ref, kc_ref, kx_ref, vp_ref, vc_ref, vx_ref, ug_ref, vn_ref, ws_ref, bs_ref, ga_ref, gs_ref,
             attn_ref, sgu_ref, mix_ref):
        n = pl.program_id(0)
        valid = _band_valid(n, s)
        ssq = jnp.zeros((BLOCK, 1), F32)
        for kh in range(N_KV_HEADS):
            kb = jnp.concatenate([kp_ref[:, _head(kh)], kc_ref[:, _head(kh)], kx_ref[:, _head(kh)]], axis=0)
            vbd = jnp.concatenate([vp_ref[:, _head(kh)], vc_ref[:, _head(kh)], vx_ref[:, _head(kh)]], axis=0)
            for g in range(GQA_GROUP):
                h = kh * GQA_GROUP + g
                p, _ = _probs(q_ref[:, _head(h)], kb, sink_ref[h], valid)
                o = jnp.dot(p.astype(BF16), vbd, preferred_element_type=F32)
                attn_ref[:, _head(h)] = o
                ssq = ssq + jnp.sum(o * o, axis=-1, keepdims=True)
        r = lax.rsqrt(ssq * (1.0 / ATTN_WIDTH) + EPS)
        mix_ref[:, 0:ATTN_WIDTH] = (attn_ref[...] * r * ga_ref[...]).astype(BF16)
        ssq = jnp.zeros((BLOCK, 1), F32)
        for h in range(N_GMLP_HEADS):
            f = jnp.dot(ws_ref[h], vn_ref[:, _head(h)], preferred_element_type=F32) + bs_ref[h]
            o = ug_ref[:, _head(h)] * f
            sgu_ref[:, _head(h)] = o
            ssq = ssq + jnp.sum(o * o, axis=-1, keepdims=True)
        r = lax.rsqrt(ssq * (1.0 / GMLP_WIDTH) + EPS)
        mix_ref[:, ATTN_WIDTH:D_MODEL] = (sgu_ref[...] * r * gs_ref[...]).astype(BF16)

    hh = (N_GMLP_HEADS, BLOCK, BLOCK)
    return _ordered_call(
        body, name=name,
        out_shape=(jax.ShapeDtypeStruct((s, ATTN_WIDTH), F32), jax.ShapeDtypeStruct((s, GMLP_WIDTH), F32),
                   jax.ShapeDtypeStruct((s, D_MODEL), BF16)),
        grid=(nb,),
        in_specs=[_smem(), _blk(ATTN_WIDTH)] + _band_specs(KV_WIDTH, nb) + _band_specs(KV_WIDTH, nb)
        + [_blk(GMLP_WIDTH), _blk(GMLP_WIDTH), _whole3(hh), _whole3(hh),
           pl.BlockSpec((1, ATTN_WIDTH), lambda n: (0, 0)), pl.BlockSpec((1, GMLP_WIDTH), lambda n: (0, 0))],
        out_specs=(_blk(ATTN_WIDTH), _blk(GMLP_WIDTH), _blk(D_MODEL)),
        compiler_params=_params(("parallel",)),
    )(sink, qn, kn, kn, kn, vb, vb, vb, ug, vn, wsb, bsb, ga, gs)


def _mixer_bwd(qn, kn, vb, ug, vn, attn, sgu, dmixed, wsb, bsb, sink, ga, gs, name):
    s = qn.shape[0]
    nb = s // BLOCK
    tn_dims = (((0,), (0,)), ((), ()))
    nt_dims = (((1,), (1,)), ((), ()))

    def body(sink_ref, q_ref, kp_ref, kc_ref, kx_ref, vp_ref, vc_ref, vx_ref, ug_ref, vn_ref, attn_ref, sgu_ref, dm_ref,
             ws_ref, bs_ref, ga_ref, gs_ref,
             dq_ref, dk_ref, dv_ref, dug_ref, dvn_ref, dws_ref, dbs_ref, dsk_ref, dga_ref, dgs_ref, dk_acc, dv_acc):
        n = pl.program_id(0)

        @pl.when(n == 0)
        def _():
            for ref in (dk_acc, dv_acc, dws_ref, dbs_ref, dsk_ref, dga_ref, dgs_ref):
                ref[...] = jnp.zeros_like(ref)

        def out_norm_bwd(o, g, dy):
            r = lax.rsqrt(_mean_last(o * o) + EPS)
            gdy = dy * g
            return r * gdy - o * ((r * r * r) * _mean_last(o * gdy)), _sum_rows(o * r * dy)

        d_attn, dga = out_norm_bwd(attn_ref[...], ga_ref[...], dm_ref[:, 0:ATTN_WIDTH])
        dga_ref[...] += dga
        d_sgu, dgs = out_norm_bwd(sgu_ref[...], gs_ref[...], dm_ref[:, ATTN_WIDTH:D_MODEL])
        dgs_ref[...] += dgs

        for h in range(N_GMLP_HEADS):
            vn_h = vn_ref[:, _head(h)]
            f = jnp.dot(ws_ref[h], vn_h, preferred_element_type=F32) + bs_ref[h]
            ds_h = d_sgu[:, _head(h)]
            dug_ref[:, _head(h)] = ds_h * f
            df = ds_h * ug_ref[:, _head(h)]
            dfb = df.astype(BF16)
            dvn_ref[:, _head(h)] = lax.dot_general(ws_ref[h], dfb, tn_dims, preferred_element_type=F32)
            dws_ref[h] += lax.dot_general(dfb, vn_h, nt_dims, preferred_element_type=F32)
            dbs_ref[h] += jnp.broadcast_to(jnp.sum(df, axis=-1, keepdims=True), (BLOCK, BLOCK))

        valid = _band_valid(n, s)
        row0 = pl.multiple_of(n * BLOCK, BLOCK)
        for kh in range(N_KV_HEADS):
            kb = jnp.concatenate([kp_ref[:, _head(kh)], kc_ref[:, _head(kh)], kx_ref[:, _head(kh)]], axis=0)
            vbd = jnp.concatenate([vp_ref[:, _head(kh)], vc_ref[:, _head(kh)], vx_ref[:, _head(kh)]], axis=0)
            dkb = jnp.zeros((3 * BLOCK, HEAD_DIM), F32)
            dvb = jnp.zeros((3 * BLOCK, HEAD_DIM), F32)
            for g in range(GQA_GROUP):
                h = kh * GQA_GROUP + g
                q = q_ref[:, _head(h)]
                p, p_sink = _probs(q, kb, sink_ref[h], valid)
                do = d_attn[:, _head(h)].astype(BF16)
                dp = lax.dot_general(do, vbd, nt_dims, preferred_element_type=F32)
                delta = jnp.sum(p * dp, axis=-1, keepdims=True)
                dsc = (p * (dp - delta) * (HEAD_DIM ** -0.5)).astype(BF16)
                dsk_ref[h:h + 1, :] += jnp.broadcast_to(_sum_all(-(p_sink * delta)), (1, BLOCK))
                dq_ref[:, _head(h)] = jnp.dot(dsc, kb, preferred_element_type=F32)
                dkb = dkb + lax.dot_general(dsc, q, tn_dims, preferred_element_type=F32)
                dvb = dvb + lax.dot_general(p.astype(BF16), do, tn_dims, preferred_element_type=F32)
            dk_acc[pl.ds(row0, 3 * BLOCK), _head(kh)] += dkb
            dv_acc[pl.ds(row0, 3 * BLOCK), _head(kh)] += dvb

        @pl.when(n == nb - 1)
        def _():
            dk_ref[...] = dk_acc[BLOCK:BLOCK + s, :]
            dv_ref[...] = dv_acc[BLOCK:BLOCK + s, :]

    hh = (N_GMLP_HEADS, BLOCK, BLOCK)
    full_kv = pl.BlockSpec((s, KV_WIDTH), lambda n: (0, 0))
    return _ordered_call(
        body, name=name,
        out_shape=(jax.ShapeDtypeStruct((s, ATTN_WIDTH), F32), jax.ShapeDtypeStruct((s, KV_WIDTH), F32),
                   jax.ShapeDtypeStruct((s, KV_WIDTH), F32), jax.ShapeDtypeStruct((s, GMLP_WIDTH), F32),
                   jax.ShapeDtypeStruct((s, GMLP_WIDTH), F32), jax.ShapeDtypeStruct(hh, F32), jax.ShapeDtypeStruct(hh, F32),
                   jax.ShapeDtypeStruct((N_Q_HEADS, BLOCK), F32), jax.ShapeDtypeStruct((1, ATTN_WIDTH), F32),
                   jax.ShapeDtypeStruct((1, GMLP_WIDTH), F32)),
        grid=(nb,),
        in_specs=[_smem(), _blk(ATTN_WIDTH)] + _band_specs(KV_WIDTH, nb) + _band_specs(KV_WIDTH, nb)
        + [_blk(GMLP_WIDTH), _blk(GMLP_WIDTH), _blk(ATTN_WIDTH), _blk(GMLP_WIDTH), _blk(D_MODEL), _whole3(hh), _whole3(hh),
           pl.BlockSpec((1, ATTN_WIDTH), lambda n: (0, 0)), pl.BlockSpec((1, GMLP_WIDTH), lambda n: (0, 0))],
        out_specs=(_blk(ATTN_WIDTH), full_kv, full_kv, _blk(GMLP_WIDTH), _blk(GMLP_WIDTH), _whole3(hh), _whole3(hh),
                   pl.BlockSpec((N_Q_HEADS, BLOCK), lambda n: (0, 0)), pl.BlockSpec((1, ATTN_WIDTH), lambda n: (0, 0)),
                   pl.BlockSpec((1, GMLP_WIDTH), lambda n: (0, 0))),
        scratch_shapes=[pltpu.VMEM((s + 2 * BLOCK, KV_WIDTH), F32), pltpu.VMEM((s + 2 * BLOCK, KV_WIDTH), F32)],
        compiler_params=_params(("arbitrary",)),
    )(sink, qn, kn, kn, kn, vb, vb, vb, ug, vn, attn, sgu, dmixed, wsb, bsb, ga, gs)


CONV_TILE = 128


PAD_ROWS = 8


def _zero_pad_rows(pad_ref):
    s = pad_ref.shape[0] - 2 * PAD_ROWS
    zeros = jnp.zeros((PAD_ROWS, pad_ref.shape[1]), F32)
    pad_ref[0:PAD_ROWS, :] = zeros
    pad_ref[PAD_ROWS + s:2 * PAD_ROWS + s, :] = zeros


def _shift_rows(a, pad_ref):
    s = a.shape[0]
    pad_ref[PAD_ROWS:PAD_ROWS + s, :] = a
    padded = pad_ref[...]
    prev = pltpu.roll(padded, 1, 0)[PAD_ROWS:PAD_ROWS + s]
    nxt = pltpu.roll(padded, s + 2 * PAD_ROWS - 1, 0)[PAD_ROWS:PAD_ROWS + s]
    return prev, nxt


def _conv_specs(s):
    tc = CONV_TILE
    nj = D_FF // tc
    return (tc, nj, pl.BlockSpec((2, s, tc), lambda j: (0, 0, j)),
            [pl.BlockSpec((3, tc), lambda j: (0, j)), pl.BlockSpec((3, tc), lambda j: (0, j + nj))],
            [pl.BlockSpec((1, tc), lambda j: (0, j)), pl.BlockSpec((1, tc), lambda j: (0, j + nj))])


def _conv_gate_fwd(a_pre, cw, cb, name):
    s = a_pre.shape[1]
    tc, nj, a_spec, w_specs, b_specs = _conv_specs(s)

    def body(a_ref, wg_ref, wu_ref, bg_ref, bu_ref, act_ref, dgu_ref, pad_ref):
        _zero_pad_rows(pad_ref)

        def conv(a, w_ref, b_ref):
            prev, nxt = _shift_rows(a, pad_ref)
            return b_ref[...] + prev * w_ref[0:1, :] + a * w_ref[1:2, :] + nxt * w_ref[2:3, :]

        g = conv(a_ref[0].astype(F32), wg_ref, bg_ref)
        u = conv(a_ref[1].astype(F32), wu_ref, bu_ref)
        sg = 1.0 / (1.0 + jnp.exp(-g))
        silu = g * sg
        act_ref[...] = (silu * u).astype(BF16)
        dgu_ref[0] = (u * (sg * (1.0 + g * (1.0 - sg)))).astype(BF16)
        dgu_ref[1] = silu.astype(BF16)

    return _ordered_call(
        body, name=name, out_shape=(jax.ShapeDtypeStruct((s, D_FF), BF16), jax.ShapeDtypeStruct((2, s, D_FF), BF16)),
        grid=(nj,), in_specs=[a_spec] + w_specs + b_specs,
        out_specs=(pl.BlockSpec((s, tc), lambda j: (0, j)), pl.BlockSpec((2, s, tc), lambda j: (0, 0, j))),
        scratch_shapes=[pltpu.VMEM((s + 2 * PAD_ROWS, tc), F32)], compiler_params=_params(("parallel",)),
    )(a_pre, cw, cw, cb, cb)


def _conv_gate_bwd(a_pre, dgu, cw, dact, name):
    s = a_pre.shape[1]
    tc, nj, a_spec, w_specs, _ = _conv_specs(s)

    def body(a_ref, dgu_ref, wg_ref, wu_ref, dact_ref, dap_ref, dcw_ref, dcb_ref, pad_ref):
        _zero_pad_rows(pad_ref)
        dact_v = dact_ref[...].astype(F32)
        for part, w_ref in enumerate((wg_ref, wu_ref)):
            da = dact_v * dgu_ref[part].astype(F32)
            a = a_ref[part].astype(F32)
            prev, nxt = _shift_rows(a, pad_ref)
            dcw_ref[part, 0:1, :] = _sum_rows(prev * da)
            dcw_ref[part, 1:2, :] = _sum_rows(a * da)
            dcw_ref[part, 2:3, :] = _sum_rows(nxt * da)
            dcb_ref[part] = _sum_rows(da)
            da_prev, da_next = _shift_rows(da, pad_ref)
            dap_ref[part] = (da_next * w_ref[0:1, :] + da * w_ref[1:2, :] + da_prev * w_ref[2:3, :]).astype(BF16)

    return _ordered_call(
        body, name=name,
        out_shape=(jax.ShapeDtypeStruct((2, s, D_FF), BF16), jax.ShapeDtypeStruct((2, 3, D_FF), F32),
                   jax.ShapeDtypeStruct((2, 1, D_FF), F32)),
        grid=(nj,),
        in_specs=[a_spec, pl.BlockSpec((2, s, tc), lambda j: (0, 0, j))] + w_specs + [pl.BlockSpec((s, tc), lambda j: (0, j))],
        out_specs=(pl.BlockSpec((2, s, tc), lambda j: (0, 0, j)), pl.BlockSpec((2, 3, tc), lambda j: (0, 0, j)),
                   pl.BlockSpec((2, 1, tc), lambda j: (0, 0, j))),
        scratch_shapes=[pltpu.VMEM((s + 2 * PAD_ROWS, tc), F32)], compiler_params=_params(("parallel",)),
    )(a_pre, dgu, cw, cw, dact)


def _loss_head(y, target, name):
    s, d = y.shape
    tr = _row_tile(s)

    def body(y_ref, t_ref, loss_ref, dy_ref, dyb_ref):
        err = y_ref[...] - t_ref[...]

        @pl.when(pl.program_id(0) == 0)
        def _():
            loss_ref[...] = jnp.zeros_like(loss_ref)

        loss_ref[...] += jnp.broadcast_to(0.5 * _sum_all(_mean_last(err * err)), (8, 128))
        dy = err * (1.0 / d)
        dy_ref[...] = dy
        dyb_ref[...] = dy.astype(BF16)

    return _ordered_call(
        body, name=name,
        out_shape=(jax.ShapeDtypeStruct((8, 128), F32), jax.ShapeDtypeStruct((s, d), F32), jax.ShapeDtypeStruct((s, d), BF16)),
        grid=(s // tr,), in_specs=[_rows(d, tr), _rows(d, tr)],
        out_specs=(_const2((8, 128)), _rows(d, tr), _rows(d, tr)), compiler_params=_params(("arbitrary",)),
    )(y, target)


def _row_block(rows, cols, budget=1 << 20):
    if rows * cols <= budget:
        return rows
    best = None
    for tr in range(16, rows, 16):
        if rows % tr == 0 and tr * cols <= budget:
            best = tr
    assert best is not None, (rows, cols)
    return best


def _place_shard(x4, layer, j_arr, out_dtype, name):
    _, nh, r, cols = x4.shape
    tr = _row_block(r, cols)

    def body(j_ref, x_ref, o_ref):
        o_ref[...] = x_ref[...].astype(out_dtype)

    grid_spec = pltpu.PrefetchScalarGridSpec(
        num_scalar_prefetch=1, grid=(nh, r // tr),
        in_specs=[pl.BlockSpec((None, None, tr, cols), lambda h, i, j_ref: (layer, h, i, 0))],
        out_specs=pl.BlockSpec((None, None, tr, cols), lambda h, i, j_ref: (j_ref[0], h, i, 0)))
    return _ordered_call(
        body, name=name, out_shape=jax.ShapeDtypeStruct((N_CHIPS, nh, r, cols), out_dtype), grid_spec=grid_spec,
        compiler_params=_params(("parallel", "parallel")),
    )(j_arr, x4)


def _adamw(w, g, m, v, name, budget=1 << 18):
    rows, cols = w.shape
    tr = _row_block(rows, cols, budget)

    def body(w_ref, g_ref, m_ref, v_ref, go_ref, d_ref, nm_ref, nv_ref):
        gv = g_ref[...]
        go_ref[...] = gv
        mn = ADAM_B1 * m_ref[...] + (1.0 - ADAM_B1) * gv
        vn = ADAM_B2 * v_ref[...] + (1.0 - ADAM_B2) * (gv * gv)
        m_hat = mn / (1.0 - ADAM_B1 ** ADAM_STEP)
        v_hat = vn / (1.0 - ADAM_B2 ** ADAM_STEP)
        d_ref[...] = -ADAM_LR * (m_hat / (jnp.sqrt(v_hat) + ADAM_EPS) + ADAM_WD * w_ref[...])
        nm_ref[...] = mn
        nv_ref[...] = vn

    sds = jax.ShapeDtypeStruct((rows, cols), F32)
    return _ordered_call(
        body, name=name, out_shape=(sds, sds, sds, sds), grid=(rows // tr,),
        in_specs=[_rows(cols, tr)] * 4, out_specs=(_rows(cols, tr),) * 4, compiler_params=_params(("parallel",)),
    )(w, g, m, v)


def _pair_sum(g5, recv, c_arr, name):
    _, _, rh, cols = g5.shape
    tr = _row_block(rh, cols)

    def body(c_ref, g_ref, r_ref, o_ref):
        o_ref[...] = (g_ref[...].astype(F32) + r_ref[...].astype(F32)).astype(BF16)

    grid_spec = pltpu.PrefetchScalarGridSpec(
        num_scalar_prefetch=1, grid=(N_CHIPS, rh // tr),
        in_specs=[pl.BlockSpec((None, None, tr, cols), lambda j, i, c_ref: (j, c_ref[0], i, 0)),
                  pl.BlockSpec((None, tr, cols), lambda j, i, c_ref: (j, i, 0))],
        out_specs=pl.BlockSpec((None, tr, cols), lambda j, i, c_ref: (j, i, 0)))
    return _ordered_call(
        body, name=name, out_shape=jax.ShapeDtypeStruct((N_CHIPS, rh, cols), BF16), grid_spec=grid_spec,
        compiler_params=_params(("parallel", "parallel")),
    )(c_arr, g5, recv)


def _chip_sum(p4, recv3, j_arr, c_arr, name):
    _, rh, cols = p4.shape
    tr = _row_block(rh, cols, 1 << 19)

    def body(j_ref, c_ref, p_ref, r_ref, o_ref):
        total = p_ref[...].astype(F32)
        for peer in range(3):
            total = total + r_ref[peer].astype(F32)
        o_ref[...] = total

    grid_spec = pltpu.PrefetchScalarGridSpec(
        num_scalar_prefetch=2, grid=(rh // tr,),
        in_specs=[pl.BlockSpec((None, tr, cols), lambda i, j_ref, c_ref: (j_ref[0], i, 0)),
                  pl.BlockSpec((3, tr, cols), lambda i, j_ref, c_ref: (0, i, 0))],
        out_specs=pl.BlockSpec((None, tr, cols), lambda i, j_ref, c_ref: (c_ref[0], i, 0)))
    return _ordered_call(
        body, name=name, out_shape=jax.ShapeDtypeStruct((2, rh, cols), F32), grid_spec=grid_spec,
        compiler_params=_params(("parallel",)),
    )(j_arr, c_arr, p4, recv3)


def _adamw_layer(w, g, m, v, layer, into, name):
    nl, rows, cols = w.shape
    tr = _row_block(rows, cols, 1 << 18)
    at_layer = pl.BlockSpec((None, tr, cols), lambda i: (layer, i, 0))

    def body(w_ref, g_ref, m_ref, v_ref, *rest):
        go_ref, d_ref, nm_ref, nv_ref = rest[-4:]
        gv = g_ref[...]
        go_ref[...] = gv
        mn = ADAM_B1 * m_ref[...] + (1.0 - ADAM_B1) * gv
        vn = ADAM_B2 * v_ref[...] + (1.0 - ADAM_B2) * (gv * gv)
        m_hat = mn / (1.0 - ADAM_B1 ** ADAM_STEP)
        v_hat = vn / (1.0 - ADAM_B2 ** ADAM_STEP)
        d_ref[...] = -ADAM_LR * (m_hat / (jnp.sqrt(v_hat) + ADAM_EPS) + ADAM_WD * w_ref[...])
        nm_ref[...] = mn
        nv_ref[...] = vn

    in_specs = [at_layer, _rows(cols, tr), at_layer, at_layer]
    operands = [w, g, m, v]
    aliases = {}
    if into is not None:
        in_specs += [ANY] * 4
        operands += list(into)
        aliases = {4 + i: i for i in range(4)}
    sds = jax.ShapeDtypeStruct((nl, rows, cols), F32)
    return _ordered_call(
        body, name=name, out_shape=(sds,) * 4, grid=(rows // tr,), in_specs=in_specs, out_specs=(at_layer,) * 4,
        input_output_aliases=aliases, compiler_params=_params(("parallel",)),
    )(*operands)


def _sum_devices(mine, landed, me_arr, name):
    rows, lanes = mine.shape

    def body(me_ref, mine_ref, landed_ref, o_ref):
        total = None
        for dev in range(8):
            part = jnp.where(me_ref[0] == dev, mine_ref[...], landed_ref[dev])
            total = part if total is None else total + part
        o_ref[...] = total

    grid_spec = pltpu.PrefetchScalarGridSpec(
        num_scalar_prefetch=1, grid=(1,),
        in_specs=[pl.BlockSpec((rows, lanes), lambda i, me_ref: (0, 0)), pl.BlockSpec((8, rows, lanes), lambda i, me_ref: (0, 0, 0))],
        out_specs=pl.BlockSpec((rows, lanes), lambda i, me_ref: (0, 0)))
    return _ordered_call(
        body, name=name, out_shape=jax.ShapeDtypeStruct((rows, lanes), F32), grid_spec=grid_spec,
        compiler_params=_params(("arbitrary",)),
    )(me_arr, mine, landed)


def _place():
    x, y, c = lax.axis_index("x"), lax.axis_index("y"), lax.axis_index("c")
    chips = [(1 - x, y), (x, 1 - y), (1 - x, 1 - y)]
    return x, y, c, chips


HBM = pl.BlockSpec(memory_space=pltpu.HBM)
SEM = pl.BlockSpec(memory_space=pltpu.SEMAPHORE)
TOKEN = jax.ShapeDtypeStruct((8, 128), F32)


def _remote(src, dst, send_sem, recv_sem, to):
    return pltpu.make_async_remote_copy(src_ref=src, dst_ref=dst, send_sem=send_sem, recv_sem=recv_sem, device_id=to,
                                        device_id_type=MESH)


def _split_call(body, name, thru, sems_in=(), fresh=(), new_sems=(), after_last=True):
    n_t, n_s, n_f = len(thru), len(sems_in), len(fresh)

    def call_body(*refs):
        outs = refs[n_t + n_s:]
        body(refs[:n_t], refs[n_t:n_t + n_s], outs[1 + n_t:1 + n_t + n_f], outs[1 + n_t + n_f:])
        outs[0][...] = jnp.zeros_like(outs[0])

    out_shape = ([TOKEN] + [pltpu.HBM(t.shape, t.dtype) for t in thru] + [pltpu.HBM(shp, dt) for shp, dt in fresh]
                 + [pltpu.SemaphoreType.DMA(shp) for shp in new_sems])
    out_specs = [pl.BlockSpec(memory_space=pltpu.VMEM)] + [HBM] * (n_t + n_f) + [SEM] * len(new_sems)
    if not after_last:
        _Order.last = None
    out = _ordered_call(
        call_body, name=name, out_shape=tuple(out_shape), in_specs=[HBM] * n_t + [SEM] * n_s, out_specs=tuple(out_specs),
        input_output_aliases={i: 1 + i for i in range(n_t)},
        compiler_params=pltpu.CompilerParams(has_side_effects=pltpu.SideEffectType.DATAFLOW_SIDE_EFFECTING),
    )(*[pltpu.with_memory_space_constraint(t, pltpu.HBM) for t in thru], *sems_in)
    return out[1:1 + n_t], out[1 + n_t:1 + n_t + n_f], out[1 + n_t + n_f:]


class _Exchange:
    def __init__(self, weights, m_in, v_in, j_arr, c_arr, me_arr):
        self.w, self.m, self.v = weights, m_in, v_in
        self.j_arr, self.c_arr, self.me_arr = j_arr, c_arr, me_arr
        self.adam, self.small = {}, {}
        self.groups = [(l, name) for l in range(DEPTH) for name in BIG_NAMES]
        self.shard_shape = {name: weights[name].shape[1:] for name in BIG_NAMES}
        self.conv_state, self.state = [], {}
        self.ready, self.conv_ready = {}, {}
        self.pending, self.tick, self.reduced = [], 0, {}

        def place(grp):
            l, name = grp
            nl, r, cols = weights[name].shape
            return _place_shard(weights[name].reshape(nl, 2, r // 2, cols), l, j_arr, BF16, f"place_{name}_l{l}")

        def start_copies(tag, convs, groups, bufs):
            n_c = len(convs)

            def start(thru, _, __, sems):
                x, y, c, chips = _place()
                j_me = 2 * x + y
                copies = []
                for i in range(len(thru)):
                    mine = thru[i].at[j_me] if i < n_c else thru[i].at[j_me, c]
                    copies += [_remote(mine, mine, sems[2 * i].at[k], sems[2 * i + 1].at[k], (*chip, c))
                               for k, chip in enumerate(chips)]
                for cp in copies:
                    cp.start()

            thru, _, sems = _split_call(start, tag, convs + bufs, new_sems=[(3,)] * (2 * (n_c + len(bufs))))
            self.conv_state += [(thru[i], sems[2 * i], sems[2 * i + 1]) for i in range(n_c)]
            for g, grp in enumerate(groups):
                self.state[grp] = (thru[n_c + g], sems[2 * (n_c + g)], sems[2 * (n_c + g) + 1])

        convs = [_place_shard(weights["conv_w"][:, None], l, j_arr, F32, f"place_conv_w_l{l}") for l in range(DEPTH)]
        start_copies("gather_start_first", convs, self.groups[:1], [place(self.groups[0])])
        start_copies("gather_start_rest", [], self.groups[1:], [place(grp) for grp in self.groups[1:]])

    def conv_w(self, l):
        if l not in self.conv_ready:
            buf, send, recv = self.conv_state[l]

            def wait(thru, sems, _, __):
                x, y, c, chips = _place()
                for k, chip in enumerate(chips):
                    mine, theirs = thru[0].at[2 * x + y], thru[0].at[2 * chip[0] + chip[1]]
                    _remote(mine, mine, sems[0].at[k], sems[1].at[k], (*chip, c)).wait_send()
                    _remote(theirs, theirs, sems[0].at[k], sems[1].at[k], (x, y, c)).wait_recv()

            (buf,), _, _ = _split_call(wait, f"gather_conv_w_l{l}", [buf], sems_in=[send, recv])
            self.conv_ready[l] = jnp.transpose(buf[:, 0], (1, 0, 2)).reshape(3, 2 * D_FF)
        return self.conv_ready[l]

    def weight(self, l, name):
        grp = (l, name)
        if grp not in self.ready:
            buf, send, recv = self.state[grp]

            def forward(thru, sems, _, new):
                x, y, c, chips = _place()
                for k, chip in enumerate(chips):
                    landed = thru[0].at[2 * chip[0] + chip[1], c]
                    _remote(landed, landed, new[0].at[k], sems[0].at[k], (x, y, c)).wait_recv()
                    _remote(landed, landed, new[0].at[k], new[1].at[k], (x, y, 1 - c)).start()

            (buf,), _, (fsend, frecv) = _split_call(forward, f"gather_pass_{name}_l{l}", [buf], sems_in=[recv],
                                                    new_sems=[(3,), (3,)])

            def finish(thru, sems, _, __):
                x, y, c, chips = _place()
                mine = thru[0].at[2 * x + y, c]
                for k, chip in enumerate(chips):
                    j_k = 2 * chip[0] + chip[1]
                    theirs, landed = thru[0].at[j_k, 1 - c], thru[0].at[j_k, c]
                    _remote(theirs, theirs, sems[1].at[k], sems[2].at[k], (x, y, c)).wait_recv()
                    _remote(landed, landed, sems[1].at[k], sems[2].at[k], (x, y, 1 - c)).wait_send()
                    _remote(mine, mine, sems[0].at[k], sems[2].at[k], (*chip, c)).wait_send()

            (buf,), _, _ = _split_call(finish, f"gather_done_{name}_l{l}", [buf], sems_in=[send, fsend, frecv])
            r, cols = self.shard_shape[name]
            self.ready[grp] = buf.reshape(N_CHIPS, r, cols) if name in ("w_in", "w_up") else buf.reshape(N_CHIPS * r, cols)
        return self.ready[grp]

    def grad(self, l, name, g):
        r, cols = self.shard_shape[name]
        g5 = g.reshape(N_CHIPS, 2, r // 2, cols)

        def start(thru, _, fresh, sems):
            x, y, c, _chips = _place()
            _remote(thru[0].at[:, 1 - c], fresh[0], sems[0], sems[1], (x, y, 1 - c)).start()

        (g5,), (recv,), sems = _split_call(start, f"pair_start_{name}_l{l}", [g5], fresh=[((N_CHIPS, r // 2, cols), BF16)],
                                          new_sems=[(), ()], after_last=False)
        self.pending.append(dict(l=l, name=name, stage=1, at=self.tick, bufs=(g5, recv), sems=sems))

    def _pair(self, grp):
        l, name = grp["l"], grp["name"]
        r, cols = self.shard_shape[name]

        def wait(thru, sems, _, __):
            x, y, c, _chips = _place()
            cp = _remote(thru[0].at[:, 1 - c], thru[1], sems[0], sems[1], (x, y, 1 - c))
            cp.wait_send()
            cp.wait_recv()

        (g5, recv), _, _ = _split_call(wait, f"pair_done_{name}_l{l}", list(grp["bufs"]), sems_in=list(grp["sems"]))
        p4 = _pair_sum(g5, recv, self.c_arr, f"pair_sum_{name}_l{l}")

        def start(thru, _, fresh, sems):
            x, y, c, chips = _place()
            for k, chip in enumerate(chips):
                _remote(thru[0].at[2 * chip[0] + chip[1]], fresh[0].at[k], sems[0].at[k], sems[1].at[k], (*chip, c)).start()

        (p4,), (recv3,), sems = _split_call(start, f"chips_start_{name}_l{l}", [p4], fresh=[((3, r // 2, cols), BF16)],
                                           new_sems=[(3,), (3,)], after_last=False)
        grp.update(stage=2, at=self.tick, bufs=(p4, recv3), sems=sems)

    def _chips(self, grp):
        l, name = grp["l"], grp["name"]

        def wait(thru, sems, _, __):
            x, y, c, chips = _place()
            for k, chip in enumerate(chips):
                cp = _remote(thru[0].at[2 * chip[0] + chip[1]], thru[1].at[k], sems[0].at[k], sems[1].at[k], (*chip, c))
                cp.wait_send()
                cp.wait_recv()

        (p4, recv3), _, _ = _split_call(wait, f"chips_done_{name}_l{l}", list(grp["bufs"]), sems_in=list(grp["sems"]))
        half = _chip_sum(p4, recv3, self.j_arr, self.c_arr, f"chip_sum_{name}_l{l}")

        def start(thru, _, __, sems):
            x, y, c, _chips = _place()
            _remote(thru[0].at[c], thru[0].at[c], sems[0], sems[1], (x, y, 1 - c)).start()

        (half,), _, sems = _split_call(start, f"join_start_{name}_l{l}", [half], new_sems=[(), ()], after_last=False)
        grp.update(stage=3, at=self.tick, bufs=(half,), sems=sems)

    def _update(self, grp):
        l, name = grp["l"], grp["name"]

        def wait(thru, sems, _, __):
            x, y, c, _chips = _place()
            _remote(thru[0].at[c], thru[0].at[c], sems[0], sems[1], (x, y, 1 - c)).wait_send()
            _remote(thru[0].at[1 - c], thru[0].at[1 - c], sems[0], sems[1], (x, y, c)).wait_recv()

        (full,), _, _ = _split_call(wait, f"join_done_{name}_l{l}", list(grp["bufs"]), sems_in=list(grp["sems"]))
        self.adam[name] = _adamw_layer(self.w[name], full.reshape(self.shard_shape[name]), self.m[name], self.v[name], l,
                                       self.adam.get(name), f"adamw_{name}_l{l}")
        grp.update(stage=4)

    def point(self, drain=False):
        self.tick += 1
        for grp in self.pending:
            if grp["stage"] == 3 and (drain or grp["at"] < self.tick):
                self._update(grp)
            elif grp["stage"] == 2 and (drain or grp["at"] + 2 <= self.tick):
                self._chips(grp)
            elif grp["stage"] == 1 and (drain or grp["at"] < self.tick):
                self._pair(grp)

    def finish(self):
        while any(grp["stage"] < 4 for grp in self.pending):
            self.point(drain=True)
        return self.adam

    @staticmethod
    def _peer(k, x, y, c):
        return (1 - x if k & 4 else x, 1 - y if k & 2 else y, 1 - c if k & 1 else c)

    def small_grads(self, l, grads, loss_tile):
        parts = [grads[nm] for nm in SMALL_NAMES] + ([loss_tile[0, 0:1]] if loss_tile is not None else [])
        packed = _pack_call(parts, f"small_pack_l{l}")
        rows = packed.shape[0]

        def start(thru, _, fresh, sems):
            x, y, c, _chips = _place()
            for k in range(1, 8):
                _remote(thru[0], fresh[0].at[4 * x + 2 * y + c], sems[0].at[k - 1], sems[1].at[k - 1],
                        self._peer(k, x, y, c)).start()

        (packed,), (landed,), sems = _split_call(start, f"small_start_l{l}", [packed], fresh=[((8, rows, PACK_LANES), F32)],
                                                 new_sems=[(7,), (7,)], after_last=False)
        self.small[l] =(packed, landed, sems, [p.shape for p in parts])

    def small_sum(self, l):
        packed, landed, sems, _shapes = self.small[l]

        def wait(thru, sems, _, __):
            x, y, c, _chips = _place()
            for k in range(1, 8):
                px, py, pc = self._peer(k, x, y, c)
                _remote(thru[0], thru[1].at[4 * x + 2 * y + c], sems[0].at[k - 1], sems[1].at[k - 1], (px, py, pc)).wait_send()
                _remote(thru[0], thru[1].at[4 * px + 2 * py + pc], sems[0].at[k - 1], sems[1].at[k - 1], (x, y, c)).wait_recv()

        (packed, landed), _, _ = _split_call(wait, f"small_done_l{l}", [packed, landed], sems_in=list(sems))
        return _sum_devices(packed, landed, self.me_arr, f"small_sum_l{l}")


def _rope_tables(s):
    inv_freq = ROPE_THETA ** (-jnp.arange(0, HEAD_DIM, 2, dtype=F32) / HEAD_DIM)
    ang = jnp.arange(s, dtype=F32)[:, None] * inv_freq[None, :]
    cos, sin = jnp.cos(ang), jnp.sin(ang)
    return jnp.concatenate([cos, cos], axis=-1), jnp.concatenate([-sin, sin], axis=-1)


def _local_step(x, target, ex, small):
    s = x.shape[0]
    cosf, sinf = _rope_tables(s)
    saved = []
    for l in range(DEPTH):
        p = small[l]
        t = f"l{l}"
        h = _rms_fwd(x, p["norm1_g"], f"norm1_{t}")
        z = _matmul(h, ex.weight(l, "w_in"), mode="nn", out_dtype=BF16, tm=1024, tn=896, tk=2048, b_parts=4, name=f"proj_in_{t}")
        qn, kn, vb, ug, vn = _proj_post(z, p["q_norm_g"], p["k_norm_g"], p["sgu_ln_g"], p["sgu_ln_b"], cosf, sinf, f"proj_post_{t}")
        attn, sgu, mixed = _mixer_fwd(qn, kn, vb, ug, vn, p["w_s_bf16"], p["b_s_tile"], p["sink"], p["attn_out_g"],
                                      p["sgu_out_g"], f"mixer_{t}")
        x1 = _matmul(mixed, ex.weight(l, "w_o"), mode="nn", out_dtype=F32, tm=1024, tn=512, tk=2048, res=x, name=f"proj_out_{t}")
        h2 = _rms_fwd(x1, p["norm2_g"], f"norm2_{t}")
        a_pre = _matmul(h2, ex.weight(l, "w_up"), mode="nn", out_dtype=BF16, tm=1024, tn=1408, tk=2048, b_parts=4, out_parts=2,
                        name=f"ffn_up_{t}")
        act, dgu = _conv_gate_fwd(a_pre, ex.conv_w(l), p["conv_b"], f"conv_gate_{t}")
        x2 = _matmul(act, ex.weight(l, "w_down"), mode="nn", out_dtype=F32, tm=512, tn=512, tk=D_FF, res=x1, name=f"ffn_down_{t}")
        saved.append(dict(x=x, h=h, z=z, qn=qn, kn=kn, vb=vb, ug=ug, vn=vn, attn=attn, sgu=sgu, mixed=mixed, x1=x1, h2=h2,
                          a_pre=a_pre, act=act, dgu=dgu))
        x = x2
    loss_tile, dx, dxb = _loss_head(x, target, "loss_head")
    for l in reversed(range(DEPTH)):
        p, sv = small[l], saved[l]
        t = f"l{l}"
        ex.grad(l, "w_down", _matmul(sv["act"], dxb, mode="tn", out_dtype=BF16, tm=512, tn=1024, tk=2048, name=f"g_w_down_{t}"))
        dact = _matmul(dxb, ex.weight(l, "w_down"), mode="nt", out_dtype=BF16, tm=1024, tn=512, tk=2048, name=f"d_act_{t}")
        dap, dcw, dcb = _conv_gate_bwd(sv["a_pre"], sv["dgu"], ex.conv_w(l), dact, f"conv_gate_bwd_{t}")
        ex.point()
        ex.grad(l, "w_up", _matmul(sv["h2"], dap, mode="tn", out_dtype=BF16, tm=1024, tn=1408, tk=2048, b_parts=2, out_parts=4,
                                   name=f"g_w_up_{t}"))
        dh2 = _matmul_nt_slabs(dap, ex.weight(l, "w_up"), tm=512, tn=256, a_parts=2, name=f"d_h2_{t}")
        dx1, dx1b, dg2 = _rms_bwd(sv["x1"], p["norm2_g"], dh2, dx, f"norm2_bwd_{t}")
        ex.point()
        ex.grad(l, "w_o", _matmul(sv["mixed"], dx1b, mode="tn", out_dtype=BF16, tm=1024, tn=512, tk=2048, name=f"g_w_o_{t}"))
        dmixed = _matmul(dx1b, ex.weight(l, "w_o"), mode="nt", out_dtype=F32, tm=1024, tn=512, tk=2048, name=f"d_mixed_{t}")
        dqn, dkn, dvb, dug, dvn, dws, dbs, dsk, dga, dgs = _mixer_bwd(
            sv["qn"], sv["kn"], sv["vb"], sv["ug"], sv["vn"], sv["attn"], sv["sgu"], dmixed, p["w_s_bf16"], p["b_s_tile"],
            p["sink"], p["attn_out_g"], p["sgu_out_g"], f"mixer_bwd_{t}")
        dz, dqg, dkg, dlg, dlb = _proj_post_bwd(sv["z"], dqn, dkn, dvb, dug, dvn, p["q_norm_g"], p["k_norm_g"], p["sgu_ln_g"],
                                                 cosf, sinf, f"proj_post_bwd_{t}")
        ex.point()
        ex.grad(l, "w_in", _matmul(sv["h"], dz, mode="tn", out_dtype=BF16, tm=1024, tn=896, tk=2048, out_parts=4,
                                   name=f"g_w_in_{t}"))
        dh = _matmul_nt_slabs(dz, ex.weight(l, "w_in"), tm=1024, tn=512, name=f"d_h_{t}")
        dx, dxb, dg1 = _rms_bwd(sv["x"], p["norm1_g"], dh, dx1, f"norm1_bwd_{t}")
        ex.point()
        ex.small_grads(l, dict(
            norm1_g=dg1[0], q_norm_g=dqg[0], k_norm_g=dkg[0], sink=dsk[:, 0], sgu_ln_g=dlg[0], sgu_ln_b=dlb[0], w_s=dws,
            b_s=dbs[:, :, 0], attn_out_g=dga[0], sgu_out_g=dgs[0], norm2_g=dg2[0],
            conv_w=jnp.concatenate([dcw[0], dcw[1]], axis=-1), conv_b=jnp.concatenate([dcb[0, 0], dcb[1, 0]], axis=-1)),
            loss_tile if l == 0 else None)
    return dx


def _small_views(l, norm1_g, q_norm_g, k_norm_g, sink, sgu_ln_g, sgu_ln_b, w_s, b_s, attn_out_g, sgu_out_g, norm2_g, conv_b):
    return dict(
        norm1_g=norm1_g[l][None], q_norm_g=q_norm_g[l][None], k_norm_g=k_norm_g[l][None], sink=sink[l],
        sgu_ln_g=sgu_ln_g[l][None], sgu_ln_b=sgu_ln_b[l][None], w_s_bf16=w_s[l].astype(BF16),
        b_s_tile=jnp.broadcast_to(b_s[l][:, :, None], (N_GMLP_HEADS, BLOCK, BLOCK)), attn_out_g=attn_out_g[l][None],
        sgu_out_g=sgu_out_g[l][None], norm2_g=norm2_g[l][None], conv_b=conv_b[l][None])


SMALL_NAMES = ("norm1_g", "q_norm_g", "k_norm_g", "sink", "sgu_ln_g", "sgu_ln_b", "w_s", "b_s", "attn_out_g", "sgu_out_g",
               "norm2_g", "conv_b", "conv_w")
REPLICATED_NAMES = SMALL_NAMES[:-1]
BIG_NAMES = ("w_in", "w_o", "w_up", "w_down")
PACK_LANES = 128
PACK_ALIGN = 8 * PACK_LANES


def _pack_rows(shape):
    return -(-math.prod(shape) // PACK_ALIGN) * 8


def _pack_parts(arrays):
    parts = []
    for a in arrays:
        flat = a.reshape(-1)
        parts.append(jnp.pad(flat, (0, _pack_rows(a.shape) * PACK_LANES - flat.shape[0])).reshape(-1, PACK_LANES))
    return parts


def _pack(arrays):
    return jnp.concatenate(_pack_parts(arrays), axis=0)


def _pack_call(arrays, name):
    parts = _pack_parts(arrays)
    total = sum(p.shape[0] for p in parts)

    def body(*refs):
        o_ref, at = refs[-1], 0
        for p_ref in refs[:-1]:
            o_ref[at:at + p_ref.shape[0], :] = p_ref[...]
            at += p_ref.shape[0]

    vm = pl.BlockSpec(memory_space=pltpu.VMEM)
    return _ordered_call(
        body, name=name, out_shape=jax.ShapeDtypeStruct((total, PACK_LANES), F32), in_specs=[vm] * len(parts), out_specs=vm,
        compiler_params=pltpu.CompilerParams(vmem_limit_bytes=V7X_VMEM_LIMIT),
    )(*parts)


def _unpack_layers(stacked, shapes):
    nl = stacked.shape[0]
    out, at = [], 0
    for shp in shapes:
        rows = _pack_rows(shp)
        out.append(stacked[:, at:at + rows].reshape(nl, -1)[:, :math.prod(shp)].reshape((nl,) + tuple(shp)))
        at += rows
    return out


def _adamw_packed(w, g, m, v, rows, layer, into, name):
    head = pl.BlockSpec((rows, PACK_LANES), lambda i: (0, 0))
    at_layer = pl.BlockSpec((None, rows, PACK_LANES), lambda i: (layer, 0, 0))

    def body(w_ref, g_ref, m_ref, v_ref, *rest):
        d_ref, nm_ref, nv_ref = rest[-3:]
        gv = g_ref[...]
        mn = ADAM_B1 * m_ref[...] + (1.0 - ADAM_B1) * gv
        vn = ADAM_B2 * v_ref[...] + (1.0 - ADAM_B2) * (gv * gv)
        m_hat = mn / (1.0 - ADAM_B1 ** ADAM_STEP)
        v_hat = vn / (1.0 - ADAM_B2 ** ADAM_STEP)
        d_ref[...] = -ADAM_LR * (m_hat / (jnp.sqrt(v_hat) + ADAM_EPS) + ADAM_WD * w_ref[...])
        nm_ref[...] = mn
        nv_ref[...] = vn

    in_specs = [head] * 4
    operands = [w, g, m, v]
    aliases = {}
    if into is not None:
        in_specs += [ANY] * 3
        operands += list(into)
        aliases = {4 + i: i for i in range(3)}
    sds = jax.ShapeDtypeStruct((DEPTH, rows, PACK_LANES), F32)
    return _ordered_call(
        body, name=name, out_shape=(sds,) * 3, grid=(1,), in_specs=in_specs, out_specs=(at_layer,) * 3,
        input_output_aliases=aliases, compiler_params=_params(("arbitrary",)),
    )(*operands)


def kernel(x, norm1_g, w_in, q_norm_g, k_norm_g, sink, sgu_ln_g, sgu_ln_b, w_s, b_s, attn_out_g, sgu_out_g, w_o, norm2_g, w_up, conv_w, conv_b, w_down, loss_target, m_norm1_g, m_w_in, m_q_norm_g, m_k_norm_g, m_sink, m_sgu_ln_g, m_sgu_ln_b, m_w_s, m_b_s, m_attn_out_g, m_sgu_out_g, m_w_o, m_norm2_g, m_w_up, m_conv_w, m_conv_b, m_w_down, v_norm1_g, v_w_in, v_q_norm_g, v_k_norm_g, v_sink, v_sgu_ln_g, v_sgu_ln_b, v_w_s, v_b_s, v_attn_out_g, v_sgu_out_g, v_w_o, v_norm2_g, v_w_up, v_conv_w, v_conv_b, v_w_down):
    weights = dict(norm1_g=norm1_g, w_in=w_in, q_norm_g=q_norm_g, k_norm_g=k_norm_g, sink=sink, sgu_ln_g=sgu_ln_g,
                   sgu_ln_b=sgu_ln_b, w_s=w_s, b_s=b_s, attn_out_g=attn_out_g, sgu_out_g=sgu_out_g, w_o=w_o, norm2_g=norm2_g,
                   w_up=w_up, conv_w=conv_w, conv_b=conv_b, w_down=w_down)
    m_in = dict(norm1_g=m_norm1_g, w_in=m_w_in, q_norm_g=m_q_norm_g, k_norm_g=m_k_norm_g, sink=m_sink, sgu_ln_g=m_sgu_ln_g,
                sgu_ln_b=m_sgu_ln_b, w_s=m_w_s, b_s=m_b_s, attn_out_g=m_attn_out_g, sgu_out_g=m_sgu_out_g, w_o=m_w_o,
                norm2_g=m_norm2_g, w_up=m_w_up, conv_w=m_conv_w, conv_b=m_conv_b, w_down=m_w_down)
    v_in = dict(norm1_g=v_norm1_g, w_in=v_w_in, q_norm_g=v_q_norm_g, k_norm_g=v_k_norm_g, sink=v_sink, sgu_ln_g=v_sgu_ln_g,
                sgu_ln_b=v_sgu_ln_b, w_s=v_w_s, b_s=v_b_s, attn_out_g=v_attn_out_g, sgu_out_g=v_sgu_out_g, w_o=v_w_o,
                norm2_g=v_norm2_g, w_up=v_w_up, conv_w=v_conv_w, conv_b=v_conv_b, w_down=v_w_down)
    cx, cy, cc = lax.axis_index("x"), lax.axis_index("y"), lax.axis_index("c")
    j_me = 2 * cx + cy
    c_arr = jnp.reshape(cc, (1,)).astype(jnp.int32)
    j_arr = jnp.reshape(j_me, (1,)).astype(jnp.int32)

    _Order.last = None
    ex = _Exchange(weights, m_in, v_in, j_arr, c_arr, jnp.reshape(4 * cx + 2 * cy + cc, (1,)).astype(jnp.int32))
    small = [_small_views(l, norm1_g, q_norm_g, k_norm_g, sink, sgu_ln_g, sgu_ln_b, w_s, b_s, attn_out_g, sgu_out_g, norm2_g,
                          conv_b) for l in range(DEPTH)]
    packed_in = [[_pack_call([src[nm][l] for nm in REPLICATED_NAMES], f"pack_{tag}_l{l}")
                  for tag, src in (("w", weights), ("m", m_in), ("v", v_in))] for l in range(DEPTH)]
    dx = _local_step(x[0], loss_target[0], ex, small)
    big_out = ex.finish()

    rep_shapes = [weights[nm].shape[1:] for nm in REPLICATED_NAMES]
    rep_rows = sum(_pack_rows(shp) for shp in rep_shapes)
    cw_shape = (3, 2 * D_FF)
    sums, adam_small = [None] * DEPTH, None
    for l in reversed(range(DEPTH)):
        sums[l] = ex.small_sum(l)
        pw, pm, pv = packed_in[l]
        adam_small = _adamw_packed(pw, sums[l], pm, pv, rep_rows, l, adam_small, f"adamw_small_l{l}")
    cw_rows = _pack_rows(cw_shape)
    loss = sums[0][rep_rows + cw_rows, 0]
    stacked = jnp.stack([sm[:rep_rows + cw_rows] for sm in sums])
    grads = dict(zip(REPLICATED_NAMES, _unpack_layers(stacked[:, :rep_rows], rep_shapes)))
    delta, new_m, new_v = (dict(zip(REPLICATED_NAMES, _unpack_layers(arr, rep_shapes))) for arr in adam_small)
    cw_cols = 2 * D_FF // N_CHIPS
    cw_grad = lax.dynamic_slice_in_dim(_unpack_layers(stacked[:, rep_rows:], [cw_shape])[0], j_me * cw_cols, cw_cols, axis=2)
    flat = lambda a: a.reshape(DEPTH * 3, cw_cols)
    cw_out = _adamw(flat(conv_w), flat(cw_grad), flat(m_conv_w), flat(v_conv_w), "adamw_conv_w")
    grads["conv_w"], delta["conv_w"], new_m["conv_w"], new_v["conv_w"] = (a.reshape(DEPTH, 3, cw_cols) for a in cw_out)

    for name in BIG_NAMES:
        grads[name], delta[name], new_m[name], new_v[name] = big_out[name]

    order = ("norm1_g", "w_in", "q_norm_g", "k_norm_g", "sink", "sgu_ln_g", "sgu_ln_b", "w_s", "b_s", "attn_out_g", "sgu_out_g",
             "w_o", "norm2_g", "w_up", "conv_w", "conv_b", "w_down")
    return (loss, dx[None], *[grads[nm] for nm in order], *[delta[nm] for nm in order], *[new_m[nm] for nm in order],
            *[new_v[nm] for nm in order])
```

```python
import functools
import math

import jax
import jax.numpy as jnp
from jax import lax
from jax.experimental import pallas as pl
from jax.experimental.pallas import tpu as pltpu

F32 = jnp.float32
BF16 = jnp.bfloat16

D_MODEL = 2048
HEAD_DIM = 128
ATTN_WIDTH = 1024
N_Q_HEADS = 8
N_KV_HEADS = 2
GQA_GROUP = 4
KV_WIDTH = 256
GMLP_WIDTH = 1024
N_GMLP_HEADS = 8
BLOCK = 128
IN_WIDTH = 3584
D_FF = 5632
DEPTH = 2
EPS = 1e-6
MASK_VALUE = -1e30
ROPE_THETA = 10000.0
N_CHIPS = 4

ADAM_LR = 0.001
ADAM_B1 = 0.9
ADAM_B2 = 0.999
ADAM_EPS = 1e-08
ADAM_WD = 0.01
ADAM_STEP = 10

V7X_VMEM_LIMIT = 48 * 1024 * 1024
MESH = pl.DeviceIdType.MESH

_GELU_C = math.sqrt(2.0 / math.pi)
_GELU_A = 0.044715


def _params(sem=None):
    return pltpu.CompilerParams(dimension_semantics=sem, vmem_limit_bytes=V7X_VMEM_LIMIT)


ANY = pl.BlockSpec(memory_space=pl.ANY)


class _Order:
    last = None


def _ordered_call(body, *, token_index=0, **kw):
    def run(*operands):
        tok = _Order.last
        if tok is None or any(op is tok for op in operands):
            call = pl.pallas_call(body, **kw)
        else:
            n_in = len(operands)

            def ordered_body(*refs):
                return body(*refs[:n_in], *refs[n_in + 1:])

            kw2 = dict(kw)
            if "grid_spec" in kw2:
                gs = kw2["grid_spec"]
                kw2["grid_spec"] = pltpu.PrefetchScalarGridSpec(
                    num_scalar_prefetch=gs.num_scalar_prefetch, grid=gs.grid, in_specs=list(gs.in_specs) + [ANY],
                    out_specs=gs.out_specs, scratch_shapes=gs.scratch_shapes)
            else:
                kw2["in_specs"] = list(kw2["in_specs"]) + [ANY]
            call = pl.pallas_call(ordered_body, **kw2)
            operands = operands + (tok,)
        out = call(*operands)
        _Order.last = out[token_index] if isinstance(out, (tuple, list)) else out
        return out

    return run


def _gelu(x):
    return x * (0.5 * (1.0 + jnp.tanh(_GELU_C * (x + _GELU_A * (x * x * x)))))


def _gelu_grad(x):
    x2 = x * x
    t = jnp.tanh(_GELU_C * (x + _GELU_A * (x * x2)))
    return 0.5 * (1.0 + t) + 0.5 * x * (1.0 - t * t) * (_GELU_C * (1.0 + 3.0 * _GELU_A * x2))


def _mean_last(x):
    return jnp.mean(x, axis=-1, keepdims=True)


def _sum_rows(x):
    return jnp.sum(x, axis=0, keepdims=True)


def _sum_all(x):
    return jnp.sum(jnp.sum(x, axis=1, keepdims=True), axis=0, keepdims=True)


def _matmul(a, b, *, mode, out_dtype, tm, tn, tk, name, res=None, a_parts=0, b_parts=0, out_parts=0, b_lead=()):
    b_full = b
    b = jax.ShapeDtypeStruct(b.shape[len(b_lead):], b.dtype)
    if mode == "nn":
        assert not a_parts
        m, k = a.shape
        n = b.shape[0] * b.shape[2] if b_parts else b.shape[1]
    elif mode == "nt":
        m, k = (a.shape[1], a.shape[0] * a.shape[2]) if a_parts else a.shape
        n = b.shape[1] if b_parts else b.shape[0]
    else:
        assert not a_parts
        k, m = a.shape
        n = b.shape[0] * b.shape[2] if b_parts else b.shape[1]
    tm, tn, tk = min(tm, m), min(tn, n), min(tk, k)
    assert m % tm == 0 and n % tn == 0 and k % tk == 0, (name, m, n, k, tm, tn, tk)
    nm, nn, nk = m // tm, n // tn, k // tk

    def slab(idx, total_tiles, parts):
        per = total_tiles // parts
        assert per * parts == total_tiles, (name, total_tiles, parts)
        return idx // per, idx % per

    if mode == "nn":
        a_spec = pl.BlockSpec((tm, tk), lambda i, j, kk: (i, kk))
        if b_parts:
            b_spec = pl.BlockSpec((None, tk, tn), lambda i, j, kk: (slab(j, nn, b_parts)[0], kk, slab(j, nn, b_parts)[1]))
        else:
            b_spec = pl.BlockSpec((tk, tn), lambda i, j, kk: (kk, j))
        dims = (((1,), (0,)), ((), ()))
    elif mode == "nt":
        if a_parts:
            a_spec = pl.BlockSpec((None, tm, tk), lambda i, j, kk: (slab(kk, nk, a_parts)[0], i, slab(kk, nk, a_parts)[1]))
        else:
            a_spec = pl.BlockSpec((tm, tk), lambda i, j, kk: (i, kk))
        if b_parts:
            b_spec = pl.BlockSpec((None, tn, tk), lambda i, j, kk: (slab(kk, nk, b_parts)[0], j, slab(kk, nk, b_parts)[1]))
        else:
            b_spec = pl.BlockSpec((tn, tk), lambda i, j, kk: (j, kk))
        dims = (((1,), (1,)), ((), ()))
    else:
        a_spec = pl.BlockSpec((tk, tm), lambda i, j, kk: (kk, i))
        if b_parts:
            b_spec = pl.BlockSpec((None, tk, tn), lambda i, j, kk: (slab(j, nn, b_parts)[0], kk, slab(j, nn, b_parts)[1]))
        else:
            b_spec = pl.BlockSpec((tk, tn), lambda i, j, kk: (kk, j))
        dims = (((0,), (0,)), ((), ()))
    if out_parts:
        out_shape = jax.ShapeDtypeStruct((out_parts, m, n // out_parts), out_dtype)
        out_spec = pl.BlockSpec((None, tm, tn), lambda i, j, kk: (slab(j, nn, out_parts)[0], i, slab(j, nn, out_parts)[1]))
    else:
        out_shape = jax.ShapeDtypeStruct((m, n), out_dtype)
        out_spec = pl.BlockSpec((tm, tn), lambda i, j, kk: (i, j))
    if b_lead:
        inner_map = b_spec.index_map
        b_spec = pl.BlockSpec((None,) * len(b_lead) + tuple(b_spec.block_shape),
                              lambda i, j, kk: tuple(b_lead) + tuple(inner_map(i, j, kk)))
    in_specs = [a_spec, b_spec]
    operands = [a, b_full]
    if res is not None:
        in_specs.append(pl.BlockSpec((tm, tn), lambda i, j, kk: (i, j)))
        operands.append(res)

    def body(*refs):
        a_ref, b_ref = refs[0], refs[1]
        res_ref = refs[2] if res is not None else None
        o_ref = refs[3] if res is not None else refs[2]
        p = lax.dot_general(a_ref[...], b_ref[...], dims, preferred_element_type=F32)

        def finish(total):
            if res_ref is not None:
                total = res_ref[...] + total
            o_ref[...] = total.astype(out_dtype)

        if nk == 1:
            finish(p)
        else:
            acc_ref = refs[-1]
            kk = pl.program_id(2)

            @pl.when(kk == 0)
            def _():
                acc_ref[...] = p

            @pl.when(jnp.logical_and(kk > 0, kk < nk - 1))
            def _():
                acc_ref[...] += p

            @pl.when(kk == nk - 1)
            def _():
                finish(acc_ref[...] + p)

    scratch = [pltpu.VMEM((tm, tn), F32)] if nk > 1 else []
    return _ordered_call(
        body, name=name, out_shape=out_shape, grid=(nm, nn, nk), in_specs=in_specs, out_specs=out_spec,
        scratch_shapes=scratch, compiler_params=_params(("parallel", "parallel", "arbitrary")),
    )(*operands)


def _matmul_nt_slabs(a, b, *, tm, tn, name, a_parts=0):
    nslab, n, ks = b.shape
    m = a.shape[1] if a_parts else a.shape[0]
    tm, tn = min(tm, m), min(tn, n)
    assert m % tm == 0 and n % tn == 0, (name, m, n, tm, tn)
    if a_parts:
        per = nslab // a_parts
        assert per * a_parts == nslab and a.shape[2] == per * ks, (name, a.shape, b.shape)
        a_spec = pl.BlockSpec((a_parts, tm, per * ks), lambda i, j: (0, i, 0))
    else:
        assert a.shape[1] == nslab * ks, (name, a.shape, b.shape)
        a_spec = pl.BlockSpec((tm, nslab * ks), lambda i, j: (i, 0))

    def body(a_ref, b_ref, o_ref):
        total = None
        for sl in range(nslab):
            if a_parts:
                a_sl = a_ref[sl // per, :, (sl % per) * ks:(sl % per + 1) * ks]
            else:
                a_sl = a_ref[:, sl * ks:(sl + 1) * ks]
            p = lax.dot_general(a_sl, b_ref[sl], (((1,), (1,)), ((), ())), preferred_element_type=F32)
            total = p if total is None else total + p
        o_ref[...] = total

    return _ordered_call(
        body, name=name, out_shape=jax.ShapeDtypeStruct((m, n), F32), grid=(m // tm, n // tn),
        in_specs=[a_spec, pl.BlockSpec((nslab, tn, ks), lambda i, j: (0, j, 0))],
        out_specs=pl.BlockSpec((tm, tn), lambda i, j: (i, j)), compiler_params=_params(("parallel", "parallel")),
    )(a, b)


def _row_tile(s):
    return min(256, s)


def _rows(width, tr):
    return pl.BlockSpec((tr, width), lambda i: (i, 0))


def _const2(shape):
    return pl.BlockSpec(shape, lambda i: (0, 0))


def _rms_fwd(x, g, name):
    s, d = x.shape
    tr = _row_tile(s)

    def body(x_ref, g_ref, o_ref):
        xv = x_ref[...]
        r = lax.rsqrt(_mean_last(xv * xv) + EPS)
        o_ref[...] = (xv * r * g_ref[...]).astype(BF16)

    return _ordered_call(
        body, name=name, out_shape=jax.ShapeDtypeStruct((s, d), BF16), grid=(s // tr,),
        in_specs=[_rows(d, tr), _const2((1, d))], out_specs=_rows(d, tr), compiler_params=_params(("parallel",)),
    )(x, g)


def _rms_bwd(x, g, dh, dres, name):
    s, d = x.shape
    tr = _row_tile(s)

    def body(x_ref, g_ref, dh_ref, dres_ref, dx_ref, dxb_ref, dg_ref):
        xv, dy = x_ref[...], dh_ref[...]
        r = lax.rsqrt(_mean_last(xv * xv) + EPS)
        gdy = dy * g_ref[...]
        dx = dres_ref[...] + r * gdy - xv * ((r * r * r) * _mean_last(xv * gdy))
        dx_ref[...] = dx
        dxb_ref[...] = dx.astype(BF16)

        @pl.when(pl.program_id(0) == 0)
        def _():
            dg_ref[...] = jnp.zeros_like(dg_ref)

        dg_ref[...] += _sum_rows(xv * r * dy)

    return _ordered_call(
        body, name=name,
        out_shape=(jax.ShapeDtypeStruct((s, d), F32), jax.ShapeDtypeStruct((s, d), BF16), jax.ShapeDtypeStruct((1, d), F32)),
        grid=(s // tr,), in_specs=[_rows(d, tr), _const2((1, d)), _rows(d, tr), _rows(d, tr)],
        out_specs=(_rows(d, tr), _rows(d, tr), _const2((1, d))), compiler_params=_params(("arbitrary",)),
    )(x, g, dh, dres)


Q0, K0, V0, GU0, GV0 = 0, ATTN_WIDTH, ATTN_WIDTH + KV_WIDTH, ATTN_WIDTH + 2 * KV_WIDTH, ATTN_WIDTH + 2 * KV_WIDTH + GMLP_WIDTH


def _head(h, base=0):
    return slice(base + h * HEAD_DIM, base + (h + 1) * HEAD_DIM)


def _proj_post(z, qg, kg, lg, lb, cosf, sinf, name):
    s = z.shape[0]
    tr = _row_tile(s)

    def body(z_ref, qg_ref, kg_ref, lg_ref, lb_ref, cos_ref, sin_ref, qn_ref, kn_ref, vb_ref, ug_ref, vn_ref):
        cos, sin = cos_ref[...], sin_ref[...]

        def norm_rope(xh, g):
            y = xh * lax.rsqrt(_mean_last(xh * xh) + EPS) * g
            return y * cos + pltpu.roll(y, HEAD_DIM // 2, 1) * sin

        for h in range(N_Q_HEADS):
            qn_ref[:, _head(h)] = norm_rope(z_ref[:, _head(h, Q0)].astype(F32), qg_ref[...]).astype(BF16)
        for h in range(N_KV_HEADS):
            kn_ref[:, _head(h)] = norm_rope(z_ref[:, _head(h, K0)].astype(F32), kg_ref[...]).astype(BF16)
        vb_ref[...] = z_ref[:, V0:GU0]
        ug_ref[...] = _gelu(z_ref[:, GU0:GV0].astype(F32))
        vg = _gelu(z_ref[:, GV0:IN_WIDTH].astype(F32))
        xc = vg - _mean_last(vg)
        y = xc * lax.rsqrt(_mean_last(xc * xc) + EPS)
        vn_ref[...] = (y * lg_ref[...] + lb_ref[...]).astype(BF16)

    return _ordered_call(
        body, name=name,
        out_shape=(jax.ShapeDtypeStruct((s, ATTN_WIDTH), BF16), jax.ShapeDtypeStruct((s, KV_WIDTH), BF16),
                   jax.ShapeDtypeStruct((s, KV_WIDTH), BF16), jax.ShapeDtypeStruct((s, GMLP_WIDTH), F32),
                   jax.ShapeDtypeStruct((s, GMLP_WIDTH), BF16)),
        grid=(s // tr,),
        in_specs=[_rows(IN_WIDTH, tr), _const2((1, HEAD_DIM)), _const2((1, HEAD_DIM)), _const2((1, GMLP_WIDTH)),
                  _const2((1, GMLP_WIDTH)), _rows(HEAD_DIM, tr), _rows(HEAD_DIM, tr)],
        out_specs=(_rows(ATTN_WIDTH, tr), _rows(KV_WIDTH, tr), _rows(KV_WIDTH, tr), _rows(GMLP_WIDTH, tr), _rows(GMLP_WIDTH, tr)),
        compiler_params=_params(("parallel",)),
    )(z, qg, kg, lg, lb, cosf, sinf)


def _proj_post_bwd(z, dqn, dkn, dvb, dug, dvn, qg, kg, lg, cosf, sinf, name):
    s = z.shape[0]
    tr = _row_tile(s)

    def body(z_ref, dqn_ref, dkn_ref, dvb_ref, dug_ref, dvn_ref, qg_ref, kg_ref, lg_ref, cos_ref, sin_ref,
             dz_ref, dqg_ref, dkg_ref, dlg_ref, dlb_ref):
        cos, sin = cos_ref[...], sin_ref[...]

        @pl.when(pl.program_id(0) == 0)
        def _():
            dqg_ref[...] = jnp.zeros_like(dqg_ref)
            dkg_ref[...] = jnp.zeros_like(dkg_ref)
            dlg_ref[...] = jnp.zeros_like(dlg_ref)
            dlb_ref[...] = jnp.zeros_like(dlb_ref)

        def norm_rope_bwd(xh, g, dout):
            dy = dout * cos - pltpu.roll(dout, HEAD_DIM // 2, 1) * sin
            r = lax.rsqrt(_mean_last(xh * xh) + EPS)
            xhat = xh * r
            gdy = dy * g
            return r * (gdy - xhat * _mean_last(xhat * gdy)), _sum_rows(xhat * dy)

        dqg = jnp.zeros((1, HEAD_DIM), F32)
        for h in range(N_Q_HEADS):
            dx, dg = norm_rope_bwd(z_ref[:, _head(h, Q0)].astype(F32), qg_ref[...], dqn_ref[:, _head(h)])
            dz_ref[:, _head(h, Q0)] = dx.astype(BF16)
            dqg = dqg + dg
        dqg_ref[...] += dqg
        dkg = jnp.zeros((1, HEAD_DIM), F32)
        for h in range(N_KV_HEADS):
            dx, dg = norm_rope_bwd(z_ref[:, _head(h, K0)].astype(F32), kg_ref[...], dkn_ref[:, _head(h)])
            dz_ref[:, _head(h, K0)] = dx.astype(BF16)
            dkg = dkg + dg
        dkg_ref[...] += dkg
        dz_ref[:, V0:GU0] = dvb_ref[...].astype(BF16)
        dz_ref[:, GU0:GV0] = (dug_ref[...] * _gelu_grad(z_ref[:, GU0:GV0].astype(F32))).astype(BF16)
        gv = z_ref[:, GV0:IN_WIDTH].astype(F32)
        vg = _gelu(gv)
        xc = vg - _mean_last(vg)
        r = lax.rsqrt(_mean_last(xc * xc) + EPS)
        xhat = xc * r
        dvn_v = dvn_ref[...]
        dlg_ref[...] += _sum_rows(xhat * dvn_v)
        dlb_ref[...] += _sum_rows(dvn_v)
        dxh = dvn_v * lg_ref[...]
        dvg = r * (dxh - _mean_last(dxh) - xhat * _mean_last(dxh * xhat))
        dz_ref[:, GV0:IN_WIDTH] = (dvg * _gelu_grad(gv)).astype(BF16)

    return _ordered_call(
        body, name=name,
        out_shape=(jax.ShapeDtypeStruct((s, IN_WIDTH), BF16), jax.ShapeDtypeStruct((1, HEAD_DIM), F32),
                   jax.ShapeDtypeStruct((1, HEAD_DIM), F32), jax.ShapeDtypeStruct((1, GMLP_WIDTH), F32),
                   jax.ShapeDtypeStruct((1, GMLP_WIDTH), F32)),
        grid=(s // tr,),
        in_specs=[_rows(IN_WIDTH, tr), _rows(ATTN_WIDTH, tr), _rows(KV_WIDTH, tr), _rows(KV_WIDTH, tr), _rows(GMLP_WIDTH, tr),
                  _rows(GMLP_WIDTH, tr), _const2((1, HEAD_DIM)), _const2((1, HEAD_DIM)), _const2((1, GMLP_WIDTH)),
                  _rows(HEAD_DIM, tr), _rows(HEAD_DIM, tr)],
        out_specs=(_rows(IN_WIDTH, tr), _const2((1, HEAD_DIM)), _const2((1, HEAD_DIM)), _const2((1, GMLP_WIDTH)),
                   _const2((1, GMLP_WIDTH))),
        compiler_params=_params(("arbitrary",)),
    )(z, dqn, dkn, dvb, dug, dvn, qg, kg, lg, cosf, sinf)


def _band_valid(n, s):
    shape = (GQA_GROUP * BLOCK, 3 * BLOCK)
    i = lax.broadcasted_iota(jnp.int32, shape, 0) & (BLOCK - 1)
    j = lax.broadcasted_iota(jnp.int32, shape, 1)
    k_pos = n * BLOCK - BLOCK + j
    return (jnp.abs(j - BLOCK - i) <= BLOCK) & (k_pos >= 0) & (k_pos < s)


def _group_rows(x, kh):
    return jnp.concatenate([x[:, _head(kh * GQA_GROUP + g)] for g in range(GQA_GROUP)], axis=0)


def _group_sinks(sink_ref, kh):
    return jnp.concatenate([jnp.full((BLOCK, 1), sink_ref[kh * GQA_GROUP + g], F32) for g in range(GQA_GROUP)], axis=0)


def _rows_of(x, g):
    return x[g * BLOCK:(g + 1) * BLOCK]


def _probs(q, kb, sink_h, valid):
    sc = lax.dot_general(q, kb, (((1,), (1,)), ((), ())), preferred_element_type=F32) * (HEAD_DIM ** -0.5)
    sc = jnp.where(valid, sc, MASK_VALUE)
    m = jnp.maximum(jnp.max(sc, axis=-1, keepdims=True), sink_h)
    p = jnp.exp(sc - m)
    es = jnp.exp(sink_h - m)
    den = jnp.sum(p, axis=-1, keepdims=True) + es
    return p / den, es / den


def _band_specs(width, nb):
    return [pl.BlockSpec((BLOCK, width), lambda n: (jnp.maximum(n - 1, 0), 0)),
            pl.BlockSpec((BLOCK, width), lambda n: (n, 0)),
            pl.BlockSpec((BLOCK, width), lambda n: (jnp.minimum(n + 1, nb - 1), 0))]


def _blk(width):
    return pl.BlockSpec((BLOCK, width), lambda n: (n, 0))


def _whole3(shape):
    return pl.BlockSpec(shape, lambda n: (0, 0, 0))


def _smem():
    return pl.BlockSpec(memory_space=pltpu.SMEM)


def _mixer_fwd(qn, kn, vb, ug, vn, wsb, bsb, sink, ga, gs, name):
    s = qn.shape[0]
    nb = s // BLOCK

    def body(sink_ref, q_ref, kp_ref, kc_ref, kx_ref, vp_ref, vc_ref, vx_ref, ug_ref, vn_ref, ws_ref, bs_ref, ga_ref, gs_ref,
             attn_ref, sgu_ref, mix_ref):
        n = pl.program_id(0)
        valid = _band_valid(n, s)
        ssq = jnp.zeros((BLOCK, 1), F32)
        for kh in range(N_KV_HEADS):
            kb = jnp.concatenate([kp_ref[:, _head(kh)], kc_ref[:, _head(kh)], kx_ref[:, _head(kh)]], axis=0)
            vbd = jnp.concatenate([vp_ref[:, _head(kh)], vc_ref[:, _head(kh)], vx_ref[:, _head(kh)]], axis=0)
            p, _ = _probs(_group_rows(q_ref, kh), kb, _group_sinks(sink_ref, kh), valid)
            o4 = jnp.dot(p.astype(BF16), vbd, preferred_element_type=F32)
            for g in range(GQA_GROUP):
                o = _rows_of(o4, g)
                attn_ref[:, _head(kh * GQA_GROUP + g)] = o
                ssq = ssq + jnp.sum(o * o, axis=-1, keepdims=True)
        r = lax.rsqrt(ssq * (1.0 / ATTN_WIDTH) + EPS)
        mix_ref[:, 0:ATTN_WIDTH] = (attn_ref[...] * r * ga_ref[...]).astype(BF16)
        ssq = jnp.zeros((BLOCK, 1), F32)
        for h in range(N_GMLP_HEADS):
            f = jnp.dot(ws_ref[h], vn_ref[:, _head(h)], preferred_element_type=F32) + bs_ref[h]
            o = ug_ref[:, _head(h)] * f
            sgu_ref[:, _head(h)] = o
            ssq = ssq + jnp.sum(o * o, axis=-1, keepdims=True)
        r = lax.rsqrt(ssq * (1.0 / GMLP_WIDTH) + EPS)
        mix_ref[:, ATTN_WIDTH:D_MODEL] = (sgu_ref[...] * r * gs_ref[...]).astype(BF16)

    hh = (N_GMLP_HEADS, BLOCK, BLOCK)
    return _ordered_call(
        body, name=name,
        out_shape=(jax.ShapeDtypeStruct((s, ATTN_WIDTH), F32), jax.ShapeDtypeStruct((s, GMLP_WIDTH), F32),
                   jax.ShapeDtypeStruct((s, D_MODEL), BF16)),
        grid=(nb,),
        in_specs=[_smem(), _blk(ATTN_WIDTH)] + _band_specs(KV_WIDTH, nb) + _band_specs(KV_WIDTH, nb)
        + [_blk(GMLP_WIDTH), _blk(GMLP_WIDTH), _whole3(hh), _whole3(hh),
           pl.BlockSpec((1, ATTN_WIDTH), lambda n: (0, 0)), pl.BlockSpec((1, GMLP_WIDTH), lambda n: (0, 0))],
        out_specs=(_blk(ATTN_WIDTH), _blk(GMLP_WIDTH), _blk(D_MODEL)),
        compiler_params=_params(("parallel",)),
    )(sink, qn, kn, kn, kn, vb, vb, vb, ug, vn, wsb, bsb, ga, gs)


def _mixer_bwd(qn, kn, vb, ug, vn, attn, sgu, dmixed, wsb, bsb, sink, ga, gs, name):
    s = qn.shape[0]
    nb = s // BLOCK
    tn_dims = (((0,), (0,)), ((), ()))
    nt_dims = (((1,), (1,)), ((), ()))

    def body(sink_ref, q_ref, kp_ref, kc_ref, kx_ref, vp_ref, vc_ref, vx_ref, ug_ref, vn_ref, attn_ref, sgu_ref, dm_ref,
             ws_ref, bs_ref, ga_ref, gs_ref,
             dq_ref, dk_ref, dv_ref, dug_ref, dvn_ref, dws_ref, dbs_ref, dsk_ref, dga_ref, dgs_ref, dk_acc, dv_acc):
        n = pl.program_id(0)

        @pl.when(n == 0)
        def _():
            for ref in (dk_acc, dv_acc, dws_ref, dbs_ref, dsk_ref, dga_ref, dgs_ref):
                ref[...] = jnp.zeros_like(ref)

        def out_norm_bwd(o, g, dy):
            r = lax.rsqrt(_mean_last(o * o) + EPS)
            gdy = dy * g
            return r * gdy - o * ((r * r * r) * _mean_last(o * gdy)), _sum_rows(o * r * dy)

        d_attn, dga = out_norm_bwd(attn_ref[...], ga_ref[...], dm_ref[:, 0:ATTN_WIDTH])
        dga_ref[...] += dga
        d_sgu, dgs = out_norm_bwd(sgu_ref[...], gs_ref[...], dm_ref[:, ATTN_WIDTH:D_MODEL])
        dgs_ref[...] += dgs

        for h in range(N_GMLP_HEADS):
            vn_h = vn_ref[:, _head(h)]
            f = jnp.dot(ws_ref[h], vn_h, preferred_element_type=F32) + bs_ref[h]
            ds_h = d_sgu[:, _head(h)]
            dug_ref[:, _head(h)] = ds_h * f
            df = ds_h * ug_ref[:, _head(h)]
            dfb = df.astype(BF16)
            dvn_ref[:, _head(h)] = lax.dot_general(ws_ref[h], dfb, tn_dims, preferred_element_type=F32)
            dws_ref[h] += lax.dot_general(dfb, vn_h, nt_dims, preferred_element_type=F32)
            dbs_ref[h] += jnp.broadcast_to(jnp.sum(df, axis=-1, keepdims=True), (BLOCK, BLOCK))

        valid = _band_valid(n, s)
        row0 = pl.multiple_of(n * BLOCK, BLOCK)
        for kh in range(N_KV_HEADS):
            kb = jnp.concatenate([kp_ref[:, _head(kh)], kc_ref[:, _head(kh)], kx_ref[:, _head(kh)]], axis=0)
            vbd = jnp.concatenate([vp_ref[:, _head(kh)], vc_ref[:, _head(kh)], vx_ref[:, _head(kh)]], axis=0)
            q4 = _group_rows(q_ref, kh)
            p, p_sink = _probs(q4, kb, _group_sinks(sink_ref, kh), valid)
            do4 = _group_rows(d_attn, kh).astype(BF16)
            dp = lax.dot_general(do4, vbd, nt_dims, preferred_element_type=F32)
            delta = jnp.sum(p * dp, axis=-1, keepdims=True)
            dsc = (p * (dp - delta) * (HEAD_DIM ** -0.5)).astype(BF16)
            d_sink = -(p_sink * delta)
            dq4 = jnp.dot(dsc, kb, preferred_element_type=F32)
            for g in range(GQA_GROUP):
                h = kh * GQA_GROUP + g
                dsk_ref[h:h + 1, :] += jnp.broadcast_to(_sum_all(_rows_of(d_sink, g)), (1, BLOCK))
                dq_ref[:, _head(h)] = _rows_of(dq4, g)
            dk_acc[pl.ds(row0, 3 * BLOCK), _head(kh)] += lax.dot_general(dsc, q4, tn_dims, preferred_element_type=F32)
            dv_acc[pl.ds(row0, 3 * BLOCK), _head(kh)] += lax.dot_general(p.astype(BF16), do4, tn_dims,
                                                                         preferred_element_type=F32)

        @pl.when(n == nb - 1)
        def _():
            dk_ref[...] = dk_acc[BLOCK:BLOCK + s, :]
            dv_ref[...] = dv_acc[BLOCK:BLOCK + s, :]

    hh = (N_GMLP_HEADS, BLOCK, BLOCK)
    full_kv = pl.BlockSpec((s, KV_WIDTH), lambda n: (0, 0))
    return _ordered_call(
        body, name=name,
        out_shape=(jax.ShapeDtypeStruct((s, ATTN_WIDTH), F32), jax.ShapeDtypeStruct((s, KV_WIDTH), F32),
                   jax.ShapeDtypeStruct((s, KV_WIDTH), F32), jax.ShapeDtypeStruct((s, GMLP_WIDTH), F32),
                   jax.ShapeDtypeStruct((s, GMLP_WIDTH), F32), jax.ShapeDtypeStruct(hh, F32), jax.ShapeDtypeStruct(hh, F32),
                   jax.ShapeDtypeStruct((N_Q_HEADS, BLOCK), F32), jax.ShapeDtypeStruct((1, ATTN_WIDTH), F32),
                   jax.ShapeDtypeStruct((1, GMLP_WIDTH), F32)),
        grid=(nb,),
        in_specs=[_smem(), _blk(ATTN_WIDTH)] + _band_specs(KV_WIDTH, nb) + _band_specs(KV_WIDTH, nb)
        + [_blk(GMLP_WIDTH), _blk(GMLP_WIDTH), _blk(ATTN_WIDTH), _blk(GMLP_WIDTH), _blk(D_MODEL), _whole3(hh), _whole3(hh),
           pl.BlockSpec((1, ATTN_WIDTH), lambda n: (0, 0)), pl.BlockSpec((1, GMLP_WIDTH), lambda n: (0, 0))],
        out_specs=(_blk(ATTN_WIDTH), full_kv, full_kv, _blk(GMLP_WIDTH), _blk(GMLP_WIDTH), _whole3(hh), _whole3(hh),
                   pl.BlockSpec((N_Q_HEADS, BLOCK), lambda n: (0, 0)), pl.BlockSpec((1, ATTN_WIDTH), lambda n: (0, 0)),
                   pl.BlockSpec((1, GMLP_WIDTH), lambda n: (0, 0))),
        scratch_shapes=[pltpu.VMEM((s + 2 * BLOCK, KV_WIDTH), F32), pltpu.VMEM((s + 2 * BLOCK, KV_WIDTH), F32)],
        compiler_params=_params(("arbitrary",)),
    )(sink, qn, kn, kn, kn, vb, vb, vb, ug, vn, attn, sgu, dmixed, wsb, bsb, ga, gs)


CONV_TILE = 128


PAD_ROWS = 8


def _zero_pad_rows(pad_ref):
    s = pad_ref.shape[0] - 2 * PAD_ROWS
    zeros = jnp.zeros((PAD_ROWS, pad_ref.shape[1]), F32)
    pad_ref[0:PAD_ROWS, :] = zeros
    pad_ref[PAD_ROWS + s:2 * PAD_ROWS + s, :] = zeros


def _shift_rows(a, pad_ref):
    s = a.shape[0]
    pad_ref[PAD_ROWS:PAD_ROWS + s, :] = a
    padded = pad_ref[...]
    prev = pltpu.roll(padded, 1, 0)[PAD_ROWS:PAD_ROWS + s]
    nxt = pltpu.roll(padded, s + 2 * PAD_ROWS - 1, 0)[PAD_ROWS:PAD_ROWS + s]
    return prev, nxt


def _conv_specs(s):
    tc = CONV_TILE
    nj = D_FF // tc
    return (tc, nj, pl.BlockSpec((2, s, tc), lambda j: (0, 0, j)),
            [pl.BlockSpec((3, tc), lambda j: (0, j)), pl.BlockSpec((3, tc), lambda j: (0, j + nj))],
            [pl.BlockSpec((1, tc), lambda j: (0, j)), pl.BlockSpec((1, tc), lambda j: (0, j + nj))])


def _conv_gate_fwd(a_pre, cw, cb, name):
    s = a_pre.shape[1]
    tc, nj, a_spec, w_specs, b_specs = _conv_specs(s)

    def body(a_ref, wg_ref, wu_ref, bg_ref, bu_ref, act_ref, dgu_ref, pad_ref):
        _zero_pad_rows(pad_ref)

        def conv(a, w_ref, b_ref):
            prev, nxt = _shift_rows(a, pad_ref)
            return b_ref[...] + prev * w_ref[0:1, :] + a * w_ref[1:2, :] + nxt * w_ref[2:3, :]

        g = conv(a_ref[0].astype(F32), wg_ref, bg_ref)
        u = conv(a_ref[1].astype(F32), wu_ref, bu_ref)
        sg = 1.0 / (1.0 + jnp.exp(-g))
        silu = g * sg
        act_ref[...] = (silu * u).astype(BF16)
        dgu_ref[0] = (u * (sg * (1.0 + g * (1.0 - sg)))).astype(BF16)
        dgu_ref[1] = silu.astype(BF16)

    return _ordered_call(
        body, name=name, out_shape=(jax.ShapeDtypeStruct((s, D_FF), BF16), jax.ShapeDtypeStruct((2, s, D_FF), BF16)),
        grid=(nj,), in_specs=[a_spec] + w_specs + b_specs,
        out_specs=(pl.BlockSpec((s, tc), lambda j: (0, j)), pl.BlockSpec((2, s, tc), lambda j: (0, 0, j))),
        scratch_shapes=[pltpu.VMEM((s + 2 * PAD_ROWS, tc), F32)], compiler_params=_params(("parallel",)),
    )(a_pre, cw, cw, cb, cb)


def _conv_gate_bwd(a_pre, dgu, cw, dact, name):
    s = a_pre.shape[1]
    tc, nj, a_spec, w_specs, _ = _conv_specs(s)

    def body(a_ref, dgu_ref, wg_ref, wu_ref, dact_ref, dap_ref, dcw_ref, dcb_ref, pad_ref):
        _zero_pad_rows(pad_ref)
        dact_v = dact_ref[...].astype(F32)
        for part, w_ref in enumerate((wg_ref, wu_ref)):
            da = dact_v * dgu_ref[part].astype(F32)
            a = a_ref[part].astype(F32)
            prev, nxt = _shift_rows(a, pad_ref)
            dcw_ref[part, 0:1, :] = _sum_rows(prev * da)
            dcw_ref[part, 1:2, :] = _sum_rows(a * da)
            dcw_ref[part, 2:3, :] = _sum_rows(nxt * da)
            dcb_ref[part] = _sum_rows(da)
            da_prev, da_next = _shift_rows(da, pad_ref)
            dap_ref[part] = (da_next * w_ref[0:1, :] + da * w_ref[1:2, :] + da_prev * w_ref[2:3, :]).astype(BF16)

    return _ordered_call(
        body, name=name,
        out_shape=(jax.ShapeDtypeStruct((2, s, D_FF), BF16), jax.ShapeDtypeStruct((2, 3, D_FF), F32),
                   jax.ShapeDtypeStruct((2, 1, D_FF), F32)),
        grid=(nj,),
        in_specs=[a_spec, pl.BlockSpec((2, s, tc), lambda j: (0, 0, j))] + w_specs + [pl.BlockSpec((s, tc), lambda j: (0, j))],
        out_specs=(pl.BlockSpec((2, s, tc), lambda j: (0, 0, j)), pl.BlockSpec((2, 3, tc), lambda j: (0, 0, j)),
                   pl.BlockSpec((2, 1, tc), lambda j: (0, 0, j))),
        scratch_shapes=[pltpu.VMEM((s + 2 * PAD_ROWS, tc), F32)], compiler_params=_params(("parallel",)),
    )(a_pre, dgu, cw, cw, dact)


def _loss_head(y, target, name):
    s, d = y.shape
    tr = _row_tile(s)

    def body(y_ref, t_ref, loss_ref, dy_ref, dyb_ref):
        err = y_ref[...] - t_ref[...]

        @pl.when(pl.program_id(0) == 0)
        def _():
            loss_ref[...] = jnp.zeros_like(loss_ref)

        loss_ref[...] += jnp.broadcast_to(0.5 * _sum_all(_mean_last(err * err)), (8, 128))
        dy = err * (1.0 / d)
        dy_ref[...] = dy
        dyb_ref[...] = dy.astype(BF16)

    return _ordered_call(
        body, name=name,
        out_shape=(jax.ShapeDtypeStruct((8, 128), F32), jax.ShapeDtypeStruct((s, d), F32), jax.ShapeDtypeStruct((s, d), BF16)),
        grid=(s // tr,), in_specs=[_rows(d, tr), _rows(d, tr)],
        out_specs=(_const2((8, 128)), _rows(d, tr), _rows(d, tr)), compiler_params=_params(("arbitrary",)),
    )(y, target)


def _row_block(rows, cols, budget=1 << 20):
    if rows * cols <= budget:
        return rows
    best = None
    for tr in range(16, rows, 16):
        if rows % tr == 0 and tr * cols <= budget:
            best = tr
    assert best is not None, (rows, cols)
    return best


def _place_shard(x4, layer, j_arr, out_dtype, name):
    _, nh, r, cols = x4.shape
    tr = _row_block(r, cols)

    def body(j_ref, x_ref, o_ref):
        o_ref[...] = x_ref[...].astype(out_dtype)

    grid_spec = pltpu.PrefetchScalarGridSpec(
        num_scalar_prefetch=1, grid=(nh, r // tr),
        in_specs=[pl.BlockSpec((None, None, tr, cols), lambda h, i, j_ref: (layer, h, i, 0))],
        out_specs=pl.BlockSpec((None, None, tr, cols), lambda h, i, j_ref: (j_ref[0], h, i, 0)))
    return _ordered_call(
        body, name=name, out_shape=jax.ShapeDtypeStruct((N_CHIPS, nh, r, cols), out_dtype), grid_spec=grid_spec,
        compiler_params=_params(("parallel", "parallel")),
    )(j_arr, x4)


def _adamw(w, g, m, v, name, budget=1 << 18):
    rows, cols = w.shape
    tr = _row_block(rows, cols, budget)

    def body(w_ref, g_ref, m_ref, v_ref, go_ref, d_ref, nm_ref, nv_ref):
        gv = g_ref[...]
        go_ref[...] = gv
        mn = ADAM_B1 * m_ref[...] + (1.0 - ADAM_B1) * gv
        vn = ADAM_B2 * v_ref[...] + (1.0 - ADAM_B2) * (gv * gv)
        m_hat = mn / (1.0 - ADAM_B1 ** ADAM_STEP)
        v_hat = vn / (1.0 - ADAM_B2 ** ADAM_STEP)
        d_ref[...] = -ADAM_LR * (m_hat / (jnp.sqrt(v_hat) + ADAM_EPS) + ADAM_WD * w_ref[...])
        nm_ref[...] = mn
        nv_ref[...] = vn

    sds = jax.ShapeDtypeStruct((rows, cols), F32)
    return _ordered_call(
        body, name=name, out_shape=(sds, sds, sds, sds), grid=(rows // tr,),
        in_specs=[_rows(cols, tr)] * 4, out_specs=(_rows(cols, tr),) * 4, compiler_params=_params(("parallel",)),
    )(w, g, m, v)


def _pair_sum(g5, recv, c_arr, name):
    _, _, rh, cols = g5.shape
    tr = _row_block(rh, cols)

    def body(c_ref, g_ref, r_ref, o_ref):
        o_ref[...] = (g_ref[...].astype(F32) + r_ref[...].astype(F32)).astype(BF16)

    grid_spec = pltpu.PrefetchScalarGridSpec(
        num_scalar_prefetch=1, grid=(N_CHIPS, rh // tr),
        in_specs=[pl.BlockSpec((None, None, tr, cols), lambda j, i, c_ref: (j, c_ref[0], i, 0)),
                  pl.BlockSpec((None, tr, cols), lambda j, i, c_ref: (j, i, 0))],
        out_specs=pl.BlockSpec((None, tr, cols), lambda j, i, c_ref: (j, i, 0)))
    return _ordered_call(
        body, name=name, out_shape=jax.ShapeDtypeStruct((N_CHIPS, rh, cols), BF16), grid_spec=grid_spec,
        compiler_params=_params(("parallel", "parallel")),
    )(c_arr, g5, recv)


def _chip_sum(p4, recv3, j_arr, c_arr, name):
    _, rh, cols = p4.shape
    tr = _row_block(rh, cols, 1 << 19)

    def body(j_ref, c_ref, p_ref, r_ref, o_ref):
        total = p_ref[...].astype(F32)
        for peer in range(3):
            total = total + r_ref[peer].astype(F32)
        o_ref[...] = total

    grid_spec = pltpu.PrefetchScalarGridSpec(
        num_scalar_prefetch=2, grid=(rh // tr,),
        in_specs=[pl.BlockSpec((None, tr, cols), lambda i, j_ref, c_ref: (j_ref[0], i, 0)),
                  pl.BlockSpec((3, tr, cols), lambda i, j_ref, c_ref: (0, i, 0))],
        out_specs=pl.BlockSpec((None, tr, cols), lambda i, j_ref, c_ref: (c_ref[0], i, 0)))
    return _ordered_call(
        body, name=name, out_shape=jax.ShapeDtypeStruct((2, rh, cols), F32), grid_spec=grid_spec,
        compiler_params=_params(("parallel",)),
    )(j_arr, c_arr, p4, recv3)


def _adamw_layer(w, g, m, v, layer, into, name):
    nl, rows, cols = w.shape
    tr = _row_block(rows, cols, 1 << 18)
    at_layer = pl.BlockSpec((None, tr, cols), lambda i: (layer, i, 0))

    def body(w_ref, g_ref, m_ref, v_ref, *rest):
        go_ref, d_ref, nm_ref, nv_ref = rest[-4:]
        gv = g_ref[...]
        go_ref[...] = gv
        mn = ADAM_B1 * m_ref[...] + (1.0 - ADAM_B1) * gv
        vn = ADAM_B2 * v_ref[...] + (1.0 - ADAM_B2) * (gv * gv)
        m_hat = mn / (1.0 - ADAM_B1 ** ADAM_STEP)
        v_hat = vn / (1.0 - ADAM_B2 ** ADAM_STEP)
        d_ref[...] = -ADAM_LR * (m_hat / (jnp.sqrt(v_hat) + ADAM_EPS) + ADAM_WD * w_ref[...])
        nm_ref[...] = mn
        nv_ref[...] = vn

    in_specs = [at_layer, _rows(cols, tr), at_layer, at_layer]
    operands = [w, g, m, v]
    aliases = {}
    if into is not None:
        in_specs += [ANY] * 4
        operands += list(into)
        aliases = {4 + i: i for i in range(4)}
    sds = jax.ShapeDtypeStruct((nl, rows, cols), F32)
    return _ordered_call(
        body, name=name, out_shape=(sds,) * 4, grid=(rows // tr,), in_specs=in_specs, out_specs=(at_layer,) * 4,
        input_output_aliases=aliases, compiler_params=_params(("parallel",)),
    )(*operands)


def _sum_devices(mine, landed, me_arr, name):
    rows, lanes = mine.shape

    def body(me_ref, mine_ref, landed_ref, o_ref):
        total = None
        for dev in range(8):
            part = jnp.where(me_ref[0] == dev, mine_ref[...], landed_ref[dev])
            total = part if total is None else total + part
        o_ref[...] = total

    grid_spec = pltpu.PrefetchScalarGridSpec(
        num_scalar_prefetch=1, grid=(1,),
        in_specs=[pl.BlockSpec((rows, lanes), lambda i, me_ref: (0, 0)), pl.BlockSpec((8, rows, lanes), lambda i, me_ref: (0, 0, 0))],
        out_specs=pl.BlockSpec((rows, lanes), lambda i, me_ref: (0, 0)))
    return _ordered_call(
        body, name=name, out_shape=jax.ShapeDtypeStruct((rows, lanes), F32), grid_spec=grid_spec,
        compiler_params=_params(("arbitrary",)),
    )(me_arr, mine, landed)


def _place():
    x, y, c = lax.axis_index("x"), lax.axis_index("y"), lax.axis_index("c")
    chips = [(1 - x, y), (x, 1 - y), (1 - x, 1 - y)]
    return x, y, c, chips


HBM = pl.BlockSpec(memory_space=pltpu.HBM)
SEM = pl.BlockSpec(memory_space=pltpu.SEMAPHORE)
TOKEN = jax.ShapeDtypeStruct((8, 128), F32)


def _remote(src, dst, send_sem, recv_sem, to):
    return pltpu.make_async_remote_copy(src_ref=src, dst_ref=dst, send_sem=send_sem, recv_sem=recv_sem, device_id=to,
                                        device_id_type=MESH)


def _split_call(body, name, thru, sems_in=(), fresh=(), new_sems=(), after_last=True):
    n_t, n_s, n_f = len(thru), len(sems_in), len(fresh)

    def call_body(*refs):
        outs = refs[n_t + n_s:]
        body(refs[:n_t], refs[n_t:n_t + n_s], outs[1 + n_t:1 + n_t + n_f], outs[1 + n_t + n_f:])
        outs[0][...] = jnp.zeros_like(outs[0])

    out_shape = ([TOKEN] + [pltpu.HBM(t.shape, t.dtype) for t in thru] + [pltpu.HBM(shp, dt) for shp, dt in fresh]
                 + [pltpu.SemaphoreType.DMA(shp) for shp in new_sems])
    out_specs = [pl.BlockSpec(memory_space=pltpu.VMEM)] + [HBM] * (n_t + n_f) + [SEM] * len(new_sems)
    if not after_last:
        _Order.last = None
    out = _ordered_call(
        call_body, name=name, out_shape=tuple(out_shape), in_specs=[HBM] * n_t + [SEM] * n_s, out_specs=tuple(out_specs),
        input_output_aliases={i: 1 + i for i in range(n_t)},
        compiler_params=pltpu.CompilerParams(has_side_effects=pltpu.SideEffectType.DATAFLOW_SIDE_EFFECTING),
    )(*[pltpu.with_memory_space_constraint(t, pltpu.HBM) for t in thru], *sems_in)
    return out[1:1 + n_t], out[1 + n_t:1 + n_t + n_f], out[1 + n_t + n_f:]


class _Exchange:
    def __init__(self, weights, m_in, v_in, j_arr, c_arr, me_arr):
        self.w, self.m, self.v = weights, m_in, v_in
        self.j_arr, self.c_arr, self.me_arr = j_arr, c_arr, me_arr
        self.adam, self.small = {}, {}
        self.groups = [(l, name) for l in range(DEPTH) for name in BIG_NAMES]
        self.shard_shape = {name: weights[name].shape[1:] for name in BIG_NAMES}
        self.conv_state, self.state = [], {}
        self.ready, self.conv_ready = {}, {}
        self.pending, self.tick, self.reduced = [], 0, {}

        def place(grp):
            l, name = grp
            nl, r, cols = weights[name].shape
            return _place_shard(weights[name].reshape(nl, 2, r // 2, cols), l, j_arr, BF16, f"place_{name}_l{l}")

        def start_copies(tag, convs, groups, bufs):
            n_c = len(convs)

            def start(thru, _, __, sems):
                x, y, c, chips = _place()
                j_me = 2 * x + y
                copies = []
                for i in range(len(thru)):
                    mine = thru[i].at[j_me] if i < n_c else thru[i].at[j_me, c]
                    copies += [_remote(mine, mine, sems[2 * i].at[k], sems[2 * i + 1].at[k], (*chip, c))
                               for k, chip in enumerate(chips)]
                for cp in copies:
                    cp.start()

            thru, _, sems = _split_call(start, tag, convs + bufs, new_sems=[(3,)] * (2 * (n_c + len(bufs))))
            self.conv_state += [(thru[i], sems[2 * i], sems[2 * i + 1]) for i in range(n_c)]
            for g, grp in enumerate(groups):
                self.state[grp] = (thru[n_c + g], sems[2 * (n_c + g)], sems[2 * (n_c + g) + 1])

        convs = [_place_shard(weights["conv_w"][:, None], l, j_arr, F32, f"place_conv_w_l{l}") for l in range(DEPTH)]
        start_copies("gather_start_first", convs, self.groups[:1], [place(self.groups[0])])
        start_copies("gather_start_rest", [], self.groups[1:], [place(grp) for grp in self.groups[1:]])

    def conv_w(self, l):
        if l not in self.conv_ready:
            buf, send, recv = self.conv_state[l]

            def wait(thru, sems, _, __):
                x, y, c, chips = _place()
                for k, chip in enumerate(chips):
                    mine, theirs = thru[0].at[2 * x + y], thru[0].at[2 * chip[0] + chip[1]]
                    _remote(mine, mine, sems[0].at[k], sems[1].at[k], (*chip, c)).wait_send()
                    _remote(theirs, theirs, sems[0].at[k], sems[1].at[k], (x, y, c)).wait_recv()

            (buf,), _, _ = _split_call(wait, f"gather_conv_w_l{l}", [buf], sems_in=[send, recv])
            self.conv_ready[l] = jnp.transpose(buf[:, 0], (1, 0, 2)).reshape(3, 2 * D_FF)
        return self.conv_ready[l]

    def weight(self, l, name):
        grp = (l, name)
        if grp not in self.ready:
            buf, send, recv = self.state[grp]

            def forward(thru, sems, _, new):
                x, y, c, chips = _place()
                for k, chip in enumerate(chips):
                    landed = thru[0].at[2 * chip[0] + chip[1], c]
                    _remote(landed, landed, new[0].at[k], sems[0].at[k], (x, y, c)).wait_recv()
                    _remote(landed, landed, new[0].at[k], new[1].at[k], (x, y, 1 - c)).start()

            (buf,), _, (fsend, frecv) = _split_call(forward, f"gather_pass_{name}_l{l}", [buf], sems_in=[recv],
                                                    new_sems=[(3,), (3,)])

            def finish(thru, sems, _, __):
                x, y, c, chips = _place()
                mine = thru[0].at[2 * x + y, c]
                for k, chip in enumerate(chips):
                    j_k = 2 * chip[0] + chip[1]
                    theirs, landed = thru[0].at[j_k, 1 - c], thru[0].at[j_k, c]
                    _remote(theirs, theirs, sems[1].at[k], sems[2].at[k], (x, y, c)).wait_recv()
                    _remote(landed, landed, sems[1].at[k], sems[2].at[k], (x, y, 1 - c)).wait_send()
                    _remote(mine, mine, sems[0].at[k], sems[2].at[k], (*chip, c)).wait_send()

            (buf,), _, _ = _split_call(finish, f"gather_done_{name}_l{l}", [buf], sems_in=[send, fsend, frecv])
            r, cols = self.shard_shape[name]
            self.ready[grp] = buf.reshape(N_CHIPS, r, cols) if name in ("w_in", "w_up") else buf.reshape(N_CHIPS * r, cols)
        return self.ready[grp]

    def grad(self, l, name, g):
        r, cols = self.shard_shape[name]
        g5 = g.reshape(N_CHIPS, 2, r // 2, cols)

        def start(thru, _, fresh, sems):
            x, y, c, _chips = _place()
            _remote(thru[0].at[:, 1 - c], fresh[0], sems[0], sems[1], (x, y, 1 - c)).start()

        (g5,), (recv,), sems = _split_call(start, f"pair_start_{name}_l{l}", [g5], fresh=[((N_CHIPS, r // 2, cols), BF16)],
                                          new_sems=[(), ()], after_last=False)
        self.pending.append(dict(l=l, name=name, stage=1, at=self.tick, bufs=(g5, recv), sems=sems))

    def _pair(self, grp):
        l, name = grp["l"], grp["name"]
        r, cols = self.shard_shape[name]

        def wait(thru, sems, _, __):
            x, y, c, _chips = _place()
            cp = _remote(thru[0].at[:, 1 - c], thru[1], sems[0], sems[1], (x, y, 1 - c))
            cp.wait_send()
            cp.wait_recv()

        (g5, recv), _, _ = _split_call(wait, f"pair_done_{name}_l{l}", list(grp["bufs"]), sems_in=list(grp["sems"]))
        p4 = _pair_sum(g5, recv, self.c_arr, f"pair_sum_{name}_l{l}")

        def start(thru, _, fresh, sems):
            x, y, c, chips = _place()
            for k, chip in enumerate(chips):
                _remote(thru[0].at[2 * chip[0] + chip[1]], fresh[0].at[k], sems[0].at[k], sems[1].at[k], (*chip, c)).start()

        (p4,), (recv3,), sems = _split_call(start, f"chips_start_{name}_l{l}", [p4], fresh=[((3, r // 2, cols), BF16)],
                                           new_sems=[(3,), (3,)], after_last=False)
        grp.update(stage=2, at=self.tick, bufs=(p4, recv3), sems=sems)

    def _chips(self, grp):
        l, name = grp["l"], grp["name"]

        def wait(thru, sems, _, __):
            x, y, c, chips = _place()
            for k, chip in enumerate(chips):
                cp = _remote(thru[0].at[2 * chip[0] + chip[1]], thru[1].at[k], sems[0].at[k], sems[1].at[k], (*chip, c))
                cp.wait_send()
                cp.wait_recv()

        (p4, recv3), _, _ = _split_call(wait, f"chips_done_{name}_l{l}", list(grp["bufs"]), sems_in=list(grp["sems"]))
        half = _chip_sum(p4, recv3, self.j_arr, self.c_arr, f"chip_sum_{name}_l{l}")

        def start(thru, _, __, sems):
            x, y, c, _chips = _place()
            _remote(thru[0].at[c], thru[0].at[c], sems[0], sems[1], (x, y, 1 - c)).start()

        (half,), _, sems = _split_call(start, f"join_start_{name}_l{l}", [half], new_sems=[(), ()], after_last=False)
        grp.update(stage=3, at=self.tick, bufs=(half,), sems=sems)

    def _update(self, grp):
        l, name = grp["l"], grp["name"]

        def wait(thru, sems, _, __):
            x, y, c, _chips = _place()
            _remote(thru[0].at[c], thru[0].at[c], sems[0], sems[1], (x, y, 1 - c)).wait_send()
            _remote(thru[0].at[1 - c], thru[0].at[1 - c], sems[0], sems[1], (x, y, c)).wait_recv()

        (full,), _, _ = _split_call(wait, f"join_done_{name}_l{l}", list(grp["bufs"]), sems_in=list(grp["sems"]))
        self.adam[name] = _adamw_layer(self.w[name], full.reshape(self.shard_shape[name]), self.m[name], self.v[name], l,
                                       self.adam.get(name), f"adamw_{name}_l{l}")
        grp.update(stage=4)

    def point(self, drain=False):
        self.tick += 1
        for grp in self.pending:
            if grp["stage"] == 3 and (drain or grp["at"] < self.tick):
                self._update(grp)
            elif grp["stage"] == 2 and (drain or grp["at"] + 2 <= self.tick):
                self._chips(grp)
            elif grp["stage"] == 1 and (drain or grp["at"] < self.tick):
                self._pair(grp)

    def finish(self):
        while any(grp["stage"] < 4 for grp in self.pending):
            self.point(drain=True)
        return self.adam

    @staticmethod
    def _peer(k, x, y, c):
        return (1 - x if k & 4 else x, 1 - y if k & 2 else y, 1 - c if k & 1 else c)

    def small_grads(self, l, grads, loss_tile):
        parts = [grads[nm] for nm in SMALL_NAMES] + ([loss_tile[0, 0:1]] if loss_tile is not None else [])
        packed = _pack_call(parts, f"small_pack_l{l}")
        rows = packed.shape[0]

        def start(thru, _, fresh, sems):
            x, y, c, _chips = _place()
            for k in range(1, 8):
                _remote(thru[0], fresh[0].at[4 * x + 2 * y + c], sems[0].at[k - 1], sems[1].at[k - 1],
                        self._peer(k, x, y, c)).start()

        (packed,), (landed,), sems = _split_call(start, f"small_start_l{l}", [packed], fresh=[((8, rows, PACK_LANES), F32)],
                                                 new_sems=[(7,), (7,)], after_last=False)
        self.small[l] =(packed, landed, sems, [p.shape for p in parts])

    def small_sum(self, l):
        packed, landed, sems, _shapes = self.small[l]

        def wait(thru, sems, _, __):
            x, y, c, _chips = _place()
            for k in range(1, 8):
                px, py, pc = self._peer(k, x, y, c)
                _remote(thru[0], thru[1].at[4 * x + 2 * y + c], sems[0].at[k - 1], sems[1].at[k - 1], (px, py, pc)).wait_send()
                _remote(thru[0], thru[1].at[4 * px + 2 * py + pc], sems[0].at[k - 1], sems[1].at[k - 1], (x, y, c)).wait_recv()

        (packed, landed), _, _ = _split_call(wait, f"small_done_l{l}", [packed, landed], sems_in=list(sems))
        return _sum_devices(packed, landed, self.me_arr, f"small_sum_l{l}")


def _rope_tables(s):
    inv_freq = ROPE_THETA ** (-jnp.arange(0, HEAD_DIM, 2, dtype=F32) / HEAD_DIM)
    ang = jnp.arange(s, dtype=F32)[:, None] * inv_freq[None, :]
    cos, sin = jnp.cos(ang), jnp.sin(ang)
    return jnp.concatenate([cos, cos], axis=-1), jnp.concatenate([-sin, sin], axis=-1)


def _local_step(x, target, ex, small):
    s = x.shape[0]
    cosf, sinf = _rope_tables(s)
    saved = []
    for l in range(DEPTH):
        p = small[l]
        t = f"l{l}"
        h = _rms_fwd(x, p["norm1_g"], f"norm1_{t}")
        z = _matmul(h, ex.weight(l, "w_in"), mode="nn", out_dtype=BF16, tm=1024, tn=896, tk=2048, b_parts=4, name=f"proj_in_{t}")
        qn, kn, vb, ug, vn = _proj_post(z, p["q_norm_g"], p["k_norm_g"], p["sgu_ln_g"], p["sgu_ln_b"], cosf, sinf, f"proj_post_{t}")
        attn, sgu, mixed = _mixer_fwd(qn, kn, vb, ug, vn, p["w_s_bf16"], p["b_s_tile"], p["sink"], p["attn_out_g"],
                                      p["sgu_out_g"], f"mixer_{t}")
        x1 = _matmul(mixed, ex.weight(l, "w_o"), mode="nn", out_dtype=F32, tm=1024, tn=512, tk=2048, res=x, name=f"proj_out_{t}")
        h2 = _rms_fwd(x1, p["norm2_g"], f"norm2_{t}")
        a_pre = _matmul(h2, ex.weight(l, "w_up"), mode="nn", out_dtype=BF16, tm=1024, tn=1408, tk=2048, b_parts=4, out_parts=2,
                        name=f"ffn_up_{t}")
        act, dgu = _conv_gate_fwd(a_pre, ex.conv_w(l), p["conv_b"], f"conv_gate_{t}")
        x2 = _matmul(act, ex.weight(l, "w_down"), mode="nn", out_dtype=F32, tm=512, tn=512, tk=D_FF, res=x1, name=f"ffn_down_{t}")
        saved.append(dict(x=x, h=h, z=z, qn=qn, kn=kn, vb=vb, ug=ug, vn=vn, attn=attn, sgu=sgu, mixed=mixed, x1=x1, h2=h2,
                          a_pre=a_pre, act=act, dgu=dgu))
        x = x2
    loss_tile, dx, dxb = _loss_head(x, target, "loss_head")
    for l in reversed(range(DEPTH)):
        p, sv = small[l], saved[l]
        t = f"l{l}"
        ex.grad(l, "w_down", _matmul(sv["act"], dxb, mode="tn", out_dtype=BF16, tm=512, tn=1024, tk=2048, name=f"g_w_down_{t}"))
        dact = _matmul(dxb, ex.weight(l, "w_down"), mode="nt", out_dtype=BF16, tm=1024, tn=512, tk=2048, name=f"d_act_{t}")
        dap, dcw, dcb = _conv_gate_bwd(sv["a_pre"], sv["dgu"], ex.conv_w(l), dact, f"conv_gate_bwd_{t}")
        ex.point()
        ex.grad(l, "w_up", _matmul(sv["h2"], dap, mode="tn", out_dtype=BF16, tm=1024, tn=1408, tk=2048, b_parts=2, out_parts=4,
                                   name=f"g_w_up_{t}"))
        dh2 = _matmul_nt_slabs(dap, ex.weight(l, "w_up"), tm=512, tn=256, a_parts=2, name=f"d_h2_{t}")
        dx1, dx1b, dg2 = _rms_bwd(sv["x1"], p["norm2_g"], dh2, dx, f"norm2_bwd_{t}")
        ex.point()
        ex.grad(l, "w_o", _matmul(sv["mixed"], dx1b, mode="tn", out_dtype=BF16, tm=1024, tn=512, tk=2048, name=f"g_w_o_{t}"))
        dmixed = _matmul(dx1b, ex.weight(l, "w_o"), mode="nt", out_dtype=F32, tm=1024, tn=512, tk=2048, name=f"d_mixed_{t}")
        dqn, dkn, dvb, dug, dvn, dws, dbs, dsk, dga, dgs = _mixer_bwd(
            sv["qn"], sv["kn"], sv["vb"], sv["ug"], sv["vn"], sv["attn"], sv["sgu"], dmixed, p["w_s_bf16"], p["b_s_tile"],
            p["sink"], p["attn_out_g"], p["sgu_out_g"], f"mixer_bwd_{t}")
        dz, dqg, dkg, dlg, dlb = _proj_post_bwd(sv["z"], dqn, dkn, dvb, dug, dvn, p["q_norm_g"], p["k_norm_g"], p["sgu_ln_g"],
                                                 cosf, sinf, f"proj_post_bwd_{t}")
        ex.point()
        ex.grad(l, "w_in", _matmul(sv["h"], dz, mode="tn", out_dtype=BF16, tm=1024, tn=896, tk=2048, out_parts=4,
                                   name=f"g_w_in_{t}"))
        dh = _matmul_nt_slabs(dz, ex.weight(l, "w_in"), tm=1024, tn=512, name=f"d_h_{t}")
        dx, dxb, dg1 = _rms_bwd(sv["x"], p["norm1_g"], dh, dx1, f"norm1_bwd_{t}")
        ex.point()
        ex.small_grads(l, dict(
            norm1_g=dg1[0], q_norm_g=dqg[0], k_norm_g=dkg[0], sink=dsk[:, 0], sgu_ln_g=dlg[0], sgu_ln_b=dlb[0], w_s=dws,
            b_s=dbs[:, :, 0], attn_out_g=dga[0], sgu_out_g=dgs[0], norm2_g=dg2[0],
            conv_w=jnp.concatenate([dcw[0], dcw[1]], axis=-1), conv_b=jnp.concatenate([dcb[0, 0], dcb[1, 0]], axis=-1)),
            loss_tile if l == 0 else None)
    return dx


def _small_views(l, norm1_g, q_norm_g, k_norm_g, sink, sgu_ln_g, sgu_ln_b, w_s, b_s, attn_out_g, sgu_out_g, norm2_g, conv_b):
    return dict(
        norm1_g=norm1_g[l][None], q_norm_g=q_norm_g[l][None], k_norm_g=k_norm_g[l][None], sink=sink[l],
        sgu_ln_g=sgu_ln_g[l][None], sgu_ln_b=sgu_ln_b[l][None], w_s_bf16=w_s[l].astype(BF16),
        b_s_tile=jnp.broadcast_to(b_s[l][:, :, None], (N_GMLP_HEADS, BLOCK, BLOCK)), attn_out_g=attn_out_g[l][None],
        sgu_out_g=sgu_out_g[l][None], norm2_g=norm2_g[l][None], conv_b=conv_b[l][None])


SMALL_NAMES = ("norm1_g", "q_norm_g", "k_norm_g", "sink", "sgu_ln_g", "sgu_ln_b", "w_s", "b_s", "attn_out_g", "sgu_out_g",
               "norm2_g", "conv_b", "conv_w")
REPLICATED_NAMES = SMALL_NAMES[:-1]
BIG_NAMES = ("w_in", "w_o", "w_up", "w_down")
PACK_LANES = 128
PACK_ALIGN = 8 * PACK_LANES


def _pack_rows(shape):
    return -(-math.prod(shape) // PACK_ALIGN) * 8


def _pack_parts(arrays):
    parts = []
    for a in arrays:
        flat = a.reshape(-1)
        parts.append(jnp.pad(flat, (0, _pack_rows(a.shape) * PACK_LANES - flat.shape[0])).reshape(-1, PACK_LANES))
    return parts


def _pack(arrays):
    return jnp.concatenate(_pack_parts(arrays), axis=0)


def _pack_call(arrays, name):
    parts = _pack_parts(arrays)
    total = sum(p.shape[0] for p in parts)

    def body(*refs):
        o_ref, at = refs[-1], 0
        for p_ref in refs[:-1]:
            o_ref[at:at + p_ref.shape[0], :] = p_ref[...]
            at += p_ref.shape[0]

    vm = pl.BlockSpec(memory_space=pltpu.VMEM)
    return _ordered_call(
        body, name=name, out_shape=jax.ShapeDtypeStruct((total, PACK_LANES), F32), in_specs=[vm] * len(parts), out_specs=vm,
        compiler_params=pltpu.CompilerParams(vmem_limit_bytes=V7X_VMEM_LIMIT),
    )(*parts)


def _unpack_layers(stacked, shapes):
    nl = stacked.shape[0]
    out, at = [], 0
    for shp in shapes:
        rows = _pack_rows(shp)
        out.append(stacked[:, at:at + rows].reshape(nl, -1)[:, :math.prod(shp)].reshape((nl,) + tuple(shp)))
        at += rows
    return out


def _adamw_packed(w, g, m, v, rows, layer, into, name):
    head = pl.BlockSpec((rows, PACK_LANES), lambda i: (0, 0))
    at_layer = pl.BlockSpec((None, rows, PACK_LANES), lambda i: (layer, 0, 0))

    def body(w_ref, g_ref, m_ref, v_ref, *rest):
        d_ref, nm_ref, nv_ref = rest[-3:]
        gv = g_ref[...]
        mn = ADAM_B1 * m_ref[...] + (1.0 - ADAM_B1) * gv
        vn = ADAM_B2 * v_ref[...] + (1.0 - ADAM_B2) * (gv * gv)
        m_hat = mn / (1.0 - ADAM_B1 ** ADAM_STEP)
        v_hat = vn / (1.0 - ADAM_B2 ** ADAM_STEP)
        d_ref[...] = -ADAM_LR * (m_hat / (jnp.sqrt(v_hat) + ADAM_EPS) + ADAM_WD * w_ref[...])
        nm_ref[...] = mn
        nv_ref[...] = vn

    in_specs = [head] * 4
    operands = [w, g, m, v]
    aliases = {}
    if into is not None:
        in_specs += [ANY] * 3
        operands += list(into)
        aliases = {4 + i: i for i in range(3)}
    sds = jax.ShapeDtypeStruct((DEPTH, rows, PACK_LANES), F32)
    return _ordered_call(
        body, name=name, out_shape=(sds,) * 3, grid=(1,), in_specs=in_specs, out_specs=(at_layer,) * 3,
        input_output_aliases=aliases, compiler_params=_params(("arbitrary",)),
    )(*operands)


def kernel(x, norm1_g, w_in, q_norm_g, k_norm_g, sink, sgu_ln_g, sgu_ln_b, w_s, b_s, attn_out_g, sgu_out_g, w_o, norm2_g, w_up, conv_w, conv_b, w_down, loss_target, m_norm1_g, m_w_in, m_q_norm_g, m_k_norm_g, m_sink, m_sgu_ln_g, m_sgu_ln_b, m_w_s, m_b_s, m_attn_out_g, m_sgu_out_g, m_w_o, m_norm2_g, m_w_up, m_conv_w, m_conv_b, m_w_down, v_norm1_g, v_w_in, v_q_norm_g, v_k_norm_g, v_sink, v_sgu_ln_g, v_sgu_ln_b, v_w_s, v_b_s, v_attn_out_g, v_sgu_out_g, v_w_o, v_norm2_g, v_w_up, v_conv_w, v_conv_b, v_w_down):
    weights = dict(norm1_g=norm1_g, w_in=w_in, q_norm_g=q_norm_g, k_norm_g=k_norm_g, sink=sink, sgu_ln_g=sgu_ln_g,
                   sgu_ln_b=sgu_ln_b, w_s=w_s, b_s=b_s, attn_out_g=attn_out_g, sgu_out_g=sgu_out_g, w_o=w_o, norm2_g=norm2_g,
                   w_up=w_up, conv_w=conv_w, conv_b=conv_b, w_down=w_down)
    m_in = dict(norm1_g=m_norm1_g, w_in=m_w_in, q_norm_g=m_q_norm_g, k_norm_g=m_k_norm_g, sink=m_sink, sgu_ln_g=m_sgu_ln_g,
                sgu_ln_b=m_sgu_ln_b, w_s=m_w_s, b_s=m_b_s, attn_out_g=m_attn_out_g, sgu_out_g=m_sgu_out_g, w_o=m_w_o,
                norm2_g=m_norm2_g, w_up=m_w_up, conv_w=m_conv_w, conv_b=m_conv_b, w_down=m_w_down)
    v_in = dict(norm1_g=v_norm1_g, w_in=v_w_in, q_norm_g=v_q_norm_g, k_norm_g=v_k_norm_g, sink=v_sink, sgu_ln_g=v_sgu_ln_g,
                sgu_ln_b=v_sgu_ln_b, w_s=v_w_s, b_s=v_b_s, attn_out_g=v_attn_out_g, sgu_out_g=v_sgu_out_g, w_o=v_w_o,
                norm2_g=v_norm2_g, w_up=v_w_up, conv_w=v_conv_w, conv_b=v_conv_b, w_down=v_w_down)
    cx, cy, cc = lax.axis_index("x"), lax.axis_index("y"), lax.axis_index("c")
    j_me = 2 * cx + cy
    c_arr = jnp.reshape(cc, (1,)).astype(jnp.int32)
    j_arr = jnp.reshape(j_me, (1,)).astype(jnp.int32)

    _Order.last = None
    ex = _Exchange(weights, m_in, v_in, j_arr, c_arr, jnp.reshape(4 * cx + 2 * cy + cc, (1,)).astype(jnp.int32))
    small = [_small_views(l, norm1_g, q_norm_g, k_norm_g, sink, sgu_ln_g, sgu_ln_b, w_s, b_s, attn_out_g, sgu_out_g, norm2_g,
                          conv_b) for l in range(DEPTH)]
    packed_in = [[_pack_call([src[nm][l] for nm in REPLICATED_NAMES], f"pack_{tag}_l{l}")
                  for tag, src in (("w", weights), ("m", m_in), ("v", v_in))] for l in range(DEPTH)]
    dx = _local_step(x[0], loss_target[0], ex, small)
    big_out = ex.finish()

    rep_shapes = [weights[nm].shape[1:] for nm in REPLICATED_NAMES]
    rep_rows = sum(_pack_rows(shp) for shp in rep_shapes)
    cw_shape = (3, 2 * D_FF)
    sums, adam_small = [None] * DEPTH, None
    for l in reversed(range(DEPTH)):
        sums[l] = ex.small_sum(l)
        pw, pm, pv = packed_in[l]
        adam_small = _adamw_packed(pw, sums[l], pm, pv, rep_rows, l, adam_small, f"adamw_small_l{l}")
    cw_rows = _pack_rows(cw_shape)
    loss = sums[0][rep_rows + cw_rows, 0]
    stacked = jnp.stack([sm[:rep_rows + cw_rows] for sm in sums])
    grads = dict(zip(REPLICATED_NAMES, _unpack_layers(stacked[:, :rep_rows], rep_shapes)))
    delta, new_m, new_v = (dict(zip(REPLICATED_NAMES, _unpack_layers(arr, rep_shapes))) for arr in adam_small)
    cw_cols = 2 * D_FF // N_CHIPS
    cw_grad = lax.dynamic_slice_in_dim(_unpack_layers(stacked[:, rep_rows:], [cw_shape])[0], j_me * cw_cols, cw_cols, axis=2)
    flat = lambda a: a.reshape(DEPTH * 3, cw_cols)
    cw_out = _adamw(flat(conv_w), flat(cw_grad), flat(m_conv_w), flat(v_conv_w), "adamw_conv_w")
    grads["conv_w"], delta["conv_w"], new_m["conv_w"], new_v["conv_w"] = (a.reshape(DEPTH, 3, cw_cols) for a in cw_out)

    for name in BIG_NAMES:
        grads[name], delta[name], new_m[name], new_v[name] = big_out[name]

    order = ("norm1_g", "w_in", "q_norm_g", "k_norm_g", "sink", "sgu_ln_g", "sgu_ln_b", "w_s", "b_s", "attn_out_g", "sgu_out_g",
             "w_o", "norm2_g", "w_up", "conv_w", "conv_b", "w_down")
    return (loss, dx[None], *[grads[nm] for nm in order], *[delta[nm] for nm in order], *[new_m[nm] for nm in order],
            *[new_v[nm] for nm in order])
```

```python
import functools
import math

import jax
import jax.numpy as jnp
from jax import lax
from jax.experimental import pallas as pl
from jax.experimental.pallas import tpu as pltpu

F32 = jnp.float32
BF16 = jnp.bfloat16

D_MODEL = 2048
HEAD_DIM = 128
ATTN_WIDTH = 1024
N_Q_HEADS = 8
N_KV_HEADS = 2
GQA_GROUP = 4
KV_WIDTH = 256
GMLP_WIDTH = 1024
N_GMLP_HEADS = 8
BLOCK = 128
IN_WIDTH = 3584
D_FF = 5632
DEPTH = 2
EPS = 1e-6
MASK_VALUE = -1e30
ROPE_THETA = 10000.0
N_CHIPS = 4

ADAM_LR = 0.001
ADAM_B1 = 0.9
ADAM_B2 = 0.999
ADAM_EPS = 1e-08
ADAM_WD = 0.01
ADAM_STEP = 10

V7X_VMEM_LIMIT = 48 * 1024 * 1024
MESH = pl.DeviceIdType.MESH

_GELU_C = math.sqrt(2.0 / math.pi)
_GELU_A = 0.044715


def _params(sem=None):
    return pltpu.CompilerParams(dimension_semantics=sem, vmem_limit_bytes=V7X_VMEM_LIMIT)


ANY = pl.BlockSpec(memory_space=pl.ANY)


class _Order:
    last = None


def _ordered_call(body, *, token_index=0, **kw):
    def run(*operands):
        tok = _Order.last
        if tok is None or any(op is tok for op in operands):
            call = pl.pallas_call(body, **kw)
        else:
            n_in = len(operands)

            def ordered_body(*refs):
                return body(*refs[:n_in], *refs[n_in + 1:])

            kw2 = dict(kw)
            if "grid_spec" in kw2:
                gs = kw2["grid_spec"]
                kw2["grid_spec"] = pltpu.PrefetchScalarGridSpec(
                    num_scalar_prefetch=gs.num_scalar_prefetch, grid=gs.grid, in_specs=list(gs.in_specs) + [ANY],
                    out_specs=gs.out_specs, scratch_shapes=gs.scratch_shapes)
            else:
                kw2["in_specs"] = list(kw2["in_specs"]) + [ANY]
            call = pl.pallas_call(ordered_body, **kw2)
            operands = operands + (tok,)
        out = call(*operands)
        _Order.last = out[token_index] if isinstance(out, (tuple, list)) else out
        return out

    return run


def _gelu(x):
    return x * (0.5 * (1.0 + jnp.tanh(_GELU_C * (x + _GELU_A * (x * x * x)))))


def _gelu_grad(x):
    x2 = x * x
    t = jnp.tanh(_GELU_C * (x + _GELU_A * (x * x2)))
    return 0.5 * (1.0 + t) + 0.5 * x * (1.0 - t * t) * (_GELU_C * (1.0 + 3.0 * _GELU_A * x2))


def _mean_last(x):
    return jnp.mean(x, axis=-1, keepdims=True)


def _sum_rows(x):
    return jnp.sum(x, axis=0, keepdims=True)


def _sum_all(x):
    return jnp.sum(jnp.sum(x, axis=1, keepdims=True), axis=0, keepdims=True)


def _matmul(a, b, *, mode, out_dtype, tm, tn, tk, name, res=None, a_parts=0, b_parts=0, out_parts=0, b_lead=()):
    b_full = b
    b = jax.ShapeDtypeStruct(b.shape[len(b_lead):], b.dtype)
    if mode == "nn":
        assert not a_parts
        m, k = a.shape
        n = b.shape[0] * b.shape[2] if b_parts else b.shape[1]
    elif mode == "nt":
        m, k = (a.shape[1], a.shape[0] * a.shape[2]) if a_parts else a.shape
        n = b.shape[1] if b_parts else b.shape[0]
    else:
        assert not a_parts
        k, m = a.shape
        n = b.shape[0] * b.shape[2] if b_parts else b.shape[1]
    tm, tn, tk = min(tm, m), min(tn, n), min(tk, k)
    assert m % tm == 0 and n % tn == 0 and k % tk == 0, (name, m, n, k, tm, tn, tk)
    nm, nn, nk = m // tm, n // tn, k // tk

    def slab(idx, total_tiles, parts):
        per = total_tiles // parts
        assert per * parts == total_tiles, (name, total_tiles, parts)
        return idx // per, idx % per

    if mode == "nn":
        a_spec = pl.BlockSpec((tm, tk), lambda i, j, kk: (i, kk))
        if b_parts:
            b_spec = pl.BlockSpec((None, tk, tn), lambda i, j, kk: (slab(j, nn, b_parts)[0], kk, slab(j, nn, b_parts)[1]))
        else:
            b_spec = pl.BlockSpec((tk, tn), lambda i, j, kk: (kk, j))
        dims = (((1,), (0,)), ((), ()))
    elif mode == "nt":
        if a_parts:
            a_spec = pl.BlockSpec((None, tm, tk), lambda i, j, kk: (slab(kk, nk, a_parts)[0], i, slab(kk, nk, a_parts)[1]))
        else:
            a_spec = pl.BlockSpec((tm, tk), lambda i, j, kk: (i, kk))
        if b_parts:
            b_spec = pl.BlockSpec((None, tn, tk), lambda i, j, kk: (slab(kk, nk, b_parts)[0], j, slab(kk, nk, b_parts)[1]))
        else:
            b_spec = pl.BlockSpec((tn, tk), lambda i, j, kk: (j, kk))
        dims = (((1,), (1,)), ((), ()))
    else:
        a_spec = pl.BlockSpec((tk, tm), lambda i, j, kk: (kk, i))
        if b_parts:
            b_spec = pl.BlockSpec((None, tk, tn), lambda i, j, kk: (slab(j, nn, b_parts)[0], kk, slab(j, nn, b_parts)[1]))
        else:
            b_spec = pl.BlockSpec((tk, tn), lambda i, j, kk: (kk, j))
        dims = (((0,), (0,)), ((), ()))
    if out_parts:
        out_shape = jax.ShapeDtypeStruct((out_parts, m, n // out_parts), out_dtype)
        out_spec = pl.BlockSpec((None, tm, tn), lambda i, j, kk: (slab(j, nn, out_parts)[0], i, slab(j, nn, out_parts)[1]))
    else:
        out_shape = jax.ShapeDtypeStruct((m, n), out_dtype)
        out_spec = pl.BlockSpec((tm, tn), lambda i, j, kk: (i, j))
    if b_lead:
        inner_map = b_spec.index_map
        b_spec = pl.BlockSpec((None,) * len(b_lead) + tuple(b_spec.block_shape),
                              lambda i, j, kk: tuple(b_lead) + tuple(inner_map(i, j, kk)))
    in_specs = [a_spec, b_spec]
    operands = [a, b_full]
    if res is not None:
        in_specs.append(pl.BlockSpec((tm, tn), lambda i, j, kk: (i, j)))
        operands.append(res)

    def body(*refs):
        a_ref, b_ref = refs[0], refs[1]
        res_ref = refs[2] if res is not None else None
        o_ref = refs[3] if res is not None else refs[2]
        p = lax.dot_general(a_ref[...], b_ref[...], dims, preferred_element_type=F32)

        def finish(total):
            if res_ref is not None:
                total = res_ref[...] + total
            o_ref[...] = total.astype(out_dtype)

        if nk == 1:
            finish(p)
        else:
            acc_ref = refs[-1]
            kk = pl.program_id(2)

            @pl.when(kk == 0)
            def _():
                acc_ref[...] = p

            @pl.when(jnp.logical_and(kk > 0, kk < nk - 1))
            def _():
                acc_ref[...] += p

            @pl.when(kk == nk - 1)
            def _():
                finish(acc_ref[...] + p)

    scratch = [pltpu.VMEM((tm, tn), F32)] if nk > 1 else []
    return _ordered_call(
        body, name=name, out_shape=out_shape, grid=(nm, nn, nk), in_specs=in_specs, out_specs=out_spec,
        scratch_shapes=scratch, compiler_params=_params(("parallel", "parallel", "arbitrary")),
    )(*operands)


def _matmul_nt_slabs(a, b, *, tm, tn, name, a_parts=0):
    nslab, n, ks = b.shape
    m = a.shape[1] if a_parts else a.shape[0]
    tm, tn = min(tm, m), min(tn, n)
    assert m % tm == 0 and n % tn == 0, (name, m, n, tm, tn)
    if a_parts:
        per = nslab // a_parts
        assert per * a_parts == nslab and a.shape[2] == per * ks, (name, a.shape, b.shape)
        a_spec = pl.BlockSpec((a_parts, tm, per * ks), lambda i, j: (0, i, 0))
    else:
        assert a.shape[1] == nslab * ks, (name, a.shape, b.shape)
        a_spec = pl.BlockSpec((tm, nslab * ks), lambda i, j: (i, 0))

    def body(a_ref, b_ref, o_ref):
        total = None
        for sl in range(nslab):
            if a_parts:
                a_sl = a_ref[sl // per, :, (sl % per) * ks:(sl % per + 1) * ks]
            else:
                a_sl = a_ref[:, sl * ks:(sl + 1) * ks]
            p = lax.dot_general(a_sl, b_ref[sl], (((1,), (1,)), ((), ())), preferred_element_type=F32)
            total = p if total is None else total + p
        o_ref[...] = total

    return _ordered_call(
        body, name=name, out_shape=jax.ShapeDtypeStruct((m, n), F32), grid=(m // tm, n // tn),
        in_specs=[a_spec, pl.BlockSpec((nslab, tn, ks), lambda i, j: (0, j, 0))],
        out_specs=pl.BlockSpec((tm, tn), lambda i, j: (i, j)), compiler_params=_params(("parallel", "parallel")),
    )(a, b)


def _row_tile(s):
    return min(256, s)


def _rows(width, tr):
    return pl.BlockSpec((tr, width), lambda i: (i, 0))


def _const2(shape):
    return pl.BlockSpec(shape, lambda i: (0, 0))


def _rms_fwd(x, g, name):
    s, d = x.shape
    tr = _row_tile(s)

    def body(x_ref, g_ref, o_ref):
        xv = x_ref[...]
        r = lax.rsqrt(_mean_last(xv * xv) + EPS)
        o_ref[...] = (xv * r * g_ref[...]).astype(BF16)

    return _ordered_call(
        body, name=name, out_shape=jax.ShapeDtypeStruct((s, d), BF16), grid=(s // tr,),
        in_specs=[_rows(d, tr), _const2((1, d))], out_specs=_rows(d, tr), compiler_params=_params(("parallel",)),
    )(x, g)


def _rms_bwd(x, g, dh, dres, name):
    s, d = x.shape
    tr = _row_tile(s)

    def body(x_ref, g_ref, dh_ref, dres_ref, dx_ref, dxb_ref, dg_ref):
        xv, dy = x_ref[...], dh_ref[...]
        r = lax.rsqrt(_mean_last(xv * xv) + EPS)
        gdy = dy * g_ref[...]
        dx = dres_ref[...] + r * gdy - xv * ((r * r * r) * _mean_last(xv * gdy))
        dx_ref[...] = dx
        dxb_ref[...] = dx.astype(BF16)

        @pl.when(pl.program_id(0) == 0)
        def _():
            dg_ref[...] = jnp.zeros_like(dg_ref)

        dg_ref[...] += _sum_rows(xv * r * dy)

    return _ordered_call(
        body, name=name,
        out_shape=(jax.ShapeDtypeStruct((s, d), F32), jax.ShapeDtypeStruct((s, d), BF16), jax.ShapeDtypeStruct((1, d), F32)),
        grid=(s // tr,), in_specs=[_rows(d, tr), _const2((1, d)), _rows(d, tr), _rows(d, tr)],
        out_specs=(_rows(d, tr), _rows(d, tr), _const2((1, d))), compiler_params=_params(("arbitrary",)),
    )(x, g, dh, dres)


Q0, K0, V0, GU0, GV0 = 0, ATTN_WIDTH, ATTN_WIDTH + KV_WIDTH, ATTN_WIDTH + 2 * KV_WIDTH, ATTN_WIDTH + 2 * KV_WIDTH + GMLP_WIDTH


def _head(h, base=0):
    return slice(base + h * HEAD_DIM, base + (h + 1) * HEAD_DIM)


def _proj_post(z, qg, kg, lg, lb, cosf, sinf, name):
    s = z.shape[0]
    tr = _row_tile(s)

    def body(z_ref, qg_ref, kg_ref, lg_ref, lb_ref, cos_ref, sin_ref, qn_ref, kn_ref, vb_ref, ug_ref, vn_ref):
        cos, sin = cos_ref[...], sin_ref[...]

        def norm_rope(xh, g):
            y = xh * lax.rsqrt(_mean_last(xh * xh) + EPS) * g
            return y * cos + pltpu.roll(y, HEAD_DIM // 2, 1) * sin

        for h in range(N_Q_HEADS):
            qn_ref[:, _head(h)] = norm_rope(z_ref[:, _head(h, Q0)].astype(F32), qg_ref[...]).astype(BF16)
        for h in range(N_KV_HEADS):
            kn_ref[:, _head(h)] = norm_rope(z_ref[:, _head(h, K0)].astype(F32), kg_ref[...]).astype(BF16)
        vb_ref[...] = z_ref[:, V0:GU0]
        ug_ref[...] = _gelu(z_ref[:, GU0:GV0].astype(F32))
        vg = _gelu(z_ref[:, GV0:IN_WIDTH].astype(F32))
        xc = vg - _mean_last(vg)
        y = xc * lax.rsqrt(_mean_last(xc * xc) + EPS)
        vn_ref[...] = (y * lg_ref[...] + lb_ref[...]).astype(BF16)

    return _ordered_call(
        body, name=name,
        out_shape=(jax.ShapeDtypeStruct((s, ATTN_WIDTH), BF16), jax.ShapeDtypeStruct((s, KV_WIDTH), BF16),
                   jax.ShapeDtypeStruct((s, KV_WIDTH), BF16), jax.ShapeDtypeStruct((s, GMLP_WIDTH), F32),
                   jax.ShapeDtypeStruct((s, GMLP_WIDTH), BF16)),
        grid=(s // tr,),
        in_specs=[_rows(IN_WIDTH, tr), _const2((1, HEAD_DIM)), _const2((1, HEAD_DIM)), _const2((1, GMLP_WIDTH)),
                  _const2((1, GMLP_WIDTH)), _rows(HEAD_DIM, tr), _rows(HEAD_DIM, tr)],
        out_specs=(_rows(ATTN_WIDTH, tr), _rows(KV_WIDTH, tr), _rows(KV_WIDTH, tr), _rows(GMLP_WIDTH, tr), _rows(GMLP_WIDTH, tr)),
        compiler_params=_params(("parallel",)),
    )(z, qg, kg, lg, lb, cosf, sinf)


def _proj_post_bwd(z, dqn, dkn, dvb, dug, dvn, qg, kg, lg, cosf, sinf, name):
    s = z.shape[0]
    tr = _row_tile(s)

    def body(z_ref, dqn_ref, dkn_ref, dvb_ref, dug_ref, dvn_ref, qg_ref, kg_ref, lg_ref, cos_ref, sin_ref,
             dz_ref, dqg_ref, dkg_ref, dlg_ref, dlb_ref):
        cos, sin = cos_ref[...], sin_ref[...]

        @pl.when(pl.program_id(0) == 0)
        def _():
            dqg_ref[...] = jnp.zeros_like(dqg_ref)
            dkg_ref[...] = jnp.zeros_like(dkg_ref)
            dlg_ref[...] = jnp.zeros_like(dlg_ref)
            dlb_ref[...] = jnp.zeros_like(dlb_ref)

        def norm_rope_bwd(xh, g, dout):
            dy = dout * cos - pltpu.roll(dout, HEAD_DIM // 2, 1) * sin
            r = lax.rsqrt(_mean_last(xh * xh) + EPS)
            xhat = xh * r
            gdy = dy * g
            return r * (gdy - xhat * _mean_last(xhat * gdy)), _sum_rows(xhat * dy)

        dqg = jnp.zeros((1, HEAD_DIM), F32)
        for h in range(N_Q_HEADS):
            dx, dg = norm_rope_bwd(z_ref[:, _head(h, Q0)].astype(F32), qg_ref[...], dqn_ref[:, _head(h)])
            dz_ref[:, _head(h, Q0)] = dx.astype(BF16)
            dqg = dqg + dg
        dqg_ref[...] += dqg
        dkg = jnp.zeros((1, HEAD_DIM), F32)
        for h in range(N_KV_HEADS):
            dx, dg = norm_rope_bwd(z_ref[:, _head(h, K0)].astype(F32), kg_ref[...], dkn_ref[:, _head(h)])
            dz_ref[:, _head(h, K0)] = dx.astype(BF16)
            dkg = dkg + dg
        dkg_ref[...] += dkg
        dz_ref[:, V0:GU0] = dvb_ref[...].astype(BF16)
        dz_ref[:, GU0:GV0] = (dug_ref[...] * _gelu_grad(z_ref[:, GU0:GV0].astype(F32))).astype(BF16)
        gv = z_ref[:, GV0:IN_WIDTH].astype(F32)
        vg = _gelu(gv)
        xc = vg - _mean_last(vg)
        r = lax.rsqrt(_mean_last(xc * xc) + EPS)
        xhat = xc * r
        dvn_v = dvn_ref[...]
        dlg_ref[...] += _sum_rows(xhat * dvn_v)
        dlb_ref[...] += _sum_rows(dvn_v)
        dxh = dvn_v * lg_ref[...]
        dvg = r * (dxh - _mean_last(dxh) - xhat * _mean_last(dxh * xhat))
        dz_ref[:, GV0:IN_WIDTH] = (dvg * _gelu_grad(gv)).astype(BF16)

    return _ordered_call(
        body, name=name,
        out_shape=(jax.ShapeDtypeStruct((s, IN_WIDTH), BF16), jax.ShapeDtypeStruct((1, HEAD_DIM), F32),
                   jax.ShapeDtypeStruct((1, HEAD_DIM), F32), jax.ShapeDtypeStruct((1, GMLP_WIDTH), F32),
                   jax.ShapeDtypeStruct((1, GMLP_WIDTH), F32)),
        grid=(s // tr,),
        in_specs=[_rows(IN_WIDTH, tr), _rows(ATTN_WIDTH, tr), _rows(KV_WIDTH, tr), _rows(KV_WIDTH, tr), _rows(GMLP_WIDTH, tr),
                  _rows(GMLP_WIDTH, tr), _const2((1, HEAD_DIM)), _const2((1, HEAD_DIM)), _const2((1, GMLP_WIDTH)),
                  _rows(HEAD_DIM, tr), _rows(HEAD_DIM, tr)],
        out_specs=(_rows(IN_WIDTH, tr), _const2((1, HEAD_DIM)), _const2((1, HEAD_DIM)), _const2((1, GMLP_WIDTH)),
                   _const2((1, GMLP_WIDTH))),
        compiler_params=_params(("arbitrary",)),
    )(z, dqn, dkn, dvb, dug, dvn, qg, kg, lg, cosf, sinf)


def _band_valid(n, s):
    shape = (GQA_GROUP * BLOCK, 3 * BLOCK)
    i = lax.broadcasted_iota(jnp.int32, shape, 0) & (BLOCK - 1)
    j = lax.broadcasted_iota(jnp.int32, shape, 1)
    k_pos = n * BLOCK - BLOCK + j
    return (jnp.abs(j - BLOCK - i) <= BLOCK) & (k_pos >= 0) & (k_pos < s)


def _group_rows(x, kh):
    return jnp.concatenate([x[:, _head(kh * GQA_GROUP + g)] for g in range(GQA_GROUP)], axis=0)


def _group_sinks(sink_ref, kh):
    return jnp.concatenate([jnp.full((BLOCK, 1), sink_ref[kh * GQA_GROUP + g], F32) for g in range(GQA_GROUP)], axis=0)


def _rows_of(x, g):
    return x[g * BLOCK:(g + 1) * BLOCK]


def _probs(q, kb, sink_h, valid):
    sc = lax.dot_general(q, kb, (((1,), (1,)), ((), ())), preferred_element_type=F32) * (HEAD_DIM ** -0.5)
    sc = jnp.where(valid, sc, MASK_VALUE)
    m = jnp.maximum(jnp.max(sc, axis=-1, keepdims=True), sink_h)
    p = jnp.exp(sc - m)
    es = jnp.exp(sink_h - m)
    den = jnp.sum(p, axis=-1, keepdims=True) + es
    inv = 1.0 / den
    return p * inv, es * inv


def _band_specs(width, nb):
    return [pl.BlockSpec((BLOCK, width), lambda n: (jnp.maximum(n - 1, 0), 0)),
            pl.BlockSpec((BLOCK, width), lambda n: (n, 0)),
            pl.BlockSpec((BLOCK, width), lambda n: (jnp.minimum(n + 1, nb - 1), 0))]


def _blk(width):
    return pl.BlockSpec((BLOCK, width), lambda n: (n, 0))


def _whole3(shape):
    return pl.BlockSpec(shape, lambda n: (0, 0, 0))


def _smem():
    return pl.BlockSpec(memory_space=pltpu.SMEM)


def _mixer_fwd(qn, kn, vb, ug, vn, wsb, bsb, sink, ga, gs, name):
    s = qn.shape[0]
    nb = s // BLOCK

    def body(sink_ref, q_ref, kp_ref, kc_ref, kx_ref, vp_ref, vc_ref, vx_ref, ug_ref, vn_ref, ws_ref, bs_ref, ga_ref, gs_ref,
             attn_ref, sgu_ref, mix_ref):
        n = pl.program_id(0)
        valid = _band_valid(n, s)
        ssq = jnp.zeros((BLOCK, 1), F32)
        for kh in range(N_KV_HEADS):
            kb = jnp.concatenate([kp_ref[:, _head(kh)], kc_ref[:, _head(kh)], kx_ref[:, _head(kh)]], axis=0)
            vbd = jnp.concatenate([vp_ref[:, _head(kh)], vc_ref[:, _head(kh)], vx_ref[:, _head(kh)]], axis=0)
            p, _ = _probs(_group_rows(q_ref, kh), kb, _group_sinks(sink_ref, kh), valid)
            o4 = jnp.dot(p.astype(BF16), vbd, preferred_element_type=F32)
            for g in range(GQA_GROUP):
                o = _rows_of(o4, g)
                attn_ref[:, _head(kh * GQA_GROUP + g)] = o
                ssq = ssq + jnp.sum(o * o, axis=-1, keepdims=True)
        r = lax.rsqrt(ssq * (1.0 / ATTN_WIDTH) + EPS)
        mix_ref[:, 0:ATTN_WIDTH] = (attn_ref[...] * r * ga_ref[...]).astype(BF16)
        ssq = jnp.zeros((BLOCK, 1), F32)
        for h in range(N_GMLP_HEADS):
            f = jnp.dot(ws_ref[h], vn_ref[:, _head(h)], preferred_element_type=F32) + bs_ref[h]
            o = ug_ref[:, _head(h)] * f
            sgu_ref[:, _head(h)] = o
            ssq = ssq + jnp.sum(o * o, axis=-1, keepdims=True)
        r = lax.rsqrt(ssq * (1.0 / GMLP_WIDTH) + EPS)
        mix_ref[:, ATTN_WIDTH:D_MODEL] = (sgu_ref[...] * r * gs_ref[...]).astype(BF16)

    hh = (N_GMLP_HEADS, BLOCK, BLOCK)
    return _ordered_call(
        body, name=name,
        out_shape=(jax.ShapeDtypeStruct((s, ATTN_WIDTH), F32), jax.ShapeDtypeStruct((s, GMLP_WIDTH), F32),
                   jax.ShapeDtypeStruct((s, D_MODEL), BF16)),
        grid=(nb,),
        in_specs=[_smem(), _blk(ATTN_WIDTH)] + _band_specs(KV_WIDTH, nb) + _band_specs(KV_WIDTH, nb)
        + [_blk(GMLP_WIDTH), _blk(GMLP_WIDTH), _whole3(hh), _whole3(hh),
           pl.BlockSpec((1, ATTN_WIDTH), lambda n: (0, 0)), pl.BlockSpec((1, GMLP_WIDTH), lambda n: (0, 0))],
        out_specs=(_blk(ATTN_WIDTH), _blk(GMLP_WIDTH), _blk(D_MODEL)),
        compiler_params=_params(("parallel",)),
    )(sink, qn, kn, kn, kn, vb, vb, vb, ug, vn, wsb, bsb, ga, gs)


def _mixer_bwd(qn, kn, vb, ug, vn, attn, sgu, dmixed, wsb, bsb, sink, ga, gs, name):
    s = qn.shape[0]
    nb = s // BLOCK
    tn_dims = (((0,), (0,)), ((), ()))
    nt_dims = (((1,), (1,)), ((), ()))

    def body(sink_ref, q_ref, kp_ref, kc_ref, kx_ref, vp_ref, vc_ref, vx_ref, ug_ref, vn_ref, attn_ref, sgu_ref, dm_ref,
             ws_ref, bs_ref, ga_ref, gs_ref,
             dq_ref, dk_ref, dv_ref, dug_ref, dvn_ref, dws_ref, dbs_ref, dsk_ref, dga_ref, dgs_ref, dk_acc, dv_acc):
        n = pl.program_id(0)

        @pl.when(n == 0)
        def _():
            for ref in (dk_acc, dv_acc, dws_ref, dbs_ref, dsk_ref, dga_ref, dgs_ref):
                ref[...] = jnp.zeros_like(ref)

        def out_norm_bwd(o, g, dy):
            r = lax.rsqrt(_mean_last(o * o) + EPS)
            gdy = dy * g
            return r * gdy - o * ((r * r * r) * _mean_last(o * gdy)), _sum_rows(o * r * dy)

        d_attn, dga = out_norm_bwd(attn_ref[...], ga_ref[...], dm_ref[:, 0:ATTN_WIDTH])
        dga_ref[...] += dga
        d_sgu, dgs = out_norm_bwd(sgu_ref[...], gs_ref[...], dm_ref[:, ATTN_WIDTH:D_MODEL])
        dgs_ref[...] += dgs

        for h in range(N_GMLP_HEADS):
            vn_h = vn_ref[:, _head(h)]
            f = jnp.dot(ws_ref[h], vn_h, preferred_element_type=F32) + bs_ref[h]
            ds_h = d_sgu[:, _head(h)]
            dug_ref[:, _head(h)] = ds_h * f
            df = ds_h * ug_ref[:, _head(h)]
            dfb = df.astype(BF16)
            dvn_ref[:, _head(h)] = lax.dot_general(ws_ref[h], dfb, tn_dims, preferred_element_type=F32)
            dws_ref[h] += lax.dot_general(dfb, vn_h, nt_dims, preferred_element_type=F32)
            dbs_ref[h] += jnp.broadcast_to(jnp.sum(df, axis=-1, keepdims=True), (BLOCK, BLOCK))

        valid = _band_valid(n, s)
        row0 = pl.multiple_of(n * BLOCK, BLOCK)
        for kh in range(N_KV_HEADS):
            kb = jnp.concatenate([kp_ref[:, _head(kh)], kc_ref[:, _head(kh)], kx_ref[:, _head(kh)]], axis=0)
            vbd = jnp.concatenate([vp_ref[:, _head(kh)], vc_ref[:, _head(kh)], vx_ref[:, _head(kh)]], axis=0)
            q4 = _group_rows(q_ref, kh)
            p, p_sink = _probs(q4, kb, _group_sinks(sink_ref, kh), valid)
            do4 = _group_rows(d_attn, kh).astype(BF16)
            dp = lax.dot_general(do4, vbd, nt_dims, preferred_element_type=F32)
            delta = jnp.sum(p * dp, axis=-1, keepdims=True)
            dsc = (p * (dp - delta) * (HEAD_DIM ** -0.5)).astype(BF16)
            d_sink = -(p_sink * delta)
            dq4 = jnp.dot(dsc, kb, preferred_element_type=F32)
            for g in range(GQA_GROUP):
                h = kh * GQA_GROUP + g
                dsk_ref[h:h + 1, :] += jnp.broadcast_to(_sum_all(_rows_of(d_sink, g)), (1, BLOCK))
                dq_ref[:, _head(h)] = _rows_of(dq4, g)
            dk_acc[pl.ds(row0, 3 * BLOCK), _head(kh)] += lax.dot_general(dsc, q4, tn_dims, preferred_element_type=F32)
            dv_acc[pl.ds(row0, 3 * BLOCK), _head(kh)] += lax.dot_general(p.astype(BF16), do4, tn_dims,
                                                                         preferred_element_type=F32)

        @pl.when(n == nb - 1)
        def _():
            dk_ref[...] = dk_acc[BLOCK:BLOCK + s, :]
            dv_ref[...] = dv_acc[BLOCK:BLOCK + s, :]

    hh = (N_GMLP_HEADS, BLOCK, BLOCK)
    full_kv = pl.BlockSpec((s, KV_WIDTH), lambda n: (0, 0))
    return _ordered_call(
        body, name=name,
        out_shape=(jax.ShapeDtypeStruct((s, ATTN_WIDTH), F32), jax.ShapeDtypeStruct((s, KV_WIDTH), F32),
                   jax.ShapeDtypeStruct((s, KV_WIDTH), F32), jax.ShapeDtypeStruct((s, GMLP_WIDTH), F32),
                   jax.ShapeDtypeStruct((s, GMLP_WIDTH), F32), jax.ShapeDtypeStruct(hh, F32), jax.ShapeDtypeStruct(hh, F32),
                   jax.ShapeDtypeStruct((N_Q_HEADS, BLOCK), F32), jax.ShapeDtypeStruct((1, ATTN_WIDTH), F32),
                   jax.ShapeDtypeStruct((1, GMLP_WIDTH), F32)),
        grid=(nb,),
        in_specs=[_smem(), _blk(ATTN_WIDTH)] + _band_specs(KV_WIDTH, nb) + _band_specs(KV_WIDTH, nb)
        + [_blk(GMLP_WIDTH), _blk(GMLP_WIDTH), _blk(ATTN_WIDTH), _blk(GMLP_WIDTH), _blk(D_MODEL), _whole3(hh), _whole3(hh),
           pl.BlockSpec((1, ATTN_WIDTH), lambda n: (0, 0)), pl.BlockSpec((1, GMLP_WIDTH), lambda n: (0, 0))],
        out_specs=(_blk(ATTN_WIDTH), full_kv, full_kv, _blk(GMLP_WIDTH), _blk(GMLP_WIDTH), _whole3(hh), _whole3(hh),
                   pl.BlockSpec((N_Q_HEADS, BLOCK), lambda n: (0, 0)), pl.BlockSpec((1, ATTN_WIDTH), lambda n: (0, 0)),
                   pl.BlockSpec((1, GMLP_WIDTH), lambda n: (0, 0))),
        scratch_shapes=[pltpu.VMEM((s + 2 * BLOCK, KV_WIDTH), F32), pltpu.VMEM((s + 2 * BLOCK, KV_WIDTH), F32)],
        compiler_params=_params(("arbitrary",)),
    )(sink, qn, kn, kn, kn, vb, vb, vb, ug, vn, attn, sgu, dmixed, wsb, bsb, ga, gs)


CONV_TILE = 128


PAD_ROWS = 8


def _zero_pad_rows(pad_ref):
    s = pad_ref.shape[0] - 2 * PAD_ROWS
    zeros = jnp.zeros((PAD_ROWS, pad_ref.shape[1]), F32)
    pad_ref[0:PAD_ROWS, :] = zeros
    pad_ref[PAD_ROWS + s:2 * PAD_ROWS + s, :] = zeros


def _shift_rows(a, pad_ref):
    s = a.shape[0]
    pad_ref[PAD_ROWS:PAD_ROWS + s, :] = a
    padded = pad_ref[...]
    prev = pltpu.roll(padded, 1, 0)[PAD_ROWS:PAD_ROWS + s]
    nxt = pltpu.roll(padded, s + 2 * PAD_ROWS - 1, 0)[PAD_ROWS:PAD_ROWS + s]
    return prev, nxt


def _conv_specs(s):
    tc = CONV_TILE
    nj = D_FF // tc
    return (tc, nj, pl.BlockSpec((2, s, tc), lambda j: (0, 0, j)),
            [pl.BlockSpec((3, tc), lambda j: (0, j)), pl.BlockSpec((3, tc), lambda j: (0, j + nj))],
            [pl.BlockSpec((1, tc), lambda j: (0, j)), pl.BlockSpec((1, tc), lambda j: (0, j + nj))])


def _conv_gate_fwd(a_pre, cw, cb, name):
    s = a_pre.shape[1]
    tc, nj, a_spec, w_specs, b_specs = _conv_specs(s)

    def body(a_ref, wg_ref, wu_ref, bg_ref, bu_ref, act_ref, dgu_ref, pad_ref):
        _zero_pad_rows(pad_ref)

        def conv(a, w_ref, b_ref):
            prev, nxt = _shift_rows(a, pad_ref)
            return b_ref[...] + prev * w_ref[0:1, :] + a * w_ref[1:2, :] + nxt * w_ref[2:3, :]

        g = conv(a_ref[0].astype(F32), wg_ref, bg_ref)
        u = conv(a_ref[1].astype(F32), wu_ref, bu_ref)
        sg = 1.0 / (1.0 + jnp.exp(-g))
        silu = g * sg
        act_ref[...] = (silu * u).astype(BF16)
        dgu_ref[0] = (u * (sg * (1.0 + g * (1.0 - sg)))).astype(BF16)
        dgu_ref[1] = silu.astype(BF16)

    return _ordered_call(
        body, name=name, out_shape=(jax.ShapeDtypeStruct((s, D_FF), BF16), jax.ShapeDtypeStruct((2, s, D_FF), BF16)),
        grid=(nj,), in_specs=[a_spec] + w_specs + b_specs,
        out_specs=(pl.BlockSpec((s, tc), lambda j: (0, j)), pl.BlockSpec((2, s, tc), lambda j: (0, 0, j))),
        scratch_shapes=[pltpu.VMEM((s + 2 * PAD_ROWS, tc), F32)], compiler_params=_params(("parallel",)),
    )(a_pre, cw, cw, cb, cb)


def _conv_gate_bwd(a_pre, dgu, cw, dact, name):
    s = a_pre.shape[1]
    tc, nj, a_spec, w_specs, _ = _conv_specs(s)

    def body(a_ref, dgu_ref, wg_ref, wu_ref, dact_ref, dap_ref, dcw_ref, dcb_ref, pad_ref):
        _zero_pad_rows(pad_ref)
        dact_v = dact_ref[...].astype(F32)
        for part, w_ref in enumerate((wg_ref, wu_ref)):
            da = dact_v * dgu_ref[part].astype(F32)
            a = a_ref[part].astype(F32)
            prev, nxt = _shift_rows(a, pad_ref)
            dcw_ref[part, 0:1, :] = _sum_rows(prev * da)
            dcw_ref[part, 1:2, :] = _sum_rows(a * da)
            dcw_ref[part, 2:3, :] = _sum_rows(nxt * da)
            dcb_ref[part] = _sum_rows(da)
            da_prev, da_next = _shift_rows(da, pad_ref)
            dap_ref[part] = (da_next * w_ref[0:1, :] + da * w_ref[1:2, :] + da_prev * w_ref[2:3, :]).astype(BF16)

    return _ordered_call(
        body, name=name,
        out_shape=(jax.ShapeDtypeStruct((2, s, D_FF), BF16), jax.ShapeDtypeStruct((2, 3, D_FF), F32),
                   jax.ShapeDtypeStruct((2, 1, D_FF), F32)),
        grid=(nj,),
        in_specs=[a_spec, pl.BlockSpec((2, s, tc), lambda j: (0, 0, j))] + w_specs + [pl.BlockSpec((s, tc), lambda j: (0, j))],
        out_specs=(pl.BlockSpec((2, s, tc), lambda j: (0, 0, j)), pl.BlockSpec((2, 3, tc), lambda j: (0, 0, j)),
                   pl.BlockSpec((2, 1, tc), lambda j: (0, 0, j))),
        scratch_shapes=[pltpu.VMEM((s + 2 * PAD_ROWS, tc), F32)], compiler_params=_params(("parallel",)),
    )(a_pre, dgu, cw, cw, dact)


def _loss_head(y, target, name):
    s, d = y.shape
    tr = _row_tile(s)

    def body(y_ref, t_ref, loss_ref, dy_ref, dyb_ref):
        err = y_ref[...] - t_ref[...]

        @pl.when(pl.program_id(0) == 0)
        def _():
            loss_ref[...] = jnp.zeros_like(loss_ref)

        loss_ref[...] += jnp.broadcast_to(0.5 * _sum_all(_mean_last(err * err)), (8, 128))
        dy = err * (1.0 / d)
        dy_ref[...] = dy
        dyb_ref[...] = dy.astype(BF16)

    return _ordered_call(
        body, name=name,
        out_shape=(jax.ShapeDtypeStruct((8, 128), F32), jax.ShapeDtypeStruct((s, d), F32), jax.ShapeDtypeStruct((s, d), BF16)),
        grid=(s // tr,), in_specs=[_rows(d, tr), _rows(d, tr)],
        out_specs=(_const2((8, 128)), _rows(d, tr), _rows(d, tr)), compiler_params=_params(("arbitrary",)),
    )(y, target)


def _row_block(rows, cols, budget=1 << 20):
    if rows * cols <= budget:
        return rows
    best = None
    for tr in range(16, rows, 16):
        if rows % tr == 0 and tr * cols <= budget:
            best = tr
    assert best is not None, (rows, cols)
    return best


def _place_shard(x4, layer, j_arr, out_dtype, name):
    _, nh, r, cols = x4.shape
    tr = _row_block(r, cols)

    def body(j_ref, x_ref, o_ref):
        o_ref[...] = x_ref[...].astype(out_dtype)

    grid_spec = pltpu.PrefetchScalarGridSpec(
        num_scalar_prefetch=1, grid=(nh, r // tr),
        in_specs=[pl.BlockSpec((None, None, tr, cols), lambda h, i, j_ref: (layer, h, i, 0))],
        out_specs=pl.BlockSpec((None, None, tr, cols), lambda h, i, j_ref: (j_ref[0], h, i, 0)))
    return _ordered_call(
        body, name=name, out_shape=jax.ShapeDtypeStruct((N_CHIPS, nh, r, cols), out_dtype), grid_spec=grid_spec,
        compiler_params=_params(("parallel", "parallel")),
    )(j_arr, x4)


def _adamw(w, g, m, v, name, budget=1 << 18):
    rows, cols = w.shape
    tr = _row_block(rows, cols, budget)

    def body(w_ref, g_ref, m_ref, v_ref, go_ref, d_ref, nm_ref, nv_ref):
        gv = g_ref[...]
        go_ref[...] = gv
        mn = ADAM_B1 * m_ref[...] + (1.0 - ADAM_B1) * gv
        vn = ADAM_B2 * v_ref[...] + (1.0 - ADAM_B2) * (gv * gv)
        m_hat = mn / (1.0 - ADAM_B1 ** ADAM_STEP)
        v_hat = vn / (1.0 - ADAM_B2 ** ADAM_STEP)
        d_ref[...] = -ADAM_LR * (m_hat / (jnp.sqrt(v_hat) + ADAM_EPS) + ADAM_WD * w_ref[...])
        nm_ref[...] = mn
        nv_ref[...] = vn

    sds = jax.ShapeDtypeStruct((rows, cols), F32)
    return _ordered_call(
        body, name=name, out_shape=(sds, sds, sds, sds), grid=(rows // tr,),
        in_specs=[_rows(cols, tr)] * 4, out_specs=(_rows(cols, tr),) * 4, compiler_params=_params(("parallel",)),
    )(w, g, m, v)


def _pair_sum(g5, recv, c_arr, name):
    _, _, rh, cols = g5.shape
    tr = _row_block(rh, cols)

    def body(c_ref, g_ref, r_ref, o_ref):
        o_ref[...] = (g_ref[...].astype(F32) + r_ref[...].astype(F32)).astype(BF16)

    grid_spec = pltpu.PrefetchScalarGridSpec(
        num_scalar_prefetch=1, grid=(N_CHIPS, rh // tr),
        in_specs=[pl.BlockSpec((None, None, tr, cols), lambda j, i, c_ref: (j, c_ref[0], i, 0)),
                  pl.BlockSpec((None, tr, cols), lambda j, i, c_ref: (j, i, 0))],
        out_specs=pl.BlockSpec((None, tr, cols), lambda j, i, c_ref: (j, i, 0)))
    return _ordered_call(
        body, name=name, out_shape=jax.ShapeDtypeStruct((N_CHIPS, rh, cols), BF16), grid_spec=grid_spec,
        compiler_params=_params(("parallel", "parallel")),
    )(c_arr, g5, recv)


def _chip_sum(p4, recv3, j_arr, c_arr, name):
    _, rh, cols = p4.shape
    tr = _row_block(rh, cols, 1 << 19)

    def body(j_ref, c_ref, p_ref, r_ref, o_ref):
        total = p_ref[...].astype(F32)
        for peer in range(3):
            total = total + r_ref[peer].astype(F32)
        o_ref[...] = total

    grid_spec = pltpu.PrefetchScalarGridSpec(
        num_scalar_prefetch=2, grid=(rh // tr,),
        in_specs=[pl.BlockSpec((None, tr, cols), lambda i, j_ref, c_ref: (j_ref[0], i, 0)),
                  pl.BlockSpec((3, tr, cols), lambda i, j_ref, c_ref: (0, i, 0))],
        out_specs=pl.BlockSpec((None, tr, cols), lambda i, j_ref, c_ref: (c_ref[0], i, 0)))
    return _ordered_call(
        body, name=name, out_shape=jax.ShapeDtypeStruct((2, rh, cols), F32), grid_spec=grid_spec,
        compiler_params=_params(("parallel",)),
    )(j_arr, c_arr, p4, recv3)


def _adamw_layer(w, g, m, v, layer, into, name):
    nl, rows, cols = w.shape
    tr = _row_block(rows, cols, 1 << 18)
    at_layer = pl.BlockSpec((None, tr, cols), lambda i: (layer, i, 0))

    def body(w_ref, g_ref, m_ref, v_ref, *rest):
        go_ref, d_ref, nm_ref, nv_ref = rest[-4:]
        gv = g_ref[...]
        go_ref[...] = gv
        mn = ADAM_B1 * m_ref[...] + (1.0 - ADAM_B1) * gv
        vn = ADAM_B2 * v_ref[...] + (1.0 - ADAM_B2) * (gv * gv)
        m_hat = mn / (1.0 - ADAM_B1 ** ADAM_STEP)
        v_hat = vn / (1.0 - ADAM_B2 ** ADAM_STEP)
        d_ref[...] = -ADAM_LR * (m_hat / (jnp.sqrt(v_hat) + ADAM_EPS) + ADAM_WD * w_ref[...])
        nm_ref[...] = mn
        nv_ref[...] = vn

    in_specs = [at_layer, _rows(cols, tr), at_layer, at_layer]
    operands = [w, g, m, v]
    aliases = {}
    if into is not None:
        in_specs += [ANY] * 4
        operands += list(into)
        aliases = {4 + i: i for i in range(4)}
    sds = jax.ShapeDtypeStruct((nl, rows, cols), F32)
    return _ordered_call(
        body, name=name, out_shape=(sds,) * 4, grid=(rows // tr,), in_specs=in_specs, out_specs=(at_layer,) * 4,
        input_output_aliases=aliases, compiler_params=_params(("parallel",)),
    )(*operands)


def _sum_devices(mine, landed, me_arr, name):
    rows, lanes = mine.shape

    def body(me_ref, mine_ref, landed_ref, o_ref):
        total = None
        for dev in range(8):
            part = jnp.where(me_ref[0] == dev, mine_ref[...], landed_ref[dev])
            total = part if total is None else total + part
        o_ref[...] = total

    grid_spec = pltpu.PrefetchScalarGridSpec(
        num_scalar_prefetch=1, grid=(1,),
        in_specs=[pl.BlockSpec((rows, lanes), lambda i, me_ref: (0, 0)), pl.BlockSpec((8, rows, lanes), lambda i, me_ref: (0, 0, 0))],
        out_specs=pl.BlockSpec((rows, lanes), lambda i, me_ref: (0, 0)))
    return _ordered_call(
        body, name=name, out_shape=jax.ShapeDtypeStruct((rows, lanes), F32), grid_spec=grid_spec,
        compiler_params=_params(("arbitrary",)),
    )(me_arr, mine, landed)


def _place():
    x, y, c = lax.axis_index("x"), lax.axis_index("y"), lax.axis_index("c")
    chips = [(1 - x, y), (x, 1 - y), (1 - x, 1 - y)]
    return x, y, c, chips


HBM = pl.BlockSpec(memory_space=pltpu.HBM)
SEM = pl.BlockSpec(memory_space=pltpu.SEMAPHORE)
TOKEN = jax.ShapeDtypeStruct((8, 128), F32)


def _remote(src, dst, send_sem, recv_sem, to):
    return pltpu.make_async_remote_copy(src_ref=src, dst_ref=dst, send_sem=send_sem, recv_sem=recv_sem, device_id=to,
                                        device_id_type=MESH)


def _split_call(body, name, thru, sems_in=(), fresh=(), new_sems=(), after_last=True):
    n_t, n_s, n_f = len(thru), len(sems_in), len(fresh)

    def call_body(*refs):
        outs = refs[n_t + n_s:]
        body(refs[:n_t], refs[n_t:n_t + n_s], outs[1 + n_t:1 + n_t + n_f], outs[1 + n_t + n_f:])
        outs[0][...] = jnp.zeros_like(outs[0])

    out_shape = ([TOKEN] + [pltpu.HBM(t.shape, t.dtype) for t in thru] + [pltpu.HBM(shp, dt) for shp, dt in fresh]
                 + [pltpu.SemaphoreType.DMA(shp) for shp in new_sems])
    out_specs = [pl.BlockSpec(memory_space=pltpu.VMEM)] + [HBM] * (n_t + n_f) + [SEM] * len(new_sems)
    if not after_last:
        _Order.last = None
    out = _ordered_call(
        call_body, name=name, out_shape=tuple(out_shape), in_specs=[HBM] * n_t + [SEM] * n_s, out_specs=tuple(out_specs),
        input_output_aliases={i: 1 + i for i in range(n_t)},
        compiler_params=pltpu.CompilerParams(has_side_effects=pltpu.SideEffectType.DATAFLOW_SIDE_EFFECTING),
    )(*[pltpu.with_memory_space_constraint(t, pltpu.HBM) for t in thru], *sems_in)
    return out[1:1 + n_t], out[1 + n_t:1 + n_t + n_f], out[1 + n_t + n_f:]


class _Exchange:
    def __init__(self, weights, m_in, v_in, j_arr, c_arr, me_arr):
        self.w, self.m, self.v = weights, m_in, v_in
        self.j_arr, self.c_arr, self.me_arr = j_arr, c_arr, me_arr
        self.adam, self.small = {}, {}
        self.groups = [(l, name) for l in range(DEPTH) for name in BIG_NAMES]
        self.shard_shape = {name: weights[name].shape[1:] for name in BIG_NAMES}
        self.conv_state, self.state = [], {}
        self.ready, self.conv_ready = {}, {}
        self.pending, self.tick, self.reduced = [], 0, {}

        def place(grp):
            l, name = grp
            nl, r, cols = weights[name].shape
            return _place_shard(weights[name].reshape(nl, 2, r // 2, cols), l, j_arr, BF16, f"place_{name}_l{l}")

        def start_copies(tag, convs, groups, bufs):
            n_c = len(convs)

            def start(thru, _, __, sems):
                x, y, c, chips = _place()
                j_me = 2 * x + y
                copies = []
                for i in range(len(thru)):
                    mine = thru[i].at[j_me] if i < n_c else thru[i].at[j_me, c]
                    copies += [_remote(mine, mine, sems[2 * i].at[k], sems[2 * i + 1].at[k], (*chip, c))
                               for k, chip in enumerate(chips)]
                for cp in copies:
                    cp.start()

            thru, _, sems = _split_call(start, tag, convs + bufs, new_sems=[(3,)] * (2 * (n_c + len(bufs))))
            self.conv_state += [(thru[i], sems[2 * i], sems[2 * i + 1]) for i in range(n_c)]
            for g, grp in enumerate(groups):
                self.state[grp] = (thru[n_c + g], sems[2 * (n_c + g)], sems[2 * (n_c + g) + 1])

        convs = [_place_shard(weights["conv_w"][:, None], l, j_arr, F32, f"place_conv_w_l{l}") for l in range(DEPTH)]
        start_copies("gather_start_first", convs, self.groups[:1], [place(self.groups[0])])
        start_copies("gather_start_rest", [], self.groups[1:], [place(grp) for grp in self.groups[1:]])

    def conv_w(self, l):
        if l not in self.conv_ready:
            buf, send, recv = self.conv_state[l]

            def wait(thru, sems, _, __):
                x, y, c, chips = _place()
                for k, chip in enumerate(chips):
                    mine, theirs = thru[0].at[2 * x + y], thru[0].at[2 * chip[0] + chip[1]]
                    _remote(mine, mine, sems[0].at[k], sems[1].at[k], (*chip, c)).wait_send()
                    _remote(theirs, theirs, sems[0].at[k], sems[1].at[k], (x, y, c)).wait_recv()

            (buf,), _, _ = _split_call(wait, f"gather_conv_w_l{l}", [buf], sems_in=[send, recv])
            self.conv_ready[l] = jnp.transpose(buf[:, 0], (1, 0, 2)).reshape(3, 2 * D_FF)
        return self.conv_ready[l]

    def weight(self, l, name):
        grp = (l, name)
        if grp not in self.ready:
            buf, send, recv = self.state[grp]

            def forward(thru, sems, _, new):
                x, y, c, chips = _place()
                for k, chip in enumerate(chips):
                    landed = thru[0].at[2 * chip[0] + chip[1], c]
                    _remote(landed, landed, new[0].at[k], sems[0].at[k], (x, y, c)).wait_recv()
                    _remote(landed, landed, new[0].at[k], new[1].at[k], (x, y, 1 - c)).start()

            (buf,), _, (fsend, frecv) = _split_call(forward, f"gather_pass_{name}_l{l}", [buf], sems_in=[recv],
                                                    new_sems=[(3,), (3,)])

            def finish(thru, sems, _, __):
                x, y, c, chips = _place()
                mine = thru[0].at[2 * x + y, c]
                for k, chip in enumerate(chips):
                    j_k = 2 * chip[0] + chip[1]
                    theirs, landed = thru[0].at[j_k, 1 - c], thru[0].at[j_k, c]
                    _remote(theirs, theirs, sems[1].at[k], sems[2].at[k], (x, y, c)).wait_recv()
                    _remote(landed, landed, sems[1].at[k], sems[2].at[k], (x, y, 1 - c)).wait_send()
                    _remote(mine, mine, sems[0].at[k], sems[2].at[k], (*chip, c)).wait_send()

            (buf,), _, _ = _split_call(finish, f"gather_done_{name}_l{l}", [buf], sems_in=[send, fsend, frecv])
            r, cols = self.shard_shape[name]
            self.ready[grp] = buf.reshape(N_CHIPS, r, cols) if name in ("w_in", "w_up") else buf.reshape(N_CHIPS * r, cols)
        return self.ready[grp]

    def grad(self, l, name, g):
        r, cols = self.shard_shape[name]
        g5 = g.reshape(N_CHIPS, 2, r // 2, cols)

        def start(thru, _, fresh, sems):
            x, y, c, _chips = _place()
            _remote(thru[0].at[:, 1 - c], fresh[0], sems[0], sems[1], (x, y, 1 - c)).start()

        (g5,), (recv,), sems = _split_call(start, f"pair_start_{name}_l{l}", [g5], fresh=[((N_CHIPS, r // 2, cols), BF16)],
                                          new_sems=[(), ()], after_last=False)
        self.pending.append(dict(l=l, name=name, stage=1, at=self.tick, bufs=(g5, recv), sems=sems))

    def _pair(self, grp):
        l, name = grp["l"], grp["name"]
        r, cols = self.shard_shape[name]

        def wait(thru, sems, _, __):
            x, y, c, _chips = _place()
            cp = _remote(thru[0].at[:, 1 - c], thru[1], sems[0], sems[1], (x, y, 1 - c))
            cp.wait_send()
            cp.wait_recv()

        (g5, recv), _, _ = _split_call(wait, f"pair_done_{name}_l{l}", list(grp["bufs"]), sems_in=list(grp["sems"]))
        p4 = _pair_sum(g5, recv, self.c_arr, f"pair_sum_{name}_l{l}")

        def start(thru, _, fresh, sems):
            x, y, c, chips = _place()
            for k, chip in enumerate(chips):
                _remote(thru[0].at[2 * chip[0] + chip[1]], fresh[0].at[k], sems[0].at[k], sems[1].at[k], (*chip, c)).start()

        (p4,), (recv3,), sems = _split_call(start, f"chips_start_{name}_l{l}", [p4], fresh=[((3, r // 2, cols), BF16)],
                                           new_sems=[(3,), (3,)], after_last=False)
        grp.update(stage=2, at=self.tick, bufs=(p4, recv3), sems=sems)

    def _chips(self, grp):
        l, name = grp["l"], grp["name"]

        def wait(thru, sems, _, __):
            x, y, c, chips = _place()
            for k, chip in enumerate(chips):
                cp = _remote(thru[0].at[2 * chip[0] + chip[1]], thru[1].at[k], sems[0].at[k], sems[1].at[k], (*chip, c))
                cp.wait_send()
                cp.wait_recv()

        (p4, recv3), _, _ = _split_call(wait, f"chips_done_{name}_l{l}", list(grp["bufs"]), sems_in=list(grp["sems"]))
        half = _chip_sum(p4, recv3, self.j_arr, self.c_arr, f"chip_sum_{name}_l{l}")

        def start(thru, _, __, sems):
            x, y, c, _chips = _place()
            _remote(thru[0].at[c], thru[0].at[c], sems[0], sems[1], (x, y, 1 - c)).start()

        (half,), _, sems = _split_call(start, f"join_start_{name}_l{l}", [half], new_sems=[(), ()], after_last=False)
        grp.update(stage=3, at=self.tick, bufs=(half,), sems=sems)

    def _update(self, grp):
        l, name = grp["l"], grp["name"]

        def wait(thru, sems, _, __):
            x, y, c, _chips = _place()
            _remote(thru[0].at[c], thru[0].at[c], sems[0], sems[1], (x, y, 1 - c)).wait_send()
            _remote(thru[0].at[1 - c], thru[0].at[1 - c], sems[0], sems[1], (x, y, c)).wait_recv()

        (full,), _, _ = _split_call(wait, f"join_done_{name}_l{l}", list(grp["bufs"]), sems_in=list(grp["sems"]))
        self.adam[name] = _adamw_layer(self.w[name], full.reshape(self.shard_shape[name]), self.m[name], self.v[name], l,
                                       self.adam.get(name), f"adamw_{name}_l{l}")
        grp.update(stage=4)

    def point(self, drain=False):
        self.tick += 1
        for grp in self.pending:
            if grp["stage"] == 3 and (drain or grp["at"] < self.tick):
                self._update(grp)
            elif grp["stage"] == 2 and (drain or grp["at"] + 2 <= self.tick):
                self._chips(grp)
            elif grp["stage"] == 1 and (drain or grp["at"] < self.tick):
                self._pair(grp)

    def finish(self):
        while any(grp["stage"] < 4 for grp in self.pending):
            self.point(drain=True)
        return self.adam

    @staticmethod
    def _peer(k, x, y, c):
        return (1 - x if k & 4 else x, 1 - y if k & 2 else y, 1 - c if k & 1 else c)

    def small_grads(self, l, grads, loss_tile):
        parts = [grads[nm] for nm in SMALL_NAMES] + ([loss_tile[0, 0:1]] if loss_tile is not None else [])
        packed = _pack_call(parts, f"small_pack_l{l}")
        rows = packed.shape[0]

        def start(thru, _, fresh, sems):
            x, y, c, _chips = _place()
            for k in range(1, 8):
                _remote(thru[0], fresh[0].at[4 * x + 2 * y + c], sems[0].at[k - 1], sems[1].at[k - 1],
                        self._peer(k, x, y, c)).start()

        (packed,), (landed,), sems = _split_call(start, f"small_start_l{l}", [packed], fresh=[((8, rows, PACK_LANES), F32)],
                                                 new_sems=[(7,), (7,)], after_last=False)
        self.small[l] =(packed, landed, sems, [p.shape for p in parts])

    def small_sum(self, l):
        packed, landed, sems, _shapes = self.small[l]

        def wait(thru, sems, _, __):
            x, y, c, _chips = _place()
            for k in range(1, 8):
                px, py, pc = self._peer(k, x, y, c)
                _remote(thru[0], thru[1].at[4 * x + 2 * y + c], sems[0].at[k - 1], sems[1].at[k - 1], (px, py, pc)).wait_send()
                _remote(thru[0], thru[1].at[4 * px + 2 * py + pc], sems[0].at[k - 1], sems[1].at[k - 1], (x, y, c)).wait_recv()

        (packed, landed), _, _ = _split_call(wait, f"small_done_l{l}", [packed, landed], sems_in=list(sems))
        return _sum_devices(packed, landed, self.me_arr, f"small_sum_l{l}")


def _rope_tables(s):
    inv_freq = ROPE_THETA ** (-jnp.arange(0, HEAD_DIM, 2, dtype=F32) / HEAD_DIM)
    ang = jnp.arange(s, dtype=F32)[:, None] * inv_freq[None, :]
    cos, sin = jnp.cos(ang), jnp.sin(ang)
    return jnp.concatenate([cos, cos], axis=-1), jnp.concatenate([-sin, sin], axis=-1)


def _local_step(x, target, ex, small):
    s = x.shape[0]
    cosf, sinf = _rope_tables(s)
    saved = []
    for l in range(DEPTH):
        p = small[l]
        t = f"l{l}"
        h = _rms_fwd(x, p["norm1_g"], f"norm1_{t}")
        z = _matmul(h, ex.weight(l, "w_in"), mode="nn", out_dtype=BF16, tm=1024, tn=896, tk=2048, b_parts=4, name=f"proj_in_{t}")
        qn, kn, vb, ug, vn = _proj_post(z, p["q_norm_g"], p["k_norm_g"], p["sgu_ln_g"], p["sgu_ln_b"], cosf, sinf, f"proj_post_{t}")
        attn, sgu, mixed = _mixer_fwd(qn, kn, vb, ug, vn, p["w_s_bf16"], p["b_s_tile"], p["sink"], p["attn_out_g"],
                                      p["sgu_out_g"], f"mixer_{t}")
        x1 = _matmul(mixed, ex.weight(l, "w_o"), mode="nn", out_dtype=F32, tm=2048, tn=256, tk=2048, res=x,
                     name=f"proj_out_{t}")
        h2 = _rms_fwd(x1, p["norm2_g"], f"norm2_{t}")
        a_pre = _matmul(h2, ex.weight(l, "w_up"), mode="nn", out_dtype=BF16, tm=1024, tn=1408, tk=2048, b_parts=4,
                        out_parts=2,
                        name=f"ffn_up_{t}")
        act, dgu = _conv_gate_fwd(a_pre, ex.conv_w(l), p["conv_b"], f"conv_gate_{t}")
        x2 = _matmul(act, ex.weight(l, "w_down"), mode="nn", out_dtype=F32, tm=1024, tn=256, tk=D_FF, res=x1,
                     name=f"ffn_down_{t}")
        saved.append(dict(x=x, h=h, z=z, qn=qn, kn=kn, vb=vb, ug=ug, vn=vn, attn=attn, sgu=sgu, mixed=mixed, x1=x1, h2=h2,
                          a_pre=a_pre, act=act, dgu=dgu))
        x = x2
    loss_tile, dx, dxb = _loss_head(x, target, "loss_head")
    for l in reversed(range(DEPTH)):
        p, sv = small[l], saved[l]
        t = f"l{l}"
        ex.grad(l, "w_down", _matmul(sv["act"], dxb, mode="tn", out_dtype=BF16, tm=1408, tn=512, tk=2048,
                                     name=f"g_w_down_{t}"))
        dact = _matmul(dxb, ex.weight(l, "w_down"), mode="nt", out_dtype=BF16, tm=1024, tn=512, tk=2048,
                       name=f"d_act_{t}")
        dap, dcw, dcb = _conv_gate_bwd(sv["a_pre"], sv["dgu"], ex.conv_w(l), dact, f"conv_gate_bwd_{t}")
        ex.point()
        ex.grad(l, "w_up", _matmul(sv["h2"], dap, mode="tn", out_dtype=BF16, tm=1024, tn=1408, tk=2048, b_parts=2,
                                   out_parts=4,
                                   name=f"g_w_up_{t}"))
        dh2 = _matmul(dap, ex.weight(l, "w_up"), mode="nt", out_dtype=F32, tm=1024, tn=1024, tk=2816, a_parts=2, b_parts=4,
                      name=f"d_h2_{t}")
        dx1, dx1b, dg2 = _rms_bwd(sv["x1"], p["norm2_g"], dh2, dx, f"norm2_bwd_{t}")
        ex.point()
        ex.grad(l, "w_o", _matmul(sv["mixed"], dx1b, mode="tn", out_dtype=BF16, tm=1024, tn=512, tk=2048,
                                  name=f"g_w_o_{t}"))
        dmixed = _matmul(dx1b, ex.weight(l, "w_o"), mode="nt", out_dtype=F32, tm=1024, tn=512, tk=2048,
                         name=f"d_mixed_{t}")
        dqn, dkn, dvb, dug, dvn, dws, dbs, dsk, dga, dgs = _mixer_bwd(
            sv["qn"], sv["kn"], sv["vb"], sv["ug"], sv["vn"], sv["attn"], sv["sgu"], dmixed, p["w_s_bf16"], p["b_s_tile"],
            p["sink"], p["attn_out_g"], p["sgu_out_g"], f"mixer_bwd_{t}")
        dz, dqg, dkg, dlg, dlb = _proj_post_bwd(sv["z"], dqn, dkn, dvb, dug, dvn, p["q_norm_g"], p["k_norm_g"], p["sgu_ln_g"],
                                                 cosf, sinf, f"proj_post_bwd_{t}")
        ex.point()
        ex.grad(l, "w_in", _matmul(sv["h"], dz, mode="tn", out_dtype=BF16, tm=1024, tn=896, tk=2048, out_parts=4,
                                   name=f"g_w_in_{t}"))
        dh = _matmul_nt_slabs(dz, ex.weight(l, "w_in"), tm=1024, tn=512, name=f"d_h_{t}")
        dx, dxb, dg1 = _rms_bwd(sv["x"], p["norm1_g"], dh, dx1, f"norm1_bwd_{t}")
        ex.point()
        ex.small_grads(l, dict(
            norm1_g=dg1[0], q_norm_g=dqg[0], k_norm_g=dkg[0], sink=dsk[:, 0], sgu_ln_g=dlg[0], sgu_ln_b=dlb[0], w_s=dws,
            b_s=dbs[:, :, 0], attn_out_g=dga[0], sgu_out_g=dgs[0], norm2_g=dg2[0],
            conv_w=jnp.concatenate([dcw[0], dcw[1]], axis=-1), conv_b=jnp.concatenate([dcb[0, 0], dcb[1, 0]], axis=-1)),
            loss_tile if l == 0 else None)
    return dx


def _small_views(l, norm1_g, q_norm_g, k_norm_g, sink, sgu_ln_g, sgu_ln_b, w_s, b_s, attn_out_g, sgu_out_g, norm2_g, conv_b):
    return dict(
        norm1_g=norm1_g[l][None], q_norm_g=q_norm_g[l][None], k_norm_g=k_norm_g[l][None], sink=sink[l],
        sgu_ln_g=sgu_ln_g[l][None], sgu_ln_b=sgu_ln_b[l][None], w_s_bf16=w_s[l].astype(BF16),
        b_s_tile=jnp.broadcast_to(b_s[l][:, :, None], (N_GMLP_HEADS, BLOCK, BLOCK)), attn_out_g=attn_out_g[l][None],
        sgu_out_g=sgu_out_g[l][None], norm2_g=norm2_g[l][None], conv_b=conv_b[l][None])


SMALL_NAMES = ("norm1_g", "q_norm_g", "k_norm_g", "sink", "sgu_ln_g", "sgu_ln_b", "w_s", "b_s", "attn_out_g", "sgu_out_g",
               "norm2_g", "conv_b", "conv_w")
REPLICATED_NAMES = SMALL_NAMES[:-1]
BIG_NAMES = ("w_in", "w_o", "w_up", "w_down")
PACK_LANES = 128
PACK_ALIGN = 8 * PACK_LANES


def _pack_rows(shape):
    return -(-math.prod(shape) // PACK_ALIGN) * 8


def _pack_parts(arrays):
    parts = []
    for a in arrays:
        flat = a.reshape(-1)
        parts.append(jnp.pad(flat, (0, _pack_rows(a.shape) * PACK_LANES - flat.shape[0])).reshape(-1, PACK_LANES))
    return parts


def _pack(arrays):
    return jnp.concatenate(_pack_parts(arrays), axis=0)


def _pack_call(arrays, name):
    parts = _pack_parts(arrays)
    total = sum(p.shape[0] for p in parts)

    def body(*refs):
        o_ref, at = refs[-1], 0
        for p_ref in refs[:-1]:
            o_ref[at:at + p_ref.shape[0], :] = p_ref[...]
            at += p_ref.shape[0]

    vm = pl.BlockSpec(memory_space=pltpu.VMEM)
    return _ordered_call(
        body, name=name, out_shape=jax.ShapeDtypeStruct((total, PACK_LANES), F32), in_specs=[vm] * len(parts), out_specs=vm,
        compiler_params=pltpu.CompilerParams(vmem_limit_bytes=V7X_VMEM_LIMIT),
    )(*parts)


def _unpack_layers(stacked, shapes):
    nl = stacked.shape[0]
    out, at = [], 0
    for shp in shapes:
        rows = _pack_rows(shp)
        out.append(stacked[:, at:at + rows].reshape(nl, -1)[:, :math.prod(shp)].reshape((nl,) + tuple(shp)))
        at += rows
    return out


def _adamw_packed(w, g, m, v, rows, layer, into, name):
    head = pl.BlockSpec((rows, PACK_LANES), lambda i: (0, 0))
    at_layer = pl.BlockSpec((None, rows, PACK_LANES), lambda i: (layer, 0, 0))

    def body(w_ref, g_ref, m_ref, v_ref, *rest):
        d_ref, nm_ref, nv_ref = rest[-3:]
        gv = g_ref[...]
        mn = ADAM_B1 * m_ref[...] + (1.0 - ADAM_B1) * gv
        vn = ADAM_B2 * v_ref[...] + (1.0 - ADAM_B2) * (gv * gv)
        m_hat = mn / (1.0 - ADAM_B1 ** ADAM_STEP)
        v_hat = vn / (1.0 - ADAM_B2 ** ADAM_STEP)
        d_ref[...] = -ADAM_LR * (m_hat / (jnp.sqrt(v_hat) + ADAM_EPS) + ADAM_WD * w_ref[...])
        nm_ref[...] = mn
        nv_ref[...] = vn

    in_specs = [head] * 4
    operands = [w, g, m, v]
    aliases = {}
    if into is not None:
        in_specs += [ANY] * 3
        operands += list(into)
        aliases = {4 + i: i for i in range(3)}
    sds = jax.ShapeDtypeStruct((DEPTH, rows, PACK_LANES), F32)
    return _ordered_call(
        body, name=name, out_shape=(sds,) * 3, grid=(1,), in_specs=in_specs, out_specs=(at_layer,) * 3,
        input_output_aliases=aliases, compiler_params=_params(("arbitrary",)),
    )(*operands)


def kernel(x, norm1_g, w_in, q_norm_g, k_norm_g, sink, sgu_ln_g, sgu_ln_b, w_s, b_s, attn_out_g, sgu_out_g, w_o, norm2_g, w_up, conv_w, conv_b, w_down, loss_target, m_norm1_g, m_w_in, m_q_norm_g, m_k_norm_g, m_sink, m_sgu_ln_g, m_sgu_ln_b, m_w_s, m_b_s, m_attn_out_g, m_sgu_out_g, m_w_o, m_norm2_g, m_w_up, m_conv_w, m_conv_b, m_w_down, v_norm1_g, v_w_in, v_q_norm_g, v_k_norm_g, v_sink, v_sgu_ln_g, v_sgu_ln_b, v_w_s, v_b_s, v_attn_out_g, v_sgu_out_g, v_w_o, v_norm2_g, v_w_up, v_conv_w, v_conv_b, v_w_down):
    weights = dict(norm1_g=norm1_g, w_in=w_in, q_norm_g=q_norm_g, k_norm_g=k_norm_g, sink=sink, sgu_ln_g=sgu_ln_g,
                   sgu_ln_b=sgu_ln_b, w_s=w_s, b_s=b_s, attn_out_g=attn_out_g, sgu_out_g=sgu_out_g, w_o=w_o, norm2_g=norm2_g,
                   w_up=w_up, conv_w=conv_w, conv_b=conv_b, w_down=w_down)
    m_in = dict(norm1_g=m_norm1_g, w_in=m_w_in, q_norm_g=m_q_norm_g, k_norm_g=m_k_norm_g, sink=m_sink, sgu_ln_g=m_sgu_ln_g,
                sgu_ln_b=m_sgu_ln_b, w_s=m_w_s, b_s=m_b_s, attn_out_g=m_attn_out_g, sgu_out_g=m_sgu_out_g, w_o=m_w_o,
                norm2_g=m_norm2_g, w_up=m_w_up, conv_w=m_conv_w, conv_b=m_conv_b, w_down=m_w_down)
    v_in = dict(norm1_g=v_norm1_g, w_in=v_w_in, q_norm_g=v_q_norm_g, k_norm_g=v_k_norm_g, sink=v_sink, sgu_ln_g=v_sgu_ln_g,
                sgu_ln_b=v_sgu_ln_b, w_s=v_w_s, b_s=v_b_s, attn_out_g=v_attn_out_g, sgu_out_g=v_sgu_out_g, w_o=v_w_o,
                norm2_g=v_norm2_g, w_up=v_w_up, conv_w=v_conv_w, conv_b=v_conv_b, w_down=v_w_down)
    cx, cy, cc = lax.axis_index("x"), lax.axis_index("y"), lax.axis_index("c")
    j_me = 2 * cx + cy
    c_arr = jnp.reshape(cc, (1,)).astype(jnp.int32)
    j_arr = jnp.reshape(j_me, (1,)).astype(jnp.int32)

    _Order.last = None
    ex = _Exchange(weights, m_in, v_in, j_arr, c_arr, jnp.reshape(4 * cx + 2 * cy + cc, (1,)).astype(jnp.int32))
    small = [_small_views(l, norm1_g, q_norm_g, k_norm_g, sink, sgu_ln_g, sgu_ln_b, w_s, b_s, attn_out_g, sgu_out_g, norm2_g,
                          conv_b) for l in range(DEPTH)]
    packed_in = [[_pack_call([src[nm][l] for nm in REPLICATED_NAMES], f"pack_{tag}_l{l}")
                  for tag, src in (("w", weights), ("m", m_in), ("v", v_in))] for l in range(DEPTH)]
    dx = _local_step(x[0], loss_target[0], ex, small)
    big_out = ex.finish()

    rep_shapes = [weights[nm].shape[1:] for nm in REPLICATED_NAMES]
    rep_rows = sum(_pack_rows(shp) for shp in rep_shapes)
    cw_shape = (3, 2 * D_FF)
    sums, adam_small = [None] * DEPTH, None
    for l in reversed(range(DEPTH)):
        sums[l] = ex.small_sum(l)
        pw, pm, pv = packed_in[l]
        adam_small = _adamw_packed(pw, sums[l], pm, pv, rep_rows, l, adam_small, f"adamw_small_l{l}")
    cw_rows = _pack_rows(cw_shape)
    loss = sums[0][rep_rows + cw_rows, 0]
    stacked = jnp.stack([sm[:rep_rows + cw_rows] for sm in sums])
    grads = dict(zip(REPLICATED_NAMES, _unpack_layers(stacked[:, :rep_rows], rep_shapes)))
    delta, new_m, new_v = (dict(zip(REPLICATED_NAMES, _unpack_layers(arr, rep_shapes))) for arr in adam_small)
    cw_cols = 2 * D_FF // N_CHIPS
    cw_grad = lax.dynamic_slice_in_dim(_unpack_layers(stacked[:, rep_rows:], [cw_shape])[0], j_me * cw_cols, cw_cols, axis=2)
    flat = lambda a: a.reshape(DEPTH * 3, cw_cols)
    cw_out = _adamw(flat(conv_w), flat(cw_grad), flat(m_conv_w), flat(v_conv_w), "adamw_conv_w")
    grads["conv_w"], delta["conv_w"], new_m["conv_w"], new_v["conv_w"] = (a.reshape(DEPTH, 3, cw_cols) for a in cw_out)

    for name in BIG_NAMES:
        grads[name], delta[name], new_m[name], new_v[name] = big_out[name]

    order = ("norm1_g", "w_in", "q_norm_g", "k_norm_g", "sink", "sgu_ln_g", "sgu_ln_b", "w_s", "b_s", "attn_out_g", "sgu_out_g",
             "w_o", "norm2_g", "w_up", "conv_w", "conv_b", "w_down")
    return (loss, dx[None], *[grads[nm] for nm in order], *[delta[nm] for nm in order], *[new_m[nm] for nm in order],
            *[new_v[nm] for nm in order])
```

```python
import functools
import math

import jax
import jax.numpy as jnp
from jax import lax
from jax.experimental import pallas as pl
from jax.experimental.pallas import tpu as pltpu

F32 = jnp.float32
BF16 = jnp.bfloat16

D_MODEL = 2048
HEAD_DIM = 128
ATTN_WIDTH = 1024
N_Q_HEADS = 8
N_KV_HEADS = 2
GQA_GROUP = 4
KV_WIDTH = 256
GMLP_WIDTH = 1024
N_GMLP_HEADS = 8
BLOCK = 128
IN_WIDTH = 3584
D_FF = 5632
DEPTH = 2
EPS = 1e-6
MASK_VALUE = -1e30
ROPE_THETA = 10000.0
N_CHIPS = 4

ADAM_LR = 0.001
ADAM_B1 = 0.9
ADAM_B2 = 0.999
ADAM_EPS = 1e-08
ADAM_WD = 0.01
ADAM_STEP = 10

V7X_VMEM_LIMIT = 48 * 1024 * 1024
MESH = pl.DeviceIdType.MESH

_GELU_C = math.sqrt(2.0 / math.pi)
_GELU_A = 0.044715


def _params(sem=None):
    return pltpu.CompilerParams(dimension_semantics=sem, vmem_limit_bytes=V7X_VMEM_LIMIT)


ANY = pl.BlockSpec(memory_space=pl.ANY)


class _Order:
    last = None


def _ordered_call(body, *, token_index=0, **kw):
    def run(*operands):
        tok = _Order.last
        if tok is None or any(op is tok for op in operands):
            call = pl.pallas_call(body, **kw)
        else:
            n_in = len(operands)

            def ordered_body(*refs):
                return body(*refs[:n_in], *refs[n_in + 1:])

            kw2 = dict(kw)
            if "grid_spec" in kw2:
                gs = kw2["grid_spec"]
                kw2["grid_spec"] = pltpu.PrefetchScalarGridSpec(
                    num_scalar_prefetch=gs.num_scalar_prefetch, grid=gs.grid, in_specs=list(gs.in_specs) + [ANY],
                    out_specs=gs.out_specs, scratch_shapes=gs.scratch_shapes)
            else:
                kw2["in_specs"] = list(kw2["in_specs"]) + [ANY]
            call = pl.pallas_call(ordered_body, **kw2)
            operands = operands + (tok,)
        out = call(*operands)
        _Order.last = out[token_index] if isinstance(out, (tuple, list)) else out
        return out

    return run


def _gelu(x):
    return x * (0.5 * (1.0 + jnp.tanh(_GELU_C * (x + _GELU_A * (x * x * x)))))


def _gelu_grad(x):
    x2 = x * x
    t = jnp.tanh(_GELU_C * (x + _GELU_A * (x * x2)))
    return 0.5 * (1.0 + t) + 0.5 * x * (1.0 - t * t) * (_GELU_C * (1.0 + 3.0 * _GELU_A * x2))


def _mean_last(x):
    return jnp.mean(x, axis=-1, keepdims=True)


def _sum_rows(x):
    return jnp.sum(x, axis=0, keepdims=True)


def _sum_all(x):
    return jnp.sum(jnp.sum(x, axis=1, keepdims=True), axis=0, keepdims=True)


def _matmul(a, b, *, mode, out_dtype, tm, tn, tk, name, res=None, a_parts=0, b_parts=0, out_parts=0, b_lead=()):
    b_full = b
    b = jax.ShapeDtypeStruct(b.shape[len(b_lead):], b.dtype)
    if mode == "nn":
        assert not a_parts
        m, k = a.shape
        n = b.shape[0] * b.shape[2] if b_parts else b.shape[1]
    elif mode == "nt":
        m, k = (a.shape[1], a.shape[0] * a.shape[2]) if a_parts else a.shape
        n = b.shape[1] if b_parts else b.shape[0]
    else:
        assert not a_parts
        k, m = a.shape
        n = b.shape[0] * b.shape[2] if b_parts else b.shape[1]
    tm, tn, tk = min(tm, m), min(tn, n), min(tk, k)
    assert m % tm == 0 and n % tn == 0 and k % tk == 0, (name, m, n, k, tm, tn, tk)
    nm, nn, nk = m // tm, n // tn, k // tk

    def slab(idx, total_tiles, parts):
        per = total_tiles // parts
        assert per * parts == total_tiles, (name, total_tiles, parts)
        return idx // per, idx % per

    if mode == "nn":
        a_spec = pl.BlockSpec((tm, tk), lambda i, j, kk: (i, kk))
        if b_parts:
            b_spec = pl.BlockSpec((None, tk, tn), lambda i, j, kk: (slab(j, nn, b_parts)[0], kk, slab(j, nn, b_parts)[1]))
        else:
            b_spec = pl.BlockSpec((tk, tn), lambda i, j, kk: (kk, j))
        dims = (((1,), (0,)), ((), ()))
    elif mode == "nt":
        if a_parts:
            a_spec = pl.BlockSpec((None, tm, tk), lambda i, j, kk: (slab(kk, nk, a_parts)[0], i, slab(kk, nk, a_parts)[1]))
        else:
            a_spec = pl.BlockSpec((tm, tk), lambda i, j, kk: (i, kk))
        if b_parts:
            b_spec = pl.BlockSpec((None, tn, tk), lambda i, j, kk: (slab(kk, nk, b_parts)[0], j, slab(kk, nk, b_parts)[1]))
        else:
            b_spec = pl.BlockSpec((tn, tk), lambda i, j, kk: (j, kk))
        dims = (((1,), (1,)), ((), ()))
    else:
        a_spec = pl.BlockSpec((tk, tm), lambda i, j, kk: (kk, i))
        if b_parts:
            b_spec = pl.BlockSpec((None, tk, tn), lambda i, j, kk: (slab(j, nn, b_parts)[0], kk, slab(j, nn, b_parts)[1]))
        else:
            b_spec = pl.BlockSpec((tk, tn), lambda i, j, kk: (kk, j))
        dims = (((0,), (0,)), ((), ()))
    if out_parts:
        out_shape = jax.ShapeDtypeStruct((out_parts, m, n // out_parts), out_dtype)
        out_spec = pl.BlockSpec((None, tm, tn), lambda i, j, kk: (slab(j, nn, out_parts)[0], i, slab(j, nn, out_parts)[1]))
    else:
        out_shape = jax.ShapeDtypeStruct((m, n), out_dtype)
        out_spec = pl.BlockSpec((tm, tn), lambda i, j, kk: (i, j))
    if b_lead:
        inner_map = b_spec.index_map
        b_spec = pl.BlockSpec((None,) * len(b_lead) + tuple(b_spec.block_shape),
                              lambda i, j, kk: tuple(b_lead) + tuple(inner_map(i, j, kk)))
    in_specs = [a_spec, b_spec]
    operands = [a, b_full]
    if res is not None:
        in_specs.append(pl.BlockSpec((tm, tn), lambda i, j, kk: (i, j)))
        operands.append(res)

    def body(*refs):
        a_ref, b_ref = refs[0], refs[1]
        res_ref = refs[2] if res is not None else None
        o_ref = refs[3] if res is not None else refs[2]
        p = lax.dot_general(a_ref[...], b_ref[...], dims, preferred_element_type=F32)

        def finish(total):
            if res_ref is not None:
                total = res_ref[...] + total
            o_ref[...] = total.astype(out_dtype)

        if nk == 1:
            finish(p)
        else:
            acc_ref = refs[-1]
            kk = pl.program_id(2)

            @pl.when(kk == 0)
            def _():
                acc_ref[...] = p

            @pl.when(jnp.logical_and(kk > 0, kk < nk - 1))
            def _():
                acc_ref[...] += p

            @pl.when(kk == nk - 1)
            def _():
                finish(acc_ref[...] + p)

    scratch = [pltpu.VMEM((tm, tn), F32)] if nk > 1 else []
    return _ordered_call(
        body, name=name, out_shape=out_shape, grid=(nm, nn, nk), in_specs=in_specs, out_specs=out_spec,
        scratch_shapes=scratch, compiler_params=_params(("parallel", "parallel", "arbitrary")),
    )(*operands)


def _matmul_nt_slabs(a, b, *, tm, tn, name, a_parts=0):
    nslab, n, ks = b.shape
    m = a.shape[1] if a_parts else a.shape[0]
    tm, tn = min(tm, m), min(tn, n)
    assert m % tm == 0 and n % tn == 0, (name, m, n, tm, tn)
    if a_parts:
        per = nslab // a_parts
        assert per * a_parts == nslab and a.shape[2] == per * ks, (name, a.shape, b.shape)
        a_spec = pl.BlockSpec((a_parts, tm, per * ks), lambda i, j: (0, i, 0))
    else:
        assert a.shape[1] == nslab * ks, (name, a.shape, b.shape)
        a_spec = pl.BlockSpec((tm, nslab * ks), lambda i, j: (i, 0))

    def body(a_ref, b_ref, o_ref):
        total = None
        for sl in range(nslab):
            if a_parts:
                a_sl = a_ref[sl // per, :, (sl % per) * ks:(sl % per + 1) * ks]
            else:
                a_sl = a_ref[:, sl * ks:(sl + 1) * ks]
            p = lax.dot_general(a_sl, b_ref[sl], (((1,), (1,)), ((), ())), preferred_element_type=F32)
            total = p if total is None else total + p
        o_ref[...] = total

    return _ordered_call(
        body, name=name, out_shape=jax.ShapeDtypeStruct((m, n), F32), grid=(m // tm, n // tn),
        in_specs=[a_spec, pl.BlockSpec((nslab, tn, ks), lambda i, j: (0, j, 0))],
        out_specs=pl.BlockSpec((tm, tn), lambda i, j: (i, j)), compiler_params=_params(("parallel", "parallel")),
    )(a, b)


GRAD_HALVES = {
    "w_in": ("rows_of_slab", 1024, 896), "w_up": ("rows_of_slab", 1024, 1408), "w_o": ("rows_of_block", 256, 2048),
    "w_down": ("cols_of_block", 1408, 512)}


def _half_shape(name, shard_shape):
    r, cols = shard_shape
    return (r, cols // 2) if GRAD_HALVES[name][0] == "cols_of_block" else (r // 2, cols)


def _grad_half(name, a, g, sel, res, call_name, g_parts=0):
    kind, tm, tn = GRAD_HALVES[name]
    s, m = a.shape
    n = g.shape[0] * g.shape[2] if g_parts else g.shape[1]
    if kind == "rows_of_slab":
        rh, hc = m // 2, n // N_CHIPS
        per = hc // tn
        grid = (rh // tm, n // tn)
        a_map = lambda i, j, sel_ref: (0, sel_ref[0] * (rh // tm) + i)
        g_col = lambda i, j, sel_ref: j
        o_map = lambda i, j, sel_ref: (j // per, i, j % per)
    elif kind == "rows_of_block":
        rh, hc = m // N_CHIPS // 2, n
        assert tm == rh
        grid = (N_CHIPS, n // tn)
        a_map = lambda i, j, sel_ref: (0, 2 * i + sel_ref[0])
        g_col = lambda i, j, sel_ref: j
        o_map = lambda i, j, sel_ref: (i, 0, j)
    else:
        rh, hc = m // N_CHIPS, n // 2
        assert tm == rh
        grid = (N_CHIPS, hc // tn)
        a_map = lambda i, j, sel_ref: (0, i)
        g_col = lambda i, j, sel_ref: sel_ref[0] * (hc // tn) + j
        o_map = lambda i, j, sel_ref: (i, 0, j)
    if g_parts:
        g_per = (n // tn) // g_parts
        g_spec = pl.BlockSpec((None, s, tn), lambda i, j, sel_ref: (g_col(i, j, sel_ref) // g_per, 0, g_col(i, j, sel_ref) % g_per))
    else:
        g_spec = pl.BlockSpec((s, tn), lambda i, j, sel_ref: (0, g_col(i, j, sel_ref)))
    o_spec = pl.BlockSpec((None, tm, tn), o_map)
    in_specs = [pl.BlockSpec((s, tm), a_map), g_spec] + ([o_spec] if res is not None else [])

    def body(sel_ref, a_ref, g_ref, *rest):
        o_ref = rest[-1]
        p = lax.dot_general(a_ref[...], g_ref[...], (((0,), (0,)), ((), ())), preferred_element_type=F32)
        if res is not None:
            p = p + rest[0][...].astype(F32)
        o_ref[...] = p.astype(BF16)

    grid_spec = pltpu.PrefetchScalarGridSpec(num_scalar_prefetch=1, grid=grid, in_specs=in_specs, out_specs=o_spec)
    return _ordered_call(
        body, name=call_name, out_shape=jax.ShapeDtypeStruct((N_CHIPS, rh, hc), BF16), grid_spec=grid_spec,
        compiler_params=_params(("parallel", "parallel")),
    )(sel, a, g, *([res] if res is not None else []))


def _row_tile(s):
    return min(256, s)


def _rows(width, tr):
    return pl.BlockSpec((tr, width), lambda i: (i, 0))


def _const2(shape):
    return pl.BlockSpec(shape, lambda i: (0, 0))


def _rms_fwd(x, g, name):
    s, d = x.shape
    tr = _row_tile(s)

    def body(x_ref, g_ref, o_ref):
        xv = x_ref[...]
        r = lax.rsqrt(_mean_last(xv * xv) + EPS)
        o_ref[...] = (xv * r * g_ref[...]).astype(BF16)

    return _ordered_call(
        body, name=name, out_shape=jax.ShapeDtypeStruct((s, d), BF16), grid=(s // tr,),
        in_specs=[_rows(d, tr), _const2((1, d))], out_specs=_rows(d, tr), compiler_params=_params(("parallel",)),
    )(x, g)


def _rms_bwd(x, g, dh, dres, name):
    s, d = x.shape
    tr = _row_tile(s)

    def body(x_ref, g_ref, dh_ref, dres_ref, dx_ref, dxb_ref, dg_ref):
        xv, dy = x_ref[...], dh_ref[...]
        r = lax.rsqrt(_mean_last(xv * xv) + EPS)
        gdy = dy * g_ref[...]
        dx = dres_ref[...] + r * gdy - xv * ((r * r * r) * _mean_last(xv * gdy))
        dx_ref[...] = dx
        dxb_ref[...] = dx.astype(BF16)

        @pl.when(pl.program_id(0) == 0)
        def _():
            dg_ref[...] = jnp.zeros_like(dg_ref)

        dg_ref[...] += _sum_rows(xv * r * dy)

    return _ordered_call(
        body, name=name,
        out_shape=(jax.ShapeDtypeStruct((s, d), F32), jax.ShapeDtypeStruct((s, d), BF16), jax.ShapeDtypeStruct((1, d), F32)),
        grid=(s // tr,), in_specs=[_rows(d, tr), _const2((1, d)), _rows(d, tr), _rows(d, tr)],
        out_specs=(_rows(d, tr), _rows(d, tr), _const2((1, d))), compiler_params=_params(("arbitrary",)),
    )(x, g, dh, dres)


Q0, K0, V0, GU0, GV0 = 0, ATTN_WIDTH, ATTN_WIDTH + KV_WIDTH, ATTN_WIDTH + 2 * KV_WIDTH, ATTN_WIDTH + 2 * KV_WIDTH + GMLP_WIDTH


def _head(h, base=0):
    return slice(base + h * HEAD_DIM, base + (h + 1) * HEAD_DIM)


def _proj_post(z, qg, kg, lg, lb, cosf, sinf, name):
    s = z.shape[0]
    tr = _row_tile(s)

    def body(z_ref, qg_ref, kg_ref, lg_ref, lb_ref, cos_ref, sin_ref, qn_ref, kn_ref, vb_ref, ug_ref, vn_ref):
        cos, sin = cos_ref[...], sin_ref[...]

        def norm_rope(xh, g):
            y = xh * lax.rsqrt(_mean_last(xh * xh) + EPS) * g
            return y * cos + pltpu.roll(y, HEAD_DIM // 2, 1) * sin

        for h in range(N_Q_HEADS):
            qn_ref[:, _head(h)] = norm_rope(z_ref[:, _head(h, Q0)].astype(F32), qg_ref[...]).astype(BF16)
        for h in range(N_KV_HEADS):
            kn_ref[:, _head(h)] = norm_rope(z_ref[:, _head(h, K0)].astype(F32), kg_ref[...]).astype(BF16)
        vb_ref[...] = z_ref[:, V0:GU0]
        ug_ref[...] = _gelu(z_ref[:, GU0:GV0].astype(F32))
        vg = _gelu(z_ref[:, GV0:IN_WIDTH].astype(F32))
        xc = vg - _mean_last(vg)
        y = xc * lax.rsqrt(_mean_last(xc * xc) + EPS)
        vn_ref[...] = (y * lg_ref[...] + lb_ref[...]).astype(BF16)

    return _ordered_call(
        body, name=name,
        out_shape=(jax.ShapeDtypeStruct((s, ATTN_WIDTH), BF16), jax.ShapeDtypeStruct((s, KV_WIDTH), BF16),
                   jax.ShapeDtypeStruct((s, KV_WIDTH), BF16), jax.ShapeDtypeStruct((s, GMLP_WIDTH), F32),
                   jax.ShapeDtypeStruct((s, GMLP_WIDTH), BF16)),
        grid=(s // tr,),
        in_specs=[_rows(IN_WIDTH, tr), _const2((1, HEAD_DIM)), _const2((1, HEAD_DIM)), _const2((1, GMLP_WIDTH)),
                  _const2((1, GMLP_WIDTH)), _rows(HEAD_DIM, tr), _rows(HEAD_DIM, tr)],
        out_specs=(_rows(ATTN_WIDTH, tr), _rows(KV_WIDTH, tr), _rows(KV_WIDTH, tr), _rows(GMLP_WIDTH, tr), _rows(GMLP_WIDTH, tr)),
        compiler_params=_params(("parallel",)),
    )(z, qg, kg, lg, lb, cosf, sinf)


def _proj_post_bwd(z, dqn, dkn, dvb, dug, dvn, qg, kg, lg, cosf, sinf, name):
    s = z.shape[0]
    tr = _row_tile(s)

    def body(z_ref, dqn_ref, dkn_ref, dvb_ref, dug_ref, dvn_ref, qg_ref, kg_ref, lg_ref, cos_ref, sin_ref,
             dz_ref, dqg_ref, dkg_ref, dlg_ref, dlb_ref):
        cos, sin = cos_ref[...], sin_ref[...]

        @pl.when(pl.program_id(0) == 0)
        def _():
            dqg_ref[...] = jnp.zeros_like(dqg_ref)
            dkg_ref[...] = jnp.zeros_like(dkg_ref)
            dlg_ref[...] = jnp.zeros_like(dlg_ref)
            dlb_ref[...] = jnp.zeros_like(dlb_ref)

        def norm_rope_bwd(xh, g, dout):
            dy = dout * cos - pltpu.roll(dout, HEAD_DIM // 2, 1) * sin
            r = lax.rsqrt(_mean_last(xh * xh) + EPS)
            xhat = xh * r
            gdy = dy * g
            return r * (gdy - xhat * _mean_last(xhat * gdy)), _sum_rows(xhat * dy)

        dqg = jnp.zeros((1, HEAD_DIM), F32)
        for h in range(N_Q_HEADS):
            dx, dg = norm_rope_bwd(z_ref[:, _head(h, Q0)].astype(F32), qg_ref[...], dqn_ref[:, _head(h)])
            dz_ref[:, _head(h, Q0)] = dx.astype(BF16)
            dqg = dqg + dg
        dqg_ref[...] += dqg
        dkg = jnp.zeros((1, HEAD_DIM), F32)
        for h in range(N_KV_HEADS):
            dx, dg = norm_rope_bwd(z_ref[:, _head(h, K0)].astype(F32), kg_ref[...], dkn_ref[:, _head(h)])
            dz_ref[:, _head(h, K0)] = dx.astype(BF16)
            dkg = dkg + dg
        dkg_ref[...] += dkg
        dz_ref[:, V0:GU0] = dvb_ref[...].astype(BF16)
        dz_ref[:, GU0:GV0] = (dug_ref[...] * _gelu_grad(z_ref[:, GU0:GV0].astype(F32))).astype(BF16)
        gv = z_ref[:, GV0:IN_WIDTH].astype(F32)
        vg = _gelu(gv)
        xc = vg - _mean_last(vg)
        r = lax.rsqrt(_mean_last(xc * xc) + EPS)
        xhat = xc * r
        dvn_v = dvn_ref[...]
        dlg_ref[...] += _sum_rows(xhat * dvn_v)
        dlb_ref[...] += _sum_rows(dvn_v)
        dxh = dvn_v * lg_ref[...]
        dvg = r * (dxh - _mean_last(dxh) - xhat * _mean_last(dxh * xhat))
        dz_ref[:, GV0:IN_WIDTH] = (dvg * _gelu_grad(gv)).astype(BF16)

    return _ordered_call(
        body, name=name,
        out_shape=(jax.ShapeDtypeStruct((s, IN_WIDTH), BF16), jax.ShapeDtypeStruct((1, HEAD_DIM), F32),
                   jax.ShapeDtypeStruct((1, HEAD_DIM), F32), jax.ShapeDtypeStruct((1, GMLP_WIDTH), F32),
                   jax.ShapeDtypeStruct((1, GMLP_WIDTH), F32)),
        grid=(s // tr,),
        in_specs=[_rows(IN_WIDTH, tr), _rows(ATTN_WIDTH, tr), _rows(KV_WIDTH, tr), _rows(KV_WIDTH, tr), _rows(GMLP_WIDTH, tr),
                  _rows(GMLP_WIDTH, tr), _const2((1, HEAD_DIM)), _const2((1, HEAD_DIM)), _const2((1, GMLP_WIDTH)),
                  _rows(HEAD_DIM, tr), _rows(HEAD_DIM, tr)],
        out_specs=(_rows(IN_WIDTH, tr), _const2((1, HEAD_DIM)), _const2((1, HEAD_DIM)), _const2((1, GMLP_WIDTH)),
                   _const2((1, GMLP_WIDTH))),
        compiler_params=_params(("arbitrary",)),
    )(z, dqn, dkn, dvb, dug, dvn, qg, kg, lg, cosf, sinf)


def _band_valid(n, s):
    shape = (GQA_GROUP * BLOCK, 3 * BLOCK)
    i = lax.broadcasted_iota(jnp.int32, shape, 0) & (BLOCK - 1)
    j = lax.broadcasted_iota(jnp.int32, shape, 1)
    k_pos = n * BLOCK - BLOCK + j
    return (jnp.abs(j - BLOCK - i) <= BLOCK) & (k_pos >= 0) & (k_pos < s)


def _group_rows(x, kh):
    return jnp.concatenate([x[:, _head(kh * GQA_GROUP + g)] for g in range(GQA_GROUP)], axis=0)


def _group_sinks(sink_ref, kh):
    return jnp.concatenate([jnp.full((BLOCK, 1), sink_ref[kh * GQA_GROUP + g], F32) for g in range(GQA_GROUP)], axis=0)


def _rows_of(x, g):
    return x[g * BLOCK:(g + 1) * BLOCK]


def _probs(q, kb, sink_h, valid):
    sc = lax.dot_general(q, kb, (((1,), (1,)), ((), ())), preferred_element_type=F32) * (HEAD_DIM ** -0.5)
    sc = jnp.where(valid, sc, MASK_VALUE)
    m = jnp.maximum(jnp.max(sc, axis=-1, keepdims=True), sink_h)
    p = jnp.exp(sc - m)
    es = jnp.exp(sink_h - m)
    den = jnp.sum(p, axis=-1, keepdims=True) + es
    inv = 1.0 / den
    return p * inv, es * inv


def _band_specs(width, nb):
    return [pl.BlockSpec((BLOCK, width), lambda n: (jnp.maximum(n - 1, 0), 0)),
            pl.BlockSpec((BLOCK, width), lambda n: (n, 0)),
            pl.BlockSpec((BLOCK, width), lambda n: (jnp.minimum(n + 1, nb - 1), 0))]


def _blk(width):
    return pl.BlockSpec((BLOCK, width), lambda n: (n, 0))


def _whole3(shape):
    return pl.BlockSpec(shape, lambda n: (0, 0, 0))


def _smem():
    return pl.BlockSpec(memory_space=pltpu.SMEM)


def _mixer_fwd(qn, kn, vb, ug, vn, wsb, bsb, sink, ga, gs, name):
    s = qn.shape[0]
    nb = s // BLOCK

    def body(sink_ref, q_ref, kp_ref, kc_ref, kx_ref, vp_ref, vc_ref, vx_ref, ug_ref, vn_ref, ws_ref, bs_ref, ga_ref, gs_ref,
             attn_ref, sgu_ref, mix_ref):
        n = pl.program_id(0)
        valid = _band_valid(n, s)
        ssq = jnp.zeros((BLOCK, 1), F32)
        for kh in range(N_KV_HEADS):
            kb = jnp.concatenate([kp_ref[:, _head(kh)], kc_ref[:, _head(kh)], kx_ref[:, _head(kh)]], axis=0)
            vbd = jnp.concatenate([vp_ref[:, _head(kh)], vc_ref[:, _head(kh)], vx_ref[:, _head(kh)]], axis=0)
            p, _ = _probs(_group_rows(q_ref, kh), kb, _group_sinks(sink_ref, kh), valid)
            o4 = jnp.dot(p.astype(BF16), vbd, preferred_element_type=F32)
            for g in range(GQA_GROUP):
                o = _rows_of(o4, g)
                attn_ref[:, _head(kh * GQA_GROUP + g)] = o
                ssq = ssq + jnp.sum(o * o, axis=-1, keepdims=True)
        r = lax.rsqrt(ssq * (1.0 / ATTN_WIDTH) + EPS)
        mix_ref[:, 0:ATTN_WIDTH] = (attn_ref[...] * r * ga_ref[...]).astype(BF16)
        ssq = jnp.zeros((BLOCK, 1), F32)
        for h in range(N_GMLP_HEADS):
            f = jnp.dot(ws_ref[h], vn_ref[:, _head(h)], preferred_element_type=F32) + bs_ref[h]
            o = ug_ref[:, _head(h)] * f
            sgu_ref[:, _head(h)] = o
            ssq = ssq + jnp.sum(o * o, axis=-1, keepdims=True)
        r = lax.rsqrt(ssq * (1.0 / GMLP_WIDTH) + EPS)
        mix_ref[:, ATTN_WIDTH:D_MODEL] = (sgu_ref[...] * r * gs_ref[...]).astype(BF16)

    hh = (N_GMLP_HEADS, BLOCK, BLOCK)
    return _ordered_call(
        body, name=name,
        out_shape=(jax.ShapeDtypeStruct((s, ATTN_WIDTH), F32), jax.ShapeDtypeStruct((s, GMLP_WIDTH), F32),
                   jax.ShapeDtypeStruct((s, D_MODEL), BF16)),
        grid=(nb,),
        in_specs=[_smem(), _blk(ATTN_WIDTH)] + _band_specs(KV_WIDTH, nb) + _band_specs(KV_WIDTH, nb)
        + [_blk(GMLP_WIDTH), _blk(GMLP_WIDTH), _whole3(hh), _whole3(hh),
           pl.BlockSpec((1, ATTN_WIDTH), lambda n: (0, 0)), pl.BlockSpec((1, GMLP_WIDTH), lambda n: (0, 0))],
        out_specs=(_blk(ATTN_WIDTH), _blk(GMLP_WIDTH), _blk(D_MODEL)),
        compiler_params=_params(("parallel",)),
    )(sink, qn, kn, kn, kn, vb, vb, vb, ug, vn, wsb, bsb, ga, gs)


def _mixer_bwd(qn, kn, vb, ug, vn, attn, sgu, dmixed, wsb, bsb, sink, ga, gs, name):
    s = qn.shape[0]
    nb = s // BLOCK
    tn_dims = (((0,), (0,)), ((), ()))
    nt_dims = (((1,), (1,)), ((), ()))

    def body(sink_ref, q_ref, kp_ref, kc_ref, kx_ref, vp_ref, vc_ref, vx_ref, ug_ref, vn_ref, attn_ref, sgu_ref, dm_ref,
             ws_ref, bs_ref, ga_ref, gs_ref,
             dq_ref, dk_ref, dv_ref, dug_ref, dvn_ref, dws_ref, dbs_ref, dsk_ref, dga_ref, dgs_ref, dk_acc, dv_acc):
        n = pl.program_id(0)

        @pl.when(n == 0)
        def _():
            for ref in (dk_acc, dv_acc, dws_ref, dbs_ref, dsk_ref, dga_ref, dgs_ref):
                ref[...] = jnp.zeros_like(ref)

        def out_norm_bwd(o, g, dy):
            r = lax.rsqrt(_mean_last(o * o) + EPS)
            gdy = dy * g
            return r * gdy - o * ((r * r * r) * _mean_last(o * gdy)), _sum_rows(o * r * dy)

        d_attn, dga = out_norm_bwd(attn_ref[...], ga_ref[...], dm_ref[:, 0:ATTN_WIDTH])
        dga_ref[...] += dga
        d_sgu, dgs = out_norm_bwd(sgu_ref[...], gs_ref[...], dm_ref[:, ATTN_WIDTH:D_MODEL])
        dgs_ref[...] += dgs

        for h in range(N_GMLP_HEADS):
            vn_h = vn_ref[:, _head(h)]
            f = jnp.dot(ws_ref[h], vn_h, preferred_element_type=F32) + bs_ref[h]
            ds_h = d_sgu[:, _head(h)]
            dug_ref[:, _head(h)] = ds_h * f
            df = ds_h * ug_ref[:, _head(h)]
            dfb = df.astype(BF16)
            dvn_ref[:, _head(h)] = lax.dot_general(ws_ref[h], dfb, tn_dims, preferred_element_type=F32)
            dws_ref[h] += lax.dot_general(dfb, vn_h, nt_dims, preferred_element_type=F32)
            dbs_ref[h] += jnp.broadcast_to(jnp.sum(df, axis=-1, keepdims=True), (BLOCK, BLOCK))

        valid = _band_valid(n, s)
        row0 = pl.multiple_of(n * BLOCK, BLOCK)
        for kh in range(N_KV_HEADS):
            kb = jnp.concatenate([kp_ref[:, _head(kh)], kc_ref[:, _head(kh)], kx_ref[:, _head(kh)]], axis=0)
            vbd = jnp.concatenate([vp_ref[:, _head(kh)], vc_ref[:, _head(kh)], vx_ref[:, _head(kh)]], axis=0)
            q4 = _group_rows(q_ref, kh)
            p, p_sink = _probs(q4, kb, _group_sinks(sink_ref, kh), valid)
            do4 = _group_rows(d_attn, kh).astype(BF16)
            dp = lax.dot_general(do4, vbd, nt_dims, preferred_element_type=F32)
            delta = jnp.sum(p * dp, axis=-1, keepdims=True)
            dsc = (p * (dp - delta) * (HEAD_DIM ** -0.5)).astype(BF16)
            d_sink = -(p_sink * delta)
            dq4 = jnp.dot(dsc, kb, preferred_element_type=F32)
            for g in range(GQA_GROUP):
                h = kh * GQA_GROUP + g
                dsk_ref[h:h + 1, :] += jnp.broadcast_to(_sum_all(_rows_of(d_sink, g)), (1, BLOCK))
                dq_ref[:, _head(h)] = _rows_of(dq4, g)
            dk_acc[pl.ds(row0, 3 * BLOCK), _head(kh)] += lax.dot_general(dsc, q4, tn_dims, preferred_element_type=F32)
            dv_acc[pl.ds(row0, 3 * BLOCK), _head(kh)] += lax.dot_general(p.astype(BF16), do4, tn_dims,
                                                                         preferred_element_type=F32)

        @pl.when(n == nb - 1)
        def _():
            dk_ref[...] = dk_acc[BLOCK:BLOCK + s, :]
            dv_ref[...] = dv_acc[BLOCK:BLOCK + s, :]

    hh = (N_GMLP_HEADS, BLOCK, BLOCK)
    full_kv = pl.BlockSpec((s, KV_WIDTH), lambda n: (0, 0))
    return _ordered_call(
        body, name=name,
        out_shape=(jax.ShapeDtypeStruct((s, ATTN_WIDTH), F32), jax.ShapeDtypeStruct((s, KV_WIDTH), F32),
                   jax.ShapeDtypeStruct((s, KV_WIDTH), F32), jax.ShapeDtypeStruct((s, GMLP_WIDTH), F32),
                   jax.ShapeDtypeStruct((s, GMLP_WIDTH), F32), jax.ShapeDtypeStruct(hh, F32), jax.ShapeDtypeStruct(hh, F32),
                   jax.ShapeDtypeStruct((N_Q_HEADS, BLOCK), F32), jax.ShapeDtypeStruct((1, ATTN_WIDTH), F32),
                   jax.ShapeDtypeStruct((1, GMLP_WIDTH), F32)),
        grid=(nb,),
        in_specs=[_smem(), _blk(ATTN_WIDTH)] + _band_specs(KV_WIDTH, nb) + _band_specs(KV_WIDTH, nb)
        + [_blk(GMLP_WIDTH), _blk(GMLP_WIDTH), _blk(ATTN_WIDTH), _blk(GMLP_WIDTH), _blk(D_MODEL), _whole3(hh), _whole3(hh),
           pl.BlockSpec((1, ATTN_WIDTH), lambda n: (0, 0)), pl.BlockSpec((1, GMLP_WIDTH), lambda n: (0, 0))],
        out_specs=(_blk(ATTN_WIDTH), full_kv, full_kv, _blk(GMLP_WIDTH), _blk(GMLP_WIDTH), _whole3(hh), _whole3(hh),
                   pl.BlockSpec((N_Q_HEADS, BLOCK), lambda n: (0, 0)), pl.BlockSpec((1, ATTN_WIDTH), lambda n: (0, 0)),
                   pl.BlockSpec((1, GMLP_WIDTH), lambda n: (0, 0))),
        scratch_shapes=[pltpu.VMEM((s + 2 * BLOCK, KV_WIDTH), F32), pltpu.VMEM((s + 2 * BLOCK, KV_WIDTH), F32)],
        compiler_params=_params(("arbitrary",)),
    )(sink, qn, kn, kn, kn, vb, vb, vb, ug, vn, attn, sgu, dmixed, wsb, bsb, ga, gs)


CONV_TILE = 128


PAD_ROWS = 8


def _zero_pad_rows(pad_ref):
    s = pad_ref.shape[0] - 2 * PAD_ROWS
    zeros = jnp.zeros((PAD_ROWS, pad_ref.shape[1]), F32)
    pad_ref[0:PAD_ROWS, :] = zeros
    pad_ref[PAD_ROWS + s:2 * PAD_ROWS + s, :] = zeros


def _shift_rows(a, pad_ref):
    s = a.shape[0]
    pad_ref[PAD_ROWS:PAD_ROWS + s, :] = a
    padded = pad_ref[...]
    prev = pltpu.roll(padded, 1, 0)[PAD_ROWS:PAD_ROWS + s]
    nxt = pltpu.roll(padded, s + 2 * PAD_ROWS - 1, 0)[PAD_ROWS:PAD_ROWS + s]
    return prev, nxt


def _conv_specs(s):
    tc = CONV_TILE
    nj = D_FF // tc
    return (tc, nj, pl.BlockSpec((2, s, tc), lambda j: (0, 0, j)),
            [pl.BlockSpec((3, tc), lambda j: (0, j)), pl.BlockSpec((3, tc), lambda j: (0, j + nj))],
            [pl.BlockSpec((1, tc), lambda j: (0, j)), pl.BlockSpec((1, tc), lambda j: (0, j + nj))])


def _conv_gate_fwd(a_pre, cw, cb, name):
    s = a_pre.shape[1]
    tc, nj, a_spec, w_specs, b_specs = _conv_specs(s)

    def body(a_ref, wg_ref, wu_ref, bg_ref, bu_ref, act_ref, dgu_ref, pad_ref):
        _zero_pad_rows(pad_ref)

        def conv(a, w_ref, b_ref):
            prev, nxt = _shift_rows(a, pad_ref)
            return b_ref[...] + prev * w_ref[0:1, :] + a * w_ref[1:2, :] + nxt * w_ref[2:3, :]

        g = conv(a_ref[0].astype(F32), wg_ref, bg_ref)
        u = conv(a_ref[1].astype(F32), wu_ref, bu_ref)
        sg = 1.0 / (1.0 + jnp.exp(-g))
        silu = g * sg
        act_ref[...] = (silu * u).astype(BF16)
        dgu_ref[0] = (u * (sg * (1.0 + g * (1.0 - sg)))).astype(BF16)
        dgu_ref[1] = silu.astype(BF16)

    return _ordered_call(
        body, name=name, out_shape=(jax.ShapeDtypeStruct((s, D_FF), BF16), jax.ShapeDtypeStruct((2, s, D_FF), BF16)),
        grid=(nj,), in_specs=[a_spec] + w_specs + b_specs,
        out_specs=(pl.BlockSpec((s, tc), lambda j: (0, j)), pl.BlockSpec((2, s, tc), lambda j: (0, 0, j))),
        scratch_shapes=[pltpu.VMEM((s + 2 * PAD_ROWS, tc), F32)], compiler_params=_params(("parallel",)),
    )(a_pre, cw, cw, cb, cb)


def _conv_gate_bwd(a_pre, dgu, cw, dact, name):
    s = a_pre.shape[1]
    tc, nj, a_spec, w_specs, _ = _conv_specs(s)

    def body(a_ref, dgu_ref, wg_ref, wu_ref, dact_ref, dap_ref, dcw_ref, dcb_ref, pad_ref):
        _zero_pad_rows(pad_ref)
        dact_v = dact_ref[...].astype(F32)
        for part, w_ref in enumerate((wg_ref, wu_ref)):
            da = dact_v * dgu_ref[part].astype(F32)
            a = a_ref[part].astype(F32)
            prev, nxt = _shift_rows(a, pad_ref)
            dcw_ref[part, 0:1, :] = _sum_rows(prev * da)
            dcw_ref[part, 1:2, :] = _sum_rows(a * da)
            dcw_ref[part, 2:3, :] = _sum_rows(nxt * da)
            dcb_ref[part] = _sum_rows(da)
            da_prev, da_next = _shift_rows(da, pad_ref)
            dap_ref[part] = (da_next * w_ref[0:1, :] + da * w_ref[1:2, :] + da_prev * w_ref[2:3, :]).astype(BF16)

    return _ordered_call(
        body, name=name,
        out_shape=(jax.ShapeDtypeStruct((2, s, D_FF), BF16), jax.ShapeDtypeStruct((2, 3, D_FF), F32),
                   jax.ShapeDtypeStruct((2, 1, D_FF), F32)),
        grid=(nj,),
        in_specs=[a_spec, pl.BlockSpec((2, s, tc), lambda j: (0, 0, j))] + w_specs + [pl.BlockSpec((s, tc), lambda j: (0, j))],
        out_specs=(pl.BlockSpec((2, s, tc), lambda j: (0, 0, j)), pl.BlockSpec((2, 3, tc), lambda j: (0, 0, j)),
                   pl.BlockSpec((2, 1, tc), lambda j: (0, 0, j))),
        scratch_shapes=[pltpu.VMEM((s + 2 * PAD_ROWS, tc), F32)], compiler_params=_params(("parallel",)),
    )(a_pre, dgu, cw, cw, dact)


def _loss_head(y, target, name):
    s, d = y.shape
    tr = _row_tile(s)

    def body(y_ref, t_ref, loss_ref, dy_ref, dyb_ref):
        err = y_ref[...] - t_ref[...]

        @pl.when(pl.program_id(0) == 0)
        def _():
            loss_ref[...] = jnp.zeros_like(loss_ref)

        loss_ref[...] += jnp.broadcast_to(0.5 * _sum_all(_mean_last(err * err)), (8, 128))
        dy = err * (1.0 / d)
        dy_ref[...] = dy
        dyb_ref[...] = dy.astype(BF16)

    return _ordered_call(
        body, name=name,
        out_shape=(jax.ShapeDtypeStruct((8, 128), F32), jax.ShapeDtypeStruct((s, d), F32), jax.ShapeDtypeStruct((s, d), BF16)),
        grid=(s // tr,), in_specs=[_rows(d, tr), _rows(d, tr)],
        out_specs=(_const2((8, 128)), _rows(d, tr), _rows(d, tr)), compiler_params=_params(("arbitrary",)),
    )(y, target)


def _row_block(rows, cols, budget=1 << 20):
    if rows * cols <= budget:
        return rows
    best = None
    for tr in range(16, rows, 16):
        if rows % tr == 0 and tr * cols <= budget:
            best = tr
    assert best is not None, (rows, cols)
    return best


def _place_shard(x4, layer, j_arr, out_dtype, name):
    _, nh, r, cols = x4.shape
    tr = _row_block(r, cols)

    def body(j_ref, x_ref, o_ref):
        o_ref[...] = x_ref[...].astype(out_dtype)

    grid_spec = pltpu.PrefetchScalarGridSpec(
        num_scalar_prefetch=1, grid=(nh, r // tr),
        in_specs=[pl.BlockSpec((None, None, tr, cols), lambda h, i, j_ref: (layer, h, i, 0))],
        out_specs=pl.BlockSpec((None, None, tr, cols), lambda h, i, j_ref: (j_ref[0], h, i, 0)))
    return _ordered_call(
        body, name=name, out_shape=jax.ShapeDtypeStruct((N_CHIPS, nh, r, cols), out_dtype), grid_spec=grid_spec,
        compiler_params=_params(("parallel", "parallel")),
    )(j_arr, x4)


def _adamw(w, g, m, v, name, budget=1 << 18):
    rows, cols = w.shape
    tr = _row_block(rows, cols, budget)

    def body(w_ref, g_ref, m_ref, v_ref, go_ref, d_ref, nm_ref, nv_ref):
        gv = g_ref[...]
        go_ref[...] = gv
        mn = ADAM_B1 * m_ref[...] + (1.0 - ADAM_B1) * gv
        vn = ADAM_B2 * v_ref[...] + (1.0 - ADAM_B2) * (gv * gv)
        m_hat = mn / (1.0 - ADAM_B1 ** ADAM_STEP)
        v_hat = vn / (1.0 - ADAM_B2 ** ADAM_STEP)
        d_ref[...] = -ADAM_LR * (m_hat / (jnp.sqrt(v_hat) + ADAM_EPS) + ADAM_WD * w_ref[...])
        nm_ref[...] = mn
        nv_ref[...] = vn

    sds = jax.ShapeDtypeStruct((rows, cols), F32)
    return _ordered_call(
        body, name=name, out_shape=(sds, sds, sds, sds), grid=(rows // tr,),
        in_specs=[_rows(cols, tr)] * 4, out_specs=(_rows(cols, tr),) * 4, compiler_params=_params(("parallel",)),
    )(w, g, m, v)


def _chip_sum(p4, recv3, j_arr, c_arr, name):
    _, rh, cols = p4.shape
    tr = _row_block(rh, cols, 1 << 19)

    def body(j_ref, c_ref, p_ref, r_ref, o_ref):
        total = p_ref[...].astype(F32)
        for peer in range(3):
            total = total + r_ref[peer].astype(F32)
        o_ref[...] = total

    grid_spec = pltpu.PrefetchScalarGridSpec(
        num_scalar_prefetch=2, grid=(rh // tr,),
        in_specs=[pl.BlockSpec((None, tr, cols), lambda i, j_ref, c_ref: (j_ref[0], i, 0)),
                  pl.BlockSpec((3, tr, cols), lambda i, j_ref, c_ref: (0, i, 0))],
        out_specs=pl.BlockSpec((None, tr, cols), lambda i, j_ref, c_ref: (c_ref[0], i, 0)))
    return _ordered_call(
        body, name=name, out_shape=jax.ShapeDtypeStruct((2, rh, cols), F32), grid_spec=grid_spec,
        compiler_params=_params(("parallel",)),
    )(j_arr, c_arr, p4, recv3)


def _adamw_layer(w, g, m, v, layer, into, name):
    nl, rows, cols = w.shape
    slabs, _, width = g.shape
    assert slabs * width == cols and g.shape[1] == rows, (name, w.shape, g.shape)
    tr = _row_block(rows, width, 1 << 18)
    at_layer = pl.BlockSpec((None, tr, width), lambda h, i: (layer, i, h))

    def body(w_ref, g_ref, m_ref, v_ref, *rest):
        go_ref, d_ref, nm_ref, nv_ref = rest[-4:]
        gv = g_ref[...]
        go_ref[...] = gv
        mn = ADAM_B1 * m_ref[...] + (1.0 - ADAM_B1) * gv
        vn = ADAM_B2 * v_ref[...] + (1.0 - ADAM_B2) * (gv * gv)
        m_hat = mn / (1.0 - ADAM_B1 ** ADAM_STEP)
        v_hat = vn / (1.0 - ADAM_B2 ** ADAM_STEP)
        d_ref[...] = -ADAM_LR * (m_hat / (jnp.sqrt(v_hat) + ADAM_EPS) + ADAM_WD * w_ref[...])
        nm_ref[...] = mn
        nv_ref[...] = vn

    in_specs = [at_layer, pl.BlockSpec((None, tr, width), lambda h, i: (h, i, 0)), at_layer, at_layer]
    operands = [w, g, m, v]
    aliases = {}
    if into is not None:
        in_specs += [ANY] * 4
        operands += list(into)
        aliases = {4 + i: i for i in range(4)}
    sds = jax.ShapeDtypeStruct((nl, rows, cols), F32)
    return _ordered_call(
        body, name=name, out_shape=(sds,) * 4, grid=(slabs, rows // tr), in_specs=in_specs, out_specs=(at_layer,) * 4,
        input_output_aliases=aliases, compiler_params=_params(("parallel", "parallel")),
    )(*operands)


def _sum_devices(mine, landed, me_arr, name):
    rows, lanes = mine.shape

    def body(me_ref, mine_ref, landed_ref, o_ref):
        total = None
        for dev in range(8):
            part = jnp.where(me_ref[0] == dev, mine_ref[...], landed_ref[dev])
            total = part if total is None else total + part
        o_ref[...] = total

    grid_spec = pltpu.PrefetchScalarGridSpec(
        num_scalar_prefetch=1, grid=(1,),
        in_specs=[pl.BlockSpec((rows, lanes), lambda i, me_ref: (0, 0)), pl.BlockSpec((8, rows, lanes), lambda i, me_ref: (0, 0, 0))],
        out_specs=pl.BlockSpec((rows, lanes), lambda i, me_ref: (0, 0)))
    return _ordered_call(
        body, name=name, out_shape=jax.ShapeDtypeStruct((rows, lanes), F32), grid_spec=grid_spec,
        compiler_params=_params(("arbitrary",)),
    )(me_arr, mine, landed)


def _place():
    x, y, c = lax.axis_index("x"), lax.axis_index("y"), lax.axis_index("c")
    chips = [(1 - x, y), (x, 1 - y), (1 - x, 1 - y)]
    return x, y, c, chips


HBM = pl.BlockSpec(memory_space=pltpu.HBM)
SEM = pl.BlockSpec(memory_space=pltpu.SEMAPHORE)
TOKEN = jax.ShapeDtypeStruct((8, 128), F32)


def _remote(src, dst, send_sem, recv_sem, to):
    return pltpu.make_async_remote_copy(src_ref=src, dst_ref=dst, send_sem=send_sem, recv_sem=recv_sem, device_id=to,
                                        device_id_type=MESH)


def _split_call(body, name, thru, sems_in=(), fresh=(), new_sems=(), after_last=True):
    n_t, n_s, n_f = len(thru), len(sems_in), len(fresh)

    def call_body(*refs):
        outs = refs[n_t + n_s:]
        body(refs[:n_t], refs[n_t:n_t + n_s], outs[1 + n_t:1 + n_t + n_f], outs[1 + n_t + n_f:])
        outs[0][...] = jnp.zeros_like(outs[0])

    out_shape = ([TOKEN] + [pltpu.HBM(t.shape, t.dtype) for t in thru] + [pltpu.HBM(shp, dt) for shp, dt in fresh]
                 + [pltpu.SemaphoreType.DMA(shp) for shp in new_sems])
    out_specs = [pl.BlockSpec(memory_space=pltpu.VMEM)] + [HBM] * (n_t + n_f) + [SEM] * len(new_sems)
    if not after_last:
        _Order.last = None
    out = _ordered_call(
        call_body, name=name, out_shape=tuple(out_shape), in_specs=[HBM] * n_t + [SEM] * n_s, out_specs=tuple(out_specs),
        input_output_aliases={i: 1 + i for i in range(n_t)},
        compiler_params=pltpu.CompilerParams(has_side_effects=pltpu.SideEffectType.DATAFLOW_SIDE_EFFECTING),
    )(*[pltpu.with_memory_space_constraint(t, pltpu.HBM) for t in thru], *sems_in)
    return out[1:1 + n_t], out[1 + n_t:1 + n_t + n_f], out[1 + n_t + n_f:]


class _Exchange:
    def __init__(self, weights, m_in, v_in, j_arr, c_arr, me_arr):
        self.w, self.m, self.v = weights, m_in, v_in
        self.j_arr, self.c_arr, self.me_arr = j_arr, c_arr, me_arr
        self.adam, self.small, self.pairs = {}, {}, {}
        self.o_arr = 1 - c_arr
        self.groups = [(l, name) for l in range(DEPTH) for name in BIG_NAMES]
        self.shard_shape = {name: weights[name].shape[1:] for name in BIG_NAMES}
        self.conv_state, self.state = [], {}
        self.ready, self.conv_ready = {}, {}
        self.pending, self.tick, self.reduced = [], 0, {}

        def place(grp):
            l, name = grp
            nl, r, cols = weights[name].shape
            return _place_shard(weights[name].reshape(nl, 2, r // 2, cols), l, j_arr, BF16, f"place_{name}_l{l}")

        def start_copies(tag, convs, groups, bufs):
            n_c = len(convs)

            def start(thru, _, __, sems):
                x, y, c, chips = _place()
                j_me = 2 * x + y
                copies = []
                for i in range(len(thru)):
                    mine = thru[i].at[j_me] if i < n_c else thru[i].at[j_me, c]
                    copies += [_remote(mine, mine, sems[2 * i].at[k], sems[2 * i + 1].at[k], (*chip, c))
                               for k, chip in enumerate(chips)]
                for cp in copies:
                    cp.start()

            thru, _, sems = _split_call(start, tag, convs + bufs, new_sems=[(3,)] * (2 * (n_c + len(bufs))))
            self.conv_state += [(thru[i], sems[2 * i], sems[2 * i + 1]) for i in range(n_c)]
            for g, grp in enumerate(groups):
                self.state[grp] = (thru[n_c + g], sems[2 * (n_c + g)], sems[2 * (n_c + g) + 1])

        convs = [_place_shard(weights["conv_w"][:, None], l, j_arr, F32, f"place_conv_w_l{l}") for l in range(DEPTH)]
        start_copies("gather_start_first", convs, self.groups[:1], [place(self.groups[0])])
        start_copies("gather_start_rest", [], self.groups[1:], [place(grp) for grp in self.groups[1:]])

    def conv_w(self, l):
        if l not in self.conv_ready:
            buf, send, recv = self.conv_state[l]

            def wait(thru, sems, _, __):
                x, y, c, chips = _place()
                for k, chip in enumerate(chips):
                    mine, theirs = thru[0].at[2 * x + y], thru[0].at[2 * chip[0] + chip[1]]
                    _remote(mine, mine, sems[0].at[k], sems[1].at[k], (*chip, c)).wait_send()
                    _remote(theirs, theirs, sems[0].at[k], sems[1].at[k], (x, y, c)).wait_recv()

            (buf,), _, _ = _split_call(wait, f"gather_conv_w_l{l}", [buf], sems_in=[send, recv])
            self.conv_ready[l] = jnp.transpose(buf[:, 0], (1, 0, 2)).reshape(3, 2 * D_FF)
        return self.conv_ready[l]

    def weight(self, l, name):
        grp = (l, name)
        if grp not in self.ready:
            buf, send, recv = self.state[grp]

            def forward(thru, sems, _, new):
                x, y, c, chips = _place()
                for k, chip in enumerate(chips):
                    landed = thru[0].at[2 * chip[0] + chip[1], c]
                    _remote(landed, landed, new[0].at[k], sems[0].at[k], (x, y, c)).wait_recv()
                    _remote(landed, landed, new[0].at[k], new[1].at[k], (x, y, 1 - c)).start()

            (buf,), _, (fsend, frecv) = _split_call(forward, f"gather_pass_{name}_l{l}", [buf], sems_in=[recv],
                                                    new_sems=[(3,), (3,)])

            def finish(thru, sems, _, __):
                x, y, c, chips = _place()
                mine = thru[0].at[2 * x + y, c]
                for k, chip in enumerate(chips):
                    j_k = 2 * chip[0] + chip[1]
                    theirs, landed = thru[0].at[j_k, 1 - c], thru[0].at[j_k, c]
                    _remote(theirs, theirs, sems[1].at[k], sems[2].at[k], (x, y, c)).wait_recv()
                    _remote(landed, landed, sems[1].at[k], sems[2].at[k], (x, y, 1 - c)).wait_send()
                    _remote(mine, mine, sems[0].at[k], sems[2].at[k], (*chip, c)).wait_send()

            (buf,), _, _ = _split_call(finish, f"gather_done_{name}_l{l}", [buf], sems_in=[send, fsend, frecv])
            r, cols = self.shard_shape[name]
            self.ready[grp] = buf.reshape(N_CHIPS, r, cols) if name in ("w_in", "w_up") else buf.reshape(N_CHIPS * r, cols)
        return self.ready[grp]

    def pair_send(self, l, name, other):
        def start(thru, _, fresh, sems):
            x, y, c, _chips = _place()
            _remote(thru[0], fresh[0], sems[0], sems[1], (x, y, 1 - c)).start()

        (other,), (recv,), sems = _split_call(start, f"pair_start_{name}_l{l}", [other], fresh=[(other.shape, BF16)],
                                             new_sems=[(), ()], after_last=False)
        self.pairs[(l, name)] = (other, recv, sems)

    def pair_recv(self, l, name):
        other, recv, sems = self.pairs.pop((l, name))

        def wait(thru, sems, _, __):
            x, y, c, _chips = _place()
            cp = _remote(thru[0], thru[1], sems[0], sems[1], (x, y, 1 - c))
            cp.wait_send()
            cp.wait_recv()

        (_, recv), _, _ = _split_call(wait, f"pair_done_{name}_l{l}", [other, recv], sems_in=list(sems))
        return recv

    def scatter(self, l, name, p4):
        def start(thru, _, fresh, sems):
            x, y, c, chips = _place()
            for k, chip in enumerate(chips):
                _remote(thru[0].at[2 * chip[0] + chip[1]], fresh[0].at[k], sems[0].at[k], sems[1].at[k], (*chip, c)).start()

        (p4,), (recv3,), sems = _split_call(start, f"chips_start_{name}_l{l}", [p4], fresh=[((3,) + p4.shape[1:], BF16)],
                                           new_sems=[(3,), (3,)], after_last=False)
        self.pending.append(dict(l=l, name=name, stage=2, at=self.tick, bufs=(p4, recv3), sems=sems))

    def _chips(self, grp):
        l, name = grp["l"], grp["name"]

        def wait(thru, sems, _, __):
            x, y, c, chips = _place()
            for k, chip in enumerate(chips):
                cp = _remote(thru[0].at[2 * chip[0] + chip[1]], thru[1].at[k], sems[0].at[k], sems[1].at[k], (*chip, c))
                cp.wait_send()
                cp.wait_recv()

        (p4, recv3), _, _ = _split_call(wait, f"chips_done_{name}_l{l}", list(grp["bufs"]), sems_in=list(grp["sems"]))
        half = _chip_sum(p4, recv3, self.j_arr, self.c_arr, f"chip_sum_{name}_l{l}")

        def start(thru, _, __, sems):
            x, y, c, _chips = _place()
            _remote(thru[0].at[c], thru[0].at[c], sems[0], sems[1], (x, y, 1 - c)).start()

        (half,), _, sems = _split_call(start, f"join_start_{name}_l{l}", [half], new_sems=[(), ()], after_last=False)
        grp.update(stage=3, at=self.tick, bufs=(half,), sems=sems)

    def _update(self, grp):
        l, name = grp["l"], grp["name"]

        def wait(thru, sems, _, __):
            x, y, c, _chips = _place()
            _remote(thru[0].at[c], thru[0].at[c], sems[0], sems[1], (x, y, 1 - c)).wait_send()
            _remote(thru[0].at[1 - c], thru[0].at[1 - c], sems[0], sems[1], (x, y, c)).wait_recv()

        (full,), _, _ = _split_call(wait, f"join_done_{name}_l{l}", list(grp["bufs"]), sems_in=list(grp["sems"]))
        if GRAD_HALVES[name][0] != "cols_of_block":
            full = full.reshape((1,) + tuple(self.shard_shape[name]))
        self.adam[name] = _adamw_layer(self.w[name], full, self.m[name], self.v[name], l, self.adam.get(name),
                                       f"adamw_{name}_l{l}")
        grp.update(stage=4)

    def point(self, drain=False):
        self.tick += 1
        for grp in self.pending:
            if grp["stage"] == 3 and (drain or grp["at"] < self.tick):
                self._update(grp)
            elif grp["stage"] == 2 and (drain or grp["at"] + 3 <= self.tick):
                self._chips(grp)

    def finish(self):
        while any(grp["stage"] < 4 for grp in self.pending):
            self.point(drain=True)
        return self.adam

    @staticmethod
    def _peer(k, x, y, c):
        return (1 - x if k & 4 else x, 1 - y if k & 2 else y, 1 - c if k & 1 else c)

    def small_grads(self, l, grads, loss_tile):
        parts = [grads[nm] for nm in SMALL_NAMES] + ([loss_tile[0, 0:1]] if loss_tile is not None else [])
        packed = _pack_call(parts, f"small_pack_l{l}")
        rows = packed.shape[0]

        def start(thru, _, fresh, sems):
            x, y, c, _chips = _place()
            for k in range(1, 8):
                _remote(thru[0], fresh[0].at[4 * x + 2 * y + c], sems[0].at[k - 1], sems[1].at[k - 1],
                        self._peer(k, x, y, c)).start()

        (packed,), (landed,), sems = _split_call(start, f"small_start_l{l}", [packed], fresh=[((8, rows, PACK_LANES), F32)],
                                                 new_sems=[(7,), (7,)], after_last=False)
        self.small[l] =(packed, landed, sems, [p.shape for p in parts])

    def small_sum(self, l):
        packed, landed, sems, _shapes = self.small[l]

        def wait(thru, sems, _, __):
            x, y, c, _chips = _place()
            for k in range(1, 8):
                px, py, pc = self._peer(k, x, y, c)
                _remote(thru[0], thru[1].at[4 * x + 2 * y + c], sems[0].at[k - 1], sems[1].at[k - 1], (px, py, pc)).wait_send()
                _remote(thru[0], thru[1].at[4 * px + 2 * py + pc], sems[0].at[k - 1], sems[1].at[k - 1], (x, y, c)).wait_recv()

        (packed, landed), _, _ = _split_call(wait, f"small_done_l{l}", [packed, landed], sems_in=list(sems))
        return _sum_devices(packed, landed, self.me_arr, f"small_sum_l{l}")


def _rope_tables(s):
    inv_freq = ROPE_THETA ** (-jnp.arange(0, HEAD_DIM, 2, dtype=F32) / HEAD_DIM)
    ang = jnp.arange(s, dtype=F32)[:, None] * inv_freq[None, :]
    cos, sin = jnp.cos(ang), jnp.sin(ang)
    return jnp.concatenate([cos, cos], axis=-1), jnp.concatenate([-sin, sin], axis=-1)


def _local_step(x, target, ex, small):
    s = x.shape[0]
    cosf, sinf = _rope_tables(s)
    saved = []
    for l in range(DEPTH):
        p = small[l]
        t = f"l{l}"
        h = _rms_fwd(x, p["norm1_g"], f"norm1_{t}")
        z = _matmul(h, ex.weight(l, "w_in"), mode="nn", out_dtype=BF16, tm=1024, tn=896, tk=2048, b_parts=4, name=f"proj_in_{t}")
        qn, kn, vb, ug, vn = _proj_post(z, p["q_norm_g"], p["k_norm_g"], p["sgu_ln_g"], p["sgu_ln_b"], cosf, sinf, f"proj_post_{t}")
        attn, sgu, mixed = _mixer_fwd(qn, kn, vb, ug, vn, p["w_s_bf16"], p["b_s_tile"], p["sink"], p["attn_out_g"],
                                      p["sgu_out_g"], f"mixer_{t}")
        x1 = _matmul(mixed, ex.weight(l, "w_o"), mode="nn", out_dtype=F32, tm=2048, tn=256, tk=2048, res=x,
                     name=f"proj_out_{t}")
        h2 = _rms_fwd(x1, p["norm2_g"], f"norm2_{t}")
        a_pre = _matmul(h2, ex.weight(l, "w_up"), mode="nn", out_dtype=BF16, tm=1024, tn=1408, tk=2048, b_parts=4,
                        out_parts=2,
                        name=f"ffn_up_{t}")
        act, dgu = _conv_gate_fwd(a_pre, ex.conv_w(l), p["conv_b"], f"conv_gate_{t}")
        x2 = _matmul(act, ex.weight(l, "w_down"), mode="nn", out_dtype=F32, tm=1024, tn=256, tk=D_FF, res=x1,
                     name=f"ffn_down_{t}")
        saved.append(dict(x=x, h=h, z=z, qn=qn, kn=kn, vb=vb, ug=ug, vn=vn, attn=attn, sgu=sgu, mixed=mixed, x1=x1, h2=h2,
                          a_pre=a_pre, act=act, dgu=dgu))
        x = x2
    loss_tile, dx, dxb = _loss_head(x, target, "loss_head")
    for l in reversed(range(DEPTH)):
        p, sv = small[l], saved[l]
        t = f"l{l}"
        def weight_grad(name, a, g, between, g_parts=0):
            ex.pair_send(l, name, _grad_half(name, a, g, ex.o_arr, None, f"g_{name}_other_{t}", g_parts))
            out = between()
            ex.scatter(l, name, _grad_half(name, a, g, ex.c_arr, ex.pair_recv(l, name), f"g_{name}_own_{t}", g_parts))
            ex.point()
            return out

        def after_down():
            dact = _matmul(dxb, ex.weight(l, "w_down"), mode="nt", out_dtype=BF16, tm=1024, tn=512, tk=2048,
                           name=f"d_act_{t}")
            return _conv_gate_bwd(sv["a_pre"], sv["dgu"], ex.conv_w(l), dact, f"conv_gate_bwd_{t}")

        dap, dcw, dcb = weight_grad("w_down", sv["act"], dxb, after_down)

        def after_up():
            dh2 = _matmul(dap, ex.weight(l, "w_up"), mode="nt", out_dtype=F32, tm=1024, tn=1024, tk=2816, a_parts=2,
                          b_parts=4, name=f"d_h2_{t}")
            return _rms_bwd(sv["x1"], p["norm2_g"], dh2, dx, f"norm2_bwd_{t}")

        dx1, dx1b, dg2 = weight_grad("w_up", sv["h2"], dap, after_up, g_parts=2)
        ex.pair_send(l, "w_o", _grad_half("w_o", sv["mixed"], dx1b, ex.o_arr, None, f"g_w_o_other_{t}"))
        dmixed = _matmul(dx1b, ex.weight(l, "w_o"), mode="nt", out_dtype=F32, tm=1024, tn=512, tk=2048,
                         name=f"d_mixed_{t}")
        dqn, dkn, dvb, dug, dvn, dws, dbs, dsk, dga, dgs = _mixer_bwd(
            sv["qn"], sv["kn"], sv["vb"], sv["ug"], sv["vn"], sv["attn"], sv["sgu"], dmixed, p["w_s_bf16"], p["b_s_tile"],
            p["sink"], p["attn_out_g"], p["sgu_out_g"], f"mixer_bwd_{t}")
        dz, dqg, dkg, dlg, dlb = _proj_post_bwd(sv["z"], dqn, dkn, dvb, dug, dvn, p["q_norm_g"], p["k_norm_g"], p["sgu_ln_g"],
                                                 cosf, sinf, f"proj_post_bwd_{t}")
        ex.scatter(l, "w_o", _grad_half("w_o", sv["mixed"], dx1b, ex.c_arr, ex.pair_recv(l, "w_o"), f"g_w_o_own_{t}"))
        ex.point()

        def after_in():
            dh = _matmul_nt_slabs(dz, ex.weight(l, "w_in"), tm=1024, tn=512, name=f"d_h_{t}")
            return _rms_bwd(sv["x"], p["norm1_g"], dh, dx1, f"norm1_bwd_{t}")

        dx, dxb, dg1 = weight_grad("w_in", sv["h"], dz, after_in)
        ex.small_grads(l, dict(
            norm1_g=dg1[0], q_norm_g=dqg[0], k_norm_g=dkg[0], sink=dsk[:, 0], sgu_ln_g=dlg[0], sgu_ln_b=dlb[0], w_s=dws,
            b_s=dbs[:, :, 0], attn_out_g=dga[0], sgu_out_g=dgs[0], norm2_g=dg2[0],
            conv_w=jnp.concatenate([dcw[0], dcw[1]], axis=-1), conv_b=jnp.concatenate([dcb[0, 0], dcb[1, 0]], axis=-1)),
            loss_tile if l == 0 else None)
    return dx


def _small_views(l, norm1_g, q_norm_g, k_norm_g, sink, sgu_ln_g, sgu_ln_b, w_s, b_s, attn_out_g, sgu_out_g, norm2_g, conv_b):
    return dict(
        norm1_g=norm1_g[l][None], q_norm_g=q_norm_g[l][None], k_norm_g=k_norm_g[l][None], sink=sink[l],
        sgu_ln_g=sgu_ln_g[l][None], sgu_ln_b=sgu_ln_b[l][None], w_s_bf16=w_s[l].astype(BF16),
        b_s_tile=jnp.broadcast_to(b_s[l][:, :, None], (N_GMLP_HEADS, BLOCK, BLOCK)), attn_out_g=attn_out_g[l][None],
        sgu_out_g=sgu_out_g[l][None], norm2_g=norm2_g[l][None], conv_b=conv_b[l][None])


SMALL_NAMES = ("norm1_g", "q_norm_g", "k_norm_g", "sink", "sgu_ln_g", "sgu_ln_b", "w_s", "b_s", "attn_out_g", "sgu_out_g",
               "norm2_g", "conv_b", "conv_w")
REPLICATED_NAMES = SMALL_NAMES[:-1]
BIG_NAMES = ("w_in", "w_o", "w_up", "w_down")
PACK_LANES = 128
PACK_ALIGN = 8 * PACK_LANES


def _pack_rows(shape):
    return -(-math.prod(shape) // PACK_ALIGN) * 8


def _pack_parts(arrays):
    parts = []
    for a in arrays:
        flat = a.reshape(-1)
        parts.append(jnp.pad(flat, (0, _pack_rows(a.shape) * PACK_LANES - flat.shape[0])).reshape(-1, PACK_LANES))
    return parts


def _pack(arrays):
    return jnp.concatenate(_pack_parts(arrays), axis=0)


def _pack_call(arrays, name):
    parts = _pack_parts(arrays)
    total = sum(p.shape[0] for p in parts)

    def body(*refs):
        o_ref, at = refs[-1], 0
        for p_ref in refs[:-1]:
            o_ref[at:at + p_ref.shape[0], :] = p_ref[...]
            at += p_ref.shape[0]

    vm = pl.BlockSpec(memory_space=pltpu.VMEM)
    return _ordered_call(
        body, name=name, out_shape=jax.ShapeDtypeStruct((total, PACK_LANES), F32), in_specs=[vm] * len(parts), out_specs=vm,
        compiler_params=pltpu.CompilerParams(vmem_limit_bytes=V7X_VMEM_LIMIT),
    )(*parts)


def _unpack_layers(stacked, shapes):
    nl = stacked.shape[0]
    out, at = [], 0
    for shp in shapes:
        rows = _pack_rows(shp)
        out.append(stacked[:, at:at + rows].reshape(nl, -1)[:, :math.prod(shp)].reshape((nl,) + tuple(shp)))
        at += rows
    return out


def _adamw_packed(w, g, m, v, rows, layer, into, name):
    head = pl.BlockSpec((rows, PACK_LANES), lambda i: (0, 0))
    at_layer = pl.BlockSpec((None, rows, PACK_LANES), lambda i: (layer, 0, 0))

    def body(w_ref, g_ref, m_ref, v_ref, *rest):
        d_ref, nm_ref, nv_ref = rest[-3:]
        gv = g_ref[...]
        mn = ADAM_B1 * m_ref[...] + (1.0 - ADAM_B1) * gv
        vn = ADAM_B2 * v_ref[...] + (1.0 - ADAM_B2) * (gv * gv)
        m_hat = mn / (1.0 - ADAM_B1 ** ADAM_STEP)
        v_hat = vn / (1.0 - ADAM_B2 ** ADAM_STEP)
        d_ref[...] = -ADAM_LR * (m_hat / (jnp.sqrt(v_hat) + ADAM_EPS) + ADAM_WD * w_ref[...])
        nm_ref[...] = mn
        nv_ref[...] = vn

    in_specs = [head] * 4
    operands = [w, g, m, v]
    aliases = {}
    if into is not None:
        in_specs += [ANY] * 3
        operands += list(into)
        aliases = {4 + i: i for i in range(3)}
    sds = jax.ShapeDtypeStruct((DEPTH, rows, PACK_LANES), F32)
    return _ordered_call(
        body, name=name, out_shape=(sds,) * 3, grid=(1,), in_specs=in_specs, out_specs=(at_layer,) * 3,
        input_output_aliases=aliases, compiler_params=_params(("arbitrary",)),
    )(*operands)


def kernel(x, norm1_g, w_in, q_norm_g, k_norm_g, sink, sgu_ln_g, sgu_ln_b, w_s, b_s, attn_out_g, sgu_out_g, w_o, norm2_g, w_up, conv_w, conv_b, w_down, loss_target, m_norm1_g, m_w_in, m_q_norm_g, m_k_norm_g, m_sink, m_sgu_ln_g, m_sgu_ln_b, m_w_s, m_b_s, m_attn_out_g, m_sgu_out_g, m_w_o, m_norm2_g, m_w_up, m_conv_w, m_conv_b, m_w_down, v_norm1_g, v_w_in, v_q_norm_g, v_k_norm_g, v_sink, v_sgu_ln_g, v_sgu_ln_b, v_w_s, v_b_s, v_attn_out_g, v_sgu_out_g, v_w_o, v_norm2_g, v_w_up, v_conv_w, v_conv_b, v_w_down):
    weights = dict(norm1_g=norm1_g, w_in=w_in, q_norm_g=q_norm_g, k_norm_g=k_norm_g, sink=sink, sgu_ln_g=sgu_ln_g,
                   sgu_ln_b=sgu_ln_b, w_s=w_s, b_s=b_s, attn_out_g=attn_out_g, sgu_out_g=sgu_out_g, w_o=w_o, norm2_g=norm2_g,
                   w_up=w_up, conv_w=conv_w, conv_b=conv_b, w_down=w_down)
    m_in = dict(norm1_g=m_norm1_g, w_in=m_w_in, q_norm_g=m_q_norm_g, k_norm_g=m_k_norm_g, sink=m_sink, sgu_ln_g=m_sgu_ln_g,
                sgu_ln_b=m_sgu_ln_b, w_s=m_w_s, b_s=m_b_s, attn_out_g=m_attn_out_g, sgu_out_g=m_sgu_out_g, w_o=m_w_o,
                norm2_g=m_norm2_g, w_up=m_w_up, conv_w=m_conv_w, conv_b=m_conv_b, w_down=m_w_down)
    v_in = dict(norm1_g=v_norm1_g, w_in=v_w_in, q_norm_g=v_q_norm_g, k_norm_g=v_k_norm_g, sink=v_sink, sgu_ln_g=v_sgu_ln_g,
                sgu_ln_b=v_sgu_ln_b, w_s=v_w_s, b_s=v_b_s, attn_out_g=v_attn_out_g, sgu_out_g=v_sgu_out_g, w_o=v_w_o,
                norm2_g=v_norm2_g, w_up=v_w_up, conv_w=v_conv_w, conv_b=v_conv_b, w_down=v_w_down)
    cx, cy, cc = lax.axis_index("x"), lax.axis_index("y"), lax.axis_index("c")
    j_me = 2 * cx + cy
    c_arr = jnp.reshape(cc, (1,)).astype(jnp.int32)
    j_arr = jnp.reshape(j_me, (1,)).astype(jnp.int32)

    _Order.last = None
    ex = _Exchange(weights, m_in, v_in, j_arr, c_arr, jnp.reshape(4 * cx + 2 * cy + cc, (1,)).astype(jnp.int32))
    small = [_small_views(l, norm1_g, q_norm_g, k_norm_g, sink, sgu_ln_g, sgu_ln_b, w_s, b_s, attn_out_g, sgu_out_g, norm2_g,
                          conv_b) for l in range(DEPTH)]
    packed_in = [[_pack_call([src[nm][l] for nm in REPLICATED_NAMES], f"pack_{tag}_l{l}")
                  for tag, src in (("w", weights), ("m", m_in), ("v", v_in))] for l in range(DEPTH)]
    dx = _local_step(x[0], loss_target[0], ex, small)
    big_out = ex.finish()

    rep_shapes = [weights[nm].shape[1:] for nm in REPLICATED_NAMES]
    rep_rows = sum(_pack_rows(shp) for shp in rep_shapes)
    cw_shape = (3, 2 * D_FF)
    sums, adam_small = [None] * DEPTH, None
    for l in reversed(range(DEPTH)):
        sums[l] = ex.small_sum(l)
        pw, pm, pv = packed_in[l]
        adam_small = _adamw_packed(pw, sums[l], pm, pv, rep_rows, l, adam_small, f"adamw_small_l{l}")
    cw_rows = _pack_rows(cw_shape)
    loss = sums[0][rep_rows + cw_rows, 0]
    stacked = jnp.stack([sm[:rep_rows + cw_rows] for sm in sums])
    grads = dict(zip(REPLICATED_NAMES, _unpack_layers(stacked[:, :rep_rows], rep_shapes)))
    delta, new_m, new_v = (dict(zip(REPLICATED_NAMES, _unpack_layers(arr, rep_shapes))) for arr in adam_small)
    cw_cols = 2 * D_FF // N_CHIPS
    cw_grad = lax.dynamic_slice_in_dim(_unpack_layers(stacked[:, rep_rows:], [cw_shape])[0], j_me * cw_cols, cw_cols, axis=2)
    flat = lambda a: a.reshape(DEPTH * 3, cw_cols)
    cw_out = _adamw(flat(conv_w), flat(cw_grad), flat(m_conv_w), flat(v_conv_w), "adamw_conv_w")
    grads["conv_w"], delta["conv_w"], new_m["conv_w"], new_v["conv_w"] = (a.reshape(DEPTH, 3, cw_cols) for a in cw_out)

    for name in BIG_NAMES:
        grads[name], delta[name], new_m[name], new_v[name] = big_out[name]

    order = ("norm1_g", "w_in", "q_norm_g", "k_norm_g", "sink", "sgu_ln_g", "sgu_ln_b", "w_s", "b_s", "attn_out_g", "sgu_out_g",
             "w_o", "norm2_g", "w_up", "conv_w", "conv_b", "w_down")
    return (loss, dx[None], *[grads[nm] for nm in order], *[delta[nm] for nm in order], *[new_m[nm] for nm in order],
            *[new_v[nm] for nm in order])
```

```python
import math

import jax
import jax.numpy as jnp
from jax import lax
from jax.experimental import pallas as pl
from jax.experimental.pallas import tpu as pltpu

F32 = jnp.float32
BF16 = jnp.bfloat16

D_MODEL = 2048
HEAD_DIM = 128
ATTN_WIDTH = 1024
N_Q_HEADS = 8
N_KV_HEADS = 2
GQA_GROUP = 4
KV_WIDTH = 256
GMLP_WIDTH = 1024
N_GMLP_HEADS = 8
BLOCK = 128
IN_WIDTH = 3584
D_FF = 5632
DEPTH = 2
EPS = 1e-6
MASK_VALUE = -1e30
ROPE_THETA = 10000.0
N_CHIPS = 4

ADAM_LR = 0.001
ADAM_B1 = 0.9
ADAM_B2 = 0.999
ADAM_EPS = 1e-08
ADAM_WD = 0.01
ADAM_STEP = 10

V7X_VMEM_LIMIT = 48 * 1024 * 1024
MESH = pl.DeviceIdType.MESH

_GELU_C = math.sqrt(2.0 / math.pi)
_GELU_A = 0.044715


def _params(sem=None):
    return pltpu.CompilerParams(dimension_semantics=sem, vmem_limit_bytes=V7X_VMEM_LIMIT)


ANY = pl.BlockSpec(memory_space=pl.ANY)


class _Order:
    last = None


def _ordered_call(body, *, token_index=0, **kw):
    def run(*operands):
        tok = _Order.last
        if tok is None or any(op is tok for op in operands):
            call = pl.pallas_call(body, **kw)
        else:
            n_in = len(operands)

            def ordered_body(*refs):
                return body(*refs[:n_in], *refs[n_in + 1:])

            kw2 = dict(kw)
            if "grid_spec" in kw2:
                gs = kw2["grid_spec"]
                kw2["grid_spec"] = pltpu.PrefetchScalarGridSpec(
                    num_scalar_prefetch=gs.num_scalar_prefetch, grid=gs.grid, in_specs=list(gs.in_specs) + [ANY],
                    out_specs=gs.out_specs, scratch_shapes=gs.scratch_shapes)
            else:
                kw2["in_specs"] = list(kw2["in_specs"]) + [ANY]
            call = pl.pallas_call(ordered_body, **kw2)
            operands = operands + (tok,)
        out = call(*operands)
        _Order.last = out[token_index] if isinstance(out, (tuple, list)) else out
        return out

    return run


def _gelu(x):
    return x * (0.5 * (1.0 + jnp.tanh(_GELU_C * (x + _GELU_A * (x * x * x)))))


def _gelu_grad(x):
    x2 = x * x
    t = jnp.tanh(_GELU_C * (x + _GELU_A * (x * x2)))
    return 0.5 * (1.0 + t) + 0.5 * x * (1.0 - t * t) * (_GELU_C * (1.0 + 3.0 * _GELU_A * x2))


def _mean_last(x):
    return jnp.mean(x, axis=-1, keepdims=True)


def _sum_rows(x):
    return jnp.sum(x, axis=0, keepdims=True)


def _sum_all(x):
    return jnp.sum(jnp.sum(x, axis=1, keepdims=True), axis=0, keepdims=True)


def _matmul(a, b, *, mode, out_dtype, tm, tn, tk, name, res=None, a_parts=0, b_parts=0, out_parts=0):
    assert mode in ("nn", "nt"), mode
    if mode == "nn":
        assert not a_parts
        m, k = a.shape
        n = b.shape[0] * b.shape[2] if b_parts else b.shape[1]
    else:
        m, k = (a.shape[1], a.shape[0] * a.shape[2]) if a_parts else a.shape
        n = b.shape[1] if b_parts else b.shape[0]
    tm, tn, tk = min(tm, m), min(tn, n), min(tk, k)
    assert m % tm == 0 and n % tn == 0 and k % tk == 0, (name, m, n, k, tm, tn, tk)
    nm, nn, nk = m // tm, n // tn, k // tk

    def slab(idx, total_tiles, parts):
        per = total_tiles // parts
        assert per * parts == total_tiles, (name, total_tiles, parts)
        return idx // per, idx % per

    if mode == "nn":
        a_spec = pl.BlockSpec((tm, tk), lambda i, j, kk: (i, kk))
        if b_parts:
            b_spec = pl.BlockSpec((None, tk, tn), lambda i, j, kk: (slab(j, nn, b_parts)[0], kk, slab(j, nn, b_parts)[1]))
        else:
            b_spec = pl.BlockSpec((tk, tn), lambda i, j, kk: (kk, j))
        dims = (((1,), (0,)), ((), ()))
    else:
        if a_parts:
            a_spec = pl.BlockSpec((None, tm, tk), lambda i, j, kk: (slab(kk, nk, a_parts)[0], i, slab(kk, nk, a_parts)[1]))
        else:
            a_spec = pl.BlockSpec((tm, tk), lambda i, j, kk: (i, kk))
        if b_parts:
            b_spec = pl.BlockSpec((None, tn, tk), lambda i, j, kk: (slab(kk, nk, b_parts)[0], j, slab(kk, nk, b_parts)[1]))
        else:
            b_spec = pl.BlockSpec((tn, tk), lambda i, j, kk: (j, kk))
        dims = (((1,), (1,)), ((), ()))
    if out_parts:
        out_shape = jax.ShapeDtypeStruct((out_parts, m, n // out_parts), out_dtype)
        out_spec = pl.BlockSpec((None, tm, tn), lambda i, j, kk: (slab(j, nn, out_parts)[0], i, slab(j, nn, out_parts)[1]))
    else:
        out_shape = jax.ShapeDtypeStruct((m, n), out_dtype)
        out_spec = pl.BlockSpec((tm, tn), lambda i, j, kk: (i, j))
    in_specs = [a_spec, b_spec]
    operands = [a, b]
    if res is not None:
        in_specs.append(pl.BlockSpec((tm, tn), lambda i, j, kk: (i, j)))
        operands.append(res)

    def body(*refs):
        a_ref, b_ref = refs[0], refs[1]
        res_ref = refs[2] if res is not None else None
        o_ref = refs[3] if res is not None else refs[2]
        p = lax.dot_general(a_ref[...], b_ref[...], dims, preferred_element_type=F32)

        def finish(total):
            if res_ref is not None:
                total = res_ref[...] + total
            o_ref[...] = total.astype(out_dtype)

        if nk == 1:
            finish(p)
        else:
            acc_ref = refs[-1]
            kk = pl.program_id(2)

            @pl.when(kk == 0)
            def _():
                acc_ref[...] = p

            @pl.when(jnp.logical_and(kk > 0, kk < nk - 1))
            def _():
                acc_ref[...] += p

            @pl.when(kk == nk - 1)
            def _():
                finish(acc_ref[...] + p)

    scratch = [pltpu.VMEM((tm, tn), F32)] if nk > 1 else []
    return _ordered_call(
        body, name=name, out_shape=out_shape, grid=(nm, nn, nk), in_specs=in_specs, out_specs=out_spec,
        scratch_shapes=scratch, compiler_params=_params(("parallel", "parallel", "arbitrary")),
    )(*operands)


def _matmul_nt_slabs(a, b, *, tm, tn, name, a_parts=0):
    nslab, n, ks = b.shape
    m = a.shape[1] if a_parts else a.shape[0]
    tm, tn = min(tm, m), min(tn, n)
    assert m % tm == 0 and n % tn == 0, (name, m, n, tm, tn)
    if a_parts:
        per = nslab // a_parts
        assert per * a_parts == nslab and a.shape[2] == per * ks, (name, a.shape, b.shape)
        a_spec = pl.BlockSpec((a_parts, tm, per * ks), lambda i, j: (0, i, 0))
    else:
        assert a.shape[1] == nslab * ks, (name, a.shape, b.shape)
        a_spec = pl.BlockSpec((tm, nslab * ks), lambda i, j: (i, 0))

    def body(a_ref, b_ref, o_ref):
        total = None
        for sl in range(nslab):
            if a_parts:
                a_sl = a_ref[sl // per, :, (sl % per) * ks:(sl % per + 1) * ks]
            else:
                a_sl = a_ref[:, sl * ks:(sl + 1) * ks]
            p = lax.dot_general(a_sl, b_ref[sl], (((1,), (1,)), ((), ())), preferred_element_type=F32)
            total = p if total is None else total + p
        o_ref[...] = total

    return _ordered_call(
        body, name=name, out_shape=jax.ShapeDtypeStruct((m, n), F32), grid=(m // tm, n // tn),
        in_specs=[a_spec, pl.BlockSpec((nslab, tn, ks), lambda i, j: (0, j, 0))],
        out_specs=pl.BlockSpec((tm, tn), lambda i, j: (i, j)), compiler_params=_params(("parallel", "parallel")),
    )(a, b)


GRAD_HALVES = {
    "w_in": ("rows_of_slab", 1024, 896), "w_up": ("rows_of_slab", 1024, 1408), "w_o": ("rows_of_block", 256, 2048),
    "w_down": ("cols_of_block", 1408, 512)}


def _half_shape(name, shard_shape):
    r, cols = shard_shape
    return (r, cols // 2) if GRAD_HALVES[name][0] == "cols_of_block" else (r // 2, cols)


def _grad_half(name, a, g, sel, res, call_name, g_parts=0):
    kind, tm, tn = GRAD_HALVES[name]
    s, m = a.shape
    n = g.shape[0] * g.shape[2] if g_parts else g.shape[1]
    if kind == "rows_of_slab":
        rh, hc = m // 2, n // N_CHIPS
        per = hc // tn
        grid = (rh // tm, n // tn)
        a_map = lambda i, j, sel_ref: (0, sel_ref[0] * (rh // tm) + i)
        g_col = lambda i, j, sel_ref: j
        o_map = lambda i, j, sel_ref: (j // per, i, j % per)
    elif kind == "rows_of_block":
        rh, hc = m // N_CHIPS // 2, n
        assert tm == rh
        grid = (N_CHIPS, n // tn)
        a_map = lambda i, j, sel_ref: (0, 2 * i + sel_ref[0])
        g_col = lambda i, j, sel_ref: j
        o_map = lambda i, j, sel_ref: (i, 0, j)
    else:
        rh, hc = m // N_CHIPS, n // 2
        assert tm == rh
        grid = (N_CHIPS, hc // tn)
        a_map = lambda i, j, sel_ref: (0, i)
        g_col = lambda i, j, sel_ref: sel_ref[0] * (hc // tn) + j
        o_map = lambda i, j, sel_ref: (i, 0, j)
    if g_parts:
        g_per = (n // tn) // g_parts
        g_spec = pl.BlockSpec((None, s, tn), lambda i, j, sel_ref: (g_col(i, j, sel_ref) // g_per, 0, g_col(i, j, sel_ref) % g_per))
    else:
        g_spec = pl.BlockSpec((s, tn), lambda i, j, sel_ref: (0, g_col(i, j, sel_ref)))
    o_spec = pl.BlockSpec((None, tm, tn), o_map)
    in_specs = [pl.BlockSpec((s, tm), a_map), g_spec] + ([o_spec] if res is not None else [])

    def body(sel_ref, a_ref, g_ref, *rest):
        o_ref = rest[-1]
        p = lax.dot_general(a_ref[...], g_ref[...], (((0,), (0,)), ((), ())), preferred_element_type=F32)
        if res is not None:
            p = p + rest[0][...].astype(F32)
        o_ref[...] = p.astype(BF16)

    grid_spec = pltpu.PrefetchScalarGridSpec(num_scalar_prefetch=1, grid=grid, in_specs=in_specs, out_specs=o_spec)
    return _ordered_call(
        body, name=call_name, out_shape=jax.ShapeDtypeStruct((N_CHIPS, rh, hc), BF16), grid_spec=grid_spec,
        compiler_params=_params(("parallel", "parallel")),
    )(sel, a, g, *([res] if res is not None else []))


def _row_tile(s):
    return min(256, s)


def _rows(width, tr):
    return pl.BlockSpec((tr, width), lambda i: (i, 0))


def _const2(shape):
    return pl.BlockSpec(shape, lambda i: (0, 0))


def _rms_fwd(x, g, name):
    s, d = x.shape
    tr = _row_tile(s)

    def body(x_ref, g_ref, o_ref):
        xv = x_ref[...]
        r = lax.rsqrt(_mean_last(xv * xv) + EPS)
        o_ref[...] = (xv * r * g_ref[...]).astype(BF16)

    return _ordered_call(
        body, name=name, out_shape=jax.ShapeDtypeStruct((s, d), BF16), grid=(s // tr,),
        in_specs=[_rows(d, tr), _const2((1, d))], out_specs=_rows(d, tr), compiler_params=_params(("parallel",)),
    )(x, g)


def _rms_bwd(x, g, dh, dres, name):
    s, d = x.shape
    tr = _row_tile(s)

    def body(x_ref, g_ref, dh_ref, dres_ref, dx_ref, dxb_ref, dg_ref):
        xv, dy = x_ref[...], dh_ref[...]
        r = lax.rsqrt(_mean_last(xv * xv) + EPS)
        gdy = dy * g_ref[...]
        dx = dres_ref[...] + r * gdy - xv * ((r * r * r) * _mean_last(xv * gdy))
        dx_ref[...] = dx
        dxb_ref[...] = dx.astype(BF16)

        @pl.when(pl.program_id(0) == 0)
        def _():
            dg_ref[...] = jnp.zeros_like(dg_ref)

        dg_ref[...] += _sum_rows(xv * r * dy)

    return _ordered_call(
        body, name=name,
        out_shape=(jax.ShapeDtypeStruct((s, d), F32), jax.ShapeDtypeStruct((s, d), BF16), jax.ShapeDtypeStruct((1, d), F32)),
        grid=(s // tr,), in_specs=[_rows(d, tr), _const2((1, d)), _rows(d, tr), _rows(d, tr)],
        out_specs=(_rows(d, tr), _rows(d, tr), _const2((1, d))), compiler_params=_params(("arbitrary",)),
    )(x, g, dh, dres)


Q0, K0, V0, GU0, GV0 = 0, ATTN_WIDTH, ATTN_WIDTH + KV_WIDTH, ATTN_WIDTH + 2 * KV_WIDTH, ATTN_WIDTH + 2 * KV_WIDTH + GMLP_WIDTH


def _head(h, base=0):
    return slice(base + h * HEAD_DIM, base + (h + 1) * HEAD_DIM)


def _proj_post(z, qg, kg, lg, lb, cosf, sinf, name):
    s = z.shape[0]
    tr = _row_tile(s)

    def body(z_ref, qg_ref, kg_ref, lg_ref, lb_ref, cos_ref, sin_ref, qn_ref, kn_ref, vb_ref, ug_ref, vn_ref):
        cos, sin = cos_ref[...], sin_ref[...]

        def norm_rope(xh, g):
            y = xh * lax.rsqrt(_mean_last(xh * xh) + EPS) * g
            return y * cos + pltpu.roll(y, HEAD_DIM // 2, 1) * sin

        for h in range(N_Q_HEADS):
            qn_ref[:, _head(h)] = norm_rope(z_ref[:, _head(h, Q0)].astype(F32), qg_ref[...]).astype(BF16)
        for h in range(N_KV_HEADS):
            kn_ref[:, _head(h)] = norm_rope(z_ref[:, _head(h, K0)].astype(F32), kg_ref[...]).astype(BF16)
        vb_ref[...] = z_ref[:, V0:GU0]
        ug_ref[...] = _gelu(z_ref[:, GU0:GV0].astype(F32))
        vg = _gelu(z_ref[:, GV0:IN_WIDTH].astype(F32))
        xc = vg - _mean_last(vg)
        y = xc * lax.rsqrt(_mean_last(xc * xc) + EPS)
        vn_ref[...] = (y * lg_ref[...] + lb_ref[...]).astype(BF16)

    return _ordered_call(
        body, name=name,
        out_shape=(jax.ShapeDtypeStruct((s, ATTN_WIDTH), BF16), jax.ShapeDtypeStruct((s, KV_WIDTH), BF16),
                   jax.ShapeDtypeStruct((s, KV_WIDTH), BF16), jax.ShapeDtypeStruct((s, GMLP_WIDTH), F32),
                   jax.ShapeDtypeStruct((s, GMLP_WIDTH), BF16)),
        grid=(s // tr,),
        in_specs=[_rows(IN_WIDTH, tr), _const2((1, HEAD_DIM)), _const2((1, HEAD_DIM)), _const2((1, GMLP_WIDTH)),
                  _const2((1, GMLP_WIDTH)), _rows(HEAD_DIM, tr), _rows(HEAD_DIM, tr)],
        out_specs=(_rows(ATTN_WIDTH, tr), _rows(KV_WIDTH, tr), _rows(KV_WIDTH, tr), _rows(GMLP_WIDTH, tr), _rows(GMLP_WIDTH, tr)),
        compiler_params=_params(("parallel",)),
    )(z, qg, kg, lg, lb, cosf, sinf)


def _proj_post_bwd(z, dqn, dkn, dvb, dug, dvn, qg, kg, lg, cosf, sinf, name):
    s = z.shape[0]
    tr = _row_tile(s)

    def body(z_ref, dqn_ref, dkn_ref, dvb_ref, dug_ref, dvn_ref, qg_ref, kg_ref, lg_ref, cos_ref, sin_ref,
             dz_ref, dqg_ref, dkg_ref, dlg_ref, dlb_ref):
        cos, sin = cos_ref[...], sin_ref[...]

        @pl.when(pl.program_id(0) == 0)
        def _():
            dqg_ref[...] = jnp.zeros_like(dqg_ref)
            dkg_ref[...] = jnp.zeros_like(dkg_ref)
            dlg_ref[...] = jnp.zeros_like(dlg_ref)
            dlb_ref[...] = jnp.zeros_like(dlb_ref)

        def norm_rope_bwd(xh, g, dout):
            dy = dout * cos - pltpu.roll(dout, HEAD_DIM // 2, 1) * sin
            r = lax.rsqrt(_mean_last(xh * xh) + EPS)
            xhat = xh * r
            gdy = dy * g
            return r * (gdy - xhat * _mean_last(xhat * gdy)), _sum_rows(xhat * dy)

        dqg = jnp.zeros((1, HEAD_DIM), F32)
        for h in range(N_Q_HEADS):
            dx, dg = norm_rope_bwd(z_ref[:, _head(h, Q0)].astype(F32), qg_ref[...], dqn_ref[:, _head(h)])
            dz_ref[:, _head(h, Q0)] = dx.astype(BF16)
            dqg = dqg + dg
        dqg_ref[...] += dqg
        dkg = jnp.zeros((1, HEAD_DIM), F32)
        for h in range(N_KV_HEADS):
            dx, dg = norm_rope_bwd(z_ref[:, _head(h, K0)].astype(F32), kg_ref[...], dkn_ref[:, _head(h)])
            dz_ref[:, _head(h, K0)] = dx.astype(BF16)
            dkg = dkg + dg
        dkg_ref[...] += dkg
        dz_ref[:, V0:GU0] = dvb_ref[...].astype(BF16)
        dz_ref[:, GU0:GV0] = (dug_ref[...] * _gelu_grad(z_ref[:, GU0:GV0].astype(F32))).astype(BF16)
        gv = z_ref[:, GV0:IN_WIDTH].astype(F32)
        vg = _gelu(gv)
        xc = vg - _mean_last(vg)
        r = lax.rsqrt(_mean_last(xc * xc) + EPS)
        xhat = xc * r
        dvn_v = dvn_ref[...]
        dlg_ref[...] += _sum_rows(xhat * dvn_v)
        dlb_ref[...] += _sum_rows(dvn_v)
        dxh = dvn_v * lg_ref[...]
        dvg = r * (dxh - _mean_last(dxh) - xhat * _mean_last(dxh * xhat))
        dz_ref[:, GV0:IN_WIDTH] = (dvg * _gelu_grad(gv)).astype(BF16)

    return _ordered_call(
        body, name=name,
        out_shape=(jax.ShapeDtypeStruct((s, IN_WIDTH), BF16), jax.ShapeDtypeStruct((1, HEAD_DIM), F32),
                   jax.ShapeDtypeStruct((1, HEAD_DIM), F32), jax.ShapeDtypeStruct((1, GMLP_WIDTH), F32),
                   jax.ShapeDtypeStruct((1, GMLP_WIDTH), F32)),
        grid=(s // tr,),
        in_specs=[_rows(IN_WIDTH, tr), _rows(ATTN_WIDTH, tr), _rows(KV_WIDTH, tr), _rows(KV_WIDTH, tr), _rows(GMLP_WIDTH, tr),
                  _rows(GMLP_WIDTH, tr), _const2((1, HEAD_DIM)), _const2((1, HEAD_DIM)), _const2((1, GMLP_WIDTH)),
                  _rows(HEAD_DIM, tr), _rows(HEAD_DIM, tr)],
        out_specs=(_rows(IN_WIDTH, tr), _const2((1, HEAD_DIM)), _const2((1, HEAD_DIM)), _const2((1, GMLP_WIDTH)),
                   _const2((1, GMLP_WIDTH))),
        compiler_params=_params(("arbitrary",)),
    )(z, dqn, dkn, dvb, dug, dvn, qg, kg, lg, cosf, sinf)


def _band_valid(n, s):
    shape = (GQA_GROUP * BLOCK, 3 * BLOCK)
    i = lax.broadcasted_iota(jnp.int32, shape, 0) & (BLOCK - 1)
    j = lax.broadcasted_iota(jnp.int32, shape, 1)
    k_pos = n * BLOCK - BLOCK + j
    return (jnp.abs(j - BLOCK - i) <= BLOCK) & (k_pos >= 0) & (k_pos < s)


def _group_rows(x, kh):
    return jnp.concatenate([x[:, _head(kh * GQA_GROUP + g)] for g in range(GQA_GROUP)], axis=0)


def _group_sinks(sink_ref, kh):
    return jnp.concatenate([jnp.full((BLOCK, 1), sink_ref[kh * GQA_GROUP + g], F32) for g in range(GQA_GROUP)], axis=0)


def _rows_of(x, g):
    return x[g * BLOCK:(g + 1) * BLOCK]


def _probs(q, kb, sink_h, valid):
    sc = lax.dot_general(q, kb, (((1,), (1,)), ((), ())), preferred_element_type=F32) * (HEAD_DIM ** -0.5)
    sc = jnp.where(valid, sc, MASK_VALUE)
    m = jnp.maximum(jnp.max(sc, axis=-1, keepdims=True), sink_h)
    p = jnp.exp(sc - m)
    es = jnp.exp(sink_h - m)
    den = jnp.sum(p, axis=-1, keepdims=True) + es
    inv = 1.0 / den
    return p * inv, es * inv


def _band_specs(width, nb):
    return [pl.BlockSpec((BLOCK, width), lambda n: (jnp.maximum(n - 1, 0), 0)),
            pl.BlockSpec((BLOCK, width), lambda n: (n, 0)),
            pl.BlockSpec((BLOCK, width), lambda n: (jnp.minimum(n + 1, nb - 1), 0))]


def _blk(width):
    return pl.BlockSpec((BLOCK, width), lambda n: (n, 0))


def _whole3(shape):
    return pl.BlockSpec(shape, lambda n: (0, 0, 0))


def _smem():
    return pl.BlockSpec(memory_space=pltpu.SMEM)


def _mixer_fwd(qn, kn, vb, ug, vn, wsb, bsb, sink, ga, gs, name):
    s = qn.shape[0]
    nb = s // BLOCK

    def body(sink_ref, q_ref, kp_ref, kc_ref, kx_ref, vp_ref, vc_ref, vx_ref, ug_ref, vn_ref, ws_ref, bs_ref, ga_ref, gs_ref,
             attn_ref, sgu_ref, mix_ref, probs_ref, psink_ref):
        n = pl.program_id(0)
        valid = _band_valid(n, s)
        ssq = jnp.zeros((BLOCK, 1), F32)
        for kh in range(N_KV_HEADS):
            kb = jnp.concatenate([kp_ref[:, _head(kh)], kc_ref[:, _head(kh)], kx_ref[:, _head(kh)]], axis=0)
            vbd = jnp.concatenate([vp_ref[:, _head(kh)], vc_ref[:, _head(kh)], vx_ref[:, _head(kh)]], axis=0)
            p, p_sink = _probs(_group_rows(q_ref, kh), kb, _group_sinks(sink_ref, kh), valid)
            pb = p.astype(BF16)
            probs_ref[kh] = pb
            psink_ref[kh] = p_sink
            o4 = jnp.dot(pb, vbd, preferred_element_type=F32)
            for g in range(GQA_GROUP):
                o = _rows_of(o4, g)
                attn_ref[:, _head(kh * GQA_GROUP + g)] = o
                ssq = ssq + jnp.sum(o * o, axis=-1, keepdims=True)
        r = lax.rsqrt(ssq * (1.0 / ATTN_WIDTH) + EPS)
        mix_ref[:, 0:ATTN_WIDTH] = (attn_ref[...] * r * ga_ref[...]).astype(BF16)
        ssq = jnp.zeros((BLOCK, 1), F32)
        for h in range(N_GMLP_HEADS):
            f = jnp.dot(ws_ref[h], vn_ref[:, _head(h)], preferred_element_type=F32) + bs_ref[h]
            o = ug_ref[:, _head(h)] * f
            sgu_ref[:, _head(h)] = o
            ssq = ssq + jnp.sum(o * o, axis=-1, keepdims=True)
        r = lax.rsqrt(ssq * (1.0 / GMLP_WIDTH) + EPS)
        mix_ref[:, ATTN_WIDTH:D_MODEL] = (sgu_ref[...] * r * gs_ref[...]).astype(BF16)

    hh = (N_GMLP_HEADS, BLOCK, BLOCK)
    return _ordered_call(
        body, name=name,
        out_shape=(jax.ShapeDtypeStruct((s, ATTN_WIDTH), F32), jax.ShapeDtypeStruct((s, GMLP_WIDTH), F32),
                   jax.ShapeDtypeStruct((s, D_MODEL), BF16), jax.ShapeDtypeStruct((nb,) + PROBS_BLOCK, BF16),
                   jax.ShapeDtypeStruct((nb,) + PSINK_BLOCK, F32)),
        grid=(nb,),
        in_specs=[_smem(), _blk(ATTN_WIDTH)] + _band_specs(KV_WIDTH, nb) + _band_specs(KV_WIDTH, nb)
        + [_blk(GMLP_WIDTH), _blk(GMLP_WIDTH), _whole3(hh), _whole3(hh),
           pl.BlockSpec((1, ATTN_WIDTH), lambda n: (0, 0)), pl.BlockSpec((1, GMLP_WIDTH), lambda n: (0, 0))],
        out_specs=(_blk(ATTN_WIDTH), _blk(GMLP_WIDTH), _blk(D_MODEL), _per_block(PROBS_BLOCK), _per_block(PSINK_BLOCK)),
        compiler_params=_params(("parallel",)),
    )(sink, qn, kn, kn, kn, vb, vb, vb, ug, vn, wsb, bsb, ga, gs)


PROBS_BLOCK = (N_KV_HEADS, GQA_GROUP * BLOCK, 3 * BLOCK)
PSINK_BLOCK = (N_KV_HEADS, GQA_GROUP * BLOCK, 1)


def _per_block(shape):
    return pl.BlockSpec((None,) + shape, lambda n: (n, 0, 0, 0))


def _mixer_bwd(qn, kn, vb, ug, vn, attn, sgu, dmixed, wsb, bsb, ga, gs, probs, psink, name):
    s = qn.shape[0]
    nb = s // BLOCK
    tn_dims = (((0,), (0,)), ((), ()))
    nt_dims = (((1,), (1,)), ((), ()))

    def body(q_ref, kp_ref, kc_ref, kx_ref, vp_ref, vc_ref, vx_ref, ug_ref, vn_ref, attn_ref, sgu_ref, dm_ref,
             ws_ref, bs_ref, ga_ref, gs_ref, probs_ref, psink_ref,
             dq_ref, dk_ref, dv_ref, dug_ref, dvn_ref, dws_ref, dbs_ref, dsk_ref, dga_ref, dgs_ref, dk_acc, dv_acc):
        n = pl.program_id(0)

        @pl.when(n == 0)
        def _():
            for ref in (dk_acc, dv_acc, dws_ref, dbs_ref, dsk_ref, dga_ref, dgs_ref):
                ref[...] = jnp.zeros_like(ref)

        def out_norm_bwd(o, g, dy):
            r = lax.rsqrt(_mean_last(o * o) + EPS)
            gdy = dy * g
            return r * gdy - o * ((r * r * r) * _mean_last(o * gdy)), _sum_rows(o * r * dy)

        d_attn, dga = out_norm_bwd(attn_ref[...], ga_ref[...], dm_ref[:, 0:ATTN_WIDTH])
        dga_ref[...] += dga
        d_sgu, dgs = out_norm_bwd(sgu_ref[...], gs_ref[...], dm_ref[:, ATTN_WIDTH:D_MODEL])
        dgs_ref[...] += dgs

        for h in range(N_GMLP_HEADS):
            vn_h = vn_ref[:, _head(h)]
            f = jnp.dot(ws_ref[h], vn_h, preferred_element_type=F32) + bs_ref[h]
            ds_h = d_sgu[:, _head(h)]
            dug_ref[:, _head(h)] = ds_h * f
            df = ds_h * ug_ref[:, _head(h)]
            dfb = df.astype(BF16)
            dvn_ref[:, _head(h)] = lax.dot_general(ws_ref[h], dfb, tn_dims, preferred_element_type=F32)
            dws_ref[h] += lax.dot_general(dfb, vn_h, nt_dims, preferred_element_type=F32)
            dbs_ref[h] += jnp.broadcast_to(jnp.sum(df, axis=-1, keepdims=True), (BLOCK, BLOCK))

        row0 = pl.multiple_of(n * BLOCK, BLOCK)
        for kh in range(N_KV_HEADS):
            kb = jnp.concatenate([kp_ref[:, _head(kh)], kc_ref[:, _head(kh)], kx_ref[:, _head(kh)]], axis=0)
            vbd = jnp.concatenate([vp_ref[:, _head(kh)], vc_ref[:, _head(kh)], vx_ref[:, _head(kh)]], axis=0)
            q4 = _group_rows(q_ref, kh)
            pb = probs_ref[kh]
            p = pb.astype(F32)
            do4 = _group_rows(d_attn, kh).astype(BF16)
            dp = lax.dot_general(do4, vbd, nt_dims, preferred_element_type=F32)
            delta = jnp.sum(p * dp, axis=-1, keepdims=True)
            dsc = (p * (dp - delta) * (HEAD_DIM ** -0.5)).astype(BF16)
            d_sink = -(psink_ref[kh] * delta)
            dq4 = jnp.dot(dsc, kb, preferred_element_type=F32)
            for g in range(GQA_GROUP):
                h = kh * GQA_GROUP + g
                dsk_ref[h:h + 1, :] += jnp.broadcast_to(_sum_all(_rows_of(d_sink, g)), (1, BLOCK))
                dq_ref[:, _head(h)] = _rows_of(dq4, g)
            dk_acc[pl.ds(row0, 3 * BLOCK), _head(kh)] += lax.dot_general(dsc, q4, tn_dims, preferred_element_type=F32)
            dv_acc[pl.ds(row0, 3 * BLOCK), _head(kh)] += lax.dot_general(pb, do4, tn_dims, preferred_element_type=F32)

        @pl.when(n == nb - 1)
        def _():
            dk_ref[...] = dk_acc[BLOCK:BLOCK + s, :]
            dv_ref[...] = dv_acc[BLOCK:BLOCK + s, :]

    hh = (N_GMLP_HEADS, BLOCK, BLOCK)
    full_kv = pl.BlockSpec((s, KV_WIDTH), lambda n: (0, 0))
    return _ordered_call(
        body, name=name,
        out_shape=(jax.ShapeDtypeStruct((s, ATTN_WIDTH), F32), jax.ShapeDtypeStruct((s, KV_WIDTH), F32),
                   jax.ShapeDtypeStruct((s, KV_WIDTH), F32), jax.ShapeDtypeStruct((s, GMLP_WIDTH), F32),
                   jax.ShapeDtypeStruct((s, GMLP_WIDTH), F32), jax.ShapeDtypeStruct(hh, F32), jax.ShapeDtypeStruct(hh, F32),
                   jax.ShapeDtypeStruct((N_Q_HEADS, BLOCK), F32), jax.ShapeDtypeStruct((1, ATTN_WIDTH), F32),
                   jax.ShapeDtypeStruct((1, GMLP_WIDTH), F32)),
        grid=(nb,),
        in_specs=[_blk(ATTN_WIDTH)] + _band_specs(KV_WIDTH, nb) + _band_specs(KV_WIDTH, nb)
        + [_blk(GMLP_WIDTH), _blk(GMLP_WIDTH), _blk(ATTN_WIDTH), _blk(GMLP_WIDTH), _blk(D_MODEL), _whole3(hh), _whole3(hh),
           pl.BlockSpec((1, ATTN_WIDTH), lambda n: (0, 0)), pl.BlockSpec((1, GMLP_WIDTH), lambda n: (0, 0)),
           _per_block(PROBS_BLOCK), _per_block(PSINK_BLOCK)],
        out_specs=(_blk(ATTN_WIDTH), full_kv, full_kv, _blk(GMLP_WIDTH), _blk(GMLP_WIDTH), _whole3(hh), _whole3(hh),
                   pl.BlockSpec((N_Q_HEADS, BLOCK), lambda n: (0, 0)), pl.BlockSpec((1, ATTN_WIDTH), lambda n: (0, 0)),
                   pl.BlockSpec((1, GMLP_WIDTH), lambda n: (0, 0))),
        scratch_shapes=[pltpu.VMEM((s + 2 * BLOCK, KV_WIDTH), F32), pltpu.VMEM((s + 2 * BLOCK, KV_WIDTH), F32)],
        compiler_params=_params(("arbitrary",)),
    )(qn, kn, kn, kn, vb, vb, vb, ug, vn, attn, sgu, dmixed, wsb, bsb, ga, gs, probs, psink)


CONV_TILE = 128


PAD_ROWS = 8


def _zero_pad_rows(pad_ref):
    s = pad_ref.shape[0] - 2 * PAD_ROWS
    zeros = jnp.zeros((PAD_ROWS, pad_ref.shape[1]), F32)
    pad_ref[0:PAD_ROWS, :] = zeros
    pad_ref[PAD_ROWS + s:2 * PAD_ROWS + s, :] = zeros


def _shift_rows(a, pad_ref):
    s = a.shape[0]
    pad_ref[PAD_ROWS:PAD_ROWS + s, :] = a
    padded = pad_ref[...]
    prev = pltpu.roll(padded, 1, 0)[PAD_ROWS:PAD_ROWS + s]
    nxt = pltpu.roll(padded, s + 2 * PAD_ROWS - 1, 0)[PAD_ROWS:PAD_ROWS + s]
    return prev, nxt


def _conv_specs(s):
    tc = CONV_TILE
    nj = D_FF // tc
    return (tc, nj, pl.BlockSpec((2, s, tc), lambda j: (0, 0, j)),
            [pl.BlockSpec((3, tc), lambda j: (0, j)), pl.BlockSpec((3, tc), lambda j: (0, j + nj))],
            [pl.BlockSpec((1, tc), lambda j: (0, j)), pl.BlockSpec((1, tc), lambda j: (0, j + nj))])


def _conv_gate_fwd(a_pre, cw, cb, name):
    s = a_pre.shape[1]
    tc, nj, a_spec, w_specs, b_specs = _conv_specs(s)

    def body(a_ref, wg_ref, wu_ref, bg_ref, bu_ref, act_ref, dgu_ref, pad_ref):
        _zero_pad_rows(pad_ref)

        def conv(a, w_ref, b_ref):
            prev, nxt = _shift_rows(a, pad_ref)
            return b_ref[...] + prev * w_ref[0:1, :] + a * w_ref[1:2, :] + nxt * w_ref[2:3, :]

        g = conv(a_ref[0].astype(F32), wg_ref, bg_ref)
        u = conv(a_ref[1].astype(F32), wu_ref, bu_ref)
        sg = 1.0 / (1.0 + jnp.exp(-g))
        silu = g * sg
        act_ref[...] = (silu * u).astype(BF16)
        dgu_ref[0] = (u * (sg * (1.0 + g * (1.0 - sg)))).astype(BF16)
        dgu_ref[1] = silu.astype(BF16)

    return _ordered_call(
        body, name=name, out_shape=(jax.ShapeDtypeStruct((s, D_FF), BF16), jax.ShapeDtypeStruct((2, s, D_FF), BF16)),
        grid=(nj,), in_specs=[a_spec] + w_specs + b_specs,
        out_specs=(pl.BlockSpec((s, tc), lambda j: (0, j)), pl.BlockSpec((2, s, tc), lambda j: (0, 0, j))),
        scratch_shapes=[pltpu.VMEM((s + 2 * PAD_ROWS, tc), F32)], compiler_params=_params(("parallel",)),
    )(a_pre, cw, cw, cb, cb)


def _conv_gate_bwd(a_pre, dgu, cw, dact, name):
    s = a_pre.shape[1]
    tc, nj, a_spec, w_specs, _ = _conv_specs(s)

    def body(a_ref, dgu_ref, wg_ref, wu_ref, dact_ref, dap_ref, dcw_ref, dcb_ref, pad_ref):
        _zero_pad_rows(pad_ref)
        dact_v = dact_ref[...].astype(F32)
        for part, w_ref in enumerate((wg_ref, wu_ref)):
            da = dact_v * dgu_ref[part].astype(F32)
            a = a_ref[part].astype(F32)
            prev, nxt = _shift_rows(a, pad_ref)
            dcw_ref[part, 0:1, :] = _sum_rows(prev * da)
            dcw_ref[part, 1:2, :] = _sum_rows(a * da)
            dcw_ref[part, 2:3, :] = _sum_rows(nxt * da)
            dcb_ref[part] = _sum_rows(da)
            da_prev, da_next = _shift_rows(da, pad_ref)
            dap_ref[part] = (da_next * w_ref[0:1, :] + da * w_ref[1:2, :] + da_prev * w_ref[2:3, :]).astype(BF16)

    return _ordered_call(
        body, name=name,
        out_shape=(jax.ShapeDtypeStruct((2, s, D_FF), BF16), jax.ShapeDtypeStruct((2, 3, D_FF), F32),
                   jax.ShapeDtypeStruct((2, 1, D_FF), F32)),
        grid=(nj,),
        in_specs=[a_spec, pl.BlockSpec((2, s, tc), lambda j: (0, 0, j))] + w_specs + [pl.BlockSpec((s, tc), lambda j: (0, j))],
        out_specs=(pl.BlockSpec((2, s, tc), lambda j: (0, 0, j)), pl.BlockSpec((2, 3, tc), lambda j: (0, 0, j)),
                   pl.BlockSpec((2, 1, tc), lambda j: (0, 0, j))),
        scratch_shapes=[pltpu.VMEM((s + 2 * PAD_ROWS, tc), F32)], compiler_params=_params(("parallel",)),
    )(a_pre, dgu, cw, cw, dact)


def _loss_head(y, target, name):
    s, d = y.shape
    tr = _row_tile(s)

    def body(y_ref, t_ref, loss_ref, dy_ref, dyb_ref):
        err = y_ref[...] - t_ref[...]

        @pl.when(pl.program_id(0) == 0)
        def _():
            loss_ref[...] = jnp.zeros_like(loss_ref)

        loss_ref[...] += jnp.broadcast_to(0.5 * _sum_all(_mean_last(err * err)), (8, 128))
        dy = err * (1.0 / d)
        dy_ref[...] = dy
        dyb_ref[...] = dy.astype(BF16)

    return _ordered_call(
        body, name=name,
        out_shape=(jax.ShapeDtypeStruct((8, 128), F32), jax.ShapeDtypeStruct((s, d), F32), jax.ShapeDtypeStruct((s, d), BF16)),
        grid=(s // tr,), in_specs=[_rows(d, tr), _rows(d, tr)],
        out_specs=(_const2((8, 128)), _rows(d, tr), _rows(d, tr)), compiler_params=_params(("arbitrary",)),
    )(y, target)


def _row_block(rows, cols, budget=1 << 20):
    if rows * cols <= budget:
        return rows
    best = None
    for tr in range(16, rows, 16):
        if rows % tr == 0 and tr * cols <= budget:
            best = tr
    assert best is not None, (rows, cols)
    return best


def _place_shard(x4, layer, j_arr, out_dtype, name):
    _, nh, r, cols = x4.shape
    tr = _row_block(r, cols)

    def body(j_ref, x_ref, o_ref):
        o_ref[...] = x_ref[...].astype(out_dtype)

    grid_spec = pltpu.PrefetchScalarGridSpec(
        num_scalar_prefetch=1, grid=(nh, r // tr),
        in_specs=[pl.BlockSpec((None, None, tr, cols), lambda h, i, j_ref: (layer, h, i, 0))],
        out_specs=pl.BlockSpec((None, None, tr, cols), lambda h, i, j_ref: (j_ref[0], h, i, 0)))
    return _ordered_call(
        body, name=name, out_shape=jax.ShapeDtypeStruct((N_CHIPS, nh, r, cols), out_dtype), grid_spec=grid_spec,
        compiler_params=_params(("parallel", "parallel")),
    )(j_arr, x4)


def _adamw(w, g, m, v, name):
    rows, cols = w.shape
    tr = _row_block(rows, cols, 1 << 18)

    def body(w_ref, g_ref, m_ref, v_ref, go_ref, d_ref, nm_ref, nv_ref):
        gv = g_ref[...]
        go_ref[...] = gv
        mn = ADAM_B1 * m_ref[...] + (1.0 - ADAM_B1) * gv
        vn = ADAM_B2 * v_ref[...] + (1.0 - ADAM_B2) * (gv * gv)
        m_hat = mn / (1.0 - ADAM_B1 ** ADAM_STEP)
        v_hat = vn / (1.0 - ADAM_B2 ** ADAM_STEP)
        d_ref[...] = -ADAM_LR * (m_hat / (jnp.sqrt(v_hat) + ADAM_EPS) + ADAM_WD * w_ref[...])
        nm_ref[...] = mn
        nv_ref[...] = vn

    sds = jax.ShapeDtypeStruct((rows, cols), F32)
    return _ordered_call(
        body, name=name, out_shape=(sds, sds, sds, sds), grid=(rows // tr,),
        in_specs=[_rows(cols, tr)] * 4, out_specs=(_rows(cols, tr),) * 4, compiler_params=_params(("parallel",)),
    )(w, g, m, v)


def _chip_sum(p4, recv3, j_arr, c_arr, name):
    _, rh, cols = p4.shape
    tr = _row_block(rh, cols, 1 << 19)

    def body(j_ref, c_ref, p_ref, r_ref, o_ref):
        total = p_ref[...].astype(F32)
        for peer in range(3):
            total = total + r_ref[peer].astype(F32)
        o_ref[...] = total

    grid_spec = pltpu.PrefetchScalarGridSpec(
        num_scalar_prefetch=2, grid=(rh // tr,),
        in_specs=[pl.BlockSpec((None, tr, cols), lambda i, j_ref, c_ref: (j_ref[0], i, 0)),
                  pl.BlockSpec((3, tr, cols), lambda i, j_ref, c_ref: (0, i, 0))],
        out_specs=pl.BlockSpec((None, tr, cols), lambda i, j_ref, c_ref: (c_ref[0], i, 0)))
    return _ordered_call(
        body, name=name, out_shape=jax.ShapeDtypeStruct((2, rh, cols), F32), grid_spec=grid_spec,
        compiler_params=_params(("parallel",)),
    )(j_arr, c_arr, p4, recv3)


def _adamw_layer(w, g, m, v, layer, into, name):
    nl, rows, cols = w.shape
    slabs, _, width = g.shape
    assert slabs * width == cols and g.shape[1] == rows, (name, w.shape, g.shape)
    tr = _row_block(rows, width, 1 << 18)
    at_layer = pl.BlockSpec((None, tr, width), lambda h, i: (layer, i, h))

    def body(w_ref, g_ref, m_ref, v_ref, *rest):
        go_ref, d_ref, nm_ref, nv_ref = rest[-4:]
        gv = g_ref[...]
        go_ref[...] = gv
        mn = ADAM_B1 * m_ref[...] + (1.0 - ADAM_B1) * gv
        vn = ADAM_B2 * v_ref[...] + (1.0 - ADAM_B2) * (gv * gv)
        m_hat = mn / (1.0 - ADAM_B1 ** ADAM_STEP)
        v_hat = vn / (1.0 - ADAM_B2 ** ADAM_STEP)
        d_ref[...] = -ADAM_LR * (m_hat / (jnp.sqrt(v_hat) + ADAM_EPS) + ADAM_WD * w_ref[...])
        nm_ref[...] = mn
        nv_ref[...] = vn

    in_specs = [at_layer, pl.BlockSpec((None, tr, width), lambda h, i: (h, i, 0)), at_layer, at_layer]
    operands = [w, g, m, v]
    aliases = {}
    if into is not None:
        in_specs += [ANY] * 4
        operands += list(into)
        aliases = {4 + i: i for i in range(4)}
    sds = jax.ShapeDtypeStruct((nl, rows, cols), F32)
    return _ordered_call(
        body, name=name, out_shape=(sds,) * 4, grid=(slabs, rows // tr), in_specs=in_specs, out_specs=(at_layer,) * 4,
        input_output_aliases=aliases, compiler_params=_params(("parallel", "parallel")),
    )(*operands)


def _sum_devices(mine, landed, me_arr, name):
    rows, lanes = mine.shape

    def body(me_ref, mine_ref, landed_ref, o_ref):
        total = None
        for dev in range(8):
            part = jnp.where(me_ref[0] == dev, mine_ref[...], landed_ref[dev])
            total = part if total is None else total + part
        o_ref[...] = total

    grid_spec = pltpu.PrefetchScalarGridSpec(
        num_scalar_prefetch=1, grid=(1,),
        in_specs=[pl.BlockSpec((rows, lanes), lambda i, me_ref: (0, 0)), pl.BlockSpec((8, rows, lanes), lambda i, me_ref: (0, 0, 0))],
        out_specs=pl.BlockSpec((rows, lanes), lambda i, me_ref: (0, 0)))
    return _ordered_call(
        body, name=name, out_shape=jax.ShapeDtypeStruct((rows, lanes), F32), grid_spec=grid_spec,
        compiler_params=_params(("arbitrary",)),
    )(me_arr, mine, landed)


def _place():
    x, y, c = lax.axis_index("x"), lax.axis_index("y"), lax.axis_index("c")
    chips = [(1 - x, y), (x, 1 - y), (1 - x, 1 - y)]
    return x, y, c, chips


HBM = pl.BlockSpec(memory_space=pltpu.HBM)
SEM = pl.BlockSpec(memory_space=pltpu.SEMAPHORE)
TOKEN = jax.ShapeDtypeStruct((8, 128), F32)


def _remote(src, dst, send_sem, recv_sem, to):
    return pltpu.make_async_remote_copy(src_ref=src, dst_ref=dst, send_sem=send_sem, recv_sem=recv_sem, device_id=to,
                                        device_id_type=MESH)


def _split_call(body, name, thru, sems_in=(), fresh=(), new_sems=(), after_last=True):
    n_t, n_s, n_f = len(thru), len(sems_in), len(fresh)

    def call_body(*refs):
        outs = refs[n_t + n_s:]
        body(refs[:n_t], refs[n_t:n_t + n_s], outs[1 + n_t:1 + n_t + n_f], outs[1 + n_t + n_f:])
        outs[0][...] = jnp.zeros_like(outs[0])

    out_shape = ([TOKEN] + [pltpu.HBM(t.shape, t.dtype) for t in thru] + [pltpu.HBM(shp, dt) for shp, dt in fresh]
                 + [pltpu.SemaphoreType.DMA(shp) for shp in new_sems])
    out_specs = [pl.BlockSpec(memory_space=pltpu.VMEM)] + [HBM] * (n_t + n_f) + [SEM] * len(new_sems)
    if not after_last:
        _Order.last = None
    out = _ordered_call(
        call_body, name=name, out_shape=tuple(out_shape), in_specs=[HBM] * n_t + [SEM] * n_s, out_specs=tuple(out_specs),
        input_output_aliases={i: 1 + i for i in range(n_t)},
        compiler_params=pltpu.CompilerParams(has_side_effects=pltpu.SideEffectType.DATAFLOW_SIDE_EFFECTING),
    )(*[pltpu.with_memory_space_constraint(t, pltpu.HBM) for t in thru], *sems_in)
    return out[1:1 + n_t], out[1 + n_t:1 + n_t + n_f], out[1 + n_t + n_f:]


class _Exchange:
    def __init__(self, weights, m_in, v_in, j_arr, c_arr, me_arr):
        self.w, self.m, self.v = weights, m_in, v_in
        self.j_arr, self.c_arr, self.me_arr = j_arr, c_arr, me_arr
        self.adam, self.small, self.pairs = {}, {}, {}
        self.o_arr = 1 - c_arr
        self.groups = [(l, name) for l in range(DEPTH) for name in BIG_NAMES]
        self.shard_shape = {name: weights[name].shape[1:] for name in BIG_NAMES}
        self.conv_state, self.state = [], {}
        self.ready, self.conv_ready = {}, {}
        self.pending, self.tick, self.reduced = [], 0, {}

        def place(grp):
            l, name = grp
            nl, r, cols = weights[name].shape
            return _place_shard(weights[name].reshape(nl, 2, r // 2, cols), l, j_arr, BF16, f"place_{name}_l{l}")

        def start_copies(tag, convs, groups, bufs):
            n_c = len(convs)

            def start(thru, _, __, sems):
                x, y, c, chips = _place()
                j_me = 2 * x + y
                copies = []
                for i in range(len(thru)):
                    mine = thru[i].at[j_me] if i < n_c else thru[i].at[j_me, c]
                    copies += [_remote(mine, mine, sems[2 * i].at[k], sems[2 * i + 1].at[k], (*chip, c))
                               for k, chip in enumerate(chips)]
                for cp in copies:
                    cp.start()

            thru, _, sems = _split_call(start, tag, convs + bufs, new_sems=[(3,)] * (2 * (n_c + len(bufs))))
            self.conv_state += [(thru[i], sems[2 * i], sems[2 * i + 1]) for i in range(n_c)]
            for g, grp in enumerate(groups):
                self.state[grp] = (thru[n_c + g], sems[2 * (n_c + g)], sems[2 * (n_c + g) + 1])

        convs = [_place_shard(weights["conv_w"][:, None], l, j_arr, F32, f"place_conv_w_l{l}") for l in range(DEPTH)]
        start_copies("gather_start_first", convs, self.groups[:1], [place(self.groups[0])])
        start_copies("gather_start_rest", [], self.groups[1:], [place(grp) for grp in self.groups[1:]])

    def conv_w(self, l):
        if l not in self.conv_ready:
            buf, send, recv = self.conv_state[l]

            def wait(thru, sems, _, __):
                x, y, c, chips = _place()
                for k, chip in enumerate(chips):
                    mine, theirs = thru[0].at[2 * x + y], thru[0].at[2 * chip[0] + chip[1]]
                    _remote(mine, mine, sems[0].at[k], sems[1].at[k], (*chip, c)).wait_send()
                    _remote(theirs, theirs, sems[0].at[k], sems[1].at[k], (x, y, c)).wait_recv()

            (buf,), _, _ = _split_call(wait, f"gather_conv_w_l{l}", [buf], sems_in=[send, recv])
            self.conv_ready[l] = jnp.transpose(buf[:, 0], (1, 0, 2)).reshape(3, 2 * D_FF)
        return self.conv_ready[l]

    def weight(self, l, name):
        grp = (l, name)
        if grp not in self.ready:
            buf, send, recv = self.state[grp]

            def forward(thru, sems, _, new):
                x, y, c, chips = _place()
                for k, chip in enumerate(chips):
                    landed = thru[0].at[2 * chip[0] + chip[1], c]
                    _remote(landed, landed, new[0].at[k], sems[0].at[k], (x, y, c)).wait_recv()
                    _remote(landed, landed, new[0].at[k], new[1].at[k], (x, y, 1 - c)).start()

            (buf,), _, (fsend, frecv) = _split_call(forward, f"gather_pass_{name}_l{l}", [buf], sems_in=[recv],
                                                    new_sems=[(3,), (3,)])

            def finish(thru, sems, _, __):
                x, y, c, chips = _place()
                mine = thru[0].at[2 * x + y, c]
                for k, chip in enumerate(chips):
                    j_k = 2 * chip[0] + chip[1]
                    theirs, landed = thru[0].at[j_k, 1 - c], thru[0].at[j_k, c]
                    _remote(theirs, theirs, sems[1].at[k], sems[2].at[k], (x, y, c)).wait_recv()
                    _remote(landed, landed, sems[1].at[k], sems[2].at[k], (x, y, 1 - c)).wait_send()
                    _remote(mine, mine, sems[0].at[k], sems[2].at[k], (*chip, c)).wait_send()

            (buf,), _, _ = _split_call(finish, f"gather_done_{name}_l{l}", [buf], sems_in=[send, fsend, frecv])
            r, cols = self.shard_shape[name]
            self.ready[grp] = buf.reshape(N_CHIPS, r, cols) if name in ("w_in", "w_up") else buf.reshape(N_CHIPS * r, cols)
        return self.ready[grp]

    def pair_send(self, l, name, other):
        def start(thru, _, fresh, sems):
            x, y, c, _chips = _place()
            _remote(thru[0], fresh[0], sems[0], sems[1], (x, y, 1 - c)).start()

        (other,), (recv,), sems = _split_call(start, f"pair_start_{name}_l{l}", [other], fresh=[(other.shape, BF16)],
                                             new_sems=[(), ()], after_last=False)
        self.pairs[(l, name)] = (other, recv, sems)

    def pair_recv(self, l, name):
        other, recv, sems = self.pairs.pop((l, name))

        def wait(thru, sems, _, __):
            x, y, c, _chips = _place()
            cp = _remote(thru[0], thru[1], sems[0], sems[1], (x, y, 1 - c))
            cp.wait_send()
            cp.wait_recv()

        (_, recv), _, _ = _split_call(wait, f"pair_done_{name}_l{l}", [other, recv], sems_in=list(sems))
        return recv

    def scatter(self, l, name, p4):
        def start(thru, _, fresh, sems):
            x, y, c, chips = _place()
            for k, chip in enumerate(chips):
                _remote(thru[0].at[2 * chip[0] + chip[1]], fresh[0].at[k], sems[0].at[k], sems[1].at[k], (*chip, c)).start()

        (p4,), (recv3,), sems = _split_call(start, f"chips_start_{name}_l{l}", [p4], fresh=[((3,) + p4.shape[1:], BF16)],
                                           new_sems=[(3,), (3,)], after_last=False)
        self.pending.append(dict(l=l, name=name, stage=2, at=self.tick, bufs=(p4, recv3), sems=sems))

    def _chips(self, grp):
        l, name = grp["l"], grp["name"]

        def wait(thru, sems, _, __):
            x, y, c, chips = _place()
            for k, chip in enumerate(chips):
                cp = _remote(thru[0].at[2 * chip[0] + chip[1]], thru[1].at[k], sems[0].at[k], sems[1].at[k], (*chip, c))
                cp.wait_send()
                cp.wait_recv()

        (p4, recv3), _, _ = _split_call(wait, f"chips_done_{name}_l{l}", list(grp["bufs"]), sems_in=list(grp["sems"]))
        half = _chip_sum(p4, recv3, self.j_arr, self.c_arr, f"chip_sum_{name}_l{l}")

        def start(thru, _, __, sems):
            x, y, c, _chips = _place()
            _remote(thru[0].at[c], thru[0].at[c], sems[0], sems[1], (x, y, 1 - c)).start()

        (half,), _, sems = _split_call(start, f"join_start_{name}_l{l}", [half], new_sems=[(), ()], after_last=False)
        grp.update(stage=3, at=self.tick, bufs=(half,), sems=sems)

    def _update(self, grp):
        l, name = grp["l"], grp["name"]

        def wait(thru, sems, _, __):
            x, y, c, _chips = _place()
            _remote(thru[0].at[c], thru[0].at[c], sems[0], sems[1], (x, y, 1 - c)).wait_send()
            _remote(thru[0].at[1 - c], thru[0].at[1 - c], sems[0], sems[1], (x, y, c)).wait_recv()

        (full,), _, _ = _split_call(wait, f"join_done_{name}_l{l}", list(grp["bufs"]), sems_in=list(grp["sems"]))
        if GRAD_HALVES[name][0] != "cols_of_block":
            full = full.reshape((1,) + tuple(self.shard_shape[name]))
        self.adam[name] = _adamw_layer(self.w[name], full, self.m[name], self.v[name], l, self.adam.get(name),
                                       f"adamw_{name}_l{l}")
        grp.update(stage=4)

    def point(self, drain=False):
        self.tick += 1
        for grp in self.pending:
            if grp["stage"] == 3 and (drain or grp["at"] < self.tick):
                self._update(grp)
            elif grp["stage"] == 2 and (drain or grp["at"] + 3 <= self.tick):
                self._chips(grp)

    def finish(self):
        while any(grp["stage"] < 4 for grp in self.pending):
            self.point(drain=True)
        return self.adam

    @staticmethod
    def _peer(k, x, y, c):
        return (1 - x if k & 4 else x, 1 - y if k & 2 else y, 1 - c if k & 1 else c)

    def small_grads(self, l, grads, loss_tile):
        parts = [grads[nm] for nm in SMALL_NAMES] + ([loss_tile[0, 0:1]] if loss_tile is not None else [])
        packed = _pack_call(parts, f"small_pack_l{l}")
        rows = packed.shape[0]

        def start(thru, _, fresh, sems):
            x, y, c, _chips = _place()
            for k in range(1, 8):
                _remote(thru[0], fresh[0].at[4 * x + 2 * y + c], sems[0].at[k - 1], sems[1].at[k - 1],
                        self._peer(k, x, y, c)).start()

        (packed,), (landed,), sems = _split_call(start, f"small_start_l{l}", [packed], fresh=[((8, rows, PACK_LANES), F32)],
                                                 new_sems=[(7,), (7,)], after_last=False)
        self.small[l] =(packed, landed, sems, [p.shape for p in parts])

    def small_sum(self, l):
        packed, landed, sems, _shapes = self.small[l]

        def wait(thru, sems, _, __):
            x, y, c, _chips = _place()
            for k in range(1, 8):
                px, py, pc = self._peer(k, x, y, c)
                _remote(thru[0], thru[1].at[4 * x + 2 * y + c], sems[0].at[k - 1], sems[1].at[k - 1], (px, py, pc)).wait_send()
                _remote(thru[0], thru[1].at[4 * px + 2 * py + pc], sems[0].at[k - 1], sems[1].at[k - 1], (x, y, c)).wait_recv()

        (packed, landed), _, _ = _split_call(wait, f"small_done_l{l}", [packed, landed], sems_in=list(sems))
        return _sum_devices(packed, landed, self.me_arr, f"small_sum_l{l}")


def _rope_tables(s):
    inv_freq = ROPE_THETA ** (-jnp.arange(0, HEAD_DIM, 2, dtype=F32) / HEAD_DIM)
    ang = jnp.arange(s, dtype=F32)[:, None] * inv_freq[None, :]
    cos, sin = jnp.cos(ang), jnp.sin(ang)
    return jnp.concatenate([cos, cos], axis=-1), jnp.concatenate([-sin, sin], axis=-1)


def _local_step(x, target, ex, small):
    s = x.shape[0]
    cosf, sinf = _rope_tables(s)
    saved = []
    for l in range(DEPTH):
        p = small[l]
        t = f"l{l}"
        h = _rms_fwd(x, p["norm1_g"], f"norm1_{t}")
        z = _matmul(h, ex.weight(l, "w_in"), mode="nn", out_dtype=BF16, tm=1024, tn=896, tk=2048, b_parts=4, name=f"proj_in_{t}")
        qn, kn, vb, ug, vn = _proj_post(z, p["q_norm_g"], p["k_norm_g"], p["sgu_ln_g"], p["sgu_ln_b"], cosf, sinf, f"proj_post_{t}")
        attn, sgu, mixed, probs, psink = _mixer_fwd(qn, kn, vb, ug, vn, p["w_s_bf16"], p["b_s_tile"], p["sink"],
                                                    p["attn_out_g"], p["sgu_out_g"], f"mixer_{t}")
        x1 = _matmul(mixed, ex.weight(l, "w_o"), mode="nn", out_dtype=F32, tm=2048, tn=256, tk=2048, res=x,
                     name=f"proj_out_{t}")
        h2 = _rms_fwd(x1, p["norm2_g"], f"norm2_{t}")
        a_pre = _matmul(h2, ex.weight(l, "w_up"), mode="nn", out_dtype=BF16, tm=1024, tn=1408, tk=2048, b_parts=4,
                        out_parts=2,
                        name=f"ffn_up_{t}")
        act, dgu = _conv_gate_fwd(a_pre, ex.conv_w(l), p["conv_b"], f"conv_gate_{t}")
        x2 = _matmul(act, ex.weight(l, "w_down"), mode="nn", out_dtype=F32, tm=1024, tn=256, tk=D_FF, res=x1,
                     name=f"ffn_down_{t}")
        saved.append(dict(x=x, h=h, z=z, qn=qn, kn=kn, vb=vb, ug=ug, vn=vn, attn=attn, sgu=sgu, mixed=mixed, x1=x1, h2=h2,
                          a_pre=a_pre, act=act, dgu=dgu, probs=probs, psink=psink))
        x = x2
    loss_tile, dx, dxb = _loss_head(x, target, "loss_head")
    for l in reversed(range(DEPTH)):
        p, sv = small[l], saved[l]
        t = f"l{l}"
        def weight_grad(name, a, g, between, g_parts=0):
            ex.pair_send(l, name, _grad_half(name, a, g, ex.o_arr, None, f"g_{name}_other_{t}", g_parts))
            out = between()
            ex.scatter(l, name, _grad_half(name, a, g, ex.c_arr, ex.pair_recv(l, name), f"g_{name}_own_{t}", g_parts))
            ex.point()
            return out

        def after_down():
            dact = _matmul(dxb, ex.weight(l, "w_down"), mode="nt", out_dtype=BF16, tm=1024, tn=512, tk=2048,
                           name=f"d_act_{t}")
            return _conv_gate_bwd(sv["a_pre"], sv["dgu"], ex.conv_w(l), dact, f"conv_gate_bwd_{t}")

        dap, dcw, dcb = weight_grad("w_down", sv["act"], dxb, after_down)

        def after_up():
            dh2 = _matmul(dap, ex.weight(l, "w_up"), mode="nt", out_dtype=F32, tm=1024, tn=1024, tk=2816, a_parts=2,
                          b_parts=4, name=f"d_h2_{t}")
            return _rms_bwd(sv["x1"], p["norm2_g"], dh2, dx, f"norm2_bwd_{t}")

        dx1, dx1b, dg2 = weight_grad("w_up", sv["h2"], dap, after_up, g_parts=2)
        ex.pair_send(l, "w_o", _grad_half("w_o", sv["mixed"], dx1b, ex.o_arr, None, f"g_w_o_other_{t}"))
        dmixed = _matmul(dx1b, ex.weight(l, "w_o"), mode="nt", out_dtype=F32, tm=1024, tn=512, tk=2048,
                         name=f"d_mixed_{t}")
        dqn, dkn, dvb, dug, dvn, dws, dbs, dsk, dga, dgs = _mixer_bwd(
            sv["qn"], sv["kn"], sv["vb"], sv["ug"], sv["vn"], sv["attn"], sv["sgu"], dmixed, p["w_s_bf16"], p["b_s_tile"],
            p["attn_out_g"], p["sgu_out_g"], sv["probs"], sv["psink"], f"mixer_bwd_{t}")
        dz, dqg, dkg, dlg, dlb = _proj_post_bwd(sv["z"], dqn, dkn, dvb, dug, dvn, p["q_norm_g"], p["k_norm_g"], p["sgu_ln_g"],
                                                 cosf, sinf, f"proj_post_bwd_{t}")
        ex.scatter(l, "w_o", _grad_half("w_o", sv["mixed"], dx1b, ex.c_arr, ex.pair_recv(l, "w_o"), f"g_w_o_own_{t}"))
        ex.point()

        def after_in():
            dh = _matmul_nt_slabs(dz, ex.weight(l, "w_in"), tm=1024, tn=512, name=f"d_h_{t}")
            return _rms_bwd(sv["x"], p["norm1_g"], dh, dx1, f"norm1_bwd_{t}")

        dx, dxb, dg1 = weight_grad("w_in", sv["h"], dz, after_in)
        ex.small_grads(l, dict(
            norm1_g=dg1[0], q_norm_g=dqg[0], k_norm_g=dkg[0], sink=dsk[:, 0], sgu_ln_g=dlg[0], sgu_ln_b=dlb[0], w_s=dws,
            b_s=dbs[:, :, 0], attn_out_g=dga[0], sgu_out_g=dgs[0], norm2_g=dg2[0],
            conv_w=jnp.concatenate([dcw[0], dcw[1]], axis=-1), conv_b=jnp.concatenate([dcb[0, 0], dcb[1, 0]], axis=-1)),
            loss_tile if l == 0 else None)
    return dx


def _small_views(l, norm1_g, q_norm_g, k_norm_g, sink, sgu_ln_g, sgu_ln_b, w_s, b_s, attn_out_g, sgu_out_g, norm2_g, conv_b):
    return dict(
        norm1_g=norm1_g[l][None], q_norm_g=q_norm_g[l][None], k_norm_g=k_norm_g[l][None], sink=sink[l],
        sgu_ln_g=sgu_ln_g[l][None], sgu_ln_b=sgu_ln_b[l][None], w_s_bf16=w_s[l].astype(BF16),
        b_s_tile=jnp.broadcast_to(b_s[l][:, :, None], (N_GMLP_HEADS, BLOCK, BLOCK)), attn_out_g=attn_out_g[l][None],
        sgu_out_g=sgu_out_g[l][None], norm2_g=norm2_g[l][None], conv_b=conv_b[l][None])


SMALL_NAMES = ("norm1_g", "q_norm_g", "k_norm_g", "sink", "sgu_ln_g", "sgu_ln_b", "w_s", "b_s", "attn_out_g", "sgu_out_g",
               "norm2_g", "conv_b", "conv_w")
REPLICATED_NAMES = SMALL_NAMES[:-1]
BIG_NAMES = ("w_in", "w_o", "w_up", "w_down")
PACK_LANES = 128
PACK_ALIGN = 8 * PACK_LANES


def _pack_rows(shape):
    return -(-math.prod(shape) // PACK_ALIGN) * 8


def _pack_parts(arrays):
    parts = []
    for a in arrays:
        flat = a.reshape(-1)
        parts.append(jnp.pad(flat, (0, _pack_rows(a.shape) * PACK_LANES - flat.shape[0])).reshape(-1, PACK_LANES))
    return parts


def _pack_call(arrays, name):
    parts = _pack_parts(arrays)
    total = sum(p.shape[0] for p in parts)

    def body(*refs):
        o_ref, at = refs[-1], 0
        for p_ref in refs[:-1]:
            o_ref[at:at + p_ref.shape[0], :] = p_ref[...]
            at += p_ref.shape[0]

    vm = pl.BlockSpec(memory_space=pltpu.VMEM)
    return _ordered_call(
        body, name=name, out_shape=jax.ShapeDtypeStruct((total, PACK_LANES), F32), in_specs=[vm] * len(parts), out_specs=vm,
        compiler_params=pltpu.CompilerParams(vmem_limit_bytes=V7X_VMEM_LIMIT),
    )(*parts)


def _unpack_layers(stacked, shapes):
    nl = stacked.shape[0]
    out, at = [], 0
    for shp in shapes:
        rows = _pack_rows(shp)
        out.append(stacked[:, at:at + rows].reshape(nl, -1)[:, :math.prod(shp)].reshape((nl,) + tuple(shp)))
        at += rows
    return out


def _adamw_packed(w, g, m, v, rows, layer, into, name):
    head = pl.BlockSpec((rows, PACK_LANES), lambda i: (0, 0))
    at_layer = pl.BlockSpec((None, rows, PACK_LANES), lambda i: (layer, 0, 0))

    def body(w_ref, g_ref, m_ref, v_ref, *rest):
        d_ref, nm_ref, nv_ref = rest[-3:]
        gv = g_ref[...]
        mn = ADAM_B1 * m_ref[...] + (1.0 - ADAM_B1) * gv
        vn = ADAM_B2 * v_ref[...] + (1.0 - ADAM_B2) * (gv * gv)
        m_hat = mn / (1.0 - ADAM_B1 ** ADAM_STEP)
        v_hat = vn / (1.0 - ADAM_B2 ** ADAM_STEP)
        d_ref[...] = -ADAM_LR * (m_hat / (jnp.sqrt(v_hat) + ADAM_EPS) + ADAM_WD * w_ref[...])
        nm_ref[...] = mn
        nv_ref[...] = vn

    in_specs = [head] * 4
    operands = [w, g, m, v]
    aliases = {}
    if into is not None:
        in_specs += [ANY] * 3
        operands += list(into)
        aliases = {4 + i: i for i in range(3)}
    sds = jax.ShapeDtypeStruct((DEPTH, rows, PACK_LANES), F32)
    return _ordered_call(
        body, name=name, out_shape=(sds,) * 3, grid=(1,), in_specs=in_specs, out_specs=(at_layer,) * 3,
        input_output_aliases=aliases, compiler_params=_params(("arbitrary",)),
    )(*operands)


def kernel(x, norm1_g, w_in, q_norm_g, k_norm_g, sink, sgu_ln_g, sgu_ln_b, w_s, b_s, attn_out_g, sgu_out_g, w_o, norm2_g, w_up, conv_w, conv_b, w_down, loss_target, m_norm1_g, m_w_in, m_q_norm_g, m_k_norm_g, m_sink, m_sgu_ln_g, m_sgu_ln_b, m_w_s, m_b_s, m_attn_out_g, m_sgu_out_g, m_w_o, m_norm2_g, m_w_up, m_conv_w, m_conv_b, m_w_down, v_norm1_g, v_w_in, v_q_norm_g, v_k_norm_g, v_sink, v_sgu_ln_g, v_sgu_ln_b, v_w_s, v_b_s, v_attn_out_g, v_sgu_out_g, v_w_o, v_norm2_g, v_w_up, v_conv_w, v_conv_b, v_w_down):
    weights = dict(norm1_g=norm1_g, w_in=w_in, q_norm_g=q_norm_g, k_norm_g=k_norm_g, sink=sink, sgu_ln_g=sgu_ln_g,
                   sgu_ln_b=sgu_ln_b, w_s=w_s, b_s=b_s, attn_out_g=attn_out_g, sgu_out_g=sgu_out_g, w_o=w_o, norm2_g=norm2_g,
                   w_up=w_up, conv_w=conv_w, conv_b=conv_b, w_down=w_down)
    m_in = dict(norm1_g=m_norm1_g, w_in=m_w_in, q_norm_g=m_q_norm_g, k_norm_g=m_k_norm_g, sink=m_sink, sgu_ln_g=m_sgu_ln_g,
                sgu_ln_b=m_sgu_ln_b, w_s=m_w_s, b_s=m_b_s, attn_out_g=m_attn_out_g, sgu_out_g=m_sgu_out_g, w_o=m_w_o,
                norm2_g=m_norm2_g, w_up=m_w_up, conv_w=m_conv_w, conv_b=m_conv_b, w_down=m_w_down)
    v_in = dict(norm1_g=v_norm1_g, w_in=v_w_in, q_norm_g=v_q_norm_g, k_norm_g=v_k_norm_g, sink=v_sink, sgu_ln_g=v_sgu_ln_g,
                sgu_ln_b=v_sgu_ln_b, w_s=v_w_s, b_s=v_b_s, attn_out_g=v_attn_out_g, sgu_out_g=v_sgu_out_g, w_o=v_w_o,
                norm2_g=v_norm2_g, w_up=v_w_up, conv_w=v_conv_w, conv_b=v_conv_b, w_down=v_w_down)
    cx, cy, cc = lax.axis_index("x"), lax.axis_index("y"), lax.axis_index("c")
    j_me = 2 * cx + cy
    c_arr = jnp.reshape(cc, (1,)).astype(jnp.int32)
    j_arr = jnp.reshape(j_me, (1,)).astype(jnp.int32)

    _Order.last = None
    ex = _Exchange(weights, m_in, v_in, j_arr, c_arr, jnp.reshape(4 * cx + 2 * cy + cc, (1,)).astype(jnp.int32))
    small = [_small_views(l, norm1_g, q_norm_g, k_norm_g, sink, sgu_ln_g, sgu_ln_b, w_s, b_s, attn_out_g, sgu_out_g, norm2_g,
                          conv_b) for l in range(DEPTH)]
    packed_in = [[_pack_call([src[nm][l] for nm in REPLICATED_NAMES], f"pack_{tag}_l{l}")
                  for tag, src in (("w", weights), ("m", m_in), ("v", v_in))] for l in range(DEPTH)]
    dx = _local_step(x[0], loss_target[0], ex, small)
    big_out = ex.finish()

    rep_shapes = [weights[nm].shape[1:] for nm in REPLICATED_NAMES]
    rep_rows = sum(_pack_rows(shp) for shp in rep_shapes)
    cw_shape = (3, 2 * D_FF)
    sums, adam_small = [None] * DEPTH, None
    for l in reversed(range(DEPTH)):
        sums[l] = ex.small_sum(l)
        pw, pm, pv = packed_in[l]
        adam_small = _adamw_packed(pw, sums[l], pm, pv, rep_rows, l, adam_small, f"adamw_small_l{l}")
    cw_rows = _pack_rows(cw_shape)
    loss = sums[0][rep_rows + cw_rows, 0]
    stacked = jnp.stack([sm[:rep_rows + cw_rows] for sm in sums])
    grads = dict(zip(REPLICATED_NAMES, _unpack_layers(stacked[:, :rep_rows], rep_shapes)))
    delta, new_m, new_v = (dict(zip(REPLICATED_NAMES, _unpack_layers(arr, rep_shapes))) for arr in adam_small)
    cw_cols = 2 * D_FF // N_CHIPS
    cw_grad = lax.dynamic_slice_in_dim(_unpack_layers(stacked[:, rep_rows:], [cw_shape])[0], j_me * cw_cols, cw_cols, axis=2)
    flat = lambda a: a.reshape(DEPTH * 3, cw_cols)
    cw_out = _adamw(flat(conv_w), flat(cw_grad), flat(m_conv_w), flat(v_conv_w), "adamw_conv_w")
    grads["conv_w"], delta["conv_w"], new_m["conv_w"], new_v["conv_w"] = (a.reshape(DEPTH, 3, cw_cols) for a in cw_out)

    for name in BIG_NAMES:
        grads[name], delta[name], new_m[name], new_v[name] = big_out[name]

    order = ("norm1_g", "w_in", "q_norm_g", "k_norm_g", "sink", "sgu_ln_g", "sgu_ln_b", "w_s", "b_s", "attn_out_g", "sgu_out_g",
             "w_o", "norm2_g", "w_up", "conv_w", "conv_b", "w_down")
    return (loss, dx[None], *[grads[nm] for nm in order], *[delta[nm] for nm in order], *[new_m[nm] for nm in order],
            *[new_v[nm] for nm in order])
```

```python
import math

import jax
import jax.numpy as jnp
from jax import lax
from jax.experimental import pallas as pl
from jax.experimental.pallas import tpu as pltpu

F32 = jnp.float32
BF16 = jnp.bfloat16

D_MODEL = 2048
HEAD_DIM = 128
ATTN_WIDTH = 1024
N_Q_HEADS = 8
N_KV_HEADS = 2
GQA_GROUP = 4
KV_WIDTH = 256
GMLP_WIDTH = 1024
N_GMLP_HEADS = 8
BLOCK = 128
IN_WIDTH = 3584
D_FF = 5632
DEPTH = 2
EPS = 1e-6
MASK_VALUE = -1e30
ROPE_THETA = 10000.0
N_CHIPS = 4

ADAM_LR = 0.001
ADAM_B1 = 0.9
ADAM_B2 = 0.999
ADAM_EPS = 1e-08
ADAM_WD = 0.01
ADAM_STEP = 10

V7X_VMEM_LIMIT = 48 * 1024 * 1024
MESH = pl.DeviceIdType.MESH

_GELU_C = math.sqrt(2.0 / math.pi)
_GELU_A = 0.044715


def _params(sem=None):
    return pltpu.CompilerParams(dimension_semantics=sem, vmem_limit_bytes=V7X_VMEM_LIMIT)


ANY = pl.BlockSpec(memory_space=pl.ANY)


class _Order:
    last = None


def _ordered_call(body, *, token_index=0, **kw):
    def run(*operands):
        tok = _Order.last
        if tok is None or any(op is tok for op in operands):
            call = pl.pallas_call(body, **kw)
        else:
            n_in = len(operands)

            def ordered_body(*refs):
                return body(*refs[:n_in], *refs[n_in + 1:])

            kw2 = dict(kw)
            if "grid_spec" in kw2:
                gs = kw2["grid_spec"]
                kw2["grid_spec"] = pltpu.PrefetchScalarGridSpec(
                    num_scalar_prefetch=gs.num_scalar_prefetch, grid=gs.grid, in_specs=list(gs.in_specs) + [ANY],
                    out_specs=gs.out_specs, scratch_shapes=gs.scratch_shapes)
            else:
                kw2["in_specs"] = list(kw2["in_specs"]) + [ANY]
            call = pl.pallas_call(ordered_body, **kw2)
            operands = operands + (tok,)
        out = call(*operands)
        _Order.last = out[token_index] if isinstance(out, (tuple, list)) else out
        return out

    return run


def _gelu(x):
    return x * (0.5 * (1.0 + jnp.tanh(_GELU_C * (x + _GELU_A * (x * x * x)))))


def _gelu_grad(x):
    x2 = x * x
    t = jnp.tanh(_GELU_C * (x + _GELU_A * (x * x2)))
    return 0.5 * (1.0 + t) + 0.5 * x * (1.0 - t * t) * (_GELU_C * (1.0 + 3.0 * _GELU_A * x2))


def _mean_last(x):
    return jnp.mean(x, axis=-1, keepdims=True)


def _sum_rows(x):
    return jnp.sum(x, axis=0, keepdims=True)


def _sum_all(x):
    return jnp.sum(jnp.sum(x, axis=1, keepdims=True), axis=0, keepdims=True)


def _matmul(a, b, *, mode, out_dtype, tm, tn, tk, name, res=None, a_parts=0, b_parts=0, out_parts=0):
    assert mode in ("nn", "nt"), mode
    if mode == "nn":
        assert not a_parts
        m, k = a.shape
        n = b.shape[0] * b.shape[2] if b_parts else b.shape[1]
    else:
        m, k = (a.shape[1], a.shape[0] * a.shape[2]) if a_parts else a.shape
        n = b.shape[1] if b_parts else b.shape[0]
    tm, tn, tk = min(tm, m), min(tn, n), min(tk, k)
    assert m % tm == 0 and n % tn == 0 and k % tk == 0, (name, m, n, k, tm, tn, tk)
    nm, nn, nk = m // tm, n // tn, k // tk

    def slab(idx, total_tiles, parts):
        per = total_tiles // parts
        assert per * parts == total_tiles, (name, total_tiles, parts)
        return idx // per, idx % per

    if mode == "nn":
        a_spec = pl.BlockSpec((tm, tk), lambda i, j, kk: (i, kk))
        if b_parts:
            b_spec = pl.BlockSpec((None, tk, tn), lambda i, j, kk: (slab(j, nn, b_parts)[0], kk, slab(j, nn, b_parts)[1]))
        else:
            b_spec = pl.BlockSpec((tk, tn), lambda i, j, kk: (kk, j))
        dims = (((1,), (0,)), ((), ()))
    else:
        if a_parts:
            a_spec = pl.BlockSpec((None, tm, tk), lambda i, j, kk: (slab(kk, nk, a_parts)[0], i, slab(kk, nk, a_parts)[1]))
        else:
            a_spec = pl.BlockSpec((tm, tk), lambda i, j, kk: (i, kk))
        if b_parts:
            b_spec = pl.BlockSpec((None, tn, tk), lambda i, j, kk: (slab(kk, nk, b_parts)[0], j, slab(kk, nk, b_parts)[1]))
        else:
            b_spec = pl.BlockSpec((tn, tk), lambda i, j, kk: (j, kk))
        dims = (((1,), (1,)), ((), ()))
    if out_parts:
        out_shape = jax.ShapeDtypeStruct((out_parts, m, n // out_parts), out_dtype)
        out_spec = pl.BlockSpec((None, tm, tn), lambda i, j, kk: (slab(j, nn, out_parts)[0], i, slab(j, nn, out_parts)[1]))
    else:
        out_shape = jax.ShapeDtypeStruct((m, n), out_dtype)
        out_spec = pl.BlockSpec((tm, tn), lambda i, j, kk: (i, j))
    in_specs = [a_spec, b_spec]
    operands = [a, b]
    if res is not None:
        in_specs.append(pl.BlockSpec((tm, tn), lambda i, j, kk: (i, j)))
        operands.append(res)

    def body(*refs):
        a_ref, b_ref = refs[0], refs[1]
        res_ref = refs[2] if res is not None else None
        o_ref = refs[3] if res is not None else refs[2]
        p = lax.dot_general(a_ref[...], b_ref[...], dims, preferred_element_type=F32)

        def finish(total):
            if res_ref is not None:
                total = res_ref[...] + total
            o_ref[...] = total.astype(out_dtype)

        if nk == 1:
            finish(p)
        else:
            acc_ref = refs[-1]
            kk = pl.program_id(2)

            @pl.when(kk == 0)
            def _():
                acc_ref[...] = p

            @pl.when(jnp.logical_and(kk > 0, kk < nk - 1))
            def _():
                acc_ref[...] += p

            @pl.when(kk == nk - 1)
            def _():
                finish(acc_ref[...] + p)

    scratch = [pltpu.VMEM((tm, tn), F32)] if nk > 1 else []
    return _ordered_call(
        body, name=name, out_shape=out_shape, grid=(nm, nn, nk), in_specs=in_specs, out_specs=out_spec,
        scratch_shapes=scratch, compiler_params=_params(("parallel", "parallel", "arbitrary")),
    )(*operands)


def _matmul_nt_slabs(a, b, *, tm, tn, name, a_parts=0):
    nslab, n, ks = b.shape
    m = a.shape[1] if a_parts else a.shape[0]
    tm, tn = min(tm, m), min(tn, n)
    assert m % tm == 0 and n % tn == 0, (name, m, n, tm, tn)
    if a_parts:
        per = nslab // a_parts
        assert per * a_parts == nslab and a.shape[2] == per * ks, (name, a.shape, b.shape)
        a_spec = pl.BlockSpec((a_parts, tm, per * ks), lambda i, j: (0, i, 0))
    else:
        assert a.shape[1] == nslab * ks, (name, a.shape, b.shape)
        a_spec = pl.BlockSpec((tm, nslab * ks), lambda i, j: (i, 0))

    def body(a_ref, b_ref, o_ref):
        total = None
        for sl in range(nslab):
            if a_parts:
                a_sl = a_ref[sl // per, :, (sl % per) * ks:(sl % per + 1) * ks]
            else:
                a_sl = a_ref[:, sl * ks:(sl + 1) * ks]
            p = lax.dot_general(a_sl, b_ref[sl], (((1,), (1,)), ((), ())), preferred_element_type=F32)
            total = p if total is None else total + p
        o_ref[...] = total

    return _ordered_call(
        body, name=name, out_shape=jax.ShapeDtypeStruct((m, n), F32), grid=(m // tm, n // tn),
        in_specs=[a_spec, pl.BlockSpec((nslab, tn, ks), lambda i, j: (0, j, 0))],
        out_specs=pl.BlockSpec((tm, tn), lambda i, j: (i, j)), compiler_params=_params(("parallel", "parallel")),
    )(a, b)


GRAD_HALVES = {
    "w_in": ("rows_of_slab", 1024, 896), "w_up": ("rows_of_slab", 1024, 1408), "w_o": ("rows_of_block", 256, 2048),
    "w_down": ("cols_of_block", 1408, 512)}


def _half_shape(name, shard_shape):
    r, cols = shard_shape
    return (r, cols // 2) if GRAD_HALVES[name][0] == "cols_of_block" else (r // 2, cols)


def _grad_half(name, a, g, sel, res, call_name, g_parts=0):
    kind, tm, tn = GRAD_HALVES[name]
    s, m = a.shape
    n = g.shape[0] * g.shape[2] if g_parts else g.shape[1]
    if kind == "rows_of_slab":
        rh, hc = m // 2, n // N_CHIPS
        per = hc // tn
        grid = (rh // tm, n // tn)
        a_map = lambda i, j, sel_ref: (0, sel_ref[0] * (rh // tm) + i)
        g_col = lambda i, j, sel_ref: j
        o_map = lambda i, j, sel_ref: (j // per, i, j % per)
    elif kind == "rows_of_block":
        rh, hc = m // N_CHIPS // 2, n
        assert tm == rh
        grid = (N_CHIPS, n // tn)
        a_map = lambda i, j, sel_ref: (0, 2 * i + sel_ref[0])
        g_col = lambda i, j, sel_ref: j
        o_map = lambda i, j, sel_ref: (i, 0, j)
    else:
        rh, hc = m // N_CHIPS, n // 2
        assert tm == rh
        grid = (N_CHIPS, hc // tn)
        a_map = lambda i, j, sel_ref: (0, i)
        g_col = lambda i, j, sel_ref: sel_ref[0] * (hc // tn) + j
        o_map = lambda i, j, sel_ref: (i, 0, j)
    if g_parts:
        g_per = (n // tn) // g_parts
        g_spec = pl.BlockSpec((None, s, tn), lambda i, j, sel_ref: (g_col(i, j, sel_ref) // g_per, 0, g_col(i, j, sel_ref) % g_per))
    else:
        g_spec = pl.BlockSpec((s, tn), lambda i, j, sel_ref: (0, g_col(i, j, sel_ref)))
    o_spec = pl.BlockSpec((None, tm, tn), o_map)
    in_specs = [pl.BlockSpec((s, tm), a_map), g_spec] + ([o_spec] if res is not None else [])

    def body(sel_ref, a_ref, g_ref, *rest):
        o_ref = rest[-1]
        p = lax.dot_general(a_ref[...], g_ref[...], (((0,), (0,)), ((), ())), preferred_element_type=F32)
        if res is not None:
            p = p + rest[0][...].astype(F32)
        o_ref[...] = p.astype(BF16)

    grid_spec = pltpu.PrefetchScalarGridSpec(num_scalar_prefetch=1, grid=grid, in_specs=in_specs, out_specs=o_spec)
    return _ordered_call(
        body, name=call_name, out_shape=jax.ShapeDtypeStruct((N_CHIPS, rh, hc), BF16), grid_spec=grid_spec,
        compiler_params=_params(("parallel", "parallel")),
    )(sel, a, g, *([res] if res is not None else []))


def _row_tile(s):
    return min(256, s)


def _rows(width, tr):
    return pl.BlockSpec((tr, width), lambda i: (i, 0))


def _const2(shape):
    return pl.BlockSpec(shape, lambda i: (0, 0))


def _rms_fwd(x, g, name):
    s, d = x.shape
    tr = _row_tile(s)

    def body(x_ref, g_ref, o_ref):
        xv = x_ref[...]
        r = lax.rsqrt(_mean_last(xv * xv) + EPS)
        o_ref[...] = (xv * r * g_ref[...]).astype(BF16)

    return _ordered_call(
        body, name=name, out_shape=jax.ShapeDtypeStruct((s, d), BF16), grid=(s // tr,),
        in_specs=[_rows(d, tr), _const2((1, d))], out_specs=_rows(d, tr), compiler_params=_params(("parallel",)),
    )(x, g)


def _rms_bwd(x, g, dh, dres, name):
    s, d = x.shape
    tr = _row_tile(s)

    def body(x_ref, g_ref, dh_ref, dres_ref, dx_ref, dxb_ref, dg_ref):
        xv, dy = x_ref[...], dh_ref[...]
        r = lax.rsqrt(_mean_last(xv * xv) + EPS)
        gdy = dy * g_ref[...]
        dx = dres_ref[...] + r * gdy - xv * ((r * r * r) * _mean_last(xv * gdy))
        dx_ref[...] = dx
        dxb_ref[...] = dx.astype(BF16)

        @pl.when(pl.program_id(0) == 0)
        def _():
            dg_ref[...] = jnp.zeros_like(dg_ref)

        dg_ref[...] += _sum_rows(xv * r * dy)

    return _ordered_call(
        body, name=name,
        out_shape=(jax.ShapeDtypeStruct((s, d), F32), jax.ShapeDtypeStruct((s, d), BF16), jax.ShapeDtypeStruct((1, d), F32)),
        grid=(s // tr,), in_specs=[_rows(d, tr), _const2((1, d)), _rows(d, tr), _rows(d, tr)],
        out_specs=(_rows(d, tr), _rows(d, tr), _const2((1, d))), compiler_params=_params(("arbitrary",)),
    )(x, g, dh, dres)


Q0, K0, V0, GU0, GV0 = 0, ATTN_WIDTH, ATTN_WIDTH + KV_WIDTH, ATTN_WIDTH + 2 * KV_WIDTH, ATTN_WIDTH + 2 * KV_WIDTH + GMLP_WIDTH


def _head(h, base=0):
    return slice(base + h * HEAD_DIM, base + (h + 1) * HEAD_DIM)


def _proj_post(z, qg, kg, lg, lb, cosf, sinf, name):
    s = z.shape[0]
    tr = _row_tile(s)

    def body(z_ref, qg_ref, kg_ref, lg_ref, lb_ref, cos_ref, sin_ref, qn_ref, kn_ref, vb_ref, ug_ref, vn_ref,
             dgu_ref, dgv_ref, xhat_ref, rstd_ref):
        cos, sin = cos_ref[...], sin_ref[...]

        def norm_rope(xh, g):
            y = xh * lax.rsqrt(_mean_last(xh * xh) + EPS) * g
            return y * cos + pltpu.roll(y, HEAD_DIM // 2, 1) * sin

        for h in range(N_Q_HEADS):
            qn_ref[:, _head(h)] = norm_rope(z_ref[:, _head(h, Q0)].astype(F32), qg_ref[...]).astype(BF16)
        for h in range(N_KV_HEADS):
            kn_ref[:, _head(h)] = norm_rope(z_ref[:, _head(h, K0)].astype(F32), kg_ref[...]).astype(BF16)
        vb_ref[...] = z_ref[:, V0:GU0]
        gu = z_ref[:, GU0:GV0].astype(F32)
        ug_ref[...] = _gelu(gu)
        dgu_ref[...] = _gelu_grad(gu).astype(BF16)
        gv = z_ref[:, GV0:IN_WIDTH].astype(F32)
        vg = _gelu(gv)
        dgv_ref[...] = _gelu_grad(gv).astype(BF16)
        xc = vg - _mean_last(vg)
        r = lax.rsqrt(_mean_last(xc * xc) + EPS)
        y = xc * r
        xhat_ref[...] = y.astype(BF16)
        rstd_ref[...] = r
        vn_ref[...] = (y * lg_ref[...] + lb_ref[...]).astype(BF16)

    wide = jax.ShapeDtypeStruct((s, GMLP_WIDTH), BF16)
    return _ordered_call(
        body, name=name,
        out_shape=(jax.ShapeDtypeStruct((s, ATTN_WIDTH), BF16), jax.ShapeDtypeStruct((s, KV_WIDTH), BF16),
                   jax.ShapeDtypeStruct((s, KV_WIDTH), BF16), jax.ShapeDtypeStruct((s, GMLP_WIDTH), F32), wide,
                   wide, wide, wide, jax.ShapeDtypeStruct((s, 1), F32)),
        grid=(s // tr,),
        in_specs=[_rows(IN_WIDTH, tr), _const2((1, HEAD_DIM)), _const2((1, HEAD_DIM)), _const2((1, GMLP_WIDTH)),
                  _const2((1, GMLP_WIDTH)), _rows(HEAD_DIM, tr), _rows(HEAD_DIM, tr)],
        out_specs=(_rows(ATTN_WIDTH, tr), _rows(KV_WIDTH, tr), _rows(KV_WIDTH, tr), _rows(GMLP_WIDTH, tr), _rows(GMLP_WIDTH, tr),
                   _rows(GMLP_WIDTH, tr), _rows(GMLP_WIDTH, tr), _rows(GMLP_WIDTH, tr), _rows(1, tr)),
        compiler_params=_params(("parallel",)),
    )(z, qg, kg, lg, lb, cosf, sinf)


def _proj_post_bwd(z, dqn, dkn, dvb, dug, dvn, gelu_grad_u, gelu_grad_v, xhat_v, rstd_v, qg, kg, lg, cosf, sinf, name):
    s = z.shape[0]
    tr = _row_tile(s)

    def body(z_ref, dqn_ref, dkn_ref, dvb_ref, dug_ref, dvn_ref, ggu_ref, ggv_ref, xhat_ref, rstd_ref, qg_ref, kg_ref, lg_ref,
             cos_ref, sin_ref, dz_ref, dqg_ref, dkg_ref, dlg_ref, dlb_ref):
        cos, sin = cos_ref[...], sin_ref[...]

        @pl.when(pl.program_id(0) == 0)
        def _():
            dqg_ref[...] = jnp.zeros_like(dqg_ref)
            dkg_ref[...] = jnp.zeros_like(dkg_ref)
            dlg_ref[...] = jnp.zeros_like(dlg_ref)
            dlb_ref[...] = jnp.zeros_like(dlb_ref)

        def norm_rope_bwd(xh, g, dout):
            dy = dout * cos - pltpu.roll(dout, HEAD_DIM // 2, 1) * sin
            r = lax.rsqrt(_mean_last(xh * xh) + EPS)
            xhat = xh * r
            gdy = dy * g
            return r * (gdy - xhat * _mean_last(xhat * gdy)), _sum_rows(xhat * dy)

        dqg = jnp.zeros((1, HEAD_DIM), F32)
        for h in range(N_Q_HEADS):
            dx, dg = norm_rope_bwd(z_ref[:, _head(h, Q0)].astype(F32), qg_ref[...], dqn_ref[:, _head(h)])
            dz_ref[:, _head(h, Q0)] = dx.astype(BF16)
            dqg = dqg + dg
        dqg_ref[...] += dqg
        dkg = jnp.zeros((1, HEAD_DIM), F32)
        for h in range(N_KV_HEADS):
            dx, dg = norm_rope_bwd(z_ref[:, _head(h, K0)].astype(F32), kg_ref[...], dkn_ref[:, _head(h)])
            dz_ref[:, _head(h, K0)] = dx.astype(BF16)
            dkg = dkg + dg
        dkg_ref[...] += dkg
        dz_ref[:, V0:GU0] = dvb_ref[...].astype(BF16)
        dz_ref[:, GU0:GV0] = (dug_ref[...] * ggu_ref[...].astype(F32)).astype(BF16)
        xhat = xhat_ref[...].astype(F32)
        dvn_v = dvn_ref[...]
        dlg_ref[...] += _sum_rows(xhat * dvn_v)
        dlb_ref[...] += _sum_rows(dvn_v)
        dxh = dvn_v * lg_ref[...]
        dvg = rstd_ref[...] * (dxh - _mean_last(dxh) - xhat * _mean_last(dxh * xhat))
        dz_ref[:, GV0:IN_WIDTH] = (dvg * ggv_ref[...].astype(F32)).astype(BF16)

    return _ordered_call(
        body, name=name,
        out_shape=(jax.ShapeDtypeStruct((s, IN_WIDTH), BF16), jax.ShapeDtypeStruct((1, HEAD_DIM), F32),
                   jax.ShapeDtypeStruct((1, HEAD_DIM), F32), jax.ShapeDtypeStruct((1, GMLP_WIDTH), F32),
                   jax.ShapeDtypeStruct((1, GMLP_WIDTH), F32)),
        grid=(s // tr,),
        in_specs=[_rows(V0, tr), _rows(ATTN_WIDTH, tr), _rows(KV_WIDTH, tr), _rows(KV_WIDTH, tr), _rows(GMLP_WIDTH, tr),
                  _rows(GMLP_WIDTH, tr), _rows(GMLP_WIDTH, tr), _rows(GMLP_WIDTH, tr), _rows(GMLP_WIDTH, tr), _rows(1, tr),
                  _const2((1, HEAD_DIM)), _const2((1, HEAD_DIM)), _const2((1, GMLP_WIDTH)), _rows(HEAD_DIM, tr),
                  _rows(HEAD_DIM, tr)],
        out_specs=(_rows(IN_WIDTH, tr), _const2((1, HEAD_DIM)), _const2((1, HEAD_DIM)), _const2((1, GMLP_WIDTH)),
                   _const2((1, GMLP_WIDTH))),
        compiler_params=_params(("arbitrary",)),
    )(z, dqn, dkn, dvb, dug, dvn, gelu_grad_u, gelu_grad_v, xhat_v, rstd_v, qg, kg, lg, cosf, sinf)


def _band_valid(n, s):
    shape = (GQA_GROUP * BLOCK, 3 * BLOCK)
    i = lax.broadcasted_iota(jnp.int32, shape, 0) & (BLOCK - 1)
    j = lax.broadcasted_iota(jnp.int32, shape, 1)
    k_pos = n * BLOCK - BLOCK + j
    return (jnp.abs(j - BLOCK - i) <= BLOCK) & (k_pos >= 0) & (k_pos < s)


def _group_rows(x, kh):
    return jnp.concatenate([x[:, _head(kh * GQA_GROUP + g)] for g in range(GQA_GROUP)], axis=0)


def _group_sinks(sink_ref, kh):
    return jnp.concatenate([jnp.full((BLOCK, 1), sink_ref[kh * GQA_GROUP + g], F32) for g in range(GQA_GROUP)], axis=0)


def _rows_of(x, g):
    return x[g * BLOCK:(g + 1) * BLOCK]


def _probs(q, kb, sink_h, valid):
    sc = lax.dot_general(q, kb, (((1,), (1,)), ((), ())), preferred_element_type=F32) * (HEAD_DIM ** -0.5)
    sc = jnp.where(valid, sc, MASK_VALUE)
    m = jnp.maximum(jnp.max(sc, axis=-1, keepdims=True), sink_h)
    p = jnp.exp(sc - m)
    es = jnp.exp(sink_h - m)
    den = jnp.sum(p, axis=-1, keepdims=True) + es
    inv = 1.0 / den
    return p * inv, es * inv


def _band_specs(width, nb):
    return [pl.BlockSpec((BLOCK, width), lambda n: (jnp.maximum(n - 1, 0), 0)),
            pl.BlockSpec((BLOCK, width), lambda n: (n, 0)),
            pl.BlockSpec((BLOCK, width), lambda n: (jnp.minimum(n + 1, nb - 1), 0))]


def _blk(width):
    return pl.BlockSpec((BLOCK, width), lambda n: (n, 0))


def _whole3(shape):
    return pl.BlockSpec(shape, lambda n: (0, 0, 0))


def _smem():
    return pl.BlockSpec(memory_space=pltpu.SMEM)


def _mixer_fwd(qn, kn, vb, ug, vn, wsb, bsb, sink, ga, gs, name):
    s = qn.shape[0]
    nb = s // BLOCK

    def body(sink_ref, q_ref, kp_ref, kc_ref, kx_ref, vp_ref, vc_ref, vx_ref, ug_ref, vn_ref, ws_ref, bs_ref, ga_ref, gs_ref,
             attn_ref, sgu_ref, mix_ref, probs_ref, psink_ref):
        n = pl.program_id(0)
        valid = _band_valid(n, s)
        ssq = jnp.zeros((BLOCK, 1), F32)
        for kh in range(N_KV_HEADS):
            kb = jnp.concatenate([kp_ref[:, _head(kh)], kc_ref[:, _head(kh)], kx_ref[:, _head(kh)]], axis=0)
            vbd = jnp.concatenate([vp_ref[:, _head(kh)], vc_ref[:, _head(kh)], vx_ref[:, _head(kh)]], axis=0)
            p, p_sink = _probs(_group_rows(q_ref, kh), kb, _group_sinks(sink_ref, kh), valid)
            pb = p.astype(BF16)
            probs_ref[kh] = pb
            psink_ref[kh] = p_sink
            o4 = jnp.dot(pb, vbd, preferred_element_type=F32)
            for g in range(GQA_GROUP):
                o = _rows_of(o4, g)
                attn_ref[:, _head(kh * GQA_GROUP + g)] = o
                ssq = ssq + jnp.sum(o * o, axis=-1, keepdims=True)
        r = lax.rsqrt(ssq * (1.0 / ATTN_WIDTH) + EPS)
        mix_ref[:, 0:ATTN_WIDTH] = (attn_ref[...] * r * ga_ref[...]).astype(BF16)
        ssq = jnp.zeros((BLOCK, 1), F32)
        for h in range(N_GMLP_HEADS):
            f = jnp.dot(ws_ref[h], vn_ref[:, _head(h)], preferred_element_type=F32) + bs_ref[h]
            o = ug_ref[:, _head(h)] * f
            sgu_ref[:, _head(h)] = o
            ssq = ssq + jnp.sum(o * o, axis=-1, keepdims=True)
        r = lax.rsqrt(ssq * (1.0 / GMLP_WIDTH) + EPS)
        mix_ref[:, ATTN_WIDTH:D_MODEL] = (sgu_ref[...] * r * gs_ref[...]).astype(BF16)

    hh = (N_GMLP_HEADS, BLOCK, BLOCK)
    return _ordered_call(
        body, name=name,
        out_shape=(jax.ShapeDtypeStruct((s, ATTN_WIDTH), F32), jax.ShapeDtypeStruct((s, GMLP_WIDTH), F32),
                   jax.ShapeDtypeStruct((s, D_MODEL), BF16), jax.ShapeDtypeStruct((nb,) + PROBS_BLOCK, BF16),
                   jax.ShapeDtypeStruct((nb,) + PSINK_BLOCK, F32)),
        grid=(nb,),
        in_specs=[_smem(), _blk(ATTN_WIDTH)] + _band_specs(KV_WIDTH, nb) + _band_specs(KV_WIDTH, nb)
        + [_blk(GMLP_WIDTH), _blk(GMLP_WIDTH), _whole3(hh), _whole3(hh),
           pl.BlockSpec((1, ATTN_WIDTH), lambda n: (0, 0)), pl.BlockSpec((1, GMLP_WIDTH), lambda n: (0, 0))],
        out_specs=(_blk(ATTN_WIDTH), _blk(GMLP_WIDTH), _blk(D_MODEL), _per_block(PROBS_BLOCK), _per_block(PSINK_BLOCK)),
        compiler_params=_params(("parallel",)),
    )(sink, qn, kn, kn, kn, vb, vb, vb, ug, vn, wsb, bsb, ga, gs)


PROBS_BLOCK = (N_KV_HEADS, GQA_GROUP * BLOCK, 3 * BLOCK)
PSINK_BLOCK = (N_KV_HEADS, GQA_GROUP * BLOCK, 1)


def _per_block(shape):
    return pl.BlockSpec((None,) + shape, lambda n: (n, 0, 0, 0))


def _mixer_bwd(qn, kn, vb, ug, vn, attn, sgu, dmixed, wsb, bsb, ga, gs, probs, psink, name):
    s = qn.shape[0]
    nb = s // BLOCK
    tn_dims = (((0,), (0,)), ((), ()))
    nt_dims = (((1,), (1,)), ((), ()))

    def body(q_ref, kp_ref, kc_ref, kx_ref, vp_ref, vc_ref, vx_ref, ug_ref, vn_ref, attn_ref, sgu_ref, dm_ref,
             ws_ref, bs_ref, ga_ref, gs_ref, probs_ref, psink_ref,
             dq_ref, dk_ref, dv_ref, dug_ref, dvn_ref, dws_ref, dbs_ref, dsk_ref, dga_ref, dgs_ref, dk_acc, dv_acc):
        n = pl.program_id(0)

        @pl.when(n == 0)
        def _():
            for ref in (dk_acc, dv_acc, dws_ref, dbs_ref, dsk_ref, dga_ref, dgs_ref):
                ref[...] = jnp.zeros_like(ref)

        def out_norm_bwd(o, g, dy):
            r = lax.rsqrt(_mean_last(o * o) + EPS)
            gdy = dy * g
            return r * gdy - o * ((r * r * r) * _mean_last(o * gdy)), _sum_rows(o * r * dy)

        d_attn, dga = out_norm_bwd(attn_ref[...], ga_ref[...], dm_ref[:, 0:ATTN_WIDTH])
        dga_ref[...] += dga
        d_sgu, dgs = out_norm_bwd(sgu_ref[...], gs_ref[...], dm_ref[:, ATTN_WIDTH:D_MODEL])
        dgs_ref[...] += dgs

        for h in range(N_GMLP_HEADS):
            vn_h = vn_ref[:, _head(h)]
            f = jnp.dot(ws_ref[h], vn_h, preferred_element_type=F32) + bs_ref[h]
            ds_h = d_sgu[:, _head(h)]
            dug_ref[:, _head(h)] = ds_h * f
            df = ds_h * ug_ref[:, _head(h)]
            dfb = df.astype(BF16)
            dvn_ref[:, _head(h)] = lax.dot_general(ws_ref[h], dfb, tn_dims, preferred_element_type=F32)
            dws_ref[h] += lax.dot_general(dfb, vn_h, nt_dims, preferred_element_type=F32)
            dbs_ref[h] += jnp.broadcast_to(jnp.sum(df, axis=-1, keepdims=True), (BLOCK, BLOCK))

        row0 = pl.multiple_of(n * BLOCK, BLOCK)
        for kh in range(N_KV_HEADS):
            kb = jnp.concatenate([kp_ref[:, _head(kh)], kc_ref[:, _head(kh)], kx_ref[:, _head(kh)]], axis=0)
            vbd = jnp.concatenate([vp_ref[:, _head(kh)], vc_ref[:, _head(kh)], vx_ref[:, _head(kh)]], axis=0)
            q4 = _group_rows(q_ref, kh)
            pb = probs_ref[kh]
            p = pb.astype(F32)
            do4 = _group_rows(d_attn, kh).astype(BF16)
            dp = lax.dot_general(do4, vbd, nt_dims, preferred_element_type=F32)
            delta = jnp.sum(p * dp, axis=-1, keepdims=True)
            dsc = (p * (dp - delta) * (HEAD_DIM ** -0.5)).astype(BF16)
            d_sink = -(psink_ref[kh] * delta)
            dq4 = jnp.dot(dsc, kb, preferred_element_type=F32)
            for g in range(GQA_GROUP):
                h = kh * GQA_GROUP + g
                dsk_ref[h:h + 1, :] += jnp.broadcast_to(_sum_all(_rows_of(d_sink, g)), (1, BLOCK))
                dq_ref[:, _head(h)] = _rows_of(dq4, g)
            dk_acc[pl.ds(row0, 3 * BLOCK), _head(kh)] += lax.dot_general(dsc, q4, tn_dims, preferred_element_type=F32)
            dv_acc[pl.ds(row0, 3 * BLOCK), _head(kh)] += lax.dot_general(pb, do4, tn_dims, preferred_element_type=F32)

        @pl.when(n == nb - 1)
        def _():
            dk_ref[...] = dk_acc[BLOCK:BLOCK + s, :]
            dv_ref[...] = dv_acc[BLOCK:BLOCK + s, :]

    hh = (N_GMLP_HEADS, BLOCK, BLOCK)
    full_kv = pl.BlockSpec((s, KV_WIDTH), lambda n: (0, 0))
    return _ordered_call(
        body, name=name,
        out_shape=(jax.ShapeDtypeStruct((s, ATTN_WIDTH), F32), jax.ShapeDtypeStruct((s, KV_WIDTH), F32),
                   jax.ShapeDtypeStruct((s, KV_WIDTH), F32), jax.ShapeDtypeStruct((s, GMLP_WIDTH), F32),
                   jax.ShapeDtypeStruct((s, GMLP_WIDTH), F32), jax.ShapeDtypeStruct(hh, F32), jax.ShapeDtypeStruct(hh, F32),
                   jax.ShapeDtypeStruct((N_Q_HEADS, BLOCK), F32), jax.ShapeDtypeStruct((1, ATTN_WIDTH), F32),
                   jax.ShapeDtypeStruct((1, GMLP_WIDTH), F32)),
        grid=(nb,),
        in_specs=[_blk(ATTN_WIDTH)] + _band_specs(KV_WIDTH, nb) + _band_specs(KV_WIDTH, nb)
        + [_blk(GMLP_WIDTH), _blk(GMLP_WIDTH), _blk(ATTN_WIDTH), _blk(GMLP_WIDTH), _blk(D_MODEL), _whole3(hh), _whole3(hh),
           pl.BlockSpec((1, ATTN_WIDTH), lambda n: (0, 0)), pl.BlockSpec((1, GMLP_WIDTH), lambda n: (0, 0)),
           _per_block(PROBS_BLOCK), _per_block(PSINK_BLOCK)],
        out_specs=(_blk(ATTN_WIDTH), full_kv, full_kv, _blk(GMLP_WIDTH), _blk(GMLP_WIDTH), _whole3(hh), _whole3(hh),
                   pl.BlockSpec((N_Q_HEADS, BLOCK), lambda n: (0, 0)), pl.BlockSpec((1, ATTN_WIDTH), lambda n: (0, 0)),
                   pl.BlockSpec((1, GMLP_WIDTH), lambda n: (0, 0))),
        scratch_shapes=[pltpu.VMEM((s + 2 * BLOCK, KV_WIDTH), F32), pltpu.VMEM((s + 2 * BLOCK, KV_WIDTH), F32)],
        compiler_params=_params(("arbitrary",)),
    )(qn, kn, kn, kn, vb, vb, vb, ug, vn, attn, sgu, dmixed, wsb, bsb, ga, gs, probs, psink)


CONV_TILE = 128


PAD_ROWS = 8


def _zero_pad_rows(pad_ref):
    s = pad_ref.shape[0] - 2 * PAD_ROWS
    zeros = jnp.zeros((PAD_ROWS, pad_ref.shape[1]), F32)
    pad_ref[0:PAD_ROWS, :] = zeros
    pad_ref[PAD_ROWS + s:2 * PAD_ROWS + s, :] = zeros


def _shift_rows(a, pad_ref):
    s = a.shape[0]
    pad_ref[PAD_ROWS:PAD_ROWS + s, :] = a
    padded = pad_ref[...]
    prev = pltpu.roll(padded, 1, 0)[PAD_ROWS:PAD_ROWS + s]
    nxt = pltpu.roll(padded, s + 2 * PAD_ROWS - 1, 0)[PAD_ROWS:PAD_ROWS + s]
    return prev, nxt


def _conv_specs(s):
    tc = CONV_TILE
    nj = D_FF // tc
    return (tc, nj, pl.BlockSpec((2, s, tc), lambda j: (0, 0, j)),
            [pl.BlockSpec((3, tc), lambda j: (0, j)), pl.BlockSpec((3, tc), lambda j: (0, j + nj))],
            [pl.BlockSpec((1, tc), lambda j: (0, j)), pl.BlockSpec((1, tc), lambda j: (0, j + nj))])


def _conv_gate_fwd(a_pre, cw, cb, name):
    s = a_pre.shape[1]
    tc, nj, a_spec, w_specs, b_specs = _conv_specs(s)

    def body(a_ref, wg_ref, wu_ref, bg_ref, bu_ref, act_ref, dgu_ref, pad_ref):
        _zero_pad_rows(pad_ref)

        def conv(a, w_ref, b_ref):
            prev, nxt = _shift_rows(a, pad_ref)
            return b_ref[...] + prev * w_ref[0:1, :] + a * w_ref[1:2, :] + nxt * w_ref[2:3, :]

        g = conv(a_ref[0].astype(F32), wg_ref, bg_ref)
        u = conv(a_ref[1].astype(F32), wu_ref, bu_ref)
        sg = 1.0 / (1.0 + jnp.exp(-g))
        silu = g * sg
        act_ref[...] = (silu * u).astype(BF16)
        dgu_ref[0] = (u * (sg * (1.0 + g * (1.0 - sg)))).astype(BF16)
        dgu_ref[1] = silu.astype(BF16)

    return _ordered_call(
        body, name=name, out_shape=(jax.ShapeDtypeStruct((s, D_FF), BF16), jax.ShapeDtypeStruct((2, s, D_FF), BF16)),
        grid=(nj,), in_specs=[a_spec] + w_specs + b_specs,
        out_specs=(pl.BlockSpec((s, tc), lambda j: (0, j)), pl.BlockSpec((2, s, tc), lambda j: (0, 0, j))),
        scratch_shapes=[pltpu.VMEM((s + 2 * PAD_ROWS, tc), F32)], compiler_params=_params(("parallel",)),
    )(a_pre, cw, cw, cb, cb)


def _conv_gate_bwd(a_pre, dgu, cw, dact, name):
    s = a_pre.shape[1]
    tc, nj, a_spec, w_specs, _ = _conv_specs(s)

    def body(a_ref, dgu_ref, wg_ref, wu_ref, dact_ref, dap_ref, dcw_ref, dcb_ref, pad_ref):
        _zero_pad_rows(pad_ref)
        dact_v = dact_ref[...].astype(F32)
        for part, w_ref in enumerate((wg_ref, wu_ref)):
            da = dact_v * dgu_ref[part].astype(F32)
            a = a_ref[part].astype(F32)
            da_prev, da_next = _shift_rows(da, pad_ref)
            dcw_ref[part, 0:1, :] = _sum_rows(a * da_next)
            dcw_ref[part, 1:2, :] = _sum_rows(a * da)
            dcw_ref[part, 2:3, :] = _sum_rows(a * da_prev)
            dcb_ref[part] = _sum_rows(da)
            dap_ref[part] = (da_next * w_ref[0:1, :] + da * w_ref[1:2, :] + da_prev * w_ref[2:3, :]).astype(BF16)

    return _ordered_call(
        body, name=name,
        out_shape=(jax.ShapeDtypeStruct((2, s, D_FF), BF16), jax.ShapeDtypeStruct((2, 3, D_FF), F32),
                   jax.ShapeDtypeStruct((2, 1, D_FF), F32)),
        grid=(nj,),
        in_specs=[a_spec, pl.BlockSpec((2, s, tc), lambda j: (0, 0, j))] + w_specs + [pl.BlockSpec((s, tc), lambda j: (0, j))],
        out_specs=(pl.BlockSpec((2, s, tc), lambda j: (0, 0, j)), pl.BlockSpec((2, 3, tc), lambda j: (0, 0, j)),
                   pl.BlockSpec((2, 1, tc), lambda j: (0, 0, j))),
        scratch_shapes=[pltpu.VMEM((s + 2 * PAD_ROWS, tc), F32)], compiler_params=_params(("parallel",)),
    )(a_pre, dgu, cw, cw, dact)


def _loss_head(y, target, name):
    s, d = y.shape
    tr = _row_tile(s)

    def body(y_ref, t_ref, loss_ref, dy_ref, dyb_ref):
        err = y_ref[...] - t_ref[...]

        @pl.when(pl.program_id(0) == 0)
        def _():
            loss_ref[...] = jnp.zeros_like(loss_ref)

        loss_ref[...] += jnp.broadcast_to(0.5 * _sum_all(_mean_last(err * err)), (8, 128))
        dy = err * (1.0 / d)
        dy_ref[...] = dy
        dyb_ref[...] = dy.astype(BF16)

    return _ordered_call(
        body, name=name,
        out_shape=(jax.ShapeDtypeStruct((8, 128), F32), jax.ShapeDtypeStruct((s, d), F32), jax.ShapeDtypeStruct((s, d), BF16)),
        grid=(s // tr,), in_specs=[_rows(d, tr), _rows(d, tr)],
        out_specs=(_const2((8, 128)), _rows(d, tr), _rows(d, tr)), compiler_params=_params(("arbitrary",)),
    )(y, target)


def _row_block(rows, cols, budget=1 << 20):
    if rows * cols <= budget:
        return rows
    best = None
    for tr in range(16, rows, 16):
        if rows % tr == 0 and tr * cols <= budget:
            best = tr
    assert best is not None, (rows, cols)
    return best


def _place_shard(x4, layer, j_arr, out_dtype, name):
    _, nh, r, cols = x4.shape
    tr = _row_block(r, cols)

    def body(j_ref, x_ref, o_ref):
        o_ref[...] = x_ref[...].astype(out_dtype)

    grid_spec = pltpu.PrefetchScalarGridSpec(
        num_scalar_prefetch=1, grid=(nh, r // tr),
        in_specs=[pl.BlockSpec((None, None, tr, cols), lambda h, i, j_ref: (layer, h, i, 0))],
        out_specs=pl.BlockSpec((None, None, tr, cols), lambda h, i, j_ref: (j_ref[0], h, i, 0)))
    return _ordered_call(
        body, name=name, out_shape=jax.ShapeDtypeStruct((N_CHIPS, nh, r, cols), out_dtype), grid_spec=grid_spec,
        compiler_params=_params(("parallel", "parallel")),
    )(j_arr, x4)


def _adamw(w, g, m, v, name):
    rows, cols = w.shape
    tr = _row_block(rows, cols, 1 << 18)

    def body(w_ref, g_ref, m_ref, v_ref, go_ref, d_ref, nm_ref, nv_ref):
        gv = g_ref[...]
        go_ref[...] = gv
        mn = ADAM_B1 * m_ref[...] + (1.0 - ADAM_B1) * gv
        vn = ADAM_B2 * v_ref[...] + (1.0 - ADAM_B2) * (gv * gv)
        m_hat = mn / (1.0 - ADAM_B1 ** ADAM_STEP)
        v_hat = vn / (1.0 - ADAM_B2 ** ADAM_STEP)
        d_ref[...] = -ADAM_LR * (m_hat / (jnp.sqrt(v_hat) + ADAM_EPS) + ADAM_WD * w_ref[...])
        nm_ref[...] = mn
        nv_ref[...] = vn

    sds = jax.ShapeDtypeStruct((rows, cols), F32)
    return _ordered_call(
        body, name=name, out_shape=(sds, sds, sds, sds), grid=(rows // tr,),
        in_specs=[_rows(cols, tr)] * 4, out_specs=(_rows(cols, tr),) * 4, compiler_params=_params(("parallel",)),
    )(w, g, m, v)


def _chip_sum(p4, recv3, j_arr, c_arr, name):
    _, rh, cols = p4.shape
    tr = _row_block(rh, cols, 1 << 19)

    def body(j_ref, c_ref, p_ref, r_ref, o_ref):
        total = p_ref[...].astype(F32)
        for peer in range(3):
            total = total + r_ref[peer].astype(F32)
        o_ref[...] = total

    grid_spec = pltpu.PrefetchScalarGridSpec(
        num_scalar_prefetch=2, grid=(rh // tr,),
        in_specs=[pl.BlockSpec((None, tr, cols), lambda i, j_ref, c_ref: (j_ref[0], i, 0)),
                  pl.BlockSpec((3, tr, cols), lambda i, j_ref, c_ref: (0, i, 0))],
        out_specs=pl.BlockSpec((None, tr, cols), lambda i, j_ref, c_ref: (c_ref[0], i, 0)))
    return _ordered_call(
        body, name=name, out_shape=jax.ShapeDtypeStruct((2, rh, cols), F32), grid_spec=grid_spec,
        compiler_params=_params(("parallel",)),
    )(j_arr, c_arr, p4, recv3)


def _adamw_layer(w, g, m, v, layer, into, name):
    nl, rows, cols = w.shape
    slabs, _, width = g.shape
    assert slabs * width == cols and g.shape[1] == rows, (name, w.shape, g.shape)
    tr = _row_block(rows, width, 1 << 18)
    at_layer = pl.BlockSpec((None, tr, width), lambda h, i: (layer, i, h))

    def body(w_ref, g_ref, m_ref, v_ref, *rest):
        go_ref, d_ref, nm_ref, nv_ref = rest[-4:]
        gv = g_ref[...]
        go_ref[...] = gv
        mn = ADAM_B1 * m_ref[...] + (1.0 - ADAM_B1) * gv
        vn = ADAM_B2 * v_ref[...] + (1.0 - ADAM_B2) * (gv * gv)
        m_hat = mn / (1.0 - ADAM_B1 ** ADAM_STEP)
        v_hat = vn / (1.0 - ADAM_B2 ** ADAM_STEP)
        d_ref[...] = -ADAM_LR * (m_hat / (jnp.sqrt(v_hat) + ADAM_EPS) + ADAM_WD * w_ref[...])
        nm_ref[...] = mn
        nv_ref[...] = vn

    in_specs = [at_layer, pl.BlockSpec((None, tr, width), lambda h, i: (h, i, 0)), at_layer, at_layer]
    operands = [w, g, m, v]
    aliases = {}
    if into is not None:
        in_specs += [ANY] * 4
        operands += list(into)
        aliases = {4 + i: i for i in range(4)}
    sds = jax.ShapeDtypeStruct((nl, rows, cols), F32)
    return _ordered_call(
        body, name=name, out_shape=(sds,) * 4, grid=(slabs, rows // tr), in_specs=in_specs, out_specs=(at_layer,) * 4,
        input_output_aliases=aliases, compiler_params=_params(("parallel", "parallel")),
    )(*operands)


def _sum_devices(mine, landed, me_arr, name):
    rows, lanes = mine.shape

    def body(me_ref, mine_ref, landed_ref, o_ref):
        total = None
        for dev in range(8):
            part = jnp.where(me_ref[0] == dev, mine_ref[...], landed_ref[dev])
            total = part if total is None else total + part
        o_ref[...] = total

    grid_spec = pltpu.PrefetchScalarGridSpec(
        num_scalar_prefetch=1, grid=(1,),
        in_specs=[pl.BlockSpec((rows, lanes), lambda i, me_ref: (0, 0)), pl.BlockSpec((8, rows, lanes), lambda i, me_ref: (0, 0, 0))],
        out_specs=pl.BlockSpec((rows, lanes), lambda i, me_ref: (0, 0)))
    return _ordered_call(
        body, name=name, out_shape=jax.ShapeDtypeStruct((rows, lanes), F32), grid_spec=grid_spec,
        compiler_params=_params(("arbitrary",)),
    )(me_arr, mine, landed)


def _place():
    x, y, c = lax.axis_index("x"), lax.axis_index("y"), lax.axis_index("c")
    chips = [(1 - x, y), (x, 1 - y), (1 - x, 1 - y)]
    return x, y, c, chips


HBM = pl.BlockSpec(memory_space=pltpu.HBM)
SEM = pl.BlockSpec(memory_space=pltpu.SEMAPHORE)
TOKEN = jax.ShapeDtypeStruct((8, 128), F32)


def _remote(src, dst, send_sem, recv_sem, to):
    return pltpu.make_async_remote_copy(src_ref=src, dst_ref=dst, send_sem=send_sem, recv_sem=recv_sem, device_id=to,
                                        device_id_type=MESH)


def _split_call(body, name, thru, sems_in=(), fresh=(), new_sems=(), after_last=True):
    n_t, n_s, n_f = len(thru), len(sems_in), len(fresh)

    def call_body(*refs):
        outs = refs[n_t + n_s:]
        body(refs[:n_t], refs[n_t:n_t + n_s], outs[1 + n_t:1 + n_t + n_f], outs[1 + n_t + n_f:])
        outs[0][...] = jnp.zeros_like(outs[0])

    out_shape = ([TOKEN] + [pltpu.HBM(t.shape, t.dtype) for t in thru] + [pltpu.HBM(shp, dt) for shp, dt in fresh]
                 + [pltpu.SemaphoreType.DMA(shp) for shp in new_sems])
    out_specs = [pl.BlockSpec(memory_space=pltpu.VMEM)] + [HBM] * (n_t + n_f) + [SEM] * len(new_sems)
    if not after_last:
        _Order.last = None
    out = _ordered_call(
        call_body, name=name, out_shape=tuple(out_shape), in_specs=[HBM] * n_t + [SEM] * n_s, out_specs=tuple(out_specs),
        input_output_aliases={i: 1 + i for i in range(n_t)},
        compiler_params=pltpu.CompilerParams(has_side_effects=pltpu.SideEffectType.DATAFLOW_SIDE_EFFECTING),
    )(*[pltpu.with_memory_space_constraint(t, pltpu.HBM) for t in thru], *sems_in)
    return out[1:1 + n_t], out[1 + n_t:1 + n_t + n_f], out[1 + n_t + n_f:]


class _Exchange:
    def __init__(self, weights, m_in, v_in, j_arr, c_arr, me_arr):
        self.w, self.m, self.v = weights, m_in, v_in
        self.j_arr, self.c_arr, self.me_arr = j_arr, c_arr, me_arr
        self.adam, self.small, self.pairs = {}, {}, {}
        self.o_arr = 1 - c_arr
        self.groups = [(l, name) for l in range(DEPTH) for name in BIG_NAMES]
        self.shard_shape = {name: weights[name].shape[1:] for name in BIG_NAMES}
        self.conv_state, self.state = [], {}
        self.ready, self.conv_ready = {}, {}
        self.pending, self.tick, self.reduced = [], 0, {}

        def place(grp):
            l, name = grp
            nl, r, cols = weights[name].shape
            return _place_shard(weights[name].reshape(nl, 2, r // 2, cols), l, j_arr, BF16, f"place_{name}_l{l}")

        def start_copies(tag, convs, groups, bufs):
            n_c = len(convs)

            def start(thru, _, __, sems):
                x, y, c, chips = _place()
                j_me = 2 * x + y
                copies = []
                for i in range(len(thru)):
                    mine = thru[i].at[j_me] if i < n_c else thru[i].at[j_me, c]
                    copies += [_remote(mine, mine, sems[2 * i].at[k], sems[2 * i + 1].at[k], (*chip, c))
                               for k, chip in enumerate(chips)]
                for cp in copies:
                    cp.start()

            thru, _, sems = _split_call(start, tag, convs + bufs, new_sems=[(3,)] * (2 * (n_c + len(bufs))))
            self.conv_state += [(thru[i], sems[2 * i], sems[2 * i + 1]) for i in range(n_c)]
            for g, grp in enumerate(groups):
                self.state[grp] = (thru[n_c + g], sems[2 * (n_c + g)], sems[2 * (n_c + g) + 1])

        convs = [_place_shard(weights["conv_w"][:, None], l, j_arr, F32, f"place_conv_w_l{l}") for l in range(DEPTH)]
        start_copies("gather_start_first", convs, self.groups[:1], [place(self.groups[0])])
        start_copies("gather_start_rest", [], self.groups[1:], [place(grp) for grp in self.groups[1:]])

    def conv_w(self, l):
        if l not in self.conv_ready:
            buf, send, recv = self.conv_state[l]

            def wait(thru, sems, _, __):
                x, y, c, chips = _place()
                for k, chip in enumerate(chips):
                    mine, theirs = thru[0].at[2 * x + y], thru[0].at[2 * chip[0] + chip[1]]
                    _remote(mine, mine, sems[0].at[k], sems[1].at[k], (*chip, c)).wait_send()
                    _remote(theirs, theirs, sems[0].at[k], sems[1].at[k], (x, y, c)).wait_recv()

            (buf,), _, _ = _split_call(wait, f"gather_conv_w_l{l}", [buf], sems_in=[send, recv])
            self.conv_ready[l] = jnp.transpose(buf[:, 0], (1, 0, 2)).reshape(3, 2 * D_FF)
        return self.conv_ready[l]

    def weight(self, l, name):
        grp = (l, name)
        if grp not in self.ready:
            buf, send, recv = self.state[grp]

            def forward(thru, sems, _, new):
                x, y, c, chips = _place()
                for k, chip in enumerate(chips):
                    landed = thru[0].at[2 * chip[0] + chip[1], c]
                    _remote(landed, landed, new[0].at[k], sems[0].at[k], (x, y, c)).wait_recv()
                    _remote(landed, landed, new[0].at[k], new[1].at[k], (x, y, 1 - c)).start()

            (buf,), _, (fsend, frecv) = _split_call(forward, f"gather_pass_{name}_l{l}", [buf], sems_in=[recv],
                                                    new_sems=[(3,), (3,)])

            def finish(thru, sems, _, __):
                x, y, c, chips = _place()
                mine = thru[0].at[2 * x + y, c]
                for k, chip in enumerate(chips):
                    j_k = 2 * chip[0] + chip[1]
                    theirs, landed = thru[0].at[j_k, 1 - c], thru[0].at[j_k, c]
                    _remote(theirs, theirs, sems[1].at[k], sems[2].at[k], (x, y, c)).wait_recv()
                    _remote(landed, landed, sems[1].at[k], sems[2].at[k], (x, y, 1 - c)).wait_send()
                    _remote(mine, mine, sems[0].at[k], sems[2].at[k], (*chip, c)).wait_send()

            (buf,), _, _ = _split_call(finish, f"gather_done_{name}_l{l}", [buf], sems_in=[send, fsend, frecv])
            r, cols = self.shard_shape[name]
            self.ready[grp] = buf.reshape(N_CHIPS, r, cols) if name in ("w_in", "w_up") else buf.reshape(N_CHIPS * r, cols)
        return self.ready[grp]

    def pair_send(self, l, name, other):
        def start(thru, _, fresh, sems):
            x, y, c, _chips = _place()
            _remote(thru[0], fresh[0], sems[0], sems[1], (x, y, 1 - c)).start()

        (other,), (recv,), sems = _split_call(start, f"pair_start_{name}_l{l}", [other], fresh=[(other.shape, BF16)],
                                             new_sems=[(), ()], after_last=False)
        self.pairs[(l, name)] = (other, recv, sems)

    def pair_recv(self, l, name):
        other, recv, sems = self.pairs.pop((l, name))

        def wait(thru, sems, _, __):
            x, y, c, _chips = _place()
            cp = _remote(thru[0], thru[1], sems[0], sems[1], (x, y, 1 - c))
            cp.wait_send()
            cp.wait_recv()

        (_, recv), _, _ = _split_call(wait, f"pair_done_{name}_l{l}", [other, recv], sems_in=list(sems))
        return recv

    def scatter(self, l, name, p4):
        def start(thru, _, fresh, sems):
            x, y, c, chips = _place()
            for k, chip in enumerate(chips):
                _remote(thru[0].at[2 * chip[0] + chip[1]], fresh[0].at[k], sems[0].at[k], sems[1].at[k], (*chip, c)).start()

        (p4,), (recv3,), sems = _split_call(start, f"chips_start_{name}_l{l}", [p4], fresh=[((3,) + p4.shape[1:], BF16)],
                                           new_sems=[(3,), (3,)], after_last=False)
        self.pending.append(dict(l=l, name=name, stage=2, at=self.tick, bufs=(p4, recv3), sems=sems))

    def _chips(self, grp):
        l, name = grp["l"], grp["name"]

        def wait(thru, sems, _, __):
            x, y, c, chips = _place()
            for k, chip in enumerate(chips):
                cp = _remote(thru[0].at[2 * chip[0] + chip[1]], thru[1].at[k], sems[0].at[k], sems[1].at[k], (*chip, c))
                cp.wait_send()
                cp.wait_recv()

        (p4, recv3), _, _ = _split_call(wait, f"chips_done_{name}_l{l}", list(grp["bufs"]), sems_in=list(grp["sems"]))
        half = _chip_sum(p4, recv3, self.j_arr, self.c_arr, f"chip_sum_{name}_l{l}")

        def start(thru, _, __, sems):
            x, y, c, _chips = _place()
            _remote(thru[0].at[c], thru[0].at[c], sems[0], sems[1], (x, y, 1 - c)).start()

        (half,), _, sems = _split_call(start, f"join_start_{name}_l{l}", [half], new_sems=[(), ()], after_last=False)
        grp.update(stage=3, at=self.tick, bufs=(half,), sems=sems)

    def _update(self, grp):
        l, name = grp["l"], grp["name"]

        def wait(thru, sems, _, __):
            x, y, c, _chips = _place()
            _remote(thru[0].at[c], thru[0].at[c], sems[0], sems[1], (x, y, 1 - c)).wait_send()
            _remote(thru[0].at[1 - c], thru[0].at[1 - c], sems[0], sems[1], (x, y, c)).wait_recv()

        (full,), _, _ = _split_call(wait, f"join_done_{name}_l{l}", list(grp["bufs"]), sems_in=list(grp["sems"]))
        if GRAD_HALVES[name][0] != "cols_of_block":
            full = full.reshape((1,) + tuple(self.shard_shape[name]))
        self.adam[name] = _adamw_layer(self.w[name], full, self.m[name], self.v[name], l, self.adam.get(name),
                                       f"adamw_{name}_l{l}")
        grp.update(stage=4)

    def point(self, drain=False):
        self.tick += 1
        for grp in self.pending:
            if grp["stage"] == 3 and (drain or grp["at"] < self.tick):
                self._update(grp)
            elif grp["stage"] == 2 and (drain or grp["at"] + 3 <= self.tick):
                self._chips(grp)

    def finish(self):
        while any(grp["stage"] < 4 for grp in self.pending):
            self.point(drain=True)
        return self.adam

    @staticmethod
    def _peer(k, x, y, c):
        return (1 - x if k & 4 else x, 1 - y if k & 2 else y, 1 - c if k & 1 else c)

    def small_grads(self, l, grads, loss_tile):
        parts = [grads[nm] for nm in SMALL_NAMES] + ([loss_tile[0, 0:1]] if loss_tile is not None else [])
        packed = _pack_call(parts, f"small_pack_l{l}")
        rows = packed.shape[0]

        def start(thru, _, fresh, sems):
            x, y, c, _chips = _place()
            for k in range(1, 8):
                _remote(thru[0], fresh[0].at[4 * x + 2 * y + c], sems[0].at[k - 1], sems[1].at[k - 1],
                        self._peer(k, x, y, c)).start()

        (packed,), (landed,), sems = _split_call(start, f"small_start_l{l}", [packed], fresh=[((8, rows, PACK_LANES), F32)],
                                                 new_sems=[(7,), (7,)], after_last=False)
        self.small[l] =(packed, landed, sems, [p.shape for p in parts])

    def small_sum(self, l):
        packed, landed, sems, _shapes = self.small[l]

        def wait(thru, sems, _, __):
            x, y, c, _chips = _place()
            for k in range(1, 8):
                px, py, pc = self._peer(k, x, y, c)
                _remote(thru[0], thru[1].at[4 * x + 2 * y + c], sems[0].at[k - 1], sems[1].at[k - 1], (px, py, pc)).wait_send()
                _remote(thru[0], thru[1].at[4 * px + 2 * py + pc], sems[0].at[k - 1], sems[1].at[k - 1], (x, y, c)).wait_recv()

        (packed, landed), _, _ = _split_call(wait, f"small_done_l{l}", [packed, landed], sems_in=list(sems))
        return _sum_devices(packed, landed, self.me_arr, f"small_sum_l{l}")


def _rope_tables(s):
    inv_freq = ROPE_THETA ** (-jnp.arange(0, HEAD_DIM, 2, dtype=F32) / HEAD_DIM)
    ang = jnp.arange(s, dtype=F32)[:, None] * inv_freq[None, :]
    cos, sin = jnp.cos(ang), jnp.sin(ang)
    return jnp.concatenate([cos, cos], axis=-1), jnp.concatenate([-sin, sin], axis=-1)


def _local_step(x, target, ex, small):
    s = x.shape[0]
    cosf, sinf = _rope_tables(s)
    saved = []
    for l in range(DEPTH):
        p = small[l]
        t = f"l{l}"
        h = _rms_fwd(x, p["norm1_g"], f"norm1_{t}")
        z = _matmul(h, ex.weight(l, "w_in"), mode="nn", out_dtype=BF16, tm=1024, tn=896, tk=2048, b_parts=4, name=f"proj_in_{t}")
        qn, kn, vb, ug, vn, *gate_kept = _proj_post(z, p["q_norm_g"], p["k_norm_g"], p["sgu_ln_g"], p["sgu_ln_b"], cosf, sinf,
                                                    f"proj_post_{t}")
        attn, sgu, mixed, probs, psink = _mixer_fwd(qn, kn, vb, ug, vn, p["w_s_bf16"], p["b_s_tile"], p["sink"],
                                                    p["attn_out_g"], p["sgu_out_g"], f"mixer_{t}")
        x1 = _matmul(mixed, ex.weight(l, "w_o"), mode="nn", out_dtype=F32, tm=2048, tn=256, tk=2048, res=x,
                     name=f"proj_out_{t}")
        h2 = _rms_fwd(x1, p["norm2_g"], f"norm2_{t}")
        a_pre = _matmul(h2, ex.weight(l, "w_up"), mode="nn", out_dtype=BF16, tm=1024, tn=1408, tk=2048, b_parts=4,
                        out_parts=2,
                        name=f"ffn_up_{t}")
        act, dgu = _conv_gate_fwd(a_pre, ex.conv_w(l), p["conv_b"], f"conv_gate_{t}")
        x2 = _matmul(act, ex.weight(l, "w_down"), mode="nn", out_dtype=F32, tm=1024, tn=256, tk=D_FF, res=x1,
                     name=f"ffn_down_{t}")
        saved.append(dict(x=x, h=h, z=z, qn=qn, kn=kn, vb=vb, ug=ug, vn=vn, attn=attn, sgu=sgu, mixed=mixed, x1=x1, h2=h2,
                          a_pre=a_pre, act=act, dgu=dgu, probs=probs, psink=psink, gate_kept=gate_kept))
        x = x2
    loss_tile, dx, dxb = _loss_head(x, target, "loss_head")
    for l in reversed(range(DEPTH)):
        p, sv = small[l], saved[l]
        t = f"l{l}"
        def weight_grad(name, a, g, between, g_parts=0):
            ex.pair_send(l, name, _grad_half(name, a, g, ex.o_arr, None, f"g_{name}_other_{t}", g_parts))
            out = between()
            ex.scatter(l, name, _grad_half(name, a, g, ex.c_arr, ex.pair_recv(l, name), f"g_{name}_own_{t}", g_parts))
            ex.point()
            return out

        def after_down():
            dact = _matmul(dxb, ex.weight(l, "w_down"), mode="nt", out_dtype=BF16, tm=1024, tn=512, tk=2048,
                           name=f"d_act_{t}")
            return _conv_gate_bwd(sv["a_pre"], sv["dgu"], ex.conv_w(l), dact, f"conv_gate_bwd_{t}")

        dap, dcw, dcb = weight_grad("w_down", sv["act"], dxb, after_down)

        def after_up():
            dh2 = _matmul(dap, ex.weight(l, "w_up"), mode="nt", out_dtype=F32, tm=1024, tn=1024, tk=2816, a_parts=2,
                          b_parts=4, name=f"d_h2_{t}")
            return _rms_bwd(sv["x1"], p["norm2_g"], dh2, dx, f"norm2_bwd_{t}")

        dx1, dx1b, dg2 = weight_grad("w_up", sv["h2"], dap, after_up, g_parts=2)
        ex.pair_send(l, "w_o", _grad_half("w_o", sv["mixed"], dx1b, ex.o_arr, None, f"g_w_o_other_{t}"))
        dmixed = _matmul(dx1b, ex.weight(l, "w_o"), mode="nt", out_dtype=F32, tm=1024, tn=512, tk=2048,
                         name=f"d_mixed_{t}")
        dqn, dkn, dvb, dug, dvn, dws, dbs, dsk, dga, dgs = _mixer_bwd(
            sv["qn"], sv["kn"], sv["vb"], sv["ug"], sv["vn"], sv["attn"], sv["sgu"], dmixed, p["w_s_bf16"], p["b_s_tile"],
            p["attn_out_g"], p["sgu_out_g"], sv["probs"], sv["psink"], f"mixer_bwd_{t}")
        dz, dqg, dkg, dlg, dlb = _proj_post_bwd(sv["z"], dqn, dkn, dvb, dug, dvn, *sv["gate_kept"], p["q_norm_g"], p["k_norm_g"],
                                                 p["sgu_ln_g"], cosf, sinf, f"proj_post_bwd_{t}")
        ex.scatter(l, "w_o", _grad_half("w_o", sv["mixed"], dx1b, ex.c_arr, ex.pair_recv(l, "w_o"), f"g_w_o_own_{t}"))
        ex.point()

        def after_in():
            dh = _matmul_nt_slabs(dz, ex.weight(l, "w_in"), tm=1024, tn=512, name=f"d_h_{t}")
            return _rms_bwd(sv["x"], p["norm1_g"], dh, dx1, f"norm1_bwd_{t}")

        dx, dxb, dg1 = weight_grad("w_in", sv["h"], dz, after_in)
        ex.small_grads(l, dict(
            norm1_g=dg1[0], q_norm_g=dqg[0], k_norm_g=dkg[0], sink=dsk[:, 0], sgu_ln_g=dlg[0], sgu_ln_b=dlb[0], w_s=dws,
            b_s=dbs[:, :, 0], attn_out_g=dga[0], sgu_out_g=dgs[0], norm2_g=dg2[0],
            conv_w=jnp.concatenate([dcw[0], dcw[1]], axis=-1), conv_b=jnp.concatenate([dcb[0, 0], dcb[1, 0]], axis=-1)),
            loss_tile if l == 0 else None)
    return dx


def _small_views(l, norm1_g, q_norm_g, k_norm_g, sink, sgu_ln_g, sgu_ln_b, w_s, b_s, attn_out_g, sgu_out_g, norm2_g, conv_b):
    return dict(
        norm1_g=norm1_g[l][None], q_norm_g=q_norm_g[l][None], k_norm_g=k_norm_g[l][None], sink=sink[l],
        sgu_ln_g=sgu_ln_g[l][None], sgu_ln_b=sgu_ln_b[l][None], w_s_bf16=w_s[l].astype(BF16),
        b_s_tile=jnp.broadcast_to(b_s[l][:, :, None], (N_GMLP_HEADS, BLOCK, BLOCK)), attn_out_g=attn_out_g[l][None],
        sgu_out_g=sgu_out_g[l][None], norm2_g=norm2_g[l][None], conv_b=conv_b[l][None])


SMALL_NAMES = ("norm1_g", "q_norm_g", "k_norm_g", "sink", "sgu_ln_g", "sgu_ln_b", "w_s", "b_s", "attn_out_g", "sgu_out_g",
               "norm2_g", "conv_b", "conv_w")
REPLICATED_NAMES = SMALL_NAMES[:-1]
BIG_NAMES = ("w_in", "w_o", "w_up", "w_down")
PACK_LANES = 128
PACK_ALIGN = 8 * PACK_LANES


def _pack_rows(shape):
    return -(-math.prod(shape) // PACK_ALIGN) * 8


def _pack_parts(arrays):
    parts = []
    for a in arrays:
        flat = a.reshape(-1)
        parts.append(jnp.pad(flat, (0, _pack_rows(a.shape) * PACK_LANES - flat.shape[0])).reshape(-1, PACK_LANES))
    return parts


def _pack_call(arrays, name):
    parts = _pack_parts(arrays)
    total = sum(p.shape[0] for p in parts)

    def body(*refs):
        o_ref, at = refs[-1], 0
        for p_ref in refs[:-1]:
            o_ref[at:at + p_ref.shape[0], :] = p_ref[...]
            at += p_ref.shape[0]

    vm = pl.BlockSpec(memory_space=pltpu.VMEM)
    return _ordered_call(
        body, name=name, out_shape=jax.ShapeDtypeStruct((total, PACK_LANES), F32), in_specs=[vm] * len(parts), out_specs=vm,
        compiler_params=pltpu.CompilerParams(vmem_limit_bytes=V7X_VMEM_LIMIT),
    )(*parts)


def _unpack_layers(stacked, shapes):
    nl = stacked.shape[0]
    out, at = [], 0
    for shp in shapes:
        rows = _pack_rows(shp)
        out.append(stacked[:, at:at + rows].reshape(nl, -1)[:, :math.prod(shp)].reshape((nl,) + tuple(shp)))
        at += rows
    return out


def _adamw_packed(w, g, m, v, rows, layer, into, name):
    head = pl.BlockSpec((rows, PACK_LANES), lambda i: (0, 0))
    at_layer = pl.BlockSpec((None, rows, PACK_LANES), lambda i: (layer, 0, 0))

    def body(w_ref, g_ref, m_ref, v_ref, *rest):
        d_ref, nm_ref, nv_ref = rest[-3:]
        gv = g_ref[...]
        mn = ADAM_B1 * m_ref[...] + (1.0 - ADAM_B1) * gv
        vn = ADAM_B2 * v_ref[...] + (1.0 - ADAM_B2) * (gv * gv)
        m_hat = mn / (1.0 - ADAM_B1 ** ADAM_STEP)
        v_hat = vn / (1.0 - ADAM_B2 ** ADAM_STEP)
        d_ref[...] = -ADAM_LR * (m_hat / (jnp.sqrt(v_hat) + ADAM_EPS) + ADAM_WD * w_ref[...])
        nm_ref[...] = mn
        nv_ref[...] = vn

    in_specs = [head] * 4
    operands = [w, g, m, v]
    aliases = {}
    if into is not None:
        in_specs += [ANY] * 3
        operands += list(into)
        aliases = {4 + i: i for i in range(3)}
    sds = jax.ShapeDtypeStruct((DEPTH, rows, PACK_LANES), F32)
    return _ordered_call(
        body, name=name, out_shape=(sds,) * 3, grid=(1,), in_specs=in_specs, out_specs=(at_layer,) * 3,
        input_output_aliases=aliases, compiler_params=_params(("arbitrary",)),
    )(*operands)


def kernel(x, norm1_g, w_in, q_norm_g, k_norm_g, sink, sgu_ln_g, sgu_ln_b, w_s, b_s, attn_out_g, sgu_out_g, w_o, norm2_g, w_up, conv_w, conv_b, w_down, loss_target, m_norm1_g, m_w_in, m_q_norm_g, m_k_norm_g, m_sink, m_sgu_ln_g, m_sgu_ln_b, m_w_s, m_b_s, m_attn_out_g, m_sgu_out_g, m_w_o, m_norm2_g, m_w_up, m_conv_w, m_conv_b, m_w_down, v_norm1_g, v_w_in, v_q_norm_g, v_k_norm_g, v_sink, v_sgu_ln_g, v_sgu_ln_b, v_w_s, v_b_s, v_attn_out_g, v_sgu_out_g, v_w_o, v_norm2_g, v_w_up, v_conv_w, v_conv_b, v_w_down):
    weights = dict(norm1_g=norm1_g, w_in=w_in, q_norm_g=q_norm_g, k_norm_g=k_norm_g, sink=sink, sgu_ln_g=sgu_ln_g,
                   sgu_ln_b=sgu_ln_b, w_s=w_s, b_s=b_s, attn_out_g=attn_out_g, sgu_out_g=sgu_out_g, w_o=w_o, norm2_g=norm2_g,
                   w_up=w_up, conv_w=conv_w, conv_b=conv_b, w_down=w_down)
    m_in = dict(norm1_g=m_norm1_g, w_in=m_w_in, q_norm_g=m_q_norm_g, k_norm_g=m_k_norm_g, sink=m_sink, sgu_ln_g=m_sgu_ln_g,
                sgu_ln_b=m_sgu_ln_b, w_s=m_w_s, b_s=m_b_s, attn_out_g=m_attn_out_g, sgu_out_g=m_sgu_out_g, w_o=m_w_o,
                norm2_g=m_norm2_g, w_up=m_w_up, conv_w=m_conv_w, conv_b=m_conv_b, w_down=m_w_down)
    v_in = dict(norm1_g=v_norm1_g, w_in=v_w_in, q_norm_g=v_q_norm_g, k_norm_g=v_k_norm_g, sink=v_sink, sgu_ln_g=v_sgu_ln_g,
                sgu_ln_b=v_sgu_ln_b, w_s=v_w_s, b_s=v_b_s, attn_out_g=v_attn_out_g, sgu_out_g=v_sgu_out_g, w_o=v_w_o,
                norm2_g=v_norm2_g, w_up=v_w_up, conv_w=v_conv_w, conv_b=v_conv_b, w_down=v_w_down)
    cx, cy, cc = lax.axis_index("x"), lax.axis_index("y"), lax.axis_index("c")
    j_me = 2 * cx + cy
    c_arr = jnp.reshape(cc, (1,)).astype(jnp.int32)
    j_arr = jnp.reshape(j_me, (1,)).astype(jnp.int32)

    _Order.last = None
    ex = _Exchange(weights, m_in, v_in, j_arr, c_arr, jnp.reshape(4 * cx + 2 * cy + cc, (1,)).astype(jnp.int32))
    small = [_small_views(l, norm1_g, q_norm_g, k_norm_g, sink, sgu_ln_g, sgu_ln_b, w_s, b_s, attn_out_g, sgu_out_g, norm2_g,
                          conv_b) for l in range(DEPTH)]
    packed_in = [[_pack_call([src[nm][l] for nm in REPLICATED_NAMES], f"pack_{tag}_l{l}")
                  for tag, src in (("w", weights), ("m", m_in), ("v", v_in))] for l in range(DEPTH)]
    dx = _local_step(x[0], loss_target[0], ex, small)
    big_out = ex.finish()

    rep_shapes = [weights[nm].shape[1:] for nm in REPLICATED_NAMES]
    rep_rows = sum(_pack_rows(shp) for shp in rep_shapes)
    cw_shape = (3, 2 * D_FF)
    sums, adam_small = [None] * DEPTH, None
    for l in reversed(range(DEPTH)):
        sums[l] = ex.small_sum(l)
        pw, pm, pv = packed_in[l]
        adam_small = _adamw_packed(pw, sums[l], pm, pv, rep_rows, l, adam_small, f"adamw_small_l{l}")
    cw_rows = _pack_rows(cw_shape)
    loss = sums[0][rep_rows + cw_rows, 0]
    stacked = jnp.stack([sm[:rep_rows + cw_rows] for sm in sums])
    grads = dict(zip(REPLICATED_NAMES, _unpack_layers(stacked[:, :rep_rows], rep_shapes)))
    delta, new_m, new_v = (dict(zip(REPLICATED_NAMES, _unpack_layers(arr, rep_shapes))) for arr in adam_small)
    cw_cols = 2 * D_FF // N_CHIPS
    cw_grad = lax.dynamic_slice_in_dim(_unpack_layers(stacked[:, rep_rows:], [cw_shape])[0], j_me * cw_cols, cw_cols, axis=2)
    flat = lambda a: a.reshape(DEPTH * 3, cw_cols)
    cw_out = _adamw(flat(conv_w), flat(cw_grad), flat(m_conv_w), flat(v_conv_w), "adamw_conv_w")
    grads["conv_w"], delta["conv_w"], new_m["conv_w"], new_v["conv_w"] = (a.reshape(DEPTH, 3, cw_cols) for a in cw_out)

    for name in BIG_NAMES:
        grads[name], delta[name], new_m[name], new_v[name] = big_out[name]

    order = ("norm1_g", "w_in", "q_norm_g", "k_norm_g", "sink", "sgu_ln_g", "sgu_ln_b", "w_s", "b_s", "attn_out_g", "sgu_out_g",
             "w_o", "norm2_g", "w_up", "conv_w", "conv_b", "w_down")
    return (loss, dx[None], *[grads[nm] for nm in order], *[delta[nm] for nm in order], *[new_m[nm] for nm in order],
            *[new_v[nm] for nm in order])
```

```python
import math

import jax
import jax.numpy as jnp
from jax import lax
from jax.experimental import pallas as pl
from jax.experimental.pallas import tpu as pltpu

F32 = jnp.float32
BF16 = jnp.bfloat16

D_MODEL = 2048
HEAD_DIM = 128
ATTN_WIDTH = 1024
N_Q_HEADS = 8
N_KV_HEADS = 2
GQA_GROUP = 4
KV_WIDTH = 256
GMLP_WIDTH = 1024
N_GMLP_HEADS = 8
BLOCK = 128
IN_WIDTH = 3584
D_FF = 5632
DEPTH = 2
EPS = 1e-6
MASK_VALUE = -1e30
ROPE_THETA = 10000.0
N_CHIPS = 4

ADAM_LR = 0.001
ADAM_B1 = 0.9
ADAM_B2 = 0.999
ADAM_EPS = 1e-08
ADAM_WD = 0.01
ADAM_STEP = 10

V7X_VMEM_LIMIT = 48 * 1024 * 1024
MESH = pl.DeviceIdType.MESH

_GELU_C = math.sqrt(2.0 / math.pi)
_GELU_A = 0.044715


def _params(sem=None):
    return pltpu.CompilerParams(dimension_semantics=sem, vmem_limit_bytes=V7X_VMEM_LIMIT)


ANY = pl.BlockSpec(memory_space=pl.ANY)


class _Order:
    last = None


def _ordered_call(body, *, token_index=0, **kw):
    def run(*operands):
        tok = _Order.last
        if tok is None or any(op is tok for op in operands):
            call = pl.pallas_call(body, **kw)
        else:
            n_in = len(operands)

            def ordered_body(*refs):
                return body(*refs[:n_in], *refs[n_in + 1:])

            kw2 = dict(kw)
            if "grid_spec" in kw2:
                gs = kw2["grid_spec"]
                kw2["grid_spec"] = pltpu.PrefetchScalarGridSpec(
                    num_scalar_prefetch=gs.num_scalar_prefetch, grid=gs.grid, in_specs=list(gs.in_specs) + [ANY],
                    out_specs=gs.out_specs, scratch_shapes=gs.scratch_shapes)
            else:
                kw2["in_specs"] = list(kw2["in_specs"]) + [ANY]
            call = pl.pallas_call(ordered_body, **kw2)
            operands = operands + (tok,)
        out = call(*operands)
        _Order.last = out[token_index] if isinstance(out, (tuple, list)) else out
        return out

    return run


def _gelu(x):
    return x * (0.5 * (1.0 + jnp.tanh(_GELU_C * (x + _GELU_A * (x * x * x)))))


def _gelu_grad(x):
    x2 = x * x
    t = jnp.tanh(_GELU_C * (x + _GELU_A * (x * x2)))
    return 0.5 * (1.0 + t) + 0.5 * x * (1.0 - t * t) * (_GELU_C * (1.0 + 3.0 * _GELU_A * x2))


def _mean_last(x):
    return jnp.mean(x, axis=-1, keepdims=True)


def _sum_rows(x):
    return jnp.sum(x, axis=0, keepdims=True)


def _sum_all(x):
    return jnp.sum(jnp.sum(x, axis=1, keepdims=True), axis=0, keepdims=True)


def _matmul(a, b, *, mode, out_dtype, tm, tn, tk, name, res=None, a_parts=0, b_parts=0, out_parts=0):
    assert mode in ("nn", "nt"), mode
    if mode == "nn":
        assert not a_parts
        m, k = a.shape
        n = b.shape[0] * b.shape[2] if b_parts else b.shape[1]
    else:
        m, k = (a.shape[1], a.shape[0] * a.shape[2]) if a_parts else a.shape
        n = b.shape[1] if b_parts else b.shape[0]
    tm, tn, tk = min(tm, m), min(tn, n), min(tk, k)
    assert m % tm == 0 and n % tn == 0 and k % tk == 0, (name, m, n, k, tm, tn, tk)
    nm, nn, nk = m // tm, n // tn, k // tk

    def slab(idx, total_tiles, parts):
        per = total_tiles // parts
        assert per * parts == total_tiles, (name, total_tiles, parts)
        return idx // per, idx % per

    if mode == "nn":
        a_spec = pl.BlockSpec((tm, tk), lambda i, j, kk: (i, kk))
        if b_parts:
            b_spec = pl.BlockSpec((None, tk, tn), lambda i, j, kk: (slab(j, nn, b_parts)[0], kk, slab(j, nn, b_parts)[1]))
        else:
            b_spec = pl.BlockSpec((tk, tn), lambda i, j, kk: (kk, j))
        dims = (((1,), (0,)), ((), ()))
    else:
        if a_parts:
            a_spec = pl.BlockSpec((None, tm, tk), lambda i, j, kk: (slab(kk, nk, a_parts)[0], i, slab(kk, nk, a_parts)[1]))
        else:
            a_spec = pl.BlockSpec((tm, tk), lambda i, j, kk: (i, kk))
        if b_parts:
            b_spec = pl.BlockSpec((None, tn, tk), lambda i, j, kk: (slab(kk, nk, b_parts)[0], j, slab(kk, nk, b_parts)[1]))
        else:
            b_spec = pl.BlockSpec((tn, tk), lambda i, j, kk: (j, kk))
        dims = (((1,), (1,)), ((), ()))
    if out_parts:
        out_shape = jax.ShapeDtypeStruct((out_parts, m, n // out_parts), out_dtype)
        out_spec = pl.BlockSpec((None, tm, tn), lambda i, j, kk: (slab(j, nn, out_parts)[0], i, slab(j, nn, out_parts)[1]))
    else:
        out_shape = jax.ShapeDtypeStruct((m, n), out_dtype)
        out_spec = pl.BlockSpec((tm, tn), lambda i, j, kk: (i, j))
    in_specs = [a_spec, b_spec]
    operands = [a, b]
    if res is not None:
        in_specs.append(pl.BlockSpec((tm, tn), lambda i, j, kk: (i, j)))
        operands.append(res)

    def body(*refs):
        a_ref, b_ref = refs[0], refs[1]
        res_ref = refs[2] if res is not None else None
        o_ref = refs[3] if res is not None else refs[2]
        p = lax.dot_general(a_ref[...], b_ref[...], dims, preferred_element_type=F32)

        def finish(total):
            if res_ref is not None:
                total = res_ref[...] + total
            o_ref[...] = total.astype(out_dtype)

        if nk == 1:
            finish(p)
        else:
            acc_ref = refs[-1]
            kk = pl.program_id(2)

            @pl.when(kk == 0)
            def _():
                acc_ref[...] = p

            @pl.when(jnp.logical_and(kk > 0, kk < nk - 1))
            def _():
                acc_ref[...] += p

            @pl.when(kk == nk - 1)
            def _():
                finish(acc_ref[...] + p)

    scratch = [pltpu.VMEM((tm, tn), F32)] if nk > 1 else []
    return _ordered_call(
        body, name=name, out_shape=out_shape, grid=(nm, nn, nk), in_specs=in_specs, out_specs=out_spec,
        scratch_shapes=scratch, compiler_params=_params(("parallel", "parallel", "arbitrary")),
    )(*operands)


def _matmul_nt_slabs(a, b, *, tm, tn, name, a_parts=0):
    nslab, n, ks = b.shape
    m = a.shape[1] if a_parts else a.shape[0]
    tm, tn = min(tm, m), min(tn, n)
    assert m % tm == 0 and n % tn == 0, (name, m, n, tm, tn)
    if a_parts:
        per = nslab // a_parts
        assert per * a_parts == nslab and a.shape[2] == per * ks, (name, a.shape, b.shape)
        a_spec = pl.BlockSpec((a_parts, tm, per * ks), lambda i, j: (0, i, 0))
    else:
        assert a.shape[1] == nslab * ks, (name, a.shape, b.shape)
        a_spec = pl.BlockSpec((tm, nslab * ks), lambda i, j: (i, 0))

    def body(a_ref, b_ref, o_ref):
        total = None
        for sl in range(nslab):
            if a_parts:
                a_sl = a_ref[sl // per, :, (sl % per) * ks:(sl % per + 1) * ks]
            else:
                a_sl = a_ref[:, sl * ks:(sl + 1) * ks]
            p = lax.dot_general(a_sl, b_ref[sl], (((1,), (1,)), ((), ())), preferred_element_type=F32)
            total = p if total is None else total + p
        o_ref[...] = total

    return _ordered_call(
        body, name=name, out_shape=jax.ShapeDtypeStruct((m, n), F32), grid=(m // tm, n // tn),
        in_specs=[a_spec, pl.BlockSpec((nslab, tn, ks), lambda i, j: (0, j, 0))],
        out_specs=pl.BlockSpec((tm, tn), lambda i, j: (i, j)), compiler_params=_params(("parallel", "parallel")),
    )(a, b)


GRAD_HALVES = {
    "w_in": ("rows_of_slab", 1024, 896), "w_up": ("rows_of_slab", 1024, 1408), "w_o": ("rows_of_block", 256, 2048),
    "w_down": ("cols_of_block", 1408, 512)}


def _half_shape(name, shard_shape):
    r, cols = shard_shape
    return (r, cols // 2) if GRAD_HALVES[name][0] == "cols_of_block" else (r // 2, cols)


def _grad_half(name, a, g, sel, res, call_name, g_parts=0):
    kind, tm, tn = GRAD_HALVES[name]
    s, m = a.shape
    n = g.shape[0] * g.shape[2] if g_parts else g.shape[1]
    if kind == "rows_of_slab":
        rh, hc = m // 2, n // N_CHIPS
        per = hc // tn
        grid = (rh // tm, n // tn)
        a_map = lambda i, j, sel_ref: (0, sel_ref[0] * (rh // tm) + i)
        g_col = lambda i, j, sel_ref: j
        o_map = lambda i, j, sel_ref: (j // per, i, j % per)
    elif kind == "rows_of_block":
        rh, hc = m // N_CHIPS // 2, n
        assert tm == rh
        grid = (N_CHIPS, n // tn)
        a_map = lambda i, j, sel_ref: (0, 2 * i + sel_ref[0])
        g_col = lambda i, j, sel_ref: j
        o_map = lambda i, j, sel_ref: (i, 0, j)
    else:
        rh, hc = m // N_CHIPS, n // 2
        assert tm == rh
        grid = (N_CHIPS, hc // tn)
        a_map = lambda i, j, sel_ref: (0, i)
        g_col = lambda i, j, sel_ref: sel_ref[0] * (hc // tn) + j
        o_map = lambda i, j, sel_ref: (i, 0, j)
    if g_parts:
        g_per = (n // tn) // g_parts
        g_spec = pl.BlockSpec((None, s, tn), lambda i, j, sel_ref: (g_col(i, j, sel_ref) // g_per, 0, g_col(i, j, sel_ref) % g_per))
    else:
        g_spec = pl.BlockSpec((s, tn), lambda i, j, sel_ref: (0, g_col(i, j, sel_ref)))
    o_spec = pl.BlockSpec((None, tm, tn), o_map)
    in_specs = [pl.BlockSpec((s, tm), a_map), g_spec] + ([o_spec] if res is not None else [])

    def body(sel_ref, a_ref, g_ref, *rest):
        o_ref = rest[-1]
        p = lax.dot_general(a_ref[...], g_ref[...], (((0,), (0,)), ((), ())), preferred_element_type=F32)
        if res is not None:
            p = p + rest[0][...].astype(F32)
        o_ref[...] = p.astype(BF16)

    grid_spec = pltpu.PrefetchScalarGridSpec(num_scalar_prefetch=1, grid=grid, in_specs=in_specs, out_specs=o_spec)
    return _ordered_call(
        body, name=call_name, out_shape=jax.ShapeDtypeStruct((N_CHIPS, rh, hc), BF16), grid_spec=grid_spec,
        compiler_params=_params(("parallel", "parallel")),
    )(sel, a, g, *([res] if res is not None else []))


def _row_tile(s):
    return min(256, s)


def _rows(width, tr):
    return pl.BlockSpec((tr, width), lambda i: (i, 0))


def _const2(shape):
    return pl.BlockSpec(shape, lambda i: (0, 0))


def _rms_fwd(x, g, name):
    s, d = x.shape
    tr = _row_tile(s)

    def body(x_ref, g_ref, o_ref):
        xv = x_ref[...]
        r = lax.rsqrt(_mean_last(xv * xv) + EPS)
        o_ref[...] = (xv * r * g_ref[...]).astype(BF16)

    return _ordered_call(
        body, name=name, out_shape=jax.ShapeDtypeStruct((s, d), BF16), grid=(s // tr,),
        in_specs=[_rows(d, tr), _const2((1, d))], out_specs=_rows(d, tr), compiler_params=_params(("parallel",)),
    )(x, g)


def _rms_bwd(x, g, dh, dres, name):
    s, d = x.shape
    tr = _row_tile(s)

    def body(x_ref, g_ref, dh_ref, dres_ref, dx_ref, dxb_ref, dg_ref):
        xv, dy = x_ref[...], dh_ref[...]
        r = lax.rsqrt(_mean_last(xv * xv) + EPS)
        gdy = dy * g_ref[...]
        dx = dres_ref[...] + r * gdy - xv * ((r * r * r) * _mean_last(xv * gdy))
        dx_ref[...] = dx
        dxb_ref[...] = dx.astype(BF16)

        @pl.when(pl.program_id(0) == 0)
        def _():
            dg_ref[...] = jnp.zeros_like(dg_ref)

        dg_ref[...] += _sum_rows(xv * r * dy)

    return _ordered_call(
        body, name=name,
        out_shape=(jax.ShapeDtypeStruct((s, d), F32), jax.ShapeDtypeStruct((s, d), BF16), jax.ShapeDtypeStruct((1, d), F32)),
        grid=(s // tr,), in_specs=[_rows(d, tr), _const2((1, d)), _rows(d, tr), _rows(d, tr)],
        out_specs=(_rows(d, tr), _rows(d, tr), _const2((1, d))), compiler_params=_params(("arbitrary",)),
    )(x, g, dh, dres)


Q0, K0, V0, GU0, GV0 = 0, ATTN_WIDTH, ATTN_WIDTH + KV_WIDTH, ATTN_WIDTH + 2 * KV_WIDTH, ATTN_WIDTH + 2 * KV_WIDTH + GMLP_WIDTH


def _head(h, base=0):
    return slice(base + h * HEAD_DIM, base + (h + 1) * HEAD_DIM)


def _proj_post(z, qg, kg, lg, lb, cosf, sinf, name):
    s = z.shape[0]
    tr = _row_tile(s)

    def body(z_ref, qg_ref, kg_ref, lg_ref, lb_ref, cos_ref, sin_ref, qn_ref, kn_ref, vb_ref, ug_ref, vn_ref,
             dgu_ref, dgv_ref, xhat_ref, rstd_ref):
        cos, sin = cos_ref[...], sin_ref[...]

        def norm_rope(xh, g):
            y = xh * lax.rsqrt(_mean_last(xh * xh) + EPS) * g
            return y * cos + pltpu.roll(y, HEAD_DIM // 2, 1) * sin

        for h in range(N_Q_HEADS):
            qn_ref[:, _head(h)] = norm_rope(z_ref[:, _head(h, Q0)].astype(F32), qg_ref[...]).astype(BF16)
        for h in range(N_KV_HEADS):
            kn_ref[:, _head(h)] = norm_rope(z_ref[:, _head(h, K0)].astype(F32), kg_ref[...]).astype(BF16)
        vb_ref[...] = z_ref[:, V0:GU0]
        gu = z_ref[:, GU0:GV0].astype(F32)
        ug_ref[...] = _gelu(gu)
        dgu_ref[...] = _gelu_grad(gu).astype(BF16)
        gv = z_ref[:, GV0:IN_WIDTH].astype(F32)
        vg = _gelu(gv)
        dgv_ref[...] = _gelu_grad(gv).astype(BF16)
        xc = vg - _mean_last(vg)
        r = lax.rsqrt(_mean_last(xc * xc) + EPS)
        y = xc * r
        xhat_ref[...] = y.astype(BF16)
        rstd_ref[...] = r
        vn_ref[...] = (y * lg_ref[...] + lb_ref[...]).astype(BF16)

    wide = jax.ShapeDtypeStruct((s, GMLP_WIDTH), BF16)
    return _ordered_call(
        body, name=name,
        out_shape=(jax.ShapeDtypeStruct((s, ATTN_WIDTH), BF16), jax.ShapeDtypeStruct((s, KV_WIDTH), BF16),
                   jax.ShapeDtypeStruct((s, KV_WIDTH), BF16), jax.ShapeDtypeStruct((s, GMLP_WIDTH), F32), wide,
                   wide, wide, wide, jax.ShapeDtypeStruct((s, 1), F32)),
        grid=(s // tr,),
        in_specs=[_rows(IN_WIDTH, tr), _const2((1, HEAD_DIM)), _const2((1, HEAD_DIM)), _const2((1, GMLP_WIDTH)),
                  _const2((1, GMLP_WIDTH)), _rows(HEAD_DIM, tr), _rows(HEAD_DIM, tr)],
        out_specs=(_rows(ATTN_WIDTH, tr), _rows(KV_WIDTH, tr), _rows(KV_WIDTH, tr), _rows(GMLP_WIDTH, tr), _rows(GMLP_WIDTH, tr),
                   _rows(GMLP_WIDTH, tr), _rows(GMLP_WIDTH, tr), _rows(GMLP_WIDTH, tr), _rows(1, tr)),
        compiler_params=_params(("parallel",)),
    )(z, qg, kg, lg, lb, cosf, sinf)


def _proj_post_bwd(z, dqn, dkn, dvb, dug, dvn, gelu_grad_u, gelu_grad_v, xhat_v, rstd_v, qg, kg, lg, cosf, sinf, name):
    s = z.shape[0]
    tr = _row_tile(s)

    def body(z_ref, dqn_ref, dkn_ref, dvb_ref, dug_ref, dvn_ref, ggu_ref, ggv_ref, xhat_ref, rstd_ref, qg_ref, kg_ref, lg_ref,
             cos_ref, sin_ref, dz_ref, dqg_ref, dkg_ref, dlg_ref, dlb_ref):
        cos, sin = cos_ref[...], sin_ref[...]

        @pl.when(pl.program_id(0) == 0)
        def _():
            dqg_ref[...] = jnp.zeros_like(dqg_ref)
            dkg_ref[...] = jnp.zeros_like(dkg_ref)
            dlg_ref[...] = jnp.zeros_like(dlg_ref)
            dlb_ref[...] = jnp.zeros_like(dlb_ref)

        def norm_rope_bwd(xh, g, dout):
            dy = dout * cos - pltpu.roll(dout, HEAD_DIM // 2, 1) * sin
            r = lax.rsqrt(_mean_last(xh * xh) + EPS)
            xhat = xh * r
            gdy = dy * g
            return r * (gdy - xhat * _mean_last(xhat * gdy)), _sum_rows(xhat * dy)

        dqg = jnp.zeros((1, HEAD_DIM), F32)
        for h in range(N_Q_HEADS):
            dx, dg = norm_rope_bwd(z_ref[:, _head(h, Q0)].astype(F32), qg_ref[...], dqn_ref[:, _head(h)])
            dz_ref[:, _head(h, Q0)] = dx.astype(BF16)
            dqg = dqg + dg
        dqg_ref[...] += dqg
        dkg = jnp.zeros((1, HEAD_DIM), F32)
        for h in range(N_KV_HEADS):
            dx, dg = norm_rope_bwd(z_ref[:, _head(h, K0)].astype(F32), kg_ref[...], dkn_ref[:, _head(h)])
            dz_ref[:, _head(h, K0)] = dx.astype(BF16)
            dkg = dkg + dg
        dkg_ref[...] += dkg
        dz_ref[:, V0:GU0] = dvb_ref[...].astype(BF16)
        dz_ref[:, GU0:GV0] = (dug_ref[...] * ggu_ref[...].astype(F32)).astype(BF16)
        xhat = xhat_ref[...].astype(F32)
        dvn_v = dvn_ref[...]
        dlg_ref[...] += _sum_rows(xhat * dvn_v)
        dlb_ref[...] += _sum_rows(dvn_v)
        dxh = dvn_v * lg_ref[...]
        dvg = rstd_ref[...] * (dxh - _mean_last(dxh) - xhat * _mean_last(dxh * xhat))
        dz_ref[:, GV0:IN_WIDTH] = (dvg * ggv_ref[...].astype(F32)).astype(BF16)

    return _ordered_call(
        body, name=name,
        out_shape=(jax.ShapeDtypeStruct((s, IN_WIDTH), BF16), jax.ShapeDtypeStruct((1, HEAD_DIM), F32),
                   jax.ShapeDtypeStruct((1, HEAD_DIM), F32), jax.ShapeDtypeStruct((1, GMLP_WIDTH), F32),
                   jax.ShapeDtypeStruct((1, GMLP_WIDTH), F32)),
        grid=(s // tr,),
        in_specs=[_rows(V0, tr), _rows(ATTN_WIDTH, tr), _rows(KV_WIDTH, tr), _rows(KV_WIDTH, tr), _rows(GMLP_WIDTH, tr),
                  _rows(GMLP_WIDTH, tr), _rows(GMLP_WIDTH, tr), _rows(GMLP_WIDTH, tr), _rows(GMLP_WIDTH, tr), _rows(1, tr),
                  _const2((1, HEAD_DIM)), _const2((1, HEAD_DIM)), _const2((1, GMLP_WIDTH)), _rows(HEAD_DIM, tr),
                  _rows(HEAD_DIM, tr)],
        out_specs=(_rows(IN_WIDTH, tr), _const2((1, HEAD_DIM)), _const2((1, HEAD_DIM)), _const2((1, GMLP_WIDTH)),
                   _const2((1, GMLP_WIDTH))),
        compiler_params=_params(("arbitrary",)),
    )(z, dqn, dkn, dvb, dug, dvn, gelu_grad_u, gelu_grad_v, xhat_v, rstd_v, qg, kg, lg, cosf, sinf)


def _band_valid(n, s):
    shape = (GQA_GROUP * BLOCK, 3 * BLOCK)
    i = lax.broadcasted_iota(jnp.int32, shape, 0) & (BLOCK - 1)
    j = lax.broadcasted_iota(jnp.int32, shape, 1)
    k_pos = n * BLOCK - BLOCK + j
    return (jnp.abs(j - BLOCK - i) <= BLOCK) & (k_pos >= 0) & (k_pos < s)


def _group_rows(x, kh):
    return jnp.concatenate([x[:, _head(kh * GQA_GROUP + g)] for g in range(GQA_GROUP)], axis=0)


def _group_sinks(sink_ref, kh):
    return jnp.concatenate([jnp.full((BLOCK, 1), sink_ref[kh * GQA_GROUP + g], F32) for g in range(GQA_GROUP)], axis=0)


def _rows_of(x, g):
    return x[g * BLOCK:(g + 1) * BLOCK]


def _probs(q, kb, sink_h, valid):
    sc = lax.dot_general(q, kb, (((1,), (1,)), ((), ())), preferred_element_type=F32) * (HEAD_DIM ** -0.5)
    sc = jnp.where(valid, sc, MASK_VALUE)
    m = jnp.maximum(jnp.max(sc, axis=-1, keepdims=True), sink_h)
    p = jnp.exp(sc - m)
    es = jnp.exp(sink_h - m)
    den = jnp.sum(p, axis=-1, keepdims=True) + es
    inv = 1.0 / den
    return p * inv, es * inv


def _band_specs(width, nb):
    return [pl.BlockSpec((BLOCK, width), lambda n: (jnp.maximum(n - 1, 0), 0)),
            pl.BlockSpec((BLOCK, width), lambda n: (n, 0)),
            pl.BlockSpec((BLOCK, width), lambda n: (jnp.minimum(n + 1, nb - 1), 0))]


def _blk(width):
    return pl.BlockSpec((BLOCK, width), lambda n: (n, 0))


def _whole3(shape):
    return pl.BlockSpec(shape, lambda n: (0, 0, 0))


def _smem():
    return pl.BlockSpec(memory_space=pltpu.SMEM)


def _mixer_fwd(qn, kn, vb, ug, vn, wsb, bsb, sink, ga, gs, name):
    s = qn.shape[0]
    nb = s // BLOCK

    def body(sink_ref, q_ref, kp_ref, kc_ref, kx_ref, vp_ref, vc_ref, vx_ref, ug_ref, vn_ref, ws_ref, bs_ref, ga_ref, gs_ref,
             attn_ref, sgu_ref, mix_ref, probs_ref, psink_ref):
        n = pl.program_id(0)
        valid = _band_valid(n, s)
        ssq = jnp.zeros((BLOCK, 1), F32)
        for kh in range(N_KV_HEADS):
            kb = jnp.concatenate([kp_ref[:, _head(kh)], kc_ref[:, _head(kh)], kx_ref[:, _head(kh)]], axis=0)
            vbd = jnp.concatenate([vp_ref[:, _head(kh)], vc_ref[:, _head(kh)], vx_ref[:, _head(kh)]], axis=0)
            p, p_sink = _probs(_group_rows(q_ref, kh), kb, _group_sinks(sink_ref, kh), valid)
            pb = p.astype(BF16)
            probs_ref[kh] = pb
            psink_ref[kh] = p_sink
            o4 = jnp.dot(pb, vbd, preferred_element_type=F32)
            for g in range(GQA_GROUP):
                o = _rows_of(o4, g)
                attn_ref[:, _head(kh * GQA_GROUP + g)] = o
                ssq = ssq + jnp.sum(o * o, axis=-1, keepdims=True)
        r = lax.rsqrt(ssq * (1.0 / ATTN_WIDTH) + EPS)
        mix_ref[:, 0:ATTN_WIDTH] = (attn_ref[...] * r * ga_ref[...]).astype(BF16)
        ssq = jnp.zeros((BLOCK, 1), F32)
        for h in range(N_GMLP_HEADS):
            f = jnp.dot(ws_ref[h], vn_ref[:, _head(h)], preferred_element_type=F32) + bs_ref[h]
            o = ug_ref[:, _head(h)] * f
            sgu_ref[:, _head(h)] = o
            ssq = ssq + jnp.sum(o * o, axis=-1, keepdims=True)
        r = lax.rsqrt(ssq * (1.0 / GMLP_WIDTH) + EPS)
        mix_ref[:, ATTN_WIDTH:D_MODEL] = (sgu_ref[...] * r * gs_ref[...]).astype(BF16)

    hh = (N_GMLP_HEADS, BLOCK, BLOCK)
    return _ordered_call(
        body, name=name,
        out_shape=(jax.ShapeDtypeStruct((s, ATTN_WIDTH), F32), jax.ShapeDtypeStruct((s, GMLP_WIDTH), F32),
                   jax.ShapeDtypeStruct((s, D_MODEL), BF16), jax.ShapeDtypeStruct((nb,) + PROBS_BLOCK, BF16),
                   jax.ShapeDtypeStruct((nb,) + PSINK_BLOCK, F32)),
        grid=(nb,),
        in_specs=[_smem(), _blk(ATTN_WIDTH)] + _band_specs(KV_WIDTH, nb) + _band_specs(KV_WIDTH, nb)
        + [_blk(GMLP_WIDTH), _blk(GMLP_WIDTH), _whole3(hh), _whole3(hh),
           pl.BlockSpec((1, ATTN_WIDTH), lambda n: (0, 0)), pl.BlockSpec((1, GMLP_WIDTH), lambda n: (0, 0))],
        out_specs=(_blk(ATTN_WIDTH), _blk(GMLP_WIDTH), _blk(D_MODEL), _per_block(PROBS_BLOCK), _per_block(PSINK_BLOCK)),
        compiler_params=_params(("parallel",)),
    )(sink, qn, kn, kn, kn, vb, vb, vb, ug, vn, wsb, bsb, ga, gs)


PROBS_BLOCK = (N_KV_HEADS, GQA_GROUP * BLOCK, 3 * BLOCK)
PSINK_BLOCK = (N_KV_HEADS, GQA_GROUP * BLOCK, 1)


def _per_block(shape):
    return pl.BlockSpec((None,) + shape, lambda n: (n, 0, 0, 0))


def _mixer_bwd(qn, kn, vb, ug, vn, attn, sgu, dmixed, wsb, bsb, ga, gs, probs, psink, name):
    s = qn.shape[0]
    nb = s // BLOCK
    tn_dims = (((0,), (0,)), ((), ()))
    nt_dims = (((1,), (1,)), ((), ()))

    def body(q_ref, kp_ref, kc_ref, kx_ref, vp_ref, vc_ref, vx_ref, ug_ref, vn_ref, attn_ref, sgu_ref, dm_ref,
             ws_ref, bs_ref, ga_ref, gs_ref, probs_ref, psink_ref,
             dq_ref, dk_ref, dv_ref, dug_ref, dvn_ref, dws_ref, dbs_ref, dsk_ref, dga_ref, dgs_ref, dk_acc, dv_acc):
        n = pl.program_id(0)

        @pl.when(n == 0)
        def _():
            for ref in (dk_acc, dv_acc, dws_ref, dbs_ref, dsk_ref, dga_ref, dgs_ref):
                ref[...] = jnp.zeros_like(ref)

        def out_norm_bwd(o, g, dy):
            r = lax.rsqrt(_mean_last(o * o) + EPS)
            gdy = dy * g
            return r * gdy - o * ((r * r * r) * _mean_last(o * gdy)), _sum_rows(o * r * dy)

        d_attn, dga = out_norm_bwd(attn_ref[...], ga_ref[...], dm_ref[:, 0:ATTN_WIDTH])
        dga_ref[...] += dga
        d_sgu, dgs = out_norm_bwd(sgu_ref[...], gs_ref[...], dm_ref[:, ATTN_WIDTH:D_MODEL])
        dgs_ref[...] += dgs

        for h in range(N_GMLP_HEADS):
            vn_h = vn_ref[:, _head(h)]
            f = jnp.dot(ws_ref[h], vn_h, preferred_element_type=F32) + bs_ref[h]
            ds_h = d_sgu[:, _head(h)]
            dug_ref[:, _head(h)] = ds_h * f
            df = ds_h * ug_ref[:, _head(h)]
            dfb = df.astype(BF16)
            dvn_ref[:, _head(h)] = lax.dot_general(ws_ref[h], dfb, tn_dims, preferred_element_type=F32)
            dws_ref[h] += lax.dot_general(dfb, vn_h, nt_dims, preferred_element_type=F32)
            dbs_ref[h] += jnp.broadcast_to(jnp.sum(df, axis=-1, keepdims=True), (BLOCK, BLOCK))

        row0 = pl.multiple_of(n * BLOCK, BLOCK)
        for kh in range(N_KV_HEADS):
            kb = jnp.concatenate([kp_ref[:, _head(kh)], kc_ref[:, _head(kh)], kx_ref[:, _head(kh)]], axis=0)
            vbd = jnp.concatenate([vp_ref[:, _head(kh)], vc_ref[:, _head(kh)], vx_ref[:, _head(kh)]], axis=0)
            q4 = _group_rows(q_ref, kh)
            pb = probs_ref[kh]
            p = pb.astype(F32)
            do4 = _group_rows(d_attn, kh).astype(BF16)
            dp = lax.dot_general(do4, vbd, nt_dims, preferred_element_type=F32)
            delta = jnp.sum(p * dp, axis=-1, keepdims=True)
            dsc = (p * (dp - delta) * (HEAD_DIM ** -0.5)).astype(BF16)
            d_sink = -(psink_ref[kh] * delta)
            dq4 = jnp.dot(dsc, kb, preferred_element_type=F32)
            for g in range(GQA_GROUP):
                h = kh * GQA_GROUP + g
                dsk_ref[h:h + 1, :] += jnp.broadcast_to(_sum_all(_rows_of(d_sink, g)), (1, BLOCK))
                dq_ref[:, _head(h)] = _rows_of(dq4, g)
            dk_acc[pl.ds(row0, 3 * BLOCK), _head(kh)] += lax.dot_general(dsc, q4, tn_dims, preferred_element_type=F32)
            dv_acc[pl.ds(row0, 3 * BLOCK), _head(kh)] += lax.dot_general(pb, do4, tn_dims, preferred_element_type=F32)

        @pl.when(n == nb - 1)
        def _():
            dk_ref[...] = dk_acc[BLOCK:BLOCK + s, :]
            dv_ref[...] = dv_acc[BLOCK:BLOCK + s, :]

    hh = (N_GMLP_HEADS, BLOCK, BLOCK)
    full_kv = pl.BlockSpec((s, KV_WIDTH), lambda n: (0, 0))
    return _ordered_call(
        body, name=name,
        out_shape=(jax.ShapeDtypeStruct((s, ATTN_WIDTH), F32), jax.ShapeDtypeStruct((s, KV_WIDTH), F32),
                   jax.ShapeDtypeStruct((s, KV_WIDTH), F32), jax.ShapeDtypeStruct((s, GMLP_WIDTH), F32),
                   jax.ShapeDtypeStruct((s, GMLP_WIDTH), F32), jax.ShapeDtypeStruct(hh, F32), jax.ShapeDtypeStruct(hh, F32),
                   jax.ShapeDtypeStruct((N_Q_HEADS, BLOCK), F32), jax.ShapeDtypeStruct((1, ATTN_WIDTH), F32),
                   jax.ShapeDtypeStruct((1, GMLP_WIDTH), F32)),
        grid=(nb,),
        in_specs=[_blk(ATTN_WIDTH)] + _band_specs(KV_WIDTH, nb) + _band_specs(KV_WIDTH, nb)
        + [_blk(GMLP_WIDTH), _blk(GMLP_WIDTH), _blk(ATTN_WIDTH), _blk(GMLP_WIDTH), _blk(D_MODEL), _whole3(hh), _whole3(hh),
           pl.BlockSpec((1, ATTN_WIDTH), lambda n: (0, 0)), pl.BlockSpec((1, GMLP_WIDTH), lambda n: (0, 0)),
           _per_block(PROBS_BLOCK), _per_block(PSINK_BLOCK)],
        out_specs=(_blk(ATTN_WIDTH), full_kv, full_kv, _blk(GMLP_WIDTH), _blk(GMLP_WIDTH), _whole3(hh), _whole3(hh),
                   pl.BlockSpec((N_Q_HEADS, BLOCK), lambda n: (0, 0)), pl.BlockSpec((1, ATTN_WIDTH), lambda n: (0, 0)),
                   pl.BlockSpec((1, GMLP_WIDTH), lambda n: (0, 0))),
        scratch_shapes=[pltpu.VMEM((s + 2 * BLOCK, KV_WIDTH), F32), pltpu.VMEM((s + 2 * BLOCK, KV_WIDTH), F32)],
        compiler_params=_params(("arbitrary",)),
    )(qn, kn, kn, kn, vb, vb, vb, ug, vn, attn, sgu, dmixed, wsb, bsb, ga, gs, probs, psink)


CONV_TILE = 128


PAD_ROWS = 8


def _zero_pad_rows(pad_ref):
    s = pad_ref.shape[0] - 2 * PAD_ROWS
    zeros = jnp.zeros((PAD_ROWS, pad_ref.shape[1]), F32)
    pad_ref[0:PAD_ROWS, :] = zeros
    pad_ref[PAD_ROWS + s:2 * PAD_ROWS + s, :] = zeros


def _shift_rows(a, pad_ref):
    s = a.shape[0]
    pad_ref[PAD_ROWS:PAD_ROWS + s, :] = a
    padded = pad_ref[...]
    prev = pltpu.roll(padded, 1, 0)[PAD_ROWS:PAD_ROWS + s]
    nxt = pltpu.roll(padded, s + 2 * PAD_ROWS - 1, 0)[PAD_ROWS:PAD_ROWS + s]
    return prev, nxt


def _conv_specs(s):
    tc = CONV_TILE
    nj = D_FF // tc
    return (tc, nj, pl.BlockSpec((2, s, tc), lambda j: (0, 0, j)),
            [pl.BlockSpec((3, tc), lambda j: (0, j)), pl.BlockSpec((3, tc), lambda j: (0, j + nj))],
            [pl.BlockSpec((1, tc), lambda j: (0, j)), pl.BlockSpec((1, tc), lambda j: (0, j + nj))])


def _conv_gate_fwd(a_pre, cw, cb, name):
    s = a_pre.shape[1]
    tc, nj, a_spec, w_specs, b_specs = _conv_specs(s)

    def body(a_ref, wg_ref, wu_ref, bg_ref, bu_ref, act_ref, dgu_ref, pad_ref):
        _zero_pad_rows(pad_ref)

        def conv(a, w_ref, b_ref):
            prev, nxt = _shift_rows(a, pad_ref)
            return b_ref[...] + prev * w_ref[0:1, :] + a * w_ref[1:2, :] + nxt * w_ref[2:3, :]

        g = conv(a_ref[0].astype(F32), wg_ref, bg_ref)
        u = conv(a_ref[1].astype(F32), wu_ref, bu_ref)
        sg = 1.0 / (1.0 + jnp.exp(-g))
        silu = g * sg
        act_ref[...] = (silu * u).astype(BF16)
        dgu_ref[0] = (u * (sg * (1.0 + g * (1.0 - sg)))).astype(BF16)
        dgu_ref[1] = silu.astype(BF16)

    return _ordered_call(
        body, name=name, out_shape=(jax.ShapeDtypeStruct((s, D_FF), BF16), jax.ShapeDtypeStruct((2, s, D_FF), BF16)),
        grid=(nj,), in_specs=[a_spec] + w_specs + b_specs,
        out_specs=(pl.BlockSpec((s, tc), lambda j: (0, j)), pl.BlockSpec((2, s, tc), lambda j: (0, 0, j))),
        scratch_shapes=[pltpu.VMEM((s + 2 * PAD_ROWS, tc), F32)], compiler_params=_params(("parallel",)),
    )(a_pre, cw, cw, cb, cb)


def _conv_gate_bwd(a_pre, dgu, cw, dact, name):
    s = a_pre.shape[1]
    tc, nj, a_spec, w_specs, _ = _conv_specs(s)

    def body(a_ref, dgu_ref, wg_ref, wu_ref, dact_ref, dap_ref, dcw_ref, dcb_ref, pad_ref):
        _zero_pad_rows(pad_ref)
        dact_v = dact_ref[...].astype(F32)
        for part, w_ref in enumerate((wg_ref, wu_ref)):
            da = dact_v * dgu_ref[part].astype(F32)
            a = a_ref[part].astype(F32)
            da_prev, da_next = _shift_rows(da, pad_ref)
            dcw_ref[part, 0:1, :] = _sum_rows(a * da_next)
            dcw_ref[part, 1:2, :] = _sum_rows(a * da)
            dcw_ref[part, 2:3, :] = _sum_rows(a * da_prev)
            dcb_ref[part] = _sum_rows(da)
            dap_ref[part] = (da_next * w_ref[0:1, :] + da * w_ref[1:2, :] + da_prev * w_ref[2:3, :]).astype(BF16)

    return _ordered_call(
        body, name=name,
        out_shape=(jax.ShapeDtypeStruct((2, s, D_FF), BF16), jax.ShapeDtypeStruct((2, 3, D_FF), F32),
                   jax.ShapeDtypeStruct((2, 1, D_FF), F32)),
        grid=(nj,),
        in_specs=[a_spec, pl.BlockSpec((2, s, tc), lambda j: (0, 0, j))] + w_specs + [pl.BlockSpec((s, tc), lambda j: (0, j))],
        out_specs=(pl.BlockSpec((2, s, tc), lambda j: (0, 0, j)), pl.BlockSpec((2, 3, tc), lambda j: (0, 0, j)),
                   pl.BlockSpec((2, 1, tc), lambda j: (0, 0, j))),
        scratch_shapes=[pltpu.VMEM((s + 2 * PAD_ROWS, tc), F32)], compiler_params=_params(("parallel",)),
    )(a_pre, dgu, cw, cw, dact)


def _loss_head(y, target, name):
    s, d = y.shape
    tr = _row_tile(s)

    def body(y_ref, t_ref, loss_ref, dy_ref, dyb_ref):
        err = y_ref[...] - t_ref[...]

        @pl.when(pl.program_id(0) == 0)
        def _():
            loss_ref[...] = jnp.zeros_like(loss_ref)

        loss_ref[...] += jnp.broadcast_to(0.5 * _sum_all(_mean_last(err * err)), (8, 128))
        dy = err * (1.0 / d)
        dy_ref[...] = dy
        dyb_ref[...] = dy.astype(BF16)

    return _ordered_call(
        body, name=name,
        out_shape=(jax.ShapeDtypeStruct((8, 128), F32), jax.ShapeDtypeStruct((s, d), F32), jax.ShapeDtypeStruct((s, d), BF16)),
        grid=(s // tr,), in_specs=[_rows(d, tr), _rows(d, tr)],
        out_specs=(_const2((8, 128)), _rows(d, tr), _rows(d, tr)), compiler_params=_params(("arbitrary",)),
    )(y, target)


def _row_block(rows, cols, budget=1 << 20):
    if rows * cols <= budget:
        return rows
    best = None
    for tr in range(16, rows, 16):
        if rows % tr == 0 and tr * cols <= budget:
            best = tr
    assert best is not None, (rows, cols)
    return best


def _place_shard(x4, layer, j_arr, out_dtype, name):
    _, nh, r, cols = x4.shape
    tr = _row_block(r, cols)

    def body(j_ref, x_ref, o_ref):
        o_ref[...] = x_ref[...].astype(out_dtype)

    grid_spec = pltpu.PrefetchScalarGridSpec(
        num_scalar_prefetch=1, grid=(nh, r // tr),
        in_specs=[pl.BlockSpec((None, None, tr, cols), lambda h, i, j_ref: (layer, h, i, 0))],
        out_specs=pl.BlockSpec((None, None, tr, cols), lambda h, i, j_ref: (j_ref[0], h, i, 0)))
    return _ordered_call(
        body, name=name, out_shape=jax.ShapeDtypeStruct((N_CHIPS, nh, r, cols), out_dtype), grid_spec=grid_spec,
        compiler_params=_params(("parallel", "parallel")),
    )(j_arr, x4)


def _adamw(w, g, m, v, name):
    rows, cols = w.shape
    tr = _row_block(rows, cols, 1 << 18)

    def body(w_ref, g_ref, m_ref, v_ref, go_ref, d_ref, nm_ref, nv_ref):
        gv = g_ref[...]
        go_ref[...] = gv
        mn = ADAM_B1 * m_ref[...] + (1.0 - ADAM_B1) * gv
        vn = ADAM_B2 * v_ref[...] + (1.0 - ADAM_B2) * (gv * gv)
        m_hat = mn / (1.0 - ADAM_B1 ** ADAM_STEP)
        v_hat = vn / (1.0 - ADAM_B2 ** ADAM_STEP)
        d_ref[...] = -ADAM_LR * (m_hat / (jnp.sqrt(v_hat) + ADAM_EPS) + ADAM_WD * w_ref[...])
        nm_ref[...] = mn
        nv_ref[...] = vn

    sds = jax.ShapeDtypeStruct((rows, cols), F32)
    return _ordered_call(
        body, name=name, out_shape=(sds, sds, sds, sds), grid=(rows // tr,),
        in_specs=[_rows(cols, tr)] * 4, out_specs=(_rows(cols, tr),) * 4, compiler_params=_params(("parallel",)),
    )(w, g, m, v)


def _chip_sum(p4, recv3, j_arr, c_arr, name):
    _, rh, cols = p4.shape
    tr = _row_block(rh, cols, 1 << 19)

    def body(j_ref, c_ref, p_ref, r_ref, o_ref):
        total = p_ref[...].astype(F32)
        for peer in range(3):
            total = total + r_ref[peer].astype(F32)
        o_ref[...] = total

    grid_spec = pltpu.PrefetchScalarGridSpec(
        num_scalar_prefetch=2, grid=(rh // tr,),
        in_specs=[pl.BlockSpec((None, tr, cols), lambda i, j_ref, c_ref: (j_ref[0], i, 0)),
                  pl.BlockSpec((3, tr, cols), lambda i, j_ref, c_ref: (0, i, 0))],
        out_specs=pl.BlockSpec((None, tr, cols), lambda i, j_ref, c_ref: (c_ref[0], i, 0)))
    return _ordered_call(
        body, name=name, out_shape=jax.ShapeDtypeStruct((2, rh, cols), F32), grid_spec=grid_spec,
        compiler_params=_params(("parallel",)),
    )(j_arr, c_arr, p4, recv3)


def _adamw_layer(w, g, m, v, layer, into, name):
    nl, rows, cols = w.shape
    slabs, _, width = g.shape
    assert slabs * width == cols and g.shape[1] == rows, (name, w.shape, g.shape)
    tr = _row_block(rows, width, 1 << 18)
    at_layer = pl.BlockSpec((None, tr, width), lambda h, i: (layer, i, h))

    def body(w_ref, g_ref, m_ref, v_ref, *rest):
        go_ref, d_ref, nm_ref, nv_ref = rest[-4:]
        gv = g_ref[...]
        go_ref[...] = gv
        mn = ADAM_B1 * m_ref[...] + (1.0 - ADAM_B1) * gv
        vn = ADAM_B2 * v_ref[...] + (1.0 - ADAM_B2) * (gv * gv)
        m_hat = mn / (1.0 - ADAM_B1 ** ADAM_STEP)
        v_hat = vn / (1.0 - ADAM_B2 ** ADAM_STEP)
        d_ref[...] = -ADAM_LR * (m_hat / (jnp.sqrt(v_hat) + ADAM_EPS) + ADAM_WD * w_ref[...])
        nm_ref[...] = mn
        nv_ref[...] = vn

    in_specs = [at_layer, pl.BlockSpec((None, tr, width), lambda h, i: (h, i, 0)), at_layer, at_layer]
    operands = [w, g, m, v]
    aliases = {}
    if into is not None:
        in_specs += [ANY] * 4
        operands += list(into)
        aliases = {4 + i: i for i in range(4)}
    sds = jax.ShapeDtypeStruct((nl, rows, cols), F32)
    return _ordered_call(
        body, name=name, out_shape=(sds,) * 4, grid=(slabs, rows // tr), in_specs=in_specs, out_specs=(at_layer,) * 4,
        input_output_aliases=aliases, compiler_params=_params(("parallel", "parallel")),
    )(*operands)


def _sum_devices(mine, landed, me_arr, name):
    rows, lanes = mine.shape

    def body(me_ref, mine_ref, landed_ref, o_ref):
        total = None
        for dev in range(8):
            part = jnp.where(me_ref[0] == dev, mine_ref[...], landed_ref[dev])
            total = part if total is None else total + part
        o_ref[...] = total

    grid_spec = pltpu.PrefetchScalarGridSpec(
        num_scalar_prefetch=1, grid=(1,),
        in_specs=[pl.BlockSpec((rows, lanes), lambda i, me_ref: (0, 0)), pl.BlockSpec((8, rows, lanes), lambda i, me_ref: (0, 0, 0))],
        out_specs=pl.BlockSpec((rows, lanes), lambda i, me_ref: (0, 0)))
    return _ordered_call(
        body, name=name, out_shape=jax.ShapeDtypeStruct((rows, lanes), F32), grid_spec=grid_spec,
        compiler_params=_params(("arbitrary",)),
    )(me_arr, mine, landed)


def _place():
    x, y, c = lax.axis_index("x"), lax.axis_index("y"), lax.axis_index("c")
    chips = [(1 - x, y), (x, 1 - y), (1 - x, 1 - y)]
    return x, y, c, chips


HBM = pl.BlockSpec(memory_space=pltpu.HBM)
SEM = pl.BlockSpec(memory_space=pltpu.SEMAPHORE)
TOKEN = jax.ShapeDtypeStruct((8, 128), F32)


def _remote(src, dst, send_sem, recv_sem, to):
    return pltpu.make_async_remote_copy(src_ref=src, dst_ref=dst, send_sem=send_sem, recv_sem=recv_sem, device_id=to,
                                        device_id_type=MESH)


def _split_call(body, name, thru, sems_in=(), fresh=(), new_sems=(), after_last=True):
    n_t, n_s, n_f = len(thru), len(sems_in), len(fresh)

    def call_body(*refs):
        outs = refs[n_t + n_s:]
        body(refs[:n_t], refs[n_t:n_t + n_s], outs[1 + n_t:1 + n_t + n_f], outs[1 + n_t + n_f:])
        outs[0][...] = jnp.zeros_like(outs[0])

    out_shape = ([TOKEN] + [pltpu.HBM(t.shape, t.dtype) for t in thru] + [pltpu.HBM(shp, dt) for shp, dt in fresh]
                 + [pltpu.SemaphoreType.DMA(shp) for shp in new_sems])
    out_specs = [pl.BlockSpec(memory_space=pltpu.VMEM)] + [HBM] * (n_t + n_f) + [SEM] * len(new_sems)
    if not after_last:
        _Order.last = None
    out = _ordered_call(
        call_body, name=name, out_shape=tuple(out_shape), in_specs=[HBM] * n_t + [SEM] * n_s, out_specs=tuple(out_specs),
        input_output_aliases={i: 1 + i for i in range(n_t)},
        compiler_params=pltpu.CompilerParams(has_side_effects=pltpu.SideEffectType.DATAFLOW_SIDE_EFFECTING),
    )(*[pltpu.with_memory_space_constraint(t, pltpu.HBM) for t in thru], *sems_in)
    return out[1:1 + n_t], out[1 + n_t:1 + n_t + n_f], out[1 + n_t + n_f:]


class _Exchange:
    def __init__(self, weights, m_in, v_in, j_arr, c_arr, me_arr):
        self.w, self.m, self.v = weights, m_in, v_in
        self.j_arr, self.c_arr, self.me_arr = j_arr, c_arr, me_arr
        self.adam, self.small, self.pairs, self.held = {}, {}, {}, None
        self.o_arr = 1 - c_arr
        self.groups = [(l, name) for l in range(DEPTH) for name in BIG_NAMES]
        self.shard_shape = {name: weights[name].shape[1:] for name in BIG_NAMES}
        self.conv_state, self.state = [], {}
        self.ready, self.conv_ready = {}, {}
        self.pending, self.tick, self.reduced = [], 0, {}

        def place(grp):
            l, name = grp
            nl, r, cols = weights[name].shape
            return _place_shard(weights[name].reshape(nl, 2, r // 2, cols), l, j_arr, BF16, f"place_{name}_l{l}")

        def start_copies(tag, convs, groups, bufs):
            n_c = len(convs)

            def start(thru, _, __, sems):
                x, y, c, chips = _place()
                j_me = 2 * x + y
                copies = []
                for i in range(len(thru)):
                    mine = thru[i].at[j_me] if i < n_c else thru[i].at[j_me, c]
                    copies += [_remote(mine, mine, sems[2 * i].at[k], sems[2 * i + 1].at[k], (*chip, c))
                               for k, chip in enumerate(chips)]
                for cp in copies:
                    cp.start()

            thru, _, sems = _split_call(start, tag, convs + bufs, new_sems=[(3,)] * (2 * (n_c + len(bufs))))
            self.conv_state += [(thru[i], sems[2 * i], sems[2 * i + 1]) for i in range(n_c)]
            for g, grp in enumerate(groups):
                self.state[grp] = (thru[n_c + g], sems[2 * (n_c + g)], sems[2 * (n_c + g) + 1])

        convs = [_place_shard(weights["conv_w"][:, None], l, j_arr, F32, f"place_conv_w_l{l}") for l in range(DEPTH)]
        start_copies("gather_start_first", convs, self.groups[:1], [place(self.groups[0])])
        start_copies("gather_start_rest", [], self.groups[1:], [place(grp) for grp in self.groups[1:]])

    def conv_w(self, l):
        if l not in self.conv_ready:
            buf, send, recv = self.conv_state[l]

            def wait(thru, sems, _, __):
                x, y, c, chips = _place()
                for k, chip in enumerate(chips):
                    mine, theirs = thru[0].at[2 * x + y], thru[0].at[2 * chip[0] + chip[1]]
                    _remote(mine, mine, sems[0].at[k], sems[1].at[k], (*chip, c)).wait_send()
                    _remote(theirs, theirs, sems[0].at[k], sems[1].at[k], (x, y, c)).wait_recv()

            (buf,), _, _ = _split_call(wait, f"gather_conv_w_l{l}", [buf], sems_in=[send, recv])
            self.conv_ready[l] = jnp.transpose(buf[:, 0], (1, 0, 2)).reshape(3, 2 * D_FF)
        return self.conv_ready[l]

    def weight(self, l, name):
        grp = (l, name)
        if grp not in self.ready:
            buf, send, recv = self.state[grp]

            def forward(thru, sems, _, new):
                x, y, c, chips = _place()
                for k, chip in enumerate(chips):
                    landed = thru[0].at[2 * chip[0] + chip[1], c]
                    _remote(landed, landed, new[0].at[k], sems[0].at[k], (x, y, c)).wait_recv()
                    _remote(landed, landed, new[0].at[k], new[1].at[k], (x, y, 1 - c)).start()

            (buf,), _, (fsend, frecv) = _split_call(forward, f"gather_pass_{name}_l{l}", [buf], sems_in=[recv],
                                                    new_sems=[(3,), (3,)])

            def finish(thru, sems, _, __):
                x, y, c, chips = _place()
                mine = thru[0].at[2 * x + y, c]
                for k, chip in enumerate(chips):
                    j_k = 2 * chip[0] + chip[1]
                    theirs, landed = thru[0].at[j_k, 1 - c], thru[0].at[j_k, c]
                    _remote(theirs, theirs, sems[1].at[k], sems[2].at[k], (x, y, c)).wait_recv()
                    _remote(landed, landed, sems[1].at[k], sems[2].at[k], (x, y, 1 - c)).wait_send()
                    _remote(mine, mine, sems[0].at[k], sems[2].at[k], (*chip, c)).wait_send()

            (buf,), _, _ = _split_call(finish, f"gather_done_{name}_l{l}", [buf], sems_in=[send, fsend, frecv])
            r, cols = self.shard_shape[name]
            self.ready[grp] = buf.reshape(N_CHIPS, r, cols) if name in ("w_in", "w_up") else buf.reshape(N_CHIPS * r, cols)
        return self.ready[grp]

    def pair_send(self, l, name, other):
        held = self.held
        self.held = None

        def start(thru, _, fresh, sems):
            x, y, c, chips = _place()
            copies = [_remote(thru[0], fresh[0], sems[0], sems[1], (x, y, 1 - c))]
            if held is not None:
                copies += [_remote(thru[1].at[2 * chip[0] + chip[1]], fresh[1].at[k], sems[2].at[k], sems[3].at[k], (*chip, c))
                           for k, chip in enumerate(chips)]
            for cp in copies:
                cp.start()

        thru, fresh, new_sems = [other], [(other.shape, BF16)], [(), ()]
        if held is not None:
            thru, fresh, new_sems = thru + [held[2]], fresh + [((3,) + held[2].shape[1:], BF16)], new_sems + [(3,), (3,)]
        thru, fresh, sems = _split_call(start, f"pair_start_{name}_l{l}", thru, fresh=fresh, new_sems=new_sems, after_last=False)
        self.pairs[(l, name)] = (thru[0], fresh[0], sems[:2])
        if held is not None:
            self.pending.append(dict(l=held[0], name=held[1], stage=2, at=self.tick, bufs=(thru[1], fresh[1]), sems=sems[2:]))

    def pair_recv(self, l, name):
        other, recv, sems = self.pairs.pop((l, name))

        def wait(thru, sems, _, __):
            x, y, c, _chips = _place()
            cp = _remote(thru[0], thru[1], sems[0], sems[1], (x, y, 1 - c))
            cp.wait_send()
            cp.wait_recv()

        (_, recv), _, _ = _split_call(wait, f"pair_done_{name}_l{l}", [other, recv], sems_in=list(sems))
        return recv

    def scatter(self, l, name, p4):
        assert self.held is None
        self.held = (l, name, p4)

    def _scatter_held(self):
        l, name, p4 = self.held
        self.held = None

        def start(thru, _, fresh, sems):
            x, y, c, chips = _place()
            for k, chip in enumerate(chips):
                _remote(thru[0].at[2 * chip[0] + chip[1]], fresh[0].at[k], sems[0].at[k], sems[1].at[k], (*chip, c)).start()

        (p4,), (recv3,), sems = _split_call(start, f"chips_start_{name}_l{l}", [p4], fresh=[((3,) + p4.shape[1:], BF16)],
                                           new_sems=[(3,), (3,)], after_last=False)
        self.pending.append(dict(l=l, name=name, stage=2, at=self.tick, bufs=(p4, recv3), sems=sems))

    def point(self, drain=False):
        self.tick += 1
        joined = [grp for grp in self.pending if grp["stage"] == 3 and (drain or grp["at"] < self.tick)]
        landed = [grp for grp in self.pending if grp["stage"] == 2 and (drain or grp["at"] + 2 <= self.tick)]
        if not joined and not landed:
            return
        n_j, n_l = len(joined), len(landed)

        def wait(thru, sems, _, __):
            x, y, c, chips = _place()
            for i in range(n_j):
                buf, send, recv = thru[i], sems[2 * i], sems[2 * i + 1]
                _remote(buf.at[c], buf.at[c], send, recv, (x, y, 1 - c)).wait_send()
                _remote(buf.at[1 - c], buf.at[1 - c], send, recv, (x, y, c)).wait_recv()
            for i in range(n_l):
                p4, recv3 = thru[n_j + 2 * i], thru[n_j + 2 * i + 1]
                send, recv = sems[2 * (n_j + i)], sems[2 * (n_j + i) + 1]
                for k, chip in enumerate(chips):
                    cp = _remote(p4.at[2 * chip[0] + chip[1]], recv3.at[k], send.at[k], recv.at[k], (*chip, c))
                    cp.wait_send()
                    cp.wait_recv()

        tag = "_".join(f"{grp['name']}{grp['l']}" for grp in joined + landed)
        bufs, _, _ = _split_call(wait, f"landed_{tag}", [b for grp in joined + landed for b in grp["bufs"]],
                                 sems_in=[sm for grp in joined + landed for sm in grp["sems"]])
        for i, grp in enumerate(joined):
            l, name, full = grp["l"], grp["name"], bufs[i]
            if GRAD_HALVES[name][0] != "cols_of_block":
                full = full.reshape((1,) + tuple(self.shard_shape[name]))
            self.adam[name] = _adamw_layer(self.w[name], full, self.m[name], self.v[name], l, self.adam.get(name),
                                           f"adamw_{name}_l{l}")
            grp.update(stage=4)
        if not landed:
            return
        halves = [_chip_sum(bufs[n_j + 2 * i], bufs[n_j + 2 * i + 1], self.j_arr, self.c_arr,
                            f"chip_sum_{grp['name']}_l{grp['l']}") for i, grp in enumerate(landed)]

        def start(thru, _, __, sems):
            x, y, c, _chips = _place()
            for i in range(n_l):
                _remote(thru[i].at[c], thru[i].at[c], sems[2 * i], sems[2 * i + 1], (x, y, 1 - c)).start()

        tag = "_".join(f"{grp['name']}{grp['l']}" for grp in landed)
        halves, _, sems = _split_call(start, f"join_start_{tag}", halves, new_sems=[()] * (2 * n_l), after_last=False)
        for i, grp in enumerate(landed):
            grp.update(stage=3, at=self.tick, bufs=(halves[i],), sems=tuple(sems[2 * i:2 * i + 2]))

    def finish(self):
        if self.held is not None:
            self._scatter_held()
        while any(grp["stage"] < 4 for grp in self.pending):
            self.point(drain=True)
        return self.adam

    @staticmethod
    def _peer(k, x, y, c):
        return (1 - x if k & 4 else x, 1 - y if k & 2 else y, 1 - c if k & 1 else c)

    def small_grads(self, l, grads, loss_tile):
        parts = [grads[nm] for nm in SMALL_NAMES] + ([loss_tile[0, 0:1]] if loss_tile is not None else [])
        packed = _pack_call(parts, f"small_pack_l{l}")
        rows = packed.shape[0]

        def start(thru, _, fresh, sems):
            x, y, c, _chips = _place()
            for k in range(1, 8):
                _remote(thru[0], fresh[0].at[4 * x + 2 * y + c], sems[0].at[k - 1], sems[1].at[k - 1],
                        self._peer(k, x, y, c)).start()

        (packed,), (landed,), sems = _split_call(start, f"small_start_l{l}", [packed], fresh=[((8, rows, PACK_LANES), F32)],
                                                 new_sems=[(7,), (7,)], after_last=False)
        self.small[l] =(packed, landed, sems, [p.shape for p in parts])

    def small_sum(self, l):
        packed, landed, sems, _shapes = self.small[l]

        def wait(thru, sems, _, __):
            x, y, c, _chips = _place()
            for k in range(1, 8):
                px, py, pc = self._peer(k, x, y, c)
                _remote(thru[0], thru[1].at[4 * x + 2 * y + c], sems[0].at[k - 1], sems[1].at[k - 1], (px, py, pc)).wait_send()
                _remote(thru[0], thru[1].at[4 * px + 2 * py + pc], sems[0].at[k - 1], sems[1].at[k - 1], (x, y, c)).wait_recv()

        (packed, landed), _, _ = _split_call(wait, f"small_done_l{l}", [packed, landed], sems_in=list(sems))
        return _sum_devices(packed, landed, self.me_arr, f"small_sum_l{l}")


def _rope_tables(s):
    inv_freq = ROPE_THETA ** (-jnp.arange(0, HEAD_DIM, 2, dtype=F32) / HEAD_DIM)
    ang = jnp.arange(s, dtype=F32)[:, None] * inv_freq[None, :]
    cos, sin = jnp.cos(ang), jnp.sin(ang)
    return jnp.concatenate([cos, cos], axis=-1), jnp.concatenate([-sin, sin], axis=-1)


def _local_step(x, target, ex, small):
    s = x.shape[0]
    cosf, sinf = _rope_tables(s)
    saved = []
    for l in range(DEPTH):
        p = small[l]
        t = f"l{l}"
        h = _rms_fwd(x, p["norm1_g"], f"norm1_{t}")
        z = _matmul(h, ex.weight(l, "w_in"), mode="nn", out_dtype=BF16, tm=1024, tn=896, tk=2048, b_parts=4, name=f"proj_in_{t}")
        qn, kn, vb, ug, vn, *gate_kept = _proj_post(z, p["q_norm_g"], p["k_norm_g"], p["sgu_ln_g"], p["sgu_ln_b"], cosf, sinf,
                                                    f"proj_post_{t}")
        attn, sgu, mixed, probs, psink = _mixer_fwd(qn, kn, vb, ug, vn, p["w_s_bf16"], p["b_s_tile"], p["sink"],
                                                    p["attn_out_g"], p["sgu_out_g"], f"mixer_{t}")
        x1 = _matmul(mixed, ex.weight(l, "w_o"), mode="nn", out_dtype=F32, tm=2048, tn=256, tk=2048, res=x,
                     name=f"proj_out_{t}")
        h2 = _rms_fwd(x1, p["norm2_g"], f"norm2_{t}")
        a_pre = _matmul(h2, ex.weight(l, "w_up"), mode="nn", out_dtype=BF16, tm=1024, tn=1408, tk=2048, b_parts=4,
                        out_parts=2,
                        name=f"ffn_up_{t}")
        act, dgu = _conv_gate_fwd(a_pre, ex.conv_w(l), p["conv_b"], f"conv_gate_{t}")
        x2 = _matmul(act, ex.weight(l, "w_down"), mode="nn", out_dtype=F32, tm=1024, tn=256, tk=D_FF, res=x1,
                     name=f"ffn_down_{t}")
        saved.append(dict(x=x, h=h, z=z, qn=qn, kn=kn, vb=vb, ug=ug, vn=vn, attn=attn, sgu=sgu, mixed=mixed, x1=x1, h2=h2,
                          a_pre=a_pre, act=act, dgu=dgu, probs=probs, psink=psink, gate_kept=gate_kept))
        x = x2
    loss_tile, dx, dxb = _loss_head(x, target, "loss_head")
    for l in reversed(range(DEPTH)):
        p, sv = small[l], saved[l]
        t = f"l{l}"
        def weight_grad(name, a, g, between, g_parts=0):
            ex.pair_send(l, name, _grad_half(name, a, g, ex.o_arr, None, f"g_{name}_other_{t}", g_parts))
            out = between()
            ex.scatter(l, name, _grad_half(name, a, g, ex.c_arr, ex.pair_recv(l, name), f"g_{name}_own_{t}", g_parts))
            ex.point()
            return out

        def after_down():
            dact = _matmul(dxb, ex.weight(l, "w_down"), mode="nt", out_dtype=BF16, tm=1024, tn=512, tk=2048,
                           name=f"d_act_{t}")
            return _conv_gate_bwd(sv["a_pre"], sv["dgu"], ex.conv_w(l), dact, f"conv_gate_bwd_{t}")

        dap, dcw, dcb = weight_grad("w_down", sv["act"], dxb, after_down)

        def after_up():
            dh2 = _matmul(dap, ex.weight(l, "w_up"), mode="nt", out_dtype=F32, tm=1024, tn=1024, tk=2816, a_parts=2,
                          b_parts=4, name=f"d_h2_{t}")
            return _rms_bwd(sv["x1"], p["norm2_g"], dh2, dx, f"norm2_bwd_{t}")

        dx1, dx1b, dg2 = weight_grad("w_up", sv["h2"], dap, after_up, g_parts=2)
        ex.pair_send(l, "w_o", _grad_half("w_o", sv["mixed"], dx1b, ex.o_arr, None, f"g_w_o_other_{t}"))
        dmixed = _matmul(dx1b, ex.weight(l, "w_o"), mode="nt", out_dtype=F32, tm=1024, tn=512, tk=2048,
                         name=f"d_mixed_{t}")
        dqn, dkn, dvb, dug, dvn, dws, dbs, dsk, dga, dgs = _mixer_bwd(
            sv["qn"], sv["kn"], sv["vb"], sv["ug"], sv["vn"], sv["attn"], sv["sgu"], dmixed, p["w_s_bf16"], p["b_s_tile"],
            p["attn_out_g"], p["sgu_out_g"], sv["probs"], sv["psink"], f"mixer_bwd_{t}")
        dz, dqg, dkg, dlg, dlb = _proj_post_bwd(sv["z"], dqn, dkn, dvb, dug, dvn, *sv["gate_kept"], p["q_norm_g"], p["k_norm_g"],
                                                 p["sgu_ln_g"], cosf, sinf, f"proj_post_bwd_{t}")
        ex.scatter(l, "w_o", _grad_half("w_o", sv["mixed"], dx1b, ex.c_arr, ex.pair_recv(l, "w_o"), f"g_w_o_own_{t}"))
        ex.point()

        def after_in():
            dh = _matmul_nt_slabs(dz, ex.weight(l, "w_in"), tm=1024, tn=512, name=f"d_h_{t}")
            return _rms_bwd(sv["x"], p["norm1_g"], dh, dx1, f"norm1_bwd_{t}")

        dx, dxb, dg1 = weight_grad("w_in", sv["h"], dz, after_in)
        ex.small_grads(l, dict(
            norm1_g=dg1[0], q_norm_g=dqg[0], k_norm_g=dkg[0], sink=dsk[:, 0], sgu_ln_g=dlg[0], sgu_ln_b=dlb[0], w_s=dws,
            b_s=dbs[:, :, 0], attn_out_g=dga[0], sgu_out_g=dgs[0], norm2_g=dg2[0],
            conv_w=jnp.concatenate([dcw[0], dcw[1]], axis=-1), conv_b=jnp.concatenate([dcb[0, 0], dcb[1, 0]], axis=-1)),
            loss_tile if l == 0 else None)
    return dx


def _small_views(l, norm1_g, q_norm_g, k_norm_g, sink, sgu_ln_g, sgu_ln_b, w_s, b_s, attn_out_g, sgu_out_g, norm2_g, conv_b):
    return dict(
        norm1_g=norm1_g[l][None], q_norm_g=q_norm_g[l][None], k_norm_g=k_norm_g[l][None], sink=sink[l],
        sgu_ln_g=sgu_ln_g[l][None], sgu_ln_b=sgu_ln_b[l][None], w_s_bf16=w_s[l].astype(BF16),
        b_s_tile=jnp.broadcast_to(b_s[l][:, :, None], (N_GMLP_HEADS, BLOCK, BLOCK)), attn_out_g=attn_out_g[l][None],
        sgu_out_g=sgu_out_g[l][None], norm2_g=norm2_g[l][None], conv_b=conv_b[l][None])


SMALL_NAMES = ("norm1_g", "q_norm_g", "k_norm_g", "sink", "sgu_ln_g", "sgu_ln_b", "w_s", "b_s", "attn_out_g", "sgu_out_g",
               "norm2_g", "conv_b", "conv_w")
REPLICATED_NAMES = SMALL_NAMES[:-1]
BIG_NAMES = ("w_in", "w_o", "w_up", "w_down")
PACK_LANES = 128
PACK_ALIGN = 8 * PACK_LANES


def _pack_rows(shape):
    return -(-math.prod(shape) // PACK_ALIGN) * 8


def _pack_parts(arrays):
    parts = []
    for a in arrays:
        flat = a.reshape(-1)
        parts.append(jnp.pad(flat, (0, _pack_rows(a.shape) * PACK_LANES - flat.shape[0])).reshape(-1, PACK_LANES))
    return parts


def _pack_call(arrays, name):
    parts = _pack_parts(arrays)
    total = sum(p.shape[0] for p in parts)

    def body(*refs):
        o_ref, at = refs[-1], 0
        for p_ref in refs[:-1]:
            o_ref[at:at + p_ref.shape[0], :] = p_ref[...]
            at += p_ref.shape[0]

    vm = pl.BlockSpec(memory_space=pltpu.VMEM)
    return _ordered_call(
        body, name=name, out_shape=jax.ShapeDtypeStruct((total, PACK_LANES), F32), in_specs=[vm] * len(parts), out_specs=vm,
        compiler_params=pltpu.CompilerParams(vmem_limit_bytes=V7X_VMEM_LIMIT),
    )(*parts)


def _unpack_layers(stacked, shapes):
    nl = stacked.shape[0]
    out, at = [], 0
    for shp in shapes:
        rows = _pack_rows(shp)
        out.append(stacked[:, at:at + rows].reshape(nl, -1)[:, :math.prod(shp)].reshape((nl,) + tuple(shp)))
        at += rows
    return out


def _adamw_packed(w, g, m, v, rows, layer, into, name):
    head = pl.BlockSpec((rows, PACK_LANES), lambda i: (0, 0))
    at_layer = pl.BlockSpec((None, rows, PACK_LANES), lambda i: (layer, 0, 0))

    def body(w_ref, g_ref, m_ref, v_ref, *rest):
        d_ref, nm_ref, nv_ref = rest[-3:]
        gv = g_ref[...]
        mn = ADAM_B1 * m_ref[...] + (1.0 - ADAM_B1) * gv
        vn = ADAM_B2 * v_ref[...] + (1.0 - ADAM_B2) * (gv * gv)
        m_hat = mn / (1.0 - ADAM_B1 ** ADAM_STEP)
        v_hat = vn / (1.0 - ADAM_B2 ** ADAM_STEP)
        d_ref[...] = -ADAM_LR * (m_hat / (jnp.sqrt(v_hat) + ADAM_EPS) + ADAM_WD * w_ref[...])
        nm_ref[...] = mn
        nv_ref[...] = vn

    in_specs = [head] * 4
    operands = [w, g, m, v]
    aliases = {}
    if into is not None:
        in_specs += [ANY] * 3
        operands += list(into)
        aliases = {4 + i: i for i in range(3)}
    sds = jax.ShapeDtypeStruct((DEPTH, rows, PACK_LANES), F32)
    return _ordered_call(
        body, name=name, out_shape=(sds,) * 3, grid=(1,), in_specs=in_specs, out_specs=(at_layer,) * 3,
        input_output_aliases=aliases, compiler_params=_params(("arbitrary",)),
    )(*operands)


def kernel(x, norm1_g, w_in, q_norm_g, k_norm_g, sink, sgu_ln_g, sgu_ln_b, w_s, b_s, attn_out_g, sgu_out_g, w_o, norm2_g, w_up, conv_w, conv_b, w_down, loss_target, m_norm1_g, m_w_in, m_q_norm_g, m_k_norm_g, m_sink, m_sgu_ln_g, m_sgu_ln_b, m_w_s, m_b_s, m_attn_out_g, m_sgu_out_g, m_w_o, m_norm2_g, m_w_up, m_conv_w, m_conv_b, m_w_down, v_norm1_g, v_w_in, v_q_norm_g, v_k_norm_g, v_sink, v_sgu_ln_g, v_sgu_ln_b, v_w_s, v_b_s, v_attn_out_g, v_sgu_out_g, v_w_o, v_norm2_g, v_w_up, v_conv_w, v_conv_b, v_w_down):
    weights = dict(norm1_g=norm1_g, w_in=w_in, q_norm_g=q_norm_g, k_norm_g=k_norm_g, sink=sink, sgu_ln_g=sgu_ln_g,
                   sgu_ln_b=sgu_ln_b, w_s=w_s, b_s=b_s, attn_out_g=attn_out_g, sgu_out_g=sgu_out_g, w_o=w_o, norm2_g=norm2_g,
                   w_up=w_up, conv_w=conv_w, conv_b=conv_b, w_down=w_down)
    m_in = dict(norm1_g=m_norm1_g, w_in=m_w_in, q_norm_g=m_q_norm_g, k_norm_g=m_k_norm_g, sink=m_sink, sgu_ln_g=m_sgu_ln_g,
                sgu_ln_b=m_sgu_ln_b, w_s=m_w_s, b_s=m_b_s, attn_out_g=m_attn_out_g, sgu_out_g=m_sgu_out_g, w_o=m_w_o,
                norm2_g=m_norm2_g, w_up=m_w_up, conv_w=m_conv_w, conv_b=m_conv_b, w_down=m_w_down)
    v_in = dict(norm1_g=v_norm1_g, w_in=v_w_in, q_norm_g=v_q_norm_g, k_norm_g=v_k_norm_g, sink=v_sink, sgu_ln_g=v_sgu_ln_g,
                sgu_ln_b=v_sgu_ln_b, w_s=v_w_s, b_s=v_b_s, attn_out_g=v_attn_out_g, sgu_out_g=v_sgu_out_g, w_o=v_w_o,
                norm2_g=v_norm2_g, w_up=v_w_up, conv_w=v_conv_w, conv_b=v_conv_b, w_down=v_w_down)
    cx, cy, cc = lax.axis_index("x"), lax.axis_index("y"), lax.axis_index("c")
    j_me = 2 * cx + cy
    c_arr = jnp.reshape(cc, (1,)).astype(jnp.int32)
    j_arr = jnp.reshape(j_me, (1,)).astype(jnp.int32)

    _Order.last = None
    ex = _Exchange(weights, m_in, v_in, j_arr, c_arr, jnp.reshape(4 * cx + 2 * cy + cc, (1,)).astype(jnp.int32))
    small = [_small_views(l, norm1_g, q_norm_g, k_norm_g, sink, sgu_ln_g, sgu_ln_b, w_s, b_s, attn_out_g, sgu_out_g, norm2_g,
                          conv_b) for l in range(DEPTH)]
    packed_in = [[_pack_call([src[nm][l] for nm in REPLICATED_NAMES], f"pack_{tag}_l{l}")
                  for tag, src in (("w", weights), ("m", m_in), ("v", v_in))] for l in range(DEPTH)]
    dx = _local_step(x[0], loss_target[0], ex, small)
    big_out = ex.finish()

    rep_shapes = [weights[nm].shape[1:] for nm in REPLICATED_NAMES]
    rep_rows = sum(_pack_rows(shp) for shp in rep_shapes)
    cw_shape = (3, 2 * D_FF)
    sums, adam_small = [None] * DEPTH, None
    for l in reversed(range(DEPTH)):
        sums[l] = ex.small_sum(l)
        pw, pm, pv = packed_in[l]
        adam_small = _adamw_packed(pw, sums[l], pm, pv, rep_rows, l, adam_small, f"adamw_small_l{l}")
    cw_rows = _pack_rows(cw_shape)
    loss = sums[0][rep_rows + cw_rows, 0]
    stacked = jnp.stack([sm[:rep_rows + cw_rows] for sm in sums])
    grads = dict(zip(REPLICATED_NAMES, _unpack_layers(stacked[:, :rep_rows], rep_shapes)))
    delta, new_m, new_v = (dict(zip(REPLICATED_NAMES, _unpack_layers(arr, rep_shapes))) for arr in adam_small)
    cw_cols = 2 * D_FF // N_CHIPS
    cw_grad = lax.dynamic_slice_in_dim(_unpack_layers(stacked[:, rep_rows:], [cw_shape])[0], j_me * cw_cols, cw_cols, axis=2)
    flat = lambda a: a.reshape(DEPTH * 3, cw_cols)
    cw_out = _adamw(flat(conv_w), flat(cw_grad), flat(m_conv_w), flat(v_conv_w), "adamw_conv_w")
    grads["conv_w"], delta["conv_w"], new_m["conv_w"], new_v["conv_w"] = (a.reshape(DEPTH, 3, cw_cols) for a in cw_out)

    for name in BIG_NAMES:
        grads[name], delta[name], new_m[name], new_v[name] = big_out[name]

    order = ("norm1_g", "w_in", "q_norm_g", "k_norm_g", "sink", "sgu_ln_g", "sgu_ln_b", "w_s", "b_s", "attn_out_g", "sgu_out_g",
             "w_o", "norm2_g", "w_up", "conv_w", "conv_b", "w_down")
    return (loss, dx[None], *[grads[nm] for nm in order], *[delta[nm] for nm in order], *[new_m[nm] for nm in order],
            *[new_v[nm] for nm in order])
```

```python
import math

import jax
import jax.numpy as jnp
from jax import lax
from jax.experimental import pallas as pl
from jax.experimental.pallas import tpu as pltpu

F32 = jnp.float32
BF16 = jnp.bfloat16

D_MODEL = 2048
HEAD_DIM = 128
ATTN_WIDTH = 1024
N_Q_HEADS = 8
N_KV_HEADS = 2
GQA_GROUP = 4
KV_WIDTH = 256
GMLP_WIDTH = 1024
N_GMLP_HEADS = 8
BLOCK = 128
IN_WIDTH = 3584
D_FF = 5632
DEPTH = 2
EPS = 1e-6
MASK_VALUE = -1e30
ROPE_THETA = 10000.0
N_CHIPS = 4

ADAM_LR = 0.001
ADAM_B1 = 0.9
ADAM_B2 = 0.999
ADAM_EPS = 1e-08
ADAM_WD = 0.01
ADAM_STEP = 10

V7X_VMEM_LIMIT = 48 * 1024 * 1024
MESH = pl.DeviceIdType.MESH

_GELU_C = math.sqrt(2.0 / math.pi)
_GELU_A = 0.044715


def _params(sem=None):
    return pltpu.CompilerParams(dimension_semantics=sem, vmem_limit_bytes=V7X_VMEM_LIMIT)


ANY = pl.BlockSpec(memory_space=pl.ANY)


class _Order:
    last = None


def _ordered_call(body, *, token_index=0, **kw):
    def run(*operands):
        tok = _Order.last
        if tok is None or any(op is tok for op in operands):
            call = pl.pallas_call(body, **kw)
        else:
            n_in = len(operands)

            def ordered_body(*refs):
                return body(*refs[:n_in], *refs[n_in + 1:])

            kw2 = dict(kw)
            if "grid_spec" in kw2:
                gs = kw2["grid_spec"]
                kw2["grid_spec"] = pltpu.PrefetchScalarGridSpec(
                    num_scalar_prefetch=gs.num_scalar_prefetch, grid=gs.grid, in_specs=list(gs.in_specs) + [ANY],
                    out_specs=gs.out_specs, scratch_shapes=gs.scratch_shapes)
            else:
                kw2["in_specs"] = list(kw2["in_specs"]) + [ANY]
            call = pl.pallas_call(ordered_body, **kw2)
            operands = operands + (tok,)
        out = call(*operands)
        _Order.last = out[token_index] if isinstance(out, (tuple, list)) else out
        return out

    return run


def _gelu(x):
    return x * (0.5 * (1.0 + jnp.tanh(_GELU_C * (x + _GELU_A * (x * x * x)))))


def _gelu_grad(x):
    x2 = x * x
    t = jnp.tanh(_GELU_C * (x + _GELU_A * (x * x2)))
    return 0.5 * (1.0 + t) + 0.5 * x * (1.0 - t * t) * (_GELU_C * (1.0 + 3.0 * _GELU_A * x2))


def _mean_last(x):
    return jnp.mean(x, axis=-1, keepdims=True)


def _sum_rows(x):
    return jnp.sum(x, axis=0, keepdims=True)


def _sum_all(x):
    return jnp.sum(jnp.sum(x, axis=1, keepdims=True), axis=0, keepdims=True)


def _matmul(a, b, *, mode, out_dtype, tm, tn, tk, name, res=None, a_parts=0, b_parts=0, out_parts=0):
    assert mode in ("nn", "nt"), mode
    if mode == "nn":
        assert not a_parts
        m, k = a.shape
        n = b.shape[0] * b.shape[2] if b_parts else b.shape[1]
    else:
        m, k = (a.shape[1], a.shape[0] * a.shape[2]) if a_parts else a.shape
        n = b.shape[1] if b_parts else b.shape[0]
    tm, tn, tk = min(tm, m), min(tn, n), min(tk, k)
    assert m % tm == 0 and n % tn == 0 and k % tk == 0, (name, m, n, k, tm, tn, tk)
    nm, nn, nk = m // tm, n // tn, k // tk

    def slab(idx, total_tiles, parts):
        per = total_tiles // parts
        assert per * parts == total_tiles, (name, total_tiles, parts)
        return idx // per, idx % per

    if mode == "nn":
        a_spec = pl.BlockSpec((tm, tk), lambda i, j, kk: (i, kk))
        if b_parts:
            b_spec = pl.BlockSpec((None, tk, tn), lambda i, j, kk: (slab(j, nn, b_parts)[0], kk, slab(j, nn, b_parts)[1]))
        else:
            b_spec = pl.BlockSpec((tk, tn), lambda i, j, kk: (kk, j))
        dims = (((1,), (0,)), ((), ()))
    else:
        if a_parts:
            a_spec = pl.BlockSpec((None, tm, tk), lambda i, j, kk: (slab(kk, nk, a_parts)[0], i, slab(kk, nk, a_parts)[1]))
        else:
            a_spec = pl.BlockSpec((tm, tk), lambda i, j, kk: (i, kk))
        if b_parts:
            b_spec = pl.BlockSpec((None, tn, tk), lambda i, j, kk: (slab(kk, nk, b_parts)[0], j, slab(kk, nk, b_parts)[1]))
        else:
            b_spec = pl.BlockSpec((tn, tk), lambda i, j, kk: (j, kk))
        dims = (((1,), (1,)), ((), ()))
    if out_parts:
        out_shape = jax.ShapeDtypeStruct((out_parts, m, n // out_parts), out_dtype)
        out_spec = pl.BlockSpec((None, tm, tn), lambda i, j, kk: (slab(j, nn, out_parts)[0], i, slab(j, nn, out_parts)[1]))
    else:
        out_shape = jax.ShapeDtypeStruct((m, n), out_dtype)
        out_spec = pl.BlockSpec((tm, tn), lambda i, j, kk: (i, j))
    in_specs = [a_spec, b_spec]
    operands = [a, b]
    if res is not None:
        in_specs.append(pl.BlockSpec((tm, tn), lambda i, j, kk: (i, j)))
        operands.append(res)

    def body(*refs):
        a_ref, b_ref = refs[0], refs[1]
        res_ref = refs[2] if res is not None else None
        o_ref = refs[3] if res is not None else refs[2]
        p = lax.dot_general(a_ref[...], b_ref[...], dims, preferred_element_type=F32)

        def finish(total):
            if res_ref is not None:
                total = res_ref[...] + total
            o_ref[...] = total.astype(out_dtype)

        if nk == 1:
            finish(p)
        else:
            acc_ref = refs[-1]
            kk = pl.program_id(2)

            @pl.when(kk == 0)
            def _():
                acc_ref[...] = p

            @pl.when(jnp.logical_and(kk > 0, kk < nk - 1))
            def _():
                acc_ref[...] += p

            @pl.when(kk == nk - 1)
            def _():
                finish(acc_ref[...] + p)

    scratch = [pltpu.VMEM((tm, tn), F32)] if nk > 1 else []
    return _ordered_call(
        body, name=name, out_shape=out_shape, grid=(nm, nn, nk), in_specs=in_specs, out_specs=out_spec,
        scratch_shapes=scratch, compiler_params=_params(("parallel", "parallel", "arbitrary")),
    )(*operands)


def _matmul_nt_slabs(a, b, *, tm, tn, name, a_parts=0):
    nslab, n, ks = b.shape
    m = a.shape[1] if a_parts else a.shape[0]
    tm, tn = min(tm, m), min(tn, n)
    assert m % tm == 0 and n % tn == 0, (name, m, n, tm, tn)
    if a_parts:
        per = nslab // a_parts
        assert per * a_parts == nslab and a.shape[2] == per * ks, (name, a.shape, b.shape)
        a_spec = pl.BlockSpec((a_parts, tm, per * ks), lambda i, j: (0, i, 0))
    else:
        assert a.shape[1] == nslab * ks, (name, a.shape, b.shape)
        a_spec = pl.BlockSpec((tm, nslab * ks), lambda i, j: (i, 0))

    def body(a_ref, b_ref, o_ref):
        total = None
        for sl in range(nslab):
            if a_parts:
                a_sl = a_ref[sl // per, :, (sl % per) * ks:(sl % per + 1) * ks]
            else:
                a_sl = a_ref[:, sl * ks:(sl + 1) * ks]
            p = lax.dot_general(a_sl, b_ref[sl], (((1,), (1,)), ((), ())), preferred_element_type=F32)
            total = p if total is None else total + p
        o_ref[...] = total

    return _ordered_call(
        body, name=name, out_shape=jax.ShapeDtypeStruct((m, n), F32), grid=(m // tm, n // tn),
        in_specs=[a_spec, pl.BlockSpec((nslab, tn, ks), lambda i, j: (0, j, 0))],
        out_specs=pl.BlockSpec((tm, tn), lambda i, j: (i, j)), compiler_params=_params(("parallel", "parallel")),
    )(a, b)


GRAD_HALVES = {
    "w_in": ("rows_of_slab", 1024, 896), "w_up": ("rows_of_slab", 1024, 1408), "w_o": ("rows_of_block", 256, 2048),
    "w_down": ("cols_of_block", 1408, 512)}


def _half_shape(name, shard_shape):
    r, cols = shard_shape
    return (r, cols // 2) if GRAD_HALVES[name][0] == "cols_of_block" else (r // 2, cols)


def _grad_half(name, a, g, sel, res, call_name, g_parts=0):
    kind, tm, tn = GRAD_HALVES[name]
    s, m = a.shape
    n = g.shape[0] * g.shape[2] if g_parts else g.shape[1]
    if kind == "rows_of_slab":
        rh, hc = m // 2, n // N_CHIPS
        per = hc // tn
        grid = (rh // tm, n // tn)
        a_map = lambda i, j, sel_ref: (0, sel_ref[0] * (rh // tm) + i)
        g_col = lambda i, j, sel_ref: j
        o_map = lambda i, j, sel_ref: (j // per, i, j % per)
    elif kind == "rows_of_block":
        rh, hc = m // N_CHIPS // 2, n
        assert tm == rh
        grid = (N_CHIPS, n // tn)
        a_map = lambda i, j, sel_ref: (0, 2 * i + sel_ref[0])
        g_col = lambda i, j, sel_ref: j
        o_map = lambda i, j, sel_ref: (i, 0, j)
    else:
        rh, hc = m // N_CHIPS, n // 2
        assert tm == rh
        grid = (N_CHIPS, hc // tn)
        a_map = lambda i, j, sel_ref: (0, i)
        g_col = lambda i, j, sel_ref: sel_ref[0] * (hc // tn) + j
        o_map = lambda i, j, sel_ref: (i, 0, j)
    if g_parts:
        g_per = (n // tn) // g_parts
        g_spec = pl.BlockSpec((None, s, tn), lambda i, j, sel_ref: (g_col(i, j, sel_ref) // g_per, 0, g_col(i, j, sel_ref) % g_per))
    else:
        g_spec = pl.BlockSpec((s, tn), lambda i, j, sel_ref: (0, g_col(i, j, sel_ref)))
    o_spec = pl.BlockSpec((None, tm, tn), o_map)
    in_specs = [pl.BlockSpec((s, tm), a_map), g_spec] + ([o_spec] if res is not None else [])

    def body(sel_ref, a_ref, g_ref, *rest):
        o_ref = rest[-1]
        p = lax.dot_general(a_ref[...], g_ref[...], (((0,), (0,)), ((), ())), preferred_element_type=F32)
        if res is not None:
            p = p + rest[0][...].astype(F32)
        o_ref[...] = p.astype(BF16)

    grid_spec = pltpu.PrefetchScalarGridSpec(num_scalar_prefetch=1, grid=grid, in_specs=in_specs, out_specs=o_spec)
    return _ordered_call(
        body, name=call_name, out_shape=jax.ShapeDtypeStruct((N_CHIPS, rh, hc), BF16), grid_spec=grid_spec,
        compiler_params=_params(("parallel", "parallel")),
    )(sel, a, g, *([res] if res is not None else []))


def _row_tile(s):
    return min(256, s)


def _rows(width, tr):
    return pl.BlockSpec((tr, width), lambda i: (i, 0))


def _const2(shape):
    return pl.BlockSpec(shape, lambda i: (0, 0))


def _rms_fwd(x, g, name):
    s, d = x.shape
    tr = _row_tile(s)

    def body(x_ref, g_ref, o_ref):
        xv = x_ref[...]
        r = lax.rsqrt(_mean_last(xv * xv) + EPS)
        o_ref[...] = (xv * r * g_ref[...]).astype(BF16)

    return _ordered_call(
        body, name=name, out_shape=jax.ShapeDtypeStruct((s, d), BF16), grid=(s // tr,),
        in_specs=[_rows(d, tr), _const2((1, d))], out_specs=_rows(d, tr), compiler_params=_params(("parallel",)),
    )(x, g)


def _rms_bwd(x, g, dh, dres, name):
    s, d = x.shape
    tr = _row_tile(s)

    def body(x_ref, g_ref, dh_ref, dres_ref, dx_ref, dxb_ref, dg_ref):
        xv, dy = x_ref[...], dh_ref[...]
        r = lax.rsqrt(_mean_last(xv * xv) + EPS)
        gdy = dy * g_ref[...]
        dx = dres_ref[...] + r * gdy - xv * ((r * r * r) * _mean_last(xv * gdy))
        dx_ref[...] = dx
        dxb_ref[...] = dx.astype(BF16)

        @pl.when(pl.program_id(0) == 0)
        def _():
            dg_ref[...] = jnp.zeros_like(dg_ref)

        dg_ref[...] += _sum_rows(xv * r * dy)

    return _ordered_call(
        body, name=name,
        out_shape=(jax.ShapeDtypeStruct((s, d), F32), jax.ShapeDtypeStruct((s, d), BF16), jax.ShapeDtypeStruct((1, d), F32)),
        grid=(s // tr,), in_specs=[_rows(d, tr), _const2((1, d)), _rows(d, tr), _rows(d, tr)],
        out_specs=(_rows(d, tr), _rows(d, tr), _const2((1, d))), compiler_params=_params(("arbitrary",)),
    )(x, g, dh, dres)


Q0, K0, V0, GU0, GV0 = 0, ATTN_WIDTH, ATTN_WIDTH + KV_WIDTH, ATTN_WIDTH + 2 * KV_WIDTH, ATTN_WIDTH + 2 * KV_WIDTH + GMLP_WIDTH


def _head(h, base=0):
    return slice(base + h * HEAD_DIM, base + (h + 1) * HEAD_DIM)


def _proj_post(z, qg, kg, lg, lb, cosf, sinf, name):
    s = z.shape[0]
    tr = _row_tile(s)

    def body(z_ref, qg_ref, kg_ref, lg_ref, lb_ref, cos_ref, sin_ref, qn_ref, kn_ref, vb_ref, ug_ref, vn_ref,
             dgu_ref, dgv_ref, xhat_ref, rstd_ref):
        cos, sin = cos_ref[...], sin_ref[...]

        def norm_rope(xh, g):
            y = xh * lax.rsqrt(_mean_last(xh * xh) + EPS) * g
            return y * cos + pltpu.roll(y, HEAD_DIM // 2, 1) * sin

        for h in range(N_Q_HEADS):
            qn_ref[:, _head(h)] = norm_rope(z_ref[:, _head(h, Q0)].astype(F32), qg_ref[...]).astype(BF16)
        for h in range(N_KV_HEADS):
            kn_ref[:, _head(h)] = norm_rope(z_ref[:, _head(h, K0)].astype(F32), kg_ref[...]).astype(BF16)
        vb_ref[...] = z_ref[:, V0:GU0]
        gu = z_ref[:, GU0:GV0].astype(F32)
        ug_ref[...] = _gelu(gu)
        dgu_ref[...] = _gelu_grad(gu).astype(BF16)
        gv = z_ref[:, GV0:IN_WIDTH].astype(F32)
        vg = _gelu(gv)
        dgv_ref[...] = _gelu_grad(gv).astype(BF16)
        xc = vg - _mean_last(vg)
        r = lax.rsqrt(_mean_last(xc * xc) + EPS)
        y = xc * r
        xhat_ref[...] = y.astype(BF16)
        rstd_ref[...] = r
        vn_ref[...] = (y * lg_ref[...] + lb_ref[...]).astype(BF16)

    wide = jax.ShapeDtypeStruct((s, GMLP_WIDTH), BF16)
    return _ordered_call(
        body, name=name,
        out_shape=(jax.ShapeDtypeStruct((s, ATTN_WIDTH), BF16), jax.ShapeDtypeStruct((s, KV_WIDTH), BF16),
                   jax.ShapeDtypeStruct((s, KV_WIDTH), BF16), jax.ShapeDtypeStruct((s, GMLP_WIDTH), F32), wide,
                   wide, wide, wide, jax.ShapeDtypeStruct((s, 1), F32)),
        grid=(s // tr,),
        in_specs=[_rows(IN_WIDTH, tr), _const2((1, HEAD_DIM)), _const2((1, HEAD_DIM)), _const2((1, GMLP_WIDTH)),
                  _const2((1, GMLP_WIDTH)), _rows(HEAD_DIM, tr), _rows(HEAD_DIM, tr)],
        out_specs=(_rows(ATTN_WIDTH, tr), _rows(KV_WIDTH, tr), _rows(KV_WIDTH, tr), _rows(GMLP_WIDTH, tr), _rows(GMLP_WIDTH, tr),
                   _rows(GMLP_WIDTH, tr), _rows(GMLP_WIDTH, tr), _rows(GMLP_WIDTH, tr), _rows(1, tr)),
        compiler_params=_params(("parallel",)),
    )(z, qg, kg, lg, lb, cosf, sinf)


def _proj_post_bwd(z, dqn, dkn, dvb, dug, dvn, gelu_grad_u, gelu_grad_v, xhat_v, rstd_v, qg, kg, lg, cosf, sinf, name):
    s = z.shape[0]
    tr = _row_tile(s)

    def body(z_ref, dqn_ref, dkn_ref, dvb_ref, dug_ref, dvn_ref, ggu_ref, ggv_ref, xhat_ref, rstd_ref, qg_ref, kg_ref, lg_ref,
             cos_ref, sin_ref, dz_ref, dqg_ref, dkg_ref, dlg_ref, dlb_ref):
        cos, sin = cos_ref[...], sin_ref[...]

        @pl.when(pl.program_id(0) == 0)
        def _():
            dqg_ref[...] = jnp.zeros_like(dqg_ref)
            dkg_ref[...] = jnp.zeros_like(dkg_ref)
            dlg_ref[...] = jnp.zeros_like(dlg_ref)
            dlb_ref[...] = jnp.zeros_like(dlb_ref)

        def norm_rope_bwd(xh, g, dout):
            dy = dout * cos - pltpu.roll(dout, HEAD_DIM // 2, 1) * sin
            r = lax.rsqrt(_mean_last(xh * xh) + EPS)
            xhat = xh * r
            gdy = dy * g
            return r * (gdy - xhat * _mean_last(xhat * gdy)), _sum_rows(xhat * dy)

        dqg = jnp.zeros((1, HEAD_DIM), F32)
        for h in range(N_Q_HEADS):
            dx, dg = norm_rope_bwd(z_ref[:, _head(h, Q0)].astype(F32), qg_ref[...], dqn_ref[:, _head(h)])
            dz_ref[:, _head(h, Q0)] = dx.astype(BF16)
            dqg = dqg + dg
        dqg_ref[...] += dqg
        dkg = jnp.zeros((1, HEAD_DIM), F32)
        for h in range(N_KV_HEADS):
            dx, dg = norm_rope_bwd(z_ref[:, _head(h, K0)].astype(F32), kg_ref[...], dkn_ref[:, _head(h)])
            dz_ref[:, _head(h, K0)] = dx.astype(BF16)
            dkg = dkg + dg
        dkg_ref[...] += dkg
        dz_ref[:, V0:GU0] = dvb_ref[...].astype(BF16)
        dz_ref[:, GU0:GV0] = (dug_ref[...] * ggu_ref[...].astype(F32)).astype(BF16)
        xhat = xhat_ref[...].astype(F32)
        dvn_v = dvn_ref[...]
        dlg_ref[...] += _sum_rows(xhat * dvn_v)
        dlb_ref[...] += _sum_rows(dvn_v)
        dxh = dvn_v * lg_ref[...]
        dvg = rstd_ref[...] * (dxh - _mean_last(dxh) - xhat * _mean_last(dxh * xhat))
        dz_ref[:, GV0:IN_WIDTH] = (dvg * ggv_ref[...].astype(F32)).astype(BF16)

    return _ordered_call(
        body, name=name,
        out_shape=(jax.ShapeDtypeStruct((s, IN_WIDTH), BF16), jax.ShapeDtypeStruct((1, HEAD_DIM), F32),
                   jax.ShapeDtypeStruct((1, HEAD_DIM), F32), jax.ShapeDtypeStruct((1, GMLP_WIDTH), F32),
                   jax.ShapeDtypeStruct((1, GMLP_WIDTH), F32)),
        grid=(s // tr,),
        in_specs=[_rows(V0, tr), _rows(ATTN_WIDTH, tr), _rows(KV_WIDTH, tr), _rows(KV_WIDTH, tr), _rows(GMLP_WIDTH, tr),
                  _rows(GMLP_WIDTH, tr), _rows(GMLP_WIDTH, tr), _rows(GMLP_WIDTH, tr), _rows(GMLP_WIDTH, tr), _rows(1, tr),
                  _const2((1, HEAD_DIM)), _const2((1, HEAD_DIM)), _const2((1, GMLP_WIDTH)), _rows(HEAD_DIM, tr),
                  _rows(HEAD_DIM, tr)],
        out_specs=(_rows(IN_WIDTH, tr), _const2((1, HEAD_DIM)), _const2((1, HEAD_DIM)), _const2((1, GMLP_WIDTH)),
                   _const2((1, GMLP_WIDTH))),
        compiler_params=_params(("arbitrary",)),
    )(z, dqn, dkn, dvb, dug, dvn, gelu_grad_u, gelu_grad_v, xhat_v, rstd_v, qg, kg, lg, cosf, sinf)


def _band_valid(n, s):
    shape = (GQA_GROUP * BLOCK, 3 * BLOCK)
    i = lax.broadcasted_iota(jnp.int32, shape, 0) & (BLOCK - 1)
    j = lax.broadcasted_iota(jnp.int32, shape, 1)
    k_pos = n * BLOCK - BLOCK + j
    return (jnp.abs(j - BLOCK - i) <= BLOCK) & (k_pos >= 0) & (k_pos < s)


def _group_rows(x, kh):
    return jnp.concatenate([x[:, _head(kh * GQA_GROUP + g)] for g in range(GQA_GROUP)], axis=0)


def _group_sinks(sink_ref, kh):
    return jnp.concatenate([jnp.full((BLOCK, 1), sink_ref[kh * GQA_GROUP + g], F32) for g in range(GQA_GROUP)], axis=0)


def _rows_of(x, g):
    return x[g * BLOCK:(g + 1) * BLOCK]


def _probs(q, kb, sink_h, valid):
    sc = lax.dot_general(q, kb, (((1,), (1,)), ((), ())), preferred_element_type=F32) * (HEAD_DIM ** -0.5)
    sc = jnp.where(valid, sc, MASK_VALUE)
    m = jnp.maximum(jnp.max(sc, axis=-1, keepdims=True), sink_h)
    p = jnp.exp(sc - m)
    es = jnp.exp(sink_h - m)
    den = jnp.sum(p, axis=-1, keepdims=True) + es
    inv = 1.0 / den
    return p * inv, es * inv


def _band_specs(width, nb):
    return [pl.BlockSpec((BLOCK, width), lambda n: (jnp.maximum(n - 1, 0), 0)),
            pl.BlockSpec((BLOCK, width), lambda n: (n, 0)),
            pl.BlockSpec((BLOCK, width), lambda n: (jnp.minimum(n + 1, nb - 1), 0))]


def _blk(width):
    return pl.BlockSpec((BLOCK, width), lambda n: (n, 0))


def _whole3(shape):
    return pl.BlockSpec(shape, lambda n: (0, 0, 0))


def _smem():
    return pl.BlockSpec(memory_space=pltpu.SMEM)


def _mixer_fwd(qn, kn, vb, ug, vn, wsb, bsb, sink, ga, gs, name):
    s = qn.shape[0]
    nb = s // BLOCK

    def body(sink_ref, q_ref, kp_ref, kc_ref, kx_ref, vp_ref, vc_ref, vx_ref, ug_ref, vn_ref, ws_ref, bs_ref, ga_ref, gs_ref,
             attn_ref, sgu_ref, mix_ref, probs_ref, psink_ref):
        n = pl.program_id(0)
        valid = _band_valid(n, s)
        ssq = jnp.zeros((BLOCK, 1), F32)
        for kh in range(N_KV_HEADS):
            kb = jnp.concatenate([kp_ref[:, _head(kh)], kc_ref[:, _head(kh)], kx_ref[:, _head(kh)]], axis=0)
            vbd = jnp.concatenate([vp_ref[:, _head(kh)], vc_ref[:, _head(kh)], vx_ref[:, _head(kh)]], axis=0)
            p, p_sink = _probs(_group_rows(q_ref, kh), kb, _group_sinks(sink_ref, kh), valid)
            pb = p.astype(BF16)
            probs_ref[kh] = pb
            psink_ref[kh] = p_sink
            o4 = jnp.dot(pb, vbd, preferred_element_type=F32)
            for g in range(GQA_GROUP):
                o = _rows_of(o4, g)
                attn_ref[:, _head(kh * GQA_GROUP + g)] = o
                ssq = ssq + jnp.sum(o * o, axis=-1, keepdims=True)
        r = lax.rsqrt(ssq * (1.0 / ATTN_WIDTH) + EPS)
        mix_ref[:, 0:ATTN_WIDTH] = (attn_ref[...] * r * ga_ref[...]).astype(BF16)
        ssq = jnp.zeros((BLOCK, 1), F32)
        for h in range(N_GMLP_HEADS):
            f = jnp.dot(ws_ref[h], vn_ref[:, _head(h)], preferred_element_type=F32) + bs_ref[h]
            o = ug_ref[:, _head(h)] * f
            sgu_ref[:, _head(h)] = o
            ssq = ssq + jnp.sum(o * o, axis=-1, keepdims=True)
        r = lax.rsqrt(ssq * (1.0 / GMLP_WIDTH) + EPS)
        mix_ref[:, ATTN_WIDTH:D_MODEL] = (sgu_ref[...] * r * gs_ref[...]).astype(BF16)

    hh = (N_GMLP_HEADS, BLOCK, BLOCK)
    return _ordered_call(
        body, name=name,
        out_shape=(jax.ShapeDtypeStruct((s, ATTN_WIDTH), F32), jax.ShapeDtypeStruct((s, GMLP_WIDTH), F32),
                   jax.ShapeDtypeStruct((s, D_MODEL), BF16), jax.ShapeDtypeStruct((nb,) + PROBS_BLOCK, BF16),
                   jax.ShapeDtypeStruct((nb,) + PSINK_BLOCK, F32)),
        grid=(nb,),
        in_specs=[_smem(), _blk(ATTN_WIDTH)] + _band_specs(KV_WIDTH, nb) + _band_specs(KV_WIDTH, nb)
        + [_blk(GMLP_WIDTH), _blk(GMLP_WIDTH), _whole3(hh), _whole3(hh),
           pl.BlockSpec((1, ATTN_WIDTH), lambda n: (0, 0)), pl.BlockSpec((1, GMLP_WIDTH), lambda n: (0, 0))],
        out_specs=(_blk(ATTN_WIDTH), _blk(GMLP_WIDTH), _blk(D_MODEL), _per_block(PROBS_BLOCK), _per_block(PSINK_BLOCK)),
        compiler_params=_params(("parallel",)),
    )(sink, qn, kn, kn, kn, vb, vb, vb, ug, vn, wsb, bsb, ga, gs)


PROBS_BLOCK = (N_KV_HEADS, GQA_GROUP * BLOCK, 3 * BLOCK)
PSINK_BLOCK = (N_KV_HEADS, GQA_GROUP * BLOCK, 1)


def _per_block(shape):
    return pl.BlockSpec((None,) + shape, lambda n: (n, 0, 0, 0))


def _mixer_bwd(qn, kn, vb, ug, vn, attn, sgu, dmixed, wsb, bsb, ga, gs, probs, psink, name):
    s = qn.shape[0]
    nb = s // BLOCK
    tn_dims = (((0,), (0,)), ((), ()))
    nt_dims = (((1,), (1,)), ((), ()))

    def body(q_ref, kp_ref, kc_ref, kx_ref, vp_ref, vc_ref, vx_ref, ug_ref, vn_ref, attn_ref, sgu_ref, dm_ref,
             ws_ref, bs_ref, ga_ref, gs_ref, probs_ref, psink_ref,
             dq_ref, dk_ref, dv_ref, dug_ref, dvn_ref, dws_ref, dbs_ref, dsk_ref, dga_ref, dgs_ref, dk_acc, dv_acc):
        n = pl.program_id(0)

        @pl.when(n == 0)
        def _():
            for ref in (dk_acc, dv_acc, dws_ref, dbs_ref, dsk_ref, dga_ref, dgs_ref):
                ref[...] = jnp.zeros_like(ref)

        def out_norm_bwd(o, g, dy):
            r = lax.rsqrt(_mean_last(o * o) + EPS)
            gdy = dy * g
            return r * gdy - o * ((r * r * r) * _mean_last(o * gdy)), _sum_rows(o * r * dy)

        d_attn, dga = out_norm_bwd(attn_ref[...], ga_ref[...], dm_ref[:, 0:ATTN_WIDTH])
        dga_ref[...] += dga
        d_sgu, dgs = out_norm_bwd(sgu_ref[...], gs_ref[...], dm_ref[:, ATTN_WIDTH:D_MODEL])
        dgs_ref[...] += dgs

        for h in range(N_GMLP_HEADS):
            vn_h = vn_ref[:, _head(h)]
            f = jnp.dot(ws_ref[h], vn_h, preferred_element_type=F32) + bs_ref[h]
            ds_h = d_sgu[:, _head(h)]
            dug_ref[:, _head(h)] = ds_h * f
            df = ds_h * ug_ref[:, _head(h)]
            dfb = df.astype(BF16)
            dvn_ref[:, _head(h)] = lax.dot_general(ws_ref[h], dfb, tn_dims, preferred_element_type=F32)
            dws_ref[h] += lax.dot_general(dfb, vn_h, nt_dims, preferred_element_type=F32)
            dbs_ref[h] += jnp.broadcast_to(jnp.sum(df, axis=-1, keepdims=True), (BLOCK, BLOCK))

        row0 = pl.multiple_of(n * BLOCK, BLOCK)
        for kh in range(N_KV_HEADS):
            kb = jnp.concatenate([kp_ref[:, _head(kh)], kc_ref[:, _head(kh)], kx_ref[:, _head(kh)]], axis=0)
            vbd = jnp.concatenate([vp_ref[:, _head(kh)], vc_ref[:, _head(kh)], vx_ref[:, _head(kh)]], axis=0)
            q4 = _group_rows(q_ref, kh)
            pb = probs_ref[kh]
            p = pb.astype(F32)
            do4 = _group_rows(d_attn, kh).astype(BF16)
            dp = lax.dot_general(do4, vbd, nt_dims, preferred_element_type=F32)
            delta = jnp.sum(p * dp, axis=-1, keepdims=True)
            dsc = (p * (dp - delta) * (HEAD_DIM ** -0.5)).astype(BF16)
            d_sink = -(psink_ref[kh] * delta)
            dq4 = jnp.dot(dsc, kb, preferred_element_type=F32)
            for g in range(GQA_GROUP):
                h = kh * GQA_GROUP + g
                dsk_ref[h:h + 1, :] += jnp.broadcast_to(_sum_all(_rows_of(d_sink, g)), (1, BLOCK))
                dq_ref[:, _head(h)] = _rows_of(dq4, g)
            dk_acc[pl.ds(row0, 3 * BLOCK), _head(kh)] += lax.dot_general(dsc, q4, tn_dims, preferred_element_type=F32)
            dv_acc[pl.ds(row0, 3 * BLOCK), _head(kh)] += lax.dot_general(pb, do4, tn_dims, preferred_element_type=F32)

        @pl.when(n == nb - 1)
        def _():
            dk_ref[...] = dk_acc[BLOCK:BLOCK + s, :]
            dv_ref[...] = dv_acc[BLOCK:BLOCK + s, :]

    hh = (N_GMLP_HEADS, BLOCK, BLOCK)
    full_kv = pl.BlockSpec((s, KV_WIDTH), lambda n: (0, 0))
    return _ordered_call(
        body, name=name,
        out_shape=(jax.ShapeDtypeStruct((s, ATTN_WIDTH), F32), jax.ShapeDtypeStruct((s, KV_WIDTH), F32),
                   jax.ShapeDtypeStruct((s, KV_WIDTH), F32), jax.ShapeDtypeStruct((s, GMLP_WIDTH), F32),
                   jax.ShapeDtypeStruct((s, GMLP_WIDTH), F32), jax.ShapeDtypeStruct(hh, F32), jax.ShapeDtypeStruct(hh, F32),
                   jax.ShapeDtypeStruct((N_Q_HEADS, BLOCK), F32), jax.ShapeDtypeStruct((1, ATTN_WIDTH), F32),
                   jax.ShapeDtypeStruct((1, GMLP_WIDTH), F32)),
        grid=(nb,),
        in_specs=[_blk(ATTN_WIDTH)] + _band_specs(KV_WIDTH, nb) + _band_specs(KV_WIDTH, nb)
        + [_blk(GMLP_WIDTH), _blk(GMLP_WIDTH), _blk(ATTN_WIDTH), _blk(GMLP_WIDTH), _blk(D_MODEL), _whole3(hh), _whole3(hh),
           pl.BlockSpec((1, ATTN_WIDTH), lambda n: (0, 0)), pl.BlockSpec((1, GMLP_WIDTH), lambda n: (0, 0)),
           _per_block(PROBS_BLOCK), _per_block(PSINK_BLOCK)],
        out_specs=(_blk(ATTN_WIDTH), full_kv, full_kv, _blk(GMLP_WIDTH), _blk(GMLP_WIDTH), _whole3(hh), _whole3(hh),
                   pl.BlockSpec((N_Q_HEADS, BLOCK), lambda n: (0, 0)), pl.BlockSpec((1, ATTN_WIDTH), lambda n: (0, 0)),
                   pl.BlockSpec((1, GMLP_WIDTH), lambda n: (0, 0))),
        scratch_shapes=[pltpu.VMEM((s + 2 * BLOCK, KV_WIDTH), F32), pltpu.VMEM((s + 2 * BLOCK, KV_WIDTH), F32)],
        compiler_params=_params(("arbitrary",)),
    )(qn, kn, kn, kn, vb, vb, vb, ug, vn, attn, sgu, dmixed, wsb, bsb, ga, gs, probs, psink)


CONV_TILE = 128


PAD_ROWS = 8


def _zero_pad_rows(pad_ref):
    s = pad_ref.shape[0] - 2 * PAD_ROWS
    zeros = jnp.zeros((PAD_ROWS, pad_ref.shape[1]), F32)
    pad_ref[0:PAD_ROWS, :] = zeros
    pad_ref[PAD_ROWS + s:2 * PAD_ROWS + s, :] = zeros


def _shift_rows(a, pad_ref):
    s = a.shape[0]
    pad_ref[PAD_ROWS:PAD_ROWS + s, :] = a
    padded = pad_ref[...]
    prev = pltpu.roll(padded, 1, 0)[PAD_ROWS:PAD_ROWS + s]
    nxt = pltpu.roll(padded, s + 2 * PAD_ROWS - 1, 0)[PAD_ROWS:PAD_ROWS + s]
    return prev, nxt


def _conv_specs(s):
    tc = CONV_TILE
    nj = D_FF // tc
    return (tc, nj, pl.BlockSpec((2, s, tc), lambda j: (0, 0, j)),
            [pl.BlockSpec((3, tc), lambda j: (0, j)), pl.BlockSpec((3, tc), lambda j: (0, j + nj))],
            [pl.BlockSpec((1, tc), lambda j: (0, j)), pl.BlockSpec((1, tc), lambda j: (0, j + nj))])


def _conv_gate_fwd(a_pre, cw, cb, name):
    s = a_pre.shape[1]
    tc, nj, a_spec, w_specs, b_specs = _conv_specs(s)

    def body(a_ref, wg_ref, wu_ref, bg_ref, bu_ref, act_ref, dgu_ref, pad_ref):
        _zero_pad_rows(pad_ref)

        def conv(a, w_ref, b_ref):
            prev, nxt = _shift_rows(a, pad_ref)
            return b_ref[...] + prev * w_ref[0:1, :] + a * w_ref[1:2, :] + nxt * w_ref[2:3, :]

        g = conv(a_ref[0].astype(F32), wg_ref, bg_ref)
        u = conv(a_ref[1].astype(F32), wu_ref, bu_ref)
        sg = 1.0 / (1.0 + jnp.exp(-g))
        silu = g * sg
        act_ref[...] = (silu * u).astype(BF16)
        dgu_ref[0] = (u * (sg * (1.0 + g * (1.0 - sg)))).astype(BF16)
        dgu_ref[1] = silu.astype(BF16)

    return _ordered_call(
        body, name=name, out_shape=(jax.ShapeDtypeStruct((s, D_FF), BF16), jax.ShapeDtypeStruct((2, s, D_FF), BF16)),
        grid=(nj,), in_specs=[a_spec] + w_specs + b_specs,
        out_specs=(pl.BlockSpec((s, tc), lambda j: (0, j)), pl.BlockSpec((2, s, tc), lambda j: (0, 0, j))),
        scratch_shapes=[pltpu.VMEM((s + 2 * PAD_ROWS, tc), F32)], compiler_params=_params(("parallel",)),
    )(a_pre, cw, cw, cb, cb)


def _conv_gate_bwd(a_pre, dgu, cw, dact, name):
    s = a_pre.shape[1]
    tc, nj, a_spec, w_specs, _ = _conv_specs(s)

    def body(a_ref, dgu_ref, wg_ref, wu_ref, dact_ref, dap_ref, dcw_ref, dcb_ref, pad_ref):
        _zero_pad_rows(pad_ref)
        dact_v = dact_ref[...].astype(F32)
        for part, w_ref in enumerate((wg_ref, wu_ref)):
            da = dact_v * dgu_ref[part].astype(F32)
            a = a_ref[part].astype(F32)
            da_prev, da_next = _shift_rows(da, pad_ref)
            dcw_ref[part, 0:1, :] = _sum_rows(a * da_next)
            dcw_ref[part, 1:2, :] = _sum_rows(a * da)
            dcw_ref[part, 2:3, :] = _sum_rows(a * da_prev)
            dcb_ref[part] = _sum_rows(da)
            dap_ref[part] = (da_next * w_ref[0:1, :] + da * w_ref[1:2, :] + da_prev * w_ref[2:3, :]).astype(BF16)

    return _ordered_call(
        body, name=name,
        out_shape=(jax.ShapeDtypeStruct((2, s, D_FF), BF16), jax.ShapeDtypeStruct((2, 3, D_FF), F32),
                   jax.ShapeDtypeStruct((2, 1, D_FF), F32)),
        grid=(nj,),
        in_specs=[a_spec, pl.BlockSpec((2, s, tc), lambda j: (0, 0, j))] + w_specs + [pl.BlockSpec((s, tc), lambda j: (0, j))],
        out_specs=(pl.BlockSpec((2, s, tc), lambda j: (0, 0, j)), pl.BlockSpec((2, 3, tc), lambda j: (0, 0, j)),
                   pl.BlockSpec((2, 1, tc), lambda j: (0, 0, j))),
        scratch_shapes=[pltpu.VMEM((s + 2 * PAD_ROWS, tc), F32)], compiler_params=_params(("parallel",)),
    )(a_pre, dgu, cw, cw, dact)


def _loss_head(y, target, name):
    s, d = y.shape
    tr = _row_tile(s)

    def body(y_ref, t_ref, loss_ref, dy_ref, dyb_ref):
        err = y_ref[...] - t_ref[...]

        @pl.when(pl.program_id(0) == 0)
        def _():
            loss_ref[...] = jnp.zeros_like(loss_ref)

        loss_ref[...] += jnp.broadcast_to(0.5 * _sum_all(_mean_last(err * err)), (8, 128))
        dy = err * (1.0 / d)
        dy_ref[...] = dy
        dyb_ref[...] = dy.astype(BF16)

    return _ordered_call(
        body, name=name,
        out_shape=(jax.ShapeDtypeStruct((8, 128), F32), jax.ShapeDtypeStruct((s, d), F32), jax.ShapeDtypeStruct((s, d), BF16)),
        grid=(s // tr,), in_specs=[_rows(d, tr), _rows(d, tr)],
        out_specs=(_const2((8, 128)), _rows(d, tr), _rows(d, tr)), compiler_params=_params(("arbitrary",)),
    )(y, target)


def _row_block(rows, cols, budget=1 << 20):
    if rows * cols <= budget:
        return rows
    best = None
    for tr in range(16, rows, 16):
        if rows % tr == 0 and tr * cols <= budget:
            best = tr
    assert best is not None, (rows, cols)
    return best


def _place_shard(x4, layer, j_arr, out_dtype, name):
    _, nh, r, cols = x4.shape
    tr = _row_block(r, cols)

    def body(j_ref, x_ref, o_ref):
        o_ref[...] = x_ref[...].astype(out_dtype)

    grid_spec = pltpu.PrefetchScalarGridSpec(
        num_scalar_prefetch=1, grid=(nh, r // tr),
        in_specs=[pl.BlockSpec((None, None, tr, cols), lambda h, i, j_ref: (layer, h, i, 0))],
        out_specs=pl.BlockSpec((None, None, tr, cols), lambda h, i, j_ref: (j_ref[0], h, i, 0)))
    return _ordered_call(
        body, name=name, out_shape=jax.ShapeDtypeStruct((N_CHIPS, nh, r, cols), out_dtype), grid_spec=grid_spec,
        compiler_params=_params(("parallel", "parallel")),
    )(j_arr, x4)


def _adamw(w, g, m, v, name):
    rows, cols = w.shape
    tr = _row_block(rows, cols, 1 << 18)

    def body(w_ref, g_ref, m_ref, v_ref, go_ref, d_ref, nm_ref, nv_ref):
        gv = g_ref[...]
        go_ref[...] = gv
        mn = ADAM_B1 * m_ref[...] + (1.0 - ADAM_B1) * gv
        vn = ADAM_B2 * v_ref[...] + (1.0 - ADAM_B2) * (gv * gv)
        m_hat = mn / (1.0 - ADAM_B1 ** ADAM_STEP)
        v_hat = vn / (1.0 - ADAM_B2 ** ADAM_STEP)
        d_ref[...] = -ADAM_LR * (m_hat / (jnp.sqrt(v_hat) + ADAM_EPS) + ADAM_WD * w_ref[...])
        nm_ref[...] = mn
        nv_ref[...] = vn

    sds = jax.ShapeDtypeStruct((rows, cols), F32)
    return _ordered_call(
        body, name=name, out_shape=(sds, sds, sds, sds), grid=(rows // tr,),
        in_specs=[_rows(cols, tr)] * 4, out_specs=(_rows(cols, tr),) * 4, compiler_params=_params(("parallel",)),
    )(w, g, m, v)


def _chip_sum(p4, recv3, j_arr, c_arr, name):
    _, rh, cols = p4.shape
    tr = _row_block(rh, cols, 1 << 19)

    def body(j_ref, c_ref, p_ref, r_ref, o_ref):
        total = p_ref[...].astype(F32)
        for peer in range(3):
            total = total + r_ref[peer].astype(F32)
        o_ref[...] = total

    grid_spec = pltpu.PrefetchScalarGridSpec(
        num_scalar_prefetch=2, grid=(rh // tr,),
        in_specs=[pl.BlockSpec((None, tr, cols), lambda i, j_ref, c_ref: (j_ref[0], i, 0)),
                  pl.BlockSpec((3, tr, cols), lambda i, j_ref, c_ref: (0, i, 0))],
        out_specs=pl.BlockSpec((None, tr, cols), lambda i, j_ref, c_ref: (c_ref[0], i, 0)))
    return _ordered_call(
        body, name=name, out_shape=jax.ShapeDtypeStruct((2, rh, cols), F32), grid_spec=grid_spec,
        compiler_params=_params(("parallel",)),
    )(j_arr, c_arr, p4, recv3)


def _adamw_layer(w, g, m, v, layer, into, name):
    nl, rows, cols = w.shape
    slabs, _, width = g.shape
    assert slabs * width == cols and g.shape[1] == rows, (name, w.shape, g.shape)
    tr = _row_block(rows, width, 1 << 18)
    at_layer = pl.BlockSpec((None, tr, width), lambda h, i: (layer, i, h))

    def body(w_ref, g_ref, m_ref, v_ref, *rest):
        go_ref, d_ref, nm_ref, nv_ref = rest[-4:]
        gv = g_ref[...]
        go_ref[...] = gv
        mn = ADAM_B1 * m_ref[...] + (1.0 - ADAM_B1) * gv
        vn = ADAM_B2 * v_ref[...] + (1.0 - ADAM_B2) * (gv * gv)
        m_hat = mn / (1.0 - ADAM_B1 ** ADAM_STEP)
        v_hat = vn / (1.0 - ADAM_B2 ** ADAM_STEP)
        d_ref[...] = -ADAM_LR * (m_hat / (jnp.sqrt(v_hat) + ADAM_EPS) + ADAM_WD * w_ref[...])
        nm_ref[...] = mn
        nv_ref[...] = vn

    in_specs = [at_layer, pl.BlockSpec((None, tr, width), lambda h, i: (h, i, 0)), at_layer, at_layer]
    operands = [w, g, m, v]
    aliases = {}
    if into is not None:
        in_specs += [ANY] * 4
        operands += list(into)
        aliases = {4 + i: i for i in range(4)}
    sds = jax.ShapeDtypeStruct((nl, rows, cols), F32)
    return _ordered_call(
        body, name=name, out_shape=(sds,) * 4, grid=(slabs, rows // tr), in_specs=in_specs, out_specs=(at_layer,) * 4,
        input_output_aliases=aliases, compiler_params=_params(("parallel", "parallel")),
    )(*operands)


def _sum_devices(mine, landed, me_arr, name):
    rows, lanes = mine.shape

    def body(me_ref, mine_ref, landed_ref, o_ref):
        total = None
        for dev in range(8):
            part = jnp.where(me_ref[0] == dev, mine_ref[...], landed_ref[dev])
            total = part if total is None else total + part
        o_ref[...] = total

    grid_spec = pltpu.PrefetchScalarGridSpec(
        num_scalar_prefetch=1, grid=(1,),
        in_specs=[pl.BlockSpec((rows, lanes), lambda i, me_ref: (0, 0)), pl.BlockSpec((8, rows, lanes), lambda i, me_ref: (0, 0, 0))],
        out_specs=pl.BlockSpec((rows, lanes), lambda i, me_ref: (0, 0)))
    return _ordered_call(
        body, name=name, out_shape=jax.ShapeDtypeStruct((rows, lanes), F32), grid_spec=grid_spec,
        compiler_params=_params(("arbitrary",)),
    )(me_arr, mine, landed)


def _place():
    x, y, c = lax.axis_index("x"), lax.axis_index("y"), lax.axis_index("c")
    chips = [(1 - x, y), (x, 1 - y), (1 - x, 1 - y)]
    return x, y, c, chips


HBM = pl.BlockSpec(memory_space=pltpu.HBM)
SEM = pl.BlockSpec(memory_space=pltpu.SEMAPHORE)
TOKEN = jax.ShapeDtypeStruct((8, 128), F32)


def _remote(src, dst, send_sem, recv_sem, to):
    return pltpu.make_async_remote_copy(src_ref=src, dst_ref=dst, send_sem=send_sem, recv_sem=recv_sem, device_id=to,
                                        device_id_type=MESH)


def _split_call(body, name, thru, sems_in=(), fresh=(), new_sems=(), after_last=True):
    n_t, n_s, n_f = len(thru), len(sems_in), len(fresh)

    def call_body(*refs):
        outs = refs[n_t + n_s:]
        body(refs[:n_t], refs[n_t:n_t + n_s], outs[1 + n_t:1 + n_t + n_f], outs[1 + n_t + n_f:])
        outs[0][...] = jnp.zeros_like(outs[0])

    out_shape = ([TOKEN] + [pltpu.HBM(t.shape, t.dtype) for t in thru] + [pltpu.HBM(shp, dt) for shp, dt in fresh]
                 + [pltpu.SemaphoreType.DMA(shp) for shp in new_sems])
    out_specs = [pl.BlockSpec(memory_space=pltpu.VMEM)] + [HBM] * (n_t + n_f) + [SEM] * len(new_sems)
    if not after_last:
        _Order.last = None
    out = _ordered_call(
        call_body, name=name, out_shape=tuple(out_shape), in_specs=[HBM] * n_t + [SEM] * n_s, out_specs=tuple(out_specs),
        input_output_aliases={i: 1 + i for i in range(n_t)},
        compiler_params=pltpu.CompilerParams(has_side_effects=pltpu.SideEffectType.DATAFLOW_SIDE_EFFECTING),
    )(*[pltpu.with_memory_space_constraint(t, pltpu.HBM) for t in thru], *sems_in)
    return out[1:1 + n_t], out[1 + n_t:1 + n_t + n_f], out[1 + n_t + n_f:]


class _Exchange:
    def __init__(self, weights, m_in, v_in, j_arr, c_arr, me_arr):
        self.w, self.m, self.v = weights, m_in, v_in
        self.j_arr, self.c_arr, self.me_arr = j_arr, c_arr, me_arr
        self.adam, self.small, self.pairs, self.held = {}, {}, {}, None
        self.o_arr = 1 - c_arr
        self.groups = [(l, name) for l in range(DEPTH) for name in BIG_NAMES]
        self.shard_shape = {name: weights[name].shape[1:] for name in BIG_NAMES}
        self.conv_state, self.state = [], {}
        self.ready, self.conv_ready = {}, {}
        self.pending, self.tick, self.reduced = [], 0, {}

        def place(grp):
            l, name = grp
            nl, r, cols = weights[name].shape
            return _place_shard(weights[name].reshape(nl, 2, r // 2, cols), l, j_arr, BF16, f"place_{name}_l{l}")

        def start_copies(tag, convs, groups, bufs):
            n_c = len(convs)

            def start(thru, _, __, sems):
                x, y, c, chips = _place()
                j_me = 2 * x + y
                copies = []
                for i in range(len(thru)):
                    mine = thru[i].at[j_me] if i < n_c else thru[i].at[j_me, c]
                    copies += [_remote(mine, mine, sems[2 * i].at[k], sems[2 * i + 1].at[k], (*chip, c))
                               for k, chip in enumerate(chips)]
                for cp in copies:
                    cp.start()

            thru, _, sems = _split_call(start, tag, convs + bufs, new_sems=[(3,)] * (2 * (n_c + len(bufs))))
            self.conv_state += [(thru[i], sems[2 * i], sems[2 * i + 1]) for i in range(n_c)]
            for g, grp in enumerate(groups):
                self.state[grp] = (thru[n_c + g], sems[2 * (n_c + g)], sems[2 * (n_c + g) + 1])

        convs = [_place_shard(weights["conv_w"][:, None], l, j_arr, F32, f"place_conv_w_l{l}") for l in range(DEPTH)]
        start_copies("gather_start_first", convs, self.groups[:1], [place(self.groups[0])])
        start_copies("gather_start_rest", [], self.groups[1:], [place(grp) for grp in self.groups[1:]])

    def conv_w(self, l):
        if l not in self.conv_ready:
            buf, send, recv = self.conv_state[l]

            def wait(thru, sems, _, __):
                x, y, c, chips = _place()
                for k, chip in enumerate(chips):
                    mine, theirs = thru[0].at[2 * x + y], thru[0].at[2 * chip[0] + chip[1]]
                    _remote(mine, mine, sems[0].at[k], sems[1].at[k], (*chip, c)).wait_send()
                    _remote(theirs, theirs, sems[0].at[k], sems[1].at[k], (x, y, c)).wait_recv()

            (buf,), _, _ = _split_call(wait, f"gather_conv_w_l{l}", [buf], sems_in=[send, recv])
            self.conv_ready[l] = jnp.transpose(buf[:, 0], (1, 0, 2)).reshape(3, 2 * D_FF)
        return self.conv_ready[l]

    def weight(self, l, name):
        grp = (l, name)
        if grp not in self.ready:
            buf, send, recv = self.state[grp]

            def forward(thru, sems, _, new):
                x, y, c, chips = _place()
                for k, chip in enumerate(chips):
                    landed = thru[0].at[2 * chip[0] + chip[1], c]
                    _remote(landed, landed, new[0].at[k], sems[0].at[k], (x, y, c)).wait_recv()
                    _remote(landed, landed, new[0].at[k], new[1].at[k], (x, y, 1 - c)).start()

            (buf,), _, (fsend, frecv) = _split_call(forward, f"gather_pass_{name}_l{l}", [buf], sems_in=[recv],
                                                    new_sems=[(3,), (3,)])

            def finish(thru, sems, _, __):
                x, y, c, chips = _place()
                mine = thru[0].at[2 * x + y, c]
                for k, chip in enumerate(chips):
                    j_k = 2 * chip[0] + chip[1]
                    theirs, landed = thru[0].at[j_k, 1 - c], thru[0].at[j_k, c]
                    _remote(theirs, theirs, sems[1].at[k], sems[2].at[k], (x, y, c)).wait_recv()
                    _remote(landed, landed, sems[1].at[k], sems[2].at[k], (x, y, 1 - c)).wait_send()
                    _remote(mine, mine, sems[0].at[k], sems[2].at[k], (*chip, c)).wait_send()

            (buf,), _, _ = _split_call(finish, f"gather_done_{name}_l{l}", [buf], sems_in=[send, fsend, frecv])
            r, cols = self.shard_shape[name]
            self.ready[grp] = buf.reshape(N_CHIPS, r, cols) if name in ("w_in", "w_up") else buf.reshape(N_CHIPS * r, cols)
        return self.ready[grp]

    def pair_send(self, l, name, other):
        held = self.held
        self.held = None

        def start(thru, _, fresh, sems):
            x, y, c, chips = _place()
            copies = [_remote(thru[0], fresh[0], sems[0], sems[1], (x, y, 1 - c))]
            if held is not None:
                copies += [_remote(thru[1].at[2 * chip[0] + chip[1]], fresh[1].at[k], sems[2].at[k], sems[3].at[k], (*chip, c))
                           for k, chip in enumerate(chips)]
            for cp in copies:
                cp.start()

        thru, fresh, new_sems = [other], [(other.shape, BF16)], [(), ()]
        if held is not None:
            thru, fresh, new_sems = thru + [held[2]], fresh + [((3,) + held[2].shape[1:], BF16)], new_sems + [(3,), (3,)]
        thru, fresh, sems = _split_call(start, f"pair_start_{name}_l{l}", thru, fresh=fresh, new_sems=new_sems, after_last=False)
        self.pairs[(l, name)] = (thru[0], fresh[0], sems[:2])
        if held is not None:
            self.pending.append(dict(l=held[0], name=held[1], stage=2, at=self.tick, bufs=(thru[1], fresh[1]), sems=sems[2:]))

    def pair_recv(self, l, name):
        other, recv, sems = self.pairs.pop((l, name))

        def wait(thru, sems, _, __):
            x, y, c, _chips = _place()
            cp = _remote(thru[0], thru[1], sems[0], sems[1], (x, y, 1 - c))
            cp.wait_send()
            cp.wait_recv()

        (_, recv), _, _ = _split_call(wait, f"pair_done_{name}_l{l}", [other, recv], sems_in=list(sems))
        return recv

    def scatter(self, l, name, p4):
        assert self.held is None
        self.held = (l, name, p4)

    def _scatter_held(self):
        l, name, p4 = self.held
        self.held = None

        def start(thru, _, fresh, sems):
            x, y, c, chips = _place()
            for k, chip in enumerate(chips):
                _remote(thru[0].at[2 * chip[0] + chip[1]], fresh[0].at[k], sems[0].at[k], sems[1].at[k], (*chip, c)).start()

        (p4,), (recv3,), sems = _split_call(start, f"chips_start_{name}_l{l}", [p4], fresh=[((3,) + p4.shape[1:], BF16)],
                                           new_sems=[(3,), (3,)], after_last=False)
        self.pending.append(dict(l=l, name=name, stage=2, at=self.tick, bufs=(p4, recv3), sems=sems))

    def point(self, drain=False):
        self.tick += 1
        if drain:
            for grp in [g for g in self.pending if g["stage"] in (2, 3)]:
                self._advance([grp] if grp["stage"] == 3 else [], [grp] if grp["stage"] == 2 else [])
        else:
            self._advance([grp for grp in self.pending if grp["stage"] == 3 and grp["at"] < self.tick],
                          [grp for grp in self.pending if grp["stage"] == 2 and grp["at"] + 2 <= self.tick])

    def _advance(self, joined, landed):
        if not joined and not landed:
            return
        n_j, n_l = len(joined), len(landed)

        def wait(thru, sems, _, __):
            x, y, c, chips = _place()
            for i in range(n_j):
                buf, send, recv = thru[i], sems[2 * i], sems[2 * i + 1]
                _remote(buf.at[c], buf.at[c], send, recv, (x, y, 1 - c)).wait_send()
                _remote(buf.at[1 - c], buf.at[1 - c], send, recv, (x, y, c)).wait_recv()
            for i in range(n_l):
                p4, recv3 = thru[n_j + 2 * i], thru[n_j + 2 * i + 1]
                send, recv = sems[2 * (n_j + i)], sems[2 * (n_j + i) + 1]
                for k, chip in enumerate(chips):
                    cp = _remote(p4.at[2 * chip[0] + chip[1]], recv3.at[k], send.at[k], recv.at[k], (*chip, c))
                    cp.wait_send()
                    cp.wait_recv()

        tag = "_".join([f"{grp['name']}{grp['l']}_halves" for grp in joined] + [f"{grp['name']}{grp['l']}_chips" for grp in landed])
        bufs, _, _ = _split_call(wait, f"landed_{tag}", [b for grp in joined + landed for b in grp["bufs"]],
                                 sems_in=[sm for grp in joined + landed for sm in grp["sems"]])
        for i, grp in enumerate(joined):
            l, name, full = grp["l"], grp["name"], bufs[i]
            if GRAD_HALVES[name][0] != "cols_of_block":
                full = full.reshape((1,) + tuple(self.shard_shape[name]))
            self.adam[name] = _adamw_layer(self.w[name], full, self.m[name], self.v[name], l, self.adam.get(name),
                                           f"adamw_{name}_l{l}")
            grp.update(stage=4)
        if not landed:
            return
        halves = [_chip_sum(bufs[n_j + 2 * i], bufs[n_j + 2 * i + 1], self.j_arr, self.c_arr,
                            f"chip_sum_{grp['name']}_l{grp['l']}") for i, grp in enumerate(landed)]

        def start(thru, _, __, sems):
            x, y, c, _chips = _place()
            for i in range(n_l):
                _remote(thru[i].at[c], thru[i].at[c], sems[2 * i], sems[2 * i + 1], (x, y, 1 - c)).start()

        tag = "_".join(f"{grp['name']}{grp['l']}" for grp in landed)
        halves, _, sems = _split_call(start, f"join_start_{tag}", halves, new_sems=[()] * (2 * n_l), after_last=False)
        for i, grp in enumerate(landed):
            grp.update(stage=3, at=self.tick, bufs=(halves[i],), sems=tuple(sems[2 * i:2 * i + 2]))

    def finish(self):
        if self.held is not None:
            self._scatter_held()
        while any(grp["stage"] < 4 for grp in self.pending):
            self.point(drain=True)
        return self.adam

    @staticmethod
    def _peer(k, x, y, c):
        return (1 - x if k & 4 else x, 1 - y if k & 2 else y, 1 - c if k & 1 else c)

    def small_grads(self, l, grads, loss_tile):
        parts = [grads[nm] for nm in SMALL_NAMES] + ([loss_tile[0, 0:1]] if loss_tile is not None else [])
        packed = _pack_call(parts, f"small_pack_l{l}")
        rows = packed.shape[0]

        def start(thru, _, fresh, sems):
            x, y, c, _chips = _place()
            for k in range(1, 8):
                _remote(thru[0], fresh[0].at[4 * x + 2 * y + c], sems[0].at[k - 1], sems[1].at[k - 1],
                        self._peer(k, x, y, c)).start()

        (packed,), (landed,), sems = _split_call(start, f"small_start_l{l}", [packed], fresh=[((8, rows, PACK_LANES), F32)],
                                                 new_sems=[(7,), (7,)], after_last=False)
        self.small[l] =(packed, landed, sems, [p.shape for p in parts])

    def small_sum(self, l):
        packed, landed, sems, _shapes = self.small[l]

        def wait(thru, sems, _, __):
            x, y, c, _chips = _place()
            for k in range(1, 8):
                px, py, pc = self._peer(k, x, y, c)
                _remote(thru[0], thru[1].at[4 * x + 2 * y + c], sems[0].at[k - 1], sems[1].at[k - 1], (px, py, pc)).wait_send()
                _remote(thru[0], thru[1].at[4 * px + 2 * py + pc], sems[0].at[k - 1], sems[1].at[k - 1], (x, y, c)).wait_recv()

        (packed, landed), _, _ = _split_call(wait, f"small_done_l{l}", [packed, landed], sems_in=list(sems))
        return _sum_devices(packed, landed, self.me_arr, f"small_sum_l{l}")


def _rope_tables(s):
    inv_freq = ROPE_THETA ** (-jnp.arange(0, HEAD_DIM, 2, dtype=F32) / HEAD_DIM)
    ang = jnp.arange(s, dtype=F32)[:, None] * inv_freq[None, :]
    cos, sin = jnp.cos(ang), jnp.sin(ang)
    return jnp.concatenate([cos, cos], axis=-1), jnp.concatenate([-sin, sin], axis=-1)


def _local_step(x, target, ex, small):
    s = x.shape[0]
    cosf, sinf = _rope_tables(s)
    saved = []
    for l in range(DEPTH):
        p = small[l]
        t = f"l{l}"
        h = _rms_fwd(x, p["norm1_g"], f"norm1_{t}")
        z = _matmul(h, ex.weight(l, "w_in"), mode="nn", out_dtype=BF16, tm=1024, tn=896, tk=2048, b_parts=4, name=f"proj_in_{t}")
        qn, kn, vb, ug, vn, *gate_kept = _proj_post(z, p["q_norm_g"], p["k_norm_g"], p["sgu_ln_g"], p["sgu_ln_b"], cosf, sinf,
                                                    f"proj_post_{t}")
        attn, sgu, mixed, probs, psink = _mixer_fwd(qn, kn, vb, ug, vn, p["w_s_bf16"], p["b_s_tile"], p["sink"],
                                                    p["attn_out_g"], p["sgu_out_g"], f"mixer_{t}")
        x1 = _matmul(mixed, ex.weight(l, "w_o"), mode="nn", out_dtype=F32, tm=2048, tn=256, tk=2048, res=x,
                     name=f"proj_out_{t}")
        h2 = _rms_fwd(x1, p["norm2_g"], f"norm2_{t}")
        a_pre = _matmul(h2, ex.weight(l, "w_up"), mode="nn", out_dtype=BF16, tm=1024, tn=1408, tk=2048, b_parts=4,
                        out_parts=2,
                        name=f"ffn_up_{t}")
        act, dgu = _conv_gate_fwd(a_pre, ex.conv_w(l), p["conv_b"], f"conv_gate_{t}")
        x2 = _matmul(act, ex.weight(l, "w_down"), mode="nn", out_dtype=F32, tm=1024, tn=256, tk=D_FF, res=x1,
                     name=f"ffn_down_{t}")
        saved.append(dict(x=x, h=h, z=z, qn=qn, kn=kn, vb=vb, ug=ug, vn=vn, attn=attn, sgu=sgu, mixed=mixed, x1=x1, h2=h2,
                          a_pre=a_pre, act=act, dgu=dgu, probs=probs, psink=psink, gate_kept=gate_kept))
        x = x2
    loss_tile, dx, dxb = _loss_head(x, target, "loss_head")
    for l in reversed(range(DEPTH)):
        p, sv = small[l], saved[l]
        t = f"l{l}"
        def weight_grad(name, a, g, between, g_parts=0):
            ex.pair_send(l, name, _grad_half(name, a, g, ex.o_arr, None, f"g_{name}_other_{t}", g_parts))
            out = between()
            ex.scatter(l, name, _grad_half(name, a, g, ex.c_arr, ex.pair_recv(l, name), f"g_{name}_own_{t}", g_parts))
            ex.point()
            return out

        def after_down():
            dact = _matmul(dxb, ex.weight(l, "w_down"), mode="nt", out_dtype=BF16, tm=1024, tn=512, tk=2048,
                           name=f"d_act_{t}")
            return _conv_gate_bwd(sv["a_pre"], sv["dgu"], ex.conv_w(l), dact, f"conv_gate_bwd_{t}")

        dap, dcw, dcb = weight_grad("w_down", sv["act"], dxb, after_down)

        def after_up():
            dh2 = _matmul(dap, ex.weight(l, "w_up"), mode="nt", out_dtype=F32, tm=1024, tn=1024, tk=2816, a_parts=2,
                          b_parts=4, name=f"d_h2_{t}")
            return _rms_bwd(sv["x1"], p["norm2_g"], dh2, dx, f"norm2_bwd_{t}")

        dx1, dx1b, dg2 = weight_grad("w_up", sv["h2"], dap, after_up, g_parts=2)
        ex.pair_send(l, "w_o", _grad_half("w_o", sv["mixed"], dx1b, ex.o_arr, None, f"g_w_o_other_{t}"))
        dmixed = _matmul(dx1b, ex.weight(l, "w_o"), mode="nt", out_dtype=F32, tm=1024, tn=512, tk=2048,
                         name=f"d_mixed_{t}")
        dqn, dkn, dvb, dug, dvn, dws, dbs, dsk, dga, dgs = _mixer_bwd(
            sv["qn"], sv["kn"], sv["vb"], sv["ug"], sv["vn"], sv["attn"], sv["sgu"], dmixed, p["w_s_bf16"], p["b_s_tile"],
            p["attn_out_g"], p["sgu_out_g"], sv["probs"], sv["psink"], f"mixer_bwd_{t}")
        dz, dqg, dkg, dlg, dlb = _proj_post_bwd(sv["z"], dqn, dkn, dvb, dug, dvn, *sv["gate_kept"], p["q_norm_g"], p["k_norm_g"],
                                                 p["sgu_ln_g"], cosf, sinf, f"proj_post_bwd_{t}")
        ex.scatter(l, "w_o", _grad_half("w_o", sv["mixed"], dx1b, ex.c_arr, ex.pair_recv(l, "w_o"), f"g_w_o_own_{t}"))
        ex.point()

        def after_in():
            dh = _matmul_nt_slabs(dz, ex.weight(l, "w_in"), tm=1024, tn=512, name=f"d_h_{t}")
            return _rms_bwd(sv["x"], p["norm1_g"], dh, dx1, f"norm1_bwd_{t}")

        dx, dxb, dg1 = weight_grad("w_in", sv["h"], dz, after_in)
        ex.small_grads(l, dict(
            norm1_g=dg1[0], q_norm_g=dqg[0], k_norm_g=dkg[0], sink=dsk[:, 0], sgu_ln_g=dlg[0], sgu_ln_b=dlb[0], w_s=dws,
            b_s=dbs[:, :, 0], attn_out_g=dga[0], sgu_out_g=dgs[0], norm2_g=dg2[0],
            conv_w=jnp.concatenate([dcw[0], dcw[1]], axis=-1), conv_b=jnp.concatenate([dcb[0, 0], dcb[1, 0]], axis=-1)),
            loss_tile if l == 0 else None)
    return dx


def _small_views(l, norm1_g, q_norm_g, k_norm_g, sink, sgu_ln_g, sgu_ln_b, w_s, b_s, attn_out_g, sgu_out_g, norm2_g, conv_b):
    return dict(
        norm1_g=norm1_g[l][None], q_norm_g=q_norm_g[l][None], k_norm_g=k_norm_g[l][None], sink=sink[l],
        sgu_ln_g=sgu_ln_g[l][None], sgu_ln_b=sgu_ln_b[l][None], w_s_bf16=w_s[l].astype(BF16),
        b_s_tile=jnp.broadcast_to(b_s[l][:, :, None], (N_GMLP_HEADS, BLOCK, BLOCK)), attn_out_g=attn_out_g[l][None],
        sgu_out_g=sgu_out_g[l][None], norm2_g=norm2_g[l][None], conv_b=conv_b[l][None])


SMALL_NAMES = ("norm1_g", "q_norm_g", "k_norm_g", "sink", "sgu_ln_g", "sgu_ln_b", "w_s", "b_s", "attn_out_g", "sgu_out_g",
               "norm2_g", "conv_b", "conv_w")
REPLICATED_NAMES = SMALL_NAMES[:-1]
BIG_NAMES = ("w_in", "w_o", "w_up", "w_down")
PACK_LANES = 128
PACK_ALIGN = 8 * PACK_LANES


def _pack_rows(shape):
    return -(-math.prod(shape) // PACK_ALIGN) * 8


def _pack_parts(arrays):
    parts = []
    for a in arrays:
        flat = a.reshape(-1)
        parts.append(jnp.pad(flat, (0, _pack_rows(a.shape) * PACK_LANES - flat.shape[0])).reshape(-1, PACK_LANES))
    return parts


def _pack_call(arrays, name):
    parts = _pack_parts(arrays)
    total = sum(p.shape[0] for p in parts)

    def body(*refs):
        o_ref, at = refs[-1], 0
        for p_ref in refs[:-1]:
            o_ref[at:at + p_ref.shape[0], :] = p_ref[...]
            at += p_ref.shape[0]

    vm = pl.BlockSpec(memory_space=pltpu.VMEM)
    return _ordered_call(
        body, name=name, out_shape=jax.ShapeDtypeStruct((total, PACK_LANES), F32), in_specs=[vm] * len(parts), out_specs=vm,
        compiler_params=pltpu.CompilerParams(vmem_limit_bytes=V7X_VMEM_LIMIT),
    )(*parts)


def _unpack_layers(stacked, shapes):
    nl = stacked.shape[0]
    out, at = [], 0
    for shp in shapes:
        rows = _pack_rows(shp)
        out.append(stacked[:, at:at + rows].reshape(nl, -1)[:, :math.prod(shp)].reshape((nl,) + tuple(shp)))
        at += rows
    return out


def _adamw_packed(w, g, m, v, rows, layer, into, name):
    head = pl.BlockSpec((rows, PACK_LANES), lambda i: (0, 0))
    at_layer = pl.BlockSpec((None, rows, PACK_LANES), lambda i: (layer, 0, 0))

    def body(w_ref, g_ref, m_ref, v_ref, *rest):
        d_ref, nm_ref, nv_ref = rest[-3:]
        gv = g_ref[...]
        mn = ADAM_B1 * m_ref[...] + (1.0 - ADAM_B1) * gv
        vn = ADAM_B2 * v_ref[...] + (1.0 - ADAM_B2) * (gv * gv)
        m_hat = mn / (1.0 - ADAM_B1 ** ADAM_STEP)
        v_hat = vn / (1.0 - ADAM_B2 ** ADAM_STEP)
        d_ref[...] = -ADAM_LR * (m_hat / (jnp.sqrt(v_hat) + ADAM_EPS) + ADAM_WD * w_ref[...])
        nm_ref[...] = mn
        nv_ref[...] = vn

    in_specs = [head] * 4
    operands = [w, g, m, v]
    aliases = {}
    if into is not None:
        in_specs += [ANY] * 3
        operands += list(into)
        aliases = {4 + i: i for i in range(3)}
    sds = jax.ShapeDtypeStruct((DEPTH, rows, PACK_LANES), F32)
    return _ordered_call(
        body, name=name, out_shape=(sds,) * 3, grid=(1,), in_specs=in_specs, out_specs=(at_layer,) * 3,
        input_output_aliases=aliases, compiler_params=_params(("arbitrary",)),
    )(*operands)


def kernel(x, norm1_g, w_in, q_norm_g, k_norm_g, sink, sgu_ln_g, sgu_ln_b, w_s, b_s, attn_out_g, sgu_out_g, w_o, norm2_g, w_up, conv_w, conv_b, w_down, loss_target, m_norm1_g, m_w_in, m_q_norm_g, m_k_norm_g, m_sink, m_sgu_ln_g, m_sgu_ln_b, m_w_s, m_b_s, m_attn_out_g, m_sgu_out_g, m_w_o, m_norm2_g, m_w_up, m_conv_w, m_conv_b, m_w_down, v_norm1_g, v_w_in, v_q_norm_g, v_k_norm_g, v_sink, v_sgu_ln_g, v_sgu_ln_b, v_w_s, v_b_s, v_attn_out_g, v_sgu_out_g, v_w_o, v_norm2_g, v_w_up, v_conv_w, v_conv_b, v_w_down):
    weights = dict(norm1_g=norm1_g, w_in=w_in, q_norm_g=q_norm_g, k_norm_g=k_norm_g, sink=sink, sgu_ln_g=sgu_ln_g,
                   sgu_ln_b=sgu_ln_b, w_s=w_s, b_s=b_s, attn_out_g=attn_out_g, sgu_out_g=sgu_out_g, w_o=w_o, norm2_g=norm2_g,
                   w_up=w_up, conv_w=conv_w, conv_b=conv_b, w_down=w_down)
    m_in = dict(norm1_g=m_norm1_g, w_in=m_w_in, q_norm_g=m_q_norm_g, k_norm_g=m_k_norm_g, sink=m_sink, sgu_ln_g=m_sgu_ln_g,
                sgu_ln_b=m_sgu_ln_b, w_s=m_w_s, b_s=m_b_s, attn_out_g=m_attn_out_g, sgu_out_g=m_sgu_out_g, w_o=m_w_o,
                norm2_g=m_norm2_g, w_up=m_w_up, conv_w=m_conv_w, conv_b=m_conv_b, w_down=m_w_down)
    v_in = dict(norm1_g=v_norm1_g, w_in=v_w_in, q_norm_g=v_q_norm_g, k_norm_g=v_k_norm_g, sink=v_sink, sgu_ln_g=v_sgu_ln_g,
                sgu_ln_b=v_sgu_ln_b, w_s=v_w_s, b_s=v_b_s, attn_out_g=v_attn_out_g, sgu_out_g=v_sgu_out_g, w_o=v_w_o,
                norm2_g=v_norm2_g, w_up=v_w_up, conv_w=v_conv_w, conv_b=v_conv_b, w_down=v_w_down)
    cx, cy, cc = lax.axis_index("x"), lax.axis_index("y"), lax.axis_index("c")
    j_me = 2 * cx + cy
    c_arr = jnp.reshape(cc, (1,)).astype(jnp.int32)
    j_arr = jnp.reshape(j_me, (1,)).astype(jnp.int32)

    _Order.last = None
    ex = _Exchange(weights, m_in, v_in, j_arr, c_arr, jnp.reshape(4 * cx + 2 * cy + cc, (1,)).astype(jnp.int32))
    small = [_small_views(l, norm1_g, q_norm_g, k_norm_g, sink, sgu_ln_g, sgu_ln_b, w_s, b_s, attn_out_g, sgu_out_g, norm2_g,
                          conv_b) for l in range(DEPTH)]
    packed_in = [[_pack_call([src[nm][l] for nm in REPLICATED_NAMES], f"pack_{tag}_l{l}")
                  for tag, src in (("w", weights), ("m", m_in), ("v", v_in))] for l in range(DEPTH)]
    dx = _local_step(x[0], loss_target[0], ex, small)
    big_out = ex.finish()

    rep_shapes = [weights[nm].shape[1:] for nm in REPLICATED_NAMES]
    rep_rows = sum(_pack_rows(shp) for shp in rep_shapes)
    cw_shape = (3, 2 * D_FF)
    sums, adam_small = [None] * DEPTH, None
    for l in reversed(range(DEPTH)):
        sums[l] = ex.small_sum(l)
        pw, pm, pv = packed_in[l]
        adam_small = _adamw_packed(pw, sums[l], pm, pv, rep_rows, l, adam_small, f"adamw_small_l{l}")
    cw_rows = _pack_rows(cw_shape)
    loss = sums[0][rep_rows + cw_rows, 0]
    stacked = jnp.stack([sm[:rep_rows + cw_rows] for sm in sums])
    grads = dict(zip(REPLICATED_NAMES, _unpack_layers(stacked[:, :rep_rows], rep_shapes)))
    delta, new_m, new_v = (dict(zip(REPLICATED_NAMES, _unpack_layers(arr, rep_shapes))) for arr in adam_small)
    cw_cols = 2 * D_FF // N_CHIPS
    cw_grad = lax.dynamic_slice_in_dim(_unpack_layers(stacked[:, rep_rows:], [cw_shape])[0], j_me * cw_cols, cw_cols, axis=2)
    flat = lambda a: a.reshape(DEPTH * 3, cw_cols)
    cw_out = _adamw(flat(conv_w), flat(cw_grad), flat(m_conv_w), flat(v_conv_w), "adamw_conv_w")
    grads["conv_w"], delta["conv_w"], new_m["conv_w"], new_v["conv_w"] = (a.reshape(DEPTH, 3, cw_cols) for a in cw_out)

    for name in BIG_NAMES:
        grads[name], delta[name], new_m[name], new_v[name] = big_out[name]

    order = ("norm1_g", "w_in", "q_norm_g", "k_norm_g", "sink", "sgu_ln_g", "sgu_ln_b", "w_s", "b_s", "attn_out_g", "sgu_out_g",
             "w_o", "norm2_g", "w_up", "conv_w", "conv_b", "w_down")
    return (loss, dx[None], *[grads[nm] for nm in order], *[delta[nm] for nm in order], *[new_m[nm] for nm in order],
            *[new_v[nm] for nm in order])
```

```python
import math

import jax
import jax.numpy as jnp
from jax import lax
from jax.experimental import pallas as pl
from jax.experimental.pallas import tpu as pltpu

F32 = jnp.float32
BF16 = jnp.bfloat16

D_MODEL = 2048
HEAD_DIM = 128
ATTN_WIDTH = 1024
N_Q_HEADS = 8
N_KV_HEADS = 2
GQA_GROUP = 4
KV_WIDTH = 256
GMLP_WIDTH = 1024
N_GMLP_HEADS = 8
BLOCK = 128
IN_WIDTH = 3584
D_FF = 5632
DEPTH = 2
EPS = 1e-6
MASK_VALUE = -1e30
ROPE_THETA = 10000.0
N_CHIPS = 4

ADAM_LR = 0.001
ADAM_B1 = 0.9
ADAM_B2 = 0.999
ADAM_EPS = 1e-08
ADAM_WD = 0.01
ADAM_STEP = 10

V7X_VMEM_LIMIT = 48 * 1024 * 1024
MESH = pl.DeviceIdType.MESH

_GELU_C = math.sqrt(2.0 / math.pi)
_GELU_A = 0.044715


def _params(sem=None):
    return pltpu.CompilerParams(dimension_semantics=sem, vmem_limit_bytes=V7X_VMEM_LIMIT)


ANY = pl.BlockSpec(memory_space=pl.ANY)


class _Order:
    last = None


def _ordered_call(body, *, token_index=0, **kw):
    def run(*operands):
        tok = _Order.last
        if tok is None or any(op is tok for op in operands):
            call = pl.pallas_call(body, **kw)
        else:
            n_in = len(operands)

            def ordered_body(*refs):
                return body(*refs[:n_in], *refs[n_in + 1:])

            kw2 = dict(kw)
            if "grid_spec" in kw2:
                gs = kw2["grid_spec"]
                kw2["grid_spec"] = pltpu.PrefetchScalarGridSpec(
                    num_scalar_prefetch=gs.num_scalar_prefetch, grid=gs.grid, in_specs=list(gs.in_specs) + [ANY],
                    out_specs=gs.out_specs, scratch_shapes=gs.scratch_shapes)
            else:
                kw2["in_specs"] = list(kw2["in_specs"]) + [ANY]
            call = pl.pallas_call(ordered_body, **kw2)
            operands = operands + (tok,)
        out = call(*operands)
        _Order.last = out[token_index] if isinstance(out, (tuple, list)) else out
        return out

    return run


def _gelu(x):
    return x * (0.5 * (1.0 + jnp.tanh(_GELU_C * (x + _GELU_A * (x * x * x)))))


def _gelu_grad(x):
    x2 = x * x
    t = jnp.tanh(_GELU_C * (x + _GELU_A * (x * x2)))
    return 0.5 * (1.0 + t) + 0.5 * x * (1.0 - t * t) * (_GELU_C * (1.0 + 3.0 * _GELU_A * x2))


def _mean_last(x):
    return jnp.mean(x, axis=-1, keepdims=True)


def _sum_rows(x):
    return jnp.sum(x, axis=0, keepdims=True)


def _sum_all(x):
    return jnp.sum(jnp.sum(x, axis=1, keepdims=True), axis=0, keepdims=True)


def _matmul(a, b, *, mode, out_dtype, tm, tn, tk, name, res=None, a_parts=0, b_parts=0, out_parts=0):
    assert mode in ("nn", "nt"), mode
    if mode == "nn":
        assert not a_parts
        m, k = a.shape
        n = b.shape[0] * b.shape[2] if b_parts else b.shape[1]
    else:
        m, k = (a.shape[1], a.shape[0] * a.shape[2]) if a_parts else a.shape
        n = b.shape[1] if b_parts else b.shape[0]
    tm, tn, tk = min(tm, m), min(tn, n), min(tk, k)
    assert m % tm == 0 and n % tn == 0 and k % tk == 0, (name, m, n, k, tm, tn, tk)
    nm, nn, nk = m // tm, n // tn, k // tk

    def slab(idx, total_tiles, parts):
        per = total_tiles // parts
        assert per * parts == total_tiles, (name, total_tiles, parts)
        return idx // per, idx % per

    if mode == "nn":
        a_spec = pl.BlockSpec((tm, tk), lambda i, j, kk: (i, kk))
        if b_parts:
            b_spec = pl.BlockSpec((None, tk, tn), lambda i, j, kk: (slab(j, nn, b_parts)[0], kk, slab(j, nn, b_parts)[1]))
        else:
            b_spec = pl.BlockSpec((tk, tn), lambda i, j, kk: (kk, j))
        dims = (((1,), (0,)), ((), ()))
    else:
        if a_parts:
            a_spec = pl.BlockSpec((None, tm, tk), lambda i, j, kk: (slab(kk, nk, a_parts)[0], i, slab(kk, nk, a_parts)[1]))
        else:
            a_spec = pl.BlockSpec((tm, tk), lambda i, j, kk: (i, kk))
        if b_parts:
            b_spec = pl.BlockSpec((None, tn, tk), lambda i, j, kk: (slab(kk, nk, b_parts)[0], j, slab(kk, nk, b_parts)[1]))
        else:
            b_spec = pl.BlockSpec((tn, tk), lambda i, j, kk: (j, kk))
        dims = (((1,), (1,)), ((), ()))
    if out_parts:
        out_shape = jax.ShapeDtypeStruct((out_parts, m, n // out_parts), out_dtype)
        out_spec = pl.BlockSpec((None, tm, tn), lambda i, j, kk: (slab(j, nn, out_parts)[0], i, slab(j, nn, out_parts)[1]))
    else:
        out_shape = jax.ShapeDtypeStruct((m, n), out_dtype)
        out_spec = pl.BlockSpec((tm, tn), lambda i, j, kk: (i, j))
    in_specs = [a_spec, b_spec]
    operands = [a, b]
    if res is not None:
        in_specs.append(pl.BlockSpec((tm, tn), lambda i, j, kk: (i, j)))
        operands.append(res)

    def body(*refs):
        a_ref, b_ref = refs[0], refs[1]
        res_ref = refs[2] if res is not None else None
        o_ref = refs[3] if res is not None else refs[2]
        p = lax.dot_general(a_ref[...], b_ref[...], dims, preferred_element_type=F32)

        def finish(total):
            if res_ref is not None:
                total = res_ref[...] + total
            o_ref[...] = total.astype(out_dtype)

        if nk == 1:
            finish(p)
        else:
            acc_ref = refs[-1]
            kk = pl.program_id(2)

            @pl.when(kk == 0)
            def _():
                acc_ref[...] = p

            @pl.when(jnp.logical_and(kk > 0, kk < nk - 1))
            def _():
                acc_ref[...] += p

            @pl.when(kk == nk - 1)
            def _():
                finish(acc_ref[...] + p)

    scratch = [pltpu.VMEM((tm, tn), F32)] if nk > 1 else []
    return _ordered_call(
        body, name=name, out_shape=out_shape, grid=(nm, nn, nk), in_specs=in_specs, out_specs=out_spec,
        scratch_shapes=scratch, compiler_params=_params(("parallel", "parallel", "arbitrary")),
    )(*operands)


def _matmul_nt_slabs(a, b, *, tm, tn, name, a_parts=0):
    nslab, n, ks = b.shape
    m = a.shape[1] if a_parts else a.shape[0]
    tm, tn = min(tm, m), min(tn, n)
    assert m % tm == 0 and n % tn == 0, (name, m, n, tm, tn)
    if a_parts:
        per = nslab // a_parts
        assert per * a_parts == nslab and a.shape[2] == per * ks, (name, a.shape, b.shape)
        a_spec = pl.BlockSpec((a_parts, tm, per * ks), lambda i, j: (0, i, 0))
    else:
        assert a.shape[1] == nslab * ks, (name, a.shape, b.shape)
        a_spec = pl.BlockSpec((tm, nslab * ks), lambda i, j: (i, 0))

    def body(a_ref, b_ref, o_ref):
        total = None
        for sl in range(nslab):
            if a_parts:
                a_sl = a_ref[sl // per, :, (sl % per) * ks:(sl % per + 1) * ks]
            else:
                a_sl = a_ref[:, sl * ks:(sl + 1) * ks]
            p = lax.dot_general(a_sl, b_ref[sl], (((1,), (1,)), ((), ())), preferred_element_type=F32)
            total = p if total is None else total + p
        o_ref[...] = total

    return _ordered_call(
        body, name=name, out_shape=jax.ShapeDtypeStruct((m, n), F32), grid=(m // tm, n // tn),
        in_specs=[a_spec, pl.BlockSpec((nslab, tn, ks), lambda i, j: (0, j, 0))],
        out_specs=pl.BlockSpec((tm, tn), lambda i, j: (i, j)), compiler_params=_params(("parallel", "parallel")),
    )(a, b)


GRAD_HALVES = {
    "w_in": ("rows_of_slab", 1024, 896), "w_up": ("rows_of_slab", 1024, 1408), "w_o": ("rows_of_block", 256, 2048),
    "w_down": ("cols_of_block", 1408, 512)}


def _half_shape(name, shard_shape):
    r, cols = shard_shape
    return (r, cols // 2) if GRAD_HALVES[name][0] == "cols_of_block" else (r // 2, cols)


def _grad_half(name, a, g, sel, res, call_name, g_parts=0):
    kind, tm, tn = GRAD_HALVES[name]
    s, m = a.shape
    n = g.shape[0] * g.shape[2] if g_parts else g.shape[1]
    if kind == "rows_of_slab":
        rh, hc = m // 2, n // N_CHIPS
        per = hc // tn
        grid = (rh // tm, n // tn)
        a_map = lambda i, j, sel_ref: (0, sel_ref[0] * (rh // tm) + i)
        g_col = lambda i, j, sel_ref: j
        o_map = lambda i, j, sel_ref: (j // per, i, j % per)
    elif kind == "rows_of_block":
        rh, hc = m // N_CHIPS // 2, n
        assert tm == rh
        grid = (N_CHIPS, n // tn)
        a_map = lambda i, j, sel_ref: (0, 2 * i + sel_ref[0])
        g_col = lambda i, j, sel_ref: j
        o_map = lambda i, j, sel_ref: (i, 0, j)
    else:
        rh, hc = m // N_CHIPS, n // 2
        assert tm == rh
        grid = (N_CHIPS, hc // tn)
        a_map = lambda i, j, sel_ref: (0, i)
        g_col = lambda i, j, sel_ref: sel_ref[0] * (hc // tn) + j
        o_map = lambda i, j, sel_ref: (i, 0, j)
    if g_parts:
        g_per = (n // tn) // g_parts
        g_spec = pl.BlockSpec((None, s, tn), lambda i, j, sel_ref: (g_col(i, j, sel_ref) // g_per, 0, g_col(i, j, sel_ref) % g_per))
    else:
        g_spec = pl.BlockSpec((s, tn), lambda i, j, sel_ref: (0, g_col(i, j, sel_ref)))
    o_spec = pl.BlockSpec((None, tm, tn), o_map)
    in_specs = [pl.BlockSpec((s, tm), a_map), g_spec] + ([o_spec] if res is not None else [])

    def body(sel_ref, a_ref, g_ref, *rest):
        o_ref = rest[-1]
        p = lax.dot_general(a_ref[...], g_ref[...], (((0,), (0,)), ((), ())), preferred_element_type=F32)
        if res is not None:
            p = p + rest[0][...].astype(F32)
        o_ref[...] = p.astype(BF16)

    grid_spec = pltpu.PrefetchScalarGridSpec(num_scalar_prefetch=1, grid=grid, in_specs=in_specs, out_specs=o_spec)
    return _ordered_call(
        body, name=call_name, out_shape=jax.ShapeDtypeStruct((N_CHIPS, rh, hc), BF16), grid_spec=grid_spec,
        compiler_params=_params(("parallel", "parallel")),
    )(sel, a, g, *([res] if res is not None else []))


def _row_tile(s):
    return min(256, s)


def _rows(width, tr):
    return pl.BlockSpec((tr, width), lambda i: (i, 0))


def _const2(shape):
    return pl.BlockSpec(shape, lambda i: (0, 0))


def _rms_fwd(x, g, name):
    s, d = x.shape
    tr = _row_tile(s)

    def body(x_ref, g_ref, o_ref):
        xv = x_ref[...]
        r = lax.rsqrt(_mean_last(xv * xv) + EPS)
        o_ref[...] = (xv * r * g_ref[...]).astype(BF16)

    return _ordered_call(
        body, name=name, out_shape=jax.ShapeDtypeStruct((s, d), BF16), grid=(s // tr,),
        in_specs=[_rows(d, tr), _const2((1, d))], out_specs=_rows(d, tr), compiler_params=_params(("parallel",)),
    )(x, g)


def _rms_bwd(x, g, dh, dres, name):
    s, d = x.shape
    tr = _row_tile(s)

    def body(x_ref, g_ref, dh_ref, dres_ref, dx_ref, dxb_ref, dg_ref):
        xv, dy = x_ref[...], dh_ref[...]
        r = lax.rsqrt(_mean_last(xv * xv) + EPS)
        gdy = dy * g_ref[...]
        dx = dres_ref[...] + r * gdy - xv * ((r * r * r) * _mean_last(xv * gdy))
        dx_ref[...] = dx
        dxb_ref[...] = dx.astype(BF16)

        @pl.when(pl.program_id(0) == 0)
        def _():
            dg_ref[...] = jnp.zeros_like(dg_ref)

        dg_ref[...] += _sum_rows(xv * r * dy)

    return _ordered_call(
        body, name=name,
        out_shape=(jax.ShapeDtypeStruct((s, d), F32), jax.ShapeDtypeStruct((s, d), BF16), jax.ShapeDtypeStruct((1, d), F32)),
        grid=(s // tr,), in_specs=[_rows(d, tr), _const2((1, d)), _rows(d, tr), _rows(d, tr)],
        out_specs=(_rows(d, tr), _rows(d, tr), _const2((1, d))), compiler_params=_params(("arbitrary",)),
    )(x, g, dh, dres)


Q0, K0, V0, GU0, GV0 = 0, ATTN_WIDTH, ATTN_WIDTH + KV_WIDTH, ATTN_WIDTH + 2 * KV_WIDTH, ATTN_WIDTH + 2 * KV_WIDTH + GMLP_WIDTH


def _head(h, base=0):
    return slice(base + h * HEAD_DIM, base + (h + 1) * HEAD_DIM)


def _proj_post(z, qg, kg, lg, lb, cosf, sinf, name):
    s = z.shape[0]
    tr = _row_tile(s)

    def body(z_ref, qg_ref, kg_ref, lg_ref, lb_ref, cos_ref, sin_ref, qn_ref, kn_ref, vb_ref, ug_ref, vn_ref,
             dgu_ref, dgv_ref, xhat_ref, rstd_ref):
        cos, sin = cos_ref[...], sin_ref[...]

        def norm_rope(xh, g):
            y = xh * lax.rsqrt(_mean_last(xh * xh) + EPS) * g
            return y * cos + pltpu.roll(y, HEAD_DIM // 2, 1) * sin

        for h in range(N_Q_HEADS):
            qn_ref[:, _head(h)] = norm_rope(z_ref[:, _head(h, Q0)].astype(F32), qg_ref[...]).astype(BF16)
        for h in range(N_KV_HEADS):
            kn_ref[:, _head(h)] = norm_rope(z_ref[:, _head(h, K0)].astype(F32), kg_ref[...]).astype(BF16)
        vb_ref[...] = z_ref[:, V0:GU0]
        gu = z_ref[:, GU0:GV0].astype(F32)
        ug_ref[...] = _gelu(gu)
        dgu_ref[...] = _gelu_grad(gu).astype(BF16)
        gv = z_ref[:, GV0:IN_WIDTH].astype(F32)
        vg = _gelu(gv)
        dgv_ref[...] = _gelu_grad(gv).astype(BF16)
        xc = vg - _mean_last(vg)
        r = lax.rsqrt(_mean_last(xc * xc) + EPS)
        y = xc * r
        xhat_ref[...] = y.astype(BF16)
        rstd_ref[...] = r
        vn_ref[...] = (y * lg_ref[...] + lb_ref[...]).astype(BF16)

    wide = jax.ShapeDtypeStruct((s, GMLP_WIDTH), BF16)
    return _ordered_call(
        body, name=name,
        out_shape=(jax.ShapeDtypeStruct((s, ATTN_WIDTH), BF16), jax.ShapeDtypeStruct((s, KV_WIDTH), BF16),
                   jax.ShapeDtypeStruct((s, KV_WIDTH), BF16), jax.ShapeDtypeStruct((s, GMLP_WIDTH), F32), wide,
                   wide, wide, wide, jax.ShapeDtypeStruct((s, 1), F32)),
        grid=(s // tr,),
        in_specs=[_rows(IN_WIDTH, tr), _const2((1, HEAD_DIM)), _const2((1, HEAD_DIM)), _const2((1, GMLP_WIDTH)),
                  _const2((1, GMLP_WIDTH)), _rows(HEAD_DIM, tr), _rows(HEAD_DIM, tr)],
        out_specs=(_rows(ATTN_WIDTH, tr), _rows(KV_WIDTH, tr), _rows(KV_WIDTH, tr), _rows(GMLP_WIDTH, tr), _rows(GMLP_WIDTH, tr),
                   _rows(GMLP_WIDTH, tr), _rows(GMLP_WIDTH, tr), _rows(GMLP_WIDTH, tr), _rows(1, tr)),
        compiler_params=_params(("parallel",)),
    )(z, qg, kg, lg, lb, cosf, sinf)


def _proj_post_bwd(z, dqn, dkn, dvb, dug, dvn, gelu_grad_u, gelu_grad_v, xhat_v, rstd_v, qg, kg, lg, cosf, sinf, name):
    s = z.shape[0]
    tr = _row_tile(s)

    def body(z_ref, dqn_ref, dkn_ref, dvb_ref, dug_ref, dvn_ref, ggu_ref, ggv_ref, xhat_ref, rstd_ref, qg_ref, kg_ref, lg_ref,
             cos_ref, sin_ref, dz_ref, dqg_ref, dkg_ref, dlg_ref, dlb_ref):
        cos, sin = cos_ref[...], sin_ref[...]

        @pl.when(pl.program_id(0) == 0)
        def _():
            dqg_ref[...] = jnp.zeros_like(dqg_ref)
            dkg_ref[...] = jnp.zeros_like(dkg_ref)
            dlg_ref[...] = jnp.zeros_like(dlg_ref)
            dlb_ref[...] = jnp.zeros_like(dlb_ref)

        def norm_rope_bwd(xh, g, dout):
            dy = dout * cos - pltpu.roll(dout, HEAD_DIM // 2, 1) * sin
            r = lax.rsqrt(_mean_last(xh * xh) + EPS)
            xhat = xh * r
            gdy = dy * g
            return r * (gdy - xhat * _mean_last(xhat * gdy)), _sum_rows(xhat * dy)

        dqg = jnp.zeros((1, HEAD_DIM), F32)
        for h in range(N_Q_HEADS):
            dx, dg = norm_rope_bwd(z_ref[:, _head(h, Q0)].astype(F32), qg_ref[...], dqn_ref[:, _head(h)])
            dz_ref[:, _head(h, Q0)] = dx.astype(BF16)
            dqg = dqg + dg
        dqg_ref[...] += dqg
        dkg = jnp.zeros((1, HEAD_DIM), F32)
        for h in range(N_KV_HEADS):
            dx, dg = norm_rope_bwd(z_ref[:, _head(h, K0)].astype(F32), kg_ref[...], dkn_ref[:, _head(h)])
            dz_ref[:, _head(h, K0)] = dx.astype(BF16)
            dkg = dkg + dg
        dkg_ref[...] += dkg
        dz_ref[:, V0:GU0] = dvb_ref[...].astype(BF16)
        dz_ref[:, GU0:GV0] = (dug_ref[...] * ggu_ref[...].astype(F32)).astype(BF16)
        xhat = xhat_ref[...].astype(F32)
        dvn_v = dvn_ref[...]
        dlg_ref[...] += _sum_rows(xhat * dvn_v)
        dlb_ref[...] += _sum_rows(dvn_v)
        dxh = dvn_v * lg_ref[...]
        dvg = rstd_ref[...] * (dxh - _mean_last(dxh) - xhat * _mean_last(dxh * xhat))
        dz_ref[:, GV0:IN_WIDTH] = (dvg * ggv_ref[...].astype(F32)).astype(BF16)

    return _ordered_call(
        body, name=name,
        out_shape=(jax.ShapeDtypeStruct((s, IN_WIDTH), BF16), jax.ShapeDtypeStruct((1, HEAD_DIM), F32),
                   jax.ShapeDtypeStruct((1, HEAD_DIM), F32), jax.ShapeDtypeStruct((1, GMLP_WIDTH), F32),
                   jax.ShapeDtypeStruct((1, GMLP_WIDTH), F32)),
        grid=(s // tr,),
        in_specs=[_rows(V0, tr), _rows(ATTN_WIDTH, tr), _rows(KV_WIDTH, tr), _rows(KV_WIDTH, tr), _rows(GMLP_WIDTH, tr),
                  _rows(GMLP_WIDTH, tr), _rows(GMLP_WIDTH, tr), _rows(GMLP_WIDTH, tr), _rows(GMLP_WIDTH, tr), _rows(1, tr),
                  _const2((1, HEAD_DIM)), _const2((1, HEAD_DIM)), _const2((1, GMLP_WIDTH)), _rows(HEAD_DIM, tr),
                  _rows(HEAD_DIM, tr)],
        out_specs=(_rows(IN_WIDTH, tr), _const2((1, HEAD_DIM)), _const2((1, HEAD_DIM)), _const2((1, GMLP_WIDTH)),
                   _const2((1, GMLP_WIDTH))),
        compiler_params=_params(("arbitrary",)),
    )(z, dqn, dkn, dvb, dug, dvn, gelu_grad_u, gelu_grad_v, xhat_v, rstd_v, qg, kg, lg, cosf, sinf)


def _band_valid(n, s):
    shape = (GQA_GROUP * BLOCK, 3 * BLOCK)
    i = lax.broadcasted_iota(jnp.int32, shape, 0) & (BLOCK - 1)
    j = lax.broadcasted_iota(jnp.int32, shape, 1)
    k_pos = n * BLOCK - BLOCK + j
    return (jnp.abs(j - BLOCK - i) <= BLOCK) & (k_pos >= 0) & (k_pos < s)


def _group_rows(x, kh):
    return jnp.concatenate([x[:, _head(kh * GQA_GROUP + g)] for g in range(GQA_GROUP)], axis=0)


def _group_sinks(sink_ref, kh):
    return jnp.concatenate([jnp.full((BLOCK, 1), sink_ref[kh * GQA_GROUP + g], F32) for g in range(GQA_GROUP)], axis=0)


def _rows_of(x, g):
    return x[g * BLOCK:(g + 1) * BLOCK]


def _probs(q, kb, sink_h, valid):
    sc = lax.dot_general(q, kb, (((1,), (1,)), ((), ())), preferred_element_type=F32) * (HEAD_DIM ** -0.5)
    sc = jnp.where(valid, sc, MASK_VALUE)
    m = jnp.maximum(jnp.max(sc, axis=-1, keepdims=True), sink_h)
    p = jnp.exp(sc - m)
    es = jnp.exp(sink_h - m)
    den = jnp.sum(p, axis=-1, keepdims=True) + es
    inv = 1.0 / den
    return p * inv, es * inv


def _band_specs(width, nb):
    return [pl.BlockSpec((BLOCK, width), lambda n: (jnp.maximum(n - 1, 0), 0)),
            pl.BlockSpec((BLOCK, width), lambda n: (n, 0)),
            pl.BlockSpec((BLOCK, width), lambda n: (jnp.minimum(n + 1, nb - 1), 0))]


def _blk(width):
    return pl.BlockSpec((BLOCK, width), lambda n: (n, 0))


def _whole3(shape):
    return pl.BlockSpec(shape, lambda n: (0, 0, 0))


def _smem():
    return pl.BlockSpec(memory_space=pltpu.SMEM)


def _mixer_fwd(qn, kn, vb, ug, vn, wsb, bsb, sink, ga, gs, name):
    s = qn.shape[0]
    nb = s // BLOCK

    def body(sink_ref, q_ref, kp_ref, kc_ref, kx_ref, vp_ref, vc_ref, vx_ref, ug_ref, vn_ref, ws_ref, bs_ref, ga_ref, gs_ref,
             attn_ref, sgu_ref, mix_ref, probs_ref, psink_ref):
        n = pl.program_id(0)
        valid = _band_valid(n, s)
        ssq = jnp.zeros((BLOCK, 1), F32)
        for kh in range(N_KV_HEADS):
            kb = jnp.concatenate([kp_ref[:, _head(kh)], kc_ref[:, _head(kh)], kx_ref[:, _head(kh)]], axis=0)
            vbd = jnp.concatenate([vp_ref[:, _head(kh)], vc_ref[:, _head(kh)], vx_ref[:, _head(kh)]], axis=0)
            p, p_sink = _probs(_group_rows(q_ref, kh), kb, _group_sinks(sink_ref, kh), valid)
            pb = p.astype(BF16)
            probs_ref[kh] = pb
            psink_ref[kh] = p_sink
            o4 = jnp.dot(pb, vbd, preferred_element_type=F32)
            for g in range(GQA_GROUP):
                o = _rows_of(o4, g)
                attn_ref[:, _head(kh * GQA_GROUP + g)] = o
                ssq = ssq + jnp.sum(o * o, axis=-1, keepdims=True)
        r = lax.rsqrt(ssq * (1.0 / ATTN_WIDTH) + EPS)
        mix_ref[:, 0:ATTN_WIDTH] = (attn_ref[...] * r * ga_ref[...]).astype(BF16)
        ssq = jnp.zeros((BLOCK, 1), F32)
        for h in range(N_GMLP_HEADS):
            f = jnp.dot(ws_ref[h], vn_ref[:, _head(h)], preferred_element_type=F32) + bs_ref[h]
            o = ug_ref[:, _head(h)] * f
            sgu_ref[:, _head(h)] = o
            ssq = ssq + jnp.sum(o * o, axis=-1, keepdims=True)
        r = lax.rsqrt(ssq * (1.0 / GMLP_WIDTH) + EPS)
        mix_ref[:, ATTN_WIDTH:D_MODEL] = (sgu_ref[...] * r * gs_ref[...]).astype(BF16)

    hh = (N_GMLP_HEADS, BLOCK, BLOCK)
    return _ordered_call(
        body, name=name,
        out_shape=(jax.ShapeDtypeStruct((s, ATTN_WIDTH), F32), jax.ShapeDtypeStruct((s, GMLP_WIDTH), F32),
                   jax.ShapeDtypeStruct((s, D_MODEL), BF16), jax.ShapeDtypeStruct((nb,) + PROBS_BLOCK, BF16),
                   jax.ShapeDtypeStruct((nb,) + PSINK_BLOCK, F32)),
        grid=(nb,),
        in_specs=[_smem(), _blk(ATTN_WIDTH)] + _band_specs(KV_WIDTH, nb) + _band_specs(KV_WIDTH, nb)
        + [_blk(GMLP_WIDTH), _blk(GMLP_WIDTH), _whole3(hh), _whole3(hh),
           pl.BlockSpec((1, ATTN_WIDTH), lambda n: (0, 0)), pl.BlockSpec((1, GMLP_WIDTH), lambda n: (0, 0))],
        out_specs=(_blk(ATTN_WIDTH), _blk(GMLP_WIDTH), _blk(D_MODEL), _per_block(PROBS_BLOCK), _per_block(PSINK_BLOCK)),
        compiler_params=_params(("parallel",)),
    )(sink, qn, kn, kn, kn, vb, vb, vb, ug, vn, wsb, bsb, ga, gs)


PROBS_BLOCK = (N_KV_HEADS, GQA_GROUP * BLOCK, 3 * BLOCK)
PSINK_BLOCK = (N_KV_HEADS, GQA_GROUP * BLOCK, 1)


def _per_block(shape):
    return pl.BlockSpec((None,) + shape, lambda n: (n, 0, 0, 0))


def _mixer_bwd(qn, kn, vb, ug, vn, attn, sgu, dmixed, wsb, bsb, ga, gs, probs, psink, name):
    s = qn.shape[0]
    nb = s // BLOCK
    tn_dims = (((0,), (0,)), ((), ()))
    nt_dims = (((1,), (1,)), ((), ()))

    def body(q_ref, kp_ref, kc_ref, kx_ref, vp_ref, vc_ref, vx_ref, ug_ref, vn_ref, attn_ref, sgu_ref, dm_ref,
             ws_ref, bs_ref, ga_ref, gs_ref, probs_ref, psink_ref,
             dq_ref, dk_ref, dv_ref, dug_ref, dvn_ref, dws_ref, dbs_ref, dsk_ref, dga_ref, dgs_ref, dk_acc, dv_acc):
        n = pl.program_id(0)

        @pl.when(n == 0)
        def _():
            for ref in (dk_acc, dv_acc, dws_ref, dbs_ref, dsk_ref, dga_ref, dgs_ref):
                ref[...] = jnp.zeros_like(ref)

        def out_norm_bwd(o, g, dy):
            r = lax.rsqrt(_mean_last(o * o) + EPS)
            gdy = dy * g
            return r * gdy - o * ((r * r * r) * _mean_last(o * gdy)), _sum_rows(o * r * dy)

        d_attn, dga = out_norm_bwd(attn_ref[...], ga_ref[...], dm_ref[:, 0:ATTN_WIDTH])
        dga_ref[...] += dga
        d_sgu, dgs = out_norm_bwd(sgu_ref[...], gs_ref[...], dm_ref[:, ATTN_WIDTH:D_MODEL])
        dgs_ref[...] += dgs

        for h in range(N_GMLP_HEADS):
            vn_h = vn_ref[:, _head(h)]
            f = jnp.dot(ws_ref[h], vn_h, preferred_element_type=F32) + bs_ref[h]
            ds_h = d_sgu[:, _head(h)]
            dug_ref[:, _head(h)] = ds_h * f
            df = ds_h * ug_ref[:, _head(h)]
            dfb = df.astype(BF16)
            dvn_ref[:, _head(h)] = lax.dot_general(ws_ref[h], dfb, tn_dims, preferred_element_type=F32)
            dws_ref[h] += lax.dot_general(dfb, vn_h, nt_dims, preferred_element_type=F32)
            dbs_ref[h] += jnp.broadcast_to(jnp.sum(df, axis=-1, keepdims=True), (BLOCK, BLOCK))

        row0 = pl.multiple_of(n * BLOCK, BLOCK)
        for kh in range(N_KV_HEADS):
            kb = jnp.concatenate([kp_ref[:, _head(kh)], kc_ref[:, _head(kh)], kx_ref[:, _head(kh)]], axis=0)
            vbd = jnp.concatenate([vp_ref[:, _head(kh)], vc_ref[:, _head(kh)], vx_ref[:, _head(kh)]], axis=0)
            q4 = _group_rows(q_ref, kh)
            pb = probs_ref[kh]
            p = pb.astype(F32)
            do4 = _group_rows(d_attn, kh).astype(BF16)
            dp = lax.dot_general(do4, vbd, nt_dims, preferred_element_type=F32)
            delta = jnp.sum(p * dp, axis=-1, keepdims=True)
            dsc = (p * (dp - delta) * (HEAD_DIM ** -0.5)).astype(BF16)
            d_sink = -(psink_ref[kh] * delta)
            dq4 = jnp.dot(dsc, kb, preferred_element_type=F32)
            for g in range(GQA_GROUP):
                h = kh * GQA_GROUP + g
                dsk_ref[h:h + 1, :] += jnp.broadcast_to(_sum_all(_rows_of(d_sink, g)), (1, BLOCK))
                dq_ref[:, _head(h)] = _rows_of(dq4, g)
            dk_acc[pl.ds(row0, 3 * BLOCK), _head(kh)] += lax.dot_general(dsc, q4, tn_dims, preferred_element_type=F32)
            dv_acc[pl.ds(row0, 3 * BLOCK), _head(kh)] += lax.dot_general(pb, do4, tn_dims, preferred_element_type=F32)

        @pl.when(n == nb - 1)
        def _():
            dk_ref[...] = dk_acc[BLOCK:BLOCK + s, :]
            dv_ref[...] = dv_acc[BLOCK:BLOCK + s, :]

    hh = (N_GMLP_HEADS, BLOCK, BLOCK)
    full_kv = pl.BlockSpec((s, KV_WIDTH), lambda n: (0, 0))
    return _ordered_call(
        body, name=name,
        out_shape=(jax.ShapeDtypeStruct((s, ATTN_WIDTH), F32), jax.ShapeDtypeStruct((s, KV_WIDTH), F32),
                   jax.ShapeDtypeStruct((s, KV_WIDTH), F32), jax.ShapeDtypeStruct((s, GMLP_WIDTH), F32),
                   jax.ShapeDtypeStruct((s, GMLP_WIDTH), F32), jax.ShapeDtypeStruct(hh, F32), jax.ShapeDtypeStruct(hh, F32),
                   jax.ShapeDtypeStruct((N_Q_HEADS, BLOCK), F32), jax.ShapeDtypeStruct((1, ATTN_WIDTH), F32),
                   jax.ShapeDtypeStruct((1, GMLP_WIDTH), F32)),
        grid=(nb,),
        in_specs=[_blk(ATTN_WIDTH)] + _band_specs(KV_WIDTH, nb) + _band_specs(KV_WIDTH, nb)
        + [_blk(GMLP_WIDTH), _blk(GMLP_WIDTH), _blk(ATTN_WIDTH), _blk(GMLP_WIDTH), _blk(D_MODEL), _whole3(hh), _whole3(hh),
           pl.BlockSpec((1, ATTN_WIDTH), lambda n: (0, 0)), pl.BlockSpec((1, GMLP_WIDTH), lambda n: (0, 0)),
           _per_block(PROBS_BLOCK), _per_block(PSINK_BLOCK)],
        out_specs=(_blk(ATTN_WIDTH), full_kv, full_kv, _blk(GMLP_WIDTH), _blk(GMLP_WIDTH), _whole3(hh), _whole3(hh),
                   pl.BlockSpec((N_Q_HEADS, BLOCK), lambda n: (0, 0)), pl.BlockSpec((1, ATTN_WIDTH), lambda n: (0, 0)),
                   pl.BlockSpec((1, GMLP_WIDTH), lambda n: (0, 0))),
        scratch_shapes=[pltpu.VMEM((s + 2 * BLOCK, KV_WIDTH), F32), pltpu.VMEM((s + 2 * BLOCK, KV_WIDTH), F32)],
        compiler_params=_params(("arbitrary",)),
    )(qn, kn, kn, kn, vb, vb, vb, ug, vn, attn, sgu, dmixed, wsb, bsb, ga, gs, probs, psink)


CONV_TILE = 128


PAD_ROWS = 8


def _zero_pad_rows(pad_ref):
    s = pad_ref.shape[0] - 2 * PAD_ROWS
    zeros = jnp.zeros((PAD_ROWS, pad_ref.shape[1]), F32)
    pad_ref[0:PAD_ROWS, :] = zeros
    pad_ref[PAD_ROWS + s:2 * PAD_ROWS + s, :] = zeros


def _shift_rows(a, pad_ref):
    s = a.shape[0]
    pad_ref[PAD_ROWS:PAD_ROWS + s, :] = a
    padded = pad_ref[...]
    prev = pltpu.roll(padded, 1, 0)[PAD_ROWS:PAD_ROWS + s]
    nxt = pltpu.roll(padded, s + 2 * PAD_ROWS - 1, 0)[PAD_ROWS:PAD_ROWS + s]
    return prev, nxt


def _conv_specs(s):
    tc = CONV_TILE
    nj = D_FF // tc
    return (tc, nj, pl.BlockSpec((2, s, tc), lambda j: (0, 0, j)),
            [pl.BlockSpec((3, tc), lambda j: (0, j)), pl.BlockSpec((3, tc), lambda j: (0, j + nj))],
            [pl.BlockSpec((1, tc), lambda j: (0, j)), pl.BlockSpec((1, tc), lambda j: (0, j + nj))])


def _conv_gate_fwd(a_pre, cw, cb, name):
    s = a_pre.shape[1]
    tc, nj, a_spec, w_specs, b_specs = _conv_specs(s)

    def body(a_ref, wg_ref, wu_ref, bg_ref, bu_ref, act_ref, dgu_ref, pad_ref):
        _zero_pad_rows(pad_ref)

        def conv(a, w_ref, b_ref):
            prev, nxt = _shift_rows(a, pad_ref)
            return b_ref[...] + prev * w_ref[0:1, :] + a * w_ref[1:2, :] + nxt * w_ref[2:3, :]

        g = conv(a_ref[0].astype(F32), wg_ref, bg_ref)
        u = conv(a_ref[1].astype(F32), wu_ref, bu_ref)
        sg = 1.0 / (1.0 + jnp.exp(-g))
        silu = g * sg
        act_ref[...] = (silu * u).astype(BF16)
        dgu_ref[0] = (u * (sg * (1.0 + g * (1.0 - sg)))).astype(BF16)
        dgu_ref[1] = silu.astype(BF16)

    return _ordered_call(
        body, name=name, out_shape=(jax.ShapeDtypeStruct((s, D_FF), BF16), jax.ShapeDtypeStruct((2, s, D_FF), BF16)),
        grid=(nj,), in_specs=[a_spec] + w_specs + b_specs,
        out_specs=(pl.BlockSpec((s, tc), lambda j: (0, j)), pl.BlockSpec((2, s, tc), lambda j: (0, 0, j))),
        scratch_shapes=[pltpu.VMEM((s + 2 * PAD_ROWS, tc), F32)], compiler_params=_params(("parallel",)),
    )(a_pre, cw, cw, cb, cb)


def _conv_gate_bwd(a_pre, dgu, cw, dact, name):
    s = a_pre.shape[1]
    tc, nj, a_spec, w_specs, _ = _conv_specs(s)

    def body(a_ref, dgu_ref, wg_ref, wu_ref, dact_ref, dap_ref, dcw_ref, dcb_ref, pad_ref):
        _zero_pad_rows(pad_ref)
        dact_v = dact_ref[...].astype(F32)
        for part, w_ref in enumerate((wg_ref, wu_ref)):
            da = dact_v * dgu_ref[part].astype(F32)
            a = a_ref[part].astype(F32)
            da_prev, da_next = _shift_rows(da, pad_ref)
            dcw_ref[part, 0:1, :] = _sum_rows(a * da_next)
            dcw_ref[part, 1:2, :] = _sum_rows(a * da)
            dcw_ref[part, 2:3, :] = _sum_rows(a * da_prev)
            dcb_ref[part] = _sum_rows(da)
            dap_ref[part] = (da_next * w_ref[0:1, :] + da * w_ref[1:2, :] + da_prev * w_ref[2:3, :]).astype(BF16)

    return _ordered_call(
        body, name=name,
        out_shape=(jax.ShapeDtypeStruct((2, s, D_FF), BF16), jax.ShapeDtypeStruct((2, 3, D_FF), F32),
                   jax.ShapeDtypeStruct((2, 1, D_FF), F32)),
        grid=(nj,),
        in_specs=[a_spec, pl.BlockSpec((2, s, tc), lambda j: (0, 0, j))] + w_specs + [pl.BlockSpec((s, tc), lambda j: (0, j))],
        out_specs=(pl.BlockSpec((2, s, tc), lambda j: (0, 0, j)), pl.BlockSpec((2, 3, tc), lambda j: (0, 0, j)),
                   pl.BlockSpec((2, 1, tc), lambda j: (0, 0, j))),
        scratch_shapes=[pltpu.VMEM((s + 2 * PAD_ROWS, tc), F32)], compiler_params=_params(("parallel",)),
    )(a_pre, dgu, cw, cw, dact)


def _loss_head(y, target, name):
    s, d = y.shape
    tr = _row_tile(s)

    def body(y_ref, t_ref, loss_ref, dy_ref, dyb_ref):
        err = y_ref[...] - t_ref[...]

        @pl.when(pl.program_id(0) == 0)
        def _():
            loss_ref[...] = jnp.zeros_like(loss_ref)

        loss_ref[...] += jnp.broadcast_to(0.5 * _sum_all(_mean_last(err * err)), (8, 128))
        dy = err * (1.0 / d)
        dy_ref[...] = dy
        dyb_ref[...] = dy.astype(BF16)

    return _ordered_call(
        body, name=name,
        out_shape=(jax.ShapeDtypeStruct((8, 128), F32), jax.ShapeDtypeStruct((s, d), F32), jax.ShapeDtypeStruct((s, d), BF16)),
        grid=(s // tr,), in_specs=[_rows(d, tr), _rows(d, tr)],
        out_specs=(_const2((8, 128)), _rows(d, tr), _rows(d, tr)), compiler_params=_params(("arbitrary",)),
    )(y, target)


def _row_block(rows, cols, budget=1 << 20):
    if rows * cols <= budget:
        return rows
    best = None
    for tr in range(16, rows, 16):
        if rows % tr == 0 and tr * cols <= budget:
            best = tr
    assert best is not None, (rows, cols)
    return best


def _place_shard(x4, layer, j_arr, out_dtype, name):
    _, nh, r, cols = x4.shape
    tr = _row_block(r, cols)

    def body(j_ref, x_ref, o_ref):
        o_ref[...] = x_ref[...].astype(out_dtype)

    grid_spec = pltpu.PrefetchScalarGridSpec(
        num_scalar_prefetch=1, grid=(nh, r // tr),
        in_specs=[pl.BlockSpec((None, None, tr, cols), lambda h, i, j_ref: (layer, h, i, 0))],
        out_specs=pl.BlockSpec((None, None, tr, cols), lambda h, i, j_ref: (j_ref[0], h, i, 0)))
    return _ordered_call(
        body, name=name, out_shape=jax.ShapeDtypeStruct((N_CHIPS, nh, r, cols), out_dtype), grid_spec=grid_spec,
        compiler_params=_params(("parallel", "parallel")),
    )(j_arr, x4)


def _adamw(w, g, m, v, name):
    rows, cols = w.shape
    tr = _row_block(rows, cols, 1 << 18)

    def body(w_ref, g_ref, m_ref, v_ref, go_ref, d_ref, nm_ref, nv_ref):
        gv = g_ref[...]
        go_ref[...] = gv
        mn = ADAM_B1 * m_ref[...] + (1.0 - ADAM_B1) * gv
        vn = ADAM_B2 * v_ref[...] + (1.0 - ADAM_B2) * (gv * gv)
        m_hat = mn / (1.0 - ADAM_B1 ** ADAM_STEP)
        v_hat = vn / (1.0 - ADAM_B2 ** ADAM_STEP)
        d_ref[...] = -ADAM_LR * (m_hat / (jnp.sqrt(v_hat) + ADAM_EPS) + ADAM_WD * w_ref[...])
        nm_ref[...] = mn
        nv_ref[...] = vn

    sds = jax.ShapeDtypeStruct((rows, cols), F32)
    return _ordered_call(
        body, name=name, out_shape=(sds, sds, sds, sds), grid=(rows // tr,),
        in_specs=[_rows(cols, tr)] * 4, out_specs=(_rows(cols, tr),) * 4, compiler_params=_params(("parallel",)),
    )(w, g, m, v)


def _chip_sum(p4, recv3, j_arr, c_arr, name):
    _, rh, cols = p4.shape
    tr = _row_block(rh, cols, 1 << 19)

    def body(j_ref, c_ref, p_ref, r_ref, o_ref):
        total = p_ref[...].astype(F32)
        for peer in range(3):
            total = total + r_ref[peer].astype(F32)
        o_ref[...] = total

    grid_spec = pltpu.PrefetchScalarGridSpec(
        num_scalar_prefetch=2, grid=(rh // tr,),
        in_specs=[pl.BlockSpec((None, tr, cols), lambda i, j_ref, c_ref: (j_ref[0], i, 0)),
                  pl.BlockSpec((3, tr, cols), lambda i, j_ref, c_ref: (0, i, 0))],
        out_specs=pl.BlockSpec((None, tr, cols), lambda i, j_ref, c_ref: (c_ref[0], i, 0)))
    return _ordered_call(
        body, name=name, out_shape=jax.ShapeDtypeStruct((2, rh, cols), F32), grid_spec=grid_spec,
        compiler_params=_params(("parallel",)),
    )(j_arr, c_arr, p4, recv3)


def _adamw_layer(w, g, m, v, layer, into, name):
    nl, rows, cols = w.shape
    slabs, _, width = g.shape
    assert slabs * width == cols and g.shape[1] == rows, (name, w.shape, g.shape)
    tr = _row_block(rows, width, 1 << 18)
    at_layer = pl.BlockSpec((None, tr, width), lambda h, i: (layer, i, h))

    def body(w_ref, g_ref, m_ref, v_ref, *rest):
        go_ref, d_ref, nm_ref, nv_ref = rest[-4:]
        gv = g_ref[...]
        go_ref[...] = gv
        mn = ADAM_B1 * m_ref[...] + (1.0 - ADAM_B1) * gv
        vn = ADAM_B2 * v_ref[...] + (1.0 - ADAM_B2) * (gv * gv)
        m_hat = mn / (1.0 - ADAM_B1 ** ADAM_STEP)
        v_hat = vn / (1.0 - ADAM_B2 ** ADAM_STEP)
        d_ref[...] = -ADAM_LR * (m_hat / (jnp.sqrt(v_hat) + ADAM_EPS) + ADAM_WD * w_ref[...])
        nm_ref[...] = mn
        nv_ref[...] = vn

    in_specs = [at_layer, pl.BlockSpec((None, tr, width), lambda h, i: (h, i, 0)), at_layer, at_layer]
    operands = [w, g, m, v]
    aliases = {}
    if into is not None:
        in_specs += [ANY] * 4
        operands += list(into)
        aliases = {4 + i: i for i in range(4)}
    sds = jax.ShapeDtypeStruct((nl, rows, cols), F32)
    return _ordered_call(
        body, name=name, out_shape=(sds,) * 4, grid=(slabs, rows // tr), in_specs=in_specs, out_specs=(at_layer,) * 4,
        input_output_aliases=aliases, compiler_params=_params(("parallel", "parallel")),
    )(*operands)


def _sum_devices(mine, landed, me_arr, name):
    rows, lanes = mine.shape

    def body(me_ref, mine_ref, landed_ref, o_ref):
        total = None
        for dev in range(8):
            part = jnp.where(me_ref[0] == dev, mine_ref[...], landed_ref[dev])
            total = part if total is None else total + part
        o_ref[...] = total

    grid_spec = pltpu.PrefetchScalarGridSpec(
        num_scalar_prefetch=1, grid=(1,),
        in_specs=[pl.BlockSpec((rows, lanes), lambda i, me_ref: (0, 0)), pl.BlockSpec((8, rows, lanes), lambda i, me_ref: (0, 0, 0))],
        out_specs=pl.BlockSpec((rows, lanes), lambda i, me_ref: (0, 0)))
    return _ordered_call(
        body, name=name, out_shape=jax.ShapeDtypeStruct((rows, lanes), F32), grid_spec=grid_spec,
        compiler_params=_params(("arbitrary",)),
    )(me_arr, mine, landed)


def _place():
    x, y, c = lax.axis_index("x"), lax.axis_index("y"), lax.axis_index("c")
    chips = [(1 - x, y), (x, 1 - y), (1 - x, 1 - y)]
    return x, y, c, chips


HBM = pl.BlockSpec(memory_space=pltpu.HBM)
SEM = pl.BlockSpec(memory_space=pltpu.SEMAPHORE)
TOKEN = jax.ShapeDtypeStruct((8, 128), F32)


def _remote(src, dst, send_sem, recv_sem, to):
    return pltpu.make_async_remote_copy(src_ref=src, dst_ref=dst, send_sem=send_sem, recv_sem=recv_sem, device_id=to,
                                        device_id_type=MESH)


def _split_call(body, name, thru, sems_in=(), fresh=(), new_sems=(), after_last=True):
    n_t, n_s, n_f = len(thru), len(sems_in), len(fresh)

    def call_body(*refs):
        outs = refs[n_t + n_s:]
        body(refs[:n_t], refs[n_t:n_t + n_s], outs[1 + n_t:1 + n_t + n_f], outs[1 + n_t + n_f:])
        outs[0][...] = jnp.zeros_like(outs[0])

    out_shape = ([TOKEN] + [pltpu.HBM(t.shape, t.dtype) for t in thru] + [pltpu.HBM(shp, dt) for shp, dt in fresh]
                 + [pltpu.SemaphoreType.DMA(shp) for shp in new_sems])
    out_specs = [pl.BlockSpec(memory_space=pltpu.VMEM)] + [HBM] * (n_t + n_f) + [SEM] * len(new_sems)
    if not after_last:
        _Order.last = None
    out = _ordered_call(
        call_body, name=name, out_shape=tuple(out_shape), in_specs=[HBM] * n_t + [SEM] * n_s, out_specs=tuple(out_specs),
        input_output_aliases={i: 1 + i for i in range(n_t)},
        compiler_params=pltpu.CompilerParams(has_side_effects=pltpu.SideEffectType.DATAFLOW_SIDE_EFFECTING),
    )(*[pltpu.with_memory_space_constraint(t, pltpu.HBM) for t in thru], *sems_in)
    return out[1:1 + n_t], out[1 + n_t:1 + n_t + n_f], out[1 + n_t + n_f:]


class _Exchange:
    def __init__(self, weights, m_in, v_in, j_arr, c_arr, me_arr):
        self.w, self.m, self.v = weights, m_in, v_in
        self.j_arr, self.c_arr, self.me_arr = j_arr, c_arr, me_arr
        self.adam, self.small, self.pairs, self.held = {}, {}, {}, None
        self.o_arr = 1 - c_arr
        self.groups = [(l, name) for l in range(DEPTH) for name in BIG_NAMES]
        self.shard_shape = {name: weights[name].shape[1:] for name in BIG_NAMES}
        self.conv_state, self.state = [], {}
        self.ready, self.conv_ready = {}, {}
        self.pending, self.tick, self.reduced = [], 0, {}

        def place(grp):
            l, name = grp
            nl, r, cols = weights[name].shape
            return _place_shard(weights[name].reshape(nl, 2, r // 2, cols), l, j_arr, BF16, f"place_{name}_l{l}")

        def start_copies(tag, convs, groups, bufs):
            n_c = len(convs)

            def start(thru, _, __, sems):
                x, y, c, chips = _place()
                j_me = 2 * x + y
                copies = []
                for i in range(len(thru)):
                    mine = thru[i].at[j_me] if i < n_c else thru[i].at[j_me, c]
                    copies += [_remote(mine, mine, sems[2 * i].at[k], sems[2 * i + 1].at[k], (*chip, c))
                               for k, chip in enumerate(chips)]
                for cp in copies:
                    cp.start()

            thru, _, sems = _split_call(start, tag, convs + bufs, new_sems=[(3,)] * (2 * (n_c + len(bufs))))
            self.conv_state += [(thru[i], sems[2 * i], sems[2 * i + 1]) for i in range(n_c)]
            for g, grp in enumerate(groups):
                self.state[grp] = (thru[n_c + g], sems[2 * (n_c + g)], sems[2 * (n_c + g) + 1])

        convs = [_place_shard(weights["conv_w"][:, None], l, j_arr, F32, f"place_conv_w_l{l}") for l in range(DEPTH)]
        start_copies("gather_start_first", convs, self.groups[:1], [place(self.groups[0])])
        start_copies("gather_start_rest", [], self.groups[1:], [place(grp) for grp in self.groups[1:]])

    def conv_w(self, l):
        if l not in self.conv_ready:
            buf, send, recv = self.conv_state[l]

            def wait(thru, sems, _, __):
                x, y, c, chips = _place()
                for k, chip in enumerate(chips):
                    mine, theirs = thru[0].at[2 * x + y], thru[0].at[2 * chip[0] + chip[1]]
                    _remote(mine, mine, sems[0].at[k], sems[1].at[k], (*chip, c)).wait_send()
                    _remote(theirs, theirs, sems[0].at[k], sems[1].at[k], (x, y, c)).wait_recv()

            (buf,), _, _ = _split_call(wait, f"gather_conv_w_l{l}", [buf], sems_in=[send, recv])
            self.conv_ready[l] = jnp.transpose(buf[:, 0], (1, 0, 2)).reshape(3, 2 * D_FF)
        return self.conv_ready[l]

    def weight(self, l, name):
        grp = (l, name)
        if grp not in self.ready:
            buf, send, recv = self.state[grp]

            def forward(thru, sems, _, new):
                x, y, c, chips = _place()
                for k, chip in enumerate(chips):
                    landed = thru[0].at[2 * chip[0] + chip[1], c]
                    _remote(landed, landed, new[0].at[k], sems[0].at[k], (x, y, c)).wait_recv()
                    _remote(landed, landed, new[0].at[k], new[1].at[k], (x, y, 1 - c)).start()

            (buf,), _, (fsend, frecv) = _split_call(forward, f"gather_pass_{name}_l{l}", [buf], sems_in=[recv],
                                                    new_sems=[(3,), (3,)])

            def finish(thru, sems, _, __):
                x, y, c, chips = _place()
                mine = thru[0].at[2 * x + y, c]
                for k, chip in enumerate(chips):
                    j_k = 2 * chip[0] + chip[1]
                    theirs, landed = thru[0].at[j_k, 1 - c], thru[0].at[j_k, c]
                    _remote(theirs, theirs, sems[1].at[k], sems[2].at[k], (x, y, c)).wait_recv()
                    _remote(landed, landed, sems[1].at[k], sems[2].at[k], (x, y, 1 - c)).wait_send()
                    _remote(mine, mine, sems[0].at[k], sems[2].at[k], (*chip, c)).wait_send()

            (buf,), _, _ = _split_call(finish, f"gather_done_{name}_l{l}", [buf], sems_in=[send, fsend, frecv])
            r, cols = self.shard_shape[name]
            self.ready[grp] = buf.reshape(N_CHIPS, r, cols) if name in ("w_in", "w_up") else buf.reshape(N_CHIPS * r, cols)
        return self.ready[grp]

    def pair_send(self, l, name, other):
        held = self.held
        self.held = None

        def start(thru, _, fresh, sems):
            x, y, c, chips = _place()
            copies = [_remote(thru[0], fresh[0], sems[0], sems[1], (x, y, 1 - c))]
            if held is not None:
                copies += [_remote(thru[1].at[2 * chip[0] + chip[1]], fresh[1].at[k], sems[2].at[k], sems[3].at[k], (*chip, c))
                           for k, chip in enumerate(chips)]
            for cp in copies:
                cp.start()

        thru, fresh, new_sems = [other], [(other.shape, BF16)], [(), ()]
        if held is not None:
            thru, fresh, new_sems = thru + [held[2]], fresh + [((3,) + held[2].shape[1:], BF16)], new_sems + [(3,), (3,)]
        thru, fresh, sems = _split_call(start, f"pair_start_{name}_l{l}", thru, fresh=fresh, new_sems=new_sems, after_last=False)
        self.pairs[(l, name)] = (thru[0], fresh[0], sems[:2])
        if held is not None:
            self.pending.append(dict(l=held[0], name=held[1], stage=2, at=self.tick, bufs=(thru[1], fresh[1]), sems=sems[2:]))

    def pair_recv(self, l, name):
        other, recv, sems = self.pairs.pop((l, name))

        def wait(thru, sems, _, __):
            x, y, c, _chips = _place()
            cp = _remote(thru[0], thru[1], sems[0], sems[1], (x, y, 1 - c))
            cp.wait_send()
            cp.wait_recv()

        (_, recv), _, _ = _split_call(wait, f"pair_done_{name}_l{l}", [other, recv], sems_in=list(sems))
        return recv

    def scatter(self, l, name, p4):
        assert self.held is None
        self.held = (l, name, p4)
        if (l, name) == (0, BIG_NAMES[0]):
            self._scatter_held()

    def _scatter_held(self):
        l, name, p4 = self.held
        self.held = None

        def start(thru, _, fresh, sems):
            x, y, c, chips = _place()
            for k, chip in enumerate(chips):
                _remote(thru[0].at[2 * chip[0] + chip[1]], fresh[0].at[k], sems[0].at[k], sems[1].at[k], (*chip, c)).start()

        (p4,), (recv3,), sems = _split_call(start, f"chips_start_{name}_l{l}", [p4], fresh=[((3,) + p4.shape[1:], BF16)],
                                           new_sems=[(3,), (3,)], after_last=False)
        self.pending.append(dict(l=l, name=name, stage=2, at=self.tick, bufs=(p4, recv3), sems=sems))

    def point(self, drain=False):
        self.tick += 1
        if drain:
            for grp in [g for g in self.pending if g["stage"] in (2, 3)]:
                self._advance([grp] if grp["stage"] == 3 else [], [grp] if grp["stage"] == 2 else [])
        else:
            self._advance([grp for grp in self.pending if grp["stage"] == 3 and grp["at"] < self.tick],
                          [grp for grp in self.pending if grp["stage"] == 2 and grp["at"] + 2 <= self.tick])

    def _advance(self, joined, landed):
        if not joined and not landed:
            return
        n_j, n_l = len(joined), len(landed)

        def wait(thru, sems, _, __):
            x, y, c, chips = _place()
            for i in range(n_j):
                buf, send, recv = thru[i], sems[2 * i], sems[2 * i + 1]
                _remote(buf.at[c], buf.at[c], send, recv, (x, y, 1 - c)).wait_send()
                _remote(buf.at[1 - c], buf.at[1 - c], send, recv, (x, y, c)).wait_recv()
            for i in range(n_l):
                p4, recv3 = thru[n_j + 2 * i], thru[n_j + 2 * i + 1]
                send, recv = sems[2 * (n_j + i)], sems[2 * (n_j + i) + 1]
                for k, chip in enumerate(chips):
                    cp = _remote(p4.at[2 * chip[0] + chip[1]], recv3.at[k], send.at[k], recv.at[k], (*chip, c))
                    cp.wait_send()
                    cp.wait_recv()

        tag = "_".join([f"{grp['name']}{grp['l']}_halves" for grp in joined] + [f"{grp['name']}{grp['l']}_chips" for grp in landed])
        bufs, _, _ = _split_call(wait, f"landed_{tag}", [b for grp in joined + landed for b in grp["bufs"]],
                                 sems_in=[sm for grp in joined + landed for sm in grp["sems"]])
        for i, grp in enumerate(joined):
            l, name, full = grp["l"], grp["name"], bufs[i]
            if GRAD_HALVES[name][0] != "cols_of_block":
                full = full.reshape((1,) + tuple(self.shard_shape[name]))
            self.adam[name] = _adamw_layer(self.w[name], full, self.m[name], self.v[name], l, self.adam.get(name),
                                           f"adamw_{name}_l{l}")
            grp.update(stage=4)
        if not landed:
            return
        halves = [_chip_sum(bufs[n_j + 2 * i], bufs[n_j + 2 * i + 1], self.j_arr, self.c_arr,
                            f"chip_sum_{grp['name']}_l{grp['l']}") for i, grp in enumerate(landed)]

        def start(thru, _, __, sems):
            x, y, c, _chips = _place()
            for i in range(n_l):
                _remote(thru[i].at[c], thru[i].at[c], sems[2 * i], sems[2 * i + 1], (x, y, 1 - c)).start()

        tag = "_".join(f"{grp['name']}{grp['l']}" for grp in landed)
        halves, _, sems = _split_call(start, f"join_start_{tag}", halves, new_sems=[()] * (2 * n_l), after_last=False)
        for i, grp in enumerate(landed):
            grp.update(stage=3, at=self.tick, bufs=(halves[i],), sems=tuple(sems[2 * i:2 * i + 2]))

    def finish(self):
        if self.held is not None:
            self._scatter_held()
        while any(grp["stage"] < 4 for grp in self.pending):
            self.point(drain=True)
        return self.adam

    @staticmethod
    def _peer(k, x, y, c):
        return (1 - x if k & 4 else x, 1 - y if k & 2 else y, 1 - c if k & 1 else c)

    def small_grads(self, l, grads, loss_tile):
        parts = [grads[nm] for nm in SMALL_NAMES] + ([loss_tile[0, 0:1]] if loss_tile is not None else [])
        packed = _pack_call(parts, f"small_pack_l{l}")
        rows = packed.shape[0]

        def start(thru, _, fresh, sems):
            x, y, c, _chips = _place()
            for k in range(1, 8):
                _remote(thru[0], fresh[0].at[4 * x + 2 * y + c], sems[0].at[k - 1], sems[1].at[k - 1],
                        self._peer(k, x, y, c)).start()

        (packed,), (landed,), sems = _split_call(start, f"small_start_l{l}", [packed], fresh=[((8, rows, PACK_LANES), F32)],
                                                 new_sems=[(7,), (7,)], after_last=False)
        self.small[l] =(packed, landed, sems, [p.shape for p in parts])

    def small_sum(self, l):
        packed, landed, sems, _shapes = self.small[l]

        def wait(thru, sems, _, __):
            x, y, c, _chips = _place()
            for k in range(1, 8):
                px, py, pc = self._peer(k, x, y, c)
                _remote(thru[0], thru[1].at[4 * x + 2 * y + c], sems[0].at[k - 1], sems[1].at[k - 1], (px, py, pc)).wait_send()
                _remote(thru[0], thru[1].at[4 * px + 2 * py + pc], sems[0].at[k - 1], sems[1].at[k - 1], (x, y, c)).wait_recv()

        (packed, landed), _, _ = _split_call(wait, f"small_done_l{l}", [packed, landed], sems_in=list(sems))
        return _sum_devices(packed, landed, self.me_arr, f"small_sum_l{l}")


def _rope_tables(s):
    inv_freq = ROPE_THETA ** (-jnp.arange(0, HEAD_DIM, 2, dtype=F32) / HEAD_DIM)
    ang = jnp.arange(s, dtype=F32)[:, None] * inv_freq[None, :]
    cos, sin = jnp.cos(ang), jnp.sin(ang)
    return jnp.concatenate([cos, cos], axis=-1), jnp.concatenate([-sin, sin], axis=-1)


def _local_step(x, target, ex, small):
    s = x.shape[0]
    cosf, sinf = _rope_tables(s)
    saved = []
    for l in range(DEPTH):
        p = small[l]
        t = f"l{l}"
        h = _rms_fwd(x, p["norm1_g"], f"norm1_{t}")
        z = _matmul(h, ex.weight(l, "w_in"), mode="nn", out_dtype=BF16, tm=1024, tn=896, tk=2048, b_parts=4, name=f"proj_in_{t}")
        qn, kn, vb, ug, vn, *gate_kept = _proj_post(z, p["q_norm_g"], p["k_norm_g"], p["sgu_ln_g"], p["sgu_ln_b"], cosf, sinf,
                                                    f"proj_post_{t}")
        attn, sgu, mixed, probs, psink = _mixer_fwd(qn, kn, vb, ug, vn, p["w_s_bf16"], p["b_s_tile"], p["sink"],
                                                    p["attn_out_g"], p["sgu_out_g"], f"mixer_{t}")
        x1 = _matmul(mixed, ex.weight(l, "w_o"), mode="nn", out_dtype=F32, tm=2048, tn=256, tk=2048, res=x,
                     name=f"proj_out_{t}")
        h2 = _rms_fwd(x1, p["norm2_g"], f"norm2_{t}")
        a_pre = _matmul(h2, ex.weight(l, "w_up"), mode="nn", out_dtype=BF16, tm=1024, tn=1408, tk=2048, b_parts=4,
                        out_parts=2,
                        name=f"ffn_up_{t}")
        act, dgu = _conv_gate_fwd(a_pre, ex.conv_w(l), p["conv_b"], f"conv_gate_{t}")
        x2 = _matmul(act, ex.weight(l, "w_down"), mode="nn", out_dtype=F32, tm=1024, tn=256, tk=D_FF, res=x1,
                     name=f"ffn_down_{t}")
        saved.append(dict(x=x, h=h, z=z, qn=qn, kn=kn, vb=vb, ug=ug, vn=vn, attn=attn, sgu=sgu, mixed=mixed, x1=x1, h2=h2,
                          a_pre=a_pre, act=act, dgu=dgu, probs=probs, psink=psink, gate_kept=gate_kept))
        x = x2
    loss_tile, dx, dxb = _loss_head(x, target, "loss_head")
    for l in reversed(range(DEPTH)):
        p, sv = small[l], saved[l]
        t = f"l{l}"
        def weight_grad(name, a, g, between, g_parts=0):
            ex.pair_send(l, name, _grad_half(name, a, g, ex.o_arr, None, f"g_{name}_other_{t}", g_parts))
            out = between()
            ex.scatter(l, name, _grad_half(name, a, g, ex.c_arr, ex.pair_recv(l, name), f"g_{name}_own_{t}", g_parts))
            ex.point()
            return out

        def after_down():
            dact = _matmul(dxb, ex.weight(l, "w_down"), mode="nt", out_dtype=BF16, tm=1024, tn=512, tk=2048,
                           name=f"d_act_{t}")
            return _conv_gate_bwd(sv["a_pre"], sv["dgu"], ex.conv_w(l), dact, f"conv_gate_bwd_{t}")

        dap, dcw, dcb = weight_grad("w_down", sv["act"], dxb, after_down)

        def after_up():
            dh2 = _matmul(dap, ex.weight(l, "w_up"), mode="nt", out_dtype=F32, tm=1024, tn=1024, tk=2816, a_parts=2,
                          b_parts=4, name=f"d_h2_{t}")
            return _rms_bwd(sv["x1"], p["norm2_g"], dh2, dx, f"norm2_bwd_{t}")

        dx1, dx1b, dg2 = weight_grad("w_up", sv["h2"], dap, after_up, g_parts=2)
        ex.pair_send(l, "w_o", _grad_half("w_o", sv["mixed"], dx1b, ex.o_arr, None, f"g_w_o_other_{t}"))
        dmixed = _matmul(dx1b, ex.weight(l, "w_o"), mode="nt", out_dtype=F32, tm=1024, tn=512, tk=2048,
                         name=f"d_mixed_{t}")
        dqn, dkn, dvb, dug, dvn, dws, dbs, dsk, dga, dgs = _mixer_bwd(
            sv["qn"], sv["kn"], sv["vb"], sv["ug"], sv["vn"], sv["attn"], sv["sgu"], dmixed, p["w_s_bf16"], p["b_s_tile"],
            p["attn_out_g"], p["sgu_out_g"], sv["probs"], sv["psink"], f"mixer_bwd_{t}")
        dz, dqg, dkg, dlg, dlb = _proj_post_bwd(sv["z"], dqn, dkn, dvb, dug, dvn, *sv["gate_kept"], p["q_norm_g"], p["k_norm_g"],
                                                 p["sgu_ln_g"], cosf, sinf, f"proj_post_bwd_{t}")
        ex.scatter(l, "w_o", _grad_half("w_o", sv["mixed"], dx1b, ex.c_arr, ex.pair_recv(l, "w_o"), f"g_w_o_own_{t}"))
        ex.point()

        def after_in():
            dh = _matmul_nt_slabs(dz, ex.weight(l, "w_in"), tm=1024, tn=512, name=f"d_h_{t}")
            return _rms_bwd(sv["x"], p["norm1_g"], dh, dx1, f"norm1_bwd_{t}")

        dx, dxb, dg1 = weight_grad("w_in", sv["h"], dz, after_in)
        ex.small_grads(l, dict(
            norm1_g=dg1[0], q_norm_g=dqg[0], k_norm_g=dkg[0], sink=dsk[:, 0], sgu_ln_g=dlg[0], sgu_ln_b=dlb[0], w_s=dws,
            b_s=dbs[:, :, 0], attn_out_g=dga[0], sgu_out_g=dgs[0], norm2_g=dg2[0],
            conv_w=jnp.concatenate([dcw[0], dcw[1]], axis=-1), conv_b=jnp.concatenate([dcb[0, 0], dcb[1, 0]], axis=-1)),
            loss_tile if l == 0 else None)
    return dx


def _small_views(l, norm1_g, q_norm_g, k_norm_g, sink, sgu_ln_g, sgu_ln_b, w_s, b_s, attn_out_g, sgu_out_g, norm2_g, conv_b):
    return dict(
        norm1_g=norm1_g[l][None], q_norm_g=q_norm_g[l][None], k_norm_g=k_norm_g[l][None], sink=sink[l],
        sgu_ln_g=sgu_ln_g[l][None], sgu_ln_b=sgu_ln_b[l][None], w_s_bf16=w_s[l].astype(BF16),
        b_s_tile=jnp.broadcast_to(b_s[l][:, :, None], (N_GMLP_HEADS, BLOCK, BLOCK)), attn_out_g=attn_out_g[l][None],
        sgu_out_g=sgu_out_g[l][None], norm2_g=norm2_g[l][None], conv_b=conv_b[l][None])


SMALL_NAMES = ("norm1_g", "q_norm_g", "k_norm_g", "sink", "sgu_ln_g", "sgu_ln_b", "w_s", "b_s", "attn_out_g", "sgu_out_g",
               "norm2_g", "conv_b", "conv_w")
REPLICATED_NAMES = SMALL_NAMES[:-1]
BIG_NAMES = ("w_in", "w_o", "w_up", "w_down")
PACK_LANES = 128
PACK_ALIGN = 8 * PACK_LANES


def _pack_rows(shape):
    return -(-math.prod(shape) // PACK_ALIGN) * 8


def _pack_parts(arrays):
    parts = []
    for a in arrays:
        flat = a.reshape(-1)
        parts.append(jnp.pad(flat, (0, _pack_rows(a.shape) * PACK_LANES - flat.shape[0])).reshape(-1, PACK_LANES))
    return parts


def _pack_call(arrays, name):
    parts = _pack_parts(arrays)
    total = sum(p.shape[0] for p in parts)

    def body(*refs):
        o_ref, at = refs[-1], 0
        for p_ref in refs[:-1]:
            o_ref[at:at + p_ref.shape[0], :] = p_ref[...]
            at += p_ref.shape[0]

    vm = pl.BlockSpec(memory_space=pltpu.VMEM)
    return _ordered_call(
        body, name=name, out_shape=jax.ShapeDtypeStruct((total, PACK_LANES), F32), in_specs=[vm] * len(parts), out_specs=vm,
        compiler_params=pltpu.CompilerParams(vmem_limit_bytes=V7X_VMEM_LIMIT),
    )(*parts)


def _unpack_layers(stacked, shapes):
    nl = stacked.shape[0]
    out, at = [], 0
    for shp in shapes:
        rows = _pack_rows(shp)
        out.append(stacked[:, at:at + rows].reshape(nl, -1)[:, :math.prod(shp)].reshape((nl,) + tuple(shp)))
        at += rows
    return out


def _adamw_packed(w, g, m, v, rows, layer, into, name):
    head = pl.BlockSpec((rows, PACK_LANES), lambda i: (0, 0))
    at_layer = pl.BlockSpec((None, rows, PACK_LANES), lambda i: (layer, 0, 0))

    def body(w_ref, g_ref, m_ref, v_ref, *rest):
        d_ref, nm_ref, nv_ref = rest[-3:]
        gv = g_ref[...]
        mn = ADAM_B1 * m_ref[...] + (1.0 - ADAM_B1) * gv
        vn = ADAM_B2 * v_ref[...] + (1.0 - ADAM_B2) * (gv * gv)
        m_hat = mn / (1.0 - ADAM_B1 ** ADAM_STEP)
        v_hat = vn / (1.0 - ADAM_B2 ** ADAM_STEP)
        d_ref[...] = -ADAM_LR * (m_hat / (jnp.sqrt(v_hat) + ADAM_EPS) + ADAM_WD * w_ref[...])
        nm_ref[...] = mn
        nv_ref[...] = vn

    in_specs = [head] * 4
    operands = [w, g, m, v]
    aliases = {}
    if into is not None:
        in_specs += [ANY] * 3
        operands += list(into)
        aliases = {4 + i: i for i in range(3)}
    sds = jax.ShapeDtypeStruct((DEPTH, rows, PACK_LANES), F32)
    return _ordered_call(
        body, name=name, out_shape=(sds,) * 3, grid=(1,), in_specs=in_specs, out_specs=(at_layer,) * 3,
        input_output_aliases=aliases, compiler_params=_params(("arbitrary",)),
    )(*operands)


def kernel(x, norm1_g, w_in, q_norm_g, k_norm_g, sink, sgu_ln_g, sgu_ln_b, w_s, b_s, attn_out_g, sgu_out_g, w_o, norm2_g, w_up, conv_w, conv_b, w_down, loss_target, m_norm1_g, m_w_in, m_q_norm_g, m_k_norm_g, m_sink, m_sgu_ln_g, m_sgu_ln_b, m_w_s, m_b_s, m_attn_out_g, m_sgu_out_g, m_w_o, m_norm2_g, m_w_up, m_conv_w, m_conv_b, m_w_down, v_norm1_g, v_w_in, v_q_norm_g, v_k_norm_g, v_sink, v_sgu_ln_g, v_sgu_ln_b, v_w_s, v_b_s, v_attn_out_g, v_sgu_out_g, v_w_o, v_norm2_g, v_w_up, v_conv_w, v_conv_b, v_w_down):
    weights = dict(norm1_g=norm1_g, w_in=w_in, q_norm_g=q_norm_g, k_norm_g=k_norm_g, sink=sink, sgu_ln_g=sgu_ln_g,
                   sgu_ln_b=sgu_ln_b, w_s=w_s, b_s=b_s, attn_out_g=attn_out_g, sgu_out_g=sgu_out_g, w_o=w_o, norm2_g=norm2_g,
                   w_up=w_up, conv_w=conv_w, conv_b=conv_b, w_down=w_down)
    m_in = dict(norm1_g=m_norm1_g, w_in=m_w_in, q_norm_g=m_q_norm_g, k_norm_g=m_k_norm_g, sink=m_sink, sgu_ln_g=m_sgu_ln_g,
                sgu_ln_b=m_sgu_ln_b, w_s=m_w_s, b_s=m_b_s, attn_out_g=m_attn_out_g, sgu_out_g=m_sgu_out_g, w_o=m_w_o,
                norm2_g=m_norm2_g, w_up=m_w_up, conv_w=m_conv_w, conv_b=m_conv_b, w_down=m_w_down)
    v_in = dict(norm1_g=v_norm1_g, w_in=v_w_in, q_norm_g=v_q_norm_g, k_norm_g=v_k_norm_g, sink=v_sink, sgu_ln_g=v_sgu_ln_g,
                sgu_ln_b=v_sgu_ln_b, w_s=v_w_s, b_s=v_b_s, attn_out_g=v_attn_out_g, sgu_out_g=v_sgu_out_g, w_o=v_w_o,
                norm2_g=v_norm2_g, w_up=v_w_up, conv_w=v_conv_w, conv_b=v_conv_b, w_down=v_w_down)
    cx, cy, cc = lax.axis_index("x"), lax.axis_index("y"), lax.axis_index("c")
    j_me = 2 * cx + cy
    c_arr = jnp.reshape(cc, (1,)).astype(jnp.int32)
    j_arr = jnp.reshape(j_me, (1,)).astype(jnp.int32)

    _Order.last = None
    ex = _Exchange(weights, m_in, v_in, j_arr, c_arr, jnp.reshape(4 * cx + 2 * cy + cc, (1,)).astype(jnp.int32))
    small = [_small_views(l, norm1_g, q_norm_g, k_norm_g, sink, sgu_ln_g, sgu_ln_b, w_s, b_s, attn_out_g, sgu_out_g, norm2_g,
                          conv_b) for l in range(DEPTH)]
    packed_in = [[_pack_call([src[nm][l] for nm in REPLICATED_NAMES], f"pack_{tag}_l{l}")
                  for tag, src in (("w", weights), ("m", m_in), ("v", v_in))] for l in range(DEPTH)]
    dx = _local_step(x[0], loss_target[0], ex, small)
    big_out = ex.finish()

    rep_shapes = [weights[nm].shape[1:] for nm in REPLICATED_NAMES]
    rep_rows = sum(_pack_rows(shp) for shp in rep_shapes)
    cw_shape = (3, 2 * D_FF)
    sums, adam_small = [None] * DEPTH, None
    for l in reversed(range(DEPTH)):
        sums[l] = ex.small_sum(l)
        pw, pm, pv = packed_in[l]
        adam_small = _adamw_packed(pw, sums[l], pm, pv, rep_rows, l, adam_small, f"adamw_small_l{l}")
    cw_rows = _pack_rows(cw_shape)
    loss = sums[0][rep_rows + cw_rows, 0]
    stacked = jnp.stack([sm[:rep_rows + cw_rows] for sm in sums])
    grads = dict(zip(REPLICATED_NAMES, _unpack_layers(stacked[:, :rep_rows], rep_shapes)))
    delta, new_m, new_v = (dict(zip(REPLICATED_NAMES, _unpack_layers(arr, rep_shapes))) for arr in adam_small)
    cw_cols = 2 * D_FF // N_CHIPS
    cw_grad = lax.dynamic_slice_in_dim(_unpack_layers(stacked[:, rep_rows:], [cw_shape])[0], j_me * cw_cols, cw_cols, axis=2)
    flat = lambda a: a.reshape(DEPTH * 3, cw_cols)
    cw_out = _adamw(flat(conv_w), flat(cw_grad), flat(m_conv_w), flat(v_conv_w), "adamw_conv_w")
    grads["conv_w"], delta["conv_w"], new_m["conv_w"], new_v["conv_w"] = (a.reshape(DEPTH, 3, cw_cols) for a in cw_out)

    for name in BIG_NAMES:
        grads[name], delta[name], new_m[name], new_v[name] = big_out[name]

    order = ("norm1_g", "w_in", "q_norm_g", "k_norm_g", "sink", "sgu_ln_g", "sgu_ln_b", "w_s", "b_s", "attn_out_g", "sgu_out_g",
             "w_o", "norm2_g", "w_up", "conv_w", "conv_b", "w_down")
    return (loss, dx[None], *[grads[nm] for nm in order], *[delta[nm] for nm in order], *[new_m[nm] for nm in order],
            *[new_v[nm] for nm in order])
```

```python
import math

import jax
import jax.numpy as jnp
from jax import lax
from jax.experimental import pallas as pl
from jax.experimental.pallas import tpu as pltpu

F32 = jnp.float32
BF16 = jnp.bfloat16

D_MODEL = 2048
HEAD_DIM = 128
ATTN_WIDTH = 1024
N_Q_HEADS = 8
N_KV_HEADS = 2
GQA_GROUP = 4
KV_WIDTH = 256
GMLP_WIDTH = 1024
N_GMLP_HEADS = 8
BLOCK = 128
IN_WIDTH = 3584
D_FF = 5632
DEPTH = 2
EPS = 1e-6
MASK_VALUE = -1e30
ROPE_THETA = 10000.0
N_CHIPS = 4

ADAM_LR = 0.001
ADAM_B1 = 0.9
ADAM_B2 = 0.999
ADAM_EPS = 1e-08
ADAM_WD = 0.01
ADAM_STEP = 10

V7X_VMEM_LIMIT = 48 * 1024 * 1024
MESH = pl.DeviceIdType.MESH

_GELU_C = math.sqrt(2.0 / math.pi)
_GELU_A = 0.044715


def _params(sem=None):
    return pltpu.CompilerParams(dimension_semantics=sem, vmem_limit_bytes=V7X_VMEM_LIMIT)


ANY = pl.BlockSpec(memory_space=pl.ANY)


class _Order:
    last = None


def _ordered_call(body, *, token_index=0, **kw):
    def run(*operands):
        tok = _Order.last
        if tok is None or any(op is tok for op in operands):
            call = pl.pallas_call(body, **kw)
        else:
            n_in = len(operands)

            def ordered_body(*refs):
                return body(*refs[:n_in], *refs[n_in + 1:])

            kw2 = dict(kw)
            if "grid_spec" in kw2:
                gs = kw2["grid_spec"]
                kw2["grid_spec"] = pltpu.PrefetchScalarGridSpec(
                    num_scalar_prefetch=gs.num_scalar_prefetch, grid=gs.grid, in_specs=list(gs.in_specs) + [ANY],
                    out_specs=gs.out_specs, scratch_shapes=gs.scratch_shapes)
            else:
                kw2["in_specs"] = list(kw2["in_specs"]) + [ANY]
            call = pl.pallas_call(ordered_body, **kw2)
            operands = operands + (tok,)
        out = call(*operands)
        _Order.last = out[token_index] if isinstance(out, (tuple, list)) else out
        return out

    return run


def _gelu(x):
    return x * (0.5 * (1.0 + jnp.tanh(_GELU_C * (x + _GELU_A * (x * x * x)))))


def _gelu_grad(x):
    x2 = x * x
    t = jnp.tanh(_GELU_C * (x + _GELU_A * (x * x2)))
    return 0.5 * (1.0 + t) + 0.5 * x * (1.0 - t * t) * (_GELU_C * (1.0 + 3.0 * _GELU_A * x2))


def _mean_last(x):
    return jnp.mean(x, axis=-1, keepdims=True)


def _sum_rows(x):
    return jnp.sum(x, axis=0, keepdims=True)


def _sum_all(x):
    return jnp.sum(jnp.sum(x, axis=1, keepdims=True), axis=0, keepdims=True)


def _matmul(a, b, *, mode, out_dtype, tm, tn, tk, name, res=None, a_parts=0, b_parts=0, out_parts=0):
    assert mode in ("nn", "nt"), mode
    if mode == "nn":
        assert not a_parts
        m, k = a.shape
        n = b.shape[0] * b.shape[2] if b_parts else b.shape[1]
    else:
        m, k = (a.shape[1], a.shape[0] * a.shape[2]) if a_parts else a.shape
        n = b.shape[1] if b_parts else b.shape[0]
    tm, tn, tk = min(tm, m), min(tn, n), min(tk, k)
    assert m % tm == 0 and n % tn == 0 and k % tk == 0, (name, m, n, k, tm, tn, tk)
    nm, nn, nk = m // tm, n // tn, k // tk

    def slab(idx, total_tiles, parts):
        per = total_tiles // parts
        assert per * parts == total_tiles, (name, total_tiles, parts)
        return idx // per, idx % per

    if mode == "nn":
        a_spec = pl.BlockSpec((tm, tk), lambda i, j, kk: (i, kk))
        if b_parts:
            b_spec = pl.BlockSpec((None, tk, tn), lambda i, j, kk: (slab(j, nn, b_parts)[0], kk, slab(j, nn, b_parts)[1]))
        else:
            b_spec = pl.BlockSpec((tk, tn), lambda i, j, kk: (kk, j))
        dims = (((1,), (0,)), ((), ()))
    else:
        if a_parts:
            a_spec = pl.BlockSpec((None, tm, tk), lambda i, j, kk: (slab(kk, nk, a_parts)[0], i, slab(kk, nk, a_parts)[1]))
        else:
            a_spec = pl.BlockSpec((tm, tk), lambda i, j, kk: (i, kk))
        if b_parts:
            b_spec = pl.BlockSpec((None, tn, tk), lambda i, j, kk: (slab(kk, nk, b_parts)[0], j, slab(kk, nk, b_parts)[1]))
        else:
            b_spec = pl.BlockSpec((tn, tk), lambda i, j, kk: (j, kk))
        dims = (((1,), (1,)), ((), ()))
    if out_parts:
        out_shape = jax.ShapeDtypeStruct((out_parts, m, n // out_parts), out_dtype)
        out_spec = pl.BlockSpec((None, tm, tn), lambda i, j, kk: (slab(j, nn, out_parts)[0], i, slab(j, nn, out_parts)[1]))
    else:
        out_shape = jax.ShapeDtypeStruct((m, n), out_dtype)
        out_spec = pl.BlockSpec((tm, tn), lambda i, j, kk: (i, j))
    in_specs = [a_spec, b_spec]
    operands = [a, b]
    if res is not None:
        in_specs.append(pl.BlockSpec((tm, tn), lambda i, j, kk: (i, j)))
        operands.append(res)

    def body(*refs):
        a_ref, b_ref = refs[0], refs[1]
        res_ref = refs[2] if res is not None else None
        o_ref = refs[3] if res is not None else refs[2]
        p = lax.dot_general(a_ref[...], b_ref[...], dims, preferred_element_type=F32)

        def finish(total):
            if res_ref is not None:
                total = res_ref[...] + total
            o_ref[...] = total.astype(out_dtype)

        if nk == 1:
            finish(p)
        else:
            acc_ref = refs[-1]
            kk = pl.program_id(2)

            @pl.when(kk == 0)
            def _():
                acc_ref[...] = p

            @pl.when(jnp.logical_and(kk > 0, kk < nk - 1))
            def _():
                acc_ref[...] += p

            @pl.when(kk == nk - 1)
            def _():
                finish(acc_ref[...] + p)

    scratch = [pltpu.VMEM((tm, tn), F32)] if nk > 1 else []
    return _ordered_call(
        body, name=name, out_shape=out_shape, grid=(nm, nn, nk), in_specs=in_specs, out_specs=out_spec,
        scratch_shapes=scratch, compiler_params=_params(("parallel", "parallel", "arbitrary")),
    )(*operands)


def _matmul_nt_slabs(a, b, *, tm, tn, name, a_parts=0):
    nslab, n, ks = b.shape
    m = a.shape[1] if a_parts else a.shape[0]
    tm, tn = min(tm, m), min(tn, n)
    assert m % tm == 0 and n % tn == 0, (name, m, n, tm, tn)
    if a_parts:
        per = nslab // a_parts
        assert per * a_parts == nslab and a.shape[2] == per * ks, (name, a.shape, b.shape)
        a_spec = pl.BlockSpec((a_parts, tm, per * ks), lambda i, j: (0, i, 0))
    else:
        assert a.shape[1] == nslab * ks, (name, a.shape, b.shape)
        a_spec = pl.BlockSpec((tm, nslab * ks), lambda i, j: (i, 0))

    def body(a_ref, b_ref, o_ref):
        total = None
        for sl in range(nslab):
            if a_parts:
                a_sl = a_ref[sl // per, :, (sl % per) * ks:(sl % per + 1) * ks]
            else:
                a_sl = a_ref[:, sl * ks:(sl + 1) * ks]
            p = lax.dot_general(a_sl, b_ref[sl], (((1,), (1,)), ((), ())), preferred_element_type=F32)
            total = p if total is None else total + p
        o_ref[...] = total

    return _ordered_call(
        body, name=name, out_shape=jax.ShapeDtypeStruct((m, n), F32), grid=(m // tm, n // tn),
        in_specs=[a_spec, pl.BlockSpec((nslab, tn, ks), lambda i, j: (0, j, 0))],
        out_specs=pl.BlockSpec((tm, tn), lambda i, j: (i, j)), compiler_params=_params(("parallel", "parallel")),
    )(a, b)


GRAD_HALVES = {
    "w_in": ("rows_of_slab", 1024, 896), "w_up": ("rows_of_slab", 1024, 1408), "w_o": ("rows_of_block", 256, 2048),
    "w_down": ("cols_of_block", 1408, 512)}


def _half_shape(name, shard_shape):
    r, cols = shard_shape
    return (r, cols // 2) if GRAD_HALVES[name][0] == "cols_of_block" else (r // 2, cols)


def _grad_half(name, a, g, sel, res, call_name, g_parts=0):
    kind, tm, tn = GRAD_HALVES[name]
    s, m = a.shape
    n = g.shape[0] * g.shape[2] if g_parts else g.shape[1]
    if kind == "rows_of_slab":
        rh, hc = m // 2, n // N_CHIPS
        per = hc // tn
        grid = (rh // tm, n // tn)
        a_map = lambda i, j, sel_ref: (0, sel_ref[0] * (rh // tm) + i)
        g_col = lambda i, j, sel_ref: j
        o_map = lambda i, j, sel_ref: (j // per, i, j % per)
    elif kind == "rows_of_block":
        rh, hc = m // N_CHIPS // 2, n
        assert tm == rh
        grid = (N_CHIPS, n // tn)
        a_map = lambda i, j, sel_ref: (0, 2 * i + sel_ref[0])
        g_col = lambda i, j, sel_ref: j
        o_map = lambda i, j, sel_ref: (i, 0, j)
    else:
        rh, hc = m // N_CHIPS, n // 2
        assert tm == rh
        grid = (N_CHIPS, hc // tn)
        a_map = lambda i, j, sel_ref: (0, i)
        g_col = lambda i, j, sel_ref: sel_ref[0] * (hc // tn) + j
        o_map = lambda i, j, sel_ref: (i, 0, j)
    if g_parts:
        g_per = (n // tn) // g_parts
        g_spec = pl.BlockSpec((None, s, tn), lambda i, j, sel_ref: (g_col(i, j, sel_ref) // g_per, 0, g_col(i, j, sel_ref) % g_per))
    else:
        g_spec = pl.BlockSpec((s, tn), lambda i, j, sel_ref: (0, g_col(i, j, sel_ref)))
    o_spec = pl.BlockSpec((None, tm, tn), o_map)
    in_specs = [pl.BlockSpec((s, tm), a_map), g_spec] + ([o_spec] if res is not None else [])

    def body(sel_ref, a_ref, g_ref, *rest):
        o_ref = rest[-1]
        p = lax.dot_general(a_ref[...], g_ref[...], (((0,), (0,)), ((), ())), preferred_element_type=F32)
        if res is not None:
            p = p + rest[0][...].astype(F32)
        o_ref[...] = p.astype(BF16)

    grid_spec = pltpu.PrefetchScalarGridSpec(num_scalar_prefetch=1, grid=grid, in_specs=in_specs, out_specs=o_spec)
    return _ordered_call(
        body, name=call_name, out_shape=jax.ShapeDtypeStruct((N_CHIPS, rh, hc), BF16), grid_spec=grid_spec,
        compiler_params=_params(("parallel", "parallel")),
    )(sel, a, g, *([res] if res is not None else []))


def _row_tile(s):
    return min(256, s)


def _rows(width, tr):
    return pl.BlockSpec((tr, width), lambda i: (i, 0))


def _const2(shape):
    return pl.BlockSpec(shape, lambda i: (0, 0))


def _rms_fwd(x, g, name):
    s, d = x.shape
    tr = _row_tile(s)

    def body(x_ref, g_ref, o_ref):
        xv = x_ref[...]
        r = lax.rsqrt(_mean_last(xv * xv) + EPS)
        o_ref[...] = (xv * r * g_ref[...]).astype(BF16)

    return _ordered_call(
        body, name=name, out_shape=jax.ShapeDtypeStruct((s, d), BF16), grid=(s // tr,),
        in_specs=[_rows(d, tr), _const2((1, d))], out_specs=_rows(d, tr), compiler_params=_params(("parallel",)),
    )(x, g)


def _rms_bwd(x, g, dh, dres, name):
    s, d = x.shape
    tr = _row_tile(s)

    def body(x_ref, g_ref, dh_ref, dres_ref, dx_ref, dxb_ref, dg_ref):
        xv, dy = x_ref[...], dh_ref[...]
        r = lax.rsqrt(_mean_last(xv * xv) + EPS)
        gdy = dy * g_ref[...]
        dx = dres_ref[...] + r * gdy - xv * ((r * r * r) * _mean_last(xv * gdy))
        dx_ref[...] = dx
        dxb_ref[...] = dx.astype(BF16)

        @pl.when(pl.program_id(0) == 0)
        def _():
            dg_ref[...] = jnp.zeros_like(dg_ref)

        dg_ref[...] += _sum_rows(xv * r * dy)

    return _ordered_call(
        body, name=name,
        out_shape=(jax.ShapeDtypeStruct((s, d), F32), jax.ShapeDtypeStruct((s, d), BF16), jax.ShapeDtypeStruct((1, d), F32)),
        grid=(s // tr,), in_specs=[_rows(d, tr), _const2((1, d)), _rows(d, tr), _rows(d, tr)],
        out_specs=(_rows(d, tr), _rows(d, tr), _const2((1, d))), compiler_params=_params(("arbitrary",)),
    )(x, g, dh, dres)


Q0, K0, V0, GU0, GV0 = 0, ATTN_WIDTH, ATTN_WIDTH + KV_WIDTH, ATTN_WIDTH + 2 * KV_WIDTH, ATTN_WIDTH + 2 * KV_WIDTH + GMLP_WIDTH


def _head(h, base=0):
    return slice(base + h * HEAD_DIM, base + (h + 1) * HEAD_DIM)


def _proj_post(z, qg, kg, lg, lb, cosf, sinf, name):
    s = z.shape[0]
    tr = _row_tile(s)

    def body(z_ref, qg_ref, kg_ref, lg_ref, lb_ref, cos_ref, sin_ref, qn_ref, kn_ref, vb_ref, ug_ref, vn_ref,
             dgu_ref, dgv_ref, xhat_ref, rstd_ref):
        cos, sin = cos_ref[...], sin_ref[...]

        def norm_rope(xh, g):
            y = xh * lax.rsqrt(_mean_last(xh * xh) + EPS) * g
            return y * cos + pltpu.roll(y, HEAD_DIM // 2, 1) * sin

        for h in range(N_Q_HEADS):
            qn_ref[:, _head(h)] = norm_rope(z_ref[:, _head(h, Q0)].astype(F32), qg_ref[...]).astype(BF16)
        for h in range(N_KV_HEADS):
            kn_ref[:, _head(h)] = norm_rope(z_ref[:, _head(h, K0)].astype(F32), kg_ref[...]).astype(BF16)
        vb_ref[...] = z_ref[:, V0:GU0]
        gu = z_ref[:, GU0:GV0].astype(F32)
        ug_ref[...] = _gelu(gu)
        dgu_ref[...] = _gelu_grad(gu).astype(BF16)
        gv = z_ref[:, GV0:IN_WIDTH].astype(F32)
        vg = _gelu(gv)
        dgv_ref[...] = _gelu_grad(gv).astype(BF16)
        xc = vg - _mean_last(vg)
        r = lax.rsqrt(_mean_last(xc * xc) + EPS)
        y = xc * r
        xhat_ref[...] = y.astype(BF16)
        rstd_ref[...] = r
        vn_ref[...] = (y * lg_ref[...] + lb_ref[...]).astype(BF16)

    wide = jax.ShapeDtypeStruct((s, GMLP_WIDTH), BF16)
    return _ordered_call(
        body, name=name,
        out_shape=(jax.ShapeDtypeStruct((s, ATTN_WIDTH), BF16), jax.ShapeDtypeStruct((s, KV_WIDTH), BF16),
                   jax.ShapeDtypeStruct((s, KV_WIDTH), BF16), jax.ShapeDtypeStruct((s, GMLP_WIDTH), F32), wide,
                   wide, wide, wide, jax.ShapeDtypeStruct((s, 1), F32)),
        grid=(s // tr,),
        in_specs=[_rows(IN_WIDTH, tr), _const2((1, HEAD_DIM)), _const2((1, HEAD_DIM)), _const2((1, GMLP_WIDTH)),
                  _const2((1, GMLP_WIDTH)), _rows(HEAD_DIM, tr), _rows(HEAD_DIM, tr)],
        out_specs=(_rows(ATTN_WIDTH, tr), _rows(KV_WIDTH, tr), _rows(KV_WIDTH, tr), _rows(GMLP_WIDTH, tr), _rows(GMLP_WIDTH, tr),
                   _rows(GMLP_WIDTH, tr), _rows(GMLP_WIDTH, tr), _rows(GMLP_WIDTH, tr), _rows(1, tr)),
        compiler_params=_params(("parallel",)),
    )(z, qg, kg, lg, lb, cosf, sinf)


def _proj_post_bwd(z, dqn, dkn, dvb, dug, dvn, gelu_grad_u, gelu_grad_v, xhat_v, rstd_v, qg, kg, lg, cosf, sinf, name):
    s = z.shape[0]
    tr = _row_tile(s)

    def body(z_ref, dqn_ref, dkn_ref, dvb_ref, dug_ref, dvn_ref, ggu_ref, ggv_ref, xhat_ref, rstd_ref, qg_ref, kg_ref, lg_ref,
             cos_ref, sin_ref, dz_ref, dqg_ref, dkg_ref, dlg_ref, dlb_ref):
        cos, sin = cos_ref[...], sin_ref[...]

        @pl.when(pl.program_id(0) == 0)
        def _():
            dqg_ref[...] = jnp.zeros_like(dqg_ref)
            dkg_ref[...] = jnp.zeros_like(dkg_ref)
            dlg_ref[...] = jnp.zeros_like(dlg_ref)
            dlb_ref[...] = jnp.zeros_like(dlb_ref)

        def norm_rope_bwd(xh, g, dout):
            dy = dout * cos - pltpu.roll(dout, HEAD_DIM // 2, 1) * sin
            r = lax.rsqrt(_mean_last(xh * xh) + EPS)
            xhat = xh * r
            gdy = dy * g
            return r * (gdy - xhat * _mean_last(xhat * gdy)), _sum_rows(xhat * dy)

        dqg = jnp.zeros((1, HEAD_DIM), F32)
        for h in range(N_Q_HEADS):
            dx, dg = norm_rope_bwd(z_ref[:, _head(h, Q0)].astype(F32), qg_ref[...], dqn_ref[:, _head(h)])
            dz_ref[:, _head(h, Q0)] = dx.astype(BF16)
            dqg = dqg + dg
        dqg_ref[...] += dqg
        dkg = jnp.zeros((1, HEAD_DIM), F32)
        for h in range(N_KV_HEADS):
            dx, dg = norm_rope_bwd(z_ref[:, _head(h, K0)].astype(F32), kg_ref[...], dkn_ref[:, _head(h)])
            dz_ref[:, _head(h, K0)] = dx.astype(BF16)
            dkg = dkg + dg
        dkg_ref[...] += dkg
        dz_ref[:, V0:GU0] = dvb_ref[...].astype(BF16)
        dz_ref[:, GU0:GV0] = (dug_ref[...] * ggu_ref[...].astype(F32)).astype(BF16)
        xhat = xhat_ref[...].astype(F32)
        dvn_v = dvn_ref[...]
        dlg_ref[...] += _sum_rows(xhat * dvn_v)
        dlb_ref[...] += _sum_rows(dvn_v)
        dxh = dvn_v * lg_ref[...]
        dvg = rstd_ref[...] * (dxh - _mean_last(dxh) - xhat * _mean_last(dxh * xhat))
        dz_ref[:, GV0:IN_WIDTH] = (dvg * ggv_ref[...].astype(F32)).astype(BF16)

    return _ordered_call(
        body, name=name,
        out_shape=(jax.ShapeDtypeStruct((s, IN_WIDTH), BF16), jax.ShapeDtypeStruct((1, HEAD_DIM), F32),
                   jax.ShapeDtypeStruct((1, HEAD_DIM), F32), jax.ShapeDtypeStruct((1, GMLP_WIDTH), F32),
                   jax.ShapeDtypeStruct((1, GMLP_WIDTH), F32)),
        grid=(s // tr,),
        in_specs=[_rows(V0, tr), _rows(ATTN_WIDTH, tr), _rows(KV_WIDTH, tr), _rows(KV_WIDTH, tr), _rows(GMLP_WIDTH, tr),
                  _rows(GMLP_WIDTH, tr), _rows(GMLP_WIDTH, tr), _rows(GMLP_WIDTH, tr), _rows(GMLP_WIDTH, tr), _rows(1, tr),
                  _const2((1, HEAD_DIM)), _const2((1, HEAD_DIM)), _const2((1, GMLP_WIDTH)), _rows(HEAD_DIM, tr),
                  _rows(HEAD_DIM, tr)],
        out_specs=(_rows(IN_WIDTH, tr), _const2((1, HEAD_DIM)), _const2((1, HEAD_DIM)), _const2((1, GMLP_WIDTH)),
                   _const2((1, GMLP_WIDTH))),
        compiler_params=_params(("arbitrary",)),
    )(z, dqn, dkn, dvb, dug, dvn, gelu_grad_u, gelu_grad_v, xhat_v, rstd_v, qg, kg, lg, cosf, sinf)


def _band_valid(n, s):
    shape = (GQA_GROUP * BLOCK, 3 * BLOCK)
    i = lax.broadcasted_iota(jnp.int32, shape, 0) & (BLOCK - 1)
    j = lax.broadcasted_iota(jnp.int32, shape, 1)
    k_pos = n * BLOCK - BLOCK + j
    return (jnp.abs(j - BLOCK - i) <= BLOCK) & (k_pos >= 0) & (k_pos < s)


def _group_rows(x, kh):
    return jnp.concatenate([x[:, _head(kh * GQA_GROUP + g)] for g in range(GQA_GROUP)], axis=0)


def _group_sinks(sink_ref, kh):
    return jnp.concatenate([jnp.full((BLOCK, 1), sink_ref[kh * GQA_GROUP + g], F32) for g in range(GQA_GROUP)], axis=0)


def _rows_of(x, g):
    return x[g * BLOCK:(g + 1) * BLOCK]


def _probs(q, kb, sink_h, valid):
    sc = lax.dot_general(q, kb, (((1,), (1,)), ((), ())), preferred_element_type=F32) * (HEAD_DIM ** -0.5)
    sc = jnp.where(valid, sc, MASK_VALUE)
    m = jnp.maximum(jnp.max(sc, axis=-1, keepdims=True), sink_h)
    p = jnp.exp(sc - m)
    es = jnp.exp(sink_h - m)
    den = jnp.sum(p, axis=-1, keepdims=True) + es
    inv = 1.0 / den
    return p * inv, es * inv


def _band_specs(width, nb):
    return [pl.BlockSpec((BLOCK, width), lambda n: (jnp.maximum(n - 1, 0), 0)),
            pl.BlockSpec((BLOCK, width), lambda n: (n, 0)),
            pl.BlockSpec((BLOCK, width), lambda n: (jnp.minimum(n + 1, nb - 1), 0))]


def _blk(width):
    return pl.BlockSpec((BLOCK, width), lambda n: (n, 0))


def _whole3(shape):
    return pl.BlockSpec(shape, lambda n: (0, 0, 0))


def _smem():
    return pl.BlockSpec(memory_space=pltpu.SMEM)


def _mixer_fwd(qn, kn, vb, ug, vn, wsb, bsb, sink, ga, gs, name):
    s = qn.shape[0]
    nb = s // BLOCK

    def body(sink_ref, q_ref, kp_ref, kc_ref, kx_ref, vp_ref, vc_ref, vx_ref, ug_ref, vn_ref, ws_ref, bs_ref, ga_ref, gs_ref,
             attn_ref, sgu_ref, mix_ref, probs_ref, psink_ref):
        n = pl.program_id(0)
        valid = _band_valid(n, s)
        ssq = jnp.zeros((BLOCK, 1), F32)
        for kh in range(N_KV_HEADS):
            kb = jnp.concatenate([kp_ref[:, _head(kh)], kc_ref[:, _head(kh)], kx_ref[:, _head(kh)]], axis=0)
            vbd = jnp.concatenate([vp_ref[:, _head(kh)], vc_ref[:, _head(kh)], vx_ref[:, _head(kh)]], axis=0)
            p, p_sink = _probs(_group_rows(q_ref, kh), kb, _group_sinks(sink_ref, kh), valid)
            pb = p.astype(BF16)
            probs_ref[kh] = pb
            psink_ref[kh] = p_sink
            o4 = jnp.dot(pb, vbd, preferred_element_type=F32)
            for g in range(GQA_GROUP):
                o = _rows_of(o4, g)
                attn_ref[:, _head(kh * GQA_GROUP + g)] = o
                ssq = ssq + jnp.sum(o * o, axis=-1, keepdims=True)
        r = lax.rsqrt(ssq * (1.0 / ATTN_WIDTH) + EPS)
        mix_ref[:, 0:ATTN_WIDTH] = (attn_ref[...] * r * ga_ref[...]).astype(BF16)
        ssq = jnp.zeros((BLOCK, 1), F32)
        for h in range(N_GMLP_HEADS):
            f = jnp.dot(ws_ref[h], vn_ref[:, _head(h)], preferred_element_type=F32) + bs_ref[h]
            o = ug_ref[:, _head(h)] * f
            sgu_ref[:, _head(h)] = o
            ssq = ssq + jnp.sum(o * o, axis=-1, keepdims=True)
        r = lax.rsqrt(ssq * (1.0 / GMLP_WIDTH) + EPS)
        mix_ref[:, ATTN_WIDTH:D_MODEL] = (sgu_ref[...] * r * gs_ref[...]).astype(BF16)

    hh = (N_GMLP_HEADS, BLOCK, BLOCK)
    return _ordered_call(
        body, name=name,
        out_shape=(jax.ShapeDtypeStruct((s, ATTN_WIDTH), F32), jax.ShapeDtypeStruct((s, GMLP_WIDTH), F32),
                   jax.ShapeDtypeStruct((s, D_MODEL), BF16), jax.ShapeDtypeStruct((nb,) + PROBS_BLOCK, BF16),
                   jax.ShapeDtypeStruct((nb,) + PSINK_BLOCK, F32)),
        grid=(nb,),
        in_specs=[_smem(), _blk(ATTN_WIDTH)] + _band_specs(KV_WIDTH, nb) + _band_specs(KV_WIDTH, nb)
        + [_blk(GMLP_WIDTH), _blk(GMLP_WIDTH), _whole3(hh), _whole3(hh),
           pl.BlockSpec((1, ATTN_WIDTH), lambda n: (0, 0)), pl.BlockSpec((1, GMLP_WIDTH), lambda n: (0, 0))],
        out_specs=(_blk(ATTN_WIDTH), _blk(GMLP_WIDTH), _blk(D_MODEL), _per_block(PROBS_BLOCK), _per_block(PSINK_BLOCK)),
        compiler_params=_params(("parallel",)),
    )(sink, qn, kn, kn, kn, vb, vb, vb, ug, vn, wsb, bsb, ga, gs)


PROBS_BLOCK = (N_KV_HEADS, GQA_GROUP * BLOCK, 3 * BLOCK)
PSINK_BLOCK = (N_KV_HEADS, GQA_GROUP * BLOCK, 1)


def _per_block(shape):
    return pl.BlockSpec((None,) + shape, lambda n: (n, 0, 0, 0))


def _mixer_bwd(qn, kn, vb, ug, vn, attn, sgu, dmixed, wsb, bsb, ga, gs, probs, psink, name):
    s = qn.shape[0]
    nb = s // BLOCK
    tn_dims = (((0,), (0,)), ((), ()))
    nt_dims = (((1,), (1,)), ((), ()))

    def body(q_ref, kp_ref, kc_ref, kx_ref, vp_ref, vc_ref, vx_ref, ug_ref, vn_ref, attn_ref, sgu_ref, dm_ref,
             ws_ref, bs_ref, ga_ref, gs_ref, probs_ref, psink_ref,
             dq_ref, dk_ref, dv_ref, dug_ref, dvn_ref, dws_ref, dbs_ref, dsk_ref, dga_ref, dgs_ref, dk_acc, dv_acc):
        n = pl.program_id(0)

        @pl.when(n == 0)
        def _():
            for ref in (dk_acc, dv_acc, dws_ref, dbs_ref, dsk_ref, dga_ref, dgs_ref):
                ref[...] = jnp.zeros_like(ref)

        def out_norm_bwd(o, g, dy):
            r = lax.rsqrt(_mean_last(o * o) + EPS)
            gdy = dy * g
            return r * gdy - o * ((r * r * r) * _mean_last(o * gdy)), _sum_rows(o * r * dy)

        d_attn, dga = out_norm_bwd(attn_ref[...], ga_ref[...], dm_ref[:, 0:ATTN_WIDTH])
        dga_ref[...] += dga
        d_sgu, dgs = out_norm_bwd(sgu_ref[...], gs_ref[...], dm_ref[:, ATTN_WIDTH:D_MODEL])
        dgs_ref[...] += dgs

        for h in range(N_GMLP_HEADS):
            vn_h = vn_ref[:, _head(h)]
            f = jnp.dot(ws_ref[h], vn_h, preferred_element_type=F32) + bs_ref[h]
            ds_h = d_sgu[:, _head(h)]
            dug_ref[:, _head(h)] = ds_h * f
            df = ds_h * ug_ref[:, _head(h)]
            dfb = df.astype(BF16)
            dvn_ref[:, _head(h)] = lax.dot_general(ws_ref[h], dfb, tn_dims, preferred_element_type=F32)
            dws_ref[h] += lax.dot_general(dfb, vn_h, nt_dims, preferred_element_type=F32)
            dbs_ref[h] += jnp.broadcast_to(jnp.sum(df, axis=-1, keepdims=True), (BLOCK, BLOCK))

        row0 = pl.multiple_of(n * BLOCK, BLOCK)
        for kh in range(N_KV_HEADS):
            kb = jnp.concatenate([kp_ref[:, _head(kh)], kc_ref[:, _head(kh)], kx_ref[:, _head(kh)]], axis=0)
            vbd = jnp.concatenate([vp_ref[:, _head(kh)], vc_ref[:, _head(kh)], vx_ref[:, _head(kh)]], axis=0)
            q4 = _group_rows(q_ref, kh)
            pb = probs_ref[kh]
            p = pb.astype(F32)
            do4 = _group_rows(d_attn, kh).astype(BF16)
            dp = lax.dot_general(do4, vbd, nt_dims, preferred_element_type=F32)
            delta = jnp.sum(p * dp, axis=-1, keepdims=True)
            dsc = (p * (dp - delta) * (HEAD_DIM ** -0.5)).astype(BF16)
            d_sink = -(psink_ref[kh] * delta)
            dq4 = jnp.dot(dsc, kb, preferred_element_type=F32)
            for g in range(GQA_GROUP):
                h = kh * GQA_GROUP + g
                dsk_ref[h:h + 1, :] += jnp.broadcast_to(_sum_all(_rows_of(d_sink, g)), (1, BLOCK))
                dq_ref[:, _head(h)] = _rows_of(dq4, g)
            dk_acc[pl.ds(row0, 3 * BLOCK), _head(kh)] += lax.dot_general(dsc, q4, tn_dims, preferred_element_type=F32)
            dv_acc[pl.ds(row0, 3 * BLOCK), _head(kh)] += lax.dot_general(pb, do4, tn_dims, preferred_element_type=F32)

        @pl.when(n == nb - 1)
        def _():
            dk_ref[...] = dk_acc[BLOCK:BLOCK + s, :]
            dv_ref[...] = dv_acc[BLOCK:BLOCK + s, :]

    hh = (N_GMLP_HEADS, BLOCK, BLOCK)
    full_kv = pl.BlockSpec((s, KV_WIDTH), lambda n: (0, 0))
    return _ordered_call(
        body, name=name,
        out_shape=(jax.ShapeDtypeStruct((s, ATTN_WIDTH), F32), jax.ShapeDtypeStruct((s, KV_WIDTH), F32),
                   jax.ShapeDtypeStruct((s, KV_WIDTH), F32), jax.ShapeDtypeStruct((s, GMLP_WIDTH), F32),
                   jax.ShapeDtypeStruct((s, GMLP_WIDTH), F32), jax.ShapeDtypeStruct(hh, F32), jax.ShapeDtypeStruct(hh, F32),
                   jax.ShapeDtypeStruct((N_Q_HEADS, BLOCK), F32), jax.ShapeDtypeStruct((1, ATTN_WIDTH), F32),
                   jax.ShapeDtypeStruct((1, GMLP_WIDTH), F32)),
        grid=(nb,),
        in_specs=[_blk(ATTN_WIDTH)] + _band_specs(KV_WIDTH, nb) + _band_specs(KV_WIDTH, nb)
        + [_blk(GMLP_WIDTH), _blk(GMLP_WIDTH), _blk(ATTN_WIDTH), _blk(GMLP_WIDTH), _blk(D_MODEL), _whole3(hh), _whole3(hh),
           pl.BlockSpec((1, ATTN_WIDTH), lambda n: (0, 0)), pl.BlockSpec((1, GMLP_WIDTH), lambda n: (0, 0)),
           _per_block(PROBS_BLOCK), _per_block(PSINK_BLOCK)],
        out_specs=(_blk(ATTN_WIDTH), full_kv, full_kv, _blk(GMLP_WIDTH), _blk(GMLP_WIDTH), _whole3(hh), _whole3(hh),
                   pl.BlockSpec((N_Q_HEADS, BLOCK), lambda n: (0, 0)), pl.BlockSpec((1, ATTN_WIDTH), lambda n: (0, 0)),
                   pl.BlockSpec((1, GMLP_WIDTH), lambda n: (0, 0))),
        scratch_shapes=[pltpu.VMEM((s + 2 * BLOCK, KV_WIDTH), F32), pltpu.VMEM((s + 2 * BLOCK, KV_WIDTH), F32)],
        compiler_params=_params(("arbitrary",)),
    )(qn, kn, kn, kn, vb, vb, vb, ug, vn, attn, sgu, dmixed, wsb, bsb, ga, gs, probs, psink)


CONV_TILE = 128


PAD_ROWS = 8


def _zero_pad_rows(pad_ref):
    s = pad_ref.shape[0] - 2 * PAD_ROWS
    zeros = jnp.zeros((PAD_ROWS, pad_ref.shape[1]), F32)
    pad_ref[0:PAD_ROWS, :] = zeros
    pad_ref[PAD_ROWS + s:2 * PAD_ROWS + s, :] = zeros


def _shift_rows(a, pad_ref):
    s = a.shape[0]
    pad_ref[PAD_ROWS:PAD_ROWS + s, :] = a
    padded = pad_ref[...]
    prev = pltpu.roll(padded, 1, 0)[PAD_ROWS:PAD_ROWS + s]
    nxt = pltpu.roll(padded, s + 2 * PAD_ROWS - 1, 0)[PAD_ROWS:PAD_ROWS + s]
    return prev, nxt


def _conv_specs(s):
    tc = CONV_TILE
    nj = D_FF // tc
    return (tc, nj, pl.BlockSpec((2, s, tc), lambda j: (0, 0, j)),
            [pl.BlockSpec((3, tc), lambda j: (0, j)), pl.BlockSpec((3, tc), lambda j: (0, j + nj))],
            [pl.BlockSpec((1, tc), lambda j: (0, j)), pl.BlockSpec((1, tc), lambda j: (0, j + nj))])


def _conv_gate_fwd(a_pre, cw, cb, name):
    s = a_pre.shape[1]
    tc, nj, a_spec, w_specs, b_specs = _conv_specs(s)

    def body(a_ref, wg_ref, wu_ref, bg_ref, bu_ref, act_ref, dgu_ref, pad_ref):
        _zero_pad_rows(pad_ref)

        def conv(a, w_ref, b_ref):
            prev, nxt = _shift_rows(a, pad_ref)
            return b_ref[...] + prev * w_ref[0:1, :] + a * w_ref[1:2, :] + nxt * w_ref[2:3, :]

        g = conv(a_ref[0].astype(F32), wg_ref, bg_ref)
        u = conv(a_ref[1].astype(F32), wu_ref, bu_ref)
        sg = 1.0 / (1.0 + jnp.exp(-g))
        silu = g * sg
        act_ref[...] = (silu * u).astype(BF16)
        dgu_ref[0] = (u * (sg * (1.0 + g * (1.0 - sg)))).astype(BF16)
        dgu_ref[1] = silu.astype(BF16)

    return _ordered_call(
        body, name=name, out_shape=(jax.ShapeDtypeStruct((s, D_FF), BF16), jax.ShapeDtypeStruct((2, s, D_FF), BF16)),
        grid=(nj,), in_specs=[a_spec] + w_specs + b_specs,
        out_specs=(pl.BlockSpec((s, tc), lambda j: (0, j)), pl.BlockSpec((2, s, tc), lambda j: (0, 0, j))),
        scratch_shapes=[pltpu.VMEM((s + 2 * PAD_ROWS, tc), F32)], compiler_params=_params(("parallel",)),
    )(a_pre, cw, cw, cb, cb)


def _conv_gate_bwd(a_pre, dgu, cw, dact, name):
    s = a_pre.shape[1]
    tc, nj, a_spec, w_specs, _ = _conv_specs(s)

    def body(a_ref, dgu_ref, wg_ref, wu_ref, dact_ref, dap_ref, dcw_ref, dcb_ref, pad_ref):
        _zero_pad_rows(pad_ref)
        dact_v = dact_ref[...].astype(F32)
        for part, w_ref in enumerate((wg_ref, wu_ref)):
            da = dact_v * dgu_ref[part].astype(F32)
            a = a_ref[part].astype(F32)
            da_prev, da_next = _shift_rows(da, pad_ref)
            dcw_ref[part, 0:1, :] = _sum_rows(a * da_next)
            dcw_ref[part, 1:2, :] = _sum_rows(a * da)
            dcw_ref[part, 2:3, :] = _sum_rows(a * da_prev)
            dcb_ref[part] = _sum_rows(da)
            dap_ref[part] = (da_next * w_ref[0:1, :] + da * w_ref[1:2, :] + da_prev * w_ref[2:3, :]).astype(BF16)

    return _ordered_call(
        body, name=name,
        out_shape=(jax.ShapeDtypeStruct((2, s, D_FF), BF16), jax.ShapeDtypeStruct((2, 3, D_FF), F32),
                   jax.ShapeDtypeStruct((2, 1, D_FF), F32)),
        grid=(nj,),
        in_specs=[a_spec, pl.BlockSpec((2, s, tc), lambda j: (0, 0, j))] + w_specs + [pl.BlockSpec((s, tc), lambda j: (0, j))],
        out_specs=(pl.BlockSpec((2, s, tc), lambda j: (0, 0, j)), pl.BlockSpec((2, 3, tc), lambda j: (0, 0, j)),
                   pl.BlockSpec((2, 1, tc), lambda j: (0, 0, j))),
        scratch_shapes=[pltpu.VMEM((s + 2 * PAD_ROWS, tc), F32)], compiler_params=_params(("parallel",)),
    )(a_pre, dgu, cw, cw, dact)


def _loss_head(y, target, name):
    s, d = y.shape
    tr = _row_tile(s)

    def body(y_ref, t_ref, loss_ref, dy_ref, dyb_ref):
        err = y_ref[...] - t_ref[...]

        @pl.when(pl.program_id(0) == 0)
        def _():
            loss_ref[...] = jnp.zeros_like(loss_ref)

        loss_ref[...] += jnp.broadcast_to(0.5 * _sum_all(_mean_last(err * err)), (8, 128))
        dy = err * (1.0 / d)
        dy_ref[...] = dy
        dyb_ref[...] = dy.astype(BF16)

    return _ordered_call(
        body, name=name,
        out_shape=(jax.ShapeDtypeStruct((8, 128), F32), jax.ShapeDtypeStruct((s, d), F32), jax.ShapeDtypeStruct((s, d), BF16)),
        grid=(s // tr,), in_specs=[_rows(d, tr), _rows(d, tr)],
        out_specs=(_const2((8, 128)), _rows(d, tr), _rows(d, tr)), compiler_params=_params(("arbitrary",)),
    )(y, target)


def _row_block(rows, cols, budget=1 << 20):
    if rows * cols <= budget:
        return rows
    best = None
    for tr in range(16, rows, 16):
        if rows % tr == 0 and tr * cols <= budget:
            best = tr
    assert best is not None, (rows, cols)
    return best


def _place_shard(x4, layer, j_arr, out_dtype, name):
    _, nh, r, cols = x4.shape
    tr = _row_block(r, cols)

    def body(j_ref, x_ref, o_ref):
        o_ref[...] = x_ref[...].astype(out_dtype)

    grid_spec = pltpu.PrefetchScalarGridSpec(
        num_scalar_prefetch=1, grid=(nh, r // tr),
        in_specs=[pl.BlockSpec((None, None, tr, cols), lambda h, i, j_ref: (layer, h, i, 0))],
        out_specs=pl.BlockSpec((None, None, tr, cols), lambda h, i, j_ref: (j_ref[0], h, i, 0)))
    return _ordered_call(
        body, name=name, out_shape=jax.ShapeDtypeStruct((N_CHIPS, nh, r, cols), out_dtype), grid_spec=grid_spec,
        compiler_params=_params(("parallel", "parallel")),
    )(j_arr, x4)


def _adamw(w, g, m, v, name):
    rows, cols = w.shape
    tr = _row_block(rows, cols, 1 << 18)

    def body(w_ref, g_ref, m_ref, v_ref, go_ref, d_ref, nm_ref, nv_ref):
        gv = g_ref[...]
        go_ref[...] = gv
        mn = ADAM_B1 * m_ref[...] + (1.0 - ADAM_B1) * gv
        vn = ADAM_B2 * v_ref[...] + (1.0 - ADAM_B2) * (gv * gv)
        m_hat = mn / (1.0 - ADAM_B1 ** ADAM_STEP)
        v_hat = vn / (1.0 - ADAM_B2 ** ADAM_STEP)
        d_ref[...] = -ADAM_LR * (m_hat / (jnp.sqrt(v_hat) + ADAM_EPS) + ADAM_WD * w_ref[...])
        nm_ref[...] = mn
        nv_ref[...] = vn

    sds = jax.ShapeDtypeStruct((rows, cols), F32)
    return _ordered_call(
        body, name=name, out_shape=(sds, sds, sds, sds), grid=(rows // tr,),
        in_specs=[_rows(cols, tr)] * 4, out_specs=(_rows(cols, tr),) * 4, compiler_params=_params(("parallel",)),
    )(w, g, m, v)


def _chip_sum(p4, recv3, j_arr, c_arr, name):
    _, rh, cols = p4.shape
    tr = _row_block(rh, cols, 1 << 19)

    def body(j_ref, c_ref, p_ref, r_ref, o_ref):
        total = p_ref[...].astype(F32)
        for peer in range(3):
            total = total + r_ref[peer].astype(F32)
        o_ref[...] = total

    grid_spec = pltpu.PrefetchScalarGridSpec(
        num_scalar_prefetch=2, grid=(rh // tr,),
        in_specs=[pl.BlockSpec((None, tr, cols), lambda i, j_ref, c_ref: (j_ref[0], i, 0)),
                  pl.BlockSpec((3, tr, cols), lambda i, j_ref, c_ref: (0, i, 0))],
        out_specs=pl.BlockSpec((None, tr, cols), lambda i, j_ref, c_ref: (c_ref[0], i, 0)))
    return _ordered_call(
        body, name=name, out_shape=jax.ShapeDtypeStruct((2, rh, cols), F32), grid_spec=grid_spec,
        compiler_params=_params(("parallel",)),
    )(j_arr, c_arr, p4, recv3)


def _adamw_layer(w, g, m, v, layer, into, name):
    nl, rows, cols = w.shape
    slabs, _, width = g.shape
    assert slabs * width == cols and g.shape[1] == rows, (name, w.shape, g.shape)
    tr = _row_block(rows, width, 1 << 18)
    at_layer = pl.BlockSpec((None, tr, width), lambda h, i: (layer, i, h))

    def body(w_ref, g_ref, m_ref, v_ref, *rest):
        go_ref, d_ref, nm_ref, nv_ref = rest[-4:]
        gv = g_ref[...]
        go_ref[...] = gv
        mn = ADAM_B1 * m_ref[...] + (1.0 - ADAM_B1) * gv
        vn = ADAM_B2 * v_ref[...] + (1.0 - ADAM_B2) * (gv * gv)
        m_hat = mn / (1.0 - ADAM_B1 ** ADAM_STEP)
        v_hat = vn / (1.0 - ADAM_B2 ** ADAM_STEP)
        d_ref[...] = -ADAM_LR * (m_hat / (jnp.sqrt(v_hat) + ADAM_EPS) + ADAM_WD * w_ref[...])
        nm_ref[...] = mn
        nv_ref[...] = vn

    in_specs = [at_layer, pl.BlockSpec((None, tr, width), lambda h, i: (h, i, 0)), at_layer, at_layer]
    operands = [w, g, m, v]
    aliases = {}
    if into is not None:
        in_specs += [ANY] * 4
        operands += list(into)
        aliases = {4 + i: i for i in range(4)}
    sds = jax.ShapeDtypeStruct((nl, rows, cols), F32)
    return _ordered_call(
        body, name=name, out_shape=(sds,) * 4, grid=(slabs, rows // tr), in_specs=in_specs, out_specs=(at_layer,) * 4,
        input_output_aliases=aliases, compiler_params=_params(("parallel", "parallel")),
    )(*operands)


def _sum_devices(mine, landed, me_arr, name):
    rows, lanes = mine.shape

    def body(me_ref, mine_ref, landed_ref, o_ref):
        total = None
        for dev in range(8):
            part = jnp.where(me_ref[0] == dev, mine_ref[...], landed_ref[dev])
            total = part if total is None else total + part
        o_ref[...] = total

    grid_spec = pltpu.PrefetchScalarGridSpec(
        num_scalar_prefetch=1, grid=(1,),
        in_specs=[pl.BlockSpec((rows, lanes), lambda i, me_ref: (0, 0)), pl.BlockSpec((8, rows, lanes), lambda i, me_ref: (0, 0, 0))],
        out_specs=pl.BlockSpec((rows, lanes), lambda i, me_ref: (0, 0)))
    return _ordered_call(
        body, name=name, out_shape=jax.ShapeDtypeStruct((rows, lanes), F32), grid_spec=grid_spec,
        compiler_params=_params(("arbitrary",)),
    )(me_arr, mine, landed)


def _place():
    x, y, c = lax.axis_index("x"), lax.axis_index("y"), lax.axis_index("c")
    chips = [(1 - x, y), (x, 1 - y), (1 - x, 1 - y)]
    return x, y, c, chips


HBM = pl.BlockSpec(memory_space=pltpu.HBM)
SEM = pl.BlockSpec(memory_space=pltpu.SEMAPHORE)
TOKEN = jax.ShapeDtypeStruct((8, 128), F32)


def _remote(src, dst, send_sem, recv_sem, to):
    return pltpu.make_async_remote_copy(src_ref=src, dst_ref=dst, send_sem=send_sem, recv_sem=recv_sem, device_id=to,
                                        device_id_type=MESH)


def _split_call(body, name, thru, sems_in=(), fresh=(), new_sems=(), after_last=True):
    n_t, n_s, n_f = len(thru), len(sems_in), len(fresh)

    def call_body(*refs):
        outs = refs[n_t + n_s:]
        body(refs[:n_t], refs[n_t:n_t + n_s], outs[1 + n_t:1 + n_t + n_f], outs[1 + n_t + n_f:])
        outs[0][...] = jnp.zeros_like(outs[0])

    out_shape = ([TOKEN] + [pltpu.HBM(t.shape, t.dtype) for t in thru] + [pltpu.HBM(shp, dt) for shp, dt in fresh]
                 + [pltpu.SemaphoreType.DMA(shp) for shp in new_sems])
    out_specs = [pl.BlockSpec(memory_space=pltpu.VMEM)] + [HBM] * (n_t + n_f) + [SEM] * len(new_sems)
    if not after_last or any(t is _Order.last for t in thru):
        _Order.last = None
    out = _ordered_call(
        call_body, name=name, out_shape=tuple(out_shape), in_specs=[HBM] * n_t + [SEM] * n_s, out_specs=tuple(out_specs),
        input_output_aliases={i: 1 + i for i in range(n_t)},
        compiler_params=pltpu.CompilerParams(has_side_effects=pltpu.SideEffectType.DATAFLOW_SIDE_EFFECTING),
    )(*[pltpu.with_memory_space_constraint(t, pltpu.HBM) for t in thru], *sems_in)
    return out[1:1 + n_t], out[1 + n_t:1 + n_t + n_f], out[1 + n_t + n_f:]


class _Exchange:
    def __init__(self, weights, m_in, v_in, j_arr, c_arr, me_arr):
        self.w, self.m, self.v = weights, m_in, v_in
        self.j_arr, self.c_arr, self.me_arr = j_arr, c_arr, me_arr
        self.adam, self.small, self.pairs, self.held = {}, {}, {}, None
        self.o_arr = 1 - c_arr
        self.groups = [(l, name) for l in range(DEPTH) for name in BIG_NAMES]
        self.shard_shape = {name: weights[name].shape[1:] for name in BIG_NAMES}
        self.conv_state, self.state = [], {}
        self.ready, self.conv_ready = {}, {}
        self.pending, self.tick, self.reduced = [], 0, {}

        def place(grp):
            l, name = grp
            nl, r, cols = weights[name].shape
            return _place_shard(weights[name].reshape(nl, 2, r // 2, cols), l, j_arr, BF16, f"place_{name}_l{l}")

        def start_copies(tag, convs, groups, bufs):
            n_c = len(convs)

            def start(thru, _, __, sems):
                x, y, c, chips = _place()
                j_me = 2 * x + y
                copies = []
                for i in range(len(thru)):
                    mine = thru[i].at[j_me] if i < n_c else thru[i].at[j_me, c]
                    copies += [_remote(mine, mine, sems[2 * i].at[k], sems[2 * i + 1].at[k], (*chip, c))
                               for k, chip in enumerate(chips)]
                for cp in copies:
                    cp.start()

            thru, _, sems = _split_call(start, tag, convs + bufs, new_sems=[(3,)] * (2 * (n_c + len(bufs))))
            self.conv_state += [(thru[i], sems[2 * i], sems[2 * i + 1]) for i in range(n_c)]
            for g, grp in enumerate(groups):
                self.state[grp] = (thru[n_c + g], sems[2 * (n_c + g)], sems[2 * (n_c + g) + 1])

        convs = [_place_shard(weights["conv_w"][:, None], l, j_arr, F32, f"place_conv_w_l{l}") for l in range(DEPTH)]
        start_copies("gather_start_first", convs, self.groups[:1], [place(self.groups[0])])
        start_copies("gather_start_rest", [], self.groups[1:], [place(grp) for grp in self.groups[1:]])

    def conv_w(self, l):
        if l not in self.conv_ready:
            buf, send, recv = self.conv_state[l]

            def wait(thru, sems, _, __):
                x, y, c, chips = _place()
                for k, chip in enumerate(chips):
                    mine, theirs = thru[0].at[2 * x + y], thru[0].at[2 * chip[0] + chip[1]]
                    _remote(mine, mine, sems[0].at[k], sems[1].at[k], (*chip, c)).wait_send()
                    _remote(theirs, theirs, sems[0].at[k], sems[1].at[k], (x, y, c)).wait_recv()

            (buf,), _, _ = _split_call(wait, f"gather_conv_w_l{l}", [buf], sems_in=[send, recv])
            self.conv_ready[l] = jnp.transpose(buf[:, 0], (1, 0, 2)).reshape(3, 2 * D_FF)
        return self.conv_ready[l]

    def weight(self, l, name):
        grp = (l, name)
        if grp not in self.ready:
            buf, send, recv = self.state[grp]

            def forward(thru, sems, _, new):
                x, y, c, chips = _place()
                for k, chip in enumerate(chips):
                    landed = thru[0].at[2 * chip[0] + chip[1], c]
                    _remote(landed, landed, new[0].at[k], sems[0].at[k], (x, y, c)).wait_recv()
                    _remote(landed, landed, new[0].at[k], new[1].at[k], (x, y, 1 - c)).start()

            (buf,), _, (fsend, frecv) = _split_call(forward, f"gather_pass_{name}_l{l}", [buf], sems_in=[recv],
                                                    new_sems=[(3,), (3,)])

            def finish(thru, sems, _, __):
                x, y, c, chips = _place()
                mine = thru[0].at[2 * x + y, c]
                for k, chip in enumerate(chips):
                    j_k = 2 * chip[0] + chip[1]
                    theirs, landed = thru[0].at[j_k, 1 - c], thru[0].at[j_k, c]
                    _remote(theirs, theirs, sems[1].at[k], sems[2].at[k], (x, y, c)).wait_recv()
                    _remote(landed, landed, sems[1].at[k], sems[2].at[k], (x, y, 1 - c)).wait_send()
                    _remote(mine, mine, sems[0].at[k], sems[2].at[k], (*chip, c)).wait_send()

            (buf,), _, _ = _split_call(finish, f"gather_done_{name}_l{l}", [buf], sems_in=[send, fsend, frecv])
            r, cols = self.shard_shape[name]
            self.ready[grp] = buf.reshape(N_CHIPS, r, cols) if name in ("w_in", "w_up") else buf.reshape(N_CHIPS * r, cols)
        return self.ready[grp]

    def pair_send(self, l, name, other):
        held = self.held
        self.held = None

        def start(thru, _, fresh, sems):
            x, y, c, chips = _place()
            copies = [_remote(thru[0], fresh[0], sems[0], sems[1], (x, y, 1 - c))]
            if held is not None:
                copies += [_remote(thru[1].at[2 * chip[0] + chip[1]], fresh[1].at[k], sems[2].at[k], sems[3].at[k], (*chip, c))
                           for k, chip in enumerate(chips)]
            for cp in copies:
                cp.start()

        thru, fresh, new_sems = [other], [(other.shape, BF16)], [(), ()]
        if held is not None:
            thru, fresh, new_sems = thru + [held[2]], fresh + [((3,) + held[2].shape[1:], BF16)], new_sems + [(3,), (3,)]
        thru, fresh, sems = _split_call(start, f"pair_start_{name}_l{l}", thru, fresh=fresh, new_sems=new_sems, after_last=False)
        self.pairs[(l, name)] = (thru[0], fresh[0], sems[:2])
        if held is not None:
            self.pending.append(dict(l=held[0], name=held[1], stage=2, at=self.tick, bufs=(thru[1], fresh[1]), sems=sems[2:]))

    def pair_recv(self, l, name):
        other, recv, sems = self.pairs.pop((l, name))

        def wait(thru, sems, _, __):
            x, y, c, _chips = _place()
            cp = _remote(thru[0], thru[1], sems[0], sems[1], (x, y, 1 - c))
            cp.wait_send()
            cp.wait_recv()

        (_, recv), _, _ = _split_call(wait, f"pair_done_{name}_l{l}", [other, recv], sems_in=list(sems))
        return recv

    def scatter(self, l, name, p4):
        assert self.held is None
        self.held = (l, name, p4)
        if (l, name) == (0, BIG_NAMES[0]):
            self._scatter_held()

    def _scatter_held(self):
        l, name, p4 = self.held
        self.held = None

        def start(thru, _, fresh, sems):
            x, y, c, chips = _place()
            for k, chip in enumerate(chips):
                _remote(thru[0].at[2 * chip[0] + chip[1]], fresh[0].at[k], sems[0].at[k], sems[1].at[k], (*chip, c)).start()

        (p4,), (recv3,), sems = _split_call(start, f"chips_start_{name}_l{l}", [p4], fresh=[((3,) + p4.shape[1:], BF16)],
                                           new_sems=[(3,), (3,)], after_last=False)
        self.pending.append(dict(l=l, name=name, stage=2, at=self.tick, bufs=(p4, recv3), sems=sems))

    def point(self, drain=False):
        self.tick += 1
        if drain:
            for grp in [g for g in self.pending if g["stage"] in (2, 3)]:
                self._advance([grp] if grp["stage"] == 3 else [], [grp] if grp["stage"] == 2 else [])
        else:
            self._advance([grp for grp in self.pending if grp["stage"] == 3 and grp["at"] < self.tick],
                          [grp for grp in self.pending if grp["stage"] == 2 and grp["at"] + 2 <= self.tick])

    def _advance(self, joined, landed):
        if not joined and not landed:
            return
        n_j, n_l = len(joined), len(landed)

        def wait(thru, sems, _, __):
            x, y, c, chips = _place()
            for i in range(n_j):
                buf, send, recv = thru[i], sems[2 * i], sems[2 * i + 1]
                _remote(buf.at[c], buf.at[c], send, recv, (x, y, 1 - c)).wait_send()
                _remote(buf.at[1 - c], buf.at[1 - c], send, recv, (x, y, c)).wait_recv()
            for i in range(n_l):
                p4, recv3 = thru[n_j + 2 * i], thru[n_j + 2 * i + 1]
                send, recv = sems[2 * (n_j + i)], sems[2 * (n_j + i) + 1]
                for k, chip in enumerate(chips):
                    cp = _remote(p4.at[2 * chip[0] + chip[1]], recv3.at[k], send.at[k], recv.at[k], (*chip, c))
                    cp.wait_send()
                    cp.wait_recv()

        tag = "_".join([f"{grp['name']}{grp['l']}_halves" for grp in joined] + [f"{grp['name']}{grp['l']}_chips" for grp in landed])
        bufs, _, _ = _split_call(wait, f"landed_{tag}", [b for grp in joined + landed for b in grp["bufs"]],
                                 sems_in=[sm for grp in joined + landed for sm in grp["sems"]])
        for i, grp in enumerate(joined):
            l, name, full = grp["l"], grp["name"], bufs[i]
            if GRAD_HALVES[name][0] != "cols_of_block":
                full = full.reshape((1,) + tuple(self.shard_shape[name]))
            self.adam[name] = _adamw_layer(self.w[name], full, self.m[name], self.v[name], l, self.adam.get(name),
                                           f"adamw_{name}_l{l}")
            grp.update(stage=4)
        if not landed:
            return
        halves = [_chip_sum(bufs[n_j + 2 * i], bufs[n_j + 2 * i + 1], self.j_arr, self.c_arr,
                            f"chip_sum_{grp['name']}_l{grp['l']}") for i, grp in enumerate(landed)]

        def start(thru, _, __, sems):
            x, y, c, _chips = _place()
            for i in range(n_l):
                _remote(thru[i].at[c], thru[i].at[c], sems[2 * i], sems[2 * i + 1], (x, y, 1 - c)).start()

        tag = "_".join(f"{grp['name']}{grp['l']}" for grp in landed)
        halves, _, sems = _split_call(start, f"join_start_{tag}", halves, new_sems=[()] * (2 * n_l), after_last=False)
        for i, grp in enumerate(landed):
            grp.update(stage=3, at=self.tick, bufs=(halves[i],), sems=tuple(sems[2 * i:2 * i + 2]))

    def finish(self):
        if self.held is not None:
            self._scatter_held()
        while any(grp["stage"] < 4 for grp in self.pending):
            self.point(drain=True)
        return self.adam

    @staticmethod
    def _peer(k, x, y, c):
        return (1 - x if k & 4 else x, 1 - y if k & 2 else y, 1 - c if k & 1 else c)

    def small_grads(self, l, grads, loss_tile):
        parts = [grads[nm] for nm in SMALL_NAMES] + ([loss_tile[0, 0:1]] if loss_tile is not None else [])
        packed = _pack_call(parts, f"small_pack_l{l}")
        rows = packed.shape[0]

        def start(thru, _, fresh, sems):
            x, y, c, _chips = _place()
            for k in range(1, 8):
                _remote(thru[0], fresh[0].at[4 * x + 2 * y + c], sems[0].at[k - 1], sems[1].at[k - 1],
                        self._peer(k, x, y, c)).start()

        (packed,), (landed,), sems = _split_call(start, f"small_start_l{l}", [packed], fresh=[((8, rows, PACK_LANES), F32)],
                                                 new_sems=[(7,), (7,)], after_last=False)
        self.small[l] =(packed, landed, sems, [p.shape for p in parts])

    def small_sum(self, l):
        packed, landed, sems, _shapes = self.small[l]

        def wait(thru, sems, _, __):
            x, y, c, _chips = _place()
            for k in range(1, 8):
                px, py, pc = self._peer(k, x, y, c)
                _remote(thru[0], thru[1].at[4 * x + 2 * y + c], sems[0].at[k - 1], sems[1].at[k - 1], (px, py, pc)).wait_send()
                _remote(thru[0], thru[1].at[4 * px + 2 * py + pc], sems[0].at[k - 1], sems[1].at[k - 1], (x, y, c)).wait_recv()

        (packed, landed), _, _ = _split_call(wait, f"small_done_l{l}", [packed, landed], sems_in=list(sems))
        return _sum_devices(packed, landed, self.me_arr, f"small_sum_l{l}")


def _rope_tables(s):
    inv_freq = ROPE_THETA ** (-jnp.arange(0, HEAD_DIM, 2, dtype=F32) / HEAD_DIM)
    ang = jnp.arange(s, dtype=F32)[:, None] * inv_freq[None, :]
    cos, sin = jnp.cos(ang), jnp.sin(ang)
    return jnp.concatenate([cos, cos], axis=-1), jnp.concatenate([-sin, sin], axis=-1)


def _local_step(x, target, ex, small):
    s = x.shape[0]
    cosf, sinf = _rope_tables(s)
    saved = []
    for l in range(DEPTH):
        p = small[l]
        t = f"l{l}"
        h = _rms_fwd(x, p["norm1_g"], f"norm1_{t}")
        z = _matmul(h, ex.weight(l, "w_in"), mode="nn", out_dtype=BF16, tm=1024, tn=896, tk=2048, b_parts=4, name=f"proj_in_{t}")
        qn, kn, vb, ug, vn, *gate_kept = _proj_post(z, p["q_norm_g"], p["k_norm_g"], p["sgu_ln_g"], p["sgu_ln_b"], cosf, sinf,
                                                    f"proj_post_{t}")
        attn, sgu, mixed, probs, psink = _mixer_fwd(qn, kn, vb, ug, vn, p["w_s_bf16"], p["b_s_tile"], p["sink"],
                                                    p["attn_out_g"], p["sgu_out_g"], f"mixer_{t}")
        x1 = _matmul(mixed, ex.weight(l, "w_o"), mode="nn", out_dtype=F32, tm=2048, tn=256, tk=2048, res=x,
                     name=f"proj_out_{t}")
        h2 = _rms_fwd(x1, p["norm2_g"], f"norm2_{t}")
        a_pre = _matmul(h2, ex.weight(l, "w_up"), mode="nn", out_dtype=BF16, tm=1024, tn=1408, tk=2048, b_parts=4,
                        out_parts=2,
                        name=f"ffn_up_{t}")
        act, dgu = _conv_gate_fwd(a_pre, ex.conv_w(l), p["conv_b"], f"conv_gate_{t}")
        x2 = _matmul(act, ex.weight(l, "w_down"), mode="nn", out_dtype=F32, tm=1024, tn=256, tk=D_FF, res=x1,
                     name=f"ffn_down_{t}")
        saved.append(dict(x=x, h=h, z=z, qn=qn, kn=kn, vb=vb, ug=ug, vn=vn, attn=attn, sgu=sgu, mixed=mixed, x1=x1, h2=h2,
                          a_pre=a_pre, act=act, dgu=dgu, probs=probs, psink=psink, gate_kept=gate_kept))
        x = x2
    loss_tile, dx, dxb = _loss_head(x, target, "loss_head")
    for l in reversed(range(DEPTH)):
        p, sv = small[l], saved[l]
        t = f"l{l}"
        def weight_grad(name, a, g, between, g_parts=0):
            ex.pair_send(l, name, _grad_half(name, a, g, ex.o_arr, None, f"g_{name}_other_{t}", g_parts))
            out = between()
            ex.scatter(l, name, _grad_half(name, a, g, ex.c_arr, ex.pair_recv(l, name), f"g_{name}_own_{t}", g_parts))
            ex.point()
            return out

        def after_down():
            dact = _matmul(dxb, ex.weight(l, "w_down"), mode="nt", out_dtype=BF16, tm=1024, tn=512, tk=2048,
                           name=f"d_act_{t}")
            return _conv_gate_bwd(sv["a_pre"], sv["dgu"], ex.conv_w(l), dact, f"conv_gate_bwd_{t}")

        dap, dcw, dcb = weight_grad("w_down", sv["act"], dxb, after_down)

        def after_up():
            dh2 = _matmul(dap, ex.weight(l, "w_up"), mode="nt", out_dtype=F32, tm=1024, tn=1024, tk=2816, a_parts=2,
                          b_parts=4, name=f"d_h2_{t}")
            return _rms_bwd(sv["x1"], p["norm2_g"], dh2, dx, f"norm2_bwd_{t}")

        dx1, dx1b, dg2 = weight_grad("w_up", sv["h2"], dap, after_up, g_parts=2)
        ex.pair_send(l, "w_o", _grad_half("w_o", sv["mixed"], dx1b, ex.o_arr, None, f"g_w_o_other_{t}"))
        dmixed = _matmul(dx1b, ex.weight(l, "w_o"), mode="nt", out_dtype=F32, tm=1024, tn=512, tk=2048,
                         name=f"d_mixed_{t}")
        dqn, dkn, dvb, dug, dvn, dws, dbs, dsk, dga, dgs = _mixer_bwd(
            sv["qn"], sv["kn"], sv["vb"], sv["ug"], sv["vn"], sv["attn"], sv["sgu"], dmixed, p["w_s_bf16"], p["b_s_tile"],
            p["attn_out_g"], p["sgu_out_g"], sv["probs"], sv["psink"], f"mixer_bwd_{t}")
        dz, dqg, dkg, dlg, dlb = _proj_post_bwd(sv["z"], dqn, dkn, dvb, dug, dvn, *sv["gate_kept"], p["q_norm_g"], p["k_norm_g"],
                                                 p["sgu_ln_g"], cosf, sinf, f"proj_post_bwd_{t}")
        ex.scatter(l, "w_o", _grad_half("w_o", sv["mixed"], dx1b, ex.c_arr, ex.pair_recv(l, "w_o"), f"g_w_o_own_{t}"))
        ex.point()

        def after_in():
            dh = _matmul_nt_slabs(dz, ex.weight(l, "w_in"), tm=1024, tn=512, name=f"d_h_{t}")
            return _rms_bwd(sv["x"], p["norm1_g"], dh, dx1, f"norm1_bwd_{t}")

        dx, dxb, dg1 = weight_grad("w_in", sv["h"], dz, after_in)
        ex.small_grads(l, dict(
            norm1_g=dg1[0], q_norm_g=dqg[0], k_norm_g=dkg[0], sink=dsk[:, 0], sgu_ln_g=dlg[0], sgu_ln_b=dlb[0], w_s=dws,
            b_s=dbs[:, :, 0], attn_out_g=dga[0], sgu_out_g=dgs[0], norm2_g=dg2[0],
            conv_w=jnp.concatenate([dcw[0], dcw[1]], axis=-1), conv_b=jnp.concatenate([dcb[0, 0], dcb[1, 0]], axis=-1)),
            loss_tile if l == 0 else None)
    return dx


def _small_views(l, norm1_g, q_norm_g, k_norm_g, sink, sgu_ln_g, sgu_ln_b, w_s, b_s, attn_out_g, sgu_out_g, norm2_g, conv_b):
    return dict(
        norm1_g=norm1_g[l][None], q_norm_g=q_norm_g[l][None], k_norm_g=k_norm_g[l][None], sink=sink[l],
        sgu_ln_g=sgu_ln_g[l][None], sgu_ln_b=sgu_ln_b[l][None], w_s_bf16=w_s[l].astype(BF16),
        b_s_tile=jnp.broadcast_to(b_s[l][:, :, None], (N_GMLP_HEADS, BLOCK, BLOCK)), attn_out_g=attn_out_g[l][None],
        sgu_out_g=sgu_out_g[l][None], norm2_g=norm2_g[l][None], conv_b=conv_b[l][None])


SMALL_NAMES = ("norm1_g", "q_norm_g", "k_norm_g", "sink", "sgu_ln_g", "sgu_ln_b", "w_s", "b_s", "attn_out_g", "sgu_out_g",
               "norm2_g", "conv_b", "conv_w")
REPLICATED_NAMES = SMALL_NAMES[:-1]
BIG_NAMES = ("w_in", "w_o", "w_up", "w_down")
PACK_LANES = 128
PACK_ALIGN = 8 * PACK_LANES


def _pack_rows(shape):
    return -(-math.prod(shape) // PACK_ALIGN) * 8


def _pack_parts(arrays):
    parts = []
    for a in arrays:
        flat = a.reshape(-1)
        parts.append(jnp.pad(flat, (0, _pack_rows(a.shape) * PACK_LANES - flat.shape[0])).reshape(-1, PACK_LANES))
    return parts


def _pack_call(arrays, name):
    parts = _pack_parts(arrays)
    total = sum(p.shape[0] for p in parts)

    def body(*refs):
        o_ref, at = refs[-1], 0
        for p_ref in refs[:-1]:
            o_ref[at:at + p_ref.shape[0], :] = p_ref[...]
            at += p_ref.shape[0]

    vm = pl.BlockSpec(memory_space=pltpu.VMEM)
    return _ordered_call(
        body, name=name, out_shape=jax.ShapeDtypeStruct((total, PACK_LANES), F32), in_specs=[vm] * len(parts), out_specs=vm,
        compiler_params=pltpu.CompilerParams(vmem_limit_bytes=V7X_VMEM_LIMIT),
    )(*parts)


def _unpack_layers(stacked, shapes):
    nl = stacked.shape[0]
    out, at = [], 0
    for shp in shapes:
        rows = _pack_rows(shp)
        out.append(stacked[:, at:at + rows].reshape(nl, -1)[:, :math.prod(shp)].reshape((nl,) + tuple(shp)))
        at += rows
    return out


def _adamw_packed(w, g, m, v, rows, layer, into, name):
    head = pl.BlockSpec((rows, PACK_LANES), lambda i: (0, 0))
    at_layer = pl.BlockSpec((None, rows, PACK_LANES), lambda i: (layer, 0, 0))

    def body(w_ref, g_ref, m_ref, v_ref, *rest):
        d_ref, nm_ref, nv_ref = rest[-3:]
        gv = g_ref[...]
        mn = ADAM_B1 * m_ref[...] + (1.0 - ADAM_B1) * gv
        vn = ADAM_B2 * v_ref[...] + (1.0 - ADAM_B2) * (gv * gv)
        m_hat = mn / (1.0 - ADAM_B1 ** ADAM_STEP)
        v_hat = vn / (1.0 - ADAM_B2 ** ADAM_STEP)
        d_ref[...] = -ADAM_LR * (m_hat / (jnp.sqrt(v_hat) + ADAM_EPS) + ADAM_WD * w_ref[...])
        nm_ref[...] = mn
        nv_ref[...] = vn

    in_specs = [head] * 4
    operands = [w, g, m, v]
    aliases = {}
    if into is not None:
        in_specs += [ANY] * 3
        operands += list(into)
        aliases = {4 + i: i for i in range(3)}
    sds = jax.ShapeDtypeStruct((DEPTH, rows, PACK_LANES), F32)
    return _ordered_call(
        body, name=name, out_shape=(sds,) * 3, grid=(1,), in_specs=in_specs, out_specs=(at_layer,) * 3,
        input_output_aliases=aliases, compiler_params=_params(("arbitrary",)),
    )(*operands)


def kernel(x, norm1_g, w_in, q_norm_g, k_norm_g, sink, sgu_ln_g, sgu_ln_b, w_s, b_s, attn_out_g, sgu_out_g, w_o, norm2_g, w_up, conv_w, conv_b, w_down, loss_target, m_norm1_g, m_w_in, m_q_norm_g, m_k_norm_g, m_sink, m_sgu_ln_g, m_sgu_ln_b, m_w_s, m_b_s, m_attn_out_g, m_sgu_out_g, m_w_o, m_norm2_g, m_w_up, m_conv_w, m_conv_b, m_w_down, v_norm1_g, v_w_in, v_q_norm_g, v_k_norm_g, v_sink, v_sgu_ln_g, v_sgu_ln_b, v_w_s, v_b_s, v_attn_out_g, v_sgu_out_g, v_w_o, v_norm2_g, v_w_up, v_conv_w, v_conv_b, v_w_down):
    weights = dict(norm1_g=norm1_g, w_in=w_in, q_norm_g=q_norm_g, k_norm_g=k_norm_g, sink=sink, sgu_ln_g=sgu_ln_g,
                   sgu_ln_b=sgu_ln_b, w_s=w_s, b_s=b_s, attn_out_g=attn_out_g, sgu_out_g=sgu_out_g, w_o=w_o, norm2_g=norm2_g,
                   w_up=w_up, conv_w=conv_w, conv_b=conv_b, w_down=w_down)
    m_in = dict(norm1_g=m_norm1_g, w_in=m_w_in, q_norm_g=m_q_norm_g, k_norm_g=m_k_norm_g, sink=m_sink, sgu_ln_g=m_sgu_ln_g,
                sgu_ln_b=m_sgu_ln_b, w_s=m_w_s, b_s=m_b_s, attn_out_g=m_attn_out_g, sgu_out_g=m_sgu_out_g, w_o=m_w_o,
                norm2_g=m_norm2_g, w_up=m_w_up, conv_w=m_conv_w, conv_b=m_conv_b, w_down=m_w_down)
    v_in = dict(norm1_g=v_norm1_g, w_in=v_w_in, q_norm_g=v_q_norm_g, k_norm_g=v_k_norm_g, sink=v_sink, sgu_ln_g=v_sgu_ln_g,
                sgu_ln_b=v_sgu_ln_b, w_s=v_w_s, b_s=v_b_s, attn_out_g=v_attn_out_g, sgu_out_g=v_sgu_out_g, w_o=v_w_o,
                norm2_g=v_norm2_g, w_up=v_w_up, conv_w=v_conv_w, conv_b=v_conv_b, w_down=v_w_down)
    cx, cy, cc = lax.axis_index("x"), lax.axis_index("y"), lax.axis_index("c")
    j_me = 2 * cx + cy
    c_arr = jnp.reshape(cc, (1,)).astype(jnp.int32)
    j_arr = jnp.reshape(j_me, (1,)).astype(jnp.int32)

    _Order.last = None
    ex = _Exchange(weights, m_in, v_in, j_arr, c_arr, jnp.reshape(4 * cx + 2 * cy + cc, (1,)).astype(jnp.int32))
    small = [_small_views(l, norm1_g, q_norm_g, k_norm_g, sink, sgu_ln_g, sgu_ln_b, w_s, b_s, attn_out_g, sgu_out_g, norm2_g,
                          conv_b) for l in range(DEPTH)]
    held_back, _ = lax.optimization_barrier(([[src[nm] for nm in REPLICATED_NAMES] for src in (weights, m_in, v_in)], _Order.last))
    packed_in = [[_pack_call([arr[l] for arr in arrays], f"pack_{tag}_l{l}") for tag, arrays in zip("wmv", held_back)]
                 for l in range(DEPTH)]
    dx = _local_step(x[0], loss_target[0], ex, small)
    big_out = ex.finish()

    rep_shapes = [weights[nm].shape[1:] for nm in REPLICATED_NAMES]
    rep_rows = sum(_pack_rows(shp) for shp in rep_shapes)
    cw_shape = (3, 2 * D_FF)
    sums, adam_small = [None] * DEPTH, None
    for l in reversed(range(DEPTH)):
        sums[l] = ex.small_sum(l)
        pw, pm, pv = packed_in[l]
        adam_small = _adamw_packed(pw, sums[l], pm, pv, rep_rows, l, adam_small, f"adamw_small_l{l}")
    cw_rows = _pack_rows(cw_shape)
    loss = sums[0][rep_rows + cw_rows, 0]
    stacked = jnp.stack([sm[:rep_rows + cw_rows] for sm in sums])
    grads = dict(zip(REPLICATED_NAMES, _unpack_layers(stacked[:, :rep_rows], rep_shapes)))
    delta, new_m, new_v = (dict(zip(REPLICATED_NAMES, _unpack_layers(arr, rep_shapes))) for arr in adam_small)
    cw_cols = 2 * D_FF // N_CHIPS
    cw_grad = lax.dynamic_slice_in_dim(_unpack_layers(stacked[:, rep_rows:], [cw_shape])[0], j_me * cw_cols, cw_cols, axis=2)
    flat = lambda a: a.reshape(DEPTH * 3, cw_cols)
    cw_out = _adamw(flat(conv_w), flat(cw_grad), flat(m_conv_w), flat(v_conv_w), "adamw_conv_w")
    grads["conv_w"], delta["conv_w"], new_m["conv_w"], new_v["conv_w"] = (a.reshape(DEPTH, 3, cw_cols) for a in cw_out)

    for name in BIG_NAMES:
        grads[name], delta[name], new_m[name], new_v[name] = big_out[name]

    order = ("norm1_g", "w_in", "q_norm_g", "k_norm_g", "sink", "sgu_ln_g", "sgu_ln_b", "w_s", "b_s", "attn_out_g", "sgu_out_g",
             "w_o", "norm2_g", "w_up", "conv_w", "conv_b", "w_down")
    return (loss, dx[None], *[grads[nm] for nm in order], *[delta[nm] for nm in order], *[new_m[nm] for nm in order],
            *[new_v[nm] for nm in order])
```

```python
import math

import jax
import jax.numpy as jnp
from jax import lax
from jax.experimental import pallas as pl
from jax.experimental.pallas import tpu as pltpu

F32 = jnp.float32
BF16 = jnp.bfloat16

D_MODEL = 2048
HEAD_DIM = 128
ATTN_WIDTH = 1024
N_Q_HEADS = 8
N_KV_HEADS = 2
GQA_GROUP = 4
KV_WIDTH = 256
GMLP_WIDTH = 1024
N_GMLP_HEADS = 8
BLOCK = 128
IN_WIDTH = 3584
D_FF = 5632
DEPTH = 2
EPS = 1e-6
MASK_VALUE = -1e30
ROPE_THETA = 10000.0
N_CHIPS = 4

ADAM_LR = 0.001
ADAM_B1 = 0.9
ADAM_B2 = 0.999
ADAM_EPS = 1e-08
ADAM_WD = 0.01
ADAM_STEP = 10

V7X_VMEM_LIMIT = 48 * 1024 * 1024
MESH = pl.DeviceIdType.MESH

_GELU_C = math.sqrt(2.0 / math.pi)
_GELU_A = 0.044715


def _params(sem=None):
    return pltpu.CompilerParams(dimension_semantics=sem, vmem_limit_bytes=V7X_VMEM_LIMIT)


ANY = pl.BlockSpec(memory_space=pl.ANY)


class _Order:
    last = None


def _ordered_call(body, *, token_index=0, **kw):
    def run(*operands):
        tok = _Order.last
        if tok is None or any(op is tok for op in operands):
            call = pl.pallas_call(body, **kw)
        else:
            n_in = len(operands)

            def ordered_body(*refs):
                return body(*refs[:n_in], *refs[n_in + 1:])

            kw2 = dict(kw)
            if "grid_spec" in kw2:
                gs = kw2["grid_spec"]
                kw2["grid_spec"] = pltpu.PrefetchScalarGridSpec(
                    num_scalar_prefetch=gs.num_scalar_prefetch, grid=gs.grid, in_specs=list(gs.in_specs) + [ANY],
                    out_specs=gs.out_specs, scratch_shapes=gs.scratch_shapes)
            else:
                kw2["in_specs"] = list(kw2["in_specs"]) + [ANY]
            call = pl.pallas_call(ordered_body, **kw2)
            operands = operands + (tok,)
        out = call(*operands)
        _Order.last = out[token_index] if isinstance(out, (tuple, list)) else out
        return out

    return run


def _gelu(x):
    return x * (0.5 * (1.0 + jnp.tanh(_GELU_C * (x + _GELU_A * (x * x * x)))))


def _gelu_grad(x):
    x2 = x * x
    t = jnp.tanh(_GELU_C * (x + _GELU_A * (x * x2)))
    return 0.5 * (1.0 + t) + 0.5 * x * (1.0 - t * t) * (_GELU_C * (1.0 + 3.0 * _GELU_A * x2))


def _mean_last(x):
    return jnp.mean(x, axis=-1, keepdims=True)


def _sum_rows(x):
    return jnp.sum(x, axis=0, keepdims=True)


def _sum_all(x):
    return jnp.sum(jnp.sum(x, axis=1, keepdims=True), axis=0, keepdims=True)


def _matmul(a, b, *, mode, out_dtype, tm, tn, tk, name, res=None, a_parts=0, b_parts=0, out_parts=0):
    assert mode in ("nn", "nt"), mode
    if mode == "nn":
        assert not a_parts
        m, k = a.shape
        n = b.shape[0] * b.shape[2] if b_parts else b.shape[1]
    else:
        m, k = (a.shape[1], a.shape[0] * a.shape[2]) if a_parts else a.shape
        n = b.shape[1] if b_parts else b.shape[0]
    tm, tn, tk = min(tm, m), min(tn, n), min(tk, k)
    assert m % tm == 0 and n % tn == 0 and k % tk == 0, (name, m, n, k, tm, tn, tk)
    nm, nn, nk = m // tm, n // tn, k // tk

    def slab(idx, total_tiles, parts):
        per = total_tiles // parts
        assert per * parts == total_tiles, (name, total_tiles, parts)
        return idx // per, idx % per

    if mode == "nn":
        a_spec = pl.BlockSpec((tm, tk), lambda i, j, kk: (i, kk))
        if b_parts:
            b_spec = pl.BlockSpec((None, tk, tn), lambda i, j, kk: (slab(j, nn, b_parts)[0], kk, slab(j, nn, b_parts)[1]))
        else:
            b_spec = pl.BlockSpec((tk, tn), lambda i, j, kk: (kk, j))
        dims = (((1,), (0,)), ((), ()))
    else:
        if a_parts:
            a_spec = pl.BlockSpec((None, tm, tk), lambda i, j, kk: (slab(kk, nk, a_parts)[0], i, slab(kk, nk, a_parts)[1]))
        else:
            a_spec = pl.BlockSpec((tm, tk), lambda i, j, kk: (i, kk))
        if b_parts:
            b_spec = pl.BlockSpec((None, tn, tk), lambda i, j, kk: (slab(kk, nk, b_parts)[0], j, slab(kk, nk, b_parts)[1]))
        else:
            b_spec = pl.BlockSpec((tn, tk), lambda i, j, kk: (j, kk))
        dims = (((1,), (1,)), ((), ()))
    if out_parts:
        out_shape = jax.ShapeDtypeStruct((out_parts, m, n // out_parts), out_dtype)
        out_spec = pl.BlockSpec((None, tm, tn), lambda i, j, kk: (slab(j, nn, out_parts)[0], i, slab(j, nn, out_parts)[1]))
    else:
        out_shape = jax.ShapeDtypeStruct((m, n), out_dtype)
        out_spec = pl.BlockSpec((tm, tn), lambda i, j, kk: (i, j))
    in_specs = [a_spec, b_spec]
    operands = [a, b]
    if res is not None:
        in_specs.append(pl.BlockSpec((tm, tn), lambda i, j, kk: (i, j)))
        operands.append(res)

    def body(*refs):
        a_ref, b_ref = refs[0], refs[1]
        res_ref = refs[2] if res is not None else None
        o_ref = refs[3] if res is not None else refs[2]
        p = lax.dot_general(a_ref[...], b_ref[...], dims, preferred_element_type=F32)

        def finish(total):
            if res_ref is not None:
                total = res_ref[...] + total
            o_ref[...] = total.astype(out_dtype)

        if nk == 1:
            finish(p)
        else:
            acc_ref = refs[-1]
            kk = pl.program_id(2)

            @pl.when(kk == 0)
            def _():
                acc_ref[...] = p

            @pl.when(jnp.logical_and(kk > 0, kk < nk - 1))
            def _():
                acc_ref[...] += p

            @pl.when(kk == nk - 1)
            def _():
                finish(acc_ref[...] + p)

    scratch = [pltpu.VMEM((tm, tn), F32)] if nk > 1 else []
    return _ordered_call(
        body, name=name, out_shape=out_shape, grid=(nm, nn, nk), in_specs=in_specs, out_specs=out_spec,
        scratch_shapes=scratch, compiler_params=_params(("parallel", "parallel", "arbitrary")),
    )(*operands)


def _matmul_nt_slabs(a, b, *, tm, tn, name, a_parts=0):
    nslab, n, ks = b.shape
    m = a.shape[1] if a_parts else a.shape[0]
    tm, tn = min(tm, m), min(tn, n)
    assert m % tm == 0 and n % tn == 0, (name, m, n, tm, tn)
    if a_parts:
        per = nslab // a_parts
        assert per * a_parts == nslab and a.shape[2] == per * ks, (name, a.shape, b.shape)
        a_spec = pl.BlockSpec((a_parts, tm, per * ks), lambda i, j: (0, i, 0))
    else:
        assert a.shape[1] == nslab * ks, (name, a.shape, b.shape)
        a_spec = pl.BlockSpec((tm, nslab * ks), lambda i, j: (i, 0))

    def body(a_ref, b_ref, o_ref):
        total = None
        for sl in range(nslab):
            if a_parts:
                a_sl = a_ref[sl // per, :, (sl % per) * ks:(sl % per + 1) * ks]
            else:
                a_sl = a_ref[:, sl * ks:(sl + 1) * ks]
            p = lax.dot_general(a_sl, b_ref[sl], (((1,), (1,)), ((), ())), preferred_element_type=F32)
            total = p if total is None else total + p
        o_ref[...] = total

    return _ordered_call(
        body, name=name, out_shape=jax.ShapeDtypeStruct((m, n), F32), grid=(m // tm, n // tn),
        in_specs=[a_spec, pl.BlockSpec((nslab, tn, ks), lambda i, j: (0, j, 0))],
        out_specs=pl.BlockSpec((tm, tn), lambda i, j: (i, j)), compiler_params=_params(("parallel", "parallel")),
    )(a, b)


GRAD_HALVES = {
    "w_in": ("rows_of_slab", 1024, 896), "w_up": ("rows_of_slab", 1024, 1408), "w_o": ("rows_of_block", 256, 2048),
    "w_down": ("cols_of_block", 1408, 512)}


def _half_shape(name, shard_shape):
    r, cols = shard_shape
    return (r, cols // 2) if GRAD_HALVES[name][0] == "cols_of_block" else (r // 2, cols)


def _grad_half(name, a, g, sel, res, call_name, g_parts=0):
    kind, tm, tn = GRAD_HALVES[name]
    s, m = a.shape
    n = g.shape[0] * g.shape[2] if g_parts else g.shape[1]
    if kind == "rows_of_slab":
        rh, hc = m // 2, n // N_CHIPS
        per = hc // tn
        grid = (rh // tm, n // tn)
        a_map = lambda i, j, sel_ref: (0, sel_ref[0] * (rh // tm) + i)
        g_col = lambda i, j, sel_ref: j
        o_map = lambda i, j, sel_ref: (j // per, i, j % per)
    elif kind == "rows_of_block":
        rh, hc = m // N_CHIPS // 2, n
        assert tm == rh
        grid = (N_CHIPS, n // tn)
        a_map = lambda i, j, sel_ref: (0, 2 * i + sel_ref[0])
        g_col = lambda i, j, sel_ref: j
        o_map = lambda i, j, sel_ref: (i, 0, j)
    else:
        rh, hc = m // N_CHIPS, n // 2
        assert tm == rh
        grid = (N_CHIPS, hc // tn)
        a_map = lambda i, j, sel_ref: (0, i)
        g_col = lambda i, j, sel_ref: sel_ref[0] * (hc // tn) + j
        o_map = lambda i, j, sel_ref: (i, 0, j)
    if g_parts:
        g_per = (n // tn) // g_parts
        g_spec = pl.BlockSpec((None, s, tn), lambda i, j, sel_ref: (g_col(i, j, sel_ref) // g_per, 0, g_col(i, j, sel_ref) % g_per))
    else:
        g_spec = pl.BlockSpec((s, tn), lambda i, j, sel_ref: (0, g_col(i, j, sel_ref)))
    o_spec = pl.BlockSpec((None, tm, tn), o_map)
    in_specs = [pl.BlockSpec((s, tm), a_map), g_spec] + ([o_spec] if res is not None else [])

    def body(sel_ref, a_ref, g_ref, *rest):
        o_ref = rest[-1]
        p = lax.dot_general(a_ref[...], g_ref[...], (((0,), (0,)), ((), ())), preferred_element_type=F32)
        if res is not None:
            p = p + rest[0][...].astype(F32)
        o_ref[...] = p.astype(BF16)

    grid_spec = pltpu.PrefetchScalarGridSpec(num_scalar_prefetch=1, grid=grid, in_specs=in_specs, out_specs=o_spec)
    return _ordered_call(
        body, name=call_name, out_shape=jax.ShapeDtypeStruct((N_CHIPS, rh, hc), BF16), grid_spec=grid_spec,
        compiler_params=_params(("parallel", "parallel")),
    )(sel, a, g, *([res] if res is not None else []))


def _row_tile(s):
    return min(256, s)


def _rows(width, tr):
    return pl.BlockSpec((tr, width), lambda i: (i, 0))


def _const2(shape):
    return pl.BlockSpec(shape, lambda i: (0, 0))


def _rms_fwd(x, g, name):
    s, d = x.shape
    tr = _row_tile(s)

    def body(x_ref, g_ref, o_ref):
        xv = x_ref[...]
        r = lax.rsqrt(_mean_last(xv * xv) + EPS)
        o_ref[...] = (xv * r * g_ref[...]).astype(BF16)

    return _ordered_call(
        body, name=name, out_shape=jax.ShapeDtypeStruct((s, d), BF16), grid=(s // tr,),
        in_specs=[_rows(d, tr), _const2((1, d))], out_specs=_rows(d, tr), compiler_params=_params(("parallel",)),
    )(x, g)


def _rms_bwd(x, g, dh, dres, name):
    s, d = x.shape
    tr = _row_tile(s)

    def body(x_ref, g_ref, dh_ref, dres_ref, dx_ref, dxb_ref, dg_ref):
        xv, dy = x_ref[...], dh_ref[...]
        r = lax.rsqrt(_mean_last(xv * xv) + EPS)
        gdy = dy * g_ref[...]
        dx = dres_ref[...] + r * gdy - xv * ((r * r * r) * _mean_last(xv * gdy))
        dx_ref[...] = dx
        dxb_ref[...] = dx.astype(BF16)

        @pl.when(pl.program_id(0) == 0)
        def _():
            dg_ref[...] = jnp.zeros_like(dg_ref)

        dg_ref[...] += _sum_rows(xv * r * dy)

    return _ordered_call(
        body, name=name,
        out_shape=(jax.ShapeDtypeStruct((s, d), F32), jax.ShapeDtypeStruct((s, d), BF16), jax.ShapeDtypeStruct((1, d), F32)),
        grid=(s // tr,), in_specs=[_rows(d, tr), _const2((1, d)), _rows(d, tr), _rows(d, tr)],
        out_specs=(_rows(d, tr), _rows(d, tr), _const2((1, d))), compiler_params=_params(("arbitrary",)),
    )(x, g, dh, dres)


Q0, K0, V0, GU0, GV0 = 0, ATTN_WIDTH, ATTN_WIDTH + KV_WIDTH, ATTN_WIDTH + 2 * KV_WIDTH, ATTN_WIDTH + 2 * KV_WIDTH + GMLP_WIDTH


def _head(h, base=0):
    return slice(base + h * HEAD_DIM, base + (h + 1) * HEAD_DIM)


def _proj_post(z, qg, kg, lg, lb, cosf, sinf, name):
    s = z.shape[0]
    tr = _row_tile(s)

    def body(z_ref, qg_ref, kg_ref, lg_ref, lb_ref, cos_ref, sin_ref, qn_ref, kn_ref, vb_ref, ug_ref, vn_ref,
             dgu_ref, dgv_ref, xhat_ref, rstd_ref):
        cos, sin = cos_ref[...], sin_ref[...]

        def norm_rope(xh, g):
            y = xh * lax.rsqrt(_mean_last(xh * xh) + EPS) * g
            return y * cos + pltpu.roll(y, HEAD_DIM // 2, 1) * sin

        for h in range(N_Q_HEADS):
            qn_ref[:, _head(h)] = norm_rope(z_ref[:, _head(h, Q0)].astype(F32), qg_ref[...]).astype(BF16)
        for h in range(N_KV_HEADS):
            kn_ref[:, _head(h)] = norm_rope(z_ref[:, _head(h, K0)].astype(F32), kg_ref[...]).astype(BF16)
        vb_ref[...] = z_ref[:, V0:GU0]
        gu = z_ref[:, GU0:GV0].astype(F32)
        ug_ref[...] = _gelu(gu)
        dgu_ref[...] = _gelu_grad(gu).astype(BF16)
        gv = z_ref[:, GV0:IN_WIDTH].astype(F32)
        vg = _gelu(gv)
        dgv_ref[...] = _gelu_grad(gv).astype(BF16)
        xc = vg - _mean_last(vg)
        r = lax.rsqrt(_mean_last(xc * xc) + EPS)
        y = xc * r
        xhat_ref[...] = y.astype(BF16)
        rstd_ref[...] = r
        vn_ref[...] = (y * lg_ref[...] + lb_ref[...]).astype(BF16)

    wide = jax.ShapeDtypeStruct((s, GMLP_WIDTH), BF16)
    return _ordered_call(
        body, name=name,
        out_shape=(jax.ShapeDtypeStruct((s, ATTN_WIDTH), BF16), jax.ShapeDtypeStruct((s, KV_WIDTH), BF16),
                   jax.ShapeDtypeStruct((s, KV_WIDTH), BF16), jax.ShapeDtypeStruct((s, GMLP_WIDTH), F32), wide,
                   wide, wide, wide, jax.ShapeDtypeStruct((s, 1), F32)),
        grid=(s // tr,),
        in_specs=[_rows(IN_WIDTH, tr), _const2((1, HEAD_DIM)), _const2((1, HEAD_DIM)), _const2((1, GMLP_WIDTH)),
                  _const2((1, GMLP_WIDTH)), _rows(HEAD_DIM, tr), _rows(HEAD_DIM, tr)],
        out_specs=(_rows(ATTN_WIDTH, tr), _rows(KV_WIDTH, tr), _rows(KV_WIDTH, tr), _rows(GMLP_WIDTH, tr), _rows(GMLP_WIDTH, tr),
                   _rows(GMLP_WIDTH, tr), _rows(GMLP_WIDTH, tr), _rows(GMLP_WIDTH, tr), _rows(1, tr)),
        compiler_params=_params(("parallel",)),
    )(z, qg, kg, lg, lb, cosf, sinf)


def _proj_post_bwd(z, dqn, dkn, dvb, dug, dvn, gelu_grad_u, gelu_grad_v, xhat_v, rstd_v, qg, kg, lg, cosf, sinf, name):
    s = z.shape[0]
    tr = _row_tile(s)

    def body(z_ref, dqn_ref, dkn_ref, dvb_ref, dug_ref, dvn_ref, ggu_ref, ggv_ref, xhat_ref, rstd_ref, qg_ref, kg_ref, lg_ref,
             cos_ref, sin_ref, dz_ref, dqg_ref, dkg_ref, dlg_ref, dlb_ref):
        cos, sin = cos_ref[...], sin_ref[...]

        @pl.when(pl.program_id(0) == 0)
        def _():
            dqg_ref[...] = jnp.zeros_like(dqg_ref)
            dkg_ref[...] = jnp.zeros_like(dkg_ref)
            dlg_ref[...] = jnp.zeros_like(dlg_ref)
            dlb_ref[...] = jnp.zeros_like(dlb_ref)

        def norm_rope_bwd(xh, g, dout):
            dy = dout * cos - pltpu.roll(dout, HEAD_DIM // 2, 1) * sin
            r = lax.rsqrt(_mean_last(xh * xh) + EPS)
            xhat = xh * r
            gdy = dy * g
            return r * (gdy - xhat * _mean_last(xhat * gdy)), _sum_rows(xhat * dy)

        dqg = jnp.zeros((1, HEAD_DIM), F32)
        for h in range(N_Q_HEADS):
            dx, dg = norm_rope_bwd(z_ref[:, _head(h, Q0)].astype(F32), qg_ref[...], dqn_ref[:, _head(h)])
            dz_ref[:, _head(h, Q0)] = dx.astype(BF16)
            dqg = dqg + dg
        dqg_ref[...] += dqg
        dkg = jnp.zeros((1, HEAD_DIM), F32)
        for h in range(N_KV_HEADS):
            dx, dg = norm_rope_bwd(z_ref[:, _head(h, K0)].astype(F32), kg_ref[...], dkn_ref[:, _head(h)])
            dz_ref[:, _head(h, K0)] = dx.astype(BF16)
            dkg = dkg + dg
        dkg_ref[...] += dkg
        dz_ref[:, V0:GU0] = dvb_ref[...].astype(BF16)
        dz_ref[:, GU0:GV0] = (dug_ref[...] * ggu_ref[...].astype(F32)).astype(BF16)
        xhat = xhat_ref[...].astype(F32)
        dvn_v = dvn_ref[...]
        dlg_ref[...] += _sum_rows(xhat * dvn_v)
        dlb_ref[...] += _sum_rows(dvn_v)
        dxh = dvn_v * lg_ref[...]
        dvg = rstd_ref[...] * (dxh - _mean_last(dxh) - xhat * _mean_last(dxh * xhat))
        dz_ref[:, GV0:IN_WIDTH] = (dvg * ggv_ref[...].astype(F32)).astype(BF16)

    return _ordered_call(
        body, name=name,
        out_shape=(jax.ShapeDtypeStruct((s, IN_WIDTH), BF16), jax.ShapeDtypeStruct((1, HEAD_DIM), F32),
                   jax.ShapeDtypeStruct((1, HEAD_DIM), F32), jax.ShapeDtypeStruct((1, GMLP_WIDTH), F32),
                   jax.ShapeDtypeStruct((1, GMLP_WIDTH), F32)),
        grid=(s // tr,),
        in_specs=[_rows(V0, tr), _rows(ATTN_WIDTH, tr), _rows(KV_WIDTH, tr), _rows(KV_WIDTH, tr), _rows(GMLP_WIDTH, tr),
                  _rows(GMLP_WIDTH, tr), _rows(GMLP_WIDTH, tr), _rows(GMLP_WIDTH, tr), _rows(GMLP_WIDTH, tr), _rows(1, tr),
                  _const2((1, HEAD_DIM)), _const2((1, HEAD_DIM)), _const2((1, GMLP_WIDTH)), _rows(HEAD_DIM, tr),
                  _rows(HEAD_DIM, tr)],
        out_specs=(_rows(IN_WIDTH, tr), _const2((1, HEAD_DIM)), _const2((1, HEAD_DIM)), _const2((1, GMLP_WIDTH)),
                   _const2((1, GMLP_WIDTH))),
        compiler_params=_params(("arbitrary",)),
    )(z, dqn, dkn, dvb, dug, dvn, gelu_grad_u, gelu_grad_v, xhat_v, rstd_v, qg, kg, lg, cosf, sinf)


def _band_valid(n, s):
    shape = (GQA_GROUP * BLOCK, 3 * BLOCK)
    i = lax.broadcasted_iota(jnp.int32, shape, 0) & (BLOCK - 1)
    j = lax.broadcasted_iota(jnp.int32, shape, 1)
    k_pos = n * BLOCK - BLOCK + j
    return (jnp.abs(j - BLOCK - i) <= BLOCK) & (k_pos >= 0) & (k_pos < s)


def _group_rows(x, kh):
    return jnp.concatenate([x[:, _head(kh * GQA_GROUP + g)] for g in range(GQA_GROUP)], axis=0)


def _group_sinks(sink_ref, kh):
    return jnp.concatenate([jnp.full((BLOCK, 1), sink_ref[kh * GQA_GROUP + g], F32) for g in range(GQA_GROUP)], axis=0)


def _rows_of(x, g):
    return x[g * BLOCK:(g + 1) * BLOCK]


def _probs(q, kb, sink_h, valid):
    sc = lax.dot_general(q, kb, (((1,), (1,)), ((), ())), preferred_element_type=F32) * (HEAD_DIM ** -0.5)
    sc = jnp.where(valid, sc, MASK_VALUE)
    m = jnp.maximum(jnp.max(sc, axis=-1, keepdims=True), sink_h)
    p = jnp.exp(sc - m)
    es = jnp.exp(sink_h - m)
    den = jnp.sum(p, axis=-1, keepdims=True) + es
    inv = 1.0 / den
    return p * inv, es * inv


def _band_specs(width, nb):
    return [pl.BlockSpec((BLOCK, width), lambda n: (jnp.maximum(n - 1, 0), 0)),
            pl.BlockSpec((BLOCK, width), lambda n: (n, 0)),
            pl.BlockSpec((BLOCK, width), lambda n: (jnp.minimum(n + 1, nb - 1), 0))]


def _blk(width):
    return pl.BlockSpec((BLOCK, width), lambda n: (n, 0))


def _whole3(shape):
    return pl.BlockSpec(shape, lambda n: (0, 0, 0))


def _smem():
    return pl.BlockSpec(memory_space=pltpu.SMEM)


def _mixer_fwd(qn, kn, vb, ug, vn, wsb, bsb, sink, ga, gs, name):
    s = qn.shape[0]
    nb = s // BLOCK

    def body(sink_ref, q_ref, kp_ref, kc_ref, kx_ref, vp_ref, vc_ref, vx_ref, ug_ref, vn_ref, ws_ref, bs_ref, ga_ref, gs_ref,
             attn_ref, sgu_ref, mix_ref, probs_ref, psink_ref):
        n = pl.program_id(0)
        valid = _band_valid(n, s)
        ssq = jnp.zeros((BLOCK, 1), F32)
        for kh in range(N_KV_HEADS):
            kb = jnp.concatenate([kp_ref[:, _head(kh)], kc_ref[:, _head(kh)], kx_ref[:, _head(kh)]], axis=0)
            vbd = jnp.concatenate([vp_ref[:, _head(kh)], vc_ref[:, _head(kh)], vx_ref[:, _head(kh)]], axis=0)
            p, p_sink = _probs(_group_rows(q_ref, kh), kb, _group_sinks(sink_ref, kh), valid)
            pb = p.astype(BF16)
            probs_ref[kh] = pb
            psink_ref[kh] = p_sink
            o4 = jnp.dot(pb, vbd, preferred_element_type=F32)
            for g in range(GQA_GROUP):
                o = _rows_of(o4, g)
                attn_ref[:, _head(kh * GQA_GROUP + g)] = o
                ssq = ssq + jnp.sum(o * o, axis=-1, keepdims=True)
        r = lax.rsqrt(ssq * (1.0 / ATTN_WIDTH) + EPS)
        mix_ref[:, 0:ATTN_WIDTH] = (attn_ref[...] * r * ga_ref[...]).astype(BF16)
        ssq = jnp.zeros((BLOCK, 1), F32)
        for h in range(N_GMLP_HEADS):
            f = jnp.dot(ws_ref[h], vn_ref[:, _head(h)], preferred_element_type=F32) + bs_ref[h]
            o = ug_ref[:, _head(h)] * f
            sgu_ref[:, _head(h)] = o
            ssq = ssq + jnp.sum(o * o, axis=-1, keepdims=True)
        r = lax.rsqrt(ssq * (1.0 / GMLP_WIDTH) + EPS)
        mix_ref[:, ATTN_WIDTH:D_MODEL] = (sgu_ref[...] * r * gs_ref[...]).astype(BF16)

    hh = (N_GMLP_HEADS, BLOCK, BLOCK)
    return _ordered_call(
        body, name=name,
        out_shape=(jax.ShapeDtypeStruct((s, ATTN_WIDTH), F32), jax.ShapeDtypeStruct((s, GMLP_WIDTH), F32),
                   jax.ShapeDtypeStruct((s, D_MODEL), BF16), jax.ShapeDtypeStruct((nb,) + PROBS_BLOCK, BF16),
                   jax.ShapeDtypeStruct((nb,) + PSINK_BLOCK, F32)),
        grid=(nb,),
        in_specs=[_smem(), _blk(ATTN_WIDTH)] + _band_specs(KV_WIDTH, nb) + _band_specs(KV_WIDTH, nb)
        + [_blk(GMLP_WIDTH), _blk(GMLP_WIDTH), _whole3(hh), _whole3(hh),
           pl.BlockSpec((1, ATTN_WIDTH), lambda n: (0, 0)), pl.BlockSpec((1, GMLP_WIDTH), lambda n: (0, 0))],
        out_specs=(_blk(ATTN_WIDTH), _blk(GMLP_WIDTH), _blk(D_MODEL), _per_block(PROBS_BLOCK), _per_block(PSINK_BLOCK)),
        compiler_params=_params(("parallel",)),
    )(sink, qn, kn, kn, kn, vb, vb, vb, ug, vn, wsb, bsb, ga, gs)


PROBS_BLOCK = (N_KV_HEADS, GQA_GROUP * BLOCK, 3 * BLOCK)
PSINK_BLOCK = (N_KV_HEADS, GQA_GROUP * BLOCK, 1)


def _per_block(shape):
    return pl.BlockSpec((None,) + shape, lambda n: (n, 0, 0, 0))


def _mixer_bwd(qn, kn, vb, ug, vn, attn, sgu, dmixed, wsb, bsb, ga, gs, probs, psink, name):
    s = qn.shape[0]
    nb = s // BLOCK
    tn_dims = (((0,), (0,)), ((), ()))
    nt_dims = (((1,), (1,)), ((), ()))

    def body(q_ref, kp_ref, kc_ref, kx_ref, vp_ref, vc_ref, vx_ref, ug_ref, vn_ref, attn_ref, sgu_ref, dm_ref,
             ws_ref, bs_ref, ga_ref, gs_ref, probs_ref, psink_ref,
             dq_ref, dk_ref, dv_ref, dug_ref, dvn_ref, dws_ref, dbs_ref, dsk_ref, dga_ref, dgs_ref, dk_acc, dv_acc):
        n = pl.program_id(0)

        @pl.when(n == 0)
        def _():
            for ref in (dk_acc, dv_acc, dws_ref, dbs_ref, dsk_ref, dga_ref, dgs_ref):
                ref[...] = jnp.zeros_like(ref)

        def out_norm_bwd(o, g, dy):
            r = lax.rsqrt(_mean_last(o * o) + EPS)
            gdy = dy * g
            return r * gdy - o * ((r * r * r) * _mean_last(o * gdy)), _sum_rows(o * r * dy)

        d_attn, dga = out_norm_bwd(attn_ref[...], ga_ref[...], dm_ref[:, 0:ATTN_WIDTH])
        dga_ref[...] += dga
        d_sgu, dgs = out_norm_bwd(sgu_ref[...], gs_ref[...], dm_ref[:, ATTN_WIDTH:D_MODEL])
        dgs_ref[...] += dgs

        for h in range(N_GMLP_HEADS):
            vn_h = vn_ref[:, _head(h)]
            f = jnp.dot(ws_ref[h], vn_h, preferred_element_type=F32) + bs_ref[h]
            ds_h = d_sgu[:, _head(h)]
            dug_ref[:, _head(h)] = ds_h * f
            df = ds_h * ug_ref[:, _head(h)]
            dfb = df.astype(BF16)
            dvn_ref[:, _head(h)] = lax.dot_general(ws_ref[h], dfb, tn_dims, preferred_element_type=F32)
            dws_ref[h] += lax.dot_general(dfb, vn_h, nt_dims, preferred_element_type=F32)
            dbs_ref[h] += jnp.broadcast_to(jnp.sum(df, axis=-1, keepdims=True), (BLOCK, BLOCK))

        row0 = pl.multiple_of(n * BLOCK, BLOCK)
        for kh in range(N_KV_HEADS):
            kb = jnp.concatenate([kp_ref[:, _head(kh)], kc_ref[:, _head(kh)], kx_ref[:, _head(kh)]], axis=0)
            vbd = jnp.concatenate([vp_ref[:, _head(kh)], vc_ref[:, _head(kh)], vx_ref[:, _head(kh)]], axis=0)
            q4 = _group_rows(q_ref, kh)
            pb = probs_ref[kh]
            p = pb.astype(F32)
            do4 = _group_rows(d_attn, kh).astype(BF16)
            dp = lax.dot_general(do4, vbd, nt_dims, preferred_element_type=F32)
            delta = jnp.sum(p * dp, axis=-1, keepdims=True)
            dsc = (p * (dp - delta) * (HEAD_DIM ** -0.5)).astype(BF16)
            d_sink = -(psink_ref[kh] * delta)
            dq4 = jnp.dot(dsc, kb, preferred_element_type=F32)
            for g in range(GQA_GROUP):
                h = kh * GQA_GROUP + g
                dsk_ref[h:h + 1, :] += jnp.broadcast_to(_sum_all(_rows_of(d_sink, g)), (1, BLOCK))
                dq_ref[:, _head(h)] = _rows_of(dq4, g)
            dk_acc[pl.ds(row0, 3 * BLOCK), _head(kh)] += lax.dot_general(dsc, q4, tn_dims, preferred_element_type=F32)
            dv_acc[pl.ds(row0, 3 * BLOCK), _head(kh)] += lax.dot_general(pb, do4, tn_dims, preferred_element_type=F32)

        @pl.when(n == nb - 1)
        def _():
            dk_ref[...] = dk_acc[BLOCK:BLOCK + s, :]
            dv_ref[...] = dv_acc[BLOCK:BLOCK + s, :]

    hh = (N_GMLP_HEADS, BLOCK, BLOCK)
    full_kv = pl.BlockSpec((s, KV_WIDTH), lambda n: (0, 0))
    return _ordered_call(
        body, name=name,
        out_shape=(jax.ShapeDtypeStruct((s, ATTN_WIDTH), F32), jax.ShapeDtypeStruct((s, KV_WIDTH), F32),
                   jax.ShapeDtypeStruct((s, KV_WIDTH), F32), jax.ShapeDtypeStruct((s, GMLP_WIDTH), F32),
                   jax.ShapeDtypeStruct((s, GMLP_WIDTH), F32), jax.ShapeDtypeStruct(hh, F32), jax.ShapeDtypeStruct(hh, F32),
                   jax.ShapeDtypeStruct((N_Q_HEADS, BLOCK), F32), jax.ShapeDtypeStruct((1, ATTN_WIDTH), F32),
                   jax.ShapeDtypeStruct((1, GMLP_WIDTH), F32)),
        grid=(nb,),
        in_specs=[_blk(ATTN_WIDTH)] + _band_specs(KV_WIDTH, nb) + _band_specs(KV_WIDTH, nb)
        + [_blk(GMLP_WIDTH), _blk(GMLP_WIDTH), _blk(ATTN_WIDTH), _blk(GMLP_WIDTH), _blk(D_MODEL), _whole3(hh), _whole3(hh),
           pl.BlockSpec((1, ATTN_WIDTH), lambda n: (0, 0)), pl.BlockSpec((1, GMLP_WIDTH), lambda n: (0, 0)),
           _per_block(PROBS_BLOCK), _per_block(PSINK_BLOCK)],
        out_specs=(_blk(ATTN_WIDTH), full_kv, full_kv, _blk(GMLP_WIDTH), _blk(GMLP_WIDTH), _whole3(hh), _whole3(hh),
                   pl.BlockSpec((N_Q_HEADS, BLOCK), lambda n: (0, 0)), pl.BlockSpec((1, ATTN_WIDTH), lambda n: (0, 0)),
                   pl.BlockSpec((1, GMLP_WIDTH), lambda n: (0, 0))),
        scratch_shapes=[pltpu.VMEM((s + 2 * BLOCK, KV_WIDTH), F32), pltpu.VMEM((s + 2 * BLOCK, KV_WIDTH), F32)],
        compiler_params=_params(("arbitrary",)),
    )(qn, kn, kn, kn, vb, vb, vb, ug, vn, attn, sgu, dmixed, wsb, bsb, ga, gs, probs, psink)


CONV_TILE = 128


PAD_ROWS = 8


def _zero_pad_rows(pad_ref):
    s = pad_ref.shape[0] - 2 * PAD_ROWS
    zeros = jnp.zeros((PAD_ROWS, pad_ref.shape[1]), F32)
    pad_ref[0:PAD_ROWS, :] = zeros
    pad_ref[PAD_ROWS + s:2 * PAD_ROWS + s, :] = zeros


def _shift_rows(a, pad_ref):
    s = a.shape[0]
    pad_ref[PAD_ROWS:PAD_ROWS + s, :] = a
    padded = pad_ref[...]
    prev = pltpu.roll(padded, 1, 0)[PAD_ROWS:PAD_ROWS + s]
    nxt = pltpu.roll(padded, s + 2 * PAD_ROWS - 1, 0)[PAD_ROWS:PAD_ROWS + s]
    return prev, nxt


def _conv_specs(s):
    tc = CONV_TILE
    nj = D_FF // tc
    return (tc, nj, pl.BlockSpec((2, s, tc), lambda j: (0, 0, j)),
            [pl.BlockSpec((3, tc), lambda j: (0, j)), pl.BlockSpec((3, tc), lambda j: (0, j + nj))],
            [pl.BlockSpec((1, tc), lambda j: (0, j)), pl.BlockSpec((1, tc), lambda j: (0, j + nj))])


def _conv_gate_fwd(a_pre, cw, cb, name):
    s = a_pre.shape[1]
    tc, nj, a_spec, w_specs, b_specs = _conv_specs(s)

    def body(a_ref, wg_ref, wu_ref, bg_ref, bu_ref, act_ref, dgu_ref, pad_ref):
        _zero_pad_rows(pad_ref)

        def conv(a, w_ref, b_ref):
            prev, nxt = _shift_rows(a, pad_ref)
            return b_ref[...] + prev * w_ref[0:1, :] + a * w_ref[1:2, :] + nxt * w_ref[2:3, :]

        g = conv(a_ref[0].astype(F32), wg_ref, bg_ref)
        u = conv(a_ref[1].astype(F32), wu_ref, bu_ref)
        sg = 1.0 / (1.0 + jnp.exp(-g))
        silu = g * sg
        act_ref[...] = (silu * u).astype(BF16)
        dgu_ref[0] = (u * (sg * (1.0 + g * (1.0 - sg)))).astype(BF16)
        dgu_ref[1] = silu.astype(BF16)

    return _ordered_call(
        body, name=name, out_shape=(jax.ShapeDtypeStruct((s, D_FF), BF16), jax.ShapeDtypeStruct((2, s, D_FF), BF16)),
        grid=(nj,), in_specs=[a_spec] + w_specs + b_specs,
        out_specs=(pl.BlockSpec((s, tc), lambda j: (0, j)), pl.BlockSpec((2, s, tc), lambda j: (0, 0, j))),
        scratch_shapes=[pltpu.VMEM((s + 2 * PAD_ROWS, tc), F32)], compiler_params=_params(("parallel",)),
    )(a_pre, cw, cw, cb, cb)


def _conv_gate_bwd(a_pre, dgu, cw, dact, name):
    s = a_pre.shape[1]
    tc, nj, a_spec, w_specs, _ = _conv_specs(s)

    def body(a_ref, dgu_ref, wg_ref, wu_ref, dact_ref, dap_ref, dcw_ref, dcb_ref, pad_ref):
        _zero_pad_rows(pad_ref)
        dact_v = dact_ref[...].astype(F32)
        for part, w_ref in enumerate((wg_ref, wu_ref)):
            da = dact_v * dgu_ref[part].astype(F32)
            a = a_ref[part].astype(F32)
            da_prev, da_next = _shift_rows(da, pad_ref)
            dcw_ref[part, 0:1, :] = _sum_rows(a * da_next)
            dcw_ref[part, 1:2, :] = _sum_rows(a * da)
            dcw_ref[part, 2:3, :] = _sum_rows(a * da_prev)
            dcb_ref[part] = _sum_rows(da)
            dap_ref[part] = (da_next * w_ref[0:1, :] + da * w_ref[1:2, :] + da_prev * w_ref[2:3, :]).astype(BF16)

    return _ordered_call(
        body, name=name,
        out_shape=(jax.ShapeDtypeStruct((2, s, D_FF), BF16), jax.ShapeDtypeStruct((2, 3, D_FF), F32),
                   jax.ShapeDtypeStruct((2, 1, D_FF), F32)),
        grid=(nj,),
        in_specs=[a_spec, pl.BlockSpec((2, s, tc), lambda j: (0, 0, j))] + w_specs + [pl.BlockSpec((s, tc), lambda j: (0, j))],
        out_specs=(pl.BlockSpec((2, s, tc), lambda j: (0, 0, j)), pl.BlockSpec((2, 3, tc), lambda j: (0, 0, j)),
                   pl.BlockSpec((2, 1, tc), lambda j: (0, 0, j))),
        scratch_shapes=[pltpu.VMEM((s + 2 * PAD_ROWS, tc), F32)], compiler_params=_params(("parallel",)),
    )(a_pre, dgu, cw, cw, dact)


def _loss_head(y, target, name):
    s, d = y.shape
    tr = _row_tile(s)

    def body(y_ref, t_ref, loss_ref, dy_ref, dyb_ref):
        err = y_ref[...] - t_ref[...]

        @pl.when(pl.program_id(0) == 0)
        def _():
            loss_ref[...] = jnp.zeros_like(loss_ref)

        loss_ref[...] += jnp.broadcast_to(0.5 * _sum_all(_mean_last(err * err)), (8, 128))
        dy = err * (1.0 / d)
        dy_ref[...] = dy
        dyb_ref[...] = dy.astype(BF16)

    return _ordered_call(
        body, name=name,
        out_shape=(jax.ShapeDtypeStruct((8, 128), F32), jax.ShapeDtypeStruct((s, d), F32), jax.ShapeDtypeStruct((s, d), BF16)),
        grid=(s // tr,), in_specs=[_rows(d, tr), _rows(d, tr)],
        out_specs=(_const2((8, 128)), _rows(d, tr), _rows(d, tr)), compiler_params=_params(("arbitrary",)),
    )(y, target)


def _row_block(rows, cols, budget=1 << 20):
    if rows * cols <= budget:
        return rows
    best = None
    for tr in range(16, rows, 16):
        if rows % tr == 0 and tr * cols <= budget:
            best = tr
    assert best is not None, (rows, cols)
    return best


def _place_shard(x4, layer, j_arr, out_dtype, name):
    _, nh, r, cols = x4.shape
    tr = _row_block(r, cols)

    def body(j_ref, x_ref, o_ref):
        o_ref[...] = x_ref[...].astype(out_dtype)

    grid_spec = pltpu.PrefetchScalarGridSpec(
        num_scalar_prefetch=1, grid=(nh, r // tr),
        in_specs=[pl.BlockSpec((None, None, tr, cols), lambda h, i, j_ref: (layer, h, i, 0))],
        out_specs=pl.BlockSpec((None, None, tr, cols), lambda h, i, j_ref: (j_ref[0], h, i, 0)))
    return _ordered_call(
        body, name=name, out_shape=jax.ShapeDtypeStruct((N_CHIPS, nh, r, cols), out_dtype), grid_spec=grid_spec,
        compiler_params=_params(("parallel", "parallel")),
    )(j_arr, x4)


def _adamw(w, g, m, v, name):
    rows, cols = w.shape
    tr = _row_block(rows, cols, 1 << 18)

    def body(w_ref, g_ref, m_ref, v_ref, go_ref, d_ref, nm_ref, nv_ref):
        gv = g_ref[...]
        go_ref[...] = gv
        mn = ADAM_B1 * m_ref[...] + (1.0 - ADAM_B1) * gv
        vn = ADAM_B2 * v_ref[...] + (1.0 - ADAM_B2) * (gv * gv)
        m_hat = mn / (1.0 - ADAM_B1 ** ADAM_STEP)
        v_hat = vn / (1.0 - ADAM_B2 ** ADAM_STEP)
        d_ref[...] = -ADAM_LR * (m_hat / (jnp.sqrt(v_hat) + ADAM_EPS) + ADAM_WD * w_ref[...])
        nm_ref[...] = mn
        nv_ref[...] = vn

    sds = jax.ShapeDtypeStruct((rows, cols), F32)
    return _ordered_call(
        body, name=name, out_shape=(sds, sds, sds, sds), grid=(rows // tr,),
        in_specs=[_rows(cols, tr)] * 4, out_specs=(_rows(cols, tr),) * 4, compiler_params=_params(("parallel",)),
    )(w, g, m, v)


def _chip_sum(p4, recv3, j_arr, c_arr, name):
    _, rh, cols = p4.shape
    tr = _row_block(rh, cols, 1 << 19)

    def body(j_ref, c_ref, p_ref, r_ref, o_ref):
        total = p_ref[...].astype(F32)
        for peer in range(3):
            total = total + r_ref[peer].astype(F32)
        o_ref[...] = total

    grid_spec = pltpu.PrefetchScalarGridSpec(
        num_scalar_prefetch=2, grid=(rh // tr,),
        in_specs=[pl.BlockSpec((None, tr, cols), lambda i, j_ref, c_ref: (j_ref[0], i, 0)),
                  pl.BlockSpec((3, tr, cols), lambda i, j_ref, c_ref: (0, i, 0))],
        out_specs=pl.BlockSpec((None, tr, cols), lambda i, j_ref, c_ref: (c_ref[0], i, 0)))
    return _ordered_call(
        body, name=name, out_shape=jax.ShapeDtypeStruct((2, rh, cols), F32), grid_spec=grid_spec,
        compiler_params=_params(("parallel",)),
    )(j_arr, c_arr, p4, recv3)


def _adamw_layer(w, g, m, v, layer, into, name):
    nl, rows, cols = w.shape
    slabs, _, width = g.shape
    assert slabs * width == cols and g.shape[1] == rows, (name, w.shape, g.shape)
    tr = _row_block(rows, width, 1 << 18)
    at_layer = pl.BlockSpec((None, tr, width), lambda h, i: (layer, i, h))

    def body(w_ref, g_ref, m_ref, v_ref, *rest):
        go_ref, d_ref, nm_ref, nv_ref = rest[-4:]
        gv = g_ref[...]
        go_ref[...] = gv
        mn = ADAM_B1 * m_ref[...] + (1.0 - ADAM_B1) * gv
        vn = ADAM_B2 * v_ref[...] + (1.0 - ADAM_B2) * (gv * gv)
        m_hat = mn / (1.0 - ADAM_B1 ** ADAM_STEP)
        v_hat = vn / (1.0 - ADAM_B2 ** ADAM_STEP)
        d_ref[...] = -ADAM_LR * (m_hat / (jnp.sqrt(v_hat) + ADAM_EPS) + ADAM_WD * w_ref[...])
        nm_ref[...] = mn
        nv_ref[...] = vn

    in_specs = [at_layer, pl.BlockSpec((None, tr, width), lambda h, i: (h, i, 0)), at_layer, at_layer]
    operands = [w, g, m, v]
    aliases = {}
    if into is not None:
        in_specs += [ANY] * 4
        operands += list(into)
        aliases = {4 + i: i for i in range(4)}
    sds = jax.ShapeDtypeStruct((nl, rows, cols), F32)
    return _ordered_call(
        body, name=name, out_shape=(sds,) * 4, grid=(slabs, rows // tr), in_specs=in_specs, out_specs=(at_layer,) * 4,
        input_output_aliases=aliases, compiler_params=_params(("parallel", "parallel")),
    )(*operands)


def _sum_devices(mine, landed, me_arr, name):
    rows, lanes = mine.shape

    def body(me_ref, mine_ref, landed_ref, o_ref):
        total = None
        for dev in range(8):
            part = jnp.where(me_ref[0] == dev, mine_ref[...], landed_ref[dev])
            total = part if total is None else total + part
        o_ref[...] = total

    grid_spec = pltpu.PrefetchScalarGridSpec(
        num_scalar_prefetch=1, grid=(1,),
        in_specs=[pl.BlockSpec((rows, lanes), lambda i, me_ref: (0, 0)), pl.BlockSpec((8, rows, lanes), lambda i, me_ref: (0, 0, 0))],
        out_specs=pl.BlockSpec((rows, lanes), lambda i, me_ref: (0, 0)))
    return _ordered_call(
        body, name=name, out_shape=jax.ShapeDtypeStruct((rows, lanes), F32), grid_spec=grid_spec,
        compiler_params=_params(("arbitrary",)),
    )(me_arr, mine, landed)


def _place():
    x, y, c = lax.axis_index("x"), lax.axis_index("y"), lax.axis_index("c")
    chips = [(1 - x, y), (x, 1 - y), (1 - x, 1 - y)]
    return x, y, c, chips


HBM = pl.BlockSpec(memory_space=pltpu.HBM)
SEM = pl.BlockSpec(memory_space=pltpu.SEMAPHORE)
TOKEN = jax.ShapeDtypeStruct((8, 128), F32)


def _remote(src, dst, send_sem, recv_sem, to):
    return pltpu.make_async_remote_copy(src_ref=src, dst_ref=dst, send_sem=send_sem, recv_sem=recv_sem, device_id=to,
                                        device_id_type=MESH)


def _split_call(body, name, thru, sems_in=(), fresh=(), new_sems=(), after_last=True):
    n_t, n_s, n_f = len(thru), len(sems_in), len(fresh)

    def call_body(*refs):
        outs = refs[n_t + n_s:]
        body(refs[:n_t], refs[n_t:n_t + n_s], outs[1 + n_t:1 + n_t + n_f], outs[1 + n_t + n_f:])
        outs[0][...] = jnp.zeros_like(outs[0])

    out_shape = ([TOKEN] + [pltpu.HBM(t.shape, t.dtype) for t in thru] + [pltpu.HBM(shp, dt) for shp, dt in fresh]
                 + [pltpu.SemaphoreType.DMA(shp) for shp in new_sems])
    out_specs = [pl.BlockSpec(memory_space=pltpu.VMEM)] + [HBM] * (n_t + n_f) + [SEM] * len(new_sems)
    if not after_last or any(t is _Order.last for t in thru):
        _Order.last = None
    out = _ordered_call(
        call_body, name=name, out_shape=tuple(out_shape), in_specs=[HBM] * n_t + [SEM] * n_s, out_specs=tuple(out_specs),
        input_output_aliases={i: 1 + i for i in range(n_t)},
        compiler_params=pltpu.CompilerParams(has_side_effects=pltpu.SideEffectType.DATAFLOW_SIDE_EFFECTING),
    )(*[pltpu.with_memory_space_constraint(t, pltpu.HBM) for t in thru], *sems_in)
    return out[1:1 + n_t], out[1 + n_t:1 + n_t + n_f], out[1 + n_t + n_f:]


class _Exchange:
    def __init__(self, weights, m_in, v_in, j_arr, c_arr, me_arr):
        self.w, self.m, self.v = weights, m_in, v_in
        self.j_arr, self.c_arr, self.me_arr = j_arr, c_arr, me_arr
        self.adam, self.small, self.pairs, self.held = {}, {}, {}, None
        self.o_arr = 1 - c_arr
        self.groups = [(l, name) for l in range(DEPTH) for name in BIG_NAMES]
        self.shard_shape = {name: weights[name].shape[1:] for name in BIG_NAMES}
        self.conv_state, self.state = [], {}
        self.ready, self.conv_ready = {}, {}
        self.pending, self.tick, self.reduced = [], 0, {}

        def place(grp):
            l, name = grp
            nl, r, cols = weights[name].shape
            return _place_shard(weights[name].reshape(nl, 2, r // 2, cols), l, j_arr, BF16, f"place_{name}_l{l}")

        def start_copies(tag, convs, groups, bufs):
            n_c = len(convs)

            def start(thru, _, __, sems):
                x, y, c, chips = _place()
                j_me = 2 * x + y
                copies = []
                for i in range(len(thru)):
                    mine = thru[i].at[j_me] if i < n_c else thru[i].at[j_me, c]
                    copies += [_remote(mine, mine, sems[2 * i].at[k], sems[2 * i + 1].at[k], (*chip, c))
                               for k, chip in enumerate(chips)]
                for cp in copies:
                    cp.start()

            thru, _, sems = _split_call(start, tag, convs + bufs, new_sems=[(3,)] * (2 * (n_c + len(bufs))))
            self.conv_state += [(thru[i], sems[2 * i], sems[2 * i + 1]) for i in range(n_c)]
            for g, grp in enumerate(groups):
                self.state[grp] = (thru[n_c + g], sems[2 * (n_c + g)], sems[2 * (n_c + g) + 1])

        convs = [_place_shard(weights["conv_w"][:, None], l, j_arr, F32, f"place_conv_w_l{l}") for l in range(DEPTH)]
        start_copies("gather_start_first", convs, self.groups[:1], [place(self.groups[0])])
        start_copies("gather_start_rest", [], self.groups[1:], [place(grp) for grp in self.groups[1:]])

    def conv_w(self, l):
        if l not in self.conv_ready:
            buf, send, recv = self.conv_state[l]

            def wait(thru, sems, _, __):
                x, y, c, chips = _place()
                for k, chip in enumerate(chips):
                    mine, theirs = thru[0].at[2 * x + y], thru[0].at[2 * chip[0] + chip[1]]
                    _remote(mine, mine, sems[0].at[k], sems[1].at[k], (*chip, c)).wait_send()
                    _remote(theirs, theirs, sems[0].at[k], sems[1].at[k], (x, y, c)).wait_recv()

            (buf,), _, _ = _split_call(wait, f"gather_conv_w_l{l}", [buf], sems_in=[send, recv])
            self.conv_ready[l] = jnp.transpose(buf[:, 0], (1, 0, 2)).reshape(3, 2 * D_FF)
        return self.conv_ready[l]

    def weight(self, l, name):
        grp = (l, name)
        if grp not in self.ready:
            buf, send, recv = self.state[grp]

            def forward(thru, sems, _, new):
                x, y, c, chips = _place()
                for k, chip in enumerate(chips):
                    landed = thru[0].at[2 * chip[0] + chip[1], c]
                    _remote(landed, landed, new[0].at[k], sems[0].at[k], (x, y, c)).wait_recv()
                    _remote(landed, landed, new[0].at[k], new[1].at[k], (x, y, 1 - c)).start()

            (buf,), _, (fsend, frecv) = _split_call(forward, f"gather_pass_{name}_l{l}", [buf], sems_in=[recv],
                                                    new_sems=[(3,), (3,)])

            def finish(thru, sems, _, __):
                x, y, c, chips = _place()
                mine = thru[0].at[2 * x + y, c]
                for k, chip in enumerate(chips):
                    j_k = 2 * chip[0] + chip[1]
                    theirs, landed = thru[0].at[j_k, 1 - c], thru[0].at[j_k, c]
                    _remote(theirs, theirs, sems[1].at[k], sems[2].at[k], (x, y, c)).wait_recv()
                    _remote(landed, landed, sems[1].at[k], sems[2].at[k], (x, y, 1 - c)).wait_send()
                    _remote(mine, mine, sems[0].at[k], sems[2].at[k], (*chip, c)).wait_send()

            (buf,), _, _ = _split_call(finish, f"gather_done_{name}_l{l}", [buf], sems_in=[send, fsend, frecv])
            r, cols = self.shard_shape[name]
            self.ready[grp] = buf.reshape(N_CHIPS, r, cols) if name in ("w_in", "w_up") else buf.reshape(N_CHIPS * r, cols)
        return self.ready[grp]

    def pair_send(self, l, name, other):
        held = self.held
        self.held = None

        def start(thru, _, fresh, sems):
            x, y, c, chips = _place()
            copies = [_remote(thru[0], fresh[0], sems[0], sems[1], (x, y, 1 - c))]
            if held is not None:
                copies += [_remote(thru[1].at[2 * chip[0] + chip[1]], fresh[1].at[k], sems[2].at[k], sems[3].at[k], (*chip, c))
                           for k, chip in enumerate(chips)]
            for cp in copies:
                cp.start()

        thru, fresh, new_sems = [other], [(other.shape, BF16)], [(), ()]
        if held is not None:
            thru, fresh, new_sems = thru + [held[2]], fresh + [((3,) + held[2].shape[1:], BF16)], new_sems + [(3,), (3,)]
        thru, fresh, sems = _split_call(start, f"pair_start_{name}_l{l}", thru, fresh=fresh, new_sems=new_sems, after_last=False)
        self.pairs[(l, name)] = (thru[0], fresh[0], sems[:2])
        if held is not None:
            self.pending.append(dict(l=held[0], name=held[1], stage=2, at=self.tick, bufs=(thru[1], fresh[1]), sems=sems[2:]))

    def pair_recv(self, l, name):
        other, recv, sems = self.pairs.pop((l, name))

        def wait(thru, sems, _, __):
            x, y, c, _chips = _place()
            cp = _remote(thru[0], thru[1], sems[0], sems[1], (x, y, 1 - c))
            cp.wait_send()
            cp.wait_recv()

        (_, recv), _, _ = _split_call(wait, f"pair_done_{name}_l{l}", [other, recv], sems_in=list(sems))
        return recv

    def scatter(self, l, name, p4):
        assert self.held is None
        self.held = (l, name, p4)
        if (l, name) == (0, BIG_NAMES[0]):
            self._scatter_held()

    def _scatter_held(self):
        l, name, p4 = self.held
        self.held = None

        def start(thru, _, fresh, sems):
            x, y, c, chips = _place()
            for k, chip in enumerate(chips):
                _remote(thru[0].at[2 * chip[0] + chip[1]], fresh[0].at[k], sems[0].at[k], sems[1].at[k], (*chip, c)).start()

        (p4,), (recv3,), sems = _split_call(start, f"chips_start_{name}_l{l}", [p4], fresh=[((3,) + p4.shape[1:], BF16)],
                                           new_sems=[(3,), (3,)], after_last=False)
        self.pending.append(dict(l=l, name=name, stage=2, at=self.tick, bufs=(p4, recv3), sems=sems))

    def point(self, drain=False):
        self.tick += 1
        if drain:
            old = [g for g in self.pending if g["stage"] == 2 and g["at"] + 2 <= self.tick]
            new = [g for g in self.pending if g["stage"] == 2 and g["at"] + 2 > self.tick]
            for grp in old + [g for g in self.pending if g["stage"] == 3] + new:
                self._advance([grp] if grp["stage"] == 3 else [], [grp] if grp["stage"] == 2 else [])
        else:
            self._advance([grp for grp in self.pending if grp["stage"] == 3 and grp["at"] < self.tick],
                          [grp for grp in self.pending if grp["stage"] == 2 and grp["at"] + 2 <= self.tick])

    def _advance(self, joined, landed):
        if not joined and not landed:
            return
        n_j, n_l = len(joined), len(landed)

        def wait(thru, sems, _, __):
            x, y, c, chips = _place()
            for i in range(n_j):
                buf, send, recv = thru[i], sems[2 * i], sems[2 * i + 1]
                _remote(buf.at[c], buf.at[c], send, recv, (x, y, 1 - c)).wait_send()
                _remote(buf.at[1 - c], buf.at[1 - c], send, recv, (x, y, c)).wait_recv()
            for i in range(n_l):
                p4, recv3 = thru[n_j + 2 * i], thru[n_j + 2 * i + 1]
                send, recv = sems[2 * (n_j + i)], sems[2 * (n_j + i) + 1]
                for k, chip in enumerate(chips):
                    cp = _remote(p4.at[2 * chip[0] + chip[1]], recv3.at[k], send.at[k], recv.at[k], (*chip, c))
                    cp.wait_send()
                    cp.wait_recv()

        tag = "_".join([f"{grp['name']}{grp['l']}_halves" for grp in joined] + [f"{grp['name']}{grp['l']}_chips" for grp in landed])
        bufs, _, _ = _split_call(wait, f"landed_{tag}", [b for grp in joined + landed for b in grp["bufs"]],
                                 sems_in=[sm for grp in joined + landed for sm in grp["sems"]])
        for i, grp in enumerate(joined):
            l, name, full = grp["l"], grp["name"], bufs[i]
            if GRAD_HALVES[name][0] != "cols_of_block":
                full = full.reshape((1,) + tuple(self.shard_shape[name]))
            self.adam[name] = _adamw_layer(self.w[name], full, self.m[name], self.v[name], l, self.adam.get(name),
                                           f"adamw_{name}_l{l}")
            grp.update(stage=4)
        if not landed:
            return
        halves = [_chip_sum(bufs[n_j + 2 * i], bufs[n_j + 2 * i + 1], self.j_arr, self.c_arr,
                            f"chip_sum_{grp['name']}_l{grp['l']}") for i, grp in enumerate(landed)]

        def start(thru, _, __, sems):
            x, y, c, _chips = _place()
            for i in range(n_l):
                _remote(thru[i].at[c], thru[i].at[c], sems[2 * i], sems[2 * i + 1], (x, y, 1 - c)).start()

        tag = "_".join(f"{grp['name']}{grp['l']}" for grp in landed)
        halves, _, sems = _split_call(start, f"join_start_{tag}", halves, new_sems=[()] * (2 * n_l), after_last=False)
        for i, grp in enumerate(landed):
            grp.update(stage=3, at=self.tick, bufs=(halves[i],), sems=tuple(sems[2 * i:2 * i + 2]))

    def finish(self):
        if self.held is not None:
            self._scatter_held()
        while any(grp["stage"] < 4 for grp in self.pending):
            self.point(drain=True)
        return self.adam

    @staticmethod
    def _peer(k, x, y, c):
        return (1 - x if k & 4 else x, 1 - y if k & 2 else y, 1 - c if k & 1 else c)

    def small_grads(self, l, grads, loss_tile):
        parts = [grads[nm] for nm in SMALL_NAMES] + ([loss_tile[0, 0:1]] if loss_tile is not None else [])
        packed = _pack_call(parts, f"small_pack_l{l}")
        rows = packed.shape[0]

        def start(thru, _, fresh, sems):
            x, y, c, _chips = _place()
            for k in range(1, 8):
                _remote(thru[0], fresh[0].at[4 * x + 2 * y + c], sems[0].at[k - 1], sems[1].at[k - 1],
                        self._peer(k, x, y, c)).start()

        (packed,), (landed,), sems = _split_call(start, f"small_start_l{l}", [packed], fresh=[((8, rows, PACK_LANES), F32)],
                                                 new_sems=[(7,), (7,)], after_last=False)
        self.small[l] =(packed, landed, sems, [p.shape for p in parts])

    def small_sum(self, l):
        packed, landed, sems, _shapes = self.small[l]

        def wait(thru, sems, _, __):
            x, y, c, _chips = _place()
            for k in range(1, 8):
                px, py, pc = self._peer(k, x, y, c)
                _remote(thru[0], thru[1].at[4 * x + 2 * y + c], sems[0].at[k - 1], sems[1].at[k - 1], (px, py, pc)).wait_send()
                _remote(thru[0], thru[1].at[4 * px + 2 * py + pc], sems[0].at[k - 1], sems[1].at[k - 1], (x, y, c)).wait_recv()

        (packed, landed), _, _ = _split_call(wait, f"small_done_l{l}", [packed, landed], sems_in=list(sems))
        return _sum_devices(packed, landed, self.me_arr, f"small_sum_l{l}")


def _rope_tables(s):
    inv_freq = ROPE_THETA ** (-jnp.arange(0, HEAD_DIM, 2, dtype=F32) / HEAD_DIM)
    ang = jnp.arange(s, dtype=F32)[:, None] * inv_freq[None, :]
    cos, sin = jnp.cos(ang), jnp.sin(ang)
    return jnp.concatenate([cos, cos], axis=-1), jnp.concatenate([-sin, sin], axis=-1)


def _local_step(x, target, ex, small):
    s = x.shape[0]
    cosf, sinf = _rope_tables(s)
    saved = []
    for l in range(DEPTH):
        p = small[l]
        t = f"l{l}"
        h = _rms_fwd(x, p["norm1_g"], f"norm1_{t}")
        z = _matmul(h, ex.weight(l, "w_in"), mode="nn", out_dtype=BF16, tm=1024, tn=896, tk=2048, b_parts=4, name=f"proj_in_{t}")
        qn, kn, vb, ug, vn, *gate_kept = _proj_post(z, p["q_norm_g"], p["k_norm_g"], p["sgu_ln_g"], p["sgu_ln_b"], cosf, sinf,
                                                    f"proj_post_{t}")
        attn, sgu, mixed, probs, psink = _mixer_fwd(qn, kn, vb, ug, vn, p["w_s_bf16"], p["b_s_tile"], p["sink"],
                                                    p["attn_out_g"], p["sgu_out_g"], f"mixer_{t}")
        x1 = _matmul(mixed, ex.weight(l, "w_o"), mode="nn", out_dtype=F32, tm=2048, tn=256, tk=2048, res=x,
                     name=f"proj_out_{t}")
        h2 = _rms_fwd(x1, p["norm2_g"], f"norm2_{t}")
        a_pre = _matmul(h2, ex.weight(l, "w_up"), mode="nn", out_dtype=BF16, tm=1024, tn=1408, tk=2048, b_parts=4,
                        out_parts=2,
                        name=f"ffn_up_{t}")
        act, dgu = _conv_gate_fwd(a_pre, ex.conv_w(l), p["conv_b"], f"conv_gate_{t}")
        x2 = _matmul(act, ex.weight(l, "w_down"), mode="nn", out_dtype=F32, tm=1024, tn=256, tk=D_FF, res=x1,
                     name=f"ffn_down_{t}")
        saved.append(dict(x=x, h=h, z=z, qn=qn, kn=kn, vb=vb, ug=ug, vn=vn, attn=attn, sgu=sgu, mixed=mixed, x1=x1, h2=h2,
                          a_pre=a_pre, act=act, dgu=dgu, probs=probs, psink=psink, gate_kept=gate_kept))
        x = x2
    loss_tile, dx, dxb = _loss_head(x, target, "loss_head")
    for l in reversed(range(DEPTH)):
        p, sv = small[l], saved[l]
        t = f"l{l}"
        def weight_grad(name, a, g, between, g_parts=0):
            ex.pair_send(l, name, _grad_half(name, a, g, ex.o_arr, None, f"g_{name}_other_{t}", g_parts))
            out = between()
            ex.scatter(l, name, _grad_half(name, a, g, ex.c_arr, ex.pair_recv(l, name), f"g_{name}_own_{t}", g_parts))
            ex.point()
            return out

        def after_down():
            dact = _matmul(dxb, ex.weight(l, "w_down"), mode="nt", out_dtype=BF16, tm=1024, tn=512, tk=2048,
                           name=f"d_act_{t}")
            return _conv_gate_bwd(sv["a_pre"], sv["dgu"], ex.conv_w(l), dact, f"conv_gate_bwd_{t}")

        dap, dcw, dcb = weight_grad("w_down", sv["act"], dxb, after_down)

        def after_up():
            dh2 = _matmul(dap, ex.weight(l, "w_up"), mode="nt", out_dtype=F32, tm=1024, tn=1024, tk=2816, a_parts=2,
                          b_parts=4, name=f"d_h2_{t}")
            return _rms_bwd(sv["x1"], p["norm2_g"], dh2, dx, f"norm2_bwd_{t}")

        dx1, dx1b, dg2 = weight_grad("w_up", sv["h2"], dap, after_up, g_parts=2)
        ex.pair_send(l, "w_o", _grad_half("w_o", sv["mixed"], dx1b, ex.o_arr, None, f"g_w_o_other_{t}"))
        dmixed = _matmul(dx1b, ex.weight(l, "w_o"), mode="nt", out_dtype=F32, tm=1024, tn=512, tk=2048,
                         name=f"d_mixed_{t}")
        dqn, dkn, dvb, dug, dvn, dws, dbs, dsk, dga, dgs = _mixer_bwd(
            sv["qn"], sv["kn"], sv["vb"], sv["ug"], sv["vn"], sv["attn"], sv["sgu"], dmixed, p["w_s_bf16"], p["b_s_tile"],
            p["attn_out_g"], p["sgu_out_g"], sv["probs"], sv["psink"], f"mixer_bwd_{t}")
        dz, dqg, dkg, dlg, dlb = _proj_post_bwd(sv["z"], dqn, dkn, dvb, dug, dvn, *sv["gate_kept"], p["q_norm_g"], p["k_norm_g"],
                                                 p["sgu_ln_g"], cosf, sinf, f"proj_post_bwd_{t}")
        ex.scatter(l, "w_o", _grad_half("w_o", sv["mixed"], dx1b, ex.c_arr, ex.pair_recv(l, "w_o"), f"g_w_o_own_{t}"))
        ex.point()

        def after_in():
            dh = _matmul_nt_slabs(dz, ex.weight(l, "w_in"), tm=1024, tn=512, name=f"d_h_{t}")
            return _rms_bwd(sv["x"], p["norm1_g"], dh, dx1, f"norm1_bwd_{t}")

        dx, dxb, dg1 = weight_grad("w_in", sv["h"], dz, after_in)
        ex.small_grads(l, dict(
            norm1_g=dg1[0], q_norm_g=dqg[0], k_norm_g=dkg[0], sink=dsk[:, 0], sgu_ln_g=dlg[0], sgu_ln_b=dlb[0], w_s=dws,
            b_s=dbs[:, :, 0], attn_out_g=dga[0], sgu_out_g=dgs[0], norm2_g=dg2[0],
            conv_w=jnp.concatenate([dcw[0], dcw[1]], axis=-1), conv_b=jnp.concatenate([dcb[0, 0], dcb[1, 0]], axis=-1)),
            loss_tile if l == 0 else None)
    return dx


def _small_views(l, norm1_g, q_norm_g, k_norm_g, sink, sgu_ln_g, sgu_ln_b, w_s, b_s, attn_out_g, sgu_out_g, norm2_g, conv_b):
    return dict(
        norm1_g=norm1_g[l][None], q_norm_g=q_norm_g[l][None], k_norm_g=k_norm_g[l][None], sink=sink[l],
        sgu_ln_g=sgu_ln_g[l][None], sgu_ln_b=sgu_ln_b[l][None], w_s_bf16=w_s[l].astype(BF16),
        b_s_tile=jnp.broadcast_to(b_s[l][:, :, None], (N_GMLP_HEADS, BLOCK, BLOCK)), attn_out_g=attn_out_g[l][None],
        sgu_out_g=sgu_out_g[l][None], norm2_g=norm2_g[l][None], conv_b=conv_b[l][None])


SMALL_NAMES = ("norm1_g", "q_norm_g", "k_norm_g", "sink", "sgu_ln_g", "sgu_ln_b", "w_s", "b_s", "attn_out_g", "sgu_out_g",
               "norm2_g", "conv_b", "conv_w")
REPLICATED_NAMES = SMALL_NAMES[:-1]
BIG_NAMES = ("w_in", "w_o", "w_up", "w_down")
PACK_LANES = 128
PACK_ALIGN = 8 * PACK_LANES


def _pack_rows(shape):
    return -(-math.prod(shape) // PACK_ALIGN) * 8


def _pack_parts(arrays):
    parts = []
    for a in arrays:
        flat = a.reshape(-1)
        parts.append(jnp.pad(flat, (0, _pack_rows(a.shape) * PACK_LANES - flat.shape[0])).reshape(-1, PACK_LANES))
    return parts


def _pack_call(arrays, name):
    parts = _pack_parts(arrays)
    total = sum(p.shape[0] for p in parts)

    def body(*refs):
        o_ref, at = refs[-1], 0
        for p_ref in refs[:-1]:
            o_ref[at:at + p_ref.shape[0], :] = p_ref[...]
            at += p_ref.shape[0]

    vm = pl.BlockSpec(memory_space=pltpu.VMEM)
    return _ordered_call(
        body, name=name, out_shape=jax.ShapeDtypeStruct((total, PACK_LANES), F32), in_specs=[vm] * len(parts), out_specs=vm,
        compiler_params=pltpu.CompilerParams(vmem_limit_bytes=V7X_VMEM_LIMIT),
    )(*parts)


def _unpack_layers(stacked, shapes):
    nl = stacked.shape[0]
    out, at = [], 0
    for shp in shapes:
        rows = _pack_rows(shp)
        out.append(stacked[:, at:at + rows].reshape(nl, -1)[:, :math.prod(shp)].reshape((nl,) + tuple(shp)))
        at += rows
    return out


def _adamw_packed(w, g, m, v, rows, layer, into, name):
    head = pl.BlockSpec((rows, PACK_LANES), lambda i: (0, 0))
    at_layer = pl.BlockSpec((None, rows, PACK_LANES), lambda i: (layer, 0, 0))

    def body(w_ref, g_ref, m_ref, v_ref, *rest):
        d_ref, nm_ref, nv_ref = rest[-3:]
        gv = g_ref[...]
        mn = ADAM_B1 * m_ref[...] + (1.0 - ADAM_B1) * gv
        vn = ADAM_B2 * v_ref[...] + (1.0 - ADAM_B2) * (gv * gv)
        m_hat = mn / (1.0 - ADAM_B1 ** ADAM_STEP)
        v_hat = vn / (1.0 - ADAM_B2 ** ADAM_STEP)
        d_ref[...] = -ADAM_LR * (m_hat / (jnp.sqrt(v_hat) + ADAM_EPS) + ADAM_WD * w_ref[...])
        nm_ref[...] = mn
        nv_ref[...] = vn

    in_specs = [head] * 4
    operands = [w, g, m, v]
    aliases = {}
    if into is not None:
        in_specs += [ANY] * 3
        operands += list(into)
        aliases = {4 + i: i for i in range(3)}
    sds = jax.ShapeDtypeStruct((DEPTH, rows, PACK_LANES), F32)
    return _ordered_call(
        body, name=name, out_shape=(sds,) * 3, grid=(1,), in_specs=in_specs, out_specs=(at_layer,) * 3,
        input_output_aliases=aliases, compiler_params=_params(("arbitrary",)),
    )(*operands)


def kernel(x, norm1_g, w_in, q_norm_g, k_norm_g, sink, sgu_ln_g, sgu_ln_b, w_s, b_s, attn_out_g, sgu_out_g, w_o, norm2_g, w_up, conv_w, conv_b, w_down, loss_target, m_norm1_g, m_w_in, m_q_norm_g, m_k_norm_g, m_sink, m_sgu_ln_g, m_sgu_ln_b, m_w_s, m_b_s, m_attn_out_g, m_sgu_out_g, m_w_o, m_norm2_g, m_w_up, m_conv_w, m_conv_b, m_w_down, v_norm1_g, v_w_in, v_q_norm_g, v_k_norm_g, v_sink, v_sgu_ln_g, v_sgu_ln_b, v_w_s, v_b_s, v_attn_out_g, v_sgu_out_g, v_w_o, v_norm2_g, v_w_up, v_conv_w, v_conv_b, v_w_down):
    weights = dict(norm1_g=norm1_g, w_in=w_in, q_norm_g=q_norm_g, k_norm_g=k_norm_g, sink=sink, sgu_ln_g=sgu_ln_g,
                   sgu_ln_b=sgu_ln_b, w_s=w_s, b_s=b_s, attn_out_g=attn_out_g, sgu_out_g=sgu_out_g, w_o=w_o, norm2_g=norm2_g,
                   w_up=w_up, conv_w=conv_w, conv_b=conv_b, w_down=w_down)
    m_in = dict(norm1_g=m_norm1_g, w_in=m_w_in, q_norm_g=m_q_norm_g, k_norm_g=m_k_norm_g, sink=m_sink, sgu_ln_g=m_sgu_ln_g,
                sgu_ln_b=m_sgu_ln_b, w_s=m_w_s, b_s=m_b_s, attn_out_g=m_attn_out_g, sgu_out_g=m_sgu_out_g, w_o=m_w_o,
                norm2_g=m_norm2_g, w_up=m_w_up, conv_w=m_conv_w, conv_b=m_conv_b, w_down=m_w_down)
    v_in = dict(norm1_g=v_norm1_g, w_in=v_w_in, q_norm_g=v_q_norm_g, k_norm_g=v_k_norm_g, sink=v_sink, sgu_ln_g=v_sgu_ln_g,
                sgu_ln_b=v_sgu_ln_b, w_s=v_w_s, b_s=v_b_s, attn_out_g=v_attn_out_g, sgu_out_g=v_sgu_out_g, w_o=v_w_o,
                norm2_g=v_norm2_g, w_up=v_w_up, conv_w=v_conv_w, conv_b=v_conv_b, w_down=v_w_down)
    cx, cy, cc = lax.axis_index("x"), lax.axis_index("y"), lax.axis_index("c")
    j_me = 2 * cx + cy
    c_arr = jnp.reshape(cc, (1,)).astype(jnp.int32)
    j_arr = jnp.reshape(j_me, (1,)).astype(jnp.int32)

    _Order.last = None
    ex = _Exchange(weights, m_in, v_in, j_arr, c_arr, jnp.reshape(4 * cx + 2 * cy + cc, (1,)).astype(jnp.int32))
    small = [_small_views(l, norm1_g, q_norm_g, k_norm_g, sink, sgu_ln_g, sgu_ln_b, w_s, b_s, attn_out_g, sgu_out_g, norm2_g,
                          conv_b) for l in range(DEPTH)]
    held_back, _ = lax.optimization_barrier(([[src[nm] for nm in REPLICATED_NAMES] for src in (weights, m_in, v_in)], _Order.last))
    packed_in = [[_pack_call([arr[l] for arr in arrays], f"pack_{tag}_l{l}") for tag, arrays in zip("wmv", held_back)]
                 for l in range(DEPTH)]
    dx = _local_step(x[0], loss_target[0], ex, small)
    big_out = ex.finish()

    rep_shapes = [weights[nm].shape[1:] for nm in REPLICATED_NAMES]
    rep_rows = sum(_pack_rows(shp) for shp in rep_shapes)
    cw_shape = (3, 2 * D_FF)
    sums, adam_small = [None] * DEPTH, None
    for l in reversed(range(DEPTH)):
        sums[l] = ex.small_sum(l)
        pw, pm, pv = packed_in[l]
        adam_small = _adamw_packed(pw, sums[l], pm, pv, rep_rows, l, adam_small, f"adamw_small_l{l}")
    cw_rows = _pack_rows(cw_shape)
    loss = sums[0][rep_rows + cw_rows, 0]
    stacked = jnp.stack([sm[:rep_rows + cw_rows] for sm in sums])
    grads = dict(zip(REPLICATED_NAMES, _unpack_layers(stacked[:, :rep_rows], rep_shapes)))
    delta, new_m, new_v = (dict(zip(REPLICATED_NAMES, _unpack_layers(arr, rep_shapes))) for arr in adam_small)
    cw_cols = 2 * D_FF // N_CHIPS
    cw_grad = lax.dynamic_slice_in_dim(_unpack_layers(stacked[:, rep_rows:], [cw_shape])[0], j_me * cw_cols, cw_cols, axis=2)
    flat = lambda a: a.reshape(DEPTH * 3, cw_cols)
    cw_out = _adamw(flat(conv_w), flat(cw_grad), flat(m_conv_w), flat(v_conv_w), "adamw_conv_w")
    grads["conv_w"], delta["conv_w"], new_m["conv_w"], new_v["conv_w"] = (a.reshape(DEPTH, 3, cw_cols) for a in cw_out)

    for name in BIG_NAMES:
        grads[name], delta[name], new_m[name], new_v[name] = big_out[name]

    order = ("norm1_g", "w_in", "q_norm_g", "k_norm_g", "sink", "sgu_ln_g", "sgu_ln_b", "w_s", "b_s", "attn_out_g", "sgu_out_g",
             "w_o", "norm2_g", "w_up", "conv_w", "conv_b", "w_down")
    return (loss, dx[None], *[grads[nm] for nm in order], *[delta[nm] for nm in order], *[new_m[nm] for nm in order],
            *[new_v[nm] for nm in order])
```

```python
import math

import jax
import jax.numpy as jnp
from jax import lax
from jax.experimental import pallas as pl
from jax.experimental.pallas import tpu as pltpu

F32 = jnp.float32
BF16 = jnp.bfloat16

D_MODEL = 2048
HEAD_DIM = 128
ATTN_WIDTH = 1024
N_Q_HEADS = 8
N_KV_HEADS = 2
GQA_GROUP = 4
KV_WIDTH = 256
GMLP_WIDTH = 1024
N_GMLP_HEADS = 8
BLOCK = 128
IN_WIDTH = 3584
D_FF = 5632
DEPTH = 2
EPS = 1e-6
MASK_VALUE = -1e30
ROPE_THETA = 10000.0
N_CHIPS = 4

ADAM_LR = 0.001
ADAM_B1 = 0.9
ADAM_B2 = 0.999
ADAM_EPS = 1e-08
ADAM_WD = 0.01
ADAM_STEP = 10

V7X_VMEM_LIMIT = 48 * 1024 * 1024
MESH = pl.DeviceIdType.MESH

_GELU_C = math.sqrt(2.0 / math.pi)
_GELU_A = 0.044715


def _params(sem=None):
    return pltpu.CompilerParams(dimension_semantics=sem, vmem_limit_bytes=V7X_VMEM_LIMIT)


ANY = pl.BlockSpec(memory_space=pl.ANY)


class _Order:
    last = None


def _ordered_call(body, *, token_index=0, **kw):
    def run(*operands):
        tok = _Order.last
        if tok is None or any(op is tok for op in operands):
            call = pl.pallas_call(body, **kw)
        else:
            n_in = len(operands)

            def ordered_body(*refs):
                return body(*refs[:n_in], *refs[n_in + 1:])

            kw2 = dict(kw)
            if "grid_spec" in kw2:
                gs = kw2["grid_spec"]
                kw2["grid_spec"] = pltpu.PrefetchScalarGridSpec(
                    num_scalar_prefetch=gs.num_scalar_prefetch, grid=gs.grid, in_specs=list(gs.in_specs) + [ANY],
                    out_specs=gs.out_specs, scratch_shapes=gs.scratch_shapes)
            else:
                kw2["in_specs"] = list(kw2["in_specs"]) + [ANY]
            call = pl.pallas_call(ordered_body, **kw2)
            operands = operands + (tok,)
        out = call(*operands)
        _Order.last = out[token_index] if isinstance(out, (tuple, list)) else out
        return out

    return run


def _gelu(x):
    return x * (0.5 * (1.0 + jnp.tanh(_GELU_C * (x + _GELU_A * (x * x * x)))))


def _gelu_grad(x):
    x2 = x * x
    t = jnp.tanh(_GELU_C * (x + _GELU_A * (x * x2)))
    return 0.5 * (1.0 + t) + 0.5 * x * (1.0 - t * t) * (_GELU_C * (1.0 + 3.0 * _GELU_A * x2))


def _mean_last(x):
    return jnp.mean(x, axis=-1, keepdims=True)


def _sum_rows(x):
    return jnp.sum(x, axis=0, keepdims=True)


def _sum_all(x):
    return jnp.sum(jnp.sum(x, axis=1, keepdims=True), axis=0, keepdims=True)


def _matmul(a, b, *, mode, out_dtype, tm, tn, tk, name, res=None, a_parts=0, b_parts=0, out_parts=0):
    assert mode in ("nn", "nt"), mode
    if mode == "nn":
        assert not a_parts
        m, k = a.shape
        n = b.shape[0] * b.shape[2] if b_parts else b.shape[1]
    else:
        m, k = (a.shape[1], a.shape[0] * a.shape[2]) if a_parts else a.shape
        n = b.shape[1] if b_parts else b.shape[0]
    tm, tn, tk = min(tm, m), min(tn, n), min(tk, k)
    assert m % tm == 0 and n % tn == 0 and k % tk == 0, (name, m, n, k, tm, tn, tk)
    nm, nn, nk = m // tm, n // tn, k // tk

    def slab(idx, total_tiles, parts):
        per = total_tiles // parts
        assert per * parts == total_tiles, (name, total_tiles, parts)
        return idx // per, idx % per

    if mode == "nn":
        a_spec = pl.BlockSpec((tm, tk), lambda i, j, kk: (i, kk))
        if b_parts:
            b_spec = pl.BlockSpec((None, tk, tn), lambda i, j, kk: (slab(j, nn, b_parts)[0], kk, slab(j, nn, b_parts)[1]))
        else:
            b_spec = pl.BlockSpec((tk, tn), lambda i, j, kk: (kk, j))
        dims = (((1,), (0,)), ((), ()))
    else:
        if a_parts:
            a_spec = pl.BlockSpec((None, tm, tk), lambda i, j, kk: (slab(kk, nk, a_parts)[0], i, slab(kk, nk, a_parts)[1]))
        else:
            a_spec = pl.BlockSpec((tm, tk), lambda i, j, kk: (i, kk))
        if b_parts:
            b_spec = pl.BlockSpec((None, tn, tk), lambda i, j, kk: (slab(kk, nk, b_parts)[0], j, slab(kk, nk, b_parts)[1]))
        else:
            b_spec = pl.BlockSpec((tn, tk), lambda i, j, kk: (j, kk))
        dims = (((1,), (1,)), ((), ()))
    if out_parts:
        out_shape = jax.ShapeDtypeStruct((out_parts, m, n // out_parts), out_dtype)
        out_spec = pl.BlockSpec((None, tm, tn), lambda i, j, kk: (slab(j, nn, out_parts)[0], i, slab(j, nn, out_parts)[1]))
    else:
        out_shape = jax.ShapeDtypeStruct((m, n), out_dtype)
        out_spec = pl.BlockSpec((tm, tn), lambda i, j, kk: (i, j))
    in_specs = [a_spec, b_spec]
    operands = [a, b]
    if res is not None:
        in_specs.append(pl.BlockSpec((tm, tn), lambda i, j, kk: (i, j)))
        operands.append(res)

    def body(*refs):
        a_ref, b_ref = refs[0], refs[1]
        res_ref = refs[2] if res is not None else None
        o_ref = refs[3] if res is not None else refs[2]
        p = lax.dot_general(a_ref[...], b_ref[...], dims, preferred_element_type=F32)

        def finish(total):
            if res_ref is not None:
                total = res_ref[...] + total
            o_ref[...] = total.astype(out_dtype)

        if nk == 1:
            finish(p)
        else:
            acc_ref = refs[-1]
            kk = pl.program_id(2)

            @pl.when(kk == 0)
            def _():
                acc_ref[...] = p

            @pl.when(jnp.logical_and(kk > 0, kk < nk - 1))
            def _():
                acc_ref[...] += p

            @pl.when(kk == nk - 1)
            def _():
                finish(acc_ref[...] + p)

    scratch = [pltpu.VMEM((tm, tn), F32)] if nk > 1 else []
    return _ordered_call(
        body, name=name, out_shape=out_shape, grid=(nm, nn, nk), in_specs=in_specs, out_specs=out_spec,
        scratch_shapes=scratch, compiler_params=_params(("parallel", "parallel", "arbitrary")),
    )(*operands)


def _matmul_nt_slabs(a, b, *, tm, tn, name, a_parts=0):
    nslab, n, ks = b.shape
    m = a.shape[1] if a_parts else a.shape[0]
    tm, tn = min(tm, m), min(tn, n)
    assert m % tm == 0 and n % tn == 0, (name, m, n, tm, tn)
    if a_parts:
        per = nslab // a_parts
        assert per * a_parts == nslab and a.shape[2] == per * ks, (name, a.shape, b.shape)
        a_spec = pl.BlockSpec((a_parts, tm, per * ks), lambda i, j: (0, i, 0))
    else:
        assert a.shape[1] == nslab * ks, (name, a.shape, b.shape)
        a_spec = pl.BlockSpec((tm, nslab * ks), lambda i, j: (i, 0))

    def body(a_ref, b_ref, o_ref):
        total = None
        for sl in range(nslab):
            if a_parts:
                a_sl = a_ref[sl // per, :, (sl % per) * ks:(sl % per + 1) * ks]
            else:
                a_sl = a_ref[:, sl * ks:(sl + 1) * ks]
            p = lax.dot_general(a_sl, b_ref[sl], (((1,), (1,)), ((), ())), preferred_element_type=F32)
            total = p if total is None else total + p
        o_ref[...] = total

    return _ordered_call(
        body, name=name, out_shape=jax.ShapeDtypeStruct((m, n), F32), grid=(m // tm, n // tn),
        in_specs=[a_spec, pl.BlockSpec((nslab, tn, ks), lambda i, j: (0, j, 0))],
        out_specs=pl.BlockSpec((tm, tn), lambda i, j: (i, j)), compiler_params=_params(("parallel", "parallel")),
    )(a, b)


GRAD_HALVES = {
    "w_in": ("rows_of_slab", 1024, 896), "w_up": ("rows_of_slab", 1024, 1408), "w_o": ("rows_of_block", 256, 2048),
    "w_down": ("cols_of_block", 1408, 512)}


def _half_shape(name, shard_shape):
    r, cols = shard_shape
    return (r, cols // 2) if GRAD_HALVES[name][0] == "cols_of_block" else (r // 2, cols)


def _grad_half(name, a, g, sel, res, call_name, g_parts=0):
    kind, tm, tn = GRAD_HALVES[name]
    s, m = a.shape
    n = g.shape[0] * g.shape[2] if g_parts else g.shape[1]
    if kind == "rows_of_slab":
        rh, hc = m // 2, n // N_CHIPS
        per = hc // tn
        grid = (rh // tm, n // tn)
        a_map = lambda i, j, sel_ref: (0, sel_ref[0] * (rh // tm) + i)
        g_col = lambda i, j, sel_ref: j
        o_map = lambda i, j, sel_ref: (j // per, i, j % per)
    elif kind == "rows_of_block":
        rh, hc = m // N_CHIPS // 2, n
        assert tm == rh
        grid = (N_CHIPS, n // tn)
        a_map = lambda i, j, sel_ref: (0, 2 * i + sel_ref[0])
        g_col = lambda i, j, sel_ref: j
        o_map = lambda i, j, sel_ref: (i, 0, j)
    else:
        rh, hc = m // N_CHIPS, n // 2
        assert tm == rh
        grid = (N_CHIPS, hc // tn)
        a_map = lambda i, j, sel_ref: (0, i)
        g_col = lambda i, j, sel_ref: sel_ref[0] * (hc // tn) + j
        o_map = lambda i, j, sel_ref: (i, 0, j)
    if g_parts:
        g_per = (n // tn) // g_parts
        g_spec = pl.BlockSpec((None, s, tn), lambda i, j, sel_ref: (g_col(i, j, sel_ref) // g_per, 0, g_col(i, j, sel_ref) % g_per))
    else:
        g_spec = pl.BlockSpec((s, tn), lambda i, j, sel_ref: (0, g_col(i, j, sel_ref)))
    o_spec = pl.BlockSpec((None, tm, tn), o_map)
    in_specs = [pl.BlockSpec((s, tm), a_map), g_spec] + ([o_spec] if res is not None else [])

    def body(sel_ref, a_ref, g_ref, *rest):
        o_ref = rest[-1]
        p = lax.dot_general(a_ref[...], g_ref[...], (((0,), (0,)), ((), ())), preferred_element_type=F32)
        if res is not None:
            p = p + rest[0][...].astype(F32)
        o_ref[...] = p.astype(BF16)

    grid_spec = pltpu.PrefetchScalarGridSpec(num_scalar_prefetch=1, grid=grid, in_specs=in_specs, out_specs=o_spec)
    return _ordered_call(
        body, name=call_name, out_shape=jax.ShapeDtypeStruct((N_CHIPS, rh, hc), BF16), grid_spec=grid_spec,
        compiler_params=_params(("parallel", "parallel")),
    )(sel, a, g, *([res] if res is not None else []))


def _row_tile(s):
    return min(256, s)


def _rows(width, tr):
    return pl.BlockSpec((tr, width), lambda i: (i, 0))


def _const2(shape):
    return pl.BlockSpec(shape, lambda i: (0, 0))


def _rms_fwd(x, g, name):
    s, d = x.shape
    tr = _row_tile(s)

    def body(x_ref, g_ref, o_ref):
        xv = x_ref[...]
        r = lax.rsqrt(_mean_last(xv * xv) + EPS)
        o_ref[...] = (xv * r * g_ref[...]).astype(BF16)

    return _ordered_call(
        body, name=name, out_shape=jax.ShapeDtypeStruct((s, d), BF16), grid=(s // tr,),
        in_specs=[_rows(d, tr), _const2((1, d))], out_specs=_rows(d, tr), compiler_params=_params(("parallel",)),
    )(x, g)


def _rms_bwd(x, g, dh, dres, name):
    s, d = x.shape
    tr = _row_tile(s)

    def body(x_ref, g_ref, dh_ref, dres_ref, dx_ref, dxb_ref, dg_ref):
        xv, dy = x_ref[...], dh_ref[...]
        r = lax.rsqrt(_mean_last(xv * xv) + EPS)
        gdy = dy * g_ref[...]
        dx = dres_ref[...] + r * gdy - xv * ((r * r * r) * _mean_last(xv * gdy))
        dx_ref[...] = dx
        dxb_ref[...] = dx.astype(BF16)

        @pl.when(pl.program_id(0) == 0)
        def _():
            dg_ref[...] = jnp.zeros_like(dg_ref)

        dg_ref[...] += _sum_rows(xv * r * dy)

    return _ordered_call(
        body, name=name,
        out_shape=(jax.ShapeDtypeStruct((s, d), F32), jax.ShapeDtypeStruct((s, d), BF16), jax.ShapeDtypeStruct((1, d), F32)),
        grid=(s // tr,), in_specs=[_rows(d, tr), _const2((1, d)), _rows(d, tr), _rows(d, tr)],
        out_specs=(_rows(d, tr), _rows(d, tr), _const2((1, d))), compiler_params=_params(("arbitrary",)),
    )(x, g, dh, dres)


Q0, K0, V0, GU0, GV0 = 0, ATTN_WIDTH, ATTN_WIDTH + KV_WIDTH, ATTN_WIDTH + 2 * KV_WIDTH, ATTN_WIDTH + 2 * KV_WIDTH + GMLP_WIDTH


def _head(h, base=0):
    return slice(base + h * HEAD_DIM, base + (h + 1) * HEAD_DIM)


def _proj_post(z, qg, kg, lg, lb, cosf, sinf, name):
    s = z.shape[0]
    tr = _row_tile(s)

    def body(z_ref, qg_ref, kg_ref, lg_ref, lb_ref, cos_ref, sin_ref, qn_ref, kn_ref, vb_ref, ug_ref, vn_ref,
             dgu_ref, dgv_ref, xhat_ref, rstd_ref):
        cos, sin = cos_ref[...], sin_ref[...]

        def norm_rope(xh, g):
            y = xh * lax.rsqrt(_mean_last(xh * xh) + EPS) * g
            return y * cos + pltpu.roll(y, HEAD_DIM // 2, 1) * sin

        for h in range(N_Q_HEADS):
            qn_ref[:, _head(h)] = norm_rope(z_ref[:, _head(h, Q0)].astype(F32), qg_ref[...]).astype(BF16)
        for h in range(N_KV_HEADS):
            kn_ref[:, _head(h)] = norm_rope(z_ref[:, _head(h, K0)].astype(F32), kg_ref[...]).astype(BF16)
        vb_ref[...] = z_ref[:, V0:GU0]
        gu = z_ref[:, GU0:GV0].astype(F32)
        ug_ref[...] = _gelu(gu)
        dgu_ref[...] = _gelu_grad(gu).astype(BF16)
        gv = z_ref[:, GV0:IN_WIDTH].astype(F32)
        vg = _gelu(gv)
        dgv_ref[...] = _gelu_grad(gv).astype(BF16)
        xc = vg - _mean_last(vg)
        r = lax.rsqrt(_mean_last(xc * xc) + EPS)
        y = xc * r
        xhat_ref[...] = y.astype(BF16)
        rstd_ref[...] = r
        vn_ref[...] = (y * lg_ref[...] + lb_ref[...]).astype(BF16)

    wide = jax.ShapeDtypeStruct((s, GMLP_WIDTH), BF16)
    return _ordered_call(
        body, name=name,
        out_shape=(jax.ShapeDtypeStruct((s, ATTN_WIDTH), BF16), jax.ShapeDtypeStruct((s, KV_WIDTH), BF16),
                   jax.ShapeDtypeStruct((s, KV_WIDTH), BF16), jax.ShapeDtypeStruct((s, GMLP_WIDTH), F32), wide,
                   wide, wide, wide, jax.ShapeDtypeStruct((s, 1), F32)),
        grid=(s // tr,),
        in_specs=[_rows(IN_WIDTH, tr), _const2((1, HEAD_DIM)), _const2((1, HEAD_DIM)), _const2((1, GMLP_WIDTH)),
                  _const2((1, GMLP_WIDTH)), _rows(HEAD_DIM, tr), _rows(HEAD_DIM, tr)],
        out_specs=(_rows(ATTN_WIDTH, tr), _rows(KV_WIDTH, tr), _rows(KV_WIDTH, tr), _rows(GMLP_WIDTH, tr), _rows(GMLP_WIDTH, tr),
                   _rows(GMLP_WIDTH, tr), _rows(GMLP_WIDTH, tr), _rows(GMLP_WIDTH, tr), _rows(1, tr)),
        compiler_params=_params(("parallel",)),
    )(z, qg, kg, lg, lb, cosf, sinf)


def _proj_post_bwd(z, dqn, dkn, dvb, dug, dvn, gelu_grad_u, gelu_grad_v, xhat_v, rstd_v, qg, kg, lg, cosf, sinf, name):
    s = z.shape[0]
    tr = _row_tile(s)

    def body(z_ref, dqn_ref, dkn_ref, dvb_ref, dug_ref, dvn_ref, ggu_ref, ggv_ref, xhat_ref, rstd_ref, qg_ref, kg_ref, lg_ref,
             cos_ref, sin_ref, dz_ref, dqg_ref, dkg_ref, dlg_ref, dlb_ref):
        cos, sin = cos_ref[...], sin_ref[...]

        @pl.when(pl.program_id(0) == 0)
        def _():
            dqg_ref[...] = jnp.zeros_like(dqg_ref)
            dkg_ref[...] = jnp.zeros_like(dkg_ref)
            dlg_ref[...] = jnp.zeros_like(dlg_ref)
            dlb_ref[...] = jnp.zeros_like(dlb_ref)

        def norm_rope_bwd(xh, g, dout):
            dy = dout * cos - pltpu.roll(dout, HEAD_DIM // 2, 1) * sin
            r = lax.rsqrt(_mean_last(xh * xh) + EPS)
            xhat = xh * r
            gdy = dy * g
            return r * (gdy - xhat * _mean_last(xhat * gdy)), _sum_rows(xhat * dy)

        dqg = jnp.zeros((1, HEAD_DIM), F32)
        for h in range(N_Q_HEADS):
            dx, dg = norm_rope_bwd(z_ref[:, _head(h, Q0)].astype(F32), qg_ref[...], dqn_ref[:, _head(h)])
            dz_ref[:, _head(h, Q0)] = dx.astype(BF16)
            dqg = dqg + dg
        dqg_ref[...] += dqg
        dkg = jnp.zeros((1, HEAD_DIM), F32)
        for h in range(N_KV_HEADS):
            dx, dg = norm_rope_bwd(z_ref[:, _head(h, K0)].astype(F32), kg_ref[...], dkn_ref[:, _head(h)])
            dz_ref[:, _head(h, K0)] = dx.astype(BF16)
            dkg = dkg + dg
        dkg_ref[...] += dkg
        dz_ref[:, V0:GU0] = dvb_ref[...].astype(BF16)
        dz_ref[:, GU0:GV0] = (dug_ref[...] * ggu_ref[...].astype(F32)).astype(BF16)
        xhat = xhat_ref[...].astype(F32)
        dvn_v = dvn_ref[...]
        dlg_ref[...] += _sum_rows(xhat * dvn_v)
        dlb_ref[...] += _sum_rows(dvn_v)
        dxh = dvn_v * lg_ref[...]
        dvg = rstd_ref[...] * (dxh - _mean_last(dxh) - xhat * _mean_last(dxh * xhat))
        dz_ref[:, GV0:IN_WIDTH] = (dvg * ggv_ref[...].astype(F32)).astype(BF16)

    return _ordered_call(
        body, name=name,
        out_shape=(jax.ShapeDtypeStruct((s, IN_WIDTH), BF16), jax.ShapeDtypeStruct((1, HEAD_DIM), F32),
                   jax.ShapeDtypeStruct((1, HEAD_DIM), F32), jax.ShapeDtypeStruct((1, GMLP_WIDTH), F32),
                   jax.ShapeDtypeStruct((1, GMLP_WIDTH), F32)),
        grid=(s // tr,),
        in_specs=[_rows(V0, tr), _rows(ATTN_WIDTH, tr), _rows(KV_WIDTH, tr), _rows(KV_WIDTH, tr), _rows(GMLP_WIDTH, tr),
                  _rows(GMLP_WIDTH, tr), _rows(GMLP_WIDTH, tr), _rows(GMLP_WIDTH, tr), _rows(GMLP_WIDTH, tr), _rows(1, tr),
                  _const2((1, HEAD_DIM)), _const2((1, HEAD_DIM)), _const2((1, GMLP_WIDTH)), _rows(HEAD_DIM, tr),
                  _rows(HEAD_DIM, tr)],
        out_specs=(_rows(IN_WIDTH, tr), _const2((1, HEAD_DIM)), _const2((1, HEAD_DIM)), _const2((1, GMLP_WIDTH)),
                   _const2((1, GMLP_WIDTH))),
        compiler_params=_params(("arbitrary",)),
    )(z, dqn, dkn, dvb, dug, dvn, gelu_grad_u, gelu_grad_v, xhat_v, rstd_v, qg, kg, lg, cosf, sinf)


def _band_valid(n, s):
    shape = (GQA_GROUP * BLOCK, 3 * BLOCK)
    i = lax.broadcasted_iota(jnp.int32, shape, 0) & (BLOCK - 1)
    j = lax.broadcasted_iota(jnp.int32, shape, 1)
    k_pos = n * BLOCK - BLOCK + j
    return (jnp.abs(j - BLOCK - i) <= BLOCK) & (k_pos >= 0) & (k_pos < s)


def _group_rows(x, kh):
    return jnp.concatenate([x[:, _head(kh * GQA_GROUP + g)] for g in range(GQA_GROUP)], axis=0)


def _group_sinks(sink_ref, kh):
    return jnp.concatenate([jnp.full((BLOCK, 1), sink_ref[kh * GQA_GROUP + g], F32) for g in range(GQA_GROUP)], axis=0)


def _rows_of(x, g):
    return x[g * BLOCK:(g + 1) * BLOCK]


def _probs(q, kb, sink_h, valid):
    sc = lax.dot_general(q, kb, (((1,), (1,)), ((), ())), preferred_element_type=F32) * (HEAD_DIM ** -0.5)
    sc = jnp.where(valid, sc, MASK_VALUE)
    m = jnp.maximum(jnp.max(sc, axis=-1, keepdims=True), sink_h)
    p = jnp.exp(sc - m)
    es = jnp.exp(sink_h - m)
    den = jnp.sum(p, axis=-1, keepdims=True) + es
    inv = 1.0 / den
    return p * inv, es * inv


def _band_specs(width, nb):
    return [pl.BlockSpec((BLOCK, width), lambda n: (jnp.maximum(n - 1, 0), 0)),
            pl.BlockSpec((BLOCK, width), lambda n: (n, 0)),
            pl.BlockSpec((BLOCK, width), lambda n: (jnp.minimum(n + 1, nb - 1), 0))]


def _blk(width):
    return pl.BlockSpec((BLOCK, width), lambda n: (n, 0))


def _whole3(shape):
    return pl.BlockSpec(shape, lambda n: (0, 0, 0))


def _smem():
    return pl.BlockSpec(memory_space=pltpu.SMEM)


def _mixer_fwd(qn, kn, vb, ug, vn, wsb, bsb, sink, ga, gs, name):
    s = qn.shape[0]
    nb = s // BLOCK

    def body(sink_ref, q_ref, kp_ref, kc_ref, kx_ref, vp_ref, vc_ref, vx_ref, ug_ref, vn_ref, ws_ref, bs_ref, ga_ref, gs_ref,
             attn_ref, sgu_ref, mix_ref, probs_ref, psink_ref):
        n = pl.program_id(0)
        valid = _band_valid(n, s)
        ssq = jnp.zeros((BLOCK, 1), F32)
        for kh in range(N_KV_HEADS):
            kb = jnp.concatenate([kp_ref[:, _head(kh)], kc_ref[:, _head(kh)], kx_ref[:, _head(kh)]], axis=0)
            vbd = jnp.concatenate([vp_ref[:, _head(kh)], vc_ref[:, _head(kh)], vx_ref[:, _head(kh)]], axis=0)
            p, p_sink = _probs(_group_rows(q_ref, kh), kb, _group_sinks(sink_ref, kh), valid)
            pb = p.astype(BF16)
            probs_ref[kh] = pb
            psink_ref[kh] = p_sink
            o4 = jnp.dot(pb, vbd, preferred_element_type=F32)
            for g in range(GQA_GROUP):
                o = _rows_of(o4, g)
                attn_ref[:, _head(kh * GQA_GROUP + g)] = o
                ssq = ssq + jnp.sum(o * o, axis=-1, keepdims=True)
        r = lax.rsqrt(ssq * (1.0 / ATTN_WIDTH) + EPS)
        mix_ref[:, 0:ATTN_WIDTH] = (attn_ref[...] * r * ga_ref[...]).astype(BF16)
        ssq = jnp.zeros((BLOCK, 1), F32)
        for h in range(N_GMLP_HEADS):
            f = jnp.dot(ws_ref[h], vn_ref[:, _head(h)], preferred_element_type=F32) + bs_ref[h]
            o = ug_ref[:, _head(h)] * f
            sgu_ref[:, _head(h)] = o
            ssq = ssq + jnp.sum(o * o, axis=-1, keepdims=True)
        r = lax.rsqrt(ssq * (1.0 / GMLP_WIDTH) + EPS)
        mix_ref[:, ATTN_WIDTH:D_MODEL] = (sgu_ref[...] * r * gs_ref[...]).astype(BF16)

    hh = (N_GMLP_HEADS, BLOCK, BLOCK)
    return _ordered_call(
        body, name=name,
        out_shape=(jax.ShapeDtypeStruct((s, ATTN_WIDTH), F32), jax.ShapeDtypeStruct((s, GMLP_WIDTH), F32),
                   jax.ShapeDtypeStruct((s, D_MODEL), BF16), jax.ShapeDtypeStruct((nb,) + PROBS_BLOCK, BF16),
                   jax.ShapeDtypeStruct((nb,) + PSINK_BLOCK, F32)),
        grid=(nb,),
        in_specs=[_smem(), _blk(ATTN_WIDTH)] + _band_specs(KV_WIDTH, nb) + _band_specs(KV_WIDTH, nb)
        + [_blk(GMLP_WIDTH), _blk(GMLP_WIDTH), _whole3(hh), _whole3(hh),
           pl.BlockSpec((1, ATTN_WIDTH), lambda n: (0, 0)), pl.BlockSpec((1, GMLP_WIDTH), lambda n: (0, 0))],
        out_specs=(_blk(ATTN_WIDTH), _blk(GMLP_WIDTH), _blk(D_MODEL), _per_block(PROBS_BLOCK), _per_block(PSINK_BLOCK)),
        compiler_params=_params(("parallel",)),
    )(sink, qn, kn, kn, kn, vb, vb, vb, ug, vn, wsb, bsb, ga, gs)


PROBS_BLOCK = (N_KV_HEADS, GQA_GROUP * BLOCK, 3 * BLOCK)
PSINK_BLOCK = (N_KV_HEADS, GQA_GROUP * BLOCK, 1)


def _per_block(shape):
    return pl.BlockSpec((None,) + shape, lambda n: (n, 0, 0, 0))


def _mixer_bwd(qn, kn, vb, ug, vn, attn, sgu, dmixed, wsb, bsb, ga, gs, probs, psink, name):
    s = qn.shape[0]
    nb = s // BLOCK
    tn_dims = (((0,), (0,)), ((), ()))
    nt_dims = (((1,), (1,)), ((), ()))

    def body(q_ref, kp_ref, kc_ref, kx_ref, vp_ref, vc_ref, vx_ref, ug_ref, vn_ref, attn_ref, sgu_ref, dm_ref,
             ws_ref, bs_ref, ga_ref, gs_ref, probs_ref, psink_ref,
             dq_ref, dk_ref, dv_ref, dug_ref, dvn_ref, dws_ref, dbs_ref, dsk_ref, dga_ref, dgs_ref, dk_acc, dv_acc):
        n = pl.program_id(0)

        @pl.when(n == 0)
        def _():
            for ref in (dk_acc, dv_acc, dws_ref, dbs_ref, dsk_ref, dga_ref, dgs_ref):
                ref[...] = jnp.zeros_like(ref)

        def out_norm_bwd(o, g, dy):
            r = lax.rsqrt(_mean_last(o * o) + EPS)
            gdy = dy * g
            return r * gdy - o * ((r * r * r) * _mean_last(o * gdy)), _sum_rows(o * r * dy)

        d_attn, dga = out_norm_bwd(attn_ref[...], ga_ref[...], dm_ref[:, 0:ATTN_WIDTH])
        dga_ref[...] += dga
        d_sgu, dgs = out_norm_bwd(sgu_ref[...], gs_ref[...], dm_ref[:, ATTN_WIDTH:D_MODEL])
        dgs_ref[...] += dgs

        for h in range(N_GMLP_HEADS):
            vn_h = vn_ref[:, _head(h)]
            f = jnp.dot(ws_ref[h], vn_h, preferred_element_type=F32) + bs_ref[h]
            ds_h = d_sgu[:, _head(h)]
            dug_ref[:, _head(h)] = ds_h * f
            df = ds_h * ug_ref[:, _head(h)]
            dfb = df.astype(BF16)
            dvn_ref[:, _head(h)] = lax.dot_general(ws_ref[h], dfb, tn_dims, preferred_element_type=F32)
            dws_ref[h] += lax.dot_general(dfb, vn_h, nt_dims, preferred_element_type=F32)
            dbs_ref[h] += jnp.broadcast_to(jnp.sum(df, axis=-1, keepdims=True), (BLOCK, BLOCK))

        row0 = pl.multiple_of(n * BLOCK, BLOCK)
        for kh in range(N_KV_HEADS):
            kb = jnp.concatenate([kp_ref[:, _head(kh)], kc_ref[:, _head(kh)], kx_ref[:, _head(kh)]], axis=0)
            vbd = jnp.concatenate([vp_ref[:, _head(kh)], vc_ref[:, _head(kh)], vx_ref[:, _head(kh)]], axis=0)
            q4 = _group_rows(q_ref, kh)
            pb = probs_ref[kh]
            p = pb.astype(F32)
            do4 = _group_rows(d_attn, kh).astype(BF16)
            dp = lax.dot_general(do4, vbd, nt_dims, preferred_element_type=F32)
            delta = jnp.sum(p * dp, axis=-1, keepdims=True)
            dsc = (p * (dp - delta) * (HEAD_DIM ** -0.5)).astype(BF16)
            d_sink = -(psink_ref[kh] * delta)
            dq4 = jnp.dot(dsc, kb, preferred_element_type=F32)
            for g in range(GQA_GROUP):
                h = kh * GQA_GROUP + g
                dsk_ref[h:h + 1, :] += jnp.broadcast_to(_sum_all(_rows_of(d_sink, g)), (1, BLOCK))
                dq_ref[:, _head(h)] = _rows_of(dq4, g)
            dk_acc[pl.ds(row0, 3 * BLOCK), _head(kh)] += lax.dot_general(dsc, q4, tn_dims, preferred_element_type=F32)
            dv_acc[pl.ds(row0, 3 * BLOCK), _head(kh)] += lax.dot_general(pb, do4, tn_dims, preferred_element_type=F32)

        @pl.when(n == nb - 1)
        def _():
            dk_ref[...] = dk_acc[BLOCK:BLOCK + s, :]
            dv_ref[...] = dv_acc[BLOCK:BLOCK + s, :]

    hh = (N_GMLP_HEADS, BLOCK, BLOCK)
    full_kv = pl.BlockSpec((s, KV_WIDTH), lambda n: (0, 0))
    return _ordered_call(
        body, name=name,
        out_shape=(jax.ShapeDtypeStruct((s, ATTN_WIDTH), F32), jax.ShapeDtypeStruct((s, KV_WIDTH), F32),
                   jax.ShapeDtypeStruct((s, KV_WIDTH), F32), jax.ShapeDtypeStruct((s, GMLP_WIDTH), F32),
                   jax.ShapeDtypeStruct((s, GMLP_WIDTH), F32), jax.ShapeDtypeStruct(hh, F32), jax.ShapeDtypeStruct(hh, F32),
                   jax.ShapeDtypeStruct((N_Q_HEADS, BLOCK), F32), jax.ShapeDtypeStruct((1, ATTN_WIDTH), F32),
                   jax.ShapeDtypeStruct((1, GMLP_WIDTH), F32)),
        grid=(nb,),
        in_specs=[_blk(ATTN_WIDTH)] + _band_specs(KV_WIDTH, nb) + _band_specs(KV_WIDTH, nb)
        + [_blk(GMLP_WIDTH), _blk(GMLP_WIDTH), _blk(ATTN_WIDTH), _blk(GMLP_WIDTH), _blk(D_MODEL), _whole3(hh), _whole3(hh),
           pl.BlockSpec((1, ATTN_WIDTH), lambda n: (0, 0)), pl.BlockSpec((1, GMLP_WIDTH), lambda n: (0, 0)),
           _per_block(PROBS_BLOCK), _per_block(PSINK_BLOCK)],
        out_specs=(_blk(ATTN_WIDTH), full_kv, full_kv, _blk(GMLP_WIDTH), _blk(GMLP_WIDTH), _whole3(hh), _whole3(hh),
                   pl.BlockSpec((N_Q_HEADS, BLOCK), lambda n: (0, 0)), pl.BlockSpec((1, ATTN_WIDTH), lambda n: (0, 0)),
                   pl.BlockSpec((1, GMLP_WIDTH), lambda n: (0, 0))),
        scratch_shapes=[pltpu.VMEM((s + 2 * BLOCK, KV_WIDTH), F32), pltpu.VMEM((s + 2 * BLOCK, KV_WIDTH), F32)],
        compiler_params=_params(("arbitrary",)),
    )(qn, kn, kn, kn, vb, vb, vb, ug, vn, attn, sgu, dmixed, wsb, bsb, ga, gs, probs, psink)


CONV_TILE = 128


PAD_ROWS = 8


def _zero_pad_rows(pad_ref):
    s = pad_ref.shape[0] - 2 * PAD_ROWS
    zeros = jnp.zeros((PAD_ROWS, pad_ref.shape[1]), F32)
    pad_ref[0:PAD_ROWS, :] = zeros
    pad_ref[PAD_ROWS + s:2 * PAD_ROWS + s, :] = zeros


def _shift_rows(a, pad_ref):
    s = a.shape[0]
    pad_ref[PAD_ROWS:PAD_ROWS + s, :] = a
    padded = pad_ref[...]
    prev = pltpu.roll(padded, 1, 0)[PAD_ROWS:PAD_ROWS + s]
    nxt = pltpu.roll(padded, s + 2 * PAD_ROWS - 1, 0)[PAD_ROWS:PAD_ROWS + s]
    return prev, nxt


def _conv_specs(s):
    tc = CONV_TILE
    nj = D_FF // tc
    return (tc, nj, pl.BlockSpec((2, s, tc), lambda j: (0, 0, j)),
            [pl.BlockSpec((3, tc), lambda j: (0, j)), pl.BlockSpec((3, tc), lambda j: (0, j + nj))],
            [pl.BlockSpec((1, tc), lambda j: (0, j)), pl.BlockSpec((1, tc), lambda j: (0, j + nj))])


def _conv_gate_fwd(a_pre, cw, cb, name):
    s = a_pre.shape[1]
    tc, nj, a_spec, w_specs, b_specs = _conv_specs(s)

    def body(a_ref, wg_ref, wu_ref, bg_ref, bu_ref, act_ref, dgu_ref, pad_ref):
        _zero_pad_rows(pad_ref)

        def conv(a, w_ref, b_ref):
            prev, nxt = _shift_rows(a, pad_ref)
            return b_ref[...] + prev * w_ref[0:1, :] + a * w_ref[1:2, :] + nxt * w_ref[2:3, :]

        g = conv(a_ref[0].astype(F32), wg_ref, bg_ref)
        u = conv(a_ref[1].astype(F32), wu_ref, bu_ref)
        sg = 1.0 / (1.0 + jnp.exp(-g))
        silu = g * sg
        act_ref[...] = (silu * u).astype(BF16)
        dgu_ref[0] = (u * (sg * (1.0 + g * (1.0 - sg)))).astype(BF16)
        dgu_ref[1] = silu.astype(BF16)

    return _ordered_call(
        body, name=name, out_shape=(jax.ShapeDtypeStruct((s, D_FF), BF16), jax.ShapeDtypeStruct((2, s, D_FF), BF16)),
        grid=(nj,), in_specs=[a_spec] + w_specs + b_specs,
        out_specs=(pl.BlockSpec((s, tc), lambda j: (0, j)), pl.BlockSpec((2, s, tc), lambda j: (0, 0, j))),
        scratch_shapes=[pltpu.VMEM((s + 2 * PAD_ROWS, tc), F32)], compiler_params=_params(("parallel",)),
    )(a_pre, cw, cw, cb, cb)


def _conv_gate_bwd(a_pre, dgu, cw, dact, name):
    s = a_pre.shape[1]
    tc, nj, a_spec, w_specs, _ = _conv_specs(s)

    def body(a_ref, dgu_ref, wg_ref, wu_ref, dact_ref, dap_ref, dcw_ref, dcb_ref, pad_ref):
        _zero_pad_rows(pad_ref)
        dact_v = dact_ref[...].astype(F32)
        for part, w_ref in enumerate((wg_ref, wu_ref)):
            da = dact_v * dgu_ref[part].astype(F32)
            a = a_ref[part].astype(F32)
            da_prev, da_next = _shift_rows(da, pad_ref)
            dcw_ref[part, 0:1, :] = _sum_rows(a * da_next)
            dcw_ref[part, 1:2, :] = _sum_rows(a * da)
            dcw_ref[part, 2:3, :] = _sum_rows(a * da_prev)
            dcb_ref[part] = _sum_rows(da)
            dap_ref[part] = (da_next * w_ref[0:1, :] + da * w_ref[1:2, :] + da_prev * w_ref[2:3, :]).astype(BF16)

    return _ordered_call(
        body, name=name,
        out_shape=(jax.ShapeDtypeStruct((2, s, D_FF), BF16), jax.ShapeDtypeStruct((2, 3, D_FF), F32),
                   jax.ShapeDtypeStruct((2, 1, D_FF), F32)),
        grid=(nj,),
        in_specs=[a_spec, pl.BlockSpec((2, s, tc), lambda j: (0, 0, j))] + w_specs + [pl.BlockSpec((s, tc), lambda j: (0, j))],
        out_specs=(pl.BlockSpec((2, s, tc), lambda j: (0, 0, j)), pl.BlockSpec((2, 3, tc), lambda j: (0, 0, j)),
                   pl.BlockSpec((2, 1, tc), lambda j: (0, 0, j))),
        scratch_shapes=[pltpu.VMEM((s + 2 * PAD_ROWS, tc), F32)], compiler_params=_params(("parallel",)),
    )(a_pre, dgu, cw, cw, dact)


def _loss_head(y, target, name):
    s, d = y.shape
    tr = _row_tile(s)

    def body(y_ref, t_ref, loss_ref, dy_ref, dyb_ref):
        err = y_ref[...] - t_ref[...]

        @pl.when(pl.program_id(0) == 0)
        def _():
            loss_ref[...] = jnp.zeros_like(loss_ref)

        loss_ref[...] += jnp.broadcast_to(0.5 * _sum_all(_mean_last(err * err)), (8, 128))
        dy = err * (1.0 / d)
        dy_ref[...] = dy
        dyb_ref[...] = dy.astype(BF16)

    return _ordered_call(
        body, name=name,
        out_shape=(jax.ShapeDtypeStruct((8, 128), F32), jax.ShapeDtypeStruct((s, d), F32), jax.ShapeDtypeStruct((s, d), BF16)),
        grid=(s // tr,), in_specs=[_rows(d, tr), _rows(d, tr)],
        out_specs=(_const2((8, 128)), _rows(d, tr), _rows(d, tr)), compiler_params=_params(("arbitrary",)),
    )(y, target)


def _row_block(rows, cols, budget=1 << 20):
    if rows * cols <= budget:
        return rows
    best = None
    for tr in range(16, rows, 16):
        if rows % tr == 0 and tr * cols <= budget:
            best = tr
    assert best is not None, (rows, cols)
    return best


def _place_shard(x4, layer, j_arr, out_dtype, name):
    _, nh, r, cols = x4.shape
    tr = _row_block(r, cols)

    def body(j_ref, x_ref, o_ref):
        o_ref[...] = x_ref[...].astype(out_dtype)

    grid_spec = pltpu.PrefetchScalarGridSpec(
        num_scalar_prefetch=1, grid=(nh, r // tr),
        in_specs=[pl.BlockSpec((None, None, tr, cols), lambda h, i, j_ref: (layer, h, i, 0))],
        out_specs=pl.BlockSpec((None, None, tr, cols), lambda h, i, j_ref: (j_ref[0], h, i, 0)))
    return _ordered_call(
        body, name=name, out_shape=jax.ShapeDtypeStruct((N_CHIPS, nh, r, cols), out_dtype), grid_spec=grid_spec,
        compiler_params=_params(("parallel", "parallel")),
    )(j_arr, x4)


def _adamw(w, g, m, v, name):
    rows, cols = w.shape
    tr = _row_block(rows, cols, 1 << 18)

    def body(w_ref, g_ref, m_ref, v_ref, go_ref, d_ref, nm_ref, nv_ref):
        gv = g_ref[...]
        go_ref[...] = gv
        mn = ADAM_B1 * m_ref[...] + (1.0 - ADAM_B1) * gv
        vn = ADAM_B2 * v_ref[...] + (1.0 - ADAM_B2) * (gv * gv)
        m_hat = mn / (1.0 - ADAM_B1 ** ADAM_STEP)
        v_hat = vn / (1.0 - ADAM_B2 ** ADAM_STEP)
        d_ref[...] = -ADAM_LR * (m_hat / (jnp.sqrt(v_hat) + ADAM_EPS) + ADAM_WD * w_ref[...])
        nm_ref[...] = mn
        nv_ref[...] = vn

    sds = jax.ShapeDtypeStruct((rows, cols), F32)
    return _ordered_call(
        body, name=name, out_shape=(sds, sds, sds, sds), grid=(rows // tr,),
        in_specs=[_rows(cols, tr)] * 4, out_specs=(_rows(cols, tr),) * 4, compiler_params=_params(("parallel",)),
    )(w, g, m, v)


def _chip_sum(p4, recv3, j_arr, c_arr, name):
    _, rh, cols = p4.shape
    tr = _row_block(rh, cols, 1 << 19)

    def body(j_ref, c_ref, p_ref, r_ref, o_ref):
        total = p_ref[...].astype(F32)
        for peer in range(3):
            total = total + r_ref[peer].astype(F32)
        o_ref[...] = total.astype(BF16)

    grid_spec = pltpu.PrefetchScalarGridSpec(
        num_scalar_prefetch=2, grid=(rh // tr,),
        in_specs=[pl.BlockSpec((None, tr, cols), lambda i, j_ref, c_ref: (j_ref[0], i, 0)),
                  pl.BlockSpec((3, tr, cols), lambda i, j_ref, c_ref: (0, i, 0))],
        out_specs=pl.BlockSpec((None, tr, cols), lambda i, j_ref, c_ref: (c_ref[0], i, 0)))
    return _ordered_call(
        body, name=name, out_shape=jax.ShapeDtypeStruct((2, rh, cols), BF16), grid_spec=grid_spec,
        compiler_params=_params(("parallel",)),
    )(j_arr, c_arr, p4, recv3)


def _adamw_layer(w, g, m, v, layer, into, name):
    nl, rows, cols = w.shape
    slabs, _, width = g.shape
    assert slabs * width == cols and g.shape[1] == rows, (name, w.shape, g.shape)
    tr = _row_block(rows, width, 1 << 19)
    at_layer = pl.BlockSpec((None, tr, width), lambda h, i: (layer, i, h))

    def body(w_ref, g_ref, m_ref, v_ref, *rest):
        go_ref, d_ref, nm_ref, nv_ref = rest[-4:]
        gv = g_ref[...].astype(F32)
        go_ref[...] = gv
        mn = ADAM_B1 * m_ref[...] + (1.0 - ADAM_B1) * gv
        vn = ADAM_B2 * v_ref[...] + (1.0 - ADAM_B2) * (gv * gv)
        m_hat = mn / (1.0 - ADAM_B1 ** ADAM_STEP)
        v_hat = vn / (1.0 - ADAM_B2 ** ADAM_STEP)
        d_ref[...] = -ADAM_LR * (m_hat / (jnp.sqrt(v_hat) + ADAM_EPS) + ADAM_WD * w_ref[...])
        nm_ref[...] = mn
        nv_ref[...] = vn

    in_specs = [at_layer, pl.BlockSpec((None, tr, width), lambda h, i: (h, i, 0)), at_layer, at_layer]
    operands = [w, g, m, v]
    aliases = {}
    if into is not None:
        in_specs += [ANY] * 4
        operands += list(into)
        aliases = {4 + i: i for i in range(4)}
    sds = jax.ShapeDtypeStruct((nl, rows, cols), F32)
    return _ordered_call(
        body, name=name, out_shape=(sds,) * 4, grid=(slabs, rows // tr), in_specs=in_specs, out_specs=(at_layer,) * 4,
        input_output_aliases=aliases, compiler_params=_params(("parallel", "parallel")),
    )(*operands)


def _sum_devices(mine, landed, me_arr, name):
    rows, lanes = mine.shape

    def body(me_ref, mine_ref, landed_ref, o_ref):
        total = None
        for dev in range(8):
            part = jnp.where(me_ref[0] == dev, mine_ref[...], landed_ref[dev])
            total = part if total is None else total + part
        o_ref[...] = total

    grid_spec = pltpu.PrefetchScalarGridSpec(
        num_scalar_prefetch=1, grid=(1,),
        in_specs=[pl.BlockSpec((rows, lanes), lambda i, me_ref: (0, 0)), pl.BlockSpec((8, rows, lanes), lambda i, me_ref: (0, 0, 0))],
        out_specs=pl.BlockSpec((rows, lanes), lambda i, me_ref: (0, 0)))
    return _ordered_call(
        body, name=name, out_shape=jax.ShapeDtypeStruct((rows, lanes), F32), grid_spec=grid_spec,
        compiler_params=_params(("arbitrary",)),
    )(me_arr, mine, landed)


def _place():
    x, y, c = lax.axis_index("x"), lax.axis_index("y"), lax.axis_index("c")
    chips = [(1 - x, y), (x, 1 - y), (1 - x, 1 - y)]
    return x, y, c, chips


HBM = pl.BlockSpec(memory_space=pltpu.HBM)
SEM = pl.BlockSpec(memory_space=pltpu.SEMAPHORE)
TOKEN = jax.ShapeDtypeStruct((8, 128), F32)


def _remote(src, dst, send_sem, recv_sem, to):
    return pltpu.make_async_remote_copy(src_ref=src, dst_ref=dst, send_sem=send_sem, recv_sem=recv_sem, device_id=to,
                                        device_id_type=MESH)


def _split_call(body, name, thru, sems_in=(), fresh=(), new_sems=(), after_last=True):
    n_t, n_s, n_f = len(thru), len(sems_in), len(fresh)

    def call_body(*refs):
        outs = refs[n_t + n_s:]
        body(refs[:n_t], refs[n_t:n_t + n_s], outs[1 + n_t:1 + n_t + n_f], outs[1 + n_t + n_f:])
        outs[0][...] = jnp.zeros_like(outs[0])

    out_shape = ([TOKEN] + [pltpu.HBM(t.shape, t.dtype) for t in thru] + [pltpu.HBM(shp, dt) for shp, dt in fresh]
                 + [pltpu.SemaphoreType.DMA(shp) for shp in new_sems])
    out_specs = [pl.BlockSpec(memory_space=pltpu.VMEM)] + [HBM] * (n_t + n_f) + [SEM] * len(new_sems)
    if not after_last or any(t is _Order.last for t in thru):
        _Order.last = None
    out = _ordered_call(
        call_body, name=name, out_shape=tuple(out_shape), in_specs=[HBM] * n_t + [SEM] * n_s, out_specs=tuple(out_specs),
        input_output_aliases={i: 1 + i for i in range(n_t)},
        compiler_params=pltpu.CompilerParams(has_side_effects=pltpu.SideEffectType.DATAFLOW_SIDE_EFFECTING),
    )(*[pltpu.with_memory_space_constraint(t, pltpu.HBM) for t in thru], *sems_in)
    return out[1:1 + n_t], out[1 + n_t:1 + n_t + n_f], out[1 + n_t + n_f:]


class _Exchange:
    def __init__(self, weights, m_in, v_in, j_arr, c_arr, me_arr):
        self.w, self.m, self.v = weights, m_in, v_in
        self.j_arr, self.c_arr, self.me_arr = j_arr, c_arr, me_arr
        self.adam, self.small, self.pairs, self.held = {}, {}, {}, None
        self.o_arr = 1 - c_arr
        self.groups = [(l, name) for l in range(DEPTH) for name in BIG_NAMES]
        self.shard_shape = {name: weights[name].shape[1:] for name in BIG_NAMES}
        self.conv_state, self.state = [], {}
        self.ready, self.conv_ready = {}, {}
        self.pending, self.tick, self.reduced = [], 0, {}

        def place(grp):
            l, name = grp
            nl, r, cols = weights[name].shape
            return _place_shard(weights[name].reshape(nl, 2, r // 2, cols), l, j_arr, BF16, f"place_{name}_l{l}")

        def start_copies(tag, convs, groups, bufs):
            n_c = len(convs)

            def start(thru, _, __, sems):
                x, y, c, chips = _place()
                j_me = 2 * x + y
                copies = []
                for i in range(len(thru)):
                    mine = thru[i].at[j_me] if i < n_c else thru[i].at[j_me, c]
                    copies += [_remote(mine, mine, sems[2 * i].at[k], sems[2 * i + 1].at[k], (*chip, c))
                               for k, chip in enumerate(chips)]
                for cp in copies:
                    cp.start()

            thru, _, sems = _split_call(start, tag, convs + bufs, new_sems=[(3,)] * (2 * (n_c + len(bufs))))
            self.conv_state += [(thru[i], sems[2 * i], sems[2 * i + 1]) for i in range(n_c)]
            for g, grp in enumerate(groups):
                self.state[grp] = (thru[n_c + g], sems[2 * (n_c + g)], sems[2 * (n_c + g) + 1])

        convs = [_place_shard(weights["conv_w"][:, None], l, j_arr, F32, f"place_conv_w_l{l}") for l in range(DEPTH)]
        start_copies("gather_start_first", convs, self.groups[:1], [place(self.groups[0])])
        start_copies("gather_start_rest", [], self.groups[1:], [place(grp) for grp in self.groups[1:]])

    def conv_w(self, l):
        if l not in self.conv_ready:
            buf, send, recv = self.conv_state[l]

            def wait(thru, sems, _, __):
                x, y, c, chips = _place()
                for k, chip in enumerate(chips):
                    mine, theirs = thru[0].at[2 * x + y], thru[0].at[2 * chip[0] + chip[1]]
                    _remote(mine, mine, sems[0].at[k], sems[1].at[k], (*chip, c)).wait_send()
                    _remote(theirs, theirs, sems[0].at[k], sems[1].at[k], (x, y, c)).wait_recv()

            (buf,), _, _ = _split_call(wait, f"gather_conv_w_l{l}", [buf], sems_in=[send, recv])
            self.conv_ready[l] = jnp.transpose(buf[:, 0], (1, 0, 2)).reshape(3, 2 * D_FF)
        return self.conv_ready[l]

    def weight(self, l, name):
        grp = (l, name)
        if grp not in self.ready:
            buf, send, recv = self.state[grp]

            def forward(thru, sems, _, new):
                x, y, c, chips = _place()
                for k, chip in enumerate(chips):
                    landed = thru[0].at[2 * chip[0] + chip[1], c]
                    _remote(landed, landed, new[0].at[k], sems[0].at[k], (x, y, c)).wait_recv()
                    _remote(landed, landed, new[0].at[k], new[1].at[k], (x, y, 1 - c)).start()

            (buf,), _, (fsend, frecv) = _split_call(forward, f"gather_pass_{name}_l{l}", [buf], sems_in=[recv],
                                                    new_sems=[(3,), (3,)])

            def finish(thru, sems, _, __):
                x, y, c, chips = _place()
                mine = thru[0].at[2 * x + y, c]
                for k, chip in enumerate(chips):
                    j_k = 2 * chip[0] + chip[1]
                    theirs, landed = thru[0].at[j_k, 1 - c], thru[0].at[j_k, c]
                    _remote(theirs, theirs, sems[1].at[k], sems[2].at[k], (x, y, c)).wait_recv()
                    _remote(landed, landed, sems[1].at[k], sems[2].at[k], (x, y, 1 - c)).wait_send()
                    _remote(mine, mine, sems[0].at[k], sems[2].at[k], (*chip, c)).wait_send()

            (buf,), _, _ = _split_call(finish, f"gather_done_{name}_l{l}", [buf], sems_in=[send, fsend, frecv])
            r, cols = self.shard_shape[name]
            self.ready[grp] = buf.reshape(N_CHIPS, r, cols) if name in ("w_in", "w_up") else buf.reshape(N_CHIPS * r, cols)
        return self.ready[grp]

    def pair_send(self, l, name, other):
        held = self.held
        self.held = None

        def start(thru, _, fresh, sems):
            x, y, c, chips = _place()
            copies = [_remote(thru[0], fresh[0], sems[0], sems[1], (x, y, 1 - c))]
            if held is not None:
                copies += [_remote(thru[1].at[2 * chip[0] + chip[1]], fresh[1].at[k], sems[2].at[k], sems[3].at[k], (*chip, c))
                           for k, chip in enumerate(chips)]
            for cp in copies:
                cp.start()

        thru, fresh, new_sems = [other], [(other.shape, BF16)], [(), ()]
        if held is not None:
            thru, fresh, new_sems = thru + [held[2]], fresh + [((3,) + held[2].shape[1:], BF16)], new_sems + [(3,), (3,)]
        thru, fresh, sems = _split_call(start, f"pair_start_{name}_l{l}", thru, fresh=fresh, new_sems=new_sems, after_last=False)
        self.pairs[(l, name)] = (thru[0], fresh[0], sems[:2])
        if held is not None:
            self.pending.append(dict(l=held[0], name=held[1], stage=2, at=self.tick, bufs=(thru[1], fresh[1]), sems=sems[2:]))

    def pair_recv(self, l, name):
        other, recv, sems = self.pairs.pop((l, name))

        def wait(thru, sems, _, __):
            x, y, c, _chips = _place()
            cp = _remote(thru[0], thru[1], sems[0], sems[1], (x, y, 1 - c))
            cp.wait_send()
            cp.wait_recv()

        (_, recv), _, _ = _split_call(wait, f"pair_done_{name}_l{l}", [other, recv], sems_in=list(sems))
        return recv

    def scatter(self, l, name, p4):
        assert self.held is None
        self.held = (l, name, p4)
        if (l, name) == (0, BIG_NAMES[0]):
            self._scatter_held()

    def _scatter_held(self):
        l, name, p4 = self.held
        self.held = None

        def start(thru, _, fresh, sems):
            x, y, c, chips = _place()
            for k, chip in enumerate(chips):
                _remote(thru[0].at[2 * chip[0] + chip[1]], fresh[0].at[k], sems[0].at[k], sems[1].at[k], (*chip, c)).start()

        (p4,), (recv3,), sems = _split_call(start, f"chips_start_{name}_l{l}", [p4], fresh=[((3,) + p4.shape[1:], BF16)],
                                           new_sems=[(3,), (3,)], after_last=False)
        self.pending.append(dict(l=l, name=name, stage=2, at=self.tick, bufs=(p4, recv3), sems=sems))

    def point(self, drain=False):
        self.tick += 1
        if drain:
            old = [g for g in self.pending if g["stage"] == 2 and g["at"] + 2 <= self.tick]
            new = [g for g in self.pending if g["stage"] == 2 and g["at"] + 2 > self.tick]
            for grp in old + [g for g in self.pending if g["stage"] == 3] + new:
                self._advance([grp] if grp["stage"] == 3 else [], [grp] if grp["stage"] == 2 else [])
        else:
            self._advance([grp for grp in self.pending if grp["stage"] == 3 and grp["at"] < self.tick],
                          [grp for grp in self.pending if grp["stage"] == 2 and grp["at"] + 2 <= self.tick])

    def _advance(self, joined, landed):
        if not joined and not landed:
            return
        n_j, n_l = len(joined), len(landed)

        def wait(thru, sems, _, __):
            x, y, c, chips = _place()
            for i in range(n_j):
                buf, send, recv = thru[i], sems[2 * i], sems[2 * i + 1]
                _remote(buf.at[c], buf.at[c], send, recv, (x, y, 1 - c)).wait_send()
                _remote(buf.at[1 - c], buf.at[1 - c], send, recv, (x, y, c)).wait_recv()
            for i in range(n_l):
                p4, recv3 = thru[n_j + 2 * i], thru[n_j + 2 * i + 1]
                send, recv = sems[2 * (n_j + i)], sems[2 * (n_j + i) + 1]
                for k, chip in enumerate(chips):
                    cp = _remote(p4.at[2 * chip[0] + chip[1]], recv3.at[k], send.at[k], recv.at[k], (*chip, c))
                    cp.wait_send()
                    cp.wait_recv()

        tag = "_".join([f"{grp['name']}{grp['l']}_halves" for grp in joined] + [f"{grp['name']}{grp['l']}_chips" for grp in landed])
        bufs, _, _ = _split_call(wait, f"landed_{tag}", [b for grp in joined + landed for b in grp["bufs"]],
                                 sems_in=[sm for grp in joined + landed for sm in grp["sems"]])
        for i, grp in enumerate(joined):
            l, name, full = grp["l"], grp["name"], bufs[i]
            if GRAD_HALVES[name][0] != "cols_of_block":
                full = full.reshape((1,) + tuple(self.shard_shape[name]))
            self.adam[name] = _adamw_layer(self.w[name], full, self.m[name], self.v[name], l, self.adam.get(name),
                                           f"adamw_{name}_l{l}")
            grp.update(stage=4)
        if not landed:
            return
        halves = [_chip_sum(bufs[n_j + 2 * i], bufs[n_j + 2 * i + 1], self.j_arr, self.c_arr,
                            f"chip_sum_{grp['name']}_l{grp['l']}") for i, grp in enumerate(landed)]

        def start(thru, _, __, sems):
            x, y, c, _chips = _place()
            for i in range(n_l):
                _remote(thru[i].at[c], thru[i].at[c], sems[2 * i], sems[2 * i + 1], (x, y, 1 - c)).start()

        tag = "_".join(f"{grp['name']}{grp['l']}" for grp in landed)
        halves, _, sems = _split_call(start, f"join_start_{tag}", halves, new_sems=[()] * (2 * n_l), after_last=False)
        for i, grp in enumerate(landed):
            grp.update(stage=3, at=self.tick, bufs=(halves[i],), sems=tuple(sems[2 * i:2 * i + 2]))

    def finish(self):
        if self.held is not None:
            self._scatter_held()
        while any(grp["stage"] < 4 for grp in self.pending):
            self.point(drain=True)
        return self.adam

    @staticmethod
    def _peer(k, x, y, c):
        return (1 - x if k & 4 else x, 1 - y if k & 2 else y, 1 - c if k & 1 else c)

    def small_grads(self, l, grads, loss_tile):
        parts = [grads[nm] for nm in SMALL_NAMES] + ([loss_tile[0, 0:1]] if loss_tile is not None else [])
        packed = _pack_call(parts, f"small_pack_l{l}")
        rows = packed.shape[0]

        def start(thru, _, fresh, sems):
            x, y, c, _chips = _place()
            for k in range(1, 8):
                _remote(thru[0], fresh[0].at[4 * x + 2 * y + c], sems[0].at[k - 1], sems[1].at[k - 1],
                        self._peer(k, x, y, c)).start()

        (packed,), (landed,), sems = _split_call(start, f"small_start_l{l}", [packed], fresh=[((8, rows, PACK_LANES), F32)],
                                                 new_sems=[(7,), (7,)], after_last=False)
        self.small[l] =(packed, landed, sems, [p.shape for p in parts])

    def small_sum(self, l):
        packed, landed, sems, _shapes = self.small[l]

        def wait(thru, sems, _, __):
            x, y, c, _chips = _place()
            for k in range(1, 8):
                px, py, pc = self._peer(k, x, y, c)
                _remote(thru[0], thru[1].at[4 * x + 2 * y + c], sems[0].at[k - 1], sems[1].at[k - 1], (px, py, pc)).wait_send()
                _remote(thru[0], thru[1].at[4 * px + 2 * py + pc], sems[0].at[k - 1], sems[1].at[k - 1], (x, y, c)).wait_recv()

        (packed, landed), _, _ = _split_call(wait, f"small_done_l{l}", [packed, landed], sems_in=list(sems))
        return _sum_devices(packed, landed, self.me_arr, f"small_sum_l{l}")


def _rope_tables(s):
    inv_freq = ROPE_THETA ** (-jnp.arange(0, HEAD_DIM, 2, dtype=F32) / HEAD_DIM)
    ang = jnp.arange(s, dtype=F32)[:, None] * inv_freq[None, :]
    cos, sin = jnp.cos(ang), jnp.sin(ang)
    return jnp.concatenate([cos, cos], axis=-1), jnp.concatenate([-sin, sin], axis=-1)


def _local_step(x, target, ex, small):
    s = x.shape[0]
    cosf, sinf = _rope_tables(s)
    saved = []
    for l in range(DEPTH):
        p = small[l]
        t = f"l{l}"
        h = _rms_fwd(x, p["norm1_g"], f"norm1_{t}")
        z = _matmul(h, ex.weight(l, "w_in"), mode="nn", out_dtype=BF16, tm=1024, tn=896, tk=2048, b_parts=4, name=f"proj_in_{t}")
        qn, kn, vb, ug, vn, *gate_kept = _proj_post(z, p["q_norm_g"], p["k_norm_g"], p["sgu_ln_g"], p["sgu_ln_b"], cosf, sinf,
                                                    f"proj_post_{t}")
        attn, sgu, mixed, probs, psink = _mixer_fwd(qn, kn, vb, ug, vn, p["w_s_bf16"], p["b_s_tile"], p["sink"],
                                                    p["attn_out_g"], p["sgu_out_g"], f"mixer_{t}")
        x1 = _matmul(mixed, ex.weight(l, "w_o"), mode="nn", out_dtype=F32, tm=2048, tn=256, tk=2048, res=x,
                     name=f"proj_out_{t}")
        h2 = _rms_fwd(x1, p["norm2_g"], f"norm2_{t}")
        a_pre = _matmul(h2, ex.weight(l, "w_up"), mode="nn", out_dtype=BF16, tm=1024, tn=1408, tk=2048, b_parts=4,
                        out_parts=2,
                        name=f"ffn_up_{t}")
        act, dgu = _conv_gate_fwd(a_pre, ex.conv_w(l), p["conv_b"], f"conv_gate_{t}")
        x2 = _matmul(act, ex.weight(l, "w_down"), mode="nn", out_dtype=F32, tm=1024, tn=256, tk=D_FF, res=x1,
                     name=f"ffn_down_{t}")
        saved.append(dict(x=x, h=h, z=z, qn=qn, kn=kn, vb=vb, ug=ug, vn=vn, attn=attn, sgu=sgu, mixed=mixed, x1=x1, h2=h2,
                          a_pre=a_pre, act=act, dgu=dgu, probs=probs, psink=psink, gate_kept=gate_kept))
        x = x2
    loss_tile, dx, dxb = _loss_head(x, target, "loss_head")
    for l in reversed(range(DEPTH)):
        p, sv = small[l], saved[l]
        t = f"l{l}"
        def weight_grad(name, a, g, between, g_parts=0):
            ex.pair_send(l, name, _grad_half(name, a, g, ex.o_arr, None, f"g_{name}_other_{t}", g_parts))
            out = between()
            ex.scatter(l, name, _grad_half(name, a, g, ex.c_arr, ex.pair_recv(l, name), f"g_{name}_own_{t}", g_parts))
            ex.point()
            return out

        def after_down():
            dact = _matmul(dxb, ex.weight(l, "w_down"), mode="nt", out_dtype=BF16, tm=1024, tn=512, tk=2048,
                           name=f"d_act_{t}")
            return _conv_gate_bwd(sv["a_pre"], sv["dgu"], ex.conv_w(l), dact, f"conv_gate_bwd_{t}")

        dap, dcw, dcb = weight_grad("w_down", sv["act"], dxb, after_down)

        def after_up():
            dh2 = _matmul(dap, ex.weight(l, "w_up"), mode="nt", out_dtype=F32, tm=1024, tn=1024, tk=2816, a_parts=2,
                          b_parts=4, name=f"d_h2_{t}")
            return _rms_bwd(sv["x1"], p["norm2_g"], dh2, dx, f"norm2_bwd_{t}")

        dx1, dx1b, dg2 = weight_grad("w_up", sv["h2"], dap, after_up, g_parts=2)
        ex.pair_send(l, "w_o", _grad_half("w_o", sv["mixed"], dx1b, ex.o_arr, None, f"g_w_o_other_{t}"))
        dmixed = _matmul(dx1b, ex.weight(l, "w_o"), mode="nt", out_dtype=F32, tm=1024, tn=512, tk=2048,
                         name=f"d_mixed_{t}")
        dqn, dkn, dvb, dug, dvn, dws, dbs, dsk, dga, dgs = _mixer_bwd(
            sv["qn"], sv["kn"], sv["vb"], sv["ug"], sv["vn"], sv["attn"], sv["sgu"], dmixed, p["w_s_bf16"], p["b_s_tile"],
            p["attn_out_g"], p["sgu_out_g"], sv["probs"], sv["psink"], f"mixer_bwd_{t}")
        dz, dqg, dkg, dlg, dlb = _proj_post_bwd(sv["z"], dqn, dkn, dvb, dug, dvn, *sv["gate_kept"], p["q_norm_g"], p["k_norm_g"],
                                                 p["sgu_ln_g"], cosf, sinf, f"proj_post_bwd_{t}")
        ex.scatter(l, "w_o", _grad_half("w_o", sv["mixed"], dx1b, ex.c_arr, ex.pair_recv(l, "w_o"), f"g_w_o_own_{t}"))
        ex.point()

        def after_in():
            dh = _matmul_nt_slabs(dz, ex.weight(l, "w_in"), tm=1024, tn=512, name=f"d_h_{t}")
            return _rms_bwd(sv["x"], p["norm1_g"], dh, dx1, f"norm1_bwd_{t}")

        dx, dxb, dg1 = weight_grad("w_in", sv["h"], dz, after_in)
        ex.small_grads(l, dict(
            norm1_g=dg1[0], q_norm_g=dqg[0], k_norm_g=dkg[0], sink=dsk[:, 0], sgu_ln_g=dlg[0], sgu_ln_b=dlb[0], w_s=dws,
            b_s=dbs[:, :, 0], attn_out_g=dga[0], sgu_out_g=dgs[0], norm2_g=dg2[0],
            conv_w=jnp.concatenate([dcw[0], dcw[1]], axis=-1), conv_b=jnp.concatenate([dcb[0, 0], dcb[1, 0]], axis=-1)),
            loss_tile if l == 0 else None)
    return dx


def _small_views(l, norm1_g, q_norm_g, k_norm_g, sink, sgu_ln_g, sgu_ln_b, w_s, b_s, attn_out_g, sgu_out_g, norm2_g, conv_b):
    return dict(
        norm1_g=norm1_g[l][None], q_norm_g=q_norm_g[l][None], k_norm_g=k_norm_g[l][None], sink=sink[l],
        sgu_ln_g=sgu_ln_g[l][None], sgu_ln_b=sgu_ln_b[l][None], w_s_bf16=w_s[l].astype(BF16),
        b_s_tile=jnp.broadcast_to(b_s[l][:, :, None], (N_GMLP_HEADS, BLOCK, BLOCK)), attn_out_g=attn_out_g[l][None],
        sgu_out_g=sgu_out_g[l][None], norm2_g=norm2_g[l][None], conv_b=conv_b[l][None])


SMALL_NAMES = ("norm1_g", "q_norm_g", "k_norm_g", "sink", "sgu_ln_g", "sgu_ln_b", "w_s", "b_s", "attn_out_g", "sgu_out_g",
               "norm2_g", "conv_b", "conv_w")
REPLICATED_NAMES = SMALL_NAMES[:-1]
BIG_NAMES = ("w_in", "w_o", "w_up", "w_down")
PACK_LANES = 128
PACK_ALIGN = 8 * PACK_LANES


def _pack_rows(shape):
    return -(-math.prod(shape) // PACK_ALIGN) * 8


def _pack_parts(arrays):
    parts = []
    for a in arrays:
        flat = a.reshape(-1)
        parts.append(jnp.pad(flat, (0, _pack_rows(a.shape) * PACK_LANES - flat.shape[0])).reshape(-1, PACK_LANES))
    return parts


def _pack_call(arrays, name):
    parts = _pack_parts(arrays)
    total = sum(p.shape[0] for p in parts)

    def body(*refs):
        o_ref, at = refs[-1], 0
        for p_ref in refs[:-1]:
            o_ref[at:at + p_ref.shape[0], :] = p_ref[...]
            at += p_ref.shape[0]

    vm = pl.BlockSpec(memory_space=pltpu.VMEM)
    return _ordered_call(
        body, name=name, out_shape=jax.ShapeDtypeStruct((total, PACK_LANES), F32), in_specs=[vm] * len(parts), out_specs=vm,
        compiler_params=pltpu.CompilerParams(vmem_limit_bytes=V7X_VMEM_LIMIT),
    )(*parts)


def _unpack_layers(stacked, shapes):
    nl = stacked.shape[0]
    out, at = [], 0
    for shp in shapes:
        rows = _pack_rows(shp)
        out.append(stacked[:, at:at + rows].reshape(nl, -1)[:, :math.prod(shp)].reshape((nl,) + tuple(shp)))
        at += rows
    return out


def _adamw_packed(w, g, m, v, rows, layer, into, name):
    head = pl.BlockSpec((rows, PACK_LANES), lambda i: (0, 0))
    at_layer = pl.BlockSpec((None, rows, PACK_LANES), lambda i: (layer, 0, 0))

    def body(w_ref, g_ref, m_ref, v_ref, *rest):
        d_ref, nm_ref, nv_ref = rest[-3:]
        gv = g_ref[...]
        mn = ADAM_B1 * m_ref[...] + (1.0 - ADAM_B1) * gv
        vn = ADAM_B2 * v_ref[...] + (1.0 - ADAM_B2) * (gv * gv)
        m_hat = mn / (1.0 - ADAM_B1 ** ADAM_STEP)
        v_hat = vn / (1.0 - ADAM_B2 ** ADAM_STEP)
        d_ref[...] = -ADAM_LR * (m_hat / (jnp.sqrt(v_hat) + ADAM_EPS) + ADAM_WD * w_ref[...])
        nm_ref[...] = mn
        nv_ref[...] = vn

    in_specs = [head] * 4
    operands = [w, g, m, v]
    aliases = {}
    if into is not None:
        in_specs += [ANY] * 3
        operands += list(into)
        aliases = {4 + i: i for i in range(3)}
    sds = jax.ShapeDtypeStruct((DEPTH, rows, PACK_LANES), F32)
    return _ordered_call(
        body, name=name, out_shape=(sds,) * 3, grid=(1,), in_specs=in_specs, out_specs=(at_layer,) * 3,
        input_output_aliases=aliases, compiler_params=_params(("arbitrary",)),
    )(*operands)


def kernel(x, norm1_g, w_in, q_norm_g, k_norm_g, sink, sgu_ln_g, sgu_ln_b, w_s, b_s, attn_out_g, sgu_out_g, w_o, norm2_g, w_up, conv_w, conv_b, w_down, loss_target, m_norm1_g, m_w_in, m_q_norm_g, m_k_norm_g, m_sink, m_sgu_ln_g, m_sgu_ln_b, m_w_s, m_b_s, m_attn_out_g, m_sgu_out_g, m_w_o, m_norm2_g, m_w_up, m_conv_w, m_conv_b, m_w_down, v_norm1_g, v_w_in, v_q_norm_g, v_k_norm_g, v_sink, v_sgu_ln_g, v_sgu_ln_b, v_w_s, v_b_s, v_attn_out_g, v_sgu_out_g, v_w_o, v_norm2_g, v_w_up, v_conv_w, v_conv_b, v_w_down):
    weights = dict(norm1_g=norm1_g, w_in=w_in, q_norm_g=q_norm_g, k_norm_g=k_norm_g, sink=sink, sgu_ln_g=sgu_ln_g,
                   sgu_ln_b=sgu_ln_b, w_s=w_s, b_s=b_s, attn_out_g=attn_out_g, sgu_out_g=sgu_out_g, w_o=w_o, norm2_g=norm2_g,
                   w_up=w_up, conv_w=conv_w, conv_b=conv_b, w_down=w_down)
    m_in = dict(norm1_g=m_norm1_g, w_in=m_w_in, q_norm_g=m_q_norm_g, k_norm_g=m_k_norm_g, sink=m_sink, sgu_ln_g=m_sgu_ln_g,
                sgu_ln_b=m_sgu_ln_b, w_s=m_w_s, b_s=m_b_s, attn_out_g=m_attn_out_g, sgu_out_g=m_sgu_out_g, w_o=m_w_o,
                norm2_g=m_norm2_g, w_up=m_w_up, conv_w=m_conv_w, conv_b=m_conv_b, w_down=m_w_down)
    v_in = dict(norm1_g=v_norm1_g, w_in=v_w_in, q_norm_g=v_q_norm_g, k_norm_g=v_k_norm_g, sink=v_sink, sgu_ln_g=v_sgu_ln_g,
                sgu_ln_b=v_sgu_ln_b, w_s=v_w_s, b_s=v_b_s, attn_out_g=v_attn_out_g, sgu_out_g=v_sgu_out_g, w_o=v_w_o,
                norm2_g=v_norm2_g, w_up=v_w_up, conv_w=v_conv_w, conv_b=v_conv_b, w_down=v_w_down)
    cx, cy, cc = lax.axis_index("x"), lax.axis_index("y"), lax.axis_index("c")
    j_me = 2 * cx + cy
    c_arr = jnp.reshape(cc, (1,)).astype(jnp.int32)
    j_arr = jnp.reshape(j_me, (1,)).astype(jnp.int32)

    _Order.last = None
    ex = _Exchange(weights, m_in, v_in, j_arr, c_arr, jnp.reshape(4 * cx + 2 * cy + cc, (1,)).astype(jnp.int32))
    small = [_small_views(l, norm1_g, q_norm_g, k_norm_g, sink, sgu_ln_g, sgu_ln_b, w_s, b_s, attn_out_g, sgu_out_g, norm2_g,
                          conv_b) for l in range(DEPTH)]
    held_back, _ = lax.optimization_barrier(([[src[nm] for nm in REPLICATED_NAMES] for src in (weights, m_in, v_in)], _Order.last))
    packed_in = [[_pack_call([arr[l] for arr in arrays], f"pack_{tag}_l{l}") for tag, arrays in zip("wmv", held_back)]
                 for l in range(DEPTH)]
    dx = _local_step(x[0], loss_target[0], ex, small)
    big_out = ex.finish()

    rep_shapes = [weights[nm].shape[1:] for nm in REPLICATED_NAMES]
    rep_rows = sum(_pack_rows(shp) for shp in rep_shapes)
    cw_shape = (3, 2 * D_FF)
    sums, adam_small = [None] * DEPTH, None
    for l in reversed(range(DEPTH)):
        sums[l] = ex.small_sum(l)
        pw, pm, pv = packed_in[l]
        adam_small = _adamw_packed(pw, sums[l], pm, pv, rep_rows, l, adam_small, f"adamw_small_l{l}")
    cw_rows = _pack_rows(cw_shape)
    loss = sums[0][rep_rows + cw_rows, 0]
    stacked = jnp.stack([sm[:rep_rows + cw_rows] for sm in sums])
    grads = dict(zip(REPLICATED_NAMES, _unpack_layers(stacked[:, :rep_rows], rep_shapes)))
    delta, new_m, new_v = (dict(zip(REPLICATED_NAMES, _unpack_layers(arr, rep_shapes))) for arr in adam_small)
    cw_cols = 2 * D_FF // N_CHIPS
    cw_grad = lax.dynamic_slice_in_dim(_unpack_layers(stacked[:, rep_rows:], [cw_shape])[0], j_me * cw_cols, cw_cols, axis=2)
    flat = lambda a: a.reshape(DEPTH * 3, cw_cols)
    cw_out = _adamw(flat(conv_w), flat(cw_grad), flat(m_conv_w), flat(v_conv_w), "adamw_conv_w")
    grads["conv_w"], delta["conv_w"], new_m["conv_w"], new_v["conv_w"] = (a.reshape(DEPTH, 3, cw_cols) for a in cw_out)

    for name in BIG_NAMES:
        grads[name], delta[name], new_m[name], new_v[name] = big_out[name]

    order = ("norm1_g", "w_in", "q_norm_g", "k_norm_g", "sink", "sgu_ln_g", "sgu_ln_b", "w_s", "b_s", "attn_out_g", "sgu_out_g",
             "w_o", "norm2_g", "w_up", "conv_w", "conv_b", "w_down")
    return (loss, dx[None], *[grads[nm] for nm in order], *[delta[nm] for nm in order], *[new_m[nm] for nm in order],
            *[new_v[nm] for nm in order])
```

```python
import math

import jax
import jax.numpy as jnp
from jax import lax
from jax.experimental import pallas as pl
from jax.experimental.pallas import tpu as pltpu

F32 = jnp.float32
BF16 = jnp.bfloat16

D_MODEL = 2048
HEAD_DIM = 128
ATTN_WIDTH = 1024
N_Q_HEADS = 8
N_KV_HEADS = 2
GQA_GROUP = 4
KV_WIDTH = 256
GMLP_WIDTH = 1024
N_GMLP_HEADS = 8
BLOCK = 128
IN_WIDTH = 3584
D_FF = 5632
DEPTH = 2
EPS = 1e-6
MASK_VALUE = -1e30
ROPE_THETA = 10000.0
N_CHIPS = 4

ADAM_LR = 0.001
ADAM_B1 = 0.9
ADAM_B2 = 0.999
ADAM_EPS = 1e-08
ADAM_WD = 0.01
ADAM_STEP = 10

V7X_VMEM_LIMIT = 48 * 1024 * 1024
MESH = pl.DeviceIdType.MESH

_GELU_C = math.sqrt(2.0 / math.pi)
_GELU_A = 0.044715


def _params(sem=None):
    return pltpu.CompilerParams(dimension_semantics=sem, vmem_limit_bytes=V7X_VMEM_LIMIT)


ANY = pl.BlockSpec(memory_space=pl.ANY)


class _Order:
    last = None


def _ordered_call(body, *, token_index=0, **kw):
    def run(*operands):
        tok = _Order.last
        if tok is None or any(op is tok for op in operands):
            call = pl.pallas_call(body, **kw)
        else:
            n_in = len(operands)

            def ordered_body(*refs):
                return body(*refs[:n_in], *refs[n_in + 1:])

            kw2 = dict(kw)
            if "grid_spec" in kw2:
                gs = kw2["grid_spec"]
                kw2["grid_spec"] = pltpu.PrefetchScalarGridSpec(
                    num_scalar_prefetch=gs.num_scalar_prefetch, grid=gs.grid, in_specs=list(gs.in_specs) + [ANY],
                    out_specs=gs.out_specs, scratch_shapes=gs.scratch_shapes)
            else:
                kw2["in_specs"] = list(kw2["in_specs"]) + [ANY]
            call = pl.pallas_call(ordered_body, **kw2)
            operands = operands + (tok,)
        out = call(*operands)
        _Order.last = out[token_index] if isinstance(out, (tuple, list)) else out
        return out

    return run


def _gelu(x):
    return x * (0.5 * (1.0 + jnp.tanh(_GELU_C * (x + _GELU_A * (x * x * x)))))


def _gelu_grad(x):
    x2 = x * x
    t = jnp.tanh(_GELU_C * (x + _GELU_A * (x * x2)))
    return 0.5 * (1.0 + t) + 0.5 * x * (1.0 - t * t) * (_GELU_C * (1.0 + 3.0 * _GELU_A * x2))


def _mean_last(x):
    return jnp.mean(x, axis=-1, keepdims=True)


def _sum_rows(x):
    return jnp.sum(x, axis=0, keepdims=True)


def _sum_all(x):
    return jnp.sum(jnp.sum(x, axis=1, keepdims=True), axis=0, keepdims=True)


def _matmul(a, b, *, mode, out_dtype, tm, tn, tk, name, res=None, a_parts=0, b_parts=0, out_parts=0):
    assert mode in ("nn", "nt"), mode
    if mode == "nn":
        assert not a_parts
        m, k = a.shape
        n = b.shape[0] * b.shape[2] if b_parts else b.shape[1]
    else:
        m, k = (a.shape[1], a.shape[0] * a.shape[2]) if a_parts else a.shape
        n = b.shape[1] if b_parts else b.shape[0]
    tm, tn, tk = min(tm, m), min(tn, n), min(tk, k)
    assert m % tm == 0 and n % tn == 0 and k % tk == 0, (name, m, n, k, tm, tn, tk)
    nm, nn, nk = m // tm, n // tn, k // tk

    def slab(idx, total_tiles, parts):
        per = total_tiles // parts
        assert per * parts == total_tiles, (name, total_tiles, parts)
        return idx // per, idx % per

    if mode == "nn":
        a_spec = pl.BlockSpec((tm, tk), lambda i, j, kk: (i, kk))
        if b_parts:
            b_spec = pl.BlockSpec((None, tk, tn), lambda i, j, kk: (slab(j, nn, b_parts)[0], kk, slab(j, nn, b_parts)[1]))
        else:
            b_spec = pl.BlockSpec((tk, tn), lambda i, j, kk: (kk, j))
        dims = (((1,), (0,)), ((), ()))
    else:
        if a_parts:
            a_spec = pl.BlockSpec((None, tm, tk), lambda i, j, kk: (slab(kk, nk, a_parts)[0], i, slab(kk, nk, a_parts)[1]))
        else:
            a_spec = pl.BlockSpec((tm, tk), lambda i, j, kk: (i, kk))
        if b_parts:
            b_spec = pl.BlockSpec((None, tn, tk), lambda i, j, kk: (slab(kk, nk, b_parts)[0], j, slab(kk, nk, b_parts)[1]))
        else:
            b_spec = pl.BlockSpec((tn, tk), lambda i, j, kk: (j, kk))
        dims = (((1,), (1,)), ((), ()))
    if out_parts:
        out_shape = jax.ShapeDtypeStruct((out_parts, m, n // out_parts), out_dtype)
        out_spec = pl.BlockSpec((None, tm, tn), lambda i, j, kk: (slab(j, nn, out_parts)[0], i, slab(j, nn, out_parts)[1]))
    else:
        out_shape = jax.ShapeDtypeStruct((m, n), out_dtype)
        out_spec = pl.BlockSpec((tm, tn), lambda i, j, kk: (i, j))
    in_specs = [a_spec, b_spec]
    operands = [a, b]
    if res is not None:
        in_specs.append(pl.BlockSpec((tm, tn), lambda i, j, kk: (i, j)))
        operands.append(res)

    def body(*refs):
        a_ref, b_ref = refs[0], refs[1]
        res_ref = refs[2] if res is not None else None
        o_ref = refs[3] if res is not None else refs[2]
        p = lax.dot_general(a_ref[...], b_ref[...], dims, preferred_element_type=F32)

        def finish(total):
            if res_ref is not None:
                total = res_ref[...] + total
            o_ref[...] = total.astype(out_dtype)

        if nk == 1:
            finish(p)
        else:
            acc_ref = refs[-1]
            kk = pl.program_id(2)

            @pl.when(kk == 0)
            def _():
                acc_ref[...] = p

            @pl.when(jnp.logical_and(kk > 0, kk < nk - 1))
            def _():
                acc_ref[...] += p

            @pl.when(kk == nk - 1)
            def _():
                finish(acc_ref[...] + p)

    scratch = [pltpu.VMEM((tm, tn), F32)] if nk > 1 else []
    return _ordered_call(
        body, name=name, out_shape=out_shape, grid=(nm, nn, nk), in_specs=in_specs, out_specs=out_spec,
        scratch_shapes=scratch, compiler_params=_params(("parallel", "parallel", "arbitrary")),
    )(*operands)


def _matmul_nt_slabs(a, b, *, tm, tn, name, a_parts=0):
    nslab, n, ks = b.shape
    m = a.shape[1] if a_parts else a.shape[0]
    tm, tn = min(tm, m), min(tn, n)
    assert m % tm == 0 and n % tn == 0, (name, m, n, tm, tn)
    if a_parts:
        per = nslab // a_parts
        assert per * a_parts == nslab and a.shape[2] == per * ks, (name, a.shape, b.shape)
        a_spec = pl.BlockSpec((a_parts, tm, per * ks), lambda i, j: (0, i, 0))
    else:
        assert a.shape[1] == nslab * ks, (name, a.shape, b.shape)
        a_spec = pl.BlockSpec((tm, nslab * ks), lambda i, j: (i, 0))

    def body(a_ref, b_ref, o_ref):
        total = None
        for sl in range(nslab):
            if a_parts:
                a_sl = a_ref[sl // per, :, (sl % per) * ks:(sl % per + 1) * ks]
            else:
                a_sl = a_ref[:, sl * ks:(sl + 1) * ks]
            p = lax.dot_general(a_sl, b_ref[sl], (((1,), (1,)), ((), ())), preferred_element_type=F32)
            total = p if total is None else total + p
        o_ref[...] = total.astype(BF16)

    return _ordered_call(
        body, name=name, out_shape=jax.ShapeDtypeStruct((m, n), BF16), grid=(m // tm, n // tn),
        in_specs=[a_spec, pl.BlockSpec((nslab, tn, ks), lambda i, j: (0, j, 0))],
        out_specs=pl.BlockSpec((tm, tn), lambda i, j: (i, j)), compiler_params=_params(("parallel", "parallel")),
    )(a, b)


GRAD_HALVES = {
    "w_in": ("rows_of_slab", 1024, 896), "w_up": ("rows_of_slab", 1024, 1408), "w_o": ("rows_of_block", 256, 2048),
    "w_down": ("cols_of_block", 1408, 512)}


def _half_shape(name, shard_shape):
    r, cols = shard_shape
    return (r, cols // 2) if GRAD_HALVES[name][0] == "cols_of_block" else (r // 2, cols)


def _grad_half(name, a, g, sel, res, call_name, g_parts=0):
    kind, tm, tn = GRAD_HALVES[name]
    s, m = a.shape
    n = g.shape[0] * g.shape[2] if g_parts else g.shape[1]
    if kind == "rows_of_slab":
        rh, hc = m // 2, n // N_CHIPS
        per = hc // tn
        grid = (rh // tm, n // tn)
        a_map = lambda i, j, sel_ref: (0, sel_ref[0] * (rh // tm) + i)
        g_col = lambda i, j, sel_ref: j
        o_map = lambda i, j, sel_ref: (j // per, i, j % per)
    elif kind == "rows_of_block":
        rh, hc = m // N_CHIPS // 2, n
        assert tm == rh
        grid = (N_CHIPS, n // tn)
        a_map = lambda i, j, sel_ref: (0, 2 * i + sel_ref[0])
        g_col = lambda i, j, sel_ref: j
        o_map = lambda i, j, sel_ref: (i, 0, j)
    else:
        rh, hc = m // N_CHIPS, n // 2
        assert tm == rh
        grid = (N_CHIPS, hc // tn)
        a_map = lambda i, j, sel_ref: (0, i)
        g_col = lambda i, j, sel_ref: sel_ref[0] * (hc // tn) + j
        o_map = lambda i, j, sel_ref: (i, 0, j)
    if g_parts:
        g_per = (n // tn) // g_parts
        g_spec = pl.BlockSpec((None, s, tn), lambda i, j, sel_ref: (g_col(i, j, sel_ref) // g_per, 0, g_col(i, j, sel_ref) % g_per))
    else:
        g_spec = pl.BlockSpec((s, tn), lambda i, j, sel_ref: (0, g_col(i, j, sel_ref)))
    o_spec = pl.BlockSpec((None, tm, tn), o_map)
    in_specs = [pl.BlockSpec((s, tm), a_map), g_spec] + ([o_spec] if res is not None else [])

    def body(sel_ref, a_ref, g_ref, *rest):
        o_ref = rest[-1]
        p = lax.dot_general(a_ref[...], g_ref[...], (((0,), (0,)), ((), ())), preferred_element_type=F32)
        if res is not None:
            p = p + rest[0][...].astype(F32)
        o_ref[...] = p.astype(BF16)

    grid_spec = pltpu.PrefetchScalarGridSpec(num_scalar_prefetch=1, grid=grid, in_specs=in_specs, out_specs=o_spec)
    return _ordered_call(
        body, name=call_name, out_shape=jax.ShapeDtypeStruct((N_CHIPS, rh, hc), BF16), grid_spec=grid_spec,
        compiler_params=_params(("parallel", "parallel")),
    )(sel, a, g, *([res] if res is not None else []))


def _row_tile(s):
    return min(256, s)


def _rows(width, tr):
    return pl.BlockSpec((tr, width), lambda i: (i, 0))


def _const2(shape):
    return pl.BlockSpec(shape, lambda i: (0, 0))


def _rms_fwd(x, g, name):
    s, d = x.shape
    tr = _row_tile(s)

    def body(x_ref, g_ref, o_ref):
        xv = x_ref[...]
        r = lax.rsqrt(_mean_last(xv * xv) + EPS)
        o_ref[...] = (xv * r * g_ref[...]).astype(BF16)

    return _ordered_call(
        body, name=name, out_shape=jax.ShapeDtypeStruct((s, d), BF16), grid=(s // tr,),
        in_specs=[_rows(d, tr), _const2((1, d))], out_specs=_rows(d, tr), compiler_params=_params(("parallel",)),
    )(x, g)


def _rms_bwd(x, g, dh, dres, name):
    s, d = x.shape
    tr = _row_tile(s)

    def body(x_ref, g_ref, dh_ref, dres_ref, dx_ref, dxb_ref, dg_ref):
        xv, dy = x_ref[...], dh_ref[...].astype(F32)
        r = lax.rsqrt(_mean_last(xv * xv) + EPS)
        gdy = dy * g_ref[...]
        dx = dres_ref[...] + r * gdy - xv * ((r * r * r) * _mean_last(xv * gdy))
        dx_ref[...] = dx
        dxb_ref[...] = dx.astype(BF16)

        @pl.when(pl.program_id(0) == 0)
        def _():
            dg_ref[...] = jnp.zeros_like(dg_ref)

        dg_ref[...] += _sum_rows(xv * r * dy)

    return _ordered_call(
        body, name=name,
        out_shape=(jax.ShapeDtypeStruct((s, d), F32), jax.ShapeDtypeStruct((s, d), BF16), jax.ShapeDtypeStruct((1, d), F32)),
        grid=(s // tr,), in_specs=[_rows(d, tr), _const2((1, d)), _rows(d, tr), _rows(d, tr)],
        out_specs=(_rows(d, tr), _rows(d, tr), _const2((1, d))), compiler_params=_params(("arbitrary",)),
    )(x, g, dh, dres)


Q0, K0, V0, GU0, GV0 = 0, ATTN_WIDTH, ATTN_WIDTH + KV_WIDTH, ATTN_WIDTH + 2 * KV_WIDTH, ATTN_WIDTH + 2 * KV_WIDTH + GMLP_WIDTH


def _head(h, base=0):
    return slice(base + h * HEAD_DIM, base + (h + 1) * HEAD_DIM)


def _proj_post(z, qg, kg, lg, lb, cosf, sinf, name):
    s = z.shape[0]
    tr = _row_tile(s)

    def body(z_ref, qg_ref, kg_ref, lg_ref, lb_ref, cos_ref, sin_ref, qn_ref, kn_ref, vb_ref, ug_ref, vn_ref,
             dgu_ref, dgv_ref, xhat_ref, rstd_ref):
        cos, sin = cos_ref[...], sin_ref[...]

        def norm_rope(xh, g):
            y = xh * lax.rsqrt(_mean_last(xh * xh) + EPS) * g
            return y * cos + pltpu.roll(y, HEAD_DIM // 2, 1) * sin

        for h in range(N_Q_HEADS):
            qn_ref[:, _head(h)] = norm_rope(z_ref[:, _head(h, Q0)].astype(F32), qg_ref[...]).astype(BF16)
        for h in range(N_KV_HEADS):
            kn_ref[:, _head(h)] = norm_rope(z_ref[:, _head(h, K0)].astype(F32), kg_ref[...]).astype(BF16)
        vb_ref[...] = z_ref[:, V0:GU0]
        gu = z_ref[:, GU0:GV0].astype(F32)
        ug_ref[...] = _gelu(gu)
        dgu_ref[...] = _gelu_grad(gu).astype(BF16)
        gv = z_ref[:, GV0:IN_WIDTH].astype(F32)
        vg = _gelu(gv)
        dgv_ref[...] = _gelu_grad(gv).astype(BF16)
        xc = vg - _mean_last(vg)
        r = lax.rsqrt(_mean_last(xc * xc) + EPS)
        y = xc * r
        xhat_ref[...] = y.astype(BF16)
        rstd_ref[...] = r
        vn_ref[...] = (y * lg_ref[...] + lb_ref[...]).astype(BF16)

    wide = jax.ShapeDtypeStruct((s, GMLP_WIDTH), BF16)
    return _ordered_call(
        body, name=name,
        out_shape=(jax.ShapeDtypeStruct((s, ATTN_WIDTH), BF16), jax.ShapeDtypeStruct((s, KV_WIDTH), BF16),
                   jax.ShapeDtypeStruct((s, KV_WIDTH), BF16), jax.ShapeDtypeStruct((s, GMLP_WIDTH), F32), wide,
                   wide, wide, wide, jax.ShapeDtypeStruct((s, 1), F32)),
        grid=(s // tr,),
        in_specs=[_rows(IN_WIDTH, tr), _const2((1, HEAD_DIM)), _const2((1, HEAD_DIM)), _const2((1, GMLP_WIDTH)),
                  _const2((1, GMLP_WIDTH)), _rows(HEAD_DIM, tr), _rows(HEAD_DIM, tr)],
        out_specs=(_rows(ATTN_WIDTH, tr), _rows(KV_WIDTH, tr), _rows(KV_WIDTH, tr), _rows(GMLP_WIDTH, tr), _rows(GMLP_WIDTH, tr),
                   _rows(GMLP_WIDTH, tr), _rows(GMLP_WIDTH, tr), _rows(GMLP_WIDTH, tr), _rows(1, tr)),
        compiler_params=_params(("parallel",)),
    )(z, qg, kg, lg, lb, cosf, sinf)


def _proj_post_bwd(z, dqn, dkn, dvb, dug, dvn, gelu_grad_u, gelu_grad_v, xhat_v, rstd_v, qg, kg, lg, cosf, sinf, name):
    s = z.shape[0]
    tr = _row_tile(s)

    def body(z_ref, dqn_ref, dkn_ref, dvb_ref, dug_ref, dvn_ref, ggu_ref, ggv_ref, xhat_ref, rstd_ref, qg_ref, kg_ref, lg_ref,
             cos_ref, sin_ref, dz_ref, dqg_ref, dkg_ref, dlg_ref, dlb_ref):
        cos, sin = cos_ref[...], sin_ref[...]

        @pl.when(pl.program_id(0) == 0)
        def _():
            dqg_ref[...] = jnp.zeros_like(dqg_ref)
            dkg_ref[...] = jnp.zeros_like(dkg_ref)
            dlg_ref[...] = jnp.zeros_like(dlg_ref)
            dlb_ref[...] = jnp.zeros_like(dlb_ref)

        def norm_rope_bwd(xh, g, dout):
            dy = dout * cos - pltpu.roll(dout, HEAD_DIM // 2, 1) * sin
            r = lax.rsqrt(_mean_last(xh * xh) + EPS)
            xhat = xh * r
            gdy = dy * g
            return r * (gdy - xhat * _mean_last(xhat * gdy)), _sum_rows(xhat * dy)

        dqg = jnp.zeros((1, HEAD_DIM), F32)
        for h in range(N_Q_HEADS):
            dx, dg = norm_rope_bwd(z_ref[:, _head(h, Q0)].astype(F32), qg_ref[...], dqn_ref[:, _head(h)])
            dz_ref[:, _head(h, Q0)] = dx.astype(BF16)
            dqg = dqg + dg
        dqg_ref[...] += dqg
        dkg = jnp.zeros((1, HEAD_DIM), F32)
        for h in range(N_KV_HEADS):
            dx, dg = norm_rope_bwd(z_ref[:, _head(h, K0)].astype(F32), kg_ref[...], dkn_ref[:, _head(h)])
            dz_ref[:, _head(h, K0)] = dx.astype(BF16)
            dkg = dkg + dg
        dkg_ref[...] += dkg
        dz_ref[:, V0:GU0] = dvb_ref[...].astype(BF16)
        dz_ref[:, GU0:GV0] = (dug_ref[...] * ggu_ref[...].astype(F32)).astype(BF16)
        xhat = xhat_ref[...].astype(F32)
        dvn_v = dvn_ref[...]
        dlg_ref[...] += _sum_rows(xhat * dvn_v)
        dlb_ref[...] += _sum_rows(dvn_v)
        dxh = dvn_v * lg_ref[...]
        dvg = rstd_ref[...] * (dxh - _mean_last(dxh) - xhat * _mean_last(dxh * xhat))
        dz_ref[:, GV0:IN_WIDTH] = (dvg * ggv_ref[...].astype(F32)).astype(BF16)

    return _ordered_call(
        body, name=name,
        out_shape=(jax.ShapeDtypeStruct((s, IN_WIDTH), BF16), jax.ShapeDtypeStruct((1, HEAD_DIM), F32),
                   jax.ShapeDtypeStruct((1, HEAD_DIM), F32), jax.ShapeDtypeStruct((1, GMLP_WIDTH), F32),
                   jax.ShapeDtypeStruct((1, GMLP_WIDTH), F32)),
        grid=(s // tr,),
        in_specs=[_rows(V0, tr), _rows(ATTN_WIDTH, tr), _rows(KV_WIDTH, tr), _rows(KV_WIDTH, tr), _rows(GMLP_WIDTH, tr),
                  _rows(GMLP_WIDTH, tr), _rows(GMLP_WIDTH, tr), _rows(GMLP_WIDTH, tr), _rows(GMLP_WIDTH, tr), _rows(1, tr),
                  _const2((1, HEAD_DIM)), _const2((1, HEAD_DIM)), _const2((1, GMLP_WIDTH)), _rows(HEAD_DIM, tr),
                  _rows(HEAD_DIM, tr)],
        out_specs=(_rows(IN_WIDTH, tr), _const2((1, HEAD_DIM)), _const2((1, HEAD_DIM)), _const2((1, GMLP_WIDTH)),
                   _const2((1, GMLP_WIDTH))),
        compiler_params=_params(("arbitrary",)),
    )(z, dqn, dkn, dvb, dug, dvn, gelu_grad_u, gelu_grad_v, xhat_v, rstd_v, qg, kg, lg, cosf, sinf)


def _band_valid(n, s):
    shape = (GQA_GROUP * BLOCK, 3 * BLOCK)
    i = lax.broadcasted_iota(jnp.int32, shape, 0) & (BLOCK - 1)
    j = lax.broadcasted_iota(jnp.int32, shape, 1)
    k_pos = n * BLOCK - BLOCK + j
    return (jnp.abs(j - BLOCK - i) <= BLOCK) & (k_pos >= 0) & (k_pos < s)


def _group_rows(x, kh):
    return jnp.concatenate([x[:, _head(kh * GQA_GROUP + g)] for g in range(GQA_GROUP)], axis=0)


def _group_sinks(sink_ref, kh):
    return jnp.concatenate([jnp.full((BLOCK, 1), sink_ref[kh * GQA_GROUP + g], F32) for g in range(GQA_GROUP)], axis=0)


def _rows_of(x, g):
    return x[g * BLOCK:(g + 1) * BLOCK]


def _probs(q, kb, sink_h, valid):
    sc = lax.dot_general(q, kb, (((1,), (1,)), ((), ())), preferred_element_type=F32) * (HEAD_DIM ** -0.5)
    sc = jnp.where(valid, sc, MASK_VALUE)
    m = jnp.maximum(jnp.max(sc, axis=-1, keepdims=True), sink_h)
    p = jnp.exp(sc - m)
    es = jnp.exp(sink_h - m)
    den = jnp.sum(p, axis=-1, keepdims=True) + es
    inv = 1.0 / den
    return p * inv, es * inv


def _band_specs(width, nb):
    return [pl.BlockSpec((BLOCK, width), lambda n: (jnp.maximum(n - 1, 0), 0)),
            pl.BlockSpec((BLOCK, width), lambda n: (n, 0)),
            pl.BlockSpec((BLOCK, width), lambda n: (jnp.minimum(n + 1, nb - 1), 0))]


def _blk(width):
    return pl.BlockSpec((BLOCK, width), lambda n: (n, 0))


def _whole3(shape):
    return pl.BlockSpec(shape, lambda n: (0, 0, 0))


def _smem():
    return pl.BlockSpec(memory_space=pltpu.SMEM)


def _mixer_fwd(qn, kn, vb, ug, vn, wsb, bsb, sink, ga, gs, name):
    s = qn.shape[0]
    nb = s // BLOCK

    def body(sink_ref, q_ref, kp_ref, kc_ref, kx_ref, vp_ref, vc_ref, vx_ref, ug_ref, vn_ref, ws_ref, bs_ref, ga_ref, gs_ref,
             attn_ref, sgu_ref, mix_ref, probs_ref, psink_ref):
        n = pl.program_id(0)
        valid = _band_valid(n, s)
        ssq = jnp.zeros((BLOCK, 1), F32)
        for kh in range(N_KV_HEADS):
            kb = jnp.concatenate([kp_ref[:, _head(kh)], kc_ref[:, _head(kh)], kx_ref[:, _head(kh)]], axis=0)
            vbd = jnp.concatenate([vp_ref[:, _head(kh)], vc_ref[:, _head(kh)], vx_ref[:, _head(kh)]], axis=0)
            p, p_sink = _probs(_group_rows(q_ref, kh), kb, _group_sinks(sink_ref, kh), valid)
            pb = p.astype(BF16)
            probs_ref[kh] = pb
            psink_ref[kh] = p_sink
            o4 = jnp.dot(pb, vbd, preferred_element_type=F32)
            for g in range(GQA_GROUP):
                o = _rows_of(o4, g)
                attn_ref[:, _head(kh * GQA_GROUP + g)] = o
                ssq = ssq + jnp.sum(o * o, axis=-1, keepdims=True)
        r = lax.rsqrt(ssq * (1.0 / ATTN_WIDTH) + EPS)
        mix_ref[:, 0:ATTN_WIDTH] = (attn_ref[...] * r * ga_ref[...]).astype(BF16)
        ssq = jnp.zeros((BLOCK, 1), F32)
        for h in range(N_GMLP_HEADS):
            f = jnp.dot(ws_ref[h], vn_ref[:, _head(h)], preferred_element_type=F32) + bs_ref[h]
            o = ug_ref[:, _head(h)] * f
            sgu_ref[:, _head(h)] = o
            ssq = ssq + jnp.sum(o * o, axis=-1, keepdims=True)
        r = lax.rsqrt(ssq * (1.0 / GMLP_WIDTH) + EPS)
        mix_ref[:, ATTN_WIDTH:D_MODEL] = (sgu_ref[...] * r * gs_ref[...]).astype(BF16)

    hh = (N_GMLP_HEADS, BLOCK, BLOCK)
    return _ordered_call(
        body, name=name,
        out_shape=(jax.ShapeDtypeStruct((s, ATTN_WIDTH), F32), jax.ShapeDtypeStruct((s, GMLP_WIDTH), F32),
                   jax.ShapeDtypeStruct((s, D_MODEL), BF16), jax.ShapeDtypeStruct((nb,) + PROBS_BLOCK, BF16),
                   jax.ShapeDtypeStruct((nb,) + PSINK_BLOCK, F32)),
        grid=(nb,),
        in_specs=[_smem(), _blk(ATTN_WIDTH)] + _band_specs(KV_WIDTH, nb) + _band_specs(KV_WIDTH, nb)
        + [_blk(GMLP_WIDTH), _blk(GMLP_WIDTH), _whole3(hh), _whole3(hh),
           pl.BlockSpec((1, ATTN_WIDTH), lambda n: (0, 0)), pl.BlockSpec((1, GMLP_WIDTH), lambda n: (0, 0))],
        out_specs=(_blk(ATTN_WIDTH), _blk(GMLP_WIDTH), _blk(D_MODEL), _per_block(PROBS_BLOCK), _per_block(PSINK_BLOCK)),
        compiler_params=_params(("parallel",)),
    )(sink, qn, kn, kn, kn, vb, vb, vb, ug, vn, wsb, bsb, ga, gs)


PROBS_BLOCK = (N_KV_HEADS, GQA_GROUP * BLOCK, 3 * BLOCK)
PSINK_BLOCK = (N_KV_HEADS, GQA_GROUP * BLOCK, 1)


def _per_block(shape):
    return pl.BlockSpec((None,) + shape, lambda n: (n, 0, 0, 0))


def _mixer_bwd(qn, kn, vb, ug, vn, attn, sgu, dmixed, wsb, bsb, ga, gs, probs, psink, name):
    s = qn.shape[0]
    nb = s // BLOCK
    tn_dims = (((0,), (0,)), ((), ()))
    nt_dims = (((1,), (1,)), ((), ()))

    def body(q_ref, kp_ref, kc_ref, kx_ref, vp_ref, vc_ref, vx_ref, ug_ref, vn_ref, attn_ref, sgu_ref, dm_ref,
             ws_ref, bs_ref, ga_ref, gs_ref, probs_ref, psink_ref,
             dq_ref, dk_ref, dv_ref, dug_ref, dvn_ref, dws_ref, dbs_ref, dsk_ref, dga_ref, dgs_ref, dk_acc, dv_acc):
        n = pl.program_id(0)

        @pl.when(n == 0)
        def _():
            for ref in (dk_acc, dv_acc, dws_ref, dbs_ref, dsk_ref, dga_ref, dgs_ref):
                ref[...] = jnp.zeros_like(ref)

        def out_norm_bwd(o, g, dy):
            r = lax.rsqrt(_mean_last(o * o) + EPS)
            gdy = dy * g
            return r * gdy - o * ((r * r * r) * _mean_last(o * gdy)), _sum_rows(o * r * dy)

        d_attn, dga = out_norm_bwd(attn_ref[...], ga_ref[...], dm_ref[:, 0:ATTN_WIDTH].astype(F32))
        dga_ref[...] += dga
        d_sgu, dgs = out_norm_bwd(sgu_ref[...], gs_ref[...], dm_ref[:, ATTN_WIDTH:D_MODEL].astype(F32))
        dgs_ref[...] += dgs

        for h in range(N_GMLP_HEADS):
            vn_h = vn_ref[:, _head(h)]
            f = jnp.dot(ws_ref[h], vn_h, preferred_element_type=F32) + bs_ref[h]
            ds_h = d_sgu[:, _head(h)]
            dug_ref[:, _head(h)] = ds_h * f
            df = ds_h * ug_ref[:, _head(h)]
            dfb = df.astype(BF16)
            dvn_ref[:, _head(h)] = lax.dot_general(ws_ref[h], dfb, tn_dims, preferred_element_type=F32)
            dws_ref[h] += lax.dot_general(dfb, vn_h, nt_dims, preferred_element_type=F32)
            dbs_ref[h] += jnp.broadcast_to(jnp.sum(df, axis=-1, keepdims=True), (BLOCK, BLOCK))

        row0 = pl.multiple_of(n * BLOCK, BLOCK)
        for kh in range(N_KV_HEADS):
            kb = jnp.concatenate([kp_ref[:, _head(kh)], kc_ref[:, _head(kh)], kx_ref[:, _head(kh)]], axis=0)
            vbd = jnp.concatenate([vp_ref[:, _head(kh)], vc_ref[:, _head(kh)], vx_ref[:, _head(kh)]], axis=0)
            q4 = _group_rows(q_ref, kh)
            pb = probs_ref[kh]
            p = pb.astype(F32)
            do4 = _group_rows(d_attn, kh).astype(BF16)
            dp = lax.dot_general(do4, vbd, nt_dims, preferred_element_type=F32)
            delta = jnp.sum(p * dp, axis=-1, keepdims=True)
            dsc = (p * (dp - delta) * (HEAD_DIM ** -0.5)).astype(BF16)
            d_sink = -(psink_ref[kh] * delta)
            dq4 = jnp.dot(dsc, kb, preferred_element_type=F32)
            for g in range(GQA_GROUP):
                h = kh * GQA_GROUP + g
                dsk_ref[h:h + 1, :] += jnp.broadcast_to(_sum_all(_rows_of(d_sink, g)), (1, BLOCK))
                dq_ref[:, _head(h)] = _rows_of(dq4, g)
            dk_acc[pl.ds(row0, 3 * BLOCK), _head(kh)] += lax.dot_general(dsc, q4, tn_dims, preferred_element_type=F32)
            dv_acc[pl.ds(row0, 3 * BLOCK), _head(kh)] += lax.dot_general(pb, do4, tn_dims, preferred_element_type=F32)

        @pl.when(n == nb - 1)
        def _():
            dk_ref[...] = dk_acc[BLOCK:BLOCK + s, :]
            dv_ref[...] = dv_acc[BLOCK:BLOCK + s, :]

    hh = (N_GMLP_HEADS, BLOCK, BLOCK)
    full_kv = pl.BlockSpec((s, KV_WIDTH), lambda n: (0, 0))
    return _ordered_call(
        body, name=name,
        out_shape=(jax.ShapeDtypeStruct((s, ATTN_WIDTH), F32), jax.ShapeDtypeStruct((s, KV_WIDTH), F32),
                   jax.ShapeDtypeStruct((s, KV_WIDTH), F32), jax.ShapeDtypeStruct((s, GMLP_WIDTH), F32),
                   jax.ShapeDtypeStruct((s, GMLP_WIDTH), F32), jax.ShapeDtypeStruct(hh, F32), jax.ShapeDtypeStruct(hh, F32),
                   jax.ShapeDtypeStruct((N_Q_HEADS, BLOCK), F32), jax.ShapeDtypeStruct((1, ATTN_WIDTH), F32),
                   jax.ShapeDtypeStruct((1, GMLP_WIDTH), F32)),
        grid=(nb,),
        in_specs=[_blk(ATTN_WIDTH)] + _band_specs(KV_WIDTH, nb) + _band_specs(KV_WIDTH, nb)
        + [_blk(GMLP_WIDTH), _blk(GMLP_WIDTH), _blk(ATTN_WIDTH), _blk(GMLP_WIDTH), _blk(D_MODEL), _whole3(hh), _whole3(hh),
           pl.BlockSpec((1, ATTN_WIDTH), lambda n: (0, 0)), pl.BlockSpec((1, GMLP_WIDTH), lambda n: (0, 0)),
           _per_block(PROBS_BLOCK), _per_block(PSINK_BLOCK)],
        out_specs=(_blk(ATTN_WIDTH), full_kv, full_kv, _blk(GMLP_WIDTH), _blk(GMLP_WIDTH), _whole3(hh), _whole3(hh),
                   pl.BlockSpec((N_Q_HEADS, BLOCK), lambda n: (0, 0)), pl.BlockSpec((1, ATTN_WIDTH), lambda n: (0, 0)),
                   pl.BlockSpec((1, GMLP_WIDTH), lambda n: (0, 0))),
        scratch_shapes=[pltpu.VMEM((s + 2 * BLOCK, KV_WIDTH), F32), pltpu.VMEM((s + 2 * BLOCK, KV_WIDTH), F32)],
        compiler_params=_params(("arbitrary",)),
    )(qn, kn, kn, kn, vb, vb, vb, ug, vn, attn, sgu, dmixed, wsb, bsb, ga, gs, probs, psink)


CONV_TILE = 128


PAD_ROWS = 8


def _zero_pad_rows(pad_ref):
    s = pad_ref.shape[0] - 2 * PAD_ROWS
    zeros = jnp.zeros((PAD_ROWS, pad_ref.shape[1]), F32)
    pad_ref[0:PAD_ROWS, :] = zeros
    pad_ref[PAD_ROWS + s:2 * PAD_ROWS + s, :] = zeros


def _shift_rows(a, pad_ref):
    s = a.shape[0]
    pad_ref[PAD_ROWS:PAD_ROWS + s, :] = a
    padded = pad_ref[...]
    prev = pltpu.roll(padded, 1, 0)[PAD_ROWS:PAD_ROWS + s]
    nxt = pltpu.roll(padded, s + 2 * PAD_ROWS - 1, 0)[PAD_ROWS:PAD_ROWS + s]
    return prev, nxt


def _conv_specs(s):
    tc = CONV_TILE
    nj = D_FF // tc
    return (tc, nj, pl.BlockSpec((2, s, tc), lambda j: (0, 0, j)),
            [pl.BlockSpec((3, tc), lambda j: (0, j)), pl.BlockSpec((3, tc), lambda j: (0, j + nj))],
            [pl.BlockSpec((1, tc), lambda j: (0, j)), pl.BlockSpec((1, tc), lambda j: (0, j + nj))])


def _conv_gate_fwd(a_pre, cw, cb, name):
    s = a_pre.shape[1]
    tc, nj, a_spec, w_specs, b_specs = _conv_specs(s)

    def body(a_ref, wg_ref, wu_ref, bg_ref, bu_ref, act_ref, dgu_ref, pad_ref):
        _zero_pad_rows(pad_ref)

        def conv(a, w_ref, b_ref):
            prev, nxt = _shift_rows(a, pad_ref)
            return b_ref[...] + prev * w_ref[0:1, :] + a * w_ref[1:2, :] + nxt * w_ref[2:3, :]

        g = conv(a_ref[0].astype(F32), wg_ref, bg_ref)
        u = conv(a_ref[1].astype(F32), wu_ref, bu_ref)
        sg = 1.0 / (1.0 + jnp.exp(-g))
        silu = g * sg
        act_ref[...] = (silu * u).astype(BF16)
        dgu_ref[0] = (u * (sg * (1.0 + g * (1.0 - sg)))).astype(BF16)
        dgu_ref[1] = silu.astype(BF16)

    return _ordered_call(
        body, name=name, out_shape=(jax.ShapeDtypeStruct((s, D_FF), BF16), jax.ShapeDtypeStruct((2, s, D_FF), BF16)),
        grid=(nj,), in_specs=[a_spec] + w_specs + b_specs,
        out_specs=(pl.BlockSpec((s, tc), lambda j: (0, j)), pl.BlockSpec((2, s, tc), lambda j: (0, 0, j))),
        scratch_shapes=[pltpu.VMEM((s + 2 * PAD_ROWS, tc), F32)], compiler_params=_params(("parallel",)),
    )(a_pre, cw, cw, cb, cb)


def _conv_gate_bwd(a_pre, dgu, cw, dact, name):
    s = a_pre.shape[1]
    tc, nj, a_spec, w_specs, _ = _conv_specs(s)

    def body(a_ref, dgu_ref, wg_ref, wu_ref, dact_ref, dap_ref, dcw_ref, dcb_ref, pad_ref):
        _zero_pad_rows(pad_ref)
        dact_v = dact_ref[...].astype(F32)
        for part, w_ref in enumerate((wg_ref, wu_ref)):
            da = dact_v * dgu_ref[part].astype(F32)
            a = a_ref[part].astype(F32)
            da_prev, da_next = _shift_rows(da, pad_ref)
            dcw_ref[part, 0:1, :] = _sum_rows(a * da_next)
            dcw_ref[part, 1:2, :] = _sum_rows(a * da)
            dcw_ref[part, 2:3, :] = _sum_rows(a * da_prev)
            dcb_ref[part] = _sum_rows(da)
            dap_ref[part] = (da_next * w_ref[0:1, :] + da * w_ref[1:2, :] + da_prev * w_ref[2:3, :]).astype(BF16)

    return _ordered_call(
        body, name=name,
        out_shape=(jax.ShapeDtypeStruct((2, s, D_FF), BF16), jax.ShapeDtypeStruct((2, 3, D_FF), F32),
                   jax.ShapeDtypeStruct((2, 1, D_FF), F32)),
        grid=(nj,),
        in_specs=[a_spec, pl.BlockSpec((2, s, tc), lambda j: (0, 0, j))] + w_specs + [pl.BlockSpec((s, tc), lambda j: (0, j))],
        out_specs=(pl.BlockSpec((2, s, tc), lambda j: (0, 0, j)), pl.BlockSpec((2, 3, tc), lambda j: (0, 0, j)),
                   pl.BlockSpec((2, 1, tc), lambda j: (0, 0, j))),
        scratch_shapes=[pltpu.VMEM((s + 2 * PAD_ROWS, tc), F32)], compiler_params=_params(("parallel",)),
    )(a_pre, dgu, cw, cw, dact)


def _loss_head(y, target, name):
    s, d = y.shape
    tr = _row_tile(s)

    def body(y_ref, t_ref, loss_ref, dy_ref, dyb_ref):
        err = y_ref[...] - t_ref[...]

        @pl.when(pl.program_id(0) == 0)
        def _():
            loss_ref[...] = jnp.zeros_like(loss_ref)

        loss_ref[...] += jnp.broadcast_to(0.5 * _sum_all(_mean_last(err * err)), (8, 128))
        dy = err * (1.0 / d)
        dy_ref[...] = dy
        dyb_ref[...] = dy.astype(BF16)

    return _ordered_call(
        body, name=name,
        out_shape=(jax.ShapeDtypeStruct((8, 128), F32), jax.ShapeDtypeStruct((s, d), F32), jax.ShapeDtypeStruct((s, d), BF16)),
        grid=(s // tr,), in_specs=[_rows(d, tr), _rows(d, tr)],
        out_specs=(_const2((8, 128)), _rows(d, tr), _rows(d, tr)), compiler_params=_params(("arbitrary",)),
    )(y, target)


def _row_block(rows, cols, budget=1 << 20):
    if rows * cols <= budget:
        return rows
    best = None
    for tr in range(16, rows, 16):
        if rows % tr == 0 and tr * cols <= budget:
            best = tr
    assert best is not None, (rows, cols)
    return best


def _place_shard(x4, layer, j_arr, out_dtype, name):
    _, nh, r, cols = x4.shape
    tr = _row_block(r, cols)

    def body(j_ref, x_ref, o_ref):
        o_ref[...] = x_ref[...].astype(out_dtype)

    grid_spec = pltpu.PrefetchScalarGridSpec(
        num_scalar_prefetch=1, grid=(nh, r // tr),
        in_specs=[pl.BlockSpec((None, None, tr, cols), lambda h, i, j_ref: (layer, h, i, 0))],
        out_specs=pl.BlockSpec((None, None, tr, cols), lambda h, i, j_ref: (j_ref[0], h, i, 0)))
    return _ordered_call(
        body, name=name, out_shape=jax.ShapeDtypeStruct((N_CHIPS, nh, r, cols), out_dtype), grid_spec=grid_spec,
        compiler_params=_params(("parallel", "parallel")),
    )(j_arr, x4)


def _adamw(w, g, m, v, name):
    rows, cols = w.shape
    tr = _row_block(rows, cols, 1 << 18)

    def body(w_ref, g_ref, m_ref, v_ref, go_ref, d_ref, nm_ref, nv_ref):
        gv = g_ref[...]
        go_ref[...] = gv
        mn = ADAM_B1 * m_ref[...] + (1.0 - ADAM_B1) * gv
        vn = ADAM_B2 * v_ref[...] + (1.0 - ADAM_B2) * (gv * gv)
        m_hat = mn / (1.0 - ADAM_B1 ** ADAM_STEP)
        v_hat = vn / (1.0 - ADAM_B2 ** ADAM_STEP)
        d_ref[...] = -ADAM_LR * (m_hat / (jnp.sqrt(v_hat) + ADAM_EPS) + ADAM_WD * w_ref[...])
        nm_ref[...] = mn
        nv_ref[...] = vn

    sds = jax.ShapeDtypeStruct((rows, cols), F32)
    return _ordered_call(
        body, name=name, out_shape=(sds, sds, sds, sds), grid=(rows // tr,),
        in_specs=[_rows(cols, tr)] * 4, out_specs=(_rows(cols, tr),) * 4, compiler_params=_params(("parallel",)),
    )(w, g, m, v)


def _chip_sum(p4, recv3, j_arr, c_arr, name):
    _, rh, cols = p4.shape
    tr = _row_block(rh, cols, 1 << 19)

    def body(j_ref, c_ref, p_ref, r_ref, o_ref):
        total = p_ref[...].astype(F32)
        for peer in range(3):
            total = total + r_ref[peer].astype(F32)
        o_ref[...] = total.astype(BF16)

    grid_spec = pltpu.PrefetchScalarGridSpec(
        num_scalar_prefetch=2, grid=(rh // tr,),
        in_specs=[pl.BlockSpec((None, tr, cols), lambda i, j_ref, c_ref: (j_ref[0], i, 0)),
                  pl.BlockSpec((3, tr, cols), lambda i, j_ref, c_ref: (0, i, 0))],
        out_specs=pl.BlockSpec((None, tr, cols), lambda i, j_ref, c_ref: (c_ref[0], i, 0)))
    return _ordered_call(
        body, name=name, out_shape=jax.ShapeDtypeStruct((2, rh, cols), BF16), grid_spec=grid_spec,
        compiler_params=_params(("parallel",)),
    )(j_arr, c_arr, p4, recv3)


def _adamw_layer(w, g, m, v, layer, into, name):
    nl, rows, cols = w.shape
    slabs, _, width = g.shape
    assert slabs * width == cols and g.shape[1] == rows, (name, w.shape, g.shape)
    tr = _row_block(rows, width, 1 << 19)
    at_layer = pl.BlockSpec((None, tr, width), lambda h, i: (layer, i, h))

    def body(w_ref, g_ref, m_ref, v_ref, *rest):
        go_ref, d_ref, nm_ref, nv_ref = rest[-4:]
        gv = g_ref[...].astype(F32)
        go_ref[...] = gv
        mn = ADAM_B1 * m_ref[...] + (1.0 - ADAM_B1) * gv
        vn = ADAM_B2 * v_ref[...] + (1.0 - ADAM_B2) * (gv * gv)
        m_hat = mn / (1.0 - ADAM_B1 ** ADAM_STEP)
        v_hat = vn / (1.0 - ADAM_B2 ** ADAM_STEP)
        d_ref[...] = -ADAM_LR * (m_hat / (jnp.sqrt(v_hat) + ADAM_EPS) + ADAM_WD * w_ref[...])
        nm_ref[...] = mn
        nv_ref[...] = vn

    in_specs = [at_layer, pl.BlockSpec((None, tr, width), lambda h, i: (h, i, 0)), at_layer, at_layer]
    operands = [w, g, m, v]
    aliases = {}
    if into is not None:
        in_specs += [ANY] * 4
        operands += list(into)
        aliases = {4 + i: i for i in range(4)}
    sds = jax.ShapeDtypeStruct((nl, rows, cols), F32)
    return _ordered_call(
        body, name=name, out_shape=(sds,) * 4, grid=(slabs, rows // tr), in_specs=in_specs, out_specs=(at_layer,) * 4,
        input_output_aliases=aliases, compiler_params=_params(("parallel", "parallel")),
    )(*operands)


def _sum_devices(mine, landed, me_arr, name):
    rows, lanes = mine.shape

    def body(me_ref, mine_ref, landed_ref, o_ref):
        total = None
        for dev in range(8):
            part = jnp.where(me_ref[0] == dev, mine_ref[...], landed_ref[dev])
            total = part if total is None else total + part
        o_ref[...] = total

    grid_spec = pltpu.PrefetchScalarGridSpec(
        num_scalar_prefetch=1, grid=(1,),
        in_specs=[pl.BlockSpec((rows, lanes), lambda i, me_ref: (0, 0)), pl.BlockSpec((8, rows, lanes), lambda i, me_ref: (0, 0, 0))],
        out_specs=pl.BlockSpec((rows, lanes), lambda i, me_ref: (0, 0)))
    return _ordered_call(
        body, name=name, out_shape=jax.ShapeDtypeStruct((rows, lanes), F32), grid_spec=grid_spec,
        compiler_params=_params(("arbitrary",)),
    )(me_arr, mine, landed)


def _place():
    x, y, c = lax.axis_index("x"), lax.axis_index("y"), lax.axis_index("c")
    chips = [(1 - x, y), (x, 1 - y), (1 - x, 1 - y)]
    return x, y, c, chips


HBM = pl.BlockSpec(memory_space=pltpu.HBM)
SEM = pl.BlockSpec(memory_space=pltpu.SEMAPHORE)
TOKEN = jax.ShapeDtypeStruct((8, 128), F32)


def _remote(src, dst, send_sem, recv_sem, to):
    return pltpu.make_async_remote_copy(src_ref=src, dst_ref=dst, send_sem=send_sem, recv_sem=recv_sem, device_id=to,
                                        device_id_type=MESH)


def _split_call(body, name, thru, sems_in=(), fresh=(), new_sems=(), after_last=True):
    n_t, n_s, n_f = len(thru), len(sems_in), len(fresh)

    def call_body(*refs):
        outs = refs[n_t + n_s:]
        body(refs[:n_t], refs[n_t:n_t + n_s], outs[1 + n_t:1 + n_t + n_f], outs[1 + n_t + n_f:])
        outs[0][...] = jnp.zeros_like(outs[0])

    out_shape = ([TOKEN] + [pltpu.HBM(t.shape, t.dtype) for t in thru] + [pltpu.HBM(shp, dt) for shp, dt in fresh]
                 + [pltpu.SemaphoreType.DMA(shp) for shp in new_sems])
    out_specs = [pl.BlockSpec(memory_space=pltpu.VMEM)] + [HBM] * (n_t + n_f) + [SEM] * len(new_sems)
    if not after_last or any(t is _Order.last for t in thru):
        _Order.last = None
    out = _ordered_call(
        call_body, name=name, out_shape=tuple(out_shape), in_specs=[HBM] * n_t + [SEM] * n_s, out_specs=tuple(out_specs),
        input_output_aliases={i: 1 + i for i in range(n_t)},
        compiler_params=pltpu.CompilerParams(has_side_effects=pltpu.SideEffectType.DATAFLOW_SIDE_EFFECTING),
    )(*[pltpu.with_memory_space_constraint(t, pltpu.HBM) for t in thru], *sems_in)
    return out[1:1 + n_t], out[1 + n_t:1 + n_t + n_f], out[1 + n_t + n_f:]


class _Exchange:
    def __init__(self, weights, m_in, v_in, j_arr, c_arr, me_arr):
        self.w, self.m, self.v = weights, m_in, v_in
        self.j_arr, self.c_arr, self.me_arr = j_arr, c_arr, me_arr
        self.adam, self.small, self.pairs, self.held = {}, {}, {}, None
        self.o_arr = 1 - c_arr
        self.groups = [(l, name) for l in range(DEPTH) for name in BIG_NAMES]
        self.shard_shape = {name: weights[name].shape[1:] for name in BIG_NAMES}
        self.conv_state, self.state = [], {}
        self.ready, self.conv_ready = {}, {}
        self.pending, self.tick, self.reduced = [], 0, {}

        def place(grp):
            l, name = grp
            nl, r, cols = weights[name].shape
            return _place_shard(weights[name].reshape(nl, 2, r // 2, cols), l, j_arr, BF16, f"place_{name}_l{l}")

        def start_copies(tag, convs, groups, bufs):
            n_c = len(convs)

            def start(thru, _, __, sems):
                x, y, c, chips = _place()
                j_me = 2 * x + y
                copies = []
                for i in range(len(thru)):
                    mine = thru[i].at[j_me] if i < n_c else thru[i].at[j_me, c]
                    copies += [_remote(mine, mine, sems[2 * i].at[k], sems[2 * i + 1].at[k], (*chip, c))
                               for k, chip in enumerate(chips)]
                for cp in copies:
                    cp.start()

            thru, _, sems = _split_call(start, tag, convs + bufs, new_sems=[(3,)] * (2 * (n_c + len(bufs))))
            self.conv_state += [(thru[i], sems[2 * i], sems[2 * i + 1]) for i in range(n_c)]
            for g, grp in enumerate(groups):
                self.state[grp] = (thru[n_c + g], sems[2 * (n_c + g)], sems[2 * (n_c + g) + 1])

        convs = [_place_shard(weights["conv_w"][:, None], l, j_arr, F32, f"place_conv_w_l{l}") for l in range(DEPTH)]
        start_copies("gather_start_first", convs, self.groups[:1], [place(self.groups[0])])
        start_copies("gather_start_rest", [], self.groups[1:], [place(grp) for grp in self.groups[1:]])

    def conv_w(self, l):
        if l not in self.conv_ready:
            buf, send, recv = self.conv_state[l]

            def wait(thru, sems, _, __):
                x, y, c, chips = _place()
                for k, chip in enumerate(chips):
                    mine, theirs = thru[0].at[2 * x + y], thru[0].at[2 * chip[0] + chip[1]]
                    _remote(mine, mine, sems[0].at[k], sems[1].at[k], (*chip, c)).wait_send()
                    _remote(theirs, theirs, sems[0].at[k], sems[1].at[k], (x, y, c)).wait_recv()

            (buf,), _, _ = _split_call(wait, f"gather_conv_w_l{l}", [buf], sems_in=[send, recv])
            self.conv_ready[l] = jnp.transpose(buf[:, 0], (1, 0, 2)).reshape(3, 2 * D_FF)
        return self.conv_ready[l]

    def weight(self, l, name):
        grp = (l, name)
        if grp not in self.ready:
            buf, send, recv = self.state[grp]

            def forward(thru, sems, _, new):
                x, y, c, chips = _place()
                for k, chip in enumerate(chips):
                    landed = thru[0].at[2 * chip[0] + chip[1], c]
                    _remote(landed, landed, new[0].at[k], sems[0].at[k], (x, y, c)).wait_recv()
                    _remote(landed, landed, new[0].at[k], new[1].at[k], (x, y, 1 - c)).start()

            (buf,), _, (fsend, frecv) = _split_call(forward, f"gather_pass_{name}_l{l}", [buf], sems_in=[recv],
                                                    new_sems=[(3,), (3,)])

            def finish(thru, sems, _, __):
                x, y, c, chips = _place()
                mine = thru[0].at[2 * x + y, c]
                for k, chip in enumerate(chips):
                    j_k = 2 * chip[0] + chip[1]
                    theirs, landed = thru[0].at[j_k, 1 - c], thru[0].at[j_k, c]
                    _remote(theirs, theirs, sems[1].at[k], sems[2].at[k], (x, y, c)).wait_recv()
                    _remote(landed, landed, sems[1].at[k], sems[2].at[k], (x, y, 1 - c)).wait_send()
                    _remote(mine, mine, sems[0].at[k], sems[2].at[k], (*chip, c)).wait_send()

            (buf,), _, _ = _split_call(finish, f"gather_done_{name}_l{l}", [buf], sems_in=[send, fsend, frecv])
            r, cols = self.shard_shape[name]
            self.ready[grp] = buf.reshape(N_CHIPS, r, cols) if name in ("w_in", "w_up") else buf.reshape(N_CHIPS * r, cols)
        return self.ready[grp]

    def pair_send(self, l, name, other):
        held = self.held
        self.held = None

        def start(thru, _, fresh, sems):
            x, y, c, chips = _place()
            copies = [_remote(thru[0], fresh[0], sems[0], sems[1], (x, y, 1 - c))]
            if held is not None:
                copies += [_remote(thru[1].at[2 * chip[0] + chip[1]], fresh[1].at[k], sems[2].at[k], sems[3].at[k], (*chip, c))
                           for k, chip in enumerate(chips)]
            for cp in copies:
                cp.start()

        thru, fresh, new_sems = [other], [(other.shape, BF16)], [(), ()]
        if held is not None:
            thru, fresh, new_sems = thru + [held[2]], fresh + [((3,) + held[2].shape[1:], BF16)], new_sems + [(3,), (3,)]
        thru, fresh, sems = _split_call(start, f"pair_start_{name}_l{l}", thru, fresh=fresh, new_sems=new_sems, after_last=False)
        self.pairs[(l, name)] = (thru[0], fresh[0], sems[:2])
        if held is not None:
            self.pending.append(dict(l=held[0], name=held[1], stage=2, at=self.tick, bufs=(thru[1], fresh[1]), sems=sems[2:]))

    def pair_recv(self, l, name):
        other, recv, sems = self.pairs.pop((l, name))

        def wait(thru, sems, _, __):
            x, y, c, _chips = _place()
            cp = _remote(thru[0], thru[1], sems[0], sems[1], (x, y, 1 - c))
            cp.wait_send()
            cp.wait_recv()

        (_, recv), _, _ = _split_call(wait, f"pair_done_{name}_l{l}", [other, recv], sems_in=list(sems))
        return recv

    def scatter(self, l, name, p4):
        assert self.held is None
        self.held = (l, name, p4)
        if (l, name) == (0, BIG_NAMES[0]):
            self._scatter_held()

    def _scatter_held(self):
        l, name, p4 = self.held
        self.held = None

        def start(thru, _, fresh, sems):
            x, y, c, chips = _place()
            for k, chip in enumerate(chips):
                _remote(thru[0].at[2 * chip[0] + chip[1]], fresh[0].at[k], sems[0].at[k], sems[1].at[k], (*chip, c)).start()

        (p4,), (recv3,), sems = _split_call(start, f"chips_start_{name}_l{l}", [p4], fresh=[((3,) + p4.shape[1:], BF16)],
                                           new_sems=[(3,), (3,)], after_last=False)
        self.pending.append(dict(l=l, name=name, stage=2, at=self.tick, bufs=(p4, recv3), sems=sems))

    def point(self, drain=False):
        self.tick += 1
        if drain:
            old = [g for g in self.pending if g["stage"] == 2 and g["at"] + 2 <= self.tick]
            new = [g for g in self.pending if g["stage"] == 2 and g["at"] + 2 > self.tick]
            for grp in old + [g for g in self.pending if g["stage"] == 3] + new:
                self._advance([grp] if grp["stage"] == 3 else [], [grp] if grp["stage"] == 2 else [])
        else:
            self._advance([grp for grp in self.pending if grp["stage"] == 3 and grp["at"] < self.tick],
                          [grp for grp in self.pending if grp["stage"] == 2 and grp["at"] + 2 <= self.tick])

    def _advance(self, joined, landed):
        if not joined and not landed:
            return
        n_j, n_l = len(joined), len(landed)

        def wait(thru, sems, _, __):
            x, y, c, chips = _place()
            for i in range(n_j):
                buf, send, recv = thru[i], sems[2 * i], sems[2 * i + 1]
                _remote(buf.at[c], buf.at[c], send, recv, (x, y, 1 - c)).wait_send()
                _remote(buf.at[1 - c], buf.at[1 - c], send, recv, (x, y, c)).wait_recv()
            for i in range(n_l):
                p4, recv3 = thru[n_j + 2 * i], thru[n_j + 2 * i + 1]
                send, recv = sems[2 * (n_j + i)], sems[2 * (n_j + i) + 1]
                for k, chip in enumerate(chips):
                    cp = _remote(p4.at[2 * chip[0] + chip[1]], recv3.at[k], send.at[k], recv.at[k], (*chip, c))
                    cp.wait_send()
                    cp.wait_recv()

        tag = "_".join([f"{grp['name']}{grp['l']}_halves" for grp in joined] + [f"{grp['name']}{grp['l']}_chips" for grp in landed])
        bufs, _, _ = _split_call(wait, f"landed_{tag}", [b for grp in joined + landed for b in grp["bufs"]],
                                 sems_in=[sm for grp in joined + landed for sm in grp["sems"]])
        for i, grp in enumerate(joined):
            l, name, full = grp["l"], grp["name"], bufs[i]
            if GRAD_HALVES[name][0] != "cols_of_block":
                full = full.reshape((1,) + tuple(self.shard_shape[name]))
            self.adam[name] = _adamw_layer(self.w[name], full, self.m[name], self.v[name], l, self.adam.get(name),
                                           f"adamw_{name}_l{l}")
            grp.update(stage=4)
        if not landed:
            return
        halves = [_chip_sum(bufs[n_j + 2 * i], bufs[n_j + 2 * i + 1], self.j_arr, self.c_arr,
                            f"chip_sum_{grp['name']}_l{grp['l']}") for i, grp in enumerate(landed)]

        def start(thru, _, __, sems):
            x, y, c, _chips = _place()
            for i in range(n_l):
                _remote(thru[i].at[c], thru[i].at[c], sems[2 * i], sems[2 * i + 1], (x, y, 1 - c)).start()

        tag = "_".join(f"{grp['name']}{grp['l']}" for grp in landed)
        halves, _, sems = _split_call(start, f"join_start_{tag}", halves, new_sems=[()] * (2 * n_l), after_last=False)
        for i, grp in enumerate(landed):
            grp.update(stage=3, at=self.tick, bufs=(halves[i],), sems=tuple(sems[2 * i:2 * i + 2]))

    def finish(self):
        if self.held is not None:
            self._scatter_held()
        while any(grp["stage"] < 4 for grp in self.pending):
            self.point(drain=True)
        return self.adam

    @staticmethod
    def _peer(k, x, y, c):
        return (1 - x if k & 4 else x, 1 - y if k & 2 else y, 1 - c if k & 1 else c)

    def small_grads(self, l, grads, loss_tile):
        parts = [grads[nm] for nm in SMALL_NAMES] + ([loss_tile[0, 0:1]] if loss_tile is not None else [])
        packed = _pack_call(parts, f"small_pack_l{l}")
        rows = packed.shape[0]

        def start(thru, _, fresh, sems):
            x, y, c, _chips = _place()
            for k in range(1, 8):
                _remote(thru[0], fresh[0].at[4 * x + 2 * y + c], sems[0].at[k - 1], sems[1].at[k - 1],
                        self._peer(k, x, y, c)).start()

        (packed,), (landed,), sems = _split_call(start, f"small_start_l{l}", [packed], fresh=[((8, rows, PACK_LANES), F32)],
                                                 new_sems=[(7,), (7,)], after_last=False)
        self.small[l] =(packed, landed, sems, [p.shape for p in parts])

    def small_sum(self, l):
        packed, landed, sems, _shapes = self.small[l]

        def wait(thru, sems, _, __):
            x, y, c, _chips = _place()
            for k in range(1, 8):
                px, py, pc = self._peer(k, x, y, c)
                _remote(thru[0], thru[1].at[4 * x + 2 * y + c], sems[0].at[k - 1], sems[1].at[k - 1], (px, py, pc)).wait_send()
                _remote(thru[0], thru[1].at[4 * px + 2 * py + pc], sems[0].at[k - 1], sems[1].at[k - 1], (x, y, c)).wait_recv()

        (packed, landed), _, _ = _split_call(wait, f"small_done_l{l}", [packed, landed], sems_in=list(sems))
        return _sum_devices(packed, landed, self.me_arr, f"small_sum_l{l}")


def _rope_tables(s):
    inv_freq = ROPE_THETA ** (-jnp.arange(0, HEAD_DIM, 2, dtype=F32) / HEAD_DIM)
    ang = jnp.arange(s, dtype=F32)[:, None] * inv_freq[None, :]
    cos, sin = jnp.cos(ang), jnp.sin(ang)
    return jnp.concatenate([cos, cos], axis=-1), jnp.concatenate([-sin, sin], axis=-1)


def _local_step(x, target, ex, small):
    s = x.shape[0]
    cosf, sinf = _rope_tables(s)
    saved = []
    for l in range(DEPTH):
        p = small[l]
        t = f"l{l}"
        h = _rms_fwd(x, p["norm1_g"], f"norm1_{t}")
        z = _matmul(h, ex.weight(l, "w_in"), mode="nn", out_dtype=BF16, tm=1024, tn=896, tk=2048, b_parts=4, name=f"proj_in_{t}")
        qn, kn, vb, ug, vn, *gate_kept = _proj_post(z, p["q_norm_g"], p["k_norm_g"], p["sgu_ln_g"], p["sgu_ln_b"], cosf, sinf,
                                                    f"proj_post_{t}")
        attn, sgu, mixed, probs, psink = _mixer_fwd(qn, kn, vb, ug, vn, p["w_s_bf16"], p["b_s_tile"], p["sink"],
                                                    p["attn_out_g"], p["sgu_out_g"], f"mixer_{t}")
        x1 = _matmul(mixed, ex.weight(l, "w_o"), mode="nn", out_dtype=F32, tm=2048, tn=256, tk=2048, res=x,
                     name=f"proj_out_{t}")
        h2 = _rms_fwd(x1, p["norm2_g"], f"norm2_{t}")
        a_pre = _matmul(h2, ex.weight(l, "w_up"), mode="nn", out_dtype=BF16, tm=1024, tn=1408, tk=2048, b_parts=4,
                        out_parts=2,
                        name=f"ffn_up_{t}")
        act, dgu = _conv_gate_fwd(a_pre, ex.conv_w(l), p["conv_b"], f"conv_gate_{t}")
        x2 = _matmul(act, ex.weight(l, "w_down"), mode="nn", out_dtype=F32, tm=1024, tn=256, tk=D_FF, res=x1,
                     name=f"ffn_down_{t}")
        saved.append(dict(x=x, h=h, z=z, qn=qn, kn=kn, vb=vb, ug=ug, vn=vn, attn=attn, sgu=sgu, mixed=mixed, x1=x1, h2=h2,
                          a_pre=a_pre, act=act, dgu=dgu, probs=probs, psink=psink, gate_kept=gate_kept))
        x = x2
    loss_tile, dx, dxb = _loss_head(x, target, "loss_head")
    for l in reversed(range(DEPTH)):
        p, sv = small[l], saved[l]
        t = f"l{l}"
        def weight_grad(name, a, g, between, g_parts=0):
            ex.pair_send(l, name, _grad_half(name, a, g, ex.o_arr, None, f"g_{name}_other_{t}", g_parts))
            out = between()
            ex.scatter(l, name, _grad_half(name, a, g, ex.c_arr, ex.pair_recv(l, name), f"g_{name}_own_{t}", g_parts))
            ex.point()
            return out

        def after_down():
            dact = _matmul(dxb, ex.weight(l, "w_down"), mode="nt", out_dtype=BF16, tm=1024, tn=512, tk=2048,
                           name=f"d_act_{t}")
            return _conv_gate_bwd(sv["a_pre"], sv["dgu"], ex.conv_w(l), dact, f"conv_gate_bwd_{t}")

        dap, dcw, dcb = weight_grad("w_down", sv["act"], dxb, after_down)

        def after_up():
            dh2 = _matmul(dap, ex.weight(l, "w_up"), mode="nt", out_dtype=BF16, tm=1024, tn=1024, tk=2816, a_parts=2,
                          b_parts=4, name=f"d_h2_{t}")
            return _rms_bwd(sv["x1"], p["norm2_g"], dh2, dx, f"norm2_bwd_{t}")

        dx1, dx1b, dg2 = weight_grad("w_up", sv["h2"], dap, after_up, g_parts=2)
        ex.pair_send(l, "w_o", _grad_half("w_o", sv["mixed"], dx1b, ex.o_arr, None, f"g_w_o_other_{t}"))
        dmixed = _matmul(dx1b, ex.weight(l, "w_o"), mode="nt", out_dtype=BF16, tm=1024, tn=512, tk=2048,
                         name=f"d_mixed_{t}")
        dqn, dkn, dvb, dug, dvn, dws, dbs, dsk, dga, dgs = _mixer_bwd(
            sv["qn"], sv["kn"], sv["vb"], sv["ug"], sv["vn"], sv["attn"], sv["sgu"], dmixed, p["w_s_bf16"], p["b_s_tile"],
            p["attn_out_g"], p["sgu_out_g"], sv["probs"], sv["psink"], f"mixer_bwd_{t}")
        dz, dqg, dkg, dlg, dlb = _proj_post_bwd(sv["z"], dqn, dkn, dvb, dug, dvn, *sv["gate_kept"], p["q_norm_g"], p["k_norm_g"],
                                                 p["sgu_ln_g"], cosf, sinf, f"proj_post_bwd_{t}")
        ex.scatter(l, "w_o", _grad_half("w_o", sv["mixed"], dx1b, ex.c_arr, ex.pair_recv(l, "w_o"), f"g_w_o_own_{t}"))
        ex.point()

        def after_in():
            dh = _matmul_nt_slabs(dz, ex.weight(l, "w_in"), tm=1024, tn=512, name=f"d_h_{t}")
            return _rms_bwd(sv["x"], p["norm1_g"], dh, dx1, f"norm1_bwd_{t}")

        dx, dxb, dg1 = weight_grad("w_in", sv["h"], dz, after_in)
        ex.small_grads(l, dict(
            norm1_g=dg1[0], q_norm_g=dqg[0], k_norm_g=dkg[0], sink=dsk[:, 0], sgu_ln_g=dlg[0], sgu_ln_b=dlb[0], w_s=dws,
            b_s=dbs[:, :, 0], attn_out_g=dga[0], sgu_out_g=dgs[0], norm2_g=dg2[0],
            conv_w=jnp.concatenate([dcw[0], dcw[1]], axis=-1), conv_b=jnp.concatenate([dcb[0, 0], dcb[1, 0]], axis=-1)),
            loss_tile if l == 0 else None)
    return dx


def _small_views(l, norm1_g, q_norm_g, k_norm_g, sink, sgu_ln_g, sgu_ln_b, w_s, b_s, attn_out_g, sgu_out_g, norm2_g, conv_b):
    return dict(
        norm1_g=norm1_g[l][None], q_norm_g=q_norm_g[l][None], k_norm_g=k_norm_g[l][None], sink=sink[l],
        sgu_ln_g=sgu_ln_g[l][None], sgu_ln_b=sgu_ln_b[l][None], w_s_bf16=w_s[l].astype(BF16),
        b_s_tile=jnp.broadcast_to(b_s[l][:, :, None], (N_GMLP_HEADS, BLOCK, BLOCK)), attn_out_g=attn_out_g[l][None],
        sgu_out_g=sgu_out_g[l][None], norm2_g=norm2_g[l][None], conv_b=conv_b[l][None])


SMALL_NAMES = ("norm1_g", "q_norm_g", "k_norm_g", "sink", "sgu_ln_g", "sgu_ln_b", "w_s", "b_s", "attn_out_g", "sgu_out_g",
               "norm2_g", "conv_b", "conv_w")
REPLICATED_NAMES = SMALL_NAMES[:-1]
BIG_NAMES = ("w_in", "w_o", "w_up", "w_down")
PACK_LANES = 128
PACK_ALIGN = 8 * PACK_LANES


def _pack_rows(shape):
    return -(-math.prod(shape) // PACK_ALIGN) * 8


def _pack_parts(arrays):
    parts = []
    for a in arrays:
        flat = a.reshape(-1)
        parts.append(jnp.pad(flat, (0, _pack_rows(a.shape) * PACK_LANES - flat.shape[0])).reshape(-1, PACK_LANES))
    return parts


def _pack_call(arrays, name):
    parts = _pack_parts(arrays)
    total = sum(p.shape[0] for p in parts)

    def body(*refs):
        o_ref, at = refs[-1], 0
        for p_ref in refs[:-1]:
            o_ref[at:at + p_ref.shape[0], :] = p_ref[...]
            at += p_ref.shape[0]

    vm = pl.BlockSpec(memory_space=pltpu.VMEM)
    return _ordered_call(
        body, name=name, out_shape=jax.ShapeDtypeStruct((total, PACK_LANES), F32), in_specs=[vm] * len(parts), out_specs=vm,
        compiler_params=pltpu.CompilerParams(vmem_limit_bytes=V7X_VMEM_LIMIT),
    )(*parts)


def _unpack_layers(stacked, shapes):
    nl = stacked.shape[0]
    out, at = [], 0
    for shp in shapes:
        rows = _pack_rows(shp)
        out.append(stacked[:, at:at + rows].reshape(nl, -1)[:, :math.prod(shp)].reshape((nl,) + tuple(shp)))
        at += rows
    return out


def _adamw_packed(w, g, m, v, rows, layer, into, name):
    head = pl.BlockSpec((rows, PACK_LANES), lambda i: (0, 0))
    at_layer = pl.BlockSpec((None, rows, PACK_LANES), lambda i: (layer, 0, 0))

    def body(w_ref, g_ref, m_ref, v_ref, *rest):
        d_ref, nm_ref, nv_ref = rest[-3:]
        gv = g_ref[...]
        mn = ADAM_B1 * m_ref[...] + (1.0 - ADAM_B1) * gv
        vn = ADAM_B2 * v_ref[...] + (1.0 - ADAM_B2) * (gv * gv)
        m_hat = mn / (1.0 - ADAM_B1 ** ADAM_STEP)
        v_hat = vn / (1.0 - ADAM_B2 ** ADAM_STEP)
        d_ref[...] = -ADAM_LR * (m_hat / (jnp.sqrt(v_hat) + ADAM_EPS) + ADAM_WD * w_ref[...])
        nm_ref[...] = mn
        nv_ref[...] = vn

    in_specs = [head] * 4
    operands = [w, g, m, v]
    aliases = {}
    if into is not None:
        in_specs += [ANY] * 3
        operands += list(into)
        aliases = {4 + i: i for i in range(3)}
    sds = jax.ShapeDtypeStruct((DEPTH, rows, PACK_LANES), F32)
    return _ordered_call(
        body, name=name, out_shape=(sds,) * 3, grid=(1,), in_specs=in_specs, out_specs=(at_layer,) * 3,
        input_output_aliases=aliases, compiler_params=_params(("arbitrary",)),
    )(*operands)


def kernel(x, norm1_g, w_in, q_norm_g, k_norm_g, sink, sgu_ln_g, sgu_ln_b, w_s, b_s, attn_out_g, sgu_out_g, w_o, norm2_g, w_up, conv_w, conv_b, w_down, loss_target, m_norm1_g, m_w_in, m_q_norm_g, m_k_norm_g, m_sink, m_sgu_ln_g, m_sgu_ln_b, m_w_s, m_b_s, m_attn_out_g, m_sgu_out_g, m_w_o, m_norm2_g, m_w_up, m_conv_w, m_conv_b, m_w_down, v_norm1_g, v_w_in, v_q_norm_g, v_k_norm_g, v_sink, v_sgu_ln_g, v_sgu_ln_b, v_w_s, v_b_s, v_attn_out_g, v_sgu_out_g, v_w_o, v_norm2_g, v_w_up, v_conv_w, v_conv_b, v_w_down):
    weights = dict(norm1_g=norm1_g, w_in=w_in, q_norm_g=q_norm_g, k_norm_g=k_norm_g, sink=sink, sgu_ln_g=sgu_ln_g,
                   sgu_ln_b=sgu_ln_b, w_s=w_s, b_s=b_s, attn_out_g=attn_out_g, sgu_out_g=sgu_out_g, w_o=w_o, norm2_g=norm2_g,
                   w_up=w_up, conv_w=conv_w, conv_b=conv_b, w_down=w_down)
    m_in = dict(norm1_g=m_norm1_g, w_in=m_w_in, q_norm_g=m_q_norm_g, k_norm_g=m_k_norm_g, sink=m_sink, sgu_ln_g=m_sgu_ln_g,
                sgu_ln_b=m_sgu_ln_b, w_s=m_w_s, b_s=m_b_s, attn_out_g=m_attn_out_g, sgu_out_g=m_sgu_out_g, w_o=m_w_o,
                norm2_g=m_norm2_g, w_up=m_w_up, conv_w=m_conv_w, conv_b=m_conv_b, w_down=m_w_down)
    v_in = dict(norm1_g=v_norm1_g, w_in=v_w_in, q_norm_g=v_q_norm_g, k_norm_g=v_k_norm_g, sink=v_sink, sgu_ln_g=v_sgu_ln_g,
                sgu_ln_b=v_sgu_ln_b, w_s=v_w_s, b_s=v_b_s, attn_out_g=v_attn_out_g, sgu_out_g=v_sgu_out_g, w_o=v_w_o,
                norm2_g=v_norm2_g, w_up=v_w_up, conv_w=v_conv_w, conv_b=v_conv_b, w_down=v_w_down)
    cx, cy, cc = lax.axis_index("x"), lax.axis_index("y"), lax.axis_index("c")
    j_me = 2 * cx + cy
    c_arr = jnp.reshape(cc, (1,)).astype(jnp.int32)
    j_arr = jnp.reshape(j_me, (1,)).astype(jnp.int32)

    _Order.last = None
    ex = _Exchange(weights, m_in, v_in, j_arr, c_arr, jnp.reshape(4 * cx + 2 * cy + cc, (1,)).astype(jnp.int32))
    small = [_small_views(l, norm1_g, q_norm_g, k_norm_g, sink, sgu_ln_g, sgu_ln_b, w_s, b_s, attn_out_g, sgu_out_g, norm2_g,
                          conv_b) for l in range(DEPTH)]
    held_back, _ = lax.optimization_barrier(([[src[nm] for nm in REPLICATED_NAMES] for src in (weights, m_in, v_in)], _Order.last))
    packed_in = [[_pack_call([arr[l] for arr in arrays], f"pack_{tag}_l{l}") for tag, arrays in zip("wmv", held_back)]
                 for l in range(DEPTH)]
    dx = _local_step(x[0], loss_target[0], ex, small)
    big_out = ex.finish()

    rep_shapes = [weights[nm].shape[1:] for nm in REPLICATED_NAMES]
    rep_rows = sum(_pack_rows(shp) for shp in rep_shapes)
    cw_shape = (3, 2 * D_FF)
    sums, adam_small = [None] * DEPTH, None
    for l in reversed(range(DEPTH)):
        sums[l] = ex.small_sum(l)
        pw, pm, pv = packed_in[l]
        adam_small = _adamw_packed(pw, sums[l], pm, pv, rep_rows, l, adam_small, f"adamw_small_l{l}")
    cw_rows = _pack_rows(cw_shape)
    loss = sums[0][rep_rows + cw_rows, 0]
    stacked = jnp.stack([sm[:rep_rows + cw_rows] for sm in sums])
    grads = dict(zip(REPLICATED_NAMES, _unpack_layers(stacked[:, :rep_rows], rep_shapes)))
    delta, new_m, new_v = (dict(zip(REPLICATED_NAMES, _unpack_layers(arr, rep_shapes))) for arr in adam_small)
    cw_cols = 2 * D_FF // N_CHIPS
    cw_grad = lax.dynamic_slice_in_dim(_unpack_layers(stacked[:, rep_rows:], [cw_shape])[0], j_me * cw_cols, cw_cols, axis=2)
    flat = lambda a: a.reshape(DEPTH * 3, cw_cols)
    cw_out = _adamw(flat(conv_w), flat(cw_grad), flat(m_conv_w), flat(v_conv_w), "adamw_conv_w")
    grads["conv_w"], delta["conv_w"], new_m["conv_w"], new_v["conv_w"] = (a.reshape(DEPTH, 3, cw_cols) for a in cw_out)

    for name in BIG_NAMES:
        grads[name], delta[name], new_m[name], new_v[name] = big_out[name]

    order = ("norm1_g", "w_in", "q_norm_g", "k_norm_g", "sink", "sgu_ln_g", "sgu_ln_b", "w_s", "b_s", "attn_out_g", "sgu_out_g",
             "w_o", "norm2_g", "w_up", "conv_w", "conv_b", "w_down")
    return (loss, dx[None], *[grads[nm] for nm in order], *[delta[nm] for nm in order], *[new_m[nm] for nm in order],
            *[new_v[nm] for nm in order])
```

```python
import math

import jax
import jax.numpy as jnp
from jax import lax
from jax.experimental import pallas as pl
from jax.experimental.pallas import tpu as pltpu

F32 = jnp.float32
BF16 = jnp.bfloat16

D_MODEL = 2048
HEAD_DIM = 128
ATTN_WIDTH = 1024
N_Q_HEADS = 8
N_KV_HEADS = 2
GQA_GROUP = 4
KV_WIDTH = 256
GMLP_WIDTH = 1024
N_GMLP_HEADS = 8
BLOCK = 128
IN_WIDTH = 3584
D_FF = 5632
DEPTH = 2
EPS = 1e-6
MASK_VALUE = -1e30
ROPE_THETA = 10000.0
N_CHIPS = 4

ADAM_LR = 0.001
ADAM_B1 = 0.9
ADAM_B2 = 0.999
ADAM_EPS = 1e-08
ADAM_WD = 0.01
ADAM_STEP = 10

V7X_VMEM_LIMIT = 48 * 1024 * 1024
MESH = pl.DeviceIdType.MESH

_GELU_C = math.sqrt(2.0 / math.pi)
_GELU_A = 0.044715


def _params(sem=None):
    return pltpu.CompilerParams(dimension_semantics=sem, vmem_limit_bytes=V7X_VMEM_LIMIT)


ANY = pl.BlockSpec(memory_space=pl.ANY)


class _Order:
    last = None


def _ordered_call(body, *, token_index=0, **kw):
    def run(*operands):
        tok = _Order.last
        if tok is None or any(op is tok for op in operands):
            call = pl.pallas_call(body, **kw)
        else:
            n_in = len(operands)

            def ordered_body(*refs):
                return body(*refs[:n_in], *refs[n_in + 1:])

            kw2 = dict(kw)
            if "grid_spec" in kw2:
                gs = kw2["grid_spec"]
                kw2["grid_spec"] = pltpu.PrefetchScalarGridSpec(
                    num_scalar_prefetch=gs.num_scalar_prefetch, grid=gs.grid, in_specs=list(gs.in_specs) + [ANY],
                    out_specs=gs.out_specs, scratch_shapes=gs.scratch_shapes)
            else:
                kw2["in_specs"] = list(kw2["in_specs"]) + [ANY]
            call = pl.pallas_call(ordered_body, **kw2)
            operands = operands + (tok,)
        out = call(*operands)
        _Order.last = out[token_index] if isinstance(out, (tuple, list)) else out
        return out

    return run


def _gelu(x):
    return x * (0.5 * (1.0 + jnp.tanh(_GELU_C * (x + _GELU_A * (x * x * x)))))


def _gelu_grad(x):
    x2 = x * x
    t = jnp.tanh(_GELU_C * (x + _GELU_A * (x * x2)))
    return 0.5 * (1.0 + t) + 0.5 * x * (1.0 - t * t) * (_GELU_C * (1.0 + 3.0 * _GELU_A * x2))


def _mean_last(x):
    return jnp.mean(x, axis=-1, keepdims=True)


def _sum_rows(x):
    return jnp.sum(x, axis=0, keepdims=True)


def _sum_all(x):
    return jnp.sum(jnp.sum(x, axis=1, keepdims=True), axis=0, keepdims=True)


def _matmul(a, b, *, mode, out_dtype, tm, tn, tk, name, res=None, a_parts=0, b_parts=0, out_parts=0):
    assert mode in ("nn", "nt"), mode
    if mode == "nn":
        assert not a_parts
        m, k = a.shape
        n = b.shape[0] * b.shape[2] if b_parts else b.shape[1]
    else:
        m, k = (a.shape[1], a.shape[0] * a.shape[2]) if a_parts else a.shape
        n = b.shape[1] if b_parts else b.shape[0]
    tm, tn, tk = min(tm, m), min(tn, n), min(tk, k)
    assert m % tm == 0 and n % tn == 0 and k % tk == 0, (name, m, n, k, tm, tn, tk)
    nm, nn, nk = m // tm, n // tn, k // tk

    def slab(idx, total_tiles, parts):
        per = total_tiles // parts
        assert per * parts == total_tiles, (name, total_tiles, parts)
        return idx // per, idx % per

    if mode == "nn":
        a_spec = pl.BlockSpec((tm, tk), lambda i, j, kk: (i, kk))
        if b_parts:
            b_spec = pl.BlockSpec((None, tk, tn), lambda i, j, kk: (slab(j, nn, b_parts)[0], kk, slab(j, nn, b_parts)[1]))
        else:
            b_spec = pl.BlockSpec((tk, tn), lambda i, j, kk: (kk, j))
        dims = (((1,), (0,)), ((), ()))
    else:
        if a_parts:
            a_spec = pl.BlockSpec((None, tm, tk), lambda i, j, kk: (slab(kk, nk, a_parts)[0], i, slab(kk, nk, a_parts)[1]))
        else:
            a_spec = pl.BlockSpec((tm, tk), lambda i, j, kk: (i, kk))
        if b_parts:
            b_spec = pl.BlockSpec((None, tn, tk), lambda i, j, kk: (slab(kk, nk, b_parts)[0], j, slab(kk, nk, b_parts)[1]))
        else:
            b_spec = pl.BlockSpec((tn, tk), lambda i, j, kk: (j, kk))
        dims = (((1,), (1,)), ((), ()))
    if out_parts:
        out_shape = jax.ShapeDtypeStruct((out_parts, m, n // out_parts), out_dtype)
        out_spec = pl.BlockSpec((None, tm, tn), lambda i, j, kk: (slab(j, nn, out_parts)[0], i, slab(j, nn, out_parts)[1]))
    else:
        out_shape = jax.ShapeDtypeStruct((m, n), out_dtype)
        out_spec = pl.BlockSpec((tm, tn), lambda i, j, kk: (i, j))
    in_specs = [a_spec, b_spec]
    operands = [a, b]
    if res is not None:
        in_specs.append(pl.BlockSpec((tm, tn), lambda i, j, kk: (i, j)))
        operands.append(res)

    def body(*refs):
        a_ref, b_ref = refs[0], refs[1]
        res_ref = refs[2] if res is not None else None
        o_ref = refs[3] if res is not None else refs[2]
        p = lax.dot_general(a_ref[...], b_ref[...], dims, preferred_element_type=F32)

        def finish(total):
            if res_ref is not None:
                total = res_ref[...] + total
            o_ref[...] = total.astype(out_dtype)

        if nk == 1:
            finish(p)
        else:
            acc_ref = refs[-1]
            kk = pl.program_id(2)

            @pl.when(kk == 0)
            def _():
                acc_ref[...] = p

            @pl.when(jnp.logical_and(kk > 0, kk < nk - 1))
            def _():
                acc_ref[...] += p

            @pl.when(kk == nk - 1)
            def _():
                finish(acc_ref[...] + p)

    scratch = [pltpu.VMEM((tm, tn), F32)] if nk > 1 else []
    return _ordered_call(
        body, name=name, out_shape=out_shape, grid=(nm, nn, nk), in_specs=in_specs, out_specs=out_spec,
        scratch_shapes=scratch, compiler_params=_params(("parallel", "parallel", "arbitrary")),
    )(*operands)


def _matmul_nt_slabs(a, b, *, tm, tn, name, a_parts=0):
    nslab, n, ks = b.shape
    m = a.shape[1] if a_parts else a.shape[0]
    tm, tn = min(tm, m), min(tn, n)
    assert m % tm == 0 and n % tn == 0, (name, m, n, tm, tn)
    if a_parts:
        per = nslab // a_parts
        assert per * a_parts == nslab and a.shape[2] == per * ks, (name, a.shape, b.shape)
        a_spec = pl.BlockSpec((a_parts, tm, per * ks), lambda i, j: (0, i, 0))
    else:
        assert a.shape[1] == nslab * ks, (name, a.shape, b.shape)
        a_spec = pl.BlockSpec((tm, nslab * ks), lambda i, j: (i, 0))

    def body(a_ref, b_ref, o_ref):
        total = None
        for sl in range(nslab):
            if a_parts:
                a_sl = a_ref[sl // per, :, (sl % per) * ks:(sl % per + 1) * ks]
            else:
                a_sl = a_ref[:, sl * ks:(sl + 1) * ks]
            p = lax.dot_general(a_sl, b_ref[sl], (((1,), (1,)), ((), ())), preferred_element_type=F32)
            total = p if total is None else total + p
        o_ref[...] = total.astype(BF16)

    return _ordered_call(
        body, name=name, out_shape=jax.ShapeDtypeStruct((m, n), BF16), grid=(m // tm, n // tn),
        in_specs=[a_spec, pl.BlockSpec((nslab, tn, ks), lambda i, j: (0, j, 0))],
        out_specs=pl.BlockSpec((tm, tn), lambda i, j: (i, j)), compiler_params=_params(("parallel", "parallel")),
    )(a, b)


GRAD_HALVES = {
    "w_in": ("rows_of_slab", 1024, 896), "w_up": ("rows_of_slab", 1024, 1408), "w_o": ("rows_of_block", 256, 2048),
    "w_down": ("cols_of_block", 1408, 512)}


def _half_shape(name, shard_shape):
    r, cols = shard_shape
    return (r, cols // 2) if GRAD_HALVES[name][0] == "cols_of_block" else (r // 2, cols)


def _grad_half(name, a, g, sel, res, call_name, g_parts=0):
    kind, tm, tn = GRAD_HALVES[name]
    s, m = a.shape
    n = g.shape[0] * g.shape[2] if g_parts else g.shape[1]
    if kind == "rows_of_slab":
        rh, hc = m // 2, n // N_CHIPS
        per = hc // tn
        grid = (rh // tm, n // tn)
        a_map = lambda i, j, sel_ref: (0, sel_ref[0] * (rh // tm) + i)
        g_col = lambda i, j, sel_ref: j
        o_map = lambda i, j, sel_ref: (j // per, i, j % per)
    elif kind == "rows_of_block":
        rh, hc = m // N_CHIPS // 2, n
        assert tm == rh
        grid = (N_CHIPS, n // tn)
        a_map = lambda i, j, sel_ref: (0, 2 * i + sel_ref[0])
        g_col = lambda i, j, sel_ref: j
        o_map = lambda i, j, sel_ref: (i, 0, j)
    else:
        rh, hc = m // N_CHIPS, n // 2
        assert tm == rh
        grid = (N_CHIPS, hc // tn)
        a_map = lambda i, j, sel_ref: (0, i)
        g_col = lambda i, j, sel_ref: sel_ref[0] * (hc // tn) + j
        o_map = lambda i, j, sel_ref: (i, 0, j)
    if g_parts:
        g_per = (n // tn) // g_parts
        g_spec = pl.BlockSpec((None, s, tn), lambda i, j, sel_ref: (g_col(i, j, sel_ref) // g_per, 0, g_col(i, j, sel_ref) % g_per))
    else:
        g_spec = pl.BlockSpec((s, tn), lambda i, j, sel_ref: (0, g_col(i, j, sel_ref)))
    o_spec = pl.BlockSpec((None, tm, tn), o_map)
    in_specs = [pl.BlockSpec((s, tm), a_map), g_spec] + ([o_spec] if res is not None else [])

    def body(sel_ref, a_ref, g_ref, *rest):
        o_ref = rest[-1]
        p = lax.dot_general(a_ref[...], g_ref[...], (((0,), (0,)), ((), ())), preferred_element_type=F32)
        if res is not None:
            p = p + rest[0][...].astype(F32)
        o_ref[...] = p.astype(BF16)

    grid_spec = pltpu.PrefetchScalarGridSpec(num_scalar_prefetch=1, grid=grid, in_specs=in_specs, out_specs=o_spec)
    return _ordered_call(
        body, name=call_name, out_shape=jax.ShapeDtypeStruct((N_CHIPS, rh, hc), BF16), grid_spec=grid_spec,
        compiler_params=_params(("parallel", "parallel")),
    )(sel, a, g, *([res] if res is not None else []))


def _row_tile(s):
    return min(256, s)


def _rows(width, tr):
    return pl.BlockSpec((tr, width), lambda i: (i, 0))


def _const2(shape):
    return pl.BlockSpec(shape, lambda i: (0, 0))


def _rms_fwd(x, g, name):
    s, d = x.shape
    tr = _row_tile(s)

    def body(x_ref, g_ref, o_ref):
        xv = x_ref[...]
        r = lax.rsqrt(_mean_last(xv * xv) + EPS)
        o_ref[...] = (xv * r * g_ref[...]).astype(BF16)

    return _ordered_call(
        body, name=name, out_shape=jax.ShapeDtypeStruct((s, d), BF16), grid=(s // tr,),
        in_specs=[_rows(d, tr), _const2((1, d))], out_specs=_rows(d, tr), compiler_params=_params(("parallel",)),
    )(x, g)


def _rms_bwd(x, g, dh, dres, name):
    s, d = x.shape
    tr = _row_tile(s)

    def body(x_ref, g_ref, dh_ref, dres_ref, dx_ref, dxb_ref, dg_ref):
        xv, dy = x_ref[...], dh_ref[...].astype(F32)
        r = lax.rsqrt(_mean_last(xv * xv) + EPS)
        gdy = dy * g_ref[...]
        dx = dres_ref[...] + r * gdy - xv * ((r * r * r) * _mean_last(xv * gdy))
        dx_ref[...] = dx
        dxb_ref[...] = dx.astype(BF16)

        @pl.when(pl.program_id(0) == 0)
        def _():
            dg_ref[...] = jnp.zeros_like(dg_ref)

        dg_ref[...] += _sum_rows(xv * r * dy)

    return _ordered_call(
        body, name=name,
        out_shape=(jax.ShapeDtypeStruct((s, d), F32), jax.ShapeDtypeStruct((s, d), BF16), jax.ShapeDtypeStruct((1, d), F32)),
        grid=(s // tr,), in_specs=[_rows(d, tr), _const2((1, d)), _rows(d, tr), _rows(d, tr)],
        out_specs=(_rows(d, tr), _rows(d, tr), _const2((1, d))), compiler_params=_params(("arbitrary",)),
    )(x, g, dh, dres)


Q0, K0, V0, GU0, GV0 = 0, ATTN_WIDTH, ATTN_WIDTH + KV_WIDTH, ATTN_WIDTH + 2 * KV_WIDTH, ATTN_WIDTH + 2 * KV_WIDTH + GMLP_WIDTH


def _head(h, base=0):
    return slice(base + h * HEAD_DIM, base + (h + 1) * HEAD_DIM)


def _proj_post(z, qg, kg, lg, lb, cosf, sinf, name):
    s = z.shape[0]
    tr = _row_tile(s)

    def body(z_ref, qg_ref, kg_ref, lg_ref, lb_ref, cos_ref, sin_ref, qn_ref, kn_ref, vb_ref, ug_ref, vn_ref,
             dgu_ref, dgv_ref, xhat_ref, rstd_ref):
        cos, sin = cos_ref[...], sin_ref[...]

        def norm_rope(xh, g):
            y = xh * lax.rsqrt(_mean_last(xh * xh) + EPS) * g
            return y * cos + pltpu.roll(y, HEAD_DIM // 2, 1) * sin

        for h in range(N_Q_HEADS):
            qn_ref[:, _head(h)] = norm_rope(z_ref[:, _head(h, Q0)].astype(F32), qg_ref[...]).astype(BF16)
        for h in range(N_KV_HEADS):
            kn_ref[:, _head(h)] = norm_rope(z_ref[:, _head(h, K0)].astype(F32), kg_ref[...]).astype(BF16)
        vb_ref[...] = z_ref[:, V0:GU0]
        gu = z_ref[:, GU0:GV0].astype(F32)
        ug_ref[...] = _gelu(gu)
        dgu_ref[...] = _gelu_grad(gu).astype(BF16)
        gv = z_ref[:, GV0:IN_WIDTH].astype(F32)
        vg = _gelu(gv)
        dgv_ref[...] = _gelu_grad(gv).astype(BF16)
        xc = vg - _mean_last(vg)
        r = lax.rsqrt(_mean_last(xc * xc) + EPS)
        y = xc * r
        xhat_ref[...] = y.astype(BF16)
        rstd_ref[...] = r
        vn_ref[...] = (y * lg_ref[...] + lb_ref[...]).astype(BF16)

    wide = jax.ShapeDtypeStruct((s, GMLP_WIDTH), BF16)
    return _ordered_call(
        body, name=name,
        out_shape=(jax.ShapeDtypeStruct((s, ATTN_WIDTH), BF16), jax.ShapeDtypeStruct((s, KV_WIDTH), BF16),
                   jax.ShapeDtypeStruct((s, KV_WIDTH), BF16), jax.ShapeDtypeStruct((s, GMLP_WIDTH), F32), wide,
                   wide, wide, wide, jax.ShapeDtypeStruct((s, 1), F32)),
        grid=(s // tr,),
        in_specs=[_rows(IN_WIDTH, tr), _const2((1, HEAD_DIM)), _const2((1, HEAD_DIM)), _const2((1, GMLP_WIDTH)),
                  _const2((1, GMLP_WIDTH)), _rows(HEAD_DIM, tr), _rows(HEAD_DIM, tr)],
        out_specs=(_rows(ATTN_WIDTH, tr), _rows(KV_WIDTH, tr), _rows(KV_WIDTH, tr), _rows(GMLP_WIDTH, tr), _rows(GMLP_WIDTH, tr),
                   _rows(GMLP_WIDTH, tr), _rows(GMLP_WIDTH, tr), _rows(GMLP_WIDTH, tr), _rows(1, tr)),
        compiler_params=_params(("parallel",)),
    )(z, qg, kg, lg, lb, cosf, sinf)


def _proj_post_bwd(z, dqn, dkn, dvb, dug, dvn, gelu_grad_u, gelu_grad_v, xhat_v, rstd_v, qg, kg, lg, cosf, sinf, name):
    s = z.shape[0]
    tr = _row_tile(s)

    def body(z_ref, dqn_ref, dkn_ref, dvb_ref, dug_ref, dvn_ref, ggu_ref, ggv_ref, xhat_ref, rstd_ref, qg_ref, kg_ref, lg_ref,
             cos_ref, sin_ref, dz_ref, dqg_ref, dkg_ref, dlg_ref, dlb_ref):
        cos, sin = cos_ref[...], sin_ref[...]

        @pl.when(pl.program_id(0) == 0)
        def _():
            dqg_ref[...] = jnp.zeros_like(dqg_ref)
            dkg_ref[...] = jnp.zeros_like(dkg_ref)
            dlg_ref[...] = jnp.zeros_like(dlg_ref)
            dlb_ref[...] = jnp.zeros_like(dlb_ref)

        def norm_rope_bwd(xh, g, dout):
            dy = dout * cos - pltpu.roll(dout, HEAD_DIM // 2, 1) * sin
            r = lax.rsqrt(_mean_last(xh * xh) + EPS)
            xhat = xh * r
            gdy = dy * g
            return r * (gdy - xhat * _mean_last(xhat * gdy)), _sum_rows(xhat * dy)

        dqg = jnp.zeros((1, HEAD_DIM), F32)
        for h in range(N_Q_HEADS):
            dx, dg = norm_rope_bwd(z_ref[:, _head(h, Q0)].astype(F32), qg_ref[...], dqn_ref[:, _head(h)])
            dz_ref[:, _head(h, Q0)] = dx.astype(BF16)
            dqg = dqg + dg
        dqg_ref[...] += dqg
        dkg = jnp.zeros((1, HEAD_DIM), F32)
        for h in range(N_KV_HEADS):
            dx, dg = norm_rope_bwd(z_ref[:, _head(h, K0)].astype(F32), kg_ref[...], dkn_ref[:, _head(h)])
            dz_ref[:, _head(h, K0)] = dx.astype(BF16)
            dkg = dkg + dg
        dkg_ref[...] += dkg
        dz_ref[:, V0:GU0] = dvb_ref[...].astype(BF16)
        dz_ref[:, GU0:GV0] = (dug_ref[...] * ggu_ref[...].astype(F32)).astype(BF16)
        xhat = xhat_ref[...].astype(F32)
        dvn_v = dvn_ref[...]
        dlg_ref[...] += _sum_rows(xhat * dvn_v)
        dlb_ref[...] += _sum_rows(dvn_v)
        dxh = dvn_v * lg_ref[...]
        dvg = rstd_ref[...] * (dxh - _mean_last(dxh) - xhat * _mean_last(dxh * xhat))
        dz_ref[:, GV0:IN_WIDTH] = (dvg * ggv_ref[...].astype(F32)).astype(BF16)

    return _ordered_call(
        body, name=name,
        out_shape=(jax.ShapeDtypeStruct((s, IN_WIDTH), BF16), jax.ShapeDtypeStruct((1, HEAD_DIM), F32),
                   jax.ShapeDtypeStruct((1, HEAD_DIM), F32), jax.ShapeDtypeStruct((1, GMLP_WIDTH), F32),
                   jax.ShapeDtypeStruct((1, GMLP_WIDTH), F32)),
        grid=(s // tr,),
        in_specs=[_rows(V0, tr), _rows(ATTN_WIDTH, tr), _rows(KV_WIDTH, tr), _rows(KV_WIDTH, tr), _rows(GMLP_WIDTH, tr),
                  _rows(GMLP_WIDTH, tr), _rows(GMLP_WIDTH, tr), _rows(GMLP_WIDTH, tr), _rows(GMLP_WIDTH, tr), _rows(1, tr),
                  _const2((1, HEAD_DIM)), _const2((1, HEAD_DIM)), _const2((1, GMLP_WIDTH)), _rows(HEAD_DIM, tr),
                  _rows(HEAD_DIM, tr)],
        out_specs=(_rows(IN_WIDTH, tr), _const2((1, HEAD_DIM)), _const2((1, HEAD_DIM)), _const2((1, GMLP_WIDTH)),
                   _const2((1, GMLP_WIDTH))),
        compiler_params=_params(("arbitrary",)),
    )(z, dqn, dkn, dvb, dug, dvn, gelu_grad_u, gelu_grad_v, xhat_v, rstd_v, qg, kg, lg, cosf, sinf)


def _band_valid(n, s):
    shape = (GQA_GROUP * BLOCK, 3 * BLOCK)
    i = lax.broadcasted_iota(jnp.int32, shape, 0) & (BLOCK - 1)
    j = lax.broadcasted_iota(jnp.int32, shape, 1)
    k_pos = n * BLOCK - BLOCK + j
    return (jnp.abs(j - BLOCK - i) <= BLOCK) & (k_pos >= 0) & (k_pos < s)


def _group_rows(x, kh):
    return jnp.concatenate([x[:, _head(kh * GQA_GROUP + g)] for g in range(GQA_GROUP)], axis=0)


def _group_sinks(sink_ref, kh):
    return jnp.concatenate([jnp.full((BLOCK, 1), sink_ref[kh * GQA_GROUP + g], F32) for g in range(GQA_GROUP)], axis=0)


def _rows_of(x, g):
    return x[g * BLOCK:(g + 1) * BLOCK]


def _probs(q, kb, sink_h, valid):
    sc = lax.dot_general(q, kb, (((1,), (1,)), ((), ())), preferred_element_type=F32) * (HEAD_DIM ** -0.5)
    sc = jnp.where(valid, sc, MASK_VALUE)
    m = jnp.maximum(jnp.max(sc, axis=-1, keepdims=True), sink_h)
    p = jnp.exp(sc - m)
    es = jnp.exp(sink_h - m)
    den = jnp.sum(p, axis=-1, keepdims=True) + es
    inv = 1.0 / den
    return p * inv, es * inv


def _band_specs(width, nb):
    return [pl.BlockSpec((BLOCK, width), lambda n: (jnp.maximum(n - 1, 0), 0)),
            pl.BlockSpec((BLOCK, width), lambda n: (n, 0)),
            pl.BlockSpec((BLOCK, width), lambda n: (jnp.minimum(n + 1, nb - 1), 0))]


def _blk(width):
    return pl.BlockSpec((BLOCK, width), lambda n: (n, 0))


def _whole3(shape):
    return pl.BlockSpec(shape, lambda n: (0, 0, 0))


def _smem():
    return pl.BlockSpec(memory_space=pltpu.SMEM)


def _mixer_fwd(qn, kn, vb, ug, vn, wsb, bsb, sink, ga, gs, name):
    s = qn.shape[0]
    nb = s // BLOCK

    def body(sink_ref, q_ref, kp_ref, kc_ref, kx_ref, vp_ref, vc_ref, vx_ref, ug_ref, vn_ref, ws_ref, bs_ref, ga_ref, gs_ref,
             attn_ref, sgu_ref, mix_ref, probs_ref, psink_ref):
        n = pl.program_id(0)
        valid = _band_valid(n, s)
        ssq = jnp.zeros((BLOCK, 1), F32)
        for kh in range(N_KV_HEADS):
            kb = jnp.concatenate([kp_ref[:, _head(kh)], kc_ref[:, _head(kh)], kx_ref[:, _head(kh)]], axis=0)
            vbd = jnp.concatenate([vp_ref[:, _head(kh)], vc_ref[:, _head(kh)], vx_ref[:, _head(kh)]], axis=0)
            p, p_sink = _probs(_group_rows(q_ref, kh), kb, _group_sinks(sink_ref, kh), valid)
            pb = p.astype(BF16)
            probs_ref[kh] = pb
            psink_ref[kh] = p_sink
            o4 = jnp.dot(pb, vbd, preferred_element_type=F32)
            for g in range(GQA_GROUP):
                o = _rows_of(o4, g)
                attn_ref[:, _head(kh * GQA_GROUP + g)] = o
                ssq = ssq + jnp.sum(o * o, axis=-1, keepdims=True)
        r = lax.rsqrt(ssq * (1.0 / ATTN_WIDTH) + EPS)
        mix_ref[:, 0:ATTN_WIDTH] = (attn_ref[...] * r * ga_ref[...]).astype(BF16)
        ssq = jnp.zeros((BLOCK, 1), F32)
        for h in range(N_GMLP_HEADS):
            f = jnp.dot(ws_ref[h], vn_ref[:, _head(h)], preferred_element_type=F32) + bs_ref[h]
            o = ug_ref[:, _head(h)] * f
            sgu_ref[:, _head(h)] = o
            ssq = ssq + jnp.sum(o * o, axis=-1, keepdims=True)
        r = lax.rsqrt(ssq * (1.0 / GMLP_WIDTH) + EPS)
        mix_ref[:, ATTN_WIDTH:D_MODEL] = (sgu_ref[...] * r * gs_ref[...]).astype(BF16)

    hh = (N_GMLP_HEADS, BLOCK, BLOCK)
    return _ordered_call(
        body, name=name,
        out_shape=(jax.ShapeDtypeStruct((s, ATTN_WIDTH), F32), jax.ShapeDtypeStruct((s, GMLP_WIDTH), F32),
                   jax.ShapeDtypeStruct((s, D_MODEL), BF16), jax.ShapeDtypeStruct((nb,) + PROBS_BLOCK, BF16),
                   jax.ShapeDtypeStruct((nb,) + PSINK_BLOCK, F32)),
        grid=(nb,),
        in_specs=[_smem(), _blk(ATTN_WIDTH)] + _band_specs(KV_WIDTH, nb) + _band_specs(KV_WIDTH, nb)
        + [_blk(GMLP_WIDTH), _blk(GMLP_WIDTH), _whole3(hh), _whole3(hh),
           pl.BlockSpec((1, ATTN_WIDTH), lambda n: (0, 0)), pl.BlockSpec((1, GMLP_WIDTH), lambda n: (0, 0))],
        out_specs=(_blk(ATTN_WIDTH), _blk(GMLP_WIDTH), _blk(D_MODEL), _per_block(PROBS_BLOCK), _per_block(PSINK_BLOCK)),
        compiler_params=_params(("parallel",)),
    )(sink, qn, kn, kn, kn, vb, vb, vb, ug, vn, wsb, bsb, ga, gs)


PROBS_BLOCK = (N_KV_HEADS, GQA_GROUP * BLOCK, 3 * BLOCK)
PSINK_BLOCK = (N_KV_HEADS, GQA_GROUP * BLOCK, 1)


def _per_block(shape):
    return pl.BlockSpec((None,) + shape, lambda n: (n, 0, 0, 0))


def _mixer_bwd(qn, kn, vb, ug, vn, attn, sgu, dmixed, wsb, bsb, ga, gs, probs, psink, name):
    s = qn.shape[0]
    nb = s // BLOCK
    tn_dims = (((0,), (0,)), ((), ()))
    nt_dims = (((1,), (1,)), ((), ()))

    def body(q_ref, kp_ref, kc_ref, kx_ref, vp_ref, vc_ref, vx_ref, ug_ref, vn_ref, attn_ref, sgu_ref, dm_ref,
             ws_ref, bs_ref, ga_ref, gs_ref, probs_ref, psink_ref,
             dq_ref, dk_ref, dv_ref, dug_ref, dvn_ref, dws_ref, dbs_ref, dsk_ref, dga_ref, dgs_ref, dk_acc, dv_acc):
        n = pl.program_id(0)

        @pl.when(n == 0)
        def _():
            for ref in (dk_acc, dv_acc, dws_ref, dbs_ref, dsk_ref, dga_ref, dgs_ref):
                ref[...] = jnp.zeros_like(ref)

        def out_norm_bwd(o, g, dy):
            r = lax.rsqrt(_mean_last(o * o) + EPS)
            gdy = dy * g
            return r * gdy - o * ((r * r * r) * _mean_last(o * gdy)), _sum_rows(o * r * dy)

        d_attn, dga = out_norm_bwd(attn_ref[...], ga_ref[...], dm_ref[:, 0:ATTN_WIDTH].astype(F32))
        dga_ref[...] += dga
        d_sgu, dgs = out_norm_bwd(sgu_ref[...], gs_ref[...], dm_ref[:, ATTN_WIDTH:D_MODEL].astype(F32))
        dgs_ref[...] += dgs

        for h in range(N_GMLP_HEADS):
            vn_h = vn_ref[:, _head(h)]
            f = jnp.dot(ws_ref[h], vn_h, preferred_element_type=F32) + bs_ref[h]
            ds_h = d_sgu[:, _head(h)]
            dug_ref[:, _head(h)] = ds_h * f
            df = ds_h * ug_ref[:, _head(h)]
            dfb = df.astype(BF16)
            dvn_ref[:, _head(h)] = lax.dot_general(ws_ref[h], dfb, tn_dims, preferred_element_type=F32)
            dws_ref[h] += lax.dot_general(dfb, vn_h, nt_dims, preferred_element_type=F32)
            dbs_ref[h] += jnp.broadcast_to(jnp.sum(df, axis=-1, keepdims=True), (BLOCK, BLOCK))

        row0 = pl.multiple_of(n * BLOCK, BLOCK)
        for kh in range(N_KV_HEADS):
            kb = jnp.concatenate([kp_ref[:, _head(kh)], kc_ref[:, _head(kh)], kx_ref[:, _head(kh)]], axis=0)
            vbd = jnp.concatenate([vp_ref[:, _head(kh)], vc_ref[:, _head(kh)], vx_ref[:, _head(kh)]], axis=0)
            q4 = _group_rows(q_ref, kh)
            pb = probs_ref[kh]
            p = pb.astype(F32)
            do4 = _group_rows(d_attn, kh).astype(BF16)
            dp = lax.dot_general(do4, vbd, nt_dims, preferred_element_type=F32)
            delta = jnp.sum(p * dp, axis=-1, keepdims=True)
            dsc = (p * (dp - delta) * (HEAD_DIM ** -0.5)).astype(BF16)
            d_sink = -(psink_ref[kh] * delta)
            dq4 = jnp.dot(dsc, kb, preferred_element_type=F32)
            for g in range(GQA_GROUP):
                h = kh * GQA_GROUP + g
                dsk_ref[h:h + 1, :] += jnp.broadcast_to(_sum_all(_rows_of(d_sink, g)), (1, BLOCK))
                dq_ref[:, _head(h)] = _rows_of(dq4, g)
            dk_acc[pl.ds(row0, 3 * BLOCK), _head(kh)] += lax.dot_general(dsc, q4, tn_dims, preferred_element_type=F32)
            dv_acc[pl.ds(row0, 3 * BLOCK), _head(kh)] += lax.dot_general(pb, do4, tn_dims, preferred_element_type=F32)

        @pl.when(n == nb - 1)
        def _():
            dk_ref[...] = dk_acc[BLOCK:BLOCK + s, :]
            dv_ref[...] = dv_acc[BLOCK:BLOCK + s, :]

    hh = (N_GMLP_HEADS, BLOCK, BLOCK)
    full_kv = pl.BlockSpec((s, KV_WIDTH), lambda n: (0, 0))
    return _ordered_call(
        body, name=name,
        out_shape=(jax.ShapeDtypeStruct((s, ATTN_WIDTH), F32), jax.ShapeDtypeStruct((s, KV_WIDTH), F32),
                   jax.ShapeDtypeStruct((s, KV_WIDTH), F32), jax.ShapeDtypeStruct((s, GMLP_WIDTH), F32),
                   jax.ShapeDtypeStruct((s, GMLP_WIDTH), F32), jax.ShapeDtypeStruct(hh, F32), jax.ShapeDtypeStruct(hh, F32),
                   jax.ShapeDtypeStruct((N_Q_HEADS, BLOCK), F32), jax.ShapeDtypeStruct((1, ATTN_WIDTH), F32),
                   jax.ShapeDtypeStruct((1, GMLP_WIDTH), F32)),
        grid=(nb,),
        in_specs=[_blk(ATTN_WIDTH)] + _band_specs(KV_WIDTH, nb) + _band_specs(KV_WIDTH, nb)
        + [_blk(GMLP_WIDTH), _blk(GMLP_WIDTH), _blk(ATTN_WIDTH), _blk(GMLP_WIDTH), _blk(D_MODEL), _whole3(hh), _whole3(hh),
           pl.BlockSpec((1, ATTN_WIDTH), lambda n: (0, 0)), pl.BlockSpec((1, GMLP_WIDTH), lambda n: (0, 0)),
           _per_block(PROBS_BLOCK), _per_block(PSINK_BLOCK)],
        out_specs=(_blk(ATTN_WIDTH), full_kv, full_kv, _blk(GMLP_WIDTH), _blk(GMLP_WIDTH), _whole3(hh), _whole3(hh),
                   pl.BlockSpec((N_Q_HEADS, BLOCK), lambda n: (0, 0)), pl.BlockSpec((1, ATTN_WIDTH), lambda n: (0, 0)),
                   pl.BlockSpec((1, GMLP_WIDTH), lambda n: (0, 0))),
        scratch_shapes=[pltpu.VMEM((s + 2 * BLOCK, KV_WIDTH), F32), pltpu.VMEM((s + 2 * BLOCK, KV_WIDTH), F32)],
        compiler_params=_params(("arbitrary",)),
    )(qn, kn, kn, kn, vb, vb, vb, ug, vn, attn, sgu, dmixed, wsb, bsb, ga, gs, probs, psink)


CONV_TILE = 256


PAD_ROWS = 8


def _zero_pad_rows(pad_ref):
    s = pad_ref.shape[0] - 2 * PAD_ROWS
    zeros = jnp.zeros((PAD_ROWS, pad_ref.shape[1]), F32)
    pad_ref[0:PAD_ROWS, :] = zeros
    pad_ref[PAD_ROWS + s:2 * PAD_ROWS + s, :] = zeros


def _shift_rows(a, pad_ref):
    s = a.shape[0]
    pad_ref[PAD_ROWS:PAD_ROWS + s, :] = a
    padded = pad_ref[...]
    prev = pltpu.roll(padded, 1, 0)[PAD_ROWS:PAD_ROWS + s]
    nxt = pltpu.roll(padded, s + 2 * PAD_ROWS - 1, 0)[PAD_ROWS:PAD_ROWS + s]
    return prev, nxt


def _conv_specs(s):
    tc = CONV_TILE
    nj = D_FF // tc
    return (tc, nj, pl.BlockSpec((2, s, tc), lambda j: (0, 0, j)),
            [pl.BlockSpec((3, tc), lambda j: (0, j)), pl.BlockSpec((3, tc), lambda j: (0, j + nj))],
            [pl.BlockSpec((1, tc), lambda j: (0, j)), pl.BlockSpec((1, tc), lambda j: (0, j + nj))])


def _conv_gate_fwd(a_pre, cw, cb, name):
    s = a_pre.shape[1]
    tc, nj, a_spec, w_specs, b_specs = _conv_specs(s)

    def body(a_ref, wg_ref, wu_ref, bg_ref, bu_ref, act_ref, dgu_ref, pad_ref):
        _zero_pad_rows(pad_ref)

        def conv(a, w_ref, b_ref):
            prev, nxt = _shift_rows(a, pad_ref)
            return b_ref[...] + prev * w_ref[0:1, :] + a * w_ref[1:2, :] + nxt * w_ref[2:3, :]

        g = conv(a_ref[0].astype(F32), wg_ref, bg_ref)
        u = conv(a_ref[1].astype(F32), wu_ref, bu_ref)
        sg = 1.0 / (1.0 + jnp.exp(-g))
        silu = g * sg
        act_ref[...] = (silu * u).astype(BF16)
        dgu_ref[0] = (u * (sg * (1.0 + g * (1.0 - sg)))).astype(BF16)
        dgu_ref[1] = silu.astype(BF16)

    return _ordered_call(
        body, name=name, out_shape=(jax.ShapeDtypeStruct((s, D_FF), BF16), jax.ShapeDtypeStruct((2, s, D_FF), BF16)),
        grid=(nj,), in_specs=[a_spec] + w_specs + b_specs,
        out_specs=(pl.BlockSpec((s, tc), lambda j: (0, j)), pl.BlockSpec((2, s, tc), lambda j: (0, 0, j))),
        scratch_shapes=[pltpu.VMEM((s + 2 * PAD_ROWS, tc), F32)], compiler_params=_params(("parallel",)),
    )(a_pre, cw, cw, cb, cb)


def _conv_gate_bwd(a_pre, dgu, cw, dact, name):
    s = a_pre.shape[1]
    tc, nj, a_spec, w_specs, _ = _conv_specs(s)

    def body(a_ref, dgu_ref, wg_ref, wu_ref, dact_ref, dap_ref, dcw_ref, dcb_ref, pad_ref):
        _zero_pad_rows(pad_ref)
        dact_v = dact_ref[...].astype(F32)
        for part, w_ref in enumerate((wg_ref, wu_ref)):
            da = dact_v * dgu_ref[part].astype(F32)
            a = a_ref[part].astype(F32)
            da_prev, da_next = _shift_rows(da, pad_ref)
            dcw_ref[part, 0:1, :] = _sum_rows(a * da_next)
            dcw_ref[part, 1:2, :] = _sum_rows(a * da)
            dcw_ref[part, 2:3, :] = _sum_rows(a * da_prev)
            dcb_ref[part] = _sum_rows(da)
            dap_ref[part] = (da_next * w_ref[0:1, :] + da * w_ref[1:2, :] + da_prev * w_ref[2:3, :]).astype(BF16)

    return _ordered_call(
        body, name=name,
        out_shape=(jax.ShapeDtypeStruct((2, s, D_FF), BF16), jax.ShapeDtypeStruct((2, 3, D_FF), F32),
                   jax.ShapeDtypeStruct((2, 1, D_FF), F32)),
        grid=(nj,),
        in_specs=[a_spec, pl.BlockSpec((2, s, tc), lambda j: (0, 0, j))] + w_specs + [pl.BlockSpec((s, tc), lambda j: (0, j))],
        out_specs=(pl.BlockSpec((2, s, tc), lambda j: (0, 0, j)), pl.BlockSpec((2, 3, tc), lambda j: (0, 0, j)),
                   pl.BlockSpec((2, 1, tc), lambda j: (0, 0, j))),
        scratch_shapes=[pltpu.VMEM((s + 2 * PAD_ROWS, tc), F32)], compiler_params=_params(("parallel",)),
    )(a_pre, dgu, cw, cw, dact)


def _loss_head(y, target, name):
    s, d = y.shape
    tr = _row_tile(s)

    def body(y_ref, t_ref, loss_ref, dy_ref, dyb_ref):
        err = y_ref[...] - t_ref[...]

        @pl.when(pl.program_id(0) == 0)
        def _():
            loss_ref[...] = jnp.zeros_like(loss_ref)

        loss_ref[...] += jnp.broadcast_to(0.5 * _sum_all(_mean_last(err * err)), (8, 128))
        dy = err * (1.0 / d)
        dy_ref[...] = dy
        dyb_ref[...] = dy.astype(BF16)

    return _ordered_call(
        body, name=name,
        out_shape=(jax.ShapeDtypeStruct((8, 128), F32), jax.ShapeDtypeStruct((s, d), F32), jax.ShapeDtypeStruct((s, d), BF16)),
        grid=(s // tr,), in_specs=[_rows(d, tr), _rows(d, tr)],
        out_specs=(_const2((8, 128)), _rows(d, tr), _rows(d, tr)), compiler_params=_params(("arbitrary",)),
    )(y, target)


def _row_block(rows, cols, budget=1 << 20):
    if rows * cols <= budget:
        return rows
    best = None
    for tr in range(16, rows, 16):
        if rows % tr == 0 and tr * cols <= budget:
            best = tr
    assert best is not None, (rows, cols)
    return best


def _place_shard(x4, layer, j_arr, out_dtype, name):
    _, nh, r, cols = x4.shape
    tr = _row_block(r, cols)

    def body(j_ref, x_ref, o_ref):
        o_ref[...] = x_ref[...].astype(out_dtype)

    grid_spec = pltpu.PrefetchScalarGridSpec(
        num_scalar_prefetch=1, grid=(nh, r // tr),
        in_specs=[pl.BlockSpec((None, None, tr, cols), lambda h, i, j_ref: (layer, h, i, 0))],
        out_specs=pl.BlockSpec((None, None, tr, cols), lambda h, i, j_ref: (j_ref[0], h, i, 0)))
    return _ordered_call(
        body, name=name, out_shape=jax.ShapeDtypeStruct((N_CHIPS, nh, r, cols), out_dtype), grid_spec=grid_spec,
        compiler_params=_params(("parallel", "parallel")),
    )(j_arr, x4)


def _adamw(w, g, m, v, name):
    rows, cols = w.shape
    tr = _row_block(rows, cols, 1 << 18)

    def body(w_ref, g_ref, m_ref, v_ref, go_ref, d_ref, nm_ref, nv_ref):
        gv = g_ref[...]
        go_ref[...] = gv
        mn = ADAM_B1 * m_ref[...] + (1.0 - ADAM_B1) * gv
        vn = ADAM_B2 * v_ref[...] + (1.0 - ADAM_B2) * (gv * gv)
        m_hat = mn / (1.0 - ADAM_B1 ** ADAM_STEP)
        v_hat = vn / (1.0 - ADAM_B2 ** ADAM_STEP)
        d_ref[...] = -ADAM_LR * (m_hat / (jnp.sqrt(v_hat) + ADAM_EPS) + ADAM_WD * w_ref[...])
        nm_ref[...] = mn
        nv_ref[...] = vn

    sds = jax.ShapeDtypeStruct((rows, cols), F32)
    return _ordered_call(
        body, name=name, out_shape=(sds, sds, sds, sds), grid=(rows // tr,),
        in_specs=[_rows(cols, tr)] * 4, out_specs=(_rows(cols, tr),) * 4, compiler_params=_params(("parallel",)),
    )(w, g, m, v)


def _chip_sum(p4, recv3, j_arr, c_arr, name):
    _, rh, cols = p4.shape
    tr = _row_block(rh, cols, 1 << 19)

    def body(j_ref, c_ref, p_ref, r_ref, o_ref):
        total = p_ref[...].astype(F32)
        for peer in range(3):
            total = total + r_ref[peer].astype(F32)
        o_ref[...] = total.astype(BF16)

    grid_spec = pltpu.PrefetchScalarGridSpec(
        num_scalar_prefetch=2, grid=(rh // tr,),
        in_specs=[pl.BlockSpec((None, tr, cols), lambda i, j_ref, c_ref: (j_ref[0], i, 0)),
                  pl.BlockSpec((3, tr, cols), lambda i, j_ref, c_ref: (0, i, 0))],
        out_specs=pl.BlockSpec((None, tr, cols), lambda i, j_ref, c_ref: (c_ref[0], i, 0)))
    return _ordered_call(
        body, name=name, out_shape=jax.ShapeDtypeStruct((2, rh, cols), BF16), grid_spec=grid_spec,
        compiler_params=_params(("parallel",)),
    )(j_arr, c_arr, p4, recv3)


def _adamw_layer(w, g, m, v, layer, into, name):
    nl, rows, cols = w.shape
    slabs, _, width = g.shape
    assert slabs * width == cols and g.shape[1] == rows, (name, w.shape, g.shape)
    tr = _row_block(rows, width, 1 << 19)
    at_layer = pl.BlockSpec((None, tr, width), lambda h, i: (layer, i, h))

    def body(w_ref, g_ref, m_ref, v_ref, *rest):
        go_ref, d_ref, nm_ref, nv_ref = rest[-4:]
        gv = g_ref[...].astype(F32)
        go_ref[...] = gv
        mn = ADAM_B1 * m_ref[...] + (1.0 - ADAM_B1) * gv
        vn = ADAM_B2 * v_ref[...] + (1.0 - ADAM_B2) * (gv * gv)
        m_hat = mn / (1.0 - ADAM_B1 ** ADAM_STEP)
        v_hat = vn / (1.0 - ADAM_B2 ** ADAM_STEP)
        d_ref[...] = -ADAM_LR * (m_hat / (jnp.sqrt(v_hat) + ADAM_EPS) + ADAM_WD * w_ref[...])
        nm_ref[...] = mn
        nv_ref[...] = vn

    in_specs = [at_layer, pl.BlockSpec((None, tr, width), lambda h, i: (h, i, 0)), at_layer, at_layer]
    operands = [w, g, m, v]
    aliases = {}
    if into is not None:
        in_specs += [ANY] * 4
        operands += list(into)
        aliases = {4 + i: i for i in range(4)}
    sds = jax.ShapeDtypeStruct((nl, rows, cols), F32)
    return _ordered_call(
        body, name=name, out_shape=(sds,) * 4, grid=(slabs, rows // tr), in_specs=in_specs, out_specs=(at_layer,) * 4,
        input_output_aliases=aliases, compiler_params=_params(("parallel", "parallel")),
    )(*operands)


def _sum_devices(mine, landed, me_arr, name):
    rows, lanes = mine.shape

    def body(me_ref, mine_ref, landed_ref, o_ref):
        total = None
        for dev in range(8):
            part = jnp.where(me_ref[0] == dev, mine_ref[...], landed_ref[dev])
            total = part if total is None else total + part
        o_ref[...] = total

    grid_spec = pltpu.PrefetchScalarGridSpec(
        num_scalar_prefetch=1, grid=(1,),
        in_specs=[pl.BlockSpec((rows, lanes), lambda i, me_ref: (0, 0)), pl.BlockSpec((8, rows, lanes), lambda i, me_ref: (0, 0, 0))],
        out_specs=pl.BlockSpec((rows, lanes), lambda i, me_ref: (0, 0)))
    return _ordered_call(
        body, name=name, out_shape=jax.ShapeDtypeStruct((rows, lanes), F32), grid_spec=grid_spec,
        compiler_params=_params(("arbitrary",)),
    )(me_arr, mine, landed)


def _place():
    x, y, c = lax.axis_index("x"), lax.axis_index("y"), lax.axis_index("c")
    chips = [(1 - x, y), (x, 1 - y), (1 - x, 1 - y)]
    return x, y, c, chips


HBM = pl.BlockSpec(memory_space=pltpu.HBM)
SEM = pl.BlockSpec(memory_space=pltpu.SEMAPHORE)
TOKEN = jax.ShapeDtypeStruct((8, 128), F32)


def _remote(src, dst, send_sem, recv_sem, to):
    return pltpu.make_async_remote_copy(src_ref=src, dst_ref=dst, send_sem=send_sem, recv_sem=recv_sem, device_id=to,
                                        device_id_type=MESH)


def _split_call(body, name, thru, sems_in=(), fresh=(), new_sems=(), after_last=True):
    n_t, n_s, n_f = len(thru), len(sems_in), len(fresh)

    def call_body(*refs):
        outs = refs[n_t + n_s:]
        body(refs[:n_t], refs[n_t:n_t + n_s], outs[1 + n_t:1 + n_t + n_f], outs[1 + n_t + n_f:])
        outs[0][...] = jnp.zeros_like(outs[0])

    out_shape = ([TOKEN] + [pltpu.HBM(t.shape, t.dtype) for t in thru] + [pltpu.HBM(shp, dt) for shp, dt in fresh]
                 + [pltpu.SemaphoreType.DMA(shp) for shp in new_sems])
    out_specs = [pl.BlockSpec(memory_space=pltpu.VMEM)] + [HBM] * (n_t + n_f) + [SEM] * len(new_sems)
    if not after_last or any(t is _Order.last for t in thru):
        _Order.last = None
    out = _ordered_call(
        call_body, name=name, out_shape=tuple(out_shape), in_specs=[HBM] * n_t + [SEM] * n_s, out_specs=tuple(out_specs),
        input_output_aliases={i: 1 + i for i in range(n_t)},
        compiler_params=pltpu.CompilerParams(has_side_effects=pltpu.SideEffectType.DATAFLOW_SIDE_EFFECTING),
    )(*[pltpu.with_memory_space_constraint(t, pltpu.HBM) for t in thru], *sems_in)
    return out[1:1 + n_t], out[1 + n_t:1 + n_t + n_f], out[1 + n_t + n_f:]


class _Exchange:
    def __init__(self, weights, m_in, v_in, j_arr, c_arr, me_arr):
        self.w, self.m, self.v = weights, m_in, v_in
        self.j_arr, self.c_arr, self.me_arr = j_arr, c_arr, me_arr
        self.adam, self.small, self.pairs, self.held = {}, {}, {}, None
        self.o_arr = 1 - c_arr
        self.groups = [(l, name) for l in range(DEPTH) for name in BIG_NAMES]
        self.shard_shape = {name: weights[name].shape[1:] for name in BIG_NAMES}
        self.conv_state, self.state = [], {}
        self.ready, self.conv_ready = {}, {}
        self.pending, self.tick, self.reduced = [], 0, {}

        def place(grp):
            l, name = grp
            nl, r, cols = weights[name].shape
            return _place_shard(weights[name].reshape(nl, 2, r // 2, cols), l, j_arr, BF16, f"place_{name}_l{l}")

        def start_copies(tag, convs, groups, bufs):
            n_c = len(convs)

            def start(thru, _, __, sems):
                x, y, c, chips = _place()
                j_me = 2 * x + y
                copies = []
                for i in range(len(thru)):
                    mine = thru[i].at[j_me] if i < n_c else thru[i].at[j_me, c]
                    copies += [_remote(mine, mine, sems[2 * i].at[k], sems[2 * i + 1].at[k], (*chip, c))
                               for k, chip in enumerate(chips)]
                for cp in copies:
                    cp.start()

            thru, _, sems = _split_call(start, tag, convs + bufs, new_sems=[(3,)] * (2 * (n_c + len(bufs))))
            self.conv_state += [(thru[i], sems[2 * i], sems[2 * i + 1]) for i in range(n_c)]
            for g, grp in enumerate(groups):
                self.state[grp] = (thru[n_c + g], sems[2 * (n_c + g)], sems[2 * (n_c + g) + 1])

        convs = [_place_shard(weights["conv_w"][:, None], l, j_arr, F32, f"place_conv_w_l{l}") for l in range(DEPTH)]
        start_copies("gather_start_first", convs, self.groups[:1], [place(self.groups[0])])
        start_copies("gather_start_rest", [], self.groups[1:], [place(grp) for grp in self.groups[1:]])

    def conv_w(self, l):
        if l not in self.conv_ready:
            buf, send, recv = self.conv_state[l]

            def wait(thru, sems, _, __):
                x, y, c, chips = _place()
                for k, chip in enumerate(chips):
                    mine, theirs = thru[0].at[2 * x + y], thru[0].at[2 * chip[0] + chip[1]]
                    _remote(mine, mine, sems[0].at[k], sems[1].at[k], (*chip, c)).wait_send()
                    _remote(theirs, theirs, sems[0].at[k], sems[1].at[k], (x, y, c)).wait_recv()

            (buf,), _, _ = _split_call(wait, f"gather_conv_w_l{l}", [buf], sems_in=[send, recv])
            self.conv_ready[l] = jnp.transpose(buf[:, 0], (1, 0, 2)).reshape(3, 2 * D_FF)
        return self.conv_ready[l]

    def weight(self, l, name):
        grp = (l, name)
        if grp not in self.ready:
            buf, send, recv = self.state[grp]

            def forward(thru, sems, _, new):
                x, y, c, chips = _place()
                for k, chip in enumerate(chips):
                    landed = thru[0].at[2 * chip[0] + chip[1], c]
                    _remote(landed, landed, new[0].at[k], sems[0].at[k], (x, y, c)).wait_recv()
                    _remote(landed, landed, new[0].at[k], new[1].at[k], (x, y, 1 - c)).start()

            (buf,), _, (fsend, frecv) = _split_call(forward, f"gather_pass_{name}_l{l}", [buf], sems_in=[recv],
                                                    new_sems=[(3,), (3,)])

            def finish(thru, sems, _, __):
                x, y, c, chips = _place()
                mine = thru[0].at[2 * x + y, c]
                for k, chip in enumerate(chips):
                    j_k = 2 * chip[0] + chip[1]
                    theirs, landed = thru[0].at[j_k, 1 - c], thru[0].at[j_k, c]
                    _remote(theirs, theirs, sems[1].at[k], sems[2].at[k], (x, y, c)).wait_recv()
                    _remote(landed, landed, sems[1].at[k], sems[2].at[k], (x, y, 1 - c)).wait_send()
                    _remote(mine, mine, sems[0].at[k], sems[2].at[k], (*chip, c)).wait_send()

            (buf,), _, _ = _split_call(finish, f"gather_done_{name}_l{l}", [buf], sems_in=[send, fsend, frecv])
            r, cols = self.shard_shape[name]
            self.ready[grp] = buf.reshape(N_CHIPS, r, cols) if name in ("w_in", "w_up") else buf.reshape(N_CHIPS * r, cols)
        return self.ready[grp]

    def pair_send(self, l, name, other):
        held = self.held
        self.held = None

        def start(thru, _, fresh, sems):
            x, y, c, chips = _place()
            copies = [_remote(thru[0], fresh[0], sems[0], sems[1], (x, y, 1 - c))]
            if held is not None:
                copies += [_remote(thru[1].at[2 * chip[0] + chip[1]], fresh[1].at[k], sems[2].at[k], sems[3].at[k], (*chip, c))
                           for k, chip in enumerate(chips)]
            for cp in copies:
                cp.start()

        thru, fresh, new_sems = [other], [(other.shape, BF16)], [(), ()]
        if held is not None:
            thru, fresh, new_sems = thru + [held[2]], fresh + [((3,) + held[2].shape[1:], BF16)], new_sems + [(3,), (3,)]
        thru, fresh, sems = _split_call(start, f"pair_start_{name}_l{l}", thru, fresh=fresh, new_sems=new_sems, after_last=False)
        self.pairs[(l, name)] = (thru[0], fresh[0], sems[:2])
        if held is not None:
            self.pending.append(dict(l=held[0], name=held[1], stage=2, at=self.tick, bufs=(thru[1], fresh[1]), sems=sems[2:]))

    def pair_recv(self, l, name):
        other, recv, sems = self.pairs.pop((l, name))

        def wait(thru, sems, _, __):
            x, y, c, _chips = _place()
            cp = _remote(thru[0], thru[1], sems[0], sems[1], (x, y, 1 - c))
            cp.wait_send()
            cp.wait_recv()

        (_, recv), _, _ = _split_call(wait, f"pair_done_{name}_l{l}", [other, recv], sems_in=list(sems))
        return recv

    def scatter(self, l, name, p4):
        assert self.held is None
        self.held = (l, name, p4)
        if (l, name) == (0, BIG_NAMES[0]):
            self._scatter_held()

    def _scatter_held(self):
        l, name, p4 = self.held
        self.held = None

        def start(thru, _, fresh, sems):
            x, y, c, chips = _place()
            for k, chip in enumerate(chips):
                _remote(thru[0].at[2 * chip[0] + chip[1]], fresh[0].at[k], sems[0].at[k], sems[1].at[k], (*chip, c)).start()

        (p4,), (recv3,), sems = _split_call(start, f"chips_start_{name}_l{l}", [p4], fresh=[((3,) + p4.shape[1:], BF16)],
                                           new_sems=[(3,), (3,)], after_last=False)
        self.pending.append(dict(l=l, name=name, stage=2, at=self.tick, bufs=(p4, recv3), sems=sems))

    def point(self, drain=False):
        self.tick += 1
        if drain:
            old = [g for g in self.pending if g["stage"] == 2 and g["at"] + 2 <= self.tick]
            new = [g for g in self.pending if g["stage"] == 2 and g["at"] + 2 > self.tick]
            for grp in old + [g for g in self.pending if g["stage"] == 3] + new:
                self._advance([grp] if grp["stage"] == 3 else [], [grp] if grp["stage"] == 2 else [])
        else:
            self._advance([grp for grp in self.pending if grp["stage"] == 3 and grp["at"] < self.tick],
                          [grp for grp in self.pending if grp["stage"] == 2 and grp["at"] + 2 <= self.tick])

    def _advance(self, joined, landed):
        if not joined and not landed:
            return
        n_j, n_l = len(joined), len(landed)

        def wait(thru, sems, _, __):
            x, y, c, chips = _place()
            for i in range(n_j):
                buf, send, recv = thru[i], sems[2 * i], sems[2 * i + 1]
                _remote(buf.at[c], buf.at[c], send, recv, (x, y, 1 - c)).wait_send()
                _remote(buf.at[1 - c], buf.at[1 - c], send, recv, (x, y, c)).wait_recv()
            for i in range(n_l):
                p4, recv3 = thru[n_j + 2 * i], thru[n_j + 2 * i + 1]
                send, recv = sems[2 * (n_j + i)], sems[2 * (n_j + i) + 1]
                for k, chip in enumerate(chips):
                    cp = _remote(p4.at[2 * chip[0] + chip[1]], recv3.at[k], send.at[k], recv.at[k], (*chip, c))
                    cp.wait_send()
                    cp.wait_recv()

        tag = "_".join([f"{grp['name']}{grp['l']}_halves" for grp in joined] + [f"{grp['name']}{grp['l']}_chips" for grp in landed])
        bufs, _, _ = _split_call(wait, f"landed_{tag}", [b for grp in joined + landed for b in grp["bufs"]],
                                 sems_in=[sm for grp in joined + landed for sm in grp["sems"]])
        for i, grp in enumerate(joined):
            l, name, full = grp["l"], grp["name"], bufs[i]
            if GRAD_HALVES[name][0] != "cols_of_block":
                full = full.reshape((1,) + tuple(self.shard_shape[name]))
            self.adam[name] = _adamw_layer(self.w[name], full, self.m[name], self.v[name], l, self.adam.get(name),
                                           f"adamw_{name}_l{l}")
            grp.update(stage=4)
        if not landed:
            return
        halves = [_chip_sum(bufs[n_j + 2 * i], bufs[n_j + 2 * i + 1], self.j_arr, self.c_arr,
                            f"chip_sum_{grp['name']}_l{grp['l']}") for i, grp in enumerate(landed)]

        def start(thru, _, __, sems):
            x, y, c, _chips = _place()
            for i in range(n_l):
                _remote(thru[i].at[c], thru[i].at[c], sems[2 * i], sems[2 * i + 1], (x, y, 1 - c)).start()

        tag = "_".join(f"{grp['name']}{grp['l']}" for grp in landed)
        halves, _, sems = _split_call(start, f"join_start_{tag}", halves, new_sems=[()] * (2 * n_l), after_last=False)
        for i, grp in enumerate(landed):
            grp.update(stage=3, at=self.tick, bufs=(halves[i],), sems=tuple(sems[2 * i:2 * i + 2]))

    def finish(self):
        if self.held is not None:
            self._scatter_held()
        while any(grp["stage"] < 4 for grp in self.pending):
            self.point(drain=True)
        return self.adam

    @staticmethod
    def _peer(k, x, y, c):
        return (1 - x if k & 4 else x, 1 - y if k & 2 else y, 1 - c if k & 1 else c)

    def small_grads(self, l, grads, loss_tile):
        parts = [grads[nm] for nm in SMALL_NAMES] + ([loss_tile[0, 0:1]] if loss_tile is not None else [])
        packed = _pack_call(parts, f"small_pack_l{l}")
        rows = packed.shape[0]

        def start(thru, _, fresh, sems):
            x, y, c, _chips = _place()
            for k in range(1, 8):
                _remote(thru[0], fresh[0].at[4 * x + 2 * y + c], sems[0].at[k - 1], sems[1].at[k - 1],
                        self._peer(k, x, y, c)).start()

        (packed,), (landed,), sems = _split_call(start, f"small_start_l{l}", [packed], fresh=[((8, rows, PACK_LANES), F32)],
                                                 new_sems=[(7,), (7,)], after_last=False)
        self.small[l] =(packed, landed, sems, [p.shape for p in parts])

    def small_sum(self, l):
        packed, landed, sems, _shapes = self.small[l]

        def wait(thru, sems, _, __):
            x, y, c, _chips = _place()
            for k in range(1, 8):
                px, py, pc = self._peer(k, x, y, c)
                _remote(thru[0], thru[1].at[4 * x + 2 * y + c], sems[0].at[k - 1], sems[1].at[k - 1], (px, py, pc)).wait_send()
                _remote(thru[0], thru[1].at[4 * px + 2 * py + pc], sems[0].at[k - 1], sems[1].at[k - 1], (x, y, c)).wait_recv()

        (packed, landed), _, _ = _split_call(wait, f"small_done_l{l}", [packed, landed], sems_in=list(sems))
        return _sum_devices(packed, landed, self.me_arr, f"small_sum_l{l}")


def _rope_tables(s):
    inv_freq = ROPE_THETA ** (-jnp.arange(0, HEAD_DIM, 2, dtype=F32) / HEAD_DIM)
    ang = jnp.arange(s, dtype=F32)[:, None] * inv_freq[None, :]
    cos, sin = jnp.cos(ang), jnp.sin(ang)
    return jnp.concatenate([cos, cos], axis=-1), jnp.concatenate([-sin, sin], axis=-1)


def _local_step(x, target, ex, small):
    s = x.shape[0]
    cosf, sinf = _rope_tables(s)
    saved = []
    for l in range(DEPTH):
        p = small[l]
        t = f"l{l}"
        h = _rms_fwd(x, p["norm1_g"], f"norm1_{t}")
        z = _matmul(h, ex.weight(l, "w_in"), mode="nn", out_dtype=BF16, tm=1024, tn=896, tk=2048, b_parts=4, name=f"proj_in_{t}")
        qn, kn, vb, ug, vn, *gate_kept = _proj_post(z, p["q_norm_g"], p["k_norm_g"], p["sgu_ln_g"], p["sgu_ln_b"], cosf, sinf,
                                                    f"proj_post_{t}")
        attn, sgu, mixed, probs, psink = _mixer_fwd(qn, kn, vb, ug, vn, p["w_s_bf16"], p["b_s_tile"], p["sink"],
                                                    p["attn_out_g"], p["sgu_out_g"], f"mixer_{t}")
        x1 = _matmul(mixed, ex.weight(l, "w_o"), mode="nn", out_dtype=F32, tm=2048, tn=256, tk=2048, res=x,
                     name=f"proj_out_{t}")
        h2 = _rms_fwd(x1, p["norm2_g"], f"norm2_{t}")
        a_pre = _matmul(h2, ex.weight(l, "w_up"), mode="nn", out_dtype=BF16, tm=1024, tn=1408, tk=2048, b_parts=4,
                        out_parts=2,
                        name=f"ffn_up_{t}")
        act, dgu = _conv_gate_fwd(a_pre, ex.conv_w(l), p["conv_b"], f"conv_gate_{t}")
        x2 = _matmul(act, ex.weight(l, "w_down"), mode="nn", out_dtype=F32, tm=1024, tn=256, tk=D_FF, res=x1,
                     name=f"ffn_down_{t}")
        saved.append(dict(x=x, h=h, z=z, qn=qn, kn=kn, vb=vb, ug=ug, vn=vn, attn=attn, sgu=sgu, mixed=mixed, x1=x1, h2=h2,
                          a_pre=a_pre, act=act, dgu=dgu, probs=probs, psink=psink, gate_kept=gate_kept))
        x = x2
    loss_tile, dx, dxb = _loss_head(x, target, "loss_head")
    for l in reversed(range(DEPTH)):
        p, sv = small[l], saved[l]
        t = f"l{l}"
        def weight_grad(name, a, g, between, g_parts=0):
            ex.pair_send(l, name, _grad_half(name, a, g, ex.o_arr, None, f"g_{name}_other_{t}", g_parts))
            out = between()
            ex.scatter(l, name, _grad_half(name, a, g, ex.c_arr, ex.pair_recv(l, name), f"g_{name}_own_{t}", g_parts))
            ex.point()
            return out

        def after_down():
            dact = _matmul(dxb, ex.weight(l, "w_down"), mode="nt", out_dtype=BF16, tm=1024, tn=512, tk=2048,
                           name=f"d_act_{t}")
            return _conv_gate_bwd(sv["a_pre"], sv["dgu"], ex.conv_w(l), dact, f"conv_gate_bwd_{t}")

        dap, dcw, dcb = weight_grad("w_down", sv["act"], dxb, after_down)

        def after_up():
            dh2 = _matmul(dap, ex.weight(l, "w_up"), mode="nt", out_dtype=BF16, tm=1024, tn=1024, tk=2816, a_parts=2,
                          b_parts=4, name=f"d_h2_{t}")
            return _rms_bwd(sv["x1"], p["norm2_g"], dh2, dx, f"norm2_bwd_{t}")

        dx1, dx1b, dg2 = weight_grad("w_up", sv["h2"], dap, after_up, g_parts=2)
        ex.pair_send(l, "w_o", _grad_half("w_o", sv["mixed"], dx1b, ex.o_arr, None, f"g_w_o_other_{t}"))
        dmixed = _matmul(dx1b, ex.weight(l, "w_o"), mode="nt", out_dtype=BF16, tm=1024, tn=512, tk=2048,
                         name=f"d_mixed_{t}")
        dqn, dkn, dvb, dug, dvn, dws, dbs, dsk, dga, dgs = _mixer_bwd(
            sv["qn"], sv["kn"], sv["vb"], sv["ug"], sv["vn"], sv["attn"], sv["sgu"], dmixed, p["w_s_bf16"], p["b_s_tile"],
            p["attn_out_g"], p["sgu_out_g"], sv["probs"], sv["psink"], f"mixer_bwd_{t}")
        dz, dqg, dkg, dlg, dlb = _proj_post_bwd(sv["z"], dqn, dkn, dvb, dug, dvn, *sv["gate_kept"], p["q_norm_g"], p["k_norm_g"],
                                                 p["sgu_ln_g"], cosf, sinf, f"proj_post_bwd_{t}")
        ex.scatter(l, "w_o", _grad_half("w_o", sv["mixed"], dx1b, ex.c_arr, ex.pair_recv(l, "w_o"), f"g_w_o_own_{t}"))
        ex.point()

        def after_in():
            dh = _matmul_nt_slabs(dz, ex.weight(l, "w_in"), tm=1024, tn=512, name=f"d_h_{t}")
            return _rms_bwd(sv["x"], p["norm1_g"], dh, dx1, f"norm1_bwd_{t}")

        dx, dxb, dg1 = weight_grad("w_in", sv["h"], dz, after_in)
        ex.small_grads(l, dict(
            norm1_g=dg1[0], q_norm_g=dqg[0], k_norm_g=dkg[0], sink=dsk[:, 0], sgu_ln_g=dlg[0], sgu_ln_b=dlb[0], w_s=dws,
            b_s=dbs[:, :, 0], attn_out_g=dga[0], sgu_out_g=dgs[0], norm2_g=dg2[0],
            conv_w=jnp.concatenate([dcw[0], dcw[1]], axis=-1), conv_b=jnp.concatenate([dcb[0, 0], dcb[1, 0]], axis=-1)),
            loss_tile if l == 0 else None)
    return dx


def _small_views(l, norm1_g, q_norm_g, k_norm_g, sink, sgu_ln_g, sgu_ln_b, w_s, b_s, attn_out_g, sgu_out_g, norm2_g, conv_b):
    return dict(
        norm1_g=norm1_g[l][None], q_norm_g=q_norm_g[l][None], k_norm_g=k_norm_g[l][None], sink=sink[l],
        sgu_ln_g=sgu_ln_g[l][None], sgu_ln_b=sgu_ln_b[l][None], w_s_bf16=w_s[l].astype(BF16),
        b_s_tile=jnp.broadcast_to(b_s[l][:, :, None], (N_GMLP_HEADS, BLOCK, BLOCK)), attn_out_g=attn_out_g[l][None],
        sgu_out_g=sgu_out_g[l][None], norm2_g=norm2_g[l][None], conv_b=conv_b[l][None])


SMALL_NAMES = ("norm1_g", "q_norm_g", "k_norm_g", "sink", "sgu_ln_g", "sgu_ln_b", "w_s", "b_s", "attn_out_g", "sgu_out_g",
               "norm2_g", "conv_b", "conv_w")
REPLICATED_NAMES = SMALL_NAMES[:-1]
BIG_NAMES = ("w_in", "w_o", "w_up", "w_down")
PACK_LANES = 128
PACK_ALIGN = 8 * PACK_LANES


def _pack_rows(shape):
    return -(-math.prod(shape) // PACK_ALIGN) * 8


def _pack_parts(arrays):
    parts = []
    for a in arrays:
        flat = a.reshape(-1)
        parts.append(jnp.pad(flat, (0, _pack_rows(a.shape) * PACK_LANES - flat.shape[0])).reshape(-1, PACK_LANES))
    return parts


def _pack_call(arrays, name):
    parts = _pack_parts(arrays)
    total = sum(p.shape[0] for p in parts)

    def body(*refs):
        o_ref, at = refs[-1], 0
        for p_ref in refs[:-1]:
            o_ref[at:at + p_ref.shape[0], :] = p_ref[...]
            at += p_ref.shape[0]

    vm = pl.BlockSpec(memory_space=pltpu.VMEM)
    return _ordered_call(
        body, name=name, out_shape=jax.ShapeDtypeStruct((total, PACK_LANES), F32), in_specs=[vm] * len(parts), out_specs=vm,
        compiler_params=pltpu.CompilerParams(vmem_limit_bytes=V7X_VMEM_LIMIT),
    )(*parts)


def _unpack_layers(stacked, shapes):
    nl = stacked.shape[0]
    out, at = [], 0
    for shp in shapes:
        rows = _pack_rows(shp)
        out.append(stacked[:, at:at + rows].reshape(nl, -1)[:, :math.prod(shp)].reshape((nl,) + tuple(shp)))
        at += rows
    return out


def _adamw_packed(w, g, m, v, rows, layer, into, name):
    head = pl.BlockSpec((rows, PACK_LANES), lambda i: (0, 0))
    at_layer = pl.BlockSpec((None, rows, PACK_LANES), lambda i: (layer, 0, 0))

    def body(w_ref, g_ref, m_ref, v_ref, *rest):
        d_ref, nm_ref, nv_ref = rest[-3:]
        gv = g_ref[...]
        mn = ADAM_B1 * m_ref[...] + (1.0 - ADAM_B1) * gv
        vn = ADAM_B2 * v_ref[...] + (1.0 - ADAM_B2) * (gv * gv)
        m_hat = mn / (1.0 - ADAM_B1 ** ADAM_STEP)
        v_hat = vn / (1.0 - ADAM_B2 ** ADAM_STEP)
        d_ref[...] = -ADAM_LR * (m_hat / (jnp.sqrt(v_hat) + ADAM_EPS) + ADAM_WD * w_ref[...])
        nm_ref[...] = mn
        nv_ref[...] = vn

    in_specs = [head] * 4
    operands = [w, g, m, v]
    aliases = {}
    if into is not None:
        in_specs += [ANY] * 3
        operands += list(into)
        aliases = {4 + i: i for i in range(3)}
    sds = jax.ShapeDtypeStruct((DEPTH, rows, PACK_LANES), F32)
    return _ordered_call(
        body, name=name, out_shape=(sds,) * 3, grid=(1,), in_specs=in_specs, out_specs=(at_layer,) * 3,
        input_output_aliases=aliases, compiler_params=_params(("arbitrary",)),
    )(*operands)


def kernel(x, norm1_g, w_in, q_norm_g, k_norm_g, sink, sgu_ln_g, sgu_ln_b, w_s, b_s, attn_out_g, sgu_out_g, w_o, norm2_g, w_up, conv_w, conv_b, w_down, loss_target, m_norm1_g, m_w_in, m_q_norm_g, m_k_norm_g, m_sink, m_sgu_ln_g, m_sgu_ln_b, m_w_s, m_b_s, m_attn_out_g, m_sgu_out_g, m_w_o, m_norm2_g, m_w_up, m_conv_w, m_conv_b, m_w_down, v_norm1_g, v_w_in, v_q_norm_g, v_k_norm_g, v_sink, v_sgu_ln_g, v_sgu_ln_b, v_w_s, v_b_s, v_attn_out_g, v_sgu_out_g, v_w_o, v_norm2_g, v_w_up, v_conv_w, v_conv_b, v_w_down):
    weights = dict(norm1_g=norm1_g, w_in=w_in, q_norm_g=q_norm_g, k_norm_g=k_norm_g, sink=sink, sgu_ln_g=sgu_ln_g,
                   sgu_ln_b=sgu_ln_b, w_s=w_s, b_s=b_s, attn_out_g=attn_out_g, sgu_out_g=sgu_out_g, w_o=w_o, norm2_g=norm2_g,
                   w_up=w_up, conv_w=conv_w, conv_b=conv_b, w_down=w_down)
    m_in = dict(norm1_g=m_norm1_g, w_in=m_w_in, q_norm_g=m_q_norm_g, k_norm_g=m_k_norm_g, sink=m_sink, sgu_ln_g=m_sgu_ln_g,
                sgu_ln_b=m_sgu_ln_b, w_s=m_w_s, b_s=m_b_s, attn_out_g=m_attn_out_g, sgu_out_g=m_sgu_out_g, w_o=m_w_o,
                norm2_g=m_norm2_g, w_up=m_w_up, conv_w=m_conv_w, conv_b=m_conv_b, w_down=m_w_down)
    v_in = dict(norm1_g=v_norm1_g, w_in=v_w_in, q_norm_g=v_q_norm_g, k_norm_g=v_k_norm_g, sink=v_sink, sgu_ln_g=v_sgu_ln_g,
                sgu_ln_b=v_sgu_ln_b, w_s=v_w_s, b_s=v_b_s, attn_out_g=v_attn_out_g, sgu_out_g=v_sgu_out_g, w_o=v_w_o,
                norm2_g=v_norm2_g, w_up=v_w_up, conv_w=v_conv_w, conv_b=v_conv_b, w_down=v_w_down)
    cx, cy, cc = lax.axis_index("x"), lax.axis_index("y"), lax.axis_index("c")
    j_me = 2 * cx + cy
    c_arr = jnp.reshape(cc, (1,)).astype(jnp.int32)
    j_arr = jnp.reshape(j_me, (1,)).astype(jnp.int32)

    _Order.last = None
    ex = _Exchange(weights, m_in, v_in, j_arr, c_arr, jnp.reshape(4 * cx + 2 * cy + cc, (1,)).astype(jnp.int32))
    small = [_small_views(l, norm1_g, q_norm_g, k_norm_g, sink, sgu_ln_g, sgu_ln_b, w_s, b_s, attn_out_g, sgu_out_g, norm2_g,
                          conv_b) for l in range(DEPTH)]
    held_back, _ = lax.optimization_barrier(([[src[nm] for nm in REPLICATED_NAMES] for src in (weights, m_in, v_in)], _Order.last))
    packed_in = [[_pack_call([arr[l] for arr in arrays], f"pack_{tag}_l{l}") for tag, arrays in zip("wmv", held_back)]
                 for l in range(DEPTH)]
    dx = _local_step(x[0], loss_target[0], ex, small)
    big_out = ex.finish()

    rep_shapes = [weights[nm].shape[1:] for nm in REPLICATED_NAMES]
    rep_rows = sum(_pack_rows(shp) for shp in rep_shapes)
    cw_shape = (3, 2 * D_FF)
    sums, adam_small = [None] * DEPTH, None
    for l in reversed(range(DEPTH)):
        sums[l] = ex.small_sum(l)
        pw, pm, pv = packed_in[l]
        adam_small = _adamw_packed(pw, sums[l], pm, pv, rep_rows, l, adam_small, f"adamw_small_l{l}")
    cw_rows = _pack_rows(cw_shape)
    loss = sums[0][rep_rows + cw_rows, 0]
    stacked = jnp.stack([sm[:rep_rows + cw_rows] for sm in sums])
    grads = dict(zip(REPLICATED_NAMES, _unpack_layers(stacked[:, :rep_rows], rep_shapes)))
    delta, new_m, new_v = (dict(zip(REPLICATED_NAMES, _unpack_layers(arr, rep_shapes))) for arr in adam_small)
    cw_cols = 2 * D_FF // N_CHIPS
    cw_grad = lax.dynamic_slice_in_dim(_unpack_layers(stacked[:, rep_rows:], [cw_shape])[0], j_me * cw_cols, cw_cols, axis=2)
    flat = lambda a: a.reshape(DEPTH * 3, cw_cols)
    cw_out = _adamw(flat(conv_w), flat(cw_grad), flat(m_conv_w), flat(v_conv_w), "adamw_conv_w")
    grads["conv_w"], delta["conv_w"], new_m["conv_w"], new_v["conv_w"] = (a.reshape(DEPTH, 3, cw_cols) for a in cw_out)

    for name in BIG_NAMES:
        grads[name], delta[name], new_m[name], new_v[name] = big_out[name]

    order = ("norm1_g", "w_in", "q_norm_g", "k_norm_g", "sink", "sgu_ln_g", "sgu_ln_b", "w_s", "b_s", "attn_out_g", "sgu_out_g",
             "w_o", "norm2_g", "w_up", "conv_w", "conv_b", "w_down")
    return (loss, dx[None], *[grads[nm] for nm in order], *[delta[nm] for nm in order], *[new_m[nm] for nm in order],
            *[new_v[nm] for nm in order])
```

```python
import math

import jax
import jax.numpy as jnp
from jax import lax
from jax.experimental import pallas as pl
from jax.experimental.pallas import tpu as pltpu

F32 = jnp.float32
BF16 = jnp.bfloat16

D_MODEL = 2048
HEAD_DIM = 128
ATTN_WIDTH = 1024
N_Q_HEADS = 8
N_KV_HEADS = 2
GQA_GROUP = 4
KV_WIDTH = 256
GMLP_WIDTH = 1024
N_GMLP_HEADS = 8
BLOCK = 128
IN_WIDTH = 3584
D_FF = 5632
DEPTH = 2
EPS = 1e-6
MASK_VALUE = -1e30
ROPE_THETA = 10000.0
N_CHIPS = 4

ADAM_LR = 0.001
ADAM_B1 = 0.9
ADAM_B2 = 0.999
ADAM_EPS = 1e-08
ADAM_WD = 0.01
ADAM_STEP = 10

V7X_VMEM_LIMIT = 48 * 1024 * 1024
MESH = pl.DeviceIdType.MESH

_GELU_C = math.sqrt(2.0 / math.pi)
_GELU_A = 0.044715


def _params(sem=None):
    return pltpu.CompilerParams(dimension_semantics=sem, vmem_limit_bytes=V7X_VMEM_LIMIT)


ANY = pl.BlockSpec(memory_space=pl.ANY)


class _Order:
    last = None


def _ordered_call(body, *, token_index=0, **kw):
    def run(*operands):
        tok = _Order.last
        if tok is None or any(op is tok for op in operands):
            call = pl.pallas_call(body, **kw)
        else:
            n_in = len(operands)

            def ordered_body(*refs):
                return body(*refs[:n_in], *refs[n_in + 1:])

            kw2 = dict(kw)
            if "grid_spec" in kw2:
                gs = kw2["grid_spec"]
                kw2["grid_spec"] = pltpu.PrefetchScalarGridSpec(
                    num_scalar_prefetch=gs.num_scalar_prefetch, grid=gs.grid, in_specs=list(gs.in_specs) + [ANY],
                    out_specs=gs.out_specs, scratch_shapes=gs.scratch_shapes)
            else:
                kw2["in_specs"] = list(kw2["in_specs"]) + [ANY]
            call = pl.pallas_call(ordered_body, **kw2)
            operands = operands + (tok,)
        out = call(*operands)
        _Order.last = out[token_index] if isinstance(out, (tuple, list)) else out
        return out

    return run


def _gelu(x):
    return x * (0.5 * (1.0 + jnp.tanh(_GELU_C * (x + _GELU_A * (x * x * x)))))


def _gelu_grad(x):
    x2 = x * x
    t = jnp.tanh(_GELU_C * (x + _GELU_A * (x * x2)))
    return 0.5 * (1.0 + t) + 0.5 * x * (1.0 - t * t) * (_GELU_C * (1.0 + 3.0 * _GELU_A * x2))


def _mean_last(x):
    return jnp.mean(x, axis=-1, keepdims=True)


def _sum_rows(x):
    return jnp.sum(x, axis=0, keepdims=True)


def _sum_all(x):
    return jnp.sum(jnp.sum(x, axis=1, keepdims=True), axis=0, keepdims=True)


def _matmul(a, b, *, mode, out_dtype, tm, tn, tk, name, res=None, a_parts=0, b_parts=0, out_parts=0):
    assert mode in ("nn", "nt"), mode
    if mode == "nn":
        assert not a_parts
        m, k = a.shape
        n = b.shape[0] * b.shape[2] if b_parts else b.shape[1]
    else:
        m, k = (a.shape[1], a.shape[0] * a.shape[2]) if a_parts else a.shape
        n = b.shape[1] if b_parts else b.shape[0]
    tm, tn, tk = min(tm, m), min(tn, n), min(tk, k)
    assert m % tm == 0 and n % tn == 0 and k % tk == 0, (name, m, n, k, tm, tn, tk)
    nm, nn, nk = m // tm, n // tn, k // tk

    def slab(idx, total_tiles, parts):
        per = total_tiles // parts
        assert per * parts == total_tiles, (name, total_tiles, parts)
        return idx // per, idx % per

    if mode == "nn":
        a_spec = pl.BlockSpec((tm, tk), lambda i, j, kk: (i, kk))
        if b_parts:
            b_spec = pl.BlockSpec((None, tk, tn), lambda i, j, kk: (slab(j, nn, b_parts)[0], kk, slab(j, nn, b_parts)[1]))
        else:
            b_spec = pl.BlockSpec((tk, tn), lambda i, j, kk: (kk, j))
        dims = (((1,), (0,)), ((), ()))
    else:
        if a_parts:
            a_spec = pl.BlockSpec((None, tm, tk), lambda i, j, kk: (slab(kk, nk, a_parts)[0], i, slab(kk, nk, a_parts)[1]))
        else:
            a_spec = pl.BlockSpec((tm, tk), lambda i, j, kk: (i, kk))
        if b_parts:
            b_spec = pl.BlockSpec((None, tn, tk), lambda i, j, kk: (slab(kk, nk, b_parts)[0], j, slab(kk, nk, b_parts)[1]))
        else:
            b_spec = pl.BlockSpec((tn, tk), lambda i, j, kk: (j, kk))
        dims = (((1,), (1,)), ((), ()))
    if out_parts:
        out_shape = jax.ShapeDtypeStruct((out_parts, m, n // out_parts), out_dtype)
        out_spec = pl.BlockSpec((None, tm, tn), lambda i, j, kk: (slab(j, nn, out_parts)[0], i, slab(j, nn, out_parts)[1]))
    else:
        out_shape = jax.ShapeDtypeStruct((m, n), out_dtype)
        out_spec = pl.BlockSpec((tm, tn), lambda i, j, kk: (i, j))
    in_specs = [a_spec, b_spec]
    operands = [a, b]
    if res is not None:
        in_specs.append(pl.BlockSpec((tm, tn), lambda i, j, kk: (i, j)))
        operands.append(res)

    def body(*refs):
        a_ref, b_ref = refs[0], refs[1]
        res_ref = refs[2] if res is not None else None
        o_ref = refs[3] if res is not None else refs[2]
        p = lax.dot_general(a_ref[...], b_ref[...], dims, preferred_element_type=F32)

        def finish(total):
            if res_ref is not None:
                total = res_ref[...] + total
            o_ref[...] = total.astype(out_dtype)

        if nk == 1:
            finish(p)
        else:
            acc_ref = refs[-1]
            kk = pl.program_id(2)

            @pl.when(kk == 0)
            def _():
                acc_ref[...] = p

            @pl.when(jnp.logical_and(kk > 0, kk < nk - 1))
            def _():
                acc_ref[...] += p

            @pl.when(kk == nk - 1)
            def _():
                finish(acc_ref[...] + p)

    scratch = [pltpu.VMEM((tm, tn), F32)] if nk > 1 else []
    return _ordered_call(
        body, name=name, out_shape=out_shape, grid=(nm, nn, nk), in_specs=in_specs, out_specs=out_spec,
        scratch_shapes=scratch, compiler_params=_params(("parallel", "parallel", "arbitrary")),
    )(*operands)


def _matmul_nt_slabs(a, b, *, tm, tn, name, a_parts=0):
    nslab, n, ks = b.shape
    m = a.shape[1] if a_parts else a.shape[0]
    tm, tn = min(tm, m), min(tn, n)
    assert m % tm == 0 and n % tn == 0, (name, m, n, tm, tn)
    if a_parts:
        per = nslab // a_parts
        assert per * a_parts == nslab and a.shape[2] == per * ks, (name, a.shape, b.shape)
        a_spec = pl.BlockSpec((a_parts, tm, per * ks), lambda i, j: (0, i, 0))
    else:
        assert a.shape[1] == nslab * ks, (name, a.shape, b.shape)
        a_spec = pl.BlockSpec((tm, nslab * ks), lambda i, j: (i, 0))

    def body(a_ref, b_ref, o_ref):
        total = None
        for sl in range(nslab):
            if a_parts:
                a_sl = a_ref[sl // per, :, (sl % per) * ks:(sl % per + 1) * ks]
            else:
                a_sl = a_ref[:, sl * ks:(sl + 1) * ks]
            p = lax.dot_general(a_sl, b_ref[sl], (((1,), (1,)), ((), ())), preferred_element_type=F32)
            total = p if total is None else total + p
        o_ref[...] = total.astype(BF16)

    return _ordered_call(
        body, name=name, out_shape=jax.ShapeDtypeStruct((m, n), BF16), grid=(m // tm, n // tn),
        in_specs=[a_spec, pl.BlockSpec((nslab, tn, ks), lambda i, j: (0, j, 0))],
        out_specs=pl.BlockSpec((tm, tn), lambda i, j: (i, j)), compiler_params=_params(("parallel", "parallel")),
    )(a, b)


GRAD_HALVES = {
    "w_in": ("rows_of_slab", 1024, 896), "w_up": ("rows_of_slab", 1024, 1408), "w_o": ("rows_of_block", 256, 2048),
    "w_down": ("cols_of_block", 1408, 512)}


def _half_shape(name, shard_shape):
    r, cols = shard_shape
    return (r, cols // 2) if GRAD_HALVES[name][0] == "cols_of_block" else (r // 2, cols)


def _grad_half(name, a, g, sel, res, call_name, g_parts=0):
    kind, tm, tn = GRAD_HALVES[name]
    s, m = a.shape
    n = g.shape[0] * g.shape[2] if g_parts else g.shape[1]
    if kind == "rows_of_slab":
        rh, hc = m // 2, n // N_CHIPS
        per = hc // tn
        grid = (rh // tm, n // tn)
        a_map = lambda i, j, sel_ref: (0, sel_ref[0] * (rh // tm) + i)
        g_col = lambda i, j, sel_ref: j
        o_map = lambda i, j, sel_ref: (j // per, i, j % per)
    elif kind == "rows_of_block":
        rh, hc = m // N_CHIPS // 2, n
        assert tm == rh
        grid = (N_CHIPS, n // tn)
        a_map = lambda i, j, sel_ref: (0, 2 * i + sel_ref[0])
        g_col = lambda i, j, sel_ref: j
        o_map = lambda i, j, sel_ref: (i, 0, j)
    else:
        rh, hc = m // N_CHIPS, n // 2
        assert tm == rh
        grid = (N_CHIPS, hc // tn)
        a_map = lambda i, j, sel_ref: (0, i)
        g_col = lambda i, j, sel_ref: sel_ref[0] * (hc // tn) + j
        o_map = lambda i, j, sel_ref: (i, 0, j)
    if g_parts:
        g_per = (n // tn) // g_parts
        g_spec = pl.BlockSpec((None, s, tn), lambda i, j, sel_ref: (g_col(i, j, sel_ref) // g_per, 0, g_col(i, j, sel_ref) % g_per))
    else:
        g_spec = pl.BlockSpec((s, tn), lambda i, j, sel_ref: (0, g_col(i, j, sel_ref)))
    o_spec = pl.BlockSpec((None, tm, tn), o_map)
    in_specs = [pl.BlockSpec((s, tm), a_map), g_spec] + ([o_spec] if res is not None else [])

    def body(sel_ref, a_ref, g_ref, *rest):
        o_ref = rest[-1]
        p = lax.dot_general(a_ref[...], g_ref[...], (((0,), (0,)), ((), ())), preferred_element_type=F32)
        if res is not None:
            p = p + rest[0][...].astype(F32)
        o_ref[...] = p.astype(BF16)

    grid_spec = pltpu.PrefetchScalarGridSpec(num_scalar_prefetch=1, grid=grid, in_specs=in_specs, out_specs=o_spec)
    return _ordered_call(
        body, name=call_name, out_shape=jax.ShapeDtypeStruct((N_CHIPS, rh, hc), BF16), grid_spec=grid_spec,
        compiler_params=_params(("parallel", "parallel")),
    )(sel, a, g, *([res] if res is not None else []))


def _row_tile(s):
    return min(512, s)


def _rows(width, tr):
    return pl.BlockSpec((tr, width), lambda i: (i, 0))


def _const2(shape):
    return pl.BlockSpec(shape, lambda i: (0, 0))


def _rms_fwd(x, g, name):
    s, d = x.shape
    tr = _row_tile(s)

    def body(x_ref, g_ref, o_ref):
        xv = x_ref[...]
        r = lax.rsqrt(_mean_last(xv * xv) + EPS)
        o_ref[...] = (xv * r * g_ref[...]).astype(BF16)

    return _ordered_call(
        body, name=name, out_shape=jax.ShapeDtypeStruct((s, d), BF16), grid=(s // tr,),
        in_specs=[_rows(d, tr), _const2((1, d))], out_specs=_rows(d, tr), compiler_params=_params(("parallel",)),
    )(x, g)


def _rms_bwd(x, g, dh, dres, name):
    s, d = x.shape
    tr = _row_tile(s)

    def body(x_ref, g_ref, dh_ref, dres_ref, dx_ref, dxb_ref, dg_ref):
        xv, dy = x_ref[...], dh_ref[...].astype(F32)
        r = lax.rsqrt(_mean_last(xv * xv) + EPS)
        gdy = dy * g_ref[...]
        dx = dres_ref[...] + r * gdy - xv * ((r * r * r) * _mean_last(xv * gdy))
        dx_ref[...] = dx
        dxb_ref[...] = dx.astype(BF16)

        @pl.when(pl.program_id(0) == 0)
        def _():
            dg_ref[...] = jnp.zeros_like(dg_ref)

        dg_ref[...] += _sum_rows(xv * r * dy)

    return _ordered_call(
        body, name=name,
        out_shape=(jax.ShapeDtypeStruct((s, d), F32), jax.ShapeDtypeStruct((s, d), BF16), jax.ShapeDtypeStruct((1, d), F32)),
        grid=(s // tr,), in_specs=[_rows(d, tr), _const2((1, d)), _rows(d, tr), _rows(d, tr)],
        out_specs=(_rows(d, tr), _rows(d, tr), _const2((1, d))), compiler_params=_params(("arbitrary",)),
    )(x, g, dh, dres)


Q0, K0, V0, GU0, GV0 = 0, ATTN_WIDTH, ATTN_WIDTH + KV_WIDTH, ATTN_WIDTH + 2 * KV_WIDTH, ATTN_WIDTH + 2 * KV_WIDTH + GMLP_WIDTH


def _head(h, base=0):
    return slice(base + h * HEAD_DIM, base + (h + 1) * HEAD_DIM)


def _proj_post(z, qg, kg, lg, lb, cosf, sinf, name):
    s = z.shape[0]
    tr = _row_tile(s)

    def body(z_ref, qg_ref, kg_ref, lg_ref, lb_ref, cos_ref, sin_ref, qn_ref, kn_ref, vb_ref, ug_ref, vn_ref,
             dgu_ref, dgv_ref, xhat_ref, rstd_ref):
        cos, sin = cos_ref[...], sin_ref[...]

        def norm_rope(xh, g):
            y = xh * lax.rsqrt(_mean_last(xh * xh) + EPS) * g
            return y * cos + pltpu.roll(y, HEAD_DIM // 2, 1) * sin

        for h in range(N_Q_HEADS):
            qn_ref[:, _head(h)] = norm_rope(z_ref[:, _head(h, Q0)].astype(F32), qg_ref[...]).astype(BF16)
        for h in range(N_KV_HEADS):
            kn_ref[:, _head(h)] = norm_rope(z_ref[:, _head(h, K0)].astype(F32), kg_ref[...]).astype(BF16)
        vb_ref[...] = z_ref[:, V0:GU0]
        gu = z_ref[:, GU0:GV0].astype(F32)
        ug_ref[...] = _gelu(gu)
        dgu_ref[...] = _gelu_grad(gu).astype(BF16)
        gv = z_ref[:, GV0:IN_WIDTH].astype(F32)
        vg = _gelu(gv)
        dgv_ref[...] = _gelu_grad(gv).astype(BF16)
        xc = vg - _mean_last(vg)
        r = lax.rsqrt(_mean_last(xc * xc) + EPS)
        y = xc * r
        xhat_ref[...] = y.astype(BF16)
        rstd_ref[...] = r
        vn_ref[...] = (y * lg_ref[...] + lb_ref[...]).astype(BF16)

    wide = jax.ShapeDtypeStruct((s, GMLP_WIDTH), BF16)
    return _ordered_call(
        body, name=name,
        out_shape=(jax.ShapeDtypeStruct((s, ATTN_WIDTH), BF16), jax.ShapeDtypeStruct((s, KV_WIDTH), BF16),
                   jax.ShapeDtypeStruct((s, KV_WIDTH), BF16), jax.ShapeDtypeStruct((s, GMLP_WIDTH), F32), wide,
                   wide, wide, wide, jax.ShapeDtypeStruct((s, 1), F32)),
        grid=(s // tr,),
        in_specs=[_rows(IN_WIDTH, tr), _const2((1, HEAD_DIM)), _const2((1, HEAD_DIM)), _const2((1, GMLP_WIDTH)),
                  _const2((1, GMLP_WIDTH)), _rows(HEAD_DIM, tr), _rows(HEAD_DIM, tr)],
        out_specs=(_rows(ATTN_WIDTH, tr), _rows(KV_WIDTH, tr), _rows(KV_WIDTH, tr), _rows(GMLP_WIDTH, tr), _rows(GMLP_WIDTH, tr),
                   _rows(GMLP_WIDTH, tr), _rows(GMLP_WIDTH, tr), _rows(GMLP_WIDTH, tr), _rows(1, tr)),
        compiler_params=_params(("parallel",)),
    )(z, qg, kg, lg, lb, cosf, sinf)


def _proj_post_bwd(z, dqn, dkn, dvb, dug, dvn, gelu_grad_u, gelu_grad_v, xhat_v, rstd_v, qg, kg, lg, cosf, sinf, name):
    s = z.shape[0]
    tr = _row_tile(s)

    def body(z_ref, dqn_ref, dkn_ref, dvb_ref, dug_ref, dvn_ref, ggu_ref, ggv_ref, xhat_ref, rstd_ref, qg_ref, kg_ref, lg_ref,
             cos_ref, sin_ref, dz_ref, dqg_ref, dkg_ref, dlg_ref, dlb_ref):
        cos, sin = cos_ref[...], sin_ref[...]

        @pl.when(pl.program_id(0) == 0)
        def _():
            dqg_ref[...] = jnp.zeros_like(dqg_ref)
            dkg_ref[...] = jnp.zeros_like(dkg_ref)
            dlg_ref[...] = jnp.zeros_like(dlg_ref)
            dlb_ref[...] = jnp.zeros_like(dlb_ref)

        def norm_rope_bwd(xh, g, dout):
            dy = dout * cos - pltpu.roll(dout, HEAD_DIM // 2, 1) * sin
            r = lax.rsqrt(_mean_last(xh * xh) + EPS)
            xhat = xh * r
            gdy = dy * g
            return r * (gdy - xhat * _mean_last(xhat * gdy)), _sum_rows(xhat * dy)

        dqg = jnp.zeros((1, HEAD_DIM), F32)
        for h in range(N_Q_HEADS):
            dx, dg = norm_rope_bwd(z_ref[:, _head(h, Q0)].astype(F32), qg_ref[...], dqn_ref[:, _head(h)])
            dz_ref[:, _head(h, Q0)] = dx.astype(BF16)
            dqg = dqg + dg
        dqg_ref[...] += dqg
        dkg = jnp.zeros((1, HEAD_DIM), F32)
        for h in range(N_KV_HEADS):
            dx, dg = norm_rope_bwd(z_ref[:, _head(h, K0)].astype(F32), kg_ref[...], dkn_ref[:, _head(h)])
            dz_ref[:, _head(h, K0)] = dx.astype(BF16)
            dkg = dkg + dg
        dkg_ref[...] += dkg
        dz_ref[:, V0:GU0] = dvb_ref[...].astype(BF16)
        dz_ref[:, GU0:GV0] = (dug_ref[...] * ggu_ref[...].astype(F32)).astype(BF16)
        xhat = xhat_ref[...].astype(F32)
        dvn_v = dvn_ref[...]
        dlg_ref[...] += _sum_rows(xhat * dvn_v)
        dlb_ref[...] += _sum_rows(dvn_v)
        dxh = dvn_v * lg_ref[...]
        dvg = rstd_ref[...] * (dxh - _mean_last(dxh) - xhat * _mean_last(dxh * xhat))
        dz_ref[:, GV0:IN_WIDTH] = (dvg * ggv_ref[...].astype(F32)).astype(BF16)

    return _ordered_call(
        body, name=name,
        out_shape=(jax.ShapeDtypeStruct((s, IN_WIDTH), BF16), jax.ShapeDtypeStruct((1, HEAD_DIM), F32),
                   jax.ShapeDtypeStruct((1, HEAD_DIM), F32), jax.ShapeDtypeStruct((1, GMLP_WIDTH), F32),
                   jax.ShapeDtypeStruct((1, GMLP_WIDTH), F32)),
        grid=(s // tr,),
        in_specs=[_rows(V0, tr), _rows(ATTN_WIDTH, tr), _rows(KV_WIDTH, tr), _rows(KV_WIDTH, tr), _rows(GMLP_WIDTH, tr),
                  _rows(GMLP_WIDTH, tr), _rows(GMLP_WIDTH, tr), _rows(GMLP_WIDTH, tr), _rows(GMLP_WIDTH, tr), _rows(1, tr),
                  _const2((1, HEAD_DIM)), _const2((1, HEAD_DIM)), _const2((1, GMLP_WIDTH)), _rows(HEAD_DIM, tr),
                  _rows(HEAD_DIM, tr)],
        out_specs=(_rows(IN_WIDTH, tr), _const2((1, HEAD_DIM)), _const2((1, HEAD_DIM)), _const2((1, GMLP_WIDTH)),
                   _const2((1, GMLP_WIDTH))),
        compiler_params=_params(("arbitrary",)),
    )(z, dqn, dkn, dvb, dug, dvn, gelu_grad_u, gelu_grad_v, xhat_v, rstd_v, qg, kg, lg, cosf, sinf)


def _band_valid(n, s):
    shape = (GQA_GROUP * BLOCK, 3 * BLOCK)
    i = lax.broadcasted_iota(jnp.int32, shape, 0) & (BLOCK - 1)
    j = lax.broadcasted_iota(jnp.int32, shape, 1)
    k_pos = n * BLOCK - BLOCK + j
    return (jnp.abs(j - BLOCK - i) <= BLOCK) & (k_pos >= 0) & (k_pos < s)


def _group_rows(x, kh):
    return jnp.concatenate([x[:, _head(kh * GQA_GROUP + g)] for g in range(GQA_GROUP)], axis=0)


def _group_sinks(sink_ref, kh):
    return jnp.concatenate([jnp.full((BLOCK, 1), sink_ref[kh * GQA_GROUP + g], F32) for g in range(GQA_GROUP)], axis=0)


def _rows_of(x, g):
    return x[g * BLOCK:(g + 1) * BLOCK]


def _probs(q, kb, sink_h, valid):
    sc = lax.dot_general(q, kb, (((1,), (1,)), ((), ())), preferred_element_type=F32) * (HEAD_DIM ** -0.5)
    sc = jnp.where(valid, sc, MASK_VALUE)
    m = jnp.maximum(jnp.max(sc, axis=-1, keepdims=True), sink_h)
    p = jnp.exp(sc - m)
    es = jnp.exp(sink_h - m)
    den = jnp.sum(p, axis=-1, keepdims=True) + es
    inv = 1.0 / den
    return p * inv, es * inv


def _band_specs(width, nb):
    return [pl.BlockSpec((BLOCK, width), lambda n: (jnp.maximum(n - 1, 0), 0)),
            pl.BlockSpec((BLOCK, width), lambda n: (n, 0)),
            pl.BlockSpec((BLOCK, width), lambda n: (jnp.minimum(n + 1, nb - 1), 0))]


def _blk(width):
    return pl.BlockSpec((BLOCK, width), lambda n: (n, 0))


def _whole3(shape):
    return pl.BlockSpec(shape, lambda n: (0, 0, 0))


def _smem():
    return pl.BlockSpec(memory_space=pltpu.SMEM)


def _mixer_fwd(qn, kn, vb, ug, vn, wsb, bsb, sink, ga, gs, name):
    s = qn.shape[0]
    nb = s // BLOCK

    def body(sink_ref, q_ref, kp_ref, kc_ref, kx_ref, vp_ref, vc_ref, vx_ref, ug_ref, vn_ref, ws_ref, bs_ref, ga_ref, gs_ref,
             attn_ref, sgu_ref, mix_ref, probs_ref, psink_ref):
        n = pl.program_id(0)
        valid = _band_valid(n, s)
        ssq = jnp.zeros((BLOCK, 1), F32)
        for kh in range(N_KV_HEADS):
            kb = jnp.concatenate([kp_ref[:, _head(kh)], kc_ref[:, _head(kh)], kx_ref[:, _head(kh)]], axis=0)
            vbd = jnp.concatenate([vp_ref[:, _head(kh)], vc_ref[:, _head(kh)], vx_ref[:, _head(kh)]], axis=0)
            p, p_sink = _probs(_group_rows(q_ref, kh), kb, _group_sinks(sink_ref, kh), valid)
            pb = p.astype(BF16)
            probs_ref[kh] = pb
            psink_ref[kh] = p_sink
            o4 = jnp.dot(pb, vbd, preferred_element_type=F32)
            for g in range(GQA_GROUP):
                o = _rows_of(o4, g)
                attn_ref[:, _head(kh * GQA_GROUP + g)] = o
                ssq = ssq + jnp.sum(o * o, axis=-1, keepdims=True)
        r = lax.rsqrt(ssq * (1.0 / ATTN_WIDTH) + EPS)
        mix_ref[:, 0:ATTN_WIDTH] = (attn_ref[...] * r * ga_ref[...]).astype(BF16)
        ssq = jnp.zeros((BLOCK, 1), F32)
        for h in range(N_GMLP_HEADS):
            f = jnp.dot(ws_ref[h], vn_ref[:, _head(h)], preferred_element_type=F32) + bs_ref[h]
            o = ug_ref[:, _head(h)] * f
            sgu_ref[:, _head(h)] = o
            ssq = ssq + jnp.sum(o * o, axis=-1, keepdims=True)
        r = lax.rsqrt(ssq * (1.0 / GMLP_WIDTH) + EPS)
        mix_ref[:, ATTN_WIDTH:D_MODEL] = (sgu_ref[...] * r * gs_ref[...]).astype(BF16)

    hh = (N_GMLP_HEADS, BLOCK, BLOCK)
    return _ordered_call(
        body, name=name,
        out_shape=(jax.ShapeDtypeStruct((s, ATTN_WIDTH), F32), jax.ShapeDtypeStruct((s, GMLP_WIDTH), F32),
                   jax.ShapeDtypeStruct((s, D_MODEL), BF16), jax.ShapeDtypeStruct((nb,) + PROBS_BLOCK, BF16),
                   jax.ShapeDtypeStruct((nb,) + PSINK_BLOCK, F32)),
        grid=(nb,),
        in_specs=[_smem(), _blk(ATTN_WIDTH)] + _band_specs(KV_WIDTH, nb) + _band_specs(KV_WIDTH, nb)
        + [_blk(GMLP_WIDTH), _blk(GMLP_WIDTH), _whole3(hh), _whole3(hh),
           pl.BlockSpec((1, ATTN_WIDTH), lambda n: (0, 0)), pl.BlockSpec((1, GMLP_WIDTH), lambda n: (0, 0))],
        out_specs=(_blk(ATTN_WIDTH), _blk(GMLP_WIDTH), _blk(D_MODEL), _per_block(PROBS_BLOCK), _per_block(PSINK_BLOCK)),
        compiler_params=_params(("parallel",)),
    )(sink, qn, kn, kn, kn, vb, vb, vb, ug, vn, wsb, bsb, ga, gs)


PROBS_BLOCK = (N_KV_HEADS, GQA_GROUP * BLOCK, 3 * BLOCK)
PSINK_BLOCK = (N_KV_HEADS, GQA_GROUP * BLOCK, 1)


def _per_block(shape):
    return pl.BlockSpec((None,) + shape, lambda n: (n, 0, 0, 0))


def _mixer_bwd(qn, kn, vb, ug, vn, attn, sgu, dmixed, wsb, bsb, ga, gs, probs, psink, name):
    s = qn.shape[0]
    nb = s // BLOCK
    tn_dims = (((0,), (0,)), ((), ()))
    nt_dims = (((1,), (1,)), ((), ()))

    def body(q_ref, kp_ref, kc_ref, kx_ref, vp_ref, vc_ref, vx_ref, ug_ref, vn_ref, attn_ref, sgu_ref, dm_ref,
             ws_ref, bs_ref, ga_ref, gs_ref, probs_ref, psink_ref,
             dq_ref, dk_ref, dv_ref, dug_ref, dvn_ref, dws_ref, dbs_ref, dsk_ref, dga_ref, dgs_ref, dk_acc, dv_acc):
        n = pl.program_id(0)

        @pl.when(n == 0)
        def _():
            for ref in (dk_acc, dv_acc, dws_ref, dbs_ref, dsk_ref, dga_ref, dgs_ref):
                ref[...] = jnp.zeros_like(ref)

        def out_norm_bwd(o, g, dy):
            r = lax.rsqrt(_mean_last(o * o) + EPS)
            gdy = dy * g
            return r * gdy - o * ((r * r * r) * _mean_last(o * gdy)), _sum_rows(o * r * dy)

        d_attn, dga = out_norm_bwd(attn_ref[...], ga_ref[...], dm_ref[:, 0:ATTN_WIDTH].astype(F32))
        dga_ref[...] += dga
        d_sgu, dgs = out_norm_bwd(sgu_ref[...], gs_ref[...], dm_ref[:, ATTN_WIDTH:D_MODEL].astype(F32))
        dgs_ref[...] += dgs

        for h in range(N_GMLP_HEADS):
            vn_h = vn_ref[:, _head(h)]
            f = jnp.dot(ws_ref[h], vn_h, preferred_element_type=F32) + bs_ref[h]
            ds_h = d_sgu[:, _head(h)]
            dug_ref[:, _head(h)] = ds_h * f
            df = ds_h * ug_ref[:, _head(h)]
            dfb = df.astype(BF16)
            dvn_ref[:, _head(h)] = lax.dot_general(ws_ref[h], dfb, tn_dims, preferred_element_type=F32)
            dws_ref[h] += lax.dot_general(dfb, vn_h, nt_dims, preferred_element_type=F32)
            dbs_ref[h] += jnp.broadcast_to(jnp.sum(df, axis=-1, keepdims=True), (BLOCK, BLOCK))

        row0 = pl.multiple_of(n * BLOCK, BLOCK)
        for kh in range(N_KV_HEADS):
            kb = jnp.concatenate([kp_ref[:, _head(kh)], kc_ref[:, _head(kh)], kx_ref[:, _head(kh)]], axis=0)
            vbd = jnp.concatenate([vp_ref[:, _head(kh)], vc_ref[:, _head(kh)], vx_ref[:, _head(kh)]], axis=0)
            q4 = _group_rows(q_ref, kh)
            pb = probs_ref[kh]
            p = pb.astype(F32)
            do4 = _group_rows(d_attn, kh).astype(BF16)
            dp = lax.dot_general(do4, vbd, nt_dims, preferred_element_type=F32)
            delta = jnp.sum(p * dp, axis=-1, keepdims=True)
            dsc = (p * (dp - delta) * (HEAD_DIM ** -0.5)).astype(BF16)
            d_sink = -(psink_ref[kh] * delta)
            dq4 = jnp.dot(dsc, kb, preferred_element_type=F32)
            for g in range(GQA_GROUP):
                h = kh * GQA_GROUP + g
                dsk_ref[h:h + 1, :] += jnp.broadcast_to(_sum_all(_rows_of(d_sink, g)), (1, BLOCK))
                dq_ref[:, _head(h)] = _rows_of(dq4, g)
            dk_acc[pl.ds(row0, 3 * BLOCK), _head(kh)] += lax.dot_general(dsc, q4, tn_dims, preferred_element_type=F32)
            dv_acc[pl.ds(row0, 3 * BLOCK), _head(kh)] += lax.dot_general(pb, do4, tn_dims, preferred_element_type=F32)

        @pl.when(n == nb - 1)
        def _():
            dk_ref[...] = dk_acc[BLOCK:BLOCK + s, :]
            dv_ref[...] = dv_acc[BLOCK:BLOCK + s, :]

    hh = (N_GMLP_HEADS, BLOCK, BLOCK)
    full_kv = pl.BlockSpec((s, KV_WIDTH), lambda n: (0, 0))
    return _ordered_call(
        body, name=name,
        out_shape=(jax.ShapeDtypeStruct((s, ATTN_WIDTH), F32), jax.ShapeDtypeStruct((s, KV_WIDTH), F32),
                   jax.ShapeDtypeStruct((s, KV_WIDTH), F32), jax.ShapeDtypeStruct((s, GMLP_WIDTH), F32),
                   jax.ShapeDtypeStruct((s, GMLP_WIDTH), F32), jax.ShapeDtypeStruct(hh, F32), jax.ShapeDtypeStruct(hh, F32),
                   jax.ShapeDtypeStruct((N_Q_HEADS, BLOCK), F32), jax.ShapeDtypeStruct((1, ATTN_WIDTH), F32),
                   jax.ShapeDtypeStruct((1, GMLP_WIDTH), F32)),
        grid=(nb,),
        in_specs=[_blk(ATTN_WIDTH)] + _band_specs(KV_WIDTH, nb) + _band_specs(KV_WIDTH, nb)
        + [_blk(GMLP_WIDTH), _blk(GMLP_WIDTH), _blk(ATTN_WIDTH), _blk(GMLP_WIDTH), _blk(D_MODEL), _whole3(hh), _whole3(hh),
           pl.BlockSpec((1, ATTN_WIDTH), lambda n: (0, 0)), pl.BlockSpec((1, GMLP_WIDTH), lambda n: (0, 0)),
           _per_block(PROBS_BLOCK), _per_block(PSINK_BLOCK)],
        out_specs=(_blk(ATTN_WIDTH), full_kv, full_kv, _blk(GMLP_WIDTH), _blk(GMLP_WIDTH), _whole3(hh), _whole3(hh),
                   pl.BlockSpec((N_Q_HEADS, BLOCK), lambda n: (0, 0)), pl.BlockSpec((1, ATTN_WIDTH), lambda n: (0, 0)),
                   pl.BlockSpec((1, GMLP_WIDTH), lambda n: (0, 0))),
        scratch_shapes=[pltpu.VMEM((s + 2 * BLOCK, KV_WIDTH), F32), pltpu.VMEM((s + 2 * BLOCK, KV_WIDTH), F32)],
        compiler_params=_params(("arbitrary",)),
    )(qn, kn, kn, kn, vb, vb, vb, ug, vn, attn, sgu, dmixed, wsb, bsb, ga, gs, probs, psink)


CONV_TILE = 256


PAD_ROWS = 8


def _zero_pad_rows(pad_ref):
    s = pad_ref.shape[0] - 2 * PAD_ROWS
    zeros = jnp.zeros((PAD_ROWS, pad_ref.shape[1]), F32)
    pad_ref[0:PAD_ROWS, :] = zeros
    pad_ref[PAD_ROWS + s:2 * PAD_ROWS + s, :] = zeros


def _shift_rows(a, pad_ref):
    s = a.shape[0]
    pad_ref[PAD_ROWS:PAD_ROWS + s, :] = a
    padded = pad_ref[...]
    prev = pltpu.roll(padded, 1, 0)[PAD_ROWS:PAD_ROWS + s]
    nxt = pltpu.roll(padded, s + 2 * PAD_ROWS - 1, 0)[PAD_ROWS:PAD_ROWS + s]
    return prev, nxt


def _conv_specs(s):
    tc = CONV_TILE
    nj = D_FF // tc
    return (tc, nj, pl.BlockSpec((2, s, tc), lambda j: (0, 0, j)),
            [pl.BlockSpec((3, tc), lambda j: (0, j)), pl.BlockSpec((3, tc), lambda j: (0, j + nj))],
            [pl.BlockSpec((1, tc), lambda j: (0, j)), pl.BlockSpec((1, tc), lambda j: (0, j + nj))])


def _conv_gate_fwd(a_pre, cw, cb, name):
    s = a_pre.shape[1]
    tc, nj, a_spec, w_specs, b_specs = _conv_specs(s)

    def body(a_ref, wg_ref, wu_ref, bg_ref, bu_ref, act_ref, dgu_ref, pad_ref):
        _zero_pad_rows(pad_ref)

        def conv(a, w_ref, b_ref):
            prev, nxt = _shift_rows(a, pad_ref)
            return b_ref[...] + prev * w_ref[0:1, :] + a * w_ref[1:2, :] + nxt * w_ref[2:3, :]

        g = conv(a_ref[0].astype(F32), wg_ref, bg_ref)
        u = conv(a_ref[1].astype(F32), wu_ref, bu_ref)
        sg = 1.0 / (1.0 + jnp.exp(-g))
        silu = g * sg
        act_ref[...] = (silu * u).astype(BF16)
        dgu_ref[0] = (u * (sg * (1.0 + g * (1.0 - sg)))).astype(BF16)
        dgu_ref[1] = silu.astype(BF16)

    return _ordered_call(
        body, name=name, out_shape=(jax.ShapeDtypeStruct((s, D_FF), BF16), jax.ShapeDtypeStruct((2, s, D_FF), BF16)),
        grid=(nj,), in_specs=[a_spec] + w_specs + b_specs,
        out_specs=(pl.BlockSpec((s, tc), lambda j: (0, j)), pl.BlockSpec((2, s, tc), lambda j: (0, 0, j))),
        scratch_shapes=[pltpu.VMEM((s + 2 * PAD_ROWS, tc), F32)], compiler_params=_params(("parallel",)),
    )(a_pre, cw, cw, cb, cb)


def _conv_gate_bwd(a_pre, dgu, cw, dact, name):
    s = a_pre.shape[1]
    tc, nj, a_spec, w_specs, _ = _conv_specs(s)

    def body(a_ref, dgu_ref, wg_ref, wu_ref, dact_ref, dap_ref, dcw_ref, dcb_ref, pad_ref):
        _zero_pad_rows(pad_ref)
        dact_v = dact_ref[...].astype(F32)
        for part, w_ref in enumerate((wg_ref, wu_ref)):
            da = dact_v * dgu_ref[part].astype(F32)
            a = a_ref[part].astype(F32)
            da_prev, da_next = _shift_rows(da, pad_ref)
            dcw_ref[part, 0:1, :] = _sum_rows(a * da_next)
            dcw_ref[part, 1:2, :] = _sum_rows(a * da)
            dcw_ref[part, 2:3, :] = _sum_rows(a * da_prev)
            dcb_ref[part] = _sum_rows(da)
            dap_ref[part] = (da_next * w_ref[0:1, :] + da * w_ref[1:2, :] + da_prev * w_ref[2:3, :]).astype(BF16)

    return _ordered_call(
        body, name=name,
        out_shape=(jax.ShapeDtypeStruct((2, s, D_FF), BF16), jax.ShapeDtypeStruct((2, 3, D_FF), F32),
                   jax.ShapeDtypeStruct((2, 1, D_FF), F32)),
        grid=(nj,),
        in_specs=[a_spec, pl.BlockSpec((2, s, tc), lambda j: (0, 0, j))] + w_specs + [pl.BlockSpec((s, tc), lambda j: (0, j))],
        out_specs=(pl.BlockSpec((2, s, tc), lambda j: (0, 0, j)), pl.BlockSpec((2, 3, tc), lambda j: (0, 0, j)),
                   pl.BlockSpec((2, 1, tc), lambda j: (0, 0, j))),
        scratch_shapes=[pltpu.VMEM((s + 2 * PAD_ROWS, tc), F32)], compiler_params=_params(("parallel",)),
    )(a_pre, dgu, cw, cw, dact)


def _loss_head(y, target, name):
    s, d = y.shape
    tr = _row_tile(s)

    def body(y_ref, t_ref, loss_ref, dy_ref, dyb_ref):
        err = y_ref[...] - t_ref[...]

        @pl.when(pl.program_id(0) == 0)
        def _():
            loss_ref[...] = jnp.zeros_like(loss_ref)

        loss_ref[...] += jnp.broadcast_to(0.5 * _sum_all(_mean_last(err * err)), (8, 128))
        dy = err * (1.0 / d)
        dy_ref[...] = dy
        dyb_ref[...] = dy.astype(BF16)

    return _ordered_call(
        body, name=name,
        out_shape=(jax.ShapeDtypeStruct((8, 128), F32), jax.ShapeDtypeStruct((s, d), F32), jax.ShapeDtypeStruct((s, d), BF16)),
        grid=(s // tr,), in_specs=[_rows(d, tr), _rows(d, tr)],
        out_specs=(_const2((8, 128)), _rows(d, tr), _rows(d, tr)), compiler_params=_params(("arbitrary",)),
    )(y, target)


def _row_block(rows, cols, budget=1 << 20):
    if rows * cols <= budget:
        return rows
    best = None
    for tr in range(16, rows, 16):
        if rows % tr == 0 and tr * cols <= budget:
            best = tr
    assert best is not None, (rows, cols)
    return best


def _place_shard(x4, layer, j_arr, out_dtype, name):
    _, nh, r, cols = x4.shape
    tr = _row_block(r, cols)

    def body(j_ref, x_ref, o_ref):
        o_ref[...] = x_ref[...].astype(out_dtype)

    grid_spec = pltpu.PrefetchScalarGridSpec(
        num_scalar_prefetch=1, grid=(nh, r // tr),
        in_specs=[pl.BlockSpec((None, None, tr, cols), lambda h, i, j_ref: (layer, h, i, 0))],
        out_specs=pl.BlockSpec((None, None, tr, cols), lambda h, i, j_ref: (j_ref[0], h, i, 0)))
    return _ordered_call(
        body, name=name, out_shape=jax.ShapeDtypeStruct((N_CHIPS, nh, r, cols), out_dtype), grid_spec=grid_spec,
        compiler_params=_params(("parallel", "parallel")),
    )(j_arr, x4)


def _adamw(w, g, m, v, name):
    rows, cols = w.shape
    tr = _row_block(rows, cols, 1 << 18)

    def body(w_ref, g_ref, m_ref, v_ref, go_ref, d_ref, nm_ref, nv_ref):
        gv = g_ref[...]
        go_ref[...] = gv
        mn = ADAM_B1 * m_ref[...] + (1.0 - ADAM_B1) * gv
        vn = ADAM_B2 * v_ref[...] + (1.0 - ADAM_B2) * (gv * gv)
        m_hat = mn / (1.0 - ADAM_B1 ** ADAM_STEP)
        v_hat = vn / (1.0 - ADAM_B2 ** ADAM_STEP)
        d_ref[...] = -ADAM_LR * (m_hat / (jnp.sqrt(v_hat) + ADAM_EPS) + ADAM_WD * w_ref[...])
        nm_ref[...] = mn
        nv_ref[...] = vn

    sds = jax.ShapeDtypeStruct((rows, cols), F32)
    return _ordered_call(
        body, name=name, out_shape=(sds, sds, sds, sds), grid=(rows // tr,),
        in_specs=[_rows(cols, tr)] * 4, out_specs=(_rows(cols, tr),) * 4, compiler_params=_params(("parallel",)),
    )(w, g, m, v)


def _chip_sum(p4, recv3, j_arr, c_arr, name):
    _, rh, cols = p4.shape
    tr = _row_block(rh, cols, 1 << 19)

    def body(j_ref, c_ref, p_ref, r_ref, o_ref):
        total = p_ref[...].astype(F32)
        for peer in range(3):
            total = total + r_ref[peer].astype(F32)
        o_ref[...] = total.astype(BF16)

    grid_spec = pltpu.PrefetchScalarGridSpec(
        num_scalar_prefetch=2, grid=(rh // tr,),
        in_specs=[pl.BlockSpec((None, tr, cols), lambda i, j_ref, c_ref: (j_ref[0], i, 0)),
                  pl.BlockSpec((3, tr, cols), lambda i, j_ref, c_ref: (0, i, 0))],
        out_specs=pl.BlockSpec((None, tr, cols), lambda i, j_ref, c_ref: (c_ref[0], i, 0)))
    return _ordered_call(
        body, name=name, out_shape=jax.ShapeDtypeStruct((2, rh, cols), BF16), grid_spec=grid_spec,
        compiler_params=_params(("parallel",)),
    )(j_arr, c_arr, p4, recv3)


def _adamw_layer(w, g, m, v, layer, into, name):
    nl, rows, cols = w.shape
    slabs, _, width = g.shape
    assert slabs * width == cols and g.shape[1] == rows, (name, w.shape, g.shape)
    tr = _row_block(rows, width, 1 << 19)
    at_layer = pl.BlockSpec((None, tr, width), lambda h, i: (layer, i, h))

    def body(w_ref, g_ref, m_ref, v_ref, *rest):
        go_ref, d_ref, nm_ref, nv_ref = rest[-4:]
        gv = g_ref[...].astype(F32)
        go_ref[...] = gv
        mn = ADAM_B1 * m_ref[...] + (1.0 - ADAM_B1) * gv
        vn = ADAM_B2 * v_ref[...] + (1.0 - ADAM_B2) * (gv * gv)
        m_hat = mn / (1.0 - ADAM_B1 ** ADAM_STEP)
        v_hat = vn / (1.0 - ADAM_B2 ** ADAM_STEP)
        d_ref[...] = -ADAM_LR * (m_hat / (jnp.sqrt(v_hat) + ADAM_EPS) + ADAM_WD * w_ref[...])
        nm_ref[...] = mn
        nv_ref[...] = vn

    in_specs = [at_layer, pl.BlockSpec((None, tr, width), lambda h, i: (h, i, 0)), at_layer, at_layer]
    operands = [w, g, m, v]
    aliases = {}
    if into is not None:
        in_specs += [ANY] * 4
        operands += list(into)
        aliases = {4 + i: i for i in range(4)}
    sds = jax.ShapeDtypeStruct((nl, rows, cols), F32)
    return _ordered_call(
        body, name=name, out_shape=(sds,) * 4, grid=(slabs, rows // tr), in_specs=in_specs, out_specs=(at_layer,) * 4,
        input_output_aliases=aliases, compiler_params=_params(("parallel", "parallel")),
    )(*operands)


def _sum_devices(mine, landed, me_arr, name):
    rows, lanes = mine.shape

    def body(me_ref, mine_ref, landed_ref, o_ref):
        total = None
        for dev in range(8):
            part = jnp.where(me_ref[0] == dev, mine_ref[...], landed_ref[dev])
            total = part if total is None else total + part
        o_ref[...] = total

    grid_spec = pltpu.PrefetchScalarGridSpec(
        num_scalar_prefetch=1, grid=(1,),
        in_specs=[pl.BlockSpec((rows, lanes), lambda i, me_ref: (0, 0)), pl.BlockSpec((8, rows, lanes), lambda i, me_ref: (0, 0, 0))],
        out_specs=pl.BlockSpec((rows, lanes), lambda i, me_ref: (0, 0)))
    return _ordered_call(
        body, name=name, out_shape=jax.ShapeDtypeStruct((rows, lanes), F32), grid_spec=grid_spec,
        compiler_params=_params(("arbitrary",)),
    )(me_arr, mine, landed)


def _place():
    x, y, c = lax.axis_index("x"), lax.axis_index("y"), lax.axis_index("c")
    chips = [(1 - x, y), (x, 1 - y), (1 - x, 1 - y)]
    return x, y, c, chips


HBM = pl.BlockSpec(memory_space=pltpu.HBM)
SEM = pl.BlockSpec(memory_space=pltpu.SEMAPHORE)
TOKEN = jax.ShapeDtypeStruct((8, 128), F32)


def _remote(src, dst, send_sem, recv_sem, to):
    return pltpu.make_async_remote_copy(src_ref=src, dst_ref=dst, send_sem=send_sem, recv_sem=recv_sem, device_id=to,
                                        device_id_type=MESH)


def _split_call(body, name, thru, sems_in=(), fresh=(), new_sems=(), after_last=True):
    n_t, n_s, n_f = len(thru), len(sems_in), len(fresh)

    def call_body(*refs):
        outs = refs[n_t + n_s:]
        body(refs[:n_t], refs[n_t:n_t + n_s], outs[1 + n_t:1 + n_t + n_f], outs[1 + n_t + n_f:])
        outs[0][...] = jnp.zeros_like(outs[0])

    out_shape = ([TOKEN] + [pltpu.HBM(t.shape, t.dtype) for t in thru] + [pltpu.HBM(shp, dt) for shp, dt in fresh]
                 + [pltpu.SemaphoreType.DMA(shp) for shp in new_sems])
    out_specs = [pl.BlockSpec(memory_space=pltpu.VMEM)] + [HBM] * (n_t + n_f) + [SEM] * len(new_sems)
    if not after_last or any(t is _Order.last for t in thru):
        _Order.last = None
    out = _ordered_call(
        call_body, name=name, out_shape=tuple(out_shape), in_specs=[HBM] * n_t + [SEM] * n_s, out_specs=tuple(out_specs),
        input_output_aliases={i: 1 + i for i in range(n_t)},
        compiler_params=pltpu.CompilerParams(has_side_effects=pltpu.SideEffectType.DATAFLOW_SIDE_EFFECTING),
    )(*[pltpu.with_memory_space_constraint(t, pltpu.HBM) for t in thru], *sems_in)
    return out[1:1 + n_t], out[1 + n_t:1 + n_t + n_f], out[1 + n_t + n_f:]


class _Exchange:
    def __init__(self, weights, m_in, v_in, j_arr, c_arr, me_arr):
        self.w, self.m, self.v = weights, m_in, v_in
        self.j_arr, self.c_arr, self.me_arr = j_arr, c_arr, me_arr
        self.adam, self.small, self.pairs, self.held = {}, {}, {}, None
        self.o_arr = 1 - c_arr
        self.groups = [(l, name) for l in range(DEPTH) for name in BIG_NAMES]
        self.shard_shape = {name: weights[name].shape[1:] for name in BIG_NAMES}
        self.conv_state, self.state = [], {}
        self.ready, self.conv_ready = {}, {}
        self.pending, self.tick, self.reduced = [], 0, {}

        def place(grp):
            l, name = grp
            nl, r, cols = weights[name].shape
            return _place_shard(weights[name].reshape(nl, 2, r // 2, cols), l, j_arr, BF16, f"place_{name}_l{l}")

        def start_copies(tag, convs, groups, bufs):
            n_c = len(convs)

            def start(thru, _, __, sems):
                x, y, c, chips = _place()
                j_me = 2 * x + y
                copies = []
                for i in range(len(thru)):
                    mine = thru[i].at[j_me] if i < n_c else thru[i].at[j_me, c]
                    copies += [_remote(mine, mine, sems[2 * i].at[k], sems[2 * i + 1].at[k], (*chip, c))
                               for k, chip in enumerate(chips)]
                for cp in copies:
                    cp.start()

            thru, _, sems = _split_call(start, tag, convs + bufs, new_sems=[(3,)] * (2 * (n_c + len(bufs))))
            self.conv_state += [(thru[i], sems[2 * i], sems[2 * i + 1]) for i in range(n_c)]
            for g, grp in enumerate(groups):
                self.state[grp] = (thru[n_c + g], sems[2 * (n_c + g)], sems[2 * (n_c + g) + 1])

        convs = [_place_shard(weights["conv_w"][:, None], l, j_arr, F32, f"place_conv_w_l{l}") for l in range(DEPTH)]
        start_copies("gather_start_first", convs, self.groups[:1], [place(self.groups[0])])
        start_copies("gather_start_rest", [], self.groups[1:], [place(grp) for grp in self.groups[1:]])

    def conv_w(self, l):
        if l not in self.conv_ready:
            buf, send, recv = self.conv_state[l]

            def wait(thru, sems, _, __):
                x, y, c, chips = _place()
                for k, chip in enumerate(chips):
                    mine, theirs = thru[0].at[2 * x + y], thru[0].at[2 * chip[0] + chip[1]]
                    _remote(mine, mine, sems[0].at[k], sems[1].at[k], (*chip, c)).wait_send()
                    _remote(theirs, theirs, sems[0].at[k], sems[1].at[k], (x, y, c)).wait_recv()

            (buf,), _, _ = _split_call(wait, f"gather_conv_w_l{l}", [buf], sems_in=[send, recv])
            self.conv_ready[l] = jnp.transpose(buf[:, 0], (1, 0, 2)).reshape(3, 2 * D_FF)
        return self.conv_ready[l]

    def weight(self, l, name):
        grp = (l, name)
        if grp not in self.ready:
            buf, send, recv = self.state[grp]

            def forward(thru, sems, _, new):
                x, y, c, chips = _place()
                for k, chip in enumerate(chips):
                    landed = thru[0].at[2 * chip[0] + chip[1], c]
                    _remote(landed, landed, new[0].at[k], sems[0].at[k], (x, y, c)).wait_recv()
                    _remote(landed, landed, new[0].at[k], new[1].at[k], (x, y, 1 - c)).start()

            (buf,), _, (fsend, frecv) = _split_call(forward, f"gather_pass_{name}_l{l}", [buf], sems_in=[recv],
                                                    new_sems=[(3,), (3,)])

            def finish(thru, sems, _, __):
                x, y, c, chips = _place()
                mine = thru[0].at[2 * x + y, c]
                for k, chip in enumerate(chips):
                    j_k = 2 * chip[0] + chip[1]
                    theirs, landed = thru[0].at[j_k, 1 - c], thru[0].at[j_k, c]
                    _remote(theirs, theirs, sems[1].at[k], sems[2].at[k], (x, y, c)).wait_recv()
                    _remote(landed, landed, sems[1].at[k], sems[2].at[k], (x, y, 1 - c)).wait_send()
                    _remote(mine, mine, sems[0].at[k], sems[2].at[k], (*chip, c)).wait_send()

            (buf,), _, _ = _split_call(finish, f"gather_done_{name}_l{l}", [buf], sems_in=[send, fsend, frecv])
            r, cols = self.shard_shape[name]
            self.ready[grp] = buf.reshape(N_CHIPS, r, cols) if name in ("w_in", "w_up") else buf.reshape(N_CHIPS * r, cols)
        return self.ready[grp]

    def pair_send(self, l, name, other):
        held = self.held
        self.held = None

        def start(thru, _, fresh, sems):
            x, y, c, chips = _place()
            copies = [_remote(thru[0], fresh[0], sems[0], sems[1], (x, y, 1 - c))]
            if held is not None:
                copies += [_remote(thru[1].at[2 * chip[0] + chip[1]], fresh[1].at[k], sems[2].at[k], sems[3].at[k], (*chip, c))
                           for k, chip in enumerate(chips)]
            for cp in copies:
                cp.start()

        thru, fresh, new_sems = [other], [(other.shape, BF16)], [(), ()]
        if held is not None:
            thru, fresh, new_sems = thru + [held[2]], fresh + [((3,) + held[2].shape[1:], BF16)], new_sems + [(3,), (3,)]
        thru, fresh, sems = _split_call(start, f"pair_start_{name}_l{l}", thru, fresh=fresh, new_sems=new_sems, after_last=False)
        self.pairs[(l, name)] = (thru[0], fresh[0], sems[:2])
        if held is not None:
            self.pending.append(dict(l=held[0], name=held[1], stage=2, at=self.tick, bufs=(thru[1], fresh[1]), sems=sems[2:]))

    def pair_recv(self, l, name):
        other, recv, sems = self.pairs.pop((l, name))

        def wait(thru, sems, _, __):
            x, y, c, _chips = _place()
            cp = _remote(thru[0], thru[1], sems[0], sems[1], (x, y, 1 - c))
            cp.wait_send()
            cp.wait_recv()

        (_, recv), _, _ = _split_call(wait, f"pair_done_{name}_l{l}", [other, recv], sems_in=list(sems))
        return recv

    def scatter(self, l, name, p4):
        assert self.held is None
        self.held = (l, name, p4)
        if (l, name) == (0, BIG_NAMES[0]):
            self._scatter_held()

    def _scatter_held(self):
        l, name, p4 = self.held
        self.held = None

        def start(thru, _, fresh, sems):
            x, y, c, chips = _place()
            for k, chip in enumerate(chips):
                _remote(thru[0].at[2 * chip[0] + chip[1]], fresh[0].at[k], sems[0].at[k], sems[1].at[k], (*chip, c)).start()

        (p4,), (recv3,), sems = _split_call(start, f"chips_start_{name}_l{l}", [p4], fresh=[((3,) + p4.shape[1:], BF16)],
                                           new_sems=[(3,), (3,)], after_last=False)
        self.pending.append(dict(l=l, name=name, stage=2, at=self.tick, bufs=(p4, recv3), sems=sems))

    def point(self, drain=False):
        self.tick += 1
        if drain:
            old = [g for g in self.pending if g["stage"] == 2 and g["at"] + 2 <= self.tick]
            new = [g for g in self.pending if g["stage"] == 2 and g["at"] + 2 > self.tick]
            for grp in old + [g for g in self.pending if g["stage"] == 3] + new:
                self._advance([grp] if grp["stage"] == 3 else [], [grp] if grp["stage"] == 2 else [])
        else:
            self._advance([grp for grp in self.pending if grp["stage"] == 3 and grp["at"] < self.tick],
                          [grp for grp in self.pending if grp["stage"] == 2 and grp["at"] + 2 <= self.tick])

    def _advance(self, joined, landed):
        if not joined and not landed:
            return
        n_j, n_l = len(joined), len(landed)

        def wait(thru, sems, _, __):
            x, y, c, chips = _place()
            for i in range(n_j):
                buf, send, recv = thru[i], sems[2 * i], sems[2 * i + 1]
                _remote(buf.at[c], buf.at[c], send, recv, (x, y, 1 - c)).wait_send()
                _remote(buf.at[1 - c], buf.at[1 - c], send, recv, (x, y, c)).wait_recv()
            for i in range(n_l):
                p4, recv3 = thru[n_j + 2 * i], thru[n_j + 2 * i + 1]
                send, recv = sems[2 * (n_j + i)], sems[2 * (n_j + i) + 1]
                for k, chip in enumerate(chips):
                    cp = _remote(p4.at[2 * chip[0] + chip[1]], recv3.at[k], send.at[k], recv.at[k], (*chip, c))
                    cp.wait_send()
                    cp.wait_recv()

        tag = "_".join([f"{grp['name']}{grp['l']}_halves" for grp in joined] + [f"{grp['name']}{grp['l']}_chips" for grp in landed])
        bufs, _, _ = _split_call(wait, f"landed_{tag}", [b for grp in joined + landed for b in grp["bufs"]],
                                 sems_in=[sm for grp in joined + landed for sm in grp["sems"]])
        for i, grp in enumerate(joined):
            l, name, full = grp["l"], grp["name"], bufs[i]
            if GRAD_HALVES[name][0] != "cols_of_block":
                full = full.reshape((1,) + tuple(self.shard_shape[name]))
            self.adam[name] = _adamw_layer(self.w[name], full, self.m[name], self.v[name], l, self.adam.get(name),
                                           f"adamw_{name}_l{l}")
            grp.update(stage=4)
        if not landed:
            return
        halves = [_chip_sum(bufs[n_j + 2 * i], bufs[n_j + 2 * i + 1], self.j_arr, self.c_arr,
                            f"chip_sum_{grp['name']}_l{grp['l']}") for i, grp in enumerate(landed)]

        def start(thru, _, __, sems):
            x, y, c, _chips = _place()
            for i in range(n_l):
                _remote(thru[i].at[c], thru[i].at[c], sems[2 * i], sems[2 * i + 1], (x, y, 1 - c)).start()

        tag = "_".join(f"{grp['name']}{grp['l']}" for grp in landed)
        halves, _, sems = _split_call(start, f"join_start_{tag}", halves, new_sems=[()] * (2 * n_l), after_last=False)
        for i, grp in enumerate(landed):
            grp.update(stage=3, at=self.tick, bufs=(halves[i],), sems=tuple(sems[2 * i:2 * i + 2]))

    def finish(self):
        if self.held is not None:
            self._scatter_held()
        while any(grp["stage"] < 4 for grp in self.pending):
            self.point(drain=True)
        return self.adam

    @staticmethod
    def _peer(k, x, y, c):
        return (1 - x if k & 4 else x, 1 - y if k & 2 else y, 1 - c if k & 1 else c)

    def small_grads(self, l, grads, loss_tile):
        parts = [grads[nm] for nm in SMALL_NAMES] + ([loss_tile[0, 0:1]] if loss_tile is not None else [])
        packed = _pack_call(parts, f"small_pack_l{l}")
        rows = packed.shape[0]

        def start(thru, _, fresh, sems):
            x, y, c, _chips = _place()
            for k in range(1, 8):
                _remote(thru[0], fresh[0].at[4 * x + 2 * y + c], sems[0].at[k - 1], sems[1].at[k - 1],
                        self._peer(k, x, y, c)).start()

        (packed,), (landed,), sems = _split_call(start, f"small_start_l{l}", [packed], fresh=[((8, rows, PACK_LANES), F32)],
                                                 new_sems=[(7,), (7,)], after_last=False)
        self.small[l] =(packed, landed, sems, [p.shape for p in parts])

    def small_sum(self, l):
        packed, landed, sems, _shapes = self.small[l]

        def wait(thru, sems, _, __):
            x, y, c, _chips = _place()
            for k in range(1, 8):
                px, py, pc = self._peer(k, x, y, c)
                _remote(thru[0], thru[1].at[4 * x + 2 * y + c], sems[0].at[k - 1], sems[1].at[k - 1], (px, py, pc)).wait_send()
                _remote(thru[0], thru[1].at[4 * px + 2 * py + pc], sems[0].at[k - 1], sems[1].at[k - 1], (x, y, c)).wait_recv()

        (packed, landed), _, _ = _split_call(wait, f"small_done_l{l}", [packed, landed], sems_in=list(sems))
        return _sum_devices(packed, landed, self.me_arr, f"small_sum_l{l}")


def _rope_tables(s):
    inv_freq = ROPE_THETA ** (-jnp.arange(0, HEAD_DIM, 2, dtype=F32) / HEAD_DIM)
    ang = jnp.arange(s, dtype=F32)[:, None] * inv_freq[None, :]
    cos, sin = jnp.cos(ang), jnp.sin(ang)
    return jnp.concatenate([cos, cos], axis=-1), jnp.concatenate([-sin, sin], axis=-1)


def _local_step(x, target, ex, small):
    s = x.shape[0]
    cosf, sinf = _rope_tables(s)
    saved = []
    for l in range(DEPTH):
        p = small[l]
        t = f"l{l}"
        h = _rms_fwd(x, p["norm1_g"], f"norm1_{t}")
        z = _matmul(h, ex.weight(l, "w_in"), mode="nn", out_dtype=BF16, tm=1024, tn=896, tk=2048, b_parts=4, name=f"proj_in_{t}")
        qn, kn, vb, ug, vn, *gate_kept = _proj_post(z, p["q_norm_g"], p["k_norm_g"], p["sgu_ln_g"], p["sgu_ln_b"], cosf, sinf,
                                                    f"proj_post_{t}")
        attn, sgu, mixed, probs, psink = _mixer_fwd(qn, kn, vb, ug, vn, p["w_s_bf16"], p["b_s_tile"], p["sink"],
                                                    p["attn_out_g"], p["sgu_out_g"], f"mixer_{t}")
        x1 = _matmul(mixed, ex.weight(l, "w_o"), mode="nn", out_dtype=F32, tm=2048, tn=256, tk=2048, res=x,
                     name=f"proj_out_{t}")
        h2 = _rms_fwd(x1, p["norm2_g"], f"norm2_{t}")
        a_pre = _matmul(h2, ex.weight(l, "w_up"), mode="nn", out_dtype=BF16, tm=1024, tn=1408, tk=2048, b_parts=4,
                        out_parts=2,
                        name=f"ffn_up_{t}")
        act, dgu = _conv_gate_fwd(a_pre, ex.conv_w(l), p["conv_b"], f"conv_gate_{t}")
        x2 = _matmul(act, ex.weight(l, "w_down"), mode="nn", out_dtype=F32, tm=1024, tn=256, tk=D_FF, res=x1,
                     name=f"ffn_down_{t}")
        saved.append(dict(x=x, h=h, z=z, qn=qn, kn=kn, vb=vb, ug=ug, vn=vn, attn=attn, sgu=sgu, mixed=mixed, x1=x1, h2=h2,
                          a_pre=a_pre, act=act, dgu=dgu, probs=probs, psink=psink, gate_kept=gate_kept))
        x = x2
    loss_tile, dx, dxb = _loss_head(x, target, "loss_head")
    for l in reversed(range(DEPTH)):
        p, sv = small[l], saved[l]
        t = f"l{l}"
        def weight_grad(name, a, g, between, g_parts=0):
            ex.pair_send(l, name, _grad_half(name, a, g, ex.o_arr, None, f"g_{name}_other_{t}", g_parts))
            out = between()
            ex.scatter(l, name, _grad_half(name, a, g, ex.c_arr, ex.pair_recv(l, name), f"g_{name}_own_{t}", g_parts))
            ex.point()
            return out

        def after_down():
            dact = _matmul(dxb, ex.weight(l, "w_down"), mode="nt", out_dtype=BF16, tm=1024, tn=512, tk=2048,
                           name=f"d_act_{t}")
            return _conv_gate_bwd(sv["a_pre"], sv["dgu"], ex.conv_w(l), dact, f"conv_gate_bwd_{t}")

        dap, dcw, dcb = weight_grad("w_down", sv["act"], dxb, after_down)

        def after_up():
            dh2 = _matmul(dap, ex.weight(l, "w_up"), mode="nt", out_dtype=BF16, tm=1024, tn=1024, tk=2816, a_parts=2,
                          b_parts=4, name=f"d_h2_{t}")
            return _rms_bwd(sv["x1"], p["norm2_g"], dh2, dx, f"norm2_bwd_{t}")

        dx1, dx1b, dg2 = weight_grad("w_up", sv["h2"], dap, after_up, g_parts=2)
        ex.pair_send(l, "w_o", _grad_half("w_o", sv["mixed"], dx1b, ex.o_arr, None, f"g_w_o_other_{t}"))
        dmixed = _matmul(dx1b, ex.weight(l, "w_o"), mode="nt", out_dtype=BF16, tm=1024, tn=512, tk=2048,
                         name=f"d_mixed_{t}")
        dqn, dkn, dvb, dug, dvn, dws, dbs, dsk, dga, dgs = _mixer_bwd(
            sv["qn"], sv["kn"], sv["vb"], sv["ug"], sv["vn"], sv["attn"], sv["sgu"], dmixed, p["w_s_bf16"], p["b_s_tile"],
            p["attn_out_g"], p["sgu_out_g"], sv["probs"], sv["psink"], f"mixer_bwd_{t}")
        dz, dqg, dkg, dlg, dlb = _proj_post_bwd(sv["z"], dqn, dkn, dvb, dug, dvn, *sv["gate_kept"], p["q_norm_g"], p["k_norm_g"],
                                                 p["sgu_ln_g"], cosf, sinf, f"proj_post_bwd_{t}")
        ex.scatter(l, "w_o", _grad_half("w_o", sv["mixed"], dx1b, ex.c_arr, ex.pair_recv(l, "w_o"), f"g_w_o_own_{t}"))
        ex.point()

        def after_in():
            dh = _matmul_nt_slabs(dz, ex.weight(l, "w_in"), tm=1024, tn=512, name=f"d_h_{t}")
            return _rms_bwd(sv["x"], p["norm1_g"], dh, dx1, f"norm1_bwd_{t}")

        dx, dxb, dg1 = weight_grad("w_in", sv["h"], dz, after_in)
        ex.small_grads(l, dict(
            norm1_g=dg1[0], q_norm_g=dqg[0], k_norm_g=dkg[0], sink=dsk[:, 0], sgu_ln_g=dlg[0], sgu_ln_b=dlb[0], w_s=dws,
            b_s=dbs[:, :, 0], attn_out_g=dga[0], sgu_out_g=dgs[0], norm2_g=dg2[0],
            conv_w=jnp.concatenate([dcw[0], dcw[1]], axis=-1), conv_b=jnp.concatenate([dcb[0, 0], dcb[1, 0]], axis=-1)),
            loss_tile if l == 0 else None)
    return dx


def _small_views(l, norm1_g, q_norm_g, k_norm_g, sink, sgu_ln_g, sgu_ln_b, w_s, b_s, attn_out_g, sgu_out_g, norm2_g, conv_b):
    return dict(
        norm1_g=norm1_g[l][None], q_norm_g=q_norm_g[l][None], k_norm_g=k_norm_g[l][None], sink=sink[l],
        sgu_ln_g=sgu_ln_g[l][None], sgu_ln_b=sgu_ln_b[l][None], w_s_bf16=w_s[l].astype(BF16),
        b_s_tile=jnp.broadcast_to(b_s[l][:, :, None], (N_GMLP_HEADS, BLOCK, BLOCK)), attn_out_g=attn_out_g[l][None],
        sgu_out_g=sgu_out_g[l][None], norm2_g=norm2_g[l][None], conv_b=conv_b[l][None])


SMALL_NAMES = ("norm1_g", "q_norm_g", "k_norm_g", "sink", "sgu_ln_g", "sgu_ln_b", "w_s", "b_s", "attn_out_g", "sgu_out_g",
               "norm2_g", "conv_b", "conv_w")
REPLICATED_NAMES = SMALL_NAMES[:-1]
BIG_NAMES = ("w_in", "w_o", "w_up", "w_down")
PACK_LANES = 128
PACK_ALIGN = 8 * PACK_LANES


def _pack_rows(shape):
    return -(-math.prod(shape) // PACK_ALIGN) * 8


def _pack_parts(arrays):
    parts = []
    for a in arrays:
        flat = a.reshape(-1)
        parts.append(jnp.pad(flat, (0, _pack_rows(a.shape) * PACK_LANES - flat.shape[0])).reshape(-1, PACK_LANES))
    return parts


def _pack_call(arrays, name):
    parts = _pack_parts(arrays)
    total = sum(p.shape[0] for p in parts)

    def body(*refs):
        o_ref, at = refs[-1], 0
        for p_ref in refs[:-1]:
            o_ref[at:at + p_ref.shape[0], :] = p_ref[...]
            at += p_ref.shape[0]

    vm = pl.BlockSpec(memory_space=pltpu.VMEM)
    return _ordered_call(
        body, name=name, out_shape=jax.ShapeDtypeStruct((total, PACK_LANES), F32), in_specs=[vm] * len(parts), out_specs=vm,
        compiler_params=pltpu.CompilerParams(vmem_limit_bytes=V7X_VMEM_LIMIT),
    )(*parts)


def _unpack_layers(stacked, shapes):
    nl = stacked.shape[0]
    out, at = [], 0
    for shp in shapes:
        rows = _pack_rows(shp)
        out.append(stacked[:, at:at + rows].reshape(nl, -1)[:, :math.prod(shp)].reshape((nl,) + tuple(shp)))
        at += rows
    return out


def _adamw_packed(w, g, m, v, rows, layer, into, name):
    head = pl.BlockSpec((rows, PACK_LANES), lambda i: (0, 0))
    at_layer = pl.BlockSpec((None, rows, PACK_LANES), lambda i: (layer, 0, 0))

    def body(w_ref, g_ref, m_ref, v_ref, *rest):
        d_ref, nm_ref, nv_ref = rest[-3:]
        gv = g_ref[...]
        mn = ADAM_B1 * m_ref[...] + (1.0 - ADAM_B1) * gv
        vn = ADAM_B2 * v_ref[...] + (1.0 - ADAM_B2) * (gv * gv)
        m_hat = mn / (1.0 - ADAM_B1 ** ADAM_STEP)
        v_hat = vn / (1.0 - ADAM_B2 ** ADAM_STEP)
        d_ref[...] = -ADAM_LR * (m_hat / (jnp.sqrt(v_hat) + ADAM_EPS) + ADAM_WD * w_ref[...])
        nm_ref[...] = mn
        nv_ref[...] = vn

    in_specs = [head] * 4
    operands = [w, g, m, v]
    aliases = {}
    if into is not None:
        in_specs += [ANY] * 3
        operands += list(into)
        aliases = {4 + i: i for i in range(3)}
    sds = jax.ShapeDtypeStruct((DEPTH, rows, PACK_LANES), F32)
    return _ordered_call(
        body, name=name, out_shape=(sds,) * 3, grid=(1,), in_specs=in_specs, out_specs=(at_layer,) * 3,
        input_output_aliases=aliases, compiler_params=_params(("arbitrary",)),
    )(*operands)


def kernel(x, norm1_g, w_in, q_norm_g, k_norm_g, sink, sgu_ln_g, sgu_ln_b, w_s, b_s, attn_out_g, sgu_out_g, w_o, norm2_g, w_up, conv_w, conv_b, w_down, loss_target, m_norm1_g, m_w_in, m_q_norm_g, m_k_norm_g, m_sink, m_sgu_ln_g, m_sgu_ln_b, m_w_s, m_b_s, m_attn_out_g, m_sgu_out_g, m_w_o, m_norm2_g, m_w_up, m_conv_w, m_conv_b, m_w_down, v_norm1_g, v_w_in, v_q_norm_g, v_k_norm_g, v_sink, v_sgu_ln_g, v_sgu_ln_b, v_w_s, v_b_s, v_attn_out_g, v_sgu_out_g, v_w_o, v_norm2_g, v_w_up, v_conv_w, v_conv_b, v_w_down):
    weights = dict(norm1_g=norm1_g, w_in=w_in, q_norm_g=q_norm_g, k_norm_g=k_norm_g, sink=sink, sgu_ln_g=sgu_ln_g,
                   sgu_ln_b=sgu_ln_b, w_s=w_s, b_s=b_s, attn_out_g=attn_out_g, sgu_out_g=sgu_out_g, w_o=w_o, norm2_g=norm2_g,
                   w_up=w_up, conv_w=conv_w, conv_b=conv_b, w_down=w_down)
    m_in = dict(norm1_g=m_norm1_g, w_in=m_w_in, q_norm_g=m_q_norm_g, k_norm_g=m_k_norm_g, sink=m_sink, sgu_ln_g=m_sgu_ln_g,
                sgu_ln_b=m_sgu_ln_b, w_s=m_w_s, b_s=m_b_s, attn_out_g=m_attn_out_g, sgu_out_g=m_sgu_out_g, w_o=m_w_o,
                norm2_g=m_norm2_g, w_up=m_w_up, conv_w=m_conv_w, conv_b=m_conv_b, w_down=m_w_down)
    v_in = dict(norm1_g=v_norm1_g, w_in=v_w_in, q_norm_g=v_q_norm_g, k_norm_g=v_k_norm_g, sink=v_sink, sgu_ln_g=v_sgu_ln_g,
                sgu_ln_b=v_sgu_ln_b, w_s=v_w_s, b_s=v_b_s, attn_out_g=v_attn_out_g, sgu_out_g=v_sgu_out_g, w_o=v_w_o,
                norm2_g=v_norm2_g, w_up=v_w_up, conv_w=v_conv_w, conv_b=v_conv_b, w_down=v_w_down)
    cx, cy, cc = lax.axis_index("x"), lax.axis_index("y"), lax.axis_index("c")
    j_me = 2 * cx + cy
    c_arr = jnp.reshape(cc, (1,)).astype(jnp.int32)
    j_arr = jnp.reshape(j_me, (1,)).astype(jnp.int32)

    _Order.last = None
    ex = _Exchange(weights, m_in, v_in, j_arr, c_arr, jnp.reshape(4 * cx + 2 * cy + cc, (1,)).astype(jnp.int32))
    small = [_small_views(l, norm1_g, q_norm_g, k_norm_g, sink, sgu_ln_g, sgu_ln_b, w_s, b_s, attn_out_g, sgu_out_g, norm2_g,
                          conv_b) for l in range(DEPTH)]
    held_back, _ = lax.optimization_barrier(([[src[nm] for nm in REPLICATED_NAMES] for src in (weights, m_in, v_in)], _Order.last))
    packed_in = [[_pack_call([arr[l] for arr in arrays], f"pack_{tag}_l{l}") for tag, arrays in zip("wmv", held_back)]
                 for l in range(DEPTH)]
    dx = _local_step(x[0], loss_target[0], ex, small)
    big_out = ex.finish()

    rep_shapes = [weights[nm].shape[1:] for nm in REPLICATED_NAMES]
    rep_rows = sum(_pack_rows(shp) for shp in rep_shapes)
    cw_shape = (3, 2 * D_FF)
    sums, adam_small = [None] * DEPTH, None
    for l in reversed(range(DEPTH)):
        sums[l] = ex.small_sum(l)
        pw, pm, pv = packed_in[l]
        adam_small = _adamw_packed(pw, sums[l], pm, pv, rep_rows, l, adam_small, f"adamw_small_l{l}")
    cw_rows = _pack_rows(cw_shape)
    loss = sums[0][rep_rows + cw_rows, 0]
    stacked = jnp.stack([sm[:rep_rows + cw_rows] for sm in sums])
    grads = dict(zip(REPLICATED_NAMES, _unpack_layers(stacked[:, :rep_rows], rep_shapes)))
    delta, new_m, new_v = (dict(zip(REPLICATED_NAMES, _unpack_layers(arr, rep_shapes))) for arr in adam_small)
    cw_cols = 2 * D_FF // N_CHIPS
    cw_grad = lax.dynamic_slice_in_dim(_unpack_layers(stacked[:, rep_rows:], [cw_shape])[0], j_me * cw_cols, cw_cols, axis=2)
    flat = lambda a: a.reshape(DEPTH * 3, cw_cols)
    cw_out = _adamw(flat(conv_w), flat(cw_grad), flat(m_conv_w), flat(v_conv_w), "adamw_conv_w")
    grads["conv_w"], delta["conv_w"], new_m["conv_w"], new_v["conv_w"] = (a.reshape(DEPTH, 3, cw_cols) for a in cw_out)

    for name in BIG_NAMES:
        grads[name], delta[name], new_m[name], new_v[name] = big_out[name]

    order = ("norm1_g", "w_in", "q_norm_g", "k_norm_g", "sink", "sgu_ln_g", "sgu_ln_b", "w_s", "b_s", "attn_out_g", "sgu_out_g",
             "w_o", "norm2_g", "w_up", "conv_w", "conv_b", "w_down")
    return (loss, dx[None], *[grads[nm] for nm in order], *[delta[nm] for nm in order], *[new_m[nm] for nm in order],
            *[new_v[nm] for nm in order])
```

```python
import math

import jax
import jax.numpy as jnp
from jax import lax
from jax.experimental import pallas as pl
from jax.experimental.pallas import tpu as pltpu

F32 = jnp.float32
BF16 = jnp.bfloat16

D_MODEL = 2048
HEAD_DIM = 128
ATTN_WIDTH = 1024
N_Q_HEADS = 8
N_KV_HEADS = 2
GQA_GROUP = 4
KV_WIDTH = 256
GMLP_WIDTH = 1024
N_GMLP_HEADS = 8
BLOCK = 128
IN_WIDTH = 3584
D_FF = 5632
DEPTH = 2
EPS = 1e-6
MASK_VALUE = -1e30
ROPE_THETA = 10000.0
N_CHIPS = 4

ADAM_LR = 0.001
ADAM_B1 = 0.9
ADAM_B2 = 0.999
ADAM_EPS = 1e-08
ADAM_WD = 0.01
ADAM_STEP = 10

V7X_VMEM_LIMIT = 48 * 1024 * 1024
MESH = pl.DeviceIdType.MESH

_GELU_C = math.sqrt(2.0 / math.pi)
_GELU_A = 0.044715


def _params(sem=None):
    return pltpu.CompilerParams(dimension_semantics=sem, vmem_limit_bytes=V7X_VMEM_LIMIT)


ANY = pl.BlockSpec(memory_space=pl.ANY)


class _Order:
    last = None


def _ordered_call(body, *, token_index=0, **kw):
    def run(*operands):
        tok = _Order.last
        if tok is None or any(op is tok for op in operands):
            call = pl.pallas_call(body, **kw)
        else:
            n_in = len(operands)

            def ordered_body(*refs):
                return body(*refs[:n_in], *refs[n_in + 1:])

            kw2 = dict(kw)
            if "grid_spec" in kw2:
                gs = kw2["grid_spec"]
                kw2["grid_spec"] = pltpu.PrefetchScalarGridSpec(
                    num_scalar_prefetch=gs.num_scalar_prefetch, grid=gs.grid, in_specs=list(gs.in_specs) + [ANY],
                    out_specs=gs.out_specs, scratch_shapes=gs.scratch_shapes)
            else:
                kw2["in_specs"] = list(kw2["in_specs"]) + [ANY]
            call = pl.pallas_call(ordered_body, **kw2)
            operands = operands + (tok,)
        out = call(*operands)
        _Order.last = out[token_index] if isinstance(out, (tuple, list)) else out
        return out

    return run


def _gelu(x):
    return x * (0.5 * (1.0 + jnp.tanh(_GELU_C * (x + _GELU_A * (x * x * x)))))


def _gelu_grad(x):
    x2 = x * x
    t = jnp.tanh(_GELU_C * (x + _GELU_A * (x * x2)))
    return 0.5 * (1.0 + t) + 0.5 * x * (1.0 - t * t) * (_GELU_C * (1.0 + 3.0 * _GELU_A * x2))


def _mean_last(x):
    return jnp.mean(x, axis=-1, keepdims=True)


def _sum_rows(x):
    return jnp.sum(x, axis=0, keepdims=True)


def _sum_all(x):
    return jnp.sum(jnp.sum(x, axis=1, keepdims=True), axis=0, keepdims=True)


def _matmul(a, b, *, mode, out_dtype, tm, tn, tk, name, res=None, a_parts=0, b_parts=0, out_parts=0):
    assert mode in ("nn", "nt"), mode
    if mode == "nn":
        assert not a_parts
        m, k = a.shape
        n = b.shape[0] * b.shape[2] if b_parts else b.shape[1]
    else:
        m, k = (a.shape[1], a.shape[0] * a.shape[2]) if a_parts else a.shape
        n = b.shape[1] if b_parts else b.shape[0]
    tm, tn, tk = min(tm, m), min(tn, n), min(tk, k)
    assert m % tm == 0 and n % tn == 0 and k % tk == 0, (name, m, n, k, tm, tn, tk)
    nm, nn, nk = m // tm, n // tn, k // tk

    def slab(idx, total_tiles, parts):
        per = total_tiles // parts
        assert per * parts == total_tiles, (name, total_tiles, parts)
        return idx // per, idx % per

    if mode == "nn":
        a_spec = pl.BlockSpec((tm, tk), lambda i, j, kk: (i, kk))
        if b_parts:
            b_spec = pl.BlockSpec((None, tk, tn), lambda i, j, kk: (slab(j, nn, b_parts)[0], kk, slab(j, nn, b_parts)[1]))
        else:
            b_spec = pl.BlockSpec((tk, tn), lambda i, j, kk: (kk, j))
        dims = (((1,), (0,)), ((), ()))
    else:
        if a_parts:
            a_spec = pl.BlockSpec((None, tm, tk), lambda i, j, kk: (slab(kk, nk, a_parts)[0], i, slab(kk, nk, a_parts)[1]))
        else:
            a_spec = pl.BlockSpec((tm, tk), lambda i, j, kk: (i, kk))
        if b_parts:
            b_spec = pl.BlockSpec((None, tn, tk), lambda i, j, kk: (slab(kk, nk, b_parts)[0], j, slab(kk, nk, b_parts)[1]))
        else:
            b_spec = pl.BlockSpec((tn, tk), lambda i, j, kk: (j, kk))
        dims = (((1,), (1,)), ((), ()))
    if out_parts:
        out_shape = jax.ShapeDtypeStruct((out_parts, m, n // out_parts), out_dtype)
        out_spec = pl.BlockSpec((None, tm, tn), lambda i, j, kk: (slab(j, nn, out_parts)[0], i, slab(j, nn, out_parts)[1]))
    else:
        out_shape = jax.ShapeDtypeStruct((m, n), out_dtype)
        out_spec = pl.BlockSpec((tm, tn), lambda i, j, kk: (i, j))
    in_specs = [a_spec, b_spec]
    operands = [a, b]
    if res is not None:
        in_specs.append(pl.BlockSpec((tm, tn), lambda i, j, kk: (i, j)))
        operands.append(res)

    def body(*refs):
        a_ref, b_ref = refs[0], refs[1]
        res_ref = refs[2] if res is not None else None
        o_ref = refs[3] if res is not None else refs[2]
        p = lax.dot_general(a_ref[...], b_ref[...], dims, preferred_element_type=F32)

        def finish(total):
            if res_ref is not None:
                total = res_ref[...] + total
            o_ref[...] = total.astype(out_dtype)

        if nk == 1:
            finish(p)
        else:
            acc_ref = refs[-1]
            kk = pl.program_id(2)

            @pl.when(kk == 0)
            def _():
                acc_ref[...] = p

            @pl.when(jnp.logical_and(kk > 0, kk < nk - 1))
            def _():
                acc_ref[...] += p

            @pl.when(kk == nk - 1)
            def _():
                finish(acc_ref[...] + p)

    scratch = [pltpu.VMEM((tm, tn), F32)] if nk > 1 else []
    return _ordered_call(
        body, name=name, out_shape=out_shape, grid=(nm, nn, nk), in_specs=in_specs, out_specs=out_spec,
        scratch_shapes=scratch, compiler_params=_params(("parallel", "parallel", "arbitrary")),
    )(*operands)


def _matmul_nt_slabs(a, b, *, tm, tn, name, a_parts=0):
    nslab, n, ks = b.shape
    m = a.shape[1] if a_parts else a.shape[0]
    tm, tn = min(tm, m), min(tn, n)
    assert m % tm == 0 and n % tn == 0, (name, m, n, tm, tn)
    if a_parts:
        per = nslab // a_parts
        assert per * a_parts == nslab and a.shape[2] == per * ks, (name, a.shape, b.shape)
        a_spec = pl.BlockSpec((a_parts, tm, per * ks), lambda i, j: (0, i, 0))
    else:
        assert a.shape[1] == nslab * ks, (name, a.shape, b.shape)
        a_spec = pl.BlockSpec((tm, nslab * ks), lambda i, j: (i, 0))

    def body(a_ref, b_ref, o_ref):
        total = None
        for sl in range(nslab):
            if a_parts:
                a_sl = a_ref[sl // per, :, (sl % per) * ks:(sl % per + 1) * ks]
            else:
                a_sl = a_ref[:, sl * ks:(sl + 1) * ks]
            p = lax.dot_general(a_sl, b_ref[sl], (((1,), (1,)), ((), ())), preferred_element_type=F32)
            total = p if total is None else total + p
        o_ref[...] = total.astype(BF16)

    return _ordered_call(
        body, name=name, out_shape=jax.ShapeDtypeStruct((m, n), BF16), grid=(m // tm, n // tn),
        in_specs=[a_spec, pl.BlockSpec((nslab, tn, ks), lambda i, j: (0, j, 0))],
        out_specs=pl.BlockSpec((tm, tn), lambda i, j: (i, j)), compiler_params=_params(("parallel", "parallel")),
    )(a, b)


GRAD_HALVES = {
    "w_in": ("rows_of_slab", 1024, 896), "w_up": ("rows_of_slab", 1024, 1408), "w_o": ("rows_of_block", 256, 2048),
    "w_down": ("cols_of_block", 1408, 512)}


def _grad_half(name, a, g, sel, res, call_name, g_parts=0):
    kind, tm, tn = GRAD_HALVES[name]
    s, m = a.shape
    n = g.shape[0] * g.shape[2] if g_parts else g.shape[1]
    if kind == "rows_of_slab":
        rh, hc = m // 2, n // N_CHIPS
        per = hc // tn
        grid = (rh // tm, n // tn)
        a_map = lambda i, j, sel_ref: (0, sel_ref[0] * (rh // tm) + i)
        g_col = lambda i, j, sel_ref: j
        o_map = lambda i, j, sel_ref: (j // per, i, j % per)
    elif kind == "rows_of_block":
        rh, hc = m // N_CHIPS // 2, n
        assert tm == rh
        grid = (N_CHIPS, n // tn)
        a_map = lambda i, j, sel_ref: (0, 2 * i + sel_ref[0])
        g_col = lambda i, j, sel_ref: j
        o_map = lambda i, j, sel_ref: (i, 0, j)
    else:
        rh, hc = m // N_CHIPS, n // 2
        assert tm == rh
        grid = (N_CHIPS, hc // tn)
        a_map = lambda i, j, sel_ref: (0, i)
        g_col = lambda i, j, sel_ref: sel_ref[0] * (hc // tn) + j
        o_map = lambda i, j, sel_ref: (i, 0, j)
    if g_parts:
        g_per = (n // tn) // g_parts
        g_spec = pl.BlockSpec((None, s, tn), lambda i, j, sel_ref: (g_col(i, j, sel_ref) // g_per, 0, g_col(i, j, sel_ref) % g_per))
    else:
        g_spec = pl.BlockSpec((s, tn), lambda i, j, sel_ref: (0, g_col(i, j, sel_ref)))
    o_spec = pl.BlockSpec((None, tm, tn), o_map)
    in_specs = [pl.BlockSpec((s, tm), a_map), g_spec] + ([o_spec] if res is not None else [])

    def body(sel_ref, a_ref, g_ref, *rest):
        o_ref = rest[-1]
        p = lax.dot_general(a_ref[...], g_ref[...], (((0,), (0,)), ((), ())), preferred_element_type=F32)
        if res is not None:
            p = p + rest[0][...].astype(F32)
        o_ref[...] = p.astype(BF16)

    grid_spec = pltpu.PrefetchScalarGridSpec(num_scalar_prefetch=1, grid=grid, in_specs=in_specs, out_specs=o_spec)
    return _ordered_call(
        body, name=call_name, out_shape=jax.ShapeDtypeStruct((N_CHIPS, rh, hc), BF16), grid_spec=grid_spec,
        compiler_params=_params(("parallel", "parallel")),
    )(sel, a, g, *([res] if res is not None else []))


def _row_tile(s):
    return min(512, s)


def _rows(width, tr):
    return pl.BlockSpec((tr, width), lambda i: (i, 0))


def _const2(shape):
    return pl.BlockSpec(shape, lambda i: (0, 0))


def _rms_fwd(x, g, name):
    s, d = x.shape
    tr = _row_tile(s)

    def body(x_ref, g_ref, o_ref):
        xv = x_ref[...]
        r = lax.rsqrt(_mean_last(xv * xv) + EPS)
        o_ref[...] = (xv * r * g_ref[...]).astype(BF16)

    return _ordered_call(
        body, name=name, out_shape=jax.ShapeDtypeStruct((s, d), BF16), grid=(s // tr,),
        in_specs=[_rows(d, tr), _const2((1, d))], out_specs=_rows(d, tr), compiler_params=_params(("parallel",)),
    )(x, g)


def _rms_bwd(x, g, dh, dres, name):
    s, d = x.shape
    tr = _row_tile(s)

    def body(x_ref, g_ref, dh_ref, dres_ref, dx_ref, dxb_ref, dg_ref):
        xv, dy = x_ref[...], dh_ref[...].astype(F32)
        r = lax.rsqrt(_mean_last(xv * xv) + EPS)
        gdy = dy * g_ref[...]
        dx = dres_ref[...] + r * gdy - xv * ((r * r * r) * _mean_last(xv * gdy))
        dx_ref[...] = dx
        dxb_ref[...] = dx.astype(BF16)

        @pl.when(pl.program_id(0) == 0)
        def _():
            dg_ref[...] = jnp.zeros_like(dg_ref)

        dg_ref[...] += _sum_rows(xv * r * dy)

    return _ordered_call(
        body, name=name,
        out_shape=(jax.ShapeDtypeStruct((s, d), F32), jax.ShapeDtypeStruct((s, d), BF16), jax.ShapeDtypeStruct((1, d), F32)),
        grid=(s // tr,), in_specs=[_rows(d, tr), _const2((1, d)), _rows(d, tr), _rows(d, tr)],
        out_specs=(_rows(d, tr), _rows(d, tr), _const2((1, d))), compiler_params=_params(("arbitrary",)),
    )(x, g, dh, dres)


Q0, K0, V0, GU0, GV0 = 0, ATTN_WIDTH, ATTN_WIDTH + KV_WIDTH, ATTN_WIDTH + 2 * KV_WIDTH, ATTN_WIDTH + 2 * KV_WIDTH + GMLP_WIDTH


def _head(h, base=0):
    return slice(base + h * HEAD_DIM, base + (h + 1) * HEAD_DIM)


def _proj_post(z, qg, kg, lg, lb, cosf, sinf, name):
    s = z.shape[0]
    tr = _row_tile(s)

    def body(z_ref, qg_ref, kg_ref, lg_ref, lb_ref, cos_ref, sin_ref, qn_ref, kn_ref, vb_ref, ug_ref, vn_ref,
             dgu_ref, dgv_ref, xhat_ref, rstd_ref):
        cos, sin = cos_ref[...], sin_ref[...]

        def norm_rope(xh, g):
            y = xh * lax.rsqrt(_mean_last(xh * xh) + EPS) * g
            return y * cos + pltpu.roll(y, HEAD_DIM // 2, 1) * sin

        for h in range(N_Q_HEADS):
            qn_ref[:, _head(h)] = norm_rope(z_ref[:, _head(h, Q0)].astype(F32), qg_ref[...]).astype(BF16)
        for h in range(N_KV_HEADS):
            kn_ref[:, _head(h)] = norm_rope(z_ref[:, _head(h, K0)].astype(F32), kg_ref[...]).astype(BF16)
        vb_ref[...] = z_ref[:, V0:GU0]
        gu = z_ref[:, GU0:GV0].astype(F32)
        ug_ref[...] = _gelu(gu)
        dgu_ref[...] = _gelu_grad(gu).astype(BF16)
        gv = z_ref[:, GV0:IN_WIDTH].astype(F32)
        vg = _gelu(gv)
        dgv_ref[...] = _gelu_grad(gv).astype(BF16)
        xc = vg - _mean_last(vg)
        r = lax.rsqrt(_mean_last(xc * xc) + EPS)
        y = xc * r
        xhat_ref[...] = y.astype(BF16)
        rstd_ref[...] = r
        vn_ref[...] = (y * lg_ref[...] + lb_ref[...]).astype(BF16)

    wide = jax.ShapeDtypeStruct((s, GMLP_WIDTH), BF16)
    return _ordered_call(
        body, name=name,
        out_shape=(jax.ShapeDtypeStruct((s, ATTN_WIDTH), BF16), jax.ShapeDtypeStruct((s, KV_WIDTH), BF16),
                   jax.ShapeDtypeStruct((s, KV_WIDTH), BF16), jax.ShapeDtypeStruct((s, GMLP_WIDTH), F32), wide,
                   wide, wide, wide, jax.ShapeDtypeStruct((s, 1), F32)),
        grid=(s // tr,),
        in_specs=[_rows(IN_WIDTH, tr), _const2((1, HEAD_DIM)), _const2((1, HEAD_DIM)), _const2((1, GMLP_WIDTH)),
                  _const2((1, GMLP_WIDTH)), _rows(HEAD_DIM, tr), _rows(HEAD_DIM, tr)],
        out_specs=(_rows(ATTN_WIDTH, tr), _rows(KV_WIDTH, tr), _rows(KV_WIDTH, tr), _rows(GMLP_WIDTH, tr), _rows(GMLP_WIDTH, tr),
                   _rows(GMLP_WIDTH, tr), _rows(GMLP_WIDTH, tr), _rows(GMLP_WIDTH, tr), _rows(1, tr)),
        compiler_params=_params(("parallel",)),
    )(z, qg, kg, lg, lb, cosf, sinf)


def _proj_post_bwd(z, dqn, dkn, dvb, dug, dvn, gelu_grad_u, gelu_grad_v, xhat_v, rstd_v, qg, kg, lg, cosf, sinf, name):
    s = z.shape[0]
    tr = _row_tile(s)

    def body(z_ref, dqn_ref, dkn_ref, dvb_ref, dug_ref, dvn_ref, ggu_ref, ggv_ref, xhat_ref, rstd_ref, qg_ref, kg_ref, lg_ref,
             cos_ref, sin_ref, dz_ref, dqg_ref, dkg_ref, dlg_ref, dlb_ref):
        cos, sin = cos_ref[...], sin_ref[...]

        @pl.when(pl.program_id(0) == 0)
        def _():
            dqg_ref[...] = jnp.zeros_like(dqg_ref)
            dkg_ref[...] = jnp.zeros_like(dkg_ref)
            dlg_ref[...] = jnp.zeros_like(dlg_ref)
            dlb_ref[...] = jnp.zeros_like(dlb_ref)

        def norm_rope_bwd(xh, g, dout):
            dy = dout * cos - pltpu.roll(dout, HEAD_DIM // 2, 1) * sin
            r = lax.rsqrt(_mean_last(xh * xh) + EPS)
            xhat = xh * r
            gdy = dy * g
            return r * (gdy - xhat * _mean_last(xhat * gdy)), _sum_rows(xhat * dy)

        dqg = jnp.zeros((1, HEAD_DIM), F32)
        for h in range(N_Q_HEADS):
            dx, dg = norm_rope_bwd(z_ref[:, _head(h, Q0)].astype(F32), qg_ref[...], dqn_ref[:, _head(h)])
            dz_ref[:, _head(h, Q0)] = dx.astype(BF16)
            dqg = dqg + dg
        dqg_ref[...] += dqg
        dkg = jnp.zeros((1, HEAD_DIM), F32)
        for h in range(N_KV_HEADS):
            dx, dg = norm_rope_bwd(z_ref[:, _head(h, K0)].astype(F32), kg_ref[...], dkn_ref[:, _head(h)])
            dz_ref[:, _head(h, K0)] = dx.astype(BF16)
            dkg = dkg + dg
        dkg_ref[...] += dkg
        dz_ref[:, V0:GU0] = dvb_ref[...].astype(BF16)
        dz_ref[:, GU0:GV0] = (dug_ref[...] * ggu_ref[...].astype(F32)).astype(BF16)
        xhat = xhat_ref[...].astype(F32)
        dvn_v = dvn_ref[...]
        dlg_ref[...] += _sum_rows(xhat * dvn_v)
        dlb_ref[...] += _sum_rows(dvn_v)
        dxh = dvn_v * lg_ref[...]
        dvg = rstd_ref[...] * (dxh - _mean_last(dxh) - xhat * _mean_last(dxh * xhat))
        dz_ref[:, GV0:IN_WIDTH] = (dvg * ggv_ref[...].astype(F32)).astype(BF16)

    return _ordered_call(
        body, name=name,
        out_shape=(jax.ShapeDtypeStruct((s, IN_WIDTH), BF16), jax.ShapeDtypeStruct((1, HEAD_DIM), F32),
                   jax.ShapeDtypeStruct((1, HEAD_DIM), F32), jax.ShapeDtypeStruct((1, GMLP_WIDTH), F32),
                   jax.ShapeDtypeStruct((1, GMLP_WIDTH), F32)),
        grid=(s // tr,),
        in_specs=[_rows(V0, tr), _rows(ATTN_WIDTH, tr), _rows(KV_WIDTH, tr), _rows(KV_WIDTH, tr), _rows(GMLP_WIDTH, tr),
                  _rows(GMLP_WIDTH, tr), _rows(GMLP_WIDTH, tr), _rows(GMLP_WIDTH, tr), _rows(GMLP_WIDTH, tr), _rows(1, tr),
                  _const2((1, HEAD_DIM)), _const2((1, HEAD_DIM)), _const2((1, GMLP_WIDTH)), _rows(HEAD_DIM, tr),
                  _rows(HEAD_DIM, tr)],
        out_specs=(_rows(IN_WIDTH, tr), _const2((1, HEAD_DIM)), _const2((1, HEAD_DIM)), _const2((1, GMLP_WIDTH)),
                   _const2((1, GMLP_WIDTH))),
        compiler_params=_params(("arbitrary",)),
    )(z, dqn, dkn, dvb, dug, dvn, gelu_grad_u, gelu_grad_v, xhat_v, rstd_v, qg, kg, lg, cosf, sinf)


def _band_valid(n, s):
    shape = (GQA_GROUP * BLOCK, 3 * BLOCK)
    i = lax.broadcasted_iota(jnp.int32, shape, 0) & (BLOCK - 1)
    j = lax.broadcasted_iota(jnp.int32, shape, 1)
    k_pos = n * BLOCK - BLOCK + j
    return (jnp.abs(j - BLOCK - i) <= BLOCK) & (k_pos >= 0) & (k_pos < s)


def _group_rows(x, kh):
    return jnp.concatenate([x[:, _head(kh * GQA_GROUP + g)] for g in range(GQA_GROUP)], axis=0)


def _group_sinks(sink_ref, kh):
    return jnp.concatenate([jnp.full((BLOCK, 1), sink_ref[kh * GQA_GROUP + g], F32) for g in range(GQA_GROUP)], axis=0)


def _rows_of(x, g):
    return x[g * BLOCK:(g + 1) * BLOCK]


def _probs(q, kb, sink_h, valid):
    sc = lax.dot_general(q, kb, (((1,), (1,)), ((), ())), preferred_element_type=F32) * (HEAD_DIM ** -0.5)
    sc = jnp.where(valid, sc, MASK_VALUE)
    m = jnp.maximum(jnp.max(sc, axis=-1, keepdims=True), sink_h)
    p = jnp.exp(sc - m)
    es = jnp.exp(sink_h - m)
    den = jnp.sum(p, axis=-1, keepdims=True) + es
    inv = 1.0 / den
    return p * inv, es * inv


def _band_specs(width, nb):
    return [pl.BlockSpec((BLOCK, width), lambda n: (jnp.maximum(n - 1, 0), 0)),
            pl.BlockSpec((BLOCK, width), lambda n: (n, 0)),
            pl.BlockSpec((BLOCK, width), lambda n: (jnp.minimum(n + 1, nb - 1), 0))]


def _blk(width):
    return pl.BlockSpec((BLOCK, width), lambda n: (n, 0))


def _whole3(shape):
    return pl.BlockSpec(shape, lambda n: (0, 0, 0))


def _smem():
    return pl.BlockSpec(memory_space=pltpu.SMEM)


def _mixer_fwd(qn, kn, vb, ug, vn, wsb, bsb, sink, ga, gs, name):
    s = qn.shape[0]
    nb = s // BLOCK

    def body(sink_ref, q_ref, kp_ref, kc_ref, kx_ref, vp_ref, vc_ref, vx_ref, ug_ref, vn_ref, ws_ref, bs_ref, ga_ref, gs_ref,
             attn_ref, sgu_ref, mix_ref, probs_ref, psink_ref):
        n = pl.program_id(0)
        valid = _band_valid(n, s)
        ssq = jnp.zeros((BLOCK, 1), F32)
        for kh in range(N_KV_HEADS):
            kb = jnp.concatenate([kp_ref[:, _head(kh)], kc_ref[:, _head(kh)], kx_ref[:, _head(kh)]], axis=0)
            vbd = jnp.concatenate([vp_ref[:, _head(kh)], vc_ref[:, _head(kh)], vx_ref[:, _head(kh)]], axis=0)
            p, p_sink = _probs(_group_rows(q_ref, kh), kb, _group_sinks(sink_ref, kh), valid)
            pb = p.astype(BF16)
            probs_ref[kh] = pb
            psink_ref[kh] = p_sink
            o4 = jnp.dot(pb, vbd, preferred_element_type=F32)
            for g in range(GQA_GROUP):
                o = _rows_of(o4, g)
                attn_ref[:, _head(kh * GQA_GROUP + g)] = o
                ssq = ssq + jnp.sum(o * o, axis=-1, keepdims=True)
        r = lax.rsqrt(ssq * (1.0 / ATTN_WIDTH) + EPS)
        mix_ref[:, 0:ATTN_WIDTH] = (attn_ref[...] * r * ga_ref[...]).astype(BF16)
        ssq = jnp.zeros((BLOCK, 1), F32)
        for h in range(N_GMLP_HEADS):
            f = jnp.dot(ws_ref[h], vn_ref[:, _head(h)], preferred_element_type=F32) + bs_ref[h]
            o = ug_ref[:, _head(h)] * f
            sgu_ref[:, _head(h)] = o
            ssq = ssq + jnp.sum(o * o, axis=-1, keepdims=True)
        r = lax.rsqrt(ssq * (1.0 / GMLP_WIDTH) + EPS)
        mix_ref[:, ATTN_WIDTH:D_MODEL] = (sgu_ref[...] * r * gs_ref[...]).astype(BF16)

    hh = (N_GMLP_HEADS, BLOCK, BLOCK)
    return _ordered_call(
        body, name=name,
        out_shape=(jax.ShapeDtypeStruct((s, ATTN_WIDTH), F32), jax.ShapeDtypeStruct((s, GMLP_WIDTH), F32),
                   jax.ShapeDtypeStruct((s, D_MODEL), BF16), jax.ShapeDtypeStruct((nb,) + PROBS_BLOCK, BF16),
                   jax.ShapeDtypeStruct((nb,) + PSINK_BLOCK, F32)),
        grid=(nb,),
        in_specs=[_smem(), _blk(ATTN_WIDTH)] + _band_specs(KV_WIDTH, nb) + _band_specs(KV_WIDTH, nb)
        + [_blk(GMLP_WIDTH), _blk(GMLP_WIDTH), _whole3(hh), _whole3(hh),
           pl.BlockSpec((1, ATTN_WIDTH), lambda n: (0, 0)), pl.BlockSpec((1, GMLP_WIDTH), lambda n: (0, 0))],
        out_specs=(_blk(ATTN_WIDTH), _blk(GMLP_WIDTH), _blk(D_MODEL), _per_block(PROBS_BLOCK), _per_block(PSINK_BLOCK)),
        compiler_params=_params(("parallel",)),
    )(sink, qn, kn, kn, kn, vb, vb, vb, ug, vn, wsb, bsb, ga, gs)


PROBS_BLOCK = (N_KV_HEADS, GQA_GROUP * BLOCK, 3 * BLOCK)
PSINK_BLOCK = (N_KV_HEADS, GQA_GROUP * BLOCK, 1)


def _per_block(shape):
    return pl.BlockSpec((None,) + shape, lambda n: (n, 0, 0, 0))


def _mixer_bwd(qn, kn, vb, ug, vn, attn, sgu, dmixed, wsb, bsb, ga, gs, probs, psink, name):
    s = qn.shape[0]
    nb = s // BLOCK
    tn_dims = (((0,), (0,)), ((), ()))
    nt_dims = (((1,), (1,)), ((), ()))

    def body(q_ref, kp_ref, kc_ref, kx_ref, vp_ref, vc_ref, vx_ref, ug_ref, vn_ref, attn_ref, sgu_ref, dm_ref,
             ws_ref, bs_ref, ga_ref, gs_ref, probs_ref, psink_ref,
             dq_ref, dk_ref, dv_ref, dug_ref, dvn_ref, dws_ref, dbs_ref, dsk_ref, dga_ref, dgs_ref, dk_acc, dv_acc):
        n = pl.program_id(0)

        @pl.when(n == 0)
        def _():
            for ref in (dk_acc, dv_acc, dws_ref, dbs_ref, dsk_ref, dga_ref, dgs_ref):
                ref[...] = jnp.zeros_like(ref)

        def out_norm_bwd(o, g, dy):
            r = lax.rsqrt(_mean_last(o * o) + EPS)
            gdy = dy * g
            return r * gdy - o * ((r * r * r) * _mean_last(o * gdy)), _sum_rows(o * r * dy)

        d_attn, dga = out_norm_bwd(attn_ref[...], ga_ref[...], dm_ref[:, 0:ATTN_WIDTH].astype(F32))
        dga_ref[...] += dga
        d_sgu, dgs = out_norm_bwd(sgu_ref[...], gs_ref[...], dm_ref[:, ATTN_WIDTH:D_MODEL].astype(F32))
        dgs_ref[...] += dgs

        for h in range(N_GMLP_HEADS):
            vn_h = vn_ref[:, _head(h)]
            f = jnp.dot(ws_ref[h], vn_h, preferred_element_type=F32) + bs_ref[h]
            ds_h = d_sgu[:, _head(h)]
            dug_ref[:, _head(h)] = ds_h * f
            df = ds_h * ug_ref[:, _head(h)]
            dfb = df.astype(BF16)
            dvn_ref[:, _head(h)] = lax.dot_general(ws_ref[h], dfb, tn_dims, preferred_element_type=F32)
            dws_ref[h] += lax.dot_general(dfb, vn_h, nt_dims, preferred_element_type=F32)
            dbs_ref[h] += jnp.broadcast_to(jnp.sum(df, axis=-1, keepdims=True), (BLOCK, BLOCK))

        row0 = pl.multiple_of(n * BLOCK, BLOCK)
        for kh in range(N_KV_HEADS):
            kb = jnp.concatenate([kp_ref[:, _head(kh)], kc_ref[:, _head(kh)], kx_ref[:, _head(kh)]], axis=0)
            vbd = jnp.concatenate([vp_ref[:, _head(kh)], vc_ref[:, _head(kh)], vx_ref[:, _head(kh)]], axis=0)
            q4 = _group_rows(q_ref, kh)
            pb = probs_ref[kh]
            p = pb.astype(F32)
            do4 = _group_rows(d_attn, kh).astype(BF16)
            dp = lax.dot_general(do4, vbd, nt_dims, preferred_element_type=F32)
            delta = jnp.sum(p * dp, axis=-1, keepdims=True)
            dsc = (p * (dp - delta) * (HEAD_DIM ** -0.5)).astype(BF16)
            d_sink = -(psink_ref[kh] * delta)
            dq4 = jnp.dot(dsc, kb, preferred_element_type=F32)
            for g in range(GQA_GROUP):
                h = kh * GQA_GROUP + g
                dsk_ref[h:h + 1, :] += jnp.broadcast_to(_sum_all(_rows_of(d_sink, g)), (1, BLOCK))
                dq_ref[:, _head(h)] = _rows_of(dq4, g)
            dk_acc[pl.ds(row0, 3 * BLOCK), _head(kh)] += lax.dot_general(dsc, q4, tn_dims, preferred_element_type=F32)
            dv_acc[pl.ds(row0, 3 * BLOCK), _head(kh)] += lax.dot_general(pb, do4, tn_dims, preferred_element_type=F32)

        @pl.when(n == nb - 1)
        def _():
            dk_ref[...] = dk_acc[BLOCK:BLOCK + s, :]
            dv_ref[...] = dv_acc[BLOCK:BLOCK + s, :]

    hh = (N_GMLP_HEADS, BLOCK, BLOCK)
    full_kv = pl.BlockSpec((s, KV_WIDTH), lambda n: (0, 0))
    return _ordered_call(
        body, name=name,
        out_shape=(jax.ShapeDtypeStruct((s, ATTN_WIDTH), F32), jax.ShapeDtypeStruct((s, KV_WIDTH), F32),
                   jax.ShapeDtypeStruct((s, KV_WIDTH), F32), jax.ShapeDtypeStruct((s, GMLP_WIDTH), F32),
                   jax.ShapeDtypeStruct((s, GMLP_WIDTH), F32), jax.ShapeDtypeStruct(hh, F32), jax.ShapeDtypeStruct(hh, F32),
                   jax.ShapeDtypeStruct((N_Q_HEADS, BLOCK), F32), jax.ShapeDtypeStruct((1, ATTN_WIDTH), F32),
                   jax.ShapeDtypeStruct((1, GMLP_WIDTH), F32)),
        grid=(nb,),
        in_specs=[_blk(ATTN_WIDTH)] + _band_specs(KV_WIDTH, nb) + _band_specs(KV_WIDTH, nb)
        + [_blk(GMLP_WIDTH), _blk(GMLP_WIDTH), _blk(ATTN_WIDTH), _blk(GMLP_WIDTH), _blk(D_MODEL), _whole3(hh), _whole3(hh),
           pl.BlockSpec((1, ATTN_WIDTH), lambda n: (0, 0)), pl.BlockSpec((1, GMLP_WIDTH), lambda n: (0, 0)),
           _per_block(PROBS_BLOCK), _per_block(PSINK_BLOCK)],
        out_specs=(_blk(ATTN_WIDTH), full_kv, full_kv, _blk(GMLP_WIDTH), _blk(GMLP_WIDTH), _whole3(hh), _whole3(hh),
                   pl.BlockSpec((N_Q_HEADS, BLOCK), lambda n: (0, 0)), pl.BlockSpec((1, ATTN_WIDTH), lambda n: (0, 0)),
                   pl.BlockSpec((1, GMLP_WIDTH), lambda n: (0, 0))),
        scratch_shapes=[pltpu.VMEM((s + 2 * BLOCK, KV_WIDTH), F32), pltpu.VMEM((s + 2 * BLOCK, KV_WIDTH), F32)],
        compiler_params=_params(("arbitrary",)),
    )(qn, kn, kn, kn, vb, vb, vb, ug, vn, attn, sgu, dmixed, wsb, bsb, ga, gs, probs, psink)


CONV_TILE = 256


PAD_ROWS = 8


def _zero_pad_rows(pad_ref):
    s = pad_ref.shape[0] - 2 * PAD_ROWS
    zeros = jnp.zeros((PAD_ROWS, pad_ref.shape[1]), F32)
    pad_ref[0:PAD_ROWS, :] = zeros
    pad_ref[PAD_ROWS + s:2 * PAD_ROWS + s, :] = zeros


def _shift_rows(a, pad_ref):
    s = a.shape[0]
    pad_ref[PAD_ROWS:PAD_ROWS + s, :] = a
    padded = pad_ref[...]
    prev = pltpu.roll(padded, 1, 0)[PAD_ROWS:PAD_ROWS + s]
    nxt = pltpu.roll(padded, s + 2 * PAD_ROWS - 1, 0)[PAD_ROWS:PAD_ROWS + s]
    return prev, nxt


def _conv_specs(s):
    tc = CONV_TILE
    nj = D_FF // tc
    return (tc, nj, pl.BlockSpec((2, s, tc), lambda j: (0, 0, j)),
            [pl.BlockSpec((3, tc), lambda j: (0, j)), pl.BlockSpec((3, tc), lambda j: (0, j + nj))],
            [pl.BlockSpec((1, tc), lambda j: (0, j)), pl.BlockSpec((1, tc), lambda j: (0, j + nj))])


def _conv_gate_fwd(a_pre, cw, cb, name):
    s = a_pre.shape[1]
    tc, nj, a_spec, w_specs, b_specs = _conv_specs(s)

    def body(a_ref, wg_ref, wu_ref, bg_ref, bu_ref, act_ref, dgu_ref, pad_ref):
        _zero_pad_rows(pad_ref)

        def conv(a, w_ref, b_ref):
            prev, nxt = _shift_rows(a, pad_ref)
            return b_ref[...] + prev * w_ref[0:1, :] + a * w_ref[1:2, :] + nxt * w_ref[2:3, :]

        g = conv(a_ref[0].astype(F32), wg_ref, bg_ref)
        u = conv(a_ref[1].astype(F32), wu_ref, bu_ref)
        sg = 1.0 / (1.0 + jnp.exp(-g))
        silu = g * sg
        act_ref[...] = (silu * u).astype(BF16)
        dgu_ref[0] = (u * (sg * (1.0 + g * (1.0 - sg)))).astype(BF16)
        dgu_ref[1] = silu.astype(BF16)

    return _ordered_call(
        body, name=name, out_shape=(jax.ShapeDtypeStruct((s, D_FF), BF16), jax.ShapeDtypeStruct((2, s, D_FF), BF16)),
        grid=(nj,), in_specs=[a_spec] + w_specs + b_specs,
        out_specs=(pl.BlockSpec((s, tc), lambda j: (0, j)), pl.BlockSpec((2, s, tc), lambda j: (0, 0, j))),
        scratch_shapes=[pltpu.VMEM((s + 2 * PAD_ROWS, tc), F32)], compiler_params=_params(("parallel",)),
    )(a_pre, cw, cw, cb, cb)


def _conv_gate_bwd(a_pre, dgu, cw, dact, name):
    s = a_pre.shape[1]
    tc, nj, a_spec, w_specs, _ = _conv_specs(s)

    def body(a_ref, dgu_ref, wg_ref, wu_ref, dact_ref, dap_ref, dcw_ref, dcb_ref, pad_ref):
        _zero_pad_rows(pad_ref)
        dact_v = dact_ref[...].astype(F32)
        for part, w_ref in enumerate((wg_ref, wu_ref)):
            da = dact_v * dgu_ref[part].astype(F32)
            a = a_ref[part].astype(F32)
            da_prev, da_next = _shift_rows(da, pad_ref)
            dcw_ref[part, 0:1, :] = _sum_rows(a * da_next)
            dcw_ref[part, 1:2, :] = _sum_rows(a * da)
            dcw_ref[part, 2:3, :] = _sum_rows(a * da_prev)
            dcb_ref[part] = _sum_rows(da)
            dap_ref[part] = (da_next * w_ref[0:1, :] + da * w_ref[1:2, :] + da_prev * w_ref[2:3, :]).astype(BF16)

    return _ordered_call(
        body, name=name,
        out_shape=(jax.ShapeDtypeStruct((2, s, D_FF), BF16), jax.ShapeDtypeStruct((2, 3, D_FF), F32),
                   jax.ShapeDtypeStruct((2, 1, D_FF), F32)),
        grid=(nj,),
        in_specs=[a_spec, pl.BlockSpec((2, s, tc), lambda j: (0, 0, j))] + w_specs + [pl.BlockSpec((s, tc), lambda j: (0, j))],
        out_specs=(pl.BlockSpec((2, s, tc), lambda j: (0, 0, j)), pl.BlockSpec((2, 3, tc), lambda j: (0, 0, j)),
                   pl.BlockSpec((2, 1, tc), lambda j: (0, 0, j))),
        scratch_shapes=[pltpu.VMEM((s + 2 * PAD_ROWS, tc), F32)], compiler_params=_params(("parallel",)),
    )(a_pre, dgu, cw, cw, dact)


def _loss_head(y, target, name):
    s, d = y.shape
    tr = _row_tile(s)

    def body(y_ref, t_ref, loss_ref, dy_ref, dyb_ref):
        err = y_ref[...] - t_ref[...]

        @pl.when(pl.program_id(0) == 0)
        def _():
            loss_ref[...] = jnp.zeros_like(loss_ref)

        loss_ref[...] += jnp.broadcast_to(0.5 * _sum_all(_mean_last(err * err)), (8, 128))
        dy = err * (1.0 / d)
        dy_ref[...] = dy
        dyb_ref[...] = dy.astype(BF16)

    return _ordered_call(
        body, name=name,
        out_shape=(jax.ShapeDtypeStruct((8, 128), F32), jax.ShapeDtypeStruct((s, d), F32), jax.ShapeDtypeStruct((s, d), BF16)),
        grid=(s // tr,), in_specs=[_rows(d, tr), _rows(d, tr)],
        out_specs=(_const2((8, 128)), _rows(d, tr), _rows(d, tr)), compiler_params=_params(("arbitrary",)),
    )(y, target)


def _row_block(rows, cols, budget=1 << 20):
    if rows * cols <= budget:
        return rows
    best = None
    for tr in range(16, rows, 16):
        if rows % tr == 0 and tr * cols <= budget:
            best = tr
    assert best is not None, (rows, cols)
    return best


def _place_shard(x4, layer, j_arr, out_dtype, name):
    _, nh, r, cols = x4.shape
    tr = _row_block(r, cols)

    def body(j_ref, x_ref, o_ref):
        o_ref[...] = x_ref[...].astype(out_dtype)

    grid_spec = pltpu.PrefetchScalarGridSpec(
        num_scalar_prefetch=1, grid=(nh, r // tr),
        in_specs=[pl.BlockSpec((None, None, tr, cols), lambda h, i, j_ref: (layer, h, i, 0))],
        out_specs=pl.BlockSpec((None, None, tr, cols), lambda h, i, j_ref: (j_ref[0], h, i, 0)))
    return _ordered_call(
        body, name=name, out_shape=jax.ShapeDtypeStruct((N_CHIPS, nh, r, cols), out_dtype), grid_spec=grid_spec,
        compiler_params=_params(("parallel", "parallel")),
    )(j_arr, x4)


def _adamw(w, g, m, v, name):
    rows, cols = w.shape
    tr = _row_block(rows, cols, 1 << 18)

    def body(w_ref, g_ref, m_ref, v_ref, go_ref, d_ref, nm_ref, nv_ref):
        gv = g_ref[...]
        go_ref[...] = gv
        mn = ADAM_B1 * m_ref[...] + (1.0 - ADAM_B1) * gv
        vn = ADAM_B2 * v_ref[...] + (1.0 - ADAM_B2) * (gv * gv)
        m_hat = mn / (1.0 - ADAM_B1 ** ADAM_STEP)
        v_hat = vn / (1.0 - ADAM_B2 ** ADAM_STEP)
        d_ref[...] = -ADAM_LR * (m_hat / (jnp.sqrt(v_hat) + ADAM_EPS) + ADAM_WD * w_ref[...])
        nm_ref[...] = mn
        nv_ref[...] = vn

    sds = jax.ShapeDtypeStruct((rows, cols), F32)
    return _ordered_call(
        body, name=name, out_shape=(sds, sds, sds, sds), grid=(rows // tr,),
        in_specs=[_rows(cols, tr)] * 4, out_specs=(_rows(cols, tr),) * 4, compiler_params=_params(("parallel",)),
    )(w, g, m, v)


def _chip_sum(p4, recv3, j_arr, c_arr, name):
    _, rh, cols = p4.shape
    tr = _row_block(rh, cols, 1 << 19)

    def body(j_ref, c_ref, p_ref, r_ref, o_ref):
        total = p_ref[...].astype(F32)
        for peer in range(3):
            total = total + r_ref[peer].astype(F32)
        o_ref[...] = total.astype(BF16)

    grid_spec = pltpu.PrefetchScalarGridSpec(
        num_scalar_prefetch=2, grid=(rh // tr,),
        in_specs=[pl.BlockSpec((None, tr, cols), lambda i, j_ref, c_ref: (j_ref[0], i, 0)),
                  pl.BlockSpec((3, tr, cols), lambda i, j_ref, c_ref: (0, i, 0))],
        out_specs=pl.BlockSpec((None, tr, cols), lambda i, j_ref, c_ref: (c_ref[0], i, 0)))
    return _ordered_call(
        body, name=name, out_shape=jax.ShapeDtypeStruct((2, rh, cols), BF16), grid_spec=grid_spec,
        compiler_params=_params(("parallel",)),
    )(j_arr, c_arr, p4, recv3)


def _adamw_layer(w, g, m, v, layer, into, name):
    nl, rows, cols = w.shape
    slabs, _, width = g.shape
    assert slabs * width == cols and g.shape[1] == rows, (name, w.shape, g.shape)
    tr = _row_block(rows, width, 1 << 19)
    at_layer = pl.BlockSpec((None, tr, width), lambda h, i: (layer, i, h))

    def body(w_ref, g_ref, m_ref, v_ref, *rest):
        go_ref, d_ref, nm_ref, nv_ref = rest[-4:]
        gv = g_ref[...].astype(F32)
        go_ref[...] = gv
        mn = ADAM_B1 * m_ref[...] + (1.0 - ADAM_B1) * gv
        vn = ADAM_B2 * v_ref[...] + (1.0 - ADAM_B2) * (gv * gv)
        m_hat = mn / (1.0 - ADAM_B1 ** ADAM_STEP)
        v_hat = vn / (1.0 - ADAM_B2 ** ADAM_STEP)
        d_ref[...] = -ADAM_LR * (m_hat / (jnp.sqrt(v_hat) + ADAM_EPS) + ADAM_WD * w_ref[...])
        nm_ref[...] = mn
        nv_ref[...] = vn

    in_specs = [at_layer, pl.BlockSpec((None, tr, width), lambda h, i: (h, i, 0)), at_layer, at_layer]
    operands = [w, g, m, v]
    aliases = {}
    if into is not None:
        in_specs += [ANY] * 4
        operands += list(into)
        aliases = {4 + i: i for i in range(4)}
    sds = jax.ShapeDtypeStruct((nl, rows, cols), F32)
    return _ordered_call(
        body, name=name, out_shape=(sds,) * 4, grid=(slabs, rows // tr), in_specs=in_specs, out_specs=(at_layer,) * 4,
        input_output_aliases=aliases, compiler_params=_params(("parallel", "parallel")),
    )(*operands)


def _sum_devices(mine, landed, me_arr, name):
    rows, lanes = mine.shape

    def body(me_ref, mine_ref, landed_ref, o_ref):
        total = None
        for dev in range(8):
            part = jnp.where(me_ref[0] == dev, mine_ref[...], landed_ref[dev])
            total = part if total is None else total + part
        o_ref[...] = total

    grid_spec = pltpu.PrefetchScalarGridSpec(
        num_scalar_prefetch=1, grid=(1,),
        in_specs=[pl.BlockSpec((rows, lanes), lambda i, me_ref: (0, 0)), pl.BlockSpec((8, rows, lanes), lambda i, me_ref: (0, 0, 0))],
        out_specs=pl.BlockSpec((rows, lanes), lambda i, me_ref: (0, 0)))
    return _ordered_call(
        body, name=name, out_shape=jax.ShapeDtypeStruct((rows, lanes), F32), grid_spec=grid_spec,
        compiler_params=_params(("arbitrary",)),
    )(me_arr, mine, landed)


def _place():
    x, y, c = lax.axis_index("x"), lax.axis_index("y"), lax.axis_index("c")
    chips = [(1 - x, y), (x, 1 - y), (1 - x, 1 - y)]
    return x, y, c, chips


HBM = pl.BlockSpec(memory_space=pltpu.HBM)
SEM = pl.BlockSpec(memory_space=pltpu.SEMAPHORE)
TOKEN = jax.ShapeDtypeStruct((8, 128), F32)


def _remote(src, dst, send_sem, recv_sem, to):
    return pltpu.make_async_remote_copy(src_ref=src, dst_ref=dst, send_sem=send_sem, recv_sem=recv_sem, device_id=to,
                                        device_id_type=MESH)


def _split_call(body, name, thru, sems_in=(), fresh=(), new_sems=(), after_last=True):
    n_t, n_s, n_f = len(thru), len(sems_in), len(fresh)

    def call_body(*refs):
        outs = refs[n_t + n_s:]
        body(refs[:n_t], refs[n_t:n_t + n_s], outs[1 + n_t:1 + n_t + n_f], outs[1 + n_t + n_f:])
        outs[0][...] = jnp.zeros_like(outs[0])

    out_shape = ([TOKEN] + [pltpu.HBM(t.shape, t.dtype) for t in thru] + [pltpu.HBM(shp, dt) for shp, dt in fresh]
                 + [pltpu.SemaphoreType.DMA(shp) for shp in new_sems])
    out_specs = [pl.BlockSpec(memory_space=pltpu.VMEM)] + [HBM] * (n_t + n_f) + [SEM] * len(new_sems)
    if not after_last or any(t is _Order.last for t in thru):
        _Order.last = None
    out = _ordered_call(
        call_body, name=name, out_shape=tuple(out_shape), in_specs=[HBM] * n_t + [SEM] * n_s, out_specs=tuple(out_specs),
        input_output_aliases={i: 1 + i for i in range(n_t)},
        compiler_params=pltpu.CompilerParams(has_side_effects=pltpu.SideEffectType.DATAFLOW_SIDE_EFFECTING),
    )(*[pltpu.with_memory_space_constraint(t, pltpu.HBM) for t in thru], *sems_in)
    return out[1:1 + n_t], out[1 + n_t:1 + n_t + n_f], out[1 + n_t + n_f:]


class _Exchange:
    def __init__(self, weights, m_in, v_in, j_arr, c_arr, me_arr):
        self.w, self.m, self.v = weights, m_in, v_in
        self.j_arr, self.c_arr, self.me_arr = j_arr, c_arr, me_arr
        self.adam, self.small, self.pairs, self.held = {}, {}, {}, None
        self.o_arr = 1 - c_arr
        self.groups = [(l, name) for l in range(DEPTH) for name in BIG_NAMES]
        self.shard_shape = {name: weights[name].shape[1:] for name in BIG_NAMES}
        self.conv_state, self.state = [], {}
        self.ready, self.conv_ready = {}, {}
        self.pending, self.tick, self.reduced = [], 0, {}

        def place(grp):
            l, name = grp
            nl, r, cols = weights[name].shape
            return _place_shard(weights[name].reshape(nl, 2, r // 2, cols), l, j_arr, BF16, f"place_{name}_l{l}")

        def start_copies(tag, convs, groups, bufs):
            n_c = len(convs)

            def start(thru, _, __, sems):
                x, y, c, chips = _place()
                j_me = 2 * x + y
                copies = []
                for i in range(len(thru)):
                    mine = thru[i].at[j_me] if i < n_c else thru[i].at[j_me, c]
                    copies += [_remote(mine, mine, sems[2 * i].at[k], sems[2 * i + 1].at[k], (*chip, c))
                               for k, chip in enumerate(chips)]
                for cp in copies:
                    cp.start()

            thru, _, sems = _split_call(start, tag, convs + bufs, new_sems=[(3,)] * (2 * (n_c + len(bufs))))
            self.conv_state += [(thru[i], sems[2 * i], sems[2 * i + 1]) for i in range(n_c)]
            for g, grp in enumerate(groups):
                self.state[grp] = (thru[n_c + g], sems[2 * (n_c + g)], sems[2 * (n_c + g) + 1])

        convs = [_place_shard(weights["conv_w"][:, None], l, j_arr, F32, f"place_conv_w_l{l}") for l in range(DEPTH)]
        start_copies("gather_start_first", convs, self.groups[:1], [place(self.groups[0])])
        start_copies("gather_start_rest", [], self.groups[1:], [place(grp) for grp in self.groups[1:]])

    def conv_w(self, l):
        if l not in self.conv_ready:
            buf, send, recv = self.conv_state[l]

            def wait(thru, sems, _, __):
                x, y, c, chips = _place()
                for k, chip in enumerate(chips):
                    mine, theirs = thru[0].at[2 * x + y], thru[0].at[2 * chip[0] + chip[1]]
                    _remote(mine, mine, sems[0].at[k], sems[1].at[k], (*chip, c)).wait_send()
                    _remote(theirs, theirs, sems[0].at[k], sems[1].at[k], (x, y, c)).wait_recv()

            (buf,), _, _ = _split_call(wait, f"gather_conv_w_l{l}", [buf], sems_in=[send, recv])
            self.conv_ready[l] = jnp.transpose(buf[:, 0], (1, 0, 2)).reshape(3, 2 * D_FF)
        return self.conv_ready[l]

    def weight(self, l, name):
        grp = (l, name)
        if grp not in self.ready:
            buf, send, recv = self.state[grp]

            def forward(thru, sems, _, new):
                x, y, c, chips = _place()
                for k, chip in enumerate(chips):
                    landed = thru[0].at[2 * chip[0] + chip[1], c]
                    _remote(landed, landed, new[0].at[k], sems[0].at[k], (x, y, c)).wait_recv()
                    _remote(landed, landed, new[0].at[k], new[1].at[k], (x, y, 1 - c)).start()

            (buf,), _, (fsend, frecv) = _split_call(forward, f"gather_pass_{name}_l{l}", [buf], sems_in=[recv],
                                                    new_sems=[(3,), (3,)])

            def finish(thru, sems, _, __):
                x, y, c, chips = _place()
                mine = thru[0].at[2 * x + y, c]
                for k, chip in enumerate(chips):
                    j_k = 2 * chip[0] + chip[1]
                    theirs, landed = thru[0].at[j_k, 1 - c], thru[0].at[j_k, c]
                    _remote(theirs, theirs, sems[1].at[k], sems[2].at[k], (x, y, c)).wait_recv()
                    _remote(landed, landed, sems[1].at[k], sems[2].at[k], (x, y, 1 - c)).wait_send()
                    _remote(mine, mine, sems[0].at[k], sems[2].at[k], (*chip, c)).wait_send()

            (buf,), _, _ = _split_call(finish, f"gather_done_{name}_l{l}", [buf], sems_in=[send, fsend, frecv])
            r, cols = self.shard_shape[name]
            self.ready[grp] = buf.reshape(N_CHIPS, r, cols) if name in ("w_in", "w_up") else buf.reshape(N_CHIPS * r, cols)
        return self.ready[grp]

    def pair_send(self, l, name, other):
        held = self.held
        self.held = None

        def start(thru, _, fresh, sems):
            x, y, c, chips = _place()
            copies = [_remote(thru[0], fresh[0], sems[0], sems[1], (x, y, 1 - c))]
            if held is not None:
                copies += [_remote(thru[1].at[2 * chip[0] + chip[1]], fresh[1].at[k], sems[2].at[k], sems[3].at[k], (*chip, c))
                           for k, chip in enumerate(chips)]
            for cp in copies:
                cp.start()

        thru, fresh, new_sems = [other], [(other.shape, BF16)], [(), ()]
        if held is not None:
            thru, fresh, new_sems = thru + [held[2]], fresh + [((3,) + held[2].shape[1:], BF16)], new_sems + [(3,), (3,)]
        thru, fresh, sems = _split_call(start, f"pair_start_{name}_l{l}", thru, fresh=fresh, new_sems=new_sems, after_last=False)
        self.pairs[(l, name)] = (thru[0], fresh[0], sems[:2])
        if held is not None:
            self.pending.append(dict(l=held[0], name=held[1], stage=2, at=self.tick, bufs=(thru[1], fresh[1]), sems=sems[2:]))

    def pair_recv(self, l, name):
        other, recv, sems = self.pairs.pop((l, name))

        def wait(thru, sems, _, __):
            x, y, c, _chips = _place()
            cp = _remote(thru[0], thru[1], sems[0], sems[1], (x, y, 1 - c))
            cp.wait_send()
            cp.wait_recv()

        (_, recv), _, _ = _split_call(wait, f"pair_done_{name}_l{l}", [other, recv], sems_in=list(sems))
        return recv

    def scatter(self, l, name, p4):
        assert self.held is None
        self.held = (l, name, p4)
        if (l, name) == (0, BIG_NAMES[0]):
            self._scatter_held()

    def _scatter_held(self):
        l, name, p4 = self.held
        self.held = None

        def start(thru, _, fresh, sems):
            x, y, c, chips = _place()
            for k, chip in enumerate(chips):
                _remote(thru[0].at[2 * chip[0] + chip[1]], fresh[0].at[k], sems[0].at[k], sems[1].at[k], (*chip, c)).start()

        (p4,), (recv3,), sems = _split_call(start, f"chips_start_{name}_l{l}", [p4], fresh=[((3,) + p4.shape[1:], BF16)],
                                           new_sems=[(3,), (3,)], after_last=False)
        self.pending.append(dict(l=l, name=name, stage=2, at=self.tick, bufs=(p4, recv3), sems=sems))

    def point(self, drain=False):
        self.tick += 1
        if drain:
            old = [g for g in self.pending if g["stage"] == 2 and g["at"] + 2 <= self.tick]
            new = [g for g in self.pending if g["stage"] == 2 and g["at"] + 2 > self.tick]
            for grp in old + [g for g in self.pending if g["stage"] == 3] + new:
                self._advance([grp] if grp["stage"] == 3 else [], [grp] if grp["stage"] == 2 else [])
        else:
            self._advance([grp for grp in self.pending if grp["stage"] == 3 and grp["at"] < self.tick],
                          [grp for grp in self.pending if grp["stage"] == 2 and grp["at"] + 2 <= self.tick])

    def _advance(self, joined, landed):
        if not joined and not landed:
            return
        n_j, n_l = len(joined), len(landed)

        def wait(thru, sems, _, __):
            x, y, c, chips = _place()
            for i in range(n_j):
                buf, send, recv = thru[i], sems[2 * i], sems[2 * i + 1]
                _remote(buf.at[c], buf.at[c], send, recv, (x, y, 1 - c)).wait_send()
                _remote(buf.at[1 - c], buf.at[1 - c], send, recv, (x, y, c)).wait_recv()
            for i in range(n_l):
                p4, recv3 = thru[n_j + 2 * i], thru[n_j + 2 * i + 1]
                send, recv = sems[2 * (n_j + i)], sems[2 * (n_j + i) + 1]
                for k, chip in enumerate(chips):
                    cp = _remote(p4.at[2 * chip[0] + chip[1]], recv3.at[k], send.at[k], recv.at[k], (*chip, c))
                    cp.wait_send()
                    cp.wait_recv()

        tag = "_".join([f"{grp['name']}{grp['l']}_halves" for grp in joined] + [f"{grp['name']}{grp['l']}_chips" for grp in landed])
        bufs, _, _ = _split_call(wait, f"landed_{tag}", [b for grp in joined + landed for b in grp["bufs"]],
                                 sems_in=[sm for grp in joined + landed for sm in grp["sems"]])
        for i, grp in enumerate(joined):
            l, name, full = grp["l"], grp["name"], bufs[i]
            if GRAD_HALVES[name][0] != "cols_of_block":
                full = full.reshape((1,) + tuple(self.shard_shape[name]))
            self.adam[name] = _adamw_layer(self.w[name], full, self.m[name], self.v[name], l, self.adam.get(name),
                                           f"adamw_{name}_l{l}")
            grp.update(stage=4)
        if not landed:
            return
        halves = [_chip_sum(bufs[n_j + 2 * i], bufs[n_j + 2 * i + 1], self.j_arr, self.c_arr,
                            f"chip_sum_{grp['name']}_l{grp['l']}") for i, grp in enumerate(landed)]

        def start(thru, _, __, sems):
            x, y, c, _chips = _place()
            for i in range(n_l):
                _remote(thru[i].at[c], thru[i].at[c], sems[2 * i], sems[2 * i + 1], (x, y, 1 - c)).start()

        tag = "_".join(f"{grp['name']}{grp['l']}" for grp in landed)
        halves, _, sems = _split_call(start, f"join_start_{tag}", halves, new_sems=[()] * (2 * n_l), after_last=False)
        for i, grp in enumerate(landed):
            grp.update(stage=3, at=self.tick, bufs=(halves[i],), sems=tuple(sems[2 * i:2 * i + 2]))

    def finish(self):
        if self.held is not None:
            self._scatter_held()
        while any(grp["stage"] < 4 for grp in self.pending):
            self.point(drain=True)
        return self.adam

    @staticmethod
    def _peer(k, x, y, c):
        return (1 - x if k & 4 else x, 1 - y if k & 2 else y, 1 - c if k & 1 else c)

    def small_grads(self, l, grads, loss_tile):
        parts = [grads[nm] for nm in SMALL_NAMES] + ([loss_tile[0, 0:1]] if loss_tile is not None else [])
        packed = _pack_call(parts, f"small_pack_l{l}")
        rows = packed.shape[0]

        def start(thru, _, fresh, sems):
            x, y, c, _chips = _place()
            for k in range(1, 8):
                _remote(thru[0], fresh[0].at[4 * x + 2 * y + c], sems[0].at[k - 1], sems[1].at[k - 1],
                        self._peer(k, x, y, c)).start()

        (packed,), (landed,), sems = _split_call(start, f"small_start_l{l}", [packed], fresh=[((8, rows, PACK_LANES), F32)],
                                                 new_sems=[(7,), (7,)], after_last=False)
        self.small[l] =(packed, landed, sems, [p.shape for p in parts])

    def small_sum(self, l):
        packed, landed, sems, _shapes = self.small[l]

        def wait(thru, sems, _, __):
            x, y, c, _chips = _place()
            for k in range(1, 8):
                px, py, pc = self._peer(k, x, y, c)
                _remote(thru[0], thru[1].at[4 * x + 2 * y + c], sems[0].at[k - 1], sems[1].at[k - 1], (px, py, pc)).wait_send()
                _remote(thru[0], thru[1].at[4 * px + 2 * py + pc], sems[0].at[k - 1], sems[1].at[k - 1], (x, y, c)).wait_recv()

        (packed, landed), _, _ = _split_call(wait, f"small_done_l{l}", [packed, landed], sems_in=list(sems))
        return _sum_devices(packed, landed, self.me_arr, f"small_sum_l{l}")


def _rope_tables(s):
    inv_freq = ROPE_THETA ** (-jnp.arange(0, HEAD_DIM, 2, dtype=F32) / HEAD_DIM)
    ang = jnp.arange(s, dtype=F32)[:, None] * inv_freq[None, :]
    cos, sin = jnp.cos(ang), jnp.sin(ang)
    return jnp.concatenate([cos, cos], axis=-1), jnp.concatenate([-sin, sin], axis=-1)


def _local_step(x, target, ex, small):
    s = x.shape[0]
    cosf, sinf = _rope_tables(s)
    saved = []
    for l in range(DEPTH):
        p = small[l]
        t = f"l{l}"
        h = _rms_fwd(x, p["norm1_g"], f"norm1_{t}")
        z = _matmul(h, ex.weight(l, "w_in"), mode="nn", out_dtype=BF16, tm=1024, tn=896, tk=2048, b_parts=4, name=f"proj_in_{t}")
        qn, kn, vb, ug, vn, *gate_kept = _proj_post(z, p["q_norm_g"], p["k_norm_g"], p["sgu_ln_g"], p["sgu_ln_b"], cosf, sinf,
                                                    f"proj_post_{t}")
        attn, sgu, mixed, probs, psink = _mixer_fwd(qn, kn, vb, ug, vn, p["w_s_bf16"], p["b_s_tile"], p["sink"],
                                                    p["attn_out_g"], p["sgu_out_g"], f"mixer_{t}")
        x1 = _matmul(mixed, ex.weight(l, "w_o"), mode="nn", out_dtype=F32, tm=2048, tn=256, tk=2048, res=x,
                     name=f"proj_out_{t}")
        h2 = _rms_fwd(x1, p["norm2_g"], f"norm2_{t}")
        a_pre = _matmul(h2, ex.weight(l, "w_up"), mode="nn", out_dtype=BF16, tm=1024, tn=1408, tk=2048, b_parts=4,
                        out_parts=2,
                        name=f"ffn_up_{t}")
        act, dgu = _conv_gate_fwd(a_pre, ex.conv_w(l), p["conv_b"], f"conv_gate_{t}")
        x2 = _matmul(act, ex.weight(l, "w_down"), mode="nn", out_dtype=F32, tm=1024, tn=256, tk=D_FF, res=x1,
                     name=f"ffn_down_{t}")
        saved.append(dict(x=x, h=h, z=z, qn=qn, kn=kn, vb=vb, ug=ug, vn=vn, attn=attn, sgu=sgu, mixed=mixed, x1=x1, h2=h2,
                          a_pre=a_pre, act=act, dgu=dgu, probs=probs, psink=psink, gate_kept=gate_kept))
        x = x2
    loss_tile, dx, dxb = _loss_head(x, target, "loss_head")
    for l in reversed(range(DEPTH)):
        p, sv = small[l], saved[l]
        t = f"l{l}"
        def weight_grad(name, a, g, between, g_parts=0):
            ex.pair_send(l, name, _grad_half(name, a, g, ex.o_arr, None, f"g_{name}_other_{t}", g_parts))
            out = between()
            ex.scatter(l, name, _grad_half(name, a, g, ex.c_arr, ex.pair_recv(l, name), f"g_{name}_own_{t}", g_parts))
            ex.point()
            return out

        def after_down():
            dact = _matmul(dxb, ex.weight(l, "w_down"), mode="nt", out_dtype=BF16, tm=1024, tn=512, tk=2048,
                           name=f"d_act_{t}")
            return _conv_gate_bwd(sv["a_pre"], sv["dgu"], ex.conv_w(l), dact, f"conv_gate_bwd_{t}")

        dap, dcw, dcb = weight_grad("w_down", sv["act"], dxb, after_down)

        def after_up():
            dh2 = _matmul(dap, ex.weight(l, "w_up"), mode="nt", out_dtype=BF16, tm=1024, tn=1024, tk=2816, a_parts=2,
                          b_parts=4, name=f"d_h2_{t}")
            return _rms_bwd(sv["x1"], p["norm2_g"], dh2, dx, f"norm2_bwd_{t}")

        dx1, dx1b, dg2 = weight_grad("w_up", sv["h2"], dap, after_up, g_parts=2)
        ex.pair_send(l, "w_o", _grad_half("w_o", sv["mixed"], dx1b, ex.o_arr, None, f"g_w_o_other_{t}"))
        dmixed = _matmul(dx1b, ex.weight(l, "w_o"), mode="nt", out_dtype=BF16, tm=1024, tn=1024, tk=2048,
                         name=f"d_mixed_{t}")
        dqn, dkn, dvb, dug, dvn, dws, dbs, dsk, dga, dgs = _mixer_bwd(
            sv["qn"], sv["kn"], sv["vb"], sv["ug"], sv["vn"], sv["attn"], sv["sgu"], dmixed, p["w_s_bf16"], p["b_s_tile"],
            p["attn_out_g"], p["sgu_out_g"], sv["probs"], sv["psink"], f"mixer_bwd_{t}")
        dz, dqg, dkg, dlg, dlb = _proj_post_bwd(sv["z"], dqn, dkn, dvb, dug, dvn, *sv["gate_kept"], p["q_norm_g"], p["k_norm_g"],
                                                 p["sgu_ln_g"], cosf, sinf, f"proj_post_bwd_{t}")
        ex.scatter(l, "w_o", _grad_half("w_o", sv["mixed"], dx1b, ex.c_arr, ex.pair_recv(l, "w_o"), f"g_w_o_own_{t}"))
        ex.point()

        def after_in():
            dh = _matmul_nt_slabs(dz, ex.weight(l, "w_in"), tm=1024, tn=512, name=f"d_h_{t}")
            return _rms_bwd(sv["x"], p["norm1_g"], dh, dx1, f"norm1_bwd_{t}")

        dx, dxb, dg1 = weight_grad("w_in", sv["h"], dz, after_in)
        ex.small_grads(l, dict(
            norm1_g=dg1[0], q_norm_g=dqg[0], k_norm_g=dkg[0], sink=dsk[:, 0], sgu_ln_g=dlg[0], sgu_ln_b=dlb[0], w_s=dws,
            b_s=dbs[:, :, 0], attn_out_g=dga[0], sgu_out_g=dgs[0], norm2_g=dg2[0],
            conv_w=jnp.concatenate([dcw[0], dcw[1]], axis=-1), conv_b=jnp.concatenate([dcb[0, 0], dcb[1, 0]], axis=-1)),
            loss_tile if l == 0 else None)
    return dx


def _small_views(l, norm1_g, q_norm_g, k_norm_g, sink, sgu_ln_g, sgu_ln_b, w_s, b_s, attn_out_g, sgu_out_g, norm2_g, conv_b):
    return dict(
        norm1_g=norm1_g[l][None], q_norm_g=q_norm_g[l][None], k_norm_g=k_norm_g[l][None], sink=sink[l],
        sgu_ln_g=sgu_ln_g[l][None], sgu_ln_b=sgu_ln_b[l][None], w_s_bf16=w_s[l].astype(BF16),
        b_s_tile=jnp.broadcast_to(b_s[l][:, :, None], (N_GMLP_HEADS, BLOCK, BLOCK)), attn_out_g=attn_out_g[l][None],
        sgu_out_g=sgu_out_g[l][None], norm2_g=norm2_g[l][None], conv_b=conv_b[l][None])


SMALL_NAMES = ("norm1_g", "q_norm_g", "k_norm_g", "sink", "sgu_ln_g", "sgu_ln_b", "w_s", "b_s", "attn_out_g", "sgu_out_g",
               "norm2_g", "conv_b", "conv_w")
REPLICATED_NAMES = SMALL_NAMES[:-1]
BIG_NAMES = ("w_in", "w_o", "w_up", "w_down")
PACK_LANES = 128
PACK_ALIGN = 8 * PACK_LANES


def _pack_rows(shape):
    return -(-math.prod(shape) // PACK_ALIGN) * 8


def _pack_parts(arrays):
    parts = []
    for a in arrays:
        flat = a.reshape(-1)
        parts.append(jnp.pad(flat, (0, _pack_rows(a.shape) * PACK_LANES - flat.shape[0])).reshape(-1, PACK_LANES))
    return parts


def _pack_call(arrays, name):
    parts = _pack_parts(arrays)
    total = sum(p.shape[0] for p in parts)

    def body(*refs):
        o_ref, at = refs[-1], 0
        for p_ref in refs[:-1]:
            o_ref[at:at + p_ref.shape[0], :] = p_ref[...]
            at += p_ref.shape[0]

    vm = pl.BlockSpec(memory_space=pltpu.VMEM)
    return _ordered_call(
        body, name=name, out_shape=jax.ShapeDtypeStruct((total, PACK_LANES), F32), in_specs=[vm] * len(parts), out_specs=vm,
        compiler_params=pltpu.CompilerParams(vmem_limit_bytes=V7X_VMEM_LIMIT),
    )(*parts)


def _unpack_layers(stacked, shapes):
    nl = stacked.shape[0]
    out, at = [], 0
    for shp in shapes:
        rows = _pack_rows(shp)
        out.append(stacked[:, at:at + rows].reshape(nl, -1)[:, :math.prod(shp)].reshape((nl,) + tuple(shp)))
        at += rows
    return out


def _adamw_packed(w, g, m, v, rows, layer, into, name):
    head = pl.BlockSpec((rows, PACK_LANES), lambda i: (0, 0))
    at_layer = pl.BlockSpec((None, rows, PACK_LANES), lambda i: (layer, 0, 0))

    def body(w_ref, g_ref, m_ref, v_ref, *rest):
        d_ref, nm_ref, nv_ref = rest[-3:]
        gv = g_ref[...]
        mn = ADAM_B1 * m_ref[...] + (1.0 - ADAM_B1) * gv
        vn = ADAM_B2 * v_ref[...] + (1.0 - ADAM_B2) * (gv * gv)
        m_hat = mn / (1.0 - ADAM_B1 ** ADAM_STEP)
        v_hat = vn / (1.0 - ADAM_B2 ** ADAM_STEP)
        d_ref[...] = -ADAM_LR * (m_hat / (jnp.sqrt(v_hat) + ADAM_EPS) + ADAM_WD * w_ref[...])
        nm_ref[...] = mn
        nv_ref[...] = vn

    in_specs = [head] * 4
    operands = [w, g, m, v]
    aliases = {}
    if into is not None:
        in_specs += [ANY] * 3
        operands += list(into)
        aliases = {4 + i: i for i in range(3)}
    sds = jax.ShapeDtypeStruct((DEPTH, rows, PACK_LANES), F32)
    return _ordered_call(
        body, name=name, out_shape=(sds,) * 3, grid=(1,), in_specs=in_specs, out_specs=(at_layer,) * 3,
        input_output_aliases=aliases, compiler_params=_params(("arbitrary",)),
    )(*operands)


def kernel(x, norm1_g, w_in, q_norm_g, k_norm_g, sink, sgu_ln_g, sgu_ln_b, w_s, b_s, attn_out_g, sgu_out_g, w_o, norm2_g, w_up, conv_w, conv_b, w_down, loss_target, m_norm1_g, m_w_in, m_q_norm_g, m_k_norm_g, m_sink, m_sgu_ln_g, m_sgu_ln_b, m_w_s, m_b_s, m_attn_out_g, m_sgu_out_g, m_w_o, m_norm2_g, m_w_up, m_conv_w, m_conv_b, m_w_down, v_norm1_g, v_w_in, v_q_norm_g, v_k_norm_g, v_sink, v_sgu_ln_g, v_sgu_ln_b, v_w_s, v_b_s, v_attn_out_g, v_sgu_out_g, v_w_o, v_norm2_g, v_w_up, v_conv_w, v_conv_b, v_w_down):
    weights = dict(norm1_g=norm1_g, w_in=w_in, q_norm_g=q_norm_g, k_norm_g=k_norm_g, sink=sink, sgu_ln_g=sgu_ln_g,
                   sgu_ln_b=sgu_ln_b, w_s=w_s, b_s=b_s, attn_out_g=attn_out_g, sgu_out_g=sgu_out_g, w_o=w_o, norm2_g=norm2_g,
                   w_up=w_up, conv_w=conv_w, conv_b=conv_b, w_down=w_down)
    m_in = dict(norm1_g=m_norm1_g, w_in=m_w_in, q_norm_g=m_q_norm_g, k_norm_g=m_k_norm_g, sink=m_sink, sgu_ln_g=m_sgu_ln_g,
                sgu_ln_b=m_sgu_ln_b, w_s=m_w_s, b_s=m_b_s, attn_out_g=m_attn_out_g, sgu_out_g=m_sgu_out_g, w_o=m_w_o,
                norm2_g=m_norm2_g, w_up=m_w_up, conv_w=m_conv_w, conv_b=m_conv_b, w_down=m_w_down)
    v_in = dict(norm1_g=v_norm1_g, w_in=v_w_in, q_norm_g=v_q_norm_g, k_norm_g=v_k_norm_g, sink=v_sink, sgu_ln_g=v_sgu_ln_g,
                sgu_ln_b=v_sgu_ln_b, w_s=v_w_s, b_s=v_b_s, attn_out_g=v_attn_out_g, sgu_out_g=v_sgu_out_g, w_o=v_w_o,
                norm2_g=v_norm2_g, w_up=v_w_up, conv_w=v_conv_w, conv_b=v_conv_b, w_down=v_w_down)
    cx, cy, cc = lax.axis_index("x"), lax.axis_index("y"), lax.axis_index("c")
    j_me = 2 * cx + cy
    c_arr = jnp.reshape(cc, (1,)).astype(jnp.int32)
    j_arr = jnp.reshape(j_me, (1,)).astype(jnp.int32)

    _Order.last = None
    ex = _Exchange(weights, m_in, v_in, j_arr, c_arr, jnp.reshape(4 * cx + 2 * cy + cc, (1,)).astype(jnp.int32))
    small = [_small_views(l, norm1_g, q_norm_g, k_norm_g, sink, sgu_ln_g, sgu_ln_b, w_s, b_s, attn_out_g, sgu_out_g, norm2_g,
                          conv_b) for l in range(DEPTH)]
    held_back, _ = lax.optimization_barrier(([[src[nm] for nm in REPLICATED_NAMES] for src in (weights, m_in, v_in)], _Order.last))
    packed_in = [[_pack_call([arr[l] for arr in arrays], f"pack_{tag}_l{l}") for tag, arrays in zip("wmv", held_back)]
                 for l in range(DEPTH)]
    dx = _local_step(x[0], loss_target[0], ex, small)
    big_out = ex.finish()

    rep_shapes = [weights[nm].shape[1:] for nm in REPLICATED_NAMES]
    rep_rows = sum(_pack_rows(shp) for shp in rep_shapes)
    cw_shape = (3, 2 * D_FF)
    sums, adam_small = [None] * DEPTH, None
    for l in reversed(range(DEPTH)):
        sums[l] = ex.small_sum(l)
        pw, pm, pv = packed_in[l]
        adam_small = _adamw_packed(pw, sums[l], pm, pv, rep_rows, l, adam_small, f"adamw_small_l{l}")
    cw_rows = _pack_rows(cw_shape)
    loss = sums[0][rep_rows + cw_rows, 0]
    stacked = jnp.stack([sm[:rep_rows + cw_rows] for sm in sums])
    grads = dict(zip(REPLICATED_NAMES, _unpack_layers(stacked[:, :rep_rows], rep_shapes)))
    delta, new_m, new_v = (dict(zip(REPLICATED_NAMES, _unpack_layers(arr, rep_shapes))) for arr in adam_small)
    cw_cols = 2 * D_FF // N_CHIPS
    cw_grad = lax.dynamic_slice_in_dim(_unpack_layers(stacked[:, rep_rows:], [cw_shape])[0], j_me * cw_cols, cw_cols, axis=2)
    flat = lambda a: a.reshape(DEPTH * 3, cw_cols)
    cw_out = _adamw(flat(conv_w), flat(cw_grad), flat(m_conv_w), flat(v_conv_w), "adamw_conv_w")
    grads["conv_w"], delta["conv_w"], new_m["conv_w"], new_v["conv_w"] = (a.reshape(DEPTH, 3, cw_cols) for a in cw_out)

    for name in BIG_NAMES:
        grads[name], delta[name], new_m[name], new_v[name] = big_out[name]

    order = ("norm1_g", "w_in", "q_norm_g", "k_norm_g", "sink", "sgu_ln_g", "sgu_ln_b", "w_s", "b_s", "attn_out_g", "sgu_out_g",
             "w_o", "norm2_g", "w_up", "conv_w", "conv_b", "w_down")
    return (loss, dx[None], *[grads[nm] for nm in order], *[delta[nm] for nm in order], *[new_m[nm] for nm in order],
            *[new_v[nm] for nm in order])
```

```python
import math

import jax
import jax.numpy as jnp
from jax import lax
from jax.experimental import pallas as pl
from jax.experimental.pallas import tpu as pltpu

F32 = jnp.float32
BF16 = jnp.bfloat16

D_MODEL = 2048
HEAD_DIM = 128
ATTN_WIDTH = 1024
N_Q_HEADS = 8
N_KV_HEADS = 2
GQA_GROUP = 4
KV_WIDTH = 256
GMLP_WIDTH = 1024
N_GMLP_HEADS = 8
BLOCK = 128
IN_WIDTH = 3584
D_FF = 5632
DEPTH = 2
EPS = 1e-6
MASK_VALUE = -1e30
ROPE_THETA = 10000.0
N_CHIPS = 4

ADAM_LR = 0.001
ADAM_B1 = 0.9
ADAM_B2 = 0.999
ADAM_EPS = 1e-08
ADAM_WD = 0.01
ADAM_STEP = 10

V7X_VMEM_LIMIT = 48 * 1024 * 1024
MESH = pl.DeviceIdType.MESH

_GELU_C = math.sqrt(2.0 / math.pi)
_GELU_A = 0.044715


def _params(sem=None):
    return pltpu.CompilerParams(dimension_semantics=sem, vmem_limit_bytes=V7X_VMEM_LIMIT)


ANY = pl.BlockSpec(memory_space=pl.ANY)


class _Order:
    last = None


def _ordered_call(body, *, token_index=0, **kw):
    def run(*operands):
        tok = _Order.last
        if tok is None or any(op is tok for op in operands):
            call = pl.pallas_call(body, **kw)
        else:
            n_in = len(operands)

            def ordered_body(*refs):
                return body(*refs[:n_in], *refs[n_in + 1:])

            kw2 = dict(kw)
            if "grid_spec" in kw2:
                gs = kw2["grid_spec"]
                kw2["grid_spec"] = pltpu.PrefetchScalarGridSpec(
                    num_scalar_prefetch=gs.num_scalar_prefetch, grid=gs.grid, in_specs=list(gs.in_specs) + [ANY],
                    out_specs=gs.out_specs, scratch_shapes=gs.scratch_shapes)
            else:
                kw2["in_specs"] = list(kw2["in_specs"]) + [ANY]
            call = pl.pallas_call(ordered_body, **kw2)
            operands = operands + (tok,)
        out = call(*operands)
        _Order.last = out[token_index] if isinstance(out, (tuple, list)) else out
        return out

    return run


def _gelu(x):
    return x * (0.5 * (1.0 + jnp.tanh(_GELU_C * (x + _GELU_A * (x * x * x)))))


def _gelu_grad(x):
    x2 = x * x
    t = jnp.tanh(_GELU_C * (x + _GELU_A * (x * x2)))
    return 0.5 * (1.0 + t) + 0.5 * x * (1.0 - t * t) * (_GELU_C * (1.0 + 3.0 * _GELU_A * x2))


def _mean_last(x):
    return jnp.mean(x, axis=-1, keepdims=True)


def _sum_rows(x):
    return jnp.sum(x, axis=0, keepdims=True)


def _sum_all(x):
    return jnp.sum(jnp.sum(x, axis=1, keepdims=True), axis=0, keepdims=True)


def _matmul(a, b, *, mode, out_dtype, tm, tn, tk, name, res=None, a_parts=0, b_parts=0, out_parts=0):
    assert mode in ("nn", "nt"), mode
    if mode == "nn":
        assert not a_parts
        m, k = a.shape
        n = b.shape[0] * b.shape[2] if b_parts else b.shape[1]
    else:
        m, k = (a.shape[1], a.shape[0] * a.shape[2]) if a_parts else a.shape
        n = b.shape[1] if b_parts else b.shape[0]
    tm, tn, tk = min(tm, m), min(tn, n), min(tk, k)
    assert m % tm == 0 and n % tn == 0 and k % tk == 0, (name, m, n, k, tm, tn, tk)
    nm, nn, nk = m // tm, n // tn, k // tk

    def slab(idx, total_tiles, parts):
        per = total_tiles // parts
        assert per * parts == total_tiles, (name, total_tiles, parts)
        return idx // per, idx % per

    if mode == "nn":
        a_spec = pl.BlockSpec((tm, tk), lambda i, j, kk: (i, kk))
        if b_parts:
            b_spec = pl.BlockSpec((None, tk, tn), lambda i, j, kk: (slab(j, nn, b_parts)[0], kk, slab(j, nn, b_parts)[1]))
        else:
            b_spec = pl.BlockSpec((tk, tn), lambda i, j, kk: (kk, j))
        dims = (((1,), (0,)), ((), ()))
    else:
        if a_parts:
            a_spec = pl.BlockSpec((None, tm, tk), lambda i, j, kk: (slab(kk, nk, a_parts)[0], i, slab(kk, nk, a_parts)[1]))
        else:
            a_spec = pl.BlockSpec((tm, tk), lambda i, j, kk: (i, kk))
        if b_parts:
            b_spec = pl.BlockSpec((None, tn, tk), lambda i, j, kk: (slab(kk, nk, b_parts)[0], j, slab(kk, nk, b_parts)[1]))
        else:
            b_spec = pl.BlockSpec((tn, tk), lambda i, j, kk: (j, kk))
        dims = (((1,), (1,)), ((), ()))
    if out_parts:
        out_shape = jax.ShapeDtypeStruct((out_parts, m, n // out_parts), out_dtype)
        out_spec = pl.BlockSpec((None, tm, tn), lambda i, j, kk: (slab(j, nn, out_parts)[0], i, slab(j, nn, out_parts)[1]))
    else:
        out_shape = jax.ShapeDtypeStruct((m, n), out_dtype)
        out_spec = pl.BlockSpec((tm, tn), lambda i, j, kk: (i, j))
    in_specs = [a_spec, b_spec]
    operands = [a, b]
    if res is not None:
        in_specs.append(pl.BlockSpec((tm, tn), lambda i, j, kk: (i, j)))
        operands.append(res)

    def body(*refs):
        a_ref, b_ref = refs[0], refs[1]
        res_ref = refs[2] if res is not None else None
        o_ref = refs[3] if res is not None else refs[2]
        p = lax.dot_general(a_ref[...], b_ref[...], dims, preferred_element_type=F32)

        def finish(total):
            if res_ref is not None:
                total = res_ref[...] + total
            o_ref[...] = total.astype(out_dtype)

        if nk == 1:
            finish(p)
        else:
            acc_ref = refs[-1]
            kk = pl.program_id(2)

            @pl.when(kk == 0)
            def _():
                acc_ref[...] = p

            @pl.when(jnp.logical_and(kk > 0, kk < nk - 1))
            def _():
                acc_ref[...] += p

            @pl.when(kk == nk - 1)
            def _():
                finish(acc_ref[...] + p)

    scratch = [pltpu.VMEM((tm, tn), F32)] if nk > 1 else []
    return _ordered_call(
        body, name=name, out_shape=out_shape, grid=(nm, nn, nk), in_specs=in_specs, out_specs=out_spec,
        scratch_shapes=scratch, compiler_params=_params(("parallel", "parallel", "arbitrary")),
    )(*operands)


def _matmul_nt_slabs(a, b, *, tm, tn, name, a_parts=0):
    nslab, n, ks = b.shape
    m = a.shape[1] if a_parts else a.shape[0]
    tm, tn = min(tm, m), min(tn, n)
    assert m % tm == 0 and n % tn == 0, (name, m, n, tm, tn)
    if a_parts:
        per = nslab // a_parts
        assert per * a_parts == nslab and a.shape[2] == per * ks, (name, a.shape, b.shape)
        a_spec = pl.BlockSpec((a_parts, tm, per * ks), lambda i, j: (0, i, 0))
    else:
        assert a.shape[1] == nslab * ks, (name, a.shape, b.shape)
        a_spec = pl.BlockSpec((tm, nslab * ks), lambda i, j: (i, 0))

    def body(a_ref, b_ref, o_ref):
        total = None
        for sl in range(nslab):
            if a_parts:
                a_sl = a_ref[sl // per, :, (sl % per) * ks:(sl % per + 1) * ks]
            else:
                a_sl = a_ref[:, sl * ks:(sl + 1) * ks]
            p = lax.dot_general(a_sl, b_ref[sl], (((1,), (1,)), ((), ())), preferred_element_type=F32)
            total = p if total is None else total + p
        o_ref[...] = total.astype(BF16)

    return _ordered_call(
        body, name=name, out_shape=jax.ShapeDtypeStruct((m, n), BF16), grid=(m // tm, n // tn),
        in_specs=[a_spec, pl.BlockSpec((nslab, tn, ks), lambda i, j: (0, j, 0))],
        out_specs=pl.BlockSpec((tm, tn), lambda i, j: (i, j)), compiler_params=_params(("parallel", "parallel")),
    )(a, b)


GRAD_HALVES = {
    "w_in": ("rows_of_slab", 1024, 896), "w_up": ("rows_of_slab", 1024, 1408), "w_o": ("rows_of_block", 256, 2048),
    "w_down": ("cols_of_block", 1408, 512)}


def _grad_half(name, a, g, sel, res, call_name, g_parts=0):
    kind, tm, tn = GRAD_HALVES[name]
    s, m = a.shape
    n = g.shape[0] * g.shape[2] if g_parts else g.shape[1]
    if kind == "rows_of_slab":
        rh, hc = m // 2, n // N_CHIPS
        per = hc // tn
        grid = (rh // tm, n // tn)
        a_map = lambda i, j, sel_ref: (0, sel_ref[0] * (rh // tm) + i)
        g_col = lambda i, j, sel_ref: j
        o_map = lambda i, j, sel_ref: (j // per, i, j % per)
    elif kind == "rows_of_block":
        rh, hc = m // N_CHIPS // 2, n
        assert tm == rh
        grid = (N_CHIPS, n // tn)
        a_map = lambda i, j, sel_ref: (0, 2 * i + sel_ref[0])
        g_col = lambda i, j, sel_ref: j
        o_map = lambda i, j, sel_ref: (i, 0, j)
    else:
        rh, hc = m // N_CHIPS, n // 2
        assert tm == rh
        grid = (N_CHIPS, hc // tn)
        a_map = lambda i, j, sel_ref: (0, i)
        g_col = lambda i, j, sel_ref: sel_ref[0] * (hc // tn) + j
        o_map = lambda i, j, sel_ref: (i, 0, j)
    if g_parts:
        g_per = (n // tn) // g_parts
        g_spec = pl.BlockSpec((None, s, tn), lambda i, j, sel_ref: (g_col(i, j, sel_ref) // g_per, 0, g_col(i, j, sel_ref) % g_per))
    else:
        g_spec = pl.BlockSpec((s, tn), lambda i, j, sel_ref: (0, g_col(i, j, sel_ref)))
    o_spec = pl.BlockSpec((None, tm, tn), o_map)
    in_specs = [pl.BlockSpec((s, tm), a_map), g_spec] + ([o_spec] if res is not None else [])

    def body(sel_ref, a_ref, g_ref, *rest):
        o_ref = rest[-1]
        p = lax.dot_general(a_ref[...], g_ref[...], (((0,), (0,)), ((), ())), preferred_element_type=F32)
        if res is not None:
            p = p + rest[0][...].astype(F32)
        o_ref[...] = p.astype(BF16)

    grid_spec = pltpu.PrefetchScalarGridSpec(num_scalar_prefetch=1, grid=grid, in_specs=in_specs, out_specs=o_spec)
    return _ordered_call(
        body, name=call_name, out_shape=jax.ShapeDtypeStruct((N_CHIPS, rh, hc), BF16), grid_spec=grid_spec,
        compiler_params=_params(("parallel", "parallel")),
    )(sel, a, g, *([res] if res is not None else []))


def _row_tile(s):
    return min(512, s)


def _rows(width, tr):
    return pl.BlockSpec((tr, width), lambda i: (i, 0))


def _const2(shape):
    return pl.BlockSpec(shape, lambda i: (0, 0))


def _rms_fwd(x, g, name):
    s, d = x.shape
    tr = _row_tile(s)

    def body(x_ref, g_ref, o_ref):
        xv = x_ref[...]
        r = lax.rsqrt(_mean_last(xv * xv) + EPS)
        o_ref[...] = (xv * r * g_ref[...]).astype(BF16)

    return _ordered_call(
        body, name=name, out_shape=jax.ShapeDtypeStruct((s, d), BF16), grid=(s // tr,),
        in_specs=[_rows(d, tr), _const2((1, d))], out_specs=_rows(d, tr), compiler_params=_params(("parallel",)),
    )(x, g)


def _rms_bwd(x, g, dh, dres, name):
    s, d = x.shape
    tr = _row_tile(s)

    def body(x_ref, g_ref, dh_ref, dres_ref, dx_ref, dxb_ref, dg_ref):
        xv, dy = x_ref[...], dh_ref[...].astype(F32)
        r = lax.rsqrt(_mean_last(xv * xv) + EPS)
        gdy = dy * g_ref[...]
        dx = dres_ref[...] + r * gdy - xv * ((r * r * r) * _mean_last(xv * gdy))
        dx_ref[...] = dx
        dxb_ref[...] = dx.astype(BF16)

        @pl.when(pl.program_id(0) == 0)
        def _():
            dg_ref[...] = jnp.zeros_like(dg_ref)

        dg_ref[...] += _sum_rows(xv * r * dy)

    return _ordered_call(
        body, name=name,
        out_shape=(jax.ShapeDtypeStruct((s, d), F32), jax.ShapeDtypeStruct((s, d), BF16), jax.ShapeDtypeStruct((1, d), F32)),
        grid=(s // tr,), in_specs=[_rows(d, tr), _const2((1, d)), _rows(d, tr), _rows(d, tr)],
        out_specs=(_rows(d, tr), _rows(d, tr), _const2((1, d))), compiler_params=_params(("arbitrary",)),
    )(x, g, dh, dres)


Q0, K0, V0, GU0, GV0 = 0, ATTN_WIDTH, ATTN_WIDTH + KV_WIDTH, ATTN_WIDTH + 2 * KV_WIDTH, ATTN_WIDTH + 2 * KV_WIDTH + GMLP_WIDTH


def _head(h, base=0):
    return slice(base + h * HEAD_DIM, base + (h + 1) * HEAD_DIM)


def _proj_post(z, qg, kg, lg, lb, cosf, sinf, name):
    s = z.shape[0]
    tr = _row_tile(s)

    def body(z_ref, qg_ref, kg_ref, lg_ref, lb_ref, cos_ref, sin_ref, qn_ref, kn_ref, vb_ref, ug_ref, vn_ref,
             dgu_ref, dgv_ref, xhat_ref, rstd_ref):
        cos, sin = cos_ref[...], sin_ref[...]

        def norm_rope(xh, g):
            y = xh * lax.rsqrt(_mean_last(xh * xh) + EPS) * g
            return y * cos + pltpu.roll(y, HEAD_DIM // 2, 1) * sin

        for h in range(N_Q_HEADS):
            qn_ref[:, _head(h)] = norm_rope(z_ref[:, _head(h, Q0)].astype(F32), qg_ref[...]).astype(BF16)
        for h in range(N_KV_HEADS):
            kn_ref[:, _head(h)] = norm_rope(z_ref[:, _head(h, K0)].astype(F32), kg_ref[...]).astype(BF16)
        vb_ref[...] = z_ref[:, V0:GU0]
        gu = z_ref[:, GU0:GV0].astype(F32)
        ug_ref[...] = _gelu(gu)
        dgu_ref[...] = _gelu_grad(gu).astype(BF16)
        gv = z_ref[:, GV0:IN_WIDTH].astype(F32)
        vg = _gelu(gv)
        dgv_ref[...] = _gelu_grad(gv).astype(BF16)
        xc = vg - _mean_last(vg)
        r = lax.rsqrt(_mean_last(xc * xc) + EPS)
        y = xc * r
        xhat_ref[...] = y.astype(BF16)
        rstd_ref[...] = r
        vn_ref[...] = (y * lg_ref[...] + lb_ref[...]).astype(BF16)

    wide = jax.ShapeDtypeStruct((s, GMLP_WIDTH), BF16)
    return _ordered_call(
        body, name=name,
        out_shape=(jax.ShapeDtypeStruct((s, ATTN_WIDTH), BF16), jax.ShapeDtypeStruct((s, KV_WIDTH), BF16),
                   jax.ShapeDtypeStruct((s, KV_WIDTH), BF16), jax.ShapeDtypeStruct((s, GMLP_WIDTH), F32), wide,
                   wide, wide, wide, jax.ShapeDtypeStruct((s, 1), F32)),
        grid=(s // tr,),
        in_specs=[_rows(IN_WIDTH, tr), _const2((1, HEAD_DIM)), _const2((1, HEAD_DIM)), _const2((1, GMLP_WIDTH)),
                  _const2((1, GMLP_WIDTH)), _rows(HEAD_DIM, tr), _rows(HEAD_DIM, tr)],
        out_specs=(_rows(ATTN_WIDTH, tr), _rows(KV_WIDTH, tr), _rows(KV_WIDTH, tr), _rows(GMLP_WIDTH, tr), _rows(GMLP_WIDTH, tr),
                   _rows(GMLP_WIDTH, tr), _rows(GMLP_WIDTH, tr), _rows(GMLP_WIDTH, tr), _rows(1, tr)),
        compiler_params=_params(("parallel",)),
    )(z, qg, kg, lg, lb, cosf, sinf)


def _proj_post_bwd(z, dqn, dkn, dvb, dug, dvn, gelu_grad_u, gelu_grad_v, xhat_v, rstd_v, qg, kg, lg, cosf, sinf, name):
    s = z.shape[0]
    tr = _row_tile(s)

    def body(z_ref, dqn_ref, dkn_ref, dvb_ref, dug_ref, dvn_ref, ggu_ref, ggv_ref, xhat_ref, rstd_ref, qg_ref, kg_ref, lg_ref,
             cos_ref, sin_ref, dz_ref, dqg_ref, dkg_ref, dlg_ref, dlb_ref):
        cos, sin = cos_ref[...], sin_ref[...]

        @pl.when(pl.program_id(0) == 0)
        def _():
            dqg_ref[...] = jnp.zeros_like(dqg_ref)
            dkg_ref[...] = jnp.zeros_like(dkg_ref)
            dlg_ref[...] = jnp.zeros_like(dlg_ref)
            dlb_ref[...] = jnp.zeros_like(dlb_ref)

        def norm_rope_bwd(xh, g, dout):
            dy = dout * cos - pltpu.roll(dout, HEAD_DIM // 2, 1) * sin
            r = lax.rsqrt(_mean_last(xh * xh) + EPS)
            xhat = xh * r
            gdy = dy * g
            return r * (gdy - xhat * _mean_last(xhat * gdy)), _sum_rows(xhat * dy)

        dqg = jnp.zeros((1, HEAD_DIM), F32)
        for h in range(N_Q_HEADS):
            dx, dg = norm_rope_bwd(z_ref[:, _head(h, Q0)].astype(F32), qg_ref[...], dqn_ref[:, _head(h)])
            dz_ref[:, _head(h, Q0)] = dx.astype(BF16)
            dqg = dqg + dg
        dqg_ref[...] += dqg
        dkg = jnp.zeros((1, HEAD_DIM), F32)
        for h in range(N_KV_HEADS):
            dx, dg = norm_rope_bwd(z_ref[:, _head(h, K0)].astype(F32), kg_ref[...], dkn_ref[:, _head(h)])
            dz_ref[:, _head(h, K0)] = dx.astype(BF16)
            dkg = dkg + dg
        dkg_ref[...] += dkg
        dz_ref[:, V0:GU0] = dvb_ref[...].astype(BF16)
        dz_ref[:, GU0:GV0] = (dug_ref[...] * ggu_ref[...].astype(F32)).astype(BF16)
        xhat = xhat_ref[...].astype(F32)
        dvn_v = dvn_ref[...]
        dlg_ref[...] += _sum_rows(xhat * dvn_v)
        dlb_ref[...] += _sum_rows(dvn_v)
        dxh = dvn_v * lg_ref[...]
        dvg = rstd_ref[...] * (dxh - _mean_last(dxh) - xhat * _mean_last(dxh * xhat))
        dz_ref[:, GV0:IN_WIDTH] = (dvg * ggv_ref[...].astype(F32)).astype(BF16)

    return _ordered_call(
        body, name=name,
        out_shape=(jax.ShapeDtypeStruct((s, IN_WIDTH), BF16), jax.ShapeDtypeStruct((1, HEAD_DIM), F32),
                   jax.ShapeDtypeStruct((1, HEAD_DIM), F32), jax.ShapeDtypeStruct((1, GMLP_WIDTH), F32),
                   jax.ShapeDtypeStruct((1, GMLP_WIDTH), F32)),
        grid=(s // tr,),
        in_specs=[_rows(V0, tr), _rows(ATTN_WIDTH, tr), _rows(KV_WIDTH, tr), _rows(KV_WIDTH, tr), _rows(GMLP_WIDTH, tr),
                  _rows(GMLP_WIDTH, tr), _rows(GMLP_WIDTH, tr), _rows(GMLP_WIDTH, tr), _rows(GMLP_WIDTH, tr), _rows(1, tr),
                  _const2((1, HEAD_DIM)), _const2((1, HEAD_DIM)), _const2((1, GMLP_WIDTH)), _rows(HEAD_DIM, tr),
                  _rows(HEAD_DIM, tr)],
        out_specs=(_rows(IN_WIDTH, tr), _const2((1, HEAD_DIM)), _const2((1, HEAD_DIM)), _const2((1, GMLP_WIDTH)),
                   _const2((1, GMLP_WIDTH))),
        compiler_params=_params(("arbitrary",)),
    )(z, dqn, dkn, dvb, dug, dvn, gelu_grad_u, gelu_grad_v, xhat_v, rstd_v, qg, kg, lg, cosf, sinf)


def _band_valid(n, s):
    shape = (GQA_GROUP * BLOCK, 3 * BLOCK)
    i = lax.broadcasted_iota(jnp.int32, shape, 0) & (BLOCK - 1)
    j = lax.broadcasted_iota(jnp.int32, shape, 1)
    k_pos = n * BLOCK - BLOCK + j
    return (jnp.abs(j - BLOCK - i) <= BLOCK) & (k_pos >= 0) & (k_pos < s)


def _group_rows(x, kh):
    return jnp.concatenate([x[:, _head(kh * GQA_GROUP + g)] for g in range(GQA_GROUP)], axis=0)


def _group_sinks(sink_ref, kh):
    return jnp.concatenate([jnp.full((BLOCK, 1), sink_ref[kh * GQA_GROUP + g], F32) for g in range(GQA_GROUP)], axis=0)


def _rows_of(x, g):
    return x[g * BLOCK:(g + 1) * BLOCK]


def _probs(q, kb, sink_h, valid):
    sc = lax.dot_general(q, kb, (((1,), (1,)), ((), ())), preferred_element_type=F32) * (HEAD_DIM ** -0.5)
    sc = jnp.where(valid, sc, MASK_VALUE)
    m = jnp.maximum(jnp.max(sc, axis=-1, keepdims=True), sink_h)
    p = jnp.exp(sc - m)
    es = jnp.exp(sink_h - m)
    den = jnp.sum(p, axis=-1, keepdims=True) + es
    inv = 1.0 / den
    return p * inv, es * inv


def _band_specs(width, nb):
    return [pl.BlockSpec((BLOCK, width), lambda n: (jnp.maximum(n - 1, 0), 0)),
            pl.BlockSpec((BLOCK, width), lambda n: (n, 0)),
            pl.BlockSpec((BLOCK, width), lambda n: (jnp.minimum(n + 1, nb - 1), 0))]


def _blk(width):
    return pl.BlockSpec((BLOCK, width), lambda n: (n, 0))


def _whole3(shape):
    return pl.BlockSpec(shape, lambda n: (0, 0, 0))


def _smem():
    return pl.BlockSpec(memory_space=pltpu.SMEM)


def _mixer_fwd(qn, kn, vb, ug, vn, wsb, bsb, sink, ga, gs, name):
    s = qn.shape[0]
    nb = s // BLOCK

    def body(sink_ref, q_ref, kp_ref, kc_ref, kx_ref, vp_ref, vc_ref, vx_ref, ug_ref, vn_ref, ws_ref, bs_ref, ga_ref, gs_ref,
             attn_ref, sgu_ref, mix_ref, probs_ref, psink_ref):
        n = pl.program_id(0)
        valid = _band_valid(n, s)
        ssq = jnp.zeros((BLOCK, 1), F32)
        for kh in range(N_KV_HEADS):
            kb = jnp.concatenate([kp_ref[:, _head(kh)], kc_ref[:, _head(kh)], kx_ref[:, _head(kh)]], axis=0)
            vbd = jnp.concatenate([vp_ref[:, _head(kh)], vc_ref[:, _head(kh)], vx_ref[:, _head(kh)]], axis=0)
            p, p_sink = _probs(_group_rows(q_ref, kh), kb, _group_sinks(sink_ref, kh), valid)
            pb = p.astype(BF16)
            probs_ref[kh] = pb
            psink_ref[kh] = p_sink
            o4 = jnp.dot(pb, vbd, preferred_element_type=F32)
            for g in range(GQA_GROUP):
                o = _rows_of(o4, g)
                attn_ref[:, _head(kh * GQA_GROUP + g)] = o
                ssq = ssq + jnp.sum(o * o, axis=-1, keepdims=True)
        r = lax.rsqrt(ssq * (1.0 / ATTN_WIDTH) + EPS)
        mix_ref[:, 0:ATTN_WIDTH] = (attn_ref[...] * r * ga_ref[...]).astype(BF16)
        ssq = jnp.zeros((BLOCK, 1), F32)
        for h in range(N_GMLP_HEADS):
            f = jnp.dot(ws_ref[h], vn_ref[:, _head(h)], preferred_element_type=F32) + bs_ref[h]
            o = ug_ref[:, _head(h)] * f
            sgu_ref[:, _head(h)] = o
            ssq = ssq + jnp.sum(o * o, axis=-1, keepdims=True)
        r = lax.rsqrt(ssq * (1.0 / GMLP_WIDTH) + EPS)
        mix_ref[:, ATTN_WIDTH:D_MODEL] = (sgu_ref[...] * r * gs_ref[...]).astype(BF16)

    hh = (N_GMLP_HEADS, BLOCK, BLOCK)
    return _ordered_call(
        body, name=name,
        out_shape=(jax.ShapeDtypeStruct((s, ATTN_WIDTH), F32), jax.ShapeDtypeStruct((s, GMLP_WIDTH), F32),
                   jax.ShapeDtypeStruct((s, D_MODEL), BF16), jax.ShapeDtypeStruct((nb,) + PROBS_BLOCK, BF16),
                   jax.ShapeDtypeStruct((nb,) + PSINK_BLOCK, F32)),
        grid=(nb,),
        in_specs=[_smem(), _blk(ATTN_WIDTH)] + _band_specs(KV_WIDTH, nb) + _band_specs(KV_WIDTH, nb)
        + [_blk(GMLP_WIDTH), _blk(GMLP_WIDTH), _whole3(hh), _whole3(hh),
           pl.BlockSpec((1, ATTN_WIDTH), lambda n: (0, 0)), pl.BlockSpec((1, GMLP_WIDTH), lambda n: (0, 0))],
        out_specs=(_blk(ATTN_WIDTH), _blk(GMLP_WIDTH), _blk(D_MODEL), _per_block(PROBS_BLOCK), _per_block(PSINK_BLOCK)),
        compiler_params=_params(("parallel",)),
    )(sink, qn, kn, kn, kn, vb, vb, vb, ug, vn, wsb, bsb, ga, gs)


PROBS_BLOCK = (N_KV_HEADS, GQA_GROUP * BLOCK, 3 * BLOCK)
PSINK_BLOCK = (N_KV_HEADS, GQA_GROUP * BLOCK, 1)


def _per_block(shape):
    return pl.BlockSpec((None,) + shape, lambda n: (n, 0, 0, 0))


def _mixer_bwd(qn, kn, vb, ug, vn, attn, sgu, dmixed, wsb, bsb, ga, gs, probs, psink, name):
    s = qn.shape[0]
    nb = s // BLOCK
    tn_dims = (((0,), (0,)), ((), ()))
    nt_dims = (((1,), (1,)), ((), ()))

    def body(q_ref, kp_ref, kc_ref, kx_ref, vp_ref, vc_ref, vx_ref, ug_ref, vn_ref, attn_ref, sgu_ref, dm_ref,
             ws_ref, bs_ref, ga_ref, gs_ref, probs_ref, psink_ref,
             dq_ref, dk_ref, dv_ref, dug_ref, dvn_ref, dws_ref, dbs_ref, dsk_ref, dga_ref, dgs_ref, dk_acc, dv_acc):
        n = pl.program_id(0)

        @pl.when(n == 0)
        def _():
            for ref in (dk_acc, dv_acc, dws_ref, dbs_ref, dsk_ref, dga_ref, dgs_ref):
                ref[...] = jnp.zeros_like(ref)

        def out_norm_bwd(o, g, dy):
            r = lax.rsqrt(_mean_last(o * o) + EPS)
            gdy = dy * g
            return r * gdy - o * ((r * r * r) * _mean_last(o * gdy)), _sum_rows(o * r * dy)

        d_attn, dga = out_norm_bwd(attn_ref[...], ga_ref[...], dm_ref[:, 0:ATTN_WIDTH].astype(F32))
        dga_ref[...] += dga
        d_sgu, dgs = out_norm_bwd(sgu_ref[...], gs_ref[...], dm_ref[:, ATTN_WIDTH:D_MODEL].astype(F32))
        dgs_ref[...] += dgs

        for h in range(N_GMLP_HEADS):
            vn_h = vn_ref[:, _head(h)]
            f = jnp.dot(ws_ref[h], vn_h, preferred_element_type=F32) + bs_ref[h]
            ds_h = d_sgu[:, _head(h)]
            dug_ref[:, _head(h)] = ds_h * f
            df = ds_h * ug_ref[:, _head(h)]
            dfb = df.astype(BF16)
            dvn_ref[:, _head(h)] = lax.dot_general(ws_ref[h], dfb, tn_dims, preferred_element_type=F32)
            dws_ref[h] += lax.dot_general(dfb, vn_h, nt_dims, preferred_element_type=F32)
            dbs_ref[h] += jnp.broadcast_to(jnp.sum(df, axis=-1, keepdims=True), (BLOCK, BLOCK))

        row0 = pl.multiple_of(n * BLOCK, BLOCK)
        for kh in range(N_KV_HEADS):
            kb = jnp.concatenate([kp_ref[:, _head(kh)], kc_ref[:, _head(kh)], kx_ref[:, _head(kh)]], axis=0)
            vbd = jnp.concatenate([vp_ref[:, _head(kh)], vc_ref[:, _head(kh)], vx_ref[:, _head(kh)]], axis=0)
            q4 = _group_rows(q_ref, kh)
            pb = probs_ref[kh]
            p = pb.astype(F32)
            do4 = _group_rows(d_attn, kh).astype(BF16)
            dp = lax.dot_general(do4, vbd, nt_dims, preferred_element_type=F32)
            delta = jnp.sum(p * dp, axis=-1, keepdims=True)
            dsc = (p * (dp - delta) * (HEAD_DIM ** -0.5)).astype(BF16)
            d_sink = -(psink_ref[kh] * delta)
            dq4 = jnp.dot(dsc, kb, preferred_element_type=F32)
            for g in range(GQA_GROUP):
                h = kh * GQA_GROUP + g
                dsk_ref[h:h + 1, :] += jnp.broadcast_to(_sum_all(_rows_of(d_sink, g)), (1, BLOCK))
                dq_ref[:, _head(h)] = _rows_of(dq4, g)
            dk_acc[pl.ds(row0, 3 * BLOCK), _head(kh)] += lax.dot_general(dsc, q4, tn_dims, preferred_element_type=F32)
            dv_acc[pl.ds(row0, 3 * BLOCK), _head(kh)] += lax.dot_general(pb, do4, tn_dims, preferred_element_type=F32)

        @pl.when(n == nb - 1)
        def _():
            dk_ref[...] = dk_acc[BLOCK:BLOCK + s, :]
            dv_ref[...] = dv_acc[BLOCK:BLOCK + s, :]

    hh = (N_GMLP_HEADS, BLOCK, BLOCK)
    full_kv = pl.BlockSpec((s, KV_WIDTH), lambda n: (0, 0))
    return _ordered_call(
        body, name=name,
        out_shape=(jax.ShapeDtypeStruct((s, ATTN_WIDTH), F32), jax.ShapeDtypeStruct((s, KV_WIDTH), F32),
                   jax.ShapeDtypeStruct((s, KV_WIDTH), F32), jax.ShapeDtypeStruct((s, GMLP_WIDTH), F32),
                   jax.ShapeDtypeStruct((s, GMLP_WIDTH), F32), jax.ShapeDtypeStruct(hh, F32), jax.ShapeDtypeStruct(hh, F32),
                   jax.ShapeDtypeStruct((N_Q_HEADS, BLOCK), F32), jax.ShapeDtypeStruct((1, ATTN_WIDTH), F32),
                   jax.ShapeDtypeStruct((1, GMLP_WIDTH), F32)),
        grid=(nb,),
        in_specs=[_blk(ATTN_WIDTH)] + _band_specs(KV_WIDTH, nb) + _band_specs(KV_WIDTH, nb)
        + [_blk(GMLP_WIDTH), _blk(GMLP_WIDTH), _blk(ATTN_WIDTH), _blk(GMLP_WIDTH), _blk(D_MODEL), _whole3(hh), _whole3(hh),
           pl.BlockSpec((1, ATTN_WIDTH), lambda n: (0, 0)), pl.BlockSpec((1, GMLP_WIDTH), lambda n: (0, 0)),
           _per_block(PROBS_BLOCK), _per_block(PSINK_BLOCK)],
        out_specs=(_blk(ATTN_WIDTH), full_kv, full_kv, _blk(GMLP_WIDTH), _blk(GMLP_WIDTH), _whole3(hh), _whole3(hh),
                   pl.BlockSpec((N_Q_HEADS, BLOCK), lambda n: (0, 0)), pl.BlockSpec((1, ATTN_WIDTH), lambda n: (0, 0)),
                   pl.BlockSpec((1, GMLP_WIDTH), lambda n: (0, 0))),
        scratch_shapes=[pltpu.VMEM((s + 2 * BLOCK, KV_WIDTH), F32), pltpu.VMEM((s + 2 * BLOCK, KV_WIDTH), F32)],
        compiler_params=_params(("arbitrary",)),
    )(qn, kn, kn, kn, vb, vb, vb, ug, vn, attn, sgu, dmixed, wsb, bsb, ga, gs, probs, psink)


CONV_TILE = 256


PAD_ROWS = 8


def _zero_pad_rows(pad_ref):
    s = pad_ref.shape[0] - 2 * PAD_ROWS
    zeros = jnp.zeros((PAD_ROWS, pad_ref.shape[1]), F32)
    pad_ref[0:PAD_ROWS, :] = zeros
    pad_ref[PAD_ROWS + s:2 * PAD_ROWS + s, :] = zeros


def _shift_rows(a, pad_ref):
    s = a.shape[0]
    pad_ref[PAD_ROWS:PAD_ROWS + s, :] = a
    padded = pad_ref[...]
    prev = pltpu.roll(padded, 1, 0)[PAD_ROWS:PAD_ROWS + s]
    nxt = pltpu.roll(padded, s + 2 * PAD_ROWS - 1, 0)[PAD_ROWS:PAD_ROWS + s]
    return prev, nxt


def _conv_specs(s):
    tc = CONV_TILE
    nj = D_FF // tc
    return (tc, nj, pl.BlockSpec((2, s, tc), lambda j: (0, 0, j)),
            [pl.BlockSpec((3, tc), lambda j: (0, j)), pl.BlockSpec((3, tc), lambda j: (0, j + nj))],
            [pl.BlockSpec((1, tc), lambda j: (0, j)), pl.BlockSpec((1, tc), lambda j: (0, j + nj))])


def _conv_gate_fwd(a_pre, cw, cb, name):
    s = a_pre.shape[1]
    tc, nj, a_spec, w_specs, b_specs = _conv_specs(s)

    def body(a_ref, wg_ref, wu_ref, bg_ref, bu_ref, act_ref, dgu_ref, pad_ref):
        _zero_pad_rows(pad_ref)

        def conv(a, w_ref, b_ref):
            prev, nxt = _shift_rows(a, pad_ref)
            return b_ref[...] + prev * w_ref[0:1, :] + a * w_ref[1:2, :] + nxt * w_ref[2:3, :]

        g = conv(a_ref[0].astype(F32), wg_ref, bg_ref)
        u = conv(a_ref[1].astype(F32), wu_ref, bu_ref)
        sg = 1.0 / (1.0 + jnp.exp(-g))
        silu = g * sg
        act_ref[...] = (silu * u).astype(BF16)
        dgu_ref[0] = (u * (sg * (1.0 + g * (1.0 - sg)))).astype(BF16)
        dgu_ref[1] = silu.astype(BF16)

    return _ordered_call(
        body, name=name, out_shape=(jax.ShapeDtypeStruct((s, D_FF), BF16), jax.ShapeDtypeStruct((2, s, D_FF), BF16)),
        grid=(nj,), in_specs=[a_spec] + w_specs + b_specs,
        out_specs=(pl.BlockSpec((s, tc), lambda j: (0, j)), pl.BlockSpec((2, s, tc), lambda j: (0, 0, j))),
        scratch_shapes=[pltpu.VMEM((s + 2 * PAD_ROWS, tc), F32)], compiler_params=_params(("parallel",)),
    )(a_pre, cw, cw, cb, cb)


def _conv_gate_bwd(a_pre, dgu, cw, dact, name):
    s = a_pre.shape[1]
    tc, nj, a_spec, w_specs, _ = _conv_specs(s)

    def body(a_ref, dgu_ref, wg_ref, wu_ref, dact_ref, dap_ref, dcw_ref, dcb_ref, pad_ref):
        _zero_pad_rows(pad_ref)
        dact_v = dact_ref[...].astype(F32)
        for part, w_ref in enumerate((wg_ref, wu_ref)):
            da = dact_v * dgu_ref[part].astype(F32)
            a = a_ref[part].astype(F32)
            da_prev, da_next = _shift_rows(da, pad_ref)
            dcw_ref[part, 0:1, :] = _sum_rows(a * da_next)
            dcw_ref[part, 1:2, :] = _sum_rows(a * da)
            dcw_ref[part, 2:3, :] = _sum_rows(a * da_prev)
            dcb_ref[part] = _sum_rows(da)
            dap_ref[part] = (da_next * w_ref[0:1, :] + da * w_ref[1:2, :] + da_prev * w_ref[2:3, :]).astype(BF16)

    return _ordered_call(
        body, name=name,
        out_shape=(jax.ShapeDtypeStruct((2, s, D_FF), BF16), jax.ShapeDtypeStruct((2, 3, D_FF), F32),
                   jax.ShapeDtypeStruct((2, 1, D_FF), F32)),
        grid=(nj,),
        in_specs=[a_spec, pl.BlockSpec((2, s, tc), lambda j: (0, 0, j))] + w_specs + [pl.BlockSpec((s, tc), lambda j: (0, j))],
        out_specs=(pl.BlockSpec((2, s, tc), lambda j: (0, 0, j)), pl.BlockSpec((2, 3, tc), lambda j: (0, 0, j)),
                   pl.BlockSpec((2, 1, tc), lambda j: (0, 0, j))),
        scratch_shapes=[pltpu.VMEM((s + 2 * PAD_ROWS, tc), F32)], compiler_params=_params(("parallel",)),
    )(a_pre, dgu, cw, cw, dact)


def _ffn_down_loss(act, w_down, x1, target, name):
    s, k = act.shape
    d = w_down.shape[1]
    tm, tn = min(1024, s), 256
    tile = pl.BlockSpec((tm, tn), lambda i, j: (i, j))

    def body(a_ref, b_ref, res_ref, t_ref, loss_ref, dy_ref, dyb_ref):
        @pl.when(jnp.logical_and(pl.program_id(0) == 0, pl.program_id(1) == 0))
        def _():
            loss_ref[...] = jnp.zeros_like(loss_ref)

        y = res_ref[...] + jnp.dot(a_ref[...], b_ref[...], preferred_element_type=F32)
        err = y - t_ref[...]
        loss_ref[...] += jnp.broadcast_to((0.5 / d) * _sum_all(err * err), (8, 128))
        dy = err * (1.0 / d)
        dy_ref[...] = dy
        dyb_ref[...] = dy.astype(BF16)

    return _ordered_call(
        body, name=name,
        out_shape=(jax.ShapeDtypeStruct((8, 128), F32), jax.ShapeDtypeStruct((s, d), F32), jax.ShapeDtypeStruct((s, d), BF16)),
        grid=(s // tm, d // tn),
        in_specs=[pl.BlockSpec((tm, k), lambda i, j: (i, 0)), pl.BlockSpec((k, tn), lambda i, j: (0, j)), tile, tile],
        out_specs=(pl.BlockSpec((8, 128), lambda i, j: (0, 0)), tile, tile),
        compiler_params=_params(("arbitrary", "arbitrary")),
    )(act, w_down, x1, target)


def _row_block(rows, cols, budget=1 << 20):
    if rows * cols <= budget:
        return rows
    best = None
    for tr in range(16, rows, 16):
        if rows % tr == 0 and tr * cols <= budget:
            best = tr
    assert best is not None, (rows, cols)
    return best


def _place_shard(x4, layer, j_arr, out_dtype, name):
    _, nh, r, cols = x4.shape
    tr = _row_block(r, cols)

    def body(j_ref, x_ref, o_ref):
        o_ref[...] = x_ref[...].astype(out_dtype)

    grid_spec = pltpu.PrefetchScalarGridSpec(
        num_scalar_prefetch=1, grid=(nh, r // tr),
        in_specs=[pl.BlockSpec((None, None, tr, cols), lambda h, i, j_ref: (layer, h, i, 0))],
        out_specs=pl.BlockSpec((None, None, tr, cols), lambda h, i, j_ref: (j_ref[0], h, i, 0)))
    return _ordered_call(
        body, name=name, out_shape=jax.ShapeDtypeStruct((N_CHIPS, nh, r, cols), out_dtype), grid_spec=grid_spec,
        compiler_params=_params(("parallel", "parallel")),
    )(j_arr, x4)


def _adamw(w, g, m, v, name):
    rows, cols = w.shape
    tr = _row_block(rows, cols, 1 << 18)

    def body(w_ref, g_ref, m_ref, v_ref, go_ref, d_ref, nm_ref, nv_ref):
        gv = g_ref[...]
        go_ref[...] = gv
        mn = ADAM_B1 * m_ref[...] + (1.0 - ADAM_B1) * gv
        vn = ADAM_B2 * v_ref[...] + (1.0 - ADAM_B2) * (gv * gv)
        m_hat = mn / (1.0 - ADAM_B1 ** ADAM_STEP)
        v_hat = vn / (1.0 - ADAM_B2 ** ADAM_STEP)
        d_ref[...] = -ADAM_LR * (m_hat / (jnp.sqrt(v_hat) + ADAM_EPS) + ADAM_WD * w_ref[...])
        nm_ref[...] = mn
        nv_ref[...] = vn

    sds = jax.ShapeDtypeStruct((rows, cols), F32)
    return _ordered_call(
        body, name=name, out_shape=(sds, sds, sds, sds), grid=(rows // tr,),
        in_specs=[_rows(cols, tr)] * 4, out_specs=(_rows(cols, tr),) * 4, compiler_params=_params(("parallel",)),
    )(w, g, m, v)


def _chip_sum(p4, recv3, j_arr, c_arr, name):
    _, rh, cols = p4.shape
    tr = _row_block(rh, cols, 1 << 19)

    def body(j_ref, c_ref, p_ref, r_ref, o_ref):
        total = p_ref[...].astype(F32)
        for peer in range(3):
            total = total + r_ref[peer].astype(F32)
        o_ref[...] = total.astype(BF16)

    grid_spec = pltpu.PrefetchScalarGridSpec(
        num_scalar_prefetch=2, grid=(rh // tr,),
        in_specs=[pl.BlockSpec((None, tr, cols), lambda i, j_ref, c_ref: (j_ref[0], i, 0)),
                  pl.BlockSpec((3, tr, cols), lambda i, j_ref, c_ref: (0, i, 0))],
        out_specs=pl.BlockSpec((None, tr, cols), lambda i, j_ref, c_ref: (c_ref[0], i, 0)))
    return _ordered_call(
        body, name=name, out_shape=jax.ShapeDtypeStruct((2, rh, cols), BF16), grid_spec=grid_spec,
        compiler_params=_params(("parallel",)),
    )(j_arr, c_arr, p4, recv3)


def _adamw_layer(w, g, m, v, layer, into, name):
    nl, rows, cols = w.shape
    slabs, _, width = g.shape
    assert slabs * width == cols and g.shape[1] == rows, (name, w.shape, g.shape)
    tr = _row_block(rows, width, 1 << 19)
    at_layer = pl.BlockSpec((None, tr, width), lambda h, i: (layer, i, h))

    def body(w_ref, g_ref, m_ref, v_ref, *rest):
        go_ref, d_ref, nm_ref, nv_ref = rest[-4:]
        gv = g_ref[...].astype(F32)
        go_ref[...] = gv
        mn = ADAM_B1 * m_ref[...] + (1.0 - ADAM_B1) * gv
        vn = ADAM_B2 * v_ref[...] + (1.0 - ADAM_B2) * (gv * gv)
        m_hat = mn / (1.0 - ADAM_B1 ** ADAM_STEP)
        v_hat = vn / (1.0 - ADAM_B2 ** ADAM_STEP)
        d_ref[...] = -ADAM_LR * (m_hat / (jnp.sqrt(v_hat) + ADAM_EPS) + ADAM_WD * w_ref[...])
        nm_ref[...] = mn
        nv_ref[...] = vn

    in_specs = [at_layer, pl.BlockSpec((None, tr, width), lambda h, i: (h, i, 0)), at_layer, at_layer]
    operands = [w, g, m, v]
    aliases = {}
    if into is not None:
        in_specs += [ANY] * 4
        operands += list(into)
        aliases = {4 + i: i for i in range(4)}
    sds = jax.ShapeDtypeStruct((nl, rows, cols), F32)
    return _ordered_call(
        body, name=name, out_shape=(sds,) * 4, grid=(slabs, rows // tr), in_specs=in_specs, out_specs=(at_layer,) * 4,
        input_output_aliases=aliases, compiler_params=_params(("parallel", "parallel")),
    )(*operands)


def _sum_devices(mine, landed, me_arr, name):
    rows, lanes = mine.shape

    def body(me_ref, mine_ref, landed_ref, o_ref):
        total = None
        for dev in range(8):
            part = jnp.where(me_ref[0] == dev, mine_ref[...], landed_ref[dev])
            total = part if total is None else total + part
        o_ref[...] = total

    grid_spec = pltpu.PrefetchScalarGridSpec(
        num_scalar_prefetch=1, grid=(1,),
        in_specs=[pl.BlockSpec((rows, lanes), lambda i, me_ref: (0, 0)), pl.BlockSpec((8, rows, lanes), lambda i, me_ref: (0, 0, 0))],
        out_specs=pl.BlockSpec((rows, lanes), lambda i, me_ref: (0, 0)))
    return _ordered_call(
        body, name=name, out_shape=jax.ShapeDtypeStruct((rows, lanes), F32), grid_spec=grid_spec,
        compiler_params=_params(("arbitrary",)),
    )(me_arr, mine, landed)


def _place():
    x, y, c = lax.axis_index("x"), lax.axis_index("y"), lax.axis_index("c")
    chips = [(1 - x, y), (x, 1 - y), (1 - x, 1 - y)]
    return x, y, c, chips


HBM = pl.BlockSpec(memory_space=pltpu.HBM)
SEM = pl.BlockSpec(memory_space=pltpu.SEMAPHORE)
TOKEN = jax.ShapeDtypeStruct((8, 128), F32)


def _remote(src, dst, send_sem, recv_sem, to):
    return pltpu.make_async_remote_copy(src_ref=src, dst_ref=dst, send_sem=send_sem, recv_sem=recv_sem, device_id=to,
                                        device_id_type=MESH)


def _split_call(body, name, thru, sems_in=(), fresh=(), new_sems=(), after_last=True):
    n_t, n_s, n_f = len(thru), len(sems_in), len(fresh)

    def call_body(*refs):
        outs = refs[n_t + n_s:]
        body(refs[:n_t], refs[n_t:n_t + n_s], outs[1 + n_t:1 + n_t + n_f], outs[1 + n_t + n_f:])
        outs[0][...] = jnp.zeros_like(outs[0])

    out_shape = ([TOKEN] + [pltpu.HBM(t.shape, t.dtype) for t in thru] + [pltpu.HBM(shp, dt) for shp, dt in fresh]
                 + [pltpu.SemaphoreType.DMA(shp) for shp in new_sems])
    out_specs = [pl.BlockSpec(memory_space=pltpu.VMEM)] + [HBM] * (n_t + n_f) + [SEM] * len(new_sems)
    if not after_last or any(t is _Order.last for t in thru):
        _Order.last = None
    out = _ordered_call(
        call_body, name=name, out_shape=tuple(out_shape), in_specs=[HBM] * n_t + [SEM] * n_s, out_specs=tuple(out_specs),
        input_output_aliases={i: 1 + i for i in range(n_t)},
        compiler_params=pltpu.CompilerParams(has_side_effects=pltpu.SideEffectType.DATAFLOW_SIDE_EFFECTING),
    )(*[pltpu.with_memory_space_constraint(t, pltpu.HBM) for t in thru], *sems_in)
    return out[1:1 + n_t], out[1 + n_t:1 + n_t + n_f], out[1 + n_t + n_f:]


class _Exchange:
    def __init__(self, weights, m_in, v_in, j_arr, c_arr, me_arr):
        self.w, self.m, self.v = weights, m_in, v_in
        self.j_arr, self.c_arr, self.me_arr = j_arr, c_arr, me_arr
        self.adam, self.small, self.pairs, self.held = {}, {}, {}, None
        self.o_arr = 1 - c_arr
        self.groups = [(l, name) for l in range(DEPTH) for name in BIG_NAMES]
        self.shard_shape = {name: weights[name].shape[1:] for name in BIG_NAMES}
        self.conv_state, self.state = [], {}
        self.ready, self.conv_ready = {}, {}
        self.pending, self.tick, self.reduced = [], 0, {}

        def place(grp):
            l, name = grp
            nl, r, cols = weights[name].shape
            return _place_shard(weights[name].reshape(nl, 2, r // 2, cols), l, j_arr, BF16, f"place_{name}_l{l}")

        def start_copies(tag, convs, groups, bufs):
            n_c = len(convs)

            def start(thru, _, __, sems):
                x, y, c, chips = _place()
                j_me = 2 * x + y
                copies = []
                for i in range(len(thru)):
                    mine = thru[i].at[j_me] if i < n_c else thru[i].at[j_me, c]
                    copies += [_remote(mine, mine, sems[2 * i].at[k], sems[2 * i + 1].at[k], (*chip, c))
                               for k, chip in enumerate(chips)]
                for cp in copies:
                    cp.start()

            thru, _, sems = _split_call(start, tag, convs + bufs, new_sems=[(3,)] * (2 * (n_c + len(bufs))))
            self.conv_state += [(thru[i], sems[2 * i], sems[2 * i + 1]) for i in range(n_c)]
            for g, grp in enumerate(groups):
                self.state[grp] = (thru[n_c + g], sems[2 * (n_c + g)], sems[2 * (n_c + g) + 1])

        convs = [_place_shard(weights["conv_w"][:, None], l, j_arr, F32, f"place_conv_w_l{l}") for l in range(DEPTH)]
        start_copies("gather_start_first", convs, self.groups[:1], [place(self.groups[0])])
        start_copies("gather_start_rest", [], self.groups[1:], [place(grp) for grp in self.groups[1:]])

    def conv_w(self, l):
        if l not in self.conv_ready:
            buf, send, recv = self.conv_state[l]

            def wait(thru, sems, _, __):
                x, y, c, chips = _place()
                for k, chip in enumerate(chips):
                    mine, theirs = thru[0].at[2 * x + y], thru[0].at[2 * chip[0] + chip[1]]
                    _remote(mine, mine, sems[0].at[k], sems[1].at[k], (*chip, c)).wait_send()
                    _remote(theirs, theirs, sems[0].at[k], sems[1].at[k], (x, y, c)).wait_recv()

            (buf,), _, _ = _split_call(wait, f"gather_conv_w_l{l}", [buf], sems_in=[send, recv])
            self.conv_ready[l] = jnp.transpose(buf[:, 0], (1, 0, 2)).reshape(3, 2 * D_FF)
        return self.conv_ready[l]

    def weight(self, l, name):
        grp = (l, name)
        if grp not in self.ready:
            buf, send, recv = self.state[grp]

            def forward(thru, sems, _, new):
                x, y, c, chips = _place()
                for k, chip in enumerate(chips):
                    landed = thru[0].at[2 * chip[0] + chip[1], c]
                    _remote(landed, landed, new[0].at[k], sems[0].at[k], (x, y, c)).wait_recv()
                    _remote(landed, landed, new[0].at[k], new[1].at[k], (x, y, 1 - c)).start()

            (buf,), _, (fsend, frecv) = _split_call(forward, f"gather_pass_{name}_l{l}", [buf], sems_in=[recv],
                                                    new_sems=[(3,), (3,)])

            def finish(thru, sems, _, __):
                x, y, c, chips = _place()
                mine = thru[0].at[2 * x + y, c]
                for k, chip in enumerate(chips):
                    j_k = 2 * chip[0] + chip[1]
                    theirs, landed = thru[0].at[j_k, 1 - c], thru[0].at[j_k, c]
                    _remote(theirs, theirs, sems[1].at[k], sems[2].at[k], (x, y, c)).wait_recv()
                    _remote(landed, landed, sems[1].at[k], sems[2].at[k], (x, y, 1 - c)).wait_send()
                    _remote(mine, mine, sems[0].at[k], sems[2].at[k], (*chip, c)).wait_send()

            (buf,), _, _ = _split_call(finish, f"gather_done_{name}_l{l}", [buf], sems_in=[send, fsend, frecv])
            r, cols = self.shard_shape[name]
            self.ready[grp] = buf.reshape(N_CHIPS, r, cols) if name in ("w_in", "w_up") else buf.reshape(N_CHIPS * r, cols)
        return self.ready[grp]

    def pair_send(self, l, name, other):
        held = self.held
        self.held = None

        def start(thru, _, fresh, sems):
            x, y, c, chips = _place()
            copies = [_remote(thru[0], fresh[0], sems[0], sems[1], (x, y, 1 - c))]
            if held is not None:
                copies += [_remote(thru[1].at[2 * chip[0] + chip[1]], fresh[1].at[k], sems[2].at[k], sems[3].at[k], (*chip, c))
                           for k, chip in enumerate(chips)]
            for cp in copies:
                cp.start()

        thru, fresh, new_sems = [other], [(other.shape, BF16)], [(), ()]
        if held is not None:
            thru, fresh, new_sems = thru + [held[2]], fresh + [((3,) + held[2].shape[1:], BF16)], new_sems + [(3,), (3,)]
        thru, fresh, sems = _split_call(start, f"pair_start_{name}_l{l}", thru, fresh=fresh, new_sems=new_sems, after_last=False)
        self.pairs[(l, name)] = (thru[0], fresh[0], sems[:2])
        if held is not None:
            self.pending.append(dict(l=held[0], name=held[1], stage=2, at=self.tick, bufs=(thru[1], fresh[1]), sems=sems[2:]))

    def pair_recv(self, l, name):
        other, recv, sems = self.pairs.pop((l, name))

        def wait(thru, sems, _, __):
            x, y, c, _chips = _place()
            cp = _remote(thru[0], thru[1], sems[0], sems[1], (x, y, 1 - c))
            cp.wait_send()
            cp.wait_recv()

        (_, recv), _, _ = _split_call(wait, f"pair_done_{name}_l{l}", [other, recv], sems_in=list(sems))
        return recv

    def scatter(self, l, name, p4):
        assert self.held is None
        self.held = (l, name, p4)
        if (l, name) == (0, BIG_NAMES[0]):
            self._scatter_held()

    def _scatter_held(self):
        l, name, p4 = self.held
        self.held = None

        def start(thru, _, fresh, sems):
            x, y, c, chips = _place()
            for k, chip in enumerate(chips):
                _remote(thru[0].at[2 * chip[0] + chip[1]], fresh[0].at[k], sems[0].at[k], sems[1].at[k], (*chip, c)).start()

        (p4,), (recv3,), sems = _split_call(start, f"chips_start_{name}_l{l}", [p4], fresh=[((3,) + p4.shape[1:], BF16)],
                                           new_sems=[(3,), (3,)], after_last=False)
        self.pending.append(dict(l=l, name=name, stage=2, at=self.tick, bufs=(p4, recv3), sems=sems))

    def point(self, drain=False):
        self.tick += 1
        if drain:
            old = [g for g in self.pending if g["stage"] == 2 and g["at"] + 2 <= self.tick]
            new = [g for g in self.pending if g["stage"] == 2 and g["at"] + 2 > self.tick]
            for grp in old + [g for g in self.pending if g["stage"] == 3] + new:
                self._advance([grp] if grp["stage"] == 3 else [], [grp] if grp["stage"] == 2 else [])
        else:
            self._advance([grp for grp in self.pending if grp["stage"] == 3 and grp["at"] < self.tick],
                          [grp for grp in self.pending if grp["stage"] == 2 and grp["at"] + 2 <= self.tick])

    def _advance(self, joined, landed):
        if not joined and not landed:
            return
        n_j, n_l = len(joined), len(landed)

        def wait(thru, sems, _, __):
            x, y, c, chips = _place()
            for i in range(n_j):
                buf, send, recv = thru[i], sems[2 * i], sems[2 * i + 1]
                _remote(buf.at[c], buf.at[c], send, recv, (x, y, 1 - c)).wait_send()
                _remote(buf.at[1 - c], buf.at[1 - c], send, recv, (x, y, c)).wait_recv()
            for i in range(n_l):
                p4, recv3 = thru[n_j + 2 * i], thru[n_j + 2 * i + 1]
                send, recv = sems[2 * (n_j + i)], sems[2 * (n_j + i) + 1]
                for k, chip in enumerate(chips):
                    cp = _remote(p4.at[2 * chip[0] + chip[1]], recv3.at[k], send.at[k], recv.at[k], (*chip, c))
                    cp.wait_send()
                    cp.wait_recv()

        tag = "_".join([f"{grp['name']}{grp['l']}_halves" for grp in joined] + [f"{grp['name']}{grp['l']}_chips" for grp in landed])
        bufs, _, _ = _split_call(wait, f"landed_{tag}", [b for grp in joined + landed for b in grp["bufs"]],
                                 sems_in=[sm for grp in joined + landed for sm in grp["sems"]])
        for i, grp in enumerate(joined):
            l, name, full = grp["l"], grp["name"], bufs[i]
            if GRAD_HALVES[name][0] != "cols_of_block":
                full = full.reshape((1,) + tuple(self.shard_shape[name]))
            self.adam[name] = _adamw_layer(self.w[name], full, self.m[name], self.v[name], l, self.adam.get(name),
                                           f"adamw_{name}_l{l}")
            grp.update(stage=4)
        if not landed:
            return
        halves = [_chip_sum(bufs[n_j + 2 * i], bufs[n_j + 2 * i + 1], self.j_arr, self.c_arr,
                            f"chip_sum_{grp['name']}_l{grp['l']}") for i, grp in enumerate(landed)]

        def start(thru, _, __, sems):
            x, y, c, _chips = _place()
            for i in range(n_l):
                _remote(thru[i].at[c], thru[i].at[c], sems[2 * i], sems[2 * i + 1], (x, y, 1 - c)).start()

        tag = "_".join(f"{grp['name']}{grp['l']}" for grp in landed)
        halves, _, sems = _split_call(start, f"join_start_{tag}", halves, new_sems=[()] * (2 * n_l), after_last=False)
        for i, grp in enumerate(landed):
            grp.update(stage=3, at=self.tick, bufs=(halves[i],), sems=tuple(sems[2 * i:2 * i + 2]))

    def finish(self):
        if self.held is not None:
            self._scatter_held()
        while any(grp["stage"] < 4 for grp in self.pending):
            self.point(drain=True)
        return self.adam

    @staticmethod
    def _peer(k, x, y, c):
        return (1 - x if k & 4 else x, 1 - y if k & 2 else y, 1 - c if k & 1 else c)

    def small_grads(self, l, grads, loss_tile):
        parts = [grads[nm] for nm in SMALL_NAMES] + ([loss_tile[0, 0:1]] if loss_tile is not None else [])
        packed = _pack_call(parts, f"small_pack_l{l}")
        rows = packed.shape[0]

        def start(thru, _, fresh, sems):
            x, y, c, _chips = _place()
            for k in range(1, 8):
                _remote(thru[0], fresh[0].at[4 * x + 2 * y + c], sems[0].at[k - 1], sems[1].at[k - 1],
                        self._peer(k, x, y, c)).start()

        (packed,), (landed,), sems = _split_call(start, f"small_start_l{l}", [packed], fresh=[((8, rows, PACK_LANES), F32)],
                                                 new_sems=[(7,), (7,)], after_last=False)
        self.small[l] =(packed, landed, sems, [p.shape for p in parts])

    def small_sum(self, l):
        packed, landed, sems, _shapes = self.small[l]

        def wait(thru, sems, _, __):
            x, y, c, _chips = _place()
            for k in range(1, 8):
                px, py, pc = self._peer(k, x, y, c)
                _remote(thru[0], thru[1].at[4 * x + 2 * y + c], sems[0].at[k - 1], sems[1].at[k - 1], (px, py, pc)).wait_send()
                _remote(thru[0], thru[1].at[4 * px + 2 * py + pc], sems[0].at[k - 1], sems[1].at[k - 1], (x, y, c)).wait_recv()

        (packed, landed), _, _ = _split_call(wait, f"small_done_l{l}", [packed, landed], sems_in=list(sems))
        return _sum_devices(packed, landed, self.me_arr, f"small_sum_l{l}")


def _rope_tables(s):
    inv_freq = ROPE_THETA ** (-jnp.arange(0, HEAD_DIM, 2, dtype=F32) / HEAD_DIM)
    ang = jnp.arange(s, dtype=F32)[:, None] * inv_freq[None, :]
    cos, sin = jnp.cos(ang), jnp.sin(ang)
    return jnp.concatenate([cos, cos], axis=-1), jnp.concatenate([-sin, sin], axis=-1)


def _local_step(x, target, ex, small):
    s = x.shape[0]
    cosf, sinf = _rope_tables(s)
    saved = []
    for l in range(DEPTH):
        p = small[l]
        t = f"l{l}"
        h = _rms_fwd(x, p["norm1_g"], f"norm1_{t}")
        z = _matmul(h, ex.weight(l, "w_in"), mode="nn", out_dtype=BF16, tm=1024, tn=896, tk=2048, b_parts=4, name=f"proj_in_{t}")
        qn, kn, vb, ug, vn, *gate_kept = _proj_post(z, p["q_norm_g"], p["k_norm_g"], p["sgu_ln_g"], p["sgu_ln_b"], cosf, sinf,
                                                    f"proj_post_{t}")
        attn, sgu, mixed, probs, psink = _mixer_fwd(qn, kn, vb, ug, vn, p["w_s_bf16"], p["b_s_tile"], p["sink"],
                                                    p["attn_out_g"], p["sgu_out_g"], f"mixer_{t}")
        x1 = _matmul(mixed, ex.weight(l, "w_o"), mode="nn", out_dtype=F32, tm=2048, tn=256, tk=2048, res=x,
                     name=f"proj_out_{t}")
        h2 = _rms_fwd(x1, p["norm2_g"], f"norm2_{t}")
        a_pre = _matmul(h2, ex.weight(l, "w_up"), mode="nn", out_dtype=BF16, tm=1024, tn=1408, tk=2048, b_parts=4,
                        out_parts=2,
                        name=f"ffn_up_{t}")
        act, dgu = _conv_gate_fwd(a_pre, ex.conv_w(l), p["conv_b"], f"conv_gate_{t}")
        saved.append(dict(x=x, h=h, z=z, qn=qn, kn=kn, vb=vb, ug=ug, vn=vn, attn=attn, sgu=sgu, mixed=mixed, x1=x1, h2=h2,
                          a_pre=a_pre, act=act, dgu=dgu, probs=probs, psink=psink, gate_kept=gate_kept))
        if l < DEPTH - 1:
            x = _matmul(act, ex.weight(l, "w_down"), mode="nn", out_dtype=F32, tm=1024, tn=256, tk=D_FF, res=x1,
                        name=f"ffn_down_{t}")
        else:
            loss_tile, dx, dxb = _ffn_down_loss(act, ex.weight(l, "w_down"), x1, target, f"ffn_down_loss_{t}")
    for l in reversed(range(DEPTH)):
        p, sv = small[l], saved[l]
        t = f"l{l}"
        def weight_grad(name, a, g, between, g_parts=0):
            ex.pair_send(l, name, _grad_half(name, a, g, ex.o_arr, None, f"g_{name}_other_{t}", g_parts))
            out = between()
            ex.scatter(l, name, _grad_half(name, a, g, ex.c_arr, ex.pair_recv(l, name), f"g_{name}_own_{t}", g_parts))
            ex.point()
            return out

        def after_down():
            dact = _matmul(dxb, ex.weight(l, "w_down"), mode="nt", out_dtype=BF16, tm=1024, tn=512, tk=2048,
                           name=f"d_act_{t}")
            return _conv_gate_bwd(sv["a_pre"], sv["dgu"], ex.conv_w(l), dact, f"conv_gate_bwd_{t}")

        dap, dcw, dcb = weight_grad("w_down", sv["act"], dxb, after_down)

        def after_up():
            dh2 = _matmul(dap, ex.weight(l, "w_up"), mode="nt", out_dtype=BF16, tm=1024, tn=1024, tk=2816, a_parts=2,
                          b_parts=4, name=f"d_h2_{t}")
            return _rms_bwd(sv["x1"], p["norm2_g"], dh2, dx, f"norm2_bwd_{t}")

        dx1, dx1b, dg2 = weight_grad("w_up", sv["h2"], dap, after_up, g_parts=2)
        ex.pair_send(l, "w_o", _grad_half("w_o", sv["mixed"], dx1b, ex.o_arr, None, f"g_w_o_other_{t}"))
        dmixed = _matmul(dx1b, ex.weight(l, "w_o"), mode="nt", out_dtype=BF16, tm=1024, tn=1024, tk=2048,
                         name=f"d_mixed_{t}")
        dqn, dkn, dvb, dug, dvn, dws, dbs, dsk, dga, dgs = _mixer_bwd(
            sv["qn"], sv["kn"], sv["vb"], sv["ug"], sv["vn"], sv["attn"], sv["sgu"], dmixed, p["w_s_bf16"], p["b_s_tile"],
            p["attn_out_g"], p["sgu_out_g"], sv["probs"], sv["psink"], f"mixer_bwd_{t}")
        dz, dqg, dkg, dlg, dlb = _proj_post_bwd(sv["z"], dqn, dkn, dvb, dug, dvn, *sv["gate_kept"], p["q_norm_g"], p["k_norm_g"],
                                                 p["sgu_ln_g"], cosf, sinf, f"proj_post_bwd_{t}")
        ex.scatter(l, "w_o", _grad_half("w_o", sv["mixed"], dx1b, ex.c_arr, ex.pair_recv(l, "w_o"), f"g_w_o_own_{t}"))
        ex.point()

        def after_in():
            dh = _matmul_nt_slabs(dz, ex.weight(l, "w_in"), tm=1024, tn=512, name=f"d_h_{t}")
            return _rms_bwd(sv["x"], p["norm1_g"], dh, dx1, f"norm1_bwd_{t}")

        dx, dxb, dg1 = weight_grad("w_in", sv["h"], dz, after_in)
        ex.small_grads(l, dict(
            norm1_g=dg1[0], q_norm_g=dqg[0], k_norm_g=dkg[0], sink=dsk[:, 0], sgu_ln_g=dlg[0], sgu_ln_b=dlb[0], w_s=dws,
            b_s=dbs[:, :, 0], attn_out_g=dga[0], sgu_out_g=dgs[0], norm2_g=dg2[0],
            conv_w=jnp.concatenate([dcw[0], dcw[1]], axis=-1), conv_b=jnp.concatenate([dcb[0, 0], dcb[1, 0]], axis=-1)),
            loss_tile if l == 0 else None)
    return dx


def _small_views(l, norm1_g, q_norm_g, k_norm_g, sink, sgu_ln_g, sgu_ln_b, w_s, b_s, attn_out_g, sgu_out_g, norm2_g, conv_b):
    return dict(
        norm1_g=norm1_g[l][None], q_norm_g=q_norm_g[l][None], k_norm_g=k_norm_g[l][None], sink=sink[l],
        sgu_ln_g=sgu_ln_g[l][None], sgu_ln_b=sgu_ln_b[l][None], w_s_bf16=w_s[l].astype(BF16),
        b_s_tile=jnp.broadcast_to(b_s[l][:, :, None], (N_GMLP_HEADS, BLOCK, BLOCK)), attn_out_g=attn_out_g[l][None],
        sgu_out_g=sgu_out_g[l][None], norm2_g=norm2_g[l][None], conv_b=conv_b[l][None])


SMALL_NAMES = ("norm1_g", "q_norm_g", "k_norm_g", "sink", "sgu_ln_g", "sgu_ln_b", "w_s", "b_s", "attn_out_g", "sgu_out_g",
               "norm2_g", "conv_b", "conv_w")
REPLICATED_NAMES = SMALL_NAMES[:-1]
BIG_NAMES = ("w_in", "w_o", "w_up", "w_down")
PACK_LANES = 128
PACK_ALIGN = 8 * PACK_LANES


def _pack_rows(shape):
    return -(-math.prod(shape) // PACK_ALIGN) * 8


def _pack_parts(arrays):
    parts = []
    for a in arrays:
        flat = a.reshape(-1)
        parts.append(jnp.pad(flat, (0, _pack_rows(a.shape) * PACK_LANES - flat.shape[0])).reshape(-1, PACK_LANES))
    return parts


def _pack_call(arrays, name):
    parts = _pack_parts(arrays)
    total = sum(p.shape[0] for p in parts)

    def body(*refs):
        o_ref, at = refs[-1], 0
        for p_ref in refs[:-1]:
            o_ref[at:at + p_ref.shape[0], :] = p_ref[...]
            at += p_ref.shape[0]

    vm = pl.BlockSpec(memory_space=pltpu.VMEM)
    return _ordered_call(
        body, name=name, out_shape=jax.ShapeDtypeStruct((total, PACK_LANES), F32), in_specs=[vm] * len(parts), out_specs=vm,
        compiler_params=pltpu.CompilerParams(vmem_limit_bytes=V7X_VMEM_LIMIT),
    )(*parts)


def _unpack_layers(stacked, shapes):
    nl = stacked.shape[0]
    out, at = [], 0
    for shp in shapes:
        rows = _pack_rows(shp)
        out.append(stacked[:, at:at + rows].reshape(nl, -1)[:, :math.prod(shp)].reshape((nl,) + tuple(shp)))
        at += rows
    return out


def _adamw_packed(w, g, m, v, rows, layer, into, name):
    head = pl.BlockSpec((rows, PACK_LANES), lambda i: (0, 0))
    at_layer = pl.BlockSpec((None, rows, PACK_LANES), lambda i: (layer, 0, 0))

    def body(w_ref, g_ref, m_ref, v_ref, *rest):
        d_ref, nm_ref, nv_ref = rest[-3:]
        gv = g_ref[...]
        mn = ADAM_B1 * m_ref[...] + (1.0 - ADAM_B1) * gv
        vn = ADAM_B2 * v_ref[...] + (1.0 - ADAM_B2) * (gv * gv)
        m_hat = mn / (1.0 - ADAM_B1 ** ADAM_STEP)
        v_hat = vn / (1.0 - ADAM_B2 ** ADAM_STEP)
        d_ref[...] = -ADAM_LR * (m_hat / (jnp.sqrt(v_hat) + ADAM_EPS) + ADAM_WD * w_ref[...])
        nm_ref[...] = mn
        nv_ref[...] = vn

    in_specs = [head] * 4
    operands = [w, g, m, v]
    aliases = {}
    if into is not None:
        in_specs += [ANY] * 3
        operands += list(into)
        aliases = {4 + i: i for i in range(3)}
    sds = jax.ShapeDtypeStruct((DEPTH, rows, PACK_LANES), F32)
    return _ordered_call(
        body, name=name, out_shape=(sds,) * 3, grid=(1,), in_specs=in_specs, out_specs=(at_layer,) * 3,
        input_output_aliases=aliases, compiler_params=_params(("arbitrary",)),
    )(*operands)


def kernel(x, norm1_g, w_in, q_norm_g, k_norm_g, sink, sgu_ln_g, sgu_ln_b, w_s, b_s, attn_out_g, sgu_out_g, w_o, norm2_g, w_up, conv_w, conv_b, w_down, loss_target, m_norm1_g, m_w_in, m_q_norm_g, m_k_norm_g, m_sink, m_sgu_ln_g, m_sgu_ln_b, m_w_s, m_b_s, m_attn_out_g, m_sgu_out_g, m_w_o, m_norm2_g, m_w_up, m_conv_w, m_conv_b, m_w_down, v_norm1_g, v_w_in, v_q_norm_g, v_k_norm_g, v_sink, v_sgu_ln_g, v_sgu_ln_b, v_w_s, v_b_s, v_attn_out_g, v_sgu_out_g, v_w_o, v_norm2_g, v_w_up, v_conv_w, v_conv_b, v_w_down):
    weights = dict(norm1_g=norm1_g, w_in=w_in, q_norm_g=q_norm_g, k_norm_g=k_norm_g, sink=sink, sgu_ln_g=sgu_ln_g,
                   sgu_ln_b=sgu_ln_b, w_s=w_s, b_s=b_s, attn_out_g=attn_out_g, sgu_out_g=sgu_out_g, w_o=w_o, norm2_g=norm2_g,
                   w_up=w_up, conv_w=conv_w, conv_b=conv_b, w_down=w_down)
    m_in = dict(norm1_g=m_norm1_g, w_in=m_w_in, q_norm_g=m_q_norm_g, k_norm_g=m_k_norm_g, sink=m_sink, sgu_ln_g=m_sgu_ln_g,
                sgu_ln_b=m_sgu_ln_b, w_s=m_w_s, b_s=m_b_s, attn_out_g=m_attn_out_g, sgu_out_g=m_sgu_out_g, w_o=m_w_o,
                norm2_g=m_norm2_g, w_up=m_w_up, conv_w=m_conv_w, conv_b=m_conv_b, w_down=m_w_down)
    v_in = dict(norm1_g=v_norm1_g, w_in=v_w_in, q_norm_g=v_q_norm_g, k_norm_g=v_k_norm_g, sink=v_sink, sgu_ln_g=v_sgu_ln_g,
                sgu_ln_b=v_sgu_ln_b, w_s=v_w_s, b_s=v_b_s, attn_out_g=v_attn_out_g, sgu_out_g=v_sgu_out_g, w_o=v_w_o,
                norm2_g=v_norm2_g, w_up=v_w_up, conv_w=v_conv_w, conv_b=v_conv_b, w_down=v_w_down)
    cx, cy, cc = lax.axis_index("x"), lax.axis_index("y"), lax.axis_index("c")
    j_me = 2 * cx + cy
    c_arr = jnp.reshape(cc, (1,)).astype(jnp.int32)
    j_arr = jnp.reshape(j_me, (1,)).astype(jnp.int32)

    _Order.last = None
    ex = _Exchange(weights, m_in, v_in, j_arr, c_arr, jnp.reshape(4 * cx + 2 * cy + cc, (1,)).astype(jnp.int32))
    small = [_small_views(l, norm1_g, q_norm_g, k_norm_g, sink, sgu_ln_g, sgu_ln_b, w_s, b_s, attn_out_g, sgu_out_g, norm2_g,
                          conv_b) for l in range(DEPTH)]
    held_back, _ = lax.optimization_barrier(([[src[nm] for nm in REPLICATED_NAMES] for src in (weights, m_in, v_in)], _Order.last))
    packed_in = [[_pack_call([arr[l] for arr in arrays], f"pack_{tag}_l{l}") for tag, arrays in zip("wmv", held_back)]
                 for l in range(DEPTH)]
    dx = _local_step(x[0], loss_target[0], ex, small)
    big_out = ex.finish()

    rep_shapes = [weights[nm].shape[1:] for nm in REPLICATED_NAMES]
    rep_rows = sum(_pack_rows(shp) for shp in rep_shapes)
    cw_shape = (3, 2 * D_FF)
    sums, adam_small = [None] * DEPTH, None
    for l in reversed(range(DEPTH)):
        sums[l] = ex.small_sum(l)
        pw, pm, pv = packed_in[l]
        adam_small = _adamw_packed(pw, sums[l], pm, pv, rep_rows, l, adam_small, f"adamw_small_l{l}")
    cw_rows = _pack_rows(cw_shape)
    loss = sums[0][rep_rows + cw_rows, 0]
    stacked = jnp.stack([sm[:rep_rows + cw_rows] for sm in sums])
    grads = dict(zip(REPLICATED_NAMES, _unpack_layers(stacked[:, :rep_rows], rep_shapes)))
    delta, new_m, new_v = (dict(zip(REPLICATED_NAMES, _unpack_layers(arr, rep_shapes))) for arr in adam_small)
    cw_cols = 2 * D_FF // N_CHIPS
    cw_grad = lax.dynamic_slice_in_dim(_unpack_layers(stacked[:, rep_rows:], [cw_shape])[0], j_me * cw_cols, cw_cols, axis=2)
    flat = lambda a: a.reshape(DEPTH * 3, cw_cols)
    cw_out = _adamw(flat(conv_w), flat(cw_grad), flat(m_conv_w), flat(v_conv_w), "adamw_conv_w")
    grads["conv_w"], delta["conv_w"], new_m["conv_w"], new_v["conv_w"] = (a.reshape(DEPTH, 3, cw_cols) for a in cw_out)

    for name in BIG_NAMES:
        grads[name], delta[name], new_m[name], new_v[name] = big_out[name]

    order = ("norm1_g", "w_in", "q_norm_g", "k_norm_g", "sink", "sgu_ln_g", "sgu_ln_b", "w_s", "b_s", "attn_out_g", "sgu_out_g",
             "w_o", "norm2_g", "w_up", "conv_w", "conv_b", "w_down")
    return (loss, dx[None], *[grads[nm] for nm in order], *[delta[nm] for nm in order], *[new_m[nm] for nm in order],
            *[new_v[nm] for nm in order])
```
